```python
import math
import jax, jax.numpy as jnp
from jax import lax
import numpy as np

D_MODEL = 1024
BATCH = 8
SEQ = 2048
DEPTH = 1

D_MIX = D_MODEL
CONV_WIDTH = D_MIX // 2
CONV_GROUPS = 8
CONV_KERNEL = 31
GDN_WIDTH = D_MIX - CONV_WIDTH
GDN_HEAD_DIM = 128
GDN_HEADS = GDN_WIDTH // GDN_HEAD_DIM
GDN_SHORT_CONV = 4
GDN_CHUNK = 64
N_IN = 2 * CONV_WIDTH + 4 * GDN_WIDTH + 2 * GDN_HEADS
D_FF = -(-8 * D_MODEL // (3 * 256)) * 256
N_MOD = 6
EPS = 1e-6

kernel_name = "hybrid_conformer_gdn_adaln_block"


def rms_norm(x, w):
    xf = x.astype(jnp.float32)
    y = xf * lax.rsqrt(jnp.mean(xf * xf, axis=-1, keepdims=True) + EPS)
    return (y * w.astype(jnp.float32)).astype(x.dtype)


def modulate(h, shift, scale):
    return h * (1 + scale[:, None, :]) + shift[:, None, :]


def causal_depthwise_conv(x, w):
    k_len, ch = w.shape
    return lax.conv_general_dilated(
        x, w[:, None, :].astype(x.dtype), window_strides=(1,), padding=((k_len - 1, 0),),
        dimension_numbers=("NWC", "WIO", "NWC"), feature_group_count=ch)


def l2norm(t):
    return t * lax.rsqrt(jnp.sum(t * t, axis=-1, keepdims=True) + EPS)


def conformer_conv_group(a, gate, w_dw, b_dw, gn_w, gn_b):
    u = a * jax.nn.sigmoid(gate)
    u = causal_depthwise_conv(u, w_dw) + b_dw
    bsz, seq, ch = u.shape
    uf = u.astype(jnp.float32).reshape(bsz, seq, CONV_GROUPS, ch // CONV_GROUPS)
    mu = jnp.mean(uf, axis=-1, keepdims=True)
    var = jnp.mean(jnp.square(uf - mu), axis=-1, keepdims=True)
    un = ((uf - mu) * lax.rsqrt(var + EPS)).reshape(bsz, seq, ch)
    un = un * gn_w.astype(jnp.float32) + gn_b.astype(jnp.float32)
    return jax.nn.silu(un).astype(a.dtype)


def chunk_gated_delta_rule(q, k, v, g, beta):
    bsz, seq, nh, dk = q.shape
    dv = v.shape[-1]
    cl = GDN_CHUNK
    nc = seq // cl
    q = q * (dk ** -0.5)

    def to_chunks(t):
        return t.reshape(bsz, nc, cl, nh, -1).transpose(1, 0, 3, 2, 4)

    qc, kc, vc = to_chunks(q), to_chunks(k), to_chunks(v)
    gc = g.reshape(bsz, nc, cl, nh).transpose(1, 0, 3, 2)
    bc = beta.reshape(bsz, nc, cl, nh).transpose(1, 0, 3, 2)
    gcum = jnp.cumsum(gc, axis=-1)
    causal = jnp.tril(jnp.ones((cl, cl), dtype=bool))
    strict = jnp.tril(jnp.ones((cl, cl), dtype=bool), -1)
    diff = gcum[..., :, None] - gcum[..., None, :]
    decay = jnp.where(causal, jnp.exp(jnp.where(causal, diff, 0.0)), 0.0)

    k_beta = kc * bc[..., None]
    a_mat = jnp.where(strict, jnp.einsum("nbhcd,nbhed->nbhce", k_beta, kc) * decay, 0.0)
    eye = jnp.eye(cl, dtype=q.dtype)
    t_mat = lax.linalg.triangular_solve(eye + a_mat, jnp.broadcast_to(eye, a_mat.shape),
                                        left_side=True, lower=True, unit_diagonal=True)
    u = jnp.einsum("nbhce,nbhev->nbhcv", t_mat, vc * bc[..., None])
    w = jnp.einsum("nbhce,nbhek->nbhck", t_mat, k_beta * jnp.exp(gcum)[..., None])
    qk = jnp.where(causal, jnp.einsum("nbhcd,nbhed->nbhce", qc, kc) * decay, 0.0)
    q_dec = qc * jnp.exp(gcum)[..., None]
    k_dec = kc * jnp.exp(gcum[..., -1:] - gcum)[..., None]
    chunk_decay = jnp.exp(gcum[..., -1])

    def step(state, inp):
        q_i, k_i, w_i, u_i, qk_i, cd_i = inp
        v_new = u_i - jnp.einsum("bhck,bhkv->bhcv", w_i, state)
        o_i = jnp.einsum("bhck,bhkv->bhcv", q_i, state) + jnp.einsum("bhce,bhev->bhcv", qk_i, v_new)
        state = state * cd_i[..., None, None] + jnp.einsum("bhck,bhcv->bhkv", k_i, v_new)
        return state, o_i

    state0 = jnp.zeros((bsz, nh, dk, dv), dtype=q.dtype)
    _, o = lax.scan(step, state0, (q_dec, k_dec, w, u, qk, chunk_decay))
    return o.transpose(1, 0, 3, 2, 4).reshape(bsz, seq, nh, dv)


def gated_deltanet_group(q, k, v, z, beta_logit, alpha_logit, w_sc, a_log, dt_bias, norm_w):
    bsz, seq, _ = q.shape
    out_dtype = q.dtype
    qkv = jax.nn.silu(causal_depthwise_conv(jnp.concatenate([q, k, v], axis=-1), w_sc))
    q, k, v = jnp.split(qkv.astype(jnp.float32), 3, axis=-1)
    shp = (bsz, seq, GDN_HEADS, GDN_HEAD_DIM)
    q = l2norm(q.reshape(shp))
    k = l2norm(k.reshape(shp))
    v = v.reshape(shp)
    beta = jax.nn.sigmoid(beta_logit.astype(jnp.float32))
    g = -jnp.exp(a_log.astype(jnp.float32)) * jax.nn.softplus(
        alpha_logit.astype(jnp.float32) + dt_bias.astype(jnp.float32))
    o = chunk_gated_delta_rule(q, k, v, g, beta)
    o = o * lax.rsqrt(jnp.mean(o * o, axis=-1, keepdims=True) + EPS) * norm_w.astype(jnp.float32)
    o = o * jax.nn.silu(z.astype(jnp.float32).reshape(shp))
    return o.reshape(bsz, seq, GDN_WIDTH).astype(out_dtype)


def _fwd_setup_inputs(seed: int = 0) -> dict:
    key = jax.random.key(seed)
    ks = jax.random.split(key, 24)
    f32 = jnp.float32
    nrm = lambda k, shp, s: jax.random.normal(k, shp, f32) * s
    dt = jnp.exp(jax.random.uniform(ks[12], (DEPTH, GDN_HEADS), f32,
                                    minval=math.log(1e-3), maxval=math.log(0.1)))
    return {
        "x": nrm(ks[0], (BATCH, SEQ, D_MODEL), 1.0),
        "c": nrm(ks[1], (BATCH, D_MODEL), 1.0),
        "w_ada": nrm(ks[2], (DEPTH, D_MODEL, N_MOD * D_MODEL), 0.5 * D_MODEL ** -0.5),
        "b_ada": nrm(ks[3], (DEPTH, N_MOD * D_MODEL), 0.02),
        "norm_mix_w": 1.0 + nrm(ks[4], (DEPTH, D_MODEL), 0.02),
        "w_in": nrm(ks[5], (DEPTH, D_MODEL, N_IN), D_MODEL ** -0.5),
        "conv_w": nrm(ks[6], (DEPTH, CONV_KERNEL, CONV_WIDTH), CONV_KERNEL ** -0.5),
        "conv_b": nrm(ks[7], (DEPTH, CONV_WIDTH), 0.02),
        "conv_gn_w": 1.0 + nrm(ks[8], (DEPTH, CONV_WIDTH), 0.02),
        "conv_gn_b": nrm(ks[9], (DEPTH, CONV_WIDTH), 0.02),
        "gdn_conv_w": nrm(ks[10], (DEPTH, GDN_SHORT_CONV, 3 * GDN_WIDTH), GDN_SHORT_CONV ** -0.5),
        "gdn_a_log": jnp.log(jax.random.uniform(ks[11], (DEPTH, GDN_HEADS), f32, minval=1.0, maxval=16.0)),
        "gdn_dt_bias": dt + jnp.log(-jnp.expm1(-dt)),
        "gdn_norm_w": 1.0 + nrm(ks[13], (DEPTH, GDN_HEAD_DIM), 0.02),
        "w_out": nrm(ks[14], (DEPTH, D_MIX, D_MODEL), D_MIX ** -0.5),
        "norm_ffn_w": 1.0 + nrm(ks[15], (DEPTH, D_MODEL), 0.02),
        "w_ffn_in": nrm(ks[16], (DEPTH, D_MODEL, 2 * D_FF), D_MODEL ** -0.5),
        "w_ffn_out": nrm(ks[17], (DEPTH, D_FF, D_MODEL), D_FF ** -0.5),
        "norm_final_w": 1.0 + nrm(ks[18], (D_MODEL,), 0.02),
    }


def _fwd_reference(x, c, w_ada, b_ada, norm_mix_w, w_in, conv_w, conv_b, conv_gn_w, conv_gn_b,
              gdn_conv_w, gdn_a_log, gdn_dt_bias, gdn_norm_w, w_out, norm_ffn_w,
              w_ffn_in, w_ffn_out, norm_final_w):
    split_at = list(np.cumsum([CONV_WIDTH, CONV_WIDTH, GDN_WIDTH, GDN_WIDTH, GDN_WIDTH, GDN_WIDTH, GDN_HEADS]))
    c_act = jax.nn.silu(c)
    for layer in range(DEPTH):
        mod = c_act @ w_ada[layer] + b_ada[layer]
        sh1, sc1, gt1, sh2, sc2, gt2 = jnp.split(mod, N_MOD, axis=-1)

        h = modulate(rms_norm(x, norm_mix_w[layer]), sh1, sc1)
        p = h @ w_in[layer]
        a_in, a_gate, q, k, v, z, beta_logit, alpha_logit = jnp.split(p, split_at, axis=-1)
        out_a = conformer_conv_group(a_in, a_gate, conv_w[layer], conv_b[layer],
                                     conv_gn_w[layer], conv_gn_b[layer])
        out_b = gated_deltanet_group(q, k, v, z, beta_logit, alpha_logit, gdn_conv_w[layer],
                                     gdn_a_log[layer], gdn_dt_bias[layer], gdn_norm_w[layer])
        mix = jnp.concatenate([out_a, out_b], axis=-1) @ w_out[layer]
        x = x + gt1[:, None, :] * mix

        h = modulate(rms_norm(x, norm_ffn_w[layer]), sh2, sc2)
        f_gate, f_up = jnp.split(h @ w_ffn_in[layer], 2, axis=-1)
        ffn = (jax.nn.silu(f_gate) * f_up) @ w_ffn_out[layer]
        x = x + gt2[:, None, :] * ffn
    return rms_norm(x, norm_final_w)


import jax as _jax
import jax.numpy as _jnp

TWIN_FORMAT = 'train_step'
FWD_PARAMS = ['x', 'c', 'w_ada', 'b_ada', 'norm_mix_w', 'w_in', 'conv_w', 'conv_b', 'conv_gn_w', 'conv_gn_b', 'gdn_conv_w', 'gdn_a_log', 'gdn_dt_bias', 'gdn_norm_w', 'w_out', 'norm_ffn_w', 'w_ffn_in', 'w_ffn_out', 'norm_final_w']
TWIN_WEIGHTS = ['w_ada', 'b_ada', 'norm_mix_w', 'w_in', 'conv_w', 'conv_b', 'conv_gn_w', 'conv_gn_b', 'gdn_conv_w', 'gdn_a_log', 'gdn_dt_bias', 'gdn_norm_w', 'w_out', 'norm_ffn_w', 'w_ffn_in', 'w_ffn_out', 'norm_final_w']
TWIN_DIFF_INPUT = 'x'
TWIN_INPUTS = ['x', 'c', 'w_ada', 'b_ada', 'norm_mix_w', 'w_in', 'conv_w', 'conv_b', 'conv_gn_w', 'conv_gn_b', 'gdn_conv_w', 'gdn_a_log', 'gdn_dt_bias', 'gdn_norm_w', 'w_out', 'norm_ffn_w', 'w_ffn_in', 'w_ffn_out', 'norm_final_w', 'loss_target', 'm_w_ada', 'm_b_ada', 'm_norm_mix_w', 'm_w_in', 'm_conv_w', 'm_conv_b', 'm_conv_gn_w', 'm_conv_gn_b', 'm_gdn_conv_w', 'm_gdn_a_log', 'm_gdn_dt_bias', 'm_gdn_norm_w', 'm_w_out', 'm_norm_ffn_w', 'm_w_ffn_in', 'm_w_ffn_out', 'm_norm_final_w', 'v_w_ada', 'v_b_ada', 'v_norm_mix_w', 'v_w_in', 'v_conv_w', 'v_conv_b', 'v_conv_gn_w', 'v_conv_gn_b', 'v_gdn_conv_w', 'v_gdn_a_log', 'v_gdn_dt_bias', 'v_gdn_norm_w', 'v_w_out', 'v_norm_ffn_w', 'v_w_ffn_in', 'v_w_ffn_out', 'v_norm_final_w']
TWIN_OUTPUTS = ['loss', 'grad_x', 'grad_w_ada', 'grad_b_ada', 'grad_norm_mix_w', 'grad_w_in', 'grad_conv_w', 'grad_conv_b', 'grad_conv_gn_w', 'grad_conv_gn_b', 'grad_gdn_conv_w', 'grad_gdn_a_log', 'grad_gdn_dt_bias', 'grad_gdn_norm_w', 'grad_w_out', 'grad_norm_ffn_w', 'grad_w_ffn_in', 'grad_w_ffn_out', 'grad_norm_final_w', 'delta_w_ada', 'delta_b_ada', 'delta_norm_mix_w', 'delta_w_in', 'delta_conv_w', 'delta_conv_b', 'delta_conv_gn_w', 'delta_conv_gn_b', 'delta_gdn_conv_w', 'delta_gdn_a_log', 'delta_gdn_dt_bias', 'delta_gdn_norm_w', 'delta_w_out', 'delta_norm_ffn_w', 'delta_w_ffn_in', 'delta_w_ffn_out', 'delta_norm_final_w', 'new_m_w_ada', 'new_m_b_ada', 'new_m_norm_mix_w', 'new_m_w_in', 'new_m_conv_w', 'new_m_conv_b', 'new_m_conv_gn_w', 'new_m_conv_gn_b', 'new_m_gdn_conv_w', 'new_m_gdn_a_log', 'new_m_gdn_dt_bias', 'new_m_gdn_norm_w', 'new_m_w_out', 'new_m_norm_ffn_w', 'new_m_w_ffn_in', 'new_m_w_ffn_out', 'new_m_norm_final_w', 'new_v_w_ada', 'new_v_b_ada', 'new_v_norm_mix_w', 'new_v_w_in', 'new_v_conv_w', 'new_v_conv_b', 'new_v_conv_gn_w', 'new_v_conv_gn_b', 'new_v_gdn_conv_w', 'new_v_gdn_a_log', 'new_v_gdn_dt_bias', 'new_v_gdn_norm_w', 'new_v_w_out', 'new_v_norm_ffn_w', 'new_v_w_ffn_in', 'new_v_w_ffn_out', 'new_v_norm_final_w']
TWIN_LEAF_KINDS = {'loss': 'loss', 'grad_x': 'grad_x', 'grad_w_ada': 'grad_w', 'grad_b_ada': 'grad_w', 'grad_norm_mix_w': 'grad_w', 'grad_w_in': 'grad_w', 'grad_conv_w': 'grad_w', 'grad_conv_b': 'grad_w', 'grad_conv_gn_w': 'grad_w', 'grad_conv_gn_b': 'grad_w', 'grad_gdn_conv_w': 'grad_w', 'grad_gdn_a_log': 'grad_w', 'grad_gdn_dt_bias': 'grad_w', 'grad_gdn_norm_w': 'grad_w', 'grad_w_out': 'grad_w', 'grad_norm_ffn_w': 'grad_w', 'grad_w_ffn_in': 'grad_w', 'grad_w_ffn_out': 'grad_w', 'grad_norm_final_w': 'grad_w', 'delta_w_ada': 'delta_w', 'delta_b_ada': 'delta_w', 'delta_norm_mix_w': 'delta_w', 'delta_w_in': 'delta_w', 'delta_conv_w': 'delta_w', 'delta_conv_b': 'delta_w', 'delta_conv_gn_w': 'delta_w', 'delta_conv_gn_b': 'delta_w', 'delta_gdn_conv_w': 'delta_w', 'delta_gdn_a_log': 'delta_w', 'delta_gdn_dt_bias': 'delta_w', 'delta_gdn_norm_w': 'delta_w', 'delta_w_out': 'delta_w', 'delta_norm_ffn_w': 'delta_w', 'delta_w_ffn_in': 'delta_w', 'delta_w_ffn_out': 'delta_w', 'delta_norm_final_w': 'delta_w', 'new_m_w_ada': 'new_m', 'new_m_b_ada': 'new_m', 'new_m_norm_mix_w': 'new_m', 'new_m_w_in': 'new_m', 'new_m_conv_w': 'new_m', 'new_m_conv_b': 'new_m', 'new_m_conv_gn_w': 'new_m', 'new_m_conv_gn_b': 'new_m', 'new_m_gdn_conv_w': 'new_m', 'new_m_gdn_a_log': 'new_m', 'new_m_gdn_dt_bias': 'new_m', 'new_m_gdn_norm_w': 'new_m', 'new_m_w_out': 'new_m', 'new_m_norm_ffn_w': 'new_m', 'new_m_w_ffn_in': 'new_m', 'new_m_w_ffn_out': 'new_m', 'new_m_norm_final_w': 'new_m', 'new_v_w_ada': 'new_v', 'new_v_b_ada': 'new_v', 'new_v_norm_mix_w': 'new_v', 'new_v_w_in': 'new_v', 'new_v_conv_w': 'new_v', 'new_v_conv_b': 'new_v', 'new_v_conv_gn_w': 'new_v', 'new_v_conv_gn_b': 'new_v', 'new_v_gdn_conv_w': 'new_v', 'new_v_gdn_a_log': 'new_v', 'new_v_gdn_dt_bias': 'new_v', 'new_v_gdn_norm_w': 'new_v', 'new_v_w_out': 'new_v', 'new_v_norm_ffn_w': 'new_v', 'new_v_w_ffn_in': 'new_v', 'new_v_w_ffn_out': 'new_v', 'new_v_norm_final_w': 'new_v'}


def _forward(args):
    return _fwd_reference(*[args[k] for k in FWD_PARAMS])


def _output_shape():
    out = _jax.eval_shape(lambda: _forward(_fwd_setup_inputs(0)))
    return out.shape, out.dtype

N_MICROBATCH = 1
ADAM_LR = 0.001
ADAM_B1 = 0.9
ADAM_B2 = 0.999
ADAM_EPS = 1e-08
ADAM_WD = 0.01
ADAM_STEP = 10
PER_EXAMPLE_BATCH_AXIS = {'x': 0, 'c': 0, 'loss_target': 0}
SHARED_INPUTS = []
_WEIGHT_DTYPES = {'w_ada': _jnp.float32, 'b_ada': _jnp.float32, 'norm_mix_w': _jnp.float32, 'w_in': _jnp.float32, 'conv_w': _jnp.float32, 'conv_b': _jnp.float32, 'conv_gn_w': _jnp.float32, 'conv_gn_b': _jnp.float32, 'gdn_conv_w': _jnp.float32, 'gdn_a_log': _jnp.float32, 'gdn_dt_bias': _jnp.float32, 'gdn_norm_w': _jnp.float32, 'w_out': _jnp.float32, 'norm_ffn_w': _jnp.float32, 'w_ffn_in': _jnp.float32, 'w_ffn_out': _jnp.float32, 'norm_final_w': _jnp.float32}
MOMENT_SCALE = {'w_ada': 3.526745e-02, 'b_ada': 5.809576e-02, 'norm_mix_w': 3.330150e-02, 'w_in': 2.033205e-02, 'conv_w': 2.416793e-02, 'conv_b': 4.700263e-02, 'conv_gn_w': 2.829553e-02, 'conv_gn_b': 2.338337e-02, 'gdn_conv_w': 1.871141e-02, 'gdn_a_log': 1.576870e-01, 'gdn_dt_bias': 1.556985e-01, 'gdn_norm_w': 4.829773e-02, 'w_out': 2.416497e-02, 'norm_ffn_w': 3.656246e-02, 'w_ffn_in': 1.633252e-02, 'w_ffn_out': 2.661759e-02, 'norm_final_w': 1.601111e+01}


def _to_microbatches(a, axis):
    t = _jnp.moveaxis(a, axis, 0)
    t = t.reshape((N_MICROBATCH, t.shape[0] // N_MICROBATCH) + t.shape[1:])
    return _jnp.moveaxis(t, 1, axis + 1)


def setup_inputs(seed: int = 0) -> dict:
    inp = _fwd_setup_inputs(seed)
    key = _jax.random.fold_in(_jax.random.key(seed), 7919)
    shape, _ = _output_shape()
    out = dict(inp)
    out["loss_target"] = _jax.random.normal(_jax.random.fold_in(key, 0), shape, _jnp.float32)
    for i, name in enumerate(TWIN_WEIGHTS):
        w = inp[name].astype(_jnp.float32)
        if MOMENT_SCALE is None:
            s = _jnp.sqrt(_jnp.mean(_jnp.square(w)) + 1e-30)
        else:
            s = MOMENT_SCALE[name]
        km, kv = _jax.random.split(_jax.random.fold_in(key, i + 1))
        out[name] = w
        out["m_" + name] = s * _jax.random.normal(km, w.shape, _jnp.float32)
        out["v_" + name] = (s * s) * _jax.random.uniform(kv, w.shape, _jnp.float32, 0.5, 1.5)
    if N_MICROBATCH > 1:
        for name, axis in PER_EXAMPLE_BATCH_AXIS.items():
            out[name] = _to_microbatches(out[name], axis)
    return {'x': out['x'], 'c': out['c'], 'w_ada': out['w_ada'], 'b_ada': out['b_ada'], 'norm_mix_w': out['norm_mix_w'], 'w_in': out['w_in'], 'conv_w': out['conv_w'], 'conv_b': out['conv_b'], 'conv_gn_w': out['conv_gn_w'], 'conv_gn_b': out['conv_gn_b'], 'gdn_conv_w': out['gdn_conv_w'], 'gdn_a_log': out['gdn_a_log'], 'gdn_dt_bias': out['gdn_dt_bias'], 'gdn_norm_w': out['gdn_norm_w'], 'w_out': out['w_out'], 'norm_ffn_w': out['norm_ffn_w'], 'w_ffn_in': out['w_ffn_in'], 'w_ffn_out': out['w_ffn_out'], 'norm_final_w': out['norm_final_w'], 'loss_target': out['loss_target'], 'm_w_ada': out['m_w_ada'], 'm_b_ada': out['m_b_ada'], 'm_norm_mix_w': out['m_norm_mix_w'], 'm_w_in': out['m_w_in'], 'm_conv_w': out['m_conv_w'], 'm_conv_b': out['m_conv_b'], 'm_conv_gn_w': out['m_conv_gn_w'], 'm_conv_gn_b': out['m_conv_gn_b'], 'm_gdn_conv_w': out['m_gdn_conv_w'], 'm_gdn_a_log': out['m_gdn_a_log'], 'm_gdn_dt_bias': out['m_gdn_dt_bias'], 'm_gdn_norm_w': out['m_gdn_norm_w'], 'm_w_out': out['m_w_out'], 'm_norm_ffn_w': out['m_norm_ffn_w'], 'm_w_ffn_in': out['m_w_ffn_in'], 'm_w_ffn_out': out['m_w_ffn_out'], 'm_norm_final_w': out['m_norm_final_w'], 'v_w_ada': out['v_w_ada'], 'v_b_ada': out['v_b_ada'], 'v_norm_mix_w': out['v_norm_mix_w'], 'v_w_in': out['v_w_in'], 'v_conv_w': out['v_conv_w'], 'v_conv_b': out['v_conv_b'], 'v_conv_gn_w': out['v_conv_gn_w'], 'v_conv_gn_b': out['v_conv_gn_b'], 'v_gdn_conv_w': out['v_gdn_conv_w'], 'v_gdn_a_log': out['v_gdn_a_log'], 'v_gdn_dt_bias': out['v_gdn_dt_bias'], 'v_gdn_norm_w': out['v_gdn_norm_w'], 'v_w_out': out['v_w_out'], 'v_norm_ffn_w': out['v_norm_ffn_w'], 'v_w_ffn_in': out['v_w_ffn_in'], 'v_w_ffn_out': out['v_w_ffn_out'], 'v_norm_final_w': out['v_norm_final_w']}


def _loss(weights, diff, rest, loss_target):
    with _jax.named_scope("forward"):
        args = {**rest, TWIN_DIFF_INPUT: diff, **{k: w.astype(_WEIGHT_DTYPES[k]) for k, w in weights.items()}}
        y = _forward(args)
    with _jax.named_scope("loss_head"):
        err = _jnp.square(y.astype(_jnp.float32) - loss_target)
        return 0.5 * _jnp.sum(_jnp.mean(err, axis=-1)) if err.ndim else 0.5 * err


def _adamw(w, g, m, v):
    m = ADAM_B1 * m + (1.0 - ADAM_B1) * g
    v = ADAM_B2 * v + (1.0 - ADAM_B2) * _jnp.square(g)
    m_hat = m / (1.0 - ADAM_B1 ** ADAM_STEP)
    v_hat = v / (1.0 - ADAM_B2 ** ADAM_STEP)
    delta = -ADAM_LR * (m_hat / (_jnp.sqrt(v_hat) + ADAM_EPS) + ADAM_WD * w)
    return delta, m, v


def reference(x, c, w_ada, b_ada, norm_mix_w, w_in, conv_w, conv_b, conv_gn_w, conv_gn_b, gdn_conv_w, gdn_a_log, gdn_dt_bias, gdn_norm_w, w_out, norm_ffn_w, w_ffn_in, w_ffn_out, norm_final_w, loss_target, m_w_ada, m_b_ada, m_norm_mix_w, m_w_in, m_conv_w, m_conv_b, m_conv_gn_w, m_conv_gn_b, m_gdn_conv_w, m_gdn_a_log, m_gdn_dt_bias, m_gdn_norm_w, m_w_out, m_norm_ffn_w, m_w_ffn_in, m_w_ffn_out, m_norm_final_w, v_w_ada, v_b_ada, v_norm_mix_w, v_w_in, v_conv_w, v_conv_b, v_conv_gn_w, v_conv_gn_b, v_gdn_conv_w, v_gdn_a_log, v_gdn_dt_bias, v_gdn_norm_w, v_w_out, v_norm_ffn_w, v_w_ffn_in, v_w_ffn_out, v_norm_final_w):
    given = dict(x=x, c=c, w_ada=w_ada, b_ada=b_ada, norm_mix_w=norm_mix_w, w_in=w_in, conv_w=conv_w, conv_b=conv_b, conv_gn_w=conv_gn_w, conv_gn_b=conv_gn_b, gdn_conv_w=gdn_conv_w, gdn_a_log=gdn_a_log, gdn_dt_bias=gdn_dt_bias, gdn_norm_w=gdn_norm_w, w_out=w_out, norm_ffn_w=norm_ffn_w, w_ffn_in=w_ffn_in, w_ffn_out=w_ffn_out, norm_final_w=norm_final_w, loss_target=loss_target, m_w_ada=m_w_ada, m_b_ada=m_b_ada, m_norm_mix_w=m_norm_mix_w, m_w_in=m_w_in, m_conv_w=m_conv_w, m_conv_b=m_conv_b, m_conv_gn_w=m_conv_gn_w, m_conv_gn_b=m_conv_gn_b, m_gdn_conv_w=m_gdn_conv_w, m_gdn_a_log=m_gdn_a_log, m_gdn_dt_bias=m_gdn_dt_bias, m_gdn_norm_w=m_gdn_norm_w, m_w_out=m_w_out, m_norm_ffn_w=m_norm_ffn_w, m_w_ffn_in=m_w_ffn_in, m_w_ffn_out=m_w_ffn_out, m_norm_final_w=m_norm_final_w, v_w_ada=v_w_ada, v_b_ada=v_b_ada, v_norm_mix_w=v_norm_mix_w, v_w_in=v_w_in, v_conv_w=v_conv_w, v_conv_b=v_conv_b, v_conv_gn_w=v_conv_gn_w, v_conv_gn_b=v_conv_gn_b, v_gdn_conv_w=v_gdn_conv_w, v_gdn_a_log=v_gdn_a_log, v_gdn_dt_bias=v_gdn_dt_bias, v_gdn_norm_w=v_gdn_norm_w, v_w_out=v_w_out, v_norm_ffn_w=v_norm_ffn_w, v_w_ffn_in=v_w_ffn_in, v_w_ffn_out=v_w_ffn_out, v_norm_final_w=v_norm_final_w)
    weights = {n: given[n] for n in TWIN_WEIGHTS}
    shared = {n: given[n] for n in SHARED_INPUTS}
    per_example = {n: given[n] for n in ['x', 'c']}
    grad_fn = _jax.value_and_grad(_loss, argnums=(0, 1))

    def one_microbatch(ex, loss_target):
        ex = dict(ex)
        diff = ex.pop(TWIN_DIFF_INPUT)
        return grad_fn(weights, diff, {**shared, **ex}, loss_target)

    if N_MICROBATCH == 1:
        loss, (grad_w, grad_x) = one_microbatch(per_example, given["loss_target"])
    else:
        def body(carry, xs):
            loss_sum, grad_sum = carry
            l_k, (gw_k, gx_k) = one_microbatch(xs[0], xs[1])
            with _jax.named_scope("update"):
                return (loss_sum + l_k, _jax.tree.map(_jnp.add, grad_sum, gw_k)), gx_k

        init = (_jnp.zeros((), _jnp.float32), _jax.tree.map(_jnp.zeros_like, weights))
        (loss, grad_w), grad_x = _jax.lax.scan(body, init, (per_example, given["loss_target"]))
    with _jax.named_scope("update"):
        delta_w, new_m, new_v = {}, {}, {}
        for n in TWIN_WEIGHTS:
            delta_w[n], new_m[n], new_v[n] = _adamw(weights[n], grad_w[n], given["m_" + n], given["v_" + n])
    return (loss, grad_x, *[grad_w[n] for n in TWIN_WEIGHTS], *[delta_w[n] for n in TWIN_WEIGHTS],
            *[new_m[n] for n in TWIN_WEIGHTS], *[new_v[n] for n in TWIN_WEIGHTS])
```

```python
import functools

import jax
import jax.numpy as jnp
from jax import lax
from jax.experimental import pallas as pl
from jax.experimental.pallas import tpu as pltpu

F32 = jnp.float32
BF16 = jnp.bfloat16
HI = lax.Precision.HIGHEST
MESH = pl.DeviceIdType.MESH

N_DEV = 8
S = 2048
D = 1024
TM = 256
NT = S // TM
CW = 512
KC = 31
NG = 8
GSZ = CW // NG
HALO = 32
GW = 512
NH = 4
DH = 128
KS = 4
SH = 8
CL = 64
NCH = S // CL
NMAIN = 2 * CW + 4 * GW
NIN = NMAIN + 2 * NH
DFF = 2816
FB = DFF // 4
EPS = 1e-6
QSCALE = DH ** -0.5
LANES = 128
SMALL_ROWS = 88

ADAM_LR = 0.001
ADAM_B1 = 0.9
ADAM_B2 = 0.999
ADAM_EPS = 1e-08
ADAM_WD = 0.01
ADAM_STEP = 10
BC1 = 1.0 - ADAM_B1 ** ADAM_STEP
BC2 = 1.0 - ADAM_B2 ** ADAM_STEP

MIB = 1024 * 1024
VMEM_LIMIT_MIB = 32


def _params(limit_mib=VMEM_LIMIT_MIB, **kw):
    return pltpu.CompilerParams(vmem_limit_bytes=limit_mib * MIB, **kw)


def _sig(x):
    return jax.nn.sigmoid(x)


def _dot(a, b, prec=None):
    return jnp.dot(a, b, preferred_element_type=F32, precision=prec)


def _dot_nt(a, b, prec=None):
    return lax.dot_general(a, b, (((1,), (1,)), ((), ())), preferred_element_type=F32, precision=prec)


def _dot_tn(a, b, prec=None):
    return lax.dot_general(a, b, (((0,), (0,)), ((), ())), preferred_element_type=F32, precision=prec)


def _rowsum(x):
    return jnp.sum(x, axis=-1, keepdims=True)


def _colsum(x):
    return jnp.sum(x, axis=0, keepdims=True)


def _mod(mod_ref, b_ref, k):
    return mod_ref[:, k * D:(k + 1) * D] + b_ref[:, k * D:(k + 1) * D]


def _const(shape):
    nd = len(shape)
    return pl.BlockSpec(shape, lambda *_: (0,) * nd)


def _const1(shape):
    nd = len(shape)
    return pl.BlockSpec(shape, lambda *_: (0,) * nd, pipeline_mode=pl.Buffered(1))


PEER_FLIPS = [(dx, dy, dc) for dx in (0, 1) for dy in (0, 1) for dc in (0, 1)][1:]


def _exchange(name, srcs, per_dest):
    n = len(srcs)
    out_shape = []
    for a, pd in zip(srcs, per_dest):
        blk = a.shape[1:] if pd else a.shape
        out_shape.append(jax.ShapeDtypeStruct((N_DEV,) + tuple(blk), a.dtype))

    def body(*refs):
        src = refs[:n]
        dst = refs[n:2 * n]
        send_sems, recv_sems, local_sems = refs[2 * n:]
        x, y, c = lax.axis_index("x"), lax.axis_index("y"), lax.axis_index("c")
        me = 4 * x + 2 * y + c

        def piece(i, j):
            return src[i].at[j] if per_dest[i] else src[i]

        copies = []
        for k, (dx, dy, dc) in enumerate(PEER_FLIPS):
            px = 1 - x if dx else x
            py = 1 - y if dy else y
            pc = 1 - c if dc else c
            pj = 4 * px + 2 * py + pc
            for i in range(n):
                cp = pltpu.make_async_remote_copy(
                    src_ref=piece(i, pj), dst_ref=dst[i].at[me],
                    send_sem=send_sems.at[k * n + i], recv_sem=recv_sems.at[k * n + i],
                    device_id=(px, py, pc), device_id_type=MESH)
                cp.start()
                arrive = pltpu.make_async_remote_copy(
                    src_ref=piece(i, pj), dst_ref=dst[i].at[pj],
                    send_sem=send_sems.at[k * n + i], recv_sem=recv_sems.at[k * n + i],
                    device_id=(px, py, pc), device_id_type=MESH)
                copies.append((cp, arrive))
        own = []
        for i in range(n):
            lc = pltpu.make_async_copy(piece(i, me), dst[i].at[me], local_sems.at[i])
            lc.start()
            own.append(lc)
        for cp, arrive in copies:
            arrive.wait_recv()
        for cp, arrive in copies:
            cp.wait_send()
        for lc in own:
            lc.wait()

    any_spec = pl.BlockSpec(memory_space=pl.ANY)
    return pl.pallas_call(
        body, name=name, out_shape=tuple(out_shape),
        in_specs=[any_spec] * n, out_specs=tuple([any_spec] * n),
        scratch_shapes=[pltpu.SemaphoreType.DMA((7 * n,)), pltpu.SemaphoreType.DMA((7 * n,)),
                        pltpu.SemaphoreType.DMA((n,))],
        compiler_params=pltpu.CompilerParams(has_side_effects=True),
    )(*srcs)


def _mod_shard(c_all, w_ada):
    def body(c_ref, w_ref, o_ref):
        cv = c_ref[...]
        ca = cv * _sig(cv)
        o_ref[...] = _dot(ca.astype(BF16), w_ref[...].astype(BF16))

    return pl.pallas_call(
        body, name="mod_shard", out_shape=jax.ShapeDtypeStruct((N_DEV, w_ada.shape[1]), F32),
        compiler_params=_params(),
    )(c_all, w_ada)


def _fwd_in(x, nw1, modnb, bada, w_main, w_ba):
    def body(x_ref, nw_ref, mod_ref, b_ref, wm_ref, wb_ref, pm_ref, pb_ref, hb_ref):
        xv = x_ref[...]
        r = lax.rsqrt(jnp.mean(xv * xv, axis=-1, keepdims=True) + EPS)
        h = (xv * r * nw_ref[...]) * (1.0 + _mod(mod_ref, b_ref, 1)) + _mod(mod_ref, b_ref, 0)
        hb = h.astype(BF16)
        hb_ref[...] = hb
        pm_ref[...] = _dot(hb, wm_ref[...])
        pb_ref[...] = _dot(hb, wb_ref[...])

    return pl.pallas_call(
        body, name="fwd_in", grid=(NT,),
        in_specs=[pl.BlockSpec((TM, D), lambda i: (i, 0)), _const((1, D)), _const((1, 6 * D)), _const((1, 6 * D)),
                  _const((D, NMAIN)), _const((D, LANES))],
        out_specs=(pl.BlockSpec((TM, NMAIN), lambda i: (i, 0)), pl.BlockSpec((TM, LANES), lambda i: (i, 0)),
                   pl.BlockSpec((TM, D), lambda i: (i, 0))),
        out_shape=(jax.ShapeDtypeStruct((S, NMAIN), F32), jax.ShapeDtypeStruct((S, LANES), F32),
                   jax.ShapeDtypeStruct((S, D), BF16)),
        compiler_params=_params(dimension_semantics=("arbitrary",)),
    )(x, nw1, modnb, bada, w_main, w_ba)


def _group_mean_matrix():
    ii = lax.broadcasted_iota(jnp.int32, (CW, CW), 0) // GSZ
    jj = lax.broadcasted_iota(jnp.int32, (CW, CW), 1) // GSZ
    return jnp.where(ii == jj, 1.0 / GSZ, 0.0).astype(F32)


def _conf_fwd(p_main, conv_w, conv_b, gn_w, gn_b):
    def body(a_ref, g_ref, w_ref, b_ref, gw_ref, gb_ref, y_ref, oa_ref, ubuf):
        i = pl.program_id(0)

        @pl.when(i == 0)
        def _():
            ubuf[0:HALO, :] = jnp.zeros((HALO, CW), F32)

        ubuf[HALO:HALO + TM, :] = a_ref[...] * _sig(g_ref[...])
        acc = jnp.zeros((TM, CW), F32) + b_ref[...]
        for k in range(KC):
            off = HALO - (KC - 1) + k
            acc = acc + w_ref[k:k + 1, :] * ubuf[off:off + TM, :]
        y_ref[...] = acc
        ubuf[0:HALO, :] = ubuf[TM:TM + HALO, :]
        pm = _group_mean_matrix()
        dlt = acc - _dot(acc, pm, HI)
        var = _dot(dlt * dlt, pm, HI)
        o = dlt * lax.rsqrt(var + EPS) * gw_ref[...] + gb_ref[...]
        oa_ref[...] = o * _sig(o)

    return pl.pallas_call(
        body, name="conf_fwd", grid=(NT,),
        in_specs=[pl.BlockSpec((TM, CW), lambda i: (i, 0)), pl.BlockSpec((TM, CW), lambda i: (i, 1)),
                  _const((KC, CW)), _const((1, CW)), _const((1, CW)), _const((1, CW))],
        out_specs=(pl.BlockSpec((TM, CW), lambda i: (i, 0)), pl.BlockSpec((TM, CW), lambda i: (i, 0))),
        out_shape=(jax.ShapeDtypeStruct((S, CW), F32), jax.ShapeDtypeStruct((S, CW), F32)),
        scratch_shapes=[pltpu.VMEM((HALO + TM, CW), F32)],
        compiler_params=_params(dimension_semantics=("arbitrary",)),
    )(p_main, p_main, conv_w, conv_b, gn_w, gn_b)


def _tri_iota():
    ii = lax.broadcasted_iota(jnp.int32, (CL, CL), 0)
    jj = lax.broadcasted_iota(jnp.int32, (CL, CL), 1)
    return ii, jj


def _gdn_gates(ba, alog_l, dt_l):
    beta_all = _sig(ba)
    xg = ba + dt_l
    sp = jnp.maximum(xg, 0.0) + jnp.log(1.0 + jnp.exp(-jnp.abs(xg)))
    neg_a = -jnp.exp(alog_l)
    return beta_all, neg_a * sp, xg, neg_a


def _gdn_cumsum(g_all):
    ii, jj = _tri_iota()
    low = jnp.where(ii >= jj, 1.0, 0.0).astype(F32)
    gcum = _dot(low, g_all, HI)
    return gcum, jnp.transpose(gcum)


def _unit_lower_inverse(a):
    ii, jj = _tri_iota()
    t = jnp.where(ii == jj, 1.0, 0.0).astype(F32) - a
    p = _dot(a, a, HI)
    for _ in range(4):
        t = t + _dot(t, p, HI)
        p = _dot(p, p, HI)
    return t + _dot(t, p, HI)


def _head_terms(qh, kh, beta, gcol, grow):
    ii, jj = _tri_iota()
    causal = ii >= jj
    strict = ii > jj
    rq = lax.rsqrt(_rowsum(qh * qh) + EPS)
    rk = lax.rsqrt(_rowsum(kh * kh) + EPS)
    qn = qh * rq
    kn = kh * rk
    qs = qn * QSCALE
    decay = jnp.where(causal, jnp.exp(jnp.where(causal, gcol - grow, 0.0)), 0.0)
    gam = jnp.exp(gcol)
    gl = gcol[CL - 1:CL, :]
    kds = jnp.exp(gl - gcol)
    cd = jnp.exp(gl)
    kb = kn * beta
    a = jnp.where(strict, _dot_nt(kb, kn, HI) * decay, 0.0)
    qk = jnp.where(causal, _dot_nt(qs, kn, HI) * decay, 0.0)
    return dict(rq=rq, rk=rk, qn=qn, kn=kn, qs=qs, decay=decay, gam=gam, kds=kds, cd=cd, kb=kb, a=a, qk=qk,
                causal=causal, strict=strict)


def _short_conv(w_ref, buf):
    acc = w_ref[0:1, :] * buf[SH - KS + 1:SH - KS + 1 + CL, :]
    for k in range(1, KS):
        off = SH - (KS - 1) + k
        acc = acc + w_ref[k:k + 1, :] * buf[off:off + CL, :]
    return acc


def _gdn_fwd(p_main, p_ba, gdn_conv_w, alog_l, dt_l, gdn_nw):
    def body(q_ref, k_ref, v_ref, z_ref, ba_ref, w_ref, al_ref, dt_ref, nw_ref,
             ob_ref, o_ref, sin_ref, t_ref, cbuf, state):
        n = pl.program_id(0)

        @pl.when(n == 0)
        def _():
            cbuf[0:SH, :] = jnp.zeros((SH, 3 * GW), F32)
            state[...] = jnp.zeros((NH, DH, DH), F32)

        cbuf[SH:SH + CL, 0:GW] = q_ref[...]
        cbuf[SH:SH + CL, GW:2 * GW] = k_ref[...]
        cbuf[SH:SH + CL, 2 * GW:3 * GW] = v_ref[...]
        conv = _short_conv(w_ref, cbuf)
        cbuf[0:SH, :] = cbuf[CL:CL + SH, :]
        qkv = conv * _sig(conv)
        beta_all, g_all, _, _ = _gdn_gates(ba_ref[...], al_ref[...], dt_ref[...])
        gcum, gcum_t = _gdn_cumsum(g_all)
        for h in range(NH):
            lo = h * DH
            qh = qkv[:, lo:lo + DH]
            kh = qkv[:, GW + lo:GW + lo + DH]
            vh = qkv[:, 2 * GW + lo:2 * GW + lo + DH]
            beta = beta_all[:, h:h + 1]
            f = _head_terms(qh, kh, beta, gcum[:, NH + h:NH + h + 1], gcum_t[NH + h:NH + h + 1, :])
            t = _unit_lower_inverse(f["a"])
            u = _dot(t, vh * beta, HI)
            w = _dot(t, f["kb"] * f["gam"], HI)
            st = state[h]
            sin_ref[0, h] = st
            t_ref[0, h] = t
            v_new = u - _dot(w, st, HI)
            o = _dot(f["qs"] * f["gam"], st, HI) + _dot(f["qk"], v_new, HI)
            state[h] = st * f["cd"] + _dot_tn(f["kn"] * f["kds"], v_new, HI)
            o_ref[:, lo:lo + DH] = o
            r = lax.rsqrt(jnp.mean(o * o, axis=-1, keepdims=True) + EPS)
            zh = z_ref[:, lo:lo + DH]
            ob_ref[:, lo:lo + DH] = o * r * nw_ref[...] * (zh * _sig(zh))

    col = lambda j: pl.BlockSpec((CL, GW), lambda n: (n, j))
    return pl.pallas_call(
        body, name="gdn_fwd", grid=(NCH,),
        in_specs=[col(2), col(3), col(4), col(5), pl.BlockSpec((CL, LANES), lambda n: (n, 0)),
                  _const((KS, 3 * GW)), _const((1, LANES)), _const((1, LANES)), _const((1, DH))],
        out_specs=(pl.BlockSpec((CL, GW), lambda n: (n, 0)), pl.BlockSpec((CL, GW), lambda n: (n, 0)),
                   pl.BlockSpec((1, NH, DH, DH), lambda n: (n, 0, 0, 0)),
                   pl.BlockSpec((1, NH, CL, CL), lambda n: (n, 0, 0, 0))),
        out_shape=(jax.ShapeDtypeStruct((S, GW), F32), jax.ShapeDtypeStruct((S, GW), F32),
                   jax.ShapeDtypeStruct((NCH, NH, DH, DH), F32), jax.ShapeDtypeStruct((NCH, NH, CL, CL), F32)),
        scratch_shapes=[pltpu.VMEM((SH + CL, 3 * GW), F32), pltpu.VMEM((NH, DH, DH), F32)],
        compiler_params=_params(dimension_semantics=("arbitrary",)),
    )(p_main, p_main, p_main, p_main, p_ba, gdn_conv_w, alog_l, dt_l, gdn_nw)


def _fwd_out(out_a, out_b, x, modnb, bada, w_out):
    def body(oa_ref, ob_ref, x_ref, mod_ref, b_ref, w_ref, x1_ref, mix_ref, oab_ref):
        oa = oa_ref[...].astype(BF16)
        ob = ob_ref[...].astype(BF16)
        oab_ref[:, 0:CW] = oa
        oab_ref[:, CW:D] = ob
        mix = _dot(oa, w_ref[0:CW, :]) + _dot(ob, w_ref[CW:D, :])
        mix_ref[...] = mix
        x1_ref[...] = x_ref[...] + _mod(mod_ref, b_ref, 2) * mix

    tile = lambda w: pl.BlockSpec((TM, w), lambda i: (i, 0))
    return pl.pallas_call(
        body, name="fwd_out", grid=(NT,),
        in_specs=[tile(CW), tile(GW), tile(D), _const((1, 6 * D)), _const((1, 6 * D)), _const((D, D))],
        out_specs=(tile(D), tile(D), tile(D)),
        out_shape=(jax.ShapeDtypeStruct((S, D), F32), jax.ShapeDtypeStruct((S, D), F32),
                   jax.ShapeDtypeStruct((S, D), BF16)),
        compiler_params=_params(dimension_semantics=("arbitrary",)),
    )(out_a, out_b, x, modnb, bada, w_out)


TF = 128
FFN_STATS = 8


def _ffn_fwd_bwd(x1, tgt, modnb, bada, nw2, nfw, w_fi, w_fo):
    def body(x1_ref, tgt_ref, mod_ref, b_ref, nw2_ref, nfw_ref, wi_ref, wo_ref,
             dx1_ref, hb_ref, act_ref, dffn_ref, df_ref, st_ref):
        i = pl.program_id(0)

        @pl.when(i == 0)
        def _():
            st_ref[...] = jnp.zeros((FFN_STATS, D), F32)

        sh2, sc2, gt2 = _mod(mod_ref, b_ref, 3), _mod(mod_ref, b_ref, 4), _mod(mod_ref, b_ref, 5)
        x1v = x1_ref[...]
        r2 = lax.rsqrt(jnp.mean(x1v * x1v, axis=-1, keepdims=True) + EPS)
        xr2 = x1v * r2
        xn2 = xr2 * nw2_ref[...]
        hb = (xn2 * (1.0 + sc2) + sh2).astype(BF16)
        hb_ref[...] = hb
        fg, fu, sg = [], [], []
        ffn = jnp.zeros((TF, D), F32)
        for j in range(4):
            fgj = _dot(hb, wi_ref[j])
            fuj = _dot(hb, wi_ref[j + 4])
            sj = _sig(fgj)
            aj = (fgj * sj * fuj).astype(BF16)
            act_ref[j] = aj
            ffn = ffn + _dot(aj, wo_ref[j])
            fg.append(fgj)
            fu.append(fuj)
            sg.append(sj)
        x2 = x1v + gt2 * ffn
        r3 = lax.rsqrt(jnp.mean(x2 * x2, axis=-1, keepdims=True) + EPS)
        xr3 = x2 * r3
        err = xr3 * nfw_ref[...] - tgt_ref[...]
        loss = 0.5 * jnp.sum(jnp.mean(err * err, axis=-1, keepdims=True), axis=0, keepdims=True)
        dy = err * (1.0 / D)
        st_ref[0:1, :] += _colsum(dy * xr3)
        dyr = dy * nfw_ref[...]
        dx2 = r3 * (dyr - xr3 * jnp.mean(dyr * xr3, axis=-1, keepdims=True))
        st_ref[1:2, :] += _colsum(dx2 * ffn)
        st_ref[5:6, :] += jnp.broadcast_to(loss, (1, D))
        dffn = (gt2 * dx2).astype(BF16)
        dffn_ref[...] = dffn
        dh = jnp.zeros((TF, D), F32)
        for j in range(4):
            dact = _dot_nt(dffn, wo_ref[j])
            dfg = (dact * fu[j] * (sg[j] * (1.0 + fg[j] * (1.0 - sg[j])))).astype(BF16)
            dfu = (dact * (fg[j] * sg[j])).astype(BF16)
            df_ref[j] = dfg
            df_ref[j + 4] = dfu
            dh = dh + _dot_nt(dfg, wi_ref[j]) + _dot_nt(dfu, wi_ref[j + 4])
        st_ref[2:3, :] += _colsum(dh)
        st_ref[3:4, :] += _colsum(dh * xn2)
        dxn = dh * (1.0 + sc2)
        st_ref[4:5, :] += _colsum(dxn * xr2)
        dxr = dxn * nw2_ref[...]
        dx1_ref[...] = dx2 + r2 * (dxr - xr2 * jnp.mean(dxr * xr2, axis=-1, keepdims=True))

    tile = lambda w: pl.BlockSpec((TF, w), lambda i: (i, 0))
    return pl.pallas_call(
        body, name="ffn_fwd_bwd", grid=(S // TF,),
        in_specs=[tile(D), tile(D), _const((1, 6 * D)), _const((1, 6 * D)), _const((1, D)), _const((1, D)),
                  _const1((N_DEV, D, FB)), _const1((4, FB, D))],
        out_specs=(tile(D), tile(D), pl.BlockSpec((4, TF, FB), lambda i: (0, i, 0)), tile(D),
                   pl.BlockSpec((N_DEV, TF, FB), lambda i: (0, i, 0)), _const((FFN_STATS, D))),
        out_shape=(jax.ShapeDtypeStruct((S, D), F32), jax.ShapeDtypeStruct((S, D), BF16),
                   jax.ShapeDtypeStruct((4, S, FB), BF16), jax.ShapeDtypeStruct((S, D), BF16),
                   jax.ShapeDtypeStruct((N_DEV, S, FB), BF16), jax.ShapeDtypeStruct((FFN_STATS, D), F32)),
        compiler_params=_params(44, dimension_semantics=("arbitrary",)),
    )(x1, tgt, modnb, bada, nw2, nfw, w_fi, w_fo)


def _grad_w(name, a, b, nb):
    m, n = a.shape[1], b.shape[1]

    def body(a_ref, b_ref, o_ref):
        o_ref[...] = _dot_tn(a_ref[...], b_ref[...])

    return pl.pallas_call(
        body, name=name, grid=(n // nb,),
        in_specs=[_const((S, m)), pl.BlockSpec((S, nb), lambda j: (0, j))],
        out_specs=pl.BlockSpec((m, nb), lambda j: (0, j)),
        out_shape=jax.ShapeDtypeStruct((m, n), F32),
        compiler_params=_params(dimension_semantics=("arbitrary",)),
    )(a, b)


def _grad_w_ffn_in(hb2, df):
    def body(a_ref, b_ref, o_ref):
        o_ref[0] = _dot_tn(a_ref[...], b_ref[0])

    return pl.pallas_call(
        body, name="grad_w_ffn_in", grid=(N_DEV,),
        in_specs=[_const((S, D)), pl.BlockSpec((1, S, FB), lambda j: (j, 0, 0))],
        out_specs=pl.BlockSpec((1, D, FB), lambda j: (j, 0, 0)),
        out_shape=jax.ShapeDtypeStruct((N_DEV, D, FB), F32),
        compiler_params=_params(dimension_semantics=("arbitrary",)),
    )(hb2, df)


def _grad_w_ffn_out(act, dffn):
    def body(a_ref, b_ref, o_ref):
        o_ref[0] = _dot_tn(a_ref[0], b_ref[...])

    return pl.pallas_call(
        body, name="grad_w_ffn_out", grid=(4,),
        in_specs=[pl.BlockSpec((1, S, FB), lambda j: (j, 0, 0)), _const((S, D))],
        out_specs=pl.BlockSpec((1, FB, D), lambda j: (j, 0, 0)),
        out_shape=jax.ShapeDtypeStruct((4, FB, D), F32),
        compiler_params=_params(dimension_semantics=("arbitrary",)),
    )(act, dffn)


def _bwd_out(dx1, mix, modnb, bada, w_out):
    def body(dx_ref, mix_ref, mod_ref, b_ref, w_ref, dmix_ref, doa_ref, dob_ref, st_ref):
        i = pl.program_id(0)

        @pl.when(i == 0)
        def _():
            st_ref[...] = jnp.zeros((8, D), F32)

        dx = dx_ref[...]
        st_ref[0:1, :] += _colsum(dx * mix_ref[...])
        dmix = (_mod(mod_ref, b_ref, 2) * dx).astype(BF16)
        dmix_ref[...] = dmix
        doa_ref[...] = _dot_nt(dmix, w_ref[0:CW, :])
        dob_ref[...] = _dot_nt(dmix, w_ref[CW:D, :])

    tile = lambda w: pl.BlockSpec((TM, w), lambda i: (i, 0))
    return pl.pallas_call(
        body, name="bwd_out", grid=(NT,),
        in_specs=[tile(D), tile(D), _const((1, 6 * D)), _const((1, 6 * D)), _const((D, D))],
        out_specs=(tile(D), tile(CW), tile(GW), _const((8, D))),
        out_shape=(jax.ShapeDtypeStruct((S, D), BF16), jax.ShapeDtypeStruct((S, CW), F32),
                   jax.ShapeDtypeStruct((S, GW), F32), jax.ShapeDtypeStruct((8, D), F32)),
        compiler_params=_params(dimension_semantics=("arbitrary",)),
    )(dx1, mix, modnb, bada, w_out)


CONF_STATS = 40


def _conf_bwd(d_out_a, y, p_main, conv_w, gn_w, gn_b):
    def body(do_ref, y_ref, a_ref, g_ref, ah_ref, gh_ref, w_ref, gw_ref, gb_ref, dp_ref, st_ref, ubuf, dybuf):
        i = pl.program_id(0)

        @pl.when(i == 0)
        def _():
            st_ref[...] = jnp.zeros((CONF_STATS, CW), F32)
            dybuf[TM:TM + HALO, :] = jnp.zeros((HALO, CW), F32)

        pm = _group_mean_matrix()
        yv = y_ref[...]
        dlt = yv - _dot(yv, pm, HI)
        rstd = lax.rsqrt(_dot(dlt * dlt, pm, HI) + EPS)
        un = dlt * rstd
        o = un * gw_ref[...] + gb_ref[...]
        so = _sig(o)
        d_o = do_ref[...] * (so * (1.0 + o * (1.0 - so)))
        st_ref[33:34, :] += _colsum(d_o)
        st_ref[32:33, :] += _colsum(d_o * un)
        dun = d_o * gw_ref[...]
        dy = rstd * (dun - _dot(dun, pm, HI) - un * _dot(dun * un, pm, HI))
        st_ref[31:32, :] += _colsum(dy)
        dybuf[0:TM, :] = dy

        a = a_ref[...]
        sg = _sig(g_ref[...])
        first = i == NT - 1
        ubuf[0:HALO, :] = jnp.where(first, 0.0, ah_ref[...] * _sig(gh_ref[...]))
        ubuf[HALO:HALO + TM, :] = a * sg
        du = jnp.zeros((TM, CW), F32)
        for k in range(KC):
            off = HALO - (KC - 1) + k
            st_ref[k:k + 1, :] += _colsum(dy * ubuf[off:off + TM, :])
            du = du + w_ref[k:k + 1, :] * dybuf[KC - 1 - k:KC - 1 - k + TM, :]
        dybuf[TM:TM + HALO, :] = dybuf[0:HALO, :]
        dp_ref[:, 0:CW] = (du * sg).astype(BF16)
        dp_ref[:, CW:2 * CW] = (du * a * sg * (1.0 - sg)).astype(BF16)

    rev = lambda w, j=0: pl.BlockSpec((TM, w), lambda i: (NT - 1 - i, j))
    halo = lambda j: pl.BlockSpec((HALO, CW), lambda i: (jnp.maximum((NT - 1 - i) * (TM // HALO) - 1, 0), j))
    return pl.pallas_call(
        body, name="conf_bwd", grid=(NT,),
        in_specs=[rev(CW), rev(CW), rev(CW, 0), rev(CW, 1), halo(0), halo(1),
                  _const((KC, CW)), _const((1, CW)), _const((1, CW))],
        out_specs=(rev(2 * CW), _const((CONF_STATS, CW))),
        out_shape=(jax.ShapeDtypeStruct((S, 2 * CW), BF16), jax.ShapeDtypeStruct((CONF_STATS, CW), F32)),
        scratch_shapes=[pltpu.VMEM((HALO + TM, CW), F32), pltpu.VMEM((TM + HALO, CW), F32)],
        compiler_params=_params(dimension_semantics=("arbitrary",)),
    )(d_out_a, y, p_main, p_main, p_main, p_main, conv_w, gn_w, gn_b)


GDN_STATS = 8


def _gdn_bwd(d_out_b, o_pre, s_in, t_inv, p_main, p_ba, gdn_conv_w, alog_l, dt_l, gdn_nw):
    def body(dob_ref, o_ref, sin_ref, t_ref, q_ref, k_ref, v_ref, z_ref, qh_ref, kh_ref, vh_ref, ba_ref,
             w_ref, al_ref, dt_ref, nw_ref, dp_ref, dba_ref, st_ref, xbuf, dcbuf, dstate):
        n = pl.program_id(0)

        @pl.when(n == 0)
        def _():
            st_ref[...] = jnp.zeros((GDN_STATS, 3 * GW), F32)
            dcbuf[CL:CL + SH, :] = jnp.zeros((SH, 3 * GW), F32)
            dstate[...] = jnp.zeros((NH, DH, DH), F32)

        first = n == NCH - 1
        xbuf[0:SH, 0:GW] = jnp.where(first, 0.0, qh_ref[...])
        xbuf[0:SH, GW:2 * GW] = jnp.where(first, 0.0, kh_ref[...])
        xbuf[0:SH, 2 * GW:3 * GW] = jnp.where(first, 0.0, vh_ref[...])
        xbuf[SH:SH + CL, 0:GW] = q_ref[...]
        xbuf[SH:SH + CL, GW:2 * GW] = k_ref[...]
        xbuf[SH:SH + CL, 2 * GW:3 * GW] = v_ref[...]
        conv = _short_conv(w_ref, xbuf)
        sc = _sig(conv)
        qkv = conv * sc
        ba = ba_ref[...]
        beta_all, g_all, xg, neg_a = _gdn_gates(ba, al_ref[...], dt_ref[...])
        gcum, gcum_t = _gdn_cumsum(g_all)
        lane = lax.broadcasted_iota(jnp.int32, (CL, LANES), 1)
        row = lax.broadcasted_iota(jnp.int32, (CL, 1), 0)
        dgcum_all = jnp.zeros((CL, LANES), F32)
        dbeta_all = jnp.zeros((CL, LANES), F32)
        for h in range(NH):
            lo = h * DH
            qh = qkv[:, lo:lo + DH]
            kh = qkv[:, GW + lo:GW + lo + DH]
            vh = qkv[:, 2 * GW + lo:2 * GW + lo + DH]
            beta = beta_all[:, h:h + 1]
            f = _head_terms(qh, kh, beta, gcum[:, NH + h:NH + h + 1], gcum_t[NH + h:NH + h + 1, :])
            qn, kn, qs, kb, gam, kds, cd, decay = (f[s] for s in ("qn", "kn", "qs", "kb", "gam", "kds", "cd", "decay"))
            t = t_ref[0, h]
            st = sin_ref[0, h]
            vb = vh * beta
            kbg = kb * gam
            u = _dot(t, vb, HI)
            w = _dot(t, kbg, HI)
            v_new = u - _dot(w, st, HI)
            q_dec = qs * gam
            k_dec = kn * kds

            o = o_ref[:, lo:lo + DH]
            zh = z_ref[:, lo:lo + DH]
            sz = _sig(zh)
            r = lax.rsqrt(jnp.mean(o * o, axis=-1, keepdims=True) + EPS)
            orr = o * r
            d_out = dob_ref[:, lo:lo + DH]
            dz = d_out * (orr * nw_ref[...]) * (sz * (1.0 + zh * (1.0 - sz)))
            don = d_out * (zh * sz)
            st_ref[4:5, 0:DH] += _colsum(don * orr)
            tt = don * nw_ref[...]
            d_o = r * (tt - orr * jnp.mean(tt * orr, axis=-1, keepdims=True))

            ds_out = dstate[h]
            dv_new = _dot_tn(f["qk"], d_o, HI) + _dot(k_dec, ds_out, HI)
            dqk = jnp.where(f["causal"], _dot_nt(d_o, v_new, HI), 0.0)
            dq_dec = _dot_nt(d_o, st, HI)
            dstate[h] = _dot_tn(q_dec, d_o, HI) + cd * ds_out - _dot_tn(w, dv_new, HI)
            dcd = jnp.sum(_rowsum(st * ds_out), axis=0, keepdims=True)
            dk_dec = _dot_nt(v_new, ds_out, HI)
            dw = -_dot_nt(dv_new, st, HI)
            dt_m = _dot_nt(dv_new, vb, HI) + _dot_nt(dw, kbg, HI)
            dvb = _dot_tn(t, dv_new, HI)
            dkbg = _dot_tn(t, dw, HI)
            da = jnp.where(f["strict"], -_dot_tn(t, _dot_nt(dt_m, t, HI), HI), 0.0)
            dad = da * decay
            dqkd = dqk * decay
            dkb = _dot(dad, kn, HI) + dkbg * gam
            dkn = _dot_tn(dad, kb, HI) + _dot_tn(dqkd, qs, HI) + dk_dec * kds + dkb * beta
            dqs = _dot(dqkd, kn, HI) + dq_dec * gam
            m = da * f["a"] + dqk * f["qk"]
            tk = _rowsum(dk_dec * k_dec)
            dgl = jnp.sum(tk, axis=0, keepdims=True) + dcd * cd
            dgc = (_rowsum(m) - _rowsum(jnp.transpose(m)) + _rowsum(dq_dec * q_dec) - tk + _rowsum(dkbg * kbg)
                   + jnp.where(row == CL - 1, dgl, 0.0))
            dbeta = _rowsum(dkb * kn) + _rowsum(dvb * vh)
            dgcum_all = dgcum_all + jnp.where(lane == NH + h, dgc, 0.0)
            dbeta_all = dbeta_all + jnp.where(lane == h, dbeta, 0.0)
            dvh = dvb * beta
            dqn = dqs * QSCALE
            dqh = f["rq"] * (dqn - qn * _rowsum(dqn * qn))
            dkh = f["rk"] * (dkn - kn * _rowsum(dkn * kn))
            dsilu = lambda c0: sc[:, c0:c0 + DH] * (1.0 + conv[:, c0:c0 + DH] * (1.0 - sc[:, c0:c0 + DH]))
            dcbuf[0:CL, lo:lo + DH] = dqh * dsilu(lo)
            dcbuf[0:CL, GW + lo:GW + lo + DH] = dkh * dsilu(GW + lo)
            dcbuf[0:CL, 2 * GW + lo:2 * GW + lo + DH] = dvh * dsilu(2 * GW + lo)
            dp_ref[:, 3 * GW + lo:3 * GW + lo + DH] = dz.astype(BF16)

        ii, jj = _tri_iota()
        upper = jnp.where(ii <= jj, 1.0, 0.0).astype(F32)
        dg_all = _dot(upper, dgcum_all, HI)
        dxg = dg_all * neg_a * _sig(xg)
        st_ref[5:6, 0:LANES] += _colsum(dg_all * g_all)
        st_ref[6:7, 0:LANES] += _colsum(dxg)
        dbl = dbeta_all * beta_all * (1.0 - beta_all)
        dba_ref[...] = jnp.where(lane < NH, dbl, jnp.where(lane < 2 * NH, dxg, 0.0)).astype(BF16)

        dconv = dcbuf[0:CL, :]
        dx = w_ref[0:1, :] * dcbuf[KS - 1:KS - 1 + CL, :]
        st_ref[0:1, :] += _colsum(dconv * xbuf[SH - KS + 1:SH - KS + 1 + CL, :])
        for k in range(1, KS):
            off = SH - (KS - 1) + k
            st_ref[k:k + 1, :] += _colsum(dconv * xbuf[off:off + CL, :])
            dx = dx + w_ref[k:k + 1, :] * dcbuf[KS - 1 - k:KS - 1 - k + CL, :]
        dcbuf[CL:CL + SH, :] = dcbuf[0:SH, :]
        dp_ref[:, 0:3 * GW] = dx.astype(BF16)

    rev = lambda w, j=0: pl.BlockSpec((CL, w), lambda n: (NCH - 1 - n, j))
    halo = lambda j: pl.BlockSpec((SH, GW), lambda n: (jnp.maximum((NCH - 1 - n) * (CL // SH) - 1, 0), j))
    blk4 = lambda a, b: pl.BlockSpec((1, NH, a, b), lambda n: (NCH - 1 - n, 0, 0, 0))
    return pl.pallas_call(
        body, name="gdn_bwd", grid=(NCH,),
        in_specs=[rev(GW), rev(GW), blk4(DH, DH), blk4(CL, CL), rev(GW, 2), rev(GW, 3), rev(GW, 4), rev(GW, 5),
                  halo(2), halo(3), halo(4), rev(LANES), _const((KS, 3 * GW)), _const((1, LANES)),
                  _const((1, LANES)), _const((1, DH))],
        out_specs=(rev(4 * GW), rev(LANES), _const((GDN_STATS, 3 * GW))),
        out_shape=(jax.ShapeDtypeStruct((S, 4 * GW), BF16), jax.ShapeDtypeStruct((S, LANES), BF16),
                   jax.ShapeDtypeStruct((GDN_STATS, 3 * GW), F32)),
        scratch_shapes=[pltpu.VMEM((SH + CL, 3 * GW), F32), pltpu.VMEM((CL + SH, 3 * GW), F32),
                        pltpu.VMEM((NH, DH, DH), F32)],
        compiler_params=_params(dimension_semantics=("arbitrary",)),
    )(d_out_b, o_pre, s_in, t_inv, p_main, p_main, p_main, p_main, p_main, p_main, p_main, p_ba,
      gdn_conv_w, alog_l, dt_l, gdn_nw)


def _bwd_in(dp_conf, dp_gdn, dp_ba, x, dx1, nw1, modnb, bada, w_main, w_ba):
    def body(dc_ref, dg_ref, db_ref, x_ref, dx1_ref, nw_ref, mod_ref, b_ref, wm_ref, wb_ref, gx_ref, st_ref):
        i = pl.program_id(0)

        @pl.when(i == 0)
        def _():
            st_ref[...] = jnp.zeros((8, D), F32)

        dh = (_dot_nt(dc_ref[...], wm_ref[:, 0:2 * CW]) + _dot_nt(dg_ref[...], wm_ref[:, 2 * CW:NMAIN])
              + _dot_nt(db_ref[...], wb_ref[...]))
        xv = x_ref[...]
        r = lax.rsqrt(jnp.mean(xv * xv, axis=-1, keepdims=True) + EPS)
        xr = xv * r
        st_ref[0:1, :] += _colsum(dh)
        st_ref[1:2, :] += _colsum(dh * (xr * nw_ref[...]))
        dxn = dh * (1.0 + _mod(mod_ref, b_ref, 1))
        st_ref[2:3, :] += _colsum(dxn * xr)
        dxr = dxn * nw_ref[...]
        gx_ref[...] = dx1_ref[...] + r * (dxr - xr * jnp.mean(dxr * xr, axis=-1, keepdims=True))

    tile = lambda w: pl.BlockSpec((TM, w), lambda i: (i, 0))
    return pl.pallas_call(
        body, name="bwd_in", grid=(NT,),
        in_specs=[tile(2 * CW), tile(4 * GW), tile(LANES), tile(D), tile(D), _const((1, D)), _const((1, 6 * D)),
                  _const((1, 6 * D)), _const((D, NMAIN)), _const((D, LANES))],
        out_specs=(tile(D), _const((8, D))),
        out_shape=(jax.ShapeDtypeStruct((S, D), F32), jax.ShapeDtypeStruct((8, D), F32)),
        compiler_params=_params(dimension_semantics=("arbitrary",)),
    )(dp_conf, dp_gdn, dp_ba, x, dx1, nw1, modnb, bada, w_main, w_ba)


def _adamw(w, g, m, v):
    m = ADAM_B1 * m + (1.0 - ADAM_B1) * g
    v = ADAM_B2 * v + (1.0 - ADAM_B2) * (g * g)
    m_hat = m / BC1
    v_hat = v / BC2
    delta = -ADAM_LR * (m_hat / (jnp.sqrt(v_hat) + ADAM_EPS) + ADAM_WD * w)
    return delta, m, v


ADAM_BLOCK_BYTES = 6 * 1024 * 1024


def _row_tile(rows, cols):
    padded = -(-cols // LANES) * LANES
    best = rows
    for tr in range(8, rows, 8):
        if rows % tr == 0 and N_DEV * tr * padded * 4 <= ADAM_BLOCK_BYTES:
            best = tr
    if N_DEV * rows * padded * 4 <= ADAM_BLOCK_BYTES:
        best = rows
    return best


def _reduce_adam(name, parts, w, m, v):
    rows, cols = w.shape
    tr = _row_tile(rows, cols)

    def body(p_ref, w_ref, m_ref, v_ref, g_ref, d_ref, nm_ref, nv_ref):
        g = p_ref[0]
        for j in range(1, N_DEV):
            g = g + p_ref[j]
        g_ref[...] = g
        d_ref[...], nm_ref[...], nv_ref[...] = _adamw(w_ref[...], g, m_ref[...], v_ref[...])

    blk = pl.BlockSpec((tr, cols), lambda i: (i, 0))
    sds = jax.ShapeDtypeStruct((rows, cols), F32)
    return pl.pallas_call(
        body, name=name, grid=(rows // tr,),
        in_specs=[pl.BlockSpec((N_DEV, tr, cols), lambda i: (0, i, 0)), blk, blk, blk],
        out_specs=(blk, blk, blk, blk), out_shape=(sds, sds, sds, sds),
        compiler_params=_params(dimension_semantics=("arbitrary",)),
    )(parts, w, m, v)


def _ada_adam(c_all, dmod_sh, w, m, v):
    rows, cols = w.shape
    tr = 256

    def body(c_ref, dm_ref, w_ref, m_ref, v_ref, g_ref, d_ref, nm_ref, nv_ref):
        cv = c_ref[...]
        g = _dot_tn(cv * _sig(cv), dm_ref[...], HI)
        g_ref[...] = g
        d_ref[...], nm_ref[...], nv_ref[...] = _adamw(w_ref[...], g, m_ref[...], v_ref[...])

    blk = pl.BlockSpec((tr, cols), lambda i: (i, 0))
    sds = jax.ShapeDtypeStruct((rows, cols), F32)
    return pl.pallas_call(
        body, name="ada_adam", grid=(rows // tr,),
        in_specs=[pl.BlockSpec((N_DEV, tr), lambda i: (0, i)), _const((N_DEV, cols)), blk, blk, blk],
        out_specs=(blk, blk, blk, blk), out_shape=(sds, sds, sds, sds),
        compiler_params=_params(dimension_semantics=("arbitrary",)),
    )(c_all, dmod_sh, w, m, v)


def _lanes(a, at=0):
    return jnp.pad(a, ((0, 0), (at, LANES - at - a.shape[1])))


WEIGHT_NAMES = ["w_ada", "b_ada", "norm_mix_w", "w_in", "conv_w", "conv_b", "conv_gn_w", "conv_gn_b", "gdn_conv_w",
                "gdn_a_log", "gdn_dt_bias", "gdn_norm_w", "w_out", "norm_ffn_w", "w_ffn_in", "w_ffn_out",
                "norm_final_w"]


def _slab(b_ada, norm_mix_w, norm_ffn_w, norm_final_w, conv_b, conv_gn_w, conv_gn_b, gdn_norm_w, a_log, dt_bias):
    return jnp.concatenate([
        b_ada.reshape(48, LANES), norm_mix_w.reshape(8, LANES), norm_ffn_w.reshape(8, LANES),
        norm_final_w.reshape(8, LANES), conv_b.reshape(4, LANES), conv_gn_w.reshape(4, LANES),
        conv_gn_b.reshape(4, LANES), gdn_norm_w.reshape(1, LANES), _lanes(a_log), _lanes(dt_bias),
        jnp.zeros((1, LANES), F32)], axis=0)


def _unslab(t):
    return dict(b_ada=t[0:48].reshape(1, 6 * D), norm_mix_w=t[48:56].reshape(1, D),
                norm_ffn_w=t[56:64].reshape(1, D), norm_final_w=t[64:72].reshape(D),
                conv_b=t[72:76].reshape(1, CW), conv_gn_w=t[76:80].reshape(1, CW),
                conv_gn_b=t[80:84].reshape(1, CW), gdn_norm_w=t[84:85], gdn_a_log=t[85:86, 0:NH],
                gdn_dt_bias=t[86:87, 0:NH])


def _local(xs, tgt, modnb, b_ada, norm_mix_w, w_in_full, conv_w_full, conv_b, conv_gn_w, conv_gn_b,
           gdn_conv_w_full, gdn_a_log, gdn_dt_bias, gdn_norm_w, w_out_full, norm_ffn_w, w_fi, w_fo, nfw):
    w_main = w_in_full[:, :NMAIN]
    w_ba = jnp.pad(w_in_full[:, NMAIN:], ((0, 0), (0, LANES - 2 * NH)))
    alog_l = _lanes(gdn_a_log, NH)
    dt_l = _lanes(gdn_dt_bias, NH)

    p_main, p_ba, hb1 = _fwd_in(xs, norm_mix_w, modnb, b_ada, w_main, w_ba)
    y_conv, out_a = _conf_fwd(p_main, conv_w_full, conv_b, conv_gn_w, conv_gn_b)
    out_b, o_pre, s_in, t_inv = _gdn_fwd(p_main, p_ba, gdn_conv_w_full, alog_l, dt_l, gdn_norm_w)
    x1, mix, oab = _fwd_out(out_a, out_b, xs, modnb, b_ada, w_out_full)
    dx1, hb2, act, dffn, df, st_ffn = _ffn_fwd_bwd(x1, tgt, modnb, b_ada, norm_ffn_w, nfw, w_fi, w_fo)
    gw_ffn_in = _grad_w_ffn_in(hb2, df)
    gw_ffn_out = _grad_w_ffn_out(act, dffn)
    dmix, d_out_a, d_out_b, st_out = _bwd_out(dx1, mix, modnb, b_ada, w_out_full)
    gw_out = _grad_w("grad_w_out", oab, dmix, 512)
    dp_conf, st_conf = _conf_bwd(d_out_a, y_conv, p_main, conv_w_full, conv_gn_w, conv_gn_b)
    dp_gdn, dp_ba, st_gdn = _gdn_bwd(d_out_b, o_pre, s_in, t_inv, p_main, p_ba, gdn_conv_w_full, alog_l, dt_l,
                                     gdn_norm_w)
    grad_x, st_in = _bwd_in(dp_conf, dp_gdn, dp_ba, xs, dx1, norm_mix_w, modnb, b_ada, w_main, w_ba)
    gw_in = jnp.concatenate(
        [_grad_w("grad_w_in_conf", hb1, dp_conf, 512), _grad_w("grad_w_in_gdn", hb1, dp_gdn, 512),
         _grad_w("grad_w_in_ba", hb1, dp_ba, LANES)[:, :2 * NH]], axis=1)
    dmod = jnp.concatenate([st_in[0:1], st_in[1:2], st_out[0:1], st_ffn[2:3], st_ffn[3:4], st_ffn[1:2]], axis=1)
    small = jnp.concatenate([
        dmod.reshape(48, LANES), st_in[2:3].reshape(8, LANES), st_ffn[4:5].reshape(8, LANES),
        st_ffn[0:1].reshape(8, LANES), st_conf[31:32].reshape(4, LANES), st_conf[32:33].reshape(4, LANES),
        st_conf[33:34].reshape(4, LANES), st_gdn[4:5, 0:LANES],
        _lanes(st_gdn[5:6, NH:2 * NH]), _lanes(st_gdn[6:7, NH:2 * NH]), st_ffn[5:6, 0:LANES]], axis=0)
    return dict(grad_x=grad_x, gw_in=gw_in, gw_out=gw_out, gw_ffn_in=gw_ffn_in, gw_ffn_out=gw_ffn_out,
                gw_conv=st_conf[0:KC], gw_gconv=st_gdn[0:KS], small=small)


def kernel(x, c, w_ada, b_ada, norm_mix_w, w_in, conv_w, conv_b, conv_gn_w, conv_gn_b, gdn_conv_w, gdn_a_log, gdn_dt_bias, gdn_norm_w, w_out, norm_ffn_w, w_ffn_in, w_ffn_out, norm_final_w, loss_target, m_w_ada, m_b_ada, m_norm_mix_w, m_w_in, m_conv_w, m_conv_b, m_conv_gn_w, m_conv_gn_b, m_gdn_conv_w, m_gdn_a_log, m_gdn_dt_bias, m_gdn_norm_w, m_w_out, m_norm_ffn_w, m_w_ffn_in, m_w_ffn_out, m_norm_final_w, v_w_ada, v_b_ada, v_norm_mix_w, v_w_in, v_conv_w, v_conv_b, v_conv_gn_w, v_conv_gn_b, v_gdn_conv_w, v_gdn_a_log, v_gdn_dt_bias, v_gdn_norm_w, v_w_out, v_norm_ffn_w, v_w_ffn_in, v_w_ffn_out, v_norm_final_w):
    me = 4 * lax.axis_index("x") + 2 * lax.axis_index("y") + lax.axis_index("c")
    xs = x.reshape(S, D)
    tgt = loss_target.reshape(S, D)

    g_c, g_cw, g_gcw, g_win, g_wout, g_wfi, g_wfo = _exchange(
        "gather_weights",
        [c, conv_w[0], gdn_conv_w[0], w_in[0].astype(BF16), w_out[0].astype(BF16), w_ffn_in[0].astype(BF16),
         w_ffn_out[0].astype(BF16)],
        [False] * 7)
    c_all = g_c.reshape(N_DEV, D)
    conv_w_full = jnp.transpose(g_cw, (1, 0, 2)).reshape(KC, CW)
    gdn_conv_w_full = jnp.transpose(g_gcw, (1, 0, 2)).reshape(KS, 3 * GW)
    w_in_full = jnp.transpose(g_win, (1, 0, 2)).reshape(D, NIN)

    (g_mod,) = _exchange("gather_mod", [_mod_shard(c_all, w_ada[0])], [False])
    modnb = lax.dynamic_index_in_dim(g_mod, me, axis=1, keepdims=False).reshape(1, 6 * D)

    loc = _local(xs, tgt, modnb, b_ada, norm_mix_w, w_in_full, conv_w_full, conv_b, conv_gn_w, conv_gn_b,
                 gdn_conv_w_full, gdn_a_log, gdn_dt_bias, gdn_norm_w, g_wout.reshape(D, D), norm_ffn_w, g_wfi,
                 g_wfo.reshape(4, FB, D), norm_final_w.reshape(1, D))

    gw_in = jnp.transpose(loc["gw_in"].reshape(D, N_DEV, NIN // N_DEV), (1, 0, 2))
    gw_conv = jnp.transpose(loc["gw_conv"].reshape(KC, N_DEV, CW // N_DEV), (1, 0, 2))
    gw_gconv = jnp.transpose(loc["gw_gconv"].reshape(KS, N_DEV, 3 * GW // N_DEV), (1, 0, 2))
    r_in, r_out, r_fi, r_fo, r_cw, r_gcw = _exchange(
        "scatter_grads",
        [gw_in, loc["gw_out"].reshape(N_DEV, D // N_DEV, D), loc["gw_ffn_in"],
         loc["gw_ffn_out"].reshape(N_DEV, DFF // N_DEV, D), gw_conv, gw_gconv], [True] * 6)

    (g_small,) = _exchange("gather_small", [loc["small"]], [False])
    sw = _slab(b_ada, norm_mix_w, norm_ffn_w, norm_final_w, conv_b, conv_gn_w, conv_gn_b, gdn_norm_w, gdn_a_log,
               gdn_dt_bias)
    sm = _slab(m_b_ada, m_norm_mix_w, m_norm_ffn_w, m_norm_final_w, m_conv_b, m_conv_gn_w, m_conv_gn_b,
               m_gdn_norm_w, m_gdn_a_log, m_gdn_dt_bias)
    sv = _slab(v_b_ada, v_norm_mix_w, v_norm_ffn_w, v_norm_final_w, v_conv_b, v_conv_gn_w, v_conv_gn_b,
               v_gdn_norm_w, v_gdn_a_log, v_gdn_dt_bias)
    small_out = _reduce_adam("adam_small", g_small, sw, sm, sv)
    loss = small_out[0][SMALL_ROWS - 1, 0]
    res = [_unslab(t) for t in small_out]

    dmod_rows = g_small[:, 0:48, :].reshape(N_DEV, 6 * D)
    dmod_sh = lax.dynamic_slice_in_dim(dmod_rows, me * (6 * D // N_DEV), 6 * D // N_DEV, axis=1)

    big = dict(
        w_ada=_ada_adam(c_all, dmod_sh, w_ada[0], m_w_ada[0], v_w_ada[0]),
        w_in=_reduce_adam("adam_w_in", r_in, w_in[0], m_w_in[0], v_w_in[0]),
        conv_w=_reduce_adam("adam_conv_w", r_cw, conv_w[0], m_conv_w[0], v_conv_w[0]),
        gdn_conv_w=_reduce_adam("adam_gdn_conv_w", r_gcw, gdn_conv_w[0], m_gdn_conv_w[0], v_gdn_conv_w[0]),
        w_out=_reduce_adam("adam_w_out", r_out, w_out[0], m_w_out[0], v_w_out[0]),
        w_ffn_in=_reduce_adam("adam_w_ffn_in", r_fi, w_ffn_in[0], m_w_ffn_in[0], v_w_ffn_in[0]),
        w_ffn_out=_reduce_adam("adam_w_ffn_out", r_fo, w_ffn_out[0], m_w_ffn_out[0], v_w_ffn_out[0]),
    )
    outs = [loss, loc["grad_x"].reshape(1, S, D)]
    for kind in range(4):
        for nm in WEIGHT_NAMES:
            outs.append(big[nm][kind][None] if nm in big else res[kind][nm])
    return tuple(outs)
```

```python
import functools

import jax
import jax.numpy as jnp
from jax import lax
from jax.experimental import pallas as pl
from jax.experimental.pallas import tpu as pltpu

F32 = jnp.float32
BF16 = jnp.bfloat16
HI = lax.Precision.HIGHEST
MESH = pl.DeviceIdType.MESH

N_DEV = 8
S = 2048
D = 1024
TM = 256
NT = S // TM
CW = 512
KC = 31
NG = 8
GSZ = CW // NG
HALO = 32
GW = 512
NH = 4
DH = 128
KS = 4
SH = 8
CL = 64
NCH = S // CL
NMAIN = 2 * CW + 4 * GW
NIN = NMAIN + 2 * NH
DFF = 2816
FB = DFF // 4
EPS = 1e-6
QSCALE = DH ** -0.5
LANES = 128
SMALL_ROWS = 88

ADAM_LR = 0.001
ADAM_B1 = 0.9
ADAM_B2 = 0.999
ADAM_EPS = 1e-08
ADAM_WD = 0.01
ADAM_STEP = 10
BC1 = 1.0 - ADAM_B1 ** ADAM_STEP
BC2 = 1.0 - ADAM_B2 ** ADAM_STEP

MIB = 1024 * 1024
VMEM_LIMIT_MIB = 32


def _params(limit_mib=VMEM_LIMIT_MIB, **kw):
    return pltpu.CompilerParams(vmem_limit_bytes=limit_mib * MIB, **kw)


def _sig(x):
    return jax.nn.sigmoid(x)


GP = BF16


def _operands(a, b, prec):
    if prec is BF16:
        return a.astype(BF16), b.astype(BF16), None
    return a, b, prec


def _dot(a, b, prec=None):
    a, b, prec = _operands(a, b, prec)
    return jnp.dot(a, b, preferred_element_type=F32, precision=prec)


def _dot_nt(a, b, prec=None):
    a, b, prec = _operands(a, b, prec)
    return lax.dot_general(a, b, (((1,), (1,)), ((), ())), preferred_element_type=F32, precision=prec)


def _dot_tn(a, b, prec=None):
    a, b, prec = _operands(a, b, prec)
    return lax.dot_general(a, b, (((0,), (0,)), ((), ())), preferred_element_type=F32, precision=prec)


def _rowsum(x):
    return jnp.sum(x, axis=-1, keepdims=True)


def _colsum(x):
    return jnp.sum(x, axis=0, keepdims=True)


def _mod(mod_ref, b_ref, k):
    return mod_ref[:, k * D:(k + 1) * D] + b_ref[:, k * D:(k + 1) * D]


def _const(shape):
    nd = len(shape)
    return pl.BlockSpec(shape, lambda *_: (0,) * nd)


def _const1(shape):
    nd = len(shape)
    return pl.BlockSpec(shape, lambda *_: (0,) * nd, pipeline_mode=pl.Buffered(1))


PEER_FLIPS = [(dx, dy, dc) for dx in (0, 1) for dy in (0, 1) for dc in (0, 1)][1:]


def _exchange(name, srcs, per_dest):
    n = len(srcs)
    out_shape = []
    for a, pd in zip(srcs, per_dest):
        blk = a.shape[1:] if pd else a.shape
        out_shape.append(jax.ShapeDtypeStruct((N_DEV,) + tuple(blk), a.dtype))

    def body(*refs):
        src = refs[:n]
        dst = refs[n:2 * n]
        send_sems, recv_sems, local_sems = refs[2 * n:]
        x, y, c = lax.axis_index("x"), lax.axis_index("y"), lax.axis_index("c")
        me = 4 * x + 2 * y + c

        def piece(i, j):
            return src[i].at[j] if per_dest[i] else src[i]

        copies = []
        for k, (dx, dy, dc) in enumerate(PEER_FLIPS):
            px = 1 - x if dx else x
            py = 1 - y if dy else y
            pc = 1 - c if dc else c
            pj = 4 * px + 2 * py + pc
            for i in range(n):
                cp = pltpu.make_async_remote_copy(
                    src_ref=piece(i, pj), dst_ref=dst[i].at[me],
                    send_sem=send_sems.at[k * n + i], recv_sem=recv_sems.at[k * n + i],
                    device_id=(px, py, pc), device_id_type=MESH)
                cp.start()
                arrive = pltpu.make_async_remote_copy(
                    src_ref=piece(i, pj), dst_ref=dst[i].at[pj],
                    send_sem=send_sems.at[k * n + i], recv_sem=recv_sems.at[k * n + i],
                    device_id=(px, py, pc), device_id_type=MESH)
                copies.append((cp, arrive))
        own = []
        for i in range(n):
            lc = pltpu.make_async_copy(piece(i, me), dst[i].at[me], local_sems.at[i])
            lc.start()
            own.append(lc)
        for cp, arrive in copies:
            arrive.wait_recv()
        for cp, arrive in copies:
            cp.wait_send()
        for lc in own:
            lc.wait()

    any_spec = pl.BlockSpec(memory_space=pl.ANY)
    return pl.pallas_call(
        body, name=name, out_shape=tuple(out_shape),
        in_specs=[any_spec] * n, out_specs=tuple([any_spec] * n),
        scratch_shapes=[pltpu.SemaphoreType.DMA((7 * n,)), pltpu.SemaphoreType.DMA((7 * n,)),
                        pltpu.SemaphoreType.DMA((n,))],
        compiler_params=pltpu.CompilerParams(has_side_effects=True),
    )(*srcs)


HBM_SPEC = pl.BlockSpec(memory_space=pltpu.HBM)
SEM_SPEC = pl.BlockSpec(memory_space=pltpu.SEMAPHORE)
DATAFLOW = pltpu.SideEffectType.DATAFLOW_SIDE_EFFECTING


def _peers():
    x, y, c = lax.axis_index("x"), lax.axis_index("y"), lax.axis_index("c")
    out = []
    for k, (dx, dy, dc) in enumerate(PEER_FLIPS):
        px = 1 - x if dx else x
        py = 1 - y if dy else y
        pc = 1 - c if dc else c
        out.append((k, (px, py, pc), 4 * px + 2 * py + pc))
    return 4 * x + 2 * y + c, out


def _exchange_start(name, srcs, lands, per_dest):
    n = len(srcs)

    def body(*refs):
        src, land = refs[:n], refs[n:2 * n]
        send_sems, recv_sems = refs[2 * n], refs[2 * n + 1]
        token = refs[-1]
        me, peers = _peers()
        for k, peer, pj in peers:
            for i in range(n):
                pltpu.make_async_remote_copy(
                    src_ref=src[i].at[pj] if per_dest[i] else src[i], dst_ref=land[i].at[me],
                    send_sem=send_sems.at[k * n + i], recv_sem=recv_sems.at[k * n + i],
                    device_id=peer, device_id_type=MESH).start()
        token[...] = jnp.zeros((8, LANES), F32)

    arrays = list(srcs) + list(lands)
    return pl.pallas_call(
        body, name=name,
        out_shape=(pltpu.SemaphoreType.DMA((7 * n,)), pltpu.SemaphoreType.DMA((7 * n,)),
                   *[pltpu.HBM(a.shape, a.dtype) for a in arrays], jax.ShapeDtypeStruct((8, LANES), F32)),
        in_specs=[HBM_SPEC] * (2 * n),
        out_specs=(SEM_SPEC, SEM_SPEC, *[HBM_SPEC] * (2 * n), pl.BlockSpec(memory_space=pltpu.VMEM)),
        input_output_aliases={i: 2 + i for i in range(2 * n)},
        compiler_params=pltpu.CompilerParams(has_side_effects=DATAFLOW),
    )(*[pltpu.with_memory_space_constraint(a, pltpu.HBM) for a in arrays])


def _exchange_wait(name, started, per_dest, after):
    n = (len(started) - 3) // 2
    send_sems, recv_sems = started[0], started[1]
    arrays = list(started[2:2 + 2 * n])

    def body(*refs):
        src, land = refs[:n], refs[n:2 * n]
        send, recv = refs[2 * n], refs[2 * n + 1]
        me, peers = _peers()
        for k, peer, pj in peers:
            for i in range(n):
                cp = pltpu.make_async_remote_copy(
                    src_ref=src[i].at[pj] if per_dest[i] else src[i], dst_ref=land[i].at[pj],
                    send_sem=send.at[k * n + i], recv_sem=recv.at[k * n + i],
                    device_id=peer, device_id_type=MESH)
                cp.wait_send()
                cp.wait_recv()

    outs = pl.pallas_call(
        body, name=name,
        out_shape=tuple(pltpu.HBM(a.shape, a.dtype) for a in arrays),
        in_specs=[HBM_SPEC] * (2 * n) + [SEM_SPEC, SEM_SPEC, pl.BlockSpec(memory_space=pl.ANY)],
        out_specs=tuple([HBM_SPEC] * (2 * n)),
        input_output_aliases={i: i for i in range(2 * n)},
        compiler_params=pltpu.CompilerParams(has_side_effects=DATAFLOW),
    )(*arrays, send_sems, recv_sems, after)
    return outs[:n], outs[n:]


def _seed_own(srcs):
    n = len(srcs)

    def body(*refs):
        me = 4 * lax.axis_index("x") + 2 * lax.axis_index("y") + lax.axis_index("c")
        sems = refs[-1]
        cps = [pltpu.make_async_copy(refs[i], refs[n + i].at[me], sems.at[i]) for i in range(n)]
        for cp in cps:
            cp.start()
        for cp in cps:
            cp.wait()

    any_spec = pl.BlockSpec(memory_space=pl.ANY)
    return pl.pallas_call(
        body, name="seed_own", out_shape=tuple(jax.ShapeDtypeStruct((N_DEV,) + a.shape, a.dtype) for a in srcs),
        in_specs=[any_spec] * n, out_specs=tuple([any_spec] * n),
        scratch_shapes=[pltpu.SemaphoreType.DMA((n,))],
    )(*srcs)


def _mod_shard(c_all, w_ada):
    def body(c_ref, w_ref, o_ref):
        cv = c_ref[...]
        ca = cv * _sig(cv)
        o_ref[...] = _dot(ca.astype(BF16), w_ref[...].astype(BF16))

    return pl.pallas_call(
        body, name="mod_shard", out_shape=jax.ShapeDtypeStruct((N_DEV, w_ada.shape[1]), F32),
        compiler_params=_params(),
    )(c_all, w_ada)


def _fwd_in(x, nw1, modnb, bada, w_main, w_ba):
    def body(x_ref, nw_ref, mod_ref, b_ref, wm_ref, wb_ref, pm_ref, pb_ref, hb_ref):
        xv = x_ref[...]
        r = lax.rsqrt(jnp.mean(xv * xv, axis=-1, keepdims=True) + EPS)
        h = (xv * r * nw_ref[...]) * (1.0 + _mod(mod_ref, b_ref, 1)) + _mod(mod_ref, b_ref, 0)
        hb = h.astype(BF16)
        hb_ref[...] = hb
        pm_ref[...] = _dot(hb, wm_ref[...])
        pb_ref[...] = _dot(hb, wb_ref[...])

    return pl.pallas_call(
        body, name="fwd_in", grid=(NT,),
        in_specs=[pl.BlockSpec((TM, D), lambda i: (i, 0)), _const((1, D)), _const((1, 6 * D)), _const((1, 6 * D)),
                  _const((D, NMAIN)), _const((D, LANES))],
        out_specs=(pl.BlockSpec((TM, NMAIN), lambda i: (i, 0)), pl.BlockSpec((TM, LANES), lambda i: (i, 0)),
                   pl.BlockSpec((TM, D), lambda i: (i, 0))),
        out_shape=(jax.ShapeDtypeStruct((S, NMAIN), F32), jax.ShapeDtypeStruct((S, LANES), F32),
                   jax.ShapeDtypeStruct((S, D), BF16)),
        compiler_params=_params(dimension_semantics=("arbitrary",)),
    )(x, nw1, modnb, bada, w_main, w_ba)


def _group_mean_matrix():
    ii = lax.broadcasted_iota(jnp.int32, (CW, CW), 0) // GSZ
    jj = lax.broadcasted_iota(jnp.int32, (CW, CW), 1) // GSZ
    return jnp.where(ii == jj, 1.0 / GSZ, 0.0).astype(F32)


def _conf_fwd(p_main, conv_w, conv_b, gn_w, gn_b):
    def body(a_ref, g_ref, w_ref, b_ref, gw_ref, gb_ref, y_ref, oa_ref, ubuf):
        i = pl.program_id(0)

        @pl.when(i == 0)
        def _():
            ubuf[0:HALO, :] = jnp.zeros((HALO, CW), F32)

        ubuf[HALO:HALO + TM, :] = a_ref[...] * _sig(g_ref[...])
        acc = jnp.zeros((TM, CW), F32) + b_ref[...]
        for k in range(KC):
            off = HALO - (KC - 1) + k
            acc = acc + w_ref[k:k + 1, :] * ubuf[off:off + TM, :]
        y_ref[...] = acc
        ubuf[0:HALO, :] = ubuf[TM:TM + HALO, :]
        pm = _group_mean_matrix()
        dlt = acc - _dot(acc, pm, HI)
        var = _dot(dlt * dlt, pm, HI)
        o = dlt * lax.rsqrt(var + EPS) * gw_ref[...] + gb_ref[...]
        oa_ref[...] = o * _sig(o)

    return pl.pallas_call(
        body, name="conf_fwd", grid=(NT,),
        in_specs=[pl.BlockSpec((TM, CW), lambda i: (i, 0)), pl.BlockSpec((TM, CW), lambda i: (i, 1)),
                  _const((KC, CW)), _const((1, CW)), _const((1, CW)), _const((1, CW))],
        out_specs=(pl.BlockSpec((TM, CW), lambda i: (i, 0)), pl.BlockSpec((TM, CW), lambda i: (i, 0))),
        out_shape=(jax.ShapeDtypeStruct((S, CW), F32), jax.ShapeDtypeStruct((S, CW), F32)),
        scratch_shapes=[pltpu.VMEM((HALO + TM, CW), F32)],
        compiler_params=_params(dimension_semantics=("arbitrary",)),
    )(p_main, p_main, conv_w, conv_b, gn_w, gn_b)


def _tri_iota():
    ii = lax.broadcasted_iota(jnp.int32, (CL, CL), 0)
    jj = lax.broadcasted_iota(jnp.int32, (CL, CL), 1)
    return ii, jj


def _gdn_gates(ba, alog_l, dt_l):
    beta_all = _sig(ba)
    xg = ba + dt_l
    sp = jnp.maximum(xg, 0.0) + jnp.log(1.0 + jnp.exp(-jnp.abs(xg)))
    neg_a = -jnp.exp(alog_l)
    return beta_all, neg_a * sp, xg, neg_a


def _gdn_cumsum(g_all):
    ii, jj = _tri_iota()
    low = jnp.where(ii >= jj, 1.0, 0.0).astype(F32)
    gcum = _dot(low, g_all, HI)
    return gcum, jnp.transpose(gcum)


def _unit_lower_inverse(a):
    ii, jj = _tri_iota()
    t = jnp.where(ii == jj, 1.0, 0.0).astype(F32) - a
    p = _dot(a, a, HI)
    for _ in range(4):
        t = t + _dot(t, p, HI)
        p = _dot(p, p, HI)
    return t + _dot(t, p, HI)


def _head_terms(qh, kh, beta, gcol, grow):
    ii, jj = _tri_iota()
    causal = ii >= jj
    strict = ii > jj
    rq = lax.rsqrt(_rowsum(qh * qh) + EPS)
    rk = lax.rsqrt(_rowsum(kh * kh) + EPS)
    qn = qh * rq
    kn = kh * rk
    qs = qn * QSCALE
    decay = jnp.where(causal, jnp.exp(jnp.where(causal, gcol - grow, 0.0)), 0.0)
    gam = jnp.exp(gcol)
    gl = gcol[CL - 1:CL, :]
    kds = jnp.exp(gl - gcol)
    cd = jnp.exp(gl)
    kb = kn * beta
    a = jnp.where(strict, _dot_nt(kb, kn, GP) * decay, 0.0)
    qk = jnp.where(causal, _dot_nt(qs, kn, GP) * decay, 0.0)
    return dict(rq=rq, rk=rk, qn=qn, kn=kn, qs=qs, decay=decay, gam=gam, kds=kds, cd=cd, kb=kb, a=a, qk=qk,
                causal=causal, strict=strict)


def _short_conv(w_ref, buf):
    acc = w_ref[0:1, :] * buf[SH - KS + 1:SH - KS + 1 + CL, :]
    for k in range(1, KS):
        off = SH - (KS - 1) + k
        acc = acc + w_ref[k:k + 1, :] * buf[off:off + CL, :]
    return acc


def _gdn_fwd(p_main, p_ba, gdn_conv_w, alog_l, dt_l, gdn_nw):
    def body(q_ref, k_ref, v_ref, z_ref, ba_ref, w_ref, al_ref, dt_ref, nw_ref,
             ob_ref, o_ref, sin_ref, t_ref, cbuf, state):
        n = pl.program_id(0)

        @pl.when(n == 0)
        def _():
            cbuf[0:SH, :] = jnp.zeros((SH, 3 * GW), F32)
            state[...] = jnp.zeros((NH, DH, DH), F32)

        cbuf[SH:SH + CL, 0:GW] = q_ref[...]
        cbuf[SH:SH + CL, GW:2 * GW] = k_ref[...]
        cbuf[SH:SH + CL, 2 * GW:3 * GW] = v_ref[...]
        conv = _short_conv(w_ref, cbuf)
        cbuf[0:SH, :] = cbuf[CL:CL + SH, :]
        qkv = conv * _sig(conv)
        beta_all, g_all, _, _ = _gdn_gates(ba_ref[...], al_ref[...], dt_ref[...])
        gcum, gcum_t = _gdn_cumsum(g_all)
        for h in range(NH):
            lo = h * DH
            qh = qkv[:, lo:lo + DH]
            kh = qkv[:, GW + lo:GW + lo + DH]
            vh = qkv[:, 2 * GW + lo:2 * GW + lo + DH]
            beta = beta_all[:, h:h + 1]
            f = _head_terms(qh, kh, beta, gcum[:, NH + h:NH + h + 1], gcum_t[NH + h:NH + h + 1, :])
            t = _unit_lower_inverse(f["a"])
            u = _dot(t, vh * beta, GP)
            w = _dot(t, f["kb"] * f["gam"], GP)
            st = state[h]
            sin_ref[0, h] = st
            t_ref[0, h] = t
            v_new = u - _dot(w, st, GP)
            o = _dot(f["qs"] * f["gam"], st, GP) + _dot(f["qk"], v_new, GP)
            state[h] = st * f["cd"] + _dot_tn(f["kn"] * f["kds"], v_new, GP)
            o_ref[:, lo:lo + DH] = o
            r = lax.rsqrt(jnp.mean(o * o, axis=-1, keepdims=True) + EPS)
            zh = z_ref[:, lo:lo + DH]
            ob_ref[:, lo:lo + DH] = o * r * nw_ref[...] * (zh * _sig(zh))

    col = lambda j: pl.BlockSpec((CL, GW), lambda n: (n, j))
    return pl.pallas_call(
        body, name="gdn_fwd", grid=(NCH,),
        in_specs=[col(2), col(3), col(4), col(5), pl.BlockSpec((CL, LANES), lambda n: (n, 0)),
                  _const((KS, 3 * GW)), _const((1, LANES)), _const((1, LANES)), _const((1, DH))],
        out_specs=(pl.BlockSpec((CL, GW), lambda n: (n, 0)), pl.BlockSpec((CL, GW), lambda n: (n, 0)),
                   pl.BlockSpec((1, NH, DH, DH), lambda n: (n, 0, 0, 0)),
                   pl.BlockSpec((1, NH, CL, CL), lambda n: (n, 0, 0, 0))),
        out_shape=(jax.ShapeDtypeStruct((S, GW), F32), jax.ShapeDtypeStruct((S, GW), F32),
                   jax.ShapeDtypeStruct((NCH, NH, DH, DH), F32), jax.ShapeDtypeStruct((NCH, NH, CL, CL), F32)),
        scratch_shapes=[pltpu.VMEM((SH + CL, 3 * GW), F32), pltpu.VMEM((NH, DH, DH), F32)],
        compiler_params=_params(dimension_semantics=("arbitrary",)),
    )(p_main, p_main, p_main, p_main, p_ba, gdn_conv_w, alog_l, dt_l, gdn_nw)


def _fwd_out(out_a, out_b, x, modnb, bada, w_out):
    def body(oa_ref, ob_ref, x_ref, mod_ref, b_ref, w_ref, x1_ref, mix_ref, oab_ref):
        oa = oa_ref[...].astype(BF16)
        ob = ob_ref[...].astype(BF16)
        oab_ref[:, 0:CW] = oa
        oab_ref[:, CW:D] = ob
        mix = _dot(oa, w_ref[0:CW, :]) + _dot(ob, w_ref[CW:D, :])
        mix_ref[...] = mix
        x1_ref[...] = x_ref[...] + _mod(mod_ref, b_ref, 2) * mix

    tile = lambda w: pl.BlockSpec((TM, w), lambda i: (i, 0))
    return pl.pallas_call(
        body, name="fwd_out", grid=(NT,),
        in_specs=[tile(CW), tile(GW), tile(D), _const((1, 6 * D)), _const((1, 6 * D)), _const((D, D))],
        out_specs=(tile(D), tile(D), tile(D)),
        out_shape=(jax.ShapeDtypeStruct((S, D), F32), jax.ShapeDtypeStruct((S, D), F32),
                   jax.ShapeDtypeStruct((S, D), BF16)),
        compiler_params=_params(dimension_semantics=("arbitrary",)),
    )(out_a, out_b, x, modnb, bada, w_out)


TF = 128
FFN_STATS = 8


def _ffn_fwd_bwd(x1, tgt, modnb, bada, nw2, nfw, w_fi, w_fo):
    def body(x1_ref, tgt_ref, mod_ref, b_ref, nw2_ref, nfw_ref, wi_ref, wo_ref,
             dx1_ref, hb_ref, act_ref, dffn_ref, df_ref, st_ref):
        i = pl.program_id(0)

        @pl.when(i == 0)
        def _():
            st_ref[...] = jnp.zeros((FFN_STATS, D), F32)

        sh2, sc2, gt2 = _mod(mod_ref, b_ref, 3), _mod(mod_ref, b_ref, 4), _mod(mod_ref, b_ref, 5)
        x1v = x1_ref[...]
        r2 = lax.rsqrt(jnp.mean(x1v * x1v, axis=-1, keepdims=True) + EPS)
        xr2 = x1v * r2
        xn2 = xr2 * nw2_ref[...]
        hb = (xn2 * (1.0 + sc2) + sh2).astype(BF16)
        hb_ref[...] = hb
        fg, fu, sg = [], [], []
        ffn = jnp.zeros((TF, D), F32)
        for j in range(4):
            fgj = _dot(hb, wi_ref[j])
            fuj = _dot(hb, wi_ref[j + 4])
            sj = _sig(fgj)
            aj = (fgj * sj * fuj).astype(BF16)
            act_ref[j] = aj
            ffn = ffn + _dot(aj, wo_ref[j])
            fg.append(fgj)
            fu.append(fuj)
            sg.append(sj)
        x2 = x1v + gt2 * ffn
        r3 = lax.rsqrt(jnp.mean(x2 * x2, axis=-1, keepdims=True) + EPS)
        xr3 = x2 * r3
        err = xr3 * nfw_ref[...] - tgt_ref[...]
        loss = 0.5 * jnp.sum(jnp.mean(err * err, axis=-1, keepdims=True), axis=0, keepdims=True)
        dy = err * (1.0 / D)
        st_ref[0:1, :] += _colsum(dy * xr3)
        dyr = dy * nfw_ref[...]
        dx2 = r3 * (dyr - xr3 * jnp.mean(dyr * xr3, axis=-1, keepdims=True))
        st_ref[1:2, :] += _colsum(dx2 * ffn)
        st_ref[5:6, :] += jnp.broadcast_to(loss, (1, D))
        dffn = (gt2 * dx2).astype(BF16)
        dffn_ref[...] = dffn
        dh = jnp.zeros((TF, D), F32)
        for j in range(4):
            dact = _dot_nt(dffn, wo_ref[j])
            dfg = (dact * fu[j] * (sg[j] * (1.0 + fg[j] * (1.0 - sg[j])))).astype(BF16)
            dfu = (dact * (fg[j] * sg[j])).astype(BF16)
            df_ref[j] = dfg
            df_ref[j + 4] = dfu
            dh = dh + _dot_nt(dfg, wi_ref[j]) + _dot_nt(dfu, wi_ref[j + 4])
        st_ref[2:3, :] += _colsum(dh)
        st_ref[3:4, :] += _colsum(dh * xn2)
        dxn = dh * (1.0 + sc2)
        st_ref[4:5, :] += _colsum(dxn * xr2)
        dxr = dxn * nw2_ref[...]
        dx1_ref[...] = dx2 + r2 * (dxr - xr2 * jnp.mean(dxr * xr2, axis=-1, keepdims=True))

    tile = lambda w: pl.BlockSpec((TF, w), lambda i: (i, 0))
    return pl.pallas_call(
        body, name="ffn_fwd_bwd", grid=(S // TF,),
        in_specs=[tile(D), tile(D), _const((1, 6 * D)), _const((1, 6 * D)), _const((1, D)), _const((1, D)),
                  _const1((N_DEV, D, FB)), _const1((4, FB, D))],
        out_specs=(tile(D), tile(D), pl.BlockSpec((4, TF, FB), lambda i: (0, i, 0)), tile(D),
                   pl.BlockSpec((N_DEV, TF, FB), lambda i: (0, i, 0)), _const((FFN_STATS, D))),
        out_shape=(jax.ShapeDtypeStruct((S, D), F32), jax.ShapeDtypeStruct((S, D), BF16),
                   jax.ShapeDtypeStruct((4, S, FB), BF16), jax.ShapeDtypeStruct((S, D), BF16),
                   jax.ShapeDtypeStruct((N_DEV, S, FB), BF16), jax.ShapeDtypeStruct((FFN_STATS, D), F32)),
        compiler_params=_params(44, dimension_semantics=("arbitrary",)),
    )(x1, tgt, modnb, bada, nw2, nfw, w_fi, w_fo)


def _grad_w(name, a, b, nb):
    m, n = a.shape[1], b.shape[1]

    def body(a_ref, b_ref, o_ref):
        o_ref[...] = _dot_tn(a_ref[...], b_ref[...]).astype(BF16)

    return pl.pallas_call(
        body, name=name, grid=(n // nb,),
        in_specs=[_const((S, m)), pl.BlockSpec((S, nb), lambda j: (0, j))],
        out_specs=pl.BlockSpec((m, nb), lambda j: (0, j)),
        out_shape=jax.ShapeDtypeStruct((m, n), BF16),
        compiler_params=_params(dimension_semantics=("arbitrary",)),
    )(a, b)


def _grad_w_ffn_in(hb2, df):
    def body(a_ref, b_ref, o_ref):
        o_ref[0] = _dot_tn(a_ref[...], b_ref[0]).astype(BF16)

    return pl.pallas_call(
        body, name="grad_w_ffn_in", grid=(N_DEV,),
        in_specs=[_const((S, D)), pl.BlockSpec((1, S, FB), lambda j: (j, 0, 0))],
        out_specs=pl.BlockSpec((1, D, FB), lambda j: (j, 0, 0)),
        out_shape=jax.ShapeDtypeStruct((N_DEV, D, FB), BF16),
        compiler_params=_params(dimension_semantics=("arbitrary",)),
    )(hb2, df)


def _grad_w_ffn_out(act, dffn):
    def body(a_ref, b_ref, o_ref):
        o_ref[0] = _dot_tn(a_ref[0], b_ref[...]).astype(BF16)

    return pl.pallas_call(
        body, name="grad_w_ffn_out", grid=(4,),
        in_specs=[pl.BlockSpec((1, S, FB), lambda j: (j, 0, 0)), _const((S, D))],
        out_specs=pl.BlockSpec((1, FB, D), lambda j: (j, 0, 0)),
        out_shape=jax.ShapeDtypeStruct((4, FB, D), BF16),
        compiler_params=_params(dimension_semantics=("arbitrary",)),
    )(act, dffn)


def _bwd_out(dx1, mix, modnb, bada, w_out):
    def body(dx_ref, mix_ref, mod_ref, b_ref, w_ref, dmix_ref, doa_ref, dob_ref, st_ref):
        i = pl.program_id(0)

        @pl.when(i == 0)
        def _():
            st_ref[...] = jnp.zeros((8, D), F32)

        dx = dx_ref[...]
        st_ref[0:1, :] += _colsum(dx * mix_ref[...])
        dmix = (_mod(mod_ref, b_ref, 2) * dx).astype(BF16)
        dmix_ref[...] = dmix
        doa_ref[...] = _dot_nt(dmix, w_ref[0:CW, :])
        dob_ref[...] = _dot_nt(dmix, w_ref[CW:D, :])

    tile = lambda w: pl.BlockSpec((TM, w), lambda i: (i, 0))
    return pl.pallas_call(
        body, name="bwd_out", grid=(NT,),
        in_specs=[tile(D), tile(D), _const((1, 6 * D)), _const((1, 6 * D)), _const((D, D))],
        out_specs=(tile(D), tile(CW), tile(GW), _const((8, D))),
        out_shape=(jax.ShapeDtypeStruct((S, D), BF16), jax.ShapeDtypeStruct((S, CW), F32),
                   jax.ShapeDtypeStruct((S, GW), F32), jax.ShapeDtypeStruct((8, D), F32)),
        compiler_params=_params(dimension_semantics=("arbitrary",)),
    )(dx1, mix, modnb, bada, w_out)


CONF_STATS = 40


def _conf_bwd(d_out_a, y, p_main, conv_w, gn_w, gn_b):
    def body(do_ref, y_ref, a_ref, g_ref, ah_ref, gh_ref, w_ref, gw_ref, gb_ref, dp_ref, st_ref, ubuf, dybuf):
        i = pl.program_id(0)

        @pl.when(i == 0)
        def _():
            st_ref[...] = jnp.zeros((CONF_STATS, CW), F32)
            dybuf[TM:TM + HALO, :] = jnp.zeros((HALO, CW), F32)

        pm = _group_mean_matrix()
        yv = y_ref[...]
        dlt = yv - _dot(yv, pm, HI)
        rstd = lax.rsqrt(_dot(dlt * dlt, pm, HI) + EPS)
        un = dlt * rstd
        o = un * gw_ref[...] + gb_ref[...]
        so = _sig(o)
        d_o = do_ref[...] * (so * (1.0 + o * (1.0 - so)))
        st_ref[33:34, :] += _colsum(d_o)
        st_ref[32:33, :] += _colsum(d_o * un)
        dun = d_o * gw_ref[...]
        dy = rstd * (dun - _dot(dun, pm, HI) - un * _dot(dun * un, pm, HI))
        st_ref[31:32, :] += _colsum(dy)
        dybuf[0:TM, :] = dy

        a = a_ref[...]
        sg = _sig(g_ref[...])
        first = i == NT - 1
        ubuf[0:HALO, :] = jnp.where(first, 0.0, ah_ref[...] * _sig(gh_ref[...]))
        ubuf[HALO:HALO + TM, :] = a * sg
        du = jnp.zeros((TM, CW), F32)
        for k in range(KC):
            off = HALO - (KC - 1) + k
            st_ref[k:k + 1, :] += _colsum(dy * ubuf[off:off + TM, :])
            du = du + w_ref[k:k + 1, :] * dybuf[KC - 1 - k:KC - 1 - k + TM, :]
        dybuf[TM:TM + HALO, :] = dybuf[0:HALO, :]
        dp_ref[:, 0:CW] = (du * sg).astype(BF16)
        dp_ref[:, CW:2 * CW] = (du * a * sg * (1.0 - sg)).astype(BF16)

    rev = lambda w, j=0: pl.BlockSpec((TM, w), lambda i: (NT - 1 - i, j))
    halo = lambda j: pl.BlockSpec((HALO, CW), lambda i: (jnp.maximum((NT - 1 - i) * (TM // HALO) - 1, 0), j))
    return pl.pallas_call(
        body, name="conf_bwd", grid=(NT,),
        in_specs=[rev(CW), rev(CW), rev(CW, 0), rev(CW, 1), halo(0), halo(1),
                  _const((KC, CW)), _const((1, CW)), _const((1, CW))],
        out_specs=(rev(2 * CW), _const((CONF_STATS, CW))),
        out_shape=(jax.ShapeDtypeStruct((S, 2 * CW), BF16), jax.ShapeDtypeStruct((CONF_STATS, CW), F32)),
        scratch_shapes=[pltpu.VMEM((HALO + TM, CW), F32), pltpu.VMEM((TM + HALO, CW), F32)],
        compiler_params=_params(dimension_semantics=("arbitrary",)),
    )(d_out_a, y, p_main, p_main, p_main, p_main, conv_w, gn_w, gn_b)


GDN_STATS = 8


def _gdn_bwd(d_out_b, o_pre, s_in, t_inv, p_main, p_ba, gdn_conv_w, alog_l, dt_l, gdn_nw):
    def body(dob_ref, o_ref, sin_ref, t_ref, q_ref, k_ref, v_ref, z_ref, qh_ref, kh_ref, vh_ref, ba_ref,
             w_ref, al_ref, dt_ref, nw_ref, dp_ref, dba_ref, st_ref, xbuf, dcbuf, dstate):
        n = pl.program_id(0)

        @pl.when(n == 0)
        def _():
            st_ref[...] = jnp.zeros((GDN_STATS, 3 * GW), F32)
            dcbuf[CL:CL + SH, :] = jnp.zeros((SH, 3 * GW), F32)
            dstate[...] = jnp.zeros((NH, DH, DH), F32)

        first = n == NCH - 1
        xbuf[0:SH, 0:GW] = jnp.where(first, 0.0, qh_ref[...])
        xbuf[0:SH, GW:2 * GW] = jnp.where(first, 0.0, kh_ref[...])
        xbuf[0:SH, 2 * GW:3 * GW] = jnp.where(first, 0.0, vh_ref[...])
        xbuf[SH:SH + CL, 0:GW] = q_ref[...]
        xbuf[SH:SH + CL, GW:2 * GW] = k_ref[...]
        xbuf[SH:SH + CL, 2 * GW:3 * GW] = v_ref[...]
        conv = _short_conv(w_ref, xbuf)
        sc = _sig(conv)
        qkv = conv * sc
        ba = ba_ref[...]
        beta_all, g_all, xg, neg_a = _gdn_gates(ba, al_ref[...], dt_ref[...])
        gcum, gcum_t = _gdn_cumsum(g_all)
        lane = lax.broadcasted_iota(jnp.int32, (CL, LANES), 1)
        row = lax.broadcasted_iota(jnp.int32, (CL, 1), 0)
        dgcum_all = jnp.zeros((CL, LANES), F32)
        dbeta_all = jnp.zeros((CL, LANES), F32)
        for h in range(NH):
            lo = h * DH
            qh = qkv[:, lo:lo + DH]
            kh = qkv[:, GW + lo:GW + lo + DH]
            vh = qkv[:, 2 * GW + lo:2 * GW + lo + DH]
            beta = beta_all[:, h:h + 1]
            f = _head_terms(qh, kh, beta, gcum[:, NH + h:NH + h + 1], gcum_t[NH + h:NH + h + 1, :])
            qn, kn, qs, kb, gam, kds, cd, decay = (f[s] for s in ("qn", "kn", "qs", "kb", "gam", "kds", "cd", "decay"))
            t = t_ref[0, h]
            st = sin_ref[0, h]
            vb = vh * beta
            kbg = kb * gam
            u = _dot(t, vb, GP)
            w = _dot(t, kbg, GP)
            v_new = u - _dot(w, st, GP)
            q_dec = qs * gam
            k_dec = kn * kds

            o = o_ref[:, lo:lo + DH]
            zh = z_ref[:, lo:lo + DH]
            sz = _sig(zh)
            r = lax.rsqrt(jnp.mean(o * o, axis=-1, keepdims=True) + EPS)
            orr = o * r
            d_out = dob_ref[:, lo:lo + DH]
            dz = d_out * (orr * nw_ref[...]) * (sz * (1.0 + zh * (1.0 - sz)))
            don = d_out * (zh * sz)
            st_ref[4:5, 0:DH] += _colsum(don * orr)
            tt = don * nw_ref[...]
            d_o = r * (tt - orr * jnp.mean(tt * orr, axis=-1, keepdims=True))

            ds_out = dstate[h]
            dv_new = _dot_tn(f["qk"], d_o, GP) + _dot(k_dec, ds_out, GP)
            dqk = jnp.where(f["causal"], _dot_nt(d_o, v_new, GP), 0.0)
            dq_dec = _dot_nt(d_o, st, GP)
            dstate[h] = _dot_tn(q_dec, d_o, GP) + cd * ds_out - _dot_tn(w, dv_new, GP)
            dcd = jnp.sum(_rowsum(st * ds_out), axis=0, keepdims=True)
            dk_dec = _dot_nt(v_new, ds_out, GP)
            dw = -_dot_nt(dv_new, st, GP)
            dt_m = _dot_nt(dv_new, vb, GP) + _dot_nt(dw, kbg, GP)
            dvb = _dot_tn(t, dv_new, GP)
            dkbg = _dot_tn(t, dw, GP)
            da = jnp.where(f["strict"], -_dot_tn(t, _dot_nt(dt_m, t, GP), GP), 0.0)
            dad = da * decay
            dqkd = dqk * decay
            dkb = _dot(dad, kn, GP) + dkbg * gam
            dkn = _dot_tn(dad, kb, GP) + _dot_tn(dqkd, qs, GP) + dk_dec * kds + dkb * beta
            dqs = _dot(dqkd, kn, GP) + dq_dec * gam
            m = da * f["a"] + dqk * f["qk"]
            tk = _rowsum(dk_dec * k_dec)
            dgl = jnp.sum(tk, axis=0, keepdims=True) + dcd * cd
            dgc = (_rowsum(m) - _rowsum(jnp.transpose(m)) + _rowsum(dq_dec * q_dec) - tk + _rowsum(dkbg * kbg)
                   + jnp.where(row == CL - 1, dgl, 0.0))
            dbeta = _rowsum(dkb * kn) + _rowsum(dvb * vh)
            dgcum_all = dgcum_all + jnp.where(lane == NH + h, dgc, 0.0)
            dbeta_all = dbeta_all + jnp.where(lane == h, dbeta, 0.0)
            dvh = dvb * beta
            dqn = dqs * QSCALE
            dqh = f["rq"] * (dqn - qn * _rowsum(dqn * qn))
            dkh = f["rk"] * (dkn - kn * _rowsum(dkn * kn))
            dsilu = lambda c0: sc[:, c0:c0 + DH] * (1.0 + conv[:, c0:c0 + DH] * (1.0 - sc[:, c0:c0 + DH]))
            dcbuf[0:CL, lo:lo + DH] = dqh * dsilu(lo)
            dcbuf[0:CL, GW + lo:GW + lo + DH] = dkh * dsilu(GW + lo)
            dcbuf[0:CL, 2 * GW + lo:2 * GW + lo + DH] = dvh * dsilu(2 * GW + lo)
            dp_ref[:, 3 * GW + lo:3 * GW + lo + DH] = dz.astype(BF16)

        ii, jj = _tri_iota()
        upper = jnp.where(ii <= jj, 1.0, 0.0).astype(F32)
        dg_all = _dot(upper, dgcum_all, HI)
        dxg = dg_all * neg_a * _sig(xg)
        st_ref[5:6, 0:LANES] += _colsum(dg_all * g_all)
        st_ref[6:7, 0:LANES] += _colsum(dxg)
        dbl = dbeta_all * beta_all * (1.0 - beta_all)
        dba_ref[...] = jnp.where(lane < NH, dbl, jnp.where(lane < 2 * NH, dxg, 0.0)).astype(BF16)

        dconv = dcbuf[0:CL, :]
        dx = w_ref[0:1, :] * dcbuf[KS - 1:KS - 1 + CL, :]
        st_ref[0:1, :] += _colsum(dconv * xbuf[SH - KS + 1:SH - KS + 1 + CL, :])
        for k in range(1, KS):
            off = SH - (KS - 1) + k
            st_ref[k:k + 1, :] += _colsum(dconv * xbuf[off:off + CL, :])
            dx = dx + w_ref[k:k + 1, :] * dcbuf[KS - 1 - k:KS - 1 - k + CL, :]
        dcbuf[CL:CL + SH, :] = dcbuf[0:SH, :]
        dp_ref[:, 0:3 * GW] = dx.astype(BF16)

    rev = lambda w, j=0: pl.BlockSpec((CL, w), lambda n: (NCH - 1 - n, j))
    halo = lambda j: pl.BlockSpec((SH, GW), lambda n: (jnp.maximum((NCH - 1 - n) * (CL // SH) - 1, 0), j))
    blk4 = lambda a, b: pl.BlockSpec((1, NH, a, b), lambda n: (NCH - 1 - n, 0, 0, 0))
    return pl.pallas_call(
        body, name="gdn_bwd", grid=(NCH,),
        in_specs=[rev(GW), rev(GW), blk4(DH, DH), blk4(CL, CL), rev(GW, 2), rev(GW, 3), rev(GW, 4), rev(GW, 5),
                  halo(2), halo(3), halo(4), rev(LANES), _const((KS, 3 * GW)), _const((1, LANES)),
                  _const((1, LANES)), _const((1, DH))],
        out_specs=(rev(4 * GW), rev(LANES), _const((GDN_STATS, 3 * GW))),
        out_shape=(jax.ShapeDtypeStruct((S, 4 * GW), BF16), jax.ShapeDtypeStruct((S, LANES), BF16),
                   jax.ShapeDtypeStruct((GDN_STATS, 3 * GW), F32)),
        scratch_shapes=[pltpu.VMEM((SH + CL, 3 * GW), F32), pltpu.VMEM((CL + SH, 3 * GW), F32),
                        pltpu.VMEM((NH, DH, DH), F32)],
        compiler_params=_params(dimension_semantics=("arbitrary",)),
    )(d_out_b, o_pre, s_in, t_inv, p_main, p_main, p_main, p_main, p_main, p_main, p_main, p_ba,
      gdn_conv_w, alog_l, dt_l, gdn_nw)


def _bwd_in(dp_conf, dp_gdn, dp_ba, x, dx1, nw1, modnb, bada, w_main, w_ba):
    def body(dc_ref, dg_ref, db_ref, x_ref, dx1_ref, nw_ref, mod_ref, b_ref, wm_ref, wb_ref, gx_ref, st_ref):
        i = pl.program_id(0)

        @pl.when(i == 0)
        def _():
            st_ref[...] = jnp.zeros((8, D), F32)

        dh = (_dot_nt(dc_ref[...], wm_ref[:, 0:2 * CW]) + _dot_nt(dg_ref[...], wm_ref[:, 2 * CW:NMAIN])
              + _dot_nt(db_ref[...], wb_ref[...]))
        xv = x_ref[...]
        r = lax.rsqrt(jnp.mean(xv * xv, axis=-1, keepdims=True) + EPS)
        xr = xv * r
        st_ref[0:1, :] += _colsum(dh)
        st_ref[1:2, :] += _colsum(dh * (xr * nw_ref[...]))
        dxn = dh * (1.0 + _mod(mod_ref, b_ref, 1))
        st_ref[2:3, :] += _colsum(dxn * xr)
        dxr = dxn * nw_ref[...]
        gx_ref[...] = dx1_ref[...] + r * (dxr - xr * jnp.mean(dxr * xr, axis=-1, keepdims=True))

    tile = lambda w: pl.BlockSpec((TM, w), lambda i: (i, 0))
    return pl.pallas_call(
        body, name="bwd_in", grid=(NT,),
        in_specs=[tile(2 * CW), tile(4 * GW), tile(LANES), tile(D), tile(D), _const((1, D)), _const((1, 6 * D)),
                  _const((1, 6 * D)), _const((D, NMAIN)), _const((D, LANES))],
        out_specs=(tile(D), _const((8, D))),
        out_shape=(jax.ShapeDtypeStruct((S, D), F32), jax.ShapeDtypeStruct((8, D), F32)),
        compiler_params=_params(dimension_semantics=("arbitrary",)),
    )(dp_conf, dp_gdn, dp_ba, x, dx1, nw1, modnb, bada, w_main, w_ba)


def _adamw(w, g, m, v):
    m = ADAM_B1 * m + (1.0 - ADAM_B1) * g
    v = ADAM_B2 * v + (1.0 - ADAM_B2) * (g * g)
    m_hat = m / BC1
    v_hat = v / BC2
    delta = -ADAM_LR * (m_hat / (jnp.sqrt(v_hat) + ADAM_EPS) + ADAM_WD * w)
    return delta, m, v


ADAM_BLOCK_BYTES = 6 * 1024 * 1024


def _row_tile(rows, cols):
    padded = -(-cols // LANES) * LANES
    best = rows
    for tr in range(8, rows, 8):
        if rows % tr == 0 and N_DEV * tr * padded * 4 <= ADAM_BLOCK_BYTES:
            best = tr
    if N_DEV * rows * padded * 4 <= ADAM_BLOCK_BYTES:
        best = rows
    return best


def _reduce_adam(name, parts, w, m, v, own=None):
    rows, cols = w.shape
    tr = _row_tile(rows, cols)

    def body(*refs):
        p_ref, w_ref, m_ref, v_ref = refs[:4]
        g_ref, d_ref, nm_ref, nv_ref = refs[-4:]
        if own is None:
            part = lambda j: p_ref[j].astype(F32)
        else:
            me = 4 * lax.axis_index("x") + 2 * lax.axis_index("y") + lax.axis_index("c")
            part = lambda j: jnp.where(me == j, refs[4][...], p_ref[j]).astype(F32)
        g = part(0)
        for j in range(1, N_DEV):
            g = g + part(j)
        g_ref[...] = g
        d_ref[...], nm_ref[...], nv_ref[...] = _adamw(w_ref[...], g, m_ref[...], v_ref[...])

    blk = pl.BlockSpec((tr, cols), lambda i: (i, 0))
    sds = jax.ShapeDtypeStruct((rows, cols), F32)
    extra = [] if own is None else [own]
    return pl.pallas_call(
        body, name=name, grid=(rows // tr,),
        in_specs=[pl.BlockSpec((N_DEV, tr, cols), lambda i: (0, i, 0)), blk, blk, blk] + [blk] * len(extra),
        out_specs=(blk, blk, blk, blk), out_shape=(sds, sds, sds, sds),
        compiler_params=_params(dimension_semantics=("arbitrary",)),
    )(parts, w, m, v, *extra)


def _ada_adam(c_all, dmod_sh, w, m, v):
    rows, cols = w.shape
    tr = 256

    def body(c_ref, dm_ref, w_ref, m_ref, v_ref, g_ref, d_ref, nm_ref, nv_ref):
        cv = c_ref[...]
        g = _dot_tn(cv * _sig(cv), dm_ref[...], HI)
        g_ref[...] = g
        d_ref[...], nm_ref[...], nv_ref[...] = _adamw(w_ref[...], g, m_ref[...], v_ref[...])

    blk = pl.BlockSpec((tr, cols), lambda i: (i, 0))
    sds = jax.ShapeDtypeStruct((rows, cols), F32)
    return pl.pallas_call(
        body, name="ada_adam", grid=(rows // tr,),
        in_specs=[pl.BlockSpec((N_DEV, tr), lambda i: (0, i)), _const((N_DEV, cols)), blk, blk, blk],
        out_specs=(blk, blk, blk, blk), out_shape=(sds, sds, sds, sds),
        compiler_params=_params(dimension_semantics=("arbitrary",)),
    )(c_all, dmod_sh, w, m, v)


def _lanes(a, at=0):
    return jnp.pad(a, ((0, 0), (at, LANES - at - a.shape[1])))


WEIGHT_NAMES = ["w_ada", "b_ada", "norm_mix_w", "w_in", "conv_w", "conv_b", "conv_gn_w", "conv_gn_b", "gdn_conv_w",
                "gdn_a_log", "gdn_dt_bias", "gdn_norm_w", "w_out", "norm_ffn_w", "w_ffn_in", "w_ffn_out",
                "norm_final_w"]


def _slab(b_ada, norm_mix_w, norm_ffn_w, norm_final_w, conv_b, conv_gn_w, conv_gn_b, gdn_norm_w, a_log, dt_bias):
    return jnp.concatenate([
        b_ada.reshape(48, LANES), norm_mix_w.reshape(8, LANES), norm_ffn_w.reshape(8, LANES),
        norm_final_w.reshape(8, LANES), conv_b.reshape(4, LANES), conv_gn_w.reshape(4, LANES),
        conv_gn_b.reshape(4, LANES), gdn_norm_w.reshape(1, LANES), _lanes(a_log), _lanes(dt_bias),
        jnp.zeros((1, LANES), F32)], axis=0)


def _unslab(t):
    return dict(b_ada=t[0:48].reshape(1, 6 * D), norm_mix_w=t[48:56].reshape(1, D),
                norm_ffn_w=t[56:64].reshape(1, D), norm_final_w=t[64:72].reshape(D),
                conv_b=t[72:76].reshape(1, CW), conv_gn_w=t[76:80].reshape(1, CW),
                conv_gn_b=t[80:84].reshape(1, CW), gdn_norm_w=t[84:85], gdn_a_log=t[85:86, 0:NH],
                gdn_dt_bias=t[86:87, 0:NH])


def _mix_forward(w, xs, modnb):
    w_main = w["w_in"][:, :NMAIN]
    w_ba = jnp.pad(w["w_in"][:, NMAIN:], ((0, 0), (0, LANES - 2 * NH)))
    alog_l = _lanes(w["gdn_a_log"], NH)
    dt_l = _lanes(w["gdn_dt_bias"], NH)
    p_main, p_ba, hb1 = _fwd_in(xs, w["norm_mix_w"], modnb, w["b_ada"], w_main, w_ba)
    y_conv, out_a = _conf_fwd(p_main, w["conv_w"], w["conv_b"], w["conv_gn_w"], w["conv_gn_b"])
    out_b, o_pre, s_in, t_inv = _gdn_fwd(p_main, p_ba, w["gdn_conv_w"], alog_l, dt_l, w["gdn_norm_w"])
    return dict(w_main=w_main, w_ba=w_ba, alog_l=alog_l, dt_l=dt_l, p_main=p_main, p_ba=p_ba, hb1=hb1,
                y_conv=y_conv, out_a=out_a, out_b=out_b, o_pre=o_pre, s_in=s_in, t_inv=t_inv)


def _ffn_stage(w, f, xs, tgt, modnb):
    x1, mix, oab = _fwd_out(f["out_a"], f["out_b"], xs, modnb, w["b_ada"], w["w_out"])
    dx1, hb2, act, dffn, df, st_ffn = _ffn_fwd_bwd(x1, tgt, modnb, w["b_ada"], w["norm_ffn_w"], w["norm_final_w"],
                                                   w["w_ffn_in"], w["w_ffn_out"])
    gw_ffn_in = _grad_w_ffn_in(hb2, df)
    gw_ffn_out = _grad_w_ffn_out(act, dffn)
    return dict(mix=mix, oab=oab, dx1=dx1, st_ffn=st_ffn, gw_ffn_in=gw_ffn_in, gw_ffn_out=gw_ffn_out)


def _mix_backward(w, f, g, xs, modnb):
    dmix, d_out_a, d_out_b, st_out = _bwd_out(g["dx1"], g["mix"], modnb, w["b_ada"], w["w_out"])
    gw_out = _grad_w("grad_w_out", g["oab"], dmix, 512)
    dp_conf, st_conf = _conf_bwd(d_out_a, f["y_conv"], f["p_main"], w["conv_w"], w["conv_gn_w"], w["conv_gn_b"])
    dp_gdn, dp_ba, st_gdn = _gdn_bwd(d_out_b, f["o_pre"], f["s_in"], f["t_inv"], f["p_main"], f["p_ba"],
                                     w["gdn_conv_w"], f["alog_l"], f["dt_l"], w["gdn_norm_w"])
    grad_x, st_in = _bwd_in(dp_conf, dp_gdn, dp_ba, xs, g["dx1"], w["norm_mix_w"], modnb, w["b_ada"], f["w_main"],
                            f["w_ba"])
    hb1 = f["hb1"]
    gw_in = jnp.concatenate(
        [_grad_w("grad_w_in_conf", hb1, dp_conf, 512), _grad_w("grad_w_in_gdn", hb1, dp_gdn, 512),
         _grad_w("grad_w_in_ba", hb1, dp_ba, LANES)[:, :2 * NH]], axis=1)
    st_ffn = g["st_ffn"]
    dmod = jnp.concatenate([st_in[0:1], st_in[1:2], st_out[0:1], st_ffn[2:3], st_ffn[3:4], st_ffn[1:2]], axis=1)
    small = jnp.concatenate([
        dmod.reshape(48, LANES), st_in[2:3].reshape(8, LANES), st_ffn[4:5].reshape(8, LANES),
        st_ffn[0:1].reshape(8, LANES), st_conf[31:32].reshape(4, LANES), st_conf[32:33].reshape(4, LANES),
        st_conf[33:34].reshape(4, LANES), st_gdn[4:5, 0:LANES],
        _lanes(st_gdn[5:6, NH:2 * NH]), _lanes(st_gdn[6:7, NH:2 * NH]), st_ffn[5:6, 0:LANES]], axis=0)
    return dict(grad_x=grad_x, gw_in=gw_in, gw_out=gw_out, gw_conv=st_conf[0:KC], gw_gconv=st_gdn[0:KS],
                small=small)


def _local(w, xs, tgt, modnb):
    f = _mix_forward(w, xs, modnb)
    g = _ffn_stage(w, f, xs, tgt, modnb)
    b = _mix_backward(w, f, g, xs, modnb)
    return dict(b, gw_ffn_in=g["gw_ffn_in"], gw_ffn_out=g["gw_ffn_out"])


def kernel(x, c, w_ada, b_ada, norm_mix_w, w_in, conv_w, conv_b, conv_gn_w, conv_gn_b, gdn_conv_w, gdn_a_log, gdn_dt_bias, gdn_norm_w, w_out, norm_ffn_w, w_ffn_in, w_ffn_out, norm_final_w, loss_target, m_w_ada, m_b_ada, m_norm_mix_w, m_w_in, m_conv_w, m_conv_b, m_conv_gn_w, m_conv_gn_b, m_gdn_conv_w, m_gdn_a_log, m_gdn_dt_bias, m_gdn_norm_w, m_w_out, m_norm_ffn_w, m_w_ffn_in, m_w_ffn_out, m_norm_final_w, v_w_ada, v_b_ada, v_norm_mix_w, v_w_in, v_conv_w, v_conv_b, v_conv_gn_w, v_conv_gn_b, v_gdn_conv_w, v_gdn_a_log, v_gdn_dt_bias, v_gdn_norm_w, v_w_out, v_norm_ffn_w, v_w_ffn_in, v_w_ffn_out, v_norm_final_w):
    me = 4 * lax.axis_index("x") + 2 * lax.axis_index("y") + lax.axis_index("c")
    xs = x.reshape(S, D)
    tgt = loss_target.reshape(S, D)

    late = [w_out[0].astype(BF16), w_ffn_in[0].astype(BF16), w_ffn_out[0].astype(BF16)]
    late_started = _exchange_start("gather_late_start", late, _seed_own(late), [False] * 3)
    token = late_started[-1]

    g_c, g_cw, g_gcw, g_win = _exchange(
        "gather_weights", [c + token[0:1, 0:1], conv_w[0], gdn_conv_w[0], w_in[0].astype(BF16)], [False] * 4)
    c_all = g_c.reshape(N_DEV, D)
    w = dict(b_ada=b_ada, norm_mix_w=norm_mix_w, conv_b=conv_b, conv_gn_w=conv_gn_w, conv_gn_b=conv_gn_b,
             gdn_a_log=gdn_a_log, gdn_dt_bias=gdn_dt_bias, gdn_norm_w=gdn_norm_w, norm_ffn_w=norm_ffn_w,
             norm_final_w=norm_final_w.reshape(1, D),
             conv_w=jnp.transpose(g_cw, (1, 0, 2)).reshape(KC, CW),
             gdn_conv_w=jnp.transpose(g_gcw, (1, 0, 2)).reshape(KS, 3 * GW),
             w_in=jnp.transpose(g_win, (1, 0, 2)).reshape(D, NIN))

    (g_mod,) = _exchange("gather_mod", [_mod_shard(c_all, w_ada[0])], [False])
    modnb = lax.dynamic_index_in_dim(g_mod, me, axis=1, keepdims=False).reshape(1, 6 * D)

    f = _mix_forward(w, xs, modnb)
    _, (g_wout, g_wfi, g_wfo) = _exchange_wait("gather_late_wait", late_started, [False] * 3, f["out_b"])
    w.update(w_out=g_wout.reshape(D, D), w_ffn_in=g_wfi, w_ffn_out=g_wfo.reshape(4, FB, D))
    g = _ffn_stage(w, f, xs, tgt, modnb)

    ffn_grads = [g["gw_ffn_in"], g["gw_ffn_out"].reshape(N_DEV, DFF // N_DEV, D)]
    ffn_started = _exchange_start("scatter_ffn_start", ffn_grads,
                                  [lax.empty(a.shape, a.dtype) for a in ffn_grads], [True] * 2)
    loc = _mix_backward(w, f, g, xs, modnb + ffn_started[-1][0:1, 0:1])

    gw_in = jnp.transpose(loc["gw_in"].reshape(D, N_DEV, NIN // N_DEV), (1, 0, 2))
    gw_conv = jnp.transpose(loc["gw_conv"].reshape(KC, N_DEV, CW // N_DEV), (1, 0, 2))
    gw_gconv = jnp.transpose(loc["gw_gconv"].reshape(KS, N_DEV, 3 * GW // N_DEV), (1, 0, 2))
    r_in, r_out, r_cw, r_gcw = _exchange(
        "scatter_grads", [gw_in, loc["gw_out"].reshape(N_DEV, D // N_DEV, D), gw_conv, gw_gconv], [True] * 4)
    (sent_fi, sent_fo), (r_fi, r_fo) = _exchange_wait("scatter_ffn_wait", ffn_started, [True] * 2, r_cw)
    own_fi = lax.dynamic_index_in_dim(sent_fi, me, axis=0, keepdims=False)
    own_fo = lax.dynamic_index_in_dim(sent_fo, me, axis=0, keepdims=False)

    (g_small,) = _exchange("gather_small", [loc["small"]], [False])
    sw = _slab(b_ada, norm_mix_w, norm_ffn_w, norm_final_w, conv_b, conv_gn_w, conv_gn_b, gdn_norm_w, gdn_a_log,
               gdn_dt_bias)
    sm = _slab(m_b_ada, m_norm_mix_w, m_norm_ffn_w, m_norm_final_w, m_conv_b, m_conv_gn_w, m_conv_gn_b,
               m_gdn_norm_w, m_gdn_a_log, m_gdn_dt_bias)
    sv = _slab(v_b_ada, v_norm_mix_w, v_norm_ffn_w, v_norm_final_w, v_conv_b, v_conv_gn_w, v_conv_gn_b,
               v_gdn_norm_w, v_gdn_a_log, v_gdn_dt_bias)
    small_out = _reduce_adam("adam_small", g_small, sw, sm, sv)
    loss = small_out[0][SMALL_ROWS - 1, 0]
    res = [_unslab(t) for t in small_out]

    dmod_rows = g_small[:, 0:48, :].reshape(N_DEV, 6 * D)
    dmod_sh = lax.dynamic_slice_in_dim(dmod_rows, me * (6 * D // N_DEV), 6 * D // N_DEV, axis=1)

    big = dict(
        w_ada=_ada_adam(c_all, dmod_sh, w_ada[0], m_w_ada[0], v_w_ada[0]),
        w_in=_reduce_adam("adam_w_in", r_in, w_in[0], m_w_in[0], v_w_in[0]),
        conv_w=_reduce_adam("adam_conv_w", r_cw, conv_w[0], m_conv_w[0], v_conv_w[0]),
        gdn_conv_w=_reduce_adam("adam_gdn_conv_w", r_gcw, gdn_conv_w[0], m_gdn_conv_w[0], v_gdn_conv_w[0]),
        w_out=_reduce_adam("adam_w_out", r_out, w_out[0], m_w_out[0], v_w_out[0]),
        w_ffn_in=_reduce_adam("adam_w_ffn_in", r_fi, w_ffn_in[0], m_w_ffn_in[0], v_w_ffn_in[0], own_fi),
        w_ffn_out=_reduce_adam("adam_w_ffn_out", r_fo, w_ffn_out[0], m_w_ffn_out[0], v_w_ffn_out[0], own_fo),
    )
    outs = [loss, loc["grad_x"].reshape(1, S, D)]
    for kind in range(4):
        for nm in WEIGHT_NAMES:
            outs.append(big[nm][kind][None] if nm in big else res[kind][nm])
    return tuple(outs)
```

```python
import functools

import jax
import jax.numpy as jnp
from jax import lax
from jax.experimental import pallas as pl
from jax.experimental.pallas import tpu as pltpu

F32 = jnp.float32
BF16 = jnp.bfloat16
HI = lax.Precision.HIGHEST
MESH = pl.DeviceIdType.MESH

N_DEV = 8
S = 2048
D = 1024
TM = 256
NT = S // TM
CW = 512
KC = 31
NG = 8
GSZ = CW // NG
HALO = 32
GW = 512
NH = 4
DH = 128
KS = 4
SH = 8
CL = 64
NCH = S // CL
NMAIN = 2 * CW + 4 * GW
NIN = NMAIN + 2 * NH
DFF = 2816
FB = DFF // 4
EPS = 1e-6
QSCALE = DH ** -0.5
LANES = 128
SMALL_ROWS = 88

ADAM_LR = 0.001
ADAM_B1 = 0.9
ADAM_B2 = 0.999
ADAM_EPS = 1e-08
ADAM_WD = 0.01
ADAM_STEP = 10
BC1 = 1.0 - ADAM_B1 ** ADAM_STEP
BC2 = 1.0 - ADAM_B2 ** ADAM_STEP

MIB = 1024 * 1024
VMEM_LIMIT_MIB = 32


def _params(limit_mib=VMEM_LIMIT_MIB, **kw):
    return pltpu.CompilerParams(vmem_limit_bytes=limit_mib * MIB, **kw)


def _sig(x):
    return jax.nn.sigmoid(x)


GP = BF16


def _operands(a, b, prec):
    if prec is BF16:
        return a.astype(BF16), b.astype(BF16), None
    return a, b, prec


def _dot(a, b, prec=None):
    a, b, prec = _operands(a, b, prec)
    return jnp.dot(a, b, preferred_element_type=F32, precision=prec)


def _dot_nt(a, b, prec=None):
    a, b, prec = _operands(a, b, prec)
    return lax.dot_general(a, b, (((1,), (1,)), ((), ())), preferred_element_type=F32, precision=prec)


def _dot_tn(a, b, prec=None):
    a, b, prec = _operands(a, b, prec)
    return lax.dot_general(a, b, (((0,), (0,)), ((), ())), preferred_element_type=F32, precision=prec)


def _rowsum(x):
    return jnp.sum(x, axis=-1, keepdims=True)


def _colsum(x):
    return jnp.sum(x, axis=0, keepdims=True)


def _mod(mod_ref, b_ref, k):
    return mod_ref[:, k * D:(k + 1) * D] + b_ref[:, k * D:(k + 1) * D]


def _const(shape):
    nd = len(shape)
    return pl.BlockSpec(shape, lambda *_: (0,) * nd)


def _const1(shape):
    nd = len(shape)
    return pl.BlockSpec(shape, lambda *_: (0,) * nd, pipeline_mode=pl.Buffered(1))


PEER_FLIPS = [(dx, dy, dc) for dx in (0, 1) for dy in (0, 1) for dc in (0, 1)][1:]


def _exchange(name, srcs, per_dest, seed_only=()):
    n = len(srcs)
    out_shape = []
    for a, pd in zip(srcs, per_dest):
        blk = a.shape[1:] if pd else a.shape
        out_shape.append(jax.ShapeDtypeStruct((N_DEV,) + tuple(blk), a.dtype))

    def body(*refs):
        src = refs[:n]
        dst = refs[n:2 * n]
        send_sems, recv_sems, local_sems = refs[2 * n:]
        x, y, c = lax.axis_index("x"), lax.axis_index("y"), lax.axis_index("c")
        me = 4 * x + 2 * y + c

        def piece(i, j):
            return src[i].at[j] if per_dest[i] else src[i]

        copies = []
        for k, (dx, dy, dc) in enumerate(PEER_FLIPS):
            px = 1 - x if dx else x
            py = 1 - y if dy else y
            pc = 1 - c if dc else c
            pj = 4 * px + 2 * py + pc
            for i in range(n):
                if i in seed_only:
                    continue
                cp = pltpu.make_async_remote_copy(
                    src_ref=piece(i, pj), dst_ref=dst[i].at[me],
                    send_sem=send_sems.at[k * n + i], recv_sem=recv_sems.at[k * n + i],
                    device_id=(px, py, pc), device_id_type=MESH)
                cp.start()
                arrive = pltpu.make_async_remote_copy(
                    src_ref=piece(i, pj), dst_ref=dst[i].at[pj],
                    send_sem=send_sems.at[k * n + i], recv_sem=recv_sems.at[k * n + i],
                    device_id=(px, py, pc), device_id_type=MESH)
                copies.append((cp, arrive))
        own = []
        for i in range(n):
            lc = pltpu.make_async_copy(piece(i, me), dst[i].at[me], local_sems.at[i])
            lc.start()
            own.append(lc)
        for cp, arrive in copies:
            arrive.wait_recv()
        for cp, arrive in copies:
            cp.wait_send()
        for lc in own:
            lc.wait()

    any_spec = pl.BlockSpec(memory_space=pl.ANY)
    return pl.pallas_call(
        body, name=name, out_shape=tuple(out_shape),
        in_specs=[any_spec] * n, out_specs=tuple([any_spec] * n),
        scratch_shapes=[pltpu.SemaphoreType.DMA((7 * n,)), pltpu.SemaphoreType.DMA((7 * n,)),
                        pltpu.SemaphoreType.DMA((n,))],
        compiler_params=pltpu.CompilerParams(has_side_effects=True),
    )(*srcs)


HBM_SPEC = pl.BlockSpec(memory_space=pltpu.HBM)
SEM_SPEC = pl.BlockSpec(memory_space=pltpu.SEMAPHORE)
DATAFLOW = pltpu.SideEffectType.DATAFLOW_SIDE_EFFECTING


def _peers():
    x, y, c = lax.axis_index("x"), lax.axis_index("y"), lax.axis_index("c")
    out = []
    for k, (dx, dy, dc) in enumerate(PEER_FLIPS):
        px = 1 - x if dx else x
        py = 1 - y if dy else y
        pc = 1 - c if dc else c
        out.append((k, (px, py, pc), 4 * px + 2 * py + pc))
    return 4 * x + 2 * y + c, out


def _exchange_start(name, srcs, lands, per_dest):
    n = len(srcs)

    def body(*refs):
        src, land = refs[:n], refs[n:2 * n]
        send_sems, recv_sems = refs[2 * n], refs[2 * n + 1]
        token = refs[-1]
        me, peers = _peers()
        for k, peer, pj in peers:
            for i in range(n):
                pltpu.make_async_remote_copy(
                    src_ref=src[i].at[pj] if per_dest[i] else src[i], dst_ref=land[i].at[me],
                    send_sem=send_sems.at[k * n + i], recv_sem=recv_sems.at[k * n + i],
                    device_id=peer, device_id_type=MESH).start()
        token[...] = jnp.zeros((8, LANES), F32)

    arrays = list(srcs) + list(lands)
    return pl.pallas_call(
        body, name=name,
        out_shape=(pltpu.SemaphoreType.DMA((7 * n,)), pltpu.SemaphoreType.DMA((7 * n,)),
                   *[pltpu.HBM(a.shape, a.dtype) for a in arrays], jax.ShapeDtypeStruct((8, LANES), F32)),
        in_specs=[HBM_SPEC] * (2 * n),
        out_specs=(SEM_SPEC, SEM_SPEC, *[HBM_SPEC] * (2 * n), pl.BlockSpec(memory_space=pltpu.VMEM)),
        input_output_aliases={i: 2 + i for i in range(2 * n)},
        compiler_params=pltpu.CompilerParams(has_side_effects=DATAFLOW),
    )(*[pltpu.with_memory_space_constraint(a, pltpu.HBM) for a in arrays])


def _exchange_wait(name, started, per_dest, after):
    n = (len(started) - 3) // 2
    send_sems, recv_sems = started[0], started[1]
    arrays = list(started[2:2 + 2 * n])

    def body(*refs):
        src, land = refs[:n], refs[n:2 * n]
        send, recv = refs[2 * n], refs[2 * n + 1]
        me, peers = _peers()
        for k, peer, pj in peers:
            for i in range(n):
                cp = pltpu.make_async_remote_copy(
                    src_ref=src[i].at[pj] if per_dest[i] else src[i], dst_ref=land[i].at[pj],
                    send_sem=send.at[k * n + i], recv_sem=recv.at[k * n + i],
                    device_id=peer, device_id_type=MESH)
                cp.wait_send()
                cp.wait_recv()

    outs = pl.pallas_call(
        body, name=name,
        out_shape=tuple(pltpu.HBM(a.shape, a.dtype) for a in arrays),
        in_specs=[HBM_SPEC] * (2 * n) + [SEM_SPEC, SEM_SPEC, pl.BlockSpec(memory_space=pl.ANY)],
        out_specs=tuple([HBM_SPEC] * (2 * n)),
        input_output_aliases={i: i for i in range(2 * n)},
        compiler_params=pltpu.CompilerParams(has_side_effects=DATAFLOW),
    )(*arrays, send_sems, recv_sems, after)
    return outs[:n], outs[n:]


def _mod_shard(c_all, w_ada):
    def body(c_ref, w_ref, o_ref):
        cv = c_ref[...]
        ca = cv * _sig(cv)
        o_ref[...] = _dot(ca.astype(BF16), w_ref[...].astype(BF16))

    return pl.pallas_call(
        body, name="mod_shard", out_shape=jax.ShapeDtypeStruct((N_DEV, w_ada.shape[1]), F32),
        compiler_params=_params(),
    )(c_all, w_ada)


def _fwd_in(x, nw1, modnb, bada, w_main, w_ba):
    def body(x_ref, nw_ref, mod_ref, b_ref, wm_ref, wb_ref, pm_ref, pb_ref, hb_ref):
        xv = x_ref[...]
        r = lax.rsqrt(jnp.mean(xv * xv, axis=-1, keepdims=True) + EPS)
        h = (xv * r * nw_ref[...]) * (1.0 + _mod(mod_ref, b_ref, 1)) + _mod(mod_ref, b_ref, 0)
        hb = h.astype(BF16)
        hb_ref[...] = hb
        pm_ref[...] = _dot_nt(hb, wm_ref[...])
        pb_ref[...] = _dot_nt(hb, wb_ref[...])

    return pl.pallas_call(
        body, name="fwd_in", grid=(NT,),
        in_specs=[pl.BlockSpec((TM, D), lambda i: (i, 0)), _const((1, D)), _const((1, 6 * D)), _const((1, 6 * D)),
                  _const((NMAIN, D)), _const((LANES, D))],
        out_specs=(pl.BlockSpec((TM, NMAIN), lambda i: (i, 0)), pl.BlockSpec((TM, LANES), lambda i: (i, 0)),
                   pl.BlockSpec((TM, D), lambda i: (i, 0))),
        out_shape=(jax.ShapeDtypeStruct((S, NMAIN), F32), jax.ShapeDtypeStruct((S, LANES), F32),
                   jax.ShapeDtypeStruct((S, D), BF16)),
        compiler_params=_params(dimension_semantics=("arbitrary",)),
    )(x, nw1, modnb, bada, w_main, w_ba)


def _group_mean_matrix():
    ii = lax.broadcasted_iota(jnp.int32, (CW, CW), 0) // GSZ
    jj = lax.broadcasted_iota(jnp.int32, (CW, CW), 1) // GSZ
    return jnp.where(ii == jj, 1.0 / GSZ, 0.0).astype(F32)


def _conf_fwd(p_main, conv_w, conv_b, gn_w, gn_b):
    def body(a_ref, g_ref, w_ref, b_ref, gw_ref, gb_ref, y_ref, oa_ref, ubuf):
        i = pl.program_id(0)

        @pl.when(i == 0)
        def _():
            ubuf[0:HALO, :] = jnp.zeros((HALO, CW), F32)

        ubuf[HALO:HALO + TM, :] = a_ref[...] * _sig(g_ref[...])
        acc = jnp.zeros((TM, CW), F32) + b_ref[...]
        for k in range(KC):
            off = HALO - (KC - 1) + k
            acc = acc + w_ref[k:k + 1, :] * ubuf[off:off + TM, :]
        y_ref[...] = acc
        ubuf[0:HALO, :] = ubuf[TM:TM + HALO, :]
        pm = _group_mean_matrix()
        dlt = acc - _dot(acc, pm, HI)
        var = _dot(dlt * dlt, pm, HI)
        o = dlt * lax.rsqrt(var + EPS) * gw_ref[...] + gb_ref[...]
        oa_ref[...] = o * _sig(o)

    return pl.pallas_call(
        body, name="conf_fwd", grid=(NT,),
        in_specs=[pl.BlockSpec((TM, CW), lambda i: (i, 0)), pl.BlockSpec((TM, CW), lambda i: (i, 1)),
                  _const((KC, CW)), _const((1, CW)), _const((1, CW)), _const((1, CW))],
        out_specs=(pl.BlockSpec((TM, CW), lambda i: (i, 0)), pl.BlockSpec((TM, CW), lambda i: (i, 0))),
        out_shape=(jax.ShapeDtypeStruct((S, CW), F32), jax.ShapeDtypeStruct((S, CW), F32)),
        scratch_shapes=[pltpu.VMEM((HALO + TM, CW), F32)],
        compiler_params=_params(dimension_semantics=("arbitrary",)),
    )(p_main, p_main, conv_w, conv_b, gn_w, gn_b)


def _tri_iota():
    ii = lax.broadcasted_iota(jnp.int32, (CL, CL), 0)
    jj = lax.broadcasted_iota(jnp.int32, (CL, CL), 1)
    return ii, jj


def _gdn_gates(ba, alog_l, dt_l):
    beta_all = _sig(ba)
    xg = ba + dt_l
    sp = jnp.maximum(xg, 0.0) + jnp.log(1.0 + jnp.exp(-jnp.abs(xg)))
    neg_a = -jnp.exp(alog_l)
    return beta_all, neg_a * sp, xg, neg_a


def _gdn_cumsum(g_all):
    ii, jj = _tri_iota()
    low = jnp.where(ii >= jj, 1.0, 0.0).astype(F32)
    gcum = _dot(low, g_all, HI)
    return gcum, jnp.transpose(gcum)


def _unit_lower_inverse(a):
    ii, jj = _tri_iota()
    t = jnp.where(ii == jj, 1.0, 0.0).astype(F32) - a
    p = _dot(a, a, HI)
    for _ in range(4):
        t = t + _dot(t, p, HI)
        p = _dot(p, p, HI)
    return t + _dot(t, p, HI)


def _head_terms(qh, kh, beta, gcol, grow):
    ii, jj = _tri_iota()
    causal = ii >= jj
    strict = ii > jj
    rq = lax.rsqrt(_rowsum(qh * qh) + EPS)
    rk = lax.rsqrt(_rowsum(kh * kh) + EPS)
    qn = qh * rq
    kn = kh * rk
    qs = qn * QSCALE
    decay = jnp.where(causal, jnp.exp(jnp.where(causal, gcol - grow, 0.0)), 0.0)
    gam = jnp.exp(gcol)
    gl = gcol[CL - 1:CL, :]
    kds = jnp.exp(gl - gcol)
    cd = jnp.exp(gl)
    kb = kn * beta
    a = jnp.where(strict, _dot_nt(kb, kn, GP) * decay, 0.0)
    qk = jnp.where(causal, _dot_nt(qs, kn, GP) * decay, 0.0)
    return dict(rq=rq, rk=rk, qn=qn, kn=kn, qs=qs, decay=decay, gam=gam, kds=kds, cd=cd, kb=kb, a=a, qk=qk,
                causal=causal, strict=strict)


def _short_conv(w_ref, buf):
    acc = w_ref[0:1, :] * buf[SH - KS + 1:SH - KS + 1 + CL, :]
    for k in range(1, KS):
        off = SH - (KS - 1) + k
        acc = acc + w_ref[k:k + 1, :] * buf[off:off + CL, :]
    return acc


def _gdn_fwd(p_main, p_ba, gdn_conv_w, alog_l, dt_l, gdn_nw):
    def body(q_ref, k_ref, v_ref, z_ref, ba_ref, w_ref, al_ref, dt_ref, nw_ref,
             ob_ref, o_ref, sin_ref, t_ref, cbuf, state):
        n = pl.program_id(0)

        @pl.when(n == 0)
        def _():
            cbuf[0:SH, :] = jnp.zeros((SH, 3 * GW), F32)
            state[...] = jnp.zeros((NH, DH, DH), F32)

        cbuf[SH:SH + CL, 0:GW] = q_ref[...]
        cbuf[SH:SH + CL, GW:2 * GW] = k_ref[...]
        cbuf[SH:SH + CL, 2 * GW:3 * GW] = v_ref[...]
        conv = _short_conv(w_ref, cbuf)
        cbuf[0:SH, :] = cbuf[CL:CL + SH, :]
        qkv = conv * _sig(conv)
        beta_all, g_all, _, _ = _gdn_gates(ba_ref[...], al_ref[...], dt_ref[...])
        gcum, gcum_t = _gdn_cumsum(g_all)
        for h in range(NH):
            lo = h * DH
            qh = qkv[:, lo:lo + DH]
            kh = qkv[:, GW + lo:GW + lo + DH]
            vh = qkv[:, 2 * GW + lo:2 * GW + lo + DH]
            beta = beta_all[:, h:h + 1]
            f = _head_terms(qh, kh, beta, gcum[:, NH + h:NH + h + 1], gcum_t[NH + h:NH + h + 1, :])
            t = _unit_lower_inverse(f["a"])
            u = _dot(t, vh * beta, GP)
            w = _dot(t, f["kb"] * f["gam"], GP)
            st = state[h]
            sin_ref[0, h] = st
            t_ref[0, h] = t
            v_new = u - _dot(w, st, GP)
            o = _dot(f["qs"] * f["gam"], st, GP) + _dot(f["qk"], v_new, GP)
            state[h] = st * f["cd"] + _dot_tn(f["kn"] * f["kds"], v_new, GP)
            o_ref[:, lo:lo + DH] = o
            r = lax.rsqrt(jnp.mean(o * o, axis=-1, keepdims=True) + EPS)
            zh = z_ref[:, lo:lo + DH]
            ob_ref[:, lo:lo + DH] = o * r * nw_ref[...] * (zh * _sig(zh))

    col = lambda j: pl.BlockSpec((CL, GW), lambda n: (n, j))
    return pl.pallas_call(
        body, name="gdn_fwd", grid=(NCH,),
        in_specs=[col(2), col(3), col(4), col(5), pl.BlockSpec((CL, LANES), lambda n: (n, 0)),
                  _const((KS, 3 * GW)), _const((1, LANES)), _const((1, LANES)), _const((1, DH))],
        out_specs=(pl.BlockSpec((CL, GW), lambda n: (n, 0)), pl.BlockSpec((CL, GW), lambda n: (n, 0)),
                   pl.BlockSpec((1, NH, DH, DH), lambda n: (n, 0, 0, 0)),
                   pl.BlockSpec((1, NH, CL, CL), lambda n: (n, 0, 0, 0))),
        out_shape=(jax.ShapeDtypeStruct((S, GW), F32), jax.ShapeDtypeStruct((S, GW), F32),
                   jax.ShapeDtypeStruct((NCH, NH, DH, DH), F32), jax.ShapeDtypeStruct((NCH, NH, CL, CL), F32)),
        scratch_shapes=[pltpu.VMEM((SH + CL, 3 * GW), F32), pltpu.VMEM((NH, DH, DH), F32)],
        compiler_params=_params(dimension_semantics=("arbitrary",)),
    )(p_main, p_main, p_main, p_main, p_ba, gdn_conv_w, alog_l, dt_l, gdn_nw)


def _fwd_out(out_a, out_b, x, modnb, bada, w_out):
    def body(oa_ref, ob_ref, x_ref, mod_ref, b_ref, w_ref, x1_ref, mix_ref, oab_ref):
        oa = oa_ref[...].astype(BF16)
        ob = ob_ref[...].astype(BF16)
        oab_ref[:, 0:CW] = oa
        oab_ref[:, CW:D] = ob
        mix = _dot(oa, w_ref[0:CW, :]) + _dot(ob, w_ref[CW:D, :])
        mix_ref[...] = mix
        x1_ref[...] = x_ref[...] + _mod(mod_ref, b_ref, 2) * mix

    tile = lambda w: pl.BlockSpec((TM, w), lambda i: (i, 0))
    return pl.pallas_call(
        body, name="fwd_out", grid=(NT,),
        in_specs=[tile(CW), tile(GW), tile(D), _const((1, 6 * D)), _const((1, 6 * D)), _const((D, D))],
        out_specs=(tile(D), tile(D), tile(D)),
        out_shape=(jax.ShapeDtypeStruct((S, D), F32), jax.ShapeDtypeStruct((S, D), F32),
                   jax.ShapeDtypeStruct((S, D), BF16)),
        compiler_params=_params(dimension_semantics=("arbitrary",)),
    )(out_a, out_b, x, modnb, bada, w_out)


TF = 128
FFN_STATS = 8


def _ffn_fwd_bwd(x1, tgt, modnb, bada, nw2, nfw, w_fi, w_fo):
    def body(x1_ref, tgt_ref, mod_ref, b_ref, nw2_ref, nfw_ref, wi_ref, wo_ref,
             dx1_ref, hb_ref, act_ref, dffn_ref, df_ref, st_ref):
        i = pl.program_id(0)

        @pl.when(i == 0)
        def _():
            st_ref[...] = jnp.zeros((FFN_STATS, D), F32)

        sh2, sc2, gt2 = _mod(mod_ref, b_ref, 3), _mod(mod_ref, b_ref, 4), _mod(mod_ref, b_ref, 5)
        x1v = x1_ref[...]
        r2 = lax.rsqrt(jnp.mean(x1v * x1v, axis=-1, keepdims=True) + EPS)
        xr2 = x1v * r2
        xn2 = xr2 * nw2_ref[...]
        hb = (xn2 * (1.0 + sc2) + sh2).astype(BF16)
        hb_ref[...] = hb
        fg, fu, sg = [], [], []
        ffn = jnp.zeros((TF, D), F32)
        for j in range(4):
            fgj = _dot_nt(hb, wi_ref[j])
            fuj = _dot_nt(hb, wi_ref[j + 4])
            sj = _sig(fgj)
            aj = (fgj * sj * fuj).astype(BF16)
            act_ref[j] = aj
            ffn = ffn + _dot(aj, wo_ref[j])
            fg.append(fgj)
            fu.append(fuj)
            sg.append(sj)
        x2 = x1v + gt2 * ffn
        r3 = lax.rsqrt(jnp.mean(x2 * x2, axis=-1, keepdims=True) + EPS)
        xr3 = x2 * r3
        err = xr3 * nfw_ref[...] - tgt_ref[...]
        loss = 0.5 * jnp.sum(jnp.mean(err * err, axis=-1, keepdims=True), axis=0, keepdims=True)
        dy = err * (1.0 / D)
        st_ref[0:1, :] += _colsum(dy * xr3)
        dyr = dy * nfw_ref[...]
        dx2 = r3 * (dyr - xr3 * jnp.mean(dyr * xr3, axis=-1, keepdims=True))
        st_ref[1:2, :] += _colsum(dx2 * ffn)
        st_ref[5:6, :] += jnp.broadcast_to(loss, (1, D))
        dffn = (gt2 * dx2).astype(BF16)
        dffn_ref[...] = dffn
        dh = jnp.zeros((TF, D), F32)
        for j in range(4):
            dact = _dot_nt(dffn, wo_ref[j])
            dfg = (dact * fu[j] * (sg[j] * (1.0 + fg[j] * (1.0 - sg[j])))).astype(BF16)
            dfu = (dact * (fg[j] * sg[j])).astype(BF16)
            df_ref[j] = dfg
            df_ref[j + 4] = dfu
            dh = dh + _dot(dfg, wi_ref[j]) + _dot(dfu, wi_ref[j + 4])
        st_ref[2:3, :] += _colsum(dh)
        st_ref[3:4, :] += _colsum(dh * xn2)
        dxn = dh * (1.0 + sc2)
        st_ref[4:5, :] += _colsum(dxn * xr2)
        dxr = dxn * nw2_ref[...]
        dx1_ref[...] = dx2 + r2 * (dxr - xr2 * jnp.mean(dxr * xr2, axis=-1, keepdims=True))

    tile = lambda w: pl.BlockSpec((TF, w), lambda i: (i, 0))
    return pl.pallas_call(
        body, name="ffn_fwd_bwd", grid=(S // TF,),
        in_specs=[tile(D), tile(D), _const((1, 6 * D)), _const((1, 6 * D)), _const((1, D)), _const((1, D)),
                  _const1((N_DEV, FB, D)), _const1((4, FB, D))],
        out_specs=(tile(D), tile(D), pl.BlockSpec((4, TF, FB), lambda i: (0, i, 0)), tile(D),
                   pl.BlockSpec((N_DEV, TF, FB), lambda i: (0, i, 0)), _const((FFN_STATS, D))),
        out_shape=(jax.ShapeDtypeStruct((S, D), F32), jax.ShapeDtypeStruct((S, D), BF16),
                   jax.ShapeDtypeStruct((4, S, FB), BF16), jax.ShapeDtypeStruct((S, D), BF16),
                   jax.ShapeDtypeStruct((N_DEV, S, FB), BF16), jax.ShapeDtypeStruct((FFN_STATS, D), F32)),
        compiler_params=_params(44, dimension_semantics=("arbitrary",)),
    )(x1, tgt, modnb, bada, nw2, nfw, w_fi, w_fo)


def _grad_w(name, a, b, nb):
    m, n = a.shape[1], b.shape[1]

    def body(a_ref, b_ref, o_ref):
        o_ref[...] = _dot_tn(a_ref[...], b_ref[...]).astype(BF16)

    return pl.pallas_call(
        body, name=name, grid=(m // nb,),
        in_specs=[pl.BlockSpec((S, nb), lambda j: (0, j)), _const((S, n))],
        out_specs=pl.BlockSpec((nb, n), lambda j: (j, 0)),
        out_shape=jax.ShapeDtypeStruct((m, n), BF16),
        compiler_params=_params(dimension_semantics=("arbitrary",)),
    )(a, b)


def _grad_w_ffn_in(hb2, df):
    def body(a_ref, b_ref, o_ref):
        o_ref[0] = _dot_tn(b_ref[0], a_ref[...]).astype(BF16)

    return pl.pallas_call(
        body, name="grad_w_ffn_in", grid=(N_DEV,),
        in_specs=[_const((S, D)), pl.BlockSpec((1, S, FB), lambda j: (j, 0, 0))],
        out_specs=pl.BlockSpec((1, FB, D), lambda j: (j, 0, 0)),
        out_shape=jax.ShapeDtypeStruct((N_DEV, FB, D), BF16),
        compiler_params=_params(dimension_semantics=("arbitrary",)),
    )(hb2, df)


def _grad_w_ffn_out(act, dffn):
    def body(a_ref, b_ref, o_ref):
        o_ref[0] = _dot_tn(a_ref[0], b_ref[...]).astype(BF16)

    return pl.pallas_call(
        body, name="grad_w_ffn_out", grid=(4,),
        in_specs=[pl.BlockSpec((1, S, FB), lambda j: (j, 0, 0)), _const((S, D))],
        out_specs=pl.BlockSpec((1, FB, D), lambda j: (j, 0, 0)),
        out_shape=jax.ShapeDtypeStruct((4, FB, D), BF16),
        compiler_params=_params(dimension_semantics=("arbitrary",)),
    )(act, dffn)


def _bwd_out(dx1, mix, modnb, bada, w_out):
    def body(dx_ref, mix_ref, mod_ref, b_ref, w_ref, dmix_ref, doa_ref, dob_ref, st_ref):
        i = pl.program_id(0)

        @pl.when(i == 0)
        def _():
            st_ref[...] = jnp.zeros((8, D), F32)

        dx = dx_ref[...]
        st_ref[0:1, :] += _colsum(dx * mix_ref[...])
        dmix = (_mod(mod_ref, b_ref, 2) * dx).astype(BF16)
        dmix_ref[...] = dmix
        doa_ref[...] = _dot_nt(dmix, w_ref[0:CW, :])
        dob_ref[...] = _dot_nt(dmix, w_ref[CW:D, :])

    tile = lambda w: pl.BlockSpec((TM, w), lambda i: (i, 0))
    return pl.pallas_call(
        body, name="bwd_out", grid=(NT,),
        in_specs=[tile(D), tile(D), _const((1, 6 * D)), _const((1, 6 * D)), _const((D, D))],
        out_specs=(tile(D), tile(CW), tile(GW), _const((8, D))),
        out_shape=(jax.ShapeDtypeStruct((S, D), BF16), jax.ShapeDtypeStruct((S, CW), F32),
                   jax.ShapeDtypeStruct((S, GW), F32), jax.ShapeDtypeStruct((8, D), F32)),
        compiler_params=_params(dimension_semantics=("arbitrary",)),
    )(dx1, mix, modnb, bada, w_out)


CONF_STATS = 40


def _conf_bwd(d_out_a, y, p_main, conv_w, gn_w, gn_b):
    def body(do_ref, y_ref, a_ref, g_ref, ah_ref, gh_ref, w_ref, gw_ref, gb_ref, dp_ref, st_ref, ubuf, dybuf):
        i = pl.program_id(0)

        @pl.when(i == 0)
        def _():
            st_ref[...] = jnp.zeros((CONF_STATS, CW), F32)
            dybuf[TM:TM + HALO, :] = jnp.zeros((HALO, CW), F32)

        pm = _group_mean_matrix()
        yv = y_ref[...]
        dlt = yv - _dot(yv, pm, HI)
        rstd = lax.rsqrt(_dot(dlt * dlt, pm, HI) + EPS)
        un = dlt * rstd
        o = un * gw_ref[...] + gb_ref[...]
        so = _sig(o)
        d_o = do_ref[...] * (so * (1.0 + o * (1.0 - so)))
        st_ref[33:34, :] += _colsum(d_o)
        st_ref[32:33, :] += _colsum(d_o * un)
        dun = d_o * gw_ref[...]
        dy = rstd * (dun - _dot(dun, pm, HI) - un * _dot(dun * un, pm, HI))
        st_ref[31:32, :] += _colsum(dy)
        dybuf[0:TM, :] = dy

        a = a_ref[...]
        sg = _sig(g_ref[...])
        first = i == NT - 1
        ubuf[0:HALO, :] = jnp.where(first, 0.0, ah_ref[...] * _sig(gh_ref[...]))
        ubuf[HALO:HALO + TM, :] = a * sg
        du = jnp.zeros((TM, CW), F32)
        for k in range(KC):
            off = HALO - (KC - 1) + k
            st_ref[k:k + 1, :] += _colsum(dy * ubuf[off:off + TM, :])
            du = du + w_ref[k:k + 1, :] * dybuf[KC - 1 - k:KC - 1 - k + TM, :]
        dybuf[TM:TM + HALO, :] = dybuf[0:HALO, :]
        dp_ref[:, 0:CW] = (du * sg).astype(BF16)
        dp_ref[:, CW:2 * CW] = (du * a * sg * (1.0 - sg)).astype(BF16)

    rev = lambda w, j=0: pl.BlockSpec((TM, w), lambda i: (NT - 1 - i, j))
    halo = lambda j: pl.BlockSpec((HALO, CW), lambda i: (jnp.maximum((NT - 1 - i) * (TM // HALO) - 1, 0), j))
    return pl.pallas_call(
        body, name="conf_bwd", grid=(NT,),
        in_specs=[rev(CW), rev(CW), rev(CW, 0), rev(CW, 1), halo(0), halo(1),
                  _const((KC, CW)), _const((1, CW)), _const((1, CW))],
        out_specs=(rev(2 * CW), _const((CONF_STATS, CW))),
        out_shape=(jax.ShapeDtypeStruct((S, 2 * CW), BF16), jax.ShapeDtypeStruct((CONF_STATS, CW), F32)),
        scratch_shapes=[pltpu.VMEM((HALO + TM, CW), F32), pltpu.VMEM((TM + HALO, CW), F32)],
        compiler_params=_params(dimension_semantics=("arbitrary",)),
    )(d_out_a, y, p_main, p_main, p_main, p_main, conv_w, gn_w, gn_b)


GDN_STATS = 8


def _gdn_bwd(d_out_b, o_pre, s_in, t_inv, p_main, p_ba, gdn_conv_w, alog_l, dt_l, gdn_nw):
    def body(dob_ref, o_ref, sin_ref, t_ref, q_ref, k_ref, v_ref, z_ref, qh_ref, kh_ref, vh_ref, ba_ref,
             w_ref, al_ref, dt_ref, nw_ref, dp_ref, dba_ref, st_ref, xbuf, dcbuf, dstate):
        n = pl.program_id(0)

        @pl.when(n == 0)
        def _():
            st_ref[...] = jnp.zeros((GDN_STATS, 3 * GW), F32)
            dcbuf[CL:CL + SH, :] = jnp.zeros((SH, 3 * GW), F32)
            dstate[...] = jnp.zeros((NH, DH, DH), F32)

        first = n == NCH - 1
        xbuf[0:SH, 0:GW] = jnp.where(first, 0.0, qh_ref[...])
        xbuf[0:SH, GW:2 * GW] = jnp.where(first, 0.0, kh_ref[...])
        xbuf[0:SH, 2 * GW:3 * GW] = jnp.where(first, 0.0, vh_ref[...])
        xbuf[SH:SH + CL, 0:GW] = q_ref[...]
        xbuf[SH:SH + CL, GW:2 * GW] = k_ref[...]
        xbuf[SH:SH + CL, 2 * GW:3 * GW] = v_ref[...]
        conv = _short_conv(w_ref, xbuf)
        sc = _sig(conv)
        qkv = conv * sc
        ba = ba_ref[...]
        beta_all, g_all, xg, neg_a = _gdn_gates(ba, al_ref[...], dt_ref[...])
        gcum, gcum_t = _gdn_cumsum(g_all)
        lane = lax.broadcasted_iota(jnp.int32, (CL, LANES), 1)
        row = lax.broadcasted_iota(jnp.int32, (CL, 1), 0)
        dgcum_all = jnp.zeros((CL, LANES), F32)
        dbeta_all = jnp.zeros((CL, LANES), F32)
        for h in range(NH):
            lo = h * DH
            qh = qkv[:, lo:lo + DH]
            kh = qkv[:, GW + lo:GW + lo + DH]
            vh = qkv[:, 2 * GW + lo:2 * GW + lo + DH]
            beta = beta_all[:, h:h + 1]
            f = _head_terms(qh, kh, beta, gcum[:, NH + h:NH + h + 1], gcum_t[NH + h:NH + h + 1, :])
            qn, kn, qs, kb, gam, kds, cd, decay = (f[s] for s in ("qn", "kn", "qs", "kb", "gam", "kds", "cd", "decay"))
            t = t_ref[0, h]
            st = sin_ref[0, h]
            vb = vh * beta
            kbg = kb * gam
            u = _dot(t, vb, GP)
            w = _dot(t, kbg, GP)
            v_new = u - _dot(w, st, GP)
            q_dec = qs * gam
            k_dec = kn * kds

            o = o_ref[:, lo:lo + DH]
            zh = z_ref[:, lo:lo + DH]
            sz = _sig(zh)
            r = lax.rsqrt(jnp.mean(o * o, axis=-1, keepdims=True) + EPS)
            orr = o * r
            d_out = dob_ref[:, lo:lo + DH]
            dz = d_out * (orr * nw_ref[...]) * (sz * (1.0 + zh * (1.0 - sz)))
            don = d_out * (zh * sz)
            st_ref[4:5, 0:DH] += _colsum(don * orr)
            tt = don * nw_ref[...]
            d_o = r * (tt - orr * jnp.mean(tt * orr, axis=-1, keepdims=True))

            ds_out = dstate[h]
            dv_new = _dot_tn(f["qk"], d_o, GP) + _dot(k_dec, ds_out, GP)
            dqk = jnp.where(f["causal"], _dot_nt(d_o, v_new, GP), 0.0)
            dq_dec = _dot_nt(d_o, st, GP)
            dstate[h] = _dot_tn(q_dec, d_o, GP) + cd * ds_out - _dot_tn(w, dv_new, GP)
            dcd = jnp.sum(_rowsum(st * ds_out), axis=0, keepdims=True)
            dk_dec = _dot_nt(v_new, ds_out, GP)
            dw = -_dot_nt(dv_new, st, GP)
            dt_m = _dot_nt(dv_new, vb, GP) + _dot_nt(dw, kbg, GP)
            dvb = _dot_tn(t, dv_new, GP)
            dkbg = _dot_tn(t, dw, GP)
            da = jnp.where(f["strict"], -_dot_tn(t, _dot_nt(dt_m, t, GP), GP), 0.0)
            dad = da * decay
            dqkd = dqk * decay
            dkb = _dot(dad, kn, GP) + dkbg * gam
            dkn = _dot_tn(dad, kb, GP) + _dot_tn(dqkd, qs, GP) + dk_dec * kds + dkb * beta
            dqs = _dot(dqkd, kn, GP) + dq_dec * gam
            m = da * f["a"] + dqk * f["qk"]
            tk = _rowsum(dk_dec * k_dec)
            dgl = jnp.sum(tk, axis=0, keepdims=True) + dcd * cd
            dgc = (_rowsum(m) - _rowsum(jnp.transpose(m)) + _rowsum(dq_dec * q_dec) - tk + _rowsum(dkbg * kbg)
                   + jnp.where(row == CL - 1, dgl, 0.0))
            dbeta = _rowsum(dkb * kn) + _rowsum(dvb * vh)
            dgcum_all = dgcum_all + jnp.where(lane == NH + h, dgc, 0.0)
            dbeta_all = dbeta_all + jnp.where(lane == h, dbeta, 0.0)
            dvh = dvb * beta
            dqn = dqs * QSCALE
            dqh = f["rq"] * (dqn - qn * _rowsum(dqn * qn))
            dkh = f["rk"] * (dkn - kn * _rowsum(dkn * kn))
            dsilu = lambda c0: sc[:, c0:c0 + DH] * (1.0 + conv[:, c0:c0 + DH] * (1.0 - sc[:, c0:c0 + DH]))
            dcbuf[0:CL, lo:lo + DH] = dqh * dsilu(lo)
            dcbuf[0:CL, GW + lo:GW + lo + DH] = dkh * dsilu(GW + lo)
            dcbuf[0:CL, 2 * GW + lo:2 * GW + lo + DH] = dvh * dsilu(2 * GW + lo)
            dp_ref[:, 3 * GW + lo:3 * GW + lo + DH] = dz.astype(BF16)

        ii, jj = _tri_iota()
        upper = jnp.where(ii <= jj, 1.0, 0.0).astype(F32)
        dg_all = _dot(upper, dgcum_all, HI)
        dxg = dg_all * neg_a * _sig(xg)
        st_ref[5:6, 0:LANES] += _colsum(dg_all * g_all)
        st_ref[6:7, 0:LANES] += _colsum(dxg)
        dbl = dbeta_all * beta_all * (1.0 - beta_all)
        dba_ref[...] = jnp.where(lane < NH, dbl, jnp.where(lane < 2 * NH, dxg, 0.0)).astype(BF16)

        dconv = dcbuf[0:CL, :]
        dx = w_ref[0:1, :] * dcbuf[KS - 1:KS - 1 + CL, :]
        st_ref[0:1, :] += _colsum(dconv * xbuf[SH - KS + 1:SH - KS + 1 + CL, :])
        for k in range(1, KS):
            off = SH - (KS - 1) + k
            st_ref[k:k + 1, :] += _colsum(dconv * xbuf[off:off + CL, :])
            dx = dx + w_ref[k:k + 1, :] * dcbuf[KS - 1 - k:KS - 1 - k + CL, :]
        dcbuf[CL:CL + SH, :] = dcbuf[0:SH, :]
        dp_ref[:, 0:3 * GW] = dx.astype(BF16)

    rev = lambda w, j=0: pl.BlockSpec((CL, w), lambda n: (NCH - 1 - n, j))
    halo = lambda j: pl.BlockSpec((SH, GW), lambda n: (jnp.maximum((NCH - 1 - n) * (CL // SH) - 1, 0), j))
    blk4 = lambda a, b: pl.BlockSpec((1, NH, a, b), lambda n: (NCH - 1 - n, 0, 0, 0))
    return pl.pallas_call(
        body, name="gdn_bwd", grid=(NCH,),
        in_specs=[rev(GW), rev(GW), blk4(DH, DH), blk4(CL, CL), rev(GW, 2), rev(GW, 3), rev(GW, 4), rev(GW, 5),
                  halo(2), halo(3), halo(4), rev(LANES), _const((KS, 3 * GW)), _const((1, LANES)),
                  _const((1, LANES)), _const((1, DH))],
        out_specs=(rev(4 * GW), rev(LANES), _const((GDN_STATS, 3 * GW))),
        out_shape=(jax.ShapeDtypeStruct((S, 4 * GW), BF16), jax.ShapeDtypeStruct((S, LANES), BF16),
                   jax.ShapeDtypeStruct((GDN_STATS, 3 * GW), F32)),
        scratch_shapes=[pltpu.VMEM((SH + CL, 3 * GW), F32), pltpu.VMEM((CL + SH, 3 * GW), F32),
                        pltpu.VMEM((NH, DH, DH), F32)],
        compiler_params=_params(dimension_semantics=("arbitrary",)),
    )(d_out_b, o_pre, s_in, t_inv, p_main, p_main, p_main, p_main, p_main, p_main, p_main, p_ba,
      gdn_conv_w, alog_l, dt_l, gdn_nw)


def _bwd_in(dp_conf, dp_gdn, dp_ba, x, dx1, nw1, modnb, bada, w_main, w_ba):
    def body(dc_ref, dg_ref, db_ref, x_ref, dx1_ref, nw_ref, mod_ref, b_ref, wm_ref, wb_ref, gx_ref, st_ref):
        i = pl.program_id(0)

        @pl.when(i == 0)
        def _():
            st_ref[...] = jnp.zeros((8, D), F32)

        dh = (_dot(dc_ref[...], wm_ref[0:2 * CW, :]) + _dot(dg_ref[...], wm_ref[2 * CW:NMAIN, :])
              + _dot(db_ref[...], wb_ref[...]))
        xv = x_ref[...]
        r = lax.rsqrt(jnp.mean(xv * xv, axis=-1, keepdims=True) + EPS)
        xr = xv * r
        st_ref[0:1, :] += _colsum(dh)
        st_ref[1:2, :] += _colsum(dh * (xr * nw_ref[...]))
        dxn = dh * (1.0 + _mod(mod_ref, b_ref, 1))
        st_ref[2:3, :] += _colsum(dxn * xr)
        dxr = dxn * nw_ref[...]
        gx_ref[...] = dx1_ref[...] + r * (dxr - xr * jnp.mean(dxr * xr, axis=-1, keepdims=True))

    tile = lambda w: pl.BlockSpec((TM, w), lambda i: (i, 0))
    return pl.pallas_call(
        body, name="bwd_in", grid=(NT,),
        in_specs=[tile(2 * CW), tile(4 * GW), tile(LANES), tile(D), tile(D), _const((1, D)), _const((1, 6 * D)),
                  _const((1, 6 * D)), _const((NMAIN, D)), _const((LANES, D))],
        out_specs=(tile(D), _const((8, D))),
        out_shape=(jax.ShapeDtypeStruct((S, D), F32), jax.ShapeDtypeStruct((8, D), F32)),
        compiler_params=_params(dimension_semantics=("arbitrary",)),
    )(dp_conf, dp_gdn, dp_ba, x, dx1, nw1, modnb, bada, w_main, w_ba)


def _adamw(w, g, m, v):
    m = ADAM_B1 * m + (1.0 - ADAM_B1) * g
    v = ADAM_B2 * v + (1.0 - ADAM_B2) * (g * g)
    m_hat = m / BC1
    v_hat = v / BC2
    delta = -ADAM_LR * (m_hat / (jnp.sqrt(v_hat) + ADAM_EPS) + ADAM_WD * w)
    return delta, m, v


ADAM_BLOCK_BYTES = 6 * 1024 * 1024


def _adam_tile(rows, cols):
    padded = -(-cols // LANES) * LANES
    if N_DEV * rows * padded * 4 <= ADAM_BLOCK_BYTES:
        return rows, cols
    best = None
    for tr in range(16, rows, 16):
        if rows % tr == 0 and N_DEV * tr * padded * 4 <= ADAM_BLOCK_BYTES:
            best = tr
    if best is not None:
        return best, cols
    rows_padded = -(-rows // 16) * 16
    tc = LANES
    for cand in range(LANES, cols, LANES):
        if cols % cand == 0 and N_DEV * rows_padded * cand * 4 <= ADAM_BLOCK_BYTES:
            tc = cand
    return rows, tc


def _reduce_adam(name, parts, w, m, v, own=None):
    rows, cols = w.shape
    tr, tc = _adam_tile(rows, cols)

    def body(*refs):
        p_ref, w_ref, m_ref, v_ref = refs[:4]
        g_ref, d_ref, nm_ref, nv_ref = refs[-4:]
        if own is None:
            part = lambda j: p_ref[j].astype(F32)
        else:
            me = 4 * lax.axis_index("x") + 2 * lax.axis_index("y") + lax.axis_index("c")
            part = lambda j: jnp.where(me == j, refs[4][...], p_ref[j]).astype(F32)
        g = part(0)
        for j in range(1, N_DEV):
            g = g + part(j)
        g_ref[...] = g
        d_ref[...], nm_ref[...], nv_ref[...] = _adamw(w_ref[...], g, m_ref[...], v_ref[...])

    blk = pl.BlockSpec((tr, tc), lambda i, j: (i, j))
    sds = jax.ShapeDtypeStruct((rows, cols), F32)
    extra = [] if own is None else [own]
    return pl.pallas_call(
        body, name=name, grid=(rows // tr, cols // tc),
        in_specs=[pl.BlockSpec((N_DEV, tr, tc), lambda i, j: (0, i, j)), blk, blk, blk] + [blk] * len(extra),
        out_specs=(blk, blk, blk, blk), out_shape=(sds, sds, sds, sds),
        compiler_params=_params(dimension_semantics=("arbitrary", "arbitrary")),
    )(parts, w, m, v, *extra)


def _ada_adam(c_all, dmod_sh, w, m, v):
    rows, cols = w.shape
    tr = 256

    def body(c_ref, dm_ref, w_ref, m_ref, v_ref, g_ref, d_ref, nm_ref, nv_ref):
        cv = c_ref[...]
        g = _dot_tn(cv * _sig(cv), dm_ref[...], HI)
        g_ref[...] = g
        d_ref[...], nm_ref[...], nv_ref[...] = _adamw(w_ref[...], g, m_ref[...], v_ref[...])

    blk = pl.BlockSpec((tr, cols), lambda i: (i, 0))
    sds = jax.ShapeDtypeStruct((rows, cols), F32)
    return pl.pallas_call(
        body, name="ada_adam", grid=(rows // tr,),
        in_specs=[pl.BlockSpec((N_DEV, tr), lambda i: (0, i)), _const((N_DEV, cols)), blk, blk, blk],
        out_specs=(blk, blk, blk, blk), out_shape=(sds, sds, sds, sds),
        compiler_params=_params(dimension_semantics=("arbitrary",)),
    )(c_all, dmod_sh, w, m, v)


def _lanes(a, at=0):
    return jnp.pad(a, ((0, 0), (at, LANES - at - a.shape[1])))


WEIGHT_NAMES = ["w_ada", "b_ada", "norm_mix_w", "w_in", "conv_w", "conv_b", "conv_gn_w", "conv_gn_b", "gdn_conv_w",
                "gdn_a_log", "gdn_dt_bias", "gdn_norm_w", "w_out", "norm_ffn_w", "w_ffn_in", "w_ffn_out",
                "norm_final_w"]


def _slab(b_ada, norm_mix_w, norm_ffn_w, norm_final_w, conv_b, conv_gn_w, conv_gn_b, gdn_norm_w, a_log, dt_bias):
    return jnp.concatenate([
        b_ada.reshape(48, LANES), norm_mix_w.reshape(8, LANES), norm_ffn_w.reshape(8, LANES),
        norm_final_w.reshape(8, LANES), conv_b.reshape(4, LANES), conv_gn_w.reshape(4, LANES),
        conv_gn_b.reshape(4, LANES), gdn_norm_w.reshape(1, LANES), _lanes(a_log), _lanes(dt_bias),
        jnp.zeros((1, LANES), F32)], axis=0)


def _unslab(t):
    return dict(b_ada=t[0:48].reshape(1, 6 * D), norm_mix_w=t[48:56].reshape(1, D),
                norm_ffn_w=t[56:64].reshape(1, D), norm_final_w=t[64:72].reshape(D),
                conv_b=t[72:76].reshape(1, CW), conv_gn_w=t[76:80].reshape(1, CW),
                conv_gn_b=t[80:84].reshape(1, CW), gdn_norm_w=t[84:85], gdn_a_log=t[85:86, 0:NH],
                gdn_dt_bias=t[86:87, 0:NH])


def _mix_forward(w, xs, modnb):
    w_main = w["w_in"][:NMAIN]
    w_ba = jnp.pad(w["w_in"][NMAIN:], ((0, LANES - 2 * NH), (0, 0)))
    alog_l = _lanes(w["gdn_a_log"], NH)
    dt_l = _lanes(w["gdn_dt_bias"], NH)
    p_main, p_ba, hb1 = _fwd_in(xs, w["norm_mix_w"], modnb, w["b_ada"], w_main, w_ba)
    y_conv, out_a = _conf_fwd(p_main, w["conv_w"], w["conv_b"], w["conv_gn_w"], w["conv_gn_b"])
    out_b, o_pre, s_in, t_inv = _gdn_fwd(p_main, p_ba, w["gdn_conv_w"], alog_l, dt_l, w["gdn_norm_w"])
    return dict(w_main=w_main, w_ba=w_ba, alog_l=alog_l, dt_l=dt_l, p_main=p_main, p_ba=p_ba, hb1=hb1,
                y_conv=y_conv, out_a=out_a, out_b=out_b, o_pre=o_pre, s_in=s_in, t_inv=t_inv)


def _ffn_stage(w, f, xs, tgt, modnb):
    x1, mix, oab = _fwd_out(f["out_a"], f["out_b"], xs, modnb, w["b_ada"], w["w_out"])
    dx1, hb2, act, dffn, df, st_ffn = _ffn_fwd_bwd(x1, tgt, modnb, w["b_ada"], w["norm_ffn_w"], w["norm_final_w"],
                                                   w["w_ffn_in"], w["w_ffn_out"])
    gw_ffn_in = _grad_w_ffn_in(hb2, df)
    gw_ffn_out = _grad_w_ffn_out(act, dffn)
    return dict(mix=mix, oab=oab, dx1=dx1, st_ffn=st_ffn, gw_ffn_in=gw_ffn_in, gw_ffn_out=gw_ffn_out)


def _mix_backward(w, f, g, xs, modnb):
    dmix, d_out_a, d_out_b, st_out = _bwd_out(g["dx1"], g["mix"], modnb, w["b_ada"], w["w_out"])
    gw_out = _grad_w("grad_w_out", g["oab"], dmix, 512)
    dp_conf, st_conf = _conf_bwd(d_out_a, f["y_conv"], f["p_main"], w["conv_w"], w["conv_gn_w"], w["conv_gn_b"])
    dp_gdn, dp_ba, st_gdn = _gdn_bwd(d_out_b, f["o_pre"], f["s_in"], f["t_inv"], f["p_main"], f["p_ba"],
                                     w["gdn_conv_w"], f["alog_l"], f["dt_l"], w["gdn_norm_w"])
    grad_x, st_in = _bwd_in(dp_conf, dp_gdn, dp_ba, xs, g["dx1"], w["norm_mix_w"], modnb, w["b_ada"], f["w_main"],
                            f["w_ba"])
    hb1 = f["hb1"]
    gw_in = jnp.concatenate(
        [_grad_w("grad_w_in_conf", dp_conf, hb1, 512), _grad_w("grad_w_in_gdn", dp_gdn, hb1, 512),
         _grad_w("grad_w_in_ba", dp_ba, hb1, LANES)[:2 * NH]], axis=0)
    st_ffn = g["st_ffn"]
    dmod = jnp.concatenate([st_in[0:1], st_in[1:2], st_out[0:1], st_ffn[2:3], st_ffn[3:4], st_ffn[1:2]], axis=1)
    small = jnp.concatenate([
        dmod.reshape(48, LANES), st_in[2:3].reshape(8, LANES), st_ffn[4:5].reshape(8, LANES),
        st_ffn[0:1].reshape(8, LANES), st_conf[31:32].reshape(4, LANES), st_conf[32:33].reshape(4, LANES),
        st_conf[33:34].reshape(4, LANES), st_gdn[4:5, 0:LANES],
        _lanes(st_gdn[5:6, NH:2 * NH]), _lanes(st_gdn[6:7, NH:2 * NH]), st_ffn[5:6, 0:LANES]], axis=0)
    return dict(grad_x=grad_x, gw_in=gw_in, gw_out=gw_out, gw_conv=st_conf[0:KC], gw_gconv=st_gdn[0:KS],
                small=small)


def _local(w, xs, tgt, modnb):
    f = _mix_forward(w, xs, modnb)
    g = _ffn_stage(w, f, xs, tgt, modnb)
    b = _mix_backward(w, f, g, xs, modnb)
    return dict(b, gw_ffn_in=g["gw_ffn_in"], gw_ffn_out=g["gw_ffn_out"])


def kernel(x, c, w_ada, b_ada, norm_mix_w, w_in, conv_w, conv_b, conv_gn_w, conv_gn_b, gdn_conv_w, gdn_a_log, gdn_dt_bias, gdn_norm_w, w_out, norm_ffn_w, w_ffn_in, w_ffn_out, norm_final_w, loss_target, m_w_ada, m_b_ada, m_norm_mix_w, m_w_in, m_conv_w, m_conv_b, m_conv_gn_w, m_conv_gn_b, m_gdn_conv_w, m_gdn_a_log, m_gdn_dt_bias, m_gdn_norm_w, m_w_out, m_norm_ffn_w, m_w_ffn_in, m_w_ffn_out, m_norm_final_w, v_w_ada, v_b_ada, v_norm_mix_w, v_w_in, v_conv_w, v_conv_b, v_conv_gn_w, v_conv_gn_b, v_gdn_conv_w, v_gdn_a_log, v_gdn_dt_bias, v_gdn_norm_w, v_w_out, v_norm_ffn_w, v_w_ffn_in, v_w_ffn_out, v_norm_final_w):
    me = 4 * lax.axis_index("x") + 2 * lax.axis_index("y") + lax.axis_index("c")
    xs = x.reshape(S, D)
    tgt = loss_target.reshape(S, D)

    late = [w_out[0].astype(BF16), jnp.transpose(w_ffn_in[0]).astype(BF16), w_ffn_out[0].astype(BF16)]
    g_c, g_cw, g_gcw, g_win, *late_lands = _exchange(
        "gather_weights", [c, conv_w[0], gdn_conv_w[0], jnp.transpose(w_in[0]).astype(BF16)] + late, [False] * 7,
        seed_only=(4, 5, 6))
    late_started = _exchange_start("gather_late_start", late, late_lands, [False] * 3)
    c_all = g_c.reshape(N_DEV, D) + late_started[-1][0:1, 0:1]
    w = dict(b_ada=b_ada, norm_mix_w=norm_mix_w, conv_b=conv_b, conv_gn_w=conv_gn_w, conv_gn_b=conv_gn_b,
             gdn_a_log=gdn_a_log, gdn_dt_bias=gdn_dt_bias, gdn_norm_w=gdn_norm_w, norm_ffn_w=norm_ffn_w,
             norm_final_w=norm_final_w.reshape(1, D),
             conv_w=jnp.transpose(g_cw, (1, 0, 2)).reshape(KC, CW),
             gdn_conv_w=jnp.transpose(g_gcw, (1, 0, 2)).reshape(KS, 3 * GW),
             w_in=g_win.reshape(NIN, D))

    (g_mod,) = _exchange("gather_mod", [_mod_shard(c_all, w_ada[0])], [False])
    modnb = lax.dynamic_index_in_dim(g_mod, me, axis=1, keepdims=False).reshape(1, 6 * D)

    f = _mix_forward(w, xs, modnb)
    _, (g_wout, g_wfi, g_wfo) = _exchange_wait("gather_late_wait", late_started, [False] * 3, f["out_b"])
    w.update(w_out=g_wout.reshape(D, D), w_ffn_in=g_wfi, w_ffn_out=g_wfo.reshape(4, FB, D))
    g = _ffn_stage(w, f, xs, tgt, modnb)

    ffn_grads = [g["gw_ffn_in"], g["gw_ffn_out"].reshape(N_DEV, DFF // N_DEV, D)]
    ffn_started = _exchange_start("scatter_ffn_start", ffn_grads,
                                  [lax.empty(a.shape, a.dtype) for a in ffn_grads], [True] * 2)
    loc = _mix_backward(w, f, g, xs, modnb + ffn_started[-1][0:1, 0:1])

    gw_in = loc["gw_in"].reshape(N_DEV, NIN // N_DEV, D)
    gw_conv = jnp.transpose(loc["gw_conv"].reshape(KC, N_DEV, CW // N_DEV), (1, 0, 2))
    gw_gconv = jnp.transpose(loc["gw_gconv"].reshape(KS, N_DEV, 3 * GW // N_DEV), (1, 0, 2))
    r_in, r_out, r_cw, r_gcw = _exchange(
        "scatter_grads", [gw_in, loc["gw_out"].reshape(N_DEV, D // N_DEV, D), gw_conv, gw_gconv], [True] * 4)
    (sent_fi, sent_fo), (r_fi, r_fo) = _exchange_wait("scatter_ffn_wait", ffn_started, [True] * 2, r_cw)
    own_fi = lax.dynamic_index_in_dim(sent_fi, me, axis=0, keepdims=False)
    own_fo = lax.dynamic_index_in_dim(sent_fo, me, axis=0, keepdims=False)

    (g_small,) = _exchange("gather_small", [loc["small"]], [False])
    sw = _slab(b_ada, norm_mix_w, norm_ffn_w, norm_final_w, conv_b, conv_gn_w, conv_gn_b, gdn_norm_w, gdn_a_log,
               gdn_dt_bias)
    sm = _slab(m_b_ada, m_norm_mix_w, m_norm_ffn_w, m_norm_final_w, m_conv_b, m_conv_gn_w, m_conv_gn_b,
               m_gdn_norm_w, m_gdn_a_log, m_gdn_dt_bias)
    sv = _slab(v_b_ada, v_norm_mix_w, v_norm_ffn_w, v_norm_final_w, v_conv_b, v_conv_gn_w, v_conv_gn_b,
               v_gdn_norm_w, v_gdn_a_log, v_gdn_dt_bias)
    small_out = _reduce_adam("adam_small", g_small, sw, sm, sv)
    loss = small_out[0][SMALL_ROWS - 1, 0]
    res = [_unslab(t) for t in small_out]

    dmod_rows = g_small[:, 0:48, :].reshape(N_DEV, 6 * D)
    dmod_sh = lax.dynamic_slice_in_dim(dmod_rows, me * (6 * D // N_DEV), 6 * D // N_DEV, axis=1)

    big = dict(
        w_ada=_ada_adam(c_all, dmod_sh, w_ada[0], m_w_ada[0], v_w_ada[0]),
        w_in=[jnp.transpose(t) for t in _reduce_adam(
            "adam_w_in", r_in, jnp.transpose(w_in[0]), jnp.transpose(m_w_in[0]), jnp.transpose(v_w_in[0]))],
        conv_w=_reduce_adam("adam_conv_w", r_cw, conv_w[0], m_conv_w[0], v_conv_w[0]),
        gdn_conv_w=_reduce_adam("adam_gdn_conv_w", r_gcw, gdn_conv_w[0], m_gdn_conv_w[0], v_gdn_conv_w[0]),
        w_out=_reduce_adam("adam_w_out", r_out, w_out[0], m_w_out[0], v_w_out[0]),
        w_ffn_in=[jnp.transpose(t) for t in _reduce_adam(
            "adam_w_ffn_in", r_fi, jnp.transpose(w_ffn_in[0]), jnp.transpose(m_w_ffn_in[0]),
            jnp.transpose(v_w_ffn_in[0]), own_fi)],
        w_ffn_out=_reduce_adam("adam_w_ffn_out", r_fo, w_ffn_out[0], m_w_ffn_out[0], v_w_ffn_out[0], own_fo),
    )
    outs = [loss, loc["grad_x"].reshape(1, S, D)]
    for kind in range(4):
        for nm in WEIGHT_NAMES:
            outs.append(big[nm][kind][None] if nm in big else res[kind][nm])
    return tuple(outs)
```

```python
import functools

import jax
import jax.numpy as jnp
from jax import lax
from jax.experimental import pallas as pl
from jax.experimental.pallas import tpu as pltpu

F32 = jnp.float32
BF16 = jnp.bfloat16
HI = lax.Precision.HIGHEST
MESH = pl.DeviceIdType.MESH

N_DEV = 8
S = 2048
D = 1024
TM = 256
NT = S // TM
CW = 512
KC = 31
NG = 8
GSZ = CW // NG
HALO = 32
GW = 512
NH = 4
DH = 128
KS = 4
SH = 8
CL = 64
NCH = S // CL
NMAIN = 2 * CW + 4 * GW
NIN = NMAIN + 2 * NH
DFF = 2816
FB = DFF // 4
EPS = 1e-6
QSCALE = DH ** -0.5
LANES = 128
SMALL_ROWS = 88

ADAM_LR = 0.001
ADAM_B1 = 0.9
ADAM_B2 = 0.999
ADAM_EPS = 1e-08
ADAM_WD = 0.01
ADAM_STEP = 10
BC1 = 1.0 - ADAM_B1 ** ADAM_STEP
BC2 = 1.0 - ADAM_B2 ** ADAM_STEP

MIB = 1024 * 1024
VMEM_LIMIT_MIB = 32


def _params(limit_mib=VMEM_LIMIT_MIB, **kw):
    return pltpu.CompilerParams(vmem_limit_bytes=limit_mib * MIB, **kw)


def _sig(x):
    return jax.nn.sigmoid(x)


GP = BF16


def _operands(a, b, prec):
    if prec is BF16:
        return a.astype(BF16), b.astype(BF16), None
    return a, b, prec


def _dot(a, b, prec=None):
    a, b, prec = _operands(a, b, prec)
    return jnp.dot(a, b, preferred_element_type=F32, precision=prec)


def _dot_nt(a, b, prec=None):
    a, b, prec = _operands(a, b, prec)
    return lax.dot_general(a, b, (((1,), (1,)), ((), ())), preferred_element_type=F32, precision=prec)


def _dot_tn(a, b, prec=None):
    a, b, prec = _operands(a, b, prec)
    return lax.dot_general(a, b, (((0,), (0,)), ((), ())), preferred_element_type=F32, precision=prec)


def _lockstep(gens):
    gens = list(gens)
    while gens:
        alive = []
        for g in gens:
            try:
                next(g)
                alive.append(g)
            except StopIteration:
                pass
        gens = alive


def _rowsum(x):
    return jnp.sum(x, axis=-1, keepdims=True)


def _colsum(x):
    return jnp.sum(x, axis=0, keepdims=True)


def _mod(mod_ref, b_ref, k):
    return mod_ref[:, k * D:(k + 1) * D] + b_ref[:, k * D:(k + 1) * D]


def _const(shape):
    nd = len(shape)
    return pl.BlockSpec(shape, lambda *_: (0,) * nd)


def _const1(shape):
    nd = len(shape)
    return pl.BlockSpec(shape, lambda *_: (0,) * nd, pipeline_mode=pl.Buffered(1))


PEER_FLIPS = [(dx, dy, dc) for dx in (0, 1) for dy in (0, 1) for dc in (0, 1)][1:]


def _exchange(name, srcs, per_dest, seed_only=()):
    n = len(srcs)
    out_shape = []
    for a, pd in zip(srcs, per_dest):
        blk = a.shape[1:] if pd else a.shape
        out_shape.append(jax.ShapeDtypeStruct((N_DEV,) + tuple(blk), a.dtype))

    def body(*refs):
        src = refs[:n]
        dst = refs[n:2 * n]
        send_sems, recv_sems, local_sems = refs[2 * n:]
        x, y, c = lax.axis_index("x"), lax.axis_index("y"), lax.axis_index("c")
        me = 4 * x + 2 * y + c

        def piece(i, j):
            return src[i].at[j] if per_dest[i] else src[i]

        copies = []
        for k, (dx, dy, dc) in enumerate(PEER_FLIPS):
            px = 1 - x if dx else x
            py = 1 - y if dy else y
            pc = 1 - c if dc else c
            pj = 4 * px + 2 * py + pc
            for i in range(n):
                if i in seed_only:
                    continue
                cp = pltpu.make_async_remote_copy(
                    src_ref=piece(i, pj), dst_ref=dst[i].at[me],
                    send_sem=send_sems.at[k * n + i], recv_sem=recv_sems.at[k * n + i],
                    device_id=(px, py, pc), device_id_type=MESH)
                cp.start()
                arrive = pltpu.make_async_remote_copy(
                    src_ref=piece(i, pj), dst_ref=dst[i].at[pj],
                    send_sem=send_sems.at[k * n + i], recv_sem=recv_sems.at[k * n + i],
                    device_id=(px, py, pc), device_id_type=MESH)
                copies.append((cp, arrive))
        own = []
        for i in range(n):
            lc = pltpu.make_async_copy(piece(i, me), dst[i].at[me], local_sems.at[i])
            lc.start()
            own.append(lc)
        for cp, arrive in copies:
            arrive.wait_recv()
        for cp, arrive in copies:
            cp.wait_send()
        for lc in own:
            lc.wait()

    any_spec = pl.BlockSpec(memory_space=pl.ANY)
    return pl.pallas_call(
        body, name=name, out_shape=tuple(out_shape),
        in_specs=[any_spec] * n, out_specs=tuple([any_spec] * n),
        scratch_shapes=[pltpu.SemaphoreType.DMA((7 * n,)), pltpu.SemaphoreType.DMA((7 * n,)),
                        pltpu.SemaphoreType.DMA((n,))],
        compiler_params=pltpu.CompilerParams(has_side_effects=True),
    )(*srcs)


HBM_SPEC = pl.BlockSpec(memory_space=pltpu.HBM)
SEM_SPEC = pl.BlockSpec(memory_space=pltpu.SEMAPHORE)
DATAFLOW = pltpu.SideEffectType.DATAFLOW_SIDE_EFFECTING


def _peers():
    x, y, c = lax.axis_index("x"), lax.axis_index("y"), lax.axis_index("c")
    out = []
    for k, (dx, dy, dc) in enumerate(PEER_FLIPS):
        px = 1 - x if dx else x
        py = 1 - y if dy else y
        pc = 1 - c if dc else c
        out.append((k, (px, py, pc), 4 * px + 2 * py + pc))
    return 4 * x + 2 * y + c, out


def _exchange_start(name, srcs, lands, per_dest):
    n = len(srcs)

    def body(*refs):
        src, land = refs[:n], refs[n:2 * n]
        send_sems, recv_sems = refs[2 * n], refs[2 * n + 1]
        token = refs[-1]
        me, peers = _peers()
        for k, peer, pj in peers:
            for i in range(n):
                pltpu.make_async_remote_copy(
                    src_ref=src[i].at[pj] if per_dest[i] else src[i], dst_ref=land[i].at[me],
                    send_sem=send_sems.at[k * n + i], recv_sem=recv_sems.at[k * n + i],
                    device_id=peer, device_id_type=MESH).start()
        token[...] = jnp.zeros((8, LANES), F32)

    arrays = list(srcs) + list(lands)
    return pl.pallas_call(
        body, name=name,
        out_shape=(pltpu.SemaphoreType.DMA((7 * n,)), pltpu.SemaphoreType.DMA((7 * n,)),
                   *[pltpu.HBM(a.shape, a.dtype) for a in arrays], jax.ShapeDtypeStruct((8, LANES), F32)),
        in_specs=[HBM_SPEC] * (2 * n),
        out_specs=(SEM_SPEC, SEM_SPEC, *[HBM_SPEC] * (2 * n), pl.BlockSpec(memory_space=pltpu.VMEM)),
        input_output_aliases={i: 2 + i for i in range(2 * n)},
        compiler_params=pltpu.CompilerParams(has_side_effects=DATAFLOW),
    )(*[pltpu.with_memory_space_constraint(a, pltpu.HBM) for a in arrays])


def _exchange_wait(name, started, per_dest, after):
    n = (len(started) - 3) // 2
    send_sems, recv_sems = started[0], started[1]
    arrays = list(started[2:2 + 2 * n])

    def body(*refs):
        src, land = refs[:n], refs[n:2 * n]
        send, recv = refs[2 * n], refs[2 * n + 1]
        me, peers = _peers()
        for k, peer, pj in peers:
            for i in range(n):
                cp = pltpu.make_async_remote_copy(
                    src_ref=src[i].at[pj] if per_dest[i] else src[i], dst_ref=land[i].at[pj],
                    send_sem=send.at[k * n + i], recv_sem=recv.at[k * n + i],
                    device_id=peer, device_id_type=MESH)
                cp.wait_send()
                cp.wait_recv()

    outs = pl.pallas_call(
        body, name=name,
        out_shape=tuple(pltpu.HBM(a.shape, a.dtype) for a in arrays),
        in_specs=[HBM_SPEC] * (2 * n) + [SEM_SPEC, SEM_SPEC, pl.BlockSpec(memory_space=pl.ANY)],
        out_specs=tuple([HBM_SPEC] * (2 * n)),
        input_output_aliases={i: i for i in range(2 * n)},
        compiler_params=pltpu.CompilerParams(has_side_effects=DATAFLOW),
    )(*arrays, send_sems, recv_sems, after)
    return outs[:n], outs[n:]


def _mod_shard(c_all, w_ada):
    def body(c_ref, w_ref, o_ref):
        cv = c_ref[...]
        ca = cv * _sig(cv)
        o_ref[...] = _dot(ca.astype(BF16), w_ref[...].astype(BF16))

    return pl.pallas_call(
        body, name="mod_shard", out_shape=jax.ShapeDtypeStruct((N_DEV, w_ada.shape[1]), F32),
        compiler_params=_params(),
    )(c_all, w_ada)


def _fwd_in(x, nw1, modnb, bada, w_main, w_ba):
    def body(x_ref, nw_ref, mod_ref, b_ref, wm_ref, wb_ref, pm_ref, pb_ref, hb_ref):
        xv = x_ref[...]
        r = lax.rsqrt(jnp.mean(xv * xv, axis=-1, keepdims=True) + EPS)
        h = (xv * r * nw_ref[...]) * (1.0 + _mod(mod_ref, b_ref, 1)) + _mod(mod_ref, b_ref, 0)
        hb = h.astype(BF16)
        hb_ref[...] = hb
        pm_ref[...] = _dot_nt(hb, wm_ref[...])
        pb_ref[...] = _dot_nt(hb, wb_ref[...])

    return pl.pallas_call(
        body, name="fwd_in", grid=(NT,),
        in_specs=[pl.BlockSpec((TM, D), lambda i: (i, 0)), _const((1, D)), _const((1, 6 * D)), _const((1, 6 * D)),
                  _const((NMAIN, D)), _const((LANES, D))],
        out_specs=(pl.BlockSpec((TM, NMAIN), lambda i: (i, 0)), pl.BlockSpec((TM, LANES), lambda i: (i, 0)),
                   pl.BlockSpec((TM, D), lambda i: (i, 0))),
        out_shape=(jax.ShapeDtypeStruct((S, NMAIN), F32), jax.ShapeDtypeStruct((S, LANES), F32),
                   jax.ShapeDtypeStruct((S, D), BF16)),
        compiler_params=_params(dimension_semantics=("arbitrary",)),
    )(x, nw1, modnb, bada, w_main, w_ba)


def _group_mean_matrix():
    ii = lax.broadcasted_iota(jnp.int32, (CW, CW), 0) // GSZ
    jj = lax.broadcasted_iota(jnp.int32, (CW, CW), 1) // GSZ
    return jnp.where(ii == jj, 1.0 / GSZ, 0.0).astype(F32)


def _conf_fwd(p_main, conv_w, conv_b, gn_w, gn_b):
    def body(a_ref, g_ref, w_ref, b_ref, gw_ref, gb_ref, y_ref, oa_ref, ubuf):
        i = pl.program_id(0)

        @pl.when(i == 0)
        def _():
            ubuf[0:HALO, :] = jnp.zeros((HALO, CW), F32)

        ubuf[HALO:HALO + TM, :] = a_ref[...] * _sig(g_ref[...])
        acc = jnp.zeros((TM, CW), F32) + b_ref[...]
        for k in range(KC):
            off = HALO - (KC - 1) + k
            acc = acc + w_ref[k:k + 1, :] * ubuf[off:off + TM, :]
        y_ref[...] = acc
        ubuf[0:HALO, :] = ubuf[TM:TM + HALO, :]
        pm = _group_mean_matrix()
        dlt = acc - _dot(acc, pm, HI)
        var = _dot(dlt * dlt, pm, HI)
        o = dlt * lax.rsqrt(var + EPS) * gw_ref[...] + gb_ref[...]
        oa_ref[...] = o * _sig(o)

    return pl.pallas_call(
        body, name="conf_fwd", grid=(NT,),
        in_specs=[pl.BlockSpec((TM, CW), lambda i: (i, 0)), pl.BlockSpec((TM, CW), lambda i: (i, 1)),
                  _const((KC, CW)), _const((1, CW)), _const((1, CW)), _const((1, CW))],
        out_specs=(pl.BlockSpec((TM, CW), lambda i: (i, 0)), pl.BlockSpec((TM, CW), lambda i: (i, 0))),
        out_shape=(jax.ShapeDtypeStruct((S, CW), F32), jax.ShapeDtypeStruct((S, CW), F32)),
        scratch_shapes=[pltpu.VMEM((HALO + TM, CW), F32)],
        compiler_params=_params(dimension_semantics=("arbitrary",)),
    )(p_main, p_main, conv_w, conv_b, gn_w, gn_b)


def _tri_iota():
    ii = lax.broadcasted_iota(jnp.int32, (CL, CL), 0)
    jj = lax.broadcasted_iota(jnp.int32, (CL, CL), 1)
    return ii, jj


def _gdn_gates(ba, alog_l, dt_l):
    beta_all = _sig(ba)
    xg = ba + dt_l
    sp = jnp.maximum(xg, 0.0) + jnp.log(1.0 + jnp.exp(-jnp.abs(xg)))
    neg_a = -jnp.exp(alog_l)
    return beta_all, neg_a * sp, xg, neg_a


def _gdn_cumsum(g_all):
    ii, jj = _tri_iota()
    low = jnp.where(ii >= jj, 1.0, 0.0).astype(F32)
    gcum = _dot(low, g_all, HI)
    return gcum, jnp.transpose(gcum)


def _unit_lower_inverses(mats):
    ii, jj = _tri_iota()
    eye = jnp.where(ii == jj, 1.0, 0.0).astype(F32)
    ts = [eye - a for a in mats]
    ps = [_dot(a, a, HI) for a in mats]
    for _ in range(4):
        ts = [t + _dot(t, p, HI) for t, p in zip(ts, ps)]
        ps = [_dot(p, p, HI) for p in ps]
    return [t + _dot(t, p, HI) for t, p in zip(ts, ps)]


def _head_terms(qh, kh, beta, gcol, grow):
    ii, jj = _tri_iota()
    causal = ii >= jj
    strict = ii > jj
    rq = lax.rsqrt(_rowsum(qh * qh) + EPS)
    rk = lax.rsqrt(_rowsum(kh * kh) + EPS)
    qn = qh * rq
    kn = kh * rk
    qs = qn * QSCALE
    decay = jnp.where(causal, jnp.exp(jnp.where(causal, gcol - grow, 0.0)), 0.0)
    gam = jnp.exp(gcol)
    gl = gcol[CL - 1:CL, :]
    kds = jnp.exp(gl - gcol)
    cd = jnp.exp(gl)
    kb = kn * beta
    a = jnp.where(strict, _dot_nt(kb, kn, GP) * decay, 0.0)
    qk = jnp.where(causal, _dot_nt(qs, kn, GP) * decay, 0.0)
    return dict(rq=rq, rk=rk, qn=qn, kn=kn, qs=qs, decay=decay, gam=gam, kds=kds, cd=cd, kb=kb, a=a, qk=qk,
                causal=causal, strict=strict)


def _short_conv(w_ref, buf, rows=CL):
    acc = w_ref[0:1, :] * buf[SH - KS + 1:SH - KS + 1 + rows, :]
    for k in range(1, KS):
        off = SH - (KS - 1) + k
        acc = acc + w_ref[k:k + 1, :] * buf[off:off + rows, :]
    return acc


CPS = 4
TG = CPS * CL


def _gdn_prep(p_main, p_ba, gdn_conv_w, alog_l, dt_l):
    def body(q_ref, k_ref, v_ref, qh_ref, kh_ref, vh_ref, ba_ref, w_ref, al_ref, dt_ref,
             wo_ref, uo_ref, qg_ref, kd_ref, qk_ref, cd_ref, t_ref, xbuf):
        i = pl.program_id(0)
        first = i == 0
        xbuf[0:SH, 0:GW] = jnp.where(first, 0.0, qh_ref[...])
        xbuf[0:SH, GW:2 * GW] = jnp.where(first, 0.0, kh_ref[...])
        xbuf[0:SH, 2 * GW:3 * GW] = jnp.where(first, 0.0, vh_ref[...])
        xbuf[SH:SH + TG, 0:GW] = q_ref[...]
        xbuf[SH:SH + TG, GW:2 * GW] = k_ref[...]
        xbuf[SH:SH + TG, 2 * GW:3 * GW] = v_ref[...]
        conv = _short_conv(w_ref, xbuf, TG)
        qkv = conv * _sig(conv)
        beta_all, g_all, _, _ = _gdn_gates(ba_ref[...], al_ref[...], dt_ref[...])
        lane = lax.broadcasted_iota(jnp.int32, (8, LANES), 1)
        cums = [_gdn_cumsum(g_all[cc * CL:(cc + 1) * CL, :]) for cc in range(CPS)]
        pairs = [(cc, h) for cc in range(CPS) for h in range(NH)]
        terms, vbs = [], []
        for cc, h in pairs:
            r0, lo = cc * CL, h * DH
            beta = beta_all[r0:r0 + CL, h:h + 1]
            gcum, gcum_t = cums[cc]
            terms.append(_head_terms(qkv[r0:r0 + CL, lo:lo + DH], qkv[r0:r0 + CL, GW + lo:GW + lo + DH], beta,
                                     gcum[:, NH + h:NH + h + 1], gcum_t[NH + h:NH + h + 1, :]))
            vbs.append(qkv[r0:r0 + CL, 2 * GW + lo:2 * GW + lo + DH] * beta)
        invs = _unit_lower_inverses([f["a"] for f in terms])
        cds = [jnp.zeros((8, LANES), F32) for _ in range(CPS)]
        for (cc, h), f, t, vb in zip(pairs, terms, invs, vbs):
            r0, lo = cc * CL, h * DH
            t_ref[cc, h] = t
            uo_ref[r0:r0 + CL, lo:lo + DH] = _dot(t, vb, GP)
            wo_ref[r0:r0 + CL, lo:lo + DH] = _dot(t, f["kb"] * f["gam"], GP).astype(BF16)
            qg_ref[r0:r0 + CL, lo:lo + DH] = (f["qs"] * f["gam"]).astype(BF16)
            kd_ref[r0:r0 + CL, lo:lo + DH] = (f["kn"] * f["kds"]).astype(BF16)
            qk_ref[cc, h] = f["qk"].astype(BF16)
            cds[cc] = cds[cc] + jnp.where(lane == h, f["cd"], 0.0)
        for cc in range(CPS):
            cd_ref[cc] = cds[cc]

    col = lambda j: pl.BlockSpec((TG, GW), lambda i: (i, j))
    halo = lambda j: pl.BlockSpec((SH, GW), lambda i: (jnp.maximum(i * (TG // SH) - 1, 0), j))
    tile = lambda: pl.BlockSpec((TG, GW), lambda i: (i, 0))
    sq = lambda: pl.BlockSpec((CPS, NH, CL, CL), lambda i: (i, 0, 0, 0))
    return pl.pallas_call(
        body, name="gdn_prep", grid=(NCH // CPS,),
        in_specs=[col(2), col(3), col(4), halo(2), halo(3), halo(4), pl.BlockSpec((TG, LANES), lambda i: (i, 0)),
                  _const((KS, 3 * GW)), _const((1, LANES)), _const((1, LANES))],
        out_specs=(tile(), tile(), tile(), tile(), sq(), pl.BlockSpec((CPS, 8, LANES), lambda i: (i, 0, 0)), sq()),
        out_shape=(jax.ShapeDtypeStruct((S, GW), BF16), jax.ShapeDtypeStruct((S, GW), F32),
                   jax.ShapeDtypeStruct((S, GW), BF16), jax.ShapeDtypeStruct((S, GW), BF16),
                   jax.ShapeDtypeStruct((NCH, NH, CL, CL), BF16), jax.ShapeDtypeStruct((NCH, 8, LANES), F32),
                   jax.ShapeDtypeStruct((NCH, NH, CL, CL), F32)),
        scratch_shapes=[pltpu.VMEM((SH + TG, 3 * GW), F32)],
        compiler_params=_params(dimension_semantics=("arbitrary",)),
    )(p_main, p_main, p_main, p_main, p_main, p_main, p_ba, gdn_conv_w, alog_l, dt_l)


def _gdn_scan(w_o, u_o, qg, kd, qk, cd, p_main, gdn_nw):
    def body(w_ref, u_ref, qg_ref, kd_ref, qk_ref, cd_ref, z_ref, nw_ref, ob_ref, o_ref, sin_ref, state):
        n = pl.program_id(0)

        @pl.when(n == 0)
        def _():
            state[...] = jnp.zeros((NH, DH, DH), F32)

        def head(h):
            lo = h * DH
            st = state[h]
            sin_ref[0, h] = st
            sb = st.astype(BF16)
            v_new = u_ref[:, lo:lo + DH] - _dot(w_ref[:, lo:lo + DH], sb)
            yield
            vb = v_new.astype(BF16)
            o = _dot(qg_ref[:, lo:lo + DH], sb) + _dot(qk_ref[0, h], vb)
            state[h] = st * cd_ref[0, 0:1, h:h + 1] + _dot_tn(kd_ref[:, lo:lo + DH], vb)
            yield
            o_ref[:, lo:lo + DH] = o
            r = lax.rsqrt(jnp.mean(o * o, axis=-1, keepdims=True) + EPS)
            zh = z_ref[:, lo:lo + DH]
            ob_ref[:, lo:lo + DH] = o * r * nw_ref[...] * (zh * _sig(zh))

        _lockstep(head(h) for h in range(NH))

    tile = lambda: pl.BlockSpec((CL, GW), lambda n: (n, 0))
    return pl.pallas_call(
        body, name="gdn_scan", grid=(NCH,),
        in_specs=[tile(), tile(), tile(), tile(), pl.BlockSpec((1, NH, CL, CL), lambda n: (n, 0, 0, 0)),
                  pl.BlockSpec((1, 8, LANES), lambda n: (n, 0, 0)), pl.BlockSpec((CL, GW), lambda n: (n, 5)),
                  _const((1, DH))],
        out_specs=(tile(), tile(), pl.BlockSpec((1, NH, DH, DH), lambda n: (n, 0, 0, 0))),
        out_shape=(jax.ShapeDtypeStruct((S, GW), F32), jax.ShapeDtypeStruct((S, GW), F32),
                   jax.ShapeDtypeStruct((NCH, NH, DH, DH), F32)),
        scratch_shapes=[pltpu.VMEM((NH, DH, DH), F32)],
        compiler_params=_params(dimension_semantics=("arbitrary",)),
    )(w_o, u_o, qg, kd, qk, cd, p_main, gdn_nw)


def _fwd_out(out_a, out_b, x, modnb, bada, w_out):
    def body(oa_ref, ob_ref, x_ref, mod_ref, b_ref, w_ref, x1_ref, mix_ref, oab_ref):
        oa = oa_ref[...].astype(BF16)
        ob = ob_ref[...].astype(BF16)
        oab_ref[:, 0:CW] = oa
        oab_ref[:, CW:D] = ob
        mix = _dot(oa, w_ref[0:CW, :]) + _dot(ob, w_ref[CW:D, :])
        mix_ref[...] = mix
        x1_ref[...] = x_ref[...] + _mod(mod_ref, b_ref, 2) * mix

    tile = lambda w: pl.BlockSpec((TM, w), lambda i: (i, 0))
    return pl.pallas_call(
        body, name="fwd_out", grid=(NT,),
        in_specs=[tile(CW), tile(GW), tile(D), _const((1, 6 * D)), _const((1, 6 * D)), _const((D, D))],
        out_specs=(tile(D), tile(D), tile(D)),
        out_shape=(jax.ShapeDtypeStruct((S, D), F32), jax.ShapeDtypeStruct((S, D), F32),
                   jax.ShapeDtypeStruct((S, D), BF16)),
        compiler_params=_params(dimension_semantics=("arbitrary",)),
    )(out_a, out_b, x, modnb, bada, w_out)


TF = 128
FFN_STATS = 8


def _ffn_fwd_bwd(x1, tgt, modnb, bada, nw2, nfw, w_fi, w_fo):
    def body(x1_ref, tgt_ref, mod_ref, b_ref, nw2_ref, nfw_ref, wi_ref, wo_ref,
             dx1_ref, hb_ref, act_ref, dffn_ref, df_ref, st_ref):
        i = pl.program_id(0)

        @pl.when(i == 0)
        def _():
            st_ref[...] = jnp.zeros((FFN_STATS, D), F32)

        sh2, sc2, gt2 = _mod(mod_ref, b_ref, 3), _mod(mod_ref, b_ref, 4), _mod(mod_ref, b_ref, 5)
        x1v = x1_ref[...]
        r2 = lax.rsqrt(jnp.mean(x1v * x1v, axis=-1, keepdims=True) + EPS)
        xr2 = x1v * r2
        xn2 = xr2 * nw2_ref[...]
        hb = (xn2 * (1.0 + sc2) + sh2).astype(BF16)
        hb_ref[...] = hb
        fg, fu, sg = [], [], []
        ffn = jnp.zeros((TF, D), F32)
        for j in range(4):
            fgj = _dot_nt(hb, wi_ref[j])
            fuj = _dot_nt(hb, wi_ref[j + 4])
            sj = _sig(fgj)
            aj = (fgj * sj * fuj).astype(BF16)
            act_ref[j] = aj
            ffn = ffn + _dot(aj, wo_ref[j])
            fg.append(fgj)
            fu.append(fuj)
            sg.append(sj)
        x2 = x1v + gt2 * ffn
        r3 = lax.rsqrt(jnp.mean(x2 * x2, axis=-1, keepdims=True) + EPS)
        xr3 = x2 * r3
        err = xr3 * nfw_ref[...] - tgt_ref[...]
        loss = 0.5 * jnp.sum(jnp.mean(err * err, axis=-1, keepdims=True), axis=0, keepdims=True)
        dy = err * (1.0 / D)
        st_ref[0:1, :] += _colsum(dy * xr3)
        dyr = dy * nfw_ref[...]
        dx2 = r3 * (dyr - xr3 * jnp.mean(dyr * xr3, axis=-1, keepdims=True))
        st_ref[1:2, :] += _colsum(dx2 * ffn)
        st_ref[5:6, :] += jnp.broadcast_to(loss, (1, D))
        dffn = (gt2 * dx2).astype(BF16)
        dffn_ref[...] = dffn
        dh = jnp.zeros((TF, D), F32)
        for j in range(4):
            dact = _dot_nt(dffn, wo_ref[j])
            dfg = (dact * fu[j] * (sg[j] * (1.0 + fg[j] * (1.0 - sg[j])))).astype(BF16)
            dfu = (dact * (fg[j] * sg[j])).astype(BF16)
            df_ref[j] = dfg
            df_ref[j + 4] = dfu
            dh = dh + _dot(dfg, wi_ref[j]) + _dot(dfu, wi_ref[j + 4])
        st_ref[2:3, :] += _colsum(dh)
        st_ref[3:4, :] += _colsum(dh * xn2)
        dxn = dh * (1.0 + sc2)
        st_ref[4:5, :] += _colsum(dxn * xr2)
        dxr = dxn * nw2_ref[...]
        dx1_ref[...] = dx2 + r2 * (dxr - xr2 * jnp.mean(dxr * xr2, axis=-1, keepdims=True))

    tile = lambda w: pl.BlockSpec((TF, w), lambda i: (i, 0))
    return pl.pallas_call(
        body, name="ffn_fwd_bwd", grid=(S // TF,),
        in_specs=[tile(D), tile(D), _const((1, 6 * D)), _const((1, 6 * D)), _const((1, D)), _const((1, D)),
                  _const1((N_DEV, FB, D)), _const1((4, FB, D))],
        out_specs=(tile(D), tile(D), pl.BlockSpec((4, TF, FB), lambda i: (0, i, 0)), tile(D),
                   pl.BlockSpec((N_DEV, TF, FB), lambda i: (0, i, 0)), _const((FFN_STATS, D))),
        out_shape=(jax.ShapeDtypeStruct((S, D), F32), jax.ShapeDtypeStruct((S, D), BF16),
                   jax.ShapeDtypeStruct((4, S, FB), BF16), jax.ShapeDtypeStruct((S, D), BF16),
                   jax.ShapeDtypeStruct((N_DEV, S, FB), BF16), jax.ShapeDtypeStruct((FFN_STATS, D), F32)),
        compiler_params=_params(44, dimension_semantics=("arbitrary",)),
    )(x1, tgt, modnb, bada, nw2, nfw, w_fi, w_fo)


def _grad_w(name, a, b, nb):
    m, n = a.shape[1], b.shape[1]

    def body(a_ref, b_ref, o_ref):
        o_ref[...] = _dot_tn(a_ref[...], b_ref[...]).astype(BF16)

    return pl.pallas_call(
        body, name=name, grid=(m // nb,),
        in_specs=[pl.BlockSpec((S, nb), lambda j: (0, j)), _const((S, n))],
        out_specs=pl.BlockSpec((nb, n), lambda j: (j, 0)),
        out_shape=jax.ShapeDtypeStruct((m, n), BF16),
        compiler_params=_params(dimension_semantics=("arbitrary",)),
    )(a, b)


def _grad_w_ffn_in(hb2, df):
    def body(a_ref, b_ref, o_ref):
        o_ref[0] = _dot_tn(b_ref[0], a_ref[...]).astype(BF16)

    return pl.pallas_call(
        body, name="grad_w_ffn_in", grid=(N_DEV,),
        in_specs=[_const((S, D)), pl.BlockSpec((1, S, FB), lambda j: (j, 0, 0))],
        out_specs=pl.BlockSpec((1, FB, D), lambda j: (j, 0, 0)),
        out_shape=jax.ShapeDtypeStruct((N_DEV, FB, D), BF16),
        compiler_params=_params(dimension_semantics=("arbitrary",)),
    )(hb2, df)


def _grad_w_ffn_out(act, dffn):
    def body(a_ref, b_ref, o_ref):
        o_ref[0] = _dot_tn(a_ref[0], b_ref[...]).astype(BF16)

    return pl.pallas_call(
        body, name="grad_w_ffn_out", grid=(4,),
        in_specs=[pl.BlockSpec((1, S, FB), lambda j: (j, 0, 0)), _const((S, D))],
        out_specs=pl.BlockSpec((1, FB, D), lambda j: (j, 0, 0)),
        out_shape=jax.ShapeDtypeStruct((4, FB, D), BF16),
        compiler_params=_params(dimension_semantics=("arbitrary",)),
    )(act, dffn)


def _bwd_out(dx1, mix, modnb, bada, w_out):
    def body(dx_ref, mix_ref, mod_ref, b_ref, w_ref, dmix_ref, doa_ref, dob_ref, st_ref):
        i = pl.program_id(0)

        @pl.when(i == 0)
        def _():
            st_ref[...] = jnp.zeros((8, D), F32)

        dx = dx_ref[...]
        st_ref[0:1, :] += _colsum(dx * mix_ref[...])
        dmix = (_mod(mod_ref, b_ref, 2) * dx).astype(BF16)
        dmix_ref[...] = dmix
        doa_ref[...] = _dot_nt(dmix, w_ref[0:CW, :])
        dob_ref[...] = _dot_nt(dmix, w_ref[CW:D, :])

    tile = lambda w: pl.BlockSpec((TM, w), lambda i: (i, 0))
    return pl.pallas_call(
        body, name="bwd_out", grid=(NT,),
        in_specs=[tile(D), tile(D), _const((1, 6 * D)), _const((1, 6 * D)), _const((D, D))],
        out_specs=(tile(D), tile(CW), tile(GW), _const((8, D))),
        out_shape=(jax.ShapeDtypeStruct((S, D), BF16), jax.ShapeDtypeStruct((S, CW), F32),
                   jax.ShapeDtypeStruct((S, GW), F32), jax.ShapeDtypeStruct((8, D), F32)),
        compiler_params=_params(dimension_semantics=("arbitrary",)),
    )(dx1, mix, modnb, bada, w_out)


CONF_STATS = 40


def _conf_bwd(d_out_a, y, p_main, conv_w, gn_w, gn_b):
    def body(do_ref, y_ref, a_ref, g_ref, ah_ref, gh_ref, w_ref, gw_ref, gb_ref, dp_ref, st_ref, ubuf, dybuf):
        i = pl.program_id(0)

        @pl.when(i == 0)
        def _():
            st_ref[...] = jnp.zeros((CONF_STATS, CW), F32)
            dybuf[TM:TM + HALO, :] = jnp.zeros((HALO, CW), F32)

        pm = _group_mean_matrix()
        yv = y_ref[...]
        dlt = yv - _dot(yv, pm, HI)
        rstd = lax.rsqrt(_dot(dlt * dlt, pm, HI) + EPS)
        un = dlt * rstd
        o = un * gw_ref[...] + gb_ref[...]
        so = _sig(o)
        d_o = do_ref[...] * (so * (1.0 + o * (1.0 - so)))
        st_ref[33:34, :] += _colsum(d_o)
        st_ref[32:33, :] += _colsum(d_o * un)
        dun = d_o * gw_ref[...]
        dy = rstd * (dun - _dot(dun, pm, HI) - un * _dot(dun * un, pm, HI))
        st_ref[31:32, :] += _colsum(dy)
        dybuf[0:TM, :] = dy

        a = a_ref[...]
        sg = _sig(g_ref[...])
        first = i == NT - 1
        ubuf[0:HALO, :] = jnp.where(first, 0.0, ah_ref[...] * _sig(gh_ref[...]))
        ubuf[HALO:HALO + TM, :] = a * sg
        du = jnp.zeros((TM, CW), F32)
        for k in range(KC):
            off = HALO - (KC - 1) + k
            st_ref[k:k + 1, :] += _colsum(dy * ubuf[off:off + TM, :])
            du = du + w_ref[k:k + 1, :] * dybuf[KC - 1 - k:KC - 1 - k + TM, :]
        dybuf[TM:TM + HALO, :] = dybuf[0:HALO, :]
        dp_ref[:, 0:CW] = (du * sg).astype(BF16)
        dp_ref[:, CW:2 * CW] = (du * a * sg * (1.0 - sg)).astype(BF16)

    rev = lambda w, j=0: pl.BlockSpec((TM, w), lambda i: (NT - 1 - i, j))
    halo = lambda j: pl.BlockSpec((HALO, CW), lambda i: (jnp.maximum((NT - 1 - i) * (TM // HALO) - 1, 0), j))
    return pl.pallas_call(
        body, name="conf_bwd", grid=(NT,),
        in_specs=[rev(CW), rev(CW), rev(CW, 0), rev(CW, 1), halo(0), halo(1),
                  _const((KC, CW)), _const((1, CW)), _const((1, CW))],
        out_specs=(rev(2 * CW), _const((CONF_STATS, CW))),
        out_shape=(jax.ShapeDtypeStruct((S, 2 * CW), BF16), jax.ShapeDtypeStruct((CONF_STATS, CW), F32)),
        scratch_shapes=[pltpu.VMEM((HALO + TM, CW), F32), pltpu.VMEM((TM + HALO, CW), F32)],
        compiler_params=_params(dimension_semantics=("arbitrary",)),
    )(d_out_a, y, p_main, p_main, p_main, p_main, conv_w, gn_w, gn_b)


GDN_STATS = 8


def _gdn_bwd(d_out_b, o_pre, s_in, t_inv, p_main, p_ba, gdn_conv_w, alog_l, dt_l, gdn_nw):
    def body(dob_ref, o_ref, sin_ref, t_ref, q_ref, k_ref, v_ref, z_ref, qh_ref, kh_ref, vh_ref, ba_ref,
             w_ref, al_ref, dt_ref, nw_ref, dp_ref, dba_ref, st_ref, xbuf, dcbuf, dstate):
        n = pl.program_id(0)

        @pl.when(n == 0)
        def _():
            st_ref[...] = jnp.zeros((GDN_STATS, 3 * GW), F32)
            dcbuf[CL:CL + SH, :] = jnp.zeros((SH, 3 * GW), F32)
            dstate[...] = jnp.zeros((NH, DH, DH), F32)

        first = n == NCH - 1
        xbuf[0:SH, 0:GW] = jnp.where(first, 0.0, qh_ref[...])
        xbuf[0:SH, GW:2 * GW] = jnp.where(first, 0.0, kh_ref[...])
        xbuf[0:SH, 2 * GW:3 * GW] = jnp.where(first, 0.0, vh_ref[...])
        xbuf[SH:SH + CL, 0:GW] = q_ref[...]
        xbuf[SH:SH + CL, GW:2 * GW] = k_ref[...]
        xbuf[SH:SH + CL, 2 * GW:3 * GW] = v_ref[...]
        conv = _short_conv(w_ref, xbuf)
        sc = _sig(conv)
        qkv = conv * sc
        ba = ba_ref[...]
        beta_all, g_all, xg, neg_a = _gdn_gates(ba, al_ref[...], dt_ref[...])
        gcum, gcum_t = _gdn_cumsum(g_all)
        lane = lax.broadcasted_iota(jnp.int32, (CL, LANES), 1)
        row = lax.broadcasted_iota(jnp.int32, (CL, 1), 0)
        acc = dict(dgcum=jnp.zeros((CL, LANES), F32), dbeta=jnp.zeros((CL, LANES), F32))

        def head(h):
            lo = h * DH
            qh = qkv[:, lo:lo + DH]
            kh = qkv[:, GW + lo:GW + lo + DH]
            vh = qkv[:, 2 * GW + lo:2 * GW + lo + DH]
            beta = beta_all[:, h:h + 1]
            f = _head_terms(qh, kh, beta, gcum[:, NH + h:NH + h + 1], gcum_t[NH + h:NH + h + 1, :])
            qn, kn, qs, kb, gam, kds, cd, decay = (f[s] for s in ("qn", "kn", "qs", "kb", "gam", "kds", "cd", "decay"))
            t = t_ref[0, h]
            st = sin_ref[0, h]
            vb = vh * beta
            kbg = kb * gam
            u = _dot(t, vb, GP)
            w = _dot(t, kbg, GP)
            yield
            v_new = u - _dot(w, st, GP)
            q_dec = qs * gam
            k_dec = kn * kds

            o = o_ref[:, lo:lo + DH]
            zh = z_ref[:, lo:lo + DH]
            sz = _sig(zh)
            r = lax.rsqrt(jnp.mean(o * o, axis=-1, keepdims=True) + EPS)
            orr = o * r
            d_out = dob_ref[:, lo:lo + DH]
            dz = d_out * (orr * nw_ref[...]) * (sz * (1.0 + zh * (1.0 - sz)))
            don = d_out * (zh * sz)
            st_ref[4:5, 0:DH] += _colsum(don * orr)
            tt = don * nw_ref[...]
            d_o = r * (tt - orr * jnp.mean(tt * orr, axis=-1, keepdims=True))

            yield
            ds_out = dstate[h]
            dv_new = _dot_tn(f["qk"], d_o, GP) + _dot(k_dec, ds_out, GP)
            dqk = jnp.where(f["causal"], _dot_nt(d_o, v_new, GP), 0.0)
            dq_dec = _dot_nt(d_o, st, GP)
            dk_dec = _dot_nt(v_new, ds_out, GP)
            yield
            dstate[h] = _dot_tn(q_dec, d_o, GP) + cd * ds_out - _dot_tn(w, dv_new, GP)
            dcd = jnp.sum(_rowsum(st * ds_out), axis=0, keepdims=True)
            dw = -_dot_nt(dv_new, st, GP)
            dvb = _dot_tn(t, dv_new, GP)
            yield
            dt_m = _dot_nt(dv_new, vb, GP) + _dot_nt(dw, kbg, GP)
            dkbg = _dot_tn(t, dw, GP)
            yield
            dtt = _dot_nt(dt_m, t, GP)
            yield
            da = jnp.where(f["strict"], -_dot_tn(t, dtt, GP), 0.0)
            yield
            dad = da * decay
            dqkd = dqk * decay
            dkb = _dot(dad, kn, GP) + dkbg * gam
            dkn = _dot_tn(dad, kb, GP) + _dot_tn(dqkd, qs, GP) + dk_dec * kds + dkb * beta
            dqs = _dot(dqkd, kn, GP) + dq_dec * gam
            yield
            m = da * f["a"] + dqk * f["qk"]
            tk = _rowsum(dk_dec * k_dec)
            dgl = jnp.sum(tk, axis=0, keepdims=True) + dcd * cd
            dgc = (_rowsum(m) - _rowsum(jnp.transpose(m)) + _rowsum(dq_dec * q_dec) - tk + _rowsum(dkbg * kbg)
                   + jnp.where(row == CL - 1, dgl, 0.0))
            dbeta = _rowsum(dkb * kn) + _rowsum(dvb * vh)
            acc["dgcum"] = acc["dgcum"] + jnp.where(lane == NH + h, dgc, 0.0)
            acc["dbeta"] = acc["dbeta"] + jnp.where(lane == h, dbeta, 0.0)
            dvh = dvb * beta
            dqn = dqs * QSCALE
            dqh = f["rq"] * (dqn - qn * _rowsum(dqn * qn))
            dkh = f["rk"] * (dkn - kn * _rowsum(dkn * kn))
            dsilu = lambda c0: sc[:, c0:c0 + DH] * (1.0 + conv[:, c0:c0 + DH] * (1.0 - sc[:, c0:c0 + DH]))
            dcbuf[0:CL, lo:lo + DH] = dqh * dsilu(lo)
            dcbuf[0:CL, GW + lo:GW + lo + DH] = dkh * dsilu(GW + lo)
            dcbuf[0:CL, 2 * GW + lo:2 * GW + lo + DH] = dvh * dsilu(2 * GW + lo)
            dp_ref[:, 3 * GW + lo:3 * GW + lo + DH] = dz.astype(BF16)

        _lockstep(head(h) for h in range(NH))
        dgcum_all, dbeta_all = acc["dgcum"], acc["dbeta"]

        ii, jj = _tri_iota()
        upper = jnp.where(ii <= jj, 1.0, 0.0).astype(F32)
        dg_all = _dot(upper, dgcum_all, HI)
        dxg = dg_all * neg_a * _sig(xg)
        st_ref[5:6, 0:LANES] += _colsum(dg_all * g_all)
        st_ref[6:7, 0:LANES] += _colsum(dxg)
        dbl = dbeta_all * beta_all * (1.0 - beta_all)
        dba_ref[...] = jnp.where(lane < NH, dbl, jnp.where(lane < 2 * NH, dxg, 0.0)).astype(BF16)

        dconv = dcbuf[0:CL, :]
        dx = w_ref[0:1, :] * dcbuf[KS - 1:KS - 1 + CL, :]
        st_ref[0:1, :] += _colsum(dconv * xbuf[SH - KS + 1:SH - KS + 1 + CL, :])
        for k in range(1, KS):
            off = SH - (KS - 1) + k
            st_ref[k:k + 1, :] += _colsum(dconv * xbuf[off:off + CL, :])
            dx = dx + w_ref[k:k + 1, :] * dcbuf[KS - 1 - k:KS - 1 - k + CL, :]
        dcbuf[CL:CL + SH, :] = dcbuf[0:SH, :]
        dp_ref[:, 0:3 * GW] = dx.astype(BF16)

    rev = lambda w, j=0: pl.BlockSpec((CL, w), lambda n: (NCH - 1 - n, j))
    halo = lambda j: pl.BlockSpec((SH, GW), lambda n: (jnp.maximum((NCH - 1 - n) * (CL // SH) - 1, 0), j))
    blk4 = lambda a, b: pl.BlockSpec((1, NH, a, b), lambda n: (NCH - 1 - n, 0, 0, 0))
    return pl.pallas_call(
        body, name="gdn_bwd", grid=(NCH,),
        in_specs=[rev(GW), rev(GW), blk4(DH, DH), blk4(CL, CL), rev(GW, 2), rev(GW, 3), rev(GW, 4), rev(GW, 5),
                  halo(2), halo(3), halo(4), rev(LANES), _const((KS, 3 * GW)), _const((1, LANES)),
                  _const((1, LANES)), _const((1, DH))],
        out_specs=(rev(4 * GW), rev(LANES), _const((GDN_STATS, 3 * GW))),
        out_shape=(jax.ShapeDtypeStruct((S, 4 * GW), BF16), jax.ShapeDtypeStruct((S, LANES), BF16),
                   jax.ShapeDtypeStruct((GDN_STATS, 3 * GW), F32)),
        scratch_shapes=[pltpu.VMEM((SH + CL, 3 * GW), F32), pltpu.VMEM((CL + SH, 3 * GW), F32),
                        pltpu.VMEM((NH, DH, DH), F32)],
        compiler_params=_params(dimension_semantics=("arbitrary",)),
    )(d_out_b, o_pre, s_in, t_inv, p_main, p_main, p_main, p_main, p_main, p_main, p_main, p_ba,
      gdn_conv_w, alog_l, dt_l, gdn_nw)


def _bwd_in(dp_conf, dp_gdn, dp_ba, x, dx1, nw1, modnb, bada, w_main, w_ba):
    def body(dc_ref, dg_ref, db_ref, x_ref, dx1_ref, nw_ref, mod_ref, b_ref, wm_ref, wb_ref, gx_ref, st_ref):
        i = pl.program_id(0)

        @pl.when(i == 0)
        def _():
            st_ref[...] = jnp.zeros((8, D), F32)

        dh = (_dot(dc_ref[...], wm_ref[0:2 * CW, :]) + _dot(dg_ref[...], wm_ref[2 * CW:NMAIN, :])
              + _dot(db_ref[...], wb_ref[...]))
        xv = x_ref[...]
        r = lax.rsqrt(jnp.mean(xv * xv, axis=-1, keepdims=True) + EPS)
        xr = xv * r
        st_ref[0:1, :] += _colsum(dh)
        st_ref[1:2, :] += _colsum(dh * (xr * nw_ref[...]))
        dxn = dh * (1.0 + _mod(mod_ref, b_ref, 1))
        st_ref[2:3, :] += _colsum(dxn * xr)
        dxr = dxn * nw_ref[...]
        gx_ref[...] = dx1_ref[...] + r * (dxr - xr * jnp.mean(dxr * xr, axis=-1, keepdims=True))

    tile = lambda w: pl.BlockSpec((TM, w), lambda i: (i, 0))
    return pl.pallas_call(
        body, name="bwd_in", grid=(NT,),
        in_specs=[tile(2 * CW), tile(4 * GW), tile(LANES), tile(D), tile(D), _const((1, D)), _const((1, 6 * D)),
                  _const((1, 6 * D)), _const((NMAIN, D)), _const((LANES, D))],
        out_specs=(tile(D), _const((8, D))),
        out_shape=(jax.ShapeDtypeStruct((S, D), F32), jax.ShapeDtypeStruct((8, D), F32)),
        compiler_params=_params(dimension_semantics=("arbitrary",)),
    )(dp_conf, dp_gdn, dp_ba, x, dx1, nw1, modnb, bada, w_main, w_ba)


def _adamw(w, g, m, v):
    m = ADAM_B1 * m + (1.0 - ADAM_B1) * g
    v = ADAM_B2 * v + (1.0 - ADAM_B2) * (g * g)
    m_hat = m / BC1
    v_hat = v / BC2
    delta = -ADAM_LR * (m_hat / (jnp.sqrt(v_hat) + ADAM_EPS) + ADAM_WD * w)
    return delta, m, v


ADAM_BLOCK_BYTES = 6 * 1024 * 1024


def _adam_tile(rows, cols):
    padded = -(-cols // LANES) * LANES
    if N_DEV * rows * padded * 4 <= ADAM_BLOCK_BYTES:
        return rows, cols
    best = None
    for tr in range(16, rows, 16):
        if rows % tr == 0 and N_DEV * tr * padded * 4 <= ADAM_BLOCK_BYTES:
            best = tr
    if best is not None:
        return best, cols
    rows_padded = -(-rows // 16) * 16
    tc = LANES
    for cand in range(LANES, cols, LANES):
        if cols % cand == 0 and N_DEV * rows_padded * cand * 4 <= ADAM_BLOCK_BYTES:
            tc = cand
    return rows, tc


def _reduce_adam(name, parts, w, m, v, own=None):
    rows, cols = w.shape
    tr, tc = _adam_tile(rows, cols)

    def body(*refs):
        p_ref, w_ref, m_ref, v_ref = refs[:4]
        g_ref, d_ref, nm_ref, nv_ref = refs[-4:]
        if own is None:
            part = lambda j: p_ref[j].astype(F32)
        else:
            me = 4 * lax.axis_index("x") + 2 * lax.axis_index("y") + lax.axis_index("c")
            part = lambda j: jnp.where(me == j, refs[4][...], p_ref[j]).astype(F32)
        g = part(0)
        for j in range(1, N_DEV):
            g = g + part(j)
        g_ref[...] = g
        d_ref[...], nm_ref[...], nv_ref[...] = _adamw(w_ref[...], g, m_ref[...], v_ref[...])

    blk = pl.BlockSpec((tr, tc), lambda i, j: (i, j))
    sds = jax.ShapeDtypeStruct((rows, cols), F32)
    extra = [] if own is None else [own]
    return pl.pallas_call(
        body, name=name, grid=(rows // tr, cols // tc),
        in_specs=[pl.BlockSpec((N_DEV, tr, tc), lambda i, j: (0, i, j)), blk, blk, blk] + [blk] * len(extra),
        out_specs=(blk, blk, blk, blk), out_shape=(sds, sds, sds, sds),
        compiler_params=_params(dimension_semantics=("arbitrary", "arbitrary")),
    )(parts, w, m, v, *extra)


def _ada_adam(c_all, dmod_sh, w, m, v):
    rows, cols = w.shape
    tr = 256

    def body(c_ref, dm_ref, w_ref, m_ref, v_ref, g_ref, d_ref, nm_ref, nv_ref):
        cv = c_ref[...]
        g = _dot_tn(cv * _sig(cv), dm_ref[...], HI)
        g_ref[...] = g
        d_ref[...], nm_ref[...], nv_ref[...] = _adamw(w_ref[...], g, m_ref[...], v_ref[...])

    blk = pl.BlockSpec((tr, cols), lambda i: (i, 0))
    sds = jax.ShapeDtypeStruct((rows, cols), F32)
    return pl.pallas_call(
        body, name="ada_adam", grid=(rows // tr,),
        in_specs=[pl.BlockSpec((N_DEV, tr), lambda i: (0, i)), _const((N_DEV, cols)), blk, blk, blk],
        out_specs=(blk, blk, blk, blk), out_shape=(sds, sds, sds, sds),
        compiler_params=_params(dimension_semantics=("arbitrary",)),
    )(c_all, dmod_sh, w, m, v)


def _lanes(a, at=0):
    return jnp.pad(a, ((0, 0), (at, LANES - at - a.shape[1])))


WEIGHT_NAMES = ["w_ada", "b_ada", "norm_mix_w", "w_in", "conv_w", "conv_b", "conv_gn_w", "conv_gn_b", "gdn_conv_w",
                "gdn_a_log", "gdn_dt_bias", "gdn_norm_w", "w_out", "norm_ffn_w", "w_ffn_in", "w_ffn_out",
                "norm_final_w"]


def _slab(b_ada, norm_mix_w, norm_ffn_w, norm_final_w, conv_b, conv_gn_w, conv_gn_b, gdn_norm_w, a_log, dt_bias):
    return jnp.concatenate([
        b_ada.reshape(48, LANES), norm_mix_w.reshape(8, LANES), norm_ffn_w.reshape(8, LANES),
        norm_final_w.reshape(8, LANES), conv_b.reshape(4, LANES), conv_gn_w.reshape(4, LANES),
        conv_gn_b.reshape(4, LANES), gdn_norm_w.reshape(1, LANES), _lanes(a_log), _lanes(dt_bias),
        jnp.zeros((1, LANES), F32)], axis=0)


def _unslab(t):
    return dict(b_ada=t[0:48].reshape(1, 6 * D), norm_mix_w=t[48:56].reshape(1, D),
                norm_ffn_w=t[56:64].reshape(1, D), norm_final_w=t[64:72].reshape(D),
                conv_b=t[72:76].reshape(1, CW), conv_gn_w=t[76:80].reshape(1, CW),
                conv_gn_b=t[80:84].reshape(1, CW), gdn_norm_w=t[84:85], gdn_a_log=t[85:86, 0:NH],
                gdn_dt_bias=t[86:87, 0:NH])


def _mix_forward(w, xs, modnb):
    w_main = w["w_in"][:NMAIN]
    w_ba = jnp.pad(w["w_in"][NMAIN:], ((0, LANES - 2 * NH), (0, 0)))
    alog_l = _lanes(w["gdn_a_log"], NH)
    dt_l = _lanes(w["gdn_dt_bias"], NH)
    p_main, p_ba, hb1 = _fwd_in(xs, w["norm_mix_w"], modnb, w["b_ada"], w_main, w_ba)
    y_conv, out_a = _conf_fwd(p_main, w["conv_w"], w["conv_b"], w["conv_gn_w"], w["conv_gn_b"])
    w_o, u_o, qg, kd, qk, cd, t_inv = _gdn_prep(p_main, p_ba, w["gdn_conv_w"], alog_l, dt_l)
    out_b, o_pre, s_in = _gdn_scan(w_o, u_o, qg, kd, qk, cd, p_main, w["gdn_norm_w"])
    return dict(w_main=w_main, w_ba=w_ba, alog_l=alog_l, dt_l=dt_l, p_main=p_main, p_ba=p_ba, hb1=hb1,
                y_conv=y_conv, out_a=out_a, out_b=out_b, o_pre=o_pre, s_in=s_in, t_inv=t_inv)


def _ffn_stage(w, f, xs, tgt, modnb):
    x1, mix, oab = _fwd_out(f["out_a"], f["out_b"], xs, modnb, w["b_ada"], w["w_out"])
    dx1, hb2, act, dffn, df, st_ffn = _ffn_fwd_bwd(x1, tgt, modnb, w["b_ada"], w["norm_ffn_w"], w["norm_final_w"],
                                                   w["w_ffn_in"], w["w_ffn_out"])
    gw_ffn_in = _grad_w_ffn_in(hb2, df)
    gw_ffn_out = _grad_w_ffn_out(act, dffn)
    return dict(mix=mix, oab=oab, dx1=dx1, st_ffn=st_ffn, gw_ffn_in=gw_ffn_in, gw_ffn_out=gw_ffn_out)


def _mix_backward(w, f, g, xs, modnb):
    dmix, d_out_a, d_out_b, st_out = _bwd_out(g["dx1"], g["mix"], modnb, w["b_ada"], w["w_out"])
    gw_out = _grad_w("grad_w_out", g["oab"], dmix, 512)
    dp_conf, st_conf = _conf_bwd(d_out_a, f["y_conv"], f["p_main"], w["conv_w"], w["conv_gn_w"], w["conv_gn_b"])
    dp_gdn, dp_ba, st_gdn = _gdn_bwd(d_out_b, f["o_pre"], f["s_in"], f["t_inv"], f["p_main"], f["p_ba"],
                                     w["gdn_conv_w"], f["alog_l"], f["dt_l"], w["gdn_norm_w"])
    grad_x, st_in = _bwd_in(dp_conf, dp_gdn, dp_ba, xs, g["dx1"], w["norm_mix_w"], modnb, w["b_ada"], f["w_main"],
                            f["w_ba"])
    hb1 = f["hb1"]
    gw_in = jnp.concatenate(
        [_grad_w("grad_w_in_conf", dp_conf, hb1, 512), _grad_w("grad_w_in_gdn", dp_gdn, hb1, 512),
         _grad_w("grad_w_in_ba", dp_ba, hb1, LANES)[:2 * NH]], axis=0)
    st_ffn = g["st_ffn"]
    dmod = jnp.concatenate([st_in[0:1], st_in[1:2], st_out[0:1], st_ffn[2:3], st_ffn[3:4], st_ffn[1:2]], axis=1)
    small = jnp.concatenate([
        dmod.reshape(48, LANES), st_in[2:3].reshape(8, LANES), st_ffn[4:5].reshape(8, LANES),
        st_ffn[0:1].reshape(8, LANES), st_conf[31:32].reshape(4, LANES), st_conf[32:33].reshape(4, LANES),
        st_conf[33:34].reshape(4, LANES), st_gdn[4:5, 0:LANES],
        _lanes(st_gdn[5:6, NH:2 * NH]), _lanes(st_gdn[6:7, NH:2 * NH]), st_ffn[5:6, 0:LANES]], axis=0)
    return dict(grad_x=grad_x, gw_in=gw_in, gw_out=gw_out, gw_conv=st_conf[0:KC], gw_gconv=st_gdn[0:KS],
                small=small)


def _local(w, xs, tgt, modnb):
    f = _mix_forward(w, xs, modnb)
    g = _ffn_stage(w, f, xs, tgt, modnb)
    b = _mix_backward(w, f, g, xs, modnb)
    return dict(b, gw_ffn_in=g["gw_ffn_in"], gw_ffn_out=g["gw_ffn_out"])


def kernel(x, c, w_ada, b_ada, norm_mix_w, w_in, conv_w, conv_b, conv_gn_w, conv_gn_b, gdn_conv_w, gdn_a_log, gdn_dt_bias, gdn_norm_w, w_out, norm_ffn_w, w_ffn_in, w_ffn_out, norm_final_w, loss_target, m_w_ada, m_b_ada, m_norm_mix_w, m_w_in, m_conv_w, m_conv_b, m_conv_gn_w, m_conv_gn_b, m_gdn_conv_w, m_gdn_a_log, m_gdn_dt_bias, m_gdn_norm_w, m_w_out, m_norm_ffn_w, m_w_ffn_in, m_w_ffn_out, m_norm_final_w, v_w_ada, v_b_ada, v_norm_mix_w, v_w_in, v_conv_w, v_conv_b, v_conv_gn_w, v_conv_gn_b, v_gdn_conv_w, v_gdn_a_log, v_gdn_dt_bias, v_gdn_norm_w, v_w_out, v_norm_ffn_w, v_w_ffn_in, v_w_ffn_out, v_norm_final_w):
    me = 4 * lax.axis_index("x") + 2 * lax.axis_index("y") + lax.axis_index("c")
    xs = x.reshape(S, D)
    tgt = loss_target.reshape(S, D)

    g_c, g_cw, g_gcw = _exchange("gather_cond", [c, conv_w[0], gdn_conv_w[0]], [False] * 3)
    c_all = g_c.reshape(N_DEV, D)
    (g_mod,) = _exchange("gather_mod", [_mod_shard(c_all, w_ada[0])], [False])
    modnb = lax.dynamic_index_in_dim(g_mod, me, axis=1, keepdims=False).reshape(1, 6 * D)

    w_in_sh, modnb = lax.optimization_barrier((jnp.transpose(w_in[0]).astype(BF16), modnb))
    late = [w_out[0].astype(BF16), jnp.transpose(w_ffn_in[0]).astype(BF16), w_ffn_out[0].astype(BF16)]
    g_win, *late_lands = _exchange("gather_weights", [w_in_sh] + late, [False] * 4, seed_only=(1, 2, 3))
    late_started = _exchange_start("gather_late_start", late, late_lands, [False] * 3)
    g_win, _ = lax.optimization_barrier((g_win, late_started[-1]))
    w = dict(b_ada=b_ada, norm_mix_w=norm_mix_w, conv_b=conv_b, conv_gn_w=conv_gn_w, conv_gn_b=conv_gn_b,
             gdn_a_log=gdn_a_log, gdn_dt_bias=gdn_dt_bias, gdn_norm_w=gdn_norm_w, norm_ffn_w=norm_ffn_w,
             norm_final_w=norm_final_w.reshape(1, D),
             conv_w=jnp.transpose(g_cw, (1, 0, 2)).reshape(KC, CW),
             gdn_conv_w=jnp.transpose(g_gcw, (1, 0, 2)).reshape(KS, 3 * GW),
             w_in=g_win.reshape(NIN, D))

    f = _mix_forward(w, xs, modnb)
    _, (g_wout, g_wfi, g_wfo) = _exchange_wait("gather_late_wait", late_started, [False] * 3, f["out_b"])
    w.update(w_out=g_wout.reshape(D, D), w_ffn_in=g_wfi, w_ffn_out=g_wfo.reshape(4, FB, D))
    g = _ffn_stage(w, f, xs, tgt, modnb)

    ffn_grads = [g["gw_ffn_in"], g["gw_ffn_out"].reshape(N_DEV, DFF // N_DEV, D)]
    ffn_started = _exchange_start("scatter_ffn_start", ffn_grads,
                                  [lax.empty(a.shape, a.dtype) for a in ffn_grads], [True] * 2)
    modnb_after, _ = lax.optimization_barrier((modnb, ffn_started[-1]))
    loc = _mix_backward(w, f, g, xs, modnb_after)

    gw_in = loc["gw_in"].reshape(N_DEV, NIN // N_DEV, D)
    gw_conv = jnp.transpose(loc["gw_conv"].reshape(KC, N_DEV, CW // N_DEV), (1, 0, 2))
    gw_gconv = jnp.transpose(loc["gw_gconv"].reshape(KS, N_DEV, 3 * GW // N_DEV), (1, 0, 2))
    r_in, r_out, r_cw, r_gcw = _exchange(
        "scatter_grads", [gw_in, loc["gw_out"].reshape(N_DEV, D // N_DEV, D), gw_conv, gw_gconv], [True] * 4)
    (sent_fi, sent_fo), (r_fi, r_fo) = _exchange_wait("scatter_ffn_wait", ffn_started, [True] * 2, r_cw)
    own_fi = lax.dynamic_index_in_dim(sent_fi, me, axis=0, keepdims=False)
    own_fo = lax.dynamic_index_in_dim(sent_fo, me, axis=0, keepdims=False)

    (g_small,) = _exchange("gather_small", [loc["small"]], [False])
    sw = _slab(b_ada, norm_mix_w, norm_ffn_w, norm_final_w, conv_b, conv_gn_w, conv_gn_b, gdn_norm_w, gdn_a_log,
               gdn_dt_bias)
    sm = _slab(m_b_ada, m_norm_mix_w, m_norm_ffn_w, m_norm_final_w, m_conv_b, m_conv_gn_w, m_conv_gn_b,
               m_gdn_norm_w, m_gdn_a_log, m_gdn_dt_bias)
    sv = _slab(v_b_ada, v_norm_mix_w, v_norm_ffn_w, v_norm_final_w, v_conv_b, v_conv_gn_w, v_conv_gn_b,
               v_gdn_norm_w, v_gdn_a_log, v_gdn_dt_bias)
    small_out = _reduce_adam("adam_small", g_small, sw, sm, sv)
    loss = small_out[0][SMALL_ROWS - 1, 0]
    res = [_unslab(t) for t in small_out]

    dmod_rows = g_small[:, 0:48, :].reshape(N_DEV, 6 * D)
    dmod_sh = lax.dynamic_slice_in_dim(dmod_rows, me * (6 * D // N_DEV), 6 * D // N_DEV, axis=1)

    big = dict(
        w_ada=_ada_adam(c_all, dmod_sh, w_ada[0], m_w_ada[0], v_w_ada[0]),
        w_in=[jnp.transpose(t) for t in _reduce_adam(
            "adam_w_in", r_in, jnp.transpose(w_in[0]), jnp.transpose(m_w_in[0]), jnp.transpose(v_w_in[0]))],
        conv_w=_reduce_adam("adam_conv_w", r_cw, conv_w[0], m_conv_w[0], v_conv_w[0]),
        gdn_conv_w=_reduce_adam("adam_gdn_conv_w", r_gcw, gdn_conv_w[0], m_gdn_conv_w[0], v_gdn_conv_w[0]),
        w_out=_reduce_adam("adam_w_out", r_out, w_out[0], m_w_out[0], v_w_out[0]),
        w_ffn_in=[jnp.transpose(t) for t in _reduce_adam(
            "adam_w_ffn_in", r_fi, jnp.transpose(w_ffn_in[0]), jnp.transpose(m_w_ffn_in[0]),
            jnp.transpose(v_w_ffn_in[0]), own_fi)],
        w_ffn_out=_reduce_adam("adam_w_ffn_out", r_fo, w_ffn_out[0], m_w_ffn_out[0], v_w_ffn_out[0], own_fo),
    )
    outs = [loss, loc["grad_x"].reshape(1, S, D)]
    for kind in range(4):
        for nm in WEIGHT_NAMES:
            outs.append(big[nm][kind][None] if nm in big else res[kind][nm])
    return tuple(outs)
```

```python
import functools

import jax
import jax.numpy as jnp
from jax import lax
from jax.experimental import pallas as pl
from jax.experimental.pallas import tpu as pltpu

F32 = jnp.float32
BF16 = jnp.bfloat16
HI = lax.Precision.HIGHEST
MESH = pl.DeviceIdType.MESH

N_DEV = 8
S = 2048
D = 1024
TM = 256
NT = S // TM
CW = 512
KC = 31
NG = 8
GSZ = CW // NG
HALO = 32
GW = 512
NH = 4
DH = 128
KS = 4
SH = 8
CL = 64
NCH = S // CL
NMAIN = 2 * CW + 4 * GW
NIN = NMAIN + 2 * NH
DFF = 2816
FB = DFF // 4
EPS = 1e-6
QSCALE = DH ** -0.5
LANES = 128
SMALL_ROWS = 88

ADAM_LR = 0.001
ADAM_B1 = 0.9
ADAM_B2 = 0.999
ADAM_EPS = 1e-08
ADAM_WD = 0.01
ADAM_STEP = 10
BC1 = 1.0 - ADAM_B1 ** ADAM_STEP
BC2 = 1.0 - ADAM_B2 ** ADAM_STEP

MIB = 1024 * 1024
VMEM_LIMIT_MIB = 32


def _params(limit_mib=VMEM_LIMIT_MIB, **kw):
    return pltpu.CompilerParams(vmem_limit_bytes=limit_mib * MIB, **kw)


def _sig(x):
    return jax.nn.sigmoid(x)


GP = BF16


def _operands(a, b, prec):
    if prec is BF16:
        return a.astype(BF16), b.astype(BF16), None
    return a, b, prec


def _dot(a, b, prec=None):
    a, b, prec = _operands(a, b, prec)
    return jnp.dot(a, b, preferred_element_type=F32, precision=prec)


def _dot_nt(a, b, prec=None):
    a, b, prec = _operands(a, b, prec)
    return lax.dot_general(a, b, (((1,), (1,)), ((), ())), preferred_element_type=F32, precision=prec)


def _dot_tn(a, b, prec=None):
    a, b, prec = _operands(a, b, prec)
    return lax.dot_general(a, b, (((0,), (0,)), ((), ())), preferred_element_type=F32, precision=prec)


def _lockstep(gens):
    gens = list(gens)
    while gens:
        alive = []
        for g in gens:
            try:
                next(g)
                alive.append(g)
            except StopIteration:
                pass
        gens = alive


def _rowsum(x):
    return jnp.sum(x, axis=-1, keepdims=True)


def _colsum(x):
    return jnp.sum(x, axis=0, keepdims=True)


def _mod(mod_ref, b_ref, k):
    return mod_ref[:, k * D:(k + 1) * D] + b_ref[:, k * D:(k + 1) * D]


def _const(shape):
    nd = len(shape)
    return pl.BlockSpec(shape, lambda *_: (0,) * nd)


def _const1(shape):
    nd = len(shape)
    return pl.BlockSpec(shape, lambda *_: (0,) * nd, pipeline_mode=pl.Buffered(1))


PEER_FLIPS = [(dx, dy, dc) for dx in (0, 1) for dy in (0, 1) for dc in (0, 1)][1:]


def _after(x, token):
    return x + token[0:1, 0:1].astype(x.dtype).reshape((1,) * x.ndim)


def _exchange(name, srcs, per_dest, seed_only=(), with_token=False):
    n = len(srcs)
    out_shape = []
    for a, pd in zip(srcs, per_dest):
        blk = a.shape[1:] if pd else a.shape
        out_shape.append(jax.ShapeDtypeStruct((N_DEV,) + tuple(blk), a.dtype))

    def body(*refs):
        src = refs[:n]
        dst = refs[n:2 * n]
        send_sems, recv_sems, local_sems = refs[-3:]
        if with_token:
            refs[2 * n][...] = jnp.zeros((8, LANES), F32)
        x, y, c = lax.axis_index("x"), lax.axis_index("y"), lax.axis_index("c")
        me = 4 * x + 2 * y + c

        def piece(i, j):
            return src[i].at[j] if per_dest[i] else src[i]

        copies = []
        for k, (dx, dy, dc) in enumerate(PEER_FLIPS):
            px = 1 - x if dx else x
            py = 1 - y if dy else y
            pc = 1 - c if dc else c
            pj = 4 * px + 2 * py + pc
            for i in range(n):
                if i in seed_only:
                    continue
                cp = pltpu.make_async_remote_copy(
                    src_ref=piece(i, pj), dst_ref=dst[i].at[me],
                    send_sem=send_sems.at[k * n + i], recv_sem=recv_sems.at[k * n + i],
                    device_id=(px, py, pc), device_id_type=MESH)
                cp.start()
                arrive = pltpu.make_async_remote_copy(
                    src_ref=piece(i, pj), dst_ref=dst[i].at[pj],
                    send_sem=send_sems.at[k * n + i], recv_sem=recv_sems.at[k * n + i],
                    device_id=(px, py, pc), device_id_type=MESH)
                copies.append((cp, arrive))
        own = []
        for i in range(n):
            lc = pltpu.make_async_copy(piece(i, me), dst[i].at[me], local_sems.at[i])
            lc.start()
            own.append(lc)
        for cp, arrive in copies:
            arrive.wait_recv()
        for cp, arrive in copies:
            cp.wait_send()
        for lc in own:
            lc.wait()

    any_spec = pl.BlockSpec(memory_space=pl.ANY)
    out_specs = [any_spec] * n
    if with_token:
        out_shape.append(jax.ShapeDtypeStruct((8, LANES), F32))
        out_specs.append(pl.BlockSpec(memory_space=pltpu.VMEM))
    return pl.pallas_call(
        body, name=name, out_shape=tuple(out_shape),
        in_specs=[any_spec] * n, out_specs=tuple(out_specs),
        scratch_shapes=[pltpu.SemaphoreType.DMA((7 * n,)), pltpu.SemaphoreType.DMA((7 * n,)),
                        pltpu.SemaphoreType.DMA((n,))],
        compiler_params=pltpu.CompilerParams(has_side_effects=True),
    )(*srcs)


HBM_SPEC = pl.BlockSpec(memory_space=pltpu.HBM)
SEM_SPEC = pl.BlockSpec(memory_space=pltpu.SEMAPHORE)
DATAFLOW = pltpu.SideEffectType.DATAFLOW_SIDE_EFFECTING


def _peers():
    x, y, c = lax.axis_index("x"), lax.axis_index("y"), lax.axis_index("c")
    out = []
    for k, (dx, dy, dc) in enumerate(PEER_FLIPS):
        px = 1 - x if dx else x
        py = 1 - y if dy else y
        pc = 1 - c if dc else c
        out.append((k, (px, py, pc), 4 * px + 2 * py + pc))
    return 4 * x + 2 * y + c, out


def _exchange_start(name, srcs, lands, per_dest):
    n = len(srcs)

    def body(*refs):
        src, land = refs[:n], refs[n:2 * n]
        send_sems, recv_sems = refs[2 * n], refs[2 * n + 1]
        token = refs[-1]
        me, peers = _peers()
        for k, peer, pj in peers:
            for i in range(n):
                pltpu.make_async_remote_copy(
                    src_ref=src[i].at[pj] if per_dest[i] else src[i], dst_ref=land[i].at[me],
                    send_sem=send_sems.at[k * n + i], recv_sem=recv_sems.at[k * n + i],
                    device_id=peer, device_id_type=MESH).start()
        token[...] = jnp.zeros((8, LANES), F32)

    arrays = list(srcs) + list(lands)
    return pl.pallas_call(
        body, name=name,
        out_shape=(pltpu.SemaphoreType.DMA((7 * n,)), pltpu.SemaphoreType.DMA((7 * n,)),
                   *[pltpu.HBM(a.shape, a.dtype) for a in arrays], jax.ShapeDtypeStruct((8, LANES), F32)),
        in_specs=[HBM_SPEC] * (2 * n),
        out_specs=(SEM_SPEC, SEM_SPEC, *[HBM_SPEC] * (2 * n), pl.BlockSpec(memory_space=pltpu.VMEM)),
        input_output_aliases={i: 2 + i for i in range(2 * n)},
        compiler_params=pltpu.CompilerParams(has_side_effects=DATAFLOW),
    )(*[pltpu.with_memory_space_constraint(a, pltpu.HBM) for a in arrays])


def _exchange_wait(name, started, per_dest, after):
    n = (len(started) - 3) // 2
    send_sems, recv_sems = started[0], started[1]
    arrays = list(started[2:2 + 2 * n])

    def body(*refs):
        src, land = refs[:n], refs[n:2 * n]
        send, recv = refs[2 * n], refs[2 * n + 1]
        me, peers = _peers()
        for k, peer, pj in peers:
            for i in range(n):
                cp = pltpu.make_async_remote_copy(
                    src_ref=src[i].at[pj] if per_dest[i] else src[i], dst_ref=land[i].at[pj],
                    send_sem=send.at[k * n + i], recv_sem=recv.at[k * n + i],
                    device_id=peer, device_id_type=MESH)
                cp.wait_send()
                cp.wait_recv()

    outs = pl.pallas_call(
        body, name=name,
        out_shape=tuple(pltpu.HBM(a.shape, a.dtype) for a in arrays),
        in_specs=[HBM_SPEC] * (2 * n) + [SEM_SPEC, SEM_SPEC, pl.BlockSpec(memory_space=pl.ANY)],
        out_specs=tuple([HBM_SPEC] * (2 * n)),
        input_output_aliases={i: i for i in range(2 * n)},
        compiler_params=pltpu.CompilerParams(has_side_effects=DATAFLOW),
    )(*arrays, send_sems, recv_sems, after)
    return outs[:n], outs[n:]


def _mod_shard(c_all, w_ada):
    def body(c_ref, w_ref, o_ref):
        cv = c_ref[...]
        ca = cv * _sig(cv)
        o_ref[...] = _dot(ca.astype(BF16), w_ref[...].astype(BF16))

    return pl.pallas_call(
        body, name="mod_shard", out_shape=jax.ShapeDtypeStruct((N_DEV, w_ada.shape[1]), F32),
        compiler_params=_params(),
    )(c_all, w_ada)


def _fwd_in(x, nw1, modnb, bada, w_main, w_ba):
    def body(x_ref, nw_ref, mod_ref, b_ref, wm_ref, wb_ref, pm_ref, pb_ref, hb_ref):
        xv = x_ref[...]
        r = lax.rsqrt(jnp.mean(xv * xv, axis=-1, keepdims=True) + EPS)
        h = (xv * r * nw_ref[...]) * (1.0 + _mod(mod_ref, b_ref, 1)) + _mod(mod_ref, b_ref, 0)
        hb = h.astype(BF16)
        hb_ref[...] = hb
        pm_ref[...] = _dot_nt(hb, wm_ref[...])
        pb_ref[...] = _dot_nt(hb, wb_ref[...])

    return pl.pallas_call(
        body, name="fwd_in", grid=(NT,),
        in_specs=[pl.BlockSpec((TM, D), lambda i: (i, 0)), _const((1, D)), _const((1, 6 * D)), _const((1, 6 * D)),
                  _const((NMAIN, D)), _const((LANES, D))],
        out_specs=(pl.BlockSpec((TM, NMAIN), lambda i: (i, 0)), pl.BlockSpec((TM, LANES), lambda i: (i, 0)),
                   pl.BlockSpec((TM, D), lambda i: (i, 0))),
        out_shape=(jax.ShapeDtypeStruct((S, NMAIN), F32), jax.ShapeDtypeStruct((S, LANES), F32),
                   jax.ShapeDtypeStruct((S, D), BF16)),
        compiler_params=_params(dimension_semantics=("arbitrary",)),
    )(x, nw1, modnb, bada, w_main, w_ba)


def _group_mean_matrix():
    ii = lax.broadcasted_iota(jnp.int32, (CW, CW), 0) // GSZ
    jj = lax.broadcasted_iota(jnp.int32, (CW, CW), 1) // GSZ
    return jnp.where(ii == jj, 1.0 / GSZ, 0.0).astype(F32)


def _conf_fwd(p_main, conv_w, conv_b, gn_w, gn_b):
    def body(a_ref, g_ref, w_ref, b_ref, gw_ref, gb_ref, y_ref, oa_ref, ubuf):
        i = pl.program_id(0)

        @pl.when(i == 0)
        def _():
            ubuf[0:HALO, :] = jnp.zeros((HALO, CW), F32)

        ubuf[HALO:HALO + TM, :] = a_ref[...] * _sig(g_ref[...])
        acc = jnp.zeros((TM, CW), F32) + b_ref[...]
        for k in range(KC):
            off = HALO - (KC - 1) + k
            acc = acc + w_ref[k:k + 1, :] * ubuf[off:off + TM, :]
        y_ref[...] = acc
        ubuf[0:HALO, :] = ubuf[TM:TM + HALO, :]
        pm = _group_mean_matrix()
        dlt = acc - _dot(acc, pm, HI)
        var = _dot(dlt * dlt, pm, HI)
        o = dlt * lax.rsqrt(var + EPS) * gw_ref[...] + gb_ref[...]
        oa_ref[...] = o * _sig(o)

    return pl.pallas_call(
        body, name="conf_fwd", grid=(NT,),
        in_specs=[pl.BlockSpec((TM, CW), lambda i: (i, 0)), pl.BlockSpec((TM, CW), lambda i: (i, 1)),
                  _const((KC, CW)), _const((1, CW)), _const((1, CW)), _const((1, CW))],
        out_specs=(pl.BlockSpec((TM, CW), lambda i: (i, 0)), pl.BlockSpec((TM, CW), lambda i: (i, 0))),
        out_shape=(jax.ShapeDtypeStruct((S, CW), F32), jax.ShapeDtypeStruct((S, CW), F32)),
        scratch_shapes=[pltpu.VMEM((HALO + TM, CW), F32)],
        compiler_params=_params(dimension_semantics=("arbitrary",)),
    )(p_main, p_main, conv_w, conv_b, gn_w, gn_b)


def _tri_iota():
    ii = lax.broadcasted_iota(jnp.int32, (CL, CL), 0)
    jj = lax.broadcasted_iota(jnp.int32, (CL, CL), 1)
    return ii, jj


def _gdn_gates(ba, alog_l, dt_l):
    beta_all = _sig(ba)
    xg = ba + dt_l
    sp = jnp.maximum(xg, 0.0) + jnp.log(1.0 + jnp.exp(-jnp.abs(xg)))
    neg_a = -jnp.exp(alog_l)
    return beta_all, neg_a * sp, xg, neg_a


def _gdn_cumsum(g_all):
    ii, jj = _tri_iota()
    low = jnp.where(ii >= jj, 1.0, 0.0).astype(F32)
    gcum = _dot(low, g_all, HI)
    return gcum, jnp.transpose(gcum)


def _unit_lower_inverses(mats):
    ii, jj = _tri_iota()
    eye = jnp.where(ii == jj, 1.0, 0.0).astype(F32)
    ts = [eye - a for a in mats]
    ps = [_dot(a, a, HI) for a in mats]
    for _ in range(4):
        ts = [t + _dot(t, p, HI) for t, p in zip(ts, ps)]
        ps = [_dot(p, p, HI) for p in ps]
    return [t + _dot(t, p, HI) for t, p in zip(ts, ps)]


def _head_terms(qh, kh, beta, gcol, grow):
    ii, jj = _tri_iota()
    causal = ii >= jj
    strict = ii > jj
    rq = lax.rsqrt(_rowsum(qh * qh) + EPS)
    rk = lax.rsqrt(_rowsum(kh * kh) + EPS)
    qn = qh * rq
    kn = kh * rk
    qs = qn * QSCALE
    decay = jnp.where(causal, jnp.exp(jnp.where(causal, gcol - grow, 0.0)), 0.0)
    gam = jnp.exp(gcol)
    gl = gcol[CL - 1:CL, :]
    kds = jnp.exp(gl - gcol)
    cd = jnp.exp(gl)
    kb = kn * beta
    a = jnp.where(strict, _dot_nt(kb, kn, GP) * decay, 0.0)
    qk = jnp.where(causal, _dot_nt(qs, kn, GP) * decay, 0.0)
    return dict(rq=rq, rk=rk, qn=qn, kn=kn, qs=qs, decay=decay, gam=gam, kds=kds, cd=cd, kb=kb, a=a, qk=qk,
                causal=causal, strict=strict)


def _short_conv(w_ref, buf, rows=CL):
    acc = w_ref[0:1, :] * buf[SH - KS + 1:SH - KS + 1 + rows, :]
    for k in range(1, KS):
        off = SH - (KS - 1) + k
        acc = acc + w_ref[k:k + 1, :] * buf[off:off + rows, :]
    return acc


CPS = 4
TG = CPS * CL


def _gdn_prep(p_main, p_ba, gdn_conv_w, alog_l, dt_l):
    def body(q_ref, k_ref, v_ref, qh_ref, kh_ref, vh_ref, ba_ref, w_ref, al_ref, dt_ref,
             wo_ref, uo_ref, qg_ref, kd_ref, qk_ref, cd_ref, t_ref, xbuf):
        i = pl.program_id(0)
        first = i == 0
        xbuf[0:SH, 0:GW] = jnp.where(first, 0.0, qh_ref[...])
        xbuf[0:SH, GW:2 * GW] = jnp.where(first, 0.0, kh_ref[...])
        xbuf[0:SH, 2 * GW:3 * GW] = jnp.where(first, 0.0, vh_ref[...])
        xbuf[SH:SH + TG, 0:GW] = q_ref[...]
        xbuf[SH:SH + TG, GW:2 * GW] = k_ref[...]
        xbuf[SH:SH + TG, 2 * GW:3 * GW] = v_ref[...]
        conv = _short_conv(w_ref, xbuf, TG)
        qkv = conv * _sig(conv)
        beta_all, g_all, _, _ = _gdn_gates(ba_ref[...], al_ref[...], dt_ref[...])
        lane = lax.broadcasted_iota(jnp.int32, (8, LANES), 1)
        cums = [_gdn_cumsum(g_all[cc * CL:(cc + 1) * CL, :]) for cc in range(CPS)]
        pairs = [(cc, h) for cc in range(CPS) for h in range(NH)]
        terms, vbs = [], []
        for cc, h in pairs:
            r0, lo = cc * CL, h * DH
            beta = beta_all[r0:r0 + CL, h:h + 1]
            gcum, gcum_t = cums[cc]
            terms.append(_head_terms(qkv[r0:r0 + CL, lo:lo + DH], qkv[r0:r0 + CL, GW + lo:GW + lo + DH], beta,
                                     gcum[:, NH + h:NH + h + 1], gcum_t[NH + h:NH + h + 1, :]))
            vbs.append(qkv[r0:r0 + CL, 2 * GW + lo:2 * GW + lo + DH] * beta)
        invs = _unit_lower_inverses([f["a"] for f in terms])
        cds = [jnp.zeros((8, LANES), F32) for _ in range(CPS)]
        for (cc, h), f, t, vb in zip(pairs, terms, invs, vbs):
            r0, lo = cc * CL, h * DH
            t_ref[cc, h] = t
            uo_ref[r0:r0 + CL, lo:lo + DH] = _dot(t, vb, GP)
            wo_ref[r0:r0 + CL, lo:lo + DH] = _dot(t, f["kb"] * f["gam"], GP).astype(BF16)
            qg_ref[r0:r0 + CL, lo:lo + DH] = (f["qs"] * f["gam"]).astype(BF16)
            kd_ref[r0:r0 + CL, lo:lo + DH] = (f["kn"] * f["kds"]).astype(BF16)
            qk_ref[cc, h] = f["qk"].astype(BF16)
            cds[cc] = cds[cc] + jnp.where(lane == h, f["cd"], 0.0)
        for cc in range(CPS):
            cd_ref[cc] = cds[cc]

    col = lambda j: pl.BlockSpec((TG, GW), lambda i: (i, j))
    halo = lambda j: pl.BlockSpec((SH, GW), lambda i: (jnp.maximum(i * (TG // SH) - 1, 0), j))
    tile = lambda: pl.BlockSpec((TG, GW), lambda i: (i, 0))
    sq = lambda: pl.BlockSpec((CPS, NH, CL, CL), lambda i: (i, 0, 0, 0))
    return pl.pallas_call(
        body, name="gdn_prep", grid=(NCH // CPS,),
        in_specs=[col(2), col(3), col(4), halo(2), halo(3), halo(4), pl.BlockSpec((TG, LANES), lambda i: (i, 0)),
                  _const((KS, 3 * GW)), _const((1, LANES)), _const((1, LANES))],
        out_specs=(tile(), tile(), tile(), tile(), sq(), pl.BlockSpec((CPS, 8, LANES), lambda i: (i, 0, 0)), sq()),
        out_shape=(jax.ShapeDtypeStruct((S, GW), BF16), jax.ShapeDtypeStruct((S, GW), F32),
                   jax.ShapeDtypeStruct((S, GW), BF16), jax.ShapeDtypeStruct((S, GW), BF16),
                   jax.ShapeDtypeStruct((NCH, NH, CL, CL), BF16), jax.ShapeDtypeStruct((NCH, 8, LANES), F32),
                   jax.ShapeDtypeStruct((NCH, NH, CL, CL), F32)),
        scratch_shapes=[pltpu.VMEM((SH + TG, 3 * GW), F32)],
        compiler_params=_params(dimension_semantics=("arbitrary",)),
    )(p_main, p_main, p_main, p_main, p_main, p_main, p_ba, gdn_conv_w, alog_l, dt_l)


def _gdn_scan(w_o, u_o, qg, kd, qk, cd, p_main, gdn_nw):
    def body(w_ref, u_ref, qg_ref, kd_ref, qk_ref, cd_ref, z_ref, nw_ref, ob_ref, o_ref, sin_ref, state):
        n = pl.program_id(0)

        @pl.when(n == 0)
        def _():
            state[...] = jnp.zeros((NH, DH, DH), F32)

        def head(h):
            lo = h * DH
            st = state[h]
            sin_ref[0, h] = st
            sb = st.astype(BF16)
            v_new = u_ref[:, lo:lo + DH] - _dot(w_ref[:, lo:lo + DH], sb)
            yield
            vb = v_new.astype(BF16)
            o = _dot(qg_ref[:, lo:lo + DH], sb) + _dot(qk_ref[0, h], vb)
            state[h] = st * cd_ref[0, 0:1, h:h + 1] + _dot_tn(kd_ref[:, lo:lo + DH], vb)
            yield
            o_ref[:, lo:lo + DH] = o
            r = lax.rsqrt(jnp.mean(o * o, axis=-1, keepdims=True) + EPS)
            zh = z_ref[:, lo:lo + DH]
            ob_ref[:, lo:lo + DH] = o * r * nw_ref[...] * (zh * _sig(zh))

        _lockstep(head(h) for h in range(NH))

    tile = lambda: pl.BlockSpec((CL, GW), lambda n: (n, 0))
    return pl.pallas_call(
        body, name="gdn_scan", grid=(NCH,),
        in_specs=[tile(), tile(), tile(), tile(), pl.BlockSpec((1, NH, CL, CL), lambda n: (n, 0, 0, 0)),
                  pl.BlockSpec((1, 8, LANES), lambda n: (n, 0, 0)), pl.BlockSpec((CL, GW), lambda n: (n, 5)),
                  _const((1, DH))],
        out_specs=(tile(), tile(), pl.BlockSpec((1, NH, DH, DH), lambda n: (n, 0, 0, 0))),
        out_shape=(jax.ShapeDtypeStruct((S, GW), F32), jax.ShapeDtypeStruct((S, GW), F32),
                   jax.ShapeDtypeStruct((NCH, NH, DH, DH), F32)),
        scratch_shapes=[pltpu.VMEM((NH, DH, DH), F32)],
        compiler_params=_params(dimension_semantics=("arbitrary",)),
    )(w_o, u_o, qg, kd, qk, cd, p_main, gdn_nw)


def _fwd_out(out_a, out_b, x, modnb, bada, w_out):
    def body(oa_ref, ob_ref, x_ref, mod_ref, b_ref, w_ref, x1_ref, mix_ref, oab_ref):
        oa = oa_ref[...].astype(BF16)
        ob = ob_ref[...].astype(BF16)
        oab_ref[:, 0:CW] = oa
        oab_ref[:, CW:D] = ob
        mix = _dot(oa, w_ref[0:CW, :]) + _dot(ob, w_ref[CW:D, :])
        mix_ref[...] = mix
        x1_ref[...] = x_ref[...] + _mod(mod_ref, b_ref, 2) * mix

    tile = lambda w: pl.BlockSpec((TM, w), lambda i: (i, 0))
    return pl.pallas_call(
        body, name="fwd_out", grid=(NT,),
        in_specs=[tile(CW), tile(GW), tile(D), _const((1, 6 * D)), _const((1, 6 * D)), _const((D, D))],
        out_specs=(tile(D), tile(D), tile(D)),
        out_shape=(jax.ShapeDtypeStruct((S, D), F32), jax.ShapeDtypeStruct((S, D), F32),
                   jax.ShapeDtypeStruct((S, D), BF16)),
        compiler_params=_params(dimension_semantics=("arbitrary",)),
    )(out_a, out_b, x, modnb, bada, w_out)


TF = 128
FFN_STATS = 8


def _ffn_fwd_bwd(x1, tgt, modnb, bada, nw2, nfw, w_fi, w_fo):
    def body(x1_ref, tgt_ref, mod_ref, b_ref, nw2_ref, nfw_ref, wi_ref, wo_ref,
             dx1_ref, hb_ref, act_ref, dffn_ref, df_ref, st_ref):
        i = pl.program_id(0)

        @pl.when(i == 0)
        def _():
            st_ref[...] = jnp.zeros((FFN_STATS, D), F32)

        sh2, sc2, gt2 = _mod(mod_ref, b_ref, 3), _mod(mod_ref, b_ref, 4), _mod(mod_ref, b_ref, 5)
        x1v = x1_ref[...]
        r2 = lax.rsqrt(jnp.mean(x1v * x1v, axis=-1, keepdims=True) + EPS)
        xr2 = x1v * r2
        xn2 = xr2 * nw2_ref[...]
        hb = (xn2 * (1.0 + sc2) + sh2).astype(BF16)
        hb_ref[...] = hb
        fg, fu, sg = [], [], []
        ffn = jnp.zeros((TF, D), F32)
        for j in range(4):
            fgj = _dot_nt(hb, wi_ref[j])
            fuj = _dot_nt(hb, wi_ref[j + 4])
            sj = _sig(fgj)
            aj = (fgj * sj * fuj).astype(BF16)
            act_ref[j] = aj
            ffn = ffn + _dot(aj, wo_ref[j])
            fg.append(fgj)
            fu.append(fuj)
            sg.append(sj)
        x2 = x1v + gt2 * ffn
        r3 = lax.rsqrt(jnp.mean(x2 * x2, axis=-1, keepdims=True) + EPS)
        xr3 = x2 * r3
        err = xr3 * nfw_ref[...] - tgt_ref[...]
        loss = 0.5 * jnp.sum(jnp.mean(err * err, axis=-1, keepdims=True), axis=0, keepdims=True)
        dy = err * (1.0 / D)
        st_ref[0:1, :] += _colsum(dy * xr3)
        dyr = dy * nfw_ref[...]
        dx2 = r3 * (dyr - xr3 * jnp.mean(dyr * xr3, axis=-1, keepdims=True))
        st_ref[1:2, :] += _colsum(dx2 * ffn)
        st_ref[5:6, :] += jnp.broadcast_to(loss, (1, D))
        dffn = (gt2 * dx2).astype(BF16)
        dffn_ref[...] = dffn
        dh = jnp.zeros((TF, D), F32)
        for j in range(4):
            dact = _dot_nt(dffn, wo_ref[j])
            dfg = (dact * fu[j] * (sg[j] * (1.0 + fg[j] * (1.0 - sg[j])))).astype(BF16)
            dfu = (dact * (fg[j] * sg[j])).astype(BF16)
            df_ref[j] = dfg
            df_ref[j + 4] = dfu
            dh = dh + _dot(dfg, wi_ref[j]) + _dot(dfu, wi_ref[j + 4])
        st_ref[2:3, :] += _colsum(dh)
        st_ref[3:4, :] += _colsum(dh * xn2)
        dxn = dh * (1.0 + sc2)
        st_ref[4:5, :] += _colsum(dxn * xr2)
        dxr = dxn * nw2_ref[...]
        dx1_ref[...] = dx2 + r2 * (dxr - xr2 * jnp.mean(dxr * xr2, axis=-1, keepdims=True))

    tile = lambda w: pl.BlockSpec((TF, w), lambda i: (i, 0))
    return pl.pallas_call(
        body, name="ffn_fwd_bwd", grid=(S // TF,),
        in_specs=[tile(D), tile(D), _const((1, 6 * D)), _const((1, 6 * D)), _const((1, D)), _const((1, D)),
                  _const1((N_DEV, FB, D)), _const1((4, FB, D))],
        out_specs=(tile(D), tile(D), pl.BlockSpec((4, TF, FB), lambda i: (0, i, 0)), tile(D),
                   pl.BlockSpec((N_DEV, TF, FB), lambda i: (0, i, 0)), _const((FFN_STATS, D))),
        out_shape=(jax.ShapeDtypeStruct((S, D), F32), jax.ShapeDtypeStruct((S, D), BF16),
                   jax.ShapeDtypeStruct((4, S, FB), BF16), jax.ShapeDtypeStruct((S, D), BF16),
                   jax.ShapeDtypeStruct((N_DEV, S, FB), BF16), jax.ShapeDtypeStruct((FFN_STATS, D), F32)),
        compiler_params=_params(44, dimension_semantics=("arbitrary",)),
    )(x1, tgt, modnb, bada, nw2, nfw, w_fi, w_fo)


def _grad_w(name, a, b, nb):
    m, n = a.shape[1], b.shape[1]

    def body(a_ref, b_ref, o_ref):
        o_ref[...] = _dot_tn(a_ref[...], b_ref[...]).astype(BF16)

    return pl.pallas_call(
        body, name=name, grid=(m // nb,),
        in_specs=[pl.BlockSpec((S, nb), lambda j: (0, j)), _const((S, n))],
        out_specs=pl.BlockSpec((nb, n), lambda j: (j, 0)),
        out_shape=jax.ShapeDtypeStruct((m, n), BF16),
        compiler_params=_params(dimension_semantics=("arbitrary",)),
    )(a, b)


def _grad_w_ffn_in(hb2, df):
    def body(a_ref, b_ref, o_ref):
        o_ref[0] = _dot_tn(b_ref[0], a_ref[...]).astype(BF16)

    return pl.pallas_call(
        body, name="grad_w_ffn_in", grid=(N_DEV,),
        in_specs=[_const((S, D)), pl.BlockSpec((1, S, FB), lambda j: (j, 0, 0))],
        out_specs=pl.BlockSpec((1, FB, D), lambda j: (j, 0, 0)),
        out_shape=jax.ShapeDtypeStruct((N_DEV, FB, D), BF16),
        compiler_params=_params(dimension_semantics=("arbitrary",)),
    )(hb2, df)


def _grad_w_ffn_out(act, dffn):
    def body(a_ref, b_ref, o_ref):
        o_ref[0] = _dot_tn(a_ref[0], b_ref[...]).astype(BF16)

    return pl.pallas_call(
        body, name="grad_w_ffn_out", grid=(4,),
        in_specs=[pl.BlockSpec((1, S, FB), lambda j: (j, 0, 0)), _const((S, D))],
        out_specs=pl.BlockSpec((1, FB, D), lambda j: (j, 0, 0)),
        out_shape=jax.ShapeDtypeStruct((4, FB, D), BF16),
        compiler_params=_params(dimension_semantics=("arbitrary",)),
    )(act, dffn)


def _bwd_out(dx1, mix, modnb, bada, w_out):
    def body(dx_ref, mix_ref, mod_ref, b_ref, w_ref, dmix_ref, doa_ref, dob_ref, st_ref):
        i = pl.program_id(0)

        @pl.when(i == 0)
        def _():
            st_ref[...] = jnp.zeros((8, D), F32)

        dx = dx_ref[...]
        st_ref[0:1, :] += _colsum(dx * mix_ref[...])
        dmix = (_mod(mod_ref, b_ref, 2) * dx).astype(BF16)
        dmix_ref[...] = dmix
        doa_ref[...] = _dot_nt(dmix, w_ref[0:CW, :])
        dob_ref[...] = _dot_nt(dmix, w_ref[CW:D, :])

    tile = lambda w: pl.BlockSpec((TM, w), lambda i: (i, 0))
    return pl.pallas_call(
        body, name="bwd_out", grid=(NT,),
        in_specs=[tile(D), tile(D), _const((1, 6 * D)), _const((1, 6 * D)), _const((D, D))],
        out_specs=(tile(D), tile(CW), tile(GW), _const((8, D))),
        out_shape=(jax.ShapeDtypeStruct((S, D), BF16), jax.ShapeDtypeStruct((S, CW), F32),
                   jax.ShapeDtypeStruct((S, GW), F32), jax.ShapeDtypeStruct((8, D), F32)),
        compiler_params=_params(dimension_semantics=("arbitrary",)),
    )(dx1, mix, modnb, bada, w_out)


CONF_STATS = 40


def _conf_bwd(d_out_a, y, p_main, conv_w, gn_w, gn_b):
    def body(do_ref, y_ref, a_ref, g_ref, ah_ref, gh_ref, w_ref, gw_ref, gb_ref, dp_ref, st_ref, ubuf, dybuf):
        i = pl.program_id(0)

        @pl.when(i == 0)
        def _():
            st_ref[...] = jnp.zeros((CONF_STATS, CW), F32)
            dybuf[TM:TM + HALO, :] = jnp.zeros((HALO, CW), F32)

        pm = _group_mean_matrix()
        yv = y_ref[...]
        dlt = yv - _dot(yv, pm, HI)
        rstd = lax.rsqrt(_dot(dlt * dlt, pm, HI) + EPS)
        un = dlt * rstd
        o = un * gw_ref[...] + gb_ref[...]
        so = _sig(o)
        d_o = do_ref[...] * (so * (1.0 + o * (1.0 - so)))
        st_ref[33:34, :] += _colsum(d_o)
        st_ref[32:33, :] += _colsum(d_o * un)
        dun = d_o * gw_ref[...]
        dy = rstd * (dun - _dot(dun, pm, HI) - un * _dot(dun * un, pm, HI))
        st_ref[31:32, :] += _colsum(dy)
        dybuf[0:TM, :] = dy

        a = a_ref[...]
        sg = _sig(g_ref[...])
        first = i == NT - 1
        ubuf[0:HALO, :] = jnp.where(first, 0.0, ah_ref[...] * _sig(gh_ref[...]))
        ubuf[HALO:HALO + TM, :] = a * sg
        du = jnp.zeros((TM, CW), F32)
        for k in range(KC):
            off = HALO - (KC - 1) + k
            st_ref[k:k + 1, :] += _colsum(dy * ubuf[off:off + TM, :])
            du = du + w_ref[k:k + 1, :] * dybuf[KC - 1 - k:KC - 1 - k + TM, :]
        dybuf[TM:TM + HALO, :] = dybuf[0:HALO, :]
        dp_ref[:, 0:CW] = (du * sg).astype(BF16)
        dp_ref[:, CW:2 * CW] = (du * a * sg * (1.0 - sg)).astype(BF16)

    rev = lambda w, j=0: pl.BlockSpec((TM, w), lambda i: (NT - 1 - i, j))
    halo = lambda j: pl.BlockSpec((HALO, CW), lambda i: (jnp.maximum((NT - 1 - i) * (TM // HALO) - 1, 0), j))
    return pl.pallas_call(
        body, name="conf_bwd", grid=(NT,),
        in_specs=[rev(CW), rev(CW), rev(CW, 0), rev(CW, 1), halo(0), halo(1),
                  _const((KC, CW)), _const((1, CW)), _const((1, CW))],
        out_specs=(rev(2 * CW), _const((CONF_STATS, CW))),
        out_shape=(jax.ShapeDtypeStruct((S, 2 * CW), BF16), jax.ShapeDtypeStruct((CONF_STATS, CW), F32)),
        scratch_shapes=[pltpu.VMEM((HALO + TM, CW), F32), pltpu.VMEM((TM + HALO, CW), F32)],
        compiler_params=_params(dimension_semantics=("arbitrary",)),
    )(d_out_a, y, p_main, p_main, p_main, p_main, conv_w, gn_w, gn_b)


GDN_STATS = 8


def _gdn_bwd(d_out_b, o_pre, s_in, t_inv, p_main, p_ba, gdn_conv_w, alog_l, dt_l, gdn_nw):
    def body(dob_ref, o_ref, sin_ref, t_ref, q_ref, k_ref, v_ref, z_ref, qh_ref, kh_ref, vh_ref, ba_ref,
             w_ref, al_ref, dt_ref, nw_ref, dp_ref, dba_ref, st_ref, xbuf, dcbuf, dstate):
        n = pl.program_id(0)

        @pl.when(n == 0)
        def _():
            st_ref[...] = jnp.zeros((GDN_STATS, 3 * GW), F32)
            dcbuf[CL:CL + SH, :] = jnp.zeros((SH, 3 * GW), F32)
            dstate[...] = jnp.zeros((NH, DH, DH), F32)

        first = n == NCH - 1
        xbuf[0:SH, 0:GW] = jnp.where(first, 0.0, qh_ref[...])
        xbuf[0:SH, GW:2 * GW] = jnp.where(first, 0.0, kh_ref[...])
        xbuf[0:SH, 2 * GW:3 * GW] = jnp.where(first, 0.0, vh_ref[...])
        xbuf[SH:SH + CL, 0:GW] = q_ref[...]
        xbuf[SH:SH + CL, GW:2 * GW] = k_ref[...]
        xbuf[SH:SH + CL, 2 * GW:3 * GW] = v_ref[...]
        conv = _short_conv(w_ref, xbuf)
        sc = _sig(conv)
        qkv = conv * sc
        ba = ba_ref[...]
        beta_all, g_all, xg, neg_a = _gdn_gates(ba, al_ref[...], dt_ref[...])
        gcum, gcum_t = _gdn_cumsum(g_all)
        lane = lax.broadcasted_iota(jnp.int32, (CL, LANES), 1)
        row = lax.broadcasted_iota(jnp.int32, (CL, 1), 0)
        acc = dict(dgcum=jnp.zeros((CL, LANES), F32), dbeta=jnp.zeros((CL, LANES), F32))

        def head(h):
            lo = h * DH
            qh = qkv[:, lo:lo + DH]
            kh = qkv[:, GW + lo:GW + lo + DH]
            vh = qkv[:, 2 * GW + lo:2 * GW + lo + DH]
            beta = beta_all[:, h:h + 1]
            f = _head_terms(qh, kh, beta, gcum[:, NH + h:NH + h + 1], gcum_t[NH + h:NH + h + 1, :])
            qn, kn, qs, kb, gam, kds, cd, decay = (f[s] for s in ("qn", "kn", "qs", "kb", "gam", "kds", "cd", "decay"))
            t = t_ref[0, h]
            st = sin_ref[0, h]
            vb = vh * beta
            kbg = kb * gam
            u = _dot(t, vb, GP)
            w = _dot(t, kbg, GP)
            yield
            v_new = u - _dot(w, st, GP)
            q_dec = qs * gam
            k_dec = kn * kds

            o = o_ref[:, lo:lo + DH]
            zh = z_ref[:, lo:lo + DH]
            sz = _sig(zh)
            r = lax.rsqrt(jnp.mean(o * o, axis=-1, keepdims=True) + EPS)
            orr = o * r
            d_out = dob_ref[:, lo:lo + DH]
            dz = d_out * (orr * nw_ref[...]) * (sz * (1.0 + zh * (1.0 - sz)))
            don = d_out * (zh * sz)
            st_ref[4:5, 0:DH] += _colsum(don * orr)
            tt = don * nw_ref[...]
            d_o = r * (tt - orr * jnp.mean(tt * orr, axis=-1, keepdims=True))

            yield
            ds_out = dstate[h]
            dv_new = _dot_tn(f["qk"], d_o, GP) + _dot(k_dec, ds_out, GP)
            dqk = jnp.where(f["causal"], _dot_nt(d_o, v_new, GP), 0.0)
            dq_dec = _dot_nt(d_o, st, GP)
            dk_dec = _dot_nt(v_new, ds_out, GP)
            yield
            dstate[h] = _dot_tn(q_dec, d_o, GP) + cd * ds_out - _dot_tn(w, dv_new, GP)
            dcd = jnp.sum(_rowsum(st * ds_out), axis=0, keepdims=True)
            dw = -_dot_nt(dv_new, st, GP)
            dvb = _dot_tn(t, dv_new, GP)
            yield
            dt_m = _dot_nt(dv_new, vb, GP) + _dot_nt(dw, kbg, GP)
            dkbg = _dot_tn(t, dw, GP)
            yield
            dtt = _dot_nt(dt_m, t, GP)
            yield
            da = jnp.where(f["strict"], -_dot_tn(t, dtt, GP), 0.0)
            yield
            dad = da * decay
            dqkd = dqk * decay
            dkb = _dot(dad, kn, GP) + dkbg * gam
            dkn = _dot_tn(dad, kb, GP) + _dot_tn(dqkd, qs, GP) + dk_dec * kds + dkb * beta
            dqs = _dot(dqkd, kn, GP) + dq_dec * gam
            yield
            m = da * f["a"] + dqk * f["qk"]
            tk = _rowsum(dk_dec * k_dec)
            dgl = jnp.sum(tk, axis=0, keepdims=True) + dcd * cd
            dgc = (_rowsum(m) - _rowsum(jnp.transpose(m)) + _rowsum(dq_dec * q_dec) - tk + _rowsum(dkbg * kbg)
                   + jnp.where(row == CL - 1, dgl, 0.0))
            dbeta = _rowsum(dkb * kn) + _rowsum(dvb * vh)
            acc["dgcum"] = acc["dgcum"] + jnp.where(lane == NH + h, dgc, 0.0)
            acc["dbeta"] = acc["dbeta"] + jnp.where(lane == h, dbeta, 0.0)
            dvh = dvb * beta
            dqn = dqs * QSCALE
            dqh = f["rq"] * (dqn - qn * _rowsum(dqn * qn))
            dkh = f["rk"] * (dkn - kn * _rowsum(dkn * kn))
            dsilu = lambda c0: sc[:, c0:c0 + DH] * (1.0 + conv[:, c0:c0 + DH] * (1.0 - sc[:, c0:c0 + DH]))
            dcbuf[0:CL, lo:lo + DH] = dqh * dsilu(lo)
            dcbuf[0:CL, GW + lo:GW + lo + DH] = dkh * dsilu(GW + lo)
            dcbuf[0:CL, 2 * GW + lo:2 * GW + lo + DH] = dvh * dsilu(2 * GW + lo)
            dp_ref[:, 3 * GW + lo:3 * GW + lo + DH] = dz.astype(BF16)

        _lockstep(head(h) for h in range(NH))
        dgcum_all, dbeta_all = acc["dgcum"], acc["dbeta"]

        ii, jj = _tri_iota()
        upper = jnp.where(ii <= jj, 1.0, 0.0).astype(F32)
        dg_all = _dot(upper, dgcum_all, HI)
        dxg = dg_all * neg_a * _sig(xg)
        st_ref[5:6, 0:LANES] += _colsum(dg_all * g_all)
        st_ref[6:7, 0:LANES] += _colsum(dxg)
        dbl = dbeta_all * beta_all * (1.0 - beta_all)
        dba_ref[...] = jnp.where(lane < NH, dbl, jnp.where(lane < 2 * NH, dxg, 0.0)).astype(BF16)

        dconv = dcbuf[0:CL, :]
        dx = w_ref[0:1, :] * dcbuf[KS - 1:KS - 1 + CL, :]
        st_ref[0:1, :] += _colsum(dconv * xbuf[SH - KS + 1:SH - KS + 1 + CL, :])
        for k in range(1, KS):
            off = SH - (KS - 1) + k
            st_ref[k:k + 1, :] += _colsum(dconv * xbuf[off:off + CL, :])
            dx = dx + w_ref[k:k + 1, :] * dcbuf[KS - 1 - k:KS - 1 - k + CL, :]
        dcbuf[CL:CL + SH, :] = dcbuf[0:SH, :]
        dp_ref[:, 0:3 * GW] = dx.astype(BF16)

    rev = lambda w, j=0: pl.BlockSpec((CL, w), lambda n: (NCH - 1 - n, j))
    halo = lambda j: pl.BlockSpec((SH, GW), lambda n: (jnp.maximum((NCH - 1 - n) * (CL // SH) - 1, 0), j))
    blk4 = lambda a, b: pl.BlockSpec((1, NH, a, b), lambda n: (NCH - 1 - n, 0, 0, 0))
    return pl.pallas_call(
        body, name="gdn_bwd", grid=(NCH,),
        in_specs=[rev(GW), rev(GW), blk4(DH, DH), blk4(CL, CL), rev(GW, 2), rev(GW, 3), rev(GW, 4), rev(GW, 5),
                  halo(2), halo(3), halo(4), rev(LANES), _const((KS, 3 * GW)), _const((1, LANES)),
                  _const((1, LANES)), _const((1, DH))],
        out_specs=(rev(4 * GW), rev(LANES), _const((GDN_STATS, 3 * GW))),
        out_shape=(jax.ShapeDtypeStruct((S, 4 * GW), BF16), jax.ShapeDtypeStruct((S, LANES), BF16),
                   jax.ShapeDtypeStruct((GDN_STATS, 3 * GW), F32)),
        scratch_shapes=[pltpu.VMEM((SH + CL, 3 * GW), F32), pltpu.VMEM((CL + SH, 3 * GW), F32),
                        pltpu.VMEM((NH, DH, DH), F32)],
        compiler_params=_params(dimension_semantics=("arbitrary",)),
    )(d_out_b, o_pre, s_in, t_inv, p_main, p_main, p_main, p_main, p_main, p_main, p_main, p_ba,
      gdn_conv_w, alog_l, dt_l, gdn_nw)


def _bwd_in(dp_conf, dp_gdn, dp_ba, x, dx1, nw1, modnb, bada, w_main, w_ba):
    def body(dc_ref, dg_ref, db_ref, x_ref, dx1_ref, nw_ref, mod_ref, b_ref, wm_ref, wb_ref, gx_ref, st_ref):
        i = pl.program_id(0)

        @pl.when(i == 0)
        def _():
            st_ref[...] = jnp.zeros((8, D), F32)

        dh = (_dot(dc_ref[...], wm_ref[0:2 * CW, :]) + _dot(dg_ref[...], wm_ref[2 * CW:NMAIN, :])
              + _dot(db_ref[...], wb_ref[...]))
        xv = x_ref[...]
        r = lax.rsqrt(jnp.mean(xv * xv, axis=-1, keepdims=True) + EPS)
        xr = xv * r
        st_ref[0:1, :] += _colsum(dh)
        st_ref[1:2, :] += _colsum(dh * (xr * nw_ref[...]))
        dxn = dh * (1.0 + _mod(mod_ref, b_ref, 1))
        st_ref[2:3, :] += _colsum(dxn * xr)
        dxr = dxn * nw_ref[...]
        gx_ref[...] = dx1_ref[...] + r * (dxr - xr * jnp.mean(dxr * xr, axis=-1, keepdims=True))

    tile = lambda w: pl.BlockSpec((TM, w), lambda i: (i, 0))
    return pl.pallas_call(
        body, name="bwd_in", grid=(NT,),
        in_specs=[tile(2 * CW), tile(4 * GW), tile(LANES), tile(D), tile(D), _const((1, D)), _const((1, 6 * D)),
                  _const((1, 6 * D)), _const((NMAIN, D)), _const((LANES, D))],
        out_specs=(tile(D), _const((8, D))),
        out_shape=(jax.ShapeDtypeStruct((S, D), F32), jax.ShapeDtypeStruct((8, D), F32)),
        compiler_params=_params(dimension_semantics=("arbitrary",)),
    )(dp_conf, dp_gdn, dp_ba, x, dx1, nw1, modnb, bada, w_main, w_ba)


def _adamw(w, g, m, v):
    m = ADAM_B1 * m + (1.0 - ADAM_B1) * g
    v = ADAM_B2 * v + (1.0 - ADAM_B2) * (g * g)
    m_hat = m / BC1
    v_hat = v / BC2
    delta = -ADAM_LR * (m_hat / (jnp.sqrt(v_hat) + ADAM_EPS) + ADAM_WD * w)
    return delta, m, v


ADAM_BLOCK_BYTES = 6 * 1024 * 1024


def _adam_tile(rows, cols):
    padded = -(-cols // LANES) * LANES
    if N_DEV * rows * padded * 4 <= ADAM_BLOCK_BYTES:
        return rows, cols
    best = None
    for tr in range(16, rows, 16):
        if rows % tr == 0 and N_DEV * tr * padded * 4 <= ADAM_BLOCK_BYTES:
            best = tr
    if best is not None:
        return best, cols
    rows_padded = -(-rows // 16) * 16
    tc = LANES
    for cand in range(LANES, cols, LANES):
        if cols % cand == 0 and N_DEV * rows_padded * cand * 4 <= ADAM_BLOCK_BYTES:
            tc = cand
    return rows, tc


def _reduce_adam(name, parts, w, m, v, own=None):
    rows, cols = w.shape
    tr, tc = _adam_tile(rows, cols)

    def body(*refs):
        p_ref, w_ref, m_ref, v_ref = refs[:4]
        g_ref, d_ref, nm_ref, nv_ref = refs[-4:]
        if own is None:
            part = lambda j: p_ref[j].astype(F32)
        else:
            me = 4 * lax.axis_index("x") + 2 * lax.axis_index("y") + lax.axis_index("c")
            part = lambda j: jnp.where(me == j, refs[4][...], p_ref[j]).astype(F32)
        g = part(0)
        for j in range(1, N_DEV):
            g = g + part(j)
        g_ref[...] = g
        d_ref[...], nm_ref[...], nv_ref[...] = _adamw(w_ref[...], g, m_ref[...], v_ref[...])

    blk = pl.BlockSpec((tr, tc), lambda i, j: (i, j))
    sds = jax.ShapeDtypeStruct((rows, cols), F32)
    extra = [] if own is None else [own]
    return pl.pallas_call(
        body, name=name, grid=(rows // tr, cols // tc),
        in_specs=[pl.BlockSpec((N_DEV, tr, tc), lambda i, j: (0, i, j)), blk, blk, blk] + [blk] * len(extra),
        out_specs=(blk, blk, blk, blk), out_shape=(sds, sds, sds, sds),
        compiler_params=_params(dimension_semantics=("arbitrary", "arbitrary")),
    )(parts, w, m, v, *extra)


def _ada_adam(c_all, dmod_sh, w, m, v):
    rows, cols = w.shape
    tr = 256

    def body(c_ref, dm_ref, w_ref, m_ref, v_ref, g_ref, d_ref, nm_ref, nv_ref):
        cv = c_ref[...]
        g = _dot_tn(cv * _sig(cv), dm_ref[...], HI)
        g_ref[...] = g
        d_ref[...], nm_ref[...], nv_ref[...] = _adamw(w_ref[...], g, m_ref[...], v_ref[...])

    blk = pl.BlockSpec((tr, cols), lambda i: (i, 0))
    sds = jax.ShapeDtypeStruct((rows, cols), F32)
    return pl.pallas_call(
        body, name="ada_adam", grid=(rows // tr,),
        in_specs=[pl.BlockSpec((N_DEV, tr), lambda i: (0, i)), _const((N_DEV, cols)), blk, blk, blk],
        out_specs=(blk, blk, blk, blk), out_shape=(sds, sds, sds, sds),
        compiler_params=_params(dimension_semantics=("arbitrary",)),
    )(c_all, dmod_sh, w, m, v)


def _lanes(a, at=0):
    return jnp.pad(a, ((0, 0), (at, LANES - at - a.shape[1])))


WEIGHT_NAMES = ["w_ada", "b_ada", "norm_mix_w", "w_in", "conv_w", "conv_b", "conv_gn_w", "conv_gn_b", "gdn_conv_w",
                "gdn_a_log", "gdn_dt_bias", "gdn_norm_w", "w_out", "norm_ffn_w", "w_ffn_in", "w_ffn_out",
                "norm_final_w"]


def _slab(b_ada, norm_mix_w, norm_ffn_w, norm_final_w, conv_b, conv_gn_w, conv_gn_b, gdn_norm_w, a_log, dt_bias):
    return jnp.concatenate([
        b_ada.reshape(48, LANES), norm_mix_w.reshape(8, LANES), norm_ffn_w.reshape(8, LANES),
        norm_final_w.reshape(8, LANES), conv_b.reshape(4, LANES), conv_gn_w.reshape(4, LANES),
        conv_gn_b.reshape(4, LANES), gdn_norm_w.reshape(1, LANES), _lanes(a_log), _lanes(dt_bias),
        jnp.zeros((1, LANES), F32)], axis=0)


def _unslab(t):
    return dict(b_ada=t[0:48].reshape(1, 6 * D), norm_mix_w=t[48:56].reshape(1, D),
                norm_ffn_w=t[56:64].reshape(1, D), norm_final_w=t[64:72].reshape(D),
                conv_b=t[72:76].reshape(1, CW), conv_gn_w=t[76:80].reshape(1, CW),
                conv_gn_b=t[80:84].reshape(1, CW), gdn_norm_w=t[84:85], gdn_a_log=t[85:86, 0:NH],
                gdn_dt_bias=t[86:87, 0:NH])


def _mix_forward(w, xs, modnb):
    w_main = w["w_in"][:NMAIN]
    w_ba = jnp.pad(w["w_in"][NMAIN:], ((0, LANES - 2 * NH), (0, 0)))
    alog_l = _lanes(w["gdn_a_log"], NH)
    dt_l = _lanes(w["gdn_dt_bias"], NH)
    p_main, p_ba, hb1 = _fwd_in(xs, w["norm_mix_w"], modnb, w["b_ada"], w_main, w_ba)
    y_conv, out_a = _conf_fwd(p_main, w["conv_w"], w["conv_b"], w["conv_gn_w"], w["conv_gn_b"])
    w_o, u_o, qg, kd, qk, cd, t_inv = _gdn_prep(p_main, p_ba, w["gdn_conv_w"], alog_l, dt_l)
    out_b, o_pre, s_in = _gdn_scan(w_o, u_o, qg, kd, qk, cd, p_main, w["gdn_norm_w"])
    return dict(w_main=w_main, w_ba=w_ba, alog_l=alog_l, dt_l=dt_l, p_main=p_main, p_ba=p_ba, hb1=hb1,
                y_conv=y_conv, out_a=out_a, out_b=out_b, o_pre=o_pre, s_in=s_in, t_inv=t_inv)


def _ffn_stage(w, f, xs, tgt, modnb):
    x1, mix, oab = _fwd_out(f["out_a"], f["out_b"], xs, modnb, w["b_ada"], w["w_out"])
    dx1, hb2, act, dffn, df, st_ffn = _ffn_fwd_bwd(x1, tgt, modnb, w["b_ada"], w["norm_ffn_w"], w["norm_final_w"],
                                                   w["w_ffn_in"], w["w_ffn_out"])
    gw_ffn_in = _grad_w_ffn_in(hb2, df)
    gw_ffn_out = _grad_w_ffn_out(act, dffn)
    return dict(mix=mix, oab=oab, dx1=dx1, st_ffn=st_ffn, gw_ffn_in=gw_ffn_in, gw_ffn_out=gw_ffn_out)


def _out_backward(w, g, modnb):
    dmix, d_out_a, d_out_b, st_out = _bwd_out(g["dx1"], g["mix"], modnb, w["b_ada"], w["w_out"])
    return dict(d_out_a=d_out_a, d_out_b=d_out_b, st_out=st_out, gw_out=_grad_w("grad_w_out", g["oab"], dmix, 512))


def _mix_backward(w, f, g, a, xs, modnb):
    d_out_a, d_out_b, st_out = a["d_out_a"], a["d_out_b"], a["st_out"]
    dp_conf, st_conf = _conf_bwd(d_out_a, f["y_conv"], f["p_main"], w["conv_w"], w["conv_gn_w"], w["conv_gn_b"])
    dp_gdn, dp_ba, st_gdn = _gdn_bwd(d_out_b, f["o_pre"], f["s_in"], f["t_inv"], f["p_main"], f["p_ba"],
                                     w["gdn_conv_w"], f["alog_l"], f["dt_l"], w["gdn_norm_w"])
    grad_x, st_in = _bwd_in(dp_conf, dp_gdn, dp_ba, xs, g["dx1"], w["norm_mix_w"], modnb, w["b_ada"], f["w_main"],
                            f["w_ba"])
    hb1 = f["hb1"]
    gw_in = jnp.concatenate(
        [_grad_w("grad_w_in_conf", dp_conf, hb1, 512), _grad_w("grad_w_in_gdn", dp_gdn, hb1, 512),
         _grad_w("grad_w_in_ba", dp_ba, hb1, LANES)[:2 * NH]], axis=0)
    st_ffn = g["st_ffn"]
    dmod = jnp.concatenate([st_in[0:1], st_in[1:2], st_out[0:1], st_ffn[2:3], st_ffn[3:4], st_ffn[1:2]], axis=1)
    small = jnp.concatenate([
        dmod.reshape(48, LANES), st_in[2:3].reshape(8, LANES), st_ffn[4:5].reshape(8, LANES),
        st_ffn[0:1].reshape(8, LANES), st_conf[31:32].reshape(4, LANES), st_conf[32:33].reshape(4, LANES),
        st_conf[33:34].reshape(4, LANES), st_gdn[4:5, 0:LANES],
        _lanes(st_gdn[5:6, NH:2 * NH]), _lanes(st_gdn[6:7, NH:2 * NH]), st_ffn[5:6, 0:LANES]], axis=0)
    return dict(grad_x=grad_x, gw_in=gw_in, gw_conv=st_conf[0:KC], gw_gconv=st_gdn[0:KS], small=small)


def _local(w, xs, tgt, modnb):
    f = _mix_forward(w, xs, modnb)
    g = _ffn_stage(w, f, xs, tgt, modnb)
    a = _out_backward(w, g, modnb)
    b = _mix_backward(w, f, g, a, xs, modnb)
    return dict(b, gw_out=a["gw_out"], gw_ffn_in=g["gw_ffn_in"], gw_ffn_out=g["gw_ffn_out"])


def kernel(x, c, w_ada, b_ada, norm_mix_w, w_in, conv_w, conv_b, conv_gn_w, conv_gn_b, gdn_conv_w, gdn_a_log, gdn_dt_bias, gdn_norm_w, w_out, norm_ffn_w, w_ffn_in, w_ffn_out, norm_final_w, loss_target, m_w_ada, m_b_ada, m_norm_mix_w, m_w_in, m_conv_w, m_conv_b, m_conv_gn_w, m_conv_gn_b, m_gdn_conv_w, m_gdn_a_log, m_gdn_dt_bias, m_gdn_norm_w, m_w_out, m_norm_ffn_w, m_w_ffn_in, m_w_ffn_out, m_norm_final_w, v_w_ada, v_b_ada, v_norm_mix_w, v_w_in, v_conv_w, v_conv_b, v_conv_gn_w, v_conv_gn_b, v_gdn_conv_w, v_gdn_a_log, v_gdn_dt_bias, v_gdn_norm_w, v_w_out, v_norm_ffn_w, v_w_ffn_in, v_w_ffn_out, v_norm_final_w):
    me = 4 * lax.axis_index("x") + 2 * lax.axis_index("y") + lax.axis_index("c")
    xs = x.reshape(S, D)
    tgt = loss_target.reshape(S, D)

    g_c, g_cw, g_gcw = _exchange("gather_cond", [c, conv_w[0], gdn_conv_w[0]], [False] * 3)
    c_all = g_c.reshape(N_DEV, D)
    (g_mod,) = _exchange("gather_mod", [_mod_shard(c_all, w_ada[0])], [False])
    modnb = lax.dynamic_index_in_dim(g_mod, me, axis=1, keepdims=False).reshape(1, 6 * D)

    late = [w_out[0].astype(BF16), jnp.transpose(w_ffn_in[0]).astype(BF16), w_ffn_out[0].astype(BF16)]
    g_win, *late_lands = _exchange("gather_weights", [jnp.transpose(w_in[0]).astype(BF16)] + late, [False] * 4,
                                   seed_only=(1, 2, 3))
    late_started = _exchange_start("gather_late_start", late, late_lands, [False] * 3)
    modnb = _after(modnb, late_started[-1])
    w = dict(b_ada=b_ada, norm_mix_w=norm_mix_w, conv_b=conv_b, conv_gn_w=conv_gn_w, conv_gn_b=conv_gn_b,
             gdn_a_log=gdn_a_log, gdn_dt_bias=gdn_dt_bias, gdn_norm_w=gdn_norm_w, norm_ffn_w=norm_ffn_w,
             norm_final_w=norm_final_w.reshape(1, D),
             conv_w=jnp.transpose(g_cw, (1, 0, 2)).reshape(KC, CW),
             gdn_conv_w=jnp.transpose(g_gcw, (1, 0, 2)).reshape(KS, 3 * GW),
             w_in=g_win.reshape(NIN, D))

    f = _mix_forward(w, xs, modnb)
    _, (g_wout, g_wfi, g_wfo) = _exchange_wait("gather_late_wait", late_started, [False] * 3, f["out_b"])
    w.update(w_out=g_wout.reshape(D, D), w_ffn_in=g_wfi, w_ffn_out=g_wfo.reshape(4, FB, D))
    g = _ffn_stage(w, f, xs, tgt, modnb)

    ffn_grads = [g["gw_ffn_in"], g["gw_ffn_out"].reshape(N_DEV, DFF // N_DEV, D)]
    ffn_started = _exchange_start("scatter_ffn_start", ffn_grads,
                                  [lax.empty(a.shape, a.dtype) for a in ffn_grads], [True] * 2)
    a = _out_backward(w, g, _after(modnb, ffn_started[-1]))
    out_grads = [a["gw_out"].reshape(N_DEV, D // N_DEV, D)]
    out_started = _exchange_start("scatter_out_start", out_grads,
                                  [lax.empty(t.shape, t.dtype) for t in out_grads], [True])
    loc = _mix_backward(w, f, g, a, xs, _after(modnb, out_started[-1]))

    g_small, small_token = _exchange("gather_small", [loc["small"]], [False], with_token=True)

    in_grads = [_after(loc["gw_in"].reshape(N_DEV, NIN // N_DEV, D), small_token),
                jnp.transpose(loc["gw_conv"].reshape(KC, N_DEV, CW // N_DEV), (1, 0, 2)),
                jnp.transpose(loc["gw_gconv"].reshape(KS, N_DEV, 3 * GW // N_DEV), (1, 0, 2))]
    in_started = _exchange_start("scatter_in_start", in_grads,
                                 [lax.empty(t.shape, t.dtype) for t in in_grads], [True] * 3)
    g_small = _after(g_small, in_started[-1])
    sw = _slab(b_ada, norm_mix_w, norm_ffn_w, norm_final_w, conv_b, conv_gn_w, conv_gn_b, gdn_norm_w, gdn_a_log,
               gdn_dt_bias)
    sm = _slab(m_b_ada, m_norm_mix_w, m_norm_ffn_w, m_norm_final_w, m_conv_b, m_conv_gn_w, m_conv_gn_b,
               m_gdn_norm_w, m_gdn_a_log, m_gdn_dt_bias)
    sv = _slab(v_b_ada, v_norm_mix_w, v_norm_ffn_w, v_norm_final_w, v_conv_b, v_conv_gn_w, v_conv_gn_b,
               v_gdn_norm_w, v_gdn_a_log, v_gdn_dt_bias)
    small_out = _reduce_adam("adam_small", g_small, sw, sm, sv)
    loss = small_out[0][SMALL_ROWS - 1, 0]
    res = [_unslab(t) for t in small_out]

    dmod_rows = g_small[:, 0:48, :].reshape(N_DEV, 6 * D)
    dmod_sh = lax.dynamic_slice_in_dim(dmod_rows, me * (6 * D // N_DEV), 6 * D // N_DEV, axis=1)

    def own(sent):
        return lax.dynamic_index_in_dim(sent, me, axis=0, keepdims=False)

    big = dict(w_ada=_ada_adam(c_all, dmod_sh, w_ada[0], m_w_ada[0], v_w_ada[0]))
    (sent_fi, sent_fo), (r_fi, r_fo) = _exchange_wait("scatter_ffn_wait", ffn_started, [True] * 2, big["w_ada"][0])
    big["w_ffn_in"] = [jnp.transpose(t) for t in _reduce_adam(
        "adam_w_ffn_in", r_fi, jnp.transpose(w_ffn_in[0]), jnp.transpose(m_w_ffn_in[0]),
        jnp.transpose(v_w_ffn_in[0]), own(sent_fi))]
    big["w_ffn_out"] = _reduce_adam("adam_w_ffn_out", r_fo, w_ffn_out[0], m_w_ffn_out[0], v_w_ffn_out[0],
                                    own(sent_fo))
    (sent_out,), (r_out,) = _exchange_wait("scatter_out_wait", out_started, [True], big["w_ffn_out"][0])
    big["w_out"] = _reduce_adam("adam_w_out", r_out, w_out[0], m_w_out[0], v_w_out[0], own(sent_out))
    (sent_in, sent_cw, sent_gcw), (r_in, r_cw, r_gcw) = _exchange_wait(
        "scatter_in_wait", in_started, [True] * 3, big["w_out"][0])
    big["w_in"] = [jnp.transpose(t) for t in _reduce_adam(
        "adam_w_in", r_in, jnp.transpose(w_in[0]), jnp.transpose(m_w_in[0]), jnp.transpose(v_w_in[0]),
        own(sent_in))]
    big["conv_w"] = _reduce_adam("adam_conv_w", r_cw, conv_w[0], m_conv_w[0], v_conv_w[0], own(sent_cw))
    big["gdn_conv_w"] = _reduce_adam("adam_gdn_conv_w", r_gcw, gdn_conv_w[0], m_gdn_conv_w[0], v_gdn_conv_w[0],
                                     own(sent_gcw))
    outs = [loss, loc["grad_x"].reshape(1, S, D)]
    for kind in range(4):
        for nm in WEIGHT_NAMES:
            outs.append(big[nm][kind][None] if nm in big else res[kind][nm])
    return tuple(outs)
```

```python
import functools

import jax
import jax.numpy as jnp
from jax import lax
from jax.experimental import pallas as pl
from jax.experimental.pallas import tpu as pltpu

F32 = jnp.float32
BF16 = jnp.bfloat16
HI = lax.Precision.HIGHEST
MESH = pl.DeviceIdType.MESH

N_DEV = 8
S = 2048
D = 1024
TM = 256
NT = S // TM
CW = 512
KC = 31
NG = 8
GSZ = CW // NG
HALO = 32
GW = 512
NH = 4
DH = 128
KS = 4
SH = 8
CL = 64
NCH = S // CL
NMAIN = 2 * CW + 4 * GW
NIN = NMAIN + 2 * NH
DFF = 2816
FB = DFF // 4
EPS = 1e-6
QSCALE = DH ** -0.5
LANES = 128
SMALL_ROWS = 88

ADAM_LR = 0.001
ADAM_B1 = 0.9
ADAM_B2 = 0.999
ADAM_EPS = 1e-08
ADAM_WD = 0.01
ADAM_STEP = 10
BC1 = 1.0 - ADAM_B1 ** ADAM_STEP
BC2 = 1.0 - ADAM_B2 ** ADAM_STEP

MIB = 1024 * 1024
VMEM_LIMIT_MIB = 32


def _params(limit_mib=VMEM_LIMIT_MIB, **kw):
    return pltpu.CompilerParams(vmem_limit_bytes=limit_mib * MIB, **kw)


def _sig(x):
    return jax.nn.sigmoid(x)


GP = BF16


def _operands(a, b, prec):
    if prec is BF16:
        return a.astype(BF16), b.astype(BF16), None
    return a, b, prec


def _dot(a, b, prec=None):
    a, b, prec = _operands(a, b, prec)
    return jnp.dot(a, b, preferred_element_type=F32, precision=prec)


def _dot_nt(a, b, prec=None):
    a, b, prec = _operands(a, b, prec)
    return lax.dot_general(a, b, (((1,), (1,)), ((), ())), preferred_element_type=F32, precision=prec)


def _dot_tn(a, b, prec=None):
    a, b, prec = _operands(a, b, prec)
    return lax.dot_general(a, b, (((0,), (0,)), ((), ())), preferred_element_type=F32, precision=prec)


def _lockstep(gens):
    gens = list(gens)
    while gens:
        alive = []
        for g in gens:
            try:
                next(g)
                alive.append(g)
            except StopIteration:
                pass
        gens = alive


def _rowsum(x):
    return jnp.sum(x, axis=-1, keepdims=True)


def _colsum(x):
    return jnp.sum(x, axis=0, keepdims=True)


def _mod(mod_ref, b_ref, k):
    return mod_ref[:, k * D:(k + 1) * D] + b_ref[:, k * D:(k + 1) * D]


def _const(shape):
    nd = len(shape)
    return pl.BlockSpec(shape, lambda *_: (0,) * nd)


def _const1(shape):
    nd = len(shape)
    return pl.BlockSpec(shape, lambda *_: (0,) * nd, pipeline_mode=pl.Buffered(1))


PEER_FLIPS = [(dx, dy, dc) for dx in (0, 1) for dy in (0, 1) for dc in (0, 1)][1:]


def _after(x, token):
    return x + token[0:1, 0:1].astype(x.dtype).reshape((1,) * x.ndim)


def _exchange(name, srcs, per_dest, seed_only=(), with_token=False):
    n = len(srcs)
    out_shape = []
    for a, pd in zip(srcs, per_dest):
        blk = a.shape[1:] if pd else a.shape
        out_shape.append(jax.ShapeDtypeStruct((N_DEV,) + tuple(blk), a.dtype))

    def body(*refs):
        src = refs[:n]
        dst = refs[n:2 * n]
        send_sems, recv_sems, local_sems = refs[-3:]
        if with_token:
            refs[2 * n][...] = jnp.zeros((8, LANES), F32)
        x, y, c = lax.axis_index("x"), lax.axis_index("y"), lax.axis_index("c")
        me = 4 * x + 2 * y + c

        def piece(i, j):
            return src[i].at[j] if per_dest[i] else src[i]

        copies = []
        for k, (dx, dy, dc) in enumerate(PEER_FLIPS):
            px = 1 - x if dx else x
            py = 1 - y if dy else y
            pc = 1 - c if dc else c
            pj = 4 * px + 2 * py + pc
            for i in range(n):
                if i in seed_only:
                    continue
                cp = pltpu.make_async_remote_copy(
                    src_ref=piece(i, pj), dst_ref=dst[i].at[me],
                    send_sem=send_sems.at[k * n + i], recv_sem=recv_sems.at[k * n + i],
                    device_id=(px, py, pc), device_id_type=MESH)
                cp.start()
                arrive = pltpu.make_async_remote_copy(
                    src_ref=piece(i, pj), dst_ref=dst[i].at[pj],
                    send_sem=send_sems.at[k * n + i], recv_sem=recv_sems.at[k * n + i],
                    device_id=(px, py, pc), device_id_type=MESH)
                copies.append((cp, arrive))
        own = []
        for i in range(n):
            lc = pltpu.make_async_copy(piece(i, me), dst[i].at[me], local_sems.at[i])
            lc.start()
            own.append(lc)
        for cp, arrive in copies:
            arrive.wait_recv()
        for cp, arrive in copies:
            cp.wait_send()
        for lc in own:
            lc.wait()

    any_spec = pl.BlockSpec(memory_space=pl.ANY)
    out_specs = [any_spec] * n
    if with_token:
        out_shape.append(jax.ShapeDtypeStruct((8, LANES), F32))
        out_specs.append(pl.BlockSpec(memory_space=pltpu.VMEM))
    return pl.pallas_call(
        body, name=name, out_shape=tuple(out_shape),
        in_specs=[any_spec] * n, out_specs=tuple(out_specs),
        scratch_shapes=[pltpu.SemaphoreType.DMA((7 * n,)), pltpu.SemaphoreType.DMA((7 * n,)),
                        pltpu.SemaphoreType.DMA((n,))],
        compiler_params=pltpu.CompilerParams(has_side_effects=True),
    )(*srcs)


HBM_SPEC = pl.BlockSpec(memory_space=pltpu.HBM)
SEM_SPEC = pl.BlockSpec(memory_space=pltpu.SEMAPHORE)
DATAFLOW = pltpu.SideEffectType.DATAFLOW_SIDE_EFFECTING


def _peers():
    x, y, c = lax.axis_index("x"), lax.axis_index("y"), lax.axis_index("c")
    out = []
    for k, (dx, dy, dc) in enumerate(PEER_FLIPS):
        px = 1 - x if dx else x
        py = 1 - y if dy else y
        pc = 1 - c if dc else c
        out.append((k, (px, py, pc), 4 * px + 2 * py + pc))
    return 4 * x + 2 * y + c, out


def _exchange_start(name, srcs, lands, per_dest):
    n = len(srcs)

    def body(*refs):
        src, land = refs[:n], refs[n:2 * n]
        send_sems, recv_sems = refs[2 * n], refs[2 * n + 1]
        token = refs[-1]
        me, peers = _peers()
        for k, peer, pj in peers:
            for i in range(n):
                pltpu.make_async_remote_copy(
                    src_ref=src[i].at[pj] if per_dest[i] else src[i], dst_ref=land[i].at[me],
                    send_sem=send_sems.at[k * n + i], recv_sem=recv_sems.at[k * n + i],
                    device_id=peer, device_id_type=MESH).start()
        token[...] = jnp.zeros((8, LANES), F32)

    arrays = list(srcs) + list(lands)
    return pl.pallas_call(
        body, name=name,
        out_shape=(pltpu.SemaphoreType.DMA((7 * n,)), pltpu.SemaphoreType.DMA((7 * n,)),
                   *[pltpu.HBM(a.shape, a.dtype) for a in arrays], jax.ShapeDtypeStruct((8, LANES), F32)),
        in_specs=[HBM_SPEC] * (2 * n),
        out_specs=(SEM_SPEC, SEM_SPEC, *[HBM_SPEC] * (2 * n), pl.BlockSpec(memory_space=pltpu.VMEM)),
        input_output_aliases={i: 2 + i for i in range(2 * n)},
        compiler_params=pltpu.CompilerParams(has_side_effects=DATAFLOW),
    )(*[pltpu.with_memory_space_constraint(a, pltpu.HBM) for a in arrays])


def _exchange_wait(name, started, per_dest, after):
    n = (len(started) - 3) // 2
    send_sems, recv_sems = started[0], started[1]
    arrays = list(started[2:2 + 2 * n])

    def body(*refs):
        src, land = refs[:n], refs[n:2 * n]
        send, recv = refs[2 * n], refs[2 * n + 1]
        me, peers = _peers()
        for k, peer, pj in peers:
            for i in range(n):
                cp = pltpu.make_async_remote_copy(
                    src_ref=src[i].at[pj] if per_dest[i] else src[i], dst_ref=land[i].at[pj],
                    send_sem=send.at[k * n + i], recv_sem=recv.at[k * n + i],
                    device_id=peer, device_id_type=MESH)
                cp.wait_send()
                cp.wait_recv()

    outs = pl.pallas_call(
        body, name=name,
        out_shape=tuple(pltpu.HBM(a.shape, a.dtype) for a in arrays),
        in_specs=[HBM_SPEC] * (2 * n) + [SEM_SPEC, SEM_SPEC, pl.BlockSpec(memory_space=pl.ANY)],
        out_specs=tuple([HBM_SPEC] * (2 * n)),
        input_output_aliases={i: i for i in range(2 * n)},
        compiler_params=pltpu.CompilerParams(has_side_effects=DATAFLOW),
    )(*arrays, send_sems, recv_sems, after)
    return outs[:n], outs[n:]


def _mod_shard(c_all, w_ada):
    def body(c_ref, w_ref, o_ref):
        cv = c_ref[...]
        ca = cv * _sig(cv)
        o_ref[...] = _dot(ca.astype(BF16), w_ref[...].astype(BF16))

    return pl.pallas_call(
        body, name="mod_shard", out_shape=jax.ShapeDtypeStruct((N_DEV, w_ada.shape[1]), F32),
        compiler_params=_params(),
    )(c_all, w_ada)


def _fwd_in(x, nw1, modnb, bada, w_main, w_ba):
    def body(x_ref, nw_ref, mod_ref, b_ref, wm_ref, wb_ref, pm_ref, pb_ref, hb_ref):
        xv = x_ref[...]
        r = lax.rsqrt(jnp.mean(xv * xv, axis=-1, keepdims=True) + EPS)
        h = (xv * r * nw_ref[...]) * (1.0 + _mod(mod_ref, b_ref, 1)) + _mod(mod_ref, b_ref, 0)
        hb = h.astype(BF16)
        hb_ref[...] = hb
        pm_ref[...] = _dot_nt(hb, wm_ref[...])
        pb_ref[...] = _dot_nt(hb, wb_ref[...])

    return pl.pallas_call(
        body, name="fwd_in", grid=(NT,),
        in_specs=[pl.BlockSpec((TM, D), lambda i: (i, 0)), _const((1, D)), _const((1, 6 * D)), _const((1, 6 * D)),
                  _const((NMAIN, D)), _const((LANES, D))],
        out_specs=(pl.BlockSpec((TM, NMAIN), lambda i: (i, 0)), pl.BlockSpec((TM, LANES), lambda i: (i, 0)),
                   pl.BlockSpec((TM, D), lambda i: (i, 0))),
        out_shape=(jax.ShapeDtypeStruct((S, NMAIN), F32), jax.ShapeDtypeStruct((S, LANES), F32),
                   jax.ShapeDtypeStruct((S, D), BF16)),
        compiler_params=_params(dimension_semantics=("arbitrary",)),
    )(x, nw1, modnb, bada, w_main, w_ba)


def _group_mean_matrix():
    ii = lax.broadcasted_iota(jnp.int32, (CW, CW), 0) // GSZ
    jj = lax.broadcasted_iota(jnp.int32, (CW, CW), 1) // GSZ
    return jnp.where(ii == jj, 1.0 / GSZ, 0.0).astype(F32)


def _conf_fwd(p_main, conv_w, conv_b, gn_w, gn_b):
    def body(a_ref, g_ref, w_ref, b_ref, gw_ref, gb_ref, y_ref, oa_ref, ubuf):
        i = pl.program_id(0)

        @pl.when(i == 0)
        def _():
            ubuf[0:HALO, :] = jnp.zeros((HALO, CW), F32)

        ubuf[HALO:HALO + TM, :] = a_ref[...] * _sig(g_ref[...])
        acc = jnp.zeros((TM, CW), F32) + b_ref[...]
        for k in range(KC):
            off = HALO - (KC - 1) + k
            acc = acc + w_ref[k:k + 1, :] * ubuf[off:off + TM, :]
        y_ref[...] = acc
        ubuf[0:HALO, :] = ubuf[TM:TM + HALO, :]
        pm = _group_mean_matrix()
        dlt = acc - _dot(acc, pm, HI)
        var = _dot(dlt * dlt, pm, HI)
        o = dlt * lax.rsqrt(var + EPS) * gw_ref[...] + gb_ref[...]
        oa_ref[...] = o * _sig(o)

    return pl.pallas_call(
        body, name="conf_fwd", grid=(NT,),
        in_specs=[pl.BlockSpec((TM, CW), lambda i: (i, 0)), pl.BlockSpec((TM, CW), lambda i: (i, 1)),
                  _const((KC, CW)), _const((1, CW)), _const((1, CW)), _const((1, CW))],
        out_specs=(pl.BlockSpec((TM, CW), lambda i: (i, 0)), pl.BlockSpec((TM, CW), lambda i: (i, 0))),
        out_shape=(jax.ShapeDtypeStruct((S, CW), F32), jax.ShapeDtypeStruct((S, CW), F32)),
        scratch_shapes=[pltpu.VMEM((HALO + TM, CW), F32)],
        compiler_params=_params(dimension_semantics=("arbitrary",)),
    )(p_main, p_main, conv_w, conv_b, gn_w, gn_b)


def _tri_iota():
    ii = lax.broadcasted_iota(jnp.int32, (CL, CL), 0)
    jj = lax.broadcasted_iota(jnp.int32, (CL, CL), 1)
    return ii, jj


def _gdn_gates(ba, alog_l, dt_l):
    beta_all = _sig(ba)
    xg = ba + dt_l
    sp = jnp.maximum(xg, 0.0) + jnp.log(1.0 + jnp.exp(-jnp.abs(xg)))
    neg_a = -jnp.exp(alog_l)
    return beta_all, neg_a * sp, xg, neg_a


def _gdn_cumsum(g_all):
    ii, jj = _tri_iota()
    low = jnp.where(ii >= jj, 1.0, 0.0).astype(F32)
    gcum = _dot(low, g_all, HI)
    return gcum, jnp.transpose(gcum)


def _unit_lower_inverses(mats):
    ii, jj = _tri_iota()
    eye = jnp.where(ii == jj, 1.0, 0.0).astype(F32)
    ts = [eye - a for a in mats]
    ps = [_dot(a, a, HI) for a in mats]
    for _ in range(4):
        ts = [t + _dot(t, p, HI) for t, p in zip(ts, ps)]
        ps = [_dot(p, p, HI) for p in ps]
    return [t + _dot(t, p, HI) for t, p in zip(ts, ps)]


def _head_terms(qh, kh, beta, gcol, grow):
    ii, jj = _tri_iota()
    causal = ii >= jj
    strict = ii > jj
    rq = lax.rsqrt(_rowsum(qh * qh) + EPS)
    rk = lax.rsqrt(_rowsum(kh * kh) + EPS)
    qn = qh * rq
    kn = kh * rk
    qs = qn * QSCALE
    decay = jnp.where(causal, jnp.exp(jnp.where(causal, gcol - grow, 0.0)), 0.0)
    gam = jnp.exp(gcol)
    gl = gcol[CL - 1:CL, :]
    kds = jnp.exp(gl - gcol)
    cd = jnp.exp(gl)
    kb = kn * beta
    a = jnp.where(strict, _dot_nt(kb, kn, GP) * decay, 0.0)
    qk = jnp.where(causal, _dot_nt(qs, kn, GP) * decay, 0.0)
    return dict(rq=rq, rk=rk, qn=qn, kn=kn, qs=qs, decay=decay, gam=gam, kds=kds, cd=cd, kb=kb, a=a, qk=qk,
                causal=causal, strict=strict)


def _short_conv(w_ref, buf, rows=CL):
    acc = w_ref[0:1, :] * buf[SH - KS + 1:SH - KS + 1 + rows, :]
    for k in range(1, KS):
        off = SH - (KS - 1) + k
        acc = acc + w_ref[k:k + 1, :] * buf[off:off + rows, :]
    return acc


CPS = 4
TG = CPS * CL


def _gdn_prep(p_main, p_ba, gdn_conv_w, alog_l, dt_l):
    def body(q_ref, k_ref, v_ref, qh_ref, kh_ref, vh_ref, ba_ref, w_ref, al_ref, dt_ref,
             wo_ref, uo_ref, qg_ref, kd_ref, qk_ref, cd_ref, t_ref, xbuf):
        i = pl.program_id(0)
        first = i == 0
        xbuf[0:SH, 0:GW] = jnp.where(first, 0.0, qh_ref[...])
        xbuf[0:SH, GW:2 * GW] = jnp.where(first, 0.0, kh_ref[...])
        xbuf[0:SH, 2 * GW:3 * GW] = jnp.where(first, 0.0, vh_ref[...])
        xbuf[SH:SH + TG, 0:GW] = q_ref[...]
        xbuf[SH:SH + TG, GW:2 * GW] = k_ref[...]
        xbuf[SH:SH + TG, 2 * GW:3 * GW] = v_ref[...]
        conv = _short_conv(w_ref, xbuf, TG)
        qkv = conv * _sig(conv)
        beta_all, g_all, _, _ = _gdn_gates(ba_ref[...], al_ref[...], dt_ref[...])
        lane = lax.broadcasted_iota(jnp.int32, (8, LANES), 1)
        cums = [_gdn_cumsum(g_all[cc * CL:(cc + 1) * CL, :]) for cc in range(CPS)]
        pairs = [(cc, h) for cc in range(CPS) for h in range(NH)]
        terms, vbs = [], []
        for cc, h in pairs:
            r0, lo = cc * CL, h * DH
            beta = beta_all[r0:r0 + CL, h:h + 1]
            gcum, gcum_t = cums[cc]
            terms.append(_head_terms(qkv[r0:r0 + CL, lo:lo + DH], qkv[r0:r0 + CL, GW + lo:GW + lo + DH], beta,
                                     gcum[:, NH + h:NH + h + 1], gcum_t[NH + h:NH + h + 1, :]))
            vbs.append(qkv[r0:r0 + CL, 2 * GW + lo:2 * GW + lo + DH] * beta)
        invs = _unit_lower_inverses([f["a"] for f in terms])
        cds = [jnp.zeros((8, LANES), F32) for _ in range(CPS)]
        for (cc, h), f, t, vb in zip(pairs, terms, invs, vbs):
            r0, lo = cc * CL, h * DH
            t_ref[cc, h] = t
            uo_ref[r0:r0 + CL, lo:lo + DH] = _dot(t, vb, GP)
            wo_ref[r0:r0 + CL, lo:lo + DH] = _dot(t, f["kb"] * f["gam"], GP).astype(BF16)
            qg_ref[r0:r0 + CL, lo:lo + DH] = (f["qs"] * f["gam"]).astype(BF16)
            kd_ref[r0:r0 + CL, lo:lo + DH] = (f["kn"] * f["kds"]).astype(BF16)
            qk_ref[cc, h] = f["qk"].astype(BF16)
            cds[cc] = cds[cc] + jnp.where(lane == h, f["cd"], 0.0)
        for cc in range(CPS):
            cd_ref[cc] = cds[cc]

    col = lambda j: pl.BlockSpec((TG, GW), lambda i: (i, j))
    halo = lambda j: pl.BlockSpec((SH, GW), lambda i: (jnp.maximum(i * (TG // SH) - 1, 0), j))
    tile = lambda: pl.BlockSpec((TG, GW), lambda i: (i, 0))
    sq = lambda: pl.BlockSpec((CPS, NH, CL, CL), lambda i: (i, 0, 0, 0))
    return pl.pallas_call(
        body, name="gdn_prep", grid=(NCH // CPS,),
        in_specs=[col(2), col(3), col(4), halo(2), halo(3), halo(4), pl.BlockSpec((TG, LANES), lambda i: (i, 0)),
                  _const((KS, 3 * GW)), _const((1, LANES)), _const((1, LANES))],
        out_specs=(tile(), tile(), tile(), tile(), sq(), pl.BlockSpec((CPS, 8, LANES), lambda i: (i, 0, 0)), sq()),
        out_shape=(jax.ShapeDtypeStruct((S, GW), BF16), jax.ShapeDtypeStruct((S, GW), F32),
                   jax.ShapeDtypeStruct((S, GW), BF16), jax.ShapeDtypeStruct((S, GW), BF16),
                   jax.ShapeDtypeStruct((NCH, NH, CL, CL), BF16), jax.ShapeDtypeStruct((NCH, 8, LANES), F32),
                   jax.ShapeDtypeStruct((NCH, NH, CL, CL), F32)),
        scratch_shapes=[pltpu.VMEM((SH + TG, 3 * GW), F32)],
        compiler_params=_params(dimension_semantics=("arbitrary",)),
    )(p_main, p_main, p_main, p_main, p_main, p_main, p_ba, gdn_conv_w, alog_l, dt_l)


def _gdn_scan(w_o, u_o, qg, kd, qk, cd, p_main, gdn_nw):
    def body(w_ref, u_ref, qg_ref, kd_ref, qk_ref, cd_ref, z_ref, nw_ref, ob_ref, o_ref, sin_ref, state):
        n = pl.program_id(0)

        @pl.when(n == 0)
        def _():
            state[...] = jnp.zeros((NH, DH, DH), F32)

        def head(h):
            lo = h * DH
            st = state[h]
            sin_ref[0, h] = st
            sb = st.astype(BF16)
            v_new = u_ref[:, lo:lo + DH] - _dot(w_ref[:, lo:lo + DH], sb)
            yield
            vb = v_new.astype(BF16)
            o = _dot(qg_ref[:, lo:lo + DH], sb) + _dot(qk_ref[0, h], vb)
            state[h] = st * cd_ref[0, 0:1, h:h + 1] + _dot_tn(kd_ref[:, lo:lo + DH], vb)
            yield
            o_ref[:, lo:lo + DH] = o
            r = lax.rsqrt(jnp.mean(o * o, axis=-1, keepdims=True) + EPS)
            zh = z_ref[:, lo:lo + DH]
            ob_ref[:, lo:lo + DH] = o * r * nw_ref[...] * (zh * _sig(zh))

        _lockstep(head(h) for h in range(NH))

    tile = lambda: pl.BlockSpec((CL, GW), lambda n: (n, 0))
    return pl.pallas_call(
        body, name="gdn_scan", grid=(NCH,),
        in_specs=[tile(), tile(), tile(), tile(), pl.BlockSpec((1, NH, CL, CL), lambda n: (n, 0, 0, 0)),
                  pl.BlockSpec((1, 8, LANES), lambda n: (n, 0, 0)), pl.BlockSpec((CL, GW), lambda n: (n, 5)),
                  _const((1, DH))],
        out_specs=(tile(), tile(), pl.BlockSpec((1, NH, DH, DH), lambda n: (n, 0, 0, 0))),
        out_shape=(jax.ShapeDtypeStruct((S, GW), F32), jax.ShapeDtypeStruct((S, GW), F32),
                   jax.ShapeDtypeStruct((NCH, NH, DH, DH), F32)),
        scratch_shapes=[pltpu.VMEM((NH, DH, DH), F32)],
        compiler_params=_params(dimension_semantics=("arbitrary",)),
    )(w_o, u_o, qg, kd, qk, cd, p_main, gdn_nw)


def _fwd_out(out_a, out_b, x, modnb, bada, w_out):
    def body(oa_ref, ob_ref, x_ref, mod_ref, b_ref, w_ref, x1_ref, mix_ref, oab_ref):
        oa = oa_ref[...].astype(BF16)
        ob = ob_ref[...].astype(BF16)
        oab_ref[:, 0:CW] = oa
        oab_ref[:, CW:D] = ob
        mix = _dot(oa, w_ref[0:CW, :]) + _dot(ob, w_ref[CW:D, :])
        mix_ref[...] = mix
        x1_ref[...] = x_ref[...] + _mod(mod_ref, b_ref, 2) * mix

    tile = lambda w: pl.BlockSpec((TM, w), lambda i: (i, 0))
    return pl.pallas_call(
        body, name="fwd_out", grid=(NT,),
        in_specs=[tile(CW), tile(GW), tile(D), _const((1, 6 * D)), _const((1, 6 * D)), _const((D, D))],
        out_specs=(tile(D), tile(D), tile(D)),
        out_shape=(jax.ShapeDtypeStruct((S, D), F32), jax.ShapeDtypeStruct((S, D), F32),
                   jax.ShapeDtypeStruct((S, D), BF16)),
        compiler_params=_params(dimension_semantics=("arbitrary",)),
    )(out_a, out_b, x, modnb, bada, w_out)


TF = 128
FFN_STATS = 8


def _ffn_fwd_bwd(x1, tgt, modnb, bada, nw2, nfw, w_fi, w_fo):
    def body(x1_ref, tgt_ref, mod_ref, b_ref, nw2_ref, nfw_ref, wi_ref, wo_ref,
             dx1_ref, hb_ref, act_ref, dffn_ref, df_ref, st_ref):
        i = pl.program_id(0)

        @pl.when(i == 0)
        def _():
            st_ref[...] = jnp.zeros((FFN_STATS, D), F32)

        sh2, sc2, gt2 = _mod(mod_ref, b_ref, 3), _mod(mod_ref, b_ref, 4), _mod(mod_ref, b_ref, 5)
        x1v = x1_ref[...]
        r2 = lax.rsqrt(jnp.mean(x1v * x1v, axis=-1, keepdims=True) + EPS)
        xr2 = x1v * r2
        xn2 = xr2 * nw2_ref[...]
        hb = (xn2 * (1.0 + sc2) + sh2).astype(BF16)
        hb_ref[...] = hb
        fg, fu, sg = [], [], []
        ffn = jnp.zeros((TF, D), F32)
        for j in range(4):
            fgj = _dot_nt(hb, wi_ref[j])
            fuj = _dot_nt(hb, wi_ref[j + 4])
            sj = _sig(fgj)
            aj = (fgj * sj * fuj).astype(BF16)
            act_ref[j] = aj
            ffn = ffn + _dot(aj, wo_ref[j])
            fg.append(fgj)
            fu.append(fuj)
            sg.append(sj)
        x2 = x1v + gt2 * ffn
        r3 = lax.rsqrt(jnp.mean(x2 * x2, axis=-1, keepdims=True) + EPS)
        xr3 = x2 * r3
        err = xr3 * nfw_ref[...] - tgt_ref[...]
        loss = 0.5 * jnp.sum(jnp.mean(err * err, axis=-1, keepdims=True), axis=0, keepdims=True)
        dy = err * (1.0 / D)
        st_ref[0:1, :] += _colsum(dy * xr3)
        dyr = dy * nfw_ref[...]
        dx2 = r3 * (dyr - xr3 * jnp.mean(dyr * xr3, axis=-1, keepdims=True))
        st_ref[1:2, :] += _colsum(dx2 * ffn)
        st_ref[5:6, :] += jnp.broadcast_to(loss, (1, D))
        dffn = (gt2 * dx2).astype(BF16)
        dffn_ref[...] = dffn
        dh = jnp.zeros((TF, D), F32)
        for j in range(4):
            dact = _dot_nt(dffn, wo_ref[j])
            dfg = (dact * fu[j] * (sg[j] * (1.0 + fg[j] * (1.0 - sg[j])))).astype(BF16)
            dfu = (dact * (fg[j] * sg[j])).astype(BF16)
            df_ref[j] = dfg
            df_ref[j + 4] = dfu
            dh = dh + _dot(dfg, wi_ref[j]) + _dot(dfu, wi_ref[j + 4])
        st_ref[2:3, :] += _colsum(dh)
        st_ref[3:4, :] += _colsum(dh * xn2)
        dxn = dh * (1.0 + sc2)
        st_ref[4:5, :] += _colsum(dxn * xr2)
        dxr = dxn * nw2_ref[...]
        dx1_ref[...] = dx2 + r2 * (dxr - xr2 * jnp.mean(dxr * xr2, axis=-1, keepdims=True))

    tile = lambda w: pl.BlockSpec((TF, w), lambda i: (i, 0))
    return pl.pallas_call(
        body, name="ffn_fwd_bwd", grid=(S // TF,),
        in_specs=[tile(D), tile(D), _const((1, 6 * D)), _const((1, 6 * D)), _const((1, D)), _const((1, D)),
                  _const1((N_DEV, FB, D)), _const1((4, FB, D))],
        out_specs=(tile(D), tile(D), pl.BlockSpec((4, TF, FB), lambda i: (0, i, 0)), tile(D),
                   pl.BlockSpec((N_DEV, TF, FB), lambda i: (0, i, 0)), _const((FFN_STATS, D))),
        out_shape=(jax.ShapeDtypeStruct((S, D), F32), jax.ShapeDtypeStruct((S, D), BF16),
                   jax.ShapeDtypeStruct((4, S, FB), BF16), jax.ShapeDtypeStruct((S, D), BF16),
                   jax.ShapeDtypeStruct((N_DEV, S, FB), BF16), jax.ShapeDtypeStruct((FFN_STATS, D), F32)),
        compiler_params=_params(44, dimension_semantics=("arbitrary",)),
    )(x1, tgt, modnb, bada, nw2, nfw, w_fi, w_fo)


def _grad_w(name, a, b, nb):
    m, n = a.shape[1], b.shape[1]

    def body(a_ref, b_ref, o_ref):
        o_ref[...] = _dot_tn(a_ref[...], b_ref[...]).astype(BF16)

    return pl.pallas_call(
        body, name=name, grid=(m // nb,),
        in_specs=[pl.BlockSpec((S, nb), lambda j: (0, j)), _const((S, n))],
        out_specs=pl.BlockSpec((nb, n), lambda j: (j, 0)),
        out_shape=jax.ShapeDtypeStruct((m, n), BF16),
        compiler_params=_params(dimension_semantics=("arbitrary",)),
    )(a, b)


def _grad_w_ffn_in(hb2, df):
    def body(a_ref, b_ref, o_ref):
        o_ref[0] = _dot_tn(b_ref[0], a_ref[...]).astype(BF16)

    return pl.pallas_call(
        body, name="grad_w_ffn_in", grid=(N_DEV,),
        in_specs=[_const((S, D)), pl.BlockSpec((1, S, FB), lambda j: (j, 0, 0))],
        out_specs=pl.BlockSpec((1, FB, D), lambda j: (j, 0, 0)),
        out_shape=jax.ShapeDtypeStruct((N_DEV, FB, D), BF16),
        compiler_params=_params(dimension_semantics=("arbitrary",)),
    )(hb2, df)


def _grad_w_ffn_out(act, dffn):
    def body(a_ref, b_ref, o_ref):
        o_ref[0] = _dot_tn(a_ref[0], b_ref[...]).astype(BF16)

    return pl.pallas_call(
        body, name="grad_w_ffn_out", grid=(4,),
        in_specs=[pl.BlockSpec((1, S, FB), lambda j: (j, 0, 0)), _const((S, D))],
        out_specs=pl.BlockSpec((1, FB, D), lambda j: (j, 0, 0)),
        out_shape=jax.ShapeDtypeStruct((4, FB, D), BF16),
        compiler_params=_params(dimension_semantics=("arbitrary",)),
    )(act, dffn)


def _bwd_out(dx1, mix, modnb, bada, w_out):
    def body(dx_ref, mix_ref, mod_ref, b_ref, w_ref, dmix_ref, doa_ref, dob_ref, st_ref):
        i = pl.program_id(0)

        @pl.when(i == 0)
        def _():
            st_ref[...] = jnp.zeros((8, D), F32)

        dx = dx_ref[...]
        st_ref[0:1, :] += _colsum(dx * mix_ref[...])
        dmix = (_mod(mod_ref, b_ref, 2) * dx).astype(BF16)
        dmix_ref[...] = dmix
        doa_ref[...] = _dot_nt(dmix, w_ref[0:CW, :])
        dob_ref[...] = _dot_nt(dmix, w_ref[CW:D, :])

    tile = lambda w: pl.BlockSpec((TM, w), lambda i: (i, 0))
    return pl.pallas_call(
        body, name="bwd_out", grid=(NT,),
        in_specs=[tile(D), tile(D), _const((1, 6 * D)), _const((1, 6 * D)), _const((D, D))],
        out_specs=(tile(D), tile(CW), tile(GW), _const((8, D))),
        out_shape=(jax.ShapeDtypeStruct((S, D), BF16), jax.ShapeDtypeStruct((S, CW), F32),
                   jax.ShapeDtypeStruct((S, GW), F32), jax.ShapeDtypeStruct((8, D), F32)),
        compiler_params=_params(dimension_semantics=("arbitrary",)),
    )(dx1, mix, modnb, bada, w_out)


CONF_STATS = 40


def _conf_bwd(d_out_a, y, p_main, conv_w, gn_w, gn_b):
    def body(do_ref, y_ref, a_ref, g_ref, ah_ref, gh_ref, w_ref, gw_ref, gb_ref, dp_ref, st_ref, ubuf, dybuf):
        i = pl.program_id(0)

        @pl.when(i == 0)
        def _():
            st_ref[...] = jnp.zeros((CONF_STATS, CW), F32)
            dybuf[TM:TM + HALO, :] = jnp.zeros((HALO, CW), F32)

        pm = _group_mean_matrix()
        yv = y_ref[...]
        dlt = yv - _dot(yv, pm, HI)
        rstd = lax.rsqrt(_dot(dlt * dlt, pm, HI) + EPS)
        un = dlt * rstd
        o = un * gw_ref[...] + gb_ref[...]
        so = _sig(o)
        d_o = do_ref[...] * (so * (1.0 + o * (1.0 - so)))
        st_ref[33:34, :] += _colsum(d_o)
        st_ref[32:33, :] += _colsum(d_o * un)
        dun = d_o * gw_ref[...]
        dy = rstd * (dun - _dot(dun, pm, HI) - un * _dot(dun * un, pm, HI))
        st_ref[31:32, :] += _colsum(dy)
        dybuf[0:TM, :] = dy

        a = a_ref[...]
        sg = _sig(g_ref[...])
        first = i == NT - 1
        ubuf[0:HALO, :] = jnp.where(first, 0.0, ah_ref[...] * _sig(gh_ref[...]))
        ubuf[HALO:HALO + TM, :] = a * sg
        du = jnp.zeros((TM, CW), F32)
        for k in range(KC):
            off = HALO - (KC - 1) + k
            st_ref[k:k + 1, :] += _colsum(dy * ubuf[off:off + TM, :])
            du = du + w_ref[k:k + 1, :] * dybuf[KC - 1 - k:KC - 1 - k + TM, :]
        dybuf[TM:TM + HALO, :] = dybuf[0:HALO, :]
        dp_ref[:, 0:CW] = (du * sg).astype(BF16)
        dp_ref[:, CW:2 * CW] = (du * a * sg * (1.0 - sg)).astype(BF16)

    rev = lambda w, j=0: pl.BlockSpec((TM, w), lambda i: (NT - 1 - i, j))
    halo = lambda j: pl.BlockSpec((HALO, CW), lambda i: (jnp.maximum((NT - 1 - i) * (TM // HALO) - 1, 0), j))
    return pl.pallas_call(
        body, name="conf_bwd", grid=(NT,),
        in_specs=[rev(CW), rev(CW), rev(CW, 0), rev(CW, 1), halo(0), halo(1),
                  _const((KC, CW)), _const((1, CW)), _const((1, CW))],
        out_specs=(rev(2 * CW), _const((CONF_STATS, CW))),
        out_shape=(jax.ShapeDtypeStruct((S, 2 * CW), BF16), jax.ShapeDtypeStruct((CONF_STATS, CW), F32)),
        scratch_shapes=[pltpu.VMEM((HALO + TM, CW), F32), pltpu.VMEM((TM + HALO, CW), F32)],
        compiler_params=_params(dimension_semantics=("arbitrary",)),
    )(d_out_a, y, p_main, p_main, p_main, p_main, conv_w, gn_w, gn_b)


GDN_STATS = 8


def _gdn_bwd(d_out_b, o_pre, s_in, t_inv, p_main, p_ba, gdn_conv_w, alog_l, dt_l, gdn_nw):
    def body(dob_ref, o_ref, sin_ref, t_ref, q_ref, k_ref, v_ref, z_ref, qh_ref, kh_ref, vh_ref, ba_ref,
             w_ref, al_ref, dt_ref, nw_ref, dp_ref, dba_ref, st_ref, xbuf, dcbuf, dstate):
        n = pl.program_id(0)

        @pl.when(n == 0)
        def _():
            st_ref[...] = jnp.zeros((GDN_STATS, 3 * GW), F32)
            dcbuf[CL:CL + SH, :] = jnp.zeros((SH, 3 * GW), F32)
            dstate[...] = jnp.zeros((NH, DH, DH), F32)

        first = n == NCH - 1
        xbuf[0:SH, 0:GW] = jnp.where(first, 0.0, qh_ref[...])
        xbuf[0:SH, GW:2 * GW] = jnp.where(first, 0.0, kh_ref[...])
        xbuf[0:SH, 2 * GW:3 * GW] = jnp.where(first, 0.0, vh_ref[...])
        xbuf[SH:SH + CL, 0:GW] = q_ref[...]
        xbuf[SH:SH + CL, GW:2 * GW] = k_ref[...]
        xbuf[SH:SH + CL, 2 * GW:3 * GW] = v_ref[...]
        conv = _short_conv(w_ref, xbuf)
        sc = _sig(conv)
        qkv = conv * sc
        ba = ba_ref[...]
        beta_all, g_all, xg, neg_a = _gdn_gates(ba, al_ref[...], dt_ref[...])
        gcum, gcum_t = _gdn_cumsum(g_all)
        lane = lax.broadcasted_iota(jnp.int32, (CL, LANES), 1)
        row = lax.broadcasted_iota(jnp.int32, (CL, 1), 0)
        acc = dict(dgcum=jnp.zeros((CL, LANES), F32), dbeta=jnp.zeros((CL, LANES), F32))

        def head(h):
            lo = h * DH
            qh = qkv[:, lo:lo + DH]
            kh = qkv[:, GW + lo:GW + lo + DH]
            vh = qkv[:, 2 * GW + lo:2 * GW + lo + DH]
            beta = beta_all[:, h:h + 1]
            f = _head_terms(qh, kh, beta, gcum[:, NH + h:NH + h + 1], gcum_t[NH + h:NH + h + 1, :])
            qn, kn, qs, kb, gam, kds, cd, decay = (f[s] for s in ("qn", "kn", "qs", "kb", "gam", "kds", "cd", "decay"))
            t = t_ref[0, h]
            st = sin_ref[0, h]
            vb = vh * beta
            kbg = kb * gam
            u = _dot(t, vb, GP)
            w = _dot(t, kbg, GP)
            yield
            v_new = u - _dot(w, st, GP)
            q_dec = qs * gam
            k_dec = kn * kds

            o = o_ref[:, lo:lo + DH]
            zh = z_ref[:, lo:lo + DH]
            sz = _sig(zh)
            r = lax.rsqrt(jnp.mean(o * o, axis=-1, keepdims=True) + EPS)
            orr = o * r
            d_out = dob_ref[:, lo:lo + DH]
            dz = d_out * (orr * nw_ref[...]) * (sz * (1.0 + zh * (1.0 - sz)))
            don = d_out * (zh * sz)
            st_ref[4:5, 0:DH] += _colsum(don * orr)
            tt = don * nw_ref[...]
            d_o = r * (tt - orr * jnp.mean(tt * orr, axis=-1, keepdims=True))

            yield
            ds_out = dstate[h]
            dv_new = _dot_tn(f["qk"], d_o, GP) + _dot(k_dec, ds_out, GP)
            dqk = jnp.where(f["causal"], _dot_nt(d_o, v_new, GP), 0.0)
            dq_dec = _dot_nt(d_o, st, GP)
            dk_dec = _dot_nt(v_new, ds_out, GP)
            yield
            dstate[h] = _dot_tn(q_dec, d_o, GP) + cd * ds_out - _dot_tn(w, dv_new, GP)
            dcd = jnp.sum(_rowsum(st * ds_out), axis=0, keepdims=True)
            dw = -_dot_nt(dv_new, st, GP)
            dvb = _dot_tn(t, dv_new, GP)
            yield
            dt_m = _dot_nt(dv_new, vb, GP) + _dot_nt(dw, kbg, GP)
            dkbg = _dot_tn(t, dw, GP)
            yield
            dtt = _dot_nt(dt_m, t, GP)
            yield
            da = jnp.where(f["strict"], -_dot_tn(t, dtt, GP), 0.0)
            yield
            dad = da * decay
            dqkd = dqk * decay
            dkb = _dot(dad, kn, GP) + dkbg * gam
            dkn = _dot_tn(dad, kb, GP) + _dot_tn(dqkd, qs, GP) + dk_dec * kds + dkb * beta
            dqs = _dot(dqkd, kn, GP) + dq_dec * gam
            yield
            m = da * f["a"] + dqk * f["qk"]
            tk = _rowsum(dk_dec * k_dec)
            dgl = jnp.sum(tk, axis=0, keepdims=True) + dcd * cd
            dgc = (_rowsum(m) - _rowsum(jnp.transpose(m)) + _rowsum(dq_dec * q_dec) - tk + _rowsum(dkbg * kbg)
                   + jnp.where(row == CL - 1, dgl, 0.0))
            dbeta = _rowsum(dkb * kn) + _rowsum(dvb * vh)
            acc["dgcum"] = acc["dgcum"] + jnp.where(lane == NH + h, dgc, 0.0)
            acc["dbeta"] = acc["dbeta"] + jnp.where(lane == h, dbeta, 0.0)
            dvh = dvb * beta
            dqn = dqs * QSCALE
            dqh = f["rq"] * (dqn - qn * _rowsum(dqn * qn))
            dkh = f["rk"] * (dkn - kn * _rowsum(dkn * kn))
            dsilu = lambda c0: sc[:, c0:c0 + DH] * (1.0 + conv[:, c0:c0 + DH] * (1.0 - sc[:, c0:c0 + DH]))
            dcbuf[0:CL, lo:lo + DH] = dqh * dsilu(lo)
            dcbuf[0:CL, GW + lo:GW + lo + DH] = dkh * dsilu(GW + lo)
            dcbuf[0:CL, 2 * GW + lo:2 * GW + lo + DH] = dvh * dsilu(2 * GW + lo)
            dp_ref[:, 3 * GW + lo:3 * GW + lo + DH] = dz.astype(BF16)

        _lockstep(head(h) for h in range(NH))
        dgcum_all, dbeta_all = acc["dgcum"], acc["dbeta"]

        ii, jj = _tri_iota()
        upper = jnp.where(ii <= jj, 1.0, 0.0).astype(F32)
        dg_all = _dot(upper, dgcum_all, HI)
        dxg = dg_all * neg_a * _sig(xg)
        st_ref[5:6, 0:LANES] += _colsum(dg_all * g_all)
        st_ref[6:7, 0:LANES] += _colsum(dxg)
        dbl = dbeta_all * beta_all * (1.0 - beta_all)
        dba_ref[...] = jnp.where(lane < NH, dbl, jnp.where(lane < 2 * NH, dxg, 0.0)).astype(BF16)

        dconv = dcbuf[0:CL, :]
        dx = w_ref[0:1, :] * dcbuf[KS - 1:KS - 1 + CL, :]
        st_ref[0:1, :] += _colsum(dconv * xbuf[SH - KS + 1:SH - KS + 1 + CL, :])
        for k in range(1, KS):
            off = SH - (KS - 1) + k
            st_ref[k:k + 1, :] += _colsum(dconv * xbuf[off:off + CL, :])
            dx = dx + w_ref[k:k + 1, :] * dcbuf[KS - 1 - k:KS - 1 - k + CL, :]
        dcbuf[CL:CL + SH, :] = dcbuf[0:SH, :]
        dp_ref[:, 0:3 * GW] = dx.astype(BF16)

    rev = lambda w, j=0: pl.BlockSpec((CL, w), lambda n: (NCH - 1 - n, j))
    halo = lambda j: pl.BlockSpec((SH, GW), lambda n: (jnp.maximum((NCH - 1 - n) * (CL // SH) - 1, 0), j))
    blk4 = lambda a, b: pl.BlockSpec((1, NH, a, b), lambda n: (NCH - 1 - n, 0, 0, 0))
    return pl.pallas_call(
        body, name="gdn_bwd", grid=(NCH,),
        in_specs=[rev(GW), rev(GW), blk4(DH, DH), blk4(CL, CL), rev(GW, 2), rev(GW, 3), rev(GW, 4), rev(GW, 5),
                  halo(2), halo(3), halo(4), rev(LANES), _const((KS, 3 * GW)), _const((1, LANES)),
                  _const((1, LANES)), _const((1, DH))],
        out_specs=(rev(4 * GW), rev(LANES), _const((GDN_STATS, 3 * GW))),
        out_shape=(jax.ShapeDtypeStruct((S, 4 * GW), BF16), jax.ShapeDtypeStruct((S, LANES), BF16),
                   jax.ShapeDtypeStruct((GDN_STATS, 3 * GW), F32)),
        scratch_shapes=[pltpu.VMEM((SH + CL, 3 * GW), F32), pltpu.VMEM((CL + SH, 3 * GW), F32),
                        pltpu.VMEM((NH, DH, DH), F32)],
        compiler_params=_params(dimension_semantics=("arbitrary",)),
    )(d_out_b, o_pre, s_in, t_inv, p_main, p_main, p_main, p_main, p_main, p_main, p_main, p_ba,
      gdn_conv_w, alog_l, dt_l, gdn_nw)


def _bwd_in(dp_conf, dp_gdn, dp_ba, x, dx1, nw1, modnb, bada, w_main, w_ba):
    def body(dc_ref, dg_ref, db_ref, x_ref, dx1_ref, nw_ref, mod_ref, b_ref, wm_ref, wb_ref, gx_ref, st_ref):
        i = pl.program_id(0)

        @pl.when(i == 0)
        def _():
            st_ref[...] = jnp.zeros((8, D), F32)

        dh = (_dot(dc_ref[...], wm_ref[0:2 * CW, :]) + _dot(dg_ref[...], wm_ref[2 * CW:NMAIN, :])
              + _dot(db_ref[...], wb_ref[...]))
        xv = x_ref[...]
        r = lax.rsqrt(jnp.mean(xv * xv, axis=-1, keepdims=True) + EPS)
        xr = xv * r
        st_ref[0:1, :] += _colsum(dh)
        st_ref[1:2, :] += _colsum(dh * (xr * nw_ref[...]))
        dxn = dh * (1.0 + _mod(mod_ref, b_ref, 1))
        st_ref[2:3, :] += _colsum(dxn * xr)
        dxr = dxn * nw_ref[...]
        gx_ref[...] = dx1_ref[...] + r * (dxr - xr * jnp.mean(dxr * xr, axis=-1, keepdims=True))

    tile = lambda w: pl.BlockSpec((TM, w), lambda i: (i, 0))
    return pl.pallas_call(
        body, name="bwd_in", grid=(NT,),
        in_specs=[tile(2 * CW), tile(4 * GW), tile(LANES), tile(D), tile(D), _const((1, D)), _const((1, 6 * D)),
                  _const((1, 6 * D)), _const((NMAIN, D)), _const((LANES, D))],
        out_specs=(tile(D), _const((8, D))),
        out_shape=(jax.ShapeDtypeStruct((S, D), F32), jax.ShapeDtypeStruct((8, D), F32)),
        compiler_params=_params(dimension_semantics=("arbitrary",)),
    )(dp_conf, dp_gdn, dp_ba, x, dx1, nw1, modnb, bada, w_main, w_ba)


def _adamw(w, g, m, v):
    m = ADAM_B1 * m + (1.0 - ADAM_B1) * g
    v = ADAM_B2 * v + (1.0 - ADAM_B2) * (g * g)
    m_hat = m / BC1
    v_hat = v / BC2
    delta = -ADAM_LR * (m_hat / (jnp.sqrt(v_hat) + ADAM_EPS) + ADAM_WD * w)
    return delta, m, v


ADAM_BLOCK_BYTES = 6 * 1024 * 1024


def _adam_tile(rows, cols):
    padded = -(-cols // LANES) * LANES
    if N_DEV * rows * padded * 4 <= ADAM_BLOCK_BYTES:
        return rows, cols
    best = None
    for tr in range(16, rows, 16):
        if rows % tr == 0 and N_DEV * tr * padded * 4 <= ADAM_BLOCK_BYTES:
            best = tr
    if best is not None:
        return best, cols
    rows_padded = -(-rows // 16) * 16
    tc = LANES
    for cand in range(LANES, cols, LANES):
        if cols % cand == 0 and N_DEV * rows_padded * cand * 4 <= ADAM_BLOCK_BYTES:
            tc = cand
    return rows, tc


def _reduce_adam(name, parts, w, m, v, own=None):
    rows, cols = w.shape
    tr, tc = _adam_tile(rows, cols)

    def body(*refs):
        p_ref, w_ref, m_ref, v_ref = refs[:4]
        g_ref, d_ref, nm_ref, nv_ref = refs[-4:]
        if own is None:
            part = lambda j: p_ref[j].astype(F32)
        else:
            me = 4 * lax.axis_index("x") + 2 * lax.axis_index("y") + lax.axis_index("c")
            part = lambda j: jnp.where(me == j, refs[4][...], p_ref[j]).astype(F32)
        g = part(0)
        for j in range(1, N_DEV):
            g = g + part(j)
        g_ref[...] = g
        d_ref[...], nm_ref[...], nv_ref[...] = _adamw(w_ref[...], g, m_ref[...], v_ref[...])

    blk = pl.BlockSpec((tr, tc), lambda i, j: (i, j))
    sds = jax.ShapeDtypeStruct((rows, cols), F32)
    extra = [] if own is None else [own]
    return pl.pallas_call(
        body, name=name, grid=(rows // tr, cols // tc),
        in_specs=[pl.BlockSpec((N_DEV, tr, tc), lambda i, j: (0, i, j)), blk, blk, blk] + [blk] * len(extra),
        out_specs=(blk, blk, blk, blk), out_shape=(sds, sds, sds, sds),
        compiler_params=_params(dimension_semantics=("arbitrary", "arbitrary")),
    )(parts, w, m, v, *extra)


def _ada_adam(c_all, dmod_sh, w, m, v):
    rows, cols = w.shape
    tr = 256

    def body(c_ref, dm_ref, w_ref, m_ref, v_ref, g_ref, d_ref, nm_ref, nv_ref):
        cv = c_ref[...]
        g = _dot_tn(cv * _sig(cv), dm_ref[...], HI)
        g_ref[...] = g
        d_ref[...], nm_ref[...], nv_ref[...] = _adamw(w_ref[...], g, m_ref[...], v_ref[...])

    blk = pl.BlockSpec((tr, cols), lambda i: (i, 0))
    sds = jax.ShapeDtypeStruct((rows, cols), F32)
    return pl.pallas_call(
        body, name="ada_adam", grid=(rows // tr,),
        in_specs=[pl.BlockSpec((N_DEV, tr), lambda i: (0, i)), _const((N_DEV, cols)), blk, blk, blk],
        out_specs=(blk, blk, blk, blk), out_shape=(sds, sds, sds, sds),
        compiler_params=_params(dimension_semantics=("arbitrary",)),
    )(c_all, dmod_sh, w, m, v)


def _lanes(a, at=0):
    return jnp.pad(a, ((0, 0), (at, LANES - at - a.shape[1])))


WEIGHT_NAMES = ["w_ada", "b_ada", "norm_mix_w", "w_in", "conv_w", "conv_b", "conv_gn_w", "conv_gn_b", "gdn_conv_w",
                "gdn_a_log", "gdn_dt_bias", "gdn_norm_w", "w_out", "norm_ffn_w", "w_ffn_in", "w_ffn_out",
                "norm_final_w"]


def _slab(b_ada, norm_mix_w, norm_ffn_w, norm_final_w, conv_b, conv_gn_w, conv_gn_b, gdn_norm_w, a_log, dt_bias):
    return jnp.concatenate([
        b_ada.reshape(48, LANES), norm_mix_w.reshape(8, LANES), norm_ffn_w.reshape(8, LANES),
        norm_final_w.reshape(8, LANES), conv_b.reshape(4, LANES), conv_gn_w.reshape(4, LANES),
        conv_gn_b.reshape(4, LANES), gdn_norm_w.reshape(1, LANES), _lanes(a_log), _lanes(dt_bias),
        jnp.zeros((1, LANES), F32)], axis=0)


def _unslab(t):
    return dict(b_ada=t[0:48].reshape(1, 6 * D), norm_mix_w=t[48:56].reshape(1, D),
                norm_ffn_w=t[56:64].reshape(1, D), norm_final_w=t[64:72].reshape(D),
                conv_b=t[72:76].reshape(1, CW), conv_gn_w=t[76:80].reshape(1, CW),
                conv_gn_b=t[80:84].reshape(1, CW), gdn_norm_w=t[84:85], gdn_a_log=t[85:86, 0:NH],
                gdn_dt_bias=t[86:87, 0:NH])


def _mix_forward(w, xs, modnb):
    w_main = w["w_in"][:NMAIN]
    w_ba = jnp.pad(w["w_in"][NMAIN:], ((0, LANES - 2 * NH), (0, 0)))
    alog_l = _lanes(w["gdn_a_log"], NH)
    dt_l = _lanes(w["gdn_dt_bias"], NH)
    p_main, p_ba, hb1 = _fwd_in(xs, w["norm_mix_w"], modnb, w["b_ada"], w_main, w_ba)
    y_conv, out_a = _conf_fwd(p_main, w["conv_w"], w["conv_b"], w["conv_gn_w"], w["conv_gn_b"])
    w_o, u_o, qg, kd, qk, cd, t_inv = _gdn_prep(p_main, p_ba, w["gdn_conv_w"], alog_l, dt_l)
    out_b, o_pre, s_in = _gdn_scan(w_o, u_o, qg, kd, qk, cd, p_main, w["gdn_norm_w"])
    return dict(w_main=w_main, w_ba=w_ba, alog_l=alog_l, dt_l=dt_l, p_main=p_main, p_ba=p_ba, hb1=hb1,
                y_conv=y_conv, out_a=out_a, out_b=out_b, o_pre=o_pre, s_in=s_in, t_inv=t_inv)


def _ffn_stage(w, f, xs, tgt, modnb):
    x1, mix, oab = _fwd_out(f["out_a"], f["out_b"], xs, modnb, w["b_ada"], w["w_out"])
    dx1, hb2, act, dffn, df, st_ffn = _ffn_fwd_bwd(x1, tgt, modnb, w["b_ada"], w["norm_ffn_w"], w["norm_final_w"],
                                                   w["w_ffn_in"], w["w_ffn_out"])
    gw_ffn_in = _grad_w_ffn_in(hb2, df)
    gw_ffn_out = _grad_w_ffn_out(act, dffn)
    return dict(mix=mix, oab=oab, dx1=dx1, st_ffn=st_ffn, gw_ffn_in=gw_ffn_in, gw_ffn_out=gw_ffn_out)


def _out_backward(w, g, modnb):
    dmix, d_out_a, d_out_b, st_out = _bwd_out(g["dx1"], g["mix"], modnb, w["b_ada"], w["w_out"])
    return dict(d_out_a=d_out_a, d_out_b=d_out_b, st_out=st_out, gw_out=_grad_w("grad_w_out", g["oab"], dmix, 512))


def _mix_backward(w, f, g, a, xs, modnb):
    d_out_a, d_out_b, st_out = a["d_out_a"], a["d_out_b"], a["st_out"]
    dp_conf, st_conf = _conf_bwd(d_out_a, f["y_conv"], f["p_main"], w["conv_w"], w["conv_gn_w"], w["conv_gn_b"])
    dp_gdn, dp_ba, st_gdn = _gdn_bwd(d_out_b, f["o_pre"], f["s_in"], f["t_inv"], f["p_main"], f["p_ba"],
                                     w["gdn_conv_w"], f["alog_l"], f["dt_l"], w["gdn_norm_w"])
    grad_x, st_in = _bwd_in(dp_conf, dp_gdn, dp_ba, xs, g["dx1"], w["norm_mix_w"], modnb, w["b_ada"], f["w_main"],
                            f["w_ba"])
    hb1 = f["hb1"]
    gw_in = jnp.concatenate(
        [_grad_w("grad_w_in_conf", dp_conf, hb1, 512), _grad_w("grad_w_in_gdn", dp_gdn, hb1, 512),
         _grad_w("grad_w_in_ba", dp_ba, hb1, LANES)[:2 * NH]], axis=0)
    st_ffn = g["st_ffn"]
    dmod = jnp.concatenate([st_in[0:1], st_in[1:2], st_out[0:1], st_ffn[2:3], st_ffn[3:4], st_ffn[1:2]], axis=1)
    small = jnp.concatenate([
        dmod.reshape(48, LANES), st_in[2:3].reshape(8, LANES), st_ffn[4:5].reshape(8, LANES),
        st_ffn[0:1].reshape(8, LANES), st_conf[31:32].reshape(4, LANES), st_conf[32:33].reshape(4, LANES),
        st_conf[33:34].reshape(4, LANES), st_gdn[4:5, 0:LANES],
        _lanes(st_gdn[5:6, NH:2 * NH]), _lanes(st_gdn[6:7, NH:2 * NH]), st_ffn[5:6, 0:LANES]], axis=0)
    return dict(grad_x=grad_x, gw_in=gw_in, gw_conv=st_conf[0:KC], gw_gconv=st_gdn[0:KS], small=small)


def _local(w, xs, tgt, modnb):
    f = _mix_forward(w, xs, modnb)
    g = _ffn_stage(w, f, xs, tgt, modnb)
    a = _out_backward(w, g, modnb)
    b = _mix_backward(w, f, g, a, xs, modnb)
    return dict(b, gw_out=a["gw_out"], gw_ffn_in=g["gw_ffn_in"], gw_ffn_out=g["gw_ffn_out"])


def kernel(x, c, w_ada, b_ada, norm_mix_w, w_in, conv_w, conv_b, conv_gn_w, conv_gn_b, gdn_conv_w, gdn_a_log, gdn_dt_bias, gdn_norm_w, w_out, norm_ffn_w, w_ffn_in, w_ffn_out, norm_final_w, loss_target, m_w_ada, m_b_ada, m_norm_mix_w, m_w_in, m_conv_w, m_conv_b, m_conv_gn_w, m_conv_gn_b, m_gdn_conv_w, m_gdn_a_log, m_gdn_dt_bias, m_gdn_norm_w, m_w_out, m_norm_ffn_w, m_w_ffn_in, m_w_ffn_out, m_norm_final_w, v_w_ada, v_b_ada, v_norm_mix_w, v_w_in, v_conv_w, v_conv_b, v_conv_gn_w, v_conv_gn_b, v_gdn_conv_w, v_gdn_a_log, v_gdn_dt_bias, v_gdn_norm_w, v_w_out, v_norm_ffn_w, v_w_ffn_in, v_w_ffn_out, v_norm_final_w):
    me = 4 * lax.axis_index("x") + 2 * lax.axis_index("y") + lax.axis_index("c")
    xs = x.reshape(S, D)
    tgt = loss_target.reshape(S, D)

    g_c, g_cw, g_gcw = _exchange("gather_cond", [c, conv_w[0], gdn_conv_w[0]], [False] * 3)
    c_all = g_c.reshape(N_DEV, D)
    g_mod, mod_token = _exchange("gather_mod", [_mod_shard(c_all, w_ada[0])], [False], with_token=True)
    modnb = lax.dynamic_index_in_dim(g_mod, me, axis=1, keepdims=False).reshape(1, 6 * D)

    late = [w_out[0].astype(BF16), jnp.transpose(w_ffn_in[0]).astype(BF16), w_ffn_out[0].astype(BF16)]
    g_win, *late_lands = _exchange(
        "gather_weights", [_after(jnp.transpose(w_in[0]), mod_token).astype(BF16)] + late, [False] * 4,
        seed_only=(1, 2, 3))
    late_started = _exchange_start("gather_late_start", late, late_lands, [False] * 3)
    modnb = _after(modnb, late_started[-1])
    w = dict(b_ada=b_ada, norm_mix_w=norm_mix_w, conv_b=conv_b, conv_gn_w=conv_gn_w, conv_gn_b=conv_gn_b,
             gdn_a_log=gdn_a_log, gdn_dt_bias=gdn_dt_bias, gdn_norm_w=gdn_norm_w, norm_ffn_w=norm_ffn_w,
             norm_final_w=norm_final_w.reshape(1, D),
             conv_w=jnp.transpose(g_cw, (1, 0, 2)).reshape(KC, CW),
             gdn_conv_w=jnp.transpose(g_gcw, (1, 0, 2)).reshape(KS, 3 * GW),
             w_in=g_win.reshape(NIN, D))

    f = _mix_forward(w, xs, modnb)
    _, (g_wout, g_wfi, g_wfo) = _exchange_wait("gather_late_wait", late_started, [False] * 3, f["out_b"])
    w.update(w_out=g_wout.reshape(D, D), w_ffn_in=g_wfi, w_ffn_out=g_wfo.reshape(4, FB, D))
    g = _ffn_stage(w, f, xs, tgt, modnb)

    ffn_grads = [g["gw_ffn_in"], g["gw_ffn_out"].reshape(N_DEV, DFF // N_DEV, D)]
    ffn_started = _exchange_start("scatter_ffn_start", ffn_grads,
                                  [lax.empty(a.shape, a.dtype) for a in ffn_grads], [True] * 2)
    a = _out_backward(w, g, _after(modnb, ffn_started[-1]))
    out_grads = [a["gw_out"].reshape(N_DEV, D // N_DEV, D)]
    out_started = _exchange_start("scatter_out_start", out_grads,
                                  [lax.empty(t.shape, t.dtype) for t in out_grads], [True])
    loc = _mix_backward(dict(w, conv_gn_w=_after(w["conv_gn_w"], out_started[-1])), f, g, a, xs, modnb)

    g_small, small_token = _exchange("gather_small", [loc["small"]], [False], with_token=True)

    in_grads = [_after(loc["gw_in"].reshape(N_DEV, NIN // N_DEV, D), small_token),
                jnp.transpose(loc["gw_conv"].reshape(KC, N_DEV, CW // N_DEV), (1, 0, 2)),
                jnp.transpose(loc["gw_gconv"].reshape(KS, N_DEV, 3 * GW // N_DEV), (1, 0, 2))]
    in_started = _exchange_start("scatter_in_start", in_grads,
                                 [lax.empty(t.shape, t.dtype) for t in in_grads], [True] * 3)
    g_small = _after(g_small, in_started[-1])
    sw = _slab(b_ada, norm_mix_w, norm_ffn_w, norm_final_w, conv_b, conv_gn_w, conv_gn_b, gdn_norm_w, gdn_a_log,
               gdn_dt_bias)
    sm = _slab(m_b_ada, m_norm_mix_w, m_norm_ffn_w, m_norm_final_w, m_conv_b, m_conv_gn_w, m_conv_gn_b,
               m_gdn_norm_w, m_gdn_a_log, m_gdn_dt_bias)
    sv = _slab(v_b_ada, v_norm_mix_w, v_norm_ffn_w, v_norm_final_w, v_conv_b, v_conv_gn_w, v_conv_gn_b,
               v_gdn_norm_w, v_gdn_a_log, v_gdn_dt_bias)
    small_out = _reduce_adam("adam_small", g_small, sw, sm, sv)
    loss = small_out[0][SMALL_ROWS - 1, 0]
    res = [_unslab(t) for t in small_out]

    dmod_rows = g_small[:, 0:48, :].reshape(N_DEV, 6 * D)
    dmod_sh = lax.dynamic_slice_in_dim(dmod_rows, me * (6 * D // N_DEV), 6 * D // N_DEV, axis=1)

    def own(sent):
        return lax.dynamic_index_in_dim(sent, me, axis=0, keepdims=False)

    big = dict(w_ada=_ada_adam(c_all, dmod_sh, w_ada[0], m_w_ada[0], v_w_ada[0]))
    (sent_fi, sent_fo), (r_fi, r_fo) = _exchange_wait("scatter_ffn_wait", ffn_started, [True] * 2, big["w_ada"][0])
    big["w_ffn_in"] = [jnp.transpose(t) for t in _reduce_adam(
        "adam_w_ffn_in", r_fi, jnp.transpose(w_ffn_in[0]), jnp.transpose(m_w_ffn_in[0]),
        jnp.transpose(v_w_ffn_in[0]), own(sent_fi))]
    big["w_ffn_out"] = _reduce_adam("adam_w_ffn_out", r_fo, w_ffn_out[0], m_w_ffn_out[0], v_w_ffn_out[0],
                                    own(sent_fo))
    (sent_out,), (r_out,) = _exchange_wait("scatter_out_wait", out_started, [True], big["w_ffn_out"][0])
    big["w_out"] = _reduce_adam("adam_w_out", r_out, w_out[0], m_w_out[0], v_w_out[0], own(sent_out))
    (sent_in, sent_cw, sent_gcw), (r_in, r_cw, r_gcw) = _exchange_wait(
        "scatter_in_wait", in_started, [True] * 3, big["w_out"][0])
    big["w_in"] = [jnp.transpose(t) for t in _reduce_adam(
        "adam_w_in", r_in, jnp.transpose(w_in[0]), jnp.transpose(m_w_in[0]), jnp.transpose(v_w_in[0]),
        own(sent_in))]
    big["conv_w"] = _reduce_adam("adam_conv_w", r_cw, conv_w[0], m_conv_w[0], v_conv_w[0], own(sent_cw))
    big["gdn_conv_w"] = _reduce_adam("adam_gdn_conv_w", r_gcw, gdn_conv_w[0], m_gdn_conv_w[0], v_gdn_conv_w[0],
                                     own(sent_gcw))
    outs = [loss, loc["grad_x"].reshape(1, S, D)]
    for kind in range(4):
        for nm in WEIGHT_NAMES:
            outs.append(big[nm][kind][None] if nm in big else res[kind][nm])
    return tuple(outs)
```

```python
import functools

import jax
import jax.numpy as jnp
from jax import lax
from jax.experimental import pallas as pl
from jax.experimental.pallas import tpu as pltpu

F32 = jnp.float32
BF16 = jnp.bfloat16
HI = lax.Precision.HIGHEST
MESH = pl.DeviceIdType.MESH

N_DEV = 8
S = 2048
D = 1024
TM = 256
NT = S // TM
CW = 512
KC = 31
NG = 8
GSZ = CW // NG
HALO = 32
GW = 512
NH = 4
DH = 128
KS = 4
SH = 8
CL = 64
NCH = S // CL
NMAIN = 2 * CW + 4 * GW
NIN = NMAIN + 2 * NH
DFF = 2816
FB = DFF // 4
EPS = 1e-6
QSCALE = DH ** -0.5
LANES = 128
SMALL_ROWS = 88

ADAM_LR = 0.001
ADAM_B1 = 0.9
ADAM_B2 = 0.999
ADAM_EPS = 1e-08
ADAM_WD = 0.01
ADAM_STEP = 10
BC1 = 1.0 - ADAM_B1 ** ADAM_STEP
BC2 = 1.0 - ADAM_B2 ** ADAM_STEP

MIB = 1024 * 1024
VMEM_LIMIT_MIB = 32


def _params(limit_mib=VMEM_LIMIT_MIB, **kw):
    return pltpu.CompilerParams(vmem_limit_bytes=limit_mib * MIB, **kw)


def _sig(x):
    return jax.nn.sigmoid(x)


GP = BF16


def _operands(a, b, prec):
    if prec is BF16:
        return a.astype(BF16), b.astype(BF16), None
    return a, b, prec


def _dot(a, b, prec=None):
    a, b, prec = _operands(a, b, prec)
    return jnp.dot(a, b, preferred_element_type=F32, precision=prec)


def _dot_nt(a, b, prec=None):
    a, b, prec = _operands(a, b, prec)
    return lax.dot_general(a, b, (((1,), (1,)), ((), ())), preferred_element_type=F32, precision=prec)


def _dot_tn(a, b, prec=None):
    a, b, prec = _operands(a, b, prec)
    return lax.dot_general(a, b, (((0,), (0,)), ((), ())), preferred_element_type=F32, precision=prec)


def _lockstep(gens):
    gens = list(gens)
    while gens:
        alive = []
        for g in gens:
            try:
                next(g)
                alive.append(g)
            except StopIteration:
                pass
        gens = alive


def _rowsum(x):
    return jnp.sum(x, axis=-1, keepdims=True)


def _colsum(x):
    return jnp.sum(x, axis=0, keepdims=True)


def _mod(mod_ref, b_ref, k):
    return mod_ref[:, k * D:(k + 1) * D] + b_ref[:, k * D:(k + 1) * D]


def _const(shape):
    nd = len(shape)
    return pl.BlockSpec(shape, lambda *_: (0,) * nd)


def _const1(shape):
    nd = len(shape)
    return pl.BlockSpec(shape, lambda *_: (0,) * nd, pipeline_mode=pl.Buffered(1))


PEER_FLIPS = [(dx, dy, dc) for dx in (0, 1) for dy in (0, 1) for dc in (0, 1)][1:]


def _after(x, token):
    return x + token[0:1, 0:1].astype(x.dtype).reshape((1,) * x.ndim)


def _exchange(name, srcs, per_dest, seed_only=(), with_token=False):
    n = len(srcs)
    out_shape = []
    for a, pd in zip(srcs, per_dest):
        blk = a.shape[1:] if pd else a.shape
        out_shape.append(jax.ShapeDtypeStruct((N_DEV,) + tuple(blk), a.dtype))

    def body(*refs):
        src = refs[:n]
        dst = refs[n:2 * n]
        send_sems, recv_sems, local_sems = refs[-3:]
        if with_token:
            refs[2 * n][...] = jnp.zeros((8, LANES), F32)
        x, y, c = lax.axis_index("x"), lax.axis_index("y"), lax.axis_index("c")
        me = 4 * x + 2 * y + c

        def piece(i, j):
            return src[i].at[j] if per_dest[i] else src[i]

        copies = []
        for k, (dx, dy, dc) in enumerate(PEER_FLIPS):
            px = 1 - x if dx else x
            py = 1 - y if dy else y
            pc = 1 - c if dc else c
            pj = 4 * px + 2 * py + pc
            for i in range(n):
                if i in seed_only:
                    continue
                cp = pltpu.make_async_remote_copy(
                    src_ref=piece(i, pj), dst_ref=dst[i].at[me],
                    send_sem=send_sems.at[k * n + i], recv_sem=recv_sems.at[k * n + i],
                    device_id=(px, py, pc), device_id_type=MESH)
                cp.start()
                arrive = pltpu.make_async_remote_copy(
                    src_ref=piece(i, pj), dst_ref=dst[i].at[pj],
                    send_sem=send_sems.at[k * n + i], recv_sem=recv_sems.at[k * n + i],
                    device_id=(px, py, pc), device_id_type=MESH)
                copies.append((cp, arrive))
        own = []
        for i in range(n):
            lc = pltpu.make_async_copy(piece(i, me), dst[i].at[me], local_sems.at[i])
            lc.start()
            own.append(lc)
        for cp, arrive in copies:
            arrive.wait_recv()
        for cp, arrive in copies:
            cp.wait_send()
        for lc in own:
            lc.wait()

    any_spec = pl.BlockSpec(memory_space=pl.ANY)
    out_specs = [any_spec] * n
    if with_token:
        out_shape.append(jax.ShapeDtypeStruct((8, LANES), F32))
        out_specs.append(pl.BlockSpec(memory_space=pltpu.VMEM))
    return pl.pallas_call(
        body, name=name, out_shape=tuple(out_shape),
        in_specs=[any_spec] * n, out_specs=tuple(out_specs),
        scratch_shapes=[pltpu.SemaphoreType.DMA((7 * n,)), pltpu.SemaphoreType.DMA((7 * n,)),
                        pltpu.SemaphoreType.DMA((n,))],
        compiler_params=pltpu.CompilerParams(has_side_effects=True),
    )(*srcs)


HBM_SPEC = pl.BlockSpec(memory_space=pltpu.HBM)
SEM_SPEC = pl.BlockSpec(memory_space=pltpu.SEMAPHORE)
DATAFLOW = pltpu.SideEffectType.DATAFLOW_SIDE_EFFECTING


def _peers():
    x, y, c = lax.axis_index("x"), lax.axis_index("y"), lax.axis_index("c")
    out = []
    for k, (dx, dy, dc) in enumerate(PEER_FLIPS):
        px = 1 - x if dx else x
        py = 1 - y if dy else y
        pc = 1 - c if dc else c
        out.append((k, (px, py, pc), 4 * px + 2 * py + pc))
    return 4 * x + 2 * y + c, out


def _exchange_start(name, srcs, lands, per_dest):
    n = len(srcs)

    def body(*refs):
        src, land = refs[:n], refs[n:2 * n]
        send_sems, recv_sems = refs[2 * n], refs[2 * n + 1]
        token = refs[-1]
        me, peers = _peers()
        for k, peer, pj in peers:
            for i in range(n):
                pltpu.make_async_remote_copy(
                    src_ref=src[i].at[pj] if per_dest[i] else src[i], dst_ref=land[i].at[me],
                    send_sem=send_sems.at[k * n + i], recv_sem=recv_sems.at[k * n + i],
                    device_id=peer, device_id_type=MESH).start()
        token[...] = jnp.zeros((8, LANES), F32)

    arrays = list(srcs) + list(lands)
    return pl.pallas_call(
        body, name=name,
        out_shape=(pltpu.SemaphoreType.DMA((7 * n,)), pltpu.SemaphoreType.DMA((7 * n,)),
                   *[pltpu.HBM(a.shape, a.dtype) for a in arrays], jax.ShapeDtypeStruct((8, LANES), F32)),
        in_specs=[HBM_SPEC] * (2 * n),
        out_specs=(SEM_SPEC, SEM_SPEC, *[HBM_SPEC] * (2 * n), pl.BlockSpec(memory_space=pltpu.VMEM)),
        input_output_aliases={i: 2 + i for i in range(2 * n)},
        compiler_params=pltpu.CompilerParams(has_side_effects=DATAFLOW),
    )(*[pltpu.with_memory_space_constraint(a, pltpu.HBM) for a in arrays])


def _exchange_wait(name, started, per_dest, after):
    n = (len(started) - 3) // 2
    send_sems, recv_sems = started[0], started[1]
    arrays = list(started[2:2 + 2 * n])

    def body(*refs):
        src, land = refs[:n], refs[n:2 * n]
        send, recv = refs[2 * n], refs[2 * n + 1]
        me, peers = _peers()
        for k, peer, pj in peers:
            for i in range(n):
                cp = pltpu.make_async_remote_copy(
                    src_ref=src[i].at[pj] if per_dest[i] else src[i], dst_ref=land[i].at[pj],
                    send_sem=send.at[k * n + i], recv_sem=recv.at[k * n + i],
                    device_id=peer, device_id_type=MESH)
                cp.wait_send()
                cp.wait_recv()

    outs = pl.pallas_call(
        body, name=name,
        out_shape=tuple(pltpu.HBM(a.shape, a.dtype) for a in arrays),
        in_specs=[HBM_SPEC] * (2 * n) + [SEM_SPEC, SEM_SPEC] + [pl.BlockSpec(memory_space=pl.ANY)] * len(after),
        out_specs=tuple([HBM_SPEC] * (2 * n)),
        input_output_aliases={i: i for i in range(2 * n)},
        compiler_params=pltpu.CompilerParams(has_side_effects=DATAFLOW),
    )(*arrays, send_sems, recv_sems, *after)
    return outs[:n], outs[n:]


def _mod_shard(c_all, w_ada):
    def body(c_ref, w_ref, o_ref):
        cv = c_ref[...]
        ca = cv * _sig(cv)
        o_ref[...] = _dot(ca.astype(BF16), w_ref[...].astype(BF16))

    return pl.pallas_call(
        body, name="mod_shard", out_shape=jax.ShapeDtypeStruct((N_DEV, w_ada.shape[1]), F32),
        compiler_params=_params(),
    )(c_all, w_ada)


def _fwd_in(x, nw1, modnb, bada, w_main, w_ba):
    def body(x_ref, nw_ref, mod_ref, b_ref, wm_ref, wb_ref, pm_ref, pb_ref, hb_ref):
        xv = x_ref[...]
        r = lax.rsqrt(jnp.mean(xv * xv, axis=-1, keepdims=True) + EPS)
        h = (xv * r * nw_ref[...]) * (1.0 + _mod(mod_ref, b_ref, 1)) + _mod(mod_ref, b_ref, 0)
        hb = h.astype(BF16)
        hb_ref[...] = hb
        pm_ref[...] = _dot_nt(hb, wm_ref[...])
        pb_ref[...] = _dot_nt(hb, wb_ref[...])

    return pl.pallas_call(
        body, name="fwd_in", grid=(NT,),
        in_specs=[pl.BlockSpec((TM, D), lambda i: (i, 0)), _const((1, D)), _const((1, 6 * D)), _const((1, 6 * D)),
                  _const((NMAIN, D)), _const((LANES, D))],
        out_specs=(pl.BlockSpec((TM, NMAIN), lambda i: (i, 0)), pl.BlockSpec((TM, LANES), lambda i: (i, 0)),
                   pl.BlockSpec((TM, D), lambda i: (i, 0))),
        out_shape=(jax.ShapeDtypeStruct((S, NMAIN), F32), jax.ShapeDtypeStruct((S, LANES), F32),
                   jax.ShapeDtypeStruct((S, D), BF16)),
        compiler_params=_params(dimension_semantics=("arbitrary",)),
    )(x, nw1, modnb, bada, w_main, w_ba)


def _group_mean_matrix():
    ii = lax.broadcasted_iota(jnp.int32, (CW, CW), 0) // GSZ
    jj = lax.broadcasted_iota(jnp.int32, (CW, CW), 1) // GSZ
    return jnp.where(ii == jj, 1.0 / GSZ, 0.0).astype(F32)


SUB = 8
SHIFT_ROWS = HALO + TM - SUB


def _fill_shifted(buf, sh):
    for b in range(1, SUB):
        sh[b - 1] = buf[b:b + SHIFT_ROWS, :]


def _rows_at(buf, sh, off):
    a, b = divmod(off, SUB)
    if b == 0:
        return buf[off:off + TM, :]
    return sh[b - 1, SUB * a:SUB * a + TM, :]


def _group_mean(x, pm):
    hi = x.astype(BF16)
    r1 = x - hi.astype(F32)
    mid = r1.astype(BF16)
    lo = (r1 - mid.astype(F32)).astype(BF16)
    return _dot(hi, pm) + _dot(mid, pm) + _dot(lo, pm)


def _conf_fwd(p_main, conv_w, conv_b, gn_w, gn_b):
    def body(a_ref, g_ref, w_ref, b_ref, gw_ref, gb_ref, y_ref, oa_ref, ubuf, ush):
        i = pl.program_id(0)

        @pl.when(i == 0)
        def _():
            ubuf[0:HALO, :] = jnp.zeros((HALO, CW), F32)

        ubuf[HALO:HALO + TM, :] = a_ref[...] * _sig(g_ref[...])
        _fill_shifted(ubuf, ush)
        acc = jnp.zeros((TM, CW), F32) + b_ref[...]
        for k in range(KC):
            acc = acc + w_ref[k:k + 1, :] * _rows_at(ubuf, ush, HALO - (KC - 1) + k)
        y_ref[...] = acc
        ubuf[0:HALO, :] = ubuf[TM:TM + HALO, :]
        pm = _group_mean_matrix().astype(BF16)
        dlt = acc - _group_mean(acc, pm)
        var = _group_mean(dlt * dlt, pm)
        o = dlt * lax.rsqrt(var + EPS) * gw_ref[...] + gb_ref[...]
        oa_ref[...] = o * _sig(o)

    return pl.pallas_call(
        body, name="conf_fwd", grid=(NT,),
        in_specs=[pl.BlockSpec((TM, CW), lambda i: (i, 0)), pl.BlockSpec((TM, CW), lambda i: (i, 1)),
                  _const((KC, CW)), _const((1, CW)), _const((1, CW)), _const((1, CW))],
        out_specs=(pl.BlockSpec((TM, CW), lambda i: (i, 0)), pl.BlockSpec((TM, CW), lambda i: (i, 0))),
        out_shape=(jax.ShapeDtypeStruct((S, CW), F32), jax.ShapeDtypeStruct((S, CW), F32)),
        scratch_shapes=[pltpu.VMEM((HALO + TM, CW), F32), pltpu.VMEM((SUB - 1, SHIFT_ROWS, CW), F32)],
        compiler_params=_params(dimension_semantics=("arbitrary",)),
    )(p_main, p_main, conv_w, conv_b, gn_w, gn_b)


def _tri_iota():
    ii = lax.broadcasted_iota(jnp.int32, (CL, CL), 0)
    jj = lax.broadcasted_iota(jnp.int32, (CL, CL), 1)
    return ii, jj


def _gdn_gates(ba, alog_l, dt_l):
    beta_all = _sig(ba)
    xg = ba + dt_l
    sp = jnp.maximum(xg, 0.0) + jnp.log(1.0 + jnp.exp(-jnp.abs(xg)))
    neg_a = -jnp.exp(alog_l)
    return beta_all, neg_a * sp, xg, neg_a


def _gdn_cumsum(g_all):
    ii, jj = _tri_iota()
    low = jnp.where(ii >= jj, 1.0, 0.0).astype(F32)
    gcum = _dot(low, g_all, HI)
    return gcum, jnp.transpose(gcum)


def _unit_lower_inverses(mats):
    ii, jj = _tri_iota()
    eye = jnp.where(ii == jj, 1.0, 0.0).astype(F32)
    ts = [eye - a for a in mats]
    ps = [_dot(a, a, HI) for a in mats]
    for _ in range(4):
        ts = [t + _dot(t, p, HI) for t, p in zip(ts, ps)]
        ps = [_dot(p, p, HI) for p in ps]
    return [t + _dot(t, p, HI) for t, p in zip(ts, ps)]


def _head_terms(qh, kh, beta, gcol, grow):
    ii, jj = _tri_iota()
    causal = ii >= jj
    strict = ii > jj
    rq = lax.rsqrt(_rowsum(qh * qh) + EPS)
    rk = lax.rsqrt(_rowsum(kh * kh) + EPS)
    qn = qh * rq
    kn = kh * rk
    qs = qn * QSCALE
    decay = jnp.where(causal, jnp.exp(jnp.where(causal, gcol - grow, 0.0)), 0.0)
    gam = jnp.exp(gcol)
    gl = gcol[CL - 1:CL, :]
    kds = jnp.exp(gl - gcol)
    cd = jnp.exp(gl)
    kb = kn * beta
    a = jnp.where(strict, _dot_nt(kb, kn, GP) * decay, 0.0)
    qk = jnp.where(causal, _dot_nt(qs, kn, GP) * decay, 0.0)
    return dict(rq=rq, rk=rk, qn=qn, kn=kn, qs=qs, decay=decay, gam=gam, kds=kds, cd=cd, kb=kb, a=a, qk=qk,
                causal=causal, strict=strict)


def _short_conv(w_ref, buf, rows=CL):
    acc = w_ref[0:1, :] * buf[SH - KS + 1:SH - KS + 1 + rows, :]
    for k in range(1, KS):
        off = SH - (KS - 1) + k
        acc = acc + w_ref[k:k + 1, :] * buf[off:off + rows, :]
    return acc


CPS = 4
TG = CPS * CL


def _gdn_prep(p_main, p_ba, gdn_conv_w, alog_l, dt_l):
    def body(q_ref, k_ref, v_ref, qh_ref, kh_ref, vh_ref, ba_ref, w_ref, al_ref, dt_ref,
             wo_ref, uo_ref, qg_ref, kd_ref, qk_ref, cd_ref, t_ref, xbuf):
        i = pl.program_id(0)
        first = i == 0
        xbuf[0:SH, 0:GW] = jnp.where(first, 0.0, qh_ref[...])
        xbuf[0:SH, GW:2 * GW] = jnp.where(first, 0.0, kh_ref[...])
        xbuf[0:SH, 2 * GW:3 * GW] = jnp.where(first, 0.0, vh_ref[...])
        xbuf[SH:SH + TG, 0:GW] = q_ref[...]
        xbuf[SH:SH + TG, GW:2 * GW] = k_ref[...]
        xbuf[SH:SH + TG, 2 * GW:3 * GW] = v_ref[...]
        conv = _short_conv(w_ref, xbuf, TG)
        qkv = conv * _sig(conv)
        beta_all, g_all, _, _ = _gdn_gates(ba_ref[...], al_ref[...], dt_ref[...])
        lane = lax.broadcasted_iota(jnp.int32, (8, LANES), 1)
        cums = [_gdn_cumsum(g_all[cc * CL:(cc + 1) * CL, :]) for cc in range(CPS)]
        pairs = [(cc, h) for cc in range(CPS) for h in range(NH)]
        terms, vbs = [], []
        for cc, h in pairs:
            r0, lo = cc * CL, h * DH
            beta = beta_all[r0:r0 + CL, h:h + 1]
            gcum, gcum_t = cums[cc]
            terms.append(_head_terms(qkv[r0:r0 + CL, lo:lo + DH], qkv[r0:r0 + CL, GW + lo:GW + lo + DH], beta,
                                     gcum[:, NH + h:NH + h + 1], gcum_t[NH + h:NH + h + 1, :]))
            vbs.append(qkv[r0:r0 + CL, 2 * GW + lo:2 * GW + lo + DH] * beta)
        invs = _unit_lower_inverses([f["a"] for f in terms])
        cds = [jnp.zeros((8, LANES), F32) for _ in range(CPS)]
        for (cc, h), f, t, vb in zip(pairs, terms, invs, vbs):
            r0, lo = cc * CL, h * DH
            t_ref[cc, h] = t
            uo_ref[r0:r0 + CL, lo:lo + DH] = _dot(t, vb, GP)
            wo_ref[r0:r0 + CL, lo:lo + DH] = _dot(t, f["kb"] * f["gam"], GP).astype(BF16)
            qg_ref[r0:r0 + CL, lo:lo + DH] = (f["qs"] * f["gam"]).astype(BF16)
            kd_ref[r0:r0 + CL, lo:lo + DH] = (f["kn"] * f["kds"]).astype(BF16)
            qk_ref[cc, h] = f["qk"].astype(BF16)
            cds[cc] = cds[cc] + jnp.where(lane == h, f["cd"], 0.0)
        for cc in range(CPS):
            cd_ref[cc] = cds[cc]

    col = lambda j: pl.BlockSpec((TG, GW), lambda i: (i, j))
    halo = lambda j: pl.BlockSpec((SH, GW), lambda i: (jnp.maximum(i * (TG // SH) - 1, 0), j))
    tile = lambda: pl.BlockSpec((TG, GW), lambda i: (i, 0))
    sq = lambda: pl.BlockSpec((CPS, NH, CL, CL), lambda i: (i, 0, 0, 0))
    return pl.pallas_call(
        body, name="gdn_prep", grid=(NCH // CPS,),
        in_specs=[col(2), col(3), col(4), halo(2), halo(3), halo(4), pl.BlockSpec((TG, LANES), lambda i: (i, 0)),
                  _const((KS, 3 * GW)), _const((1, LANES)), _const((1, LANES))],
        out_specs=(tile(), tile(), tile(), tile(), sq(), pl.BlockSpec((CPS, 8, LANES), lambda i: (i, 0, 0)), sq()),
        out_shape=(jax.ShapeDtypeStruct((S, GW), BF16), jax.ShapeDtypeStruct((S, GW), F32),
                   jax.ShapeDtypeStruct((S, GW), BF16), jax.ShapeDtypeStruct((S, GW), BF16),
                   jax.ShapeDtypeStruct((NCH, NH, CL, CL), BF16), jax.ShapeDtypeStruct((NCH, 8, LANES), F32),
                   jax.ShapeDtypeStruct((NCH, NH, CL, CL), F32)),
        scratch_shapes=[pltpu.VMEM((SH + TG, 3 * GW), F32)],
        compiler_params=_params(dimension_semantics=("arbitrary",)),
    )(p_main, p_main, p_main, p_main, p_main, p_main, p_ba, gdn_conv_w, alog_l, dt_l)


def _gdn_scan(w_o, u_o, qg, kd, qk, cd, p_main, gdn_nw):
    def body(w_ref, u_ref, qg_ref, kd_ref, qk_ref, cd_ref, z_ref, nw_ref, ob_ref, o_ref, sin_ref, state):
        n = pl.program_id(0)

        @pl.when(n == 0)
        def _():
            state[...] = jnp.zeros((NH, DH, DH), F32)

        def head(h):
            lo = h * DH
            st = state[h]
            sin_ref[0, h] = st
            sb = st.astype(BF16)
            v_new = u_ref[:, lo:lo + DH] - _dot(w_ref[:, lo:lo + DH], sb)
            yield
            vb = v_new.astype(BF16)
            o = _dot(qg_ref[:, lo:lo + DH], sb) + _dot(qk_ref[0, h], vb)
            state[h] = st * cd_ref[0, 0:1, h:h + 1] + _dot_tn(kd_ref[:, lo:lo + DH], vb)
            yield
            o_ref[:, lo:lo + DH] = o
            r = lax.rsqrt(jnp.mean(o * o, axis=-1, keepdims=True) + EPS)
            zh = z_ref[:, lo:lo + DH]
            ob_ref[:, lo:lo + DH] = o * r * nw_ref[...] * (zh * _sig(zh))

        _lockstep(head(h) for h in range(NH))

    tile = lambda: pl.BlockSpec((CL, GW), lambda n: (n, 0))
    return pl.pallas_call(
        body, name="gdn_scan", grid=(NCH,),
        in_specs=[tile(), tile(), tile(), tile(), pl.BlockSpec((1, NH, CL, CL), lambda n: (n, 0, 0, 0)),
                  pl.BlockSpec((1, 8, LANES), lambda n: (n, 0, 0)), pl.BlockSpec((CL, GW), lambda n: (n, 5)),
                  _const((1, DH))],
        out_specs=(tile(), tile(), pl.BlockSpec((1, NH, DH, DH), lambda n: (n, 0, 0, 0))),
        out_shape=(jax.ShapeDtypeStruct((S, GW), F32), jax.ShapeDtypeStruct((S, GW), F32),
                   jax.ShapeDtypeStruct((NCH, NH, DH, DH), F32)),
        scratch_shapes=[pltpu.VMEM((NH, DH, DH), F32)],
        compiler_params=_params(dimension_semantics=("arbitrary",)),
    )(w_o, u_o, qg, kd, qk, cd, p_main, gdn_nw)


def _fwd_out(out_a, out_b, x, modnb, bada, w_out):
    def body(oa_ref, ob_ref, x_ref, mod_ref, b_ref, w_ref, x1_ref, mix_ref, oab_ref):
        oa = oa_ref[...].astype(BF16)
        ob = ob_ref[...].astype(BF16)
        oab_ref[:, 0:CW] = oa
        oab_ref[:, CW:D] = ob
        mix = _dot(oa, w_ref[0:CW, :]) + _dot(ob, w_ref[CW:D, :])
        mix_ref[...] = mix
        x1_ref[...] = x_ref[...] + _mod(mod_ref, b_ref, 2) * mix

    tile = lambda w: pl.BlockSpec((TM, w), lambda i: (i, 0))
    return pl.pallas_call(
        body, name="fwd_out", grid=(NT,),
        in_specs=[tile(CW), tile(GW), tile(D), _const((1, 6 * D)), _const((1, 6 * D)), _const((D, D))],
        out_specs=(tile(D), tile(D), tile(D)),
        out_shape=(jax.ShapeDtypeStruct((S, D), F32), jax.ShapeDtypeStruct((S, D), F32),
                   jax.ShapeDtypeStruct((S, D), BF16)),
        compiler_params=_params(dimension_semantics=("arbitrary",)),
    )(out_a, out_b, x, modnb, bada, w_out)


TF = 128
FFN_STATS = 8


def _ffn_fwd_bwd(x1, tgt, modnb, bada, nw2, nfw, w_fi, w_fo):
    def body(x1_ref, tgt_ref, mod_ref, b_ref, nw2_ref, nfw_ref, wi_ref, wo_ref,
             dx1_ref, hb_ref, act_ref, dffn_ref, df_ref, st_ref):
        i = pl.program_id(0)

        @pl.when(i == 0)
        def _():
            st_ref[...] = jnp.zeros((FFN_STATS, D), F32)

        sh2, sc2, gt2 = _mod(mod_ref, b_ref, 3), _mod(mod_ref, b_ref, 4), _mod(mod_ref, b_ref, 5)
        x1v = x1_ref[...]
        r2 = lax.rsqrt(jnp.mean(x1v * x1v, axis=-1, keepdims=True) + EPS)
        xr2 = x1v * r2
        xn2 = xr2 * nw2_ref[...]
        hb = (xn2 * (1.0 + sc2) + sh2).astype(BF16)
        hb_ref[...] = hb
        fg, fu, sg = [], [], []
        ffn = jnp.zeros((TF, D), F32)
        for j in range(4):
            fgj = _dot_nt(hb, wi_ref[j])
            fuj = _dot_nt(hb, wi_ref[j + 4])
            sj = _sig(fgj)
            aj = (fgj * sj * fuj).astype(BF16)
            act_ref[j] = aj
            ffn = ffn + _dot(aj, wo_ref[j])
            fg.append(fgj)
            fu.append(fuj)
            sg.append(sj)
        x2 = x1v + gt2 * ffn
        r3 = lax.rsqrt(jnp.mean(x2 * x2, axis=-1, keepdims=True) + EPS)
        xr3 = x2 * r3
        err = xr3 * nfw_ref[...] - tgt_ref[...]
        loss = 0.5 * jnp.sum(jnp.mean(err * err, axis=-1, keepdims=True), axis=0, keepdims=True)
        dy = err * (1.0 / D)
        st_ref[0:1, :] += _colsum(dy * xr3)
        dyr = dy * nfw_ref[...]
        dx2 = r3 * (dyr - xr3 * jnp.mean(dyr * xr3, axis=-1, keepdims=True))
        st_ref[1:2, :] += _colsum(dx2 * ffn)
        st_ref[5:6, :] += jnp.broadcast_to(loss, (1, D))
        dffn = (gt2 * dx2).astype(BF16)
        dffn_ref[...] = dffn
        dh = jnp.zeros((TF, D), F32)
        for j in range(4):
            dact = _dot_nt(dffn, wo_ref[j])
            dfg = (dact * fu[j] * (sg[j] * (1.0 + fg[j] * (1.0 - sg[j])))).astype(BF16)
            dfu = (dact * (fg[j] * sg[j])).astype(BF16)
            df_ref[j] = dfg
            df_ref[j + 4] = dfu
            dh = dh + _dot(dfg, wi_ref[j]) + _dot(dfu, wi_ref[j + 4])
        st_ref[2:3, :] += _colsum(dh)
        st_ref[3:4, :] += _colsum(dh * xn2)
        dxn = dh * (1.0 + sc2)
        st_ref[4:5, :] += _colsum(dxn * xr2)
        dxr = dxn * nw2_ref[...]
        dx1_ref[...] = dx2 + r2 * (dxr - xr2 * jnp.mean(dxr * xr2, axis=-1, keepdims=True))

    tile = lambda w: pl.BlockSpec((TF, w), lambda i: (i, 0))
    return pl.pallas_call(
        body, name="ffn_fwd_bwd", grid=(S // TF,),
        in_specs=[tile(D), tile(D), _const((1, 6 * D)), _const((1, 6 * D)), _const((1, D)), _const((1, D)),
                  _const1((N_DEV, FB, D)), _const1((4, FB, D))],
        out_specs=(tile(D), tile(D), pl.BlockSpec((4, TF, FB), lambda i: (0, i, 0)), tile(D),
                   pl.BlockSpec((N_DEV, TF, FB), lambda i: (0, i, 0)), _const((FFN_STATS, D))),
        out_shape=(jax.ShapeDtypeStruct((S, D), F32), jax.ShapeDtypeStruct((S, D), BF16),
                   jax.ShapeDtypeStruct((4, S, FB), BF16), jax.ShapeDtypeStruct((S, D), BF16),
                   jax.ShapeDtypeStruct((N_DEV, S, FB), BF16), jax.ShapeDtypeStruct((FFN_STATS, D), F32)),
        compiler_params=_params(44, dimension_semantics=("arbitrary",)),
    )(x1, tgt, modnb, bada, nw2, nfw, w_fi, w_fo)


def _grad_w(name, a, b, nb):
    m, n = a.shape[1], b.shape[1]

    def body(a_ref, b_ref, o_ref):
        o_ref[...] = _dot_tn(a_ref[...], b_ref[...]).astype(BF16)

    return pl.pallas_call(
        body, name=name, grid=(m // nb,),
        in_specs=[pl.BlockSpec((S, nb), lambda j: (0, j)), _const((S, n))],
        out_specs=pl.BlockSpec((nb, n), lambda j: (j, 0)),
        out_shape=jax.ShapeDtypeStruct((m, n), BF16),
        compiler_params=_params(dimension_semantics=("arbitrary",)),
    )(a, b)


def _grad_w_ffn_in(hb2, df):
    def body(a_ref, b_ref, o_ref):
        o_ref[0] = _dot_tn(b_ref[0], a_ref[...]).astype(BF16)

    return pl.pallas_call(
        body, name="grad_w_ffn_in", grid=(N_DEV,),
        in_specs=[_const((S, D)), pl.BlockSpec((1, S, FB), lambda j: (j, 0, 0))],
        out_specs=pl.BlockSpec((1, FB, D), lambda j: (j, 0, 0)),
        out_shape=jax.ShapeDtypeStruct((N_DEV, FB, D), BF16),
        compiler_params=_params(dimension_semantics=("arbitrary",)),
    )(hb2, df)


def _grad_w_ffn_out(act, dffn):
    def body(a_ref, b_ref, o_ref):
        o_ref[0] = _dot_tn(a_ref[0], b_ref[...]).astype(BF16)

    return pl.pallas_call(
        body, name="grad_w_ffn_out", grid=(4,),
        in_specs=[pl.BlockSpec((1, S, FB), lambda j: (j, 0, 0)), _const((S, D))],
        out_specs=pl.BlockSpec((1, FB, D), lambda j: (j, 0, 0)),
        out_shape=jax.ShapeDtypeStruct((4, FB, D), BF16),
        compiler_params=_params(dimension_semantics=("arbitrary",)),
    )(act, dffn)


def _bwd_out(dx1, mix, modnb, bada, w_out):
    def body(dx_ref, mix_ref, mod_ref, b_ref, w_ref, dmix_ref, doa_ref, dob_ref, st_ref):
        i = pl.program_id(0)

        @pl.when(i == 0)
        def _():
            st_ref[...] = jnp.zeros((8, D), F32)

        dx = dx_ref[...]
        st_ref[0:1, :] += _colsum(dx * mix_ref[...])
        dmix = (_mod(mod_ref, b_ref, 2) * dx).astype(BF16)
        dmix_ref[...] = dmix
        doa_ref[...] = _dot_nt(dmix, w_ref[0:CW, :])
        dob_ref[...] = _dot_nt(dmix, w_ref[CW:D, :])

    tile = lambda w: pl.BlockSpec((TM, w), lambda i: (i, 0))
    return pl.pallas_call(
        body, name="bwd_out", grid=(NT,),
        in_specs=[tile(D), tile(D), _const((1, 6 * D)), _const((1, 6 * D)), _const((D, D))],
        out_specs=(tile(D), tile(CW), tile(GW), _const((8, D))),
        out_shape=(jax.ShapeDtypeStruct((S, D), BF16), jax.ShapeDtypeStruct((S, CW), F32),
                   jax.ShapeDtypeStruct((S, GW), F32), jax.ShapeDtypeStruct((8, D), F32)),
        compiler_params=_params(dimension_semantics=("arbitrary",)),
    )(dx1, mix, modnb, bada, w_out)


CONF_STATS = 40


def _conf_bwd(d_out_a, y, p_main, conv_w, gn_w, gn_b):
    def body(do_ref, y_ref, a_ref, g_ref, ah_ref, gh_ref, w_ref, gw_ref, gb_ref, dp_ref, st_ref,
             ubuf, dybuf, ush, dysh):
        i = pl.program_id(0)

        @pl.when(i == 0)
        def _():
            st_ref[...] = jnp.zeros((CONF_STATS, CW), F32)
            dybuf[TM:TM + HALO, :] = jnp.zeros((HALO, CW), F32)

        pm = _group_mean_matrix().astype(BF16)
        yv = y_ref[...]
        dlt = yv - _group_mean(yv, pm)
        rstd = lax.rsqrt(_group_mean(dlt * dlt, pm) + EPS)
        un = dlt * rstd
        o = un * gw_ref[...] + gb_ref[...]
        so = _sig(o)
        d_o = do_ref[...] * (so * (1.0 + o * (1.0 - so)))
        st_ref[33:34, :] += _colsum(d_o)
        st_ref[32:33, :] += _colsum(d_o * un)
        dun = d_o * gw_ref[...]
        dy = rstd * (dun - _group_mean(dun, pm) - un * _group_mean(dun * un, pm))
        st_ref[31:32, :] += _colsum(dy)
        dybuf[0:TM, :] = dy
        _fill_shifted(dybuf, dysh)

        a = a_ref[...]
        sg = _sig(g_ref[...])
        first = i == NT - 1
        ubuf[0:HALO, :] = jnp.where(first, 0.0, ah_ref[...] * _sig(gh_ref[...]))
        ubuf[HALO:HALO + TM, :] = a * sg
        _fill_shifted(ubuf, ush)
        du = jnp.zeros((TM, CW), F32)
        for k in range(KC):
            st_ref[k:k + 1, :] += _colsum(dy * _rows_at(ubuf, ush, HALO - (KC - 1) + k))
            du = du + w_ref[k:k + 1, :] * _rows_at(dybuf, dysh, KC - 1 - k)
        dybuf[TM:TM + HALO, :] = dybuf[0:HALO, :]
        dp_ref[:, 0:CW] = (du * sg).astype(BF16)
        dp_ref[:, CW:2 * CW] = (du * a * sg * (1.0 - sg)).astype(BF16)

    rev = lambda w, j=0: pl.BlockSpec((TM, w), lambda i: (NT - 1 - i, j))
    halo = lambda j: pl.BlockSpec((HALO, CW), lambda i: (jnp.maximum((NT - 1 - i) * (TM // HALO) - 1, 0), j))
    return pl.pallas_call(
        body, name="conf_bwd", grid=(NT,),
        in_specs=[rev(CW), rev(CW), rev(CW, 0), rev(CW, 1), halo(0), halo(1),
                  _const((KC, CW)), _const((1, CW)), _const((1, CW))],
        out_specs=(rev(2 * CW), _const((CONF_STATS, CW))),
        out_shape=(jax.ShapeDtypeStruct((S, 2 * CW), BF16), jax.ShapeDtypeStruct((CONF_STATS, CW), F32)),
        scratch_shapes=[pltpu.VMEM((HALO + TM, CW), F32), pltpu.VMEM((TM + HALO, CW), F32),
                        pltpu.VMEM((SUB - 1, SHIFT_ROWS, CW), F32), pltpu.VMEM((SUB - 1, SHIFT_ROWS, CW), F32)],
        compiler_params=_params(dimension_semantics=("arbitrary",)),
    )(d_out_a, y, p_main, p_main, p_main, p_main, conv_w, gn_w, gn_b)


GDN_STATS = 8


def _gdn_bwd(d_out_b, o_pre, s_in, t_inv, p_main, p_ba, gdn_conv_w, alog_l, dt_l, gdn_nw):
    def body(dob_ref, o_ref, sin_ref, t_ref, q_ref, k_ref, v_ref, z_ref, qh_ref, kh_ref, vh_ref, ba_ref,
             w_ref, al_ref, dt_ref, nw_ref, dp_ref, dba_ref, st_ref, xbuf, dcbuf, dstate):
        n = pl.program_id(0)

        @pl.when(n == 0)
        def _():
            st_ref[...] = jnp.zeros((GDN_STATS, 3 * GW), F32)
            dcbuf[CL:CL + SH, :] = jnp.zeros((SH, 3 * GW), F32)
            dstate[...] = jnp.zeros((NH, DH, DH), F32)

        first = n == NCH - 1
        xbuf[0:SH, 0:GW] = jnp.where(first, 0.0, qh_ref[...])
        xbuf[0:SH, GW:2 * GW] = jnp.where(first, 0.0, kh_ref[...])
        xbuf[0:SH, 2 * GW:3 * GW] = jnp.where(first, 0.0, vh_ref[...])
        xbuf[SH:SH + CL, 0:GW] = q_ref[...]
        xbuf[SH:SH + CL, GW:2 * GW] = k_ref[...]
        xbuf[SH:SH + CL, 2 * GW:3 * GW] = v_ref[...]
        conv = _short_conv(w_ref, xbuf)
        sc = _sig(conv)
        qkv = conv * sc
        ba = ba_ref[...]
        beta_all, g_all, xg, neg_a = _gdn_gates(ba, al_ref[...], dt_ref[...])
        gcum, gcum_t = _gdn_cumsum(g_all)
        lane = lax.broadcasted_iota(jnp.int32, (CL, LANES), 1)
        row = lax.broadcasted_iota(jnp.int32, (CL, 1), 0)
        acc = dict(dgcum=jnp.zeros((CL, LANES), F32), dbeta=jnp.zeros((CL, LANES), F32))

        def head(h):
            lo = h * DH
            qh = qkv[:, lo:lo + DH]
            kh = qkv[:, GW + lo:GW + lo + DH]
            vh = qkv[:, 2 * GW + lo:2 * GW + lo + DH]
            beta = beta_all[:, h:h + 1]
            f = _head_terms(qh, kh, beta, gcum[:, NH + h:NH + h + 1], gcum_t[NH + h:NH + h + 1, :])
            qn, kn, qs, kb, gam, kds, cd, decay = (f[s] for s in ("qn", "kn", "qs", "kb", "gam", "kds", "cd", "decay"))
            t = t_ref[0, h]
            st = sin_ref[0, h]
            vb = vh * beta
            kbg = kb * gam
            u = _dot(t, vb, GP)
            w = _dot(t, kbg, GP)
            yield
            v_new = u - _dot(w, st, GP)
            q_dec = qs * gam
            k_dec = kn * kds

            o = o_ref[:, lo:lo + DH]
            zh = z_ref[:, lo:lo + DH]
            sz = _sig(zh)
            r = lax.rsqrt(jnp.mean(o * o, axis=-1, keepdims=True) + EPS)
            orr = o * r
            d_out = dob_ref[:, lo:lo + DH]
            dz = d_out * (orr * nw_ref[...]) * (sz * (1.0 + zh * (1.0 - sz)))
            don = d_out * (zh * sz)
            st_ref[4:5, 0:DH] += _colsum(don * orr)
            tt = don * nw_ref[...]
            d_o = r * (tt - orr * jnp.mean(tt * orr, axis=-1, keepdims=True))

            yield
            ds_out = dstate[h]
            dv_new = _dot_tn(f["qk"], d_o, GP) + _dot(k_dec, ds_out, GP)
            dqk = jnp.where(f["causal"], _dot_nt(d_o, v_new, GP), 0.0)
            dq_dec = _dot_nt(d_o, st, GP)
            dk_dec = _dot_nt(v_new, ds_out, GP)
            yield
            dstate[h] = _dot_tn(q_dec, d_o, GP) + cd * ds_out - _dot_tn(w, dv_new, GP)
            dcd = jnp.sum(_rowsum(st * ds_out), axis=0, keepdims=True)
            dw = -_dot_nt(dv_new, st, GP)
            dvb = _dot_tn(t, dv_new, GP)
            yield
            dt_m = _dot_nt(dv_new, vb, GP) + _dot_nt(dw, kbg, GP)
            dkbg = _dot_tn(t, dw, GP)
            yield
            dtt = _dot_nt(dt_m, t, GP)
            yield
            da = jnp.where(f["strict"], -_dot_tn(t, dtt, GP), 0.0)
            yield
            dad = da * decay
            dqkd = dqk * decay
            dkb = _dot(dad, kn, GP) + dkbg * gam
            dkn = _dot_tn(dad, kb, GP) + _dot_tn(dqkd, qs, GP) + dk_dec * kds + dkb * beta
            dqs = _dot(dqkd, kn, GP) + dq_dec * gam
            yield
            m = da * f["a"] + dqk * f["qk"]
            tk = _rowsum(dk_dec * k_dec)
            dgl = jnp.sum(tk, axis=0, keepdims=True) + dcd * cd
            dgc = (_rowsum(m) - _rowsum(jnp.transpose(m)) + _rowsum(dq_dec * q_dec) - tk + _rowsum(dkbg * kbg)
                   + jnp.where(row == CL - 1, dgl, 0.0))
            dbeta = _rowsum(dkb * kn) + _rowsum(dvb * vh)
            acc["dgcum"] = acc["dgcum"] + jnp.where(lane == NH + h, dgc, 0.0)
            acc["dbeta"] = acc["dbeta"] + jnp.where(lane == h, dbeta, 0.0)
            dvh = dvb * beta
            dqn = dqs * QSCALE
            dqh = f["rq"] * (dqn - qn * _rowsum(dqn * qn))
            dkh = f["rk"] * (dkn - kn * _rowsum(dkn * kn))
            dsilu = lambda c0: sc[:, c0:c0 + DH] * (1.0 + conv[:, c0:c0 + DH] * (1.0 - sc[:, c0:c0 + DH]))
            dcbuf[0:CL, lo:lo + DH] = dqh * dsilu(lo)
            dcbuf[0:CL, GW + lo:GW + lo + DH] = dkh * dsilu(GW + lo)
            dcbuf[0:CL, 2 * GW + lo:2 * GW + lo + DH] = dvh * dsilu(2 * GW + lo)
            dp_ref[:, 3 * GW + lo:3 * GW + lo + DH] = dz.astype(BF16)

        _lockstep(head(h) for h in range(NH))
        dgcum_all, dbeta_all = acc["dgcum"], acc["dbeta"]

        ii, jj = _tri_iota()
        upper = jnp.where(ii <= jj, 1.0, 0.0).astype(F32)
        dg_all = _dot(upper, dgcum_all, HI)
        dxg = dg_all * neg_a * _sig(xg)
        st_ref[5:6, 0:LANES] += _colsum(dg_all * g_all)
        st_ref[6:7, 0:LANES] += _colsum(dxg)
        dbl = dbeta_all * beta_all * (1.0 - beta_all)
        dba_ref[...] = jnp.where(lane < NH, dbl, jnp.where(lane < 2 * NH, dxg, 0.0)).astype(BF16)

        dconv = dcbuf[0:CL, :]
        dx = w_ref[0:1, :] * dcbuf[KS - 1:KS - 1 + CL, :]
        st_ref[0:1, :] += _colsum(dconv * xbuf[SH - KS + 1:SH - KS + 1 + CL, :])
        for k in range(1, KS):
            off = SH - (KS - 1) + k
            st_ref[k:k + 1, :] += _colsum(dconv * xbuf[off:off + CL, :])
            dx = dx + w_ref[k:k + 1, :] * dcbuf[KS - 1 - k:KS - 1 - k + CL, :]
        dcbuf[CL:CL + SH, :] = dcbuf[0:SH, :]
        dp_ref[:, 0:3 * GW] = dx.astype(BF16)

    rev = lambda w, j=0: pl.BlockSpec((CL, w), lambda n: (NCH - 1 - n, j))
    halo = lambda j: pl.BlockSpec((SH, GW), lambda n: (jnp.maximum((NCH - 1 - n) * (CL // SH) - 1, 0), j))
    blk4 = lambda a, b: pl.BlockSpec((1, NH, a, b), lambda n: (NCH - 1 - n, 0, 0, 0))
    return pl.pallas_call(
        body, name="gdn_bwd", grid=(NCH,),
        in_specs=[rev(GW), rev(GW), blk4(DH, DH), blk4(CL, CL), rev(GW, 2), rev(GW, 3), rev(GW, 4), rev(GW, 5),
                  halo(2), halo(3), halo(4), rev(LANES), _const((KS, 3 * GW)), _const((1, LANES)),
                  _const((1, LANES)), _const((1, DH))],
        out_specs=(rev(4 * GW), rev(LANES), _const((GDN_STATS, 3 * GW))),
        out_shape=(jax.ShapeDtypeStruct((S, 4 * GW), BF16), jax.ShapeDtypeStruct((S, LANES), BF16),
                   jax.ShapeDtypeStruct((GDN_STATS, 3 * GW), F32)),
        scratch_shapes=[pltpu.VMEM((SH + CL, 3 * GW), F32), pltpu.VMEM((CL + SH, 3 * GW), F32),
                        pltpu.VMEM((NH, DH, DH), F32)],
        compiler_params=_params(dimension_semantics=("arbitrary",)),
    )(d_out_b, o_pre, s_in, t_inv, p_main, p_main, p_main, p_main, p_main, p_main, p_main, p_ba,
      gdn_conv_w, alog_l, dt_l, gdn_nw)


def _bwd_in(dp_conf, dp_gdn, dp_ba, x, dx1, nw1, modnb, bada, w_main, w_ba):
    def body(dc_ref, dg_ref, db_ref, x_ref, dx1_ref, nw_ref, mod_ref, b_ref, wm_ref, wb_ref, gx_ref, st_ref):
        i = pl.program_id(0)

        @pl.when(i == 0)
        def _():
            st_ref[...] = jnp.zeros((8, D), F32)

        dh = (_dot(dc_ref[...], wm_ref[0:2 * CW, :]) + _dot(dg_ref[...], wm_ref[2 * CW:NMAIN, :])
              + _dot(db_ref[...], wb_ref[...]))
        xv = x_ref[...]
        r = lax.rsqrt(jnp.mean(xv * xv, axis=-1, keepdims=True) + EPS)
        xr = xv * r
        st_ref[0:1, :] += _colsum(dh)
        st_ref[1:2, :] += _colsum(dh * (xr * nw_ref[...]))
        dxn = dh * (1.0 + _mod(mod_ref, b_ref, 1))
        st_ref[2:3, :] += _colsum(dxn * xr)
        dxr = dxn * nw_ref[...]
        gx_ref[...] = dx1_ref[...] + r * (dxr - xr * jnp.mean(dxr * xr, axis=-1, keepdims=True))

    tile = lambda w: pl.BlockSpec((TM, w), lambda i: (i, 0))
    return pl.pallas_call(
        body, name="bwd_in", grid=(NT,),
        in_specs=[tile(2 * CW), tile(4 * GW), tile(LANES), tile(D), tile(D), _const((1, D)), _const((1, 6 * D)),
                  _const((1, 6 * D)), _const((NMAIN, D)), _const((LANES, D))],
        out_specs=(tile(D), _const((8, D))),
        out_shape=(jax.ShapeDtypeStruct((S, D), F32), jax.ShapeDtypeStruct((8, D), F32)),
        compiler_params=_params(dimension_semantics=("arbitrary",)),
    )(dp_conf, dp_gdn, dp_ba, x, dx1, nw1, modnb, bada, w_main, w_ba)


def _adamw(w, g, m, v):
    m = ADAM_B1 * m + (1.0 - ADAM_B1) * g
    v = ADAM_B2 * v + (1.0 - ADAM_B2) * (g * g)
    m_hat = m / BC1
    v_hat = v / BC2
    delta = -ADAM_LR * (m_hat / (jnp.sqrt(v_hat) + ADAM_EPS) + ADAM_WD * w)
    return delta, m, v


ADAM_BLOCK_BYTES = 6 * 1024 * 1024


def _adam_tile(rows, cols):
    padded = -(-cols // LANES) * LANES
    if N_DEV * rows * padded * 4 <= ADAM_BLOCK_BYTES:
        return rows, cols
    best = None
    for tr in range(16, rows, 16):
        if rows % tr == 0 and N_DEV * tr * padded * 4 <= ADAM_BLOCK_BYTES:
            best = tr
    if best is not None:
        return best, cols
    rows_padded = -(-rows // 16) * 16
    tc = LANES
    for cand in range(LANES, cols, LANES):
        if cols % cand == 0 and N_DEV * rows_padded * cand * 4 <= ADAM_BLOCK_BYTES:
            tc = cand
    return rows, tc


def _reduce_adam(name, parts, w, m, v, own=None):
    rows, cols = w.shape
    tr, tc = _adam_tile(rows, cols)

    def body(*refs):
        p_ref, w_ref, m_ref, v_ref = refs[:4]
        g_ref, d_ref, nm_ref, nv_ref = refs[-4:]
        if own is None:
            part = lambda j: p_ref[j].astype(F32)
        else:
            me = 4 * lax.axis_index("x") + 2 * lax.axis_index("y") + lax.axis_index("c")
            part = lambda j: jnp.where(me == j, refs[4][...], p_ref[j]).astype(F32)
        g = part(0)
        for j in range(1, N_DEV):
            g = g + part(j)
        g_ref[...] = g
        d_ref[...], nm_ref[...], nv_ref[...] = _adamw(w_ref[...], g, m_ref[...], v_ref[...])

    blk = pl.BlockSpec((tr, tc), lambda i, j: (i, j))
    sds = jax.ShapeDtypeStruct((rows, cols), F32)
    extra = [] if own is None else [own]
    return pl.pallas_call(
        body, name=name, grid=(rows // tr, cols // tc),
        in_specs=[pl.BlockSpec((N_DEV, tr, tc), lambda i, j: (0, i, j)), blk, blk, blk] + [blk] * len(extra),
        out_specs=(blk, blk, blk, blk), out_shape=(sds, sds, sds, sds),
        compiler_params=_params(dimension_semantics=("arbitrary", "arbitrary")),
    )(parts, w, m, v, *extra)


def _ada_adam(c_all, dmod_sh, w, m, v):
    rows, cols = w.shape
    tr = 256

    def body(c_ref, dm_ref, w_ref, m_ref, v_ref, g_ref, d_ref, nm_ref, nv_ref):
        cv = c_ref[...]
        g = _dot_tn(cv * _sig(cv), dm_ref[...], HI)
        g_ref[...] = g
        d_ref[...], nm_ref[...], nv_ref[...] = _adamw(w_ref[...], g, m_ref[...], v_ref[...])

    blk = pl.BlockSpec((tr, cols), lambda i: (i, 0))
    sds = jax.ShapeDtypeStruct((rows, cols), F32)
    return pl.pallas_call(
        body, name="ada_adam", grid=(rows // tr,),
        in_specs=[pl.BlockSpec((N_DEV, tr), lambda i: (0, i)), _const((N_DEV, cols)), blk, blk, blk],
        out_specs=(blk, blk, blk, blk), out_shape=(sds, sds, sds, sds),
        compiler_params=_params(dimension_semantics=("arbitrary",)),
    )(c_all, dmod_sh, w, m, v)


def _lanes(a, at=0):
    return jnp.pad(a, ((0, 0), (at, LANES - at - a.shape[1])))


WEIGHT_NAMES = ["w_ada", "b_ada", "norm_mix_w", "w_in", "conv_w", "conv_b", "conv_gn_w", "conv_gn_b", "gdn_conv_w",
                "gdn_a_log", "gdn_dt_bias", "gdn_norm_w", "w_out", "norm_ffn_w", "w_ffn_in", "w_ffn_out",
                "norm_final_w"]


def _slab(b_ada, norm_mix_w, norm_ffn_w, norm_final_w, conv_b, conv_gn_w, conv_gn_b, gdn_norm_w, a_log, dt_bias):
    return jnp.concatenate([
        b_ada.reshape(48, LANES), norm_mix_w.reshape(8, LANES), norm_ffn_w.reshape(8, LANES),
        norm_final_w.reshape(8, LANES), conv_b.reshape(4, LANES), conv_gn_w.reshape(4, LANES),
        conv_gn_b.reshape(4, LANES), gdn_norm_w.reshape(1, LANES), _lanes(a_log), _lanes(dt_bias),
        jnp.zeros((1, LANES), F32)], axis=0)


def _unslab(t):
    return dict(b_ada=t[0:48].reshape(1, 6 * D), norm_mix_w=t[48:56].reshape(1, D),
                norm_ffn_w=t[56:64].reshape(1, D), norm_final_w=t[64:72].reshape(D),
                conv_b=t[72:76].reshape(1, CW), conv_gn_w=t[76:80].reshape(1, CW),
                conv_gn_b=t[80:84].reshape(1, CW), gdn_norm_w=t[84:85], gdn_a_log=t[85:86, 0:NH],
                gdn_dt_bias=t[86:87, 0:NH])


def _mix_forward(w, xs, modnb):
    w_main = w["w_in"]
    w_ba = jnp.pad(w["w_in"][NMAIN:], ((0, LANES - 2 * NH), (0, 0)))
    alog_l = _lanes(w["gdn_a_log"], NH)
    dt_l = _lanes(w["gdn_dt_bias"], NH)
    p_main, p_ba, hb1 = _fwd_in(xs, w["norm_mix_w"], modnb, w["b_ada"], w_main, w_ba)
    y_conv, out_a = _conf_fwd(p_main, w["conv_w"], w["conv_b"], w["conv_gn_w"], w["conv_gn_b"])
    w_o, u_o, qg, kd, qk, cd, t_inv = _gdn_prep(p_main, p_ba, w["gdn_conv_w"], alog_l, dt_l)
    out_b, o_pre, s_in = _gdn_scan(w_o, u_o, qg, kd, qk, cd, p_main, w["gdn_norm_w"])
    return dict(w_main=w_main, w_ba=w_ba, alog_l=alog_l, dt_l=dt_l, p_main=p_main, p_ba=p_ba, hb1=hb1,
                y_conv=y_conv, out_a=out_a, out_b=out_b, o_pre=o_pre, s_in=s_in, t_inv=t_inv)


def _ffn_stage(w, f, xs, tgt, modnb):
    x1, mix, oab = _fwd_out(f["out_a"], f["out_b"], xs, modnb, w["b_ada"], w["w_out"])
    dx1, hb2, act, dffn, df, st_ffn = _ffn_fwd_bwd(x1, tgt, modnb, w["b_ada"], w["norm_ffn_w"], w["norm_final_w"],
                                                   w["w_ffn_in"], w["w_ffn_out"])
    gw_ffn_in = _grad_w_ffn_in(hb2, df)
    gw_ffn_out = _grad_w_ffn_out(act, dffn)
    return dict(mix=mix, oab=oab, dx1=dx1, st_ffn=st_ffn, gw_ffn_in=gw_ffn_in, gw_ffn_out=gw_ffn_out)


def _out_backward(w, g, modnb):
    dmix, d_out_a, d_out_b, st_out = _bwd_out(g["dx1"], g["mix"], modnb, w["b_ada"], w["w_out"])
    return dict(d_out_a=d_out_a, d_out_b=d_out_b, st_out=st_out, gw_out=_grad_w("grad_w_out", g["oab"], dmix, 512))


def _mix_backward(w, f, g, a, xs, modnb):
    d_out_a, d_out_b, st_out = a["d_out_a"], a["d_out_b"], a["st_out"]
    dp_conf, st_conf = _conf_bwd(d_out_a, f["y_conv"], f["p_main"], w["conv_w"], w["conv_gn_w"], w["conv_gn_b"])
    dp_gdn, dp_ba, st_gdn = _gdn_bwd(d_out_b, f["o_pre"], f["s_in"], f["t_inv"], f["p_main"], f["p_ba"],
                                     w["gdn_conv_w"], f["alog_l"], f["dt_l"], w["gdn_norm_w"])
    grad_x, st_in = _bwd_in(dp_conf, dp_gdn, dp_ba, xs, g["dx1"], w["norm_mix_w"], modnb, w["b_ada"], f["w_main"],
                            f["w_ba"])
    hb1 = f["hb1"]
    gw_in = jnp.concatenate(
        [_grad_w("grad_w_in_conf", dp_conf, hb1, 512), _grad_w("grad_w_in_gdn", dp_gdn, hb1, 512),
         _grad_w("grad_w_in_ba", dp_ba, hb1, LANES)[:2 * NH]], axis=0)
    st_ffn = g["st_ffn"]
    dmod = jnp.concatenate([st_in[0:1], st_in[1:2], st_out[0:1], st_ffn[2:3], st_ffn[3:4], st_ffn[1:2]], axis=1)
    small = jnp.concatenate([
        dmod.reshape(48, LANES), st_in[2:3].reshape(8, LANES), st_ffn[4:5].reshape(8, LANES),
        st_ffn[0:1].reshape(8, LANES), st_conf[31:32].reshape(4, LANES), st_conf[32:33].reshape(4, LANES),
        st_conf[33:34].reshape(4, LANES), st_gdn[4:5, 0:LANES],
        _lanes(st_gdn[5:6, NH:2 * NH]), _lanes(st_gdn[6:7, NH:2 * NH]), st_ffn[5:6, 0:LANES]], axis=0)
    return dict(grad_x=grad_x, gw_in=gw_in, gw_conv=st_conf[0:KC], gw_gconv=st_gdn[0:KS], small=small)


def _local(w, xs, tgt, modnb):
    f = _mix_forward(w, xs, modnb)
    g = _ffn_stage(w, f, xs, tgt, modnb)
    a = _out_backward(w, g, modnb)
    b = _mix_backward(w, f, g, a, xs, modnb)
    return dict(b, gw_out=a["gw_out"], gw_ffn_in=g["gw_ffn_in"], gw_ffn_out=g["gw_ffn_out"])


def kernel(x, c, w_ada, b_ada, norm_mix_w, w_in, conv_w, conv_b, conv_gn_w, conv_gn_b, gdn_conv_w, gdn_a_log, gdn_dt_bias, gdn_norm_w, w_out, norm_ffn_w, w_ffn_in, w_ffn_out, norm_final_w, loss_target, m_w_ada, m_b_ada, m_norm_mix_w, m_w_in, m_conv_w, m_conv_b, m_conv_gn_w, m_conv_gn_b, m_gdn_conv_w, m_gdn_a_log, m_gdn_dt_bias, m_gdn_norm_w, m_w_out, m_norm_ffn_w, m_w_ffn_in, m_w_ffn_out, m_norm_final_w, v_w_ada, v_b_ada, v_norm_mix_w, v_w_in, v_conv_w, v_conv_b, v_conv_gn_w, v_conv_gn_b, v_gdn_conv_w, v_gdn_a_log, v_gdn_dt_bias, v_gdn_norm_w, v_w_out, v_norm_ffn_w, v_w_ffn_in, v_w_ffn_out, v_norm_final_w):
    me = 4 * lax.axis_index("x") + 2 * lax.axis_index("y") + lax.axis_index("c")
    xs = x.reshape(S, D)
    tgt = loss_target.reshape(S, D)

    g_c, g_cw, g_gcw = _exchange("gather_cond", [c, conv_w[0], gdn_conv_w[0]], [False] * 3)
    c_all = g_c.reshape(N_DEV, D)
    g_mod, mod_token = _exchange("gather_mod", [_mod_shard(c_all, w_ada[0])], [False], with_token=True)
    modnb = lax.dynamic_index_in_dim(g_mod, me, axis=1, keepdims=False).reshape(1, 6 * D)

    late = [w_out[0].astype(BF16), jnp.transpose(w_ffn_in[0]).astype(BF16), w_ffn_out[0].astype(BF16)]
    g_win, *late_lands = _exchange(
        "gather_weights", [_after(jnp.transpose(w_in[0]), mod_token).astype(BF16)] + late, [False] * 4,
        seed_only=(1, 2, 3))
    late_started = _exchange_start("gather_late_start", late, late_lands, [False] * 3)
    modnb = _after(modnb, late_started[-1])
    w = dict(b_ada=b_ada, norm_mix_w=norm_mix_w, conv_b=conv_b, conv_gn_w=conv_gn_w, conv_gn_b=conv_gn_b,
             gdn_a_log=gdn_a_log, gdn_dt_bias=gdn_dt_bias, gdn_norm_w=gdn_norm_w, norm_ffn_w=norm_ffn_w,
             norm_final_w=norm_final_w.reshape(1, D),
             conv_w=jnp.transpose(g_cw, (1, 0, 2)).reshape(KC, CW),
             gdn_conv_w=jnp.transpose(g_gcw, (1, 0, 2)).reshape(KS, 3 * GW),
             w_in=g_win.reshape(NIN, D))

    f = _mix_forward(w, xs, modnb)
    _, (g_wout, g_wfi, g_wfo) = _exchange_wait("gather_late_wait", late_started, [False] * 3,
                                                  (f["out_a"], f["out_b"]))
    w.update(w_out=g_wout.reshape(D, D), w_ffn_in=g_wfi, w_ffn_out=g_wfo.reshape(4, FB, D))
    g = _ffn_stage(w, f, xs, tgt, modnb)

    ffn_grads = [g["gw_ffn_in"], g["gw_ffn_out"].reshape(N_DEV, DFF // N_DEV, D)]
    ffn_started = _exchange_start("scatter_ffn_start", ffn_grads,
                                  [lax.empty(a.shape, a.dtype) for a in ffn_grads], [True] * 2)
    a = _out_backward(w, g, _after(modnb, ffn_started[-1]))
    out_grads = [a["gw_out"].reshape(N_DEV, D // N_DEV, D)]
    out_started = _exchange_start("scatter_out_start", out_grads,
                                  [lax.empty(t.shape, t.dtype) for t in out_grads], [True])
    loc = _mix_backward(dict(w, conv_gn_w=_after(w["conv_gn_w"], out_started[-1])), f, g, a, xs, modnb)

    g_small, small_token = _exchange("gather_small", [loc["small"]], [False], with_token=True)

    in_grads = [loc["gw_in"].reshape(N_DEV, NIN // N_DEV, D),
                _after(jnp.transpose(loc["gw_conv"].reshape(KC, N_DEV, CW // N_DEV), (1, 0, 2)), small_token),
                jnp.transpose(loc["gw_gconv"].reshape(KS, N_DEV, 3 * GW // N_DEV), (1, 0, 2))]
    in_started = _exchange_start("scatter_in_start", in_grads,
                                 [lax.empty(t.shape, t.dtype) for t in in_grads], [True] * 3)
    g_small = _after(g_small, in_started[-1])
    sw = _slab(b_ada, norm_mix_w, norm_ffn_w, norm_final_w, conv_b, conv_gn_w, conv_gn_b, gdn_norm_w, gdn_a_log,
               gdn_dt_bias)
    sm = _slab(m_b_ada, m_norm_mix_w, m_norm_ffn_w, m_norm_final_w, m_conv_b, m_conv_gn_w, m_conv_gn_b,
               m_gdn_norm_w, m_gdn_a_log, m_gdn_dt_bias)
    sv = _slab(v_b_ada, v_norm_mix_w, v_norm_ffn_w, v_norm_final_w, v_conv_b, v_conv_gn_w, v_conv_gn_b,
               v_gdn_norm_w, v_gdn_a_log, v_gdn_dt_bias)
    small_out = _reduce_adam("adam_small", g_small, sw, sm, sv)
    loss = small_out[0][SMALL_ROWS - 1, 0]
    res = [_unslab(t) for t in small_out]

    dmod_rows = g_small[:, 0:48, :].reshape(N_DEV, 6 * D)
    dmod_sh = lax.dynamic_slice_in_dim(dmod_rows, me * (6 * D // N_DEV), 6 * D // N_DEV, axis=1)

    def own(sent):
        return lax.dynamic_index_in_dim(sent, me, axis=0, keepdims=False)

    big = dict(w_ada=_ada_adam(c_all, dmod_sh, w_ada[0], m_w_ada[0], v_w_ada[0]))
    (sent_fi, sent_fo), (r_fi, r_fo) = _exchange_wait("scatter_ffn_wait", ffn_started, [True] * 2,
                                                         (big["w_ada"][0],))
    big["w_ffn_in"] = [jnp.transpose(t) for t in _reduce_adam(
        "adam_w_ffn_in", r_fi, jnp.transpose(w_ffn_in[0]), jnp.transpose(m_w_ffn_in[0]),
        jnp.transpose(v_w_ffn_in[0]), own(sent_fi))]
    big["w_ffn_out"] = _reduce_adam("adam_w_ffn_out", r_fo, w_ffn_out[0], m_w_ffn_out[0], v_w_ffn_out[0],
                                    own(sent_fo))
    (sent_out,), (r_out,) = _exchange_wait("scatter_out_wait", out_started, [True], (big["w_ffn_out"][0],))
    big["w_out"] = _reduce_adam("adam_w_out", r_out, w_out[0], m_w_out[0], v_w_out[0], own(sent_out))
    (sent_in, sent_cw, sent_gcw), (r_in, r_cw, r_gcw) = _exchange_wait(
        "scatter_in_wait", in_started, [True] * 3, (big["w_out"][0],))
    big["w_in"] = [jnp.transpose(t) for t in _reduce_adam(
        "adam_w_in", r_in, jnp.transpose(w_in[0]), jnp.transpose(m_w_in[0]), jnp.transpose(v_w_in[0]),
        own(sent_in))]
    big["conv_w"] = _reduce_adam("adam_conv_w", r_cw, conv_w[0], m_conv_w[0], v_conv_w[0], own(sent_cw))
    big["gdn_conv_w"] = _reduce_adam("adam_gdn_conv_w", r_gcw, gdn_conv_w[0], m_gdn_conv_w[0], v_gdn_conv_w[0],
                                     own(sent_gcw))
    outs = [loss, loc["grad_x"].reshape(1, S, D)]
    for kind in range(4):
        for nm in WEIGHT_NAMES:
            outs.append(big[nm][kind][None] if nm in big else res[kind][nm])
    return tuple(outs)
```

```python
import functools

import jax
import jax.numpy as jnp
from jax import lax
from jax.experimental import pallas as pl
from jax.experimental.pallas import tpu as pltpu

F32 = jnp.float32
BF16 = jnp.bfloat16
HI = lax.Precision.HIGHEST
MESH = pl.DeviceIdType.MESH

N_DEV = 8
S = 2048
D = 1024
TM = 256
NT = S // TM
CW = 512
KC = 31
NG = 8
GSZ = CW // NG
HALO = 32
GW = 512
NH = 4
DH = 128
KS = 4
SH = 8
CL = 64
NCH = S // CL
NMAIN = 2 * CW + 4 * GW
NIN = NMAIN + 2 * NH
DFF = 2816
FB = DFF // 4
EPS = 1e-6
QSCALE = DH ** -0.5
LANES = 128
SMALL_ROWS = 88

ADAM_LR = 0.001
ADAM_B1 = 0.9
ADAM_B2 = 0.999
ADAM_EPS = 1e-08
ADAM_WD = 0.01
ADAM_STEP = 10
BC1 = 1.0 - ADAM_B1 ** ADAM_STEP
BC2 = 1.0 - ADAM_B2 ** ADAM_STEP

MIB = 1024 * 1024
VMEM_LIMIT_MIB = 32


def _params(limit_mib=VMEM_LIMIT_MIB, **kw):
    return pltpu.CompilerParams(vmem_limit_bytes=limit_mib * MIB, **kw)


def _sig(x):
    return jax.nn.sigmoid(x)


GP = BF16


def _operands(a, b, prec):
    if prec is BF16:
        return a.astype(BF16), b.astype(BF16), None
    return a, b, prec


def _dot(a, b, prec=None):
    a, b, prec = _operands(a, b, prec)
    return jnp.dot(a, b, preferred_element_type=F32, precision=prec)


def _dot_nt(a, b, prec=None):
    a, b, prec = _operands(a, b, prec)
    return lax.dot_general(a, b, (((1,), (1,)), ((), ())), preferred_element_type=F32, precision=prec)


def _dot_tn(a, b, prec=None):
    a, b, prec = _operands(a, b, prec)
    return lax.dot_general(a, b, (((0,), (0,)), ((), ())), preferred_element_type=F32, precision=prec)


def _lockstep(gens):
    gens = list(gens)
    while gens:
        alive = []
        for g in gens:
            try:
                next(g)
                alive.append(g)
            except StopIteration:
                pass
        gens = alive


def _rowsum(x):
    return jnp.sum(x, axis=-1, keepdims=True)


def _colsum(x):
    return jnp.sum(x, axis=0, keepdims=True)


def _mod(mod_ref, b_ref, k):
    return mod_ref[:, k * D:(k + 1) * D] + b_ref[:, k * D:(k + 1) * D]


def _const(shape):
    nd = len(shape)
    return pl.BlockSpec(shape, lambda *_: (0,) * nd)


def _const1(shape):
    nd = len(shape)
    return pl.BlockSpec(shape, lambda *_: (0,) * nd, pipeline_mode=pl.Buffered(1))


PEER_FLIPS = [(dx, dy, dc) for dx in (0, 1) for dy in (0, 1) for dc in (0, 1)][1:]


def _after(x, token):
    return x + token[0:1, 0:1].astype(x.dtype).reshape((1,) * x.ndim)


def _exchange(name, srcs, per_dest, seed_only=(), with_token=False):
    n = len(srcs)
    out_shape = []
    for a, pd in zip(srcs, per_dest):
        blk = a.shape[1:] if pd else a.shape
        out_shape.append(jax.ShapeDtypeStruct((N_DEV,) + tuple(blk), a.dtype))

    def body(*refs):
        src = refs[:n]
        dst = refs[n:2 * n]
        send_sems, recv_sems, local_sems = refs[-3:]
        if with_token:
            refs[2 * n][...] = jnp.zeros((8, LANES), F32)
        x, y, c = lax.axis_index("x"), lax.axis_index("y"), lax.axis_index("c")
        me = 4 * x + 2 * y + c

        def piece(i, j):
            return src[i].at[j] if per_dest[i] else src[i]

        copies = []
        for k, (dx, dy, dc) in enumerate(PEER_FLIPS):
            px = 1 - x if dx else x
            py = 1 - y if dy else y
            pc = 1 - c if dc else c
            pj = 4 * px + 2 * py + pc
            for i in range(n):
                if i in seed_only:
                    continue
                cp = pltpu.make_async_remote_copy(
                    src_ref=piece(i, pj), dst_ref=dst[i].at[me],
                    send_sem=send_sems.at[k * n + i], recv_sem=recv_sems.at[k * n + i],
                    device_id=(px, py, pc), device_id_type=MESH)
                cp.start()
                arrive = pltpu.make_async_remote_copy(
                    src_ref=piece(i, pj), dst_ref=dst[i].at[pj],
                    send_sem=send_sems.at[k * n + i], recv_sem=recv_sems.at[k * n + i],
                    device_id=(px, py, pc), device_id_type=MESH)
                copies.append((cp, arrive))
        own = []
        for i in range(n):
            lc = pltpu.make_async_copy(piece(i, me), dst[i].at[me], local_sems.at[i])
            lc.start()
            own.append(lc)
        for cp, arrive in copies:
            arrive.wait_recv()
        for cp, arrive in copies:
            cp.wait_send()
        for lc in own:
            lc.wait()

    any_spec = pl.BlockSpec(memory_space=pl.ANY)
    out_specs = [any_spec] * n
    if with_token:
        out_shape.append(jax.ShapeDtypeStruct((8, LANES), F32))
        out_specs.append(pl.BlockSpec(memory_space=pltpu.VMEM))
    return pl.pallas_call(
        body, name=name, out_shape=tuple(out_shape),
        in_specs=[any_spec] * n, out_specs=tuple(out_specs),
        scratch_shapes=[pltpu.SemaphoreType.DMA((7 * n,)), pltpu.SemaphoreType.DMA((7 * n,)),
                        pltpu.SemaphoreType.DMA((n,))],
        compiler_params=pltpu.CompilerParams(has_side_effects=True),
    )(*srcs)


CHIP_FLIPS = [(0, 1), (1, 0), (1, 1)]
LEVEL_ONE = [k for k, (dx, dy, dc) in enumerate(PEER_FLIPS) if (dx, dy, dc) == (0, 0, 1) or dc == 0]


def _chip_peers(x, y):
    return [(1 - x if dx else x, 1 - y if dy else y) for dx, dy in CHIP_FLIPS]


def _gather_two_level(name, srcs, seed_only=()):
    n = len(srcs)
    live = [i for i in range(n) if i not in seed_only]

    def body(*refs):
        src, dst = refs[:n], refs[n:2 * n]
        send_sems, recv_sems, local_sems = refs[-3:]
        x, y, c = lax.axis_index("x"), lax.axis_index("y"), lax.axis_index("c")
        me = 4 * x + 2 * y + c
        sibling = (x, y, 1 - c)
        chips = _chip_peers(x, y)

        def copy(k, i, src_ref, slot, to):
            return pltpu.make_async_remote_copy(
                src_ref=src_ref, dst_ref=dst[i].at[slot], send_sem=send_sems.at[k * n + i],
                recv_sem=recv_sems.at[k * n + i], device_id=to, device_id_type=MESH)

        first = []
        for i in live:
            first.append(copy(0, i, src[i], me, sibling))
            first += [copy(1 + j, i, src[i], me, (px, py, c)) for j, (px, py) in enumerate(chips)]
        for cp in first:
            cp.start()
        own = [pltpu.make_async_copy(src[i], dst[i].at[me], local_sems.at[i]) for i in range(n)]
        for cp in own:
            cp.start()
        passed = []
        for j, (px, py) in enumerate(chips):
            slot = 4 * px + 2 * py + c
            for i in live:
                copy(1 + j, i, src[i], slot, (px, py, c)).wait_recv()
                fwd = copy(4 + j, i, dst[i].at[slot], slot, sibling)
                fwd.start()
                passed.append(fwd)
        for i in live:
            copy(0, i, src[i], 4 * x + 2 * y + 1 - c, sibling).wait_recv()
            for j, (px, py) in enumerate(chips):
                copy(4 + j, i, src[i], 4 * px + 2 * py + 1 - c, sibling).wait_recv()
        for cp in first + passed:
            cp.wait_send()
        for cp in own:
            cp.wait()

    any_spec = pl.BlockSpec(memory_space=pl.ANY)
    return pl.pallas_call(
        body, name=name, out_shape=tuple(jax.ShapeDtypeStruct((N_DEV,) + a.shape, a.dtype) for a in srcs),
        in_specs=[any_spec] * n, out_specs=tuple([any_spec] * n),
        scratch_shapes=[pltpu.SemaphoreType.DMA((7 * n,)), pltpu.SemaphoreType.DMA((7 * n,)),
                        pltpu.SemaphoreType.DMA((n,))],
        compiler_params=pltpu.CompilerParams(has_side_effects=True),
    )(*srcs)


def _relay_to_sibling(name, lands):
    n = len(lands)

    def body(*refs):
        land = refs[n:2 * n]
        send_sems, recv_sems = refs[-2:]
        x, y, c = lax.axis_index("x"), lax.axis_index("y"), lax.axis_index("c")
        sibling = (x, y, 1 - c)
        sends = []
        for j, (px, py) in enumerate(_chip_peers(x, y)):
            slot = 4 * px + 2 * py + c
            for i in range(n):
                cp = pltpu.make_async_remote_copy(
                    src_ref=land[i].at[slot], dst_ref=land[i].at[slot], send_sem=send_sems.at[j * n + i],
                    recv_sem=recv_sems.at[j * n + i], device_id=sibling, device_id_type=MESH)
                cp.start()
                sends.append(cp)
        for j, (px, py) in enumerate(_chip_peers(x, y)):
            slot = 4 * px + 2 * py + 1 - c
            for i in range(n):
                pltpu.make_async_remote_copy(
                    src_ref=land[i].at[slot], dst_ref=land[i].at[slot], send_sem=send_sems.at[j * n + i],
                    recv_sem=recv_sems.at[j * n + i], device_id=sibling, device_id_type=MESH).wait_recv()
        for cp in sends:
            cp.wait_send()

    any_spec = pl.BlockSpec(memory_space=pl.ANY)
    return pl.pallas_call(
        body, name=name, out_shape=tuple(jax.ShapeDtypeStruct(a.shape, a.dtype) for a in lands),
        in_specs=[any_spec] * n, out_specs=tuple([any_spec] * n),
        input_output_aliases={i: i for i in range(n)},
        scratch_shapes=[pltpu.SemaphoreType.DMA((3 * n,)), pltpu.SemaphoreType.DMA((3 * n,))],
        compiler_params=pltpu.CompilerParams(has_side_effects=True),
    )(*lands)


HBM_SPEC = pl.BlockSpec(memory_space=pltpu.HBM)
SEM_SPEC = pl.BlockSpec(memory_space=pltpu.SEMAPHORE)
DATAFLOW = pltpu.SideEffectType.DATAFLOW_SIDE_EFFECTING


def _peers(only=None):
    x, y, c = lax.axis_index("x"), lax.axis_index("y"), lax.axis_index("c")
    out = []
    for k, (dx, dy, dc) in enumerate(PEER_FLIPS):
        if only is not None and k not in only:
            continue
        px = 1 - x if dx else x
        py = 1 - y if dy else y
        pc = 1 - c if dc else c
        out.append((k, (px, py, pc), 4 * px + 2 * py + pc))
    return 4 * x + 2 * y + c, out


def _exchange_start(name, srcs, lands, per_dest, only=None):
    n = len(srcs)

    def body(*refs):
        src, land = refs[:n], refs[n:2 * n]
        send_sems, recv_sems = refs[2 * n], refs[2 * n + 1]
        token = refs[-1]
        me, peers = _peers(only)
        for k, peer, pj in peers:
            for i in range(n):
                pltpu.make_async_remote_copy(
                    src_ref=src[i].at[pj] if per_dest[i] else src[i], dst_ref=land[i].at[me],
                    send_sem=send_sems.at[k * n + i], recv_sem=recv_sems.at[k * n + i],
                    device_id=peer, device_id_type=MESH).start()
        token[...] = jnp.zeros((8, LANES), F32)

    arrays = list(srcs) + list(lands)
    return pl.pallas_call(
        body, name=name,
        out_shape=(pltpu.SemaphoreType.DMA((7 * n,)), pltpu.SemaphoreType.DMA((7 * n,)),
                   *[pltpu.HBM(a.shape, a.dtype) for a in arrays], jax.ShapeDtypeStruct((8, LANES), F32)),
        in_specs=[HBM_SPEC] * (2 * n),
        out_specs=(SEM_SPEC, SEM_SPEC, *[HBM_SPEC] * (2 * n), pl.BlockSpec(memory_space=pltpu.VMEM)),
        input_output_aliases={i: 2 + i for i in range(2 * n)},
        compiler_params=pltpu.CompilerParams(has_side_effects=DATAFLOW),
    )(*[pltpu.with_memory_space_constraint(a, pltpu.HBM) for a in arrays])


def _exchange_wait(name, started, per_dest, after, only=None):
    n = (len(started) - 3) // 2
    send_sems, recv_sems = started[0], started[1]
    arrays = list(started[2:2 + 2 * n])

    def body(*refs):
        src, land = refs[:n], refs[n:2 * n]
        send, recv = refs[2 * n], refs[2 * n + 1]
        me, peers = _peers(only)
        for k, peer, pj in peers:
            for i in range(n):
                cp = pltpu.make_async_remote_copy(
                    src_ref=src[i].at[pj] if per_dest[i] else src[i], dst_ref=land[i].at[pj],
                    send_sem=send.at[k * n + i], recv_sem=recv.at[k * n + i],
                    device_id=peer, device_id_type=MESH)
                cp.wait_send()
                cp.wait_recv()

    outs = pl.pallas_call(
        body, name=name,
        out_shape=tuple(pltpu.HBM(a.shape, a.dtype) for a in arrays),
        in_specs=[HBM_SPEC] * (2 * n) + [SEM_SPEC, SEM_SPEC] + [pl.BlockSpec(memory_space=pl.ANY)] * len(after),
        out_specs=tuple([HBM_SPEC] * (2 * n)),
        input_output_aliases={i: i for i in range(2 * n)},
        compiler_params=pltpu.CompilerParams(has_side_effects=DATAFLOW),
    )(*arrays, send_sems, recv_sems, *after)
    return outs[:n], outs[n:]


def _mod_shard(c_all, w_ada):
    def body(c_ref, w_ref, o_ref):
        cv = c_ref[...]
        ca = cv * _sig(cv)
        o_ref[...] = _dot(ca.astype(BF16), w_ref[...].astype(BF16))

    return pl.pallas_call(
        body, name="mod_shard", out_shape=jax.ShapeDtypeStruct((N_DEV, w_ada.shape[1]), F32),
        compiler_params=_params(),
    )(c_all, w_ada)


def _fwd_in(x, nw1, modnb, bada, w_main, w_ba):
    def body(x_ref, nw_ref, mod_ref, b_ref, wm_ref, wb_ref, pm_ref, pb_ref, hb_ref):
        xv = x_ref[...]
        r = lax.rsqrt(jnp.mean(xv * xv, axis=-1, keepdims=True) + EPS)
        h = (xv * r * nw_ref[...]) * (1.0 + _mod(mod_ref, b_ref, 1)) + _mod(mod_ref, b_ref, 0)
        hb = h.astype(BF16)
        hb_ref[...] = hb
        pm_ref[...] = _dot_nt(hb, wm_ref[...])
        pb_ref[...] = _dot_nt(hb, wb_ref[...])

    return pl.pallas_call(
        body, name="fwd_in", grid=(NT,),
        in_specs=[pl.BlockSpec((TM, D), lambda i: (i, 0)), _const((1, D)), _const((1, 6 * D)), _const((1, 6 * D)),
                  _const((NMAIN, D)), _const((LANES, D))],
        out_specs=(pl.BlockSpec((TM, NMAIN), lambda i: (i, 0)), pl.BlockSpec((TM, LANES), lambda i: (i, 0)),
                   pl.BlockSpec((TM, D), lambda i: (i, 0))),
        out_shape=(jax.ShapeDtypeStruct((S, NMAIN), F32), jax.ShapeDtypeStruct((S, LANES), F32),
                   jax.ShapeDtypeStruct((S, D), BF16)),
        compiler_params=_params(dimension_semantics=("arbitrary",)),
    )(x, nw1, modnb, bada, w_main, w_ba)


def _group_mean_matrix():
    ii = lax.broadcasted_iota(jnp.int32, (CW, CW), 0) // GSZ
    jj = lax.broadcasted_iota(jnp.int32, (CW, CW), 1) // GSZ
    return jnp.where(ii == jj, 1.0 / GSZ, 0.0).astype(F32)


SUB = 8
SHIFT_ROWS = HALO + TM - SUB


def _fill_shifted(buf, sh):
    for b in range(1, SUB):
        sh[b - 1] = buf[b:b + SHIFT_ROWS, :]


def _rows_at(buf, sh, off):
    a, b = divmod(off, SUB)
    if b == 0:
        return buf[off:off + TM, :]
    return sh[b - 1, SUB * a:SUB * a + TM, :]


def _group_mean(x, pm):
    hi = x.astype(BF16)
    r1 = x - hi.astype(F32)
    mid = r1.astype(BF16)
    lo = (r1 - mid.astype(F32)).astype(BF16)
    return _dot(hi, pm) + _dot(mid, pm) + _dot(lo, pm)


def _conf_fwd(p_main, conv_w, conv_b, gn_w, gn_b):
    def body(a_ref, g_ref, w_ref, b_ref, gw_ref, gb_ref, y_ref, oa_ref, ubuf, ush):
        i = pl.program_id(0)

        @pl.when(i == 0)
        def _():
            ubuf[0:HALO, :] = jnp.zeros((HALO, CW), F32)

        ubuf[HALO:HALO + TM, :] = a_ref[...] * _sig(g_ref[...])
        _fill_shifted(ubuf, ush)
        acc = jnp.zeros((TM, CW), F32) + b_ref[...]
        for k in range(KC):
            acc = acc + w_ref[k:k + 1, :] * _rows_at(ubuf, ush, HALO - (KC - 1) + k)
        y_ref[...] = acc
        ubuf[0:HALO, :] = ubuf[TM:TM + HALO, :]
        pm = _group_mean_matrix().astype(BF16)
        dlt = acc - _group_mean(acc, pm)
        var = _group_mean(dlt * dlt, pm)
        o = dlt * lax.rsqrt(var + EPS) * gw_ref[...] + gb_ref[...]
        oa_ref[...] = o * _sig(o)

    return pl.pallas_call(
        body, name="conf_fwd", grid=(NT,),
        in_specs=[pl.BlockSpec((TM, CW), lambda i: (i, 0)), pl.BlockSpec((TM, CW), lambda i: (i, 1)),
                  _const((KC, CW)), _const((1, CW)), _const((1, CW)), _const((1, CW))],
        out_specs=(pl.BlockSpec((TM, CW), lambda i: (i, 0)), pl.BlockSpec((TM, CW), lambda i: (i, 0))),
        out_shape=(jax.ShapeDtypeStruct((S, CW), F32), jax.ShapeDtypeStruct((S, CW), F32)),
        scratch_shapes=[pltpu.VMEM((HALO + TM, CW), F32), pltpu.VMEM((SUB - 1, SHIFT_ROWS, CW), F32)],
        compiler_params=_params(dimension_semantics=("arbitrary",)),
    )(p_main, p_main, conv_w, conv_b, gn_w, gn_b)


def _tri_iota():
    ii = lax.broadcasted_iota(jnp.int32, (CL, CL), 0)
    jj = lax.broadcasted_iota(jnp.int32, (CL, CL), 1)
    return ii, jj


def _gdn_gates(ba, alog_l, dt_l):
    beta_all = _sig(ba)
    xg = ba + dt_l
    sp = jnp.maximum(xg, 0.0) + jnp.log(1.0 + jnp.exp(-jnp.abs(xg)))
    neg_a = -jnp.exp(alog_l)
    return beta_all, neg_a * sp, xg, neg_a


def _gdn_cumsum(g_all):
    ii, jj = _tri_iota()
    low = jnp.where(ii >= jj, 1.0, 0.0).astype(F32)
    gcum = _dot(low, g_all, HI)
    return gcum, jnp.transpose(gcum)


def _unit_lower_inverses(mats):
    ii, jj = _tri_iota()
    eye = jnp.where(ii == jj, 1.0, 0.0).astype(F32)
    ts = [eye - a for a in mats]
    ps = [_dot(a, a, HI) for a in mats]
    for _ in range(4):
        ts = [t + _dot(t, p, HI) for t, p in zip(ts, ps)]
        ps = [_dot(p, p, HI) for p in ps]
    return [t + _dot(t, p, HI) for t, p in zip(ts, ps)]


def _head_terms(qh, kh, beta, gcol, grow):
    ii, jj = _tri_iota()
    causal = ii >= jj
    strict = ii > jj
    rq = lax.rsqrt(_rowsum(qh * qh) + EPS)
    rk = lax.rsqrt(_rowsum(kh * kh) + EPS)
    qn = qh * rq
    kn = kh * rk
    qs = qn * QSCALE
    decay = jnp.where(causal, jnp.exp(jnp.where(causal, gcol - grow, 0.0)), 0.0)
    gam = jnp.exp(gcol)
    gl = gcol[CL - 1:CL, :]
    kds = jnp.exp(gl - gcol)
    cd = jnp.exp(gl)
    kb = kn * beta
    a = jnp.where(strict, _dot_nt(kb, kn, GP) * decay, 0.0)
    qk = jnp.where(causal, _dot_nt(qs, kn, GP) * decay, 0.0)
    return dict(rq=rq, rk=rk, qn=qn, kn=kn, qs=qs, decay=decay, gam=gam, kds=kds, cd=cd, kb=kb, a=a, qk=qk,
                causal=causal, strict=strict)


def _short_conv(w_ref, buf, rows=CL):
    acc = w_ref[0:1, :] * buf[SH - KS + 1:SH - KS + 1 + rows, :]
    for k in range(1, KS):
        off = SH - (KS - 1) + k
        acc = acc + w_ref[k:k + 1, :] * buf[off:off + rows, :]
    return acc


CPS = 4
TG = CPS * CL


def _gdn_prep(p_main, p_ba, gdn_conv_w, alog_l, dt_l):
    def body(q_ref, k_ref, v_ref, qh_ref, kh_ref, vh_ref, ba_ref, w_ref, al_ref, dt_ref,
             wo_ref, uo_ref, qg_ref, kd_ref, qk_ref, cd_ref, t_ref, xbuf):
        i = pl.program_id(0)
        first = i == 0
        xbuf[0:SH, 0:GW] = jnp.where(first, 0.0, qh_ref[...])
        xbuf[0:SH, GW:2 * GW] = jnp.where(first, 0.0, kh_ref[...])
        xbuf[0:SH, 2 * GW:3 * GW] = jnp.where(first, 0.0, vh_ref[...])
        xbuf[SH:SH + TG, 0:GW] = q_ref[...]
        xbuf[SH:SH + TG, GW:2 * GW] = k_ref[...]
        xbuf[SH:SH + TG, 2 * GW:3 * GW] = v_ref[...]
        conv = _short_conv(w_ref, xbuf, TG)
        qkv = conv * _sig(conv)
        beta_all, g_all, _, _ = _gdn_gates(ba_ref[...], al_ref[...], dt_ref[...])
        lane = lax.broadcasted_iota(jnp.int32, (8, LANES), 1)
        cums = [_gdn_cumsum(g_all[cc * CL:(cc + 1) * CL, :]) for cc in range(CPS)]
        pairs = [(cc, h) for cc in range(CPS) for h in range(NH)]
        terms, vbs = [], []
        for cc, h in pairs:
            r0, lo = cc * CL, h * DH
            beta = beta_all[r0:r0 + CL, h:h + 1]
            gcum, gcum_t = cums[cc]
            terms.append(_head_terms(qkv[r0:r0 + CL, lo:lo + DH], qkv[r0:r0 + CL, GW + lo:GW + lo + DH], beta,
                                     gcum[:, NH + h:NH + h + 1], gcum_t[NH + h:NH + h + 1, :]))
            vbs.append(qkv[r0:r0 + CL, 2 * GW + lo:2 * GW + lo + DH] * beta)
        invs = _unit_lower_inverses([f["a"] for f in terms])
        cds = [jnp.zeros((8, LANES), F32) for _ in range(CPS)]
        for (cc, h), f, t, vb in zip(pairs, terms, invs, vbs):
            r0, lo = cc * CL, h * DH
            t_ref[cc, h] = t
            uo_ref[r0:r0 + CL, lo:lo + DH] = _dot(t, vb, GP)
            wo_ref[r0:r0 + CL, lo:lo + DH] = _dot(t, f["kb"] * f["gam"], GP).astype(BF16)
            qg_ref[r0:r0 + CL, lo:lo + DH] = (f["qs"] * f["gam"]).astype(BF16)
            kd_ref[r0:r0 + CL, lo:lo + DH] = (f["kn"] * f["kds"]).astype(BF16)
            qk_ref[cc, h] = f["qk"].astype(BF16)
            cds[cc] = cds[cc] + jnp.where(lane == h, f["cd"], 0.0)
        for cc in range(CPS):
            cd_ref[cc] = cds[cc]

    col = lambda j: pl.BlockSpec((TG, GW), lambda i: (i, j))
    halo = lambda j: pl.BlockSpec((SH, GW), lambda i: (jnp.maximum(i * (TG // SH) - 1, 0), j))
    tile = lambda: pl.BlockSpec((TG, GW), lambda i: (i, 0))
    sq = lambda: pl.BlockSpec((CPS, NH, CL, CL), lambda i: (i, 0, 0, 0))
    return pl.pallas_call(
        body, name="gdn_prep", grid=(NCH // CPS,),
        in_specs=[col(2), col(3), col(4), halo(2), halo(3), halo(4), pl.BlockSpec((TG, LANES), lambda i: (i, 0)),
                  _const((KS, 3 * GW)), _const((1, LANES)), _const((1, LANES))],
        out_specs=(tile(), tile(), tile(), tile(), sq(), pl.BlockSpec((CPS, 8, LANES), lambda i: (i, 0, 0)), sq()),
        out_shape=(jax.ShapeDtypeStruct((S, GW), BF16), jax.ShapeDtypeStruct((S, GW), F32),
                   jax.ShapeDtypeStruct((S, GW), BF16), jax.ShapeDtypeStruct((S, GW), BF16),
                   jax.ShapeDtypeStruct((NCH, NH, CL, CL), BF16), jax.ShapeDtypeStruct((NCH, 8, LANES), F32),
                   jax.ShapeDtypeStruct((NCH, NH, CL, CL), F32)),
        scratch_shapes=[pltpu.VMEM((SH + TG, 3 * GW), F32)],
        compiler_params=_params(dimension_semantics=("arbitrary",)),
    )(p_main, p_main, p_main, p_main, p_main, p_main, p_ba, gdn_conv_w, alog_l, dt_l)


def _gdn_scan(w_o, u_o, qg, kd, qk, cd, p_main, gdn_nw):
    def body(w_ref, u_ref, qg_ref, kd_ref, qk_ref, cd_ref, z_ref, nw_ref, ob_ref, o_ref, sin_ref, state):
        n = pl.program_id(0)

        @pl.when(n == 0)
        def _():
            state[...] = jnp.zeros((NH, DH, DH), F32)

        def head(h):
            lo = h * DH
            st = state[h]
            sin_ref[0, h] = st
            sb = st.astype(BF16)
            v_new = u_ref[:, lo:lo + DH] - _dot(w_ref[:, lo:lo + DH], sb)
            yield
            vb = v_new.astype(BF16)
            o = _dot(qg_ref[:, lo:lo + DH], sb) + _dot(qk_ref[0, h], vb)
            state[h] = st * cd_ref[0, 0:1, h:h + 1] + _dot_tn(kd_ref[:, lo:lo + DH], vb)
            yield
            o_ref[:, lo:lo + DH] = o
            r = lax.rsqrt(jnp.mean(o * o, axis=-1, keepdims=True) + EPS)
            zh = z_ref[:, lo:lo + DH]
            ob_ref[:, lo:lo + DH] = o * r * nw_ref[...] * (zh * _sig(zh))

        _lockstep(head(h) for h in range(NH))

    tile = lambda: pl.BlockSpec((CL, GW), lambda n: (n, 0))
    return pl.pallas_call(
        body, name="gdn_scan", grid=(NCH,),
        in_specs=[tile(), tile(), tile(), tile(), pl.BlockSpec((1, NH, CL, CL), lambda n: (n, 0, 0, 0)),
                  pl.BlockSpec((1, 8, LANES), lambda n: (n, 0, 0)), pl.BlockSpec((CL, GW), lambda n: (n, 5)),
                  _const((1, DH))],
        out_specs=(tile(), tile(), pl.BlockSpec((1, NH, DH, DH), lambda n: (n, 0, 0, 0))),
        out_shape=(jax.ShapeDtypeStruct((S, GW), F32), jax.ShapeDtypeStruct((S, GW), F32),
                   jax.ShapeDtypeStruct((NCH, NH, DH, DH), F32)),
        scratch_shapes=[pltpu.VMEM((NH, DH, DH), F32)],
        compiler_params=_params(dimension_semantics=("arbitrary",)),
    )(w_o, u_o, qg, kd, qk, cd, p_main, gdn_nw)


def _fwd_out(out_a, out_b, x, modnb, bada, w_out):
    def body(oa_ref, ob_ref, x_ref, mod_ref, b_ref, w_ref, x1_ref, mix_ref, oab_ref):
        oa = oa_ref[...].astype(BF16)
        ob = ob_ref[...].astype(BF16)
        oab_ref[:, 0:CW] = oa
        oab_ref[:, CW:D] = ob
        mix = _dot(oa, w_ref[0:CW, :]) + _dot(ob, w_ref[CW:D, :])
        mix_ref[...] = mix
        x1_ref[...] = x_ref[...] + _mod(mod_ref, b_ref, 2) * mix

    tile = lambda w: pl.BlockSpec((TM, w), lambda i: (i, 0))
    return pl.pallas_call(
        body, name="fwd_out", grid=(NT,),
        in_specs=[tile(CW), tile(GW), tile(D), _const((1, 6 * D)), _const((1, 6 * D)), _const((D, D))],
        out_specs=(tile(D), tile(D), tile(D)),
        out_shape=(jax.ShapeDtypeStruct((S, D), F32), jax.ShapeDtypeStruct((S, D), F32),
                   jax.ShapeDtypeStruct((S, D), BF16)),
        compiler_params=_params(dimension_semantics=("arbitrary",)),
    )(out_a, out_b, x, modnb, bada, w_out)


TF = 128
FFN_STATS = 8


def _ffn_fwd_bwd(x1, tgt, modnb, bada, nw2, nfw, w_fi, w_fo):
    def body(x1_ref, tgt_ref, mod_ref, b_ref, nw2_ref, nfw_ref, wi_ref, wo_ref,
             dx1_ref, hb_ref, act_ref, dffn_ref, df_ref, st_ref):
        i = pl.program_id(0)

        @pl.when(i == 0)
        def _():
            st_ref[...] = jnp.zeros((FFN_STATS, D), F32)

        sh2, sc2, gt2 = _mod(mod_ref, b_ref, 3), _mod(mod_ref, b_ref, 4), _mod(mod_ref, b_ref, 5)
        x1v = x1_ref[...]
        r2 = lax.rsqrt(jnp.mean(x1v * x1v, axis=-1, keepdims=True) + EPS)
        xr2 = x1v * r2
        xn2 = xr2 * nw2_ref[...]
        hb = (xn2 * (1.0 + sc2) + sh2).astype(BF16)
        hb_ref[...] = hb
        fg, fu, sg = [], [], []
        ffn = jnp.zeros((TF, D), F32)
        for j in range(4):
            fgj = _dot_nt(hb, wi_ref[j])
            fuj = _dot_nt(hb, wi_ref[j + 4])
            sj = _sig(fgj)
            aj = (fgj * sj * fuj).astype(BF16)
            act_ref[j] = aj
            ffn = ffn + _dot(aj, wo_ref[j])
            fg.append(fgj)
            fu.append(fuj)
            sg.append(sj)
        x2 = x1v + gt2 * ffn
        r3 = lax.rsqrt(jnp.mean(x2 * x2, axis=-1, keepdims=True) + EPS)
        xr3 = x2 * r3
        err = xr3 * nfw_ref[...] - tgt_ref[...]
        loss = 0.5 * jnp.sum(jnp.mean(err * err, axis=-1, keepdims=True), axis=0, keepdims=True)
        dy = err * (1.0 / D)
        st_ref[0:1, :] += _colsum(dy * xr3)
        dyr = dy * nfw_ref[...]
        dx2 = r3 * (dyr - xr3 * jnp.mean(dyr * xr3, axis=-1, keepdims=True))
        st_ref[1:2, :] += _colsum(dx2 * ffn)
        st_ref[5:6, :] += jnp.broadcast_to(loss, (1, D))
        dffn = (gt2 * dx2).astype(BF16)
        dffn_ref[...] = dffn
        dh = jnp.zeros((TF, D), F32)
        for j in range(4):
            dact = _dot_nt(dffn, wo_ref[j])
            dfg = (dact * fu[j] * (sg[j] * (1.0 + fg[j] * (1.0 - sg[j])))).astype(BF16)
            dfu = (dact * (fg[j] * sg[j])).astype(BF16)
            df_ref[j] = dfg
            df_ref[j + 4] = dfu
            dh = dh + _dot(dfg, wi_ref[j]) + _dot(dfu, wi_ref[j + 4])
        st_ref[2:3, :] += _colsum(dh)
        st_ref[3:4, :] += _colsum(dh * xn2)
        dxn = dh * (1.0 + sc2)
        st_ref[4:5, :] += _colsum(dxn * xr2)
        dxr = dxn * nw2_ref[...]
        dx1_ref[...] = dx2 + r2 * (dxr - xr2 * jnp.mean(dxr * xr2, axis=-1, keepdims=True))

    tile = lambda w: pl.BlockSpec((TF, w), lambda i: (i, 0))
    return pl.pallas_call(
        body, name="ffn_fwd_bwd", grid=(S // TF,),
        in_specs=[tile(D), tile(D), _const((1, 6 * D)), _const((1, 6 * D)), _const((1, D)), _const((1, D)),
                  _const1((N_DEV, FB, D)), _const1((4, FB, D))],
        out_specs=(tile(D), tile(D), pl.BlockSpec((4, TF, FB), lambda i: (0, i, 0)), tile(D),
                   pl.BlockSpec((N_DEV, TF, FB), lambda i: (0, i, 0)), _const((FFN_STATS, D))),
        out_shape=(jax.ShapeDtypeStruct((S, D), F32), jax.ShapeDtypeStruct((S, D), BF16),
                   jax.ShapeDtypeStruct((4, S, FB), BF16), jax.ShapeDtypeStruct((S, D), BF16),
                   jax.ShapeDtypeStruct((N_DEV, S, FB), BF16), jax.ShapeDtypeStruct((FFN_STATS, D), F32)),
        compiler_params=_params(44, dimension_semantics=("arbitrary",)),
    )(x1, tgt, modnb, bada, nw2, nfw, w_fi, w_fo)


def _grad_w(name, a, b, nb):
    m, n = a.shape[1], b.shape[1]

    def body(a_ref, b_ref, o_ref):
        o_ref[...] = _dot_tn(a_ref[...], b_ref[...]).astype(BF16)

    return pl.pallas_call(
        body, name=name, grid=(m // nb,),
        in_specs=[pl.BlockSpec((S, nb), lambda j: (0, j)), _const((S, n))],
        out_specs=pl.BlockSpec((nb, n), lambda j: (j, 0)),
        out_shape=jax.ShapeDtypeStruct((m, n), BF16),
        compiler_params=_params(dimension_semantics=("arbitrary",)),
    )(a, b)


def _grad_w_ffn_in(hb2, df):
    def body(a_ref, b_ref, o_ref):
        o_ref[0] = _dot_tn(b_ref[0], a_ref[...]).astype(BF16)

    return pl.pallas_call(
        body, name="grad_w_ffn_in", grid=(N_DEV,),
        in_specs=[_const((S, D)), pl.BlockSpec((1, S, FB), lambda j: (j, 0, 0))],
        out_specs=pl.BlockSpec((1, FB, D), lambda j: (j, 0, 0)),
        out_shape=jax.ShapeDtypeStruct((N_DEV, FB, D), BF16),
        compiler_params=_params(dimension_semantics=("arbitrary",)),
    )(hb2, df)


def _grad_w_ffn_out(act, dffn):
    def body(a_ref, b_ref, o_ref):
        o_ref[0] = _dot_tn(a_ref[0], b_ref[...]).astype(BF16)

    return pl.pallas_call(
        body, name="grad_w_ffn_out", grid=(4,),
        in_specs=[pl.BlockSpec((1, S, FB), lambda j: (j, 0, 0)), _const((S, D))],
        out_specs=pl.BlockSpec((1, FB, D), lambda j: (j, 0, 0)),
        out_shape=jax.ShapeDtypeStruct((4, FB, D), BF16),
        compiler_params=_params(dimension_semantics=("arbitrary",)),
    )(act, dffn)


def _bwd_out(dx1, mix, modnb, bada, w_out):
    def body(dx_ref, mix_ref, mod_ref, b_ref, w_ref, dmix_ref, doa_ref, dob_ref, st_ref):
        i = pl.program_id(0)

        @pl.when(i == 0)
        def _():
            st_ref[...] = jnp.zeros((8, D), F32)

        dx = dx_ref[...]
        st_ref[0:1, :] += _colsum(dx * mix_ref[...])
        dmix = (_mod(mod_ref, b_ref, 2) * dx).astype(BF16)
        dmix_ref[...] = dmix
        doa_ref[...] = _dot_nt(dmix, w_ref[0:CW, :])
        dob_ref[...] = _dot_nt(dmix, w_ref[CW:D, :])

    tile = lambda w: pl.BlockSpec((TM, w), lambda i: (i, 0))
    return pl.pallas_call(
        body, name="bwd_out", grid=(NT,),
        in_specs=[tile(D), tile(D), _const((1, 6 * D)), _const((1, 6 * D)), _const((D, D))],
        out_specs=(tile(D), tile(CW), tile(GW), _const((8, D))),
        out_shape=(jax.ShapeDtypeStruct((S, D), BF16), jax.ShapeDtypeStruct((S, CW), F32),
                   jax.ShapeDtypeStruct((S, GW), F32), jax.ShapeDtypeStruct((8, D), F32)),
        compiler_params=_params(dimension_semantics=("arbitrary",)),
    )(dx1, mix, modnb, bada, w_out)


CONF_STATS = 40


def _conf_bwd(d_out_a, y, p_main, conv_w, gn_w, gn_b):
    def body(do_ref, y_ref, a_ref, g_ref, ah_ref, gh_ref, w_ref, gw_ref, gb_ref, dp_ref, st_ref,
             ubuf, dybuf, ush, dysh):
        i = pl.program_id(0)

        @pl.when(i == 0)
        def _():
            st_ref[...] = jnp.zeros((CONF_STATS, CW), F32)
            dybuf[TM:TM + HALO, :] = jnp.zeros((HALO, CW), F32)

        pm = _group_mean_matrix().astype(BF16)
        yv = y_ref[...]
        dlt = yv - _group_mean(yv, pm)
        rstd = lax.rsqrt(_group_mean(dlt * dlt, pm) + EPS)
        un = dlt * rstd
        o = un * gw_ref[...] + gb_ref[...]
        so = _sig(o)
        d_o = do_ref[...] * (so * (1.0 + o * (1.0 - so)))
        st_ref[33:34, :] += _colsum(d_o)
        st_ref[32:33, :] += _colsum(d_o * un)
        dun = d_o * gw_ref[...]
        dy = rstd * (dun - _group_mean(dun, pm) - un * _group_mean(dun * un, pm))
        st_ref[31:32, :] += _colsum(dy)
        dybuf[0:TM, :] = dy
        _fill_shifted(dybuf, dysh)

        a = a_ref[...]
        sg = _sig(g_ref[...])
        first = i == NT - 1
        ubuf[0:HALO, :] = jnp.where(first, 0.0, ah_ref[...] * _sig(gh_ref[...]))
        ubuf[HALO:HALO + TM, :] = a * sg
        _fill_shifted(ubuf, ush)
        du = jnp.zeros((TM, CW), F32)
        for k in range(KC):
            st_ref[k:k + 1, :] += _colsum(dy * _rows_at(ubuf, ush, HALO - (KC - 1) + k))
            du = du + w_ref[k:k + 1, :] * _rows_at(dybuf, dysh, KC - 1 - k)
        dybuf[TM:TM + HALO, :] = dybuf[0:HALO, :]
        dp_ref[:, 0:CW] = (du * sg).astype(BF16)
        dp_ref[:, CW:2 * CW] = (du * a * sg * (1.0 - sg)).astype(BF16)

    rev = lambda w, j=0: pl.BlockSpec((TM, w), lambda i: (NT - 1 - i, j))
    halo = lambda j: pl.BlockSpec((HALO, CW), lambda i: (jnp.maximum((NT - 1 - i) * (TM // HALO) - 1, 0), j))
    return pl.pallas_call(
        body, name="conf_bwd", grid=(NT,),
        in_specs=[rev(CW), rev(CW), rev(CW, 0), rev(CW, 1), halo(0), halo(1),
                  _const((KC, CW)), _const((1, CW)), _const((1, CW))],
        out_specs=(rev(2 * CW), _const((CONF_STATS, CW))),
        out_shape=(jax.ShapeDtypeStruct((S, 2 * CW), BF16), jax.ShapeDtypeStruct((CONF_STATS, CW), F32)),
        scratch_shapes=[pltpu.VMEM((HALO + TM, CW), F32), pltpu.VMEM((TM + HALO, CW), F32),
                        pltpu.VMEM((SUB - 1, SHIFT_ROWS, CW), F32), pltpu.VMEM((SUB - 1, SHIFT_ROWS, CW), F32)],
        compiler_params=_params(dimension_semantics=("arbitrary",)),
    )(d_out_a, y, p_main, p_main, p_main, p_main, conv_w, gn_w, gn_b)


GDN_STATS = 8


def _gdn_bwd(d_out_b, o_pre, s_in, t_inv, p_main, p_ba, gdn_conv_w, alog_l, dt_l, gdn_nw):
    def body(dob_ref, o_ref, sin_ref, t_ref, q_ref, k_ref, v_ref, z_ref, qh_ref, kh_ref, vh_ref, ba_ref,
             w_ref, al_ref, dt_ref, nw_ref, dp_ref, dba_ref, st_ref, xbuf, dcbuf, dstate):
        n = pl.program_id(0)

        @pl.when(n == 0)
        def _():
            st_ref[...] = jnp.zeros((GDN_STATS, 3 * GW), F32)
            dcbuf[CL:CL + SH, :] = jnp.zeros((SH, 3 * GW), F32)
            dstate[...] = jnp.zeros((NH, DH, DH), F32)

        first = n == NCH - 1
        xbuf[0:SH, 0:GW] = jnp.where(first, 0.0, qh_ref[...])
        xbuf[0:SH, GW:2 * GW] = jnp.where(first, 0.0, kh_ref[...])
        xbuf[0:SH, 2 * GW:3 * GW] = jnp.where(first, 0.0, vh_ref[...])
        xbuf[SH:SH + CL, 0:GW] = q_ref[...]
        xbuf[SH:SH + CL, GW:2 * GW] = k_ref[...]
        xbuf[SH:SH + CL, 2 * GW:3 * GW] = v_ref[...]
        conv = _short_conv(w_ref, xbuf)
        sc = _sig(conv)
        qkv = conv * sc
        ba = ba_ref[...]
        beta_all, g_all, xg, neg_a = _gdn_gates(ba, al_ref[...], dt_ref[...])
        gcum, gcum_t = _gdn_cumsum(g_all)
        lane = lax.broadcasted_iota(jnp.int32, (CL, LANES), 1)
        row = lax.broadcasted_iota(jnp.int32, (CL, 1), 0)
        acc = dict(dgcum=jnp.zeros((CL, LANES), F32), dbeta=jnp.zeros((CL, LANES), F32))

        def head(h):
            lo = h * DH
            qh = qkv[:, lo:lo + DH]
            kh = qkv[:, GW + lo:GW + lo + DH]
            vh = qkv[:, 2 * GW + lo:2 * GW + lo + DH]
            beta = beta_all[:, h:h + 1]
            f = _head_terms(qh, kh, beta, gcum[:, NH + h:NH + h + 1], gcum_t[NH + h:NH + h + 1, :])
            qn, kn, qs, kb, gam, kds, cd, decay = (f[s] for s in ("qn", "kn", "qs", "kb", "gam", "kds", "cd", "decay"))
            t = t_ref[0, h]
            st = sin_ref[0, h]
            vb = vh * beta
            kbg = kb * gam
            u = _dot(t, vb, GP)
            w = _dot(t, kbg, GP)
            yield
            v_new = u - _dot(w, st, GP)
            q_dec = qs * gam
            k_dec = kn * kds

            o = o_ref[:, lo:lo + DH]
            zh = z_ref[:, lo:lo + DH]
            sz = _sig(zh)
            r = lax.rsqrt(jnp.mean(o * o, axis=-1, keepdims=True) + EPS)
            orr = o * r
            d_out = dob_ref[:, lo:lo + DH]
            dz = d_out * (orr * nw_ref[...]) * (sz * (1.0 + zh * (1.0 - sz)))
            don = d_out * (zh * sz)
            st_ref[4:5, 0:DH] += _colsum(don * orr)
            tt = don * nw_ref[...]
            d_o = r * (tt - orr * jnp.mean(tt * orr, axis=-1, keepdims=True))

            yield
            ds_out = dstate[h]
            dv_new = _dot_tn(f["qk"], d_o, GP) + _dot(k_dec, ds_out, GP)
            dqk = jnp.where(f["causal"], _dot_nt(d_o, v_new, GP), 0.0)
            dq_dec = _dot_nt(d_o, st, GP)
            dk_dec = _dot_nt(v_new, ds_out, GP)
            yield
            dstate[h] = _dot_tn(q_dec, d_o, GP) + cd * ds_out - _dot_tn(w, dv_new, GP)
            dcd = jnp.sum(_rowsum(st * ds_out), axis=0, keepdims=True)
            dw = -_dot_nt(dv_new, st, GP)
            dvb = _dot_tn(t, dv_new, GP)
            yield
            dt_m = _dot_nt(dv_new, vb, GP) + _dot_nt(dw, kbg, GP)
            dkbg = _dot_tn(t, dw, GP)
            yield
            dtt = _dot_nt(dt_m, t, GP)
            yield
            da = jnp.where(f["strict"], -_dot_tn(t, dtt, GP), 0.0)
            yield
            dad = da * decay
            dqkd = dqk * decay
            dkb = _dot(dad, kn, GP) + dkbg * gam
            dkn = _dot_tn(dad, kb, GP) + _dot_tn(dqkd, qs, GP) + dk_dec * kds + dkb * beta
            dqs = _dot(dqkd, kn, GP) + dq_dec * gam
            yield
            m = da * f["a"] + dqk * f["qk"]
            tk = _rowsum(dk_dec * k_dec)
            dgl = jnp.sum(tk, axis=0, keepdims=True) + dcd * cd
            dgc = (_rowsum(m) - _rowsum(jnp.transpose(m)) + _rowsum(dq_dec * q_dec) - tk + _rowsum(dkbg * kbg)
                   + jnp.where(row == CL - 1, dgl, 0.0))
            dbeta = _rowsum(dkb * kn) + _rowsum(dvb * vh)
            acc["dgcum"] = acc["dgcum"] + jnp.where(lane == NH + h, dgc, 0.0)
            acc["dbeta"] = acc["dbeta"] + jnp.where(lane == h, dbeta, 0.0)
            dvh = dvb * beta
            dqn = dqs * QSCALE
            dqh = f["rq"] * (dqn - qn * _rowsum(dqn * qn))
            dkh = f["rk"] * (dkn - kn * _rowsum(dkn * kn))
            dsilu = lambda c0: sc[:, c0:c0 + DH] * (1.0 + conv[:, c0:c0 + DH] * (1.0 - sc[:, c0:c0 + DH]))
            dcbuf[0:CL, lo:lo + DH] = dqh * dsilu(lo)
            dcbuf[0:CL, GW + lo:GW + lo + DH] = dkh * dsilu(GW + lo)
            dcbuf[0:CL, 2 * GW + lo:2 * GW + lo + DH] = dvh * dsilu(2 * GW + lo)
            dp_ref[:, 3 * GW + lo:3 * GW + lo + DH] = dz.astype(BF16)

        _lockstep(head(h) for h in range(NH))
        dgcum_all, dbeta_all = acc["dgcum"], acc["dbeta"]

        ii, jj = _tri_iota()
        upper = jnp.where(ii <= jj, 1.0, 0.0).astype(F32)
        dg_all = _dot(upper, dgcum_all, HI)
        dxg = dg_all * neg_a * _sig(xg)
        st_ref[5:6, 0:LANES] += _colsum(dg_all * g_all)
        st_ref[6:7, 0:LANES] += _colsum(dxg)
        dbl = dbeta_all * beta_all * (1.0 - beta_all)
        dba_ref[...] = jnp.where(lane < NH, dbl, jnp.where(lane < 2 * NH, dxg, 0.0)).astype(BF16)

        dconv = dcbuf[0:CL, :]
        dx = w_ref[0:1, :] * dcbuf[KS - 1:KS - 1 + CL, :]
        st_ref[0:1, :] += _colsum(dconv * xbuf[SH - KS + 1:SH - KS + 1 + CL, :])
        for k in range(1, KS):
            off = SH - (KS - 1) + k
            st_ref[k:k + 1, :] += _colsum(dconv * xbuf[off:off + CL, :])
            dx = dx + w_ref[k:k + 1, :] * dcbuf[KS - 1 - k:KS - 1 - k + CL, :]
        dcbuf[CL:CL + SH, :] = dcbuf[0:SH, :]
        dp_ref[:, 0:3 * GW] = dx.astype(BF16)

    rev = lambda w, j=0: pl.BlockSpec((CL, w), lambda n: (NCH - 1 - n, j))
    halo = lambda j: pl.BlockSpec((SH, GW), lambda n: (jnp.maximum((NCH - 1 - n) * (CL // SH) - 1, 0), j))
    blk4 = lambda a, b: pl.BlockSpec((1, NH, a, b), lambda n: (NCH - 1 - n, 0, 0, 0))
    return pl.pallas_call(
        body, name="gdn_bwd", grid=(NCH,),
        in_specs=[rev(GW), rev(GW), blk4(DH, DH), blk4(CL, CL), rev(GW, 2), rev(GW, 3), rev(GW, 4), rev(GW, 5),
                  halo(2), halo(3), halo(4), rev(LANES), _const((KS, 3 * GW)), _const((1, LANES)),
                  _const((1, LANES)), _const((1, DH))],
        out_specs=(rev(4 * GW), rev(LANES), _const((GDN_STATS, 3 * GW))),
        out_shape=(jax.ShapeDtypeStruct((S, 4 * GW), BF16), jax.ShapeDtypeStruct((S, LANES), BF16),
                   jax.ShapeDtypeStruct((GDN_STATS, 3 * GW), F32)),
        scratch_shapes=[pltpu.VMEM((SH + CL, 3 * GW), F32), pltpu.VMEM((CL + SH, 3 * GW), F32),
                        pltpu.VMEM((NH, DH, DH), F32)],
        compiler_params=_params(dimension_semantics=("arbitrary",)),
    )(d_out_b, o_pre, s_in, t_inv, p_main, p_main, p_main, p_main, p_main, p_main, p_main, p_ba,
      gdn_conv_w, alog_l, dt_l, gdn_nw)


def _bwd_in(dp_conf, dp_gdn, dp_ba, x, dx1, nw1, modnb, bada, w_main, w_ba):
    def body(dc_ref, dg_ref, db_ref, x_ref, dx1_ref, nw_ref, mod_ref, b_ref, wm_ref, wb_ref, gx_ref, st_ref):
        i = pl.program_id(0)

        @pl.when(i == 0)
        def _():
            st_ref[...] = jnp.zeros((8, D), F32)

        dh = (_dot(dc_ref[...], wm_ref[0:2 * CW, :]) + _dot(dg_ref[...], wm_ref[2 * CW:NMAIN, :])
              + _dot(db_ref[...], wb_ref[...]))
        xv = x_ref[...]
        r = lax.rsqrt(jnp.mean(xv * xv, axis=-1, keepdims=True) + EPS)
        xr = xv * r
        st_ref[0:1, :] += _colsum(dh)
        st_ref[1:2, :] += _colsum(dh * (xr * nw_ref[...]))
        dxn = dh * (1.0 + _mod(mod_ref, b_ref, 1))
        st_ref[2:3, :] += _colsum(dxn * xr)
        dxr = dxn * nw_ref[...]
        gx_ref[...] = dx1_ref[...] + r * (dxr - xr * jnp.mean(dxr * xr, axis=-1, keepdims=True))

    tile = lambda w: pl.BlockSpec((TM, w), lambda i: (i, 0))
    return pl.pallas_call(
        body, name="bwd_in", grid=(NT,),
        in_specs=[tile(2 * CW), tile(4 * GW), tile(LANES), tile(D), tile(D), _const((1, D)), _const((1, 6 * D)),
                  _const((1, 6 * D)), _const((NMAIN, D)), _const((LANES, D))],
        out_specs=(tile(D), _const((8, D))),
        out_shape=(jax.ShapeDtypeStruct((S, D), F32), jax.ShapeDtypeStruct((8, D), F32)),
        compiler_params=_params(dimension_semantics=("arbitrary",)),
    )(dp_conf, dp_gdn, dp_ba, x, dx1, nw1, modnb, bada, w_main, w_ba)


def _adamw(w, g, m, v):
    m = ADAM_B1 * m + (1.0 - ADAM_B1) * g
    v = ADAM_B2 * v + (1.0 - ADAM_B2) * (g * g)
    m_hat = m / BC1
    v_hat = v / BC2
    delta = -ADAM_LR * (m_hat / (jnp.sqrt(v_hat) + ADAM_EPS) + ADAM_WD * w)
    return delta, m, v


ADAM_BLOCK_BYTES = 6 * 1024 * 1024


def _adam_tile(rows, cols):
    padded = -(-cols // LANES) * LANES
    if N_DEV * rows * padded * 4 <= ADAM_BLOCK_BYTES:
        return rows, cols
    best = None
    for tr in range(16, rows, 16):
        if rows % tr == 0 and N_DEV * tr * padded * 4 <= ADAM_BLOCK_BYTES:
            best = tr
    if best is not None:
        return best, cols
    rows_padded = -(-rows // 16) * 16
    tc = LANES
    for cand in range(LANES, cols, LANES):
        if cols % cand == 0 and N_DEV * rows_padded * cand * 4 <= ADAM_BLOCK_BYTES:
            tc = cand
    return rows, tc


def _reduce_adam(name, parts, w, m, v, own=None):
    rows, cols = w.shape
    tr, tc = _adam_tile(rows, cols)

    def body(*refs):
        p_ref, w_ref, m_ref, v_ref = refs[:4]
        g_ref, d_ref, nm_ref, nv_ref = refs[-4:]
        if own is None:
            part = lambda j: p_ref[j].astype(F32)
        else:
            me = 4 * lax.axis_index("x") + 2 * lax.axis_index("y") + lax.axis_index("c")
            part = lambda j: jnp.where(me == j, refs[4][...], p_ref[j]).astype(F32)
        g = part(0)
        for j in range(1, N_DEV):
            g = g + part(j)
        g_ref[...] = g
        d_ref[...], nm_ref[...], nv_ref[...] = _adamw(w_ref[...], g, m_ref[...], v_ref[...])

    blk = pl.BlockSpec((tr, tc), lambda i, j: (i, j))
    sds = jax.ShapeDtypeStruct((rows, cols), F32)
    extra = [] if own is None else [own]
    return pl.pallas_call(
        body, name=name, grid=(rows // tr, cols // tc),
        in_specs=[pl.BlockSpec((N_DEV, tr, tc), lambda i, j: (0, i, j)), blk, blk, blk] + [blk] * len(extra),
        out_specs=(blk, blk, blk, blk), out_shape=(sds, sds, sds, sds),
        compiler_params=_params(dimension_semantics=("arbitrary", "arbitrary")),
    )(parts, w, m, v, *extra)


def _ada_adam(c_all, dmod_sh, w, m, v):
    rows, cols = w.shape
    tr = 256

    def body(c_ref, dm_ref, w_ref, m_ref, v_ref, g_ref, d_ref, nm_ref, nv_ref):
        cv = c_ref[...]
        g = _dot_tn(cv * _sig(cv), dm_ref[...], HI)
        g_ref[...] = g
        d_ref[...], nm_ref[...], nv_ref[...] = _adamw(w_ref[...], g, m_ref[...], v_ref[...])

    blk = pl.BlockSpec((tr, cols), lambda i: (i, 0))
    sds = jax.ShapeDtypeStruct((rows, cols), F32)
    return pl.pallas_call(
        body, name="ada_adam", grid=(rows // tr,),
        in_specs=[pl.BlockSpec((N_DEV, tr), lambda i: (0, i)), _const((N_DEV, cols)), blk, blk, blk],
        out_specs=(blk, blk, blk, blk), out_shape=(sds, sds, sds, sds),
        compiler_params=_params(dimension_semantics=("arbitrary",)),
    )(c_all, dmod_sh, w, m, v)


def _lanes(a, at=0):
    return jnp.pad(a, ((0, 0), (at, LANES - at - a.shape[1])))


WEIGHT_NAMES = ["w_ada", "b_ada", "norm_mix_w", "w_in", "conv_w", "conv_b", "conv_gn_w", "conv_gn_b", "gdn_conv_w",
                "gdn_a_log", "gdn_dt_bias", "gdn_norm_w", "w_out", "norm_ffn_w", "w_ffn_in", "w_ffn_out",
                "norm_final_w"]


def _slab(b_ada, norm_mix_w, norm_ffn_w, norm_final_w, conv_b, conv_gn_w, conv_gn_b, gdn_norm_w, a_log, dt_bias):
    return jnp.concatenate([
        b_ada.reshape(48, LANES), norm_mix_w.reshape(8, LANES), norm_ffn_w.reshape(8, LANES),
        norm_final_w.reshape(8, LANES), conv_b.reshape(4, LANES), conv_gn_w.reshape(4, LANES),
        conv_gn_b.reshape(4, LANES), gdn_norm_w.reshape(1, LANES), _lanes(a_log), _lanes(dt_bias),
        jnp.zeros((1, LANES), F32)], axis=0)


def _unslab(t):
    return dict(b_ada=t[0:48].reshape(1, 6 * D), norm_mix_w=t[48:56].reshape(1, D),
                norm_ffn_w=t[56:64].reshape(1, D), norm_final_w=t[64:72].reshape(D),
                conv_b=t[72:76].reshape(1, CW), conv_gn_w=t[76:80].reshape(1, CW),
                conv_gn_b=t[80:84].reshape(1, CW), gdn_norm_w=t[84:85], gdn_a_log=t[85:86, 0:NH],
                gdn_dt_bias=t[86:87, 0:NH])


def _mix_forward(w, xs, modnb):
    w_main = w["w_in"]
    w_ba = jnp.pad(w["w_in"][NMAIN:], ((0, LANES - 2 * NH), (0, 0)))
    alog_l = _lanes(w["gdn_a_log"], NH)
    dt_l = _lanes(w["gdn_dt_bias"], NH)
    p_main, p_ba, hb1 = _fwd_in(xs, w["norm_mix_w"], modnb, w["b_ada"], w_main, w_ba)
    y_conv, out_a = _conf_fwd(p_main, w["conv_w"], w["conv_b"], w["conv_gn_w"], w["conv_gn_b"])
    w_o, u_o, qg, kd, qk, cd, t_inv = _gdn_prep(p_main, p_ba, w["gdn_conv_w"], alog_l, dt_l)
    out_b, o_pre, s_in = _gdn_scan(w_o, u_o, qg, kd, qk, cd, p_main, w["gdn_norm_w"])
    return dict(w_main=w_main, w_ba=w_ba, alog_l=alog_l, dt_l=dt_l, p_main=p_main, p_ba=p_ba, hb1=hb1,
                y_conv=y_conv, out_a=out_a, out_b=out_b, o_pre=o_pre, s_in=s_in, t_inv=t_inv)


def _ffn_stage(w, f, xs, tgt, modnb):
    x1, mix, oab = _fwd_out(f["out_a"], f["out_b"], xs, modnb, w["b_ada"], w["w_out"])
    dx1, hb2, act, dffn, df, st_ffn = _ffn_fwd_bwd(x1, tgt, modnb, w["b_ada"], w["norm_ffn_w"], w["norm_final_w"],
                                                   w["w_ffn_in"], w["w_ffn_out"])
    gw_ffn_in = _grad_w_ffn_in(hb2, df)
    gw_ffn_out = _grad_w_ffn_out(act, dffn)
    return dict(mix=mix, oab=oab, dx1=dx1, st_ffn=st_ffn, gw_ffn_in=gw_ffn_in, gw_ffn_out=gw_ffn_out)


def _out_backward(w, g, modnb):
    dmix, d_out_a, d_out_b, st_out = _bwd_out(g["dx1"], g["mix"], modnb, w["b_ada"], w["w_out"])
    return dict(d_out_a=d_out_a, d_out_b=d_out_b, st_out=st_out, gw_out=_grad_w("grad_w_out", g["oab"], dmix, 512))


def _mix_backward(w, f, g, a, xs, modnb):
    d_out_a, d_out_b, st_out = a["d_out_a"], a["d_out_b"], a["st_out"]
    dp_conf, st_conf = _conf_bwd(d_out_a, f["y_conv"], f["p_main"], w["conv_w"], w["conv_gn_w"], w["conv_gn_b"])
    dp_gdn, dp_ba, st_gdn = _gdn_bwd(d_out_b, f["o_pre"], f["s_in"], f["t_inv"], f["p_main"], f["p_ba"],
                                     w["gdn_conv_w"], f["alog_l"], f["dt_l"], w["gdn_norm_w"])
    grad_x, st_in = _bwd_in(dp_conf, dp_gdn, dp_ba, xs, g["dx1"], w["norm_mix_w"], modnb, w["b_ada"], f["w_main"],
                            f["w_ba"])
    hb1 = f["hb1"]
    gw_in = jnp.concatenate(
        [_grad_w("grad_w_in_conf", dp_conf, hb1, 512), _grad_w("grad_w_in_gdn", dp_gdn, hb1, 512),
         _grad_w("grad_w_in_ba", dp_ba, hb1, LANES)[:2 * NH]], axis=0)
    st_ffn = g["st_ffn"]
    dmod = jnp.concatenate([st_in[0:1], st_in[1:2], st_out[0:1], st_ffn[2:3], st_ffn[3:4], st_ffn[1:2]], axis=1)
    small = jnp.concatenate([
        dmod.reshape(48, LANES), st_in[2:3].reshape(8, LANES), st_ffn[4:5].reshape(8, LANES),
        st_ffn[0:1].reshape(8, LANES), st_conf[31:32].reshape(4, LANES), st_conf[32:33].reshape(4, LANES),
        st_conf[33:34].reshape(4, LANES), st_gdn[4:5, 0:LANES],
        _lanes(st_gdn[5:6, NH:2 * NH]), _lanes(st_gdn[6:7, NH:2 * NH]), st_ffn[5:6, 0:LANES]], axis=0)
    return dict(grad_x=grad_x, gw_in=gw_in, gw_conv=st_conf[0:KC], gw_gconv=st_gdn[0:KS], small=small)


def _local(w, xs, tgt, modnb):
    f = _mix_forward(w, xs, modnb)
    g = _ffn_stage(w, f, xs, tgt, modnb)
    a = _out_backward(w, g, modnb)
    b = _mix_backward(w, f, g, a, xs, modnb)
    return dict(b, gw_out=a["gw_out"], gw_ffn_in=g["gw_ffn_in"], gw_ffn_out=g["gw_ffn_out"])


def kernel(x, c, w_ada, b_ada, norm_mix_w, w_in, conv_w, conv_b, conv_gn_w, conv_gn_b, gdn_conv_w, gdn_a_log, gdn_dt_bias, gdn_norm_w, w_out, norm_ffn_w, w_ffn_in, w_ffn_out, norm_final_w, loss_target, m_w_ada, m_b_ada, m_norm_mix_w, m_w_in, m_conv_w, m_conv_b, m_conv_gn_w, m_conv_gn_b, m_gdn_conv_w, m_gdn_a_log, m_gdn_dt_bias, m_gdn_norm_w, m_w_out, m_norm_ffn_w, m_w_ffn_in, m_w_ffn_out, m_norm_final_w, v_w_ada, v_b_ada, v_norm_mix_w, v_w_in, v_conv_w, v_conv_b, v_conv_gn_w, v_conv_gn_b, v_gdn_conv_w, v_gdn_a_log, v_gdn_dt_bias, v_gdn_norm_w, v_w_out, v_norm_ffn_w, v_w_ffn_in, v_w_ffn_out, v_norm_final_w):
    me = 4 * lax.axis_index("x") + 2 * lax.axis_index("y") + lax.axis_index("c")
    xs = x.reshape(S, D)
    tgt = loss_target.reshape(S, D)

    g_c, g_cw, g_gcw = _exchange("gather_cond", [c, conv_w[0], gdn_conv_w[0]], [False] * 3)
    c_all = g_c.reshape(N_DEV, D)
    g_mod, mod_token = _exchange("gather_mod", [_mod_shard(c_all, w_ada[0])], [False], with_token=True)
    modnb = lax.dynamic_index_in_dim(g_mod, me, axis=1, keepdims=False).reshape(1, 6 * D)

    late = [w_out[0].astype(BF16), jnp.transpose(w_ffn_in[0]).astype(BF16), w_ffn_out[0].astype(BF16)]
    g_win, *late_lands = _gather_two_level(
        "gather_weights", [_after(jnp.transpose(w_in[0]), mod_token).astype(BF16)] + late, seed_only=(1, 2, 3))
    late_started = _exchange_start("gather_late_start", late, late_lands, [False] * 3, only=LEVEL_ONE)
    modnb = _after(modnb, late_started[-1])
    w = dict(b_ada=b_ada, norm_mix_w=norm_mix_w, conv_b=conv_b, conv_gn_w=conv_gn_w, conv_gn_b=conv_gn_b,
             gdn_a_log=gdn_a_log, gdn_dt_bias=gdn_dt_bias, gdn_norm_w=gdn_norm_w, norm_ffn_w=norm_ffn_w,
             norm_final_w=norm_final_w.reshape(1, D),
             conv_w=jnp.transpose(g_cw, (1, 0, 2)).reshape(KC, CW),
             gdn_conv_w=jnp.transpose(g_gcw, (1, 0, 2)).reshape(KS, 3 * GW),
             w_in=g_win.reshape(NIN, D))

    f = _mix_forward(w, xs, modnb)
    _, late_landed = _exchange_wait("gather_late_wait", late_started, [False] * 3, (f["out_a"], f["out_b"]),
                                    only=LEVEL_ONE)
    g_wout, g_wfi, g_wfo = _relay_to_sibling("gather_late_relay", late_landed)
    w.update(w_out=g_wout.reshape(D, D), w_ffn_in=g_wfi, w_ffn_out=g_wfo.reshape(4, FB, D))
    g = _ffn_stage(w, f, xs, tgt, modnb)

    ffn_grads = [g["gw_ffn_in"], g["gw_ffn_out"].reshape(N_DEV, DFF // N_DEV, D)]
    ffn_started = _exchange_start("scatter_ffn_start", ffn_grads,
                                  [lax.empty(a.shape, a.dtype) for a in ffn_grads], [True] * 2)
    a = _out_backward(w, g, _after(modnb, ffn_started[-1]))
    out_grads = [a["gw_out"].reshape(N_DEV, D // N_DEV, D)]
    out_started = _exchange_start("scatter_out_start", out_grads,
                                  [lax.empty(t.shape, t.dtype) for t in out_grads], [True])
    loc = _mix_backward(dict(w, conv_gn_w=_after(w["conv_gn_w"], out_started[-1])), f, g, a, xs, modnb)

    g_small, small_token = _exchange("gather_small", [loc["small"]], [False], with_token=True)

    in_grads = [loc["gw_in"].reshape(N_DEV, NIN // N_DEV, D),
                _after(jnp.transpose(loc["gw_conv"].reshape(KC, N_DEV, CW // N_DEV), (1, 0, 2)), small_token),
                jnp.transpose(loc["gw_gconv"].reshape(KS, N_DEV, 3 * GW // N_DEV), (1, 0, 2))]
    in_started = _exchange_start("scatter_in_start", in_grads,
                                 [lax.empty(t.shape, t.dtype) for t in in_grads], [True] * 3)
    g_small = _after(g_small, in_started[-1])
    sw = _slab(b_ada, norm_mix_w, norm_ffn_w, norm_final_w, conv_b, conv_gn_w, conv_gn_b, gdn_norm_w, gdn_a_log,
               gdn_dt_bias)
    sm = _slab(m_b_ada, m_norm_mix_w, m_norm_ffn_w, m_norm_final_w, m_conv_b, m_conv_gn_w, m_conv_gn_b,
               m_gdn_norm_w, m_gdn_a_log, m_gdn_dt_bias)
    sv = _slab(v_b_ada, v_norm_mix_w, v_norm_ffn_w, v_norm_final_w, v_conv_b, v_conv_gn_w, v_conv_gn_b,
               v_gdn_norm_w, v_gdn_a_log, v_gdn_dt_bias)
    small_out = _reduce_adam("adam_small", g_small, sw, sm, sv)
    loss = small_out[0][SMALL_ROWS - 1, 0]
    res = [_unslab(t) for t in small_out]

    dmod_rows = g_small[:, 0:48, :].reshape(N_DEV, 6 * D)
    dmod_sh = lax.dynamic_slice_in_dim(dmod_rows, me * (6 * D // N_DEV), 6 * D // N_DEV, axis=1)

    def own(sent):
        return lax.dynamic_index_in_dim(sent, me, axis=0, keepdims=False)

    big = dict(w_ada=_ada_adam(c_all, dmod_sh, w_ada[0], m_w_ada[0], v_w_ada[0]))
    (sent_fi, sent_fo), (r_fi, r_fo) = _exchange_wait("scatter_ffn_wait", ffn_started, [True] * 2,
                                                         (big["w_ada"][0],))
    big["w_ffn_in"] = [jnp.transpose(t) for t in _reduce_adam(
        "adam_w_ffn_in", r_fi, jnp.transpose(w_ffn_in[0]), jnp.transpose(m_w_ffn_in[0]),
        jnp.transpose(v_w_ffn_in[0]), own(sent_fi))]
    big["w_ffn_out"] = _reduce_adam("adam_w_ffn_out", r_fo, w_ffn_out[0], m_w_ffn_out[0], v_w_ffn_out[0],
                                    own(sent_fo))
    (sent_out,), (r_out,) = _exchange_wait("scatter_out_wait", out_started, [True], (big["w_ffn_out"][0],))
    big["w_out"] = _reduce_adam("adam_w_out", r_out, w_out[0], m_w_out[0], v_w_out[0], own(sent_out))
    (sent_in, sent_cw, sent_gcw), (r_in, r_cw, r_gcw) = _exchange_wait(
        "scatter_in_wait", in_started, [True] * 3, (big["w_out"][0],))
    big["w_in"] = [jnp.transpose(t) for t in _reduce_adam(
        "adam_w_in", r_in, jnp.transpose(w_in[0]), jnp.transpose(m_w_in[0]), jnp.transpose(v_w_in[0]),
        own(sent_in))]
    big["conv_w"] = _reduce_adam("adam_conv_w", r_cw, conv_w[0], m_conv_w[0], v_conv_w[0], own(sent_cw))
    big["gdn_conv_w"] = _reduce_adam("adam_gdn_conv_w", r_gcw, gdn_conv_w[0], m_gdn_conv_w[0], v_gdn_conv_w[0],
                                     own(sent_gcw))
    outs = [loss, loc["grad_x"].reshape(1, S, D)]
    for kind in range(4):
        for nm in WEIGHT_NAMES:
            outs.append(big[nm][kind][None] if nm in big else res[kind][nm])
    return tuple(outs)
```

```python
import functools

import jax
import jax.numpy as jnp
from jax import lax
from jax.experimental import pallas as pl
from jax.experimental.pallas import tpu as pltpu

F32 = jnp.float32
BF16 = jnp.bfloat16
HI = lax.Precision.HIGHEST
MESH = pl.DeviceIdType.MESH

N_DEV = 8
S = 2048
D = 1024
TM = 256
NT = S // TM
CW = 512
KC = 31
NG = 8
GSZ = CW // NG
HALO = 32
GW = 512
NH = 4
DH = 128
KS = 4
SH = 8
CL = 64
NCH = S // CL
NMAIN = 2 * CW + 4 * GW
NIN = NMAIN + 2 * NH
DFF = 2816
FB = DFF // 4
EPS = 1e-6
QSCALE = DH ** -0.5
LANES = 128
SMALL_ROWS = 88

ADAM_LR = 0.001
ADAM_B1 = 0.9
ADAM_B2 = 0.999
ADAM_EPS = 1e-08
ADAM_WD = 0.01
ADAM_STEP = 10
BC1 = 1.0 - ADAM_B1 ** ADAM_STEP
BC2 = 1.0 - ADAM_B2 ** ADAM_STEP

MIB = 1024 * 1024
VMEM_LIMIT_MIB = 32


def _params(limit_mib=VMEM_LIMIT_MIB, **kw):
    return pltpu.CompilerParams(vmem_limit_bytes=limit_mib * MIB, **kw)


def _sig(x):
    return jax.nn.sigmoid(x)


GP = BF16


def _operands(a, b, prec):
    if prec is BF16:
        return a.astype(BF16), b.astype(BF16), None
    return a, b, prec


def _dot(a, b, prec=None):
    a, b, prec = _operands(a, b, prec)
    return jnp.dot(a, b, preferred_element_type=F32, precision=prec)


def _dot_nt(a, b, prec=None):
    a, b, prec = _operands(a, b, prec)
    return lax.dot_general(a, b, (((1,), (1,)), ((), ())), preferred_element_type=F32, precision=prec)


def _dot_tn(a, b, prec=None):
    a, b, prec = _operands(a, b, prec)
    return lax.dot_general(a, b, (((0,), (0,)), ((), ())), preferred_element_type=F32, precision=prec)


def _lockstep(gens):
    gens = list(gens)
    while gens:
        alive = []
        for g in gens:
            try:
                next(g)
                alive.append(g)
            except StopIteration:
                pass
        gens = alive


def _rowsum(x):
    return jnp.sum(x, axis=-1, keepdims=True)


def _colsum(x):
    return jnp.sum(x, axis=0, keepdims=True)


def _mod(mod_ref, b_ref, k):
    return mod_ref[:, k * D:(k + 1) * D] + b_ref[:, k * D:(k + 1) * D]


def _const(shape):
    nd = len(shape)
    return pl.BlockSpec(shape, lambda *_: (0,) * nd)


def _const1(shape):
    nd = len(shape)
    return pl.BlockSpec(shape, lambda *_: (0,) * nd, pipeline_mode=pl.Buffered(1))


PEER_FLIPS = [(dx, dy, dc) for dx in (0, 1) for dy in (0, 1) for dc in (0, 1)][1:]


def _after(x, token):
    return x + token[0:1, 0:1].astype(x.dtype).reshape((1,) * x.ndim)


def _exchange(name, srcs, per_dest, seed_only=(), with_token=False):
    n = len(srcs)
    out_shape = []
    for a, pd in zip(srcs, per_dest):
        blk = a.shape[1:] if pd else a.shape
        out_shape.append(jax.ShapeDtypeStruct((N_DEV,) + tuple(blk), a.dtype))

    def body(*refs):
        src = refs[:n]
        dst = refs[n:2 * n]
        send_sems, recv_sems, local_sems = refs[-3:]
        if with_token:
            refs[2 * n][...] = jnp.zeros((8, LANES), F32)
        x, y, c = lax.axis_index("x"), lax.axis_index("y"), lax.axis_index("c")
        me = 4 * x + 2 * y + c

        def piece(i, j):
            return src[i].at[j] if per_dest[i] else src[i]

        copies = []
        for k, (dx, dy, dc) in enumerate(PEER_FLIPS):
            px = 1 - x if dx else x
            py = 1 - y if dy else y
            pc = 1 - c if dc else c
            pj = 4 * px + 2 * py + pc
            for i in range(n):
                if i in seed_only:
                    continue
                cp = pltpu.make_async_remote_copy(
                    src_ref=piece(i, pj), dst_ref=dst[i].at[me],
                    send_sem=send_sems.at[k * n + i], recv_sem=recv_sems.at[k * n + i],
                    device_id=(px, py, pc), device_id_type=MESH)
                cp.start()
                arrive = pltpu.make_async_remote_copy(
                    src_ref=piece(i, pj), dst_ref=dst[i].at[pj],
                    send_sem=send_sems.at[k * n + i], recv_sem=recv_sems.at[k * n + i],
                    device_id=(px, py, pc), device_id_type=MESH)
                copies.append((cp, arrive))
        own = []
        for i in range(n):
            lc = pltpu.make_async_copy(piece(i, me), dst[i].at[me], local_sems.at[i])
            lc.start()
            own.append(lc)
        for cp, arrive in copies:
            arrive.wait_recv()
        for cp, arrive in copies:
            cp.wait_send()
        for lc in own:
            lc.wait()

    any_spec = pl.BlockSpec(memory_space=pl.ANY)
    out_specs = [any_spec] * n
    if with_token:
        out_shape.append(jax.ShapeDtypeStruct((8, LANES), F32))
        out_specs.append(pl.BlockSpec(memory_space=pltpu.VMEM))
    return pl.pallas_call(
        body, name=name, out_shape=tuple(out_shape),
        in_specs=[any_spec] * n, out_specs=tuple(out_specs),
        scratch_shapes=[pltpu.SemaphoreType.DMA((7 * n,)), pltpu.SemaphoreType.DMA((7 * n,)),
                        pltpu.SemaphoreType.DMA((n,))],
        compiler_params=pltpu.CompilerParams(has_side_effects=True),
    )(*srcs)


CHIP_FLIPS = [(0, 1), (1, 0), (1, 1)]
LEVEL_ONE = [k for k, (dx, dy, dc) in enumerate(PEER_FLIPS) if (dx, dy, dc) == (0, 0, 1) or dc == 0]


def _chip_peers(x, y):
    return [(1 - x if dx else x, 1 - y if dy else y) for dx, dy in CHIP_FLIPS]


def _gather_two_level(name, srcs, seed_only=()):
    n = len(srcs)
    live = [i for i in range(n) if i not in seed_only]

    def body(*refs):
        src, dst = refs[:n], refs[n:2 * n]
        send_sems, recv_sems, local_sems = refs[2 * n:2 * n + 3]
        bounce = refs[2 * n + 3:]
        x, y, c = lax.axis_index("x"), lax.axis_index("y"), lax.axis_index("c")
        me = 4 * x + 2 * y + c
        sibling = (x, y, 1 - c)
        chips = _chip_peers(x, y)

        def copy(k, i, src_ref, slot, to):
            return pltpu.make_async_remote_copy(
                src_ref=src_ref, dst_ref=dst[i].at[slot], send_sem=send_sems.at[k * n + i],
                recv_sem=recv_sems.at[k * n + i], device_id=to, device_id_type=MESH)

        first = []
        for i in live:
            first.append(copy(0, i, src[i], me, sibling))
            first += [copy(1 + j, i, src[i], me, (px, py, c)) for j, (px, py) in enumerate(chips)]
        for cp in first:
            cp.start()
        up = [pltpu.make_async_copy(src[i], bounce[i], local_sems.at[i]) for i in range(n)]
        for cp in up:
            cp.start()
        for cp in up:
            cp.wait()
        own = [pltpu.make_async_copy(bounce[i], dst[i].at[me], local_sems.at[i]) for i in range(n)]
        for cp in own:
            cp.start()
        passed = []
        for j, (px, py) in enumerate(chips):
            slot = 4 * px + 2 * py + c
            for i in live:
                copy(1 + j, i, src[i], slot, (px, py, c)).wait_recv()
                fwd = copy(4 + j, i, dst[i].at[slot], slot, sibling)
                fwd.start()
                passed.append(fwd)
        for i in live:
            copy(0, i, src[i], 4 * x + 2 * y + 1 - c, sibling).wait_recv()
            for j, (px, py) in enumerate(chips):
                copy(4 + j, i, src[i], 4 * px + 2 * py + 1 - c, sibling).wait_recv()
        for cp in first + passed:
            cp.wait_send()
        for cp in own:
            cp.wait()

    any_spec = pl.BlockSpec(memory_space=pl.ANY)
    return pl.pallas_call(
        body, name=name, out_shape=tuple(jax.ShapeDtypeStruct((N_DEV,) + a.shape, a.dtype) for a in srcs),
        in_specs=[any_spec] * n, out_specs=tuple([any_spec] * n),
        scratch_shapes=[pltpu.SemaphoreType.DMA((7 * n,)), pltpu.SemaphoreType.DMA((7 * n,)),
                        pltpu.SemaphoreType.DMA((n,))] + [pltpu.VMEM(a.shape, a.dtype) for a in srcs],
        compiler_params=pltpu.CompilerParams(has_side_effects=True),
    )(*srcs)


def _relay_to_sibling(name, lands):
    n = len(lands)

    def body(*refs):
        land = refs[n:2 * n]
        send_sems, recv_sems = refs[-2:]
        x, y, c = lax.axis_index("x"), lax.axis_index("y"), lax.axis_index("c")
        sibling = (x, y, 1 - c)
        sends = []
        for j, (px, py) in enumerate(_chip_peers(x, y)):
            slot = 4 * px + 2 * py + c
            for i in range(n):
                cp = pltpu.make_async_remote_copy(
                    src_ref=land[i].at[slot], dst_ref=land[i].at[slot], send_sem=send_sems.at[j * n + i],
                    recv_sem=recv_sems.at[j * n + i], device_id=sibling, device_id_type=MESH)
                cp.start()
                sends.append(cp)
        for j, (px, py) in enumerate(_chip_peers(x, y)):
            slot = 4 * px + 2 * py + 1 - c
            for i in range(n):
                pltpu.make_async_remote_copy(
                    src_ref=land[i].at[slot], dst_ref=land[i].at[slot], send_sem=send_sems.at[j * n + i],
                    recv_sem=recv_sems.at[j * n + i], device_id=sibling, device_id_type=MESH).wait_recv()
        for cp in sends:
            cp.wait_send()

    any_spec = pl.BlockSpec(memory_space=pl.ANY)
    return pl.pallas_call(
        body, name=name, out_shape=tuple(jax.ShapeDtypeStruct(a.shape, a.dtype) for a in lands),
        in_specs=[any_spec] * n, out_specs=tuple([any_spec] * n),
        input_output_aliases={i: i for i in range(n)},
        scratch_shapes=[pltpu.SemaphoreType.DMA((3 * n,)), pltpu.SemaphoreType.DMA((3 * n,))],
        compiler_params=pltpu.CompilerParams(has_side_effects=True),
    )(*lands)


HBM_SPEC = pl.BlockSpec(memory_space=pltpu.HBM)
SEM_SPEC = pl.BlockSpec(memory_space=pltpu.SEMAPHORE)
DATAFLOW = pltpu.SideEffectType.DATAFLOW_SIDE_EFFECTING


def _peers(only=None):
    x, y, c = lax.axis_index("x"), lax.axis_index("y"), lax.axis_index("c")
    out = []
    for k, (dx, dy, dc) in enumerate(PEER_FLIPS):
        if only is not None and k not in only:
            continue
        px = 1 - x if dx else x
        py = 1 - y if dy else y
        pc = 1 - c if dc else c
        out.append((k, (px, py, pc), 4 * px + 2 * py + pc))
    return 4 * x + 2 * y + c, out


def _exchange_start(name, srcs, lands, per_dest, only=None):
    n = len(srcs)

    def body(*refs):
        src, land = refs[:n], refs[n:2 * n]
        send_sems, recv_sems = refs[2 * n], refs[2 * n + 1]
        token = refs[-1]
        me, peers = _peers(only)
        for k, peer, pj in peers:
            for i in range(n):
                pltpu.make_async_remote_copy(
                    src_ref=src[i].at[pj] if per_dest[i] else src[i], dst_ref=land[i].at[me],
                    send_sem=send_sems.at[k * n + i], recv_sem=recv_sems.at[k * n + i],
                    device_id=peer, device_id_type=MESH).start()
        token[...] = jnp.zeros((8, LANES), F32)

    arrays = list(srcs) + list(lands)
    return pl.pallas_call(
        body, name=name,
        out_shape=(pltpu.SemaphoreType.DMA((7 * n,)), pltpu.SemaphoreType.DMA((7 * n,)),
                   *[pltpu.HBM(a.shape, a.dtype) for a in arrays], jax.ShapeDtypeStruct((8, LANES), F32)),
        in_specs=[HBM_SPEC] * (2 * n),
        out_specs=(SEM_SPEC, SEM_SPEC, *[HBM_SPEC] * (2 * n), pl.BlockSpec(memory_space=pltpu.VMEM)),
        input_output_aliases={i: 2 + i for i in range(2 * n)},
        compiler_params=pltpu.CompilerParams(has_side_effects=DATAFLOW),
    )(*[pltpu.with_memory_space_constraint(a, pltpu.HBM) for a in arrays])


def _exchange_wait(name, started, per_dest, after, only=None):
    n = (len(started) - 3) // 2
    send_sems, recv_sems = started[0], started[1]
    arrays = list(started[2:2 + 2 * n])

    def body(*refs):
        src, land = refs[:n], refs[n:2 * n]
        send, recv = refs[2 * n], refs[2 * n + 1]
        me, peers = _peers(only)
        for k, peer, pj in peers:
            for i in range(n):
                cp = pltpu.make_async_remote_copy(
                    src_ref=src[i].at[pj] if per_dest[i] else src[i], dst_ref=land[i].at[pj],
                    send_sem=send.at[k * n + i], recv_sem=recv.at[k * n + i],
                    device_id=peer, device_id_type=MESH)
                cp.wait_send()
                cp.wait_recv()

    outs = pl.pallas_call(
        body, name=name,
        out_shape=tuple(pltpu.HBM(a.shape, a.dtype) for a in arrays),
        in_specs=[HBM_SPEC] * (2 * n) + [SEM_SPEC, SEM_SPEC] + [pl.BlockSpec(memory_space=pl.ANY)] * len(after),
        out_specs=tuple([HBM_SPEC] * (2 * n)),
        input_output_aliases={i: i for i in range(2 * n)},
        compiler_params=pltpu.CompilerParams(has_side_effects=DATAFLOW),
    )(*arrays, send_sems, recv_sems, *after)
    return outs[:n], outs[n:]


def _mod_shard(c_all, w_ada):
    def body(c_ref, w_ref, o_ref):
        cv = c_ref[...]
        ca = cv * _sig(cv)
        o_ref[...] = _dot(ca.astype(BF16), w_ref[...].astype(BF16))

    return pl.pallas_call(
        body, name="mod_shard", out_shape=jax.ShapeDtypeStruct((N_DEV, w_ada.shape[1]), F32),
        compiler_params=_params(),
    )(c_all, w_ada)


def _fwd_in(x, nw1, modnb, bada, w_main, w_ba):
    def body(x_ref, nw_ref, mod_ref, b_ref, wm_ref, wb_ref, pm_ref, pb_ref, hb_ref):
        xv = x_ref[...]
        r = lax.rsqrt(jnp.mean(xv * xv, axis=-1, keepdims=True) + EPS)
        h = (xv * r * nw_ref[...]) * (1.0 + _mod(mod_ref, b_ref, 1)) + _mod(mod_ref, b_ref, 0)
        hb = h.astype(BF16)
        hb_ref[...] = hb
        pm_ref[...] = _dot_nt(hb, wm_ref[...])
        pb_ref[...] = _dot_nt(hb, wb_ref[...])

    return pl.pallas_call(
        body, name="fwd_in", grid=(NT,),
        in_specs=[pl.BlockSpec((TM, D), lambda i: (i, 0)), _const((1, D)), _const((1, 6 * D)), _const((1, 6 * D)),
                  _const((NMAIN, D)), _const((LANES, D))],
        out_specs=(pl.BlockSpec((TM, NMAIN), lambda i: (i, 0)), pl.BlockSpec((TM, LANES), lambda i: (i, 0)),
                   pl.BlockSpec((TM, D), lambda i: (i, 0))),
        out_shape=(jax.ShapeDtypeStruct((S, NMAIN), F32), jax.ShapeDtypeStruct((S, LANES), F32),
                   jax.ShapeDtypeStruct((S, D), BF16)),
        compiler_params=_params(dimension_semantics=("arbitrary",)),
    )(x, nw1, modnb, bada, w_main, w_ba)


def _group_mean_matrix():
    ii = lax.broadcasted_iota(jnp.int32, (CW, CW), 0) // GSZ
    jj = lax.broadcasted_iota(jnp.int32, (CW, CW), 1) // GSZ
    return jnp.where(ii == jj, 1.0 / GSZ, 0.0).astype(F32)


SUB = 8
SHIFT_ROWS = HALO + TM - SUB


def _fill_shifted(buf, sh):
    for b in range(1, SUB):
        sh[b - 1] = buf[b:b + SHIFT_ROWS, :]


def _rows_at(buf, sh, off):
    a, b = divmod(off, SUB)
    if b == 0:
        return buf[off:off + TM, :]
    return sh[b - 1, SUB * a:SUB * a + TM, :]


def _group_mean(x, pm):
    hi = x.astype(BF16)
    r1 = x - hi.astype(F32)
    mid = r1.astype(BF16)
    lo = (r1 - mid.astype(F32)).astype(BF16)
    return _dot(hi, pm) + _dot(mid, pm) + _dot(lo, pm)


def _conf_fwd(p_main, conv_w, conv_b, gn_w, gn_b):
    def body(a_ref, g_ref, w_ref, b_ref, gw_ref, gb_ref, y_ref, oa_ref, ubuf, ush):
        i = pl.program_id(0)

        @pl.when(i == 0)
        def _():
            ubuf[0:HALO, :] = jnp.zeros((HALO, CW), F32)

        ubuf[HALO:HALO + TM, :] = a_ref[...] * _sig(g_ref[...])
        _fill_shifted(ubuf, ush)
        acc = jnp.zeros((TM, CW), F32) + b_ref[...]
        for k in range(KC):
            acc = acc + w_ref[k:k + 1, :] * _rows_at(ubuf, ush, HALO - (KC - 1) + k)
        y_ref[...] = acc
        ubuf[0:HALO, :] = ubuf[TM:TM + HALO, :]
        pm = _group_mean_matrix().astype(BF16)
        dlt = acc - _group_mean(acc, pm)
        var = _group_mean(dlt * dlt, pm)
        o = dlt * lax.rsqrt(var + EPS) * gw_ref[...] + gb_ref[...]
        oa_ref[...] = o * _sig(o)

    return pl.pallas_call(
        body, name="conf_fwd", grid=(NT,),
        in_specs=[pl.BlockSpec((TM, CW), lambda i: (i, 0)), pl.BlockSpec((TM, CW), lambda i: (i, 1)),
                  _const((KC, CW)), _const((1, CW)), _const((1, CW)), _const((1, CW))],
        out_specs=(pl.BlockSpec((TM, CW), lambda i: (i, 0)), pl.BlockSpec((TM, CW), lambda i: (i, 0))),
        out_shape=(jax.ShapeDtypeStruct((S, CW), F32), jax.ShapeDtypeStruct((S, CW), F32)),
        scratch_shapes=[pltpu.VMEM((HALO + TM, CW), F32), pltpu.VMEM((SUB - 1, SHIFT_ROWS, CW), F32)],
        compiler_params=_params(dimension_semantics=("arbitrary",)),
    )(p_main, p_main, conv_w, conv_b, gn_w, gn_b)


def _tri_iota():
    ii = lax.broadcasted_iota(jnp.int32, (CL, CL), 0)
    jj = lax.broadcasted_iota(jnp.int32, (CL, CL), 1)
    return ii, jj


def _gdn_gates(ba, alog_l, dt_l):
    beta_all = _sig(ba)
    xg = ba + dt_l
    sp = jnp.maximum(xg, 0.0) + jnp.log(1.0 + jnp.exp(-jnp.abs(xg)))
    neg_a = -jnp.exp(alog_l)
    return beta_all, neg_a * sp, xg, neg_a


def _gdn_cumsum(g_all):
    ii, jj = _tri_iota()
    low = jnp.where(ii >= jj, 1.0, 0.0).astype(F32)
    gcum = _dot(low, g_all, HI)
    return gcum, jnp.transpose(gcum)


def _unit_lower_inverses(mats):
    ii, jj = _tri_iota()
    eye = jnp.where(ii == jj, 1.0, 0.0).astype(F32)
    ts = [eye - a for a in mats]
    ps = [_dot(a, a, HI) for a in mats]
    for _ in range(4):
        ts = [t + _dot(t, p, HI) for t, p in zip(ts, ps)]
        ps = [_dot(p, p, HI) for p in ps]
    return [t + _dot(t, p, HI) for t, p in zip(ts, ps)]


def _head_terms(qh, kh, beta, gcol, grow):
    ii, jj = _tri_iota()
    causal = ii >= jj
    strict = ii > jj
    rq = lax.rsqrt(_rowsum(qh * qh) + EPS)
    rk = lax.rsqrt(_rowsum(kh * kh) + EPS)
    qn = qh * rq
    kn = kh * rk
    qs = qn * QSCALE
    decay = jnp.where(causal, jnp.exp(jnp.where(causal, gcol - grow, 0.0)), 0.0)
    gam = jnp.exp(gcol)
    gl = gcol[CL - 1:CL, :]
    kds = jnp.exp(gl - gcol)
    cd = jnp.exp(gl)
    kb = kn * beta
    a = jnp.where(strict, _dot_nt(kb, kn, GP) * decay, 0.0)
    qk = jnp.where(causal, _dot_nt(qs, kn, GP) * decay, 0.0)
    return dict(rq=rq, rk=rk, qn=qn, kn=kn, qs=qs, decay=decay, gam=gam, kds=kds, cd=cd, kb=kb, a=a, qk=qk,
                causal=causal, strict=strict)


def _short_conv(w_ref, buf, rows=CL):
    acc = w_ref[0:1, :] * buf[SH - KS + 1:SH - KS + 1 + rows, :]
    for k in range(1, KS):
        off = SH - (KS - 1) + k
        acc = acc + w_ref[k:k + 1, :] * buf[off:off + rows, :]
    return acc


CPS = 4
TG = CPS * CL


def _gdn_prep(p_main, p_ba, gdn_conv_w, alog_l, dt_l):
    def body(q_ref, k_ref, v_ref, qh_ref, kh_ref, vh_ref, ba_ref, w_ref, al_ref, dt_ref,
             wo_ref, uo_ref, qg_ref, kd_ref, qk_ref, cd_ref, t_ref, xbuf):
        i = pl.program_id(0)
        first = i == 0
        xbuf[0:SH, 0:GW] = jnp.where(first, 0.0, qh_ref[...])
        xbuf[0:SH, GW:2 * GW] = jnp.where(first, 0.0, kh_ref[...])
        xbuf[0:SH, 2 * GW:3 * GW] = jnp.where(first, 0.0, vh_ref[...])
        xbuf[SH:SH + TG, 0:GW] = q_ref[...]
        xbuf[SH:SH + TG, GW:2 * GW] = k_ref[...]
        xbuf[SH:SH + TG, 2 * GW:3 * GW] = v_ref[...]
        conv = _short_conv(w_ref, xbuf, TG)
        qkv = conv * _sig(conv)
        beta_all, g_all, _, _ = _gdn_gates(ba_ref[...], al_ref[...], dt_ref[...])
        lane = lax.broadcasted_iota(jnp.int32, (8, LANES), 1)
        cums = [_gdn_cumsum(g_all[cc * CL:(cc + 1) * CL, :]) for cc in range(CPS)]
        pairs = [(cc, h) for cc in range(CPS) for h in range(NH)]
        terms, vbs = [], []
        for cc, h in pairs:
            r0, lo = cc * CL, h * DH
            beta = beta_all[r0:r0 + CL, h:h + 1]
            gcum, gcum_t = cums[cc]
            terms.append(_head_terms(qkv[r0:r0 + CL, lo:lo + DH], qkv[r0:r0 + CL, GW + lo:GW + lo + DH], beta,
                                     gcum[:, NH + h:NH + h + 1], gcum_t[NH + h:NH + h + 1, :]))
            vbs.append(qkv[r0:r0 + CL, 2 * GW + lo:2 * GW + lo + DH] * beta)
        invs = _unit_lower_inverses([f["a"] for f in terms])
        cds = [jnp.zeros((8, LANES), F32) for _ in range(CPS)]
        for (cc, h), f, t, vb in zip(pairs, terms, invs, vbs):
            r0, lo = cc * CL, h * DH
            t_ref[cc, h] = t
            uo_ref[r0:r0 + CL, lo:lo + DH] = _dot(t, vb, GP)
            wo_ref[r0:r0 + CL, lo:lo + DH] = _dot(t, f["kb"] * f["gam"], GP).astype(BF16)
            qg_ref[r0:r0 + CL, lo:lo + DH] = (f["qs"] * f["gam"]).astype(BF16)
            kd_ref[r0:r0 + CL, lo:lo + DH] = (f["kn"] * f["kds"]).astype(BF16)
            qk_ref[cc, h] = f["qk"].astype(BF16)
            cds[cc] = cds[cc] + jnp.where(lane == h, f["cd"], 0.0)
        for cc in range(CPS):
            cd_ref[cc] = cds[cc]

    col = lambda j: pl.BlockSpec((TG, GW), lambda i: (i, j))
    halo = lambda j: pl.BlockSpec((SH, GW), lambda i: (jnp.maximum(i * (TG // SH) - 1, 0), j))
    tile = lambda: pl.BlockSpec((TG, GW), lambda i: (i, 0))
    sq = lambda: pl.BlockSpec((CPS, NH, CL, CL), lambda i: (i, 0, 0, 0))
    return pl.pallas_call(
        body, name="gdn_prep", grid=(NCH // CPS,),
        in_specs=[col(2), col(3), col(4), halo(2), halo(3), halo(4), pl.BlockSpec((TG, LANES), lambda i: (i, 0)),
                  _const((KS, 3 * GW)), _const((1, LANES)), _const((1, LANES))],
        out_specs=(tile(), tile(), tile(), tile(), sq(), pl.BlockSpec((CPS, 8, LANES), lambda i: (i, 0, 0)), sq()),
        out_shape=(jax.ShapeDtypeStruct((S, GW), BF16), jax.ShapeDtypeStruct((S, GW), F32),
                   jax.ShapeDtypeStruct((S, GW), BF16), jax.ShapeDtypeStruct((S, GW), BF16),
                   jax.ShapeDtypeStruct((NCH, NH, CL, CL), BF16), jax.ShapeDtypeStruct((NCH, 8, LANES), F32),
                   jax.ShapeDtypeStruct((NCH, NH, CL, CL), F32)),
        scratch_shapes=[pltpu.VMEM((SH + TG, 3 * GW), F32)],
        compiler_params=_params(dimension_semantics=("arbitrary",)),
    )(p_main, p_main, p_main, p_main, p_main, p_main, p_ba, gdn_conv_w, alog_l, dt_l)


def _gdn_scan(w_o, u_o, qg, kd, qk, cd, p_main, gdn_nw):
    def body(w_ref, u_ref, qg_ref, kd_ref, qk_ref, cd_ref, z_ref, nw_ref, ob_ref, o_ref, sin_ref, state):
        n = pl.program_id(0)

        @pl.when(n == 0)
        def _():
            state[...] = jnp.zeros((NH, DH, DH), F32)

        def head(h):
            lo = h * DH
            st = state[h]
            sin_ref[0, h] = st
            sb = st.astype(BF16)
            v_new = u_ref[:, lo:lo + DH] - _dot(w_ref[:, lo:lo + DH], sb)
            yield
            vb = v_new.astype(BF16)
            o = _dot(qg_ref[:, lo:lo + DH], sb) + _dot(qk_ref[0, h], vb)
            state[h] = st * cd_ref[0, 0:1, h:h + 1] + _dot_tn(kd_ref[:, lo:lo + DH], vb)
            yield
            o_ref[:, lo:lo + DH] = o
            r = lax.rsqrt(jnp.mean(o * o, axis=-1, keepdims=True) + EPS)
            zh = z_ref[:, lo:lo + DH]
            ob_ref[:, lo:lo + DH] = o * r * nw_ref[...] * (zh * _sig(zh))

        _lockstep(head(h) for h in range(NH))

    tile = lambda: pl.BlockSpec((CL, GW), lambda n: (n, 0))
    return pl.pallas_call(
        body, name="gdn_scan", grid=(NCH,),
        in_specs=[tile(), tile(), tile(), tile(), pl.BlockSpec((1, NH, CL, CL), lambda n: (n, 0, 0, 0)),
                  pl.BlockSpec((1, 8, LANES), lambda n: (n, 0, 0)), pl.BlockSpec((CL, GW), lambda n: (n, 5)),
                  _const((1, DH))],
        out_specs=(tile(), tile(), pl.BlockSpec((1, NH, DH, DH), lambda n: (n, 0, 0, 0))),
        out_shape=(jax.ShapeDtypeStruct((S, GW), F32), jax.ShapeDtypeStruct((S, GW), F32),
                   jax.ShapeDtypeStruct((NCH, NH, DH, DH), F32)),
        scratch_shapes=[pltpu.VMEM((NH, DH, DH), F32)],
        compiler_params=_params(dimension_semantics=("arbitrary",)),
    )(w_o, u_o, qg, kd, qk, cd, p_main, gdn_nw)


def _fwd_out(out_a, out_b, x, modnb, bada, w_out):
    def body(oa_ref, ob_ref, x_ref, mod_ref, b_ref, w_ref, x1_ref, mix_ref, oab_ref):
        oa = oa_ref[...].astype(BF16)
        ob = ob_ref[...].astype(BF16)
        oab_ref[:, 0:CW] = oa
        oab_ref[:, CW:D] = ob
        mix = _dot(oa, w_ref[0:CW, :]) + _dot(ob, w_ref[CW:D, :])
        mix_ref[...] = mix
        x1_ref[...] = x_ref[...] + _mod(mod_ref, b_ref, 2) * mix

    tile = lambda w: pl.BlockSpec((TM, w), lambda i: (i, 0))
    return pl.pallas_call(
        body, name="fwd_out", grid=(NT,),
        in_specs=[tile(CW), tile(GW), tile(D), _const((1, 6 * D)), _const((1, 6 * D)), _const((D, D))],
        out_specs=(tile(D), tile(D), tile(D)),
        out_shape=(jax.ShapeDtypeStruct((S, D), F32), jax.ShapeDtypeStruct((S, D), F32),
                   jax.ShapeDtypeStruct((S, D), BF16)),
        compiler_params=_params(dimension_semantics=("arbitrary",)),
    )(out_a, out_b, x, modnb, bada, w_out)


FFN_STATS = 8


def _ffn_forward(x1, tgt, modnb, bada, nw2, nfw, w_fi, w_fo):
    def body(x1_ref, tgt_ref, mod_ref, b_ref, nw2_ref, nfw_ref, wi_ref, wo_ref,
             hb_ref, act_ref, pre_ref, dx2_ref, dffn_ref, st_ref):
        i = pl.program_id(0)

        @pl.when(i == 0)
        def _():
            st_ref[...] = jnp.zeros((FFN_STATS, D), F32)

        sh2, sc2, gt2 = _mod(mod_ref, b_ref, 3), _mod(mod_ref, b_ref, 4), _mod(mod_ref, b_ref, 5)
        x1v = x1_ref[...]
        r2 = lax.rsqrt(jnp.mean(x1v * x1v, axis=-1, keepdims=True) + EPS)
        hb = ((x1v * r2 * nw2_ref[...]) * (1.0 + sc2) + sh2).astype(BF16)
        hb_ref[...] = hb
        ffn = jnp.zeros((TM, D), F32)
        for j in range(4):
            fgj = _dot_nt(hb, wi_ref[j])
            fuj = _dot_nt(hb, wi_ref[j + 4])
            pre_ref[j] = fgj.astype(BF16)
            pre_ref[j + 4] = fuj.astype(BF16)
            aj = (fgj * _sig(fgj) * fuj).astype(BF16)
            act_ref[j] = aj
            ffn = ffn + _dot(aj, wo_ref[j])
        x2 = x1v + gt2 * ffn
        r3 = lax.rsqrt(jnp.mean(x2 * x2, axis=-1, keepdims=True) + EPS)
        xr3 = x2 * r3
        err = xr3 * nfw_ref[...] - tgt_ref[...]
        loss = 0.5 * jnp.sum(jnp.mean(err * err, axis=-1, keepdims=True), axis=0, keepdims=True)
        dy = err * (1.0 / D)
        st_ref[0:1, :] += _colsum(dy * xr3)
        dyr = dy * nfw_ref[...]
        dx2 = r3 * (dyr - xr3 * jnp.mean(dyr * xr3, axis=-1, keepdims=True))
        st_ref[1:2, :] += _colsum(dx2 * ffn)
        st_ref[5:6, :] += jnp.broadcast_to(loss, (1, D))
        dx2_ref[...] = dx2
        dffn_ref[...] = (gt2 * dx2).astype(BF16)

    tile = lambda w: pl.BlockSpec((TM, w), lambda i: (i, 0))
    return pl.pallas_call(
        body, name="ffn_forward", grid=(NT,),
        in_specs=[tile(D), tile(D), _const((1, 6 * D)), _const((1, 6 * D)), _const((1, D)), _const((1, D)),
                  _const1((N_DEV, FB, D)), _const1((4, FB, D))],
        out_specs=(tile(D), pl.BlockSpec((4, TM, FB), lambda i: (0, i, 0)),
                   pl.BlockSpec((N_DEV, TM, FB), lambda i: (0, i, 0)), tile(D), tile(D), _const((FFN_STATS, D))),
        out_shape=(jax.ShapeDtypeStruct((S, D), BF16), jax.ShapeDtypeStruct((4, S, FB), BF16),
                   jax.ShapeDtypeStruct((N_DEV, S, FB), BF16), jax.ShapeDtypeStruct((S, D), F32),
                   jax.ShapeDtypeStruct((S, D), BF16), jax.ShapeDtypeStruct((FFN_STATS, D), F32)),
        compiler_params=_params(42, dimension_semantics=("arbitrary",)),
    )(x1, tgt, modnb, bada, nw2, nfw, w_fi, w_fo)


def _ffn_backward(dffn, pre, x1, dx2, modnb, bada, nw2, w_fi, w_fo):
    def body(dffn_ref, pre_ref, x1_ref, dx2_ref, mod_ref, b_ref, nw2_ref, wi_ref, wo_ref, df_ref, dx1_ref, st_ref):
        i = pl.program_id(0)

        @pl.when(i == 0)
        def _():
            st_ref[...] = jnp.zeros((FFN_STATS, D), F32)

        dffn = dffn_ref[...]
        dh = jnp.zeros((TM, D), F32)
        for j in range(4):
            fg = pre_ref[j].astype(F32)
            fu = pre_ref[j + 4].astype(F32)
            sg = _sig(fg)
            dact = _dot_nt(dffn, wo_ref[j])
            dfg = (dact * fu * (sg * (1.0 + fg * (1.0 - sg)))).astype(BF16)
            dfu = (dact * (fg * sg)).astype(BF16)
            df_ref[j] = dfg
            df_ref[j + 4] = dfu
            dh = dh + _dot(dfg, wi_ref[j]) + _dot(dfu, wi_ref[j + 4])
        x1v = x1_ref[...]
        r2 = lax.rsqrt(jnp.mean(x1v * x1v, axis=-1, keepdims=True) + EPS)
        xr2 = x1v * r2
        st_ref[2:3, :] += _colsum(dh)
        st_ref[3:4, :] += _colsum(dh * (xr2 * nw2_ref[...]))
        dxn = dh * (1.0 + _mod(mod_ref, b_ref, 4))
        st_ref[4:5, :] += _colsum(dxn * xr2)
        dxr = dxn * nw2_ref[...]
        dx1_ref[...] = dx2_ref[...] + r2 * (dxr - xr2 * jnp.mean(dxr * xr2, axis=-1, keepdims=True))

    tile = lambda w: pl.BlockSpec((TM, w), lambda i: (i, 0))
    wide = lambda: pl.BlockSpec((N_DEV, TM, FB), lambda i: (0, i, 0))
    return pl.pallas_call(
        body, name="ffn_backward", grid=(NT,),
        in_specs=[tile(D), wide(), tile(D), tile(D), _const((1, 6 * D)), _const((1, 6 * D)), _const((1, D)),
                  _const1((N_DEV, FB, D)), _const1((4, FB, D))],
        out_specs=(wide(), tile(D), _const((FFN_STATS, D))),
        out_shape=(jax.ShapeDtypeStruct((N_DEV, S, FB), BF16), jax.ShapeDtypeStruct((S, D), F32),
                   jax.ShapeDtypeStruct((FFN_STATS, D), F32)),
        compiler_params=_params(44, dimension_semantics=("arbitrary",)),
    )(dffn, pre, x1, dx2, modnb, bada, nw2, w_fi, w_fo)


def _grad_w(name, a, b, nb):
    m, n = a.shape[1], b.shape[1]

    def body(a_ref, b_ref, o_ref):
        o_ref[...] = _dot_tn(a_ref[...], b_ref[...]).astype(BF16)

    return pl.pallas_call(
        body, name=name, grid=(m // nb,),
        in_specs=[pl.BlockSpec((S, nb), lambda j: (0, j)), _const((S, n))],
        out_specs=pl.BlockSpec((nb, n), lambda j: (j, 0)),
        out_shape=jax.ShapeDtypeStruct((m, n), BF16),
        compiler_params=_params(dimension_semantics=("arbitrary",)),
    )(a, b)


def _grad_w_ffn_in(hb2, df):
    def body(a_ref, b_ref, o_ref):
        o_ref[0] = _dot_tn(b_ref[0], a_ref[...]).astype(BF16)

    return pl.pallas_call(
        body, name="grad_w_ffn_in", grid=(N_DEV,),
        in_specs=[_const((S, D)), pl.BlockSpec((1, S, FB), lambda j: (j, 0, 0))],
        out_specs=pl.BlockSpec((1, FB, D), lambda j: (j, 0, 0)),
        out_shape=jax.ShapeDtypeStruct((N_DEV, FB, D), BF16),
        compiler_params=_params(dimension_semantics=("arbitrary",)),
    )(hb2, df)


def _grad_w_ffn_out(act, dffn):
    def body(a_ref, b_ref, o_ref):
        o_ref[0] = _dot_tn(a_ref[0], b_ref[...]).astype(BF16)

    return pl.pallas_call(
        body, name="grad_w_ffn_out", grid=(4,),
        in_specs=[pl.BlockSpec((1, S, FB), lambda j: (j, 0, 0)), _const((S, D))],
        out_specs=pl.BlockSpec((1, FB, D), lambda j: (j, 0, 0)),
        out_shape=jax.ShapeDtypeStruct((4, FB, D), BF16),
        compiler_params=_params(dimension_semantics=("arbitrary",)),
    )(act, dffn)


def _bwd_out(dx1, mix, modnb, bada, w_out):
    def body(dx_ref, mix_ref, mod_ref, b_ref, w_ref, dmix_ref, doa_ref, dob_ref, st_ref):
        i = pl.program_id(0)

        @pl.when(i == 0)
        def _():
            st_ref[...] = jnp.zeros((8, D), F32)

        dx = dx_ref[...]
        st_ref[0:1, :] += _colsum(dx * mix_ref[...])
        dmix = (_mod(mod_ref, b_ref, 2) * dx).astype(BF16)
        dmix_ref[...] = dmix
        doa_ref[...] = _dot_nt(dmix, w_ref[0:CW, :])
        dob_ref[...] = _dot_nt(dmix, w_ref[CW:D, :])

    tile = lambda w: pl.BlockSpec((TM, w), lambda i: (i, 0))
    return pl.pallas_call(
        body, name="bwd_out", grid=(NT,),
        in_specs=[tile(D), tile(D), _const((1, 6 * D)), _const((1, 6 * D)), _const((D, D))],
        out_specs=(tile(D), tile(CW), tile(GW), _const((8, D))),
        out_shape=(jax.ShapeDtypeStruct((S, D), BF16), jax.ShapeDtypeStruct((S, CW), F32),
                   jax.ShapeDtypeStruct((S, GW), F32), jax.ShapeDtypeStruct((8, D), F32)),
        compiler_params=_params(dimension_semantics=("arbitrary",)),
    )(dx1, mix, modnb, bada, w_out)


CONF_STATS = 40


def _conf_bwd(d_out_a, y, p_main, conv_w, gn_w, gn_b):
    def body(do_ref, y_ref, a_ref, g_ref, ah_ref, gh_ref, w_ref, gw_ref, gb_ref, dp_ref, st_ref,
             ubuf, dybuf, ush, dysh):
        i = pl.program_id(0)

        @pl.when(i == 0)
        def _():
            st_ref[...] = jnp.zeros((CONF_STATS, CW), F32)
            dybuf[TM:TM + HALO, :] = jnp.zeros((HALO, CW), F32)

        pm = _group_mean_matrix().astype(BF16)
        yv = y_ref[...]
        dlt = yv - _group_mean(yv, pm)
        rstd = lax.rsqrt(_group_mean(dlt * dlt, pm) + EPS)
        un = dlt * rstd
        o = un * gw_ref[...] + gb_ref[...]
        so = _sig(o)
        d_o = do_ref[...] * (so * (1.0 + o * (1.0 - so)))
        st_ref[33:34, :] += _colsum(d_o)
        st_ref[32:33, :] += _colsum(d_o * un)
        dun = d_o * gw_ref[...]
        dy = rstd * (dun - _group_mean(dun, pm) - un * _group_mean(dun * un, pm))
        st_ref[31:32, :] += _colsum(dy)
        dybuf[0:TM, :] = dy
        _fill_shifted(dybuf, dysh)

        a = a_ref[...]
        sg = _sig(g_ref[...])
        first = i == NT - 1
        ubuf[0:HALO, :] = jnp.where(first, 0.0, ah_ref[...] * _sig(gh_ref[...]))
        ubuf[HALO:HALO + TM, :] = a * sg
        _fill_shifted(ubuf, ush)
        du = jnp.zeros((TM, CW), F32)
        for k in range(KC):
            st_ref[k:k + 1, :] += _colsum(dy * _rows_at(ubuf, ush, HALO - (KC - 1) + k))
            du = du + w_ref[k:k + 1, :] * _rows_at(dybuf, dysh, KC - 1 - k)
        dybuf[TM:TM + HALO, :] = dybuf[0:HALO, :]
        dp_ref[:, 0:CW] = (du * sg).astype(BF16)
        dp_ref[:, CW:2 * CW] = (du * a * sg * (1.0 - sg)).astype(BF16)

    rev = lambda w, j=0: pl.BlockSpec((TM, w), lambda i: (NT - 1 - i, j))
    halo = lambda j: pl.BlockSpec((HALO, CW), lambda i: (jnp.maximum((NT - 1 - i) * (TM // HALO) - 1, 0), j))
    return pl.pallas_call(
        body, name="conf_bwd", grid=(NT,),
        in_specs=[rev(CW), rev(CW), rev(CW, 0), rev(CW, 1), halo(0), halo(1),
                  _const((KC, CW)), _const((1, CW)), _const((1, CW))],
        out_specs=(rev(2 * CW), _const((CONF_STATS, CW))),
        out_shape=(jax.ShapeDtypeStruct((S, 2 * CW), BF16), jax.ShapeDtypeStruct((CONF_STATS, CW), F32)),
        scratch_shapes=[pltpu.VMEM((HALO + TM, CW), F32), pltpu.VMEM((TM + HALO, CW), F32),
                        pltpu.VMEM((SUB - 1, SHIFT_ROWS, CW), F32), pltpu.VMEM((SUB - 1, SHIFT_ROWS, CW), F32)],
        compiler_params=_params(dimension_semantics=("arbitrary",)),
    )(d_out_a, y, p_main, p_main, p_main, p_main, conv_w, gn_w, gn_b)


GDN_STATS = 8


def _gdn_bwd(d_out_b, o_pre, s_in, t_inv, p_main, p_ba, gdn_conv_w, alog_l, dt_l, gdn_nw):
    def body(dob_ref, o_ref, sin_ref, t_ref, q_ref, k_ref, v_ref, z_ref, qh_ref, kh_ref, vh_ref, ba_ref,
             w_ref, al_ref, dt_ref, nw_ref, dp_ref, dba_ref, st_ref, xbuf, dcbuf, dstate):
        n = pl.program_id(0)

        @pl.when(n == 0)
        def _():
            st_ref[...] = jnp.zeros((GDN_STATS, 3 * GW), F32)
            dcbuf[CL:CL + SH, :] = jnp.zeros((SH, 3 * GW), F32)
            dstate[...] = jnp.zeros((NH, DH, DH), F32)

        first = n == NCH - 1
        xbuf[0:SH, 0:GW] = jnp.where(first, 0.0, qh_ref[...])
        xbuf[0:SH, GW:2 * GW] = jnp.where(first, 0.0, kh_ref[...])
        xbuf[0:SH, 2 * GW:3 * GW] = jnp.where(first, 0.0, vh_ref[...])
        xbuf[SH:SH + CL, 0:GW] = q_ref[...]
        xbuf[SH:SH + CL, GW:2 * GW] = k_ref[...]
        xbuf[SH:SH + CL, 2 * GW:3 * GW] = v_ref[...]
        conv = _short_conv(w_ref, xbuf)
        sc = _sig(conv)
        qkv = conv * sc
        ba = ba_ref[...]
        beta_all, g_all, xg, neg_a = _gdn_gates(ba, al_ref[...], dt_ref[...])
        gcum, gcum_t = _gdn_cumsum(g_all)
        lane = lax.broadcasted_iota(jnp.int32, (CL, LANES), 1)
        row = lax.broadcasted_iota(jnp.int32, (CL, 1), 0)
        acc = dict(dgcum=jnp.zeros((CL, LANES), F32), dbeta=jnp.zeros((CL, LANES), F32))

        def head(h):
            lo = h * DH
            qh = qkv[:, lo:lo + DH]
            kh = qkv[:, GW + lo:GW + lo + DH]
            vh = qkv[:, 2 * GW + lo:2 * GW + lo + DH]
            beta = beta_all[:, h:h + 1]
            f = _head_terms(qh, kh, beta, gcum[:, NH + h:NH + h + 1], gcum_t[NH + h:NH + h + 1, :])
            qn, kn, qs, kb, gam, kds, cd, decay = (f[s] for s in ("qn", "kn", "qs", "kb", "gam", "kds", "cd", "decay"))
            t = t_ref[0, h]
            st = sin_ref[0, h]
            vb = vh * beta
            kbg = kb * gam
            u = _dot(t, vb, GP)
            w = _dot(t, kbg, GP)
            yield
            v_new = u - _dot(w, st, GP)
            q_dec = qs * gam
            k_dec = kn * kds

            o = o_ref[:, lo:lo + DH]
            zh = z_ref[:, lo:lo + DH]
            sz = _sig(zh)
            r = lax.rsqrt(jnp.mean(o * o, axis=-1, keepdims=True) + EPS)
            orr = o * r
            d_out = dob_ref[:, lo:lo + DH]
            dz = d_out * (orr * nw_ref[...]) * (sz * (1.0 + zh * (1.0 - sz)))
            don = d_out * (zh * sz)
            st_ref[4:5, 0:DH] += _colsum(don * orr)
            tt = don * nw_ref[...]
            d_o = r * (tt - orr * jnp.mean(tt * orr, axis=-1, keepdims=True))

            yield
            ds_out = dstate[h]
            dv_new = _dot_tn(f["qk"], d_o, GP) + _dot(k_dec, ds_out, GP)
            dqk = jnp.where(f["causal"], _dot_nt(d_o, v_new, GP), 0.0)
            dq_dec = _dot_nt(d_o, st, GP)
            dk_dec = _dot_nt(v_new, ds_out, GP)
            yield
            dstate[h] = _dot_tn(q_dec, d_o, GP) + cd * ds_out - _dot_tn(w, dv_new, GP)
            dcd = jnp.sum(_rowsum(st * ds_out), axis=0, keepdims=True)
            dw = -_dot_nt(dv_new, st, GP)
            dvb = _dot_tn(t, dv_new, GP)
            yield
            dt_m = _dot_nt(dv_new, vb, GP) + _dot_nt(dw, kbg, GP)
            dkbg = _dot_tn(t, dw, GP)
            yield
            dtt = _dot_nt(dt_m, t, GP)
            yield
            da = jnp.where(f["strict"], -_dot_tn(t, dtt, GP), 0.0)
            yield
            dad = da * decay
            dqkd = dqk * decay
            dkb = _dot(dad, kn, GP) + dkbg * gam
            dkn = _dot_tn(dad, kb, GP) + _dot_tn(dqkd, qs, GP) + dk_dec * kds + dkb * beta
            dqs = _dot(dqkd, kn, GP) + dq_dec * gam
            yield
            m = da * f["a"] + dqk * f["qk"]
            tk = _rowsum(dk_dec * k_dec)
            dgl = jnp.sum(tk, axis=0, keepdims=True) + dcd * cd
            dgc = (_rowsum(m) - _rowsum(jnp.transpose(m)) + _rowsum(dq_dec * q_dec) - tk + _rowsum(dkbg * kbg)
                   + jnp.where(row == CL - 1, dgl, 0.0))
            dbeta = _rowsum(dkb * kn) + _rowsum(dvb * vh)
            acc["dgcum"] = acc["dgcum"] + jnp.where(lane == NH + h, dgc, 0.0)
            acc["dbeta"] = acc["dbeta"] + jnp.where(lane == h, dbeta, 0.0)
            dvh = dvb * beta
            dqn = dqs * QSCALE
            dqh = f["rq"] * (dqn - qn * _rowsum(dqn * qn))
            dkh = f["rk"] * (dkn - kn * _rowsum(dkn * kn))
            dsilu = lambda c0: sc[:, c0:c0 + DH] * (1.0 + conv[:, c0:c0 + DH] * (1.0 - sc[:, c0:c0 + DH]))
            dcbuf[0:CL, lo:lo + DH] = dqh * dsilu(lo)
            dcbuf[0:CL, GW + lo:GW + lo + DH] = dkh * dsilu(GW + lo)
            dcbuf[0:CL, 2 * GW + lo:2 * GW + lo + DH] = dvh * dsilu(2 * GW + lo)
            dp_ref[:, 3 * GW + lo:3 * GW + lo + DH] = dz.astype(BF16)

        _lockstep(head(h) for h in range(NH))
        dgcum_all, dbeta_all = acc["dgcum"], acc["dbeta"]

        ii, jj = _tri_iota()
        upper = jnp.where(ii <= jj, 1.0, 0.0).astype(F32)
        dg_all = _dot(upper, dgcum_all, HI)
        dxg = dg_all * neg_a * _sig(xg)
        st_ref[5:6, 0:LANES] += _colsum(dg_all * g_all)
        st_ref[6:7, 0:LANES] += _colsum(dxg)
        dbl = dbeta_all * beta_all * (1.0 - beta_all)
        dba_ref[...] = jnp.where(lane < NH, dbl, jnp.where(lane < 2 * NH, dxg, 0.0)).astype(BF16)

        dconv = dcbuf[0:CL, :]
        dx = w_ref[0:1, :] * dcbuf[KS - 1:KS - 1 + CL, :]
        st_ref[0:1, :] += _colsum(dconv * xbuf[SH - KS + 1:SH - KS + 1 + CL, :])
        for k in range(1, KS):
            off = SH - (KS - 1) + k
            st_ref[k:k + 1, :] += _colsum(dconv * xbuf[off:off + CL, :])
            dx = dx + w_ref[k:k + 1, :] * dcbuf[KS - 1 - k:KS - 1 - k + CL, :]
        dcbuf[CL:CL + SH, :] = dcbuf[0:SH, :]
        dp_ref[:, 0:3 * GW] = dx.astype(BF16)

    rev = lambda w, j=0: pl.BlockSpec((CL, w), lambda n: (NCH - 1 - n, j))
    halo = lambda j: pl.BlockSpec((SH, GW), lambda n: (jnp.maximum((NCH - 1 - n) * (CL // SH) - 1, 0), j))
    blk4 = lambda a, b: pl.BlockSpec((1, NH, a, b), lambda n: (NCH - 1 - n, 0, 0, 0))
    return pl.pallas_call(
        body, name="gdn_bwd", grid=(NCH,),
        in_specs=[rev(GW), rev(GW), blk4(DH, DH), blk4(CL, CL), rev(GW, 2), rev(GW, 3), rev(GW, 4), rev(GW, 5),
                  halo(2), halo(3), halo(4), rev(LANES), _const((KS, 3 * GW)), _const((1, LANES)),
                  _const((1, LANES)), _const((1, DH))],
        out_specs=(rev(4 * GW), rev(LANES), _const((GDN_STATS, 3 * GW))),
        out_shape=(jax.ShapeDtypeStruct((S, 4 * GW), BF16), jax.ShapeDtypeStruct((S, LANES), BF16),
                   jax.ShapeDtypeStruct((GDN_STATS, 3 * GW), F32)),
        scratch_shapes=[pltpu.VMEM((SH + CL, 3 * GW), F32), pltpu.VMEM((CL + SH, 3 * GW), F32),
                        pltpu.VMEM((NH, DH, DH), F32)],
        compiler_params=_params(dimension_semantics=("arbitrary",)),
    )(d_out_b, o_pre, s_in, t_inv, p_main, p_main, p_main, p_main, p_main, p_main, p_main, p_ba,
      gdn_conv_w, alog_l, dt_l, gdn_nw)


def _bwd_in(dp_conf, dp_gdn, dp_ba, x, dx1, nw1, modnb, bada, w_main, w_ba):
    def body(dc_ref, dg_ref, db_ref, x_ref, dx1_ref, nw_ref, mod_ref, b_ref, wm_ref, wb_ref, gx_ref, st_ref):
        i = pl.program_id(0)

        @pl.when(i == 0)
        def _():
            st_ref[...] = jnp.zeros((8, D), F32)

        dh = (_dot(dc_ref[...], wm_ref[0:2 * CW, :]) + _dot(dg_ref[...], wm_ref[2 * CW:NMAIN, :])
              + _dot(db_ref[...], wb_ref[...]))
        xv = x_ref[...]
        r = lax.rsqrt(jnp.mean(xv * xv, axis=-1, keepdims=True) + EPS)
        xr = xv * r
        st_ref[0:1, :] += _colsum(dh)
        st_ref[1:2, :] += _colsum(dh * (xr * nw_ref[...]))
        dxn = dh * (1.0 + _mod(mod_ref, b_ref, 1))
        st_ref[2:3, :] += _colsum(dxn * xr)
        dxr = dxn * nw_ref[...]
        gx_ref[...] = dx1_ref[...] + r * (dxr - xr * jnp.mean(dxr * xr, axis=-1, keepdims=True))

    tile = lambda w: pl.BlockSpec((TM, w), lambda i: (i, 0))
    return pl.pallas_call(
        body, name="bwd_in", grid=(NT,),
        in_specs=[tile(2 * CW), tile(4 * GW), tile(LANES), tile(D), tile(D), _const((1, D)), _const((1, 6 * D)),
                  _const((1, 6 * D)), _const((NMAIN, D)), _const((LANES, D))],
        out_specs=(tile(D), _const((8, D))),
        out_shape=(jax.ShapeDtypeStruct((S, D), F32), jax.ShapeDtypeStruct((8, D), F32)),
        compiler_params=_params(dimension_semantics=("arbitrary",)),
    )(dp_conf, dp_gdn, dp_ba, x, dx1, nw1, modnb, bada, w_main, w_ba)


def _adamw(w, g, m, v):
    m = ADAM_B1 * m + (1.0 - ADAM_B1) * g
    v = ADAM_B2 * v + (1.0 - ADAM_B2) * (g * g)
    m_hat = m / BC1
    v_hat = v / BC2
    delta = -ADAM_LR * (m_hat / (jnp.sqrt(v_hat) + ADAM_EPS) + ADAM_WD * w)
    return delta, m, v


ADAM_BLOCK_BYTES = 6 * 1024 * 1024


def _adam_tile(rows, cols):
    padded = -(-cols // LANES) * LANES
    if N_DEV * rows * padded * 4 <= ADAM_BLOCK_BYTES:
        return rows, cols
    best = None
    for tr in range(16, rows, 16):
        if rows % tr == 0 and N_DEV * tr * padded * 4 <= ADAM_BLOCK_BYTES:
            best = tr
    if best is not None:
        return best, cols
    rows_padded = -(-rows // 16) * 16
    tc = LANES
    for cand in range(LANES, cols, LANES):
        if cols % cand == 0 and N_DEV * rows_padded * cand * 4 <= ADAM_BLOCK_BYTES:
            tc = cand
    return rows, tc


def _reduce_adam(name, parts, w, m, v, own=None):
    rows, cols = w.shape
    tr, tc = _adam_tile(rows, cols)

    def body(*refs):
        p_ref, w_ref, m_ref, v_ref = refs[:4]
        g_ref, d_ref, nm_ref, nv_ref = refs[-4:]
        if own is None:
            part = lambda j: p_ref[j].astype(F32)
        else:
            me = 4 * lax.axis_index("x") + 2 * lax.axis_index("y") + lax.axis_index("c")
            part = lambda j: jnp.where(me == j, refs[4][...], p_ref[j]).astype(F32)
        g = part(0)
        for j in range(1, N_DEV):
            g = g + part(j)
        g_ref[...] = g
        d_ref[...], nm_ref[...], nv_ref[...] = _adamw(w_ref[...], g, m_ref[...], v_ref[...])

    blk = pl.BlockSpec((tr, tc), lambda i, j: (i, j))
    sds = jax.ShapeDtypeStruct((rows, cols), F32)
    extra = [] if own is None else [own]
    return pl.pallas_call(
        body, name=name, grid=(rows // tr, cols // tc),
        in_specs=[pl.BlockSpec((N_DEV, tr, tc), lambda i, j: (0, i, j)), blk, blk, blk] + [blk] * len(extra),
        out_specs=(blk, blk, blk, blk), out_shape=(sds, sds, sds, sds),
        compiler_params=_params(dimension_semantics=("arbitrary", "arbitrary")),
    )(parts, w, m, v, *extra)


def _ada_adam(c_all, dmod_sh, w, m, v):
    rows, cols = w.shape
    tr = 256

    def body(c_ref, dm_ref, w_ref, m_ref, v_ref, g_ref, d_ref, nm_ref, nv_ref):
        cv = c_ref[...]
        g = _dot_tn(cv * _sig(cv), dm_ref[...], HI)
        g_ref[...] = g
        d_ref[...], nm_ref[...], nv_ref[...] = _adamw(w_ref[...], g, m_ref[...], v_ref[...])

    blk = pl.BlockSpec((tr, cols), lambda i: (i, 0))
    sds = jax.ShapeDtypeStruct((rows, cols), F32)
    return pl.pallas_call(
        body, name="ada_adam", grid=(rows // tr,),
        in_specs=[pl.BlockSpec((N_DEV, tr), lambda i: (0, i)), _const((N_DEV, cols)), blk, blk, blk],
        out_specs=(blk, blk, blk, blk), out_shape=(sds, sds, sds, sds),
        compiler_params=_params(dimension_semantics=("arbitrary",)),
    )(c_all, dmod_sh, w, m, v)


def _lanes(a, at=0):
    return jnp.pad(a, ((0, 0), (at, LANES - at - a.shape[1])))


WEIGHT_NAMES = ["w_ada", "b_ada", "norm_mix_w", "w_in", "conv_w", "conv_b", "conv_gn_w", "conv_gn_b", "gdn_conv_w",
                "gdn_a_log", "gdn_dt_bias", "gdn_norm_w", "w_out", "norm_ffn_w", "w_ffn_in", "w_ffn_out",
                "norm_final_w"]


def _slab(b_ada, norm_mix_w, norm_ffn_w, norm_final_w, conv_b, conv_gn_w, conv_gn_b, gdn_norm_w, a_log, dt_bias):
    return jnp.concatenate([
        b_ada.reshape(48, LANES), norm_mix_w.reshape(8, LANES), norm_ffn_w.reshape(8, LANES),
        norm_final_w.reshape(8, LANES), conv_b.reshape(4, LANES), conv_gn_w.reshape(4, LANES),
        conv_gn_b.reshape(4, LANES), gdn_norm_w.reshape(1, LANES), _lanes(a_log), _lanes(dt_bias),
        jnp.zeros((1, LANES), F32)], axis=0)


def _unslab(t):
    return dict(b_ada=t[0:48].reshape(1, 6 * D), norm_mix_w=t[48:56].reshape(1, D),
                norm_ffn_w=t[56:64].reshape(1, D), norm_final_w=t[64:72].reshape(D),
                conv_b=t[72:76].reshape(1, CW), conv_gn_w=t[76:80].reshape(1, CW),
                conv_gn_b=t[80:84].reshape(1, CW), gdn_norm_w=t[84:85], gdn_a_log=t[85:86, 0:NH],
                gdn_dt_bias=t[86:87, 0:NH])


def _mix_forward(w, xs, modnb):
    w_main = w["w_in"]
    w_ba = jnp.pad(w["w_in"][NMAIN:], ((0, LANES - 2 * NH), (0, 0)))
    alog_l = _lanes(w["gdn_a_log"], NH)
    dt_l = _lanes(w["gdn_dt_bias"], NH)
    p_main, p_ba, hb1 = _fwd_in(xs, w["norm_mix_w"], modnb, w["b_ada"], w_main, w_ba)
    y_conv, out_a = _conf_fwd(p_main, w["conv_w"], w["conv_b"], w["conv_gn_w"], w["conv_gn_b"])
    w_o, u_o, qg, kd, qk, cd, t_inv = _gdn_prep(p_main, p_ba, w["gdn_conv_w"], alog_l, dt_l)
    out_b, o_pre, s_in = _gdn_scan(w_o, u_o, qg, kd, qk, cd, p_main, w["gdn_norm_w"])
    return dict(w_main=w_main, w_ba=w_ba, alog_l=alog_l, dt_l=dt_l, p_main=p_main, p_ba=p_ba, hb1=hb1,
                y_conv=y_conv, out_a=out_a, out_b=out_b, o_pre=o_pre, s_in=s_in, t_inv=t_inv)


def _ffn_stage(w, f, xs, tgt, modnb):
    x1, mix, oab = _fwd_out(f["out_a"], f["out_b"], xs, modnb, w["b_ada"], w["w_out"])
    hb2, act, pre, dx2, dffn, st_fwd = _ffn_forward(x1, tgt, modnb, w["b_ada"], w["norm_ffn_w"],
                                                    w["norm_final_w"], w["w_ffn_in"], w["w_ffn_out"])
    gw_ffn_out = _grad_w_ffn_out(act, dffn)
    df, dx1, st_bwd = _ffn_backward(dffn, pre, x1, dx2, modnb, w["b_ada"], w["norm_ffn_w"], w["w_ffn_in"],
                                    w["w_ffn_out"])
    gw_ffn_in = _grad_w_ffn_in(hb2, df)
    return dict(mix=mix, oab=oab, dx1=dx1, st_ffn=st_fwd + st_bwd, gw_ffn_in=gw_ffn_in, gw_ffn_out=gw_ffn_out)


def _out_backward(w, g, modnb):
    dmix, d_out_a, d_out_b, st_out = _bwd_out(g["dx1"], g["mix"], modnb, w["b_ada"], w["w_out"])
    return dict(d_out_a=d_out_a, d_out_b=d_out_b, st_out=st_out, gw_out=_grad_w("grad_w_out", g["oab"], dmix, 512))


def _mix_backward(w, f, g, a, xs, modnb):
    d_out_a, d_out_b, st_out = a["d_out_a"], a["d_out_b"], a["st_out"]
    dp_conf, st_conf = _conf_bwd(d_out_a, f["y_conv"], f["p_main"], w["conv_w"], w["conv_gn_w"], w["conv_gn_b"])
    dp_gdn, dp_ba, st_gdn = _gdn_bwd(d_out_b, f["o_pre"], f["s_in"], f["t_inv"], f["p_main"], f["p_ba"],
                                     w["gdn_conv_w"], f["alog_l"], f["dt_l"], w["gdn_norm_w"])
    grad_x, st_in = _bwd_in(dp_conf, dp_gdn, dp_ba, xs, g["dx1"], w["norm_mix_w"], modnb, w["b_ada"], f["w_main"],
                            f["w_ba"])
    hb1 = f["hb1"]
    gw_in = jnp.concatenate(
        [_grad_w("grad_w_in_conf", dp_conf, hb1, 512), _grad_w("grad_w_in_gdn", dp_gdn, hb1, 512),
         _grad_w("grad_w_in_ba", dp_ba, hb1, LANES)[:2 * NH]], axis=0)
    st_ffn = g["st_ffn"]
    dmod = jnp.concatenate([st_in[0:1], st_in[1:2], st_out[0:1], st_ffn[2:3], st_ffn[3:4], st_ffn[1:2]], axis=1)
    small = jnp.concatenate([
        dmod.reshape(48, LANES), st_in[2:3].reshape(8, LANES), st_ffn[4:5].reshape(8, LANES),
        st_ffn[0:1].reshape(8, LANES), st_conf[31:32].reshape(4, LANES), st_conf[32:33].reshape(4, LANES),
        st_conf[33:34].reshape(4, LANES), st_gdn[4:5, 0:LANES],
        _lanes(st_gdn[5:6, NH:2 * NH]), _lanes(st_gdn[6:7, NH:2 * NH]), st_ffn[5:6, 0:LANES]], axis=0)
    return dict(grad_x=grad_x, gw_in=gw_in, gw_conv=st_conf[0:KC], gw_gconv=st_gdn[0:KS], small=small)


def _local(w, xs, tgt, modnb):
    f = _mix_forward(w, xs, modnb)
    g = _ffn_stage(w, f, xs, tgt, modnb)
    a = _out_backward(w, g, modnb)
    b = _mix_backward(w, f, g, a, xs, modnb)
    return dict(b, gw_out=a["gw_out"], gw_ffn_in=g["gw_ffn_in"], gw_ffn_out=g["gw_ffn_out"])


def kernel(x, c, w_ada, b_ada, norm_mix_w, w_in, conv_w, conv_b, conv_gn_w, conv_gn_b, gdn_conv_w, gdn_a_log, gdn_dt_bias, gdn_norm_w, w_out, norm_ffn_w, w_ffn_in, w_ffn_out, norm_final_w, loss_target, m_w_ada, m_b_ada, m_norm_mix_w, m_w_in, m_conv_w, m_conv_b, m_conv_gn_w, m_conv_gn_b, m_gdn_conv_w, m_gdn_a_log, m_gdn_dt_bias, m_gdn_norm_w, m_w_out, m_norm_ffn_w, m_w_ffn_in, m_w_ffn_out, m_norm_final_w, v_w_ada, v_b_ada, v_norm_mix_w, v_w_in, v_conv_w, v_conv_b, v_conv_gn_w, v_conv_gn_b, v_gdn_conv_w, v_gdn_a_log, v_gdn_dt_bias, v_gdn_norm_w, v_w_out, v_norm_ffn_w, v_w_ffn_in, v_w_ffn_out, v_norm_final_w):
    me = 4 * lax.axis_index("x") + 2 * lax.axis_index("y") + lax.axis_index("c")
    xs = x.reshape(S, D)
    tgt = loss_target.reshape(S, D)

    g_c, g_cw, g_gcw = _exchange("gather_cond", [c, conv_w[0], gdn_conv_w[0]], [False] * 3)
    c_all = g_c.reshape(N_DEV, D)
    g_mod, mod_token = _exchange("gather_mod", [_mod_shard(c_all, w_ada[0])], [False], with_token=True)
    modnb = lax.dynamic_index_in_dim(g_mod, me, axis=1, keepdims=False).reshape(1, 6 * D)

    late = [w_out[0].astype(BF16), jnp.transpose(w_ffn_in[0]).astype(BF16), w_ffn_out[0].astype(BF16)]
    g_win, *late_lands = _gather_two_level(
        "gather_weights", [_after(jnp.transpose(w_in[0]), mod_token).astype(BF16)] + late, seed_only=(1, 2, 3))
    late_started = _exchange_start("gather_late_start", late, late_lands, [False] * 3, only=LEVEL_ONE)
    modnb = _after(modnb, late_started[-1])
    w = dict(b_ada=b_ada, norm_mix_w=norm_mix_w, conv_b=conv_b, conv_gn_w=conv_gn_w, conv_gn_b=conv_gn_b,
             gdn_a_log=gdn_a_log, gdn_dt_bias=gdn_dt_bias, gdn_norm_w=gdn_norm_w, norm_ffn_w=norm_ffn_w,
             norm_final_w=norm_final_w.reshape(1, D),
             conv_w=jnp.transpose(g_cw, (1, 0, 2)).reshape(KC, CW),
             gdn_conv_w=jnp.transpose(g_gcw, (1, 0, 2)).reshape(KS, 3 * GW),
             w_in=g_win.reshape(NIN, D))

    f = _mix_forward(w, xs, modnb)
    _, late_landed = _exchange_wait("gather_late_wait", late_started, [False] * 3, (f["out_a"], f["out_b"]),
                                    only=LEVEL_ONE)
    g_wout, g_wfi, g_wfo = _relay_to_sibling("gather_late_relay", late_landed)
    w.update(w_out=g_wout.reshape(D, D), w_ffn_in=g_wfi, w_ffn_out=g_wfo.reshape(4, FB, D))
    g = _ffn_stage(w, f, xs, tgt, modnb)

    ffn_grads = [g["gw_ffn_in"], g["gw_ffn_out"].reshape(N_DEV, DFF // N_DEV, D)]
    ffn_started = _exchange_start("scatter_ffn_start", ffn_grads,
                                  [lax.empty(a.shape, a.dtype) for a in ffn_grads], [True] * 2)
    a = _out_backward(w, g, _after(modnb, ffn_started[-1]))
    out_grads = [a["gw_out"].reshape(N_DEV, D // N_DEV, D)]
    out_started = _exchange_start("scatter_out_start", out_grads,
                                  [lax.empty(t.shape, t.dtype) for t in out_grads], [True])
    loc = _mix_backward(dict(w, conv_gn_w=_after(w["conv_gn_w"], out_started[-1])), f, g, a, xs, modnb)

    g_small, small_token = _exchange("gather_small", [loc["small"]], [False], with_token=True)

    in_grads = [loc["gw_in"].reshape(N_DEV, NIN // N_DEV, D),
                _after(jnp.transpose(loc["gw_conv"].reshape(KC, N_DEV, CW // N_DEV), (1, 0, 2)), small_token),
                jnp.transpose(loc["gw_gconv"].reshape(KS, N_DEV, 3 * GW // N_DEV), (1, 0, 2))]
    in_started = _exchange_start("scatter_in_start", in_grads,
                                 [lax.empty(t.shape, t.dtype) for t in in_grads], [True] * 3)
    g_small = _after(g_small, in_started[-1])
    sw = _slab(b_ada, norm_mix_w, norm_ffn_w, norm_final_w, conv_b, conv_gn_w, conv_gn_b, gdn_norm_w, gdn_a_log,
               gdn_dt_bias)
    sm = _slab(m_b_ada, m_norm_mix_w, m_norm_ffn_w, m_norm_final_w, m_conv_b, m_conv_gn_w, m_conv_gn_b,
               m_gdn_norm_w, m_gdn_a_log, m_gdn_dt_bias)
    sv = _slab(v_b_ada, v_norm_mix_w, v_norm_ffn_w, v_norm_final_w, v_conv_b, v_conv_gn_w, v_conv_gn_b,
               v_gdn_norm_w, v_gdn_a_log, v_gdn_dt_bias)
    small_out = _reduce_adam("adam_small", g_small, sw, sm, sv)
    loss = small_out[0][SMALL_ROWS - 1, 0]
    res = [_unslab(t) for t in small_out]

    dmod_rows = g_small[:, 0:48, :].reshape(N_DEV, 6 * D)
    dmod_sh = lax.dynamic_slice_in_dim(dmod_rows, me * (6 * D // N_DEV), 6 * D // N_DEV, axis=1)

    def own(sent):
        return lax.dynamic_index_in_dim(sent, me, axis=0, keepdims=False)

    big = dict(w_ada=_ada_adam(c_all, dmod_sh, w_ada[0], m_w_ada[0], v_w_ada[0]))
    (sent_fi, sent_fo), (r_fi, r_fo) = _exchange_wait("scatter_ffn_wait", ffn_started, [True] * 2,
                                                         (big["w_ada"][0],))
    big["w_ffn_in"] = [jnp.transpose(t) for t in _reduce_adam(
        "adam_w_ffn_in", r_fi, jnp.transpose(w_ffn_in[0]), jnp.transpose(m_w_ffn_in[0]),
        jnp.transpose(v_w_ffn_in[0]), own(sent_fi))]
    big["w_ffn_out"] = _reduce_adam("adam_w_ffn_out", r_fo, w_ffn_out[0], m_w_ffn_out[0], v_w_ffn_out[0],
                                    own(sent_fo))
    (sent_out,), (r_out,) = _exchange_wait("scatter_out_wait", out_started, [True], (big["w_ffn_out"][0],))
    big["w_out"] = _reduce_adam("adam_w_out", r_out, w_out[0], m_w_out[0], v_w_out[0], own(sent_out))
    (sent_in, sent_cw, sent_gcw), (r_in, r_cw, r_gcw) = _exchange_wait(
        "scatter_in_wait", in_started, [True] * 3, (big["w_out"][0],))
    big["w_in"] = [jnp.transpose(t) for t in _reduce_adam(
        "adam_w_in", r_in, jnp.transpose(w_in[0]), jnp.transpose(m_w_in[0]), jnp.transpose(v_w_in[0]),
        own(sent_in))]
    big["conv_w"] = _reduce_adam("adam_conv_w", r_cw, conv_w[0], m_conv_w[0], v_conv_w[0], own(sent_cw))
    big["gdn_conv_w"] = _reduce_adam("adam_gdn_conv_w", r_gcw, gdn_conv_w[0], m_gdn_conv_w[0], v_gdn_conv_w[0],
                                     own(sent_gcw))
    outs = [loss, loc["grad_x"].reshape(1, S, D)]
    for kind in range(4):
        for nm in WEIGHT_NAMES:
            outs.append(big[nm][kind][None] if nm in big else res[kind][nm])
    return tuple(outs)
```

```python
import functools

import jax
import jax.numpy as jnp
from jax import lax
from jax.experimental import pallas as pl
from jax.experimental.pallas import tpu as pltpu

F32 = jnp.float32
BF16 = jnp.bfloat16
HI = lax.Precision.HIGHEST
MESH = pl.DeviceIdType.MESH

N_DEV = 8
S = 2048
D = 1024
TM = 256
NT = S // TM
CW = 512
KC = 31
NG = 8
GSZ = CW // NG
HALO = 32
GW = 512
NH = 4
DH = 128
KS = 4
SH = 8
CL = 64
NCH = S // CL
NMAIN = 2 * CW + 4 * GW
NIN = NMAIN + 2 * NH
DFF = 2816
FB = DFF // 4
EPS = 1e-6
QSCALE = DH ** -0.5
LANES = 128
SMALL_ROWS = 88

ADAM_LR = 0.001
ADAM_B1 = 0.9
ADAM_B2 = 0.999
ADAM_EPS = 1e-08
ADAM_WD = 0.01
ADAM_STEP = 10
BC1 = 1.0 - ADAM_B1 ** ADAM_STEP
BC2 = 1.0 - ADAM_B2 ** ADAM_STEP

MIB = 1024 * 1024
VMEM_LIMIT_MIB = 32


def _params(limit_mib=VMEM_LIMIT_MIB, **kw):
    return pltpu.CompilerParams(vmem_limit_bytes=limit_mib * MIB, **kw)


def _sig(x):
    return jax.nn.sigmoid(x)


GP = BF16


def _operands(a, b, prec):
    if prec is BF16:
        return a.astype(BF16), b.astype(BF16), None
    return a, b, prec


def _dot(a, b, prec=None):
    a, b, prec = _operands(a, b, prec)
    return jnp.dot(a, b, preferred_element_type=F32, precision=prec)


def _dot_nt(a, b, prec=None):
    a, b, prec = _operands(a, b, prec)
    return lax.dot_general(a, b, (((1,), (1,)), ((), ())), preferred_element_type=F32, precision=prec)


def _dot_tn(a, b, prec=None):
    a, b, prec = _operands(a, b, prec)
    return lax.dot_general(a, b, (((0,), (0,)), ((), ())), preferred_element_type=F32, precision=prec)


def _lockstep(gens):
    gens = list(gens)
    while gens:
        alive = []
        for g in gens:
            try:
                next(g)
                alive.append(g)
            except StopIteration:
                pass
        gens = alive


def _rowsum(x):
    return jnp.sum(x, axis=-1, keepdims=True)


def _colsum(x):
    return jnp.sum(x, axis=0, keepdims=True)


def _mod(mod_ref, b_ref, k):
    return mod_ref[:, k * D:(k + 1) * D] + b_ref[:, k * D:(k + 1) * D]


def _const(shape):
    nd = len(shape)
    return pl.BlockSpec(shape, lambda *_: (0,) * nd)


def _const1(shape):
    nd = len(shape)
    return pl.BlockSpec(shape, lambda *_: (0,) * nd, pipeline_mode=pl.Buffered(1))


PEER_FLIPS = [(dx, dy, dc) for dx in (0, 1) for dy in (0, 1) for dc in (0, 1)][1:]


def _after(x, token):
    return x + token[0:1, 0:1].astype(x.dtype).reshape((1,) * x.ndim)


def _exchange(name, srcs, per_dest, seed_only=(), with_token=False):
    n = len(srcs)
    out_shape = []
    for a, pd in zip(srcs, per_dest):
        blk = a.shape[1:] if pd else a.shape
        out_shape.append(jax.ShapeDtypeStruct((N_DEV,) + tuple(blk), a.dtype))

    def body(*refs):
        src = refs[:n]
        dst = refs[n:2 * n]
        send_sems, recv_sems, local_sems = refs[-3:]
        if with_token:
            refs[2 * n][...] = jnp.zeros((8, LANES), F32)
        x, y, c = lax.axis_index("x"), lax.axis_index("y"), lax.axis_index("c")
        me = 4 * x + 2 * y + c

        def piece(i, j):
            return src[i].at[j] if per_dest[i] else src[i]

        copies = []
        for k, (dx, dy, dc) in enumerate(PEER_FLIPS):
            px = 1 - x if dx else x
            py = 1 - y if dy else y
            pc = 1 - c if dc else c
            pj = 4 * px + 2 * py + pc
            for i in range(n):
                if i in seed_only:
                    continue
                cp = pltpu.make_async_remote_copy(
                    src_ref=piece(i, pj), dst_ref=dst[i].at[me],
                    send_sem=send_sems.at[k * n + i], recv_sem=recv_sems.at[k * n + i],
                    device_id=(px, py, pc), device_id_type=MESH)
                cp.start()
                arrive = pltpu.make_async_remote_copy(
                    src_ref=piece(i, pj), dst_ref=dst[i].at[pj],
                    send_sem=send_sems.at[k * n + i], recv_sem=recv_sems.at[k * n + i],
                    device_id=(px, py, pc), device_id_type=MESH)
                copies.append((cp, arrive))
        own = []
        for i in range(n):
            lc = pltpu.make_async_copy(piece(i, me), dst[i].at[me], local_sems.at[i])
            lc.start()
            own.append(lc)
        for cp, arrive in copies:
            arrive.wait_recv()
        for cp, arrive in copies:
            cp.wait_send()
        for lc in own:
            lc.wait()

    any_spec = pl.BlockSpec(memory_space=pl.ANY)
    out_specs = [any_spec] * n
    if with_token:
        out_shape.append(jax.ShapeDtypeStruct((8, LANES), F32))
        out_specs.append(pl.BlockSpec(memory_space=pltpu.VMEM))
    return pl.pallas_call(
        body, name=name, out_shape=tuple(out_shape),
        in_specs=[any_spec] * n, out_specs=tuple(out_specs),
        scratch_shapes=[pltpu.SemaphoreType.DMA((7 * n,)), pltpu.SemaphoreType.DMA((7 * n,)),
                        pltpu.SemaphoreType.DMA((n,))],
        compiler_params=pltpu.CompilerParams(has_side_effects=True),
    )(*srcs)


CHIP_FLIPS = [(0, 1), (1, 0), (1, 1)]
LEVEL_ONE = [k for k, (dx, dy, dc) in enumerate(PEER_FLIPS) if (dx, dy, dc) == (0, 0, 1) or dc == 0]


def _chip_peers(x, y):
    return [(1 - x if dx else x, 1 - y if dy else y) for dx, dy in CHIP_FLIPS]


def _gather_two_level(name, srcs, seed_only=()):
    n = len(srcs)
    live = [i for i in range(n) if i not in seed_only]

    def body(*refs):
        src, dst = refs[:n], refs[n:2 * n]
        send_sems, recv_sems, local_sems = refs[2 * n:2 * n + 3]
        bounce = refs[2 * n + 3:]
        x, y, c = lax.axis_index("x"), lax.axis_index("y"), lax.axis_index("c")
        me = 4 * x + 2 * y + c
        sibling = (x, y, 1 - c)
        chips = _chip_peers(x, y)

        def copy(k, i, src_ref, slot, to):
            return pltpu.make_async_remote_copy(
                src_ref=src_ref, dst_ref=dst[i].at[slot], send_sem=send_sems.at[k * n + i],
                recv_sem=recv_sems.at[k * n + i], device_id=to, device_id_type=MESH)

        first = []
        for i in live:
            first.append(copy(0, i, src[i], me, sibling))
            first += [copy(1 + j, i, src[i], me, (px, py, c)) for j, (px, py) in enumerate(chips)]
        for cp in first:
            cp.start()
        up = [pltpu.make_async_copy(src[i], bounce[i], local_sems.at[i]) for i in range(n)]
        for cp in up:
            cp.start()
        for cp in up:
            cp.wait()
        own = [pltpu.make_async_copy(bounce[i], dst[i].at[me], local_sems.at[i]) for i in range(n)]
        for cp in own:
            cp.start()
        passed = []
        for j, (px, py) in enumerate(chips):
            slot = 4 * px + 2 * py + c
            for i in live:
                copy(1 + j, i, src[i], slot, (px, py, c)).wait_recv()
                fwd = copy(4 + j, i, dst[i].at[slot], slot, sibling)
                fwd.start()
                passed.append(fwd)
        for i in live:
            copy(0, i, src[i], 4 * x + 2 * y + 1 - c, sibling).wait_recv()
            for j, (px, py) in enumerate(chips):
                copy(4 + j, i, src[i], 4 * px + 2 * py + 1 - c, sibling).wait_recv()
        for cp in first + passed:
            cp.wait_send()
        for cp in own:
            cp.wait()

    any_spec = pl.BlockSpec(memory_space=pl.ANY)
    return pl.pallas_call(
        body, name=name, out_shape=tuple(jax.ShapeDtypeStruct((N_DEV,) + a.shape, a.dtype) for a in srcs),
        in_specs=[any_spec] * n, out_specs=tuple([any_spec] * n),
        scratch_shapes=[pltpu.SemaphoreType.DMA((7 * n,)), pltpu.SemaphoreType.DMA((7 * n,)),
                        pltpu.SemaphoreType.DMA((n,))] + [pltpu.VMEM(a.shape, a.dtype) for a in srcs],
        compiler_params=pltpu.CompilerParams(has_side_effects=True),
    )(*srcs)


def _relay_to_sibling(name, lands):
    n = len(lands)

    def body(*refs):
        land = refs[n:2 * n]
        send_sems, recv_sems = refs[-2:]
        x, y, c = lax.axis_index("x"), lax.axis_index("y"), lax.axis_index("c")
        sibling = (x, y, 1 - c)
        sends = []
        for j, (px, py) in enumerate(_chip_peers(x, y)):
            slot = 4 * px + 2 * py + c
            for i in range(n):
                cp = pltpu.make_async_remote_copy(
                    src_ref=land[i].at[slot], dst_ref=land[i].at[slot], send_sem=send_sems.at[j * n + i],
                    recv_sem=recv_sems.at[j * n + i], device_id=sibling, device_id_type=MESH)
                cp.start()
                sends.append(cp)
        for j, (px, py) in enumerate(_chip_peers(x, y)):
            slot = 4 * px + 2 * py + 1 - c
            for i in range(n):
                pltpu.make_async_remote_copy(
                    src_ref=land[i].at[slot], dst_ref=land[i].at[slot], send_sem=send_sems.at[j * n + i],
                    recv_sem=recv_sems.at[j * n + i], device_id=sibling, device_id_type=MESH).wait_recv()
        for cp in sends:
            cp.wait_send()

    any_spec = pl.BlockSpec(memory_space=pl.ANY)
    return pl.pallas_call(
        body, name=name, out_shape=tuple(jax.ShapeDtypeStruct(a.shape, a.dtype) for a in lands),
        in_specs=[any_spec] * n, out_specs=tuple([any_spec] * n),
        input_output_aliases={i: i for i in range(n)},
        scratch_shapes=[pltpu.SemaphoreType.DMA((3 * n,)), pltpu.SemaphoreType.DMA((3 * n,))],
        compiler_params=pltpu.CompilerParams(has_side_effects=True),
    )(*lands)


HBM_SPEC = pl.BlockSpec(memory_space=pltpu.HBM)
SEM_SPEC = pl.BlockSpec(memory_space=pltpu.SEMAPHORE)
DATAFLOW = pltpu.SideEffectType.DATAFLOW_SIDE_EFFECTING


def _peers(only=None):
    x, y, c = lax.axis_index("x"), lax.axis_index("y"), lax.axis_index("c")
    out = []
    for k, (dx, dy, dc) in enumerate(PEER_FLIPS):
        if only is not None and k not in only:
            continue
        px = 1 - x if dx else x
        py = 1 - y if dy else y
        pc = 1 - c if dc else c
        out.append((k, (px, py, pc), 4 * px + 2 * py + pc))
    return 4 * x + 2 * y + c, out


def _exchange_start(name, srcs, lands, per_dest, only=None):
    n = len(srcs)

    def body(*refs):
        src, land = refs[:n], refs[n:2 * n]
        send_sems, recv_sems = refs[2 * n], refs[2 * n + 1]
        token = refs[-1]
        me, peers = _peers(only)
        for k, peer, pj in peers:
            for i in range(n):
                pltpu.make_async_remote_copy(
                    src_ref=src[i].at[pj] if per_dest[i] else src[i], dst_ref=land[i].at[me],
                    send_sem=send_sems.at[k * n + i], recv_sem=recv_sems.at[k * n + i],
                    device_id=peer, device_id_type=MESH).start()
        token[...] = jnp.zeros((8, LANES), F32)

    arrays = list(srcs) + list(lands)
    return pl.pallas_call(
        body, name=name,
        out_shape=(pltpu.SemaphoreType.DMA((7 * n,)), pltpu.SemaphoreType.DMA((7 * n,)),
                   *[pltpu.HBM(a.shape, a.dtype) for a in arrays], jax.ShapeDtypeStruct((8, LANES), F32)),
        in_specs=[HBM_SPEC] * (2 * n),
        out_specs=(SEM_SPEC, SEM_SPEC, *[HBM_SPEC] * (2 * n), pl.BlockSpec(memory_space=pltpu.VMEM)),
        input_output_aliases={i: 2 + i for i in range(2 * n)},
        compiler_params=pltpu.CompilerParams(has_side_effects=DATAFLOW),
    )(*[pltpu.with_memory_space_constraint(a, pltpu.HBM) for a in arrays])


def _exchange_wait(name, started, per_dest, after, only=None):
    n = (len(started) - 3) // 2
    send_sems, recv_sems = started[0], started[1]
    arrays = list(started[2:2 + 2 * n])

    def body(*refs):
        src, land = refs[:n], refs[n:2 * n]
        send, recv = refs[2 * n], refs[2 * n + 1]
        me, peers = _peers(only)
        for k, peer, pj in peers:
            for i in range(n):
                cp = pltpu.make_async_remote_copy(
                    src_ref=src[i].at[pj] if per_dest[i] else src[i], dst_ref=land[i].at[pj],
                    send_sem=send.at[k * n + i], recv_sem=recv.at[k * n + i],
                    device_id=peer, device_id_type=MESH)
                cp.wait_send()
                cp.wait_recv()

    outs = pl.pallas_call(
        body, name=name,
        out_shape=tuple(pltpu.HBM(a.shape, a.dtype) for a in arrays),
        in_specs=[HBM_SPEC] * (2 * n) + [SEM_SPEC, SEM_SPEC] + [pl.BlockSpec(memory_space=pl.ANY)] * len(after),
        out_specs=tuple([HBM_SPEC] * (2 * n)),
        input_output_aliases={i: i for i in range(2 * n)},
        compiler_params=pltpu.CompilerParams(has_side_effects=DATAFLOW),
    )(*arrays, send_sems, recv_sems, *after)
    return outs[:n], outs[n:]


def _mod_shard(c_all, w_ada):
    def body(c_ref, w_ref, o_ref):
        cv = c_ref[...]
        ca = cv * _sig(cv)
        o_ref[...] = _dot(ca.astype(BF16), w_ref[...].astype(BF16))

    return pl.pallas_call(
        body, name="mod_shard", out_shape=jax.ShapeDtypeStruct((N_DEV, w_ada.shape[1]), F32),
        compiler_params=_params(),
    )(c_all, w_ada)


def _fwd_in(x, nw1, modnb, bada, w_main, w_ba):
    def body(x_ref, nw_ref, mod_ref, b_ref, wm_ref, wb_ref, pm_ref, pb_ref, hb_ref):
        xv = x_ref[...]
        r = lax.rsqrt(jnp.mean(xv * xv, axis=-1, keepdims=True) + EPS)
        h = (xv * r * nw_ref[...]) * (1.0 + _mod(mod_ref, b_ref, 1)) + _mod(mod_ref, b_ref, 0)
        hb = h.astype(BF16)
        hb_ref[...] = hb
        pm_ref[...] = _dot_nt(hb, wm_ref[...])
        pb_ref[...] = _dot_nt(hb, wb_ref[...])

    return pl.pallas_call(
        body, name="fwd_in", grid=(NT,),
        in_specs=[pl.BlockSpec((TM, D), lambda i: (i, 0)), _const((1, D)), _const((1, 6 * D)), _const((1, 6 * D)),
                  _const((NMAIN, D)), _const((LANES, D))],
        out_specs=(pl.BlockSpec((TM, NMAIN), lambda i: (i, 0)), pl.BlockSpec((TM, LANES), lambda i: (i, 0)),
                   pl.BlockSpec((TM, D), lambda i: (i, 0))),
        out_shape=(jax.ShapeDtypeStruct((S, NMAIN), F32), jax.ShapeDtypeStruct((S, LANES), F32),
                   jax.ShapeDtypeStruct((S, D), BF16)),
        compiler_params=_params(dimension_semantics=("arbitrary",)),
    )(x, nw1, modnb, bada, w_main, w_ba)


def _group_mean_matrix():
    ii = lax.broadcasted_iota(jnp.int32, (CW, CW), 0) // GSZ
    jj = lax.broadcasted_iota(jnp.int32, (CW, CW), 1) // GSZ
    return jnp.where(ii == jj, 1.0 / GSZ, 0.0).astype(F32)


SUB = 8
SHIFT_ROWS = HALO + TM - SUB


def _fill_shifted(buf, sh):
    for b in range(1, SUB):
        sh[b - 1] = buf[b:b + SHIFT_ROWS, :]


def _rows_at(buf, sh, off):
    a, b = divmod(off, SUB)
    if b == 0:
        return buf[off:off + TM, :]
    return sh[b - 1, SUB * a:SUB * a + TM, :]


def _group_mean(x, pm):
    hi = x.astype(BF16)
    r1 = x - hi.astype(F32)
    mid = r1.astype(BF16)
    lo = (r1 - mid.astype(F32)).astype(BF16)
    return _dot(hi, pm) + _dot(mid, pm) + _dot(lo, pm)


def _conf_fwd(p_main, conv_w, conv_b, gn_w, gn_b):
    def body(a_ref, g_ref, w_ref, b_ref, gw_ref, gb_ref, y_ref, oa_ref, ubuf, ush):
        i = pl.program_id(0)

        @pl.when(i == 0)
        def _():
            ubuf[0:HALO, :] = jnp.zeros((HALO, CW), F32)

        ubuf[HALO:HALO + TM, :] = a_ref[...] * _sig(g_ref[...])
        _fill_shifted(ubuf, ush)
        acc = jnp.zeros((TM, CW), F32) + b_ref[...]
        for k in range(KC):
            acc = acc + w_ref[k:k + 1, :] * _rows_at(ubuf, ush, HALO - (KC - 1) + k)
        y_ref[...] = acc
        ubuf[0:HALO, :] = ubuf[TM:TM + HALO, :]
        pm = _group_mean_matrix().astype(BF16)
        dlt = acc - _group_mean(acc, pm)
        var = _group_mean(dlt * dlt, pm)
        o = dlt * lax.rsqrt(var + EPS) * gw_ref[...] + gb_ref[...]
        oa_ref[...] = o * _sig(o)

    return pl.pallas_call(
        body, name="conf_fwd", grid=(NT,),
        in_specs=[pl.BlockSpec((TM, CW), lambda i: (i, 0)), pl.BlockSpec((TM, CW), lambda i: (i, 1)),
                  _const((KC, CW)), _const((1, CW)), _const((1, CW)), _const((1, CW))],
        out_specs=(pl.BlockSpec((TM, CW), lambda i: (i, 0)), pl.BlockSpec((TM, CW), lambda i: (i, 0))),
        out_shape=(jax.ShapeDtypeStruct((S, CW), F32), jax.ShapeDtypeStruct((S, CW), F32)),
        scratch_shapes=[pltpu.VMEM((HALO + TM, CW), F32), pltpu.VMEM((SUB - 1, SHIFT_ROWS, CW), F32)],
        compiler_params=_params(dimension_semantics=("arbitrary",)),
    )(p_main, p_main, conv_w, conv_b, gn_w, gn_b)


def _tri_iota():
    ii = lax.broadcasted_iota(jnp.int32, (CL, CL), 0)
    jj = lax.broadcasted_iota(jnp.int32, (CL, CL), 1)
    return ii, jj


def _gdn_gates(ba, alog_l, dt_l):
    beta_all = _sig(ba)
    xg = ba + dt_l
    sp = jnp.maximum(xg, 0.0) + jnp.log(1.0 + jnp.exp(-jnp.abs(xg)))
    neg_a = -jnp.exp(alog_l)
    return beta_all, neg_a * sp, xg, neg_a


def _ones_dot(ones, x):
    hi = x.astype(BF16)
    r1 = x - hi.astype(F32)
    mid = r1.astype(BF16)
    lo = (r1 - mid.astype(F32)).astype(BF16)
    return _dot(ones, hi) + _dot(ones, mid) + _dot(ones, lo)


def _gdn_cumsum(g_all):
    ii, jj = _tri_iota()
    low = jnp.where(ii >= jj, 1.0, 0.0).astype(BF16)
    gcum = _ones_dot(low, g_all)
    return gcum, jnp.transpose(gcum)


def _split(x):
    hi = x.astype(BF16)
    return hi, (x - hi.astype(F32)).astype(BF16)


def _dot_split(a, b):
    (ah, al), (bh, bl) = a, b
    return _dot(ah, bh) + (_dot(ah, bl) + _dot(al, bh))


def _unit_lower_inverses(mats):
    ii, jj = _tri_iota()
    eye = jnp.where(ii == jj, 1.0, 0.0).astype(F32)
    ts = [eye - a for a in mats]
    ps = [_dot_split(s, s) for s in map(_split, mats)]
    for _ in range(4):
        sp = [_split(p) for p in ps]
        ts = [t + _dot_split(_split(t), s) for t, s in zip(ts, sp)]
        ps = [_dot_split(s, s) for s in sp]
    return [t + _dot_split(_split(t), _split(p)) for t, p in zip(ts, ps)]


def _head_terms(qh, kh, beta, gcol, grow):
    ii, jj = _tri_iota()
    causal = ii >= jj
    strict = ii > jj
    rq = lax.rsqrt(_rowsum(qh * qh) + EPS)
    rk = lax.rsqrt(_rowsum(kh * kh) + EPS)
    qn = qh * rq
    kn = kh * rk
    qs = qn * QSCALE
    decay = jnp.where(causal, jnp.exp(jnp.where(causal, gcol - grow, 0.0)), 0.0)
    gam = jnp.exp(gcol)
    gl = gcol[CL - 1:CL, :]
    kds = jnp.exp(gl - gcol)
    cd = jnp.exp(gl)
    kb = kn * beta
    a = jnp.where(strict, _dot_nt(kb, kn, GP) * decay, 0.0)
    qk = jnp.where(causal, _dot_nt(qs, kn, GP) * decay, 0.0)
    return dict(rq=rq, rk=rk, qn=qn, kn=kn, qs=qs, decay=decay, gam=gam, kds=kds, cd=cd, kb=kb, a=a, qk=qk,
                causal=causal, strict=strict)


def _short_conv(w_ref, buf, rows=CL):
    acc = w_ref[0:1, :] * buf[SH - KS + 1:SH - KS + 1 + rows, :]
    for k in range(1, KS):
        off = SH - (KS - 1) + k
        acc = acc + w_ref[k:k + 1, :] * buf[off:off + rows, :]
    return acc


CPS = 4
TG = CPS * CL


def _gdn_prep(p_main, p_ba, gdn_conv_w, alog_l, dt_l):
    def body(q_ref, k_ref, v_ref, qh_ref, kh_ref, vh_ref, ba_ref, w_ref, al_ref, dt_ref,
             wo_ref, uo_ref, qg_ref, kd_ref, qk_ref, cd_ref, t_ref, xbuf):
        i = pl.program_id(0)
        first = i == 0
        xbuf[0:SH, 0:GW] = jnp.where(first, 0.0, qh_ref[...])
        xbuf[0:SH, GW:2 * GW] = jnp.where(first, 0.0, kh_ref[...])
        xbuf[0:SH, 2 * GW:3 * GW] = jnp.where(first, 0.0, vh_ref[...])
        xbuf[SH:SH + TG, 0:GW] = q_ref[...]
        xbuf[SH:SH + TG, GW:2 * GW] = k_ref[...]
        xbuf[SH:SH + TG, 2 * GW:3 * GW] = v_ref[...]
        conv = _short_conv(w_ref, xbuf, TG)
        qkv = conv * _sig(conv)
        beta_all, g_all, _, _ = _gdn_gates(ba_ref[...], al_ref[...], dt_ref[...])
        lane = lax.broadcasted_iota(jnp.int32, (8, LANES), 1)
        cums = [_gdn_cumsum(g_all[cc * CL:(cc + 1) * CL, :]) for cc in range(CPS)]
        pairs = [(cc, h) for cc in range(CPS) for h in range(NH)]
        terms, vbs = [], []
        for cc, h in pairs:
            r0, lo = cc * CL, h * DH
            beta = beta_all[r0:r0 + CL, h:h + 1]
            gcum, gcum_t = cums[cc]
            terms.append(_head_terms(qkv[r0:r0 + CL, lo:lo + DH], qkv[r0:r0 + CL, GW + lo:GW + lo + DH], beta,
                                     gcum[:, NH + h:NH + h + 1], gcum_t[NH + h:NH + h + 1, :]))
            vbs.append(qkv[r0:r0 + CL, 2 * GW + lo:2 * GW + lo + DH] * beta)
        invs = _unit_lower_inverses([f["a"] for f in terms])
        cds = [jnp.zeros((8, LANES), F32) for _ in range(CPS)]
        for (cc, h), f, t, vb in zip(pairs, terms, invs, vbs):
            r0, lo = cc * CL, h * DH
            t_ref[cc, h] = t
            uo_ref[r0:r0 + CL, lo:lo + DH] = _dot(t, vb, GP)
            wo_ref[r0:r0 + CL, lo:lo + DH] = _dot(t, f["kb"] * f["gam"], GP).astype(BF16)
            qg_ref[r0:r0 + CL, lo:lo + DH] = (f["qs"] * f["gam"]).astype(BF16)
            kd_ref[r0:r0 + CL, lo:lo + DH] = (f["kn"] * f["kds"]).astype(BF16)
            qk_ref[cc, h] = f["qk"].astype(BF16)
            cds[cc] = cds[cc] + jnp.where(lane == h, f["cd"], 0.0)
        for cc in range(CPS):
            cd_ref[cc] = cds[cc]

    col = lambda j: pl.BlockSpec((TG, GW), lambda i: (i, j))
    halo = lambda j: pl.BlockSpec((SH, GW), lambda i: (jnp.maximum(i * (TG // SH) - 1, 0), j))
    tile = lambda: pl.BlockSpec((TG, GW), lambda i: (i, 0))
    sq = lambda: pl.BlockSpec((CPS, NH, CL, CL), lambda i: (i, 0, 0, 0))
    return pl.pallas_call(
        body, name="gdn_prep", grid=(NCH // CPS,),
        in_specs=[col(2), col(3), col(4), halo(2), halo(3), halo(4), pl.BlockSpec((TG, LANES), lambda i: (i, 0)),
                  _const((KS, 3 * GW)), _const((1, LANES)), _const((1, LANES))],
        out_specs=(tile(), tile(), tile(), tile(), sq(), pl.BlockSpec((CPS, 8, LANES), lambda i: (i, 0, 0)), sq()),
        out_shape=(jax.ShapeDtypeStruct((S, GW), BF16), jax.ShapeDtypeStruct((S, GW), F32),
                   jax.ShapeDtypeStruct((S, GW), BF16), jax.ShapeDtypeStruct((S, GW), BF16),
                   jax.ShapeDtypeStruct((NCH, NH, CL, CL), BF16), jax.ShapeDtypeStruct((NCH, 8, LANES), F32),
                   jax.ShapeDtypeStruct((NCH, NH, CL, CL), F32)),
        scratch_shapes=[pltpu.VMEM((SH + TG, 3 * GW), F32)],
        compiler_params=_params(dimension_semantics=("arbitrary",)),
    )(p_main, p_main, p_main, p_main, p_main, p_main, p_ba, gdn_conv_w, alog_l, dt_l)


def _gdn_scan(w_o, u_o, qg, kd, qk, cd, p_main, gdn_nw):
    def body(w_ref, u_ref, qg_ref, kd_ref, qk_ref, cd_ref, z_ref, nw_ref, ob_ref, o_ref, sin_ref, state):
        n = pl.program_id(0)

        @pl.when(n == 0)
        def _():
            state[...] = jnp.zeros((NH, DH, DH), F32)

        def head(h):
            lo = h * DH
            st = state[h]
            sin_ref[0, h] = st
            sb = st.astype(BF16)
            v_new = u_ref[:, lo:lo + DH] - _dot(w_ref[:, lo:lo + DH], sb)
            yield
            vb = v_new.astype(BF16)
            o = _dot(qg_ref[:, lo:lo + DH], sb) + _dot(qk_ref[0, h], vb)
            state[h] = st * cd_ref[0, 0:1, h:h + 1] + _dot_tn(kd_ref[:, lo:lo + DH], vb)
            yield
            o_ref[:, lo:lo + DH] = o
            r = lax.rsqrt(jnp.mean(o * o, axis=-1, keepdims=True) + EPS)
            zh = z_ref[:, lo:lo + DH]
            ob_ref[:, lo:lo + DH] = o * r * nw_ref[...] * (zh * _sig(zh))

        _lockstep(head(h) for h in range(NH))

    tile = lambda: pl.BlockSpec((CL, GW), lambda n: (n, 0))
    return pl.pallas_call(
        body, name="gdn_scan", grid=(NCH,),
        in_specs=[tile(), tile(), tile(), tile(), pl.BlockSpec((1, NH, CL, CL), lambda n: (n, 0, 0, 0)),
                  pl.BlockSpec((1, 8, LANES), lambda n: (n, 0, 0)), pl.BlockSpec((CL, GW), lambda n: (n, 5)),
                  _const((1, DH))],
        out_specs=(tile(), tile(), pl.BlockSpec((1, NH, DH, DH), lambda n: (n, 0, 0, 0))),
        out_shape=(jax.ShapeDtypeStruct((S, GW), F32), jax.ShapeDtypeStruct((S, GW), F32),
                   jax.ShapeDtypeStruct((NCH, NH, DH, DH), F32)),
        scratch_shapes=[pltpu.VMEM((NH, DH, DH), F32)],
        compiler_params=_params(dimension_semantics=("arbitrary",)),
    )(w_o, u_o, qg, kd, qk, cd, p_main, gdn_nw)


def _fwd_out(out_a, out_b, x, modnb, bada, w_out):
    def body(oa_ref, ob_ref, x_ref, mod_ref, b_ref, w_ref, x1_ref, mix_ref, oab_ref):
        oa = oa_ref[...].astype(BF16)
        ob = ob_ref[...].astype(BF16)
        oab_ref[:, 0:CW] = oa
        oab_ref[:, CW:D] = ob
        mix = _dot(oa, w_ref[0:CW, :]) + _dot(ob, w_ref[CW:D, :])
        mix_ref[...] = mix
        x1_ref[...] = x_ref[...] + _mod(mod_ref, b_ref, 2) * mix

    tile = lambda w: pl.BlockSpec((TM, w), lambda i: (i, 0))
    return pl.pallas_call(
        body, name="fwd_out", grid=(NT,),
        in_specs=[tile(CW), tile(GW), tile(D), _const((1, 6 * D)), _const((1, 6 * D)), _const((D, D))],
        out_specs=(tile(D), tile(D), tile(D)),
        out_shape=(jax.ShapeDtypeStruct((S, D), F32), jax.ShapeDtypeStruct((S, D), F32),
                   jax.ShapeDtypeStruct((S, D), BF16)),
        compiler_params=_params(dimension_semantics=("arbitrary",)),
    )(out_a, out_b, x, modnb, bada, w_out)


FFN_STATS = 8


def _ffn_forward(x1, tgt, modnb, bada, nw2, nfw, w_fi, w_fo):
    def body(x1_ref, tgt_ref, mod_ref, b_ref, nw2_ref, nfw_ref, wi_ref, wo_ref,
             hb_ref, act_ref, pre_ref, dx2_ref, dffn_ref, st_ref):
        i = pl.program_id(0)

        @pl.when(i == 0)
        def _():
            st_ref[...] = jnp.zeros((FFN_STATS, D), F32)

        sh2, sc2, gt2 = _mod(mod_ref, b_ref, 3), _mod(mod_ref, b_ref, 4), _mod(mod_ref, b_ref, 5)
        x1v = x1_ref[...]
        r2 = lax.rsqrt(jnp.mean(x1v * x1v, axis=-1, keepdims=True) + EPS)
        hb = ((x1v * r2 * nw2_ref[...]) * (1.0 + sc2) + sh2).astype(BF16)
        hb_ref[...] = hb
        ffn = jnp.zeros((TM, D), F32)
        for j in range(4):
            fgj = _dot_nt(hb, wi_ref[j])
            fuj = _dot_nt(hb, wi_ref[j + 4])
            pre_ref[j] = fgj.astype(BF16)
            pre_ref[j + 4] = fuj.astype(BF16)
            aj = (fgj * _sig(fgj) * fuj).astype(BF16)
            act_ref[j] = aj
            ffn = ffn + _dot(aj, wo_ref[j])
        x2 = x1v + gt2 * ffn
        r3 = lax.rsqrt(jnp.mean(x2 * x2, axis=-1, keepdims=True) + EPS)
        xr3 = x2 * r3
        err = xr3 * nfw_ref[...] - tgt_ref[...]
        loss = 0.5 * jnp.sum(jnp.mean(err * err, axis=-1, keepdims=True), axis=0, keepdims=True)
        dy = err * (1.0 / D)
        st_ref[0:1, :] += _colsum(dy * xr3)
        dyr = dy * nfw_ref[...]
        dx2 = r3 * (dyr - xr3 * jnp.mean(dyr * xr3, axis=-1, keepdims=True))
        st_ref[1:2, :] += _colsum(dx2 * ffn)
        st_ref[5:6, :] += jnp.broadcast_to(loss, (1, D))
        dx2_ref[...] = dx2
        dffn_ref[...] = (gt2 * dx2).astype(BF16)

    tile = lambda w: pl.BlockSpec((TM, w), lambda i: (i, 0))
    return pl.pallas_call(
        body, name="ffn_forward", grid=(NT,),
        in_specs=[tile(D), tile(D), _const((1, 6 * D)), _const((1, 6 * D)), _const((1, D)), _const((1, D)),
                  _const1((N_DEV, FB, D)), _const1((4, FB, D))],
        out_specs=(tile(D), pl.BlockSpec((4, TM, FB), lambda i: (0, i, 0)),
                   pl.BlockSpec((N_DEV, TM, FB), lambda i: (0, i, 0)), tile(D), tile(D), _const((FFN_STATS, D))),
        out_shape=(jax.ShapeDtypeStruct((S, D), BF16), jax.ShapeDtypeStruct((4, S, FB), BF16),
                   jax.ShapeDtypeStruct((N_DEV, S, FB), BF16), jax.ShapeDtypeStruct((S, D), F32),
                   jax.ShapeDtypeStruct((S, D), BF16), jax.ShapeDtypeStruct((FFN_STATS, D), F32)),
        compiler_params=_params(42, dimension_semantics=("arbitrary",)),
    )(x1, tgt, modnb, bada, nw2, nfw, w_fi, w_fo)


def _ffn_backward(dffn, pre, x1, dx2, modnb, bada, nw2, w_fi, w_fo):
    def body(dffn_ref, pre_ref, x1_ref, dx2_ref, mod_ref, b_ref, nw2_ref, wi_ref, wo_ref, df_ref, dx1_ref, st_ref):
        i = pl.program_id(0)

        @pl.when(i == 0)
        def _():
            st_ref[...] = jnp.zeros((FFN_STATS, D), F32)

        dffn = dffn_ref[...]
        dh = jnp.zeros((TM, D), F32)
        for j in range(4):
            fg = pre_ref[j].astype(F32)
            fu = pre_ref[j + 4].astype(F32)
            sg = _sig(fg)
            dact = _dot_nt(dffn, wo_ref[j])
            dfg = (dact * fu * (sg * (1.0 + fg * (1.0 - sg)))).astype(BF16)
            dfu = (dact * (fg * sg)).astype(BF16)
            df_ref[j] = dfg
            df_ref[j + 4] = dfu
            dh = dh + _dot(dfg, wi_ref[j]) + _dot(dfu, wi_ref[j + 4])
        x1v = x1_ref[...]
        r2 = lax.rsqrt(jnp.mean(x1v * x1v, axis=-1, keepdims=True) + EPS)
        xr2 = x1v * r2
        st_ref[2:3, :] += _colsum(dh)
        st_ref[3:4, :] += _colsum(dh * (xr2 * nw2_ref[...]))
        dxn = dh * (1.0 + _mod(mod_ref, b_ref, 4))
        st_ref[4:5, :] += _colsum(dxn * xr2)
        dxr = dxn * nw2_ref[...]
        dx1_ref[...] = dx2_ref[...] + r2 * (dxr - xr2 * jnp.mean(dxr * xr2, axis=-1, keepdims=True))

    tile = lambda w: pl.BlockSpec((TM, w), lambda i: (i, 0))
    wide = lambda: pl.BlockSpec((N_DEV, TM, FB), lambda i: (0, i, 0))
    return pl.pallas_call(
        body, name="ffn_backward", grid=(NT,),
        in_specs=[tile(D), wide(), tile(D), tile(D), _const((1, 6 * D)), _const((1, 6 * D)), _const((1, D)),
                  _const1((N_DEV, FB, D)), _const1((4, FB, D))],
        out_specs=(wide(), tile(D), _const((FFN_STATS, D))),
        out_shape=(jax.ShapeDtypeStruct((N_DEV, S, FB), BF16), jax.ShapeDtypeStruct((S, D), F32),
                   jax.ShapeDtypeStruct((FFN_STATS, D), F32)),
        compiler_params=_params(44, dimension_semantics=("arbitrary",)),
    )(dffn, pre, x1, dx2, modnb, bada, nw2, w_fi, w_fo)


def _grad_w(name, a, b, nb):
    m, n = a.shape[1], b.shape[1]

    def body(a_ref, b_ref, o_ref):
        o_ref[...] = _dot_tn(a_ref[...], b_ref[...]).astype(BF16)

    return pl.pallas_call(
        body, name=name, grid=(m // nb,),
        in_specs=[pl.BlockSpec((S, nb), lambda j: (0, j)), _const((S, n))],
        out_specs=pl.BlockSpec((nb, n), lambda j: (j, 0)),
        out_shape=jax.ShapeDtypeStruct((m, n), BF16),
        compiler_params=_params(dimension_semantics=("arbitrary",)),
    )(a, b)


def _grad_w_ffn_in(hb2, df):
    def body(a_ref, b_ref, o_ref):
        o_ref[0] = _dot_tn(b_ref[0], a_ref[...]).astype(BF16)

    return pl.pallas_call(
        body, name="grad_w_ffn_in", grid=(N_DEV,),
        in_specs=[_const((S, D)), pl.BlockSpec((1, S, FB), lambda j: (j, 0, 0))],
        out_specs=pl.BlockSpec((1, FB, D), lambda j: (j, 0, 0)),
        out_shape=jax.ShapeDtypeStruct((N_DEV, FB, D), BF16),
        compiler_params=_params(dimension_semantics=("arbitrary",)),
    )(hb2, df)


def _grad_w_ffn_out(act, dffn):
    def body(a_ref, b_ref, o_ref):
        o_ref[0] = _dot_tn(a_ref[0], b_ref[...]).astype(BF16)

    return pl.pallas_call(
        body, name="grad_w_ffn_out", grid=(4,),
        in_specs=[pl.BlockSpec((1, S, FB), lambda j: (j, 0, 0)), _const((S, D))],
        out_specs=pl.BlockSpec((1, FB, D), lambda j: (j, 0, 0)),
        out_shape=jax.ShapeDtypeStruct((4, FB, D), BF16),
        compiler_params=_params(dimension_semantics=("arbitrary",)),
    )(act, dffn)


def _bwd_out(dx1, mix, modnb, bada, w_out):
    def body(dx_ref, mix_ref, mod_ref, b_ref, w_ref, dmix_ref, doa_ref, dob_ref, st_ref):
        i = pl.program_id(0)

        @pl.when(i == 0)
        def _():
            st_ref[...] = jnp.zeros((8, D), F32)

        dx = dx_ref[...]
        st_ref[0:1, :] += _colsum(dx * mix_ref[...])
        dmix = (_mod(mod_ref, b_ref, 2) * dx).astype(BF16)
        dmix_ref[...] = dmix
        doa_ref[...] = _dot_nt(dmix, w_ref[0:CW, :])
        dob_ref[...] = _dot_nt(dmix, w_ref[CW:D, :])

    tile = lambda w: pl.BlockSpec((TM, w), lambda i: (i, 0))
    return pl.pallas_call(
        body, name="bwd_out", grid=(NT,),
        in_specs=[tile(D), tile(D), _const((1, 6 * D)), _const((1, 6 * D)), _const((D, D))],
        out_specs=(tile(D), tile(CW), tile(GW), _const((8, D))),
        out_shape=(jax.ShapeDtypeStruct((S, D), BF16), jax.ShapeDtypeStruct((S, CW), F32),
                   jax.ShapeDtypeStruct((S, GW), F32), jax.ShapeDtypeStruct((8, D), F32)),
        compiler_params=_params(dimension_semantics=("arbitrary",)),
    )(dx1, mix, modnb, bada, w_out)


CONF_STATS = 40


def _conf_bwd(d_out_a, y, p_main, conv_w, gn_w, gn_b):
    def body(do_ref, y_ref, a_ref, g_ref, ah_ref, gh_ref, w_ref, gw_ref, gb_ref, dp_ref, st_ref,
             ubuf, dybuf, ush, dysh):
        i = pl.program_id(0)

        @pl.when(i == 0)
        def _():
            st_ref[...] = jnp.zeros((CONF_STATS, CW), F32)
            dybuf[TM:TM + HALO, :] = jnp.zeros((HALO, CW), F32)

        pm = _group_mean_matrix().astype(BF16)
        yv = y_ref[...]
        dlt = yv - _group_mean(yv, pm)
        rstd = lax.rsqrt(_group_mean(dlt * dlt, pm) + EPS)
        un = dlt * rstd
        o = un * gw_ref[...] + gb_ref[...]
        so = _sig(o)
        d_o = do_ref[...] * (so * (1.0 + o * (1.0 - so)))
        st_ref[33:34, :] += _colsum(d_o)
        st_ref[32:33, :] += _colsum(d_o * un)
        dun = d_o * gw_ref[...]
        dy = rstd * (dun - _group_mean(dun, pm) - un * _group_mean(dun * un, pm))
        st_ref[31:32, :] += _colsum(dy)
        dybuf[0:TM, :] = dy
        _fill_shifted(dybuf, dysh)

        a = a_ref[...]
        sg = _sig(g_ref[...])
        first = i == NT - 1
        ubuf[0:HALO, :] = jnp.where(first, 0.0, ah_ref[...] * _sig(gh_ref[...]))
        ubuf[HALO:HALO + TM, :] = a * sg
        _fill_shifted(ubuf, ush)
        du = jnp.zeros((TM, CW), F32)
        for k in range(KC):
            st_ref[k:k + 1, :] += _colsum(dy * _rows_at(ubuf, ush, HALO - (KC - 1) + k))
            du = du + w_ref[k:k + 1, :] * _rows_at(dybuf, dysh, KC - 1 - k)
        dybuf[TM:TM + HALO, :] = dybuf[0:HALO, :]
        dp_ref[:, 0:CW] = (du * sg).astype(BF16)
        dp_ref[:, CW:2 * CW] = (du * a * sg * (1.0 - sg)).astype(BF16)

    rev = lambda w, j=0: pl.BlockSpec((TM, w), lambda i: (NT - 1 - i, j))
    halo = lambda j: pl.BlockSpec((HALO, CW), lambda i: (jnp.maximum((NT - 1 - i) * (TM // HALO) - 1, 0), j))
    return pl.pallas_call(
        body, name="conf_bwd", grid=(NT,),
        in_specs=[rev(CW), rev(CW), rev(CW, 0), rev(CW, 1), halo(0), halo(1),
                  _const((KC, CW)), _const((1, CW)), _const((1, CW))],
        out_specs=(rev(2 * CW), _const((CONF_STATS, CW))),
        out_shape=(jax.ShapeDtypeStruct((S, 2 * CW), BF16), jax.ShapeDtypeStruct((CONF_STATS, CW), F32)),
        scratch_shapes=[pltpu.VMEM((HALO + TM, CW), F32), pltpu.VMEM((TM + HALO, CW), F32),
                        pltpu.VMEM((SUB - 1, SHIFT_ROWS, CW), F32), pltpu.VMEM((SUB - 1, SHIFT_ROWS, CW), F32)],
        compiler_params=_params(dimension_semantics=("arbitrary",)),
    )(d_out_a, y, p_main, p_main, p_main, p_main, conv_w, gn_w, gn_b)


GDN_STATS = 8


def _gdn_bwd(d_out_b, o_pre, s_in, t_inv, p_main, p_ba, gdn_conv_w, alog_l, dt_l, gdn_nw):
    def body(dob_ref, o_ref, sin_ref, t_ref, q_ref, k_ref, v_ref, z_ref, qh_ref, kh_ref, vh_ref, ba_ref,
             w_ref, al_ref, dt_ref, nw_ref, dp_ref, dba_ref, st_ref, xbuf, dcbuf, dstate):
        n = pl.program_id(0)

        @pl.when(n == 0)
        def _():
            st_ref[...] = jnp.zeros((GDN_STATS, 3 * GW), F32)
            dcbuf[CL:CL + SH, :] = jnp.zeros((SH, 3 * GW), F32)
            dstate[...] = jnp.zeros((NH, DH, DH), F32)

        first = n == NCH - 1
        xbuf[0:SH, 0:GW] = jnp.where(first, 0.0, qh_ref[...])
        xbuf[0:SH, GW:2 * GW] = jnp.where(first, 0.0, kh_ref[...])
        xbuf[0:SH, 2 * GW:3 * GW] = jnp.where(first, 0.0, vh_ref[...])
        xbuf[SH:SH + CL, 0:GW] = q_ref[...]
        xbuf[SH:SH + CL, GW:2 * GW] = k_ref[...]
        xbuf[SH:SH + CL, 2 * GW:3 * GW] = v_ref[...]
        conv = _short_conv(w_ref, xbuf)
        sc = _sig(conv)
        qkv = conv * sc
        ba = ba_ref[...]
        beta_all, g_all, xg, neg_a = _gdn_gates(ba, al_ref[...], dt_ref[...])
        gcum, gcum_t = _gdn_cumsum(g_all)
        lane = lax.broadcasted_iota(jnp.int32, (CL, LANES), 1)
        row = lax.broadcasted_iota(jnp.int32, (CL, 1), 0)
        acc = dict(dgcum=jnp.zeros((CL, LANES), F32), dbeta=jnp.zeros((CL, LANES), F32))

        def head(h):
            lo = h * DH
            qh = qkv[:, lo:lo + DH]
            kh = qkv[:, GW + lo:GW + lo + DH]
            vh = qkv[:, 2 * GW + lo:2 * GW + lo + DH]
            beta = beta_all[:, h:h + 1]
            f = _head_terms(qh, kh, beta, gcum[:, NH + h:NH + h + 1], gcum_t[NH + h:NH + h + 1, :])
            qn, kn, qs, kb, gam, kds, cd, decay = (f[s] for s in ("qn", "kn", "qs", "kb", "gam", "kds", "cd", "decay"))
            t = t_ref[0, h]
            st = sin_ref[0, h]
            vb = vh * beta
            kbg = kb * gam
            u = _dot(t, vb, GP)
            w = _dot(t, kbg, GP)
            yield
            v_new = u - _dot(w, st, GP)
            q_dec = qs * gam
            k_dec = kn * kds

            o = o_ref[:, lo:lo + DH]
            zh = z_ref[:, lo:lo + DH]
            sz = _sig(zh)
            r = lax.rsqrt(jnp.mean(o * o, axis=-1, keepdims=True) + EPS)
            orr = o * r
            d_out = dob_ref[:, lo:lo + DH]
            dz = d_out * (orr * nw_ref[...]) * (sz * (1.0 + zh * (1.0 - sz)))
            don = d_out * (zh * sz)
            st_ref[4:5, 0:DH] += _colsum(don * orr)
            tt = don * nw_ref[...]
            d_o = r * (tt - orr * jnp.mean(tt * orr, axis=-1, keepdims=True))

            yield
            ds_out = dstate[h]
            dv_new = _dot_tn(f["qk"], d_o, GP) + _dot(k_dec, ds_out, GP)
            dqk = jnp.where(f["causal"], _dot_nt(d_o, v_new, GP), 0.0)
            dq_dec = _dot_nt(d_o, st, GP)
            dk_dec = _dot_nt(v_new, ds_out, GP)
            yield
            dstate[h] = _dot_tn(q_dec, d_o, GP) + cd * ds_out - _dot_tn(w, dv_new, GP)
            dcd = jnp.sum(_rowsum(st * ds_out), axis=0, keepdims=True)
            dw = -_dot_nt(dv_new, st, GP)
            dvb = _dot_tn(t, dv_new, GP)
            yield
            dt_m = _dot_nt(dv_new, vb, GP) + _dot_nt(dw, kbg, GP)
            dkbg = _dot_tn(t, dw, GP)
            yield
            dtt = _dot_nt(dt_m, t, GP)
            yield
            da = jnp.where(f["strict"], -_dot_tn(t, dtt, GP), 0.0)
            yield
            dad = da * decay
            dqkd = dqk * decay
            dkb = _dot(dad, kn, GP) + dkbg * gam
            dkn = _dot_tn(dad, kb, GP) + _dot_tn(dqkd, qs, GP) + dk_dec * kds + dkb * beta
            dqs = _dot(dqkd, kn, GP) + dq_dec * gam
            yield
            m = da * f["a"] + dqk * f["qk"]
            tk = _rowsum(dk_dec * k_dec)
            dgl = jnp.sum(tk, axis=0, keepdims=True) + dcd * cd
            dgc = (_rowsum(m) - _rowsum(jnp.transpose(m)) + _rowsum(dq_dec * q_dec) - tk + _rowsum(dkbg * kbg)
                   + jnp.where(row == CL - 1, dgl, 0.0))
            dbeta = _rowsum(dkb * kn) + _rowsum(dvb * vh)
            acc["dgcum"] = acc["dgcum"] + jnp.where(lane == NH + h, dgc, 0.0)
            acc["dbeta"] = acc["dbeta"] + jnp.where(lane == h, dbeta, 0.0)
            dvh = dvb * beta
            dqn = dqs * QSCALE
            dqh = f["rq"] * (dqn - qn * _rowsum(dqn * qn))
            dkh = f["rk"] * (dkn - kn * _rowsum(dkn * kn))
            dsilu = lambda c0: sc[:, c0:c0 + DH] * (1.0 + conv[:, c0:c0 + DH] * (1.0 - sc[:, c0:c0 + DH]))
            dcbuf[0:CL, lo:lo + DH] = dqh * dsilu(lo)
            dcbuf[0:CL, GW + lo:GW + lo + DH] = dkh * dsilu(GW + lo)
            dcbuf[0:CL, 2 * GW + lo:2 * GW + lo + DH] = dvh * dsilu(2 * GW + lo)
            dp_ref[:, 3 * GW + lo:3 * GW + lo + DH] = dz.astype(BF16)

        _lockstep(head(h) for h in range(NH))
        dgcum_all, dbeta_all = acc["dgcum"], acc["dbeta"]

        ii, jj = _tri_iota()
        upper = jnp.where(ii <= jj, 1.0, 0.0).astype(BF16)
        dg_all = _ones_dot(upper, dgcum_all)
        dxg = dg_all * neg_a * _sig(xg)
        st_ref[5:6, 0:LANES] += _colsum(dg_all * g_all)
        st_ref[6:7, 0:LANES] += _colsum(dxg)
        dbl = dbeta_all * beta_all * (1.0 - beta_all)
        dba_ref[...] = jnp.where(lane < NH, dbl, jnp.where(lane < 2 * NH, dxg, 0.0)).astype(BF16)

        dconv = dcbuf[0:CL, :]
        dx = w_ref[0:1, :] * dcbuf[KS - 1:KS - 1 + CL, :]
        st_ref[0:1, :] += _colsum(dconv * xbuf[SH - KS + 1:SH - KS + 1 + CL, :])
        for k in range(1, KS):
            off = SH - (KS - 1) + k
            st_ref[k:k + 1, :] += _colsum(dconv * xbuf[off:off + CL, :])
            dx = dx + w_ref[k:k + 1, :] * dcbuf[KS - 1 - k:KS - 1 - k + CL, :]
        dcbuf[CL:CL + SH, :] = dcbuf[0:SH, :]
        dp_ref[:, 0:3 * GW] = dx.astype(BF16)

    rev = lambda w, j=0: pl.BlockSpec((CL, w), lambda n: (NCH - 1 - n, j))
    halo = lambda j: pl.BlockSpec((SH, GW), lambda n: (jnp.maximum((NCH - 1 - n) * (CL // SH) - 1, 0), j))
    blk4 = lambda a, b: pl.BlockSpec((1, NH, a, b), lambda n: (NCH - 1 - n, 0, 0, 0))
    return pl.pallas_call(
        body, name="gdn_bwd", grid=(NCH,),
        in_specs=[rev(GW), rev(GW), blk4(DH, DH), blk4(CL, CL), rev(GW, 2), rev(GW, 3), rev(GW, 4), rev(GW, 5),
                  halo(2), halo(3), halo(4), rev(LANES), _const((KS, 3 * GW)), _const((1, LANES)),
                  _const((1, LANES)), _const((1, DH))],
        out_specs=(rev(4 * GW), rev(LANES), _const((GDN_STATS, 3 * GW))),
        out_shape=(jax.ShapeDtypeStruct((S, 4 * GW), BF16), jax.ShapeDtypeStruct((S, LANES), BF16),
                   jax.ShapeDtypeStruct((GDN_STATS, 3 * GW), F32)),
        scratch_shapes=[pltpu.VMEM((SH + CL, 3 * GW), F32), pltpu.VMEM((CL + SH, 3 * GW), F32),
                        pltpu.VMEM((NH, DH, DH), F32)],
        compiler_params=_params(dimension_semantics=("arbitrary",)),
    )(d_out_b, o_pre, s_in, t_inv, p_main, p_main, p_main, p_main, p_main, p_main, p_main, p_ba,
      gdn_conv_w, alog_l, dt_l, gdn_nw)


def _bwd_in(dp_conf, dp_gdn, dp_ba, x, dx1, nw1, modnb, bada, w_main, w_ba):
    def body(dc_ref, dg_ref, db_ref, x_ref, dx1_ref, nw_ref, mod_ref, b_ref, wm_ref, wb_ref, gx_ref, st_ref):
        i = pl.program_id(0)

        @pl.when(i == 0)
        def _():
            st_ref[...] = jnp.zeros((8, D), F32)

        dh = (_dot(dc_ref[...], wm_ref[0:2 * CW, :]) + _dot(dg_ref[...], wm_ref[2 * CW:NMAIN, :])
              + _dot(db_ref[...], wb_ref[...]))
        xv = x_ref[...]
        r = lax.rsqrt(jnp.mean(xv * xv, axis=-1, keepdims=True) + EPS)
        xr = xv * r
        st_ref[0:1, :] += _colsum(dh)
        st_ref[1:2, :] += _colsum(dh * (xr * nw_ref[...]))
        dxn = dh * (1.0 + _mod(mod_ref, b_ref, 1))
        st_ref[2:3, :] += _colsum(dxn * xr)
        dxr = dxn * nw_ref[...]
        gx_ref[...] = dx1_ref[...] + r * (dxr - xr * jnp.mean(dxr * xr, axis=-1, keepdims=True))

    tile = lambda w: pl.BlockSpec((TM, w), lambda i: (i, 0))
    return pl.pallas_call(
        body, name="bwd_in", grid=(NT,),
        in_specs=[tile(2 * CW), tile(4 * GW), tile(LANES), tile(D), tile(D), _const((1, D)), _const((1, 6 * D)),
                  _const((1, 6 * D)), _const((NMAIN, D)), _const((LANES, D))],
        out_specs=(tile(D), _const((8, D))),
        out_shape=(jax.ShapeDtypeStruct((S, D), F32), jax.ShapeDtypeStruct((8, D), F32)),
        compiler_params=_params(dimension_semantics=("arbitrary",)),
    )(dp_conf, dp_gdn, dp_ba, x, dx1, nw1, modnb, bada, w_main, w_ba)


def _adamw(w, g, m, v):
    m = ADAM_B1 * m + (1.0 - ADAM_B1) * g
    v = ADAM_B2 * v + (1.0 - ADAM_B2) * (g * g)
    m_hat = m / BC1
    v_hat = v / BC2
    delta = -ADAM_LR * (m_hat / (jnp.sqrt(v_hat) + ADAM_EPS) + ADAM_WD * w)
    return delta, m, v


ADAM_BLOCK_BYTES = 6 * 1024 * 1024


def _adam_tile(rows, cols):
    padded = -(-cols // LANES) * LANES
    if N_DEV * rows * padded * 4 <= ADAM_BLOCK_BYTES:
        return rows, cols
    best = None
    for tr in range(16, rows, 16):
        if rows % tr == 0 and N_DEV * tr * padded * 4 <= ADAM_BLOCK_BYTES:
            best = tr
    if best is not None:
        return best, cols
    rows_padded = -(-rows // 16) * 16
    tc = LANES
    for cand in range(LANES, cols, LANES):
        if cols % cand == 0 and N_DEV * rows_padded * cand * 4 <= ADAM_BLOCK_BYTES:
            tc = cand
    return rows, tc


def _reduce_adam(name, parts, w, m, v, own=None):
    rows, cols = w.shape
    tr, tc = _adam_tile(rows, cols)

    def body(*refs):
        p_ref, w_ref, m_ref, v_ref = refs[:4]
        g_ref, d_ref, nm_ref, nv_ref = refs[-4:]
        if own is None:
            part = lambda j: p_ref[j].astype(F32)
        else:
            me = 4 * lax.axis_index("x") + 2 * lax.axis_index("y") + lax.axis_index("c")
            part = lambda j: jnp.where(me == j, refs[4][...], p_ref[j]).astype(F32)
        g = part(0)
        for j in range(1, N_DEV):
            g = g + part(j)
        g_ref[...] = g
        d_ref[...], nm_ref[...], nv_ref[...] = _adamw(w_ref[...], g, m_ref[...], v_ref[...])

    blk = pl.BlockSpec((tr, tc), lambda i, j: (i, j))
    sds = jax.ShapeDtypeStruct((rows, cols), F32)
    extra = [] if own is None else [own]
    return pl.pallas_call(
        body, name=name, grid=(rows // tr, cols // tc),
        in_specs=[pl.BlockSpec((N_DEV, tr, tc), lambda i, j: (0, i, j)), blk, blk, blk] + [blk] * len(extra),
        out_specs=(blk, blk, blk, blk), out_shape=(sds, sds, sds, sds),
        compiler_params=_params(dimension_semantics=("arbitrary", "arbitrary")),
    )(parts, w, m, v, *extra)


def _ada_adam(c_all, dmod_sh, w, m, v):
    rows, cols = w.shape
    tr = 256

    def body(c_ref, dm_ref, w_ref, m_ref, v_ref, g_ref, d_ref, nm_ref, nv_ref):
        cv = c_ref[...]
        g = _dot_tn(cv * _sig(cv), dm_ref[...], HI)
        g_ref[...] = g
        d_ref[...], nm_ref[...], nv_ref[...] = _adamw(w_ref[...], g, m_ref[...], v_ref[...])

    blk = pl.BlockSpec((tr, cols), lambda i: (i, 0))
    sds = jax.ShapeDtypeStruct((rows, cols), F32)
    return pl.pallas_call(
        body, name="ada_adam", grid=(rows // tr,),
        in_specs=[pl.BlockSpec((N_DEV, tr), lambda i: (0, i)), _const((N_DEV, cols)), blk, blk, blk],
        out_specs=(blk, blk, blk, blk), out_shape=(sds, sds, sds, sds),
        compiler_params=_params(dimension_semantics=("arbitrary",)),
    )(c_all, dmod_sh, w, m, v)


def _lanes(a, at=0):
    return jnp.pad(a, ((0, 0), (at, LANES - at - a.shape[1])))


WEIGHT_NAMES = ["w_ada", "b_ada", "norm_mix_w", "w_in", "conv_w", "conv_b", "conv_gn_w", "conv_gn_b", "gdn_conv_w",
                "gdn_a_log", "gdn_dt_bias", "gdn_norm_w", "w_out", "norm_ffn_w", "w_ffn_in", "w_ffn_out",
                "norm_final_w"]


def _slab(b_ada, norm_mix_w, norm_ffn_w, norm_final_w, conv_b, conv_gn_w, conv_gn_b, gdn_norm_w, a_log, dt_bias):
    return jnp.concatenate([
        b_ada.reshape(48, LANES), norm_mix_w.reshape(8, LANES), norm_ffn_w.reshape(8, LANES),
        norm_final_w.reshape(8, LANES), conv_b.reshape(4, LANES), conv_gn_w.reshape(4, LANES),
        conv_gn_b.reshape(4, LANES), gdn_norm_w.reshape(1, LANES), _lanes(a_log), _lanes(dt_bias),
        jnp.zeros((1, LANES), F32)], axis=0)


def _unslab(t):
    return dict(b_ada=t[0:48].reshape(1, 6 * D), norm_mix_w=t[48:56].reshape(1, D),
                norm_ffn_w=t[56:64].reshape(1, D), norm_final_w=t[64:72].reshape(D),
                conv_b=t[72:76].reshape(1, CW), conv_gn_w=t[76:80].reshape(1, CW),
                conv_gn_b=t[80:84].reshape(1, CW), gdn_norm_w=t[84:85], gdn_a_log=t[85:86, 0:NH],
                gdn_dt_bias=t[86:87, 0:NH])


def _mix_forward(w, xs, modnb):
    w_main = w["w_in"]
    w_ba = jnp.pad(w["w_in"][NMAIN:], ((0, LANES - 2 * NH), (0, 0)))
    alog_l = _lanes(w["gdn_a_log"], NH)
    dt_l = _lanes(w["gdn_dt_bias"], NH)
    p_main, p_ba, hb1 = _fwd_in(xs, w["norm_mix_w"], modnb, w["b_ada"], w_main, w_ba)
    y_conv, out_a = _conf_fwd(p_main, w["conv_w"], w["conv_b"], w["conv_gn_w"], w["conv_gn_b"])
    w_o, u_o, qg, kd, qk, cd, t_inv = _gdn_prep(p_main, p_ba, w["gdn_conv_w"], alog_l, dt_l)
    out_b, o_pre, s_in = _gdn_scan(w_o, u_o, qg, kd, qk, cd, p_main, w["gdn_norm_w"])
    return dict(w_main=w_main, w_ba=w_ba, alog_l=alog_l, dt_l=dt_l, p_main=p_main, p_ba=p_ba, hb1=hb1,
                y_conv=y_conv, out_a=out_a, out_b=out_b, o_pre=o_pre, s_in=s_in, t_inv=t_inv)


def _ffn_stage(w, f, xs, tgt, modnb):
    x1, mix, oab = _fwd_out(f["out_a"], f["out_b"], xs, modnb, w["b_ada"], w["w_out"])
    hb2, act, pre, dx2, dffn, st_fwd = _ffn_forward(x1, tgt, modnb, w["b_ada"], w["norm_ffn_w"],
                                                    w["norm_final_w"], w["w_ffn_in"], w["w_ffn_out"])
    gw_ffn_out = _grad_w_ffn_out(act, dffn)
    df, dx1, st_bwd = _ffn_backward(dffn, pre, x1, dx2, modnb, w["b_ada"], w["norm_ffn_w"], w["w_ffn_in"],
                                    w["w_ffn_out"])
    gw_ffn_in = _grad_w_ffn_in(hb2, df)
    return dict(mix=mix, oab=oab, dx1=dx1, st_ffn=st_fwd + st_bwd, gw_ffn_in=gw_ffn_in, gw_ffn_out=gw_ffn_out)


def _out_backward(w, g, modnb):
    dmix, d_out_a, d_out_b, st_out = _bwd_out(g["dx1"], g["mix"], modnb, w["b_ada"], w["w_out"])
    return dict(d_out_a=d_out_a, d_out_b=d_out_b, st_out=st_out, gw_out=_grad_w("grad_w_out", g["oab"], dmix, 512))


def _mix_backward(w, f, g, a, xs, modnb):
    d_out_a, d_out_b, st_out = a["d_out_a"], a["d_out_b"], a["st_out"]
    dp_conf, st_conf = _conf_bwd(d_out_a, f["y_conv"], f["p_main"], w["conv_w"], w["conv_gn_w"], w["conv_gn_b"])
    dp_gdn, dp_ba, st_gdn = _gdn_bwd(d_out_b, f["o_pre"], f["s_in"], f["t_inv"], f["p_main"], f["p_ba"],
                                     w["gdn_conv_w"], f["alog_l"], f["dt_l"], w["gdn_norm_w"])
    grad_x, st_in = _bwd_in(dp_conf, dp_gdn, dp_ba, xs, g["dx1"], w["norm_mix_w"], modnb, w["b_ada"], f["w_main"],
                            f["w_ba"])
    hb1 = f["hb1"]
    gw_in = jnp.concatenate(
        [_grad_w("grad_w_in_conf", dp_conf, hb1, 512), _grad_w("grad_w_in_gdn", dp_gdn, hb1, 512),
         _grad_w("grad_w_in_ba", dp_ba, hb1, LANES)[:2 * NH]], axis=0)
    st_ffn = g["st_ffn"]
    dmod = jnp.concatenate([st_in[0:1], st_in[1:2], st_out[0:1], st_ffn[2:3], st_ffn[3:4], st_ffn[1:2]], axis=1)
    small = jnp.concatenate([
        dmod.reshape(48, LANES), st_in[2:3].reshape(8, LANES), st_ffn[4:5].reshape(8, LANES),
        st_ffn[0:1].reshape(8, LANES), st_conf[31:32].reshape(4, LANES), st_conf[32:33].reshape(4, LANES),
        st_conf[33:34].reshape(4, LANES), st_gdn[4:5, 0:LANES],
        _lanes(st_gdn[5:6, NH:2 * NH]), _lanes(st_gdn[6:7, NH:2 * NH]), st_ffn[5:6, 0:LANES]], axis=0)
    return dict(grad_x=grad_x, gw_in=gw_in, gw_conv=st_conf[0:KC], gw_gconv=st_gdn[0:KS], small=small)


def _local(w, xs, tgt, modnb):
    f = _mix_forward(w, xs, modnb)
    g = _ffn_stage(w, f, xs, tgt, modnb)
    a = _out_backward(w, g, modnb)
    b = _mix_backward(w, f, g, a, xs, modnb)
    return dict(b, gw_out=a["gw_out"], gw_ffn_in=g["gw_ffn_in"], gw_ffn_out=g["gw_ffn_out"])


def kernel(x, c, w_ada, b_ada, norm_mix_w, w_in, conv_w, conv_b, conv_gn_w, conv_gn_b, gdn_conv_w, gdn_a_log, gdn_dt_bias, gdn_norm_w, w_out, norm_ffn_w, w_ffn_in, w_ffn_out, norm_final_w, loss_target, m_w_ada, m_b_ada, m_norm_mix_w, m_w_in, m_conv_w, m_conv_b, m_conv_gn_w, m_conv_gn_b, m_gdn_conv_w, m_gdn_a_log, m_gdn_dt_bias, m_gdn_norm_w, m_w_out, m_norm_ffn_w, m_w_ffn_in, m_w_ffn_out, m_norm_final_w, v_w_ada, v_b_ada, v_norm_mix_w, v_w_in, v_conv_w, v_conv_b, v_conv_gn_w, v_conv_gn_b, v_gdn_conv_w, v_gdn_a_log, v_gdn_dt_bias, v_gdn_norm_w, v_w_out, v_norm_ffn_w, v_w_ffn_in, v_w_ffn_out, v_norm_final_w):
    me = 4 * lax.axis_index("x") + 2 * lax.axis_index("y") + lax.axis_index("c")
    xs = x.reshape(S, D)
    tgt = loss_target.reshape(S, D)

    g_c, g_cw, g_gcw = _exchange("gather_cond", [c, conv_w[0], gdn_conv_w[0]], [False] * 3)
    c_all = g_c.reshape(N_DEV, D)
    g_mod, mod_token = _exchange("gather_mod", [_mod_shard(c_all, w_ada[0])], [False], with_token=True)
    modnb = lax.dynamic_index_in_dim(g_mod, me, axis=1, keepdims=False).reshape(1, 6 * D)

    late = [w_out[0].astype(BF16), jnp.transpose(w_ffn_in[0]).astype(BF16), w_ffn_out[0].astype(BF16)]
    g_win, *late_lands = _gather_two_level(
        "gather_weights", [_after(jnp.transpose(w_in[0]), mod_token).astype(BF16)] + late, seed_only=(1, 2, 3))
    late_started = _exchange_start("gather_late_start", late, late_lands, [False] * 3, only=LEVEL_ONE)
    modnb = _after(modnb, late_started[-1])
    w = dict(b_ada=b_ada, norm_mix_w=norm_mix_w, conv_b=conv_b, conv_gn_w=conv_gn_w, conv_gn_b=conv_gn_b,
             gdn_a_log=gdn_a_log, gdn_dt_bias=gdn_dt_bias, gdn_norm_w=gdn_norm_w, norm_ffn_w=norm_ffn_w,
             norm_final_w=norm_final_w.reshape(1, D),
             conv_w=jnp.transpose(g_cw, (1, 0, 2)).reshape(KC, CW),
             gdn_conv_w=jnp.transpose(g_gcw, (1, 0, 2)).reshape(KS, 3 * GW),
             w_in=g_win.reshape(NIN, D))

    f = _mix_forward(w, xs, modnb)
    _, late_landed = _exchange_wait("gather_late_wait", late_started, [False] * 3, (f["out_a"], f["out_b"]),
                                    only=LEVEL_ONE)
    g_wout, g_wfi, g_wfo = _relay_to_sibling("gather_late_relay", late_landed)
    w.update(w_out=g_wout.reshape(D, D), w_ffn_in=g_wfi, w_ffn_out=g_wfo.reshape(4, FB, D))
    g = _ffn_stage(w, f, xs, tgt, modnb)

    ffn_grads = [g["gw_ffn_in"], g["gw_ffn_out"].reshape(N_DEV, DFF // N_DEV, D)]
    ffn_started = _exchange_start("scatter_ffn_start", ffn_grads,
                                  [lax.empty(a.shape, a.dtype) for a in ffn_grads], [True] * 2)
    a = _out_backward(w, g, _after(modnb, ffn_started[-1]))
    out_grads = [a["gw_out"].reshape(N_DEV, D // N_DEV, D)]
    out_started = _exchange_start("scatter_out_start", out_grads,
                                  [lax.empty(t.shape, t.dtype) for t in out_grads], [True])
    loc = _mix_backward(dict(w, conv_gn_w=_after(w["conv_gn_w"], out_started[-1])), f, g, a, xs, modnb)

    gw_in = loc["gw_in"].reshape(N_DEV, NIN // N_DEV, D)
    ready = (gw_in[0:1, 0, 0:1] != gw_in[0:1, 0, 0:1]).astype(F32)
    g_small, small_token = _exchange("gather_small", [loc["small"] + ready], [False], with_token=True)

    in_grads = [gw_in,
                _after(jnp.transpose(loc["gw_conv"].reshape(KC, N_DEV, CW // N_DEV), (1, 0, 2)), small_token),
                jnp.transpose(loc["gw_gconv"].reshape(KS, N_DEV, 3 * GW // N_DEV), (1, 0, 2))]
    in_started = _exchange_start("scatter_in_start", in_grads,
                                 [lax.empty(t.shape, t.dtype) for t in in_grads], [True] * 3)
    g_small = _after(g_small, in_started[-1])
    sw = _slab(b_ada, norm_mix_w, norm_ffn_w, norm_final_w, conv_b, conv_gn_w, conv_gn_b, gdn_norm_w, gdn_a_log,
               gdn_dt_bias)
    sm = _slab(m_b_ada, m_norm_mix_w, m_norm_ffn_w, m_norm_final_w, m_conv_b, m_conv_gn_w, m_conv_gn_b,
               m_gdn_norm_w, m_gdn_a_log, m_gdn_dt_bias)
    sv = _slab(v_b_ada, v_norm_mix_w, v_norm_ffn_w, v_norm_final_w, v_conv_b, v_conv_gn_w, v_conv_gn_b,
               v_gdn_norm_w, v_gdn_a_log, v_gdn_dt_bias)
    small_out = _reduce_adam("adam_small", g_small, sw, sm, sv)
    loss = small_out[0][SMALL_ROWS - 1, 0]
    res = [_unslab(t) for t in small_out]

    dmod_rows = g_small[:, 0:48, :].reshape(N_DEV, 6 * D)
    dmod_sh = lax.dynamic_slice_in_dim(dmod_rows, me * (6 * D // N_DEV), 6 * D // N_DEV, axis=1)

    def own(sent):
        return lax.dynamic_index_in_dim(sent, me, axis=0, keepdims=False)

    big = dict(w_ada=_ada_adam(c_all, dmod_sh, w_ada[0], m_w_ada[0], v_w_ada[0]))
    (sent_fi, sent_fo), (r_fi, r_fo) = _exchange_wait("scatter_ffn_wait", ffn_started, [True] * 2,
                                                         (big["w_ada"][0],))
    big["w_ffn_in"] = [jnp.transpose(t) for t in _reduce_adam(
        "adam_w_ffn_in", r_fi, jnp.transpose(w_ffn_in[0]), jnp.transpose(m_w_ffn_in[0]),
        jnp.transpose(v_w_ffn_in[0]), own(sent_fi))]
    big["w_ffn_out"] = _reduce_adam("adam_w_ffn_out", r_fo, w_ffn_out[0], m_w_ffn_out[0], v_w_ffn_out[0],
                                    own(sent_fo))
    (sent_out,), (r_out,) = _exchange_wait("scatter_out_wait", out_started, [True], (big["w_ffn_out"][0],))
    big["w_out"] = _reduce_adam("adam_w_out", r_out, w_out[0], m_w_out[0], v_w_out[0], own(sent_out))
    (sent_in, sent_cw, sent_gcw), (r_in, r_cw, r_gcw) = _exchange_wait(
        "scatter_in_wait", in_started, [True] * 3, (big["w_out"][0],))
    big["w_in"] = [jnp.transpose(t) for t in _reduce_adam(
        "adam_w_in", r_in, jnp.transpose(w_in[0]), jnp.transpose(m_w_in[0]), jnp.transpose(v_w_in[0]),
        own(sent_in))]
    big["conv_w"] = _reduce_adam("adam_conv_w", r_cw, conv_w[0], m_conv_w[0], v_conv_w[0], own(sent_cw))
    big["gdn_conv_w"] = _reduce_adam("adam_gdn_conv_w", r_gcw, gdn_conv_w[0], m_gdn_conv_w[0], v_gdn_conv_w[0],
                                     own(sent_gcw))
    outs = [loss, loc["grad_x"].reshape(1, S, D)]
    for kind in range(4):
        for nm in WEIGHT_NAMES:
            outs.append(big[nm][kind][None] if nm in big else res[kind][nm])
    return tuple(outs)
```

```python
import functools

import jax
import jax.numpy as jnp
from jax import lax
from jax.experimental import pallas as pl
from jax.experimental.pallas import tpu as pltpu

F32 = jnp.float32
BF16 = jnp.bfloat16
HI = lax.Precision.HIGHEST
MESH = pl.DeviceIdType.MESH

N_DEV = 8
S = 2048
D = 1024
TM = 256
NT = S // TM
CW = 512
KC = 31
NG = 8
GSZ = CW // NG
HALO = 32
GW = 512
NH = 4
DH = 128
KS = 4
SH = 8
CL = 64
NCH = S // CL
NMAIN = 2 * CW + 4 * GW
NIN = NMAIN + 2 * NH
DFF = 2816
FB = DFF // 4
EPS = 1e-6
QSCALE = DH ** -0.5
LANES = 128
SMALL_ROWS = 88

ADAM_LR = 0.001
ADAM_B1 = 0.9
ADAM_B2 = 0.999
ADAM_EPS = 1e-08
ADAM_WD = 0.01
ADAM_STEP = 10
BC1 = 1.0 - ADAM_B1 ** ADAM_STEP
BC2 = 1.0 - ADAM_B2 ** ADAM_STEP

MIB = 1024 * 1024
VMEM_LIMIT_MIB = 32


def _params(limit_mib=VMEM_LIMIT_MIB, **kw):
    return pltpu.CompilerParams(vmem_limit_bytes=limit_mib * MIB, **kw)


def _sig(x):
    return jax.nn.sigmoid(x)


GP = BF16


def _operands(a, b, prec):
    if prec is BF16:
        return a.astype(BF16), b.astype(BF16), None
    return a, b, prec


def _dot(a, b, prec=None):
    a, b, prec = _operands(a, b, prec)
    return jnp.dot(a, b, preferred_element_type=F32, precision=prec)


def _dot_nt(a, b, prec=None):
    a, b, prec = _operands(a, b, prec)
    return lax.dot_general(a, b, (((1,), (1,)), ((), ())), preferred_element_type=F32, precision=prec)


def _dot_tn(a, b, prec=None):
    a, b, prec = _operands(a, b, prec)
    return lax.dot_general(a, b, (((0,), (0,)), ((), ())), preferred_element_type=F32, precision=prec)


def _lockstep(gens):
    gens = list(gens)
    while gens:
        alive = []
        for g in gens:
            try:
                next(g)
                alive.append(g)
            except StopIteration:
                pass
        gens = alive


def _rowsum(x):
    return jnp.sum(x, axis=-1, keepdims=True)


def _colsum(x):
    return jnp.sum(x, axis=0, keepdims=True)


def _mod(mod_ref, b_ref, k):
    return mod_ref[:, k * D:(k + 1) * D] + b_ref[:, k * D:(k + 1) * D]


def _pallas(body, *, out_shape, **kw):
    single = isinstance(out_shape, jax.ShapeDtypeStruct)
    shapes = [out_shape] if single else list(out_shape)
    shapes = [pltpu.HBM(s.shape, s.dtype) for s in shapes]
    call = pl.pallas_call(body, out_shape=shapes[0] if single else tuple(shapes), **kw)
    return lambda *args: call(*[pltpu.with_memory_space_constraint(a, pltpu.HBM) for a in args])


def _const(shape):
    nd = len(shape)
    return pl.BlockSpec(shape, lambda *_: (0,) * nd)


def _const1(shape):
    nd = len(shape)
    return pl.BlockSpec(shape, lambda *_: (0,) * nd, pipeline_mode=pl.Buffered(1))


PEER_FLIPS = [(dx, dy, dc) for dx in (0, 1) for dy in (0, 1) for dc in (0, 1)][1:]


def _after(x, token):
    return x + token[0:1, 0:1].astype(x.dtype).reshape((1,) * x.ndim)


def _exchange(name, srcs, per_dest, seed_only=(), with_token=False):
    n = len(srcs)
    out_shape = []
    for a, pd in zip(srcs, per_dest):
        blk = a.shape[1:] if pd else a.shape
        out_shape.append(jax.ShapeDtypeStruct((N_DEV,) + tuple(blk), a.dtype))

    def body(*refs):
        src = refs[:n]
        dst = refs[n:2 * n]
        send_sems, recv_sems, local_sems = refs[-3:]
        if with_token:
            refs[2 * n][...] = jnp.zeros((8, LANES), F32)
        x, y, c = lax.axis_index("x"), lax.axis_index("y"), lax.axis_index("c")
        me = 4 * x + 2 * y + c

        def piece(i, j):
            return src[i].at[j] if per_dest[i] else src[i]

        copies = []
        for k, (dx, dy, dc) in enumerate(PEER_FLIPS):
            px = 1 - x if dx else x
            py = 1 - y if dy else y
            pc = 1 - c if dc else c
            pj = 4 * px + 2 * py + pc
            for i in range(n):
                if i in seed_only:
                    continue
                cp = pltpu.make_async_remote_copy(
                    src_ref=piece(i, pj), dst_ref=dst[i].at[me],
                    send_sem=send_sems.at[k * n + i], recv_sem=recv_sems.at[k * n + i],
                    device_id=(px, py, pc), device_id_type=MESH)
                cp.start()
                arrive = pltpu.make_async_remote_copy(
                    src_ref=piece(i, pj), dst_ref=dst[i].at[pj],
                    send_sem=send_sems.at[k * n + i], recv_sem=recv_sems.at[k * n + i],
                    device_id=(px, py, pc), device_id_type=MESH)
                copies.append((cp, arrive))
        own = []
        for i in range(n):
            lc = pltpu.make_async_copy(piece(i, me), dst[i].at[me], local_sems.at[i])
            lc.start()
            own.append(lc)
        for cp, arrive in copies:
            arrive.wait_recv()
        for cp, arrive in copies:
            cp.wait_send()
        for lc in own:
            lc.wait()

    any_spec = pl.BlockSpec(memory_space=pl.ANY)
    out_specs = [any_spec] * n
    if with_token:
        out_shape.append(jax.ShapeDtypeStruct((8, LANES), F32))
        out_specs.append(pl.BlockSpec(memory_space=pltpu.VMEM))
    return pl.pallas_call(
        body, name=name, out_shape=tuple(out_shape),
        in_specs=[any_spec] * n, out_specs=tuple(out_specs),
        scratch_shapes=[pltpu.SemaphoreType.DMA((7 * n,)), pltpu.SemaphoreType.DMA((7 * n,)),
                        pltpu.SemaphoreType.DMA((n,))],
        compiler_params=pltpu.CompilerParams(has_side_effects=True),
    )(*srcs)


CHIP_FLIPS = [(0, 1), (1, 0), (1, 1)]
LEVEL_ONE = [k for k, (dx, dy, dc) in enumerate(PEER_FLIPS) if (dx, dy, dc) == (0, 0, 1) or dc == 0]


def _chip_peers(x, y):
    return [(1 - x if dx else x, 1 - y if dy else y) for dx, dy in CHIP_FLIPS]


def _gather_two_level(name, srcs, seed_only=()):
    n = len(srcs)
    live = [i for i in range(n) if i not in seed_only]

    def body(*refs):
        src, dst = refs[:n], refs[n:2 * n]
        send_sems, recv_sems, local_sems = refs[2 * n:2 * n + 3]
        bounce = refs[2 * n + 3:]
        x, y, c = lax.axis_index("x"), lax.axis_index("y"), lax.axis_index("c")
        me = 4 * x + 2 * y + c
        sibling = (x, y, 1 - c)
        chips = _chip_peers(x, y)

        def copy(k, i, src_ref, slot, to):
            return pltpu.make_async_remote_copy(
                src_ref=src_ref, dst_ref=dst[i].at[slot], send_sem=send_sems.at[k * n + i],
                recv_sem=recv_sems.at[k * n + i], device_id=to, device_id_type=MESH)

        first = []
        for i in live:
            first.append(copy(0, i, src[i], me, sibling))
            first += [copy(1 + j, i, src[i], me, (px, py, c)) for j, (px, py) in enumerate(chips)]
        for cp in first:
            cp.start()
        up = [pltpu.make_async_copy(src[i], bounce[i], local_sems.at[i]) for i in range(n)]
        for cp in up:
            cp.start()
        for cp in up:
            cp.wait()
        own = [pltpu.make_async_copy(bounce[i], dst[i].at[me], local_sems.at[i]) for i in range(n)]
        for cp in own:
            cp.start()
        passed = []
        for j, (px, py) in enumerate(chips):
            slot = 4 * px + 2 * py + c
            for i in live:
                copy(1 + j, i, src[i], slot, (px, py, c)).wait_recv()
                fwd = copy(4 + j, i, dst[i].at[slot], slot, sibling)
                fwd.start()
                passed.append(fwd)
        for i in live:
            copy(0, i, src[i], 4 * x + 2 * y + 1 - c, sibling).wait_recv()
            for j, (px, py) in enumerate(chips):
                copy(4 + j, i, src[i], 4 * px + 2 * py + 1 - c, sibling).wait_recv()
        for cp in first + passed:
            cp.wait_send()
        for cp in own:
            cp.wait()

    any_spec = pl.BlockSpec(memory_space=pl.ANY)
    return pl.pallas_call(
        body, name=name, out_shape=tuple(jax.ShapeDtypeStruct((N_DEV,) + a.shape, a.dtype) for a in srcs),
        in_specs=[any_spec] * n, out_specs=tuple([any_spec] * n),
        scratch_shapes=[pltpu.SemaphoreType.DMA((7 * n,)), pltpu.SemaphoreType.DMA((7 * n,)),
                        pltpu.SemaphoreType.DMA((n,))] + [pltpu.VMEM(a.shape, a.dtype) for a in srcs],
        compiler_params=pltpu.CompilerParams(has_side_effects=True),
    )(*srcs)


def _relay_to_sibling(name, lands):
    n = len(lands)

    def body(*refs):
        land = refs[n:2 * n]
        send_sems, recv_sems = refs[-2:]
        x, y, c = lax.axis_index("x"), lax.axis_index("y"), lax.axis_index("c")
        sibling = (x, y, 1 - c)
        sends = []
        for j, (px, py) in enumerate(_chip_peers(x, y)):
            slot = 4 * px + 2 * py + c
            for i in range(n):
                cp = pltpu.make_async_remote_copy(
                    src_ref=land[i].at[slot], dst_ref=land[i].at[slot], send_sem=send_sems.at[j * n + i],
                    recv_sem=recv_sems.at[j * n + i], device_id=sibling, device_id_type=MESH)
                cp.start()
                sends.append(cp)
        for j, (px, py) in enumerate(_chip_peers(x, y)):
            slot = 4 * px + 2 * py + 1 - c
            for i in range(n):
                pltpu.make_async_remote_copy(
                    src_ref=land[i].at[slot], dst_ref=land[i].at[slot], send_sem=send_sems.at[j * n + i],
                    recv_sem=recv_sems.at[j * n + i], device_id=sibling, device_id_type=MESH).wait_recv()
        for cp in sends:
            cp.wait_send()

    any_spec = pl.BlockSpec(memory_space=pl.ANY)
    return pl.pallas_call(
        body, name=name, out_shape=tuple(jax.ShapeDtypeStruct(a.shape, a.dtype) for a in lands),
        in_specs=[any_spec] * n, out_specs=tuple([any_spec] * n),
        input_output_aliases={i: i for i in range(n)},
        scratch_shapes=[pltpu.SemaphoreType.DMA((3 * n,)), pltpu.SemaphoreType.DMA((3 * n,))],
        compiler_params=pltpu.CompilerParams(has_side_effects=True),
    )(*lands)


HBM_SPEC = pl.BlockSpec(memory_space=pltpu.HBM)
SEM_SPEC = pl.BlockSpec(memory_space=pltpu.SEMAPHORE)
DATAFLOW = pltpu.SideEffectType.DATAFLOW_SIDE_EFFECTING


def _peers(only=None):
    x, y, c = lax.axis_index("x"), lax.axis_index("y"), lax.axis_index("c")
    out = []
    for k, (dx, dy, dc) in enumerate(PEER_FLIPS):
        if only is not None and k not in only:
            continue
        px = 1 - x if dx else x
        py = 1 - y if dy else y
        pc = 1 - c if dc else c
        out.append((k, (px, py, pc), 4 * px + 2 * py + pc))
    return 4 * x + 2 * y + c, out


def _exchange_start(name, srcs, lands, per_dest, only=None):
    n = len(srcs)

    def body(*refs):
        src, land = refs[:n], refs[n:2 * n]
        send_sems, recv_sems = refs[2 * n], refs[2 * n + 1]
        token = refs[-1]
        me, peers = _peers(only)
        for k, peer, pj in peers:
            for i in range(n):
                pltpu.make_async_remote_copy(
                    src_ref=src[i].at[pj] if per_dest[i] else src[i], dst_ref=land[i].at[me],
                    send_sem=send_sems.at[k * n + i], recv_sem=recv_sems.at[k * n + i],
                    device_id=peer, device_id_type=MESH).start()
        token[...] = jnp.zeros((8, LANES), F32)

    arrays = list(srcs) + list(lands)
    return pl.pallas_call(
        body, name=name,
        out_shape=(pltpu.SemaphoreType.DMA((7 * n,)), pltpu.SemaphoreType.DMA((7 * n,)),
                   *[pltpu.HBM(a.shape, a.dtype) for a in arrays], jax.ShapeDtypeStruct((8, LANES), F32)),
        in_specs=[HBM_SPEC] * (2 * n),
        out_specs=(SEM_SPEC, SEM_SPEC, *[HBM_SPEC] * (2 * n), pl.BlockSpec(memory_space=pltpu.VMEM)),
        input_output_aliases={i: 2 + i for i in range(2 * n)},
        compiler_params=pltpu.CompilerParams(has_side_effects=DATAFLOW),
    )(*[pltpu.with_memory_space_constraint(a, pltpu.HBM) for a in arrays])


def _exchange_wait(name, started, per_dest, after, only=None):
    n = (len(started) - 3) // 2
    send_sems, recv_sems = started[0], started[1]
    arrays = list(started[2:2 + 2 * n])

    def body(*refs):
        src, land = refs[:n], refs[n:2 * n]
        send, recv = refs[2 * n], refs[2 * n + 1]
        me, peers = _peers(only)
        for k, peer, pj in peers:
            for i in range(n):
                cp = pltpu.make_async_remote_copy(
                    src_ref=src[i].at[pj] if per_dest[i] else src[i], dst_ref=land[i].at[pj],
                    send_sem=send.at[k * n + i], recv_sem=recv.at[k * n + i],
                    device_id=peer, device_id_type=MESH)
                cp.wait_send()
                cp.wait_recv()

    outs = pl.pallas_call(
        body, name=name,
        out_shape=tuple(pltpu.HBM(a.shape, a.dtype) for a in arrays),
        in_specs=[HBM_SPEC] * (2 * n) + [SEM_SPEC, SEM_SPEC] + [pl.BlockSpec(memory_space=pl.ANY)] * len(after),
        out_specs=tuple([HBM_SPEC] * (2 * n)),
        input_output_aliases={i: i for i in range(2 * n)},
        compiler_params=pltpu.CompilerParams(has_side_effects=DATAFLOW),
    )(*arrays, send_sems, recv_sems, *after)
    return outs[:n], outs[n:]


def _mod_shard(c_all, w_ada):
    def body(c_ref, w_ref, o_ref):
        cv = c_ref[...]
        ca = cv * _sig(cv)
        o_ref[...] = _dot(ca.astype(BF16), w_ref[...].astype(BF16))

    return pl.pallas_call(
        body, name="mod_shard", out_shape=jax.ShapeDtypeStruct((N_DEV, w_ada.shape[1]), F32),
        compiler_params=_params(),
    )(c_all, w_ada)


def _fwd_in(x, nw1, modnb, bada, w_main, w_ba):
    def body(x_ref, nw_ref, mod_ref, b_ref, wm_ref, wb_ref, pm_ref, pb_ref, hb_ref):
        xv = x_ref[...]
        r = lax.rsqrt(jnp.mean(xv * xv, axis=-1, keepdims=True) + EPS)
        h = (xv * r * nw_ref[...]) * (1.0 + _mod(mod_ref, b_ref, 1)) + _mod(mod_ref, b_ref, 0)
        hb = h.astype(BF16)
        hb_ref[...] = hb
        pm_ref[...] = _dot_nt(hb, wm_ref[...])
        pb_ref[...] = _dot_nt(hb, wb_ref[...])

    return _pallas(
        body, name="fwd_in", grid=(NT,),
        in_specs=[pl.BlockSpec((TM, D), lambda i: (i, 0)), _const((1, D)), _const((1, 6 * D)), _const((1, 6 * D)),
                  _const((NMAIN, D)), _const((LANES, D))],
        out_specs=(pl.BlockSpec((TM, NMAIN), lambda i: (i, 0)), pl.BlockSpec((TM, LANES), lambda i: (i, 0)),
                   pl.BlockSpec((TM, D), lambda i: (i, 0))),
        out_shape=(jax.ShapeDtypeStruct((S, NMAIN), F32), jax.ShapeDtypeStruct((S, LANES), F32),
                   jax.ShapeDtypeStruct((S, D), BF16)),
        compiler_params=_params(dimension_semantics=("arbitrary",)),
    )(x, nw1, modnb, bada, w_main, w_ba)


def _group_mean_matrix():
    ii = lax.broadcasted_iota(jnp.int32, (CW, CW), 0) // GSZ
    jj = lax.broadcasted_iota(jnp.int32, (CW, CW), 1) // GSZ
    return jnp.where(ii == jj, 1.0 / GSZ, 0.0).astype(F32)


SUB = 8
SHIFT_ROWS = HALO + TM - SUB


def _fill_shifted(buf, sh):
    for b in range(1, SUB):
        sh[b - 1] = buf[b:b + SHIFT_ROWS, :]


def _rows_at(buf, sh, off):
    a, b = divmod(off, SUB)
    if b == 0:
        return buf[off:off + TM, :]
    return sh[b - 1, SUB * a:SUB * a + TM, :]


def _group_mean(x, pm):
    hi = x.astype(BF16)
    r1 = x - hi.astype(F32)
    mid = r1.astype(BF16)
    lo = (r1 - mid.astype(F32)).astype(BF16)
    return _dot(hi, pm) + _dot(mid, pm) + _dot(lo, pm)


def _conf_fwd(p_main, conv_w, conv_b, gn_w, gn_b):
    def body(a_ref, g_ref, w_ref, b_ref, gw_ref, gb_ref, y_ref, oa_ref, ubuf, ush):
        i = pl.program_id(0)

        @pl.when(i == 0)
        def _():
            ubuf[0:HALO, :] = jnp.zeros((HALO, CW), F32)

        ubuf[HALO:HALO + TM, :] = a_ref[...] * _sig(g_ref[...])
        _fill_shifted(ubuf, ush)
        acc = jnp.zeros((TM, CW), F32) + b_ref[...]
        for k in range(KC):
            acc = acc + w_ref[k:k + 1, :] * _rows_at(ubuf, ush, HALO - (KC - 1) + k)
        y_ref[...] = acc
        ubuf[0:HALO, :] = ubuf[TM:TM + HALO, :]
        pm = _group_mean_matrix().astype(BF16)
        dlt = acc - _group_mean(acc, pm)
        var = _group_mean(dlt * dlt, pm)
        o = dlt * lax.rsqrt(var + EPS) * gw_ref[...] + gb_ref[...]
        oa_ref[...] = o * _sig(o)

    return _pallas(
        body, name="conf_fwd", grid=(NT,),
        in_specs=[pl.BlockSpec((TM, CW), lambda i: (i, 0)), pl.BlockSpec((TM, CW), lambda i: (i, 1)),
                  _const((KC, CW)), _const((1, CW)), _const((1, CW)), _const((1, CW))],
        out_specs=(pl.BlockSpec((TM, CW), lambda i: (i, 0)), pl.BlockSpec((TM, CW), lambda i: (i, 0))),
        out_shape=(jax.ShapeDtypeStruct((S, CW), F32), jax.ShapeDtypeStruct((S, CW), F32)),
        scratch_shapes=[pltpu.VMEM((HALO + TM, CW), F32), pltpu.VMEM((SUB - 1, SHIFT_ROWS, CW), F32)],
        compiler_params=_params(dimension_semantics=("arbitrary",)),
    )(p_main, p_main, conv_w, conv_b, gn_w, gn_b)


def _tri_iota():
    ii = lax.broadcasted_iota(jnp.int32, (CL, CL), 0)
    jj = lax.broadcasted_iota(jnp.int32, (CL, CL), 1)
    return ii, jj


def _gdn_gates(ba, alog_l, dt_l):
    beta_all = _sig(ba)
    xg = ba + dt_l
    sp = jnp.maximum(xg, 0.0) + jnp.log(1.0 + jnp.exp(-jnp.abs(xg)))
    neg_a = -jnp.exp(alog_l)
    return beta_all, neg_a * sp, xg, neg_a


def _ones_dot(ones, x):
    hi = x.astype(BF16)
    r1 = x - hi.astype(F32)
    mid = r1.astype(BF16)
    lo = (r1 - mid.astype(F32)).astype(BF16)
    return _dot(ones, hi) + _dot(ones, mid) + _dot(ones, lo)


def _gdn_cumsum(g_all):
    ii, jj = _tri_iota()
    low = jnp.where(ii >= jj, 1.0, 0.0).astype(BF16)
    gcum = _ones_dot(low, g_all)
    return gcum, jnp.transpose(gcum)


def _split(x):
    hi = x.astype(BF16)
    return hi, (x - hi.astype(F32)).astype(BF16)


def _dot_split(a, b):
    (ah, al), (bh, bl) = a, b
    return _dot(ah, bh) + (_dot(ah, bl) + _dot(al, bh))


def _unit_lower_inverses(mats):
    ii, jj = _tri_iota()
    eye = jnp.where(ii == jj, 1.0, 0.0).astype(F32)
    ts = [eye - a for a in mats]
    ps = [_dot_split(s, s) for s in map(_split, mats)]
    for _ in range(4):
        sp = [_split(p) for p in ps]
        ts = [t + _dot_split(_split(t), s) for t, s in zip(ts, sp)]
        ps = [_dot_split(s, s) for s in sp]
    return [t + _dot_split(_split(t), _split(p)) for t, p in zip(ts, ps)]


def _head_terms(qh, kh, beta, gcol, grow):
    ii, jj = _tri_iota()
    causal = ii >= jj
    strict = ii > jj
    rq = lax.rsqrt(_rowsum(qh * qh) + EPS)
    rk = lax.rsqrt(_rowsum(kh * kh) + EPS)
    qn = qh * rq
    kn = kh * rk
    qs = qn * QSCALE
    decay = jnp.where(causal, jnp.exp(jnp.where(causal, gcol - grow, 0.0)), 0.0)
    gam = jnp.exp(gcol)
    gl = gcol[CL - 1:CL, :]
    kds = jnp.exp(gl - gcol)
    cd = jnp.exp(gl)
    kb = kn * beta
    a = jnp.where(strict, _dot_nt(kb, kn, GP) * decay, 0.0)
    qk = jnp.where(causal, _dot_nt(qs, kn, GP) * decay, 0.0)
    return dict(rq=rq, rk=rk, qn=qn, kn=kn, qs=qs, decay=decay, gam=gam, kds=kds, cd=cd, kb=kb, a=a, qk=qk,
                causal=causal, strict=strict)


def _short_conv(w_ref, buf, rows=CL):
    acc = w_ref[0:1, :] * buf[SH - KS + 1:SH - KS + 1 + rows, :]
    for k in range(1, KS):
        off = SH - (KS - 1) + k
        acc = acc + w_ref[k:k + 1, :] * buf[off:off + rows, :]
    return acc


CPS = 4
TG = CPS * CL


def _gdn_prep(p_main, p_ba, gdn_conv_w, alog_l, dt_l):
    def body(q_ref, k_ref, v_ref, qh_ref, kh_ref, vh_ref, ba_ref, w_ref, al_ref, dt_ref,
             wo_ref, uo_ref, qg_ref, kd_ref, qk_ref, cd_ref, t_ref, xbuf):
        i = pl.program_id(0)
        first = i == 0
        xbuf[0:SH, 0:GW] = jnp.where(first, 0.0, qh_ref[...])
        xbuf[0:SH, GW:2 * GW] = jnp.where(first, 0.0, kh_ref[...])
        xbuf[0:SH, 2 * GW:3 * GW] = jnp.where(first, 0.0, vh_ref[...])
        xbuf[SH:SH + TG, 0:GW] = q_ref[...]
        xbuf[SH:SH + TG, GW:2 * GW] = k_ref[...]
        xbuf[SH:SH + TG, 2 * GW:3 * GW] = v_ref[...]
        conv = _short_conv(w_ref, xbuf, TG)
        qkv = conv * _sig(conv)
        beta_all, g_all, _, _ = _gdn_gates(ba_ref[...], al_ref[...], dt_ref[...])
        lane = lax.broadcasted_iota(jnp.int32, (8, LANES), 1)
        cums = [_gdn_cumsum(g_all[cc * CL:(cc + 1) * CL, :]) for cc in range(CPS)]
        pairs = [(cc, h) for cc in range(CPS) for h in range(NH)]
        terms, vbs = [], []
        for cc, h in pairs:
            r0, lo = cc * CL, h * DH
            beta = beta_all[r0:r0 + CL, h:h + 1]
            gcum, gcum_t = cums[cc]
            terms.append(_head_terms(qkv[r0:r0 + CL, lo:lo + DH], qkv[r0:r0 + CL, GW + lo:GW + lo + DH], beta,
                                     gcum[:, NH + h:NH + h + 1], gcum_t[NH + h:NH + h + 1, :]))
            vbs.append(qkv[r0:r0 + CL, 2 * GW + lo:2 * GW + lo + DH] * beta)
        invs = _unit_lower_inverses([f["a"] for f in terms])
        cds = [jnp.zeros((8, LANES), F32) for _ in range(CPS)]
        for (cc, h), f, t, vb in zip(pairs, terms, invs, vbs):
            r0, lo = cc * CL, h * DH
            t_ref[cc, h] = t
            uo_ref[r0:r0 + CL, lo:lo + DH] = _dot(t, vb, GP)
            wo_ref[r0:r0 + CL, lo:lo + DH] = _dot(t, f["kb"] * f["gam"], GP).astype(BF16)
            qg_ref[r0:r0 + CL, lo:lo + DH] = (f["qs"] * f["gam"]).astype(BF16)
            kd_ref[r0:r0 + CL, lo:lo + DH] = (f["kn"] * f["kds"]).astype(BF16)
            qk_ref[cc, h] = f["qk"].astype(BF16)
            cds[cc] = cds[cc] + jnp.where(lane == h, f["cd"], 0.0)
        for cc in range(CPS):
            cd_ref[cc] = cds[cc]

    col = lambda j: pl.BlockSpec((TG, GW), lambda i: (i, j))
    halo = lambda j: pl.BlockSpec((SH, GW), lambda i: (jnp.maximum(i * (TG // SH) - 1, 0), j))
    tile = lambda: pl.BlockSpec((TG, GW), lambda i: (i, 0))
    sq = lambda: pl.BlockSpec((CPS, NH, CL, CL), lambda i: (i, 0, 0, 0))
    return _pallas(
        body, name="gdn_prep", grid=(NCH // CPS,),
        in_specs=[col(2), col(3), col(4), halo(2), halo(3), halo(4), pl.BlockSpec((TG, LANES), lambda i: (i, 0)),
                  _const((KS, 3 * GW)), _const((1, LANES)), _const((1, LANES))],
        out_specs=(tile(), tile(), tile(), tile(), sq(), pl.BlockSpec((CPS, 8, LANES), lambda i: (i, 0, 0)), sq()),
        out_shape=(jax.ShapeDtypeStruct((S, GW), BF16), jax.ShapeDtypeStruct((S, GW), F32),
                   jax.ShapeDtypeStruct((S, GW), BF16), jax.ShapeDtypeStruct((S, GW), BF16),
                   jax.ShapeDtypeStruct((NCH, NH, CL, CL), BF16), jax.ShapeDtypeStruct((NCH, 8, LANES), F32),
                   jax.ShapeDtypeStruct((NCH, NH, CL, CL), F32)),
        scratch_shapes=[pltpu.VMEM((SH + TG, 3 * GW), F32)],
        compiler_params=_params(dimension_semantics=("arbitrary",)),
    )(p_main, p_main, p_main, p_main, p_main, p_main, p_ba, gdn_conv_w, alog_l, dt_l)


def _gdn_scan(w_o, u_o, qg, kd, qk, cd, p_main, gdn_nw):
    def body(w_ref, u_ref, qg_ref, kd_ref, qk_ref, cd_ref, z_ref, nw_ref, ob_ref, o_ref, sin_ref, state):
        n = pl.program_id(0)

        @pl.when(n == 0)
        def _():
            state[...] = jnp.zeros((NH, DH, DH), F32)

        def head(h):
            lo = h * DH
            st = state[h]
            sin_ref[0, h] = st
            sb = st.astype(BF16)
            v_new = u_ref[:, lo:lo + DH] - _dot(w_ref[:, lo:lo + DH], sb)
            yield
            vb = v_new.astype(BF16)
            o = _dot(qg_ref[:, lo:lo + DH], sb) + _dot(qk_ref[0, h], vb)
            state[h] = st * cd_ref[0, 0:1, h:h + 1] + _dot_tn(kd_ref[:, lo:lo + DH], vb)
            yield
            o_ref[:, lo:lo + DH] = o
            r = lax.rsqrt(jnp.mean(o * o, axis=-1, keepdims=True) + EPS)
            zh = z_ref[:, lo:lo + DH]
            ob_ref[:, lo:lo + DH] = o * r * nw_ref[...] * (zh * _sig(zh))

        _lockstep(head(h) for h in range(NH))

    tile = lambda: pl.BlockSpec((CL, GW), lambda n: (n, 0))
    return _pallas(
        body, name="gdn_scan", grid=(NCH,),
        in_specs=[tile(), tile(), tile(), tile(), pl.BlockSpec((1, NH, CL, CL), lambda n: (n, 0, 0, 0)),
                  pl.BlockSpec((1, 8, LANES), lambda n: (n, 0, 0)), pl.BlockSpec((CL, GW), lambda n: (n, 5)),
                  _const((1, DH))],
        out_specs=(tile(), tile(), pl.BlockSpec((1, NH, DH, DH), lambda n: (n, 0, 0, 0))),
        out_shape=(jax.ShapeDtypeStruct((S, GW), F32), jax.ShapeDtypeStruct((S, GW), F32),
                   jax.ShapeDtypeStruct((NCH, NH, DH, DH), F32)),
        scratch_shapes=[pltpu.VMEM((NH, DH, DH), F32)],
        compiler_params=_params(dimension_semantics=("arbitrary",)),
    )(w_o, u_o, qg, kd, qk, cd, p_main, gdn_nw)


def _fwd_out(out_a, out_b, x, modnb, bada, w_out):
    def body(oa_ref, ob_ref, x_ref, mod_ref, b_ref, w_ref, x1_ref, mix_ref, oab_ref):
        oa = oa_ref[...].astype(BF16)
        ob = ob_ref[...].astype(BF16)
        oab_ref[:, 0:CW] = oa
        oab_ref[:, CW:D] = ob
        mix = _dot(oa, w_ref[0:CW, :]) + _dot(ob, w_ref[CW:D, :])
        mix_ref[...] = mix
        x1_ref[...] = x_ref[...] + _mod(mod_ref, b_ref, 2) * mix

    tile = lambda w: pl.BlockSpec((TM, w), lambda i: (i, 0))
    return _pallas(
        body, name="fwd_out", grid=(NT,),
        in_specs=[tile(CW), tile(GW), tile(D), _const((1, 6 * D)), _const((1, 6 * D)), _const((D, D))],
        out_specs=(tile(D), tile(D), tile(D)),
        out_shape=(jax.ShapeDtypeStruct((S, D), F32), jax.ShapeDtypeStruct((S, D), F32),
                   jax.ShapeDtypeStruct((S, D), BF16)),
        compiler_params=_params(dimension_semantics=("arbitrary",)),
    )(out_a, out_b, x, modnb, bada, w_out)


FFN_STATS = 8


def _ffn_forward(x1, tgt, modnb, bada, nw2, nfw, w_fi, w_fo):
    def body(x1_ref, tgt_ref, mod_ref, b_ref, nw2_ref, nfw_ref, wi_ref, wo_ref,
             hb_ref, act_ref, pre_ref, dx2_ref, dffn_ref, st_ref):
        i = pl.program_id(0)

        @pl.when(i == 0)
        def _():
            st_ref[...] = jnp.zeros((FFN_STATS, D), F32)

        sh2, sc2, gt2 = _mod(mod_ref, b_ref, 3), _mod(mod_ref, b_ref, 4), _mod(mod_ref, b_ref, 5)
        x1v = x1_ref[...]
        r2 = lax.rsqrt(jnp.mean(x1v * x1v, axis=-1, keepdims=True) + EPS)
        hb = ((x1v * r2 * nw2_ref[...]) * (1.0 + sc2) + sh2).astype(BF16)
        hb_ref[...] = hb
        ffn = jnp.zeros((TM, D), F32)
        for j in range(4):
            fgj = _dot_nt(hb, wi_ref[j])
            fuj = _dot_nt(hb, wi_ref[j + 4])
            pre_ref[j] = fgj.astype(BF16)
            pre_ref[j + 4] = fuj.astype(BF16)
            aj = (fgj * _sig(fgj) * fuj).astype(BF16)
            act_ref[j] = aj
            ffn = ffn + _dot(aj, wo_ref[j])
        x2 = x1v + gt2 * ffn
        r3 = lax.rsqrt(jnp.mean(x2 * x2, axis=-1, keepdims=True) + EPS)
        xr3 = x2 * r3
        err = xr3 * nfw_ref[...] - tgt_ref[...]
        loss = 0.5 * jnp.sum(jnp.mean(err * err, axis=-1, keepdims=True), axis=0, keepdims=True)
        dy = err * (1.0 / D)
        st_ref[0:1, :] += _colsum(dy * xr3)
        dyr = dy * nfw_ref[...]
        dx2 = r3 * (dyr - xr3 * jnp.mean(dyr * xr3, axis=-1, keepdims=True))
        st_ref[1:2, :] += _colsum(dx2 * ffn)
        st_ref[5:6, :] += jnp.broadcast_to(loss, (1, D))
        dx2_ref[...] = dx2
        dffn_ref[...] = (gt2 * dx2).astype(BF16)

    tile = lambda w: pl.BlockSpec((TM, w), lambda i: (i, 0))
    return _pallas(
        body, name="ffn_forward", grid=(NT,),
        in_specs=[tile(D), tile(D), _const((1, 6 * D)), _const((1, 6 * D)), _const((1, D)), _const((1, D)),
                  _const1((N_DEV, FB, D)), _const1((4, FB, D))],
        out_specs=(tile(D), pl.BlockSpec((4, TM, FB), lambda i: (0, i, 0)),
                   pl.BlockSpec((N_DEV, TM, FB), lambda i: (0, i, 0)), tile(D), tile(D), _const((FFN_STATS, D))),
        out_shape=(jax.ShapeDtypeStruct((S, D), BF16), jax.ShapeDtypeStruct((4, S, FB), BF16),
                   jax.ShapeDtypeStruct((N_DEV, S, FB), BF16), jax.ShapeDtypeStruct((S, D), F32),
                   jax.ShapeDtypeStruct((S, D), BF16), jax.ShapeDtypeStruct((FFN_STATS, D), F32)),
        compiler_params=_params(42, dimension_semantics=("arbitrary",)),
    )(x1, tgt, modnb, bada, nw2, nfw, w_fi, w_fo)


def _ffn_backward(dffn, pre, x1, dx2, modnb, bada, nw2, w_fi, w_fo):
    def body(dffn_ref, pre_ref, x1_ref, dx2_ref, mod_ref, b_ref, nw2_ref, wi_ref, wo_ref, df_ref, dx1_ref, st_ref):
        i = pl.program_id(0)

        @pl.when(i == 0)
        def _():
            st_ref[...] = jnp.zeros((FFN_STATS, D), F32)

        dffn = dffn_ref[...]
        dh = jnp.zeros((TM, D), F32)
        for j in range(4):
            fg = pre_ref[j].astype(F32)
            fu = pre_ref[j + 4].astype(F32)
            sg = _sig(fg)
            dact = _dot_nt(dffn, wo_ref[j])
            dfg = (dact * fu * (sg * (1.0 + fg * (1.0 - sg)))).astype(BF16)
            dfu = (dact * (fg * sg)).astype(BF16)
            df_ref[j] = dfg
            df_ref[j + 4] = dfu
            dh = dh + _dot(dfg, wi_ref[j]) + _dot(dfu, wi_ref[j + 4])
        x1v = x1_ref[...]
        r2 = lax.rsqrt(jnp.mean(x1v * x1v, axis=-1, keepdims=True) + EPS)
        xr2 = x1v * r2
        st_ref[2:3, :] += _colsum(dh)
        st_ref[3:4, :] += _colsum(dh * (xr2 * nw2_ref[...]))
        dxn = dh * (1.0 + _mod(mod_ref, b_ref, 4))
        st_ref[4:5, :] += _colsum(dxn * xr2)
        dxr = dxn * nw2_ref[...]
        dx1_ref[...] = dx2_ref[...] + r2 * (dxr - xr2 * jnp.mean(dxr * xr2, axis=-1, keepdims=True))

    tile = lambda w: pl.BlockSpec((TM, w), lambda i: (i, 0))
    wide = lambda: pl.BlockSpec((N_DEV, TM, FB), lambda i: (0, i, 0))
    return _pallas(
        body, name="ffn_backward", grid=(NT,),
        in_specs=[tile(D), wide(), tile(D), tile(D), _const((1, 6 * D)), _const((1, 6 * D)), _const((1, D)),
                  _const1((N_DEV, FB, D)), _const1((4, FB, D))],
        out_specs=(wide(), tile(D), _const((FFN_STATS, D))),
        out_shape=(jax.ShapeDtypeStruct((N_DEV, S, FB), BF16), jax.ShapeDtypeStruct((S, D), F32),
                   jax.ShapeDtypeStruct((FFN_STATS, D), F32)),
        compiler_params=_params(44, dimension_semantics=("arbitrary",)),
    )(dffn, pre, x1, dx2, modnb, bada, nw2, w_fi, w_fo)


def _grad_w(name, a, b, nb):
    m, n = a.shape[1], b.shape[1]

    def body(a_ref, b_ref, o_ref):
        o_ref[...] = _dot_tn(a_ref[...], b_ref[...]).astype(BF16)

    return _pallas(
        body, name=name, grid=(m // nb,),
        in_specs=[pl.BlockSpec((S, nb), lambda j: (0, j)), _const((S, n))],
        out_specs=pl.BlockSpec((nb, n), lambda j: (j, 0)),
        out_shape=jax.ShapeDtypeStruct((m, n), BF16),
        compiler_params=_params(dimension_semantics=("arbitrary",)),
    )(a, b)


def _grad_w_ffn_in(hb2, df):
    def body(a_ref, b_ref, o_ref):
        o_ref[0] = _dot_tn(b_ref[0], a_ref[...]).astype(BF16)

    return _pallas(
        body, name="grad_w_ffn_in", grid=(N_DEV,),
        in_specs=[_const((S, D)), pl.BlockSpec((1, S, FB), lambda j: (j, 0, 0))],
        out_specs=pl.BlockSpec((1, FB, D), lambda j: (j, 0, 0)),
        out_shape=jax.ShapeDtypeStruct((N_DEV, FB, D), BF16),
        compiler_params=_params(dimension_semantics=("arbitrary",)),
    )(hb2, df)


def _grad_w_ffn_out(act, dffn):
    def body(a_ref, b_ref, o_ref):
        o_ref[0] = _dot_tn(a_ref[0], b_ref[...]).astype(BF16)

    return _pallas(
        body, name="grad_w_ffn_out", grid=(4,),
        in_specs=[pl.BlockSpec((1, S, FB), lambda j: (j, 0, 0)), _const((S, D))],
        out_specs=pl.BlockSpec((1, FB, D), lambda j: (j, 0, 0)),
        out_shape=jax.ShapeDtypeStruct((4, FB, D), BF16),
        compiler_params=_params(dimension_semantics=("arbitrary",)),
    )(act, dffn)


def _bwd_out(dx1, mix, modnb, bada, w_out):
    def body(dx_ref, mix_ref, mod_ref, b_ref, w_ref, dmix_ref, doa_ref, dob_ref, st_ref):
        i = pl.program_id(0)

        @pl.when(i == 0)
        def _():
            st_ref[...] = jnp.zeros((8, D), F32)

        dx = dx_ref[...]
        st_ref[0:1, :] += _colsum(dx * mix_ref[...])
        dmix = (_mod(mod_ref, b_ref, 2) * dx).astype(BF16)
        dmix_ref[...] = dmix
        doa_ref[...] = _dot_nt(dmix, w_ref[0:CW, :])
        dob_ref[...] = _dot_nt(dmix, w_ref[CW:D, :])

    tile = lambda w: pl.BlockSpec((TM, w), lambda i: (i, 0))
    return _pallas(
        body, name="bwd_out", grid=(NT,),
        in_specs=[tile(D), tile(D), _const((1, 6 * D)), _const((1, 6 * D)), _const((D, D))],
        out_specs=(tile(D), tile(CW), tile(GW), _const((8, D))),
        out_shape=(jax.ShapeDtypeStruct((S, D), BF16), jax.ShapeDtypeStruct((S, CW), F32),
                   jax.ShapeDtypeStruct((S, GW), F32), jax.ShapeDtypeStruct((8, D), F32)),
        compiler_params=_params(dimension_semantics=("arbitrary",)),
    )(dx1, mix, modnb, bada, w_out)


CONF_STATS = 40


def _conf_bwd(d_out_a, y, p_main, conv_w, gn_w, gn_b):
    def body(do_ref, y_ref, a_ref, g_ref, ah_ref, gh_ref, w_ref, gw_ref, gb_ref, dp_ref, st_ref,
             ubuf, dybuf, ush, dysh):
        i = pl.program_id(0)

        @pl.when(i == 0)
        def _():
            st_ref[...] = jnp.zeros((CONF_STATS, CW), F32)
            dybuf[TM:TM + HALO, :] = jnp.zeros((HALO, CW), F32)

        pm = _group_mean_matrix().astype(BF16)
        yv = y_ref[...]
        dlt = yv - _group_mean(yv, pm)
        rstd = lax.rsqrt(_group_mean(dlt * dlt, pm) + EPS)
        un = dlt * rstd
        o = un * gw_ref[...] + gb_ref[...]
        so = _sig(o)
        d_o = do_ref[...] * (so * (1.0 + o * (1.0 - so)))
        st_ref[33:34, :] += _colsum(d_o)
        st_ref[32:33, :] += _colsum(d_o * un)
        dun = d_o * gw_ref[...]
        dy = rstd * (dun - _group_mean(dun, pm) - un * _group_mean(dun * un, pm))
        st_ref[31:32, :] += _colsum(dy)
        dybuf[0:TM, :] = dy
        _fill_shifted(dybuf, dysh)

        a = a_ref[...]
        sg = _sig(g_ref[...])
        first = i == NT - 1
        ubuf[0:HALO, :] = jnp.where(first, 0.0, ah_ref[...] * _sig(gh_ref[...]))
        ubuf[HALO:HALO + TM, :] = a * sg
        _fill_shifted(ubuf, ush)
        du = jnp.zeros((TM, CW), F32)
        for k in range(KC):
            st_ref[k:k + 1, :] += _colsum(dy * _rows_at(ubuf, ush, HALO - (KC - 1) + k))
            du = du + w_ref[k:k + 1, :] * _rows_at(dybuf, dysh, KC - 1 - k)
        dybuf[TM:TM + HALO, :] = dybuf[0:HALO, :]
        dp_ref[:, 0:CW] = (du * sg).astype(BF16)
        dp_ref[:, CW:2 * CW] = (du * a * sg * (1.0 - sg)).astype(BF16)

    rev = lambda w, j=0: pl.BlockSpec((TM, w), lambda i: (NT - 1 - i, j))
    halo = lambda j: pl.BlockSpec((HALO, CW), lambda i: (jnp.maximum((NT - 1 - i) * (TM // HALO) - 1, 0), j))
    return _pallas(
        body, name="conf_bwd", grid=(NT,),
        in_specs=[rev(CW), rev(CW), rev(CW, 0), rev(CW, 1), halo(0), halo(1),
                  _const((KC, CW)), _const((1, CW)), _const((1, CW))],
        out_specs=(rev(2 * CW), _const((CONF_STATS, CW))),
        out_shape=(jax.ShapeDtypeStruct((S, 2 * CW), BF16), jax.ShapeDtypeStruct((CONF_STATS, CW), F32)),
        scratch_shapes=[pltpu.VMEM((HALO + TM, CW), F32), pltpu.VMEM((TM + HALO, CW), F32),
                        pltpu.VMEM((SUB - 1, SHIFT_ROWS, CW), F32), pltpu.VMEM((SUB - 1, SHIFT_ROWS, CW), F32)],
        compiler_params=_params(dimension_semantics=("arbitrary",)),
    )(d_out_a, y, p_main, p_main, p_main, p_main, conv_w, gn_w, gn_b)


GDN_STATS = 8


def _gdn_bwd(d_out_b, o_pre, s_in, t_inv, p_main, p_ba, gdn_conv_w, alog_l, dt_l, gdn_nw):
    def body(dob_ref, o_ref, sin_ref, t_ref, q_ref, k_ref, v_ref, z_ref, qh_ref, kh_ref, vh_ref, ba_ref,
             w_ref, al_ref, dt_ref, nw_ref, dp_ref, dba_ref, st_ref, xbuf, dcbuf, dstate):
        n = pl.program_id(0)

        @pl.when(n == 0)
        def _():
            st_ref[...] = jnp.zeros((GDN_STATS, 3 * GW), F32)
            dcbuf[CL:CL + SH, :] = jnp.zeros((SH, 3 * GW), F32)
            dstate[...] = jnp.zeros((NH, DH, DH), F32)

        first = n == NCH - 1
        xbuf[0:SH, 0:GW] = jnp.where(first, 0.0, qh_ref[...])
        xbuf[0:SH, GW:2 * GW] = jnp.where(first, 0.0, kh_ref[...])
        xbuf[0:SH, 2 * GW:3 * GW] = jnp.where(first, 0.0, vh_ref[...])
        xbuf[SH:SH + CL, 0:GW] = q_ref[...]
        xbuf[SH:SH + CL, GW:2 * GW] = k_ref[...]
        xbuf[SH:SH + CL, 2 * GW:3 * GW] = v_ref[...]
        conv = _short_conv(w_ref, xbuf)
        sc = _sig(conv)
        qkv = conv * sc
        ba = ba_ref[...]
        beta_all, g_all, xg, neg_a = _gdn_gates(ba, al_ref[...], dt_ref[...])
        gcum, gcum_t = _gdn_cumsum(g_all)
        lane = lax.broadcasted_iota(jnp.int32, (CL, LANES), 1)
        row = lax.broadcasted_iota(jnp.int32, (CL, 1), 0)
        acc = dict(dgcum=jnp.zeros((CL, LANES), F32), dbeta=jnp.zeros((CL, LANES), F32))

        def head(h):
            lo = h * DH
            qh = qkv[:, lo:lo + DH]
            kh = qkv[:, GW + lo:GW + lo + DH]
            vh = qkv[:, 2 * GW + lo:2 * GW + lo + DH]
            beta = beta_all[:, h:h + 1]
            f = _head_terms(qh, kh, beta, gcum[:, NH + h:NH + h + 1], gcum_t[NH + h:NH + h + 1, :])
            qn, kn, qs, kb, gam, kds, cd, decay = (f[s] for s in ("qn", "kn", "qs", "kb", "gam", "kds", "cd", "decay"))
            t = t_ref[0, h]
            st = sin_ref[0, h]
            vb = vh * beta
            kbg = kb * gam
            u = _dot(t, vb, GP)
            w = _dot(t, kbg, GP)
            yield
            v_new = u - _dot(w, st, GP)
            q_dec = qs * gam
            k_dec = kn * kds

            o = o_ref[:, lo:lo + DH]
            zh = z_ref[:, lo:lo + DH]
            sz = _sig(zh)
            r = lax.rsqrt(jnp.mean(o * o, axis=-1, keepdims=True) + EPS)
            orr = o * r
            d_out = dob_ref[:, lo:lo + DH]
            dz = d_out * (orr * nw_ref[...]) * (sz * (1.0 + zh * (1.0 - sz)))
            don = d_out * (zh * sz)
            st_ref[4:5, 0:DH] += _colsum(don * orr)
            tt = don * nw_ref[...]
            d_o = r * (tt - orr * jnp.mean(tt * orr, axis=-1, keepdims=True))

            yield
            ds_out = dstate[h]
            dv_new = _dot_tn(f["qk"], d_o, GP) + _dot(k_dec, ds_out, GP)
            dqk = jnp.where(f["causal"], _dot_nt(d_o, v_new, GP), 0.0)
            dq_dec = _dot_nt(d_o, st, GP)
            dk_dec = _dot_nt(v_new, ds_out, GP)
            yield
            dstate[h] = _dot_tn(q_dec, d_o, GP) + cd * ds_out - _dot_tn(w, dv_new, GP)
            dcd = jnp.sum(_rowsum(st * ds_out), axis=0, keepdims=True)
            dw = -_dot_nt(dv_new, st, GP)
            dvb = _dot_tn(t, dv_new, GP)
            yield
            dt_m = _dot_nt(dv_new, vb, GP) + _dot_nt(dw, kbg, GP)
            dkbg = _dot_tn(t, dw, GP)
            yield
            dtt = _dot_nt(dt_m, t, GP)
            yield
            da = jnp.where(f["strict"], -_dot_tn(t, dtt, GP), 0.0)
            yield
            dad = da * decay
            dqkd = dqk * decay
            dkb = _dot(dad, kn, GP) + dkbg * gam
            dkn = _dot_tn(dad, kb, GP) + _dot_tn(dqkd, qs, GP) + dk_dec * kds + dkb * beta
            dqs = _dot(dqkd, kn, GP) + dq_dec * gam
            yield
            m = da * f["a"] + dqk * f["qk"]
            tk = _rowsum(dk_dec * k_dec)
            dgl = jnp.sum(tk, axis=0, keepdims=True) + dcd * cd
            dgc = (_rowsum(m) - _rowsum(jnp.transpose(m)) + _rowsum(dq_dec * q_dec) - tk + _rowsum(dkbg * kbg)
                   + jnp.where(row == CL - 1, dgl, 0.0))
            dbeta = _rowsum(dkb * kn) + _rowsum(dvb * vh)
            acc["dgcum"] = acc["dgcum"] + jnp.where(lane == NH + h, dgc, 0.0)
            acc["dbeta"] = acc["dbeta"] + jnp.where(lane == h, dbeta, 0.0)
            dvh = dvb * beta
            dqn = dqs * QSCALE
            dqh = f["rq"] * (dqn - qn * _rowsum(dqn * qn))
            dkh = f["rk"] * (dkn - kn * _rowsum(dkn * kn))
            dsilu = lambda c0: sc[:, c0:c0 + DH] * (1.0 + conv[:, c0:c0 + DH] * (1.0 - sc[:, c0:c0 + DH]))
            dcbuf[0:CL, lo:lo + DH] = dqh * dsilu(lo)
            dcbuf[0:CL, GW + lo:GW + lo + DH] = dkh * dsilu(GW + lo)
            dcbuf[0:CL, 2 * GW + lo:2 * GW + lo + DH] = dvh * dsilu(2 * GW + lo)
            dp_ref[:, 3 * GW + lo:3 * GW + lo + DH] = dz.astype(BF16)

        _lockstep(head(h) for h in range(NH))
        dgcum_all, dbeta_all = acc["dgcum"], acc["dbeta"]

        ii, jj = _tri_iota()
        upper = jnp.where(ii <= jj, 1.0, 0.0).astype(BF16)
        dg_all = _ones_dot(upper, dgcum_all)
        dxg = dg_all * neg_a * _sig(xg)
        st_ref[5:6, 0:LANES] += _colsum(dg_all * g_all)
        st_ref[6:7, 0:LANES] += _colsum(dxg)
        dbl = dbeta_all * beta_all * (1.0 - beta_all)
        dba_ref[...] = jnp.where(lane < NH, dbl, jnp.where(lane < 2 * NH, dxg, 0.0)).astype(BF16)

        dconv = dcbuf[0:CL, :]
        dx = w_ref[0:1, :] * dcbuf[KS - 1:KS - 1 + CL, :]
        st_ref[0:1, :] += _colsum(dconv * xbuf[SH - KS + 1:SH - KS + 1 + CL, :])
        for k in range(1, KS):
            off = SH - (KS - 1) + k
            st_ref[k:k + 1, :] += _colsum(dconv * xbuf[off:off + CL, :])
            dx = dx + w_ref[k:k + 1, :] * dcbuf[KS - 1 - k:KS - 1 - k + CL, :]
        dcbuf[CL:CL + SH, :] = dcbuf[0:SH, :]
        dp_ref[:, 0:3 * GW] = dx.astype(BF16)

    rev = lambda w, j=0: pl.BlockSpec((CL, w), lambda n: (NCH - 1 - n, j))
    halo = lambda j: pl.BlockSpec((SH, GW), lambda n: (jnp.maximum((NCH - 1 - n) * (CL // SH) - 1, 0), j))
    blk4 = lambda a, b: pl.BlockSpec((1, NH, a, b), lambda n: (NCH - 1 - n, 0, 0, 0))
    return _pallas(
        body, name="gdn_bwd", grid=(NCH,),
        in_specs=[rev(GW), rev(GW), blk4(DH, DH), blk4(CL, CL), rev(GW, 2), rev(GW, 3), rev(GW, 4), rev(GW, 5),
                  halo(2), halo(3), halo(4), rev(LANES), _const((KS, 3 * GW)), _const((1, LANES)),
                  _const((1, LANES)), _const((1, DH))],
        out_specs=(rev(4 * GW), rev(LANES), _const((GDN_STATS, 3 * GW))),
        out_shape=(jax.ShapeDtypeStruct((S, 4 * GW), BF16), jax.ShapeDtypeStruct((S, LANES), BF16),
                   jax.ShapeDtypeStruct((GDN_STATS, 3 * GW), F32)),
        scratch_shapes=[pltpu.VMEM((SH + CL, 3 * GW), F32), pltpu.VMEM((CL + SH, 3 * GW), F32),
                        pltpu.VMEM((NH, DH, DH), F32)],
        compiler_params=_params(dimension_semantics=("arbitrary",)),
    )(d_out_b, o_pre, s_in, t_inv, p_main, p_main, p_main, p_main, p_main, p_main, p_main, p_ba,
      gdn_conv_w, alog_l, dt_l, gdn_nw)


def _bwd_in(dp_conf, dp_gdn, dp_ba, x, dx1, nw1, modnb, bada, w_main, w_ba):
    def body(dc_ref, dg_ref, db_ref, x_ref, dx1_ref, nw_ref, mod_ref, b_ref, wm_ref, wb_ref, gx_ref, st_ref):
        i = pl.program_id(0)

        @pl.when(i == 0)
        def _():
            st_ref[...] = jnp.zeros((8, D), F32)

        dh = (_dot(dc_ref[...], wm_ref[0:2 * CW, :]) + _dot(dg_ref[...], wm_ref[2 * CW:NMAIN, :])
              + _dot(db_ref[...], wb_ref[...]))
        xv = x_ref[...]
        r = lax.rsqrt(jnp.mean(xv * xv, axis=-1, keepdims=True) + EPS)
        xr = xv * r
        st_ref[0:1, :] += _colsum(dh)
        st_ref[1:2, :] += _colsum(dh * (xr * nw_ref[...]))
        dxn = dh * (1.0 + _mod(mod_ref, b_ref, 1))
        st_ref[2:3, :] += _colsum(dxn * xr)
        dxr = dxn * nw_ref[...]
        gx_ref[...] = dx1_ref[...] + r * (dxr - xr * jnp.mean(dxr * xr, axis=-1, keepdims=True))

    tile = lambda w: pl.BlockSpec((TM, w), lambda i: (i, 0))
    return _pallas(
        body, name="bwd_in", grid=(NT,),
        in_specs=[tile(2 * CW), tile(4 * GW), tile(LANES), tile(D), tile(D), _const((1, D)), _const((1, 6 * D)),
                  _const((1, 6 * D)), _const((NMAIN, D)), _const((LANES, D))],
        out_specs=(tile(D), _const((8, D))),
        out_shape=(jax.ShapeDtypeStruct((S, D), F32), jax.ShapeDtypeStruct((8, D), F32)),
        compiler_params=_params(dimension_semantics=("arbitrary",)),
    )(dp_conf, dp_gdn, dp_ba, x, dx1, nw1, modnb, bada, w_main, w_ba)


def _adamw(w, g, m, v):
    m = ADAM_B1 * m + (1.0 - ADAM_B1) * g
    v = ADAM_B2 * v + (1.0 - ADAM_B2) * (g * g)
    m_hat = m / BC1
    v_hat = v / BC2
    delta = -ADAM_LR * (m_hat / (jnp.sqrt(v_hat) + ADAM_EPS) + ADAM_WD * w)
    return delta, m, v


ADAM_BLOCK_BYTES = 6 * 1024 * 1024


def _adam_tile(rows, cols):
    padded = -(-cols // LANES) * LANES
    if N_DEV * rows * padded * 4 <= ADAM_BLOCK_BYTES:
        return rows, cols
    best = None
    for tr in range(16, rows, 16):
        if rows % tr == 0 and N_DEV * tr * padded * 4 <= ADAM_BLOCK_BYTES:
            best = tr
    if best is not None:
        return best, cols
    rows_padded = -(-rows // 16) * 16
    tc = LANES
    for cand in range(LANES, cols, LANES):
        if cols % cand == 0 and N_DEV * rows_padded * cand * 4 <= ADAM_BLOCK_BYTES:
            tc = cand
    return rows, tc


def _reduce_adam(name, parts, w, m, v, own=None):
    rows, cols = w.shape
    tr, tc = _adam_tile(rows, cols)

    def body(*refs):
        p_ref, w_ref, m_ref, v_ref = refs[:4]
        g_ref, d_ref, nm_ref, nv_ref = refs[-4:]
        if own is None:
            part = lambda j: p_ref[j].astype(F32)
        else:
            me = 4 * lax.axis_index("x") + 2 * lax.axis_index("y") + lax.axis_index("c")
            part = lambda j: jnp.where(me == j, refs[4][...], p_ref[j]).astype(F32)
        g = part(0)
        for j in range(1, N_DEV):
            g = g + part(j)
        g_ref[...] = g
        d_ref[...], nm_ref[...], nv_ref[...] = _adamw(w_ref[...], g, m_ref[...], v_ref[...])

    blk = pl.BlockSpec((tr, tc), lambda i, j: (i, j))
    sds = jax.ShapeDtypeStruct((rows, cols), F32)
    extra = [] if own is None else [own]
    return _pallas(
        body, name=name, grid=(rows // tr, cols // tc),
        in_specs=[pl.BlockSpec((N_DEV, tr, tc), lambda i, j: (0, i, j)), blk, blk, blk] + [blk] * len(extra),
        out_specs=(blk, blk, blk, blk), out_shape=(sds, sds, sds, sds),
        compiler_params=_params(dimension_semantics=("arbitrary", "arbitrary")),
    )(parts, w, m, v, *extra)


def _ada_adam(c_all, dmod_sh, w, m, v):
    rows, cols = w.shape
    tr = 256

    def body(c_ref, dm_ref, w_ref, m_ref, v_ref, g_ref, d_ref, nm_ref, nv_ref):
        cv = c_ref[...]
        g = _dot_tn(cv * _sig(cv), dm_ref[...], HI)
        g_ref[...] = g
        d_ref[...], nm_ref[...], nv_ref[...] = _adamw(w_ref[...], g, m_ref[...], v_ref[...])

    blk = pl.BlockSpec((tr, cols), lambda i: (i, 0))
    sds = jax.ShapeDtypeStruct((rows, cols), F32)
    return _pallas(
        body, name="ada_adam", grid=(rows // tr,),
        in_specs=[pl.BlockSpec((N_DEV, tr), lambda i: (0, i)), _const((N_DEV, cols)), blk, blk, blk],
        out_specs=(blk, blk, blk, blk), out_shape=(sds, sds, sds, sds),
        compiler_params=_params(dimension_semantics=("arbitrary",)),
    )(c_all, dmod_sh, w, m, v)


def _lanes(a, at=0):
    return jnp.pad(a, ((0, 0), (at, LANES - at - a.shape[1])))


WEIGHT_NAMES = ["w_ada", "b_ada", "norm_mix_w", "w_in", "conv_w", "conv_b", "conv_gn_w", "conv_gn_b", "gdn_conv_w",
                "gdn_a_log", "gdn_dt_bias", "gdn_norm_w", "w_out", "norm_ffn_w", "w_ffn_in", "w_ffn_out",
                "norm_final_w"]


def _slab(b_ada, norm_mix_w, norm_ffn_w, norm_final_w, conv_b, conv_gn_w, conv_gn_b, gdn_norm_w, a_log, dt_bias):
    return jnp.concatenate([
        b_ada.reshape(48, LANES), norm_mix_w.reshape(8, LANES), norm_ffn_w.reshape(8, LANES),
        norm_final_w.reshape(8, LANES), conv_b.reshape(4, LANES), conv_gn_w.reshape(4, LANES),
        conv_gn_b.reshape(4, LANES), gdn_norm_w.reshape(1, LANES), _lanes(a_log), _lanes(dt_bias),
        jnp.zeros((1, LANES), F32)], axis=0)


def _unslab(t):
    return dict(b_ada=t[0:48].reshape(1, 6 * D), norm_mix_w=t[48:56].reshape(1, D),
                norm_ffn_w=t[56:64].reshape(1, D), norm_final_w=t[64:72].reshape(D),
                conv_b=t[72:76].reshape(1, CW), conv_gn_w=t[76:80].reshape(1, CW),
                conv_gn_b=t[80:84].reshape(1, CW), gdn_norm_w=t[84:85], gdn_a_log=t[85:86, 0:NH],
                gdn_dt_bias=t[86:87, 0:NH])


def _mix_forward(w, xs, modnb):
    w_main = w["w_in"]
    w_ba = jnp.pad(w["w_in"][NMAIN:], ((0, LANES - 2 * NH), (0, 0)))
    alog_l = _lanes(w["gdn_a_log"], NH)
    dt_l = _lanes(w["gdn_dt_bias"], NH)
    p_main, p_ba, hb1 = _fwd_in(xs, w["norm_mix_w"], modnb, w["b_ada"], w_main, w_ba)
    y_conv, out_a = _conf_fwd(p_main, w["conv_w"], w["conv_b"], w["conv_gn_w"], w["conv_gn_b"])
    w_o, u_o, qg, kd, qk, cd, t_inv = _gdn_prep(p_main, p_ba, w["gdn_conv_w"], alog_l, dt_l)
    out_b, o_pre, s_in = _gdn_scan(w_o, u_o, qg, kd, qk, cd, p_main, w["gdn_norm_w"])
    return dict(w_main=w_main, w_ba=w_ba, alog_l=alog_l, dt_l=dt_l, p_main=p_main, p_ba=p_ba, hb1=hb1,
                y_conv=y_conv, out_a=out_a, out_b=out_b, o_pre=o_pre, s_in=s_in, t_inv=t_inv)


def _ffn_stage(w, f, xs, tgt, modnb):
    x1, mix, oab = _fwd_out(f["out_a"], f["out_b"], xs, modnb, w["b_ada"], w["w_out"])
    hb2, act, pre, dx2, dffn, st_fwd = _ffn_forward(x1, tgt, modnb, w["b_ada"], w["norm_ffn_w"],
                                                    w["norm_final_w"], w["w_ffn_in"], w["w_ffn_out"])
    gw_ffn_out = _grad_w_ffn_out(act, dffn)
    df, dx1, st_bwd = _ffn_backward(dffn, pre, x1, dx2, modnb, w["b_ada"], w["norm_ffn_w"], w["w_ffn_in"],
                                    w["w_ffn_out"])
    gw_ffn_in = _grad_w_ffn_in(hb2, df)
    return dict(mix=mix, oab=oab, dx1=dx1, st_ffn=st_fwd + st_bwd, gw_ffn_in=gw_ffn_in, gw_ffn_out=gw_ffn_out)


def _out_backward(w, g, modnb):
    dmix, d_out_a, d_out_b, st_out = _bwd_out(g["dx1"], g["mix"], modnb, w["b_ada"], w["w_out"])
    return dict(d_out_a=d_out_a, d_out_b=d_out_b, st_out=st_out, gw_out=_grad_w("grad_w_out", g["oab"], dmix, 512))


def _mix_backward(w, f, g, a, xs, modnb):
    d_out_a, d_out_b, st_out = a["d_out_a"], a["d_out_b"], a["st_out"]
    dp_conf, st_conf = _conf_bwd(d_out_a, f["y_conv"], f["p_main"], w["conv_w"], w["conv_gn_w"], w["conv_gn_b"])
    dp_gdn, dp_ba, st_gdn = _gdn_bwd(d_out_b, f["o_pre"], f["s_in"], f["t_inv"], f["p_main"], f["p_ba"],
                                     w["gdn_conv_w"], f["alog_l"], f["dt_l"], w["gdn_norm_w"])
    grad_x, st_in = _bwd_in(dp_conf, dp_gdn, dp_ba, xs, g["dx1"], w["norm_mix_w"], modnb, w["b_ada"], f["w_main"],
                            f["w_ba"])
    hb1 = f["hb1"]
    gw_in = jnp.concatenate(
        [_grad_w("grad_w_in_conf", dp_conf, hb1, 512), _grad_w("grad_w_in_gdn", dp_gdn, hb1, 512),
         _grad_w("grad_w_in_ba", dp_ba, hb1, LANES)[:2 * NH]], axis=0)
    st_ffn = g["st_ffn"]
    dmod = jnp.concatenate([st_in[0:1], st_in[1:2], st_out[0:1], st_ffn[2:3], st_ffn[3:4], st_ffn[1:2]], axis=1)
    small = jnp.concatenate([
        dmod.reshape(48, LANES), st_in[2:3].reshape(8, LANES), st_ffn[4:5].reshape(8, LANES),
        st_ffn[0:1].reshape(8, LANES), st_conf[31:32].reshape(4, LANES), st_conf[32:33].reshape(4, LANES),
        st_conf[33:34].reshape(4, LANES), st_gdn[4:5, 0:LANES],
        _lanes(st_gdn[5:6, NH:2 * NH]), _lanes(st_gdn[6:7, NH:2 * NH]), st_ffn[5:6, 0:LANES]], axis=0)
    return dict(grad_x=grad_x, gw_in=gw_in, gw_conv=st_conf[0:KC], gw_gconv=st_gdn[0:KS], small=small)


def _local(w, xs, tgt, modnb):
    f = _mix_forward(w, xs, modnb)
    g = _ffn_stage(w, f, xs, tgt, modnb)
    a = _out_backward(w, g, modnb)
    b = _mix_backward(w, f, g, a, xs, modnb)
    return dict(b, gw_out=a["gw_out"], gw_ffn_in=g["gw_ffn_in"], gw_ffn_out=g["gw_ffn_out"])


def kernel(x, c, w_ada, b_ada, norm_mix_w, w_in, conv_w, conv_b, conv_gn_w, conv_gn_b, gdn_conv_w, gdn_a_log, gdn_dt_bias, gdn_norm_w, w_out, norm_ffn_w, w_ffn_in, w_ffn_out, norm_final_w, loss_target, m_w_ada, m_b_ada, m_norm_mix_w, m_w_in, m_conv_w, m_conv_b, m_conv_gn_w, m_conv_gn_b, m_gdn_conv_w, m_gdn_a_log, m_gdn_dt_bias, m_gdn_norm_w, m_w_out, m_norm_ffn_w, m_w_ffn_in, m_w_ffn_out, m_norm_final_w, v_w_ada, v_b_ada, v_norm_mix_w, v_w_in, v_conv_w, v_conv_b, v_conv_gn_w, v_conv_gn_b, v_gdn_conv_w, v_gdn_a_log, v_gdn_dt_bias, v_gdn_norm_w, v_w_out, v_norm_ffn_w, v_w_ffn_in, v_w_ffn_out, v_norm_final_w):
    me = 4 * lax.axis_index("x") + 2 * lax.axis_index("y") + lax.axis_index("c")
    xs = x.reshape(S, D)
    tgt = loss_target.reshape(S, D)

    g_c, g_cw, g_gcw = _exchange("gather_cond", [c, conv_w[0], gdn_conv_w[0]], [False] * 3)
    c_all = g_c.reshape(N_DEV, D)
    g_mod, mod_token = _exchange("gather_mod", [_mod_shard(c_all, w_ada[0])], [False], with_token=True)
    modnb = lax.dynamic_index_in_dim(g_mod, me, axis=1, keepdims=False).reshape(1, 6 * D)

    late = [w_out[0].astype(BF16), jnp.transpose(w_ffn_in[0]).astype(BF16), w_ffn_out[0].astype(BF16)]
    g_win, *late_lands = _gather_two_level(
        "gather_weights", [_after(jnp.transpose(w_in[0]), mod_token).astype(BF16)] + late, seed_only=(1, 2, 3))
    late_started = _exchange_start("gather_late_start", late, late_lands, [False] * 3, only=LEVEL_ONE)
    modnb = _after(modnb, late_started[-1])
    w = dict(b_ada=b_ada, norm_mix_w=norm_mix_w, conv_b=conv_b, conv_gn_w=conv_gn_w, conv_gn_b=conv_gn_b,
             gdn_a_log=gdn_a_log, gdn_dt_bias=gdn_dt_bias, gdn_norm_w=gdn_norm_w, norm_ffn_w=norm_ffn_w,
             norm_final_w=norm_final_w.reshape(1, D),
             conv_w=jnp.transpose(g_cw, (1, 0, 2)).reshape(KC, CW),
             gdn_conv_w=jnp.transpose(g_gcw, (1, 0, 2)).reshape(KS, 3 * GW),
             w_in=g_win.reshape(NIN, D))

    f = _mix_forward(w, xs, modnb)
    _, late_landed = _exchange_wait("gather_late_wait", late_started, [False] * 3, (f["out_a"], f["out_b"]),
                                    only=LEVEL_ONE)
    g_wout, g_wfi, g_wfo = _relay_to_sibling("gather_late_relay", late_landed)
    w.update(w_out=g_wout.reshape(D, D), w_ffn_in=g_wfi, w_ffn_out=g_wfo.reshape(4, FB, D))
    g = _ffn_stage(w, f, xs, tgt, modnb)

    ffn_grads = [g["gw_ffn_in"], g["gw_ffn_out"].reshape(N_DEV, DFF // N_DEV, D)]
    ffn_started = _exchange_start("scatter_ffn_start", ffn_grads,
                                  [lax.empty(a.shape, a.dtype) for a in ffn_grads], [True] * 2)
    a = _out_backward(w, g, _after(modnb, ffn_started[-1]))
    out_grads = [a["gw_out"].reshape(N_DEV, D // N_DEV, D)]
    out_started = _exchange_start("scatter_out_start", out_grads,
                                  [lax.empty(t.shape, t.dtype) for t in out_grads], [True])
    loc = _mix_backward(dict(w, conv_gn_w=_after(w["conv_gn_w"], out_started[-1])), f, g, a, xs, modnb)

    gw_in = loc["gw_in"].reshape(N_DEV, NIN // N_DEV, D)
    ready = (gw_in[0:1, 0, 0:1] != gw_in[0:1, 0, 0:1]).astype(F32)
    g_small, small_token = _exchange("gather_small", [loc["small"] + ready], [False], with_token=True)

    in_grads = [gw_in,
                _after(jnp.transpose(loc["gw_conv"].reshape(KC, N_DEV, CW // N_DEV), (1, 0, 2)), small_token),
                jnp.transpose(loc["gw_gconv"].reshape(KS, N_DEV, 3 * GW // N_DEV), (1, 0, 2))]
    in_started = _exchange_start("scatter_in_start", in_grads,
                                 [lax.empty(t.shape, t.dtype) for t in in_grads], [True] * 3)
    g_small = _after(g_small, in_started[-1])
    sw = _slab(b_ada, norm_mix_w, norm_ffn_w, norm_final_w, conv_b, conv_gn_w, conv_gn_b, gdn_norm_w, gdn_a_log,
               gdn_dt_bias)
    sm = _slab(m_b_ada, m_norm_mix_w, m_norm_ffn_w, m_norm_final_w, m_conv_b, m_conv_gn_w, m_conv_gn_b,
               m_gdn_norm_w, m_gdn_a_log, m_gdn_dt_bias)
    sv = _slab(v_b_ada, v_norm_mix_w, v_norm_ffn_w, v_norm_final_w, v_conv_b, v_conv_gn_w, v_conv_gn_b,
               v_gdn_norm_w, v_gdn_a_log, v_gdn_dt_bias)
    small_out = _reduce_adam("adam_small", g_small, sw, sm, sv)
    loss = small_out[0][SMALL_ROWS - 1, 0]
    res = [_unslab(t) for t in small_out]

    dmod_rows = g_small[:, 0:48, :].reshape(N_DEV, 6 * D)
    dmod_sh = lax.dynamic_slice_in_dim(dmod_rows, me * (6 * D // N_DEV), 6 * D // N_DEV, axis=1)

    def own(sent):
        return lax.dynamic_index_in_dim(sent, me, axis=0, keepdims=False)

    big = dict(w_ada=_ada_adam(c_all, dmod_sh, w_ada[0], m_w_ada[0], v_w_ada[0]))
    (sent_fi, sent_fo), (r_fi, r_fo) = _exchange_wait("scatter_ffn_wait", ffn_started, [True] * 2,
                                                         (big["w_ada"][0],))
    big["w_ffn_in"] = [jnp.transpose(t) for t in _reduce_adam(
        "adam_w_ffn_in", r_fi, jnp.transpose(w_ffn_in[0]), jnp.transpose(m_w_ffn_in[0]),
        jnp.transpose(v_w_ffn_in[0]), own(sent_fi))]
    big["w_ffn_out"] = _reduce_adam("adam_w_ffn_out", r_fo, w_ffn_out[0], m_w_ffn_out[0], v_w_ffn_out[0],
                                    own(sent_fo))
    (sent_out,), (r_out,) = _exchange_wait("scatter_out_wait", out_started, [True], (big["w_ffn_out"][0],))
    big["w_out"] = _reduce_adam("adam_w_out", r_out, w_out[0], m_w_out[0], v_w_out[0], own(sent_out))
    (sent_in, sent_cw, sent_gcw), (r_in, r_cw, r_gcw) = _exchange_wait(
        "scatter_in_wait", in_started, [True] * 3, (big["w_out"][0],))
    big["w_in"] = [jnp.transpose(t) for t in _reduce_adam(
        "adam_w_in", r_in, jnp.transpose(w_in[0]), jnp.transpose(m_w_in[0]), jnp.transpose(v_w_in[0]),
        own(sent_in))]
    big["conv_w"] = _reduce_adam("adam_conv_w", r_cw, conv_w[0], m_conv_w[0], v_conv_w[0], own(sent_cw))
    big["gdn_conv_w"] = _reduce_adam("adam_gdn_conv_w", r_gcw, gdn_conv_w[0], m_gdn_conv_w[0], v_gdn_conv_w[0],
                                     own(sent_gcw))
    outs = [loss, loc["grad_x"].reshape(1, S, D)]
    for kind in range(4):
        for nm in WEIGHT_NAMES:
            outs.append(big[nm][kind][None] if nm in big else res[kind][nm])
    return tuple(outs)
```

```python
import functools

import jax
import jax.numpy as jnp
from jax import lax
from jax.experimental import pallas as pl
from jax.experimental.pallas import tpu as pltpu

F32 = jnp.float32
BF16 = jnp.bfloat16
HI = lax.Precision.HIGHEST
MESH = pl.DeviceIdType.MESH

N_DEV = 8
S = 2048
D = 1024
TM = 256
NT = S // TM
CW = 512
KC = 31
NG = 8
GSZ = CW // NG
HALO = 32
GW = 512
NH = 4
DH = 128
KS = 4
SH = 8
CL = 64
NCH = S // CL
NMAIN = 2 * CW + 4 * GW
NIN = NMAIN + 2 * NH
DFF = 2816
FB = DFF // 4
EPS = 1e-6
QSCALE = DH ** -0.5
LANES = 128
SMALL_ROWS = 88

ADAM_LR = 0.001
ADAM_B1 = 0.9
ADAM_B2 = 0.999
ADAM_EPS = 1e-08
ADAM_WD = 0.01
ADAM_STEP = 10
BC1 = 1.0 - ADAM_B1 ** ADAM_STEP
BC2 = 1.0 - ADAM_B2 ** ADAM_STEP

MIB = 1024 * 1024
VMEM_LIMIT_MIB = 32


def _params(limit_mib=VMEM_LIMIT_MIB, **kw):
    return pltpu.CompilerParams(vmem_limit_bytes=limit_mib * MIB, **kw)


def _sig(x):
    return jax.nn.sigmoid(x)


GP = BF16


def _operands(a, b, prec):
    if prec is BF16:
        return a.astype(BF16), b.astype(BF16), None
    return a, b, prec


def _dot(a, b, prec=None):
    a, b, prec = _operands(a, b, prec)
    return jnp.dot(a, b, preferred_element_type=F32, precision=prec)


def _dot_nt(a, b, prec=None):
    a, b, prec = _operands(a, b, prec)
    return lax.dot_general(a, b, (((1,), (1,)), ((), ())), preferred_element_type=F32, precision=prec)


def _dot_tn(a, b, prec=None):
    a, b, prec = _operands(a, b, prec)
    return lax.dot_general(a, b, (((0,), (0,)), ((), ())), preferred_element_type=F32, precision=prec)


def _lockstep(gens):
    gens = list(gens)
    while gens:
        alive = []
        for g in gens:
            try:
                next(g)
                alive.append(g)
            except StopIteration:
                pass
        gens = alive


def _rowsum(x):
    return jnp.sum(x, axis=-1, keepdims=True)


def _colsum(x):
    return jnp.sum(x, axis=0, keepdims=True)


def _mod(mod_ref, b_ref, k):
    return mod_ref[:, k * D:(k + 1) * D] + b_ref[:, k * D:(k + 1) * D]


def _const(shape):
    nd = len(shape)
    return pl.BlockSpec(shape, lambda *_: (0,) * nd)


def _const1(shape):
    nd = len(shape)
    return pl.BlockSpec(shape, lambda *_: (0,) * nd, pipeline_mode=pl.Buffered(1))


PEER_FLIPS = [(dx, dy, dc) for dx in (0, 1) for dy in (0, 1) for dc in (0, 1)][1:]


def _after(x, token):
    return x + token[0:1, 0:1].astype(x.dtype).reshape((1,) * x.ndim)


def _exchange(name, srcs, per_dest, seed_only=(), with_token=False):
    n = len(srcs)
    out_shape = []
    for a, pd in zip(srcs, per_dest):
        blk = a.shape[1:] if pd else a.shape
        out_shape.append(jax.ShapeDtypeStruct((N_DEV,) + tuple(blk), a.dtype))

    def body(*refs):
        src = refs[:n]
        dst = refs[n:2 * n]
        send_sems, recv_sems, local_sems = refs[-3:]
        if with_token:
            refs[2 * n][...] = jnp.zeros((8, LANES), F32)
        x, y, c = lax.axis_index("x"), lax.axis_index("y"), lax.axis_index("c")
        me = 4 * x + 2 * y + c

        def piece(i, j):
            return src[i].at[j] if per_dest[i] else src[i]

        copies = []
        for k, (dx, dy, dc) in enumerate(PEER_FLIPS):
            px = 1 - x if dx else x
            py = 1 - y if dy else y
            pc = 1 - c if dc else c
            pj = 4 * px + 2 * py + pc
            for i in range(n):
                if i in seed_only:
                    continue
                cp = pltpu.make_async_remote_copy(
                    src_ref=piece(i, pj), dst_ref=dst[i].at[me],
                    send_sem=send_sems.at[k * n + i], recv_sem=recv_sems.at[k * n + i],
                    device_id=(px, py, pc), device_id_type=MESH)
                cp.start()
                arrive = pltpu.make_async_remote_copy(
                    src_ref=piece(i, pj), dst_ref=dst[i].at[pj],
                    send_sem=send_sems.at[k * n + i], recv_sem=recv_sems.at[k * n + i],
                    device_id=(px, py, pc), device_id_type=MESH)
                copies.append((cp, arrive))
        own = []
        for i in range(n):
            lc = pltpu.make_async_copy(piece(i, me), dst[i].at[me], local_sems.at[i])
            lc.start()
            own.append(lc)
        for cp, arrive in copies:
            arrive.wait_recv()
        for cp, arrive in copies:
            cp.wait_send()
        for lc in own:
            lc.wait()

    any_spec = pl.BlockSpec(memory_space=pl.ANY)
    out_specs = [any_spec] * n
    if with_token:
        out_shape.append(jax.ShapeDtypeStruct((8, LANES), F32))
        out_specs.append(pl.BlockSpec(memory_space=pltpu.VMEM))
    return pl.pallas_call(
        body, name=name, out_shape=tuple(out_shape),
        in_specs=[any_spec] * n, out_specs=tuple(out_specs),
        scratch_shapes=[pltpu.SemaphoreType.DMA((7 * n,)), pltpu.SemaphoreType.DMA((7 * n,)),
                        pltpu.SemaphoreType.DMA((n,))],
        compiler_params=pltpu.CompilerParams(has_side_effects=True),
    )(*srcs)


CHIP_FLIPS = [(0, 1), (1, 0), (1, 1)]
LEVEL_ONE = [k for k, (dx, dy, dc) in enumerate(PEER_FLIPS) if (dx, dy, dc) == (0, 0, 1) or dc == 0]


def _chip_peers(x, y):
    return [(1 - x if dx else x, 1 - y if dy else y) for dx, dy in CHIP_FLIPS]


def _gather_two_level(name, srcs, seed_only=()):
    n = len(srcs)
    live = [i for i in range(n) if i not in seed_only]

    def body(*refs):
        src, dst = refs[:n], refs[n:2 * n]
        send_sems, recv_sems, local_sems = refs[2 * n:2 * n + 3]
        bounce = refs[2 * n + 3:]
        x, y, c = lax.axis_index("x"), lax.axis_index("y"), lax.axis_index("c")
        me = 4 * x + 2 * y + c
        sibling = (x, y, 1 - c)
        chips = _chip_peers(x, y)

        def copy(k, i, src_ref, slot, to):
            return pltpu.make_async_remote_copy(
                src_ref=src_ref, dst_ref=dst[i].at[slot], send_sem=send_sems.at[k * n + i],
                recv_sem=recv_sems.at[k * n + i], device_id=to, device_id_type=MESH)

        first = []
        for i in live:
            first.append(copy(0, i, src[i], me, sibling))
            first += [copy(1 + j, i, src[i], me, (px, py, c)) for j, (px, py) in enumerate(chips)]
        for cp in first:
            cp.start()
        up = [pltpu.make_async_copy(src[i], bounce[i], local_sems.at[i]) for i in range(n)]
        for cp in up:
            cp.start()
        for cp in up:
            cp.wait()
        own = [pltpu.make_async_copy(bounce[i], dst[i].at[me], local_sems.at[i]) for i in range(n)]
        for cp in own:
            cp.start()
        passed = []
        for j, (px, py) in enumerate(chips):
            slot = 4 * px + 2 * py + c
            for i in live:
                copy(1 + j, i, src[i], slot, (px, py, c)).wait_recv()
                fwd = copy(4 + j, i, dst[i].at[slot], slot, sibling)
                fwd.start()
                passed.append(fwd)
        for i in live:
            copy(0, i, src[i], 4 * x + 2 * y + 1 - c, sibling).wait_recv()
            for j, (px, py) in enumerate(chips):
                copy(4 + j, i, src[i], 4 * px + 2 * py + 1 - c, sibling).wait_recv()
        for cp in first + passed:
            cp.wait_send()
        for cp in own:
            cp.wait()

    any_spec = pl.BlockSpec(memory_space=pl.ANY)
    return pl.pallas_call(
        body, name=name, out_shape=tuple(jax.ShapeDtypeStruct((N_DEV,) + a.shape, a.dtype) for a in srcs),
        in_specs=[any_spec] * n, out_specs=tuple([any_spec] * n),
        scratch_shapes=[pltpu.SemaphoreType.DMA((7 * n,)), pltpu.SemaphoreType.DMA((7 * n,)),
                        pltpu.SemaphoreType.DMA((n,))] + [pltpu.VMEM(a.shape, a.dtype) for a in srcs],
        compiler_params=pltpu.CompilerParams(has_side_effects=True),
    )(*srcs)


def _relay_to_sibling(name, lands):
    n = len(lands)

    def body(*refs):
        land = refs[n:2 * n]
        send_sems, recv_sems = refs[-2:]
        x, y, c = lax.axis_index("x"), lax.axis_index("y"), lax.axis_index("c")
        sibling = (x, y, 1 - c)
        sends = []
        for j, (px, py) in enumerate(_chip_peers(x, y)):
            slot = 4 * px + 2 * py + c
            for i in range(n):
                cp = pltpu.make_async_remote_copy(
                    src_ref=land[i].at[slot], dst_ref=land[i].at[slot], send_sem=send_sems.at[j * n + i],
                    recv_sem=recv_sems.at[j * n + i], device_id=sibling, device_id_type=MESH)
                cp.start()
                sends.append(cp)
        for j, (px, py) in enumerate(_chip_peers(x, y)):
            slot = 4 * px + 2 * py + 1 - c
            for i in range(n):
                pltpu.make_async_remote_copy(
                    src_ref=land[i].at[slot], dst_ref=land[i].at[slot], send_sem=send_sems.at[j * n + i],
                    recv_sem=recv_sems.at[j * n + i], device_id=sibling, device_id_type=MESH).wait_recv()
        for cp in sends:
            cp.wait_send()

    any_spec = pl.BlockSpec(memory_space=pl.ANY)
    return pl.pallas_call(
        body, name=name, out_shape=tuple(jax.ShapeDtypeStruct(a.shape, a.dtype) for a in lands),
        in_specs=[any_spec] * n, out_specs=tuple([any_spec] * n),
        input_output_aliases={i: i for i in range(n)},
        scratch_shapes=[pltpu.SemaphoreType.DMA((3 * n,)), pltpu.SemaphoreType.DMA((3 * n,))],
        compiler_params=pltpu.CompilerParams(has_side_effects=True),
    )(*lands)


HBM_SPEC = pl.BlockSpec(memory_space=pltpu.HBM)
SEM_SPEC = pl.BlockSpec(memory_space=pltpu.SEMAPHORE)
DATAFLOW = pltpu.SideEffectType.DATAFLOW_SIDE_EFFECTING


def _peers(only=None):
    x, y, c = lax.axis_index("x"), lax.axis_index("y"), lax.axis_index("c")
    out = []
    for k, (dx, dy, dc) in enumerate(PEER_FLIPS):
        if only is not None and k not in only:
            continue
        px = 1 - x if dx else x
        py = 1 - y if dy else y
        pc = 1 - c if dc else c
        out.append((k, (px, py, pc), 4 * px + 2 * py + pc))
    return 4 * x + 2 * y + c, out


def _exchange_start(name, srcs, lands, per_dest, only=None):
    n = len(srcs)

    def body(*refs):
        src, land = refs[:n], refs[n:2 * n]
        send_sems, recv_sems = refs[2 * n], refs[2 * n + 1]
        token = refs[-1]
        me, peers = _peers(only)
        for k, peer, pj in peers:
            for i in range(n):
                pltpu.make_async_remote_copy(
                    src_ref=src[i].at[pj] if per_dest[i] else src[i], dst_ref=land[i].at[me],
                    send_sem=send_sems.at[k * n + i], recv_sem=recv_sems.at[k * n + i],
                    device_id=peer, device_id_type=MESH).start()
        token[...] = jnp.zeros((8, LANES), F32)

    arrays = list(srcs) + list(lands)
    return pl.pallas_call(
        body, name=name,
        out_shape=(pltpu.SemaphoreType.DMA((7 * n,)), pltpu.SemaphoreType.DMA((7 * n,)),
                   *[pltpu.HBM(a.shape, a.dtype) for a in arrays], jax.ShapeDtypeStruct((8, LANES), F32)),
        in_specs=[HBM_SPEC] * (2 * n),
        out_specs=(SEM_SPEC, SEM_SPEC, *[HBM_SPEC] * (2 * n), pl.BlockSpec(memory_space=pltpu.VMEM)),
        input_output_aliases={i: 2 + i for i in range(2 * n)},
        compiler_params=pltpu.CompilerParams(has_side_effects=DATAFLOW),
    )(*[pltpu.with_memory_space_constraint(a, pltpu.HBM) for a in arrays])


def _exchange_wait(name, started, per_dest, after, only=None):
    n = (len(started) - 3) // 2
    send_sems, recv_sems = started[0], started[1]
    arrays = list(started[2:2 + 2 * n])

    def body(*refs):
        src, land = refs[:n], refs[n:2 * n]
        send, recv = refs[2 * n], refs[2 * n + 1]
        me, peers = _peers(only)
        for k, peer, pj in peers:
            for i in range(n):
                cp = pltpu.make_async_remote_copy(
                    src_ref=src[i].at[pj] if per_dest[i] else src[i], dst_ref=land[i].at[pj],
                    send_sem=send.at[k * n + i], recv_sem=recv.at[k * n + i],
                    device_id=peer, device_id_type=MESH)
                cp.wait_send()
                cp.wait_recv()

    outs = pl.pallas_call(
        body, name=name,
        out_shape=tuple(pltpu.HBM(a.shape, a.dtype) for a in arrays),
        in_specs=[HBM_SPEC] * (2 * n) + [SEM_SPEC, SEM_SPEC] + [pl.BlockSpec(memory_space=pl.ANY)] * len(after),
        out_specs=tuple([HBM_SPEC] * (2 * n)),
        input_output_aliases={i: i for i in range(2 * n)},
        compiler_params=pltpu.CompilerParams(has_side_effects=DATAFLOW),
    )(*arrays, send_sems, recv_sems, *after)
    return outs[:n], outs[n:]


def _mod_shard(c_all, w_ada):
    def body(c_ref, w_ref, o_ref):
        cv = c_ref[...]
        ca = cv * _sig(cv)
        o_ref[...] = _dot(ca.astype(BF16), w_ref[...].astype(BF16))

    return pl.pallas_call(
        body, name="mod_shard", out_shape=jax.ShapeDtypeStruct((N_DEV, w_ada.shape[1]), F32),
        compiler_params=_params(),
    )(c_all, w_ada)


def _fwd_in(x, nw1, modnb, bada, w_main, w_ba):
    def body(x_ref, nw_ref, mod_ref, b_ref, wm_ref, wb_ref, pm_ref, pb_ref, hb_ref):
        xv = x_ref[...]
        r = lax.rsqrt(jnp.mean(xv * xv, axis=-1, keepdims=True) + EPS)
        h = (xv * r * nw_ref[...]) * (1.0 + _mod(mod_ref, b_ref, 1)) + _mod(mod_ref, b_ref, 0)
        hb = h.astype(BF16)
        hb_ref[...] = hb
        pm_ref[...] = _dot_nt(hb, wm_ref[...])
        pb_ref[...] = _dot_nt(hb, wb_ref[...])

    return pl.pallas_call(
        body, name="fwd_in", grid=(NT,),
        in_specs=[pl.BlockSpec((TM, D), lambda i: (i, 0)), _const((1, D)), _const((1, 6 * D)), _const((1, 6 * D)),
                  _const((NMAIN, D)), _const((LANES, D))],
        out_specs=(pl.BlockSpec((TM, NMAIN), lambda i: (i, 0)), pl.BlockSpec((TM, LANES), lambda i: (i, 0)),
                   pl.BlockSpec((TM, D), lambda i: (i, 0))),
        out_shape=(jax.ShapeDtypeStruct((S, NMAIN), F32), jax.ShapeDtypeStruct((S, LANES), F32),
                   jax.ShapeDtypeStruct((S, D), BF16)),
        compiler_params=_params(dimension_semantics=("arbitrary",)),
    )(x, nw1, modnb, bada, w_main, w_ba)


def _group_mean_matrix():
    ii = lax.broadcasted_iota(jnp.int32, (CW, CW), 0) // GSZ
    jj = lax.broadcasted_iota(jnp.int32, (CW, CW), 1) // GSZ
    return jnp.where(ii == jj, 1.0 / GSZ, 0.0).astype(F32)


SUB = 8
SHIFT_ROWS = HALO + TM - SUB


def _fill_shifted(buf, sh):
    for b in range(1, SUB):
        sh[b - 1] = buf[b:b + SHIFT_ROWS, :]


def _rows_at(buf, sh, off):
    a, b = divmod(off, SUB)
    if b == 0:
        return buf[off:off + TM, :]
    return sh[b - 1, SUB * a:SUB * a + TM, :]


def _group_mean(x, pm):
    hi = x.astype(BF16)
    r1 = x - hi.astype(F32)
    mid = r1.astype(BF16)
    lo = (r1 - mid.astype(F32)).astype(BF16)
    return _dot(hi, pm) + _dot(mid, pm) + _dot(lo, pm)


def _conf_fwd(p_main, conv_w, conv_b, gn_w, gn_b):
    def body(a_ref, g_ref, w_ref, b_ref, gw_ref, gb_ref, y_ref, oa_ref, ubuf, ush):
        i = pl.program_id(0)

        @pl.when(i == 0)
        def _():
            ubuf[0:HALO, :] = jnp.zeros((HALO, CW), F32)

        ubuf[HALO:HALO + TM, :] = a_ref[...] * _sig(g_ref[...])
        _fill_shifted(ubuf, ush)
        acc = jnp.zeros((TM, CW), F32) + b_ref[...]
        for k in range(KC):
            acc = acc + w_ref[k:k + 1, :] * _rows_at(ubuf, ush, HALO - (KC - 1) + k)
        y_ref[...] = acc
        ubuf[0:HALO, :] = ubuf[TM:TM + HALO, :]
        pm = _group_mean_matrix().astype(BF16)
        dlt = acc - _group_mean(acc, pm)
        var = _group_mean(dlt * dlt, pm)
        o = dlt * lax.rsqrt(var + EPS) * gw_ref[...] + gb_ref[...]
        oa_ref[...] = o * _sig(o)

    return pl.pallas_call(
        body, name="conf_fwd", grid=(NT,),
        in_specs=[pl.BlockSpec((TM, CW), lambda i: (i, 0)), pl.BlockSpec((TM, CW), lambda i: (i, 1)),
                  _const((KC, CW)), _const((1, CW)), _const((1, CW)), _const((1, CW))],
        out_specs=(pl.BlockSpec((TM, CW), lambda i: (i, 0)), pl.BlockSpec((TM, CW), lambda i: (i, 0))),
        out_shape=(jax.ShapeDtypeStruct((S, CW), F32), jax.ShapeDtypeStruct((S, CW), F32)),
        scratch_shapes=[pltpu.VMEM((HALO + TM, CW), F32), pltpu.VMEM((SUB - 1, SHIFT_ROWS, CW), F32)],
        compiler_params=_params(dimension_semantics=("arbitrary",)),
    )(p_main, p_main, conv_w, conv_b, gn_w, gn_b)


def _tri_iota():
    ii = lax.broadcasted_iota(jnp.int32, (CL, CL), 0)
    jj = lax.broadcasted_iota(jnp.int32, (CL, CL), 1)
    return ii, jj


def _gdn_gates(ba, alog_l, dt_l):
    beta_all = _sig(ba)
    xg = ba + dt_l
    sp = jnp.maximum(xg, 0.0) + jnp.log(1.0 + jnp.exp(-jnp.abs(xg)))
    neg_a = -jnp.exp(alog_l)
    return beta_all, neg_a * sp, xg, neg_a


def _ones_dot(ones, x):
    hi = x.astype(BF16)
    r1 = x - hi.astype(F32)
    mid = r1.astype(BF16)
    lo = (r1 - mid.astype(F32)).astype(BF16)
    return _dot(ones, hi) + _dot(ones, mid) + _dot(ones, lo)


def _gdn_cumsum(g_all):
    ii, jj = _tri_iota()
    low = jnp.where(ii >= jj, 1.0, 0.0).astype(BF16)
    gcum = _ones_dot(low, g_all)
    return gcum, jnp.transpose(gcum)


def _split(x):
    hi = x.astype(BF16)
    return hi, (x - hi.astype(F32)).astype(BF16)


def _dot_split(a, b):
    (ah, al), (bh, bl) = a, b
    return _dot(ah, bh) + (_dot(ah, bl) + _dot(al, bh))


def _unit_lower_inverses(mats):
    ii, jj = _tri_iota()
    eye = jnp.where(ii == jj, 1.0, 0.0).astype(F32)
    ts = [eye - a for a in mats]
    ps = [_dot_split(s, s) for s in map(_split, mats)]
    for _ in range(4):
        sp = [_split(p) for p in ps]
        ts = [t + _dot_split(_split(t), s) for t, s in zip(ts, sp)]
        ps = [_dot_split(s, s) for s in sp]
    return [t + _dot_split(_split(t), _split(p)) for t, p in zip(ts, ps)]


def _head_terms(qh, kh, beta, gcol, grow):
    ii, jj = _tri_iota()
    causal = ii >= jj
    strict = ii > jj
    rq = lax.rsqrt(_rowsum(qh * qh) + EPS)
    rk = lax.rsqrt(_rowsum(kh * kh) + EPS)
    qn = qh * rq
    kn = kh * rk
    qs = qn * QSCALE
    decay = jnp.where(causal, jnp.exp(jnp.where(causal, gcol - grow, 0.0)), 0.0)
    gam = jnp.exp(gcol)
    gl = gcol[CL - 1:CL, :]
    kds = jnp.exp(gl - gcol)
    cd = jnp.exp(gl)
    kb = kn * beta
    a = jnp.where(strict, _dot_nt(kb, kn, GP) * decay, 0.0)
    qk = jnp.where(causal, _dot_nt(qs, kn, GP) * decay, 0.0)
    return dict(rq=rq, rk=rk, qn=qn, kn=kn, qs=qs, decay=decay, gam=gam, kds=kds, cd=cd, kb=kb, a=a, qk=qk,
                causal=causal, strict=strict)


def _short_conv(w_ref, buf, rows=CL):
    acc = w_ref[0:1, :] * buf[SH - KS + 1:SH - KS + 1 + rows, :]
    for k in range(1, KS):
        off = SH - (KS - 1) + k
        acc = acc + w_ref[k:k + 1, :] * buf[off:off + rows, :]
    return acc


CPS = 4
TG = CPS * CL


def _gdn_prep(p_main, p_ba, gdn_conv_w, alog_l, dt_l):
    def body(q_ref, k_ref, v_ref, qh_ref, kh_ref, vh_ref, ba_ref, w_ref, al_ref, dt_ref,
             wo_ref, uo_ref, qg_ref, kd_ref, qk_ref, cd_ref, t_ref, xbuf):
        i = pl.program_id(0)
        first = i == 0
        xbuf[0:SH, 0:GW] = jnp.where(first, 0.0, qh_ref[...])
        xbuf[0:SH, GW:2 * GW] = jnp.where(first, 0.0, kh_ref[...])
        xbuf[0:SH, 2 * GW:3 * GW] = jnp.where(first, 0.0, vh_ref[...])
        xbuf[SH:SH + TG, 0:GW] = q_ref[...]
        xbuf[SH:SH + TG, GW:2 * GW] = k_ref[...]
        xbuf[SH:SH + TG, 2 * GW:3 * GW] = v_ref[...]
        conv = _short_conv(w_ref, xbuf, TG)
        qkv = conv * _sig(conv)
        beta_all, g_all, _, _ = _gdn_gates(ba_ref[...], al_ref[...], dt_ref[...])
        lane = lax.broadcasted_iota(jnp.int32, (8, LANES), 1)
        cums = [_gdn_cumsum(g_all[cc * CL:(cc + 1) * CL, :]) for cc in range(CPS)]
        pairs = [(cc, h) for cc in range(CPS) for h in range(NH)]
        terms, vbs = [], []
        for cc, h in pairs:
            r0, lo = cc * CL, h * DH
            beta = beta_all[r0:r0 + CL, h:h + 1]
            gcum, gcum_t = cums[cc]
            terms.append(_head_terms(qkv[r0:r0 + CL, lo:lo + DH], qkv[r0:r0 + CL, GW + lo:GW + lo + DH], beta,
                                     gcum[:, NH + h:NH + h + 1], gcum_t[NH + h:NH + h + 1, :]))
            vbs.append(qkv[r0:r0 + CL, 2 * GW + lo:2 * GW + lo + DH] * beta)
        invs = _unit_lower_inverses([f["a"] for f in terms])
        cds = [jnp.zeros((8, LANES), F32) for _ in range(CPS)]
        for (cc, h), f, t, vb in zip(pairs, terms, invs, vbs):
            r0, lo = cc * CL, h * DH
            t_ref[cc, h] = t
            uo_ref[r0:r0 + CL, lo:lo + DH] = _dot(t, vb, GP)
            wo_ref[r0:r0 + CL, lo:lo + DH] = _dot(t, f["kb"] * f["gam"], GP).astype(BF16)
            qg_ref[r0:r0 + CL, lo:lo + DH] = (f["qs"] * f["gam"]).astype(BF16)
            kd_ref[r0:r0 + CL, lo:lo + DH] = (f["kn"] * f["kds"]).astype(BF16)
            qk_ref[cc, h] = f["qk"].astype(BF16)
            cds[cc] = cds[cc] + jnp.where(lane == h, f["cd"], 0.0)
        for cc in range(CPS):
            cd_ref[cc] = cds[cc]

    col = lambda j: pl.BlockSpec((TG, GW), lambda i: (i, j))
    halo = lambda j: pl.BlockSpec((SH, GW), lambda i: (jnp.maximum(i * (TG // SH) - 1, 0), j))
    tile = lambda: pl.BlockSpec((TG, GW), lambda i: (i, 0))
    sq = lambda: pl.BlockSpec((CPS, NH, CL, CL), lambda i: (i, 0, 0, 0))
    return pl.pallas_call(
        body, name="gdn_prep", grid=(NCH // CPS,),
        in_specs=[col(2), col(3), col(4), halo(2), halo(3), halo(4), pl.BlockSpec((TG, LANES), lambda i: (i, 0)),
                  _const((KS, 3 * GW)), _const((1, LANES)), _const((1, LANES))],
        out_specs=(tile(), tile(), tile(), tile(), sq(), pl.BlockSpec((CPS, 8, LANES), lambda i: (i, 0, 0)), sq()),
        out_shape=(jax.ShapeDtypeStruct((S, GW), BF16), jax.ShapeDtypeStruct((S, GW), F32),
                   jax.ShapeDtypeStruct((S, GW), BF16), jax.ShapeDtypeStruct((S, GW), BF16),
                   jax.ShapeDtypeStruct((NCH, NH, CL, CL), BF16), jax.ShapeDtypeStruct((NCH, 8, LANES), F32),
                   jax.ShapeDtypeStruct((NCH, NH, CL, CL), F32)),
        scratch_shapes=[pltpu.VMEM((SH + TG, 3 * GW), F32)],
        compiler_params=_params(dimension_semantics=("arbitrary",)),
    )(p_main, p_main, p_main, p_main, p_main, p_main, p_ba, gdn_conv_w, alog_l, dt_l)


def _gdn_scan(w_o, u_o, qg, kd, qk, cd, p_main, gdn_nw):
    def body(w_ref, u_ref, qg_ref, kd_ref, qk_ref, cd_ref, z_ref, nw_ref, ob_ref, o_ref, sin_ref, state):
        n = pl.program_id(0)

        @pl.when(n == 0)
        def _():
            state[...] = jnp.zeros((NH, DH, DH), F32)

        def head(h):
            lo = h * DH
            st = state[h]
            sin_ref[0, h] = st
            sb = st.astype(BF16)
            v_new = u_ref[:, lo:lo + DH] - _dot(w_ref[:, lo:lo + DH], sb)
            yield
            vb = v_new.astype(BF16)
            o = _dot(qg_ref[:, lo:lo + DH], sb) + _dot(qk_ref[0, h], vb)
            state[h] = st * cd_ref[0, 0:1, h:h + 1] + _dot_tn(kd_ref[:, lo:lo + DH], vb)
            yield
            o_ref[:, lo:lo + DH] = o
            r = lax.rsqrt(jnp.mean(o * o, axis=-1, keepdims=True) + EPS)
            zh = z_ref[:, lo:lo + DH]
            ob_ref[:, lo:lo + DH] = o * r * nw_ref[...] * (zh * _sig(zh))

        _lockstep(head(h) for h in range(NH))

    tile = lambda: pl.BlockSpec((CL, GW), lambda n: (n, 0))
    return pl.pallas_call(
        body, name="gdn_scan", grid=(NCH,),
        in_specs=[tile(), tile(), tile(), tile(), pl.BlockSpec((1, NH, CL, CL), lambda n: (n, 0, 0, 0)),
                  pl.BlockSpec((1, 8, LANES), lambda n: (n, 0, 0)), pl.BlockSpec((CL, GW), lambda n: (n, 5)),
                  _const((1, DH))],
        out_specs=(tile(), tile(), pl.BlockSpec((1, NH, DH, DH), lambda n: (n, 0, 0, 0))),
        out_shape=(jax.ShapeDtypeStruct((S, GW), F32), jax.ShapeDtypeStruct((S, GW), F32),
                   jax.ShapeDtypeStruct((NCH, NH, DH, DH), F32)),
        scratch_shapes=[pltpu.VMEM((NH, DH, DH), F32)],
        compiler_params=_params(dimension_semantics=("arbitrary",)),
    )(w_o, u_o, qg, kd, qk, cd, p_main, gdn_nw)


def _fwd_out(out_a, out_b, x, modnb, bada, w_out):
    def body(oa_ref, ob_ref, x_ref, mod_ref, b_ref, w_ref, x1_ref, mix_ref, oab_ref):
        oa = oa_ref[...].astype(BF16)
        ob = ob_ref[...].astype(BF16)
        oab_ref[:, 0:CW] = oa
        oab_ref[:, CW:D] = ob
        mix = _dot(oa, w_ref[0:CW, :]) + _dot(ob, w_ref[CW:D, :])
        mix_ref[...] = mix
        x1_ref[...] = x_ref[...] + _mod(mod_ref, b_ref, 2) * mix

    tile = lambda w: pl.BlockSpec((TM, w), lambda i: (i, 0))
    return pl.pallas_call(
        body, name="fwd_out", grid=(NT,),
        in_specs=[tile(CW), tile(GW), tile(D), _const((1, 6 * D)), _const((1, 6 * D)), _const((D, D))],
        out_specs=(tile(D), tile(D), tile(D)),
        out_shape=(jax.ShapeDtypeStruct((S, D), F32), jax.ShapeDtypeStruct((S, D), F32),
                   jax.ShapeDtypeStruct((S, D), BF16)),
        compiler_params=_params(dimension_semantics=("arbitrary",)),
    )(out_a, out_b, x, modnb, bada, w_out)


FFN_STATS = 8


def _ffn_forward(x1, tgt, modnb, bada, nw2, nfw, w_fi, w_fo):
    def body(x1_ref, tgt_ref, mod_ref, b_ref, nw2_ref, nfw_ref, wi_ref, wo_ref,
             hb_ref, act_ref, pre_ref, dx2_ref, dffn_ref, st_ref):
        i = pl.program_id(0)

        @pl.when(i == 0)
        def _():
            st_ref[...] = jnp.zeros((FFN_STATS, D), F32)

        sh2, sc2, gt2 = _mod(mod_ref, b_ref, 3), _mod(mod_ref, b_ref, 4), _mod(mod_ref, b_ref, 5)
        x1v = x1_ref[...]
        r2 = lax.rsqrt(jnp.mean(x1v * x1v, axis=-1, keepdims=True) + EPS)
        hb = ((x1v * r2 * nw2_ref[...]) * (1.0 + sc2) + sh2).astype(BF16)
        hb_ref[...] = hb
        ffn = jnp.zeros((TM, D), F32)
        for j in range(4):
            fgj = _dot_nt(hb, wi_ref[j])
            fuj = _dot_nt(hb, wi_ref[j + 4])
            pre_ref[j] = fgj.astype(BF16)
            pre_ref[j + 4] = fuj.astype(BF16)
            aj = (fgj * _sig(fgj) * fuj).astype(BF16)
            act_ref[j] = aj
            ffn = ffn + _dot(aj, wo_ref[j])
        x2 = x1v + gt2 * ffn
        r3 = lax.rsqrt(jnp.mean(x2 * x2, axis=-1, keepdims=True) + EPS)
        xr3 = x2 * r3
        err = xr3 * nfw_ref[...] - tgt_ref[...]
        loss = 0.5 * jnp.sum(jnp.mean(err * err, axis=-1, keepdims=True), axis=0, keepdims=True)
        dy = err * (1.0 / D)
        st_ref[0:1, :] += _colsum(dy * xr3)
        dyr = dy * nfw_ref[...]
        dx2 = r3 * (dyr - xr3 * jnp.mean(dyr * xr3, axis=-1, keepdims=True))
        st_ref[1:2, :] += _colsum(dx2 * ffn)
        st_ref[5:6, :] += jnp.broadcast_to(loss, (1, D))
        dx2_ref[...] = dx2
        dffn_ref[...] = (gt2 * dx2).astype(BF16)

    tile = lambda w: pl.BlockSpec((TM, w), lambda i: (i, 0))
    return pl.pallas_call(
        body, name="ffn_forward", grid=(NT,),
        in_specs=[tile(D), tile(D), _const((1, 6 * D)), _const((1, 6 * D)), _const((1, D)), _const((1, D)),
                  _const1((N_DEV, FB, D)), _const1((4, FB, D))],
        out_specs=(tile(D), pl.BlockSpec((4, TM, FB), lambda i: (0, i, 0)),
                   pl.BlockSpec((N_DEV, TM, FB), lambda i: (0, i, 0)), tile(D), tile(D), _const((FFN_STATS, D))),
        out_shape=(jax.ShapeDtypeStruct((S, D), BF16), jax.ShapeDtypeStruct((4, S, FB), BF16),
                   jax.ShapeDtypeStruct((N_DEV, S, FB), BF16), jax.ShapeDtypeStruct((S, D), F32),
                   jax.ShapeDtypeStruct((S, D), BF16), jax.ShapeDtypeStruct((FFN_STATS, D), F32)),
        compiler_params=_params(42, dimension_semantics=("arbitrary",)),
    )(x1, tgt, modnb, bada, nw2, nfw, w_fi, w_fo)


def _ffn_backward(dffn, pre, x1, dx2, modnb, bada, nw2, w_fi, w_fo):
    def body(dffn_ref, pre_ref, x1_ref, dx2_ref, mod_ref, b_ref, nw2_ref, wi_ref, wo_ref, df_ref, dx1_ref, st_ref):
        i = pl.program_id(0)

        @pl.when(i == 0)
        def _():
            st_ref[...] = jnp.zeros((FFN_STATS, D), F32)

        dffn = dffn_ref[...]
        dh = jnp.zeros((TM, D), F32)
        for j in range(4):
            fg = pre_ref[j].astype(F32)
            fu = pre_ref[j + 4].astype(F32)
            sg = _sig(fg)
            dact = _dot_nt(dffn, wo_ref[j])
            dfg = (dact * fu * (sg * (1.0 + fg * (1.0 - sg)))).astype(BF16)
            dfu = (dact * (fg * sg)).astype(BF16)
            df_ref[j] = dfg
            df_ref[j + 4] = dfu
            dh = dh + _dot(dfg, wi_ref[j]) + _dot(dfu, wi_ref[j + 4])
        x1v = x1_ref[...]
        r2 = lax.rsqrt(jnp.mean(x1v * x1v, axis=-1, keepdims=True) + EPS)
        xr2 = x1v * r2
        st_ref[2:3, :] += _colsum(dh)
        st_ref[3:4, :] += _colsum(dh * (xr2 * nw2_ref[...]))
        dxn = dh * (1.0 + _mod(mod_ref, b_ref, 4))
        st_ref[4:5, :] += _colsum(dxn * xr2)
        dxr = dxn * nw2_ref[...]
        dx1_ref[...] = dx2_ref[...] + r2 * (dxr - xr2 * jnp.mean(dxr * xr2, axis=-1, keepdims=True))

    tile = lambda w: pl.BlockSpec((TM, w), lambda i: (i, 0))
    wide = lambda: pl.BlockSpec((N_DEV, TM, FB), lambda i: (0, i, 0))
    return pl.pallas_call(
        body, name="ffn_backward", grid=(NT,),
        in_specs=[tile(D), wide(), tile(D), tile(D), _const((1, 6 * D)), _const((1, 6 * D)), _const((1, D)),
                  _const1((N_DEV, FB, D)), _const1((4, FB, D))],
        out_specs=(wide(), tile(D), _const((FFN_STATS, D))),
        out_shape=(jax.ShapeDtypeStruct((N_DEV, S, FB), BF16), jax.ShapeDtypeStruct((S, D), F32),
                   jax.ShapeDtypeStruct((FFN_STATS, D), F32)),
        compiler_params=_params(44, dimension_semantics=("arbitrary",)),
    )(dffn, pre, x1, dx2, modnb, bada, nw2, w_fi, w_fo)


def _grad_w(name, a, b, nb):
    m, n = a.shape[1], b.shape[1]

    def body(a_ref, b_ref, o_ref):
        o_ref[...] = _dot_tn(a_ref[...], b_ref[...]).astype(BF16)

    return pl.pallas_call(
        body, name=name, grid=(m // nb,),
        in_specs=[pl.BlockSpec((S, nb), lambda j: (0, j)), _const((S, n))],
        out_specs=pl.BlockSpec((nb, n), lambda j: (j, 0)),
        out_shape=jax.ShapeDtypeStruct((m, n), BF16),
        compiler_params=_params(dimension_semantics=("arbitrary",)),
    )(a, b)


def _grad_w_ffn_in(hb2, df):
    def body(a_ref, b_ref, o_ref):
        o_ref[0] = _dot_tn(b_ref[0], a_ref[...]).astype(BF16)

    return pl.pallas_call(
        body, name="grad_w_ffn_in", grid=(N_DEV,),
        in_specs=[_const((S, D)), pl.BlockSpec((1, S, FB), lambda j: (j, 0, 0))],
        out_specs=pl.BlockSpec((1, FB, D), lambda j: (j, 0, 0)),
        out_shape=jax.ShapeDtypeStruct((N_DEV, FB, D), BF16),
        compiler_params=_params(dimension_semantics=("arbitrary",)),
    )(hb2, df)


def _grad_w_ffn_out(act, dffn):
    def body(a_ref, b_ref, o_ref):
        o_ref[0] = _dot_tn(a_ref[0], b_ref[...]).astype(BF16)

    return pl.pallas_call(
        body, name="grad_w_ffn_out", grid=(4,),
        in_specs=[pl.BlockSpec((1, S, FB), lambda j: (j, 0, 0)), _const((S, D))],
        out_specs=pl.BlockSpec((1, FB, D), lambda j: (j, 0, 0)),
        out_shape=jax.ShapeDtypeStruct((4, FB, D), BF16),
        compiler_params=_params(dimension_semantics=("arbitrary",)),
    )(act, dffn)


def _bwd_out(dx1, mix, modnb, bada, w_out):
    def body(dx_ref, mix_ref, mod_ref, b_ref, w_ref, dmix_ref, doa_ref, dob_ref, st_ref):
        i = pl.program_id(0)

        @pl.when(i == 0)
        def _():
            st_ref[...] = jnp.zeros((8, D), F32)

        dx = dx_ref[...]
        st_ref[0:1, :] += _colsum(dx * mix_ref[...])
        dmix = (_mod(mod_ref, b_ref, 2) * dx).astype(BF16)
        dmix_ref[...] = dmix
        doa_ref[...] = _dot_nt(dmix, w_ref[0:CW, :])
        dob_ref[...] = _dot_nt(dmix, w_ref[CW:D, :])

    tile = lambda w: pl.BlockSpec((TM, w), lambda i: (i, 0))
    return pl.pallas_call(
        body, name="bwd_out", grid=(NT,),
        in_specs=[tile(D), tile(D), _const((1, 6 * D)), _const((1, 6 * D)), _const((D, D))],
        out_specs=(tile(D), tile(CW), tile(GW), _const((8, D))),
        out_shape=(jax.ShapeDtypeStruct((S, D), BF16), jax.ShapeDtypeStruct((S, CW), F32),
                   jax.ShapeDtypeStruct((S, GW), F32), jax.ShapeDtypeStruct((8, D), F32)),
        compiler_params=_params(dimension_semantics=("arbitrary",)),
    )(dx1, mix, modnb, bada, w_out)


CONF_STATS = 40


def _conf_bwd(d_out_a, y, p_main, conv_w, gn_w, gn_b):
    def body(do_ref, y_ref, a_ref, g_ref, ah_ref, gh_ref, w_ref, gw_ref, gb_ref, dp_ref, st_ref,
             ubuf, dybuf, ush, dysh):
        i = pl.program_id(0)

        @pl.when(i == 0)
        def _():
            st_ref[...] = jnp.zeros((CONF_STATS, CW), F32)
            dybuf[TM:TM + HALO, :] = jnp.zeros((HALO, CW), F32)

        pm = _group_mean_matrix().astype(BF16)
        yv = y_ref[...]
        dlt = yv - _group_mean(yv, pm)
        rstd = lax.rsqrt(_group_mean(dlt * dlt, pm) + EPS)
        un = dlt * rstd
        o = un * gw_ref[...] + gb_ref[...]
        so = _sig(o)
        d_o = do_ref[...] * (so * (1.0 + o * (1.0 - so)))
        st_ref[33:34, :] += _colsum(d_o)
        st_ref[32:33, :] += _colsum(d_o * un)
        dun = d_o * gw_ref[...]
        dy = rstd * (dun - _group_mean(dun, pm) - un * _group_mean(dun * un, pm))
        st_ref[31:32, :] += _colsum(dy)
        dybuf[0:TM, :] = dy
        _fill_shifted(dybuf, dysh)

        a = a_ref[...]
        sg = _sig(g_ref[...])
        first = i == NT - 1
        ubuf[0:HALO, :] = jnp.where(first, 0.0, ah_ref[...] * _sig(gh_ref[...]))
        ubuf[HALO:HALO + TM, :] = a * sg
        _fill_shifted(ubuf, ush)
        du = jnp.zeros((TM, CW), F32)
        for k in range(KC):
            st_ref[k:k + 1, :] += _colsum(dy * _rows_at(ubuf, ush, HALO - (KC - 1) + k))
            du = du + w_ref[k:k + 1, :] * _rows_at(dybuf, dysh, KC - 1 - k)
        dybuf[TM:TM + HALO, :] = dybuf[0:HALO, :]
        dp_ref[:, 0:CW] = (du * sg).astype(BF16)
        dp_ref[:, CW:2 * CW] = (du * a * sg * (1.0 - sg)).astype(BF16)

    rev = lambda w, j=0: pl.BlockSpec((TM, w), lambda i: (NT - 1 - i, j))
    halo = lambda j: pl.BlockSpec((HALO, CW), lambda i: (jnp.maximum((NT - 1 - i) * (TM // HALO) - 1, 0), j))
    return pl.pallas_call(
        body, name="conf_bwd", grid=(NT,),
        in_specs=[rev(CW), rev(CW), rev(CW, 0), rev(CW, 1), halo(0), halo(1),
                  _const((KC, CW)), _const((1, CW)), _const((1, CW))],
        out_specs=(rev(2 * CW), _const((CONF_STATS, CW))),
        out_shape=(jax.ShapeDtypeStruct((S, 2 * CW), BF16), jax.ShapeDtypeStruct((CONF_STATS, CW), F32)),
        scratch_shapes=[pltpu.VMEM((HALO + TM, CW), F32), pltpu.VMEM((TM + HALO, CW), F32),
                        pltpu.VMEM((SUB - 1, SHIFT_ROWS, CW), F32), pltpu.VMEM((SUB - 1, SHIFT_ROWS, CW), F32)],
        compiler_params=_params(dimension_semantics=("arbitrary",)),
    )(d_out_a, y, p_main, p_main, p_main, p_main, conv_w, gn_w, gn_b)


GDN_STATS = 8


def _gdn_bwd(d_out_b, o_pre, s_in, t_inv, p_main, p_ba, gdn_conv_w, alog_l, dt_l, gdn_nw):
    def body(dob_ref, o_ref, sin_ref, t_ref, q_ref, k_ref, v_ref, z_ref, qh_ref, kh_ref, vh_ref, ba_ref,
             w_ref, al_ref, dt_ref, nw_ref, dp_ref, dba_ref, st_ref, xbuf, dcbuf, dstate):
        n = pl.program_id(0)

        @pl.when(n == 0)
        def _():
            st_ref[...] = jnp.zeros((GDN_STATS, 3 * GW), F32)
            dcbuf[CL:CL + SH, :] = jnp.zeros((SH, 3 * GW), F32)
            dstate[...] = jnp.zeros((NH, DH, DH), F32)

        first = n == NCH - 1
        xbuf[0:SH, 0:GW] = jnp.where(first, 0.0, qh_ref[...])
        xbuf[0:SH, GW:2 * GW] = jnp.where(first, 0.0, kh_ref[...])
        xbuf[0:SH, 2 * GW:3 * GW] = jnp.where(first, 0.0, vh_ref[...])
        xbuf[SH:SH + CL, 0:GW] = q_ref[...]
        xbuf[SH:SH + CL, GW:2 * GW] = k_ref[...]
        xbuf[SH:SH + CL, 2 * GW:3 * GW] = v_ref[...]
        conv = _short_conv(w_ref, xbuf)
        sc = _sig(conv)
        qkv = conv * sc
        ba = ba_ref[...]
        beta_all, g_all, xg, neg_a = _gdn_gates(ba, al_ref[...], dt_ref[...])
        gcum, gcum_t = _gdn_cumsum(g_all)
        lane = lax.broadcasted_iota(jnp.int32, (CL, LANES), 1)
        row = lax.broadcasted_iota(jnp.int32, (CL, 1), 0)
        acc = dict(dgcum=jnp.zeros((CL, LANES), F32), dbeta=jnp.zeros((CL, LANES), F32))

        def head(h):
            lo = h * DH
            qh = qkv[:, lo:lo + DH]
            kh = qkv[:, GW + lo:GW + lo + DH]
            vh = qkv[:, 2 * GW + lo:2 * GW + lo + DH]
            beta = beta_all[:, h:h + 1]
            f = _head_terms(qh, kh, beta, gcum[:, NH + h:NH + h + 1], gcum_t[NH + h:NH + h + 1, :])
            qn, kn, qs, kb, gam, kds, cd, decay = (f[s] for s in ("qn", "kn", "qs", "kb", "gam", "kds", "cd", "decay"))
            t = t_ref[0, h]
            st = sin_ref[0, h]
            vb = vh * beta
            kbg = kb * gam
            u = _dot(t, vb, GP)
            w = _dot(t, kbg, GP)
            yield
            v_new = u - _dot(w, st, GP)
            q_dec = qs * gam
            k_dec = kn * kds

            o = o_ref[:, lo:lo + DH]
            zh = z_ref[:, lo:lo + DH]
            sz = _sig(zh)
            r = lax.rsqrt(jnp.mean(o * o, axis=-1, keepdims=True) + EPS)
            orr = o * r
            d_out = dob_ref[:, lo:lo + DH]
            dz = d_out * (orr * nw_ref[...]) * (sz * (1.0 + zh * (1.0 - sz)))
            don = d_out * (zh * sz)
            st_ref[4:5, 0:DH] += _colsum(don * orr)
            tt = don * nw_ref[...]
            d_o = r * (tt - orr * jnp.mean(tt * orr, axis=-1, keepdims=True))

            yield
            ds_out = dstate[h]
            dv_new = _dot_tn(f["qk"], d_o, GP) + _dot(k_dec, ds_out, GP)
            dqk = jnp.where(f["causal"], _dot_nt(d_o, v_new, GP), 0.0)
            dq_dec = _dot_nt(d_o, st, GP)
            dk_dec = _dot_nt(v_new, ds_out, GP)
            yield
            dstate[h] = _dot_tn(q_dec, d_o, GP) + cd * ds_out - _dot_tn(w, dv_new, GP)
            dcd = jnp.sum(_rowsum(st * ds_out), axis=0, keepdims=True)
            dw = -_dot_nt(dv_new, st, GP)
            dvb = _dot_tn(t, dv_new, GP)
            yield
            dt_m = _dot_nt(dv_new, vb, GP) + _dot_nt(dw, kbg, GP)
            dkbg = _dot_tn(t, dw, GP)
            yield
            dtt = _dot_nt(dt_m, t, GP)
            yield
            da = jnp.where(f["strict"], -_dot_tn(t, dtt, GP), 0.0)
            yield
            dad = da * decay
            dqkd = dqk * decay
            dkb = _dot(dad, kn, GP) + dkbg * gam
            dkn = _dot_tn(dad, kb, GP) + _dot_tn(dqkd, qs, GP) + dk_dec * kds + dkb * beta
            dqs = _dot(dqkd, kn, GP) + dq_dec * gam
            yield
            m = da * f["a"] + dqk * f["qk"]
            tk = _rowsum(dk_dec * k_dec)
            dgl = jnp.sum(tk, axis=0, keepdims=True) + dcd * cd
            dgc = (_rowsum(m) - _rowsum(jnp.transpose(m)) + _rowsum(dq_dec * q_dec) - tk + _rowsum(dkbg * kbg)
                   + jnp.where(row == CL - 1, dgl, 0.0))
            dbeta = _rowsum(dkb * kn) + _rowsum(dvb * vh)
            acc["dgcum"] = acc["dgcum"] + jnp.where(lane == NH + h, dgc, 0.0)
            acc["dbeta"] = acc["dbeta"] + jnp.where(lane == h, dbeta, 0.0)
            dvh = dvb * beta
            dqn = dqs * QSCALE
            dqh = f["rq"] * (dqn - qn * _rowsum(dqn * qn))
            dkh = f["rk"] * (dkn - kn * _rowsum(dkn * kn))
            dsilu = lambda c0: sc[:, c0:c0 + DH] * (1.0 + conv[:, c0:c0 + DH] * (1.0 - sc[:, c0:c0 + DH]))
            dcbuf[0:CL, lo:lo + DH] = dqh * dsilu(lo)
            dcbuf[0:CL, GW + lo:GW + lo + DH] = dkh * dsilu(GW + lo)
            dcbuf[0:CL, 2 * GW + lo:2 * GW + lo + DH] = dvh * dsilu(2 * GW + lo)
            dp_ref[:, 3 * GW + lo:3 * GW + lo + DH] = dz.astype(BF16)

        _lockstep(head(h) for h in range(NH))
        dgcum_all, dbeta_all = acc["dgcum"], acc["dbeta"]

        ii, jj = _tri_iota()
        upper = jnp.where(ii <= jj, 1.0, 0.0).astype(BF16)
        dg_all = _ones_dot(upper, dgcum_all)
        dxg = dg_all * neg_a * _sig(xg)
        st_ref[5:6, 0:LANES] += _colsum(dg_all * g_all)
        st_ref[6:7, 0:LANES] += _colsum(dxg)
        dbl = dbeta_all * beta_all * (1.0 - beta_all)
        dba_ref[...] = jnp.where(lane < NH, dbl, jnp.where(lane < 2 * NH, dxg, 0.0)).astype(BF16)

        dconv = dcbuf[0:CL, :]
        dx = w_ref[0:1, :] * dcbuf[KS - 1:KS - 1 + CL, :]
        st_ref[0:1, :] += _colsum(dconv * xbuf[SH - KS + 1:SH - KS + 1 + CL, :])
        for k in range(1, KS):
            off = SH - (KS - 1) + k
            st_ref[k:k + 1, :] += _colsum(dconv * xbuf[off:off + CL, :])
            dx = dx + w_ref[k:k + 1, :] * dcbuf[KS - 1 - k:KS - 1 - k + CL, :]
        dcbuf[CL:CL + SH, :] = dcbuf[0:SH, :]
        dp_ref[:, 0:3 * GW] = dx.astype(BF16)

    rev = lambda w, j=0: pl.BlockSpec((CL, w), lambda n: (NCH - 1 - n, j))
    halo = lambda j: pl.BlockSpec((SH, GW), lambda n: (jnp.maximum((NCH - 1 - n) * (CL // SH) - 1, 0), j))
    blk4 = lambda a, b: pl.BlockSpec((1, NH, a, b), lambda n: (NCH - 1 - n, 0, 0, 0))
    return pl.pallas_call(
        body, name="gdn_bwd", grid=(NCH,),
        in_specs=[rev(GW), rev(GW), blk4(DH, DH), blk4(CL, CL), rev(GW, 2), rev(GW, 3), rev(GW, 4), rev(GW, 5),
                  halo(2), halo(3), halo(4), rev(LANES), _const((KS, 3 * GW)), _const((1, LANES)),
                  _const((1, LANES)), _const((1, DH))],
        out_specs=(rev(4 * GW), rev(LANES), _const((GDN_STATS, 3 * GW))),
        out_shape=(jax.ShapeDtypeStruct((S, 4 * GW), BF16), jax.ShapeDtypeStruct((S, LANES), BF16),
                   jax.ShapeDtypeStruct((GDN_STATS, 3 * GW), F32)),
        scratch_shapes=[pltpu.VMEM((SH + CL, 3 * GW), F32), pltpu.VMEM((CL + SH, 3 * GW), F32),
                        pltpu.VMEM((NH, DH, DH), F32)],
        compiler_params=_params(dimension_semantics=("arbitrary",)),
    )(d_out_b, o_pre, s_in, t_inv, p_main, p_main, p_main, p_main, p_main, p_main, p_main, p_ba,
      gdn_conv_w, alog_l, dt_l, gdn_nw)


def _bwd_in(dp_conf, dp_gdn, dp_ba, x, dx1, nw1, modnb, bada, w_main, w_ba):
    def body(dc_ref, dg_ref, db_ref, x_ref, dx1_ref, nw_ref, mod_ref, b_ref, wm_ref, wb_ref, gx_ref, st_ref):
        i = pl.program_id(0)

        @pl.when(i == 0)
        def _():
            st_ref[...] = jnp.zeros((8, D), F32)

        dh = (_dot(dc_ref[...], wm_ref[0:2 * CW, :]) + _dot(dg_ref[...], wm_ref[2 * CW:NMAIN, :])
              + _dot(db_ref[...], wb_ref[...]))
        xv = x_ref[...]
        r = lax.rsqrt(jnp.mean(xv * xv, axis=-1, keepdims=True) + EPS)
        xr = xv * r
        st_ref[0:1, :] += _colsum(dh)
        st_ref[1:2, :] += _colsum(dh * (xr * nw_ref[...]))
        dxn = dh * (1.0 + _mod(mod_ref, b_ref, 1))
        st_ref[2:3, :] += _colsum(dxn * xr)
        dxr = dxn * nw_ref[...]
        gx_ref[...] = dx1_ref[...] + r * (dxr - xr * jnp.mean(dxr * xr, axis=-1, keepdims=True))

    tile = lambda w: pl.BlockSpec((TM, w), lambda i: (i, 0))
    return pl.pallas_call(
        body, name="bwd_in", grid=(NT,),
        in_specs=[tile(2 * CW), tile(4 * GW), tile(LANES), tile(D), tile(D), _const((1, D)), _const((1, 6 * D)),
                  _const((1, 6 * D)), _const((NMAIN, D)), _const((LANES, D))],
        out_specs=(tile(D), _const((8, D))),
        out_shape=(jax.ShapeDtypeStruct((S, D), F32), jax.ShapeDtypeStruct((8, D), F32)),
        compiler_params=_params(dimension_semantics=("arbitrary",)),
    )(dp_conf, dp_gdn, dp_ba, x, dx1, nw1, modnb, bada, w_main, w_ba)


def _adamw(w, g, m, v):
    m = ADAM_B1 * m + (1.0 - ADAM_B1) * g
    v = ADAM_B2 * v + (1.0 - ADAM_B2) * (g * g)
    m_hat = m / BC1
    v_hat = v / BC2
    delta = -ADAM_LR * (m_hat / (jnp.sqrt(v_hat) + ADAM_EPS) + ADAM_WD * w)
    return delta, m, v


ADAM_BLOCK_BYTES = 6 * 1024 * 1024


def _adam_tile(rows, cols):
    padded = -(-cols // LANES) * LANES
    if N_DEV * rows * padded * 4 <= ADAM_BLOCK_BYTES:
        return rows, cols
    best = None
    for tr in range(16, rows, 16):
        if rows % tr == 0 and N_DEV * tr * padded * 4 <= ADAM_BLOCK_BYTES:
            best = tr
    if best is not None:
        return best, cols
    rows_padded = -(-rows // 16) * 16
    tc = LANES
    for cand in range(LANES, cols, LANES):
        if cols % cand == 0 and N_DEV * rows_padded * cand * 4 <= ADAM_BLOCK_BYTES:
            tc = cand
    return rows, tc


def _reduce_adam(name, parts, w, m, v, own=None):
    rows, cols = w.shape
    tr, tc = _adam_tile(rows, cols)

    def body(*refs):
        p_ref, w_ref, m_ref, v_ref = refs[:4]
        g_ref, d_ref, nm_ref, nv_ref = refs[-4:]
        if own is None:
            part = lambda j: p_ref[j].astype(F32)
        else:
            me = 4 * lax.axis_index("x") + 2 * lax.axis_index("y") + lax.axis_index("c")
            part = lambda j: jnp.where(me == j, refs[4][...], p_ref[j]).astype(F32)
        g = part(0)
        for j in range(1, N_DEV):
            g = g + part(j)
        g_ref[...] = g
        d_ref[...], nm_ref[...], nv_ref[...] = _adamw(w_ref[...], g, m_ref[...], v_ref[...])

    blk = pl.BlockSpec((tr, tc), lambda i, j: (i, j))
    sds = jax.ShapeDtypeStruct((rows, cols), F32)
    extra = [] if own is None else [own]
    return pl.pallas_call(
        body, name=name, grid=(rows // tr, cols // tc),
        in_specs=[pl.BlockSpec((N_DEV, tr, tc), lambda i, j: (0, i, j)), blk, blk, blk] + [blk] * len(extra),
        out_specs=(blk, blk, blk, blk), out_shape=(sds, sds, sds, sds),
        compiler_params=_params(dimension_semantics=("arbitrary", "arbitrary")),
    )(parts, w, m, v, *extra)


def _ada_adam(c_all, dmod_sh, w, m, v):
    rows, cols = w.shape
    tr = 256

    def body(c_ref, dm_ref, w_ref, m_ref, v_ref, g_ref, d_ref, nm_ref, nv_ref):
        cv = c_ref[...]
        g = _dot_tn(cv * _sig(cv), dm_ref[...], HI)
        g_ref[...] = g
        d_ref[...], nm_ref[...], nv_ref[...] = _adamw(w_ref[...], g, m_ref[...], v_ref[...])

    blk = pl.BlockSpec((tr, cols), lambda i: (i, 0))
    sds = jax.ShapeDtypeStruct((rows, cols), F32)
    return pl.pallas_call(
        body, name="ada_adam", grid=(rows // tr,),
        in_specs=[pl.BlockSpec((N_DEV, tr), lambda i: (0, i)), _const((N_DEV, cols)), blk, blk, blk],
        out_specs=(blk, blk, blk, blk), out_shape=(sds, sds, sds, sds),
        compiler_params=_params(dimension_semantics=("arbitrary",)),
    )(c_all, dmod_sh, w, m, v)


def _lanes(a, at=0):
    return jnp.pad(a, ((0, 0), (at, LANES - at - a.shape[1])))


WEIGHT_NAMES = ["w_ada", "b_ada", "norm_mix_w", "w_in", "conv_w", "conv_b", "conv_gn_w", "conv_gn_b", "gdn_conv_w",
                "gdn_a_log", "gdn_dt_bias", "gdn_norm_w", "w_out", "norm_ffn_w", "w_ffn_in", "w_ffn_out",
                "norm_final_w"]


SMALL_LAYOUT = [("b_ada", 0, 48, LANES), ("norm_mix_w", 48, 8, LANES), ("norm_ffn_w", 56, 8, LANES),
                ("norm_final_w", 64, 8, LANES), ("conv_b", 72, 4, LANES), ("conv_gn_w", 76, 4, LANES),
                ("conv_gn_b", 80, 4, LANES), ("gdn_norm_w", 84, 1, LANES), ("gdn_a_log", 85, 1, NH),
                ("gdn_dt_bias", 86, 1, NH)]
LOSS_ROW = 87


def _adam_small(g_small, weights, m1, m2):
    names = [nm for nm, _, _, _ in SMALL_LAYOUT]
    k = len(names)

    def body(*refs):
        g_ref = refs[0]
        w_refs, m_refs, v_refs = refs[1:1 + k], refs[1 + k:1 + 2 * k], refs[1 + 2 * k:1 + 3 * k]
        loss_ref = refs[1 + 3 * k]
        outs = refs[2 + 3 * k:2 + 7 * k]
        total = refs[-1]
        g = g_ref[0]
        for j in range(1, N_DEV):
            g = g + g_ref[j]
        total[...] = g
        loss_ref[...] = total[LOSS_ROW:LOSS_ROW + 1, :]
        for i, (_, r0, rows, lanes) in enumerate(SMALL_LAYOUT):
            gp = total[r0:r0 + rows, 0:lanes]
            outs[i][...] = gp
            outs[k + i][...], outs[2 * k + i][...], outs[3 * k + i][...] = _adamw(
                w_refs[i][...], gp, m_refs[i][...], v_refs[i][...])

    shapes = [jax.ShapeDtypeStruct((rows, lanes), F32) for _, _, rows, lanes in SMALL_LAYOUT]
    res = pl.pallas_call(
        body, name="adam_small",
        out_shape=tuple([jax.ShapeDtypeStruct((1, LANES), F32)] + shapes * 4),
        scratch_shapes=[pltpu.VMEM((SMALL_ROWS, LANES), F32)],
        compiler_params=_params(),
    )(g_small, *[weights[n] for n in names], *[m1[n] for n in names], *[m2[n] for n in names])
    kinds = [dict(zip(names, res[1 + q * k:1 + (q + 1) * k])) for q in range(4)]
    return res[0], kinds


def _mix_forward(w, xs, modnb):
    w_main = w["w_in"]
    w_ba = jnp.pad(w["w_in"][NMAIN:], ((0, LANES - 2 * NH), (0, 0)))
    alog_l = _lanes(w["gdn_a_log"], NH)
    dt_l = _lanes(w["gdn_dt_bias"], NH)
    p_main, p_ba, hb1 = _fwd_in(xs, w["norm_mix_w"], modnb, w["b_ada"], w_main, w_ba)
    y_conv, out_a = _conf_fwd(p_main, w["conv_w"], w["conv_b"], w["conv_gn_w"], w["conv_gn_b"])
    w_o, u_o, qg, kd, qk, cd, t_inv = _gdn_prep(p_main, p_ba, w["gdn_conv_w"], alog_l, dt_l)
    out_b, o_pre, s_in = _gdn_scan(w_o, u_o, qg, kd, qk, cd, p_main, w["gdn_norm_w"])
    return dict(w_main=w_main, w_ba=w_ba, alog_l=alog_l, dt_l=dt_l, p_main=p_main, p_ba=p_ba, hb1=hb1,
                y_conv=y_conv, out_a=out_a, out_b=out_b, o_pre=o_pre, s_in=s_in, t_inv=t_inv)


def _ffn_stage(w, f, xs, tgt, modnb):
    x1, mix, oab = _fwd_out(f["out_a"], f["out_b"], xs, modnb, w["b_ada"], w["w_out"])
    hb2, act, pre, dx2, dffn, st_fwd = _ffn_forward(x1, tgt, modnb, w["b_ada"], w["norm_ffn_w"],
                                                    w["norm_final_w"], w["w_ffn_in"], w["w_ffn_out"])
    gw_ffn_out = _grad_w_ffn_out(act, dffn)
    df, dx1, st_bwd = _ffn_backward(dffn, pre, x1, dx2, modnb, w["b_ada"], w["norm_ffn_w"], w["w_ffn_in"],
                                    w["w_ffn_out"])
    gw_ffn_in = _grad_w_ffn_in(hb2, df)
    return dict(mix=mix, oab=oab, dx1=dx1, st_ffn=st_fwd + st_bwd, gw_ffn_in=gw_ffn_in, gw_ffn_out=gw_ffn_out)


def _out_backward(w, g, modnb):
    dmix, d_out_a, d_out_b, st_out = _bwd_out(g["dx1"], g["mix"], modnb, w["b_ada"], w["w_out"])
    return dict(d_out_a=d_out_a, d_out_b=d_out_b, st_out=st_out, gw_out=_grad_w("grad_w_out", g["oab"], dmix, 512))


def _heads_backward(w, f, a):
    dp_conf, st_conf = _conf_bwd(a["d_out_a"], f["y_conv"], f["p_main"], w["conv_w"], w["conv_gn_w"],
                                 w["conv_gn_b"])
    dp_gdn, dp_ba, st_gdn = _gdn_bwd(a["d_out_b"], f["o_pre"], f["s_in"], f["t_inv"], f["p_main"], f["p_ba"],
                                     w["gdn_conv_w"], f["alog_l"], f["dt_l"], w["gdn_norm_w"])
    hb1 = f["hb1"]
    gw_in = jnp.concatenate(
        [_grad_w("grad_w_in_conf", dp_conf, hb1, 512), _grad_w("grad_w_in_gdn", dp_gdn, hb1, 512),
         _grad_w("grad_w_in_ba", dp_ba, hb1, LANES)[:2 * NH]], axis=0)
    return dict(dp_conf=dp_conf, dp_gdn=dp_gdn, dp_ba=dp_ba, st_conf=st_conf, st_gdn=st_gdn, gw_in=gw_in,
                gw_conv=st_conf[0:KC], gw_gconv=st_gdn[0:KS])


def _in_backward(w, f, g, a, h, xs, modnb):
    st_out, st_conf, st_gdn, st_ffn = a["st_out"], h["st_conf"], h["st_gdn"], g["st_ffn"]
    grad_x, st_in = _bwd_in(h["dp_conf"], h["dp_gdn"], h["dp_ba"], xs, g["dx1"], w["norm_mix_w"], modnb,
                            w["b_ada"], f["w_main"], f["w_ba"])
    dmod = jnp.concatenate([st_in[0:1], st_in[1:2], st_out[0:1], st_ffn[2:3], st_ffn[3:4], st_ffn[1:2]], axis=1)
    small = jnp.concatenate([
        dmod.reshape(48, LANES), st_in[2:3].reshape(8, LANES), st_ffn[4:5].reshape(8, LANES),
        st_ffn[0:1].reshape(8, LANES), st_conf[31:32].reshape(4, LANES), st_conf[32:33].reshape(4, LANES),
        st_conf[33:34].reshape(4, LANES), st_gdn[4:5, 0:LANES],
        _lanes(st_gdn[5:6, NH:2 * NH]), _lanes(st_gdn[6:7, NH:2 * NH]), st_ffn[5:6, 0:LANES]], axis=0)
    return dict(grad_x=grad_x, small=small)


def _local(w, xs, tgt, modnb):
    f = _mix_forward(w, xs, modnb)
    g = _ffn_stage(w, f, xs, tgt, modnb)
    a = _out_backward(w, g, modnb)
    h = _heads_backward(w, f, a)
    b = _in_backward(w, f, g, a, h, xs, modnb)
    return dict(b, gw_in=h["gw_in"], gw_conv=h["gw_conv"], gw_gconv=h["gw_gconv"], gw_out=a["gw_out"],
                gw_ffn_in=g["gw_ffn_in"], gw_ffn_out=g["gw_ffn_out"])


def kernel(x, c, w_ada, b_ada, norm_mix_w, w_in, conv_w, conv_b, conv_gn_w, conv_gn_b, gdn_conv_w, gdn_a_log, gdn_dt_bias, gdn_norm_w, w_out, norm_ffn_w, w_ffn_in, w_ffn_out, norm_final_w, loss_target, m_w_ada, m_b_ada, m_norm_mix_w, m_w_in, m_conv_w, m_conv_b, m_conv_gn_w, m_conv_gn_b, m_gdn_conv_w, m_gdn_a_log, m_gdn_dt_bias, m_gdn_norm_w, m_w_out, m_norm_ffn_w, m_w_ffn_in, m_w_ffn_out, m_norm_final_w, v_w_ada, v_b_ada, v_norm_mix_w, v_w_in, v_conv_w, v_conv_b, v_conv_gn_w, v_conv_gn_b, v_gdn_conv_w, v_gdn_a_log, v_gdn_dt_bias, v_gdn_norm_w, v_w_out, v_norm_ffn_w, v_w_ffn_in, v_w_ffn_out, v_norm_final_w):
    me = 4 * lax.axis_index("x") + 2 * lax.axis_index("y") + lax.axis_index("c")
    xs = x.reshape(S, D)
    tgt = loss_target.reshape(S, D)

    g_c, g_cw, g_gcw = _exchange("gather_cond", [c, conv_w[0], gdn_conv_w[0]], [False] * 3)
    c_all = g_c.reshape(N_DEV, D)
    g_mod, mod_token = _exchange("gather_mod", [_mod_shard(c_all, w_ada[0])], [False], with_token=True)
    modnb = lax.dynamic_index_in_dim(g_mod, me, axis=1, keepdims=False).reshape(1, 6 * D)

    late = [w_out[0].astype(BF16), jnp.transpose(w_ffn_in[0]).astype(BF16), w_ffn_out[0].astype(BF16)]
    g_win, *late_lands = _gather_two_level(
        "gather_weights", [_after(jnp.transpose(w_in[0]), mod_token).astype(BF16)] + late, seed_only=(1, 2, 3))
    late_started = _exchange_start("gather_late_start", late, late_lands, [False] * 3, only=LEVEL_ONE)
    modnb = _after(modnb, late_started[-1])
    w = dict(b_ada=b_ada, norm_mix_w=norm_mix_w, conv_b=conv_b, conv_gn_w=conv_gn_w, conv_gn_b=conv_gn_b,
             gdn_a_log=gdn_a_log, gdn_dt_bias=gdn_dt_bias, gdn_norm_w=gdn_norm_w, norm_ffn_w=norm_ffn_w,
             norm_final_w=norm_final_w.reshape(1, D),
             conv_w=jnp.transpose(g_cw, (1, 0, 2)).reshape(KC, CW),
             gdn_conv_w=jnp.transpose(g_gcw, (1, 0, 2)).reshape(KS, 3 * GW),
             w_in=g_win.reshape(NIN, D))

    f = _mix_forward(w, xs, modnb)
    _, late_landed = _exchange_wait("gather_late_wait", late_started, [False] * 3, (f["out_a"], f["out_b"]),
                                    only=LEVEL_ONE)
    g_wout, g_wfi, g_wfo = _relay_to_sibling("gather_late_relay", late_landed)
    w.update(w_out=g_wout.reshape(D, D), w_ffn_in=g_wfi, w_ffn_out=g_wfo.reshape(4, FB, D))
    g = _ffn_stage(w, f, xs, tgt, modnb)

    ffn_grads = [g["gw_ffn_in"], g["gw_ffn_out"].reshape(N_DEV, DFF // N_DEV, D)]
    ffn_started = _exchange_start("scatter_ffn_start", ffn_grads,
                                  [lax.empty(a.shape, a.dtype) for a in ffn_grads], [True] * 2)
    a = _out_backward(w, g, _after(modnb, ffn_started[-1]))
    out_grads = [a["gw_out"].reshape(N_DEV, D // N_DEV, D)]
    out_started = _exchange_start("scatter_out_start", out_grads,
                                  [lax.empty(t.shape, t.dtype) for t in out_grads], [True])
    h = _heads_backward(dict(w, conv_gn_w=_after(w["conv_gn_w"], out_started[-1])), f, a)

    in_grads = [h["gw_in"].reshape(N_DEV, NIN // N_DEV, D),
                jnp.transpose(h["gw_conv"].reshape(KC, N_DEV, CW // N_DEV), (1, 0, 2)),
                jnp.transpose(h["gw_gconv"].reshape(KS, N_DEV, 3 * GW // N_DEV), (1, 0, 2))]
    in_started = _exchange_start("scatter_in_start", in_grads,
                                 [lax.empty(t.shape, t.dtype) for t in in_grads], [True] * 3)
    loc = _in_backward(w, f, g, a, h, xs, _after(modnb, in_started[-1]))
    small_started = _exchange_start("gather_small_start", [loc["small"]],
                                    [lax.empty((N_DEV, SMALL_ROWS, LANES), F32)], [False])

    def own(sent):
        return lax.dynamic_index_in_dim(sent, me, axis=0, keepdims=False)

    big = {}
    (sent_fi, sent_fo), (r_fi, r_fo) = _exchange_wait("scatter_ffn_wait", ffn_started, [True] * 2,
                                                         (small_started[-1],))
    big["w_ffn_in"] = [jnp.transpose(t) for t in _reduce_adam(
        "adam_w_ffn_in", r_fi, jnp.transpose(w_ffn_in[0]), jnp.transpose(m_w_ffn_in[0]),
        jnp.transpose(v_w_ffn_in[0]), own(sent_fi))]
    big["w_ffn_out"] = _reduce_adam("adam_w_ffn_out", r_fo, w_ffn_out[0], m_w_ffn_out[0], v_w_ffn_out[0],
                                    own(sent_fo))
    (sent_out,), (r_out,) = _exchange_wait("scatter_out_wait", out_started, [True], (big["w_ffn_out"][0],))
    big["w_out"] = _reduce_adam("adam_w_out", r_out, w_out[0], m_w_out[0], v_w_out[0], own(sent_out))

    (sent_small,), (r_small,) = _exchange_wait("gather_small_wait", small_started, [False], (big["w_out"][0],))
    slot = lax.broadcasted_iota(jnp.int32, (N_DEV, 1, 1), 0)
    g_small = jnp.where(slot == me, sent_small[None], r_small)
    def views(b_, nm_, nf_, nl_, cb_, gw_, gb_, gn_, al_, dt_):
        arrs = [b_, nm_, nf_, nl_, cb_, gw_, gb_, gn_, al_, dt_]
        return {nm: t.reshape(rows, lanes) for (nm, _, rows, lanes), t in zip(SMALL_LAYOUT, arrs)}

    loss_row, res = _adam_small(
        g_small,
        views(b_ada, norm_mix_w, norm_ffn_w, norm_final_w, conv_b, conv_gn_w, conv_gn_b, gdn_norm_w, gdn_a_log,
              gdn_dt_bias),
        views(m_b_ada, m_norm_mix_w, m_norm_ffn_w, m_norm_final_w, m_conv_b, m_conv_gn_w, m_conv_gn_b,
              m_gdn_norm_w, m_gdn_a_log, m_gdn_dt_bias),
        views(v_b_ada, v_norm_mix_w, v_norm_ffn_w, v_norm_final_w, v_conv_b, v_conv_gn_w, v_conv_gn_b,
              v_gdn_norm_w, v_gdn_a_log, v_gdn_dt_bias))
    loss = loss_row[0, 0]
    small_shapes = dict(b_ada=(1, 6 * D), norm_mix_w=(1, D), norm_ffn_w=(1, D), norm_final_w=(D,),
                        conv_b=(1, CW), conv_gn_w=(1, CW), conv_gn_b=(1, CW), gdn_norm_w=(1, DH),
                        gdn_a_log=(1, NH), gdn_dt_bias=(1, NH))
    res = [{nm: t.reshape(small_shapes[nm]) for nm, t in kind.items()} for kind in res]

    dmod_rows = g_small[:, 0:48, :].reshape(N_DEV, 6 * D)
    dmod_sh = lax.dynamic_slice_in_dim(dmod_rows, me * (6 * D // N_DEV), 6 * D // N_DEV, axis=1)

    big["w_ada"] = _ada_adam(c_all, dmod_sh, w_ada[0], m_w_ada[0], v_w_ada[0])
    (sent_in, sent_cw, sent_gcw), (r_in, r_cw, r_gcw) = _exchange_wait(
        "scatter_in_wait", in_started, [True] * 3, (big["w_ada"][0],))
    big["w_in"] = [jnp.transpose(t) for t in _reduce_adam(
        "adam_w_in", r_in, jnp.transpose(w_in[0]), jnp.transpose(m_w_in[0]), jnp.transpose(v_w_in[0]),
        own(sent_in))]
    big["conv_w"] = _reduce_adam("adam_conv_w", r_cw, conv_w[0], m_conv_w[0], v_conv_w[0], own(sent_cw))
    big["gdn_conv_w"] = _reduce_adam("adam_gdn_conv_w", r_gcw, gdn_conv_w[0], m_gdn_conv_w[0], v_gdn_conv_w[0],
                                     own(sent_gcw))
    outs = [loss, loc["grad_x"].reshape(1, S, D)]
    for kind in range(4):
        for nm in WEIGHT_NAMES:
            outs.append(big[nm][kind][None] if nm in big else res[kind][nm])
    return tuple(outs)
```

```python
import functools

import jax
import jax.numpy as jnp
from jax import lax
from jax.experimental import pallas as pl
from jax.experimental.pallas import tpu as pltpu

F32 = jnp.float32
BF16 = jnp.bfloat16
HI = lax.Precision.HIGHEST
MESH = pl.DeviceIdType.MESH

N_DEV = 8
S = 2048
D = 1024
TM = 256
NT = S // TM
CW = 512
KC = 31
NG = 8
GSZ = CW // NG
HALO = 32
GW = 512
NH = 4
DH = 128
KS = 4
SH = 8
CL = 64
NCH = S // CL
NMAIN = 2 * CW + 4 * GW
NIN = NMAIN + 2 * NH
DFF = 2816
FB = DFF // 4
EPS = 1e-6
QSCALE = DH ** -0.5
LANES = 128
SMALL_ROWS = 88

ADAM_LR = 0.001
ADAM_B1 = 0.9
ADAM_B2 = 0.999
ADAM_EPS = 1e-08
ADAM_WD = 0.01
ADAM_STEP = 10
BC1 = 1.0 - ADAM_B1 ** ADAM_STEP
BC2 = 1.0 - ADAM_B2 ** ADAM_STEP

MIB = 1024 * 1024
VMEM_LIMIT_MIB = 32


def _params(limit_mib=VMEM_LIMIT_MIB, **kw):
    return pltpu.CompilerParams(vmem_limit_bytes=limit_mib * MIB, **kw)


def _sig(x):
    return jax.nn.sigmoid(x)


GP = BF16


def _operands(a, b, prec):
    if prec is BF16:
        return a.astype(BF16), b.astype(BF16), None
    return a, b, prec


def _dot(a, b, prec=None):
    a, b, prec = _operands(a, b, prec)
    return jnp.dot(a, b, preferred_element_type=F32, precision=prec)


def _dot_nt(a, b, prec=None):
    a, b, prec = _operands(a, b, prec)
    return lax.dot_general(a, b, (((1,), (1,)), ((), ())), preferred_element_type=F32, precision=prec)


def _dot_tn(a, b, prec=None):
    a, b, prec = _operands(a, b, prec)
    return lax.dot_general(a, b, (((0,), (0,)), ((), ())), preferred_element_type=F32, precision=prec)


def _lockstep(gens):
    gens = list(gens)
    while gens:
        alive = []
        for g in gens:
            try:
                next(g)
                alive.append(g)
            except StopIteration:
                pass
        gens = alive


def _rowsum(x):
    return jnp.sum(x, axis=-1, keepdims=True)


def _colsum(x):
    return jnp.sum(x, axis=0, keepdims=True)


def _mod(mod_ref, b_ref, k):
    return mod_ref[:, k * D:(k + 1) * D] + b_ref[:, k * D:(k + 1) * D]


def _const(shape):
    nd = len(shape)
    return pl.BlockSpec(shape, lambda *_: (0,) * nd)


def _const1(shape):
    nd = len(shape)
    return pl.BlockSpec(shape, lambda *_: (0,) * nd, pipeline_mode=pl.Buffered(1))


PEER_FLIPS = [(dx, dy, dc) for dx in (0, 1) for dy in (0, 1) for dc in (0, 1)][1:]


def _after(x, token):
    return x + token[0:1, 0:1].astype(x.dtype).reshape((1,) * x.ndim)


def _exchange(name, srcs, per_dest, seed_only=(), with_token=False):
    n = len(srcs)
    out_shape = []
    for a, pd in zip(srcs, per_dest):
        blk = a.shape[1:] if pd else a.shape
        out_shape.append(jax.ShapeDtypeStruct((N_DEV,) + tuple(blk), a.dtype))

    def body(*refs):
        src = refs[:n]
        dst = refs[n:2 * n]
        send_sems, recv_sems, local_sems = refs[-3:]
        if with_token:
            refs[2 * n][...] = jnp.zeros((8, LANES), F32)
        x, y, c = lax.axis_index("x"), lax.axis_index("y"), lax.axis_index("c")
        me = 4 * x + 2 * y + c

        def piece(i, j):
            return src[i].at[j] if per_dest[i] else src[i]

        copies = []
        for k, (dx, dy, dc) in enumerate(PEER_FLIPS):
            px = 1 - x if dx else x
            py = 1 - y if dy else y
            pc = 1 - c if dc else c
            pj = 4 * px + 2 * py + pc
            for i in range(n):
                if i in seed_only:
                    continue
                cp = pltpu.make_async_remote_copy(
                    src_ref=piece(i, pj), dst_ref=dst[i].at[me],
                    send_sem=send_sems.at[k * n + i], recv_sem=recv_sems.at[k * n + i],
                    device_id=(px, py, pc), device_id_type=MESH)
                cp.start()
                arrive = pltpu.make_async_remote_copy(
                    src_ref=piece(i, pj), dst_ref=dst[i].at[pj],
                    send_sem=send_sems.at[k * n + i], recv_sem=recv_sems.at[k * n + i],
                    device_id=(px, py, pc), device_id_type=MESH)
                copies.append((cp, arrive))
        own = []
        for i in range(n):
            lc = pltpu.make_async_copy(piece(i, me), dst[i].at[me], local_sems.at[i])
            lc.start()
            own.append(lc)
        for cp, arrive in copies:
            arrive.wait_recv()
        for cp, arrive in copies:
            cp.wait_send()
        for lc in own:
            lc.wait()

    any_spec = pl.BlockSpec(memory_space=pl.ANY)
    out_specs = [any_spec] * n
    if with_token:
        out_shape.append(jax.ShapeDtypeStruct((8, LANES), F32))
        out_specs.append(pl.BlockSpec(memory_space=pltpu.VMEM))
    return pl.pallas_call(
        body, name=name, out_shape=tuple(out_shape),
        in_specs=[any_spec] * n, out_specs=tuple(out_specs),
        scratch_shapes=[pltpu.SemaphoreType.DMA((7 * n,)), pltpu.SemaphoreType.DMA((7 * n,)),
                        pltpu.SemaphoreType.DMA((n,))],
        compiler_params=pltpu.CompilerParams(has_side_effects=True),
    )(*srcs)


CHIP_FLIPS = [(0, 1), (1, 0), (1, 1)]
LEVEL_ONE = [k for k, (dx, dy, dc) in enumerate(PEER_FLIPS) if (dx, dy, dc) == (0, 0, 1) or dc == 0]


def _chip_peers(x, y):
    return [(1 - x if dx else x, 1 - y if dy else y) for dx, dy in CHIP_FLIPS]


def _gather_two_level(name, srcs, seed_only=()):
    n = len(srcs)
    live = [i for i in range(n) if i not in seed_only]

    def body(*refs):
        src, dst = refs[:n], refs[n:2 * n]
        send_sems, recv_sems, local_sems = refs[2 * n:2 * n + 3]
        bounce = refs[2 * n + 3:]
        x, y, c = lax.axis_index("x"), lax.axis_index("y"), lax.axis_index("c")
        me = 4 * x + 2 * y + c
        sibling = (x, y, 1 - c)
        chips = _chip_peers(x, y)

        def copy(k, i, src_ref, slot, to):
            return pltpu.make_async_remote_copy(
                src_ref=src_ref, dst_ref=dst[i].at[slot], send_sem=send_sems.at[k * n + i],
                recv_sem=recv_sems.at[k * n + i], device_id=to, device_id_type=MESH)

        first = []
        for i in live:
            first.append(copy(0, i, src[i], me, sibling))
            first += [copy(1 + j, i, src[i], me, (px, py, c)) for j, (px, py) in enumerate(chips)]
        for cp in first:
            cp.start()
        up = [pltpu.make_async_copy(src[i], bounce[i], local_sems.at[i]) for i in range(n)]
        for cp in up:
            cp.start()
        for cp in up:
            cp.wait()
        own = [pltpu.make_async_copy(bounce[i], dst[i].at[me], local_sems.at[i]) for i in range(n)]
        for cp in own:
            cp.start()
        passed = []
        for j, (px, py) in enumerate(chips):
            slot = 4 * px + 2 * py + c
            for i in live:
                copy(1 + j, i, src[i], slot, (px, py, c)).wait_recv()
                fwd = copy(4 + j, i, dst[i].at[slot], slot, sibling)
                fwd.start()
                passed.append(fwd)
        for i in live:
            copy(0, i, src[i], 4 * x + 2 * y + 1 - c, sibling).wait_recv()
            for j, (px, py) in enumerate(chips):
                copy(4 + j, i, src[i], 4 * px + 2 * py + 1 - c, sibling).wait_recv()
        for cp in first + passed:
            cp.wait_send()
        for cp in own:
            cp.wait()

    any_spec = pl.BlockSpec(memory_space=pl.ANY)
    return pl.pallas_call(
        body, name=name, out_shape=tuple(jax.ShapeDtypeStruct((N_DEV,) + a.shape, a.dtype) for a in srcs),
        in_specs=[any_spec] * n, out_specs=tuple([any_spec] * n),
        scratch_shapes=[pltpu.SemaphoreType.DMA((7 * n,)), pltpu.SemaphoreType.DMA((7 * n,)),
                        pltpu.SemaphoreType.DMA((n,))] + [pltpu.VMEM(a.shape, a.dtype) for a in srcs],
        compiler_params=pltpu.CompilerParams(has_side_effects=True),
    )(*srcs)


def _relay_copy(land, sems, i, n, j, slot, sibling):
    send_sems, recv_sems = sems
    return pltpu.make_async_remote_copy(
        src_ref=land[i].at[slot], dst_ref=land[i].at[slot], send_sem=send_sems.at[j * n + i],
        recv_sem=recv_sems.at[j * n + i], device_id=sibling, device_id_type=MESH)


def _relay_start(name, lands):
    n = len(lands)

    def body(*refs):
        land = refs[:n]
        sems = refs[n], refs[n + 1]
        x, y, c = lax.axis_index("x"), lax.axis_index("y"), lax.axis_index("c")
        for j, (px, py) in enumerate(_chip_peers(x, y)):
            for i in range(n):
                _relay_copy(land, sems, i, n, j, 4 * px + 2 * py + c, (x, y, 1 - c)).start()
        refs[-1][...] = jnp.zeros((8, LANES), F32)

    return pl.pallas_call(
        body, name=name,
        out_shape=(pltpu.SemaphoreType.DMA((3 * n,)), pltpu.SemaphoreType.DMA((3 * n,)),
                   *[pltpu.HBM(a.shape, a.dtype) for a in lands], jax.ShapeDtypeStruct((8, LANES), F32)),
        in_specs=[HBM_SPEC] * n,
        out_specs=(SEM_SPEC, SEM_SPEC, *[HBM_SPEC] * n, pl.BlockSpec(memory_space=pltpu.VMEM)),
        input_output_aliases={i: 2 + i for i in range(n)},
        compiler_params=pltpu.CompilerParams(has_side_effects=DATAFLOW),
    )(*[pltpu.with_memory_space_constraint(a, pltpu.HBM) for a in lands])


def _relay_wait(name, started, after):
    n = len(started) - 3
    arrays = list(started[2:2 + n])

    def body(*refs):
        land = refs[:n]
        sems = refs[n], refs[n + 1]
        x, y, c = lax.axis_index("x"), lax.axis_index("y"), lax.axis_index("c")
        for j, (px, py) in enumerate(_chip_peers(x, y)):
            for i in range(n):
                _relay_copy(land, sems, i, n, j, 4 * px + 2 * py + c, (x, y, 1 - c)).wait_send()
                _relay_copy(land, sems, i, n, j, 4 * px + 2 * py + 1 - c, (x, y, 1 - c)).wait_recv()

    return pl.pallas_call(
        body, name=name,
        out_shape=tuple(pltpu.HBM(a.shape, a.dtype) for a in arrays),
        in_specs=[HBM_SPEC] * n + [SEM_SPEC, SEM_SPEC] + [pl.BlockSpec(memory_space=pl.ANY)] * len(after),
        out_specs=tuple([HBM_SPEC] * n),
        input_output_aliases={i: i for i in range(n)},
        compiler_params=pltpu.CompilerParams(has_side_effects=DATAFLOW),
    )(*arrays, started[0], started[1], *after)


HBM_SPEC = pl.BlockSpec(memory_space=pltpu.HBM)
SEM_SPEC = pl.BlockSpec(memory_space=pltpu.SEMAPHORE)
DATAFLOW = pltpu.SideEffectType.DATAFLOW_SIDE_EFFECTING


def _peers(only=None):
    x, y, c = lax.axis_index("x"), lax.axis_index("y"), lax.axis_index("c")
    out = []
    for k, (dx, dy, dc) in enumerate(PEER_FLIPS):
        if only is not None and k not in only:
            continue
        px = 1 - x if dx else x
        py = 1 - y if dy else y
        pc = 1 - c if dc else c
        out.append((k, (px, py, pc), 4 * px + 2 * py + pc))
    return 4 * x + 2 * y + c, out


def _exchange_start(name, srcs, lands, per_dest, only=None):
    n = len(srcs)

    def body(*refs):
        src, land = refs[:n], refs[n:2 * n]
        send_sems, recv_sems = refs[2 * n], refs[2 * n + 1]
        token = refs[-1]
        me, peers = _peers(only)
        for k, peer, pj in peers:
            for i in range(n):
                pltpu.make_async_remote_copy(
                    src_ref=src[i].at[pj] if per_dest[i] else src[i], dst_ref=land[i].at[me],
                    send_sem=send_sems.at[k * n + i], recv_sem=recv_sems.at[k * n + i],
                    device_id=peer, device_id_type=MESH).start()
        token[...] = jnp.zeros((8, LANES), F32)

    arrays = list(srcs) + list(lands)
    return pl.pallas_call(
        body, name=name,
        out_shape=(pltpu.SemaphoreType.DMA((7 * n,)), pltpu.SemaphoreType.DMA((7 * n,)),
                   *[pltpu.HBM(a.shape, a.dtype) for a in arrays], jax.ShapeDtypeStruct((8, LANES), F32)),
        in_specs=[HBM_SPEC] * (2 * n),
        out_specs=(SEM_SPEC, SEM_SPEC, *[HBM_SPEC] * (2 * n), pl.BlockSpec(memory_space=pltpu.VMEM)),
        input_output_aliases={i: 2 + i for i in range(2 * n)},
        compiler_params=pltpu.CompilerParams(has_side_effects=DATAFLOW),
    )(*[pltpu.with_memory_space_constraint(a, pltpu.HBM) for a in arrays])


def _exchange_wait(name, started, per_dest, after, only=None):
    n = (len(started) - 3) // 2
    send_sems, recv_sems = started[0], started[1]
    arrays = list(started[2:2 + 2 * n])

    def body(*refs):
        src, land = refs[:n], refs[n:2 * n]
        send, recv = refs[2 * n], refs[2 * n + 1]
        me, peers = _peers(only)
        for k, peer, pj in peers:
            for i in range(n):
                cp = pltpu.make_async_remote_copy(
                    src_ref=src[i].at[pj] if per_dest[i] else src[i], dst_ref=land[i].at[pj],
                    send_sem=send.at[k * n + i], recv_sem=recv.at[k * n + i],
                    device_id=peer, device_id_type=MESH)
                cp.wait_send()
                cp.wait_recv()

    outs = pl.pallas_call(
        body, name=name,
        out_shape=tuple(pltpu.HBM(a.shape, a.dtype) for a in arrays),
        in_specs=[HBM_SPEC] * (2 * n) + [SEM_SPEC, SEM_SPEC] + [pl.BlockSpec(memory_space=pl.ANY)] * len(after),
        out_specs=tuple([HBM_SPEC] * (2 * n)),
        input_output_aliases={i: i for i in range(2 * n)},
        compiler_params=pltpu.CompilerParams(has_side_effects=DATAFLOW),
    )(*arrays, send_sems, recv_sems, *after)
    return outs[:n], outs[n:]


def _mod_shard(c_all, w_ada):
    def body(c_ref, w_ref, o_ref):
        cv = c_ref[...]
        ca = cv * _sig(cv)
        o_ref[...] = _dot(ca.astype(BF16), w_ref[...].astype(BF16))

    return pl.pallas_call(
        body, name="mod_shard", out_shape=jax.ShapeDtypeStruct((N_DEV, w_ada.shape[1]), F32),
        compiler_params=_params(),
    )(c_all, w_ada)


def _fwd_in(x, nw1, modnb, bada, w_main, w_ba):
    def body(x_ref, nw_ref, mod_ref, b_ref, wm_ref, wb_ref, pm_ref, pb_ref, hb_ref):
        xv = x_ref[...]
        r = lax.rsqrt(jnp.mean(xv * xv, axis=-1, keepdims=True) + EPS)
        h = (xv * r * nw_ref[...]) * (1.0 + _mod(mod_ref, b_ref, 1)) + _mod(mod_ref, b_ref, 0)
        hb = h.astype(BF16)
        hb_ref[...] = hb
        pm_ref[...] = _dot_nt(hb, wm_ref[...])
        pb_ref[...] = _dot_nt(hb, wb_ref[...])

    return pl.pallas_call(
        body, name="fwd_in", grid=(NT,),
        in_specs=[pl.BlockSpec((TM, D), lambda i: (i, 0)), _const((1, D)), _const((1, 6 * D)), _const((1, 6 * D)),
                  _const((NMAIN, D)), _const((LANES, D))],
        out_specs=(pl.BlockSpec((TM, NMAIN), lambda i: (i, 0)), pl.BlockSpec((TM, LANES), lambda i: (i, 0)),
                   pl.BlockSpec((TM, D), lambda i: (i, 0))),
        out_shape=(jax.ShapeDtypeStruct((S, NMAIN), F32), jax.ShapeDtypeStruct((S, LANES), F32),
                   jax.ShapeDtypeStruct((S, D), BF16)),
        compiler_params=_params(dimension_semantics=("arbitrary",)),
    )(x, nw1, modnb, bada, w_main, w_ba)


def _group_mean_matrix():
    ii = lax.broadcasted_iota(jnp.int32, (CW, CW), 0) // GSZ
    jj = lax.broadcasted_iota(jnp.int32, (CW, CW), 1) // GSZ
    return jnp.where(ii == jj, 1.0 / GSZ, 0.0).astype(F32)


SUB = 8
SHIFT_ROWS = HALO + TM - SUB


def _fill_shifted(buf, sh):
    for b in range(1, SUB):
        sh[b - 1] = buf[b:b + SHIFT_ROWS, :]


def _rows_at(buf, sh, off):
    a, b = divmod(off, SUB)
    if b == 0:
        return buf[off:off + TM, :]
    return sh[b - 1, SUB * a:SUB * a + TM, :]


def _group_mean(x, pm):
    hi = x.astype(BF16)
    r1 = x - hi.astype(F32)
    mid = r1.astype(BF16)
    lo = (r1 - mid.astype(F32)).astype(BF16)
    return _dot(hi, pm) + _dot(mid, pm) + _dot(lo, pm)


def _conf_fwd(p_main, conv_w, conv_b, gn_w, gn_b):
    def body(a_ref, g_ref, w_ref, b_ref, gw_ref, gb_ref, y_ref, oa_ref, ubuf, ush):
        i = pl.program_id(0)

        @pl.when(i == 0)
        def _():
            ubuf[0:HALO, :] = jnp.zeros((HALO, CW), F32)

        ubuf[HALO:HALO + TM, :] = a_ref[...] * _sig(g_ref[...])
        _fill_shifted(ubuf, ush)
        acc = jnp.zeros((TM, CW), F32) + b_ref[...]
        for k in range(KC):
            acc = acc + w_ref[k:k + 1, :] * _rows_at(ubuf, ush, HALO - (KC - 1) + k)
        y_ref[...] = acc
        ubuf[0:HALO, :] = ubuf[TM:TM + HALO, :]
        pm = _group_mean_matrix().astype(BF16)
        dlt = acc - _group_mean(acc, pm)
        var = _group_mean(dlt * dlt, pm)
        o = dlt * lax.rsqrt(var + EPS) * gw_ref[...] + gb_ref[...]
        oa_ref[...] = o * _sig(o)

    return pl.pallas_call(
        body, name="conf_fwd", grid=(NT,),
        in_specs=[pl.BlockSpec((TM, CW), lambda i: (i, 0)), pl.BlockSpec((TM, CW), lambda i: (i, 1)),
                  _const((KC, CW)), _const((1, CW)), _const((1, CW)), _const((1, CW))],
        out_specs=(pl.BlockSpec((TM, CW), lambda i: (i, 0)), pl.BlockSpec((TM, CW), lambda i: (i, 0))),
        out_shape=(jax.ShapeDtypeStruct((S, CW), F32), jax.ShapeDtypeStruct((S, CW), F32)),
        scratch_shapes=[pltpu.VMEM((HALO + TM, CW), F32), pltpu.VMEM((SUB - 1, SHIFT_ROWS, CW), F32)],
        compiler_params=_params(dimension_semantics=("arbitrary",)),
    )(p_main, p_main, conv_w, conv_b, gn_w, gn_b)


def _tri_iota():
    ii = lax.broadcasted_iota(jnp.int32, (CL, CL), 0)
    jj = lax.broadcasted_iota(jnp.int32, (CL, CL), 1)
    return ii, jj


def _gdn_gates(ba, alog_l, dt_l):
    beta_all = _sig(ba)
    xg = ba + dt_l
    sp = jnp.maximum(xg, 0.0) + jnp.log(1.0 + jnp.exp(-jnp.abs(xg)))
    neg_a = -jnp.exp(alog_l)
    return beta_all, neg_a * sp, xg, neg_a


def _ones_dot(ones, x):
    hi = x.astype(BF16)
    r1 = x - hi.astype(F32)
    mid = r1.astype(BF16)
    lo = (r1 - mid.astype(F32)).astype(BF16)
    return _dot(ones, hi) + _dot(ones, mid) + _dot(ones, lo)


def _gdn_cumsum(g_all):
    ii, jj = _tri_iota()
    low = jnp.where(ii >= jj, 1.0, 0.0).astype(BF16)
    gcum = _ones_dot(low, g_all)
    return gcum, jnp.transpose(gcum)


def _split(x):
    hi = x.astype(BF16)
    return hi, (x - hi.astype(F32)).astype(BF16)


def _dot_split(a, b):
    (ah, al), (bh, bl) = a, b
    return _dot(ah, bh) + (_dot(ah, bl) + _dot(al, bh))


def _unit_lower_inverses(mats):
    ii, jj = _tri_iota()
    eye = jnp.where(ii == jj, 1.0, 0.0).astype(F32)
    ts = [eye - a for a in mats]
    ps = [_dot_split(s, s) for s in map(_split, mats)]
    for _ in range(4):
        sp = [_split(p) for p in ps]
        ts = [t + _dot_split(_split(t), s) for t, s in zip(ts, sp)]
        ps = [_dot_split(s, s) for s in sp]
    return [t + _dot_split(_split(t), _split(p)) for t, p in zip(ts, ps)]


def _head_terms(qh, kh, beta, gcol, grow):
    ii, jj = _tri_iota()
    causal = ii >= jj
    strict = ii > jj
    rq = lax.rsqrt(_rowsum(qh * qh) + EPS)
    rk = lax.rsqrt(_rowsum(kh * kh) + EPS)
    qn = qh * rq
    kn = kh * rk
    qs = qn * QSCALE
    decay = jnp.where(causal, jnp.exp(jnp.where(causal, gcol - grow, 0.0)), 0.0)
    gam = jnp.exp(gcol)
    gl = gcol[CL - 1:CL, :]
    kds = jnp.exp(gl - gcol)
    cd = jnp.exp(gl)
    kb = kn * beta
    a = jnp.where(strict, _dot_nt(kb, kn, GP) * decay, 0.0)
    qk = jnp.where(causal, _dot_nt(qs, kn, GP) * decay, 0.0)
    return dict(rq=rq, rk=rk, qn=qn, kn=kn, qs=qs, decay=decay, gam=gam, kds=kds, cd=cd, kb=kb, a=a, qk=qk,
                causal=causal, strict=strict)


def _short_conv(w_ref, buf, rows=CL):
    acc = w_ref[0:1, :] * buf[SH - KS + 1:SH - KS + 1 + rows, :]
    for k in range(1, KS):
        off = SH - (KS - 1) + k
        acc = acc + w_ref[k:k + 1, :] * buf[off:off + rows, :]
    return acc


CPS = 4
TG = CPS * CL


def _gdn_prep(p_main, p_ba, gdn_conv_w, alog_l, dt_l):
    def body(q_ref, k_ref, v_ref, qh_ref, kh_ref, vh_ref, ba_ref, w_ref, al_ref, dt_ref,
             wo_ref, uo_ref, qg_ref, kd_ref, qk_ref, cd_ref, t_ref, xbuf):
        i = pl.program_id(0)
        first = i == 0
        xbuf[0:SH, 0:GW] = jnp.where(first, 0.0, qh_ref[...])
        xbuf[0:SH, GW:2 * GW] = jnp.where(first, 0.0, kh_ref[...])
        xbuf[0:SH, 2 * GW:3 * GW] = jnp.where(first, 0.0, vh_ref[...])
        xbuf[SH:SH + TG, 0:GW] = q_ref[...]
        xbuf[SH:SH + TG, GW:2 * GW] = k_ref[...]
        xbuf[SH:SH + TG, 2 * GW:3 * GW] = v_ref[...]
        conv = _short_conv(w_ref, xbuf, TG)
        qkv = conv * _sig(conv)
        beta_all, g_all, _, _ = _gdn_gates(ba_ref[...], al_ref[...], dt_ref[...])
        lane = lax.broadcasted_iota(jnp.int32, (8, LANES), 1)
        cums = [_gdn_cumsum(g_all[cc * CL:(cc + 1) * CL, :]) for cc in range(CPS)]
        pairs = [(cc, h) for cc in range(CPS) for h in range(NH)]
        terms, vbs = [], []
        for cc, h in pairs:
            r0, lo = cc * CL, h * DH
            beta = beta_all[r0:r0 + CL, h:h + 1]
            gcum, gcum_t = cums[cc]
            terms.append(_head_terms(qkv[r0:r0 + CL, lo:lo + DH], qkv[r0:r0 + CL, GW + lo:GW + lo + DH], beta,
                                     gcum[:, NH + h:NH + h + 1], gcum_t[NH + h:NH + h + 1, :]))
            vbs.append(qkv[r0:r0 + CL, 2 * GW + lo:2 * GW + lo + DH] * beta)
        invs = _unit_lower_inverses([f["a"] for f in terms])
        cds = [jnp.zeros((8, LANES), F32) for _ in range(CPS)]
        for (cc, h), f, t, vb in zip(pairs, terms, invs, vbs):
            r0, lo = cc * CL, h * DH
            t_ref[cc, h] = t
            uo_ref[r0:r0 + CL, lo:lo + DH] = _dot(t, vb, GP)
            wo_ref[r0:r0 + CL, lo:lo + DH] = _dot(t, f["kb"] * f["gam"], GP).astype(BF16)
            qg_ref[r0:r0 + CL, lo:lo + DH] = (f["qs"] * f["gam"]).astype(BF16)
            kd_ref[r0:r0 + CL, lo:lo + DH] = (f["kn"] * f["kds"]).astype(BF16)
            qk_ref[cc, h] = f["qk"].astype(BF16)
            cds[cc] = cds[cc] + jnp.where(lane == h, f["cd"], 0.0)
        for cc in range(CPS):
            cd_ref[cc] = cds[cc]

    col = lambda j: pl.BlockSpec((TG, GW), lambda i: (i, j))
    halo = lambda j: pl.BlockSpec((SH, GW), lambda i: (jnp.maximum(i * (TG // SH) - 1, 0), j))
    tile = lambda: pl.BlockSpec((TG, GW), lambda i: (i, 0))
    sq = lambda: pl.BlockSpec((CPS, NH, CL, CL), lambda i: (i, 0, 0, 0))
    return pl.pallas_call(
        body, name="gdn_prep", grid=(NCH // CPS,),
        in_specs=[col(2), col(3), col(4), halo(2), halo(3), halo(4), pl.BlockSpec((TG, LANES), lambda i: (i, 0)),
                  _const((KS, 3 * GW)), _const((1, LANES)), _const((1, LANES))],
        out_specs=(tile(), tile(), tile(), tile(), sq(), pl.BlockSpec((CPS, 8, LANES), lambda i: (i, 0, 0)), sq()),
        out_shape=(jax.ShapeDtypeStruct((S, GW), BF16), jax.ShapeDtypeStruct((S, GW), F32),
                   jax.ShapeDtypeStruct((S, GW), BF16), jax.ShapeDtypeStruct((S, GW), BF16),
                   jax.ShapeDtypeStruct((NCH, NH, CL, CL), BF16), jax.ShapeDtypeStruct((NCH, 8, LANES), F32),
                   jax.ShapeDtypeStruct((NCH, NH, CL, CL), F32)),
        scratch_shapes=[pltpu.VMEM((SH + TG, 3 * GW), F32)],
        compiler_params=_params(dimension_semantics=("arbitrary",)),
    )(p_main, p_main, p_main, p_main, p_main, p_main, p_ba, gdn_conv_w, alog_l, dt_l)


def _gdn_scan(w_o, u_o, qg, kd, qk, cd, p_main, gdn_nw):
    def body(w_ref, u_ref, qg_ref, kd_ref, qk_ref, cd_ref, z_ref, nw_ref, ob_ref, o_ref, sin_ref, state):
        n = pl.program_id(0)

        @pl.when(n == 0)
        def _():
            state[...] = jnp.zeros((NH, DH, DH), F32)

        def head(h):
            lo = h * DH
            st = state[h]
            sin_ref[0, h] = st
            sb = st.astype(BF16)
            v_new = u_ref[:, lo:lo + DH] - _dot(w_ref[:, lo:lo + DH], sb)
            yield
            vb = v_new.astype(BF16)
            o = _dot(qg_ref[:, lo:lo + DH], sb) + _dot(qk_ref[0, h], vb)
            state[h] = st * cd_ref[0, 0:1, h:h + 1] + _dot_tn(kd_ref[:, lo:lo + DH], vb)
            yield
            o_ref[:, lo:lo + DH] = o
            r = lax.rsqrt(jnp.mean(o * o, axis=-1, keepdims=True) + EPS)
            zh = z_ref[:, lo:lo + DH]
            ob_ref[:, lo:lo + DH] = o * r * nw_ref[...] * (zh * _sig(zh))

        _lockstep(head(h) for h in range(NH))

    tile = lambda: pl.BlockSpec((CL, GW), lambda n: (n, 0))
    return pl.pallas_call(
        body, name="gdn_scan", grid=(NCH,),
        in_specs=[tile(), tile(), tile(), tile(), pl.BlockSpec((1, NH, CL, CL), lambda n: (n, 0, 0, 0)),
                  pl.BlockSpec((1, 8, LANES), lambda n: (n, 0, 0)), pl.BlockSpec((CL, GW), lambda n: (n, 5)),
                  _const((1, DH))],
        out_specs=(tile(), tile(), pl.BlockSpec((1, NH, DH, DH), lambda n: (n, 0, 0, 0))),
        out_shape=(jax.ShapeDtypeStruct((S, GW), F32), jax.ShapeDtypeStruct((S, GW), F32),
                   jax.ShapeDtypeStruct((NCH, NH, DH, DH), F32)),
        scratch_shapes=[pltpu.VMEM((NH, DH, DH), F32)],
        compiler_params=_params(dimension_semantics=("arbitrary",)),
    )(w_o, u_o, qg, kd, qk, cd, p_main, gdn_nw)


def _fwd_out(out_a, out_b, x, modnb, bada, w_out):
    def body(oa_ref, ob_ref, x_ref, mod_ref, b_ref, w_ref, x1_ref, mix_ref, oab_ref):
        oa = oa_ref[...].astype(BF16)
        ob = ob_ref[...].astype(BF16)
        oab_ref[:, 0:CW] = oa
        oab_ref[:, CW:D] = ob
        mix = _dot(oa, w_ref[0:CW, :]) + _dot(ob, w_ref[CW:D, :])
        mix_ref[...] = mix
        x1_ref[...] = x_ref[...] + _mod(mod_ref, b_ref, 2) * mix

    tile = lambda w: pl.BlockSpec((TM, w), lambda i: (i, 0))
    return pl.pallas_call(
        body, name="fwd_out", grid=(NT,),
        in_specs=[tile(CW), tile(GW), tile(D), _const((1, 6 * D)), _const((1, 6 * D)), _const((D, D))],
        out_specs=(tile(D), tile(D), tile(D)),
        out_shape=(jax.ShapeDtypeStruct((S, D), F32), jax.ShapeDtypeStruct((S, D), F32),
                   jax.ShapeDtypeStruct((S, D), BF16)),
        compiler_params=_params(dimension_semantics=("arbitrary",)),
    )(out_a, out_b, x, modnb, bada, w_out)


FFN_STATS = 8


def _ffn_forward(x1, tgt, modnb, bada, nw2, nfw, w_fi, w_fo):
    def body(x1_ref, tgt_ref, mod_ref, b_ref, nw2_ref, nfw_ref, wi_ref, wo_ref,
             hb_ref, act_ref, pre_ref, dx2_ref, dffn_ref, st_ref):
        i = pl.program_id(0)

        @pl.when(i == 0)
        def _():
            st_ref[...] = jnp.zeros((FFN_STATS, D), F32)

        sh2, sc2, gt2 = _mod(mod_ref, b_ref, 3), _mod(mod_ref, b_ref, 4), _mod(mod_ref, b_ref, 5)
        x1v = x1_ref[...]
        r2 = lax.rsqrt(jnp.mean(x1v * x1v, axis=-1, keepdims=True) + EPS)
        hb = ((x1v * r2 * nw2_ref[...]) * (1.0 + sc2) + sh2).astype(BF16)
        hb_ref[...] = hb
        ffn = jnp.zeros((TM, D), F32)
        for j in range(4):
            fgj = _dot_nt(hb, wi_ref[j])
            fuj = _dot_nt(hb, wi_ref[j + 4])
            pre_ref[j] = fgj.astype(BF16)
            pre_ref[j + 4] = fuj.astype(BF16)
            aj = (fgj * _sig(fgj) * fuj).astype(BF16)
            act_ref[j] = aj
            ffn = ffn + _dot(aj, wo_ref[j])
        x2 = x1v + gt2 * ffn
        r3 = lax.rsqrt(jnp.mean(x2 * x2, axis=-1, keepdims=True) + EPS)
        xr3 = x2 * r3
        err = xr3 * nfw_ref[...] - tgt_ref[...]
        loss = 0.5 * jnp.sum(jnp.mean(err * err, axis=-1, keepdims=True), axis=0, keepdims=True)
        dy = err * (1.0 / D)
        st_ref[0:1, :] += _colsum(dy * xr3)
        dyr = dy * nfw_ref[...]
        dx2 = r3 * (dyr - xr3 * jnp.mean(dyr * xr3, axis=-1, keepdims=True))
        st_ref[1:2, :] += _colsum(dx2 * ffn)
        st_ref[5:6, :] += jnp.broadcast_to(loss, (1, D))
        dx2_ref[...] = dx2
        dffn_ref[...] = (gt2 * dx2).astype(BF16)

    tile = lambda w: pl.BlockSpec((TM, w), lambda i: (i, 0))
    return pl.pallas_call(
        body, name="ffn_forward", grid=(NT,),
        in_specs=[tile(D), tile(D), _const((1, 6 * D)), _const((1, 6 * D)), _const((1, D)), _const((1, D)),
                  _const1((N_DEV, FB, D)), _const1((4, FB, D))],
        out_specs=(tile(D), pl.BlockSpec((4, TM, FB), lambda i: (0, i, 0)),
                   pl.BlockSpec((N_DEV, TM, FB), lambda i: (0, i, 0)), tile(D), tile(D), _const((FFN_STATS, D))),
        out_shape=(jax.ShapeDtypeStruct((S, D), BF16), jax.ShapeDtypeStruct((4, S, FB), BF16),
                   jax.ShapeDtypeStruct((N_DEV, S, FB), BF16), jax.ShapeDtypeStruct((S, D), F32),
                   jax.ShapeDtypeStruct((S, D), BF16), jax.ShapeDtypeStruct((FFN_STATS, D), F32)),
        compiler_params=_params(42, dimension_semantics=("arbitrary",)),
    )(x1, tgt, modnb, bada, nw2, nfw, w_fi, w_fo)


def _ffn_backward(dffn, pre, x1, dx2, modnb, bada, nw2, w_fi, w_fo):
    def body(dffn_ref, pre_ref, x1_ref, dx2_ref, mod_ref, b_ref, nw2_ref, wi_ref, wo_ref, df_ref, dx1_ref, st_ref):
        i = pl.program_id(0)

        @pl.when(i == 0)
        def _():
            st_ref[...] = jnp.zeros((FFN_STATS, D), F32)

        dffn = dffn_ref[...]
        dh = jnp.zeros((TM, D), F32)
        for j in range(4):
            fg = pre_ref[j].astype(F32)
            fu = pre_ref[j + 4].astype(F32)
            sg = _sig(fg)
            dact = _dot_nt(dffn, wo_ref[j])
            dfg = (dact * fu * (sg * (1.0 + fg * (1.0 - sg)))).astype(BF16)
            dfu = (dact * (fg * sg)).astype(BF16)
            df_ref[j] = dfg
            df_ref[j + 4] = dfu
            dh = dh + _dot(dfg, wi_ref[j]) + _dot(dfu, wi_ref[j + 4])
        x1v = x1_ref[...]
        r2 = lax.rsqrt(jnp.mean(x1v * x1v, axis=-1, keepdims=True) + EPS)
        xr2 = x1v * r2
        st_ref[2:3, :] += _colsum(dh)
        st_ref[3:4, :] += _colsum(dh * (xr2 * nw2_ref[...]))
        dxn = dh * (1.0 + _mod(mod_ref, b_ref, 4))
        st_ref[4:5, :] += _colsum(dxn * xr2)
        dxr = dxn * nw2_ref[...]
        dx1_ref[...] = dx2_ref[...] + r2 * (dxr - xr2 * jnp.mean(dxr * xr2, axis=-1, keepdims=True))

    tile = lambda w: pl.BlockSpec((TM, w), lambda i: (i, 0))
    wide = lambda: pl.BlockSpec((N_DEV, TM, FB), lambda i: (0, i, 0))
    return pl.pallas_call(
        body, name="ffn_backward", grid=(NT,),
        in_specs=[tile(D), wide(), tile(D), tile(D), _const((1, 6 * D)), _const((1, 6 * D)), _const((1, D)),
                  _const1((N_DEV, FB, D)), _const1((4, FB, D))],
        out_specs=(wide(), tile(D), _const((FFN_STATS, D))),
        out_shape=(jax.ShapeDtypeStruct((N_DEV, S, FB), BF16), jax.ShapeDtypeStruct((S, D), F32),
                   jax.ShapeDtypeStruct((FFN_STATS, D), F32)),
        compiler_params=_params(44, dimension_semantics=("arbitrary",)),
    )(dffn, pre, x1, dx2, modnb, bada, nw2, w_fi, w_fo)


def _grad_w(name, a, b, nb):
    m, n = a.shape[1], b.shape[1]

    def body(a_ref, b_ref, o_ref):
        o_ref[...] = _dot_tn(a_ref[...], b_ref[...]).astype(BF16)

    return pl.pallas_call(
        body, name=name, grid=(m // nb,),
        in_specs=[pl.BlockSpec((S, nb), lambda j: (0, j)), _const((S, n))],
        out_specs=pl.BlockSpec((nb, n), lambda j: (j, 0)),
        out_shape=jax.ShapeDtypeStruct((m, n), BF16),
        compiler_params=_params(dimension_semantics=("arbitrary",)),
    )(a, b)


GW_IN_ROWS = NMAIN + LANES


def _grad_w_in(dp_conf, dp_gdn, dp_ba, hb1):
    nb = 512
    n_conf, n_gdn = 2 * CW // nb, 4 * GW // nb

    def body(c_ref, g_ref, ba_ref, h_ref, o_ref):
        j = pl.program_id(0)

        @pl.when(j < n_conf)
        def _():
            o_ref[...] = _dot_tn(c_ref[...], h_ref[...]).astype(BF16)

        @pl.when((j >= n_conf) & (j < n_conf + n_gdn))
        def _():
            o_ref[...] = _dot_tn(g_ref[...], h_ref[...]).astype(BF16)

        @pl.when(j == n_conf + n_gdn)
        def _():
            o_ref[0:LANES, :] = _dot_tn(ba_ref[...], h_ref[...]).astype(BF16)

    return pl.pallas_call(
        body, name="grad_w_in", grid=(n_conf + n_gdn + 1,),
        in_specs=[pl.BlockSpec((S, nb), lambda j: (0, jnp.minimum(j, n_conf - 1))),
                  pl.BlockSpec((S, nb), lambda j: (0, jnp.clip(j - n_conf, 0, n_gdn - 1))),
                  _const((S, LANES)), _const((S, D))],
        out_specs=pl.BlockSpec((nb, D), lambda j: (j, 0)),
        out_shape=jax.ShapeDtypeStruct((GW_IN_ROWS, D), BF16),
        compiler_params=_params(dimension_semantics=("arbitrary",)),
    )(dp_conf, dp_gdn, dp_ba, hb1)


def _grad_w_ffn_in(hb2, df):
    def body(a_ref, b_ref, o_ref):
        o_ref[0] = _dot_tn(b_ref[0], a_ref[...]).astype(BF16)

    return pl.pallas_call(
        body, name="grad_w_ffn_in", grid=(N_DEV,),
        in_specs=[_const((S, D)), pl.BlockSpec((1, S, FB), lambda j: (j, 0, 0))],
        out_specs=pl.BlockSpec((1, FB, D), lambda j: (j, 0, 0)),
        out_shape=jax.ShapeDtypeStruct((N_DEV, FB, D), BF16),
        compiler_params=_params(dimension_semantics=("arbitrary",)),
    )(hb2, df)


def _grad_w_ffn_out(act, dffn):
    def body(a_ref, b_ref, o_ref):
        o_ref[0] = _dot_tn(a_ref[0], b_ref[...]).astype(BF16)

    return pl.pallas_call(
        body, name="grad_w_ffn_out", grid=(4,),
        in_specs=[pl.BlockSpec((1, S, FB), lambda j: (j, 0, 0)), _const((S, D))],
        out_specs=pl.BlockSpec((1, FB, D), lambda j: (j, 0, 0)),
        out_shape=jax.ShapeDtypeStruct((4, FB, D), BF16),
        compiler_params=_params(dimension_semantics=("arbitrary",)),
    )(act, dffn)


def _bwd_out(dx1, mix, modnb, bada, w_out):
    def body(dx_ref, mix_ref, mod_ref, b_ref, w_ref, dmix_ref, doa_ref, dob_ref, st_ref):
        i = pl.program_id(0)

        @pl.when(i == 0)
        def _():
            st_ref[...] = jnp.zeros((8, D), F32)

        dx = dx_ref[...]
        st_ref[0:1, :] += _colsum(dx * mix_ref[...])
        dmix = (_mod(mod_ref, b_ref, 2) * dx).astype(BF16)
        dmix_ref[...] = dmix
        doa_ref[...] = _dot_nt(dmix, w_ref[0:CW, :])
        dob_ref[...] = _dot_nt(dmix, w_ref[CW:D, :])

    tile = lambda w: pl.BlockSpec((TM, w), lambda i: (i, 0))
    return pl.pallas_call(
        body, name="bwd_out", grid=(NT,),
        in_specs=[tile(D), tile(D), _const((1, 6 * D)), _const((1, 6 * D)), _const((D, D))],
        out_specs=(tile(D), tile(CW), tile(GW), _const((8, D))),
        out_shape=(jax.ShapeDtypeStruct((S, D), BF16), jax.ShapeDtypeStruct((S, CW), F32),
                   jax.ShapeDtypeStruct((S, GW), F32), jax.ShapeDtypeStruct((8, D), F32)),
        compiler_params=_params(dimension_semantics=("arbitrary",)),
    )(dx1, mix, modnb, bada, w_out)


CONF_STATS = 40


def _conf_bwd(d_out_a, y, p_main, conv_w, gn_w, gn_b):
    def body(do_ref, y_ref, a_ref, g_ref, ah_ref, gh_ref, w_ref, gw_ref, gb_ref, dp_ref, st_ref,
             ubuf, dybuf, ush, dysh):
        i = pl.program_id(0)

        @pl.when(i == 0)
        def _():
            st_ref[...] = jnp.zeros((CONF_STATS, CW), F32)
            dybuf[TM:TM + HALO, :] = jnp.zeros((HALO, CW), F32)

        pm = _group_mean_matrix().astype(BF16)
        yv = y_ref[...]
        dlt = yv - _group_mean(yv, pm)
        rstd = lax.rsqrt(_group_mean(dlt * dlt, pm) + EPS)
        un = dlt * rstd
        o = un * gw_ref[...] + gb_ref[...]
        so = _sig(o)
        d_o = do_ref[...] * (so * (1.0 + o * (1.0 - so)))
        st_ref[33:34, :] += _colsum(d_o)
        st_ref[32:33, :] += _colsum(d_o * un)
        dun = d_o * gw_ref[...]
        dy = rstd * (dun - _group_mean(dun, pm) - un * _group_mean(dun * un, pm))
        st_ref[31:32, :] += _colsum(dy)
        dybuf[0:TM, :] = dy
        _fill_shifted(dybuf, dysh)

        a = a_ref[...]
        sg = _sig(g_ref[...])
        first = i == NT - 1
        ubuf[0:HALO, :] = jnp.where(first, 0.0, ah_ref[...] * _sig(gh_ref[...]))
        ubuf[HALO:HALO + TM, :] = a * sg
        _fill_shifted(ubuf, ush)
        du = jnp.zeros((TM, CW), F32)
        for k in range(KC):
            st_ref[k:k + 1, :] += _colsum(dy * _rows_at(ubuf, ush, HALO - (KC - 1) + k))
            du = du + w_ref[k:k + 1, :] * _rows_at(dybuf, dysh, KC - 1 - k)
        dybuf[TM:TM + HALO, :] = dybuf[0:HALO, :]
        dp_ref[:, 0:CW] = (du * sg).astype(BF16)
        dp_ref[:, CW:2 * CW] = (du * a * sg * (1.0 - sg)).astype(BF16)

    rev = lambda w, j=0: pl.BlockSpec((TM, w), lambda i: (NT - 1 - i, j))
    halo = lambda j: pl.BlockSpec((HALO, CW), lambda i: (jnp.maximum((NT - 1 - i) * (TM // HALO) - 1, 0), j))
    return pl.pallas_call(
        body, name="conf_bwd", grid=(NT,),
        in_specs=[rev(CW), rev(CW), rev(CW, 0), rev(CW, 1), halo(0), halo(1),
                  _const((KC, CW)), _const((1, CW)), _const((1, CW))],
        out_specs=(rev(2 * CW), _const((CONF_STATS, CW))),
        out_shape=(jax.ShapeDtypeStruct((S, 2 * CW), BF16), jax.ShapeDtypeStruct((CONF_STATS, CW), F32)),
        scratch_shapes=[pltpu.VMEM((HALO + TM, CW), F32), pltpu.VMEM((TM + HALO, CW), F32),
                        pltpu.VMEM((SUB - 1, SHIFT_ROWS, CW), F32), pltpu.VMEM((SUB - 1, SHIFT_ROWS, CW), F32)],
        compiler_params=_params(dimension_semantics=("arbitrary",)),
    )(d_out_a, y, p_main, p_main, p_main, p_main, conv_w, gn_w, gn_b)


GDN_STATS = 8


def _gdn_bwd(d_out_b, o_pre, s_in, t_inv, p_main, p_ba, gdn_conv_w, alog_l, dt_l, gdn_nw):
    def body(dob_ref, o_ref, sin_ref, t_ref, q_ref, k_ref, v_ref, z_ref, qh_ref, kh_ref, vh_ref, ba_ref,
             w_ref, al_ref, dt_ref, nw_ref, dp_ref, dba_ref, st_ref, xbuf, dcbuf, dstate):
        n = pl.program_id(0)

        @pl.when(n == 0)
        def _():
            st_ref[...] = jnp.zeros((GDN_STATS, 3 * GW), F32)
            dcbuf[CL:CL + SH, :] = jnp.zeros((SH, 3 * GW), F32)
            dstate[...] = jnp.zeros((NH, DH, DH), F32)

        first = n == NCH - 1
        xbuf[0:SH, 0:GW] = jnp.where(first, 0.0, qh_ref[...])
        xbuf[0:SH, GW:2 * GW] = jnp.where(first, 0.0, kh_ref[...])
        xbuf[0:SH, 2 * GW:3 * GW] = jnp.where(first, 0.0, vh_ref[...])
        xbuf[SH:SH + CL, 0:GW] = q_ref[...]
        xbuf[SH:SH + CL, GW:2 * GW] = k_ref[...]
        xbuf[SH:SH + CL, 2 * GW:3 * GW] = v_ref[...]
        conv = _short_conv(w_ref, xbuf)
        sc = _sig(conv)
        qkv = conv * sc
        ba = ba_ref[...]
        beta_all, g_all, xg, neg_a = _gdn_gates(ba, al_ref[...], dt_ref[...])
        gcum, gcum_t = _gdn_cumsum(g_all)
        lane = lax.broadcasted_iota(jnp.int32, (CL, LANES), 1)
        row = lax.broadcasted_iota(jnp.int32, (CL, 1), 0)
        acc = dict(dgcum=jnp.zeros((CL, LANES), F32), dbeta=jnp.zeros((CL, LANES), F32))

        def head(h):
            lo = h * DH
            qh = qkv[:, lo:lo + DH]
            kh = qkv[:, GW + lo:GW + lo + DH]
            vh = qkv[:, 2 * GW + lo:2 * GW + lo + DH]
            beta = beta_all[:, h:h + 1]
            f = _head_terms(qh, kh, beta, gcum[:, NH + h:NH + h + 1], gcum_t[NH + h:NH + h + 1, :])
            qn, kn, qs, kb, gam, kds, cd, decay = (f[s] for s in ("qn", "kn", "qs", "kb", "gam", "kds", "cd", "decay"))
            t = t_ref[0, h]
            st = sin_ref[0, h]
            vb = vh * beta
            kbg = kb * gam
            u = _dot(t, vb, GP)
            w = _dot(t, kbg, GP)
            yield
            v_new = u - _dot(w, st, GP)
            q_dec = qs * gam
            k_dec = kn * kds

            o = o_ref[:, lo:lo + DH]
            zh = z_ref[:, lo:lo + DH]
            sz = _sig(zh)
            r = lax.rsqrt(jnp.mean(o * o, axis=-1, keepdims=True) + EPS)
            orr = o * r
            d_out = dob_ref[:, lo:lo + DH]
            dz = d_out * (orr * nw_ref[...]) * (sz * (1.0 + zh * (1.0 - sz)))
            don = d_out * (zh * sz)
            st_ref[4:5, 0:DH] += _colsum(don * orr)
            tt = don * nw_ref[...]
            d_o = r * (tt - orr * jnp.mean(tt * orr, axis=-1, keepdims=True))

            yield
            ds_out = dstate[h]
            dv_new = _dot_tn(f["qk"], d_o, GP) + _dot(k_dec, ds_out, GP)
            dqk = jnp.where(f["causal"], _dot_nt(d_o, v_new, GP), 0.0)
            dq_dec = _dot_nt(d_o, st, GP)
            dk_dec = _dot_nt(v_new, ds_out, GP)
            yield
            dstate[h] = _dot_tn(q_dec, d_o, GP) + cd * ds_out - _dot_tn(w, dv_new, GP)
            dcd = jnp.sum(_rowsum(st * ds_out), axis=0, keepdims=True)
            dw = -_dot_nt(dv_new, st, GP)
            dvb = _dot_tn(t, dv_new, GP)
            yield
            dt_m = _dot_nt(dv_new, vb, GP) + _dot_nt(dw, kbg, GP)
            dkbg = _dot_tn(t, dw, GP)
            yield
            dtt = _dot_nt(dt_m, t, GP)
            yield
            da = jnp.where(f["strict"], -_dot_tn(t, dtt, GP), 0.0)
            yield
            dad = da * decay
            dqkd = dqk * decay
            dkb = _dot(dad, kn, GP) + dkbg * gam
            dkn = _dot_tn(dad, kb, GP) + _dot_tn(dqkd, qs, GP) + dk_dec * kds + dkb * beta
            dqs = _dot(dqkd, kn, GP) + dq_dec * gam
            yield
            m = da * f["a"] + dqk * f["qk"]
            tk = _rowsum(dk_dec * k_dec)
            dgl = jnp.sum(tk, axis=0, keepdims=True) + dcd * cd
            dgc = (_rowsum(m) - _rowsum(jnp.transpose(m)) + _rowsum(dq_dec * q_dec) - tk + _rowsum(dkbg * kbg)
                   + jnp.where(row == CL - 1, dgl, 0.0))
            dbeta = _rowsum(dkb * kn) + _rowsum(dvb * vh)
            acc["dgcum"] = acc["dgcum"] + jnp.where(lane == NH + h, dgc, 0.0)
            acc["dbeta"] = acc["dbeta"] + jnp.where(lane == h, dbeta, 0.0)
            dvh = dvb * beta
            dqn = dqs * QSCALE
            dqh = f["rq"] * (dqn - qn * _rowsum(dqn * qn))
            dkh = f["rk"] * (dkn - kn * _rowsum(dkn * kn))
            dsilu = lambda c0: sc[:, c0:c0 + DH] * (1.0 + conv[:, c0:c0 + DH] * (1.0 - sc[:, c0:c0 + DH]))
            dcbuf[0:CL, lo:lo + DH] = dqh * dsilu(lo)
            dcbuf[0:CL, GW + lo:GW + lo + DH] = dkh * dsilu(GW + lo)
            dcbuf[0:CL, 2 * GW + lo:2 * GW + lo + DH] = dvh * dsilu(2 * GW + lo)
            dp_ref[:, 3 * GW + lo:3 * GW + lo + DH] = dz.astype(BF16)

        _lockstep(head(h) for h in range(NH))
        dgcum_all, dbeta_all = acc["dgcum"], acc["dbeta"]

        ii, jj = _tri_iota()
        upper = jnp.where(ii <= jj, 1.0, 0.0).astype(BF16)
        dg_all = _ones_dot(upper, dgcum_all)
        dxg = dg_all * neg_a * _sig(xg)
        st_ref[5:6, 0:LANES] += _colsum(dg_all * g_all)
        st_ref[6:7, 0:LANES] += _colsum(dxg)
        dbl = dbeta_all * beta_all * (1.0 - beta_all)
        dba_ref[...] = jnp.where(lane < NH, dbl, jnp.where(lane < 2 * NH, dxg, 0.0)).astype(BF16)

        dconv = dcbuf[0:CL, :]
        dx = w_ref[0:1, :] * dcbuf[KS - 1:KS - 1 + CL, :]
        st_ref[0:1, :] += _colsum(dconv * xbuf[SH - KS + 1:SH - KS + 1 + CL, :])
        for k in range(1, KS):
            off = SH - (KS - 1) + k
            st_ref[k:k + 1, :] += _colsum(dconv * xbuf[off:off + CL, :])
            dx = dx + w_ref[k:k + 1, :] * dcbuf[KS - 1 - k:KS - 1 - k + CL, :]
        dcbuf[CL:CL + SH, :] = dcbuf[0:SH, :]
        dp_ref[:, 0:3 * GW] = dx.astype(BF16)

    rev = lambda w, j=0: pl.BlockSpec((CL, w), lambda n: (NCH - 1 - n, j))
    halo = lambda j: pl.BlockSpec((SH, GW), lambda n: (jnp.maximum((NCH - 1 - n) * (CL // SH) - 1, 0), j))
    blk4 = lambda a, b: pl.BlockSpec((1, NH, a, b), lambda n: (NCH - 1 - n, 0, 0, 0))
    return pl.pallas_call(
        body, name="gdn_bwd", grid=(NCH,),
        in_specs=[rev(GW), rev(GW), blk4(DH, DH), blk4(CL, CL), rev(GW, 2), rev(GW, 3), rev(GW, 4), rev(GW, 5),
                  halo(2), halo(3), halo(4), rev(LANES), _const((KS, 3 * GW)), _const((1, LANES)),
                  _const((1, LANES)), _const((1, DH))],
        out_specs=(rev(4 * GW), rev(LANES), _const((GDN_STATS, 3 * GW))),
        out_shape=(jax.ShapeDtypeStruct((S, 4 * GW), BF16), jax.ShapeDtypeStruct((S, LANES), BF16),
                   jax.ShapeDtypeStruct((GDN_STATS, 3 * GW), F32)),
        scratch_shapes=[pltpu.VMEM((SH + CL, 3 * GW), F32), pltpu.VMEM((CL + SH, 3 * GW), F32),
                        pltpu.VMEM((NH, DH, DH), F32)],
        compiler_params=_params(dimension_semantics=("arbitrary",)),
    )(d_out_b, o_pre, s_in, t_inv, p_main, p_main, p_main, p_main, p_main, p_main, p_main, p_ba,
      gdn_conv_w, alog_l, dt_l, gdn_nw)


def _bwd_in(dp_conf, dp_gdn, dp_ba, x, dx1, nw1, modnb, bada, w_main, w_ba):
    def body(dc_ref, dg_ref, db_ref, x_ref, dx1_ref, nw_ref, mod_ref, b_ref, wm_ref, wb_ref, gx_ref, st_ref):
        i = pl.program_id(0)

        @pl.when(i == 0)
        def _():
            st_ref[...] = jnp.zeros((8, D), F32)

        dh = (_dot(dc_ref[...], wm_ref[0:2 * CW, :]) + _dot(dg_ref[...], wm_ref[2 * CW:NMAIN, :])
              + _dot(db_ref[...], wb_ref[...]))
        xv = x_ref[...]
        r = lax.rsqrt(jnp.mean(xv * xv, axis=-1, keepdims=True) + EPS)
        xr = xv * r
        st_ref[0:1, :] += _colsum(dh)
        st_ref[1:2, :] += _colsum(dh * (xr * nw_ref[...]))
        dxn = dh * (1.0 + _mod(mod_ref, b_ref, 1))
        st_ref[2:3, :] += _colsum(dxn * xr)
        dxr = dxn * nw_ref[...]
        gx_ref[...] = dx1_ref[...] + r * (dxr - xr * jnp.mean(dxr * xr, axis=-1, keepdims=True))

    tile = lambda w: pl.BlockSpec((TM, w), lambda i: (i, 0))
    return pl.pallas_call(
        body, name="bwd_in", grid=(NT,),
        in_specs=[tile(2 * CW), tile(4 * GW), tile(LANES), tile(D), tile(D), _const((1, D)), _const((1, 6 * D)),
                  _const((1, 6 * D)), _const((NMAIN, D)), _const((LANES, D))],
        out_specs=(tile(D), _const((8, D))),
        out_shape=(jax.ShapeDtypeStruct((S, D), F32), jax.ShapeDtypeStruct((8, D), F32)),
        compiler_params=_params(dimension_semantics=("arbitrary",)),
    )(dp_conf, dp_gdn, dp_ba, x, dx1, nw1, modnb, bada, w_main, w_ba)


def _adamw(w, g, m, v):
    m = ADAM_B1 * m + (1.0 - ADAM_B1) * g
    v = ADAM_B2 * v + (1.0 - ADAM_B2) * (g * g)
    m_hat = m / BC1
    v_hat = v / BC2
    delta = -ADAM_LR * (m_hat / (jnp.sqrt(v_hat) + ADAM_EPS) + ADAM_WD * w)
    return delta, m, v


ADAM_BLOCK_BYTES = 6 * 1024 * 1024


def _adam_tile(rows, cols):
    padded = -(-cols // LANES) * LANES
    if N_DEV * rows * padded * 4 <= ADAM_BLOCK_BYTES:
        return rows, cols
    best = None
    for tr in range(16, rows, 16):
        if rows % tr == 0 and N_DEV * tr * padded * 4 <= ADAM_BLOCK_BYTES:
            best = tr
    if best is not None:
        return best, cols
    rows_padded = -(-rows // 16) * 16
    tc = LANES
    for cand in range(LANES, cols, LANES):
        if cols % cand == 0 and N_DEV * rows_padded * cand * 4 <= ADAM_BLOCK_BYTES:
            tc = cand
    return rows, tc


def _reduce_adam(name, parts, w, m, v, own=None):
    rows, cols = w.shape
    tr, tc = _adam_tile(rows, cols)

    def body(*refs):
        p_ref, w_ref, m_ref, v_ref = refs[:4]
        g_ref, d_ref, nm_ref, nv_ref = refs[-4:]
        if own is None:
            part = lambda j: p_ref[j].astype(F32)
        else:
            me = 4 * lax.axis_index("x") + 2 * lax.axis_index("y") + lax.axis_index("c")
            part = lambda j: jnp.where(me == j, refs[4][...], p_ref[j]).astype(F32)
        g = part(0)
        for j in range(1, N_DEV):
            g = g + part(j)
        g_ref[...] = g
        d_ref[...], nm_ref[...], nv_ref[...] = _adamw(w_ref[...], g, m_ref[...], v_ref[...])

    blk = pl.BlockSpec((tr, tc), lambda i, j: (i, j))
    sds = jax.ShapeDtypeStruct((rows, cols), F32)
    extra = [] if own is None else [own]
    return pl.pallas_call(
        body, name=name, grid=(rows // tr, cols // tc),
        in_specs=[pl.BlockSpec((N_DEV, tr, tc), lambda i, j: (0, i, j)), blk, blk, blk] + [blk] * len(extra),
        out_specs=(blk, blk, blk, blk), out_shape=(sds, sds, sds, sds),
        compiler_params=_params(dimension_semantics=("arbitrary", "arbitrary")),
    )(parts, w, m, v, *extra)


def _ada_adam(c_all, dmod_sh, w, m, v):
    rows, cols = w.shape
    tr = 256

    def body(c_ref, dm_ref, w_ref, m_ref, v_ref, g_ref, d_ref, nm_ref, nv_ref):
        cv = c_ref[...]
        g = _dot_tn(cv * _sig(cv), dm_ref[...], HI)
        g_ref[...] = g
        d_ref[...], nm_ref[...], nv_ref[...] = _adamw(w_ref[...], g, m_ref[...], v_ref[...])

    blk = pl.BlockSpec((tr, cols), lambda i: (i, 0))
    sds = jax.ShapeDtypeStruct((rows, cols), F32)
    return pl.pallas_call(
        body, name="ada_adam", grid=(rows // tr,),
        in_specs=[pl.BlockSpec((N_DEV, tr), lambda i: (0, i)), _const((N_DEV, cols)), blk, blk, blk],
        out_specs=(blk, blk, blk, blk), out_shape=(sds, sds, sds, sds),
        compiler_params=_params(dimension_semantics=("arbitrary",)),
    )(c_all, dmod_sh, w, m, v)


def _lanes(a, at=0):
    return jnp.pad(a, ((0, 0), (at, LANES - at - a.shape[1])))


WEIGHT_NAMES = ["w_ada", "b_ada", "norm_mix_w", "w_in", "conv_w", "conv_b", "conv_gn_w", "conv_gn_b", "gdn_conv_w",
                "gdn_a_log", "gdn_dt_bias", "gdn_norm_w", "w_out", "norm_ffn_w", "w_ffn_in", "w_ffn_out",
                "norm_final_w"]


SMALL_LAYOUT = [("b_ada", 0, 48, LANES), ("norm_mix_w", 48, 8, LANES), ("norm_ffn_w", 56, 8, LANES),
                ("norm_final_w", 64, 8, LANES), ("conv_b", 72, 4, LANES), ("conv_gn_w", 76, 4, LANES),
                ("conv_gn_b", 80, 4, LANES), ("gdn_norm_w", 84, 1, LANES), ("gdn_a_log", 85, 1, NH),
                ("gdn_dt_bias", 86, 1, NH)]
LOSS_ROW = 87


def _adam_small(g_small, weights, m1, m2):
    names = [nm for nm, _, _, _ in SMALL_LAYOUT]
    k = len(names)

    def body(*refs):
        g_ref = refs[0]
        w_refs, m_refs, v_refs = refs[1:1 + k], refs[1 + k:1 + 2 * k], refs[1 + 2 * k:1 + 3 * k]
        loss_ref = refs[1 + 3 * k]
        outs = refs[2 + 3 * k:2 + 7 * k]
        total = refs[-1]
        g = g_ref[0]
        for j in range(1, N_DEV):
            g = g + g_ref[j]
        total[...] = g
        loss_ref[...] = total[LOSS_ROW:LOSS_ROW + 1, :]
        for i, (_, r0, rows, lanes) in enumerate(SMALL_LAYOUT):
            gp = total[r0:r0 + rows, 0:lanes]
            outs[i][...] = gp
            outs[k + i][...], outs[2 * k + i][...], outs[3 * k + i][...] = _adamw(
                w_refs[i][...], gp, m_refs[i][...], v_refs[i][...])

    shapes = [jax.ShapeDtypeStruct((rows, lanes), F32) for _, _, rows, lanes in SMALL_LAYOUT]
    res = pl.pallas_call(
        body, name="adam_small",
        out_shape=tuple([jax.ShapeDtypeStruct((1, LANES), F32)] + shapes * 4),
        scratch_shapes=[pltpu.VMEM((SMALL_ROWS, LANES), F32)],
        compiler_params=_params(),
    )(g_small, *[weights[n] for n in names], *[m1[n] for n in names], *[m2[n] for n in names])
    kinds = [dict(zip(names, res[1 + q * k:1 + (q + 1) * k])) for q in range(4)]
    return res[0], kinds


def _mix_forward(w, xs, modnb, between=None):
    w_main = w["w_in"]
    w_ba = jnp.pad(w["w_in"][NMAIN:], ((0, LANES - 2 * NH), (0, 0)))
    alog_l = _lanes(w["gdn_a_log"], NH)
    dt_l = _lanes(w["gdn_dt_bias"], NH)
    p_main, p_ba, hb1 = _fwd_in(xs, w["norm_mix_w"], modnb, w["b_ada"], w_main, w_ba)
    w_o, u_o, qg, kd, qk, cd, t_inv = _gdn_prep(p_main, p_ba, w["gdn_conv_w"], alog_l, dt_l)
    out_b, o_pre, s_in = _gdn_scan(w_o, u_o, qg, kd, qk, cd, p_main, w["gdn_norm_w"])
    conv_b = w["conv_b"] if between is None else _after(w["conv_b"], between(out_b))
    y_conv, out_a = _conf_fwd(p_main, w["conv_w"], conv_b, w["conv_gn_w"], w["conv_gn_b"])
    return dict(w_main=w_main, w_ba=w_ba, alog_l=alog_l, dt_l=dt_l, p_main=p_main, p_ba=p_ba, hb1=hb1,
                y_conv=y_conv, out_a=out_a, out_b=out_b, o_pre=o_pre, s_in=s_in, t_inv=t_inv)


def _ffn_stage(w, f, xs, tgt, modnb):
    x1, mix, oab = _fwd_out(f["out_a"], f["out_b"], xs, modnb, w["b_ada"], w["w_out"])
    hb2, act, pre, dx2, dffn, st_fwd = _ffn_forward(x1, tgt, modnb, w["b_ada"], w["norm_ffn_w"],
                                                    w["norm_final_w"], w["w_ffn_in"], w["w_ffn_out"])
    gw_ffn_out = _grad_w_ffn_out(act, dffn)
    df, dx1, st_bwd = _ffn_backward(dffn, pre, x1, dx2, modnb, w["b_ada"], w["norm_ffn_w"], w["w_ffn_in"],
                                    w["w_ffn_out"])
    gw_ffn_in = _grad_w_ffn_in(hb2, df)
    return dict(mix=mix, oab=oab, dx1=dx1, st_ffn=st_fwd + st_bwd, gw_ffn_in=gw_ffn_in, gw_ffn_out=gw_ffn_out)


def _out_backward(w, g, modnb):
    dmix, d_out_a, d_out_b, st_out = _bwd_out(g["dx1"], g["mix"], modnb, w["b_ada"], w["w_out"])
    return dict(d_out_a=d_out_a, d_out_b=d_out_b, st_out=st_out, gw_out=_grad_w("grad_w_out", g["oab"], dmix, 512))


def _heads_backward(w, f, a):
    dp_conf, st_conf = _conf_bwd(a["d_out_a"], f["y_conv"], f["p_main"], w["conv_w"], w["conv_gn_w"],
                                 w["conv_gn_b"])
    dp_gdn, dp_ba, st_gdn = _gdn_bwd(a["d_out_b"], f["o_pre"], f["s_in"], f["t_inv"], f["p_main"], f["p_ba"],
                                     w["gdn_conv_w"], f["alog_l"], f["dt_l"], w["gdn_norm_w"])
    gw_in = _grad_w_in(dp_conf, dp_gdn, dp_ba, f["hb1"])[:NIN]
    return dict(dp_conf=dp_conf, dp_gdn=dp_gdn, dp_ba=dp_ba, st_conf=st_conf, st_gdn=st_gdn, gw_in=gw_in,
                gw_conv=st_conf[0:KC], gw_gconv=st_gdn[0:KS])


def _in_backward(w, f, g, a, h, xs, modnb):
    st_out, st_conf, st_gdn, st_ffn = a["st_out"], h["st_conf"], h["st_gdn"], g["st_ffn"]
    grad_x, st_in = _bwd_in(h["dp_conf"], h["dp_gdn"], h["dp_ba"], xs, g["dx1"], w["norm_mix_w"], modnb,
                            w["b_ada"], f["w_main"], f["w_ba"])
    dmod = jnp.concatenate([st_in[0:1], st_in[1:2], st_out[0:1], st_ffn[2:3], st_ffn[3:4], st_ffn[1:2]], axis=1)
    small = jnp.concatenate([
        dmod.reshape(48, LANES), st_in[2:3].reshape(8, LANES), st_ffn[4:5].reshape(8, LANES),
        st_ffn[0:1].reshape(8, LANES), st_conf[31:32].reshape(4, LANES), st_conf[32:33].reshape(4, LANES),
        st_conf[33:34].reshape(4, LANES), st_gdn[4:5, 0:LANES],
        _lanes(st_gdn[5:6, NH:2 * NH]), _lanes(st_gdn[6:7, NH:2 * NH]), st_ffn[5:6, 0:LANES]], axis=0)
    return dict(grad_x=grad_x, small=small)


def _local(w, xs, tgt, modnb):
    f = _mix_forward(w, xs, modnb)
    g = _ffn_stage(w, f, xs, tgt, modnb)
    a = _out_backward(w, g, modnb)
    h = _heads_backward(w, f, a)
    b = _in_backward(w, f, g, a, h, xs, modnb)
    return dict(b, gw_in=h["gw_in"], gw_conv=h["gw_conv"], gw_gconv=h["gw_gconv"], gw_out=a["gw_out"],
                gw_ffn_in=g["gw_ffn_in"], gw_ffn_out=g["gw_ffn_out"])


def kernel(x, c, w_ada, b_ada, norm_mix_w, w_in, conv_w, conv_b, conv_gn_w, conv_gn_b, gdn_conv_w, gdn_a_log, gdn_dt_bias, gdn_norm_w, w_out, norm_ffn_w, w_ffn_in, w_ffn_out, norm_final_w, loss_target, m_w_ada, m_b_ada, m_norm_mix_w, m_w_in, m_conv_w, m_conv_b, m_conv_gn_w, m_conv_gn_b, m_gdn_conv_w, m_gdn_a_log, m_gdn_dt_bias, m_gdn_norm_w, m_w_out, m_norm_ffn_w, m_w_ffn_in, m_w_ffn_out, m_norm_final_w, v_w_ada, v_b_ada, v_norm_mix_w, v_w_in, v_conv_w, v_conv_b, v_conv_gn_w, v_conv_gn_b, v_gdn_conv_w, v_gdn_a_log, v_gdn_dt_bias, v_gdn_norm_w, v_w_out, v_norm_ffn_w, v_w_ffn_in, v_w_ffn_out, v_norm_final_w):
    me = 4 * lax.axis_index("x") + 2 * lax.axis_index("y") + lax.axis_index("c")
    xs = x.reshape(S, D)
    tgt = loss_target.reshape(S, D)

    g_c, g_cw, g_gcw = _exchange("gather_cond", [c, conv_w[0], gdn_conv_w[0]], [False] * 3)
    c_all = g_c.reshape(N_DEV, D)
    g_mod, mod_token = _exchange("gather_mod", [_mod_shard(c_all, w_ada[0])], [False], with_token=True)
    modnb = lax.dynamic_index_in_dim(g_mod, me, axis=1, keepdims=False).reshape(1, 6 * D)

    late = [w_out[0].astype(BF16), jnp.transpose(w_ffn_in[0]).astype(BF16), w_ffn_out[0].astype(BF16)]
    g_win, *late_lands = _gather_two_level(
        "gather_weights", [_after(jnp.transpose(w_in[0]), mod_token).astype(BF16)] + late, seed_only=(1, 2, 3))
    late_started = _exchange_start("gather_late_start", late, late_lands, [False] * 3, only=LEVEL_ONE)
    modnb = _after(modnb, late_started[-1])
    w = dict(b_ada=b_ada, norm_mix_w=norm_mix_w, conv_b=conv_b, conv_gn_w=conv_gn_w, conv_gn_b=conv_gn_b,
             gdn_a_log=gdn_a_log, gdn_dt_bias=gdn_dt_bias, gdn_norm_w=gdn_norm_w, norm_ffn_w=norm_ffn_w,
             norm_final_w=norm_final_w.reshape(1, D),
             conv_w=jnp.transpose(g_cw, (1, 0, 2)).reshape(KC, CW),
             gdn_conv_w=jnp.transpose(g_gcw, (1, 0, 2)).reshape(KS, 3 * GW),
             w_in=g_win.reshape(NIN, D))

    relay = {}

    def relay_late(out_b):
        _, late_landed = _exchange_wait("gather_late_wait", late_started, [False] * 3, (out_b,), only=LEVEL_ONE)
        relay["started"] = _relay_start("gather_late_relay_start", late_landed)
        return relay["started"][-1]

    f = _mix_forward(w, xs, modnb, relay_late)
    g_wout, g_wfi, g_wfo = _relay_wait("gather_late_relay_wait", relay["started"], (f["out_a"],))
    w.update(w_out=g_wout.reshape(D, D), w_ffn_in=g_wfi, w_ffn_out=g_wfo.reshape(4, FB, D))
    g = _ffn_stage(w, f, xs, tgt, modnb)

    ffn_grads = [g["gw_ffn_in"], g["gw_ffn_out"].reshape(N_DEV, DFF // N_DEV, D)]
    ffn_started = _exchange_start("scatter_ffn_start", ffn_grads,
                                  [lax.empty(a.shape, a.dtype) for a in ffn_grads], [True] * 2)
    a = _out_backward(w, g, _after(modnb, ffn_started[-1]))
    out_grads = [a["gw_out"].reshape(N_DEV, D // N_DEV, D)]
    out_started = _exchange_start("scatter_out_start", out_grads,
                                  [lax.empty(t.shape, t.dtype) for t in out_grads], [True])
    h = _heads_backward(dict(w, conv_gn_w=_after(w["conv_gn_w"], out_started[-1])), f, a)

    in_grads = [h["gw_in"].reshape(N_DEV, NIN // N_DEV, D),
                jnp.transpose(h["gw_conv"].reshape(KC, N_DEV, CW // N_DEV), (1, 0, 2)),
                jnp.transpose(h["gw_gconv"].reshape(KS, N_DEV, 3 * GW // N_DEV), (1, 0, 2))]
    in_started = _exchange_start("scatter_in_start", in_grads,
                                 [lax.empty(t.shape, t.dtype) for t in in_grads], [True] * 3)
    loc = _in_backward(w, f, g, a, h, xs, _after(modnb, in_started[-1]))
    small_started = _exchange_start("gather_small_start", [loc["small"]],
                                    [lax.empty((N_DEV, SMALL_ROWS, LANES), F32)], [False])

    def own(sent):
        return lax.dynamic_index_in_dim(sent, me, axis=0, keepdims=False)

    big = {}
    (sent_fi, sent_fo), (r_fi, r_fo) = _exchange_wait("scatter_ffn_wait", ffn_started, [True] * 2,
                                                         (small_started[-1],))
    big["w_ffn_in"] = [jnp.transpose(t) for t in _reduce_adam(
        "adam_w_ffn_in", r_fi, jnp.transpose(w_ffn_in[0]), jnp.transpose(m_w_ffn_in[0]),
        jnp.transpose(v_w_ffn_in[0]), own(sent_fi))]
    big["w_ffn_out"] = _reduce_adam("adam_w_ffn_out", r_fo, w_ffn_out[0], m_w_ffn_out[0], v_w_ffn_out[0],
                                    own(sent_fo))
    (sent_out,), (r_out,) = _exchange_wait("scatter_out_wait", out_started, [True], (big["w_ffn_out"][0],))
    big["w_out"] = _reduce_adam("adam_w_out", r_out, w_out[0], m_w_out[0], v_w_out[0], own(sent_out))

    (sent_small,), (r_small,) = _exchange_wait("gather_small_wait", small_started, [False], (big["w_out"][0],))
    slot = lax.broadcasted_iota(jnp.int32, (N_DEV, 1, 1), 0)
    g_small = jnp.where(slot == me, sent_small[None], r_small)
    def views(b_, nm_, nf_, nl_, cb_, gw_, gb_, gn_, al_, dt_):
        arrs = [b_, nm_, nf_, nl_, cb_, gw_, gb_, gn_, al_, dt_]
        return {nm: t.reshape(rows, lanes) for (nm, _, rows, lanes), t in zip(SMALL_LAYOUT, arrs)}

    loss_row, res = _adam_small(
        g_small,
        views(b_ada, norm_mix_w, norm_ffn_w, norm_final_w, conv_b, conv_gn_w, conv_gn_b, gdn_norm_w, gdn_a_log,
              gdn_dt_bias),
        views(m_b_ada, m_norm_mix_w, m_norm_ffn_w, m_norm_final_w, m_conv_b, m_conv_gn_w, m_conv_gn_b,
              m_gdn_norm_w, m_gdn_a_log, m_gdn_dt_bias),
        views(v_b_ada, v_norm_mix_w, v_norm_ffn_w, v_norm_final_w, v_conv_b, v_conv_gn_w, v_conv_gn_b,
              v_gdn_norm_w, v_gdn_a_log, v_gdn_dt_bias))
    loss = loss_row[0, 0]
    small_shapes = dict(b_ada=(1, 6 * D), norm_mix_w=(1, D), norm_ffn_w=(1, D), norm_final_w=(D,),
                        conv_b=(1, CW), conv_gn_w=(1, CW), conv_gn_b=(1, CW), gdn_norm_w=(1, DH),
                        gdn_a_log=(1, NH), gdn_dt_bias=(1, NH))
    res = [{nm: t.reshape(small_shapes[nm]) for nm, t in kind.items()} for kind in res]

    dmod_rows = g_small[:, 0:48, :].reshape(N_DEV, 6 * D)
    dmod_sh = lax.dynamic_slice_in_dim(dmod_rows, me * (6 * D // N_DEV), 6 * D // N_DEV, axis=1)

    big["w_ada"] = _ada_adam(c_all, dmod_sh, w_ada[0], m_w_ada[0], v_w_ada[0])
    (sent_in, sent_cw, sent_gcw), (r_in, r_cw, r_gcw) = _exchange_wait(
        "scatter_in_wait", in_started, [True] * 3, (big["w_ada"][0],))
    big["w_in"] = [jnp.transpose(t) for t in _reduce_adam(
        "adam_w_in", r_in, jnp.transpose(w_in[0]), jnp.transpose(m_w_in[0]), jnp.transpose(v_w_in[0]),
        own(sent_in))]
    big["conv_w"] = _reduce_adam("adam_conv_w", r_cw, conv_w[0], m_conv_w[0], v_conv_w[0], own(sent_cw))
    big["gdn_conv_w"] = _reduce_adam("adam_gdn_conv_w", r_gcw, gdn_conv_w[0], m_gdn_conv_w[0], v_gdn_conv_w[0],
                                     own(sent_gcw))
    outs = [loss, loc["grad_x"].reshape(1, S, D)]
    for kind in range(4):
        for nm in WEIGHT_NAMES:
            outs.append(big[nm][kind][None] if nm in big else res[kind][nm])
    return tuple(outs)
```

```python
import functools

import jax
import jax.numpy as jnp
from jax import lax
from jax.experimental import pallas as pl
from jax.experimental.pallas import tpu as pltpu

F32 = jnp.float32
BF16 = jnp.bfloat16
HI = lax.Precision.HIGHEST
MESH = pl.DeviceIdType.MESH

N_DEV = 8
S = 2048
D = 1024
TM = 256
NT = S // TM
CW = 512
KC = 31
NG = 8
GSZ = CW // NG
HALO = 32
GW = 512
NH = 4
DH = 128
KS = 4
SH = 8
CL = 64
NCH = S // CL
NMAIN = 2 * CW + 4 * GW
NIN = NMAIN + 2 * NH
DFF = 2816
FB = DFF // 4
EPS = 1e-6
QSCALE = DH ** -0.5
LANES = 128
SMALL_ROWS = 88

ADAM_LR = 0.001
ADAM_B1 = 0.9
ADAM_B2 = 0.999
ADAM_EPS = 1e-08
ADAM_WD = 0.01
ADAM_STEP = 10
BC1 = 1.0 - ADAM_B1 ** ADAM_STEP
BC2 = 1.0 - ADAM_B2 ** ADAM_STEP

MIB = 1024 * 1024
VMEM_LIMIT_MIB = 32


def _params(limit_mib=VMEM_LIMIT_MIB, **kw):
    return pltpu.CompilerParams(vmem_limit_bytes=limit_mib * MIB, **kw)


def _sig(x):
    return jax.nn.sigmoid(x)


GP = BF16


def _operands(a, b, prec):
    if prec is BF16:
        return a.astype(BF16), b.astype(BF16), None
    return a, b, prec


def _dot(a, b, prec=None):
    a, b, prec = _operands(a, b, prec)
    return jnp.dot(a, b, preferred_element_type=F32, precision=prec)


def _dot_nt(a, b, prec=None):
    a, b, prec = _operands(a, b, prec)
    return lax.dot_general(a, b, (((1,), (1,)), ((), ())), preferred_element_type=F32, precision=prec)


def _dot_tn(a, b, prec=None):
    a, b, prec = _operands(a, b, prec)
    return lax.dot_general(a, b, (((0,), (0,)), ((), ())), preferred_element_type=F32, precision=prec)


def _lockstep(gens):
    gens = list(gens)
    while gens:
        alive = []
        for g in gens:
            try:
                next(g)
                alive.append(g)
            except StopIteration:
                pass
        gens = alive


def _rowsum(x):
    return jnp.sum(x, axis=-1, keepdims=True)


def _colsum(x):
    return jnp.sum(x, axis=0, keepdims=True)


def _mod(mod_ref, b_ref, k):
    return mod_ref[:, k * D:(k + 1) * D] + b_ref[:, k * D:(k + 1) * D]


def _const(shape):
    nd = len(shape)
    return pl.BlockSpec(shape, lambda *_: (0,) * nd)


def _const1(shape):
    nd = len(shape)
    return pl.BlockSpec(shape, lambda *_: (0,) * nd, pipeline_mode=pl.Buffered(1))


PEER_FLIPS = [(dx, dy, dc) for dx in (0, 1) for dy in (0, 1) for dc in (0, 1)][1:]


def _after(x, token):
    return x + token[0:1, 0:1].astype(x.dtype).reshape((1,) * x.ndim)


def _exchange(name, srcs, per_dest, seed_only=(), with_token=False):
    n = len(srcs)
    out_shape = []
    for a, pd in zip(srcs, per_dest):
        blk = a.shape[1:] if pd else a.shape
        out_shape.append(jax.ShapeDtypeStruct((N_DEV,) + tuple(blk), a.dtype))

    def body(*refs):
        src = refs[:n]
        dst = refs[n:2 * n]
        send_sems, recv_sems, local_sems = refs[-3:]
        if with_token:
            refs[2 * n][...] = jnp.zeros((8, LANES), F32)
        x, y, c = lax.axis_index("x"), lax.axis_index("y"), lax.axis_index("c")
        me = 4 * x + 2 * y + c

        def piece(i, j):
            return src[i].at[j] if per_dest[i] else src[i]

        copies = []
        for k, (dx, dy, dc) in enumerate(PEER_FLIPS):
            px = 1 - x if dx else x
            py = 1 - y if dy else y
            pc = 1 - c if dc else c
            pj = 4 * px + 2 * py + pc
            for i in range(n):
                if i in seed_only:
                    continue
                cp = pltpu.make_async_remote_copy(
                    src_ref=piece(i, pj), dst_ref=dst[i].at[me],
                    send_sem=send_sems.at[k * n + i], recv_sem=recv_sems.at[k * n + i],
                    device_id=(px, py, pc), device_id_type=MESH)
                cp.start()
                arrive = pltpu.make_async_remote_copy(
                    src_ref=piece(i, pj), dst_ref=dst[i].at[pj],
                    send_sem=send_sems.at[k * n + i], recv_sem=recv_sems.at[k * n + i],
                    device_id=(px, py, pc), device_id_type=MESH)
                copies.append((cp, arrive))
        own = []
        for i in range(n):
            lc = pltpu.make_async_copy(piece(i, me), dst[i].at[me], local_sems.at[i])
            lc.start()
            own.append(lc)
        for cp, arrive in copies:
            arrive.wait_recv()
        for cp, arrive in copies:
            cp.wait_send()
        for lc in own:
            lc.wait()

    any_spec = pl.BlockSpec(memory_space=pl.ANY)
    out_specs = [any_spec] * n
    if with_token:
        out_shape.append(jax.ShapeDtypeStruct((8, LANES), F32))
        out_specs.append(pl.BlockSpec(memory_space=pltpu.VMEM))
    return pl.pallas_call(
        body, name=name, out_shape=tuple(out_shape),
        in_specs=[any_spec] * n, out_specs=tuple(out_specs),
        scratch_shapes=[pltpu.SemaphoreType.DMA((7 * n,)), pltpu.SemaphoreType.DMA((7 * n,)),
                        pltpu.SemaphoreType.DMA((n,))],
        compiler_params=pltpu.CompilerParams(has_side_effects=True),
    )(*srcs)


CHIP_FLIPS = [(0, 1), (1, 0), (1, 1)]
LEVEL_ONE = [k for k, (dx, dy, dc) in enumerate(PEER_FLIPS) if (dx, dy, dc) == (0, 0, 1) or dc == 0]


def _chip_peers(x, y):
    return [(1 - x if dx else x, 1 - y if dy else y) for dx, dy in CHIP_FLIPS]


def _gather_two_level(name, srcs, seed_only=()):
    n = len(srcs)
    live = [i for i in range(n) if i not in seed_only]

    def body(*refs):
        src, dst = refs[:n], refs[n:2 * n]
        send_sems, recv_sems, local_sems = refs[2 * n:2 * n + 3]
        bounce = refs[2 * n + 3:]
        x, y, c = lax.axis_index("x"), lax.axis_index("y"), lax.axis_index("c")
        me = 4 * x + 2 * y + c
        sibling = (x, y, 1 - c)
        chips = _chip_peers(x, y)

        def copy(k, i, src_ref, slot, to):
            return pltpu.make_async_remote_copy(
                src_ref=src_ref, dst_ref=dst[i].at[slot], send_sem=send_sems.at[k * n + i],
                recv_sem=recv_sems.at[k * n + i], device_id=to, device_id_type=MESH)

        first = []
        for i in live:
            first.append(copy(0, i, src[i], me, sibling))
            first += [copy(1 + j, i, src[i], me, (px, py, c)) for j, (px, py) in enumerate(chips)]
        for cp in first:
            cp.start()
        up = [pltpu.make_async_copy(src[i], bounce[i], local_sems.at[i]) for i in range(n)]
        for cp in up:
            cp.start()
        for cp in up:
            cp.wait()
        own = [pltpu.make_async_copy(bounce[i], dst[i].at[me], local_sems.at[i]) for i in range(n)]
        for cp in own:
            cp.start()
        passed = []
        for j, (px, py) in enumerate(chips):
            slot = 4 * px + 2 * py + c
            for i in live:
                copy(1 + j, i, src[i], slot, (px, py, c)).wait_recv()
                fwd = copy(4 + j, i, dst[i].at[slot], slot, sibling)
                fwd.start()
                passed.append(fwd)
        for i in live:
            copy(0, i, src[i], 4 * x + 2 * y + 1 - c, sibling).wait_recv()
            for j, (px, py) in enumerate(chips):
                copy(4 + j, i, src[i], 4 * px + 2 * py + 1 - c, sibling).wait_recv()
        for cp in first + passed:
            cp.wait_send()
        for cp in own:
            cp.wait()

    any_spec = pl.BlockSpec(memory_space=pl.ANY)
    return pl.pallas_call(
        body, name=name, out_shape=tuple(jax.ShapeDtypeStruct((N_DEV,) + a.shape, a.dtype) for a in srcs),
        in_specs=[any_spec] * n, out_specs=tuple([any_spec] * n),
        scratch_shapes=[pltpu.SemaphoreType.DMA((7 * n,)), pltpu.SemaphoreType.DMA((7 * n,)),
                        pltpu.SemaphoreType.DMA((n,))] + [pltpu.VMEM(a.shape, a.dtype) for a in srcs],
        compiler_params=pltpu.CompilerParams(has_side_effects=True),
    )(*srcs)


def _relay_copy(land, sems, i, n, j, slot, sibling):
    send_sems, recv_sems = sems
    return pltpu.make_async_remote_copy(
        src_ref=land[i].at[slot], dst_ref=land[i].at[slot], send_sem=send_sems.at[j * n + i],
        recv_sem=recv_sems.at[j * n + i], device_id=sibling, device_id_type=MESH)


def _relay_start(name, lands):
    n = len(lands)

    def body(*refs):
        land = refs[:n]
        sems = refs[n], refs[n + 1]
        x, y, c = lax.axis_index("x"), lax.axis_index("y"), lax.axis_index("c")
        for j, (px, py) in enumerate(_chip_peers(x, y)):
            for i in range(n):
                _relay_copy(land, sems, i, n, j, 4 * px + 2 * py + c, (x, y, 1 - c)).start()
        refs[-1][...] = jnp.zeros((8, LANES), F32)

    return pl.pallas_call(
        body, name=name,
        out_shape=(pltpu.SemaphoreType.DMA((3 * n,)), pltpu.SemaphoreType.DMA((3 * n,)),
                   *[pltpu.HBM(a.shape, a.dtype) for a in lands], jax.ShapeDtypeStruct((8, LANES), F32)),
        in_specs=[HBM_SPEC] * n,
        out_specs=(SEM_SPEC, SEM_SPEC, *[HBM_SPEC] * n, pl.BlockSpec(memory_space=pltpu.VMEM)),
        input_output_aliases={i: 2 + i for i in range(n)},
        compiler_params=pltpu.CompilerParams(has_side_effects=DATAFLOW),
    )(*[pltpu.with_memory_space_constraint(a, pltpu.HBM) for a in lands])


def _relay_wait(name, started, after):
    n = len(started) - 3
    arrays = list(started[2:2 + n])

    def body(*refs):
        land = refs[:n]
        sems = refs[n], refs[n + 1]
        x, y, c = lax.axis_index("x"), lax.axis_index("y"), lax.axis_index("c")
        for j, (px, py) in enumerate(_chip_peers(x, y)):
            for i in range(n):
                _relay_copy(land, sems, i, n, j, 4 * px + 2 * py + c, (x, y, 1 - c)).wait_send()
                _relay_copy(land, sems, i, n, j, 4 * px + 2 * py + 1 - c, (x, y, 1 - c)).wait_recv()

    return pl.pallas_call(
        body, name=name,
        out_shape=tuple(pltpu.HBM(a.shape, a.dtype) for a in arrays),
        in_specs=[HBM_SPEC] * n + [SEM_SPEC, SEM_SPEC] + [pl.BlockSpec(memory_space=pl.ANY)] * len(after),
        out_specs=tuple([HBM_SPEC] * n),
        input_output_aliases={i: i for i in range(n)},
        compiler_params=pltpu.CompilerParams(has_side_effects=DATAFLOW),
    )(*arrays, started[0], started[1], *after)


HBM_SPEC = pl.BlockSpec(memory_space=pltpu.HBM)
SEM_SPEC = pl.BlockSpec(memory_space=pltpu.SEMAPHORE)
DATAFLOW = pltpu.SideEffectType.DATAFLOW_SIDE_EFFECTING


def _peers(only=None):
    x, y, c = lax.axis_index("x"), lax.axis_index("y"), lax.axis_index("c")
    out = []
    for k, (dx, dy, dc) in enumerate(PEER_FLIPS):
        if only is not None and k not in only:
            continue
        px = 1 - x if dx else x
        py = 1 - y if dy else y
        pc = 1 - c if dc else c
        out.append((k, (px, py, pc), 4 * px + 2 * py + pc))
    return 4 * x + 2 * y + c, out


def _exchange_start(name, srcs, lands, per_dest, only=None):
    n = len(srcs)

    def body(*refs):
        src, land = refs[:n], refs[n:2 * n]
        send_sems, recv_sems = refs[2 * n], refs[2 * n + 1]
        token = refs[-1]
        me, peers = _peers(only)
        for k, peer, pj in peers:
            for i in range(n):
                pltpu.make_async_remote_copy(
                    src_ref=src[i].at[pj] if per_dest[i] else src[i], dst_ref=land[i].at[me],
                    send_sem=send_sems.at[k * n + i], recv_sem=recv_sems.at[k * n + i],
                    device_id=peer, device_id_type=MESH).start()
        token[...] = jnp.zeros((8, LANES), F32)

    arrays = list(srcs) + list(lands)
    return pl.pallas_call(
        body, name=name,
        out_shape=(pltpu.SemaphoreType.DMA((7 * n,)), pltpu.SemaphoreType.DMA((7 * n,)),
                   *[pltpu.HBM(a.shape, a.dtype) for a in arrays], jax.ShapeDtypeStruct((8, LANES), F32)),
        in_specs=[HBM_SPEC] * (2 * n),
        out_specs=(SEM_SPEC, SEM_SPEC, *[HBM_SPEC] * (2 * n), pl.BlockSpec(memory_space=pltpu.VMEM)),
        input_output_aliases={i: 2 + i for i in range(2 * n)},
        compiler_params=pltpu.CompilerParams(has_side_effects=DATAFLOW),
    )(*[pltpu.with_memory_space_constraint(a, pltpu.HBM) for a in arrays])


def _exchange_wait(name, started, per_dest, after, only=None):
    n = (len(started) - 3) // 2
    send_sems, recv_sems = started[0], started[1]
    arrays = list(started[2:2 + 2 * n])

    def body(*refs):
        src, land = refs[:n], refs[n:2 * n]
        send, recv = refs[2 * n], refs[2 * n + 1]
        me, peers = _peers(only)
        for k, peer, pj in peers:
            for i in range(n):
                cp = pltpu.make_async_remote_copy(
                    src_ref=src[i].at[pj] if per_dest[i] else src[i], dst_ref=land[i].at[pj],
                    send_sem=send.at[k * n + i], recv_sem=recv.at[k * n + i],
                    device_id=peer, device_id_type=MESH)
                cp.wait_send()
                cp.wait_recv()

    outs = pl.pallas_call(
        body, name=name,
        out_shape=tuple(pltpu.HBM(a.shape, a.dtype) for a in arrays),
        in_specs=[HBM_SPEC] * (2 * n) + [SEM_SPEC, SEM_SPEC] + [pl.BlockSpec(memory_space=pl.ANY)] * len(after),
        out_specs=tuple([HBM_SPEC] * (2 * n)),
        input_output_aliases={i: i for i in range(2 * n)},
        compiler_params=pltpu.CompilerParams(has_side_effects=DATAFLOW),
    )(*arrays, send_sems, recv_sems, *after)
    return outs[:n], outs[n:]


def _scatter_start(name, srcs, lands, dests):
    n = len(srcs)
    lo, hi = dests

    def body(*refs):
        src, land = refs[:n], refs[n:2 * n]
        send_sems, recv_sems = refs[2 * n], refs[2 * n + 1]
        me, peers = _peers()
        for k, peer, pj in peers:
            @pl.when((pj >= lo) & (pj < hi))
            def _():
                for i in range(n):
                    pltpu.make_async_remote_copy(
                        src_ref=src[i].at[pj - lo], dst_ref=land[i].at[me],
                        send_sem=send_sems.at[k * n + i], recv_sem=recv_sems.at[k * n + i],
                        device_id=peer, device_id_type=MESH).start()
        refs[-1][...] = jnp.zeros((8, LANES), F32)

    arrays = list(srcs) + list(lands)
    return pl.pallas_call(
        body, name=name,
        out_shape=(pltpu.SemaphoreType.DMA((7 * n,)), pltpu.SemaphoreType.DMA((7 * n,)),
                   *[pltpu.HBM(a.shape, a.dtype) for a in arrays], jax.ShapeDtypeStruct((8, LANES), F32)),
        in_specs=[HBM_SPEC] * (2 * n),
        out_specs=(SEM_SPEC, SEM_SPEC, *[HBM_SPEC] * (2 * n), pl.BlockSpec(memory_space=pltpu.VMEM)),
        input_output_aliases={i: 2 + i for i in range(2 * n)},
        compiler_params=pltpu.CompilerParams(has_side_effects=DATAFLOW),
    )(*[pltpu.with_memory_space_constraint(a, pltpu.HBM) for a in arrays])


def _scatter_wait(name, started, dests, after):
    n = (len(started) - 3) // 2
    lo, hi = dests
    arrays = list(started[2:2 + 2 * n])

    def body(*refs):
        src, land = refs[:n], refs[n:2 * n]
        send, recv = refs[2 * n], refs[2 * n + 1]
        me, peers = _peers()

        def copy(k, i, pj):
            return pltpu.make_async_remote_copy(
                src_ref=src[i].at[0], dst_ref=land[i].at[pj], send_sem=send.at[k * n + i],
                recv_sem=recv.at[k * n + i], device_id=peers[k][1], device_id_type=MESH)

        for k, peer, pj in peers:
            @pl.when((pj >= lo) & (pj < hi))
            def _():
                for i in range(n):
                    copy(k, i, pj).wait_send()

        @pl.when((me >= lo) & (me < hi))
        def _():
            for k, peer, pj in peers:
                for i in range(n):
                    copy(k, i, pj).wait_recv()

    outs = pl.pallas_call(
        body, name=name,
        out_shape=tuple(pltpu.HBM(a.shape, a.dtype) for a in arrays),
        in_specs=[HBM_SPEC] * (2 * n) + [SEM_SPEC, SEM_SPEC] + [pl.BlockSpec(memory_space=pl.ANY)] * len(after),
        out_specs=tuple([HBM_SPEC] * (2 * n)),
        input_output_aliases={i: i for i in range(2 * n)},
        compiler_params=pltpu.CompilerParams(has_side_effects=DATAFLOW),
    )(*arrays, started[0], started[1], *after)
    return outs[:n], outs[n:]


def _mod_shard(c_all, w_ada):
    def body(c_ref, w_ref, o_ref):
        cv = c_ref[...]
        ca = cv * _sig(cv)
        o_ref[...] = _dot(ca.astype(BF16), w_ref[...].astype(BF16))

    return pl.pallas_call(
        body, name="mod_shard", out_shape=jax.ShapeDtypeStruct((N_DEV, w_ada.shape[1]), F32),
        compiler_params=_params(),
    )(c_all, w_ada)


def _fwd_in(x, nw1, modnb, bada, w_main, w_ba):
    def body(x_ref, nw_ref, mod_ref, b_ref, wm_ref, wb_ref, pm_ref, pb_ref, hb_ref):
        xv = x_ref[...]
        r = lax.rsqrt(jnp.mean(xv * xv, axis=-1, keepdims=True) + EPS)
        h = (xv * r * nw_ref[...]) * (1.0 + _mod(mod_ref, b_ref, 1)) + _mod(mod_ref, b_ref, 0)
        hb = h.astype(BF16)
        hb_ref[...] = hb
        pm_ref[...] = _dot_nt(hb, wm_ref[...])
        pb_ref[...] = _dot_nt(hb, wb_ref[...])

    return pl.pallas_call(
        body, name="fwd_in", grid=(NT,),
        in_specs=[pl.BlockSpec((TM, D), lambda i: (i, 0)), _const((1, D)), _const((1, 6 * D)), _const((1, 6 * D)),
                  _const((NMAIN, D)), _const((LANES, D))],
        out_specs=(pl.BlockSpec((TM, NMAIN), lambda i: (i, 0)), pl.BlockSpec((TM, LANES), lambda i: (i, 0)),
                   pl.BlockSpec((TM, D), lambda i: (i, 0))),
        out_shape=(jax.ShapeDtypeStruct((S, NMAIN), F32), jax.ShapeDtypeStruct((S, LANES), F32),
                   jax.ShapeDtypeStruct((S, D), BF16)),
        compiler_params=_params(dimension_semantics=("arbitrary",)),
    )(x, nw1, modnb, bada, w_main, w_ba)


def _group_mean_matrix():
    ii = lax.broadcasted_iota(jnp.int32, (CW, CW), 0) // GSZ
    jj = lax.broadcasted_iota(jnp.int32, (CW, CW), 1) // GSZ
    return jnp.where(ii == jj, 1.0 / GSZ, 0.0).astype(F32)


SUB = 8
SHIFT_ROWS = HALO + TM - SUB


def _fill_shifted(buf, sh):
    for b in range(1, SUB):
        sh[b - 1] = buf[b:b + SHIFT_ROWS, :]


def _rows_at(buf, sh, off):
    a, b = divmod(off, SUB)
    if b == 0:
        return buf[off:off + TM, :]
    return sh[b - 1, SUB * a:SUB * a + TM, :]


def _group_mean(x, pm):
    hi = x.astype(BF16)
    r1 = x - hi.astype(F32)
    mid = r1.astype(BF16)
    lo = (r1 - mid.astype(F32)).astype(BF16)
    return _dot(hi, pm) + _dot(mid, pm) + _dot(lo, pm)


def _conf_fwd(p_main, conv_w, conv_b, gn_w, gn_b):
    def body(a_ref, g_ref, w_ref, b_ref, gw_ref, gb_ref, y_ref, oa_ref, ubuf, ush):
        i = pl.program_id(0)

        @pl.when(i == 0)
        def _():
            ubuf[0:HALO, :] = jnp.zeros((HALO, CW), F32)

        ubuf[HALO:HALO + TM, :] = a_ref[...] * _sig(g_ref[...])
        _fill_shifted(ubuf, ush)
        acc = jnp.zeros((TM, CW), F32) + b_ref[...]
        for k in range(KC):
            acc = acc + w_ref[k:k + 1, :] * _rows_at(ubuf, ush, HALO - (KC - 1) + k)
        y_ref[...] = acc
        ubuf[0:HALO, :] = ubuf[TM:TM + HALO, :]
        pm = _group_mean_matrix().astype(BF16)
        dlt = acc - _group_mean(acc, pm)
        var = _group_mean(dlt * dlt, pm)
        o = dlt * lax.rsqrt(var + EPS) * gw_ref[...] + gb_ref[...]
        oa_ref[...] = o * _sig(o)

    return pl.pallas_call(
        body, name="conf_fwd", grid=(NT,),
        in_specs=[pl.BlockSpec((TM, CW), lambda i: (i, 0)), pl.BlockSpec((TM, CW), lambda i: (i, 1)),
                  _const((KC, CW)), _const((1, CW)), _const((1, CW)), _const((1, CW))],
        out_specs=(pl.BlockSpec((TM, CW), lambda i: (i, 0)), pl.BlockSpec((TM, CW), lambda i: (i, 0))),
        out_shape=(jax.ShapeDtypeStruct((S, CW), F32), jax.ShapeDtypeStruct((S, CW), F32)),
        scratch_shapes=[pltpu.VMEM((HALO + TM, CW), F32), pltpu.VMEM((SUB - 1, SHIFT_ROWS, CW), F32)],
        compiler_params=_params(dimension_semantics=("arbitrary",)),
    )(p_main, p_main, conv_w, conv_b, gn_w, gn_b)


def _tri_iota():
    ii = lax.broadcasted_iota(jnp.int32, (CL, CL), 0)
    jj = lax.broadcasted_iota(jnp.int32, (CL, CL), 1)
    return ii, jj


def _gdn_gates(ba, alog_l, dt_l):
    beta_all = _sig(ba)
    xg = ba + dt_l
    sp = jnp.maximum(xg, 0.0) + jnp.log(1.0 + jnp.exp(-jnp.abs(xg)))
    neg_a = -jnp.exp(alog_l)
    return beta_all, neg_a * sp, xg, neg_a


def _ones_dot(ones, x):
    hi = x.astype(BF16)
    r1 = x - hi.astype(F32)
    mid = r1.astype(BF16)
    lo = (r1 - mid.astype(F32)).astype(BF16)
    return _dot(ones, hi) + _dot(ones, mid) + _dot(ones, lo)


def _gdn_cumsum(g_all):
    ii, jj = _tri_iota()
    low = jnp.where(ii >= jj, 1.0, 0.0).astype(BF16)
    gcum = _ones_dot(low, g_all)
    return gcum, jnp.transpose(gcum)


def _split(x):
    hi = x.astype(BF16)
    return hi, (x - hi.astype(F32)).astype(BF16)


def _dot_split(a, b):
    (ah, al), (bh, bl) = a, b
    return _dot(ah, bh) + (_dot(ah, bl) + _dot(al, bh))


def _unit_lower_inverses(mats):
    ii, jj = _tri_iota()
    eye = jnp.where(ii == jj, 1.0, 0.0).astype(F32)
    ts = [eye - a for a in mats]
    ps = [_dot_split(s, s) for s in map(_split, mats)]
    for _ in range(4):
        sp = [_split(p) for p in ps]
        ts = [t + _dot_split(_split(t), s) for t, s in zip(ts, sp)]
        ps = [_dot_split(s, s) for s in sp]
    return [t + _dot_split(_split(t), _split(p)) for t, p in zip(ts, ps)]


def _head_terms(qh, kh, beta, gcol, grow):
    ii, jj = _tri_iota()
    causal = ii >= jj
    strict = ii > jj
    rq = lax.rsqrt(_rowsum(qh * qh) + EPS)
    rk = lax.rsqrt(_rowsum(kh * kh) + EPS)
    qn = qh * rq
    kn = kh * rk
    qs = qn * QSCALE
    decay = jnp.where(causal, jnp.exp(jnp.where(causal, gcol - grow, 0.0)), 0.0)
    gam = jnp.exp(gcol)
    gl = gcol[CL - 1:CL, :]
    kds = jnp.exp(gl - gcol)
    cd = jnp.exp(gl)
    kb = kn * beta
    a = jnp.where(strict, _dot_nt(kb, kn, GP) * decay, 0.0)
    qk = jnp.where(causal, _dot_nt(qs, kn, GP) * decay, 0.0)
    return dict(rq=rq, rk=rk, qn=qn, kn=kn, qs=qs, decay=decay, gam=gam, kds=kds, cd=cd, kb=kb, a=a, qk=qk,
                causal=causal, strict=strict)


def _short_conv(w_ref, buf, rows=CL):
    acc = w_ref[0:1, :] * buf[SH - KS + 1:SH - KS + 1 + rows, :]
    for k in range(1, KS):
        off = SH - (KS - 1) + k
        acc = acc + w_ref[k:k + 1, :] * buf[off:off + rows, :]
    return acc


CPS = 4
TG = CPS * CL


def _gdn_prep(p_main, p_ba, gdn_conv_w, alog_l, dt_l):
    def body(q_ref, k_ref, v_ref, qh_ref, kh_ref, vh_ref, ba_ref, w_ref, al_ref, dt_ref,
             wo_ref, uo_ref, qg_ref, kd_ref, qk_ref, cd_ref, t_ref, xbuf):
        i = pl.program_id(0)
        first = i == 0
        xbuf[0:SH, 0:GW] = jnp.where(first, 0.0, qh_ref[...])
        xbuf[0:SH, GW:2 * GW] = jnp.where(first, 0.0, kh_ref[...])
        xbuf[0:SH, 2 * GW:3 * GW] = jnp.where(first, 0.0, vh_ref[...])
        xbuf[SH:SH + TG, 0:GW] = q_ref[...]
        xbuf[SH:SH + TG, GW:2 * GW] = k_ref[...]
        xbuf[SH:SH + TG, 2 * GW:3 * GW] = v_ref[...]
        conv = _short_conv(w_ref, xbuf, TG)
        qkv = conv * _sig(conv)
        beta_all, g_all, _, _ = _gdn_gates(ba_ref[...], al_ref[...], dt_ref[...])
        lane = lax.broadcasted_iota(jnp.int32, (8, LANES), 1)
        cums = [_gdn_cumsum(g_all[cc * CL:(cc + 1) * CL, :]) for cc in range(CPS)]
        pairs = [(cc, h) for cc in range(CPS) for h in range(NH)]
        terms, vbs = [], []
        for cc, h in pairs:
            r0, lo = cc * CL, h * DH
            beta = beta_all[r0:r0 + CL, h:h + 1]
            gcum, gcum_t = cums[cc]
            terms.append(_head_terms(qkv[r0:r0 + CL, lo:lo + DH], qkv[r0:r0 + CL, GW + lo:GW + lo + DH], beta,
                                     gcum[:, NH + h:NH + h + 1], gcum_t[NH + h:NH + h + 1, :]))
            vbs.append(qkv[r0:r0 + CL, 2 * GW + lo:2 * GW + lo + DH] * beta)
        invs = _unit_lower_inverses([f["a"] for f in terms])
        cds = [jnp.zeros((8, LANES), F32) for _ in range(CPS)]
        for (cc, h), f, t, vb in zip(pairs, terms, invs, vbs):
            r0, lo = cc * CL, h * DH
            t_ref[cc, h] = t
            uo_ref[r0:r0 + CL, lo:lo + DH] = _dot(t, vb, GP)
            wo_ref[r0:r0 + CL, lo:lo + DH] = _dot(t, f["kb"] * f["gam"], GP).astype(BF16)
            qg_ref[r0:r0 + CL, lo:lo + DH] = (f["qs"] * f["gam"]).astype(BF16)
            kd_ref[r0:r0 + CL, lo:lo + DH] = (f["kn"] * f["kds"]).astype(BF16)
            qk_ref[cc, h] = f["qk"].astype(BF16)
            cds[cc] = cds[cc] + jnp.where(lane == h, f["cd"], 0.0)
        for cc in range(CPS):
            cd_ref[cc] = cds[cc]

    col = lambda j: pl.BlockSpec((TG, GW), lambda i: (i, j))
    halo = lambda j: pl.BlockSpec((SH, GW), lambda i: (jnp.maximum(i * (TG // SH) - 1, 0), j))
    tile = lambda: pl.BlockSpec((TG, GW), lambda i: (i, 0))
    sq = lambda: pl.BlockSpec((CPS, NH, CL, CL), lambda i: (i, 0, 0, 0))
    return pl.pallas_call(
        body, name="gdn_prep", grid=(NCH // CPS,),
        in_specs=[col(2), col(3), col(4), halo(2), halo(3), halo(4), pl.BlockSpec((TG, LANES), lambda i: (i, 0)),
                  _const((KS, 3 * GW)), _const((1, LANES)), _const((1, LANES))],
        out_specs=(tile(), tile(), tile(), tile(), sq(), pl.BlockSpec((CPS, 8, LANES), lambda i: (i, 0, 0)), sq()),
        out_shape=(jax.ShapeDtypeStruct((S, GW), BF16), jax.ShapeDtypeStruct((S, GW), F32),
                   jax.ShapeDtypeStruct((S, GW), BF16), jax.ShapeDtypeStruct((S, GW), BF16),
                   jax.ShapeDtypeStruct((NCH, NH, CL, CL), BF16), jax.ShapeDtypeStruct((NCH, 8, LANES), F32),
                   jax.ShapeDtypeStruct((NCH, NH, CL, CL), F32)),
        scratch_shapes=[pltpu.VMEM((SH + TG, 3 * GW), F32)],
        compiler_params=_params(dimension_semantics=("arbitrary",)),
    )(p_main, p_main, p_main, p_main, p_main, p_main, p_ba, gdn_conv_w, alog_l, dt_l)


def _gdn_scan(w_o, u_o, qg, kd, qk, cd, p_main, gdn_nw):
    def body(w_ref, u_ref, qg_ref, kd_ref, qk_ref, cd_ref, z_ref, nw_ref, ob_ref, o_ref, sin_ref, state):
        n = pl.program_id(0)

        @pl.when(n == 0)
        def _():
            state[...] = jnp.zeros((NH, DH, DH), F32)

        def head(h):
            lo = h * DH
            st = state[h]
            sin_ref[0, h] = st
            sb = st.astype(BF16)
            v_new = u_ref[:, lo:lo + DH] - _dot(w_ref[:, lo:lo + DH], sb)
            yield
            vb = v_new.astype(BF16)
            o = _dot(qg_ref[:, lo:lo + DH], sb) + _dot(qk_ref[0, h], vb)
            state[h] = st * cd_ref[0, 0:1, h:h + 1] + _dot_tn(kd_ref[:, lo:lo + DH], vb)
            yield
            o_ref[:, lo:lo + DH] = o
            r = lax.rsqrt(jnp.mean(o * o, axis=-1, keepdims=True) + EPS)
            zh = z_ref[:, lo:lo + DH]
            ob_ref[:, lo:lo + DH] = o * r * nw_ref[...] * (zh * _sig(zh))

        _lockstep(head(h) for h in range(NH))

    tile = lambda: pl.BlockSpec((CL, GW), lambda n: (n, 0))
    return pl.pallas_call(
        body, name="gdn_scan", grid=(NCH,),
        in_specs=[tile(), tile(), tile(), tile(), pl.BlockSpec((1, NH, CL, CL), lambda n: (n, 0, 0, 0)),
                  pl.BlockSpec((1, 8, LANES), lambda n: (n, 0, 0)), pl.BlockSpec((CL, GW), lambda n: (n, 5)),
                  _const((1, DH))],
        out_specs=(tile(), tile(), pl.BlockSpec((1, NH, DH, DH), lambda n: (n, 0, 0, 0))),
        out_shape=(jax.ShapeDtypeStruct((S, GW), F32), jax.ShapeDtypeStruct((S, GW), F32),
                   jax.ShapeDtypeStruct((NCH, NH, DH, DH), F32)),
        scratch_shapes=[pltpu.VMEM((NH, DH, DH), F32)],
        compiler_params=_params(dimension_semantics=("arbitrary",)),
    )(w_o, u_o, qg, kd, qk, cd, p_main, gdn_nw)


def _fwd_out(out_a, out_b, x, modnb, bada, w_out):
    def body(oa_ref, ob_ref, x_ref, mod_ref, b_ref, w_ref, x1_ref, mix_ref, oab_ref):
        oa = oa_ref[...].astype(BF16)
        ob = ob_ref[...].astype(BF16)
        oab_ref[:, 0:CW] = oa
        oab_ref[:, CW:D] = ob
        mix = _dot(oa, w_ref[0:CW, :]) + _dot(ob, w_ref[CW:D, :])
        mix_ref[...] = mix
        x1_ref[...] = x_ref[...] + _mod(mod_ref, b_ref, 2) * mix

    tile = lambda w: pl.BlockSpec((TM, w), lambda i: (i, 0))
    return pl.pallas_call(
        body, name="fwd_out", grid=(NT,),
        in_specs=[tile(CW), tile(GW), tile(D), _const((1, 6 * D)), _const((1, 6 * D)), _const((D, D))],
        out_specs=(tile(D), tile(D), tile(D)),
        out_shape=(jax.ShapeDtypeStruct((S, D), F32), jax.ShapeDtypeStruct((S, D), F32),
                   jax.ShapeDtypeStruct((S, D), BF16)),
        compiler_params=_params(dimension_semantics=("arbitrary",)),
    )(out_a, out_b, x, modnb, bada, w_out)


FFN_STATS = 8


def _ffn_forward(x1, tgt, modnb, bada, nw2, nfw, w_fi, w_fo):
    def body(x1_ref, tgt_ref, mod_ref, b_ref, nw2_ref, nfw_ref, wi_ref, wo_ref,
             hb_ref, act_ref, pre_ref, dx2_ref, dffn_ref, st_ref):
        i = pl.program_id(0)

        @pl.when(i == 0)
        def _():
            st_ref[...] = jnp.zeros((FFN_STATS, D), F32)

        sh2, sc2, gt2 = _mod(mod_ref, b_ref, 3), _mod(mod_ref, b_ref, 4), _mod(mod_ref, b_ref, 5)
        x1v = x1_ref[...]
        r2 = lax.rsqrt(jnp.mean(x1v * x1v, axis=-1, keepdims=True) + EPS)
        hb = ((x1v * r2 * nw2_ref[...]) * (1.0 + sc2) + sh2).astype(BF16)
        hb_ref[...] = hb
        ffn = jnp.zeros((TM, D), F32)
        for j in range(4):
            fgj = _dot_nt(hb, wi_ref[j])
            fuj = _dot_nt(hb, wi_ref[j + 4])
            pre_ref[j] = fgj.astype(BF16)
            pre_ref[j + 4] = fuj.astype(BF16)
            aj = (fgj * _sig(fgj) * fuj).astype(BF16)
            act_ref[j] = aj
            ffn = ffn + _dot(aj, wo_ref[j])
        x2 = x1v + gt2 * ffn
        r3 = lax.rsqrt(jnp.mean(x2 * x2, axis=-1, keepdims=True) + EPS)
        xr3 = x2 * r3
        err = xr3 * nfw_ref[...] - tgt_ref[...]
        loss = 0.5 * jnp.sum(jnp.mean(err * err, axis=-1, keepdims=True), axis=0, keepdims=True)
        dy = err * (1.0 / D)
        st_ref[0:1, :] += _colsum(dy * xr3)
        dyr = dy * nfw_ref[...]
        dx2 = r3 * (dyr - xr3 * jnp.mean(dyr * xr3, axis=-1, keepdims=True))
        st_ref[1:2, :] += _colsum(dx2 * ffn)
        st_ref[5:6, :] += jnp.broadcast_to(loss, (1, D))
        dx2_ref[...] = dx2
        dffn_ref[...] = (gt2 * dx2).astype(BF16)

    tile = lambda w: pl.BlockSpec((TM, w), lambda i: (i, 0))
    return pl.pallas_call(
        body, name="ffn_forward", grid=(NT,),
        in_specs=[tile(D), tile(D), _const((1, 6 * D)), _const((1, 6 * D)), _const((1, D)), _const((1, D)),
                  _const1((N_DEV, FB, D)), _const1((4, FB, D))],
        out_specs=(tile(D), pl.BlockSpec((4, TM, FB), lambda i: (0, i, 0)),
                   pl.BlockSpec((N_DEV, TM, FB), lambda i: (0, i, 0)), tile(D), tile(D), _const((FFN_STATS, D))),
        out_shape=(jax.ShapeDtypeStruct((S, D), BF16), jax.ShapeDtypeStruct((4, S, FB), BF16),
                   jax.ShapeDtypeStruct((N_DEV, S, FB), BF16), jax.ShapeDtypeStruct((S, D), F32),
                   jax.ShapeDtypeStruct((S, D), BF16), jax.ShapeDtypeStruct((FFN_STATS, D), F32)),
        compiler_params=_params(42, dimension_semantics=("arbitrary",)),
    )(x1, tgt, modnb, bada, nw2, nfw, w_fi, w_fo)


def _ffn_backward(dffn, pre, x1, dx2, modnb, bada, nw2, w_fi, w_fo):
    def body(dffn_ref, pre_ref, x1_ref, dx2_ref, mod_ref, b_ref, nw2_ref, wi_ref, wo_ref, df_ref, dx1_ref, st_ref):
        i = pl.program_id(0)

        @pl.when(i == 0)
        def _():
            st_ref[...] = jnp.zeros((FFN_STATS, D), F32)

        dffn = dffn_ref[...]
        dh = jnp.zeros((TM, D), F32)
        for j in range(4):
            fg = pre_ref[j].astype(F32)
            fu = pre_ref[j + 4].astype(F32)
            sg = _sig(fg)
            dact = _dot_nt(dffn, wo_ref[j])
            dfg = (dact * fu * (sg * (1.0 + fg * (1.0 - sg)))).astype(BF16)
            dfu = (dact * (fg * sg)).astype(BF16)
            df_ref[j] = dfg
            df_ref[j + 4] = dfu
            dh = dh + _dot(dfg, wi_ref[j]) + _dot(dfu, wi_ref[j + 4])
        x1v = x1_ref[...]
        r2 = lax.rsqrt(jnp.mean(x1v * x1v, axis=-1, keepdims=True) + EPS)
        xr2 = x1v * r2
        st_ref[2:3, :] += _colsum(dh)
        st_ref[3:4, :] += _colsum(dh * (xr2 * nw2_ref[...]))
        dxn = dh * (1.0 + _mod(mod_ref, b_ref, 4))
        st_ref[4:5, :] += _colsum(dxn * xr2)
        dxr = dxn * nw2_ref[...]
        dx1_ref[...] = dx2_ref[...] + r2 * (dxr - xr2 * jnp.mean(dxr * xr2, axis=-1, keepdims=True))

    tile = lambda w: pl.BlockSpec((TM, w), lambda i: (i, 0))
    wide = lambda: pl.BlockSpec((N_DEV, TM, FB), lambda i: (0, i, 0))
    return pl.pallas_call(
        body, name="ffn_backward", grid=(NT,),
        in_specs=[tile(D), wide(), tile(D), tile(D), _const((1, 6 * D)), _const((1, 6 * D)), _const((1, D)),
                  _const1((N_DEV, FB, D)), _const1((4, FB, D))],
        out_specs=(wide(), tile(D), _const((FFN_STATS, D))),
        out_shape=(jax.ShapeDtypeStruct((N_DEV, S, FB), BF16), jax.ShapeDtypeStruct((S, D), F32),
                   jax.ShapeDtypeStruct((FFN_STATS, D), F32)),
        compiler_params=_params(44, dimension_semantics=("arbitrary",)),
    )(dffn, pre, x1, dx2, modnb, bada, nw2, w_fi, w_fo)


def _grad_w(name, a, b, nb):
    m, n = a.shape[1], b.shape[1]

    def body(a_ref, b_ref, o_ref):
        o_ref[...] = _dot_tn(a_ref[...], b_ref[...]).astype(BF16)

    return pl.pallas_call(
        body, name=name, grid=(m // nb,),
        in_specs=[pl.BlockSpec((S, nb), lambda j: (0, j)), _const((S, n))],
        out_specs=pl.BlockSpec((nb, n), lambda j: (j, 0)),
        out_shape=jax.ShapeDtypeStruct((m, n), BF16),
        compiler_params=_params(dimension_semantics=("arbitrary",)),
    )(a, b)


def _grad_w_ffn_in(hb2, df):
    def body(a_ref, b_ref, o_ref):
        o_ref[0] = _dot_tn(b_ref[0], a_ref[...]).astype(BF16)

    return pl.pallas_call(
        body, name="grad_w_ffn_in", grid=(N_DEV,),
        in_specs=[_const((S, D)), pl.BlockSpec((1, S, FB), lambda j: (j, 0, 0))],
        out_specs=pl.BlockSpec((1, FB, D), lambda j: (j, 0, 0)),
        out_shape=jax.ShapeDtypeStruct((N_DEV, FB, D), BF16),
        compiler_params=_params(dimension_semantics=("arbitrary",)),
    )(hb2, df)


def _grad_w_ffn_out(act, dffn):
    def body(a_ref, b_ref, o_ref):
        o_ref[0] = _dot_tn(a_ref[0], b_ref[...]).astype(BF16)

    return pl.pallas_call(
        body, name="grad_w_ffn_out", grid=(4,),
        in_specs=[pl.BlockSpec((1, S, FB), lambda j: (j, 0, 0)), _const((S, D))],
        out_specs=pl.BlockSpec((1, FB, D), lambda j: (j, 0, 0)),
        out_shape=jax.ShapeDtypeStruct((4, FB, D), BF16),
        compiler_params=_params(dimension_semantics=("arbitrary",)),
    )(act, dffn)


def _bwd_out(dx1, mix, modnb, bada, w_out):
    def body(dx_ref, mix_ref, mod_ref, b_ref, w_ref, dmix_ref, doa_ref, dob_ref, st_ref):
        i = pl.program_id(0)

        @pl.when(i == 0)
        def _():
            st_ref[...] = jnp.zeros((8, D), F32)

        dx = dx_ref[...]
        st_ref[0:1, :] += _colsum(dx * mix_ref[...])
        dmix = (_mod(mod_ref, b_ref, 2) * dx).astype(BF16)
        dmix_ref[...] = dmix
        doa_ref[...] = _dot_nt(dmix, w_ref[0:CW, :])
        dob_ref[...] = _dot_nt(dmix, w_ref[CW:D, :])

    tile = lambda w: pl.BlockSpec((TM, w), lambda i: (i, 0))
    return pl.pallas_call(
        body, name="bwd_out", grid=(NT,),
        in_specs=[tile(D), tile(D), _const((1, 6 * D)), _const((1, 6 * D)), _const((D, D))],
        out_specs=(tile(D), tile(CW), tile(GW), _const((8, D))),
        out_shape=(jax.ShapeDtypeStruct((S, D), BF16), jax.ShapeDtypeStruct((S, CW), F32),
                   jax.ShapeDtypeStruct((S, GW), F32), jax.ShapeDtypeStruct((8, D), F32)),
        compiler_params=_params(dimension_semantics=("arbitrary",)),
    )(dx1, mix, modnb, bada, w_out)


CONF_STATS = 40


def _conf_bwd(d_out_a, y, p_main, conv_w, gn_w, gn_b):
    def body(do_ref, y_ref, a_ref, g_ref, ah_ref, gh_ref, w_ref, gw_ref, gb_ref, dp_ref, st_ref,
             ubuf, dybuf, ush, dysh):
        i = pl.program_id(0)

        @pl.when(i == 0)
        def _():
            st_ref[...] = jnp.zeros((CONF_STATS, CW), F32)
            dybuf[TM:TM + HALO, :] = jnp.zeros((HALO, CW), F32)

        pm = _group_mean_matrix().astype(BF16)
        yv = y_ref[...]
        dlt = yv - _group_mean(yv, pm)
        rstd = lax.rsqrt(_group_mean(dlt * dlt, pm) + EPS)
        un = dlt * rstd
        o = un * gw_ref[...] + gb_ref[...]
        so = _sig(o)
        d_o = do_ref[...] * (so * (1.0 + o * (1.0 - so)))
        st_ref[33:34, :] += _colsum(d_o)
        st_ref[32:33, :] += _colsum(d_o * un)
        dun = d_o * gw_ref[...]
        dy = rstd * (dun - _group_mean(dun, pm) - un * _group_mean(dun * un, pm))
        st_ref[31:32, :] += _colsum(dy)
        dybuf[0:TM, :] = dy
        _fill_shifted(dybuf, dysh)

        a = a_ref[...]
        sg = _sig(g_ref[...])
        first = i == NT - 1
        ubuf[0:HALO, :] = jnp.where(first, 0.0, ah_ref[...] * _sig(gh_ref[...]))
        ubuf[HALO:HALO + TM, :] = a * sg
        _fill_shifted(ubuf, ush)
        du = jnp.zeros((TM, CW), F32)
        for k in range(KC):
            st_ref[k:k + 1, :] += _colsum(dy * _rows_at(ubuf, ush, HALO - (KC - 1) + k))
            du = du + w_ref[k:k + 1, :] * _rows_at(dybuf, dysh, KC - 1 - k)
        dybuf[TM:TM + HALO, :] = dybuf[0:HALO, :]
        dp_ref[:, 0:CW] = (du * sg).astype(BF16)
        dp_ref[:, CW:2 * CW] = (du * a * sg * (1.0 - sg)).astype(BF16)

    rev = lambda w, j=0: pl.BlockSpec((TM, w), lambda i: (NT - 1 - i, j))
    halo = lambda j: pl.BlockSpec((HALO, CW), lambda i: (jnp.maximum((NT - 1 - i) * (TM // HALO) - 1, 0), j))
    return pl.pallas_call(
        body, name="conf_bwd", grid=(NT,),
        in_specs=[rev(CW), rev(CW), rev(CW, 0), rev(CW, 1), halo(0), halo(1),
                  _const((KC, CW)), _const((1, CW)), _const((1, CW))],
        out_specs=(rev(2 * CW), _const((CONF_STATS, CW))),
        out_shape=(jax.ShapeDtypeStruct((S, 2 * CW), BF16), jax.ShapeDtypeStruct((CONF_STATS, CW), F32)),
        scratch_shapes=[pltpu.VMEM((HALO + TM, CW), F32), pltpu.VMEM((TM + HALO, CW), F32),
                        pltpu.VMEM((SUB - 1, SHIFT_ROWS, CW), F32), pltpu.VMEM((SUB - 1, SHIFT_ROWS, CW), F32)],
        compiler_params=_params(dimension_semantics=("arbitrary",)),
    )(d_out_a, y, p_main, p_main, p_main, p_main, conv_w, gn_w, gn_b)


GDN_STATS = 8


def _gdn_bwd(d_out_b, o_pre, s_in, t_inv, p_main, p_ba, gdn_conv_w, alog_l, dt_l, gdn_nw):
    def body(dob_ref, o_ref, sin_ref, t_ref, q_ref, k_ref, v_ref, z_ref, qh_ref, kh_ref, vh_ref, ba_ref,
             w_ref, al_ref, dt_ref, nw_ref, dp_ref, dba_ref, st_ref, xbuf, dcbuf, dstate):
        n = pl.program_id(0)

        @pl.when(n == 0)
        def _():
            st_ref[...] = jnp.zeros((GDN_STATS, 3 * GW), F32)
            dcbuf[CL:CL + SH, :] = jnp.zeros((SH, 3 * GW), F32)
            dstate[...] = jnp.zeros((NH, DH, DH), F32)

        first = n == NCH - 1
        xbuf[0:SH, 0:GW] = jnp.where(first, 0.0, qh_ref[...])
        xbuf[0:SH, GW:2 * GW] = jnp.where(first, 0.0, kh_ref[...])
        xbuf[0:SH, 2 * GW:3 * GW] = jnp.where(first, 0.0, vh_ref[...])
        xbuf[SH:SH + CL, 0:GW] = q_ref[...]
        xbuf[SH:SH + CL, GW:2 * GW] = k_ref[...]
        xbuf[SH:SH + CL, 2 * GW:3 * GW] = v_ref[...]
        conv = _short_conv(w_ref, xbuf)
        sc = _sig(conv)
        qkv = conv * sc
        ba = ba_ref[...]
        beta_all, g_all, xg, neg_a = _gdn_gates(ba, al_ref[...], dt_ref[...])
        gcum, gcum_t = _gdn_cumsum(g_all)
        lane = lax.broadcasted_iota(jnp.int32, (CL, LANES), 1)
        row = lax.broadcasted_iota(jnp.int32, (CL, 1), 0)
        acc = dict(dgcum=jnp.zeros((CL, LANES), F32), dbeta=jnp.zeros((CL, LANES), F32))

        def head(h):
            lo = h * DH
            qh = qkv[:, lo:lo + DH]
            kh = qkv[:, GW + lo:GW + lo + DH]
            vh = qkv[:, 2 * GW + lo:2 * GW + lo + DH]
            beta = beta_all[:, h:h + 1]
            f = _head_terms(qh, kh, beta, gcum[:, NH + h:NH + h + 1], gcum_t[NH + h:NH + h + 1, :])
            qn, kn, qs, kb, gam, kds, cd, decay = (f[s] for s in ("qn", "kn", "qs", "kb", "gam", "kds", "cd", "decay"))
            t = t_ref[0, h]
            st = sin_ref[0, h]
            vb = vh * beta
            kbg = kb * gam
            u = _dot(t, vb, GP)
            w = _dot(t, kbg, GP)
            yield
            v_new = u - _dot(w, st, GP)
            q_dec = qs * gam
            k_dec = kn * kds

            o = o_ref[:, lo:lo + DH]
            zh = z_ref[:, lo:lo + DH]
            sz = _sig(zh)
            r = lax.rsqrt(jnp.mean(o * o, axis=-1, keepdims=True) + EPS)
            orr = o * r
            d_out = dob_ref[:, lo:lo + DH]
            dz = d_out * (orr * nw_ref[...]) * (sz * (1.0 + zh * (1.0 - sz)))
            don = d_out * (zh * sz)
            st_ref[4:5, 0:DH] += _colsum(don * orr)
            tt = don * nw_ref[...]
            d_o = r * (tt - orr * jnp.mean(tt * orr, axis=-1, keepdims=True))

            yield
            ds_out = dstate[h]
            dv_new = _dot_tn(f["qk"], d_o, GP) + _dot(k_dec, ds_out, GP)
            dqk = jnp.where(f["causal"], _dot_nt(d_o, v_new, GP), 0.0)
            dq_dec = _dot_nt(d_o, st, GP)
            dk_dec = _dot_nt(v_new, ds_out, GP)
            yield
            dstate[h] = _dot_tn(q_dec, d_o, GP) + cd * ds_out - _dot_tn(w, dv_new, GP)
            dcd = jnp.sum(_rowsum(st * ds_out), axis=0, keepdims=True)
            dw = -_dot_nt(dv_new, st, GP)
            dvb = _dot_tn(t, dv_new, GP)
            yield
            dt_m = _dot_nt(dv_new, vb, GP) + _dot_nt(dw, kbg, GP)
            dkbg = _dot_tn(t, dw, GP)
            yield
            dtt = _dot_nt(dt_m, t, GP)
            yield
            da = jnp.where(f["strict"], -_dot_tn(t, dtt, GP), 0.0)
            yield
            dad = da * decay
            dqkd = dqk * decay
            dkb = _dot(dad, kn, GP) + dkbg * gam
            dkn = _dot_tn(dad, kb, GP) + _dot_tn(dqkd, qs, GP) + dk_dec * kds + dkb * beta
            dqs = _dot(dqkd, kn, GP) + dq_dec * gam
            yield
            m = da * f["a"] + dqk * f["qk"]
            tk = _rowsum(dk_dec * k_dec)
            dgl = jnp.sum(tk, axis=0, keepdims=True) + dcd * cd
            dgc = (_rowsum(m) - _rowsum(jnp.transpose(m)) + _rowsum(dq_dec * q_dec) - tk + _rowsum(dkbg * kbg)
                   + jnp.where(row == CL - 1, dgl, 0.0))
            dbeta = _rowsum(dkb * kn) + _rowsum(dvb * vh)
            acc["dgcum"] = acc["dgcum"] + jnp.where(lane == NH + h, dgc, 0.0)
            acc["dbeta"] = acc["dbeta"] + jnp.where(lane == h, dbeta, 0.0)
            dvh = dvb * beta
            dqn = dqs * QSCALE
            dqh = f["rq"] * (dqn - qn * _rowsum(dqn * qn))
            dkh = f["rk"] * (dkn - kn * _rowsum(dkn * kn))
            dsilu = lambda c0: sc[:, c0:c0 + DH] * (1.0 + conv[:, c0:c0 + DH] * (1.0 - sc[:, c0:c0 + DH]))
            dcbuf[0:CL, lo:lo + DH] = dqh * dsilu(lo)
            dcbuf[0:CL, GW + lo:GW + lo + DH] = dkh * dsilu(GW + lo)
            dcbuf[0:CL, 2 * GW + lo:2 * GW + lo + DH] = dvh * dsilu(2 * GW + lo)
            dp_ref[:, 3 * GW + lo:3 * GW + lo + DH] = dz.astype(BF16)

        _lockstep(head(h) for h in range(NH))
        dgcum_all, dbeta_all = acc["dgcum"], acc["dbeta"]

        ii, jj = _tri_iota()
        upper = jnp.where(ii <= jj, 1.0, 0.0).astype(BF16)
        dg_all = _ones_dot(upper, dgcum_all)
        dxg = dg_all * neg_a * _sig(xg)
        st_ref[5:6, 0:LANES] += _colsum(dg_all * g_all)
        st_ref[6:7, 0:LANES] += _colsum(dxg)
        dbl = dbeta_all * beta_all * (1.0 - beta_all)
        dba_ref[...] = jnp.where(lane < NH, dbl, jnp.where(lane < 2 * NH, dxg, 0.0)).astype(BF16)

        dconv = dcbuf[0:CL, :]
        dx = w_ref[0:1, :] * dcbuf[KS - 1:KS - 1 + CL, :]
        st_ref[0:1, :] += _colsum(dconv * xbuf[SH - KS + 1:SH - KS + 1 + CL, :])
        for k in range(1, KS):
            off = SH - (KS - 1) + k
            st_ref[k:k + 1, :] += _colsum(dconv * xbuf[off:off + CL, :])
            dx = dx + w_ref[k:k + 1, :] * dcbuf[KS - 1 - k:KS - 1 - k + CL, :]
        dcbuf[CL:CL + SH, :] = dcbuf[0:SH, :]
        dp_ref[:, 0:3 * GW] = dx.astype(BF16)

    rev = lambda w, j=0: pl.BlockSpec((CL, w), lambda n: (NCH - 1 - n, j))
    halo = lambda j: pl.BlockSpec((SH, GW), lambda n: (jnp.maximum((NCH - 1 - n) * (CL // SH) - 1, 0), j))
    blk4 = lambda a, b: pl.BlockSpec((1, NH, a, b), lambda n: (NCH - 1 - n, 0, 0, 0))
    return pl.pallas_call(
        body, name="gdn_bwd", grid=(NCH,),
        in_specs=[rev(GW), rev(GW), blk4(DH, DH), blk4(CL, CL), rev(GW, 2), rev(GW, 3), rev(GW, 4), rev(GW, 5),
                  halo(2), halo(3), halo(4), rev(LANES), _const((KS, 3 * GW)), _const((1, LANES)),
                  _const((1, LANES)), _const((1, DH))],
        out_specs=(rev(4 * GW), rev(LANES), _const((GDN_STATS, 3 * GW))),
        out_shape=(jax.ShapeDtypeStruct((S, 4 * GW), BF16), jax.ShapeDtypeStruct((S, LANES), BF16),
                   jax.ShapeDtypeStruct((GDN_STATS, 3 * GW), F32)),
        scratch_shapes=[pltpu.VMEM((SH + CL, 3 * GW), F32), pltpu.VMEM((CL + SH, 3 * GW), F32),
                        pltpu.VMEM((NH, DH, DH), F32)],
        compiler_params=_params(dimension_semantics=("arbitrary",)),
    )(d_out_b, o_pre, s_in, t_inv, p_main, p_main, p_main, p_main, p_main, p_main, p_main, p_ba,
      gdn_conv_w, alog_l, dt_l, gdn_nw)


def _bwd_in(dp_conf, dp_gdn, dp_ba, x, dx1, nw1, modnb, bada, w_main, w_ba):
    def body(dc_ref, dg_ref, db_ref, x_ref, dx1_ref, nw_ref, mod_ref, b_ref, wm_ref, wb_ref, gx_ref, st_ref):
        i = pl.program_id(0)

        @pl.when(i == 0)
        def _():
            st_ref[...] = jnp.zeros((8, D), F32)

        dh = (_dot(dc_ref[...], wm_ref[0:2 * CW, :]) + _dot(dg_ref[...], wm_ref[2 * CW:NMAIN, :])
              + _dot(db_ref[...], wb_ref[...]))
        xv = x_ref[...]
        r = lax.rsqrt(jnp.mean(xv * xv, axis=-1, keepdims=True) + EPS)
        xr = xv * r
        st_ref[0:1, :] += _colsum(dh)
        st_ref[1:2, :] += _colsum(dh * (xr * nw_ref[...]))
        dxn = dh * (1.0 + _mod(mod_ref, b_ref, 1))
        st_ref[2:3, :] += _colsum(dxn * xr)
        dxr = dxn * nw_ref[...]
        gx_ref[...] = dx1_ref[...] + r * (dxr - xr * jnp.mean(dxr * xr, axis=-1, keepdims=True))

    tile = lambda w: pl.BlockSpec((TM, w), lambda i: (i, 0))
    return pl.pallas_call(
        body, name="bwd_in", grid=(NT,),
        in_specs=[tile(2 * CW), tile(4 * GW), tile(LANES), tile(D), tile(D), _const((1, D)), _const((1, 6 * D)),
                  _const((1, 6 * D)), _const((NMAIN, D)), _const((LANES, D))],
        out_specs=(tile(D), _const((8, D))),
        out_shape=(jax.ShapeDtypeStruct((S, D), F32), jax.ShapeDtypeStruct((8, D), F32)),
        compiler_params=_params(dimension_semantics=("arbitrary",)),
    )(dp_conf, dp_gdn, dp_ba, x, dx1, nw1, modnb, bada, w_main, w_ba)


def _adamw(w, g, m, v):
    m = ADAM_B1 * m + (1.0 - ADAM_B1) * g
    v = ADAM_B2 * v + (1.0 - ADAM_B2) * (g * g)
    m_hat = m / BC1
    v_hat = v / BC2
    delta = -ADAM_LR * (m_hat / (jnp.sqrt(v_hat) + ADAM_EPS) + ADAM_WD * w)
    return delta, m, v


ADAM_BLOCK_BYTES = 6 * 1024 * 1024


def _adam_tile(rows, cols):
    padded = -(-cols // LANES) * LANES
    if N_DEV * rows * padded * 4 <= ADAM_BLOCK_BYTES:
        return rows, cols
    best = None
    for tr in range(16, rows, 16):
        if rows % tr == 0 and N_DEV * tr * padded * 4 <= ADAM_BLOCK_BYTES:
            best = tr
    if best is not None:
        return best, cols
    rows_padded = -(-rows // 16) * 16
    tc = LANES
    for cand in range(LANES, cols, LANES):
        if cols % cand == 0 and N_DEV * rows_padded * cand * 4 <= ADAM_BLOCK_BYTES:
            tc = cand
    return rows, tc


def _reduce_adam(name, parts, w, m, v, own=None, low=None, low_below=0):
    rows, cols = w.shape
    tr, tc = _adam_tile(rows, cols)

    def body(*refs):
        p_ref, w_ref, m_ref, v_ref = refs[:4]
        g_ref, d_ref, nm_ref, nv_ref = refs[-4:]
        if own is None:
            part = lambda j: p_ref[j].astype(F32)
        else:
            me = 4 * lax.axis_index("x") + 2 * lax.axis_index("y") + lax.axis_index("c")
            if low is None:
                landed = lambda j: p_ref[j]
            else:
                landed = lambda j: jnp.where(me < low_below, refs[5][j], p_ref[j])
            part = lambda j: jnp.where(me == j, refs[4][...], landed(j)).astype(F32)
        g = part(0)
        for j in range(1, N_DEV):
            g = g + part(j)
        g_ref[...] = g
        d_ref[...], nm_ref[...], nv_ref[...] = _adamw(w_ref[...], g, m_ref[...], v_ref[...])

    blk = pl.BlockSpec((tr, tc), lambda i, j: (i, j))
    sds = jax.ShapeDtypeStruct((rows, cols), F32)
    wide = pl.BlockSpec((N_DEV, tr, tc), lambda i, j: (0, i, j))
    extra = [] if own is None else [own]
    extra_specs = [blk] * len(extra)
    if low is not None:
        extra.append(low)
        extra_specs.append(wide)
    return pl.pallas_call(
        body, name=name, grid=(rows // tr, cols // tc),
        in_specs=[wide, blk, blk, blk] + extra_specs,
        out_specs=(blk, blk, blk, blk), out_shape=(sds, sds, sds, sds),
        compiler_params=_params(dimension_semantics=("arbitrary", "arbitrary")),
    )(parts, w, m, v, *extra)


def _ada_adam(c_all, dmod_sh, w, m, v):
    rows, cols = w.shape
    tr = 256

    def body(c_ref, dm_ref, w_ref, m_ref, v_ref, g_ref, d_ref, nm_ref, nv_ref):
        cv = c_ref[...]
        g = _dot_tn(cv * _sig(cv), dm_ref[...], HI)
        g_ref[...] = g
        d_ref[...], nm_ref[...], nv_ref[...] = _adamw(w_ref[...], g, m_ref[...], v_ref[...])

    blk = pl.BlockSpec((tr, cols), lambda i: (i, 0))
    sds = jax.ShapeDtypeStruct((rows, cols), F32)
    return pl.pallas_call(
        body, name="ada_adam", grid=(rows // tr,),
        in_specs=[pl.BlockSpec((N_DEV, tr), lambda i: (0, i)), _const((N_DEV, cols)), blk, blk, blk],
        out_specs=(blk, blk, blk, blk), out_shape=(sds, sds, sds, sds),
        compiler_params=_params(dimension_semantics=("arbitrary",)),
    )(c_all, dmod_sh, w, m, v)


def _lanes(a, at=0):
    return jnp.pad(a, ((0, 0), (at, LANES - at - a.shape[1])))


WEIGHT_NAMES = ["w_ada", "b_ada", "norm_mix_w", "w_in", "conv_w", "conv_b", "conv_gn_w", "conv_gn_b", "gdn_conv_w",
                "gdn_a_log", "gdn_dt_bias", "gdn_norm_w", "w_out", "norm_ffn_w", "w_ffn_in", "w_ffn_out",
                "norm_final_w"]


SMALL_LAYOUT = [("b_ada", 0, 48, LANES), ("norm_mix_w", 48, 8, LANES), ("norm_ffn_w", 56, 8, LANES),
                ("norm_final_w", 64, 8, LANES), ("conv_b", 72, 4, LANES), ("conv_gn_w", 76, 4, LANES),
                ("conv_gn_b", 80, 4, LANES), ("gdn_norm_w", 84, 1, LANES), ("gdn_a_log", 85, 1, NH),
                ("gdn_dt_bias", 86, 1, NH)]
LOSS_ROW = 87


def _adam_small(g_small, weights, m1, m2):
    names = [nm for nm, _, _, _ in SMALL_LAYOUT]
    k = len(names)

    def body(*refs):
        g_ref = refs[0]
        w_refs, m_refs, v_refs = refs[1:1 + k], refs[1 + k:1 + 2 * k], refs[1 + 2 * k:1 + 3 * k]
        loss_ref = refs[1 + 3 * k]
        outs = refs[2 + 3 * k:2 + 7 * k]
        total = refs[-1]
        g = g_ref[0]
        for j in range(1, N_DEV):
            g = g + g_ref[j]
        total[...] = g
        loss_ref[...] = total[LOSS_ROW:LOSS_ROW + 1, :]
        for i, (_, r0, rows, lanes) in enumerate(SMALL_LAYOUT):
            gp = total[r0:r0 + rows, 0:lanes]
            outs[i][...] = gp
            outs[k + i][...], outs[2 * k + i][...], outs[3 * k + i][...] = _adamw(
                w_refs[i][...], gp, m_refs[i][...], v_refs[i][...])

    shapes = [jax.ShapeDtypeStruct((rows, lanes), F32) for _, _, rows, lanes in SMALL_LAYOUT]
    res = pl.pallas_call(
        body, name="adam_small",
        out_shape=tuple([jax.ShapeDtypeStruct((1, LANES), F32)] + shapes * 4),
        scratch_shapes=[pltpu.VMEM((SMALL_ROWS, LANES), F32)],
        compiler_params=_params(),
    )(g_small, *[weights[n] for n in names], *[m1[n] for n in names], *[m2[n] for n in names])
    kinds = [dict(zip(names, res[1 + q * k:1 + (q + 1) * k])) for q in range(4)]
    return res[0], kinds


def _mix_forward(w, xs, modnb, between=None):
    w_main = w["w_in"]
    w_ba = jnp.pad(w["w_in"][NMAIN:], ((0, LANES - 2 * NH), (0, 0)))
    alog_l = _lanes(w["gdn_a_log"], NH)
    dt_l = _lanes(w["gdn_dt_bias"], NH)
    p_main, p_ba, hb1 = _fwd_in(xs, w["norm_mix_w"], modnb, w["b_ada"], w_main, w_ba)
    w_o, u_o, qg, kd, qk, cd, t_inv = _gdn_prep(p_main, p_ba, w["gdn_conv_w"], alog_l, dt_l)
    out_b, o_pre, s_in = _gdn_scan(w_o, u_o, qg, kd, qk, cd, p_main, w["gdn_norm_w"])
    conv_b = w["conv_b"] if between is None else _after(w["conv_b"], between(out_b))
    y_conv, out_a = _conf_fwd(p_main, w["conv_w"], conv_b, w["conv_gn_w"], w["conv_gn_b"])
    return dict(w_main=w_main, w_ba=w_ba, alog_l=alog_l, dt_l=dt_l, p_main=p_main, p_ba=p_ba, hb1=hb1,
                y_conv=y_conv, out_a=out_a, out_b=out_b, o_pre=o_pre, s_in=s_in, t_inv=t_inv)


def _ffn_stage(w, f, xs, tgt, modnb):
    x1, mix, oab = _fwd_out(f["out_a"], f["out_b"], xs, modnb, w["b_ada"], w["w_out"])
    hb2, act, pre, dx2, dffn, st_fwd = _ffn_forward(x1, tgt, modnb, w["b_ada"], w["norm_ffn_w"],
                                                    w["norm_final_w"], w["w_ffn_in"], w["w_ffn_out"])
    gw_ffn_out = _grad_w_ffn_out(act, dffn)
    df, dx1, st_bwd = _ffn_backward(dffn, pre, x1, dx2, modnb, w["b_ada"], w["norm_ffn_w"], w["w_ffn_in"],
                                    w["w_ffn_out"])
    gw_ffn_in = _grad_w_ffn_in(hb2, df)
    return dict(mix=mix, oab=oab, dx1=dx1, st_ffn=st_fwd + st_bwd, gw_ffn_in=gw_ffn_in, gw_ffn_out=gw_ffn_out)


def _out_backward(w, g, modnb):
    dmix, d_out_a, d_out_b, st_out = _bwd_out(g["dx1"], g["mix"], modnb, w["b_ada"], w["w_out"])
    return dict(d_out_a=d_out_a, d_out_b=d_out_b, st_out=st_out, gw_out=_grad_w("grad_w_out", g["oab"], dmix, 512))


W_IN_BLOCK = NIN // N_DEV
LOW_BLOCKS = 3
LOW_TAIL = LOW_BLOCKS * W_IN_BLOCK - 2 * CW


def _gdn_backward(w, f, a):
    dp_gdn, dp_ba, st_gdn = _gdn_bwd(a["d_out_b"], f["o_pre"], f["s_in"], f["t_inv"], f["p_main"], f["p_ba"],
                                     w["gdn_conv_w"], f["alog_l"], f["dt_l"], w["gdn_norm_w"])
    gw_gdn = _grad_w("grad_w_in_gdn", dp_gdn, f["hb1"], 512)
    gw_ba = _grad_w("grad_w_in_ba", dp_ba, f["hb1"], LANES)
    gw_high = jnp.concatenate([gw_gdn[LOW_TAIL:], gw_ba[:2 * NH]], axis=0).reshape(
        N_DEV - LOW_BLOCKS, W_IN_BLOCK, D)
    return dict(dp_gdn=dp_gdn, dp_ba=dp_ba, st_gdn=st_gdn, gw_high=gw_high, gw_low_tail=gw_gdn[:LOW_TAIL],
                gw_gconv=st_gdn[0:KS])


def _conf_backward(w, f, a, h):
    dp_conf, st_conf = _conf_bwd(a["d_out_a"], f["y_conv"], f["p_main"], w["conv_w"], w["conv_gn_w"],
                                 w["conv_gn_b"])
    gw_conf = _grad_w("grad_w_in_conf", dp_conf, f["hb1"], 512)
    gw_low = jnp.concatenate([gw_conf, h["gw_low_tail"]], axis=0).reshape(LOW_BLOCKS, W_IN_BLOCK, D)
    return dict(h, dp_conf=dp_conf, st_conf=st_conf, gw_low=gw_low, gw_conv=st_conf[0:KC])


def _in_backward(w, f, g, a, h, xs, modnb):
    st_out, st_conf, st_gdn, st_ffn = a["st_out"], h["st_conf"], h["st_gdn"], g["st_ffn"]
    grad_x, st_in = _bwd_in(h["dp_conf"], h["dp_gdn"], h["dp_ba"], xs, g["dx1"], w["norm_mix_w"], modnb,
                            w["b_ada"], f["w_main"], f["w_ba"])
    dmod = jnp.concatenate([st_in[0:1], st_in[1:2], st_out[0:1], st_ffn[2:3], st_ffn[3:4], st_ffn[1:2]], axis=1)
    small = jnp.concatenate([
        dmod.reshape(48, LANES), st_in[2:3].reshape(8, LANES), st_ffn[4:5].reshape(8, LANES),
        st_ffn[0:1].reshape(8, LANES), st_conf[31:32].reshape(4, LANES), st_conf[32:33].reshape(4, LANES),
        st_conf[33:34].reshape(4, LANES), st_gdn[4:5, 0:LANES],
        _lanes(st_gdn[5:6, NH:2 * NH]), _lanes(st_gdn[6:7, NH:2 * NH]), st_ffn[5:6, 0:LANES]], axis=0)
    return dict(grad_x=grad_x, small=small)


def _local(w, xs, tgt, modnb):
    f = _mix_forward(w, xs, modnb)
    g = _ffn_stage(w, f, xs, tgt, modnb)
    a = _out_backward(w, g, modnb)
    h = _conf_backward(w, f, a, _gdn_backward(w, f, a))
    b = _in_backward(w, f, g, a, h, xs, modnb)
    gw_in = jnp.concatenate([h["gw_low"].reshape(-1, D), h["gw_high"].reshape(-1, D)], axis=0)
    return dict(b, gw_in=gw_in, gw_conv=h["gw_conv"], gw_gconv=h["gw_gconv"], gw_out=a["gw_out"],
                gw_ffn_in=g["gw_ffn_in"], gw_ffn_out=g["gw_ffn_out"])


def kernel(x, c, w_ada, b_ada, norm_mix_w, w_in, conv_w, conv_b, conv_gn_w, conv_gn_b, gdn_conv_w, gdn_a_log, gdn_dt_bias, gdn_norm_w, w_out, norm_ffn_w, w_ffn_in, w_ffn_out, norm_final_w, loss_target, m_w_ada, m_b_ada, m_norm_mix_w, m_w_in, m_conv_w, m_conv_b, m_conv_gn_w, m_conv_gn_b, m_gdn_conv_w, m_gdn_a_log, m_gdn_dt_bias, m_gdn_norm_w, m_w_out, m_norm_ffn_w, m_w_ffn_in, m_w_ffn_out, m_norm_final_w, v_w_ada, v_b_ada, v_norm_mix_w, v_w_in, v_conv_w, v_conv_b, v_conv_gn_w, v_conv_gn_b, v_gdn_conv_w, v_gdn_a_log, v_gdn_dt_bias, v_gdn_norm_w, v_w_out, v_norm_ffn_w, v_w_ffn_in, v_w_ffn_out, v_norm_final_w):
    me = 4 * lax.axis_index("x") + 2 * lax.axis_index("y") + lax.axis_index("c")
    xs = x.reshape(S, D)
    tgt = loss_target.reshape(S, D)

    g_c, g_cw, g_gcw = _exchange("gather_cond", [c, conv_w[0], gdn_conv_w[0]], [False] * 3)
    c_all = g_c.reshape(N_DEV, D)
    g_mod, mod_token = _exchange("gather_mod", [_mod_shard(c_all, w_ada[0])], [False], with_token=True)
    modnb = lax.dynamic_index_in_dim(g_mod, me, axis=1, keepdims=False).reshape(1, 6 * D)

    late = [w_out[0].astype(BF16), jnp.transpose(w_ffn_in[0]).astype(BF16), w_ffn_out[0].astype(BF16)]
    g_win, *late_lands = _gather_two_level(
        "gather_weights", [_after(jnp.transpose(w_in[0]), mod_token).astype(BF16)] + late, seed_only=(1, 2, 3))
    late_started = _exchange_start("gather_late_start", late, late_lands, [False] * 3, only=LEVEL_ONE)
    modnb = _after(modnb, late_started[-1])
    w = dict(b_ada=b_ada, norm_mix_w=norm_mix_w, conv_b=conv_b, conv_gn_w=conv_gn_w, conv_gn_b=conv_gn_b,
             gdn_a_log=gdn_a_log, gdn_dt_bias=gdn_dt_bias, gdn_norm_w=gdn_norm_w, norm_ffn_w=norm_ffn_w,
             norm_final_w=norm_final_w.reshape(1, D),
             conv_w=jnp.transpose(g_cw, (1, 0, 2)).reshape(KC, CW),
             gdn_conv_w=jnp.transpose(g_gcw, (1, 0, 2)).reshape(KS, 3 * GW),
             w_in=g_win.reshape(NIN, D))

    relay = {}

    def relay_late(out_b):
        _, late_landed = _exchange_wait("gather_late_wait", late_started, [False] * 3, (out_b,), only=LEVEL_ONE)
        relay["started"] = _relay_start("gather_late_relay_start", late_landed)
        return relay["started"][-1]

    f = _mix_forward(w, xs, modnb, relay_late)
    g_wout, g_wfi, g_wfo = _relay_wait("gather_late_relay_wait", relay["started"], (f["out_a"],))
    w.update(w_out=g_wout.reshape(D, D), w_ffn_in=g_wfi, w_ffn_out=g_wfo.reshape(4, FB, D))
    g = _ffn_stage(w, f, xs, tgt, modnb)

    ffn_grads = [g["gw_ffn_in"], g["gw_ffn_out"].reshape(N_DEV, DFF // N_DEV, D)]
    ffn_started = _exchange_start("scatter_ffn_start", ffn_grads,
                                  [lax.empty(a.shape, a.dtype) for a in ffn_grads], [True] * 2)
    a = _out_backward(w, g, _after(modnb, ffn_started[-1]))
    out_grads = [a["gw_out"].reshape(N_DEV, D // N_DEV, D)]
    out_started = _exchange_start("scatter_out_start", out_grads,
                                  [lax.empty(t.shape, t.dtype) for t in out_grads], [True])
    h = _gdn_backward(dict(w, gdn_norm_w=_after(w["gdn_norm_w"], out_started[-1])), f, a)
    in_land = lambda: lax.empty((N_DEV, W_IN_BLOCK, D), BF16)
    high_started = _scatter_start("scatter_in_high_start", [h["gw_high"]], [in_land()], (LOW_BLOCKS, N_DEV))
    h = _conf_backward(dict(w, conv_gn_w=_after(w["conv_gn_w"], high_started[-1])), f, a, h)
    low_started = _scatter_start("scatter_in_low_start", [h["gw_low"]], [in_land()], (0, LOW_BLOCKS))
    conv_grads = [_after(jnp.transpose(h["gw_conv"].reshape(KC, N_DEV, CW // N_DEV), (1, 0, 2)), low_started[-1]),
                  jnp.transpose(h["gw_gconv"].reshape(KS, N_DEV, 3 * GW // N_DEV), (1, 0, 2))]
    in_started = _exchange_start("scatter_conv_start", conv_grads,
                                 [lax.empty(t.shape, t.dtype) for t in conv_grads], [True] * 2)
    loc = _in_backward(w, f, g, a, h, xs, _after(modnb, in_started[-1]))
    small_started = _exchange_start("gather_small_start", [loc["small"]],
                                    [lax.empty((N_DEV, SMALL_ROWS, LANES), F32)], [False])

    def own(sent):
        return lax.dynamic_index_in_dim(sent, me, axis=0, keepdims=False)

    big = {}
    (sent_fi, sent_fo), (r_fi, r_fo) = _exchange_wait("scatter_ffn_wait", ffn_started, [True] * 2,
                                                         (small_started[-1],))
    big["w_ffn_in"] = [jnp.transpose(t) for t in _reduce_adam(
        "adam_w_ffn_in", r_fi, jnp.transpose(w_ffn_in[0]), jnp.transpose(m_w_ffn_in[0]),
        jnp.transpose(v_w_ffn_in[0]), own(sent_fi))]
    big["w_ffn_out"] = _reduce_adam("adam_w_ffn_out", r_fo, w_ffn_out[0], m_w_ffn_out[0], v_w_ffn_out[0],
                                    own(sent_fo))
    (sent_out,), (r_out,) = _exchange_wait("scatter_out_wait", out_started, [True], (big["w_ffn_out"][0],))
    big["w_out"] = _reduce_adam("adam_w_out", r_out, w_out[0], m_w_out[0], v_w_out[0], own(sent_out))

    (sent_small,), (r_small,) = _exchange_wait("gather_small_wait", small_started, [False], (big["w_out"][0],))
    slot = lax.broadcasted_iota(jnp.int32, (N_DEV, 1, 1), 0)
    g_small = jnp.where(slot == me, sent_small[None], r_small)
    def views(b_, nm_, nf_, nl_, cb_, gw_, gb_, gn_, al_, dt_):
        arrs = [b_, nm_, nf_, nl_, cb_, gw_, gb_, gn_, al_, dt_]
        return {nm: t.reshape(rows, lanes) for (nm, _, rows, lanes), t in zip(SMALL_LAYOUT, arrs)}

    loss_row, res = _adam_small(
        g_small,
        views(b_ada, norm_mix_w, norm_ffn_w, norm_final_w, conv_b, conv_gn_w, conv_gn_b, gdn_norm_w, gdn_a_log,
              gdn_dt_bias),
        views(m_b_ada, m_norm_mix_w, m_norm_ffn_w, m_norm_final_w, m_conv_b, m_conv_gn_w, m_conv_gn_b,
              m_gdn_norm_w, m_gdn_a_log, m_gdn_dt_bias),
        views(v_b_ada, v_norm_mix_w, v_norm_ffn_w, v_norm_final_w, v_conv_b, v_conv_gn_w, v_conv_gn_b,
              v_gdn_norm_w, v_gdn_a_log, v_gdn_dt_bias))
    loss = loss_row[0, 0]
    small_shapes = dict(b_ada=(1, 6 * D), norm_mix_w=(1, D), norm_ffn_w=(1, D), norm_final_w=(D,),
                        conv_b=(1, CW), conv_gn_w=(1, CW), conv_gn_b=(1, CW), gdn_norm_w=(1, DH),
                        gdn_a_log=(1, NH), gdn_dt_bias=(1, NH))
    res = [{nm: t.reshape(small_shapes[nm]) for nm, t in kind.items()} for kind in res]

    dmod_rows = g_small[:, 0:48, :].reshape(N_DEV, 6 * D)
    dmod_sh = lax.dynamic_slice_in_dim(dmod_rows, me * (6 * D // N_DEV), 6 * D // N_DEV, axis=1)

    big["w_ada"] = _ada_adam(c_all, dmod_sh, w_ada[0], m_w_ada[0], v_w_ada[0])
    (sent_high,), (r_high,) = _scatter_wait("scatter_in_high_wait", high_started, (LOW_BLOCKS, N_DEV),
                                            (big["w_ada"][0],))
    (sent_low,), (r_low,) = _scatter_wait("scatter_in_low_wait", low_started, (0, LOW_BLOCKS), (r_high,))
    (sent_cw, sent_gcw), (r_cw, r_gcw) = _exchange_wait("scatter_conv_wait", in_started, [True] * 2, (r_low,))
    own_in = jnp.where(
        me < LOW_BLOCKS,
        lax.dynamic_index_in_dim(sent_low, jnp.minimum(me, LOW_BLOCKS - 1), axis=0, keepdims=False),
        lax.dynamic_index_in_dim(sent_high, jnp.maximum(me - LOW_BLOCKS, 0), axis=0, keepdims=False))
    big["w_in"] = [jnp.transpose(t) for t in _reduce_adam(
        "adam_w_in", r_high, jnp.transpose(w_in[0]), jnp.transpose(m_w_in[0]), jnp.transpose(v_w_in[0]),
        own_in, low=r_low, low_below=LOW_BLOCKS)]
    big["conv_w"] = _reduce_adam("adam_conv_w", r_cw, conv_w[0], m_conv_w[0], v_conv_w[0], own(sent_cw))
    big["gdn_conv_w"] = _reduce_adam("adam_gdn_conv_w", r_gcw, gdn_conv_w[0], m_gdn_conv_w[0], v_gdn_conv_w[0],
                                     own(sent_gcw))
    outs = [loss, loc["grad_x"].reshape(1, S, D)]
    for kind in range(4):
        for nm in WEIGHT_NAMES:
            outs.append(big[nm][kind][None] if nm in big else res[kind][nm])
    return tuple(outs)
```

```python
import functools

import jax
import jax.numpy as jnp
from jax import lax
from jax.experimental import pallas as pl
from jax.experimental.pallas import tpu as pltpu

F32 = jnp.float32
BF16 = jnp.bfloat16
HI = lax.Precision.HIGHEST
MESH = pl.DeviceIdType.MESH

N_DEV = 8
S = 2048
D = 1024
TM = 256
NT = S // TM
CW = 512
KC = 31
NG = 8
GSZ = CW // NG
HALO = 32
GW = 512
NH = 4
DH = 128
KS = 4
SH = 8
CL = 64
NCH = S // CL
NMAIN = 2 * CW + 4 * GW
NIN = NMAIN + 2 * NH
DFF = 2816
FB = DFF // 4
EPS = 1e-6
QSCALE = DH ** -0.5
LANES = 128
SMALL_ROWS = 88

ADAM_LR = 0.001
ADAM_B1 = 0.9
ADAM_B2 = 0.999
ADAM_EPS = 1e-08
ADAM_WD = 0.01
ADAM_STEP = 10
BC1 = 1.0 - ADAM_B1 ** ADAM_STEP
BC2 = 1.0 - ADAM_B2 ** ADAM_STEP

MIB = 1024 * 1024
VMEM_LIMIT_MIB = 32


def _params(limit_mib=VMEM_LIMIT_MIB, **kw):
    return pltpu.CompilerParams(vmem_limit_bytes=limit_mib * MIB, **kw)


def _sig(x):
    return jax.nn.sigmoid(x)


GP = BF16


def _operands(a, b, prec):
    if prec is BF16:
        return a.astype(BF16), b.astype(BF16), None
    return a, b, prec


def _dot(a, b, prec=None):
    a, b, prec = _operands(a, b, prec)
    return jnp.dot(a, b, preferred_element_type=F32, precision=prec)


def _dot_nt(a, b, prec=None):
    a, b, prec = _operands(a, b, prec)
    return lax.dot_general(a, b, (((1,), (1,)), ((), ())), preferred_element_type=F32, precision=prec)


def _dot_tn(a, b, prec=None):
    a, b, prec = _operands(a, b, prec)
    return lax.dot_general(a, b, (((0,), (0,)), ((), ())), preferred_element_type=F32, precision=prec)


def _lockstep(gens):
    gens = list(gens)
    while gens:
        alive = []
        for g in gens:
            try:
                next(g)
                alive.append(g)
            except StopIteration:
                pass
        gens = alive


def _rowsum(x):
    return jnp.sum(x, axis=-1, keepdims=True)


def _colsum(x):
    return jnp.sum(x, axis=0, keepdims=True)


def _mod(mod_ref, b_ref, k):
    return mod_ref[:, k * D:(k + 1) * D] + b_ref[:, k * D:(k + 1) * D]


def _const(shape):
    nd = len(shape)
    return pl.BlockSpec(shape, lambda *_: (0,) * nd)


def _const1(shape):
    nd = len(shape)
    return pl.BlockSpec(shape, lambda *_: (0,) * nd, pipeline_mode=pl.Buffered(1))


PEER_FLIPS = [(dx, dy, dc) for dx in (0, 1) for dy in (0, 1) for dc in (0, 1)][1:]


def _after(x, token):
    return x + token[0:1, 0:1].astype(x.dtype).reshape((1,) * x.ndim)


def _exchange(name, srcs, per_dest, seed_only=(), with_token=False):
    n = len(srcs)
    out_shape = []
    for a, pd in zip(srcs, per_dest):
        blk = a.shape[1:] if pd else a.shape
        out_shape.append(jax.ShapeDtypeStruct((N_DEV,) + tuple(blk), a.dtype))

    def body(*refs):
        src = refs[:n]
        dst = refs[n:2 * n]
        send_sems, recv_sems, local_sems = refs[-3:]
        if with_token:
            refs[2 * n][...] = jnp.zeros((8, LANES), F32)
        x, y, c = lax.axis_index("x"), lax.axis_index("y"), lax.axis_index("c")
        me = 4 * x + 2 * y + c

        def piece(i, j):
            return src[i].at[j] if per_dest[i] else src[i]

        copies = []
        for k, (dx, dy, dc) in enumerate(PEER_FLIPS):
            px = 1 - x if dx else x
            py = 1 - y if dy else y
            pc = 1 - c if dc else c
            pj = 4 * px + 2 * py + pc
            for i in range(n):
                if i in seed_only:
                    continue
                cp = pltpu.make_async_remote_copy(
                    src_ref=piece(i, pj), dst_ref=dst[i].at[me],
                    send_sem=send_sems.at[k * n + i], recv_sem=recv_sems.at[k * n + i],
                    device_id=(px, py, pc), device_id_type=MESH)
                cp.start()
                arrive = pltpu.make_async_remote_copy(
                    src_ref=piece(i, pj), dst_ref=dst[i].at[pj],
                    send_sem=send_sems.at[k * n + i], recv_sem=recv_sems.at[k * n + i],
                    device_id=(px, py, pc), device_id_type=MESH)
                copies.append((cp, arrive))
        own = []
        for i in range(n):
            lc = pltpu.make_async_copy(piece(i, me), dst[i].at[me], local_sems.at[i])
            lc.start()
            own.append(lc)
        for cp, arrive in copies:
            arrive.wait_recv()
        for cp, arrive in copies:
            cp.wait_send()
        for lc in own:
            lc.wait()

    any_spec = pl.BlockSpec(memory_space=pl.ANY)
    out_specs = [any_spec] * n
    if with_token:
        out_shape.append(jax.ShapeDtypeStruct((8, LANES), F32))
        out_specs.append(pl.BlockSpec(memory_space=pltpu.VMEM))
    return pl.pallas_call(
        body, name=name, out_shape=tuple(out_shape),
        in_specs=[any_spec] * n, out_specs=tuple(out_specs),
        scratch_shapes=[pltpu.SemaphoreType.DMA((7 * n,)), pltpu.SemaphoreType.DMA((7 * n,)),
                        pltpu.SemaphoreType.DMA((n,))],
        compiler_params=pltpu.CompilerParams(has_side_effects=True),
    )(*srcs)


CHIP_FLIPS = [(0, 1), (1, 0), (1, 1)]
LEVEL_ONE = [k for k, (dx, dy, dc) in enumerate(PEER_FLIPS) if (dx, dy, dc) == (0, 0, 1) or dc == 0]


def _chip_peers(x, y):
    return [(1 - x if dx else x, 1 - y if dy else y) for dx, dy in CHIP_FLIPS]


def _gather_two_level(name, srcs, seed_only=()):
    n = len(srcs)
    live = [i for i in range(n) if i not in seed_only]

    def body(*refs):
        src, dst = refs[:n], refs[n:2 * n]
        send_sems, recv_sems, local_sems = refs[2 * n:2 * n + 3]
        bounce = refs[2 * n + 3:]
        x, y, c = lax.axis_index("x"), lax.axis_index("y"), lax.axis_index("c")
        me = 4 * x + 2 * y + c
        sibling = (x, y, 1 - c)
        chips = _chip_peers(x, y)

        def copy(k, i, src_ref, slot, to):
            return pltpu.make_async_remote_copy(
                src_ref=src_ref, dst_ref=dst[i].at[slot], send_sem=send_sems.at[k * n + i],
                recv_sem=recv_sems.at[k * n + i], device_id=to, device_id_type=MESH)

        first = []
        for i in live:
            first.append(copy(0, i, src[i], me, sibling))
            first += [copy(1 + j, i, src[i], me, (px, py, c)) for j, (px, py) in enumerate(chips)]
        for cp in first:
            cp.start()
        up = [pltpu.make_async_copy(src[i], bounce[i], local_sems.at[i]) for i in range(n)]
        for cp in up:
            cp.start()
        for cp in up:
            cp.wait()
        own = [pltpu.make_async_copy(bounce[i], dst[i].at[me], local_sems.at[i]) for i in range(n)]
        for cp in own:
            cp.start()
        passed = []
        for j, (px, py) in enumerate(chips):
            slot = 4 * px + 2 * py + c
            for i in live:
                copy(1 + j, i, src[i], slot, (px, py, c)).wait_recv()
                fwd = copy(4 + j, i, dst[i].at[slot], slot, sibling)
                fwd.start()
                passed.append(fwd)
        for i in live:
            copy(0, i, src[i], 4 * x + 2 * y + 1 - c, sibling).wait_recv()
            for j, (px, py) in enumerate(chips):
                copy(4 + j, i, src[i], 4 * px + 2 * py + 1 - c, sibling).wait_recv()
        for cp in first + passed:
            cp.wait_send()
        for cp in own:
            cp.wait()

    any_spec = pl.BlockSpec(memory_space=pl.ANY)
    return pl.pallas_call(
        body, name=name, out_shape=tuple(jax.ShapeDtypeStruct((N_DEV,) + a.shape, a.dtype) for a in srcs),
        in_specs=[any_spec] * n, out_specs=tuple([any_spec] * n),
        scratch_shapes=[pltpu.SemaphoreType.DMA((7 * n,)), pltpu.SemaphoreType.DMA((7 * n,)),
                        pltpu.SemaphoreType.DMA((n,))] + [pltpu.VMEM(a.shape, a.dtype) for a in srcs],
        compiler_params=pltpu.CompilerParams(has_side_effects=True),
    )(*srcs)


def _relay_copy(land, sems, i, n, j, slot, sibling):
    send_sems, recv_sems = sems
    return pltpu.make_async_remote_copy(
        src_ref=land[i].at[slot], dst_ref=land[i].at[slot], send_sem=send_sems.at[j * n + i],
        recv_sem=recv_sems.at[j * n + i], device_id=sibling, device_id_type=MESH)


def _relay_start(name, lands):
    n = len(lands)

    def body(*refs):
        land = refs[:n]
        sems = refs[n], refs[n + 1]
        x, y, c = lax.axis_index("x"), lax.axis_index("y"), lax.axis_index("c")
        for j, (px, py) in enumerate(_chip_peers(x, y)):
            for i in range(n):
                _relay_copy(land, sems, i, n, j, 4 * px + 2 * py + c, (x, y, 1 - c)).start()
        refs[-1][...] = jnp.zeros((8, LANES), F32)

    return pl.pallas_call(
        body, name=name,
        out_shape=(pltpu.SemaphoreType.DMA((3 * n,)), pltpu.SemaphoreType.DMA((3 * n,)),
                   *[pltpu.HBM(a.shape, a.dtype) for a in lands], jax.ShapeDtypeStruct((8, LANES), F32)),
        in_specs=[HBM_SPEC] * n,
        out_specs=(SEM_SPEC, SEM_SPEC, *[HBM_SPEC] * n, pl.BlockSpec(memory_space=pltpu.VMEM)),
        input_output_aliases={i: 2 + i for i in range(n)},
        compiler_params=pltpu.CompilerParams(has_side_effects=DATAFLOW),
    )(*[pltpu.with_memory_space_constraint(a, pltpu.HBM) for a in lands])


def _relay_wait(name, started, after):
    n = len(started) - 3
    arrays = list(started[2:2 + n])

    def body(*refs):
        land = refs[:n]
        sems = refs[n], refs[n + 1]
        x, y, c = lax.axis_index("x"), lax.axis_index("y"), lax.axis_index("c")
        for j, (px, py) in enumerate(_chip_peers(x, y)):
            for i in range(n):
                _relay_copy(land, sems, i, n, j, 4 * px + 2 * py + c, (x, y, 1 - c)).wait_send()
                _relay_copy(land, sems, i, n, j, 4 * px + 2 * py + 1 - c, (x, y, 1 - c)).wait_recv()

    return pl.pallas_call(
        body, name=name,
        out_shape=tuple(pltpu.HBM(a.shape, a.dtype) for a in arrays),
        in_specs=[HBM_SPEC] * n + [SEM_SPEC, SEM_SPEC] + [pl.BlockSpec(memory_space=pl.ANY)] * len(after),
        out_specs=tuple([HBM_SPEC] * n),
        input_output_aliases={i: i for i in range(n)},
        compiler_params=pltpu.CompilerParams(has_side_effects=DATAFLOW),
    )(*arrays, started[0], started[1], *after)


HBM_SPEC = pl.BlockSpec(memory_space=pltpu.HBM)
SEM_SPEC = pl.BlockSpec(memory_space=pltpu.SEMAPHORE)
DATAFLOW = pltpu.SideEffectType.DATAFLOW_SIDE_EFFECTING


def _peers(only=None):
    x, y, c = lax.axis_index("x"), lax.axis_index("y"), lax.axis_index("c")
    out = []
    for k, (dx, dy, dc) in enumerate(PEER_FLIPS):
        if only is not None and k not in only:
            continue
        px = 1 - x if dx else x
        py = 1 - y if dy else y
        pc = 1 - c if dc else c
        out.append((k, (px, py, pc), 4 * px + 2 * py + pc))
    return 4 * x + 2 * y + c, out


def _exchange_start(name, srcs, lands, per_dest, only=None):
    n = len(srcs)

    def body(*refs):
        src, land = refs[:n], refs[n:2 * n]
        send_sems, recv_sems = refs[2 * n], refs[2 * n + 1]
        token = refs[-1]
        me, peers = _peers(only)
        for k, peer, pj in peers:
            for i in range(n):
                pltpu.make_async_remote_copy(
                    src_ref=src[i].at[pj] if per_dest[i] else src[i], dst_ref=land[i].at[me],
                    send_sem=send_sems.at[k * n + i], recv_sem=recv_sems.at[k * n + i],
                    device_id=peer, device_id_type=MESH).start()
        token[...] = jnp.zeros((8, LANES), F32)

    arrays = list(srcs) + list(lands)
    return pl.pallas_call(
        body, name=name,
        out_shape=(pltpu.SemaphoreType.DMA((7 * n,)), pltpu.SemaphoreType.DMA((7 * n,)),
                   *[pltpu.HBM(a.shape, a.dtype) for a in arrays], jax.ShapeDtypeStruct((8, LANES), F32)),
        in_specs=[HBM_SPEC] * (2 * n),
        out_specs=(SEM_SPEC, SEM_SPEC, *[HBM_SPEC] * (2 * n), pl.BlockSpec(memory_space=pltpu.VMEM)),
        input_output_aliases={i: 2 + i for i in range(2 * n)},
        compiler_params=pltpu.CompilerParams(has_side_effects=DATAFLOW),
    )(*[pltpu.with_memory_space_constraint(a, pltpu.HBM) for a in arrays])


def _exchange_wait(name, started, per_dest, after, only=None):
    n = (len(started) - 3) // 2
    send_sems, recv_sems = started[0], started[1]
    arrays = list(started[2:2 + 2 * n])

    def body(*refs):
        src, land = refs[:n], refs[n:2 * n]
        send, recv = refs[2 * n], refs[2 * n + 1]
        me, peers = _peers(only)
        for k, peer, pj in peers:
            for i in range(n):
                cp = pltpu.make_async_remote_copy(
                    src_ref=src[i].at[pj] if per_dest[i] else src[i], dst_ref=land[i].at[pj],
                    send_sem=send.at[k * n + i], recv_sem=recv.at[k * n + i],
                    device_id=peer, device_id_type=MESH)
                cp.wait_send()
                cp.wait_recv()

    outs = pl.pallas_call(
        body, name=name,
        out_shape=tuple(pltpu.HBM(a.shape, a.dtype) for a in arrays),
        in_specs=[HBM_SPEC] * (2 * n) + [SEM_SPEC, SEM_SPEC] + [pl.BlockSpec(memory_space=pl.ANY)] * len(after),
        out_specs=tuple([HBM_SPEC] * (2 * n)),
        input_output_aliases={i: i for i in range(2 * n)},
        compiler_params=pltpu.CompilerParams(has_side_effects=DATAFLOW),
    )(*arrays, send_sems, recv_sems, *after)
    return outs[:n], outs[n:]


def _mod_shard(c_all, w_ada):
    def body(c_ref, w_ref, o_ref):
        cv = c_ref[...]
        ca = cv * _sig(cv)
        o_ref[...] = _dot(ca.astype(BF16), w_ref[...].astype(BF16))

    return pl.pallas_call(
        body, name="mod_shard", out_shape=jax.ShapeDtypeStruct((N_DEV, w_ada.shape[1]), F32),
        compiler_params=_params(),
    )(c_all, w_ada)


def _fwd_in(x, nw1, modnb, bada, w_main, w_ba):
    def body(x_ref, nw_ref, mod_ref, b_ref, wm_ref, wb_ref, pm_ref, pb_ref, hb_ref):
        xv = x_ref[...]
        r = lax.rsqrt(jnp.mean(xv * xv, axis=-1, keepdims=True) + EPS)
        h = (xv * r * nw_ref[...]) * (1.0 + _mod(mod_ref, b_ref, 1)) + _mod(mod_ref, b_ref, 0)
        hb = h.astype(BF16)
        hb_ref[...] = hb
        pm_ref[...] = _dot_nt(hb, wm_ref[...])
        pb_ref[...] = _dot_nt(hb, wb_ref[...])

    return pl.pallas_call(
        body, name="fwd_in", grid=(NT,),
        in_specs=[pl.BlockSpec((TM, D), lambda i: (i, 0)), _const((1, D)), _const((1, 6 * D)), _const((1, 6 * D)),
                  _const((NMAIN, D)), _const((LANES, D))],
        out_specs=(pl.BlockSpec((TM, NMAIN), lambda i: (i, 0)), pl.BlockSpec((TM, LANES), lambda i: (i, 0)),
                   pl.BlockSpec((TM, D), lambda i: (i, 0))),
        out_shape=(jax.ShapeDtypeStruct((S, NMAIN), F32), jax.ShapeDtypeStruct((S, LANES), F32),
                   jax.ShapeDtypeStruct((S, D), BF16)),
        compiler_params=_params(dimension_semantics=("arbitrary",)),
    )(x, nw1, modnb, bada, w_main, w_ba)


def _group_mean_matrix():
    ii = lax.broadcasted_iota(jnp.int32, (CW, CW), 0) // GSZ
    jj = lax.broadcasted_iota(jnp.int32, (CW, CW), 1) // GSZ
    return jnp.where(ii == jj, 1.0 / GSZ, 0.0).astype(F32)


SUB = 8
SHIFT_ROWS = HALO + TM - SUB


def _fill_shifted(buf, sh):
    for b in range(1, SUB):
        sh[b - 1] = buf[b:b + SHIFT_ROWS, :]


def _rows_at(buf, sh, off):
    a, b = divmod(off, SUB)
    if b == 0:
        return buf[off:off + TM, :]
    return sh[b - 1, SUB * a:SUB * a + TM, :]


def _group_mean(x, pm):
    hi = x.astype(BF16)
    r1 = x - hi.astype(F32)
    mid = r1.astype(BF16)
    lo = (r1 - mid.astype(F32)).astype(BF16)
    return _dot(hi, pm) + _dot(mid, pm) + _dot(lo, pm)


def _conf_fwd(p_main, conv_w, conv_b, gn_w, gn_b):
    def body(a_ref, g_ref, w_ref, b_ref, gw_ref, gb_ref, y_ref, oa_ref, ubuf, ush):
        i = pl.program_id(0)

        @pl.when(i == 0)
        def _():
            ubuf[0:HALO, :] = jnp.zeros((HALO, CW), F32)

        ubuf[HALO:HALO + TM, :] = a_ref[...] * _sig(g_ref[...])
        _fill_shifted(ubuf, ush)
        acc = jnp.zeros((TM, CW), F32) + b_ref[...]
        for k in range(KC):
            acc = acc + w_ref[k:k + 1, :] * _rows_at(ubuf, ush, HALO - (KC - 1) + k)
        y_ref[...] = acc
        ubuf[0:HALO, :] = ubuf[TM:TM + HALO, :]
        pm = _group_mean_matrix().astype(BF16)
        dlt = acc - _group_mean(acc, pm)
        var = _group_mean(dlt * dlt, pm)
        o = dlt * lax.rsqrt(var + EPS) * gw_ref[...] + gb_ref[...]
        oa_ref[...] = o * _sig(o)

    return pl.pallas_call(
        body, name="conf_fwd", grid=(NT,),
        in_specs=[pl.BlockSpec((TM, CW), lambda i: (i, 0)), pl.BlockSpec((TM, CW), lambda i: (i, 1)),
                  _const((KC, CW)), _const((1, CW)), _const((1, CW)), _const((1, CW))],
        out_specs=(pl.BlockSpec((TM, CW), lambda i: (i, 0)), pl.BlockSpec((TM, CW), lambda i: (i, 0))),
        out_shape=(jax.ShapeDtypeStruct((S, CW), F32), jax.ShapeDtypeStruct((S, CW), F32)),
        scratch_shapes=[pltpu.VMEM((HALO + TM, CW), F32), pltpu.VMEM((SUB - 1, SHIFT_ROWS, CW), F32)],
        compiler_params=_params(dimension_semantics=("arbitrary",)),
    )(p_main, p_main, conv_w, conv_b, gn_w, gn_b)


def _tri_iota():
    ii = lax.broadcasted_iota(jnp.int32, (CL, CL), 0)
    jj = lax.broadcasted_iota(jnp.int32, (CL, CL), 1)
    return ii, jj


def _gdn_gates(ba, alog_l, dt_l):
    beta_all = _sig(ba)
    xg = ba + dt_l
    sp = jnp.maximum(xg, 0.0) + jnp.log(1.0 + jnp.exp(-jnp.abs(xg)))
    neg_a = -jnp.exp(alog_l)
    return beta_all, neg_a * sp, xg, neg_a


def _ones_dot(ones, x):
    hi = x.astype(BF16)
    r1 = x - hi.astype(F32)
    mid = r1.astype(BF16)
    lo = (r1 - mid.astype(F32)).astype(BF16)
    return _dot(ones, hi) + _dot(ones, mid) + _dot(ones, lo)


def _gdn_cumsum(g_all):
    ii, jj = _tri_iota()
    low = jnp.where(ii >= jj, 1.0, 0.0).astype(BF16)
    gcum = _ones_dot(low, g_all)
    return gcum, jnp.transpose(gcum)


def _split(x):
    hi = x.astype(BF16)
    return hi, (x - hi.astype(F32)).astype(BF16)


def _dot_split(a, b):
    (ah, al), (bh, bl) = a, b
    return _dot(ah, bh) + (_dot(ah, bl) + _dot(al, bh))


def _unit_lower_inverses(mats):
    ii, jj = _tri_iota()
    eye = jnp.where(ii == jj, 1.0, 0.0).astype(F32)
    ts = [eye - a for a in mats]
    ps = [_dot_split(s, s) for s in map(_split, mats)]
    for _ in range(4):
        sp = [_split(p) for p in ps]
        ts = [t + _dot_split(_split(t), s) for t, s in zip(ts, sp)]
        ps = [_dot_split(s, s) for s in sp]
    return [t + _dot_split(_split(t), _split(p)) for t, p in zip(ts, ps)]


def _head_terms(qh, kh, beta, gcol, grow):
    ii, jj = _tri_iota()
    causal = ii >= jj
    strict = ii > jj
    rq = lax.rsqrt(_rowsum(qh * qh) + EPS)
    rk = lax.rsqrt(_rowsum(kh * kh) + EPS)
    qn = qh * rq
    kn = kh * rk
    qs = qn * QSCALE
    decay = jnp.where(causal, jnp.exp(jnp.where(causal, gcol - grow, 0.0)), 0.0)
    gam = jnp.exp(gcol)
    gl = gcol[CL - 1:CL, :]
    kds = jnp.exp(gl - gcol)
    cd = jnp.exp(gl)
    kb = kn * beta
    a = jnp.where(strict, _dot_nt(kb, kn, GP) * decay, 0.0)
    qk = jnp.where(causal, _dot_nt(qs, kn, GP) * decay, 0.0)
    return dict(rq=rq, rk=rk, qn=qn, kn=kn, qs=qs, decay=decay, gam=gam, kds=kds, cd=cd, kb=kb, a=a, qk=qk,
                causal=causal, strict=strict)


def _short_conv(w_ref, buf, rows=CL):
    acc = w_ref[0:1, :] * buf[SH - KS + 1:SH - KS + 1 + rows, :]
    for k in range(1, KS):
        off = SH - (KS - 1) + k
        acc = acc + w_ref[k:k + 1, :] * buf[off:off + rows, :]
    return acc


CPS = 4
TG = CPS * CL


def _gdn_prep(p_main, p_ba, gdn_conv_w, alog_l, dt_l):
    def body(q_ref, k_ref, v_ref, qh_ref, kh_ref, vh_ref, ba_ref, w_ref, al_ref, dt_ref,
             wo_ref, uo_ref, qg_ref, kd_ref, qk_ref, cd_ref, t_ref, xbuf):
        i = pl.program_id(0)
        first = i == 0
        xbuf[0:SH, 0:GW] = jnp.where(first, 0.0, qh_ref[...])
        xbuf[0:SH, GW:2 * GW] = jnp.where(first, 0.0, kh_ref[...])
        xbuf[0:SH, 2 * GW:3 * GW] = jnp.where(first, 0.0, vh_ref[...])
        xbuf[SH:SH + TG, 0:GW] = q_ref[...]
        xbuf[SH:SH + TG, GW:2 * GW] = k_ref[...]
        xbuf[SH:SH + TG, 2 * GW:3 * GW] = v_ref[...]
        conv = _short_conv(w_ref, xbuf, TG)
        qkv = conv * _sig(conv)
        beta_all, g_all, _, _ = _gdn_gates(ba_ref[...], al_ref[...], dt_ref[...])
        lane = lax.broadcasted_iota(jnp.int32, (8, LANES), 1)
        cums = [_gdn_cumsum(g_all[cc * CL:(cc + 1) * CL, :]) for cc in range(CPS)]
        pairs = [(cc, h) for cc in range(CPS) for h in range(NH)]
        terms, vbs = [], []
        for cc, h in pairs:
            r0, lo = cc * CL, h * DH
            beta = beta_all[r0:r0 + CL, h:h + 1]
            gcum, gcum_t = cums[cc]
            terms.append(_head_terms(qkv[r0:r0 + CL, lo:lo + DH], qkv[r0:r0 + CL, GW + lo:GW + lo + DH], beta,
                                     gcum[:, NH + h:NH + h + 1], gcum_t[NH + h:NH + h + 1, :]))
            vbs.append(qkv[r0:r0 + CL, 2 * GW + lo:2 * GW + lo + DH] * beta)
        invs = _unit_lower_inverses([f["a"] for f in terms])
        cds = [jnp.zeros((8, LANES), F32) for _ in range(CPS)]
        for (cc, h), f, t, vb in zip(pairs, terms, invs, vbs):
            r0, lo = cc * CL, h * DH
            t_ref[cc, h] = t
            uo_ref[r0:r0 + CL, lo:lo + DH] = _dot(t, vb, GP)
            wo_ref[r0:r0 + CL, lo:lo + DH] = _dot(t, f["kb"] * f["gam"], GP).astype(BF16)
            qg_ref[r0:r0 + CL, lo:lo + DH] = (f["qs"] * f["gam"]).astype(BF16)
            kd_ref[r0:r0 + CL, lo:lo + DH] = (f["kn"] * f["kds"]).astype(BF16)
            qk_ref[cc, h] = f["qk"].astype(BF16)
            cds[cc] = cds[cc] + jnp.where(lane == h, f["cd"], 0.0)
        for cc in range(CPS):
            cd_ref[cc] = cds[cc]

    col = lambda j: pl.BlockSpec((TG, GW), lambda i: (i, j))
    halo = lambda j: pl.BlockSpec((SH, GW), lambda i: (jnp.maximum(i * (TG // SH) - 1, 0), j))
    tile = lambda: pl.BlockSpec((TG, GW), lambda i: (i, 0))
    sq = lambda: pl.BlockSpec((CPS, NH, CL, CL), lambda i: (i, 0, 0, 0))
    return pl.pallas_call(
        body, name="gdn_prep", grid=(NCH // CPS,),
        in_specs=[col(2), col(3), col(4), halo(2), halo(3), halo(4), pl.BlockSpec((TG, LANES), lambda i: (i, 0)),
                  _const((KS, 3 * GW)), _const((1, LANES)), _const((1, LANES))],
        out_specs=(tile(), tile(), tile(), tile(), sq(), pl.BlockSpec((CPS, 8, LANES), lambda i: (i, 0, 0)), sq()),
        out_shape=(jax.ShapeDtypeStruct((S, GW), BF16), jax.ShapeDtypeStruct((S, GW), F32),
                   jax.ShapeDtypeStruct((S, GW), BF16), jax.ShapeDtypeStruct((S, GW), BF16),
                   jax.ShapeDtypeStruct((NCH, NH, CL, CL), BF16), jax.ShapeDtypeStruct((NCH, 8, LANES), F32),
                   jax.ShapeDtypeStruct((NCH, NH, CL, CL), F32)),
        scratch_shapes=[pltpu.VMEM((SH + TG, 3 * GW), F32)],
        compiler_params=_params(dimension_semantics=("arbitrary",)),
    )(p_main, p_main, p_main, p_main, p_main, p_main, p_ba, gdn_conv_w, alog_l, dt_l)


def _gdn_scan(w_o, u_o, qg, kd, qk, cd, p_main, gdn_nw):
    def body(w_ref, u_ref, qg_ref, kd_ref, qk_ref, cd_ref, z_ref, nw_ref, ob_ref, o_ref, sin_ref, state):
        n = pl.program_id(0)

        @pl.when(n == 0)
        def _():
            state[...] = jnp.zeros((NH, DH, DH), F32)

        def head(cc, h):
            rows, lo = pl.ds(cc * CL, CL), h * DH
            st = state[h]
            sin_ref[cc, h] = st
            sb = st.astype(BF16)
            v_new = u_ref[rows, lo:lo + DH] - _dot(w_ref[rows, lo:lo + DH], sb)
            yield
            vb = v_new.astype(BF16)
            o = _dot(qg_ref[rows, lo:lo + DH], sb) + _dot(qk_ref[cc, h], vb)
            state[h] = st * cd_ref[cc, 0:1, h:h + 1] + _dot_tn(kd_ref[rows, lo:lo + DH], vb)
            yield
            o_ref[rows, lo:lo + DH] = o
            r = lax.rsqrt(jnp.mean(o * o, axis=-1, keepdims=True) + EPS)
            zh = z_ref[rows, lo:lo + DH]
            ob_ref[rows, lo:lo + DH] = o * r * nw_ref[...] * (zh * _sig(zh))

        for cc in range(CPS):
            _lockstep(head(cc, h) for h in range(NH))

    tile = lambda: pl.BlockSpec((TG, GW), lambda n: (n, 0))
    return pl.pallas_call(
        body, name="gdn_scan", grid=(NCH // CPS,),
        in_specs=[tile(), tile(), tile(), tile(), pl.BlockSpec((CPS, NH, CL, CL), lambda n: (n, 0, 0, 0)),
                  pl.BlockSpec((CPS, 8, LANES), lambda n: (n, 0, 0)), pl.BlockSpec((TG, GW), lambda n: (n, 5)),
                  _const((1, DH))],
        out_specs=(tile(), tile(), pl.BlockSpec((CPS, NH, DH, DH), lambda n: (n, 0, 0, 0))),
        out_shape=(jax.ShapeDtypeStruct((S, GW), F32), jax.ShapeDtypeStruct((S, GW), F32),
                   jax.ShapeDtypeStruct((NCH, NH, DH, DH), F32)),
        scratch_shapes=[pltpu.VMEM((NH, DH, DH), F32)],
        compiler_params=_params(dimension_semantics=("arbitrary",)),
    )(w_o, u_o, qg, kd, qk, cd, p_main, gdn_nw)


def _fwd_out(out_a, out_b, x, modnb, bada, w_out):
    def body(oa_ref, ob_ref, x_ref, mod_ref, b_ref, w_ref, x1_ref, mix_ref, oab_ref):
        oa = oa_ref[...].astype(BF16)
        ob = ob_ref[...].astype(BF16)
        oab_ref[:, 0:CW] = oa
        oab_ref[:, CW:D] = ob
        mix = _dot(oa, w_ref[0:CW, :]) + _dot(ob, w_ref[CW:D, :])
        mix_ref[...] = mix
        x1_ref[...] = x_ref[...] + _mod(mod_ref, b_ref, 2) * mix

    tile = lambda w: pl.BlockSpec((TM, w), lambda i: (i, 0))
    return pl.pallas_call(
        body, name="fwd_out", grid=(NT,),
        in_specs=[tile(CW), tile(GW), tile(D), _const((1, 6 * D)), _const((1, 6 * D)), _const((D, D))],
        out_specs=(tile(D), tile(D), tile(D)),
        out_shape=(jax.ShapeDtypeStruct((S, D), F32), jax.ShapeDtypeStruct((S, D), F32),
                   jax.ShapeDtypeStruct((S, D), BF16)),
        compiler_params=_params(dimension_semantics=("arbitrary",)),
    )(out_a, out_b, x, modnb, bada, w_out)


FFN_STATS = 8


def _ffn_forward(x1, tgt, modnb, bada, nw2, nfw, w_fi, w_fo):
    def body(x1_ref, tgt_ref, mod_ref, b_ref, nw2_ref, nfw_ref, wi_ref, wo_ref,
             hb_ref, act_ref, pre_ref, dx2_ref, dffn_ref, st_ref):
        i = pl.program_id(0)

        @pl.when(i == 0)
        def _():
            st_ref[...] = jnp.zeros((FFN_STATS, D), F32)

        sh2, sc2, gt2 = _mod(mod_ref, b_ref, 3), _mod(mod_ref, b_ref, 4), _mod(mod_ref, b_ref, 5)
        x1v = x1_ref[...]
        r2 = lax.rsqrt(jnp.mean(x1v * x1v, axis=-1, keepdims=True) + EPS)
        hb = ((x1v * r2 * nw2_ref[...]) * (1.0 + sc2) + sh2).astype(BF16)
        hb_ref[...] = hb
        ffn = jnp.zeros((TM, D), F32)
        for j in range(4):
            fgj = _dot_nt(hb, wi_ref[j])
            fuj = _dot_nt(hb, wi_ref[j + 4])
            pre_ref[j] = fgj.astype(BF16)
            pre_ref[j + 4] = fuj.astype(BF16)
            aj = (fgj * _sig(fgj) * fuj).astype(BF16)
            act_ref[j] = aj
            ffn = ffn + _dot(aj, wo_ref[j])
        x2 = x1v + gt2 * ffn
        r3 = lax.rsqrt(jnp.mean(x2 * x2, axis=-1, keepdims=True) + EPS)
        xr3 = x2 * r3
        err = xr3 * nfw_ref[...] - tgt_ref[...]
        loss = 0.5 * jnp.sum(jnp.mean(err * err, axis=-1, keepdims=True), axis=0, keepdims=True)
        dy = err * (1.0 / D)
        st_ref[0:1, :] += _colsum(dy * xr3)
        dyr = dy * nfw_ref[...]
        dx2 = r3 * (dyr - xr3 * jnp.mean(dyr * xr3, axis=-1, keepdims=True))
        st_ref[1:2, :] += _colsum(dx2 * ffn)
        st_ref[5:6, :] += jnp.broadcast_to(loss, (1, D))
        dx2_ref[...] = dx2
        dffn_ref[...] = (gt2 * dx2).astype(BF16)

    tile = lambda w: pl.BlockSpec((TM, w), lambda i: (i, 0))
    return pl.pallas_call(
        body, name="ffn_forward", grid=(NT,),
        in_specs=[tile(D), tile(D), _const((1, 6 * D)), _const((1, 6 * D)), _const((1, D)), _const((1, D)),
                  _const1((N_DEV, FB, D)), _const1((4, FB, D))],
        out_specs=(tile(D), pl.BlockSpec((4, TM, FB), lambda i: (0, i, 0)),
                   pl.BlockSpec((N_DEV, TM, FB), lambda i: (0, i, 0)), tile(D), tile(D), _const((FFN_STATS, D))),
        out_shape=(jax.ShapeDtypeStruct((S, D), BF16), jax.ShapeDtypeStruct((4, S, FB), BF16),
                   jax.ShapeDtypeStruct((N_DEV, S, FB), BF16), jax.ShapeDtypeStruct((S, D), F32),
                   jax.ShapeDtypeStruct((S, D), BF16), jax.ShapeDtypeStruct((FFN_STATS, D), F32)),
        compiler_params=_params(42, dimension_semantics=("arbitrary",)),
    )(x1, tgt, modnb, bada, nw2, nfw, w_fi, w_fo)


def _ffn_backward(dffn, pre, x1, dx2, modnb, bada, nw2, w_fi, w_fo):
    def body(dffn_ref, pre_ref, x1_ref, dx2_ref, mod_ref, b_ref, nw2_ref, wi_ref, wo_ref, df_ref, dx1_ref, st_ref):
        i = pl.program_id(0)

        @pl.when(i == 0)
        def _():
            st_ref[...] = jnp.zeros((FFN_STATS, D), F32)

        dffn = dffn_ref[...]
        dh = jnp.zeros((TM, D), F32)
        for j in range(4):
            fg = pre_ref[j].astype(F32)
            fu = pre_ref[j + 4].astype(F32)
            sg = _sig(fg)
            dact = _dot_nt(dffn, wo_ref[j])
            dfg = (dact * fu * (sg * (1.0 + fg * (1.0 - sg)))).astype(BF16)
            dfu = (dact * (fg * sg)).astype(BF16)
            df_ref[j] = dfg
            df_ref[j + 4] = dfu
            dh = dh + _dot(dfg, wi_ref[j]) + _dot(dfu, wi_ref[j + 4])
        x1v = x1_ref[...]
        r2 = lax.rsqrt(jnp.mean(x1v * x1v, axis=-1, keepdims=True) + EPS)
        xr2 = x1v * r2
        st_ref[2:3, :] += _colsum(dh)
        st_ref[3:4, :] += _colsum(dh * (xr2 * nw2_ref[...]))
        dxn = dh * (1.0 + _mod(mod_ref, b_ref, 4))
        st_ref[4:5, :] += _colsum(dxn * xr2)
        dxr = dxn * nw2_ref[...]
        dx1_ref[...] = dx2_ref[...] + r2 * (dxr - xr2 * jnp.mean(dxr * xr2, axis=-1, keepdims=True))

    tile = lambda w: pl.BlockSpec((TM, w), lambda i: (i, 0))
    wide = lambda: pl.BlockSpec((N_DEV, TM, FB), lambda i: (0, i, 0))
    return pl.pallas_call(
        body, name="ffn_backward", grid=(NT,),
        in_specs=[tile(D), wide(), tile(D), tile(D), _const((1, 6 * D)), _const((1, 6 * D)), _const((1, D)),
                  _const1((N_DEV, FB, D)), _const1((4, FB, D))],
        out_specs=(wide(), tile(D), _const((FFN_STATS, D))),
        out_shape=(jax.ShapeDtypeStruct((N_DEV, S, FB), BF16), jax.ShapeDtypeStruct((S, D), F32),
                   jax.ShapeDtypeStruct((FFN_STATS, D), F32)),
        compiler_params=_params(44, dimension_semantics=("arbitrary",)),
    )(dffn, pre, x1, dx2, modnb, bada, nw2, w_fi, w_fo)


def _grad_w(name, a, b, nb):
    m, n = a.shape[1], b.shape[1]

    def body(a_ref, b_ref, o_ref):
        o_ref[...] = _dot_tn(a_ref[...], b_ref[...]).astype(BF16)

    return pl.pallas_call(
        body, name=name, grid=(m // nb,),
        in_specs=[pl.BlockSpec((S, nb), lambda j: (0, j)), _const((S, n))],
        out_specs=pl.BlockSpec((nb, n), lambda j: (j, 0)),
        out_shape=jax.ShapeDtypeStruct((m, n), BF16),
        compiler_params=_params(dimension_semantics=("arbitrary",)),
    )(a, b)


GW_IN_ROWS = NMAIN + LANES


def _grad_w_in(dp_conf, dp_gdn, dp_ba, hb1):
    nb = 512
    n_conf, n_gdn = 2 * CW // nb, 4 * GW // nb

    def body(c_ref, g_ref, ba_ref, h_ref, o_ref):
        j = pl.program_id(0)

        @pl.when(j < n_conf)
        def _():
            o_ref[...] = _dot_tn(c_ref[...], h_ref[...]).astype(BF16)

        @pl.when((j >= n_conf) & (j < n_conf + n_gdn))
        def _():
            o_ref[...] = _dot_tn(g_ref[...], h_ref[...]).astype(BF16)

        @pl.when(j == n_conf + n_gdn)
        def _():
            o_ref[0:LANES, :] = _dot_tn(ba_ref[...], h_ref[...]).astype(BF16)

    return pl.pallas_call(
        body, name="grad_w_in", grid=(n_conf + n_gdn + 1,),
        in_specs=[pl.BlockSpec((S, nb), lambda j: (0, jnp.minimum(j, n_conf - 1))),
                  pl.BlockSpec((S, nb), lambda j: (0, jnp.clip(j - n_conf, 0, n_gdn - 1))),
                  _const((S, LANES)), _const((S, D))],
        out_specs=pl.BlockSpec((nb, D), lambda j: (j, 0)),
        out_shape=jax.ShapeDtypeStruct((GW_IN_ROWS, D), BF16),
        compiler_params=_params(dimension_semantics=("arbitrary",)),
    )(dp_conf, dp_gdn, dp_ba, hb1)


def _grad_w_ffn_in(hb2, df):
    def body(a_ref, b_ref, o_ref):
        o_ref[0] = _dot_tn(b_ref[0], a_ref[...]).astype(BF16)

    return pl.pallas_call(
        body, name="grad_w_ffn_in", grid=(N_DEV,),
        in_specs=[_const((S, D)), pl.BlockSpec((1, S, FB), lambda j: (j, 0, 0))],
        out_specs=pl.BlockSpec((1, FB, D), lambda j: (j, 0, 0)),
        out_shape=jax.ShapeDtypeStruct((N_DEV, FB, D), BF16),
        compiler_params=_params(dimension_semantics=("arbitrary",)),
    )(hb2, df)


def _grad_w_ffn_out(act, dffn):
    def body(a_ref, b_ref, o_ref):
        o_ref[0] = _dot_tn(a_ref[0], b_ref[...]).astype(BF16)

    return pl.pallas_call(
        body, name="grad_w_ffn_out", grid=(4,),
        in_specs=[pl.BlockSpec((1, S, FB), lambda j: (j, 0, 0)), _const((S, D))],
        out_specs=pl.BlockSpec((1, FB, D), lambda j: (j, 0, 0)),
        out_shape=jax.ShapeDtypeStruct((4, FB, D), BF16),
        compiler_params=_params(dimension_semantics=("arbitrary",)),
    )(act, dffn)


def _bwd_out(dx1, mix, modnb, bada, w_out):
    def body(dx_ref, mix_ref, mod_ref, b_ref, w_ref, dmix_ref, doa_ref, dob_ref, st_ref):
        i = pl.program_id(0)

        @pl.when(i == 0)
        def _():
            st_ref[...] = jnp.zeros((8, D), F32)

        dx = dx_ref[...]
        st_ref[0:1, :] += _colsum(dx * mix_ref[...])
        dmix = (_mod(mod_ref, b_ref, 2) * dx).astype(BF16)
        dmix_ref[...] = dmix
        doa_ref[...] = _dot_nt(dmix, w_ref[0:CW, :])
        dob_ref[...] = _dot_nt(dmix, w_ref[CW:D, :])

    tile = lambda w: pl.BlockSpec((TM, w), lambda i: (i, 0))
    return pl.pallas_call(
        body, name="bwd_out", grid=(NT,),
        in_specs=[tile(D), tile(D), _const((1, 6 * D)), _const((1, 6 * D)), _const((D, D))],
        out_specs=(tile(D), tile(CW), tile(GW), _const((8, D))),
        out_shape=(jax.ShapeDtypeStruct((S, D), BF16), jax.ShapeDtypeStruct((S, CW), F32),
                   jax.ShapeDtypeStruct((S, GW), F32), jax.ShapeDtypeStruct((8, D), F32)),
        compiler_params=_params(dimension_semantics=("arbitrary",)),
    )(dx1, mix, modnb, bada, w_out)


CONF_STATS = 40


def _conf_bwd(d_out_a, y, p_main, conv_w, gn_w, gn_b):
    def body(do_ref, y_ref, a_ref, g_ref, ah_ref, gh_ref, w_ref, gw_ref, gb_ref, dp_ref, st_ref,
             ubuf, dybuf, ush, dysh):
        i = pl.program_id(0)

        @pl.when(i == 0)
        def _():
            st_ref[...] = jnp.zeros((CONF_STATS, CW), F32)
            dybuf[TM:TM + HALO, :] = jnp.zeros((HALO, CW), F32)

        pm = _group_mean_matrix().astype(BF16)
        yv = y_ref[...]
        dlt = yv - _group_mean(yv, pm)
        rstd = lax.rsqrt(_group_mean(dlt * dlt, pm) + EPS)
        un = dlt * rstd
        o = un * gw_ref[...] + gb_ref[...]
        so = _sig(o)
        d_o = do_ref[...] * (so * (1.0 + o * (1.0 - so)))
        st_ref[33:34, :] += _colsum(d_o)
        st_ref[32:33, :] += _colsum(d_o * un)
        dun = d_o * gw_ref[...]
        dy = rstd * (dun - _group_mean(dun, pm) - un * _group_mean(dun * un, pm))
        st_ref[31:32, :] += _colsum(dy)
        dybuf[0:TM, :] = dy
        _fill_shifted(dybuf, dysh)

        a = a_ref[...]
        sg = _sig(g_ref[...])
        first = i == NT - 1
        ubuf[0:HALO, :] = jnp.where(first, 0.0, ah_ref[...] * _sig(gh_ref[...]))
        ubuf[HALO:HALO + TM, :] = a * sg
        _fill_shifted(ubuf, ush)
        du = jnp.zeros((TM, CW), F32)
        for k in range(KC):
            st_ref[k:k + 1, :] += _colsum(dy * _rows_at(ubuf, ush, HALO - (KC - 1) + k))
            du = du + w_ref[k:k + 1, :] * _rows_at(dybuf, dysh, KC - 1 - k)
        dybuf[TM:TM + HALO, :] = dybuf[0:HALO, :]
        dp_ref[:, 0:CW] = (du * sg).astype(BF16)
        dp_ref[:, CW:2 * CW] = (du * a * sg * (1.0 - sg)).astype(BF16)

    rev = lambda w, j=0: pl.BlockSpec((TM, w), lambda i: (NT - 1 - i, j))
    halo = lambda j: pl.BlockSpec((HALO, CW), lambda i: (jnp.maximum((NT - 1 - i) * (TM // HALO) - 1, 0), j))
    return pl.pallas_call(
        body, name="conf_bwd", grid=(NT,),
        in_specs=[rev(CW), rev(CW), rev(CW, 0), rev(CW, 1), halo(0), halo(1),
                  _const((KC, CW)), _const((1, CW)), _const((1, CW))],
        out_specs=(rev(2 * CW), _const((CONF_STATS, CW))),
        out_shape=(jax.ShapeDtypeStruct((S, 2 * CW), BF16), jax.ShapeDtypeStruct((CONF_STATS, CW), F32)),
        scratch_shapes=[pltpu.VMEM((HALO + TM, CW), F32), pltpu.VMEM((TM + HALO, CW), F32),
                        pltpu.VMEM((SUB - 1, SHIFT_ROWS, CW), F32), pltpu.VMEM((SUB - 1, SHIFT_ROWS, CW), F32)],
        compiler_params=_params(dimension_semantics=("arbitrary",)),
    )(d_out_a, y, p_main, p_main, p_main, p_main, conv_w, gn_w, gn_b)


GDN_STATS = 8


def _gdn_bwd(d_out_b, o_pre, s_in, t_inv, p_main, p_ba, gdn_conv_w, alog_l, dt_l, gdn_nw):
    def body(dob_ref, o_ref, sin_ref, t_ref, q_ref, k_ref, v_ref, z_ref, qh_ref, kh_ref, vh_ref, ba_ref,
             w_ref, al_ref, dt_ref, nw_ref, dp_ref, dba_ref, st_ref, xbuf, dcbuf, dstate):
        n = pl.program_id(0)

        @pl.when(n == 0)
        def _():
            st_ref[...] = jnp.zeros((GDN_STATS, 3 * GW), F32)
            dcbuf[CL:CL + SH, :] = jnp.zeros((SH, 3 * GW), F32)
            dstate[...] = jnp.zeros((NH, DH, DH), F32)

        first = n == NCH - 1
        xbuf[0:SH, 0:GW] = jnp.where(first, 0.0, qh_ref[...])
        xbuf[0:SH, GW:2 * GW] = jnp.where(first, 0.0, kh_ref[...])
        xbuf[0:SH, 2 * GW:3 * GW] = jnp.where(first, 0.0, vh_ref[...])
        xbuf[SH:SH + CL, 0:GW] = q_ref[...]
        xbuf[SH:SH + CL, GW:2 * GW] = k_ref[...]
        xbuf[SH:SH + CL, 2 * GW:3 * GW] = v_ref[...]
        conv = _short_conv(w_ref, xbuf)
        sc = _sig(conv)
        qkv = conv * sc
        ba = ba_ref[...]
        beta_all, g_all, xg, neg_a = _gdn_gates(ba, al_ref[...], dt_ref[...])
        gcum, gcum_t = _gdn_cumsum(g_all)
        lane = lax.broadcasted_iota(jnp.int32, (CL, LANES), 1)
        row = lax.broadcasted_iota(jnp.int32, (CL, 1), 0)
        acc = dict(dgcum=jnp.zeros((CL, LANES), F32), dbeta=jnp.zeros((CL, LANES), F32))

        def head(h):
            lo = h * DH
            qh = qkv[:, lo:lo + DH]
            kh = qkv[:, GW + lo:GW + lo + DH]
            vh = qkv[:, 2 * GW + lo:2 * GW + lo + DH]
            beta = beta_all[:, h:h + 1]
            f = _head_terms(qh, kh, beta, gcum[:, NH + h:NH + h + 1], gcum_t[NH + h:NH + h + 1, :])
            qn, kn, qs, kb, gam, kds, cd, decay = (f[s] for s in ("qn", "kn", "qs", "kb", "gam", "kds", "cd", "decay"))
            t = t_ref[0, h]
            st = sin_ref[0, h]
            vb = vh * beta
            kbg = kb * gam
            u = _dot(t, vb, GP)
            w = _dot(t, kbg, GP)
            yield
            v_new = u - _dot(w, st, GP)
            q_dec = qs * gam
            k_dec = kn * kds

            o = o_ref[:, lo:lo + DH]
            zh = z_ref[:, lo:lo + DH]
            sz = _sig(zh)
            r = lax.rsqrt(jnp.mean(o * o, axis=-1, keepdims=True) + EPS)
            orr = o * r
            d_out = dob_ref[:, lo:lo + DH]
            dz = d_out * (orr * nw_ref[...]) * (sz * (1.0 + zh * (1.0 - sz)))
            don = d_out * (zh * sz)
            st_ref[4:5, 0:DH] += _colsum(don * orr)
            tt = don * nw_ref[...]
            d_o = r * (tt - orr * jnp.mean(tt * orr, axis=-1, keepdims=True))

            yield
            ds_out = dstate[h]
            dv_new = _dot_tn(f["qk"], d_o, GP) + _dot(k_dec, ds_out, GP)
            dqk = jnp.where(f["causal"], _dot_nt(d_o, v_new, GP), 0.0)
            dq_dec = _dot_nt(d_o, st, GP)
            dk_dec = _dot_nt(v_new, ds_out, GP)
            yield
            dstate[h] = _dot_tn(q_dec, d_o, GP) + cd * ds_out - _dot_tn(w, dv_new, GP)
            dcd = jnp.sum(_rowsum(st * ds_out), axis=0, keepdims=True)
            dw = -_dot_nt(dv_new, st, GP)
            dvb = _dot_tn(t, dv_new, GP)
            yield
            dt_m = _dot_nt(dv_new, vb, GP) + _dot_nt(dw, kbg, GP)
            dkbg = _dot_tn(t, dw, GP)
            yield
            dtt = _dot_nt(dt_m, t, GP)
            yield
            da = jnp.where(f["strict"], -_dot_tn(t, dtt, GP), 0.0)
            yield
            dad = da * decay
            dqkd = dqk * decay
            dkb = _dot(dad, kn, GP) + dkbg * gam
            dkn = _dot_tn(dad, kb, GP) + _dot_tn(dqkd, qs, GP) + dk_dec * kds + dkb * beta
            dqs = _dot(dqkd, kn, GP) + dq_dec * gam
            yield
            m = da * f["a"] + dqk * f["qk"]
            tk = _rowsum(dk_dec * k_dec)
            dgl = jnp.sum(tk, axis=0, keepdims=True) + dcd * cd
            dgc = (_rowsum(m) - _rowsum(jnp.transpose(m)) + _rowsum(dq_dec * q_dec) - tk + _rowsum(dkbg * kbg)
                   + jnp.where(row == CL - 1, dgl, 0.0))
            dbeta = _rowsum(dkb * kn) + _rowsum(dvb * vh)
            acc["dgcum"] = acc["dgcum"] + jnp.where(lane == NH + h, dgc, 0.0)
            acc["dbeta"] = acc["dbeta"] + jnp.where(lane == h, dbeta, 0.0)
            dvh = dvb * beta
            dqn = dqs * QSCALE
            dqh = f["rq"] * (dqn - qn * _rowsum(dqn * qn))
            dkh = f["rk"] * (dkn - kn * _rowsum(dkn * kn))
            dsilu = lambda c0: sc[:, c0:c0 + DH] * (1.0 + conv[:, c0:c0 + DH] * (1.0 - sc[:, c0:c0 + DH]))
            dcbuf[0:CL, lo:lo + DH] = dqh * dsilu(lo)
            dcbuf[0:CL, GW + lo:GW + lo + DH] = dkh * dsilu(GW + lo)
            dcbuf[0:CL, 2 * GW + lo:2 * GW + lo + DH] = dvh * dsilu(2 * GW + lo)
            dp_ref[:, 3 * GW + lo:3 * GW + lo + DH] = dz.astype(BF16)

        _lockstep(head(h) for h in range(NH))
        dgcum_all, dbeta_all = acc["dgcum"], acc["dbeta"]

        ii, jj = _tri_iota()
        upper = jnp.where(ii <= jj, 1.0, 0.0).astype(BF16)
        dg_all = _ones_dot(upper, dgcum_all)
        dxg = dg_all * neg_a * _sig(xg)
        st_ref[5:6, 0:LANES] += _colsum(dg_all * g_all)
        st_ref[6:7, 0:LANES] += _colsum(dxg)
        dbl = dbeta_all * beta_all * (1.0 - beta_all)
        dba_ref[...] = jnp.where(lane < NH, dbl, jnp.where(lane < 2 * NH, dxg, 0.0)).astype(BF16)

        dconv = dcbuf[0:CL, :]
        dx = w_ref[0:1, :] * dcbuf[KS - 1:KS - 1 + CL, :]
        st_ref[0:1, :] += _colsum(dconv * xbuf[SH - KS + 1:SH - KS + 1 + CL, :])
        for k in range(1, KS):
            off = SH - (KS - 1) + k
            st_ref[k:k + 1, :] += _colsum(dconv * xbuf[off:off + CL, :])
            dx = dx + w_ref[k:k + 1, :] * dcbuf[KS - 1 - k:KS - 1 - k + CL, :]
        dcbuf[CL:CL + SH, :] = dcbuf[0:SH, :]
        dp_ref[:, 0:3 * GW] = dx.astype(BF16)

    rev = lambda w, j=0: pl.BlockSpec((CL, w), lambda n: (NCH - 1 - n, j))
    halo = lambda j: pl.BlockSpec((SH, GW), lambda n: (jnp.maximum((NCH - 1 - n) * (CL // SH) - 1, 0), j))
    blk4 = lambda a, b: pl.BlockSpec((1, NH, a, b), lambda n: (NCH - 1 - n, 0, 0, 0))
    return pl.pallas_call(
        body, name="gdn_bwd", grid=(NCH,),
        in_specs=[rev(GW), rev(GW), blk4(DH, DH), blk4(CL, CL), rev(GW, 2), rev(GW, 3), rev(GW, 4), rev(GW, 5),
                  halo(2), halo(3), halo(4), rev(LANES), _const((KS, 3 * GW)), _const((1, LANES)),
                  _const((1, LANES)), _const((1, DH))],
        out_specs=(rev(4 * GW), rev(LANES), _const((GDN_STATS, 3 * GW))),
        out_shape=(jax.ShapeDtypeStruct((S, 4 * GW), BF16), jax.ShapeDtypeStruct((S, LANES), BF16),
                   jax.ShapeDtypeStruct((GDN_STATS, 3 * GW), F32)),
        scratch_shapes=[pltpu.VMEM((SH + CL, 3 * GW), F32), pltpu.VMEM((CL + SH, 3 * GW), F32),
                        pltpu.VMEM((NH, DH, DH), F32)],
        compiler_params=_params(dimension_semantics=("arbitrary",)),
    )(d_out_b, o_pre, s_in, t_inv, p_main, p_main, p_main, p_main, p_main, p_main, p_main, p_ba,
      gdn_conv_w, alog_l, dt_l, gdn_nw)


def _bwd_in(dp_conf, dp_gdn, dp_ba, x, dx1, nw1, modnb, bada, w_main, w_ba):
    def body(dc_ref, dg_ref, db_ref, x_ref, dx1_ref, nw_ref, mod_ref, b_ref, wm_ref, wb_ref, gx_ref, st_ref):
        i = pl.program_id(0)

        @pl.when(i == 0)
        def _():
            st_ref[...] = jnp.zeros((8, D), F32)

        dh = (_dot(dc_ref[...], wm_ref[0:2 * CW, :]) + _dot(dg_ref[...], wm_ref[2 * CW:NMAIN, :])
              + _dot(db_ref[...], wb_ref[...]))
        xv = x_ref[...]
        r = lax.rsqrt(jnp.mean(xv * xv, axis=-1, keepdims=True) + EPS)
        xr = xv * r
        st_ref[0:1, :] += _colsum(dh)
        st_ref[1:2, :] += _colsum(dh * (xr * nw_ref[...]))
        dxn = dh * (1.0 + _mod(mod_ref, b_ref, 1))
        st_ref[2:3, :] += _colsum(dxn * xr)
        dxr = dxn * nw_ref[...]
        gx_ref[...] = dx1_ref[...] + r * (dxr - xr * jnp.mean(dxr * xr, axis=-1, keepdims=True))

    tile = lambda w: pl.BlockSpec((TM, w), lambda i: (i, 0))
    return pl.pallas_call(
        body, name="bwd_in", grid=(NT,),
        in_specs=[tile(2 * CW), tile(4 * GW), tile(LANES), tile(D), tile(D), _const((1, D)), _const((1, 6 * D)),
                  _const((1, 6 * D)), _const((NMAIN, D)), _const((LANES, D))],
        out_specs=(tile(D), _const((8, D))),
        out_shape=(jax.ShapeDtypeStruct((S, D), F32), jax.ShapeDtypeStruct((8, D), F32)),
        compiler_params=_params(dimension_semantics=("arbitrary",)),
    )(dp_conf, dp_gdn, dp_ba, x, dx1, nw1, modnb, bada, w_main, w_ba)


def _adamw(w, g, m, v):
    m = ADAM_B1 * m + (1.0 - ADAM_B1) * g
    v = ADAM_B2 * v + (1.0 - ADAM_B2) * (g * g)
    m_hat = m / BC1
    v_hat = v / BC2
    delta = -ADAM_LR * (m_hat / (jnp.sqrt(v_hat) + ADAM_EPS) + ADAM_WD * w)
    return delta, m, v


ADAM_BLOCK_BYTES = 6 * 1024 * 1024


def _adam_tile(rows, cols):
    padded = -(-cols // LANES) * LANES
    if N_DEV * rows * padded * 4 <= ADAM_BLOCK_BYTES:
        return rows, cols
    best = None
    for tr in range(16, rows, 16):
        if rows % tr == 0 and N_DEV * tr * padded * 4 <= ADAM_BLOCK_BYTES:
            best = tr
    if best is not None:
        return best, cols
    rows_padded = -(-rows // 16) * 16
    tc = LANES
    for cand in range(LANES, cols, LANES):
        if cols % cand == 0 and N_DEV * rows_padded * cand * 4 <= ADAM_BLOCK_BYTES:
            tc = cand
    return rows, tc


def _reduce_adam(name, parts, w, m, v, own=None):
    rows, cols = w.shape
    tr, tc = _adam_tile(rows, cols)

    def body(*refs):
        p_ref, w_ref, m_ref, v_ref = refs[:4]
        g_ref, d_ref, nm_ref, nv_ref = refs[-4:]
        if own is None:
            part = lambda j: p_ref[j].astype(F32)
        else:
            me = 4 * lax.axis_index("x") + 2 * lax.axis_index("y") + lax.axis_index("c")
            part = lambda j: jnp.where(me == j, refs[4][...], p_ref[j]).astype(F32)
        g = part(0)
        for j in range(1, N_DEV):
            g = g + part(j)
        g_ref[...] = g
        d_ref[...], nm_ref[...], nv_ref[...] = _adamw(w_ref[...], g, m_ref[...], v_ref[...])

    blk = pl.BlockSpec((tr, tc), lambda i, j: (i, j))
    sds = jax.ShapeDtypeStruct((rows, cols), F32)
    extra = [] if own is None else [own]
    return pl.pallas_call(
        body, name=name, grid=(rows // tr, cols // tc),
        in_specs=[pl.BlockSpec((N_DEV, tr, tc), lambda i, j: (0, i, j)), blk, blk, blk] + [blk] * len(extra),
        out_specs=(blk, blk, blk, blk), out_shape=(sds, sds, sds, sds),
        compiler_params=_params(dimension_semantics=("arbitrary", "arbitrary")),
    )(parts, w, m, v, *extra)


def _ada_adam(c_all, dmod_sh, w, m, v):
    rows, cols = w.shape
    tr = 256

    def body(c_ref, dm_ref, w_ref, m_ref, v_ref, g_ref, d_ref, nm_ref, nv_ref):
        cv = c_ref[...]
        g = _dot_tn(cv * _sig(cv), dm_ref[...], HI)
        g_ref[...] = g
        d_ref[...], nm_ref[...], nv_ref[...] = _adamw(w_ref[...], g, m_ref[...], v_ref[...])

    blk = pl.BlockSpec((tr, cols), lambda i: (i, 0))
    sds = jax.ShapeDtypeStruct((rows, cols), F32)
    return pl.pallas_call(
        body, name="ada_adam", grid=(rows // tr,),
        in_specs=[pl.BlockSpec((N_DEV, tr), lambda i: (0, i)), _const((N_DEV, cols)), blk, blk, blk],
        out_specs=(blk, blk, blk, blk), out_shape=(sds, sds, sds, sds),
        compiler_params=_params(dimension_semantics=("arbitrary",)),
    )(c_all, dmod_sh, w, m, v)


def _lanes(a, at=0):
    return jnp.pad(a, ((0, 0), (at, LANES - at - a.shape[1])))


WEIGHT_NAMES = ["w_ada", "b_ada", "norm_mix_w", "w_in", "conv_w", "conv_b", "conv_gn_w", "conv_gn_b", "gdn_conv_w",
                "gdn_a_log", "gdn_dt_bias", "gdn_norm_w", "w_out", "norm_ffn_w", "w_ffn_in", "w_ffn_out",
                "norm_final_w"]


SMALL_LAYOUT = [("b_ada", 0, 48, LANES), ("norm_mix_w", 48, 8, LANES), ("norm_ffn_w", 56, 8, LANES),
                ("norm_final_w", 64, 8, LANES), ("conv_b", 72, 4, LANES), ("conv_gn_w", 76, 4, LANES),
                ("conv_gn_b", 80, 4, LANES), ("gdn_norm_w", 84, 1, LANES), ("gdn_a_log", 85, 1, NH),
                ("gdn_dt_bias", 86, 1, NH)]
LOSS_ROW = 87


def _adam_small(g_small, weights, m1, m2):
    names = [nm for nm, _, _, _ in SMALL_LAYOUT]
    k = len(names)

    def body(*refs):
        g_ref = refs[0]
        w_refs, m_refs, v_refs = refs[1:1 + k], refs[1 + k:1 + 2 * k], refs[1 + 2 * k:1 + 3 * k]
        loss_ref = refs[1 + 3 * k]
        outs = refs[2 + 3 * k:2 + 7 * k]
        total = refs[-1]
        g = g_ref[0]
        for j in range(1, N_DEV):
            g = g + g_ref[j]
        total[...] = g
        loss_ref[...] = total[LOSS_ROW:LOSS_ROW + 1, :]
        for i, (_, r0, rows, lanes) in enumerate(SMALL_LAYOUT):
            gp = total[r0:r0 + rows, 0:lanes]
            outs[i][...] = gp
            outs[k + i][...], outs[2 * k + i][...], outs[3 * k + i][...] = _adamw(
                w_refs[i][...], gp, m_refs[i][...], v_refs[i][...])

    shapes = [jax.ShapeDtypeStruct((rows, lanes), F32) for _, _, rows, lanes in SMALL_LAYOUT]
    res = pl.pallas_call(
        body, name="adam_small",
        out_shape=tuple([jax.ShapeDtypeStruct((1, LANES), F32)] + shapes * 4),
        scratch_shapes=[pltpu.VMEM((SMALL_ROWS, LANES), F32)],
        compiler_params=_params(),
    )(g_small, *[weights[n] for n in names], *[m1[n] for n in names], *[m2[n] for n in names])
    kinds = [dict(zip(names, res[1 + q * k:1 + (q + 1) * k])) for q in range(4)]
    return res[0], kinds


def _mix_forward(w, xs, modnb, between=None):
    w_main = w["w_in"]
    w_ba = jnp.pad(w["w_in"][NMAIN:], ((0, LANES - 2 * NH), (0, 0)))
    alog_l = _lanes(w["gdn_a_log"], NH)
    dt_l = _lanes(w["gdn_dt_bias"], NH)
    p_main, p_ba, hb1 = _fwd_in(xs, w["norm_mix_w"], modnb, w["b_ada"], w_main, w_ba)
    w_o, u_o, qg, kd, qk, cd, t_inv = _gdn_prep(p_main, p_ba, w["gdn_conv_w"], alog_l, dt_l)
    out_b, o_pre, s_in = _gdn_scan(w_o, u_o, qg, kd, qk, cd, p_main, w["gdn_norm_w"])
    conv_b = w["conv_b"] if between is None else _after(w["conv_b"], between(out_b))
    y_conv, out_a = _conf_fwd(p_main, w["conv_w"], conv_b, w["conv_gn_w"], w["conv_gn_b"])
    return dict(w_main=w_main, w_ba=w_ba, alog_l=alog_l, dt_l=dt_l, p_main=p_main, p_ba=p_ba, hb1=hb1,
                y_conv=y_conv, out_a=out_a, out_b=out_b, o_pre=o_pre, s_in=s_in, t_inv=t_inv)


def _ffn_stage(w, f, xs, tgt, modnb):
    x1, mix, oab = _fwd_out(f["out_a"], f["out_b"], xs, modnb, w["b_ada"], w["w_out"])
    hb2, act, pre, dx2, dffn, st_fwd = _ffn_forward(x1, tgt, modnb, w["b_ada"], w["norm_ffn_w"],
                                                    w["norm_final_w"], w["w_ffn_in"], w["w_ffn_out"])
    gw_ffn_out = _grad_w_ffn_out(act, dffn)
    df, dx1, st_bwd = _ffn_backward(dffn, pre, x1, dx2, modnb, w["b_ada"], w["norm_ffn_w"], w["w_ffn_in"],
                                    w["w_ffn_out"])
    gw_ffn_in = _grad_w_ffn_in(hb2, df)
    return dict(mix=mix, oab=oab, dx1=dx1, st_ffn=st_fwd + st_bwd, gw_ffn_in=gw_ffn_in, gw_ffn_out=gw_ffn_out)


def _out_backward(w, g, modnb):
    dmix, d_out_a, d_out_b, st_out = _bwd_out(g["dx1"], g["mix"], modnb, w["b_ada"], w["w_out"])
    return dict(d_out_a=d_out_a, d_out_b=d_out_b, st_out=st_out, gw_out=_grad_w("grad_w_out", g["oab"], dmix, 512))


def _heads_backward(w, f, a):
    dp_conf, st_conf = _conf_bwd(a["d_out_a"], f["y_conv"], f["p_main"], w["conv_w"], w["conv_gn_w"],
                                 w["conv_gn_b"])
    dp_gdn, dp_ba, st_gdn = _gdn_bwd(a["d_out_b"], f["o_pre"], f["s_in"], f["t_inv"], f["p_main"], f["p_ba"],
                                     w["gdn_conv_w"], f["alog_l"], f["dt_l"], w["gdn_norm_w"])
    gw_in = _grad_w_in(dp_conf, dp_gdn, dp_ba, f["hb1"])[:NIN]
    return dict(dp_conf=dp_conf, dp_gdn=dp_gdn, dp_ba=dp_ba, st_conf=st_conf, st_gdn=st_gdn, gw_in=gw_in,
                gw_conv=st_conf[0:KC], gw_gconv=st_gdn[0:KS])


def _in_backward(w, f, g, a, h, xs, modnb):
    st_out, st_conf, st_gdn, st_ffn = a["st_out"], h["st_conf"], h["st_gdn"], g["st_ffn"]
    grad_x, st_in = _bwd_in(h["dp_conf"], h["dp_gdn"], h["dp_ba"], xs, g["dx1"], w["norm_mix_w"], modnb,
                            w["b_ada"], f["w_main"], f["w_ba"])
    dmod = jnp.concatenate([st_in[0:1], st_in[1:2], st_out[0:1], st_ffn[2:3], st_ffn[3:4], st_ffn[1:2]], axis=1)
    small = jnp.concatenate([
        dmod.reshape(48, LANES), st_in[2:3].reshape(8, LANES), st_ffn[4:5].reshape(8, LANES),
        st_ffn[0:1].reshape(8, LANES), st_conf[31:32].reshape(4, LANES), st_conf[32:33].reshape(4, LANES),
        st_conf[33:34].reshape(4, LANES), st_gdn[4:5, 0:LANES],
        _lanes(st_gdn[5:6, NH:2 * NH]), _lanes(st_gdn[6:7, NH:2 * NH]), st_ffn[5:6, 0:LANES]], axis=0)
    return dict(grad_x=grad_x, small=small)


def _local(w, xs, tgt, modnb):
    f = _mix_forward(w, xs, modnb)
    g = _ffn_stage(w, f, xs, tgt, modnb)
    a = _out_backward(w, g, modnb)
    h = _heads_backward(w, f, a)
    b = _in_backward(w, f, g, a, h, xs, modnb)
    return dict(b, gw_in=h["gw_in"], gw_conv=h["gw_conv"], gw_gconv=h["gw_gconv"], gw_out=a["gw_out"],
                gw_ffn_in=g["gw_ffn_in"], gw_ffn_out=g["gw_ffn_out"])


def kernel(x, c, w_ada, b_ada, norm_mix_w, w_in, conv_w, conv_b, conv_gn_w, conv_gn_b, gdn_conv_w, gdn_a_log, gdn_dt_bias, gdn_norm_w, w_out, norm_ffn_w, w_ffn_in, w_ffn_out, norm_final_w, loss_target, m_w_ada, m_b_ada, m_norm_mix_w, m_w_in, m_conv_w, m_conv_b, m_conv_gn_w, m_conv_gn_b, m_gdn_conv_w, m_gdn_a_log, m_gdn_dt_bias, m_gdn_norm_w, m_w_out, m_norm_ffn_w, m_w_ffn_in, m_w_ffn_out, m_norm_final_w, v_w_ada, v_b_ada, v_norm_mix_w, v_w_in, v_conv_w, v_conv_b, v_conv_gn_w, v_conv_gn_b, v_gdn_conv_w, v_gdn_a_log, v_gdn_dt_bias, v_gdn_norm_w, v_w_out, v_norm_ffn_w, v_w_ffn_in, v_w_ffn_out, v_norm_final_w):
    me = 4 * lax.axis_index("x") + 2 * lax.axis_index("y") + lax.axis_index("c")
    xs = x.reshape(S, D)
    tgt = loss_target.reshape(S, D)

    g_c, g_cw, g_gcw = _exchange("gather_cond", [c, conv_w[0], gdn_conv_w[0]], [False] * 3)
    c_all = g_c.reshape(N_DEV, D)
    g_mod, mod_token = _exchange("gather_mod", [_mod_shard(c_all, w_ada[0])], [False], with_token=True)
    modnb = lax.dynamic_index_in_dim(g_mod, me, axis=1, keepdims=False).reshape(1, 6 * D)

    late = [w_out[0].astype(BF16), jnp.transpose(w_ffn_in[0]).astype(BF16), w_ffn_out[0].astype(BF16)]
    g_win, *late_lands = _gather_two_level(
        "gather_weights", [_after(jnp.transpose(w_in[0]), mod_token).astype(BF16)] + late, seed_only=(1, 2, 3))
    late_started = _exchange_start("gather_late_start", late, late_lands, [False] * 3, only=LEVEL_ONE)
    modnb = _after(modnb, late_started[-1])
    w = dict(b_ada=b_ada, norm_mix_w=norm_mix_w, conv_b=conv_b, conv_gn_w=conv_gn_w, conv_gn_b=conv_gn_b,
             gdn_a_log=gdn_a_log, gdn_dt_bias=gdn_dt_bias, gdn_norm_w=gdn_norm_w, norm_ffn_w=norm_ffn_w,
             norm_final_w=norm_final_w.reshape(1, D),
             conv_w=jnp.transpose(g_cw, (1, 0, 2)).reshape(KC, CW),
             gdn_conv_w=jnp.transpose(g_gcw, (1, 0, 2)).reshape(KS, 3 * GW),
             w_in=g_win.reshape(NIN, D))

    relay = {}

    def relay_late(out_b):
        _, late_landed = _exchange_wait("gather_late_wait", late_started, [False] * 3, (out_b,), only=LEVEL_ONE)
        relay["started"] = _relay_start("gather_late_relay_start", late_landed)
        return relay["started"][-1]

    f = _mix_forward(w, xs, modnb, relay_late)
    g_wout, g_wfi, g_wfo = _relay_wait("gather_late_relay_wait", relay["started"], (f["out_a"],))
    w.update(w_out=g_wout.reshape(D, D), w_ffn_in=g_wfi, w_ffn_out=g_wfo.reshape(4, FB, D))
    g = _ffn_stage(w, f, xs, tgt, modnb)

    ffn_grads = [g["gw_ffn_in"], g["gw_ffn_out"].reshape(N_DEV, DFF // N_DEV, D)]
    ffn_started = _exchange_start("scatter_ffn_start", ffn_grads,
                                  [lax.empty(a.shape, a.dtype) for a in ffn_grads], [True] * 2)
    a = _out_backward(w, g, _after(modnb, ffn_started[-1]))
    out_grads = [a["gw_out"].reshape(N_DEV, D // N_DEV, D)]
    out_started = _exchange_start("scatter_out_start", out_grads,
                                  [lax.empty(t.shape, t.dtype) for t in out_grads], [True])
    h = _heads_backward(dict(w, conv_gn_w=_after(w["conv_gn_w"], out_started[-1])), f, a)

    in_grads = [h["gw_in"].reshape(N_DEV, NIN // N_DEV, D),
                jnp.transpose(h["gw_conv"].reshape(KC, N_DEV, CW // N_DEV), (1, 0, 2)),
                jnp.transpose(h["gw_gconv"].reshape(KS, N_DEV, 3 * GW // N_DEV), (1, 0, 2))]
    in_started = _exchange_start("scatter_in_start", in_grads,
                                 [lax.empty(t.shape, t.dtype) for t in in_grads], [True] * 3)
    loc = _in_backward(w, f, g, a, h, xs, _after(modnb, in_started[-1]))
    small_started = _exchange_start("gather_small_start", [loc["small"]],
                                    [lax.empty((N_DEV, SMALL_ROWS, LANES), F32)], [False])

    def own(sent):
        return lax.dynamic_index_in_dim(sent, me, axis=0, keepdims=False)

    big = {}
    (sent_fi, sent_fo), (r_fi, r_fo) = _exchange_wait("scatter_ffn_wait", ffn_started, [True] * 2,
                                                         (small_started[-1],))
    big["w_ffn_in"] = [jnp.transpose(t) for t in _reduce_adam(
        "adam_w_ffn_in", r_fi, jnp.transpose(w_ffn_in[0]), jnp.transpose(m_w_ffn_in[0]),
        jnp.transpose(v_w_ffn_in[0]), own(sent_fi))]
    big["w_ffn_out"] = _reduce_adam("adam_w_ffn_out", r_fo, w_ffn_out[0], m_w_ffn_out[0], v_w_ffn_out[0],
                                    own(sent_fo))
    (sent_out,), (r_out,) = _exchange_wait("scatter_out_wait", out_started, [True], (big["w_ffn_out"][0],))
    big["w_out"] = _reduce_adam("adam_w_out", r_out, w_out[0], m_w_out[0], v_w_out[0], own(sent_out))

    (sent_small,), (r_small,) = _exchange_wait("gather_small_wait", small_started, [False], (big["w_out"][0],))
    slot = lax.broadcasted_iota(jnp.int32, (N_DEV, 1, 1), 0)
    g_small = jnp.where(slot == me, sent_small[None], r_small)
    def views(b_, nm_, nf_, nl_, cb_, gw_, gb_, gn_, al_, dt_):
        arrs = [b_, nm_, nf_, nl_, cb_, gw_, gb_, gn_, al_, dt_]
        return {nm: t.reshape(rows, lanes) for (nm, _, rows, lanes), t in zip(SMALL_LAYOUT, arrs)}

    loss_row, res = _adam_small(
        g_small,
        views(b_ada, norm_mix_w, norm_ffn_w, norm_final_w, conv_b, conv_gn_w, conv_gn_b, gdn_norm_w, gdn_a_log,
              gdn_dt_bias),
        views(m_b_ada, m_norm_mix_w, m_norm_ffn_w, m_norm_final_w, m_conv_b, m_conv_gn_w, m_conv_gn_b,
              m_gdn_norm_w, m_gdn_a_log, m_gdn_dt_bias),
        views(v_b_ada, v_norm_mix_w, v_norm_ffn_w, v_norm_final_w, v_conv_b, v_conv_gn_w, v_conv_gn_b,
              v_gdn_norm_w, v_gdn_a_log, v_gdn_dt_bias))
    loss = loss_row[0, 0]
    small_shapes = dict(b_ada=(1, 6 * D), norm_mix_w=(1, D), norm_ffn_w=(1, D), norm_final_w=(D,),
                        conv_b=(1, CW), conv_gn_w=(1, CW), conv_gn_b=(1, CW), gdn_norm_w=(1, DH),
                        gdn_a_log=(1, NH), gdn_dt_bias=(1, NH))
    res = [{nm: t.reshape(small_shapes[nm]) for nm, t in kind.items()} for kind in res]

    dmod_rows = g_small[:, 0:48, :].reshape(N_DEV, 6 * D)
    dmod_sh = lax.dynamic_slice_in_dim(dmod_rows, me * (6 * D // N_DEV), 6 * D // N_DEV, axis=1)

    big["w_ada"] = _ada_adam(c_all, dmod_sh, w_ada[0], m_w_ada[0], v_w_ada[0])
    (sent_in, sent_cw, sent_gcw), (r_in, r_cw, r_gcw) = _exchange_wait(
        "scatter_in_wait", in_started, [True] * 3, (big["w_ada"][0],))
    big["w_in"] = [jnp.transpose(t) for t in _reduce_adam(
        "adam_w_in", r_in, jnp.transpose(w_in[0]), jnp.transpose(m_w_in[0]), jnp.transpose(v_w_in[0]),
        own(sent_in))]
    big["conv_w"] = _reduce_adam("adam_conv_w", r_cw, conv_w[0], m_conv_w[0], v_conv_w[0], own(sent_cw))
    big["gdn_conv_w"] = _reduce_adam("adam_gdn_conv_w", r_gcw, gdn_conv_w[0], m_gdn_conv_w[0], v_gdn_conv_w[0],
                                     own(sent_gcw))
    outs = [loss, loc["grad_x"].reshape(1, S, D)]
    for kind in range(4):
        for nm in WEIGHT_NAMES:
            outs.append(big[nm][kind][None] if nm in big else res[kind][nm])
    return tuple(outs)
```

```python
import functools

import jax
import jax.numpy as jnp
from jax import lax
from jax.experimental import pallas as pl
from jax.experimental.pallas import tpu as pltpu

F32 = jnp.float32
BF16 = jnp.bfloat16
HI = lax.Precision.HIGHEST
MESH = pl.DeviceIdType.MESH

N_DEV = 8
S = 2048
D = 1024
TM = 256
NT = S // TM
CW = 512
KC = 31
NG = 8
GSZ = CW // NG
HALO = 32
GW = 512
NH = 4
DH = 128
KS = 4
SH = 8
CL = 64
NCH = S // CL
NMAIN = 2 * CW + 4 * GW
NIN = NMAIN + 2 * NH
DFF = 2816
FB = DFF // 4
EPS = 1e-6
QSCALE = DH ** -0.5
LANES = 128
SMALL_ROWS = 88

ADAM_LR = 0.001
ADAM_B1 = 0.9
ADAM_B2 = 0.999
ADAM_EPS = 1e-08
ADAM_WD = 0.01
ADAM_STEP = 10
BC1 = 1.0 - ADAM_B1 ** ADAM_STEP
BC2 = 1.0 - ADAM_B2 ** ADAM_STEP

MIB = 1024 * 1024
VMEM_LIMIT_MIB = 32


def _params(limit_mib=VMEM_LIMIT_MIB, **kw):
    return pltpu.CompilerParams(vmem_limit_bytes=limit_mib * MIB, **kw)


def _sig(x):
    return jax.nn.sigmoid(x)


GP = BF16


def _operands(a, b, prec):
    if prec is BF16:
        return a.astype(BF16), b.astype(BF16), None
    return a, b, prec


def _dot(a, b, prec=None):
    a, b, prec = _operands(a, b, prec)
    return jnp.dot(a, b, preferred_element_type=F32, precision=prec)


def _dot_nt(a, b, prec=None):
    a, b, prec = _operands(a, b, prec)
    return lax.dot_general(a, b, (((1,), (1,)), ((), ())), preferred_element_type=F32, precision=prec)


def _dot_tn(a, b, prec=None):
    a, b, prec = _operands(a, b, prec)
    return lax.dot_general(a, b, (((0,), (0,)), ((), ())), preferred_element_type=F32, precision=prec)


def _lockstep(gens):
    gens = list(gens)
    while gens:
        alive = []
        for g in gens:
            try:
                next(g)
                alive.append(g)
            except StopIteration:
                pass
        gens = alive


def _rowsum(x):
    return jnp.sum(x, axis=-1, keepdims=True)


def _colsum(x):
    return jnp.sum(x, axis=0, keepdims=True)


def _mod(mod_ref, b_ref, k):
    return mod_ref[:, k * D:(k + 1) * D] + b_ref[:, k * D:(k + 1) * D]


def _const(shape):
    nd = len(shape)
    return pl.BlockSpec(shape, lambda *_: (0,) * nd)


def _const1(shape):
    nd = len(shape)
    return pl.BlockSpec(shape, lambda *_: (0,) * nd, pipeline_mode=pl.Buffered(1))


PEER_FLIPS = [(dx, dy, dc) for dx in (0, 1) for dy in (0, 1) for dc in (0, 1)][1:]


def _after(x, token):
    return x + token[0:1, 0:1].astype(x.dtype).reshape((1,) * x.ndim)


def _exchange(name, srcs, per_dest, seed_only=(), with_token=False):
    n = len(srcs)
    out_shape = []
    for a, pd in zip(srcs, per_dest):
        blk = a.shape[1:] if pd else a.shape
        out_shape.append(jax.ShapeDtypeStruct((N_DEV,) + tuple(blk), a.dtype))

    def body(*refs):
        src = refs[:n]
        dst = refs[n:2 * n]
        send_sems, recv_sems, local_sems = refs[-3:]
        if with_token:
            refs[2 * n][...] = jnp.zeros((8, LANES), F32)
        x, y, c = lax.axis_index("x"), lax.axis_index("y"), lax.axis_index("c")
        me = 4 * x + 2 * y + c

        def piece(i, j):
            return src[i].at[j] if per_dest[i] else src[i]

        copies = []
        for k, (dx, dy, dc) in enumerate(PEER_FLIPS):
            px = 1 - x if dx else x
            py = 1 - y if dy else y
            pc = 1 - c if dc else c
            pj = 4 * px + 2 * py + pc
            for i in range(n):
                if i in seed_only:
                    continue
                cp = pltpu.make_async_remote_copy(
                    src_ref=piece(i, pj), dst_ref=dst[i].at[me],
                    send_sem=send_sems.at[k * n + i], recv_sem=recv_sems.at[k * n + i],
                    device_id=(px, py, pc), device_id_type=MESH)
                cp.start()
                arrive = pltpu.make_async_remote_copy(
                    src_ref=piece(i, pj), dst_ref=dst[i].at[pj],
                    send_sem=send_sems.at[k * n + i], recv_sem=recv_sems.at[k * n + i],
                    device_id=(px, py, pc), device_id_type=MESH)
                copies.append((cp, arrive))
        own = []
        for i in range(n):
            lc = pltpu.make_async_copy(piece(i, me), dst[i].at[me], local_sems.at[i])
            lc.start()
            own.append(lc)
        for cp, arrive in copies:
            arrive.wait_recv()
        for cp, arrive in copies:
            cp.wait_send()
        for lc in own:
            lc.wait()

    any_spec = pl.BlockSpec(memory_space=pl.ANY)
    out_specs = [any_spec] * n
    if with_token:
        out_shape.append(jax.ShapeDtypeStruct((8, LANES), F32))
        out_specs.append(pl.BlockSpec(memory_space=pltpu.VMEM))
    return pl.pallas_call(
        body, name=name, out_shape=tuple(out_shape),
        in_specs=[any_spec] * n, out_specs=tuple(out_specs),
        scratch_shapes=[pltpu.SemaphoreType.DMA((7 * n,)), pltpu.SemaphoreType.DMA((7 * n,)),
                        pltpu.SemaphoreType.DMA((n,))],
        compiler_params=pltpu.CompilerParams(has_side_effects=True),
    )(*srcs)


CHIP_FLIPS = [(0, 1), (1, 0), (1, 1)]
LEVEL_ONE = [k for k, (dx, dy, dc) in enumerate(PEER_FLIPS) if (dx, dy, dc) == (0, 0, 1) or dc == 0]


def _chip_peers(x, y):
    return [(1 - x if dx else x, 1 - y if dy else y) for dx, dy in CHIP_FLIPS]


def _gather_two_level(name, srcs, seed_only=()):
    n = len(srcs)
    live = [i for i in range(n) if i not in seed_only]

    def body(*refs):
        src, dst = refs[:n], refs[n:2 * n]
        send_sems, recv_sems, local_sems = refs[2 * n:2 * n + 3]
        bounce = refs[2 * n + 3:]
        x, y, c = lax.axis_index("x"), lax.axis_index("y"), lax.axis_index("c")
        me = 4 * x + 2 * y + c
        sibling = (x, y, 1 - c)
        chips = _chip_peers(x, y)

        def copy(k, i, src_ref, slot, to):
            return pltpu.make_async_remote_copy(
                src_ref=src_ref, dst_ref=dst[i].at[slot], send_sem=send_sems.at[k * n + i],
                recv_sem=recv_sems.at[k * n + i], device_id=to, device_id_type=MESH)

        first = []
        for i in live:
            first.append(copy(0, i, src[i], me, sibling))
            first += [copy(1 + j, i, src[i], me, (px, py, c)) for j, (px, py) in enumerate(chips)]
        for cp in first:
            cp.start()
        up = [pltpu.make_async_copy(src[i], bounce[i], local_sems.at[i]) for i in range(n)]
        for cp in up:
            cp.start()
        for cp in up:
            cp.wait()
        own = [pltpu.make_async_copy(bounce[i], dst[i].at[me], local_sems.at[i]) for i in range(n)]
        for cp in own:
            cp.start()
        passed = []
        for j, (px, py) in enumerate(chips):
            slot = 4 * px + 2 * py + c
            for i in live:
                copy(1 + j, i, src[i], slot, (px, py, c)).wait_recv()
                fwd = copy(4 + j, i, dst[i].at[slot], slot, sibling)
                fwd.start()
                passed.append(fwd)
        for i in live:
            copy(0, i, src[i], 4 * x + 2 * y + 1 - c, sibling).wait_recv()
            for j, (px, py) in enumerate(chips):
                copy(4 + j, i, src[i], 4 * px + 2 * py + 1 - c, sibling).wait_recv()
        for cp in first + passed:
            cp.wait_send()
        for cp in own:
            cp.wait()

    any_spec = pl.BlockSpec(memory_space=pl.ANY)
    return pl.pallas_call(
        body, name=name, out_shape=tuple(jax.ShapeDtypeStruct((N_DEV,) + a.shape, a.dtype) for a in srcs),
        in_specs=[any_spec] * n, out_specs=tuple([any_spec] * n),
        scratch_shapes=[pltpu.SemaphoreType.DMA((7 * n,)), pltpu.SemaphoreType.DMA((7 * n,)),
                        pltpu.SemaphoreType.DMA((n,))] + [pltpu.VMEM(a.shape, a.dtype) for a in srcs],
        compiler_params=pltpu.CompilerParams(has_side_effects=True),
    )(*srcs)


def _relay_copy(land, sems, i, n, j, slot, sibling):
    send_sems, recv_sems = sems
    return pltpu.make_async_remote_copy(
        src_ref=land[i].at[slot], dst_ref=land[i].at[slot], send_sem=send_sems.at[j * n + i],
        recv_sem=recv_sems.at[j * n + i], device_id=sibling, device_id_type=MESH)


def _relay_start(name, lands):
    n = len(lands)

    def body(*refs):
        land = refs[:n]
        sems = refs[n], refs[n + 1]
        x, y, c = lax.axis_index("x"), lax.axis_index("y"), lax.axis_index("c")
        for j, (px, py) in enumerate(_chip_peers(x, y)):
            for i in range(n):
                _relay_copy(land, sems, i, n, j, 4 * px + 2 * py + c, (x, y, 1 - c)).start()
        refs[-1][...] = jnp.zeros((8, LANES), F32)

    return pl.pallas_call(
        body, name=name,
        out_shape=(pltpu.SemaphoreType.DMA((3 * n,)), pltpu.SemaphoreType.DMA((3 * n,)),
                   *[pltpu.HBM(a.shape, a.dtype) for a in lands], jax.ShapeDtypeStruct((8, LANES), F32)),
        in_specs=[HBM_SPEC] * n,
        out_specs=(SEM_SPEC, SEM_SPEC, *[HBM_SPEC] * n, pl.BlockSpec(memory_space=pltpu.VMEM)),
        input_output_aliases={i: 2 + i for i in range(n)},
        compiler_params=pltpu.CompilerParams(has_side_effects=DATAFLOW),
    )(*[pltpu.with_memory_space_constraint(a, pltpu.HBM) for a in lands])


def _relay_wait(name, started, after):
    n = len(started) - 3
    arrays = list(started[2:2 + n])

    def body(*refs):
        land = refs[:n]
        sems = refs[n], refs[n + 1]
        x, y, c = lax.axis_index("x"), lax.axis_index("y"), lax.axis_index("c")
        for j, (px, py) in enumerate(_chip_peers(x, y)):
            for i in range(n):
                _relay_copy(land, sems, i, n, j, 4 * px + 2 * py + c, (x, y, 1 - c)).wait_send()
                _relay_copy(land, sems, i, n, j, 4 * px + 2 * py + 1 - c, (x, y, 1 - c)).wait_recv()

    return pl.pallas_call(
        body, name=name,
        out_shape=tuple(pltpu.HBM(a.shape, a.dtype) for a in arrays),
        in_specs=[HBM_SPEC] * n + [SEM_SPEC, SEM_SPEC] + [pl.BlockSpec(memory_space=pl.ANY)] * len(after),
        out_specs=tuple([HBM_SPEC] * n),
        input_output_aliases={i: i for i in range(n)},
        compiler_params=pltpu.CompilerParams(has_side_effects=DATAFLOW),
    )(*arrays, started[0], started[1], *after)


HBM_SPEC = pl.BlockSpec(memory_space=pltpu.HBM)
SEM_SPEC = pl.BlockSpec(memory_space=pltpu.SEMAPHORE)
DATAFLOW = pltpu.SideEffectType.DATAFLOW_SIDE_EFFECTING


def _peers(only=None):
    x, y, c = lax.axis_index("x"), lax.axis_index("y"), lax.axis_index("c")
    out = []
    for k, (dx, dy, dc) in enumerate(PEER_FLIPS):
        if only is not None and k not in only:
            continue
        px = 1 - x if dx else x
        py = 1 - y if dy else y
        pc = 1 - c if dc else c
        out.append((k, (px, py, pc), 4 * px + 2 * py + pc))
    return 4 * x + 2 * y + c, out


def _exchange_start(name, srcs, lands, per_dest, only=None):
    n = len(srcs)

    def body(*refs):
        src, land = refs[:n], refs[n:2 * n]
        send_sems, recv_sems = refs[2 * n], refs[2 * n + 1]
        token = refs[-1]
        me, peers = _peers(only)
        for k, peer, pj in peers:
            for i in range(n):
                pltpu.make_async_remote_copy(
                    src_ref=src[i].at[pj] if per_dest[i] else src[i], dst_ref=land[i].at[me],
                    send_sem=send_sems.at[k * n + i], recv_sem=recv_sems.at[k * n + i],
                    device_id=peer, device_id_type=MESH).start()
        token[...] = jnp.zeros((8, LANES), F32)

    arrays = list(srcs) + list(lands)
    return pl.pallas_call(
        body, name=name,
        out_shape=(pltpu.SemaphoreType.DMA((7 * n,)), pltpu.SemaphoreType.DMA((7 * n,)),
                   *[pltpu.HBM(a.shape, a.dtype) for a in arrays], jax.ShapeDtypeStruct((8, LANES), F32)),
        in_specs=[HBM_SPEC] * (2 * n),
        out_specs=(SEM_SPEC, SEM_SPEC, *[HBM_SPEC] * (2 * n), pl.BlockSpec(memory_space=pltpu.VMEM)),
        input_output_aliases={i: 2 + i for i in range(2 * n)},
        compiler_params=pltpu.CompilerParams(has_side_effects=DATAFLOW),
    )(*[pltpu.with_memory_space_constraint(a, pltpu.HBM) for a in arrays])


def _exchange_wait(name, started, per_dest, after, only=None):
    n = (len(started) - 3) // 2
    send_sems, recv_sems = started[0], started[1]
    arrays = list(started[2:2 + 2 * n])

    def body(*refs):
        src, land = refs[:n], refs[n:2 * n]
        send, recv = refs[2 * n], refs[2 * n + 1]
        me, peers = _peers(only)
        for k, peer, pj in peers:
            for i in range(n):
                cp = pltpu.make_async_remote_copy(
                    src_ref=src[i].at[pj] if per_dest[i] else src[i], dst_ref=land[i].at[pj],
                    send_sem=send.at[k * n + i], recv_sem=recv.at[k * n + i],
                    device_id=peer, device_id_type=MESH)
                cp.wait_send()
                cp.wait_recv()

    outs = pl.pallas_call(
        body, name=name,
        out_shape=tuple(pltpu.HBM(a.shape, a.dtype) for a in arrays),
        in_specs=[HBM_SPEC] * (2 * n) + [SEM_SPEC, SEM_SPEC] + [pl.BlockSpec(memory_space=pl.ANY)] * len(after),
        out_specs=tuple([HBM_SPEC] * (2 * n)),
        input_output_aliases={i: i for i in range(2 * n)},
        compiler_params=pltpu.CompilerParams(has_side_effects=DATAFLOW),
    )(*arrays, send_sems, recv_sems, *after)
    return outs[:n], outs[n:]


def _mod_shard(c_all, w_ada):
    def body(c_ref, w_ref, o_ref):
        cv = c_ref[...]
        ca = cv * _sig(cv)
        o_ref[...] = _dot(ca.astype(BF16), w_ref[...].astype(BF16))

    return pl.pallas_call(
        body, name="mod_shard", out_shape=jax.ShapeDtypeStruct((N_DEV, w_ada.shape[1]), F32),
        compiler_params=_params(),
    )(c_all, w_ada)


def _fwd_in(x, nw1, modnb, bada, w_main, w_ba):
    def body(x_ref, nw_ref, mod_ref, b_ref, wm_ref, wb_ref, pm_ref, pb_ref, hb_ref):
        xv = x_ref[...]
        r = lax.rsqrt(jnp.mean(xv * xv, axis=-1, keepdims=True) + EPS)
        h = (xv * r * nw_ref[...]) * (1.0 + _mod(mod_ref, b_ref, 1)) + _mod(mod_ref, b_ref, 0)
        hb = h.astype(BF16)
        hb_ref[...] = hb
        pm_ref[...] = _dot_nt(hb, wm_ref[...])
        pb_ref[...] = _dot_nt(hb, wb_ref[...])

    return pl.pallas_call(
        body, name="fwd_in", grid=(NT,),
        in_specs=[pl.BlockSpec((TM, D), lambda i: (i, 0)), _const((1, D)), _const((1, 6 * D)), _const((1, 6 * D)),
                  _const((NMAIN, D)), _const((LANES, D))],
        out_specs=(pl.BlockSpec((TM, NMAIN), lambda i: (i, 0)), pl.BlockSpec((TM, LANES), lambda i: (i, 0)),
                   pl.BlockSpec((TM, D), lambda i: (i, 0))),
        out_shape=(jax.ShapeDtypeStruct((S, NMAIN), F32), jax.ShapeDtypeStruct((S, LANES), F32),
                   jax.ShapeDtypeStruct((S, D), BF16)),
        compiler_params=_params(dimension_semantics=("arbitrary",)),
    )(x, nw1, modnb, bada, w_main, w_ba)


def _group_mean_matrix():
    ii = lax.broadcasted_iota(jnp.int32, (CW, CW), 0) // GSZ
    jj = lax.broadcasted_iota(jnp.int32, (CW, CW), 1) // GSZ
    return jnp.where(ii == jj, 1.0 / GSZ, 0.0).astype(F32)


SUB = 8
SHIFT_ROWS = HALO + TM - SUB


def _fill_shifted(buf, sh):
    for b in range(1, SUB):
        sh[b - 1] = buf[b:b + SHIFT_ROWS, :]


def _rows_at(buf, sh, off):
    a, b = divmod(off, SUB)
    if b == 0:
        return buf[off:off + TM, :]
    return sh[b - 1, SUB * a:SUB * a + TM, :]


def _group_mean(x, pm):
    hi = x.astype(BF16)
    r1 = x - hi.astype(F32)
    mid = r1.astype(BF16)
    lo = (r1 - mid.astype(F32)).astype(BF16)
    return _dot(hi, pm) + _dot(mid, pm) + _dot(lo, pm)


def _conf_fwd(p_main, conv_w, conv_b, gn_w, gn_b):
    def body(a_ref, g_ref, w_ref, b_ref, gw_ref, gb_ref, y_ref, oa_ref, ubuf, ush):
        i = pl.program_id(0)

        @pl.when(i == 0)
        def _():
            ubuf[0:HALO, :] = jnp.zeros((HALO, CW), F32)

        ubuf[HALO:HALO + TM, :] = a_ref[...] * _sig(g_ref[...])
        _fill_shifted(ubuf, ush)
        acc = jnp.zeros((TM, CW), F32) + b_ref[...]
        for k in range(KC):
            acc = acc + w_ref[k:k + 1, :] * _rows_at(ubuf, ush, HALO - (KC - 1) + k)
        y_ref[...] = acc
        ubuf[0:HALO, :] = ubuf[TM:TM + HALO, :]
        pm = _group_mean_matrix().astype(BF16)
        dlt = acc - _group_mean(acc, pm)
        var = _group_mean(dlt * dlt, pm)
        o = dlt * lax.rsqrt(var + EPS) * gw_ref[...] + gb_ref[...]
        oa_ref[...] = o * _sig(o)

    return pl.pallas_call(
        body, name="conf_fwd", grid=(NT,),
        in_specs=[pl.BlockSpec((TM, CW), lambda i: (i, 0)), pl.BlockSpec((TM, CW), lambda i: (i, 1)),
                  _const((KC, CW)), _const((1, CW)), _const((1, CW)), _const((1, CW))],
        out_specs=(pl.BlockSpec((TM, CW), lambda i: (i, 0)), pl.BlockSpec((TM, CW), lambda i: (i, 0))),
        out_shape=(jax.ShapeDtypeStruct((S, CW), F32), jax.ShapeDtypeStruct((S, CW), F32)),
        scratch_shapes=[pltpu.VMEM((HALO + TM, CW), F32), pltpu.VMEM((SUB - 1, SHIFT_ROWS, CW), F32)],
        compiler_params=_params(dimension_semantics=("arbitrary",)),
    )(p_main, p_main, conv_w, conv_b, gn_w, gn_b)


def _tri_iota():
    ii = lax.broadcasted_iota(jnp.int32, (CL, CL), 0)
    jj = lax.broadcasted_iota(jnp.int32, (CL, CL), 1)
    return ii, jj


def _gdn_gates(ba, alog_l, dt_l):
    beta_all = _sig(ba)
    xg = ba + dt_l
    sp = jnp.maximum(xg, 0.0) + jnp.log(1.0 + jnp.exp(-jnp.abs(xg)))
    neg_a = -jnp.exp(alog_l)
    return beta_all, neg_a * sp, xg, neg_a


def _ones_dot(ones, x):
    hi = x.astype(BF16)
    r1 = x - hi.astype(F32)
    mid = r1.astype(BF16)
    lo = (r1 - mid.astype(F32)).astype(BF16)
    return _dot(ones, hi) + _dot(ones, mid) + _dot(ones, lo)


def _gdn_cumsum(g_all):
    ii, jj = _tri_iota()
    low = jnp.where(ii >= jj, 1.0, 0.0).astype(BF16)
    gcum = _ones_dot(low, g_all)
    return gcum, jnp.transpose(gcum)


def _split(x):
    hi = x.astype(BF16)
    return hi, (x - hi.astype(F32)).astype(BF16)


def _dot_split(a, b):
    (ah, al), (bh, bl) = a, b
    return _dot(ah, bh) + (_dot(ah, bl) + _dot(al, bh))


def _unit_lower_inverses(mats):
    ii, jj = _tri_iota()
    eye = jnp.where(ii == jj, 1.0, 0.0).astype(F32)
    ts = [eye - a for a in mats]
    ps = [_dot_split(s, s) for s in map(_split, mats)]
    for _ in range(4):
        sp = [_split(p) for p in ps]
        ts = [t + _dot_split(_split(t), s) for t, s in zip(ts, sp)]
        ps = [_dot_split(s, s) for s in sp]
    return [t + _dot_split(_split(t), _split(p)) for t, p in zip(ts, ps)]


def _head_terms(qh, kh, beta, gcol, grow):
    ii, jj = _tri_iota()
    causal = ii >= jj
    strict = ii > jj
    rq = lax.rsqrt(_rowsum(qh * qh) + EPS)
    rk = lax.rsqrt(_rowsum(kh * kh) + EPS)
    qn = qh * rq
    kn = kh * rk
    qs = qn * QSCALE
    decay = jnp.where(causal, jnp.exp(jnp.where(causal, gcol - grow, 0.0)), 0.0)
    gam = jnp.exp(gcol)
    gl = gcol[CL - 1:CL, :]
    kds = jnp.exp(gl - gcol)
    cd = jnp.exp(gl)
    kb = kn * beta
    a = jnp.where(strict, _dot_nt(kb, kn, GP) * decay, 0.0)
    qk = jnp.where(causal, _dot_nt(qs, kn, GP) * decay, 0.0)
    return dict(rq=rq, rk=rk, qn=qn, kn=kn, qs=qs, decay=decay, gam=gam, kds=kds, cd=cd, kb=kb, a=a, qk=qk,
                causal=causal, strict=strict)


def _short_conv(w_ref, buf, rows=CL):
    acc = w_ref[0:1, :] * buf[SH - KS + 1:SH - KS + 1 + rows, :]
    for k in range(1, KS):
        off = SH - (KS - 1) + k
        acc = acc + w_ref[k:k + 1, :] * buf[off:off + rows, :]
    return acc


CPS = 4
TG = CPS * CL


def _gdn_prep(p_main, p_ba, gdn_conv_w, alog_l, dt_l):
    def body(q_ref, k_ref, v_ref, qh_ref, kh_ref, vh_ref, ba_ref, w_ref, al_ref, dt_ref,
             wo_ref, uo_ref, qg_ref, kd_ref, qk_ref, cd_ref, t_ref, xbuf):
        i = pl.program_id(0)
        first = i == 0
        xbuf[0:SH, 0:GW] = jnp.where(first, 0.0, qh_ref[...])
        xbuf[0:SH, GW:2 * GW] = jnp.where(first, 0.0, kh_ref[...])
        xbuf[0:SH, 2 * GW:3 * GW] = jnp.where(first, 0.0, vh_ref[...])
        xbuf[SH:SH + TG, 0:GW] = q_ref[...]
        xbuf[SH:SH + TG, GW:2 * GW] = k_ref[...]
        xbuf[SH:SH + TG, 2 * GW:3 * GW] = v_ref[...]
        conv = _short_conv(w_ref, xbuf, TG)
        qkv = conv * _sig(conv)
        beta_all, g_all, _, _ = _gdn_gates(ba_ref[...], al_ref[...], dt_ref[...])
        lane = lax.broadcasted_iota(jnp.int32, (8, LANES), 1)
        cums = [_gdn_cumsum(g_all[cc * CL:(cc + 1) * CL, :]) for cc in range(CPS)]
        pairs = [(cc, h) for cc in range(CPS) for h in range(NH)]
        terms, vbs = [], []
        for cc, h in pairs:
            r0, lo = cc * CL, h * DH
            beta = beta_all[r0:r0 + CL, h:h + 1]
            gcum, gcum_t = cums[cc]
            terms.append(_head_terms(qkv[r0:r0 + CL, lo:lo + DH], qkv[r0:r0 + CL, GW + lo:GW + lo + DH], beta,
                                     gcum[:, NH + h:NH + h + 1], gcum_t[NH + h:NH + h + 1, :]))
            vbs.append(qkv[r0:r0 + CL, 2 * GW + lo:2 * GW + lo + DH] * beta)
        invs = _unit_lower_inverses([f["a"] for f in terms])
        cds = [jnp.zeros((8, LANES), F32) for _ in range(CPS)]
        for (cc, h), f, t, vb in zip(pairs, terms, invs, vbs):
            r0, lo = cc * CL, h * DH
            t_ref[cc, h] = t
            uo_ref[r0:r0 + CL, lo:lo + DH] = _dot(t, vb, GP)
            wo_ref[r0:r0 + CL, lo:lo + DH] = _dot(t, f["kb"] * f["gam"], GP).astype(BF16)
            qg_ref[r0:r0 + CL, lo:lo + DH] = (f["qs"] * f["gam"]).astype(BF16)
            kd_ref[r0:r0 + CL, lo:lo + DH] = (f["kn"] * f["kds"]).astype(BF16)
            qk_ref[cc, h] = f["qk"].astype(BF16)
            cds[cc] = cds[cc] + jnp.where(lane == h, f["cd"], 0.0)
        for cc in range(CPS):
            cd_ref[cc] = cds[cc]

    col = lambda j: pl.BlockSpec((TG, GW), lambda i: (i, j))
    halo = lambda j: pl.BlockSpec((SH, GW), lambda i: (jnp.maximum(i * (TG // SH) - 1, 0), j))
    tile = lambda: pl.BlockSpec((TG, GW), lambda i: (i, 0))
    sq = lambda: pl.BlockSpec((CPS, NH, CL, CL), lambda i: (i, 0, 0, 0))
    return pl.pallas_call(
        body, name="gdn_prep", grid=(NCH // CPS,),
        in_specs=[col(2), col(3), col(4), halo(2), halo(3), halo(4), pl.BlockSpec((TG, LANES), lambda i: (i, 0)),
                  _const((KS, 3 * GW)), _const((1, LANES)), _const((1, LANES))],
        out_specs=(tile(), tile(), tile(), tile(), sq(), pl.BlockSpec((CPS, 8, LANES), lambda i: (i, 0, 0)), sq()),
        out_shape=(jax.ShapeDtypeStruct((S, GW), BF16), jax.ShapeDtypeStruct((S, GW), F32),
                   jax.ShapeDtypeStruct((S, GW), BF16), jax.ShapeDtypeStruct((S, GW), BF16),
                   jax.ShapeDtypeStruct((NCH, NH, CL, CL), BF16), jax.ShapeDtypeStruct((NCH, 8, LANES), F32),
                   jax.ShapeDtypeStruct((NCH, NH, CL, CL), F32)),
        scratch_shapes=[pltpu.VMEM((SH + TG, 3 * GW), F32)],
        compiler_params=_params(dimension_semantics=("arbitrary",)),
    )(p_main, p_main, p_main, p_main, p_main, p_main, p_ba, gdn_conv_w, alog_l, dt_l)


def _gdn_scan(w_o, u_o, qg, kd, qk, cd, p_main, gdn_nw):
    def body(w_ref, u_ref, qg_ref, kd_ref, qk_ref, cd_ref, z_ref, nw_ref, ob_ref, o_ref, sin_ref, state):
        n = pl.program_id(0)

        @pl.when(n == 0)
        def _():
            state[...] = jnp.zeros((NH, DH, DH), F32)

        def head(cc, h):
            rows, lo = pl.ds(cc * CL, CL), h * DH
            st = state[h]
            sin_ref[cc, h] = st
            sb = st.astype(BF16)
            v_new = u_ref[rows, lo:lo + DH] - _dot(w_ref[rows, lo:lo + DH], sb)
            yield
            vb = v_new.astype(BF16)
            o = _dot(qg_ref[rows, lo:lo + DH], sb) + _dot(qk_ref[cc, h], vb)
            state[h] = st * cd_ref[cc, 0:1, h:h + 1] + _dot_tn(kd_ref[rows, lo:lo + DH], vb)
            yield
            o_ref[rows, lo:lo + DH] = o
            r = lax.rsqrt(jnp.mean(o * o, axis=-1, keepdims=True) + EPS)
            zh = z_ref[rows, lo:lo + DH]
            ob_ref[rows, lo:lo + DH] = o * r * nw_ref[...] * (zh * _sig(zh))

        for cc in range(CPS):
            _lockstep(head(cc, h) for h in range(NH))

    tile = lambda: pl.BlockSpec((TG, GW), lambda n: (n, 0))
    return pl.pallas_call(
        body, name="gdn_scan", grid=(NCH // CPS,),
        in_specs=[tile(), tile(), tile(), tile(), pl.BlockSpec((CPS, NH, CL, CL), lambda n: (n, 0, 0, 0)),
                  pl.BlockSpec((CPS, 8, LANES), lambda n: (n, 0, 0)), pl.BlockSpec((TG, GW), lambda n: (n, 5)),
                  _const((1, DH))],
        out_specs=(tile(), tile(), pl.BlockSpec((CPS, NH, DH, DH), lambda n: (n, 0, 0, 0))),
        out_shape=(jax.ShapeDtypeStruct((S, GW), F32), jax.ShapeDtypeStruct((S, GW), F32),
                   jax.ShapeDtypeStruct((NCH, NH, DH, DH), F32)),
        scratch_shapes=[pltpu.VMEM((NH, DH, DH), F32)],
        compiler_params=_params(dimension_semantics=("arbitrary",)),
    )(w_o, u_o, qg, kd, qk, cd, p_main, gdn_nw)


def _fwd_out(out_a, out_b, x, modnb, bada, w_out):
    def body(oa_ref, ob_ref, x_ref, mod_ref, b_ref, w_ref, x1_ref, mix_ref, oab_ref):
        oa = oa_ref[...].astype(BF16)
        ob = ob_ref[...].astype(BF16)
        oab_ref[:, 0:CW] = oa
        oab_ref[:, CW:D] = ob
        mix = _dot(oa, w_ref[0:CW, :]) + _dot(ob, w_ref[CW:D, :])
        mix_ref[...] = mix
        x1_ref[...] = x_ref[...] + _mod(mod_ref, b_ref, 2) * mix

    tile = lambda w: pl.BlockSpec((TM, w), lambda i: (i, 0))
    return pl.pallas_call(
        body, name="fwd_out", grid=(NT,),
        in_specs=[tile(CW), tile(GW), tile(D), _const((1, 6 * D)), _const((1, 6 * D)), _const((D, D))],
        out_specs=(tile(D), tile(D), tile(D)),
        out_shape=(jax.ShapeDtypeStruct((S, D), F32), jax.ShapeDtypeStruct((S, D), F32),
                   jax.ShapeDtypeStruct((S, D), BF16)),
        compiler_params=_params(dimension_semantics=("arbitrary",)),
    )(out_a, out_b, x, modnb, bada, w_out)


FFN_STATS = 8


def _ffn_forward(x1, tgt, modnb, bada, nw2, nfw, w_fi, w_fo):
    def body(x1_ref, tgt_ref, mod_ref, b_ref, nw2_ref, nfw_ref, wi_ref, wo_ref,
             hb_ref, act_ref, pre_ref, dx2_ref, dffn_ref, st_ref):
        i = pl.program_id(0)

        @pl.when(i == 0)
        def _():
            st_ref[...] = jnp.zeros((FFN_STATS, D), F32)

        sh2, sc2, gt2 = _mod(mod_ref, b_ref, 3), _mod(mod_ref, b_ref, 4), _mod(mod_ref, b_ref, 5)
        x1v = x1_ref[...]
        r2 = lax.rsqrt(jnp.mean(x1v * x1v, axis=-1, keepdims=True) + EPS)
        hb = ((x1v * r2 * nw2_ref[...]) * (1.0 + sc2) + sh2).astype(BF16)
        hb_ref[...] = hb
        ffn = jnp.zeros((TM, D), F32)
        for j in range(4):
            fgj = _dot_nt(hb, wi_ref[j])
            fuj = _dot_nt(hb, wi_ref[j + 4])
            pre_ref[j] = fgj.astype(BF16)
            pre_ref[j + 4] = fuj.astype(BF16)
            aj = (fgj * _sig(fgj) * fuj).astype(BF16)
            act_ref[j] = aj
            ffn = ffn + _dot(aj, wo_ref[j])
        x2 = x1v + gt2 * ffn
        r3 = lax.rsqrt(jnp.mean(x2 * x2, axis=-1, keepdims=True) + EPS)
        xr3 = x2 * r3
        err = xr3 * nfw_ref[...] - tgt_ref[...]
        loss = 0.5 * jnp.sum(jnp.mean(err * err, axis=-1, keepdims=True), axis=0, keepdims=True)
        dy = err * (1.0 / D)
        st_ref[0:1, :] += _colsum(dy * xr3)
        dyr = dy * nfw_ref[...]
        dx2 = r3 * (dyr - xr3 * jnp.mean(dyr * xr3, axis=-1, keepdims=True))
        st_ref[1:2, :] += _colsum(dx2 * ffn)
        st_ref[5:6, :] += jnp.broadcast_to(loss, (1, D))
        dx2_ref[...] = dx2
        dffn_ref[...] = (gt2 * dx2).astype(BF16)

    tile = lambda w: pl.BlockSpec((TM, w), lambda i: (i, 0))
    return pl.pallas_call(
        body, name="ffn_forward", grid=(NT,),
        in_specs=[tile(D), tile(D), _const((1, 6 * D)), _const((1, 6 * D)), _const((1, D)), _const((1, D)),
                  _const1((N_DEV, FB, D)), _const1((4, FB, D))],
        out_specs=(tile(D), pl.BlockSpec((4, TM, FB), lambda i: (0, i, 0)),
                   pl.BlockSpec((N_DEV, TM, FB), lambda i: (0, i, 0)), tile(D), tile(D), _const((FFN_STATS, D))),
        out_shape=(jax.ShapeDtypeStruct((S, D), BF16), jax.ShapeDtypeStruct((4, S, FB), BF16),
                   jax.ShapeDtypeStruct((N_DEV, S, FB), BF16), jax.ShapeDtypeStruct((S, D), F32),
                   jax.ShapeDtypeStruct((S, D), BF16), jax.ShapeDtypeStruct((FFN_STATS, D), F32)),
        compiler_params=_params(42, dimension_semantics=("arbitrary",)),
    )(x1, tgt, modnb, bada, nw2, nfw, w_fi, w_fo)


def _ffn_backward(dffn, pre, x1, dx2, modnb, bada, nw2, w_fi, w_fo):
    def body(dffn_ref, pre_ref, x1_ref, dx2_ref, mod_ref, b_ref, nw2_ref, wi_ref, wo_ref, df_ref, dx1_ref, st_ref):
        i = pl.program_id(0)

        @pl.when(i == 0)
        def _():
            st_ref[...] = jnp.zeros((FFN_STATS, D), F32)

        dffn = dffn_ref[...]
        dh = jnp.zeros((TM, D), F32)
        for j in range(4):
            fg = pre_ref[j].astype(F32)
            fu = pre_ref[j + 4].astype(F32)
            sg = _sig(fg)
            dact = _dot_nt(dffn, wo_ref[j])
            dfg = (dact * fu * (sg * (1.0 + fg * (1.0 - sg)))).astype(BF16)
            dfu = (dact * (fg * sg)).astype(BF16)
            df_ref[j] = dfg
            df_ref[j + 4] = dfu
            dh = dh + _dot(dfg, wi_ref[j]) + _dot(dfu, wi_ref[j + 4])
        x1v = x1_ref[...]
        r2 = lax.rsqrt(jnp.mean(x1v * x1v, axis=-1, keepdims=True) + EPS)
        xr2 = x1v * r2
        st_ref[2:3, :] += _colsum(dh)
        st_ref[3:4, :] += _colsum(dh * (xr2 * nw2_ref[...]))
        dxn = dh * (1.0 + _mod(mod_ref, b_ref, 4))
        st_ref[4:5, :] += _colsum(dxn * xr2)
        dxr = dxn * nw2_ref[...]
        dx1_ref[...] = dx2_ref[...] + r2 * (dxr - xr2 * jnp.mean(dxr * xr2, axis=-1, keepdims=True))

    tile = lambda w: pl.BlockSpec((TM, w), lambda i: (i, 0))
    wide = lambda: pl.BlockSpec((N_DEV, TM, FB), lambda i: (0, i, 0))
    return pl.pallas_call(
        body, name="ffn_backward", grid=(NT,),
        in_specs=[tile(D), wide(), tile(D), tile(D), _const((1, 6 * D)), _const((1, 6 * D)), _const((1, D)),
                  _const1((N_DEV, FB, D)), _const1((4, FB, D))],
        out_specs=(wide(), tile(D), _const((FFN_STATS, D))),
        out_shape=(jax.ShapeDtypeStruct((N_DEV, S, FB), BF16), jax.ShapeDtypeStruct((S, D), F32),
                   jax.ShapeDtypeStruct((FFN_STATS, D), F32)),
        compiler_params=_params(44, dimension_semantics=("arbitrary",)),
    )(dffn, pre, x1, dx2, modnb, bada, nw2, w_fi, w_fo)


def _grad_w(name, a, b, nb):
    m, n = a.shape[1], b.shape[1]

    def body(a_ref, b_ref, o_ref):
        o_ref[...] = _dot_tn(a_ref[...], b_ref[...]).astype(BF16)

    return pl.pallas_call(
        body, name=name, grid=(m // nb,),
        in_specs=[pl.BlockSpec((S, nb), lambda j: (0, j)), _const((S, n))],
        out_specs=pl.BlockSpec((nb, n), lambda j: (j, 0)),
        out_shape=jax.ShapeDtypeStruct((m, n), BF16),
        compiler_params=_params(dimension_semantics=("arbitrary",)),
    )(a, b)


GW_IN_ROWS = NMAIN + LANES


def _grad_w_in(dp_conf, dp_gdn, dp_ba, hb1):
    nb = 512
    n_conf, n_gdn = 2 * CW // nb, 4 * GW // nb

    def body(c_ref, g_ref, ba_ref, h_ref, o_ref):
        j = pl.program_id(0)

        @pl.when(j < n_conf)
        def _():
            o_ref[...] = _dot_tn(c_ref[...], h_ref[...]).astype(BF16)

        @pl.when((j >= n_conf) & (j < n_conf + n_gdn))
        def _():
            o_ref[...] = _dot_tn(g_ref[...], h_ref[...]).astype(BF16)

        @pl.when(j == n_conf + n_gdn)
        def _():
            o_ref[0:LANES, :] = _dot_tn(ba_ref[...], h_ref[...]).astype(BF16)

    return pl.pallas_call(
        body, name="grad_w_in", grid=(n_conf + n_gdn + 1,),
        in_specs=[pl.BlockSpec((S, nb), lambda j: (0, jnp.minimum(j, n_conf - 1))),
                  pl.BlockSpec((S, nb), lambda j: (0, jnp.clip(j - n_conf, 0, n_gdn - 1))),
                  _const((S, LANES)), _const((S, D))],
        out_specs=pl.BlockSpec((nb, D), lambda j: (j, 0)),
        out_shape=jax.ShapeDtypeStruct((GW_IN_ROWS, D), BF16),
        compiler_params=_params(dimension_semantics=("arbitrary",)),
    )(dp_conf, dp_gdn, dp_ba, hb1)


def _grad_w_ffn_in(hb2, df):
    def body(a_ref, b_ref, o_ref):
        o_ref[0] = _dot_tn(b_ref[0], a_ref[...]).astype(BF16)

    return pl.pallas_call(
        body, name="grad_w_ffn_in", grid=(N_DEV,),
        in_specs=[_const((S, D)), pl.BlockSpec((1, S, FB), lambda j: (j, 0, 0))],
        out_specs=pl.BlockSpec((1, FB, D), lambda j: (j, 0, 0)),
        out_shape=jax.ShapeDtypeStruct((N_DEV, FB, D), BF16),
        compiler_params=_params(dimension_semantics=("arbitrary",)),
    )(hb2, df)


def _grad_w_ffn_out(act, dffn):
    def body(a_ref, b_ref, o_ref):
        o_ref[0] = _dot_tn(a_ref[0], b_ref[...]).astype(BF16)

    return pl.pallas_call(
        body, name="grad_w_ffn_out", grid=(4,),
        in_specs=[pl.BlockSpec((1, S, FB), lambda j: (j, 0, 0)), _const((S, D))],
        out_specs=pl.BlockSpec((1, FB, D), lambda j: (j, 0, 0)),
        out_shape=jax.ShapeDtypeStruct((4, FB, D), BF16),
        compiler_params=_params(dimension_semantics=("arbitrary",)),
    )(act, dffn)


def _bwd_out(dx1, mix, modnb, bada, w_out):
    def body(dx_ref, mix_ref, mod_ref, b_ref, w_ref, dmix_ref, doa_ref, dob_ref, st_ref):
        i = pl.program_id(0)

        @pl.when(i == 0)
        def _():
            st_ref[...] = jnp.zeros((8, D), F32)

        dx = dx_ref[...]
        st_ref[0:1, :] += _colsum(dx * mix_ref[...])
        dmix = (_mod(mod_ref, b_ref, 2) * dx).astype(BF16)
        dmix_ref[...] = dmix
        doa_ref[...] = _dot_nt(dmix, w_ref[0:CW, :])
        dob_ref[...] = _dot_nt(dmix, w_ref[CW:D, :])

    tile = lambda w: pl.BlockSpec((TM, w), lambda i: (i, 0))
    return pl.pallas_call(
        body, name="bwd_out", grid=(NT,),
        in_specs=[tile(D), tile(D), _const((1, 6 * D)), _const((1, 6 * D)), _const((D, D))],
        out_specs=(tile(D), tile(CW), tile(GW), _const((8, D))),
        out_shape=(jax.ShapeDtypeStruct((S, D), BF16), jax.ShapeDtypeStruct((S, CW), F32),
                   jax.ShapeDtypeStruct((S, GW), F32), jax.ShapeDtypeStruct((8, D), F32)),
        compiler_params=_params(dimension_semantics=("arbitrary",)),
    )(dx1, mix, modnb, bada, w_out)


CONF_STATS = 40


def _conf_bwd(d_out_a, y, p_main, conv_w, gn_w, gn_b):
    def body(do_ref, y_ref, a_ref, g_ref, ah_ref, gh_ref, w_ref, gw_ref, gb_ref, dp_ref, st_ref,
             ubuf, dybuf, ush, dysh):
        i = pl.program_id(0)

        @pl.when(i == 0)
        def _():
            st_ref[...] = jnp.zeros((CONF_STATS, CW), F32)
            dybuf[TM:TM + HALO, :] = jnp.zeros((HALO, CW), F32)

        pm = _group_mean_matrix().astype(BF16)
        yv = y_ref[...]
        dlt = yv - _group_mean(yv, pm)
        rstd = lax.rsqrt(_group_mean(dlt * dlt, pm) + EPS)
        un = dlt * rstd
        o = un * gw_ref[...] + gb_ref[...]
        so = _sig(o)
        d_o = do_ref[...] * (so * (1.0 + o * (1.0 - so)))
        st_ref[33:34, :] += _colsum(d_o)
        st_ref[32:33, :] += _colsum(d_o * un)
        dun = d_o * gw_ref[...]
        dy = rstd * (dun - _group_mean(dun, pm) - un * _group_mean(dun * un, pm))
        st_ref[31:32, :] += _colsum(dy)
        dybuf[0:TM, :] = dy
        _fill_shifted(dybuf, dysh)

        a = a_ref[...]
        sg = _sig(g_ref[...])
        first = i == NT - 1
        ubuf[0:HALO, :] = jnp.where(first, 0.0, ah_ref[...] * _sig(gh_ref[...]))
        ubuf[HALO:HALO + TM, :] = a * sg
        _fill_shifted(ubuf, ush)
        du = jnp.zeros((TM, CW), F32)
        for k in range(KC):
            st_ref[k:k + 1, :] += _colsum(dy * _rows_at(ubuf, ush, HALO - (KC - 1) + k))
            du = du + w_ref[k:k + 1, :] * _rows_at(dybuf, dysh, KC - 1 - k)
        dybuf[TM:TM + HALO, :] = dybuf[0:HALO, :]
        dp_ref[:, 0:CW] = (du * sg).astype(BF16)
        dp_ref[:, CW:2 * CW] = (du * a * sg * (1.0 - sg)).astype(BF16)

    rev = lambda w, j=0: pl.BlockSpec((TM, w), lambda i: (NT - 1 - i, j))
    halo = lambda j: pl.BlockSpec((HALO, CW), lambda i: (jnp.maximum((NT - 1 - i) * (TM // HALO) - 1, 0), j))
    return pl.pallas_call(
        body, name="conf_bwd", grid=(NT,),
        in_specs=[rev(CW), rev(CW), rev(CW, 0), rev(CW, 1), halo(0), halo(1),
                  _const((KC, CW)), _const((1, CW)), _const((1, CW))],
        out_specs=(rev(2 * CW), _const((CONF_STATS, CW))),
        out_shape=(jax.ShapeDtypeStruct((S, 2 * CW), BF16), jax.ShapeDtypeStruct((CONF_STATS, CW), F32)),
        scratch_shapes=[pltpu.VMEM((HALO + TM, CW), F32), pltpu.VMEM((TM + HALO, CW), F32),
                        pltpu.VMEM((SUB - 1, SHIFT_ROWS, CW), F32), pltpu.VMEM((SUB - 1, SHIFT_ROWS, CW), F32)],
        compiler_params=_params(dimension_semantics=("arbitrary",)),
    )(d_out_a, y, p_main, p_main, p_main, p_main, conv_w, gn_w, gn_b)


GDN_STATS = 8


def _gdn_bwd(d_out_b, o_pre, s_in, t_inv, p_main, p_ba, gdn_conv_w, alog_l, dt_l, gdn_nw):
    def body(dob_ref, o_ref, sin_ref, t_ref, q_ref, k_ref, v_ref, z_ref, qh_ref, kh_ref, vh_ref, ba_ref,
             w_ref, al_ref, dt_ref, nw_ref, dp_ref, dba_ref, st_ref, xbuf, dcbuf, dstate):
        n = pl.program_id(0)

        @pl.when(n == 0)
        def _():
            st_ref[...] = jnp.zeros((GDN_STATS, 3 * GW), F32)
            dcbuf[CL:CL + SH, :] = jnp.zeros((SH, 3 * GW), F32)
            dstate[...] = jnp.zeros((NH, DH, DH), F32)

        for cc in reversed(range(CPS)):
            chunk(n, cc, dob_ref, o_ref, sin_ref, t_ref, q_ref, k_ref, v_ref, z_ref, qh_ref, kh_ref, vh_ref, ba_ref,
                  w_ref, al_ref, dt_ref, nw_ref, dp_ref, dba_ref, st_ref, xbuf, dcbuf, dstate)

    def chunk(n, cc, dob_ref, o_ref, sin_ref, t_ref, q_ref, k_ref, v_ref, z_ref, qh_ref, kh_ref, vh_ref, ba_ref,
              w_ref, al_ref, dt_ref, nw_ref, dp_ref, dba_ref, st_ref, xbuf, dcbuf, dstate):
        r0 = cc * CL
        if cc == 0:
            first = n == NCH // CPS - 1
            xbuf[0:SH, 0:GW] = jnp.where(first, 0.0, qh_ref[...])
            xbuf[0:SH, GW:2 * GW] = jnp.where(first, 0.0, kh_ref[...])
            xbuf[0:SH, 2 * GW:3 * GW] = jnp.where(first, 0.0, vh_ref[...])
        else:
            xbuf[0:SH, 0:GW] = q_ref[r0 - SH:r0, :]
            xbuf[0:SH, GW:2 * GW] = k_ref[r0 - SH:r0, :]
            xbuf[0:SH, 2 * GW:3 * GW] = v_ref[r0 - SH:r0, :]
        xbuf[SH:SH + CL, 0:GW] = q_ref[r0:r0 + CL, :]
        xbuf[SH:SH + CL, GW:2 * GW] = k_ref[r0:r0 + CL, :]
        xbuf[SH:SH + CL, 2 * GW:3 * GW] = v_ref[r0:r0 + CL, :]
        conv = _short_conv(w_ref, xbuf)
        sc = _sig(conv)
        qkv = conv * sc
        ba = ba_ref[r0:r0 + CL, :]
        beta_all, g_all, xg, neg_a = _gdn_gates(ba, al_ref[...], dt_ref[...])
        gcum, gcum_t = _gdn_cumsum(g_all)
        lane = lax.broadcasted_iota(jnp.int32, (CL, LANES), 1)
        row = lax.broadcasted_iota(jnp.int32, (CL, 1), 0)
        acc = dict(dgcum=jnp.zeros((CL, LANES), F32), dbeta=jnp.zeros((CL, LANES), F32))

        def head(h):
            lo = h * DH
            qh = qkv[:, lo:lo + DH]
            kh = qkv[:, GW + lo:GW + lo + DH]
            vh = qkv[:, 2 * GW + lo:2 * GW + lo + DH]
            beta = beta_all[:, h:h + 1]
            f = _head_terms(qh, kh, beta, gcum[:, NH + h:NH + h + 1], gcum_t[NH + h:NH + h + 1, :])
            qn, kn, qs, kb, gam, kds, cd, decay = (f[s] for s in ("qn", "kn", "qs", "kb", "gam", "kds", "cd", "decay"))
            t = t_ref[cc, h]
            st = sin_ref[cc, h]
            vb = vh * beta
            kbg = kb * gam
            u = _dot(t, vb, GP)
            w = _dot(t, kbg, GP)
            yield
            v_new = u - _dot(w, st, GP)
            q_dec = qs * gam
            k_dec = kn * kds

            o = o_ref[r0:r0 + CL, lo:lo + DH]
            zh = z_ref[r0:r0 + CL, lo:lo + DH]
            sz = _sig(zh)
            r = lax.rsqrt(jnp.mean(o * o, axis=-1, keepdims=True) + EPS)
            orr = o * r
            d_out = dob_ref[r0:r0 + CL, lo:lo + DH]
            dz = d_out * (orr * nw_ref[...]) * (sz * (1.0 + zh * (1.0 - sz)))
            don = d_out * (zh * sz)
            st_ref[4:5, 0:DH] += _colsum(don * orr)
            tt = don * nw_ref[...]
            d_o = r * (tt - orr * jnp.mean(tt * orr, axis=-1, keepdims=True))

            yield
            ds_out = dstate[h]
            dv_new = _dot_tn(f["qk"], d_o, GP) + _dot(k_dec, ds_out, GP)
            dqk = jnp.where(f["causal"], _dot_nt(d_o, v_new, GP), 0.0)
            dq_dec = _dot_nt(d_o, st, GP)
            dk_dec = _dot_nt(v_new, ds_out, GP)
            yield
            dstate[h] = _dot_tn(q_dec, d_o, GP) + cd * ds_out - _dot_tn(w, dv_new, GP)
            dcd = jnp.sum(_rowsum(st * ds_out), axis=0, keepdims=True)
            dw = -_dot_nt(dv_new, st, GP)
            dvb = _dot_tn(t, dv_new, GP)
            yield
            dt_m = _dot_nt(dv_new, vb, GP) + _dot_nt(dw, kbg, GP)
            dkbg = _dot_tn(t, dw, GP)
            yield
            dtt = _dot_nt(dt_m, t, GP)
            yield
            da = jnp.where(f["strict"], -_dot_tn(t, dtt, GP), 0.0)
            yield
            dad = da * decay
            dqkd = dqk * decay
            dkb = _dot(dad, kn, GP) + dkbg * gam
            dkn = _dot_tn(dad, kb, GP) + _dot_tn(dqkd, qs, GP) + dk_dec * kds + dkb * beta
            dqs = _dot(dqkd, kn, GP) + dq_dec * gam
            yield
            m = da * f["a"] + dqk * f["qk"]
            tk = _rowsum(dk_dec * k_dec)
            dgl = jnp.sum(tk, axis=0, keepdims=True) + dcd * cd
            dgc = (_rowsum(m) - _rowsum(jnp.transpose(m)) + _rowsum(dq_dec * q_dec) - tk + _rowsum(dkbg * kbg)
                   + jnp.where(row == CL - 1, dgl, 0.0))
            dbeta = _rowsum(dkb * kn) + _rowsum(dvb * vh)
            acc["dgcum"] = acc["dgcum"] + jnp.where(lane == NH + h, dgc, 0.0)
            acc["dbeta"] = acc["dbeta"] + jnp.where(lane == h, dbeta, 0.0)
            dvh = dvb * beta
            dqn = dqs * QSCALE
            dqh = f["rq"] * (dqn - qn * _rowsum(dqn * qn))
            dkh = f["rk"] * (dkn - kn * _rowsum(dkn * kn))
            dsilu = lambda c0: sc[:, c0:c0 + DH] * (1.0 + conv[:, c0:c0 + DH] * (1.0 - sc[:, c0:c0 + DH]))
            dcbuf[0:CL, lo:lo + DH] = dqh * dsilu(lo)
            dcbuf[0:CL, GW + lo:GW + lo + DH] = dkh * dsilu(GW + lo)
            dcbuf[0:CL, 2 * GW + lo:2 * GW + lo + DH] = dvh * dsilu(2 * GW + lo)
            dp_ref[r0:r0 + CL, 3 * GW + lo:3 * GW + lo + DH] = dz.astype(BF16)

        _lockstep(head(h) for h in range(NH))
        dgcum_all, dbeta_all = acc["dgcum"], acc["dbeta"]

        ii, jj = _tri_iota()
        upper = jnp.where(ii <= jj, 1.0, 0.0).astype(BF16)
        dg_all = _ones_dot(upper, dgcum_all)
        dxg = dg_all * neg_a * _sig(xg)
        st_ref[5:6, 0:LANES] += _colsum(dg_all * g_all)
        st_ref[6:7, 0:LANES] += _colsum(dxg)
        dbl = dbeta_all * beta_all * (1.0 - beta_all)
        dba_ref[r0:r0 + CL, :] = jnp.where(lane < NH, dbl, jnp.where(lane < 2 * NH, dxg, 0.0)).astype(BF16)

        dconv = dcbuf[0:CL, :]
        dx = w_ref[0:1, :] * dcbuf[KS - 1:KS - 1 + CL, :]
        st_ref[0:1, :] += _colsum(dconv * xbuf[SH - KS + 1:SH - KS + 1 + CL, :])
        for k in range(1, KS):
            off = SH - (KS - 1) + k
            st_ref[k:k + 1, :] += _colsum(dconv * xbuf[off:off + CL, :])
            dx = dx + w_ref[k:k + 1, :] * dcbuf[KS - 1 - k:KS - 1 - k + CL, :]
        dcbuf[CL:CL + SH, :] = dcbuf[0:SH, :]
        dp_ref[r0:r0 + CL, 0:3 * GW] = dx.astype(BF16)

    steps = NCH // CPS
    rev = lambda w, j=0: pl.BlockSpec((TG, w), lambda n: (steps - 1 - n, j))
    halo = lambda j: pl.BlockSpec((SH, GW), lambda n: (jnp.maximum((steps - 1 - n) * (TG // SH) - 1, 0), j))
    blk4 = lambda a, b: pl.BlockSpec((CPS, NH, a, b), lambda n: (steps - 1 - n, 0, 0, 0))
    return pl.pallas_call(
        body, name="gdn_bwd", grid=(steps,),
        in_specs=[rev(GW), rev(GW), blk4(DH, DH), blk4(CL, CL), rev(GW, 2), rev(GW, 3), rev(GW, 4), rev(GW, 5),
                  halo(2), halo(3), halo(4), rev(LANES), _const((KS, 3 * GW)), _const((1, LANES)),
                  _const((1, LANES)), _const((1, DH))],
        out_specs=(rev(4 * GW), rev(LANES), _const((GDN_STATS, 3 * GW))),
        out_shape=(jax.ShapeDtypeStruct((S, 4 * GW), BF16), jax.ShapeDtypeStruct((S, LANES), BF16),
                   jax.ShapeDtypeStruct((GDN_STATS, 3 * GW), F32)),
        scratch_shapes=[pltpu.VMEM((SH + CL, 3 * GW), F32), pltpu.VMEM((CL + SH, 3 * GW), F32),
                        pltpu.VMEM((NH, DH, DH), F32)],
        compiler_params=_params(dimension_semantics=("arbitrary",)),
    )(d_out_b, o_pre, s_in, t_inv, p_main, p_main, p_main, p_main, p_main, p_main, p_main, p_ba,
      gdn_conv_w, alog_l, dt_l, gdn_nw)


def _bwd_in(dp_conf, dp_gdn, dp_ba, x, dx1, nw1, modnb, bada, w_main, w_ba):
    def body(dc_ref, dg_ref, db_ref, x_ref, dx1_ref, nw_ref, mod_ref, b_ref, wm_ref, wb_ref, gx_ref, st_ref):
        i = pl.program_id(0)

        @pl.when(i == 0)
        def _():
            st_ref[...] = jnp.zeros((8, D), F32)

        dh = (_dot(dc_ref[...], wm_ref[0:2 * CW, :]) + _dot(dg_ref[...], wm_ref[2 * CW:NMAIN, :])
              + _dot(db_ref[...], wb_ref[...]))
        xv = x_ref[...]
        r = lax.rsqrt(jnp.mean(xv * xv, axis=-1, keepdims=True) + EPS)
        xr = xv * r
        st_ref[0:1, :] += _colsum(dh)
        st_ref[1:2, :] += _colsum(dh * (xr * nw_ref[...]))
        dxn = dh * (1.0 + _mod(mod_ref, b_ref, 1))
        st_ref[2:3, :] += _colsum(dxn * xr)
        dxr = dxn * nw_ref[...]
        gx_ref[...] = dx1_ref[...] + r * (dxr - xr * jnp.mean(dxr * xr, axis=-1, keepdims=True))

    tile = lambda w: pl.BlockSpec((TM, w), lambda i: (i, 0))
    return pl.pallas_call(
        body, name="bwd_in", grid=(NT,),
        in_specs=[tile(2 * CW), tile(4 * GW), tile(LANES), tile(D), tile(D), _const((1, D)), _const((1, 6 * D)),
                  _const((1, 6 * D)), _const((NMAIN, D)), _const((LANES, D))],
        out_specs=(tile(D), _const((8, D))),
        out_shape=(jax.ShapeDtypeStruct((S, D), F32), jax.ShapeDtypeStruct((8, D), F32)),
        compiler_params=_params(dimension_semantics=("arbitrary",)),
    )(dp_conf, dp_gdn, dp_ba, x, dx1, nw1, modnb, bada, w_main, w_ba)


def _adamw(w, g, m, v):
    m = ADAM_B1 * m + (1.0 - ADAM_B1) * g
    v = ADAM_B2 * v + (1.0 - ADAM_B2) * (g * g)
    m_hat = m / BC1
    v_hat = v / BC2
    delta = -ADAM_LR * (m_hat / (jnp.sqrt(v_hat) + ADAM_EPS) + ADAM_WD * w)
    return delta, m, v


ADAM_BLOCK_BYTES = 6 * 1024 * 1024


def _adam_tile(rows, cols):
    padded = -(-cols // LANES) * LANES
    if N_DEV * rows * padded * 4 <= ADAM_BLOCK_BYTES:
        return rows, cols
    best = None
    for tr in range(16, rows, 16):
        if rows % tr == 0 and N_DEV * tr * padded * 4 <= ADAM_BLOCK_BYTES:
            best = tr
    if best is not None:
        return best, cols
    rows_padded = -(-rows // 16) * 16
    tc = LANES
    for cand in range(LANES, cols, LANES):
        if cols % cand == 0 and N_DEV * rows_padded * cand * 4 <= ADAM_BLOCK_BYTES:
            tc = cand
    return rows, tc


def _reduce_adam(name, parts, w, m, v, own=None):
    rows, cols = w.shape
    tr, tc = _adam_tile(rows, cols)

    def body(*refs):
        p_ref, w_ref, m_ref, v_ref = refs[:4]
        g_ref, d_ref, nm_ref, nv_ref = refs[-4:]
        if own is None:
            part = lambda j: p_ref[j].astype(F32)
        else:
            me = 4 * lax.axis_index("x") + 2 * lax.axis_index("y") + lax.axis_index("c")
            part = lambda j: jnp.where(me == j, refs[4][...], p_ref[j]).astype(F32)
        g = part(0)
        for j in range(1, N_DEV):
            g = g + part(j)
        g_ref[...] = g
        d_ref[...], nm_ref[...], nv_ref[...] = _adamw(w_ref[...], g, m_ref[...], v_ref[...])

    blk = pl.BlockSpec((tr, tc), lambda i, j: (i, j))
    sds = jax.ShapeDtypeStruct((rows, cols), F32)
    extra = [] if own is None else [own]
    return pl.pallas_call(
        body, name=name, grid=(rows // tr, cols // tc),
        in_specs=[pl.BlockSpec((N_DEV, tr, tc), lambda i, j: (0, i, j)), blk, blk, blk] + [blk] * len(extra),
        out_specs=(blk, blk, blk, blk), out_shape=(sds, sds, sds, sds),
        compiler_params=_params(dimension_semantics=("arbitrary", "arbitrary")),
    )(parts, w, m, v, *extra)


def _ada_adam(c_all, dmod_sh, w, m, v):
    rows, cols = w.shape
    tr = 256

    def body(c_ref, dm_ref, w_ref, m_ref, v_ref, g_ref, d_ref, nm_ref, nv_ref):
        cv = c_ref[...]
        g = _dot_tn(cv * _sig(cv), dm_ref[...], HI)
        g_ref[...] = g
        d_ref[...], nm_ref[...], nv_ref[...] = _adamw(w_ref[...], g, m_ref[...], v_ref[...])

    blk = pl.BlockSpec((tr, cols), lambda i: (i, 0))
    sds = jax.ShapeDtypeStruct((rows, cols), F32)
    return pl.pallas_call(
        body, name="ada_adam", grid=(rows // tr,),
        in_specs=[pl.BlockSpec((N_DEV, tr), lambda i: (0, i)), _const((N_DEV, cols)), blk, blk, blk],
        out_specs=(blk, blk, blk, blk), out_shape=(sds, sds, sds, sds),
        compiler_params=_params(dimension_semantics=("arbitrary",)),
    )(c_all, dmod_sh, w, m, v)


def _lanes(a, at=0):
    return jnp.pad(a, ((0, 0), (at, LANES - at - a.shape[1])))


WEIGHT_NAMES = ["w_ada", "b_ada", "norm_mix_w", "w_in", "conv_w", "conv_b", "conv_gn_w", "conv_gn_b", "gdn_conv_w",
                "gdn_a_log", "gdn_dt_bias", "gdn_norm_w", "w_out", "norm_ffn_w", "w_ffn_in", "w_ffn_out",
                "norm_final_w"]


SMALL_LAYOUT = [("b_ada", 0, 48, LANES), ("norm_mix_w", 48, 8, LANES), ("norm_ffn_w", 56, 8, LANES),
                ("norm_final_w", 64, 8, LANES), ("conv_b", 72, 4, LANES), ("conv_gn_w", 76, 4, LANES),
                ("conv_gn_b", 80, 4, LANES), ("gdn_norm_w", 84, 1, LANES), ("gdn_a_log", 85, 1, NH),
                ("gdn_dt_bias", 86, 1, NH)]
LOSS_ROW = 87


def _adam_small(g_small, weights, m1, m2):
    names = [nm for nm, _, _, _ in SMALL_LAYOUT]
    k = len(names)

    def body(*refs):
        g_ref = refs[0]
        w_refs, m_refs, v_refs = refs[1:1 + k], refs[1 + k:1 + 2 * k], refs[1 + 2 * k:1 + 3 * k]
        loss_ref = refs[1 + 3 * k]
        outs = refs[2 + 3 * k:2 + 7 * k]
        total = refs[-1]
        g = g_ref[0]
        for j in range(1, N_DEV):
            g = g + g_ref[j]
        total[...] = g
        loss_ref[...] = total[LOSS_ROW:LOSS_ROW + 1, :]
        for i, (_, r0, rows, lanes) in enumerate(SMALL_LAYOUT):
            gp = total[r0:r0 + rows, 0:lanes]
            outs[i][...] = gp
            outs[k + i][...], outs[2 * k + i][...], outs[3 * k + i][...] = _adamw(
                w_refs[i][...], gp, m_refs[i][...], v_refs[i][...])

    shapes = [jax.ShapeDtypeStruct((rows, lanes), F32) for _, _, rows, lanes in SMALL_LAYOUT]
    res = pl.pallas_call(
        body, name="adam_small",
        out_shape=tuple([jax.ShapeDtypeStruct((1, LANES), F32)] + shapes * 4),
        scratch_shapes=[pltpu.VMEM((SMALL_ROWS, LANES), F32)],
        compiler_params=_params(),
    )(g_small, *[weights[n] for n in names], *[m1[n] for n in names], *[m2[n] for n in names])
    kinds = [dict(zip(names, res[1 + q * k:1 + (q + 1) * k])) for q in range(4)]
    return res[0], kinds


def _mix_forward(w, xs, modnb, between=None):
    w_main = w["w_in"]
    w_ba = jnp.pad(w["w_in"][NMAIN:], ((0, LANES - 2 * NH), (0, 0)))
    alog_l = _lanes(w["gdn_a_log"], NH)
    dt_l = _lanes(w["gdn_dt_bias"], NH)
    p_main, p_ba, hb1 = _fwd_in(xs, w["norm_mix_w"], modnb, w["b_ada"], w_main, w_ba)
    w_o, u_o, qg, kd, qk, cd, t_inv = _gdn_prep(p_main, p_ba, w["gdn_conv_w"], alog_l, dt_l)
    out_b, o_pre, s_in = _gdn_scan(w_o, u_o, qg, kd, qk, cd, p_main, w["gdn_norm_w"])
    conv_b = w["conv_b"] if between is None else _after(w["conv_b"], between(out_b))
    y_conv, out_a = _conf_fwd(p_main, w["conv_w"], conv_b, w["conv_gn_w"], w["conv_gn_b"])
    return dict(w_main=w_main, w_ba=w_ba, alog_l=alog_l, dt_l=dt_l, p_main=p_main, p_ba=p_ba, hb1=hb1,
                y_conv=y_conv, out_a=out_a, out_b=out_b, o_pre=o_pre, s_in=s_in, t_inv=t_inv)


def _ffn_stage(w, f, xs, tgt, modnb):
    x1, mix, oab = _fwd_out(f["out_a"], f["out_b"], xs, modnb, w["b_ada"], w["w_out"])
    hb2, act, pre, dx2, dffn, st_fwd = _ffn_forward(x1, tgt, modnb, w["b_ada"], w["norm_ffn_w"],
                                                    w["norm_final_w"], w["w_ffn_in"], w["w_ffn_out"])
    gw_ffn_out = _grad_w_ffn_out(act, dffn)
    df, dx1, st_bwd = _ffn_backward(dffn, pre, x1, dx2, modnb, w["b_ada"], w["norm_ffn_w"], w["w_ffn_in"],
                                    w["w_ffn_out"])
    gw_ffn_in = _grad_w_ffn_in(hb2, df)
    return dict(mix=mix, oab=oab, dx1=dx1, st_ffn=st_fwd + st_bwd, gw_ffn_in=gw_ffn_in, gw_ffn_out=gw_ffn_out)


def _out_backward(w, g, modnb):
    dmix, d_out_a, d_out_b, st_out = _bwd_out(g["dx1"], g["mix"], modnb, w["b_ada"], w["w_out"])
    return dict(d_out_a=d_out_a, d_out_b=d_out_b, st_out=st_out, gw_out=_grad_w("grad_w_out", g["oab"], dmix, 512))


def _heads_backward(w, f, a):
    dp_conf, st_conf = _conf_bwd(a["d_out_a"], f["y_conv"], f["p_main"], w["conv_w"], w["conv_gn_w"],
                                 w["conv_gn_b"])
    dp_gdn, dp_ba, st_gdn = _gdn_bwd(a["d_out_b"], f["o_pre"], f["s_in"], f["t_inv"], f["p_main"], f["p_ba"],
                                     w["gdn_conv_w"], f["alog_l"], f["dt_l"], w["gdn_norm_w"])
    gw_in = _grad_w_in(dp_conf, dp_gdn, dp_ba, f["hb1"])[:NIN]
    return dict(dp_conf=dp_conf, dp_gdn=dp_gdn, dp_ba=dp_ba, st_conf=st_conf, st_gdn=st_gdn, gw_in=gw_in,
                gw_conv=st_conf[0:KC], gw_gconv=st_gdn[0:KS])


def _in_backward(w, f, g, a, h, xs, modnb):
    st_out, st_conf, st_gdn, st_ffn = a["st_out"], h["st_conf"], h["st_gdn"], g["st_ffn"]
    grad_x, st_in = _bwd_in(h["dp_conf"], h["dp_gdn"], h["dp_ba"], xs, g["dx1"], w["norm_mix_w"], modnb,
                            w["b_ada"], f["w_main"], f["w_ba"])
    dmod = jnp.concatenate([st_in[0:1], st_in[1:2], st_out[0:1], st_ffn[2:3], st_ffn[3:4], st_ffn[1:2]], axis=1)
    small = jnp.concatenate([
        dmod.reshape(48, LANES), st_in[2:3].reshape(8, LANES), st_ffn[4:5].reshape(8, LANES),
        st_ffn[0:1].reshape(8, LANES), st_conf[31:32].reshape(4, LANES), st_conf[32:33].reshape(4, LANES),
        st_conf[33:34].reshape(4, LANES), st_gdn[4:5, 0:LANES],
        _lanes(st_gdn[5:6, NH:2 * NH]), _lanes(st_gdn[6:7, NH:2 * NH]), st_ffn[5:6, 0:LANES]], axis=0)
    return dict(grad_x=grad_x, small=small)


def _local(w, xs, tgt, modnb):
    f = _mix_forward(w, xs, modnb)
    g = _ffn_stage(w, f, xs, tgt, modnb)
    a = _out_backward(w, g, modnb)
    h = _heads_backward(w, f, a)
    b = _in_backward(w, f, g, a, h, xs, modnb)
    return dict(b, gw_in=h["gw_in"], gw_conv=h["gw_conv"], gw_gconv=h["gw_gconv"], gw_out=a["gw_out"],
                gw_ffn_in=g["gw_ffn_in"], gw_ffn_out=g["gw_ffn_out"])


def kernel(x, c, w_ada, b_ada, norm_mix_w, w_in, conv_w, conv_b, conv_gn_w, conv_gn_b, gdn_conv_w, gdn_a_log, gdn_dt_bias, gdn_norm_w, w_out, norm_ffn_w, w_ffn_in, w_ffn_out, norm_final_w, loss_target, m_w_ada, m_b_ada, m_norm_mix_w, m_w_in, m_conv_w, m_conv_b, m_conv_gn_w, m_conv_gn_b, m_gdn_conv_w, m_gdn_a_log, m_gdn_dt_bias, m_gdn_norm_w, m_w_out, m_norm_ffn_w, m_w_ffn_in, m_w_ffn_out, m_norm_final_w, v_w_ada, v_b_ada, v_norm_mix_w, v_w_in, v_conv_w, v_conv_b, v_conv_gn_w, v_conv_gn_b, v_gdn_conv_w, v_gdn_a_log, v_gdn_dt_bias, v_gdn_norm_w, v_w_out, v_norm_ffn_w, v_w_ffn_in, v_w_ffn_out, v_norm_final_w):
    me = 4 * lax.axis_index("x") + 2 * lax.axis_index("y") + lax.axis_index("c")
    xs = x.reshape(S, D)
    tgt = loss_target.reshape(S, D)

    g_c, g_cw, g_gcw = _exchange("gather_cond", [c, conv_w[0], gdn_conv_w[0]], [False] * 3)
    c_all = g_c.reshape(N_DEV, D)
    g_mod, mod_token = _exchange("gather_mod", [_mod_shard(c_all, w_ada[0])], [False], with_token=True)
    modnb = lax.dynamic_index_in_dim(g_mod, me, axis=1, keepdims=False).reshape(1, 6 * D)

    late = [w_out[0].astype(BF16), jnp.transpose(w_ffn_in[0]).astype(BF16), w_ffn_out[0].astype(BF16)]
    g_win, *late_lands = _gather_two_level(
        "gather_weights", [_after(jnp.transpose(w_in[0]), mod_token).astype(BF16)] + late, seed_only=(1, 2, 3))
    late_started = _exchange_start("gather_late_start", late, late_lands, [False] * 3, only=LEVEL_ONE)
    modnb = _after(modnb, late_started[-1])
    w = dict(b_ada=b_ada, norm_mix_w=norm_mix_w, conv_b=conv_b, conv_gn_w=conv_gn_w, conv_gn_b=conv_gn_b,
             gdn_a_log=gdn_a_log, gdn_dt_bias=gdn_dt_bias, gdn_norm_w=gdn_norm_w, norm_ffn_w=norm_ffn_w,
             norm_final_w=norm_final_w.reshape(1, D),
             conv_w=jnp.transpose(g_cw, (1, 0, 2)).reshape(KC, CW),
             gdn_conv_w=jnp.transpose(g_gcw, (1, 0, 2)).reshape(KS, 3 * GW),
             w_in=g_win.reshape(NIN, D))

    relay = {}

    def relay_late(out_b):
        _, late_landed = _exchange_wait("gather_late_wait", late_started, [False] * 3, (out_b,), only=LEVEL_ONE)
        relay["started"] = _relay_start("gather_late_relay_start", late_landed)
        return relay["started"][-1]

    f = _mix_forward(w, xs, modnb, relay_late)
    g_wout, g_wfi, g_wfo = _relay_wait("gather_late_relay_wait", relay["started"], (f["out_a"],))
    w.update(w_out=g_wout.reshape(D, D), w_ffn_in=g_wfi, w_ffn_out=g_wfo.reshape(4, FB, D))
    g = _ffn_stage(w, f, xs, tgt, modnb)

    ffn_grads = [g["gw_ffn_in"], g["gw_ffn_out"].reshape(N_DEV, DFF // N_DEV, D)]
    ffn_started = _exchange_start("scatter_ffn_start", ffn_grads,
                                  [lax.empty(a.shape, a.dtype) for a in ffn_grads], [True] * 2)
    a = _out_backward(w, g, _after(modnb, ffn_started[-1]))
    out_grads = [a["gw_out"].reshape(N_DEV, D // N_DEV, D)]
    out_started = _exchange_start("scatter_out_start", out_grads,
                                  [lax.empty(t.shape, t.dtype) for t in out_grads], [True])
    h = _heads_backward(dict(w, conv_gn_w=_after(w["conv_gn_w"], out_started[-1])), f, a)

    in_grads = [h["gw_in"].reshape(N_DEV, NIN // N_DEV, D),
                jnp.transpose(h["gw_conv"].reshape(KC, N_DEV, CW // N_DEV), (1, 0, 2)),
                jnp.transpose(h["gw_gconv"].reshape(KS, N_DEV, 3 * GW // N_DEV), (1, 0, 2))]
    in_started = _exchange_start("scatter_in_start", in_grads,
                                 [lax.empty(t.shape, t.dtype) for t in in_grads], [True] * 3)
    loc = _in_backward(w, f, g, a, h, xs, _after(modnb, in_started[-1]))
    small_started = _exchange_start("gather_small_start", [loc["small"]],
                                    [lax.empty((N_DEV, SMALL_ROWS, LANES), F32)], [False])

    def own(sent):
        return lax.dynamic_index_in_dim(sent, me, axis=0, keepdims=False)

    big = {}
    (sent_fi, sent_fo), (r_fi, r_fo) = _exchange_wait("scatter_ffn_wait", ffn_started, [True] * 2,
                                                         (small_started[-1],))
    big["w_ffn_in"] = [jnp.transpose(t) for t in _reduce_adam(
        "adam_w_ffn_in", r_fi, jnp.transpose(w_ffn_in[0]), jnp.transpose(m_w_ffn_in[0]),
        jnp.transpose(v_w_ffn_in[0]), own(sent_fi))]
    big["w_ffn_out"] = _reduce_adam("adam_w_ffn_out", r_fo, w_ffn_out[0], m_w_ffn_out[0], v_w_ffn_out[0],
                                    own(sent_fo))
    (sent_out,), (r_out,) = _exchange_wait("scatter_out_wait", out_started, [True], (big["w_ffn_out"][0],))
    big["w_out"] = _reduce_adam("adam_w_out", r_out, w_out[0], m_w_out[0], v_w_out[0], own(sent_out))

    (sent_small,), (r_small,) = _exchange_wait("gather_small_wait", small_started, [False], (big["w_out"][0],))
    slot = lax.broadcasted_iota(jnp.int32, (N_DEV, 1, 1), 0)
    g_small = jnp.where(slot == me, sent_small[None], r_small)
    def views(b_, nm_, nf_, nl_, cb_, gw_, gb_, gn_, al_, dt_):
        arrs = [b_, nm_, nf_, nl_, cb_, gw_, gb_, gn_, al_, dt_]
        return {nm: t.reshape(rows, lanes) for (nm, _, rows, lanes), t in zip(SMALL_LAYOUT, arrs)}

    loss_row, res = _adam_small(
        g_small,
        views(b_ada, norm_mix_w, norm_ffn_w, norm_final_w, conv_b, conv_gn_w, conv_gn_b, gdn_norm_w, gdn_a_log,
              gdn_dt_bias),
        views(m_b_ada, m_norm_mix_w, m_norm_ffn_w, m_norm_final_w, m_conv_b, m_conv_gn_w, m_conv_gn_b,
              m_gdn_norm_w, m_gdn_a_log, m_gdn_dt_bias),
        views(v_b_ada, v_norm_mix_w, v_norm_ffn_w, v_norm_final_w, v_conv_b, v_conv_gn_w, v_conv_gn_b,
              v_gdn_norm_w, v_gdn_a_log, v_gdn_dt_bias))
    loss = loss_row[0, 0]
    small_shapes = dict(b_ada=(1, 6 * D), norm_mix_w=(1, D), norm_ffn_w=(1, D), norm_final_w=(D,),
                        conv_b=(1, CW), conv_gn_w=(1, CW), conv_gn_b=(1, CW), gdn_norm_w=(1, DH),
                        gdn_a_log=(1, NH), gdn_dt_bias=(1, NH))
    res = [{nm: t.reshape(small_shapes[nm]) for nm, t in kind.items()} for kind in res]

    dmod_rows = g_small[:, 0:48, :].reshape(N_DEV, 6 * D)
    dmod_sh = lax.dynamic_slice_in_dim(dmod_rows, me * (6 * D // N_DEV), 6 * D // N_DEV, axis=1)

    big["w_ada"] = _ada_adam(c_all, dmod_sh, w_ada[0], m_w_ada[0], v_w_ada[0])
    (sent_in, sent_cw, sent_gcw), (r_in, r_cw, r_gcw) = _exchange_wait(
        "scatter_in_wait", in_started, [True] * 3, (big["w_ada"][0],))
    big["w_in"] = [jnp.transpose(t) for t in _reduce_adam(
        "adam_w_in", r_in, jnp.transpose(w_in[0]), jnp.transpose(m_w_in[0]), jnp.transpose(v_w_in[0]),
        own(sent_in))]
    big["conv_w"] = _reduce_adam("adam_conv_w", r_cw, conv_w[0], m_conv_w[0], v_conv_w[0], own(sent_cw))
    big["gdn_conv_w"] = _reduce_adam("adam_gdn_conv_w", r_gcw, gdn_conv_w[0], m_gdn_conv_w[0], v_gdn_conv_w[0],
                                     own(sent_gcw))
    outs = [loss, loc["grad_x"].reshape(1, S, D)]
    for kind in range(4):
        for nm in WEIGHT_NAMES:
            outs.append(big[nm][kind][None] if nm in big else res[kind][nm])
    return tuple(outs)
```

```python
import functools

import jax
import jax.numpy as jnp
from jax import lax
from jax.experimental import pallas as pl
from jax.experimental.pallas import tpu as pltpu

F32 = jnp.float32
BF16 = jnp.bfloat16
HI = lax.Precision.HIGHEST
MESH = pl.DeviceIdType.MESH

N_DEV = 8
S = 2048
D = 1024
TM = 256
NT = S // TM
CW = 512
KC = 31
NG = 8
GSZ = CW // NG
HALO = 32
GW = 512
NH = 4
DH = 128
KS = 4
SH = 8
CL = 64
NCH = S // CL
NMAIN = 2 * CW + 4 * GW
NIN = NMAIN + 2 * NH
DFF = 2816
FB = DFF // 4
EPS = 1e-6
QSCALE = DH ** -0.5
LANES = 128
SMALL_ROWS = 88

ADAM_LR = 0.001
ADAM_B1 = 0.9
ADAM_B2 = 0.999
ADAM_EPS = 1e-08
ADAM_WD = 0.01
ADAM_STEP = 10
BC1 = 1.0 - ADAM_B1 ** ADAM_STEP
BC2 = 1.0 - ADAM_B2 ** ADAM_STEP

MIB = 1024 * 1024
VMEM_LIMIT_MIB = 32


def _params(limit_mib=VMEM_LIMIT_MIB, **kw):
    return pltpu.CompilerParams(vmem_limit_bytes=limit_mib * MIB, **kw)


def _sig(x):
    return jax.nn.sigmoid(x)


GP = BF16


def _operands(a, b, prec):
    if prec is BF16:
        return a.astype(BF16), b.astype(BF16), None
    return a, b, prec


def _dot(a, b, prec=None):
    a, b, prec = _operands(a, b, prec)
    return jnp.dot(a, b, preferred_element_type=F32, precision=prec)


def _dot_nt(a, b, prec=None):
    a, b, prec = _operands(a, b, prec)
    return lax.dot_general(a, b, (((1,), (1,)), ((), ())), preferred_element_type=F32, precision=prec)


def _dot_tn(a, b, prec=None):
    a, b, prec = _operands(a, b, prec)
    return lax.dot_general(a, b, (((0,), (0,)), ((), ())), preferred_element_type=F32, precision=prec)


def _lockstep(gens):
    gens = list(gens)
    while gens:
        alive = []
        for g in gens:
            try:
                next(g)
                alive.append(g)
            except StopIteration:
                pass
        gens = alive


def _rowsum(x):
    return jnp.sum(x, axis=-1, keepdims=True)


def _colsum(x):
    return jnp.sum(x, axis=0, keepdims=True)


def _mod(mod_ref, b_ref, k):
    return mod_ref[:, k * D:(k + 1) * D] + b_ref[:, k * D:(k + 1) * D]


def _pallas_hbm(body, *, out_shape, **kw):
    shapes = [pltpu.HBM(s.shape, s.dtype) for s in out_shape]
    call = pl.pallas_call(body, out_shape=tuple(shapes), **kw)
    return lambda *args: call(*[pltpu.with_memory_space_constraint(a, pltpu.HBM) for a in args])


def _const(shape):
    nd = len(shape)
    return pl.BlockSpec(shape, lambda *_: (0,) * nd)


def _const1(shape):
    nd = len(shape)
    return pl.BlockSpec(shape, lambda *_: (0,) * nd, pipeline_mode=pl.Buffered(1))


PEER_FLIPS = [(dx, dy, dc) for dx in (0, 1) for dy in (0, 1) for dc in (0, 1)][1:]


def _after(x, token):
    return x + token[0:1, 0:1].astype(x.dtype).reshape((1,) * x.ndim)


def _exchange(name, srcs, per_dest, seed_only=(), with_token=False):
    n = len(srcs)
    out_shape = []
    for a, pd in zip(srcs, per_dest):
        blk = a.shape[1:] if pd else a.shape
        out_shape.append(jax.ShapeDtypeStruct((N_DEV,) + tuple(blk), a.dtype))

    def body(*refs):
        src = refs[:n]
        dst = refs[n:2 * n]
        send_sems, recv_sems, local_sems = refs[-3:]
        if with_token:
            refs[2 * n][...] = jnp.zeros((8, LANES), F32)
        x, y, c = lax.axis_index("x"), lax.axis_index("y"), lax.axis_index("c")
        me = 4 * x + 2 * y + c

        def piece(i, j):
            return src[i].at[j] if per_dest[i] else src[i]

        copies = []
        for k, (dx, dy, dc) in enumerate(PEER_FLIPS):
            px = 1 - x if dx else x
            py = 1 - y if dy else y
            pc = 1 - c if dc else c
            pj = 4 * px + 2 * py + pc
            for i in range(n):
                if i in seed_only:
                    continue
                cp = pltpu.make_async_remote_copy(
                    src_ref=piece(i, pj), dst_ref=dst[i].at[me],
                    send_sem=send_sems.at[k * n + i], recv_sem=recv_sems.at[k * n + i],
                    device_id=(px, py, pc), device_id_type=MESH)
                cp.start()
                arrive = pltpu.make_async_remote_copy(
                    src_ref=piece(i, pj), dst_ref=dst[i].at[pj],
                    send_sem=send_sems.at[k * n + i], recv_sem=recv_sems.at[k * n + i],
                    device_id=(px, py, pc), device_id_type=MESH)
                copies.append((cp, arrive))
        own = []
        for i in range(n):
            lc = pltpu.make_async_copy(piece(i, me), dst[i].at[me], local_sems.at[i])
            lc.start()
            own.append(lc)
        for cp, arrive in copies:
            arrive.wait_recv()
        for cp, arrive in copies:
            cp.wait_send()
        for lc in own:
            lc.wait()

    any_spec = pl.BlockSpec(memory_space=pl.ANY)
    out_specs = [any_spec] * n
    if with_token:
        out_shape.append(jax.ShapeDtypeStruct((8, LANES), F32))
        out_specs.append(pl.BlockSpec(memory_space=pltpu.VMEM))
    return pl.pallas_call(
        body, name=name, out_shape=tuple(out_shape),
        in_specs=[any_spec] * n, out_specs=tuple(out_specs),
        scratch_shapes=[pltpu.SemaphoreType.DMA((7 * n,)), pltpu.SemaphoreType.DMA((7 * n,)),
                        pltpu.SemaphoreType.DMA((n,))],
        compiler_params=pltpu.CompilerParams(has_side_effects=True),
    )(*srcs)


CHIP_FLIPS = [(0, 1), (1, 0), (1, 1)]
LEVEL_ONE = [k for k, (dx, dy, dc) in enumerate(PEER_FLIPS) if (dx, dy, dc) == (0, 0, 1) or dc == 0]


def _chip_peers(x, y):
    return [(1 - x if dx else x, 1 - y if dy else y) for dx, dy in CHIP_FLIPS]


def _gather_two_level(name, srcs, seed_only=()):
    n = len(srcs)
    live = [i for i in range(n) if i not in seed_only]

    def body(*refs):
        src, dst = refs[:n], refs[n:2 * n]
        send_sems, recv_sems, local_sems = refs[2 * n:2 * n + 3]
        bounce = refs[2 * n + 3:]
        x, y, c = lax.axis_index("x"), lax.axis_index("y"), lax.axis_index("c")
        me = 4 * x + 2 * y + c
        sibling = (x, y, 1 - c)
        chips = _chip_peers(x, y)

        def copy(k, i, src_ref, slot, to):
            return pltpu.make_async_remote_copy(
                src_ref=src_ref, dst_ref=dst[i].at[slot], send_sem=send_sems.at[k * n + i],
                recv_sem=recv_sems.at[k * n + i], device_id=to, device_id_type=MESH)

        first = []
        for i in live:
            first.append(copy(0, i, src[i], me, sibling))
            first += [copy(1 + j, i, src[i], me, (px, py, c)) for j, (px, py) in enumerate(chips)]
        for cp in first:
            cp.start()
        up = [pltpu.make_async_copy(src[i], bounce[i], local_sems.at[i]) for i in range(n)]
        for cp in up:
            cp.start()
        for cp in up:
            cp.wait()
        own = [pltpu.make_async_copy(bounce[i], dst[i].at[me], local_sems.at[i]) for i in range(n)]
        for cp in own:
            cp.start()
        passed = []
        for j, (px, py) in enumerate(chips):
            slot = 4 * px + 2 * py + c
            for i in live:
                copy(1 + j, i, src[i], slot, (px, py, c)).wait_recv()
                fwd = copy(4 + j, i, dst[i].at[slot], slot, sibling)
                fwd.start()
                passed.append(fwd)
        for i in live:
            copy(0, i, src[i], 4 * x + 2 * y + 1 - c, sibling).wait_recv()
            for j, (px, py) in enumerate(chips):
                copy(4 + j, i, src[i], 4 * px + 2 * py + 1 - c, sibling).wait_recv()
        for cp in first + passed:
            cp.wait_send()
        for cp in own:
            cp.wait()

    any_spec = pl.BlockSpec(memory_space=pl.ANY)
    return pl.pallas_call(
        body, name=name, out_shape=tuple(jax.ShapeDtypeStruct((N_DEV,) + a.shape, a.dtype) for a in srcs),
        in_specs=[any_spec] * n, out_specs=tuple([any_spec] * n),
        scratch_shapes=[pltpu.SemaphoreType.DMA((7 * n,)), pltpu.SemaphoreType.DMA((7 * n,)),
                        pltpu.SemaphoreType.DMA((n,))] + [pltpu.VMEM(a.shape, a.dtype) for a in srcs],
        compiler_params=pltpu.CompilerParams(has_side_effects=True),
    )(*srcs)


def _relay_copy(land, sems, i, n, j, slot, sibling):
    send_sems, recv_sems = sems
    return pltpu.make_async_remote_copy(
        src_ref=land[i].at[slot], dst_ref=land[i].at[slot], send_sem=send_sems.at[j * n + i],
        recv_sem=recv_sems.at[j * n + i], device_id=sibling, device_id_type=MESH)


def _relay_start(name, lands):
    n = len(lands)

    def body(*refs):
        land = refs[:n]
        sems = refs[n], refs[n + 1]
        x, y, c = lax.axis_index("x"), lax.axis_index("y"), lax.axis_index("c")
        for j, (px, py) in enumerate(_chip_peers(x, y)):
            for i in range(n):
                _relay_copy(land, sems, i, n, j, 4 * px + 2 * py + c, (x, y, 1 - c)).start()
        refs[-1][...] = jnp.zeros((8, LANES), F32)

    return pl.pallas_call(
        body, name=name,
        out_shape=(pltpu.SemaphoreType.DMA((3 * n,)), pltpu.SemaphoreType.DMA((3 * n,)),
                   *[pltpu.HBM(a.shape, a.dtype) for a in lands], jax.ShapeDtypeStruct((8, LANES), F32)),
        in_specs=[HBM_SPEC] * n,
        out_specs=(SEM_SPEC, SEM_SPEC, *[HBM_SPEC] * n, pl.BlockSpec(memory_space=pltpu.VMEM)),
        input_output_aliases={i: 2 + i for i in range(n)},
        compiler_params=pltpu.CompilerParams(has_side_effects=DATAFLOW),
    )(*[pltpu.with_memory_space_constraint(a, pltpu.HBM) for a in lands])


def _relay_wait(name, started, after):
    n = len(started) - 3
    arrays = list(started[2:2 + n])

    def body(*refs):
        land = refs[:n]
        sems = refs[n], refs[n + 1]
        x, y, c = lax.axis_index("x"), lax.axis_index("y"), lax.axis_index("c")
        for j, (px, py) in enumerate(_chip_peers(x, y)):
            for i in range(n):
                _relay_copy(land, sems, i, n, j, 4 * px + 2 * py + c, (x, y, 1 - c)).wait_send()
                _relay_copy(land, sems, i, n, j, 4 * px + 2 * py + 1 - c, (x, y, 1 - c)).wait_recv()

    return pl.pallas_call(
        body, name=name,
        out_shape=tuple(pltpu.HBM(a.shape, a.dtype) for a in arrays),
        in_specs=[HBM_SPEC] * n + [SEM_SPEC, SEM_SPEC] + [pl.BlockSpec(memory_space=pl.ANY)] * len(after),
        out_specs=tuple([HBM_SPEC] * n),
        input_output_aliases={i: i for i in range(n)},
        compiler_params=pltpu.CompilerParams(has_side_effects=DATAFLOW),
    )(*arrays, started[0], started[1], *after)


HBM_SPEC = pl.BlockSpec(memory_space=pltpu.HBM)
SEM_SPEC = pl.BlockSpec(memory_space=pltpu.SEMAPHORE)
DATAFLOW = pltpu.SideEffectType.DATAFLOW_SIDE_EFFECTING


def _peers(only=None):
    x, y, c = lax.axis_index("x"), lax.axis_index("y"), lax.axis_index("c")
    out = []
    for k, (dx, dy, dc) in enumerate(PEER_FLIPS):
        if only is not None and k not in only:
            continue
        px = 1 - x if dx else x
        py = 1 - y if dy else y
        pc = 1 - c if dc else c
        out.append((k, (px, py, pc), 4 * px + 2 * py + pc))
    return 4 * x + 2 * y + c, out


def _exchange_start(name, srcs, lands, per_dest, only=None):
    n = len(srcs)

    def body(*refs):
        src, land = refs[:n], refs[n:2 * n]
        send_sems, recv_sems = refs[2 * n], refs[2 * n + 1]
        token = refs[-1]
        me, peers = _peers(only)
        for k, peer, pj in peers:
            for i in range(n):
                pltpu.make_async_remote_copy(
                    src_ref=src[i].at[pj] if per_dest[i] else src[i], dst_ref=land[i].at[me],
                    send_sem=send_sems.at[k * n + i], recv_sem=recv_sems.at[k * n + i],
                    device_id=peer, device_id_type=MESH).start()
        token[...] = jnp.zeros((8, LANES), F32)

    arrays = list(srcs) + list(lands)
    return pl.pallas_call(
        body, name=name,
        out_shape=(pltpu.SemaphoreType.DMA((7 * n,)), pltpu.SemaphoreType.DMA((7 * n,)),
                   *[pltpu.HBM(a.shape, a.dtype) for a in arrays], jax.ShapeDtypeStruct((8, LANES), F32)),
        in_specs=[HBM_SPEC] * (2 * n),
        out_specs=(SEM_SPEC, SEM_SPEC, *[HBM_SPEC] * (2 * n), pl.BlockSpec(memory_space=pltpu.VMEM)),
        input_output_aliases={i: 2 + i for i in range(2 * n)},
        compiler_params=pltpu.CompilerParams(has_side_effects=DATAFLOW),
    )(*[pltpu.with_memory_space_constraint(a, pltpu.HBM) for a in arrays])


def _exchange_wait(name, started, per_dest, after, only=None):
    n = (len(started) - 3) // 2
    send_sems, recv_sems = started[0], started[1]
    arrays = list(started[2:2 + 2 * n])

    def body(*refs):
        src, land = refs[:n], refs[n:2 * n]
        send, recv = refs[2 * n], refs[2 * n + 1]
        me, peers = _peers(only)
        for k, peer, pj in peers:
            for i in range(n):
                cp = pltpu.make_async_remote_copy(
                    src_ref=src[i].at[pj] if per_dest[i] else src[i], dst_ref=land[i].at[pj],
                    send_sem=send.at[k * n + i], recv_sem=recv.at[k * n + i],
                    device_id=peer, device_id_type=MESH)
                cp.wait_send()
                cp.wait_recv()

    outs = pl.pallas_call(
        body, name=name,
        out_shape=tuple(pltpu.HBM(a.shape, a.dtype) for a in arrays),
        in_specs=[HBM_SPEC] * (2 * n) + [SEM_SPEC, SEM_SPEC] + [pl.BlockSpec(memory_space=pl.ANY)] * len(after),
        out_specs=tuple([HBM_SPEC] * (2 * n)),
        input_output_aliases={i: i for i in range(2 * n)},
        compiler_params=pltpu.CompilerParams(has_side_effects=DATAFLOW),
    )(*arrays, send_sems, recv_sems, *after)
    return outs[:n], outs[n:]


def _mod_shard(c_all, w_ada):
    def body(c_ref, w_ref, o_ref):
        cv = c_ref[...]
        ca = cv * _sig(cv)
        o_ref[...] = _dot(ca.astype(BF16), w_ref[...].astype(BF16))

    return pl.pallas_call(
        body, name="mod_shard", out_shape=jax.ShapeDtypeStruct((N_DEV, w_ada.shape[1]), F32),
        compiler_params=_params(),
    )(c_all, w_ada)


def _fwd_in(x, nw1, modnb, bada, w_main, w_ba):
    def body(x_ref, nw_ref, mod_ref, b_ref, wm_ref, wb_ref, pm_ref, pb_ref, hb_ref):
        xv = x_ref[...]
        r = lax.rsqrt(jnp.mean(xv * xv, axis=-1, keepdims=True) + EPS)
        h = (xv * r * nw_ref[...]) * (1.0 + _mod(mod_ref, b_ref, 1)) + _mod(mod_ref, b_ref, 0)
        hb = h.astype(BF16)
        hb_ref[...] = hb
        pm_ref[...] = _dot_nt(hb, wm_ref[...])
        pb_ref[...] = _dot_nt(hb, wb_ref[...])

    return pl.pallas_call(
        body, name="fwd_in", grid=(NT,),
        in_specs=[pl.BlockSpec((TM, D), lambda i: (i, 0)), _const((1, D)), _const((1, 6 * D)), _const((1, 6 * D)),
                  _const((NMAIN, D)), _const((LANES, D))],
        out_specs=(pl.BlockSpec((TM, NMAIN), lambda i: (i, 0)), pl.BlockSpec((TM, LANES), lambda i: (i, 0)),
                   pl.BlockSpec((TM, D), lambda i: (i, 0))),
        out_shape=(jax.ShapeDtypeStruct((S, NMAIN), F32), jax.ShapeDtypeStruct((S, LANES), F32),
                   jax.ShapeDtypeStruct((S, D), BF16)),
        compiler_params=_params(dimension_semantics=("arbitrary",)),
    )(x, nw1, modnb, bada, w_main, w_ba)


def _group_mean_matrix():
    ii = lax.broadcasted_iota(jnp.int32, (CW, CW), 0) // GSZ
    jj = lax.broadcasted_iota(jnp.int32, (CW, CW), 1) // GSZ
    return jnp.where(ii == jj, 1.0 / GSZ, 0.0).astype(F32)


SUB = 8
SHIFT_ROWS = HALO + TM - SUB


def _fill_shifted(buf, sh):
    for b in range(1, SUB):
        sh[b - 1] = buf[b:b + SHIFT_ROWS, :]


def _rows_at(buf, sh, off):
    a, b = divmod(off, SUB)
    if b == 0:
        return buf[off:off + TM, :]
    return sh[b - 1, SUB * a:SUB * a + TM, :]


def _group_mean(x, pm):
    hi = x.astype(BF16)
    r1 = x - hi.astype(F32)
    mid = r1.astype(BF16)
    lo = (r1 - mid.astype(F32)).astype(BF16)
    return _dot(hi, pm) + _dot(mid, pm) + _dot(lo, pm)


def _conf_fwd(p_main, conv_w, conv_b, gn_w, gn_b):
    def body(a_ref, g_ref, w_ref, b_ref, gw_ref, gb_ref, y_ref, oa_ref, ubuf, ush):
        i = pl.program_id(0)

        @pl.when(i == 0)
        def _():
            ubuf[0:HALO, :] = jnp.zeros((HALO, CW), F32)

        ubuf[HALO:HALO + TM, :] = a_ref[...] * _sig(g_ref[...])
        _fill_shifted(ubuf, ush)
        acc = jnp.zeros((TM, CW), F32) + b_ref[...]
        for k in range(KC):
            acc = acc + w_ref[k:k + 1, :] * _rows_at(ubuf, ush, HALO - (KC - 1) + k)
        y_ref[...] = acc
        ubuf[0:HALO, :] = ubuf[TM:TM + HALO, :]
        pm = _group_mean_matrix().astype(BF16)
        dlt = acc - _group_mean(acc, pm)
        var = _group_mean(dlt * dlt, pm)
        o = dlt * lax.rsqrt(var + EPS) * gw_ref[...] + gb_ref[...]
        oa_ref[...] = o * _sig(o)

    return pl.pallas_call(
        body, name="conf_fwd", grid=(NT,),
        in_specs=[pl.BlockSpec((TM, CW), lambda i: (i, 0)), pl.BlockSpec((TM, CW), lambda i: (i, 1)),
                  _const((KC, CW)), _const((1, CW)), _const((1, CW)), _const((1, CW))],
        out_specs=(pl.BlockSpec((TM, CW), lambda i: (i, 0)), pl.BlockSpec((TM, CW), lambda i: (i, 0))),
        out_shape=(jax.ShapeDtypeStruct((S, CW), F32), jax.ShapeDtypeStruct((S, CW), F32)),
        scratch_shapes=[pltpu.VMEM((HALO + TM, CW), F32), pltpu.VMEM((SUB - 1, SHIFT_ROWS, CW), F32)],
        compiler_params=_params(dimension_semantics=("arbitrary",)),
    )(p_main, p_main, conv_w, conv_b, gn_w, gn_b)


def _tri_iota():
    ii = lax.broadcasted_iota(jnp.int32, (CL, CL), 0)
    jj = lax.broadcasted_iota(jnp.int32, (CL, CL), 1)
    return ii, jj


def _gdn_gates(ba, alog_l, dt_l):
    beta_all = _sig(ba)
    xg = ba + dt_l
    sp = jnp.maximum(xg, 0.0) + jnp.log(1.0 + jnp.exp(-jnp.abs(xg)))
    neg_a = -jnp.exp(alog_l)
    return beta_all, neg_a * sp, xg, neg_a


def _ones_dot(ones, x):
    hi = x.astype(BF16)
    r1 = x - hi.astype(F32)
    mid = r1.astype(BF16)
    lo = (r1 - mid.astype(F32)).astype(BF16)
    return _dot(ones, hi) + _dot(ones, mid) + _dot(ones, lo)


def _gdn_cumsum(g_all):
    ii, jj = _tri_iota()
    low = jnp.where(ii >= jj, 1.0, 0.0).astype(BF16)
    gcum = _ones_dot(low, g_all)
    return gcum, jnp.transpose(gcum)


def _split(x):
    hi = x.astype(BF16)
    return hi, (x - hi.astype(F32)).astype(BF16)


def _dot_split(a, b):
    (ah, al), (bh, bl) = a, b
    return _dot(ah, bh) + (_dot(ah, bl) + _dot(al, bh))


def _unit_lower_inverses(mats):
    ii, jj = _tri_iota()
    eye = jnp.where(ii == jj, 1.0, 0.0).astype(F32)
    ts = [eye - a for a in mats]
    ps = [_dot_split(s, s) for s in map(_split, mats)]
    for _ in range(4):
        sp = [_split(p) for p in ps]
        ts = [t + _dot_split(_split(t), s) for t, s in zip(ts, sp)]
        ps = [_dot_split(s, s) for s in sp]
    return [t + _dot_split(_split(t), _split(p)) for t, p in zip(ts, ps)]


def _head_terms(qh, kh, beta, gcol, grow):
    ii, jj = _tri_iota()
    causal = ii >= jj
    strict = ii > jj
    rq = lax.rsqrt(_rowsum(qh * qh) + EPS)
    rk = lax.rsqrt(_rowsum(kh * kh) + EPS)
    qn = qh * rq
    kn = kh * rk
    qs = qn * QSCALE
    decay = jnp.where(causal, jnp.exp(jnp.where(causal, gcol - grow, 0.0)), 0.0)
    gam = jnp.exp(gcol)
    gl = gcol[CL - 1:CL, :]
    kds = jnp.exp(gl - gcol)
    cd = jnp.exp(gl)
    kb = kn * beta
    a = jnp.where(strict, _dot_nt(kb, kn, GP) * decay, 0.0)
    qk = jnp.where(causal, _dot_nt(qs, kn, GP) * decay, 0.0)
    return dict(rq=rq, rk=rk, qn=qn, kn=kn, qs=qs, decay=decay, gam=gam, kds=kds, cd=cd, kb=kb, a=a, qk=qk,
                causal=causal, strict=strict)


def _short_conv(w_ref, buf, rows=CL):
    acc = w_ref[0:1, :] * buf[SH - KS + 1:SH - KS + 1 + rows, :]
    for k in range(1, KS):
        off = SH - (KS - 1) + k
        acc = acc + w_ref[k:k + 1, :] * buf[off:off + rows, :]
    return acc


CPS = 4
TG = CPS * CL


def _gdn_prep(p_main, p_ba, gdn_conv_w, alog_l, dt_l):
    def body(q_ref, k_ref, v_ref, qh_ref, kh_ref, vh_ref, ba_ref, w_ref, al_ref, dt_ref,
             wo_ref, uo_ref, qg_ref, kd_ref, qk_ref, cd_ref, t_ref, xbuf):
        i = pl.program_id(0)
        first = i == 0
        xbuf[0:SH, 0:GW] = jnp.where(first, 0.0, qh_ref[...])
        xbuf[0:SH, GW:2 * GW] = jnp.where(first, 0.0, kh_ref[...])
        xbuf[0:SH, 2 * GW:3 * GW] = jnp.where(first, 0.0, vh_ref[...])
        xbuf[SH:SH + TG, 0:GW] = q_ref[...]
        xbuf[SH:SH + TG, GW:2 * GW] = k_ref[...]
        xbuf[SH:SH + TG, 2 * GW:3 * GW] = v_ref[...]
        conv = _short_conv(w_ref, xbuf, TG)
        qkv = conv * _sig(conv)
        beta_all, g_all, _, _ = _gdn_gates(ba_ref[...], al_ref[...], dt_ref[...])
        lane = lax.broadcasted_iota(jnp.int32, (8, LANES), 1)
        cums = [_gdn_cumsum(g_all[cc * CL:(cc + 1) * CL, :]) for cc in range(CPS)]
        pairs = [(cc, h) for cc in range(CPS) for h in range(NH)]
        terms, vbs = [], []
        for cc, h in pairs:
            r0, lo = cc * CL, h * DH
            beta = beta_all[r0:r0 + CL, h:h + 1]
            gcum, gcum_t = cums[cc]
            terms.append(_head_terms(qkv[r0:r0 + CL, lo:lo + DH], qkv[r0:r0 + CL, GW + lo:GW + lo + DH], beta,
                                     gcum[:, NH + h:NH + h + 1], gcum_t[NH + h:NH + h + 1, :]))
            vbs.append(qkv[r0:r0 + CL, 2 * GW + lo:2 * GW + lo + DH] * beta)
        invs = _unit_lower_inverses([f["a"] for f in terms])
        cds = [jnp.zeros((8, LANES), F32) for _ in range(CPS)]
        for (cc, h), f, t, vb in zip(pairs, terms, invs, vbs):
            r0, lo = cc * CL, h * DH
            t_ref[cc, h] = t
            uo_ref[r0:r0 + CL, lo:lo + DH] = _dot(t, vb, GP)
            wo_ref[r0:r0 + CL, lo:lo + DH] = _dot(t, f["kb"] * f["gam"], GP).astype(BF16)
            qg_ref[r0:r0 + CL, lo:lo + DH] = (f["qs"] * f["gam"]).astype(BF16)
            kd_ref[r0:r0 + CL, lo:lo + DH] = (f["kn"] * f["kds"]).astype(BF16)
            qk_ref[cc, h] = f["qk"].astype(BF16)
            cds[cc] = cds[cc] + jnp.where(lane == h, f["cd"], 0.0)
        for cc in range(CPS):
            cd_ref[cc] = cds[cc]

    col = lambda j: pl.BlockSpec((TG, GW), lambda i: (i, j))
    halo = lambda j: pl.BlockSpec((SH, GW), lambda i: (jnp.maximum(i * (TG // SH) - 1, 0), j))
    tile = lambda: pl.BlockSpec((TG, GW), lambda i: (i, 0))
    sq = lambda: pl.BlockSpec((CPS, NH, CL, CL), lambda i: (i, 0, 0, 0))
    return _pallas_hbm(
        body, name="gdn_prep", grid=(NCH // CPS,),
        in_specs=[col(2), col(3), col(4), halo(2), halo(3), halo(4), pl.BlockSpec((TG, LANES), lambda i: (i, 0)),
                  _const((KS, 3 * GW)), _const((1, LANES)), _const((1, LANES))],
        out_specs=(tile(), tile(), tile(), tile(), sq(), pl.BlockSpec((CPS, 8, LANES), lambda i: (i, 0, 0)), sq()),
        out_shape=(jax.ShapeDtypeStruct((S, GW), BF16), jax.ShapeDtypeStruct((S, GW), F32),
                   jax.ShapeDtypeStruct((S, GW), BF16), jax.ShapeDtypeStruct((S, GW), BF16),
                   jax.ShapeDtypeStruct((NCH, NH, CL, CL), BF16), jax.ShapeDtypeStruct((NCH, 8, LANES), F32),
                   jax.ShapeDtypeStruct((NCH, NH, CL, CL), F32)),
        scratch_shapes=[pltpu.VMEM((SH + TG, 3 * GW), F32)],
        compiler_params=_params(dimension_semantics=("arbitrary",)),
    )(p_main, p_main, p_main, p_main, p_main, p_main, p_ba, gdn_conv_w, alog_l, dt_l)


def _gdn_scan(w_o, u_o, qg, kd, qk, cd, p_main, gdn_nw):
    def body(w_ref, u_ref, qg_ref, kd_ref, qk_ref, cd_ref, z_ref, nw_ref, ob_ref, o_ref, sin_ref, state):
        n = pl.program_id(0)

        @pl.when(n == 0)
        def _():
            state[...] = jnp.zeros((NH, DH, DH), F32)

        def head(cc, h):
            rows, lo = pl.ds(cc * CL, CL), h * DH
            st = state[h]
            sin_ref[cc, h] = st
            sb = st.astype(BF16)
            v_new = u_ref[rows, lo:lo + DH] - _dot(w_ref[rows, lo:lo + DH], sb)
            yield
            vb = v_new.astype(BF16)
            o = _dot(qg_ref[rows, lo:lo + DH], sb) + _dot(qk_ref[cc, h], vb)
            state[h] = st * cd_ref[cc, 0:1, h:h + 1] + _dot_tn(kd_ref[rows, lo:lo + DH], vb)
            yield
            o_ref[rows, lo:lo + DH] = o
            r = lax.rsqrt(jnp.mean(o * o, axis=-1, keepdims=True) + EPS)
            zh = z_ref[rows, lo:lo + DH]
            ob_ref[rows, lo:lo + DH] = o * r * nw_ref[...] * (zh * _sig(zh))

        for cc in range(CPS):
            _lockstep(head(cc, h) for h in range(NH))

    tile = lambda: pl.BlockSpec((TG, GW), lambda n: (n, 0))
    return pl.pallas_call(
        body, name="gdn_scan", grid=(NCH // CPS,),
        in_specs=[tile(), tile(), tile(), tile(), pl.BlockSpec((CPS, NH, CL, CL), lambda n: (n, 0, 0, 0)),
                  pl.BlockSpec((CPS, 8, LANES), lambda n: (n, 0, 0)), pl.BlockSpec((TG, GW), lambda n: (n, 5)),
                  _const((1, DH))],
        out_specs=(tile(), tile(), pl.BlockSpec((CPS, NH, DH, DH), lambda n: (n, 0, 0, 0))),
        out_shape=(jax.ShapeDtypeStruct((S, GW), F32), jax.ShapeDtypeStruct((S, GW), F32),
                   jax.ShapeDtypeStruct((NCH, NH, DH, DH), F32)),
        scratch_shapes=[pltpu.VMEM((NH, DH, DH), F32)],
        compiler_params=_params(dimension_semantics=("arbitrary",)),
    )(w_o, u_o, qg, kd, qk, cd, p_main, gdn_nw)


def _fwd_out(out_a, out_b, x, modnb, bada, w_out):
    def body(oa_ref, ob_ref, x_ref, mod_ref, b_ref, w_ref, x1_ref, mix_ref, oab_ref):
        oa = oa_ref[...].astype(BF16)
        ob = ob_ref[...].astype(BF16)
        oab_ref[:, 0:CW] = oa
        oab_ref[:, CW:D] = ob
        mix = _dot(oa, w_ref[0:CW, :]) + _dot(ob, w_ref[CW:D, :])
        mix_ref[...] = mix
        x1_ref[...] = x_ref[...] + _mod(mod_ref, b_ref, 2) * mix

    tile = lambda w: pl.BlockSpec((TM, w), lambda i: (i, 0))
    return pl.pallas_call(
        body, name="fwd_out", grid=(NT,),
        in_specs=[tile(CW), tile(GW), tile(D), _const((1, 6 * D)), _const((1, 6 * D)), _const((D, D))],
        out_specs=(tile(D), tile(D), tile(D)),
        out_shape=(jax.ShapeDtypeStruct((S, D), F32), jax.ShapeDtypeStruct((S, D), F32),
                   jax.ShapeDtypeStruct((S, D), BF16)),
        compiler_params=_params(dimension_semantics=("arbitrary",)),
    )(out_a, out_b, x, modnb, bada, w_out)


FFN_STATS = 8


def _ffn_forward(x1, tgt, modnb, bada, nw2, nfw, w_fi, w_fo):
    def body(x1_ref, tgt_ref, mod_ref, b_ref, nw2_ref, nfw_ref, wi_ref, wo_ref,
             hb_ref, act_ref, pre_ref, dx2_ref, dffn_ref, st_ref):
        i = pl.program_id(0)

        @pl.when(i == 0)
        def _():
            st_ref[...] = jnp.zeros((FFN_STATS, D), F32)

        sh2, sc2, gt2 = _mod(mod_ref, b_ref, 3), _mod(mod_ref, b_ref, 4), _mod(mod_ref, b_ref, 5)
        x1v = x1_ref[...]
        r2 = lax.rsqrt(jnp.mean(x1v * x1v, axis=-1, keepdims=True) + EPS)
        hb = ((x1v * r2 * nw2_ref[...]) * (1.0 + sc2) + sh2).astype(BF16)
        hb_ref[...] = hb
        ffn = jnp.zeros((TM, D), F32)
        for j in range(4):
            fgj = _dot_nt(hb, wi_ref[j])
            fuj = _dot_nt(hb, wi_ref[j + 4])
            pre_ref[j] = fgj.astype(BF16)
            pre_ref[j + 4] = fuj.astype(BF16)
            aj = (fgj * _sig(fgj) * fuj).astype(BF16)
            act_ref[j] = aj
            ffn = ffn + _dot(aj, wo_ref[j])
        x2 = x1v + gt2 * ffn
        r3 = lax.rsqrt(jnp.mean(x2 * x2, axis=-1, keepdims=True) + EPS)
        xr3 = x2 * r3
        err = xr3 * nfw_ref[...] - tgt_ref[...]
        loss = 0.5 * jnp.sum(jnp.mean(err * err, axis=-1, keepdims=True), axis=0, keepdims=True)
        dy = err * (1.0 / D)
        st_ref[0:1, :] += _colsum(dy * xr3)
        dyr = dy * nfw_ref[...]
        dx2 = r3 * (dyr - xr3 * jnp.mean(dyr * xr3, axis=-1, keepdims=True))
        st_ref[1:2, :] += _colsum(dx2 * ffn)
        st_ref[5:6, :] += jnp.broadcast_to(loss, (1, D))
        dx2_ref[...] = dx2
        dffn_ref[...] = (gt2 * dx2).astype(BF16)

    tile = lambda w: pl.BlockSpec((TM, w), lambda i: (i, 0))
    return _pallas_hbm(
        body, name="ffn_forward", grid=(NT,),
        in_specs=[tile(D), tile(D), _const((1, 6 * D)), _const((1, 6 * D)), _const((1, D)), _const((1, D)),
                  _const1((N_DEV, FB, D)), _const1((4, FB, D))],
        out_specs=(tile(D), pl.BlockSpec((4, TM, FB), lambda i: (0, i, 0)),
                   pl.BlockSpec((N_DEV, TM, FB), lambda i: (0, i, 0)), tile(D), tile(D), _const((FFN_STATS, D))),
        out_shape=(jax.ShapeDtypeStruct((S, D), BF16), jax.ShapeDtypeStruct((4, S, FB), BF16),
                   jax.ShapeDtypeStruct((N_DEV, S, FB), BF16), jax.ShapeDtypeStruct((S, D), F32),
                   jax.ShapeDtypeStruct((S, D), BF16), jax.ShapeDtypeStruct((FFN_STATS, D), F32)),
        compiler_params=_params(42, dimension_semantics=("arbitrary",)),
    )(x1, tgt, modnb, bada, nw2, nfw, w_fi, w_fo)


def _ffn_backward(dffn, pre, x1, dx2, modnb, bada, nw2, w_fi, w_fo):
    def body(dffn_ref, pre_ref, x1_ref, dx2_ref, mod_ref, b_ref, nw2_ref, wi_ref, wo_ref, df_ref, dx1_ref, st_ref):
        i = pl.program_id(0)

        @pl.when(i == 0)
        def _():
            st_ref[...] = jnp.zeros((FFN_STATS, D), F32)

        dffn = dffn_ref[...]
        dh = jnp.zeros((TM, D), F32)
        for j in range(4):
            fg = pre_ref[j].astype(F32)
            fu = pre_ref[j + 4].astype(F32)
            sg = _sig(fg)
            dact = _dot_nt(dffn, wo_ref[j])
            dfg = (dact * fu * (sg * (1.0 + fg * (1.0 - sg)))).astype(BF16)
            dfu = (dact * (fg * sg)).astype(BF16)
            df_ref[j] = dfg
            df_ref[j + 4] = dfu
            dh = dh + _dot(dfg, wi_ref[j]) + _dot(dfu, wi_ref[j + 4])
        x1v = x1_ref[...]
        r2 = lax.rsqrt(jnp.mean(x1v * x1v, axis=-1, keepdims=True) + EPS)
        xr2 = x1v * r2
        st_ref[2:3, :] += _colsum(dh)
        st_ref[3:4, :] += _colsum(dh * (xr2 * nw2_ref[...]))
        dxn = dh * (1.0 + _mod(mod_ref, b_ref, 4))
        st_ref[4:5, :] += _colsum(dxn * xr2)
        dxr = dxn * nw2_ref[...]
        dx1_ref[...] = dx2_ref[...] + r2 * (dxr - xr2 * jnp.mean(dxr * xr2, axis=-1, keepdims=True))

    tile = lambda w: pl.BlockSpec((TM, w), lambda i: (i, 0))
    wide = lambda: pl.BlockSpec((N_DEV, TM, FB), lambda i: (0, i, 0))
    return _pallas_hbm(
        body, name="ffn_backward", grid=(NT,),
        in_specs=[tile(D), wide(), tile(D), tile(D), _const((1, 6 * D)), _const((1, 6 * D)), _const((1, D)),
                  _const1((N_DEV, FB, D)), _const1((4, FB, D))],
        out_specs=(wide(), tile(D), _const((FFN_STATS, D))),
        out_shape=(jax.ShapeDtypeStruct((N_DEV, S, FB), BF16), jax.ShapeDtypeStruct((S, D), F32),
                   jax.ShapeDtypeStruct((FFN_STATS, D), F32)),
        compiler_params=_params(44, dimension_semantics=("arbitrary",)),
    )(dffn, pre, x1, dx2, modnb, bada, nw2, w_fi, w_fo)


def _grad_w(name, a, b, nb):
    m, n = a.shape[1], b.shape[1]

    def body(a_ref, b_ref, o_ref):
        o_ref[...] = _dot_tn(a_ref[...], b_ref[...]).astype(BF16)

    return pl.pallas_call(
        body, name=name, grid=(m // nb,),
        in_specs=[pl.BlockSpec((S, nb), lambda j: (0, j)), _const((S, n))],
        out_specs=pl.BlockSpec((nb, n), lambda j: (j, 0)),
        out_shape=jax.ShapeDtypeStruct((m, n), BF16),
        compiler_params=_params(dimension_semantics=("arbitrary",)),
    )(a, b)


GW_IN_ROWS = NMAIN + LANES


def _grad_w_in(dp_conf, dp_gdn, dp_ba, hb1):
    nb = 512
    n_conf, n_gdn = 2 * CW // nb, 4 * GW // nb

    def body(c_ref, g_ref, ba_ref, h_ref, o_ref):
        j = pl.program_id(0)

        @pl.when(j < n_conf)
        def _():
            o_ref[...] = _dot_tn(c_ref[...], h_ref[...]).astype(BF16)

        @pl.when((j >= n_conf) & (j < n_conf + n_gdn))
        def _():
            o_ref[...] = _dot_tn(g_ref[...], h_ref[...]).astype(BF16)

        @pl.when(j == n_conf + n_gdn)
        def _():
            o_ref[0:LANES, :] = _dot_tn(ba_ref[...], h_ref[...]).astype(BF16)

    return pl.pallas_call(
        body, name="grad_w_in", grid=(n_conf + n_gdn + 1,),
        in_specs=[pl.BlockSpec((S, nb), lambda j: (0, jnp.minimum(j, n_conf - 1))),
                  pl.BlockSpec((S, nb), lambda j: (0, jnp.clip(j - n_conf, 0, n_gdn - 1))),
                  _const((S, LANES)), _const((S, D))],
        out_specs=pl.BlockSpec((nb, D), lambda j: (j, 0)),
        out_shape=jax.ShapeDtypeStruct((GW_IN_ROWS, D), BF16),
        compiler_params=_params(dimension_semantics=("arbitrary",)),
    )(dp_conf, dp_gdn, dp_ba, hb1)


def _grad_w_ffn_in(hb2, df):
    def body(a_ref, b_ref, o_ref):
        o_ref[0] = _dot_tn(b_ref[0], a_ref[...]).astype(BF16)

    return pl.pallas_call(
        body, name="grad_w_ffn_in", grid=(N_DEV,),
        in_specs=[_const((S, D)), pl.BlockSpec((1, S, FB), lambda j: (j, 0, 0))],
        out_specs=pl.BlockSpec((1, FB, D), lambda j: (j, 0, 0)),
        out_shape=jax.ShapeDtypeStruct((N_DEV, FB, D), BF16),
        compiler_params=_params(dimension_semantics=("arbitrary",)),
    )(hb2, df)


def _grad_w_ffn_out(act, dffn):
    def body(a_ref, b_ref, o_ref):
        o_ref[0] = _dot_tn(a_ref[0], b_ref[...]).astype(BF16)

    return pl.pallas_call(
        body, name="grad_w_ffn_out", grid=(4,),
        in_specs=[pl.BlockSpec((1, S, FB), lambda j: (j, 0, 0)), _const((S, D))],
        out_specs=pl.BlockSpec((1, FB, D), lambda j: (j, 0, 0)),
        out_shape=jax.ShapeDtypeStruct((4, FB, D), BF16),
        compiler_params=_params(dimension_semantics=("arbitrary",)),
    )(act, dffn)


def _bwd_out(dx1, mix, modnb, bada, w_out):
    def body(dx_ref, mix_ref, mod_ref, b_ref, w_ref, dmix_ref, doa_ref, dob_ref, st_ref):
        i = pl.program_id(0)

        @pl.when(i == 0)
        def _():
            st_ref[...] = jnp.zeros((8, D), F32)

        dx = dx_ref[...]
        st_ref[0:1, :] += _colsum(dx * mix_ref[...])
        dmix = (_mod(mod_ref, b_ref, 2) * dx).astype(BF16)
        dmix_ref[...] = dmix
        doa_ref[...] = _dot_nt(dmix, w_ref[0:CW, :])
        dob_ref[...] = _dot_nt(dmix, w_ref[CW:D, :])

    tile = lambda w: pl.BlockSpec((TM, w), lambda i: (i, 0))
    return pl.pallas_call(
        body, name="bwd_out", grid=(NT,),
        in_specs=[tile(D), tile(D), _const((1, 6 * D)), _const((1, 6 * D)), _const((D, D))],
        out_specs=(tile(D), tile(CW), tile(GW), _const((8, D))),
        out_shape=(jax.ShapeDtypeStruct((S, D), BF16), jax.ShapeDtypeStruct((S, CW), F32),
                   jax.ShapeDtypeStruct((S, GW), F32), jax.ShapeDtypeStruct((8, D), F32)),
        compiler_params=_params(dimension_semantics=("arbitrary",)),
    )(dx1, mix, modnb, bada, w_out)


CONF_STATS = 40


def _conf_bwd(d_out_a, y, p_main, conv_w, gn_w, gn_b):
    def body(do_ref, y_ref, a_ref, g_ref, ah_ref, gh_ref, w_ref, gw_ref, gb_ref, dp_ref, st_ref,
             ubuf, dybuf, ush, dysh):
        i = pl.program_id(0)

        @pl.when(i == 0)
        def _():
            st_ref[...] = jnp.zeros((CONF_STATS, CW), F32)
            dybuf[TM:TM + HALO, :] = jnp.zeros((HALO, CW), F32)

        pm = _group_mean_matrix().astype(BF16)
        yv = y_ref[...]
        dlt = yv - _group_mean(yv, pm)
        rstd = lax.rsqrt(_group_mean(dlt * dlt, pm) + EPS)
        un = dlt * rstd
        o = un * gw_ref[...] + gb_ref[...]
        so = _sig(o)
        d_o = do_ref[...] * (so * (1.0 + o * (1.0 - so)))
        st_ref[33:34, :] += _colsum(d_o)
        st_ref[32:33, :] += _colsum(d_o * un)
        dun = d_o * gw_ref[...]
        dy = rstd * (dun - _group_mean(dun, pm) - un * _group_mean(dun * un, pm))
        st_ref[31:32, :] += _colsum(dy)
        dybuf[0:TM, :] = dy
        _fill_shifted(dybuf, dysh)

        a = a_ref[...]
        sg = _sig(g_ref[...])
        first = i == NT - 1
        ubuf[0:HALO, :] = jnp.where(first, 0.0, ah_ref[...] * _sig(gh_ref[...]))
        ubuf[HALO:HALO + TM, :] = a * sg
        _fill_shifted(ubuf, ush)
        du = jnp.zeros((TM, CW), F32)
        for k in range(KC):
            st_ref[k:k + 1, :] += _colsum(dy * _rows_at(ubuf, ush, HALO - (KC - 1) + k))
            du = du + w_ref[k:k + 1, :] * _rows_at(dybuf, dysh, KC - 1 - k)
        dybuf[TM:TM + HALO, :] = dybuf[0:HALO, :]
        dp_ref[:, 0:CW] = (du * sg).astype(BF16)
        dp_ref[:, CW:2 * CW] = (du * a * sg * (1.0 - sg)).astype(BF16)

    rev = lambda w, j=0: pl.BlockSpec((TM, w), lambda i: (NT - 1 - i, j))
    halo = lambda j: pl.BlockSpec((HALO, CW), lambda i: (jnp.maximum((NT - 1 - i) * (TM // HALO) - 1, 0), j))
    return pl.pallas_call(
        body, name="conf_bwd", grid=(NT,),
        in_specs=[rev(CW), rev(CW), rev(CW, 0), rev(CW, 1), halo(0), halo(1),
                  _const((KC, CW)), _const((1, CW)), _const((1, CW))],
        out_specs=(rev(2 * CW), _const((CONF_STATS, CW))),
        out_shape=(jax.ShapeDtypeStruct((S, 2 * CW), BF16), jax.ShapeDtypeStruct((CONF_STATS, CW), F32)),
        scratch_shapes=[pltpu.VMEM((HALO + TM, CW), F32), pltpu.VMEM((TM + HALO, CW), F32),
                        pltpu.VMEM((SUB - 1, SHIFT_ROWS, CW), F32), pltpu.VMEM((SUB - 1, SHIFT_ROWS, CW), F32)],
        compiler_params=_params(dimension_semantics=("arbitrary",)),
    )(d_out_a, y, p_main, p_main, p_main, p_main, conv_w, gn_w, gn_b)


GDN_STATS = 8


def _gdn_bwd(d_out_b, o_pre, s_in, t_inv, p_main, p_ba, gdn_conv_w, alog_l, dt_l, gdn_nw):
    def body(dob_ref, o_ref, sin_ref, t_ref, q_ref, k_ref, v_ref, z_ref, qh_ref, kh_ref, vh_ref, ba_ref,
             w_ref, al_ref, dt_ref, nw_ref, dp_ref, dba_ref, st_ref, xbuf, dcbuf, dstate):
        n = pl.program_id(0)

        @pl.when(n == 0)
        def _():
            st_ref[...] = jnp.zeros((GDN_STATS, 3 * GW), F32)
            dcbuf[CL:CL + SH, :] = jnp.zeros((SH, 3 * GW), F32)
            dstate[...] = jnp.zeros((NH, DH, DH), F32)

        for cc in reversed(range(CPS)):
            chunk(n, cc, dob_ref, o_ref, sin_ref, t_ref, q_ref, k_ref, v_ref, z_ref, qh_ref, kh_ref, vh_ref, ba_ref,
                  w_ref, al_ref, dt_ref, nw_ref, dp_ref, dba_ref, st_ref, xbuf, dcbuf, dstate)

    def chunk(n, cc, dob_ref, o_ref, sin_ref, t_ref, q_ref, k_ref, v_ref, z_ref, qh_ref, kh_ref, vh_ref, ba_ref,
              w_ref, al_ref, dt_ref, nw_ref, dp_ref, dba_ref, st_ref, xbuf, dcbuf, dstate):
        r0 = cc * CL
        if cc == 0:
            first = n == NCH // CPS - 1
            xbuf[0:SH, 0:GW] = jnp.where(first, 0.0, qh_ref[...])
            xbuf[0:SH, GW:2 * GW] = jnp.where(first, 0.0, kh_ref[...])
            xbuf[0:SH, 2 * GW:3 * GW] = jnp.where(first, 0.0, vh_ref[...])
        else:
            xbuf[0:SH, 0:GW] = q_ref[r0 - SH:r0, :]
            xbuf[0:SH, GW:2 * GW] = k_ref[r0 - SH:r0, :]
            xbuf[0:SH, 2 * GW:3 * GW] = v_ref[r0 - SH:r0, :]
        xbuf[SH:SH + CL, 0:GW] = q_ref[r0:r0 + CL, :]
        xbuf[SH:SH + CL, GW:2 * GW] = k_ref[r0:r0 + CL, :]
        xbuf[SH:SH + CL, 2 * GW:3 * GW] = v_ref[r0:r0 + CL, :]
        conv = _short_conv(w_ref, xbuf)
        sc = _sig(conv)
        qkv = conv * sc
        ba = ba_ref[r0:r0 + CL, :]
        beta_all, g_all, xg, neg_a = _gdn_gates(ba, al_ref[...], dt_ref[...])
        gcum, gcum_t = _gdn_cumsum(g_all)
        lane = lax.broadcasted_iota(jnp.int32, (CL, LANES), 1)
        row = lax.broadcasted_iota(jnp.int32, (CL, 1), 0)
        acc = dict(dgcum=jnp.zeros((CL, LANES), F32), dbeta=jnp.zeros((CL, LANES), F32))

        def head(h):
            lo = h * DH
            qh = qkv[:, lo:lo + DH]
            kh = qkv[:, GW + lo:GW + lo + DH]
            vh = qkv[:, 2 * GW + lo:2 * GW + lo + DH]
            beta = beta_all[:, h:h + 1]
            f = _head_terms(qh, kh, beta, gcum[:, NH + h:NH + h + 1], gcum_t[NH + h:NH + h + 1, :])
            qn, kn, qs, kb, gam, kds, cd, decay = (f[s] for s in ("qn", "kn", "qs", "kb", "gam", "kds", "cd", "decay"))
            t = t_ref[cc, h]
            st = sin_ref[cc, h]
            vb = vh * beta
            kbg = kb * gam
            u = _dot(t, vb, GP)
            w = _dot(t, kbg, GP)
            yield
            v_new = u - _dot(w, st, GP)
            q_dec = qs * gam
            k_dec = kn * kds

            o = o_ref[r0:r0 + CL, lo:lo + DH]
            zh = z_ref[r0:r0 + CL, lo:lo + DH]
            sz = _sig(zh)
            r = lax.rsqrt(jnp.mean(o * o, axis=-1, keepdims=True) + EPS)
            orr = o * r
            d_out = dob_ref[r0:r0 + CL, lo:lo + DH]
            dz = d_out * (orr * nw_ref[...]) * (sz * (1.0 + zh * (1.0 - sz)))
            don = d_out * (zh * sz)
            st_ref[4:5, 0:DH] += _colsum(don * orr)
            tt = don * nw_ref[...]
            d_o = r * (tt - orr * jnp.mean(tt * orr, axis=-1, keepdims=True))

            yield
            ds_out = dstate[h]
            dv_new = _dot_tn(f["qk"], d_o, GP) + _dot(k_dec, ds_out, GP)
            dqk = jnp.where(f["causal"], _dot_nt(d_o, v_new, GP), 0.0)
            dq_dec = _dot_nt(d_o, st, GP)
            dk_dec = _dot_nt(v_new, ds_out, GP)
            yield
            dstate[h] = _dot_tn(q_dec, d_o, GP) + cd * ds_out - _dot_tn(w, dv_new, GP)
            dcd = jnp.sum(_rowsum(st * ds_out), axis=0, keepdims=True)
            dw = -_dot_nt(dv_new, st, GP)
            dvb = _dot_tn(t, dv_new, GP)
            yield
            dt_m = _dot_nt(dv_new, vb, GP) + _dot_nt(dw, kbg, GP)
            dkbg = _dot_tn(t, dw, GP)
            yield
            dtt = _dot_nt(dt_m, t, GP)
            yield
            da = jnp.where(f["strict"], -_dot_tn(t, dtt, GP), 0.0)
            yield
            dad = da * decay
            dqkd = dqk * decay
            dkb = _dot(dad, kn, GP) + dkbg * gam
            dkn = _dot_tn(dad, kb, GP) + _dot_tn(dqkd, qs, GP) + dk_dec * kds + dkb * beta
            dqs = _dot(dqkd, kn, GP) + dq_dec * gam
            yield
            m = da * f["a"] + dqk * f["qk"]
            tk = _rowsum(dk_dec * k_dec)
            dgl = jnp.sum(tk, axis=0, keepdims=True) + dcd * cd
            dgc = (_rowsum(m) - _rowsum(jnp.transpose(m)) + _rowsum(dq_dec * q_dec) - tk + _rowsum(dkbg * kbg)
                   + jnp.where(row == CL - 1, dgl, 0.0))
            dbeta = _rowsum(dkb * kn) + _rowsum(dvb * vh)
            acc["dgcum"] = acc["dgcum"] + jnp.where(lane == NH + h, dgc, 0.0)
            acc["dbeta"] = acc["dbeta"] + jnp.where(lane == h, dbeta, 0.0)
            dvh = dvb * beta
            dqn = dqs * QSCALE
            dqh = f["rq"] * (dqn - qn * _rowsum(dqn * qn))
            dkh = f["rk"] * (dkn - kn * _rowsum(dkn * kn))
            dsilu = lambda c0: sc[:, c0:c0 + DH] * (1.0 + conv[:, c0:c0 + DH] * (1.0 - sc[:, c0:c0 + DH]))
            dcbuf[0:CL, lo:lo + DH] = dqh * dsilu(lo)
            dcbuf[0:CL, GW + lo:GW + lo + DH] = dkh * dsilu(GW + lo)
            dcbuf[0:CL, 2 * GW + lo:2 * GW + lo + DH] = dvh * dsilu(2 * GW + lo)
            dp_ref[r0:r0 + CL, 3 * GW + lo:3 * GW + lo + DH] = dz.astype(BF16)

        _lockstep(head(h) for h in range(NH))
        dgcum_all, dbeta_all = acc["dgcum"], acc["dbeta"]

        ii, jj = _tri_iota()
        upper = jnp.where(ii <= jj, 1.0, 0.0).astype(BF16)
        dg_all = _ones_dot(upper, dgcum_all)
        dxg = dg_all * neg_a * _sig(xg)
        st_ref[5:6, 0:LANES] += _colsum(dg_all * g_all)
        st_ref[6:7, 0:LANES] += _colsum(dxg)
        dbl = dbeta_all * beta_all * (1.0 - beta_all)
        dba_ref[r0:r0 + CL, :] = jnp.where(lane < NH, dbl, jnp.where(lane < 2 * NH, dxg, 0.0)).astype(BF16)

        dconv = dcbuf[0:CL, :]
        dx = w_ref[0:1, :] * dcbuf[KS - 1:KS - 1 + CL, :]
        st_ref[0:1, :] += _colsum(dconv * xbuf[SH - KS + 1:SH - KS + 1 + CL, :])
        for k in range(1, KS):
            off = SH - (KS - 1) + k
            st_ref[k:k + 1, :] += _colsum(dconv * xbuf[off:off + CL, :])
            dx = dx + w_ref[k:k + 1, :] * dcbuf[KS - 1 - k:KS - 1 - k + CL, :]
        dcbuf[CL:CL + SH, :] = dcbuf[0:SH, :]
        dp_ref[r0:r0 + CL, 0:3 * GW] = dx.astype(BF16)

    steps = NCH // CPS
    rev = lambda w, j=0: pl.BlockSpec((TG, w), lambda n: (steps - 1 - n, j))
    halo = lambda j: pl.BlockSpec((SH, GW), lambda n: (jnp.maximum((steps - 1 - n) * (TG // SH) - 1, 0), j))
    blk4 = lambda a, b: pl.BlockSpec((CPS, NH, a, b), lambda n: (steps - 1 - n, 0, 0, 0))
    return _pallas_hbm(
        body, name="gdn_bwd", grid=(steps,),
        in_specs=[rev(GW), rev(GW), blk4(DH, DH), blk4(CL, CL), rev(GW, 2), rev(GW, 3), rev(GW, 4), rev(GW, 5),
                  halo(2), halo(3), halo(4), rev(LANES), _const((KS, 3 * GW)), _const((1, LANES)),
                  _const((1, LANES)), _const((1, DH))],
        out_specs=(rev(4 * GW), rev(LANES), _const((GDN_STATS, 3 * GW))),
        out_shape=(jax.ShapeDtypeStruct((S, 4 * GW), BF16), jax.ShapeDtypeStruct((S, LANES), BF16),
                   jax.ShapeDtypeStruct((GDN_STATS, 3 * GW), F32)),
        scratch_shapes=[pltpu.VMEM((SH + CL, 3 * GW), F32), pltpu.VMEM((CL + SH, 3 * GW), F32),
                        pltpu.VMEM((NH, DH, DH), F32)],
        compiler_params=_params(dimension_semantics=("arbitrary",)),
    )(d_out_b, o_pre, s_in, t_inv, p_main, p_main, p_main, p_main, p_main, p_main, p_main, p_ba,
      gdn_conv_w, alog_l, dt_l, gdn_nw)


def _bwd_in(dp_conf, dp_gdn, dp_ba, x, dx1, nw1, modnb, bada, w_main, w_ba):
    def body(dc_ref, dg_ref, db_ref, x_ref, dx1_ref, nw_ref, mod_ref, b_ref, wm_ref, wb_ref, gx_ref, st_ref):
        i = pl.program_id(0)

        @pl.when(i == 0)
        def _():
            st_ref[...] = jnp.zeros((8, D), F32)

        dh = (_dot(dc_ref[...], wm_ref[0:2 * CW, :]) + _dot(dg_ref[...], wm_ref[2 * CW:NMAIN, :])
              + _dot(db_ref[...], wb_ref[...]))
        xv = x_ref[...]
        r = lax.rsqrt(jnp.mean(xv * xv, axis=-1, keepdims=True) + EPS)
        xr = xv * r
        st_ref[0:1, :] += _colsum(dh)
        st_ref[1:2, :] += _colsum(dh * (xr * nw_ref[...]))
        dxn = dh * (1.0 + _mod(mod_ref, b_ref, 1))
        st_ref[2:3, :] += _colsum(dxn * xr)
        dxr = dxn * nw_ref[...]
        gx_ref[...] = dx1_ref[...] + r * (dxr - xr * jnp.mean(dxr * xr, axis=-1, keepdims=True))

    tile = lambda w: pl.BlockSpec((TM, w), lambda i: (i, 0))
    return pl.pallas_call(
        body, name="bwd_in", grid=(NT,),
        in_specs=[tile(2 * CW), tile(4 * GW), tile(LANES), tile(D), tile(D), _const((1, D)), _const((1, 6 * D)),
                  _const((1, 6 * D)), _const((NMAIN, D)), _const((LANES, D))],
        out_specs=(tile(D), _const((8, D))),
        out_shape=(jax.ShapeDtypeStruct((S, D), F32), jax.ShapeDtypeStruct((8, D), F32)),
        compiler_params=_params(dimension_semantics=("arbitrary",)),
    )(dp_conf, dp_gdn, dp_ba, x, dx1, nw1, modnb, bada, w_main, w_ba)


def _adamw(w, g, m, v):
    m = ADAM_B1 * m + (1.0 - ADAM_B1) * g
    v = ADAM_B2 * v + (1.0 - ADAM_B2) * (g * g)
    m_hat = m / BC1
    v_hat = v / BC2
    delta = -ADAM_LR * (m_hat / (jnp.sqrt(v_hat) + ADAM_EPS) + ADAM_WD * w)
    return delta, m, v


ADAM_BLOCK_BYTES = 6 * 1024 * 1024


def _adam_tile(rows, cols):
    padded = -(-cols // LANES) * LANES
    if N_DEV * rows * padded * 4 <= ADAM_BLOCK_BYTES:
        return rows, cols
    best = None
    for tr in range(16, rows, 16):
        if rows % tr == 0 and N_DEV * tr * padded * 4 <= ADAM_BLOCK_BYTES:
            best = tr
    if best is not None:
        return best, cols
    rows_padded = -(-rows // 16) * 16
    tc = LANES
    for cand in range(LANES, cols, LANES):
        if cols % cand == 0 and N_DEV * rows_padded * cand * 4 <= ADAM_BLOCK_BYTES:
            tc = cand
    return rows, tc


def _reduce_adam(name, parts, w, m, v, own=None):
    rows, cols = w.shape
    tr, tc = _adam_tile(rows, cols)

    def body(*refs):
        p_ref, w_ref, m_ref, v_ref = refs[:4]
        g_ref, d_ref, nm_ref, nv_ref = refs[-4:]
        if own is None:
            part = lambda j: p_ref[j].astype(F32)
        else:
            me = 4 * lax.axis_index("x") + 2 * lax.axis_index("y") + lax.axis_index("c")
            part = lambda j: jnp.where(me == j, refs[4][...], p_ref[j]).astype(F32)
        g = part(0)
        for j in range(1, N_DEV):
            g = g + part(j)
        g_ref[...] = g
        d_ref[...], nm_ref[...], nv_ref[...] = _adamw(w_ref[...], g, m_ref[...], v_ref[...])

    blk = pl.BlockSpec((tr, tc), lambda i, j: (i, j))
    sds = jax.ShapeDtypeStruct((rows, cols), F32)
    extra = [] if own is None else [own]
    return pl.pallas_call(
        body, name=name, grid=(rows // tr, cols // tc),
        in_specs=[pl.BlockSpec((N_DEV, tr, tc), lambda i, j: (0, i, j)), blk, blk, blk] + [blk] * len(extra),
        out_specs=(blk, blk, blk, blk), out_shape=(sds, sds, sds, sds),
        compiler_params=_params(dimension_semantics=("arbitrary", "arbitrary")),
    )(parts, w, m, v, *extra)


def _ada_adam(c_all, dmod_sh, w, m, v):
    rows, cols = w.shape
    tr = 256

    def body(c_ref, dm_ref, w_ref, m_ref, v_ref, g_ref, d_ref, nm_ref, nv_ref):
        cv = c_ref[...]
        g = _dot_tn(cv * _sig(cv), dm_ref[...], HI)
        g_ref[...] = g
        d_ref[...], nm_ref[...], nv_ref[...] = _adamw(w_ref[...], g, m_ref[...], v_ref[...])

    blk = pl.BlockSpec((tr, cols), lambda i: (i, 0))
    sds = jax.ShapeDtypeStruct((rows, cols), F32)
    return pl.pallas_call(
        body, name="ada_adam", grid=(rows // tr,),
        in_specs=[pl.BlockSpec((N_DEV, tr), lambda i: (0, i)), _const((N_DEV, cols)), blk, blk, blk],
        out_specs=(blk, blk, blk, blk), out_shape=(sds, sds, sds, sds),
        compiler_params=_params(dimension_semantics=("arbitrary",)),
    )(c_all, dmod_sh, w, m, v)


def _lanes(a, at=0):
    return jnp.pad(a, ((0, 0), (at, LANES - at - a.shape[1])))


WEIGHT_NAMES = ["w_ada", "b_ada", "norm_mix_w", "w_in", "conv_w", "conv_b", "conv_gn_w", "conv_gn_b", "gdn_conv_w",
                "gdn_a_log", "gdn_dt_bias", "gdn_norm_w", "w_out", "norm_ffn_w", "w_ffn_in", "w_ffn_out",
                "norm_final_w"]


SMALL_LAYOUT = [("b_ada", 0, 48, LANES), ("norm_mix_w", 48, 8, LANES), ("norm_ffn_w", 56, 8, LANES),
                ("norm_final_w", 64, 8, LANES), ("conv_b", 72, 4, LANES), ("conv_gn_w", 76, 4, LANES),
                ("conv_gn_b", 80, 4, LANES), ("gdn_norm_w", 84, 1, LANES), ("gdn_a_log", 85, 1, NH),
                ("gdn_dt_bias", 86, 1, NH)]
LOSS_ROW = 87


def _adam_small(g_small, weights, m1, m2):
    names = [nm for nm, _, _, _ in SMALL_LAYOUT]
    k = len(names)

    def body(*refs):
        g_ref = refs[0]
        w_refs, m_refs, v_refs = refs[1:1 + k], refs[1 + k:1 + 2 * k], refs[1 + 2 * k:1 + 3 * k]
        loss_ref = refs[1 + 3 * k]
        outs = refs[2 + 3 * k:2 + 7 * k]
        total = refs[-1]
        g = g_ref[0]
        for j in range(1, N_DEV):
            g = g + g_ref[j]
        total[...] = g
        loss_ref[...] = total[LOSS_ROW:LOSS_ROW + 1, :]
        for i, (_, r0, rows, lanes) in enumerate(SMALL_LAYOUT):
            gp = total[r0:r0 + rows, 0:lanes]
            outs[i][...] = gp
            outs[k + i][...], outs[2 * k + i][...], outs[3 * k + i][...] = _adamw(
                w_refs[i][...], gp, m_refs[i][...], v_refs[i][...])

    shapes = [jax.ShapeDtypeStruct((rows, lanes), F32) for _, _, rows, lanes in SMALL_LAYOUT]
    res = pl.pallas_call(
        body, name="adam_small",
        out_shape=tuple([jax.ShapeDtypeStruct((1, LANES), F32)] + shapes * 4),
        scratch_shapes=[pltpu.VMEM((SMALL_ROWS, LANES), F32)],
        compiler_params=_params(),
    )(g_small, *[weights[n] for n in names], *[m1[n] for n in names], *[m2[n] for n in names])
    kinds = [dict(zip(names, res[1 + q * k:1 + (q + 1) * k])) for q in range(4)]
    return res[0], kinds


def _mix_forward(w, xs, modnb, between=None):
    w_main = w["w_in"]
    w_ba = jnp.pad(w["w_in"][NMAIN:], ((0, LANES - 2 * NH), (0, 0)))
    alog_l = _lanes(w["gdn_a_log"], NH)
    dt_l = _lanes(w["gdn_dt_bias"], NH)
    p_main, p_ba, hb1 = _fwd_in(xs, w["norm_mix_w"], modnb, w["b_ada"], w_main, w_ba)
    w_o, u_o, qg, kd, qk, cd, t_inv = _gdn_prep(p_main, p_ba, w["gdn_conv_w"], alog_l, dt_l)
    out_b, o_pre, s_in = _gdn_scan(w_o, u_o, qg, kd, qk, cd, p_main, w["gdn_norm_w"])
    conv_b = w["conv_b"] if between is None else _after(w["conv_b"], between(out_b))
    y_conv, out_a = _conf_fwd(p_main, w["conv_w"], conv_b, w["conv_gn_w"], w["conv_gn_b"])
    return dict(w_main=w_main, w_ba=w_ba, alog_l=alog_l, dt_l=dt_l, p_main=p_main, p_ba=p_ba, hb1=hb1,
                y_conv=y_conv, out_a=out_a, out_b=out_b, o_pre=o_pre, s_in=s_in, t_inv=t_inv)


def _ffn_stage(w, f, xs, tgt, modnb):
    x1, mix, oab = _fwd_out(f["out_a"], f["out_b"], xs, modnb, w["b_ada"], w["w_out"])
    hb2, act, pre, dx2, dffn, st_fwd = _ffn_forward(x1, tgt, modnb, w["b_ada"], w["norm_ffn_w"],
                                                    w["norm_final_w"], w["w_ffn_in"], w["w_ffn_out"])
    gw_ffn_out = _grad_w_ffn_out(act, dffn)
    df, dx1, st_bwd = _ffn_backward(dffn, pre, x1, dx2, modnb, w["b_ada"], w["norm_ffn_w"], w["w_ffn_in"],
                                    w["w_ffn_out"])
    gw_ffn_in = _grad_w_ffn_in(hb2, df)
    return dict(mix=mix, oab=oab, dx1=dx1, st_ffn=st_fwd + st_bwd, gw_ffn_in=gw_ffn_in, gw_ffn_out=gw_ffn_out)


def _out_backward(w, g, modnb):
    dmix, d_out_a, d_out_b, st_out = _bwd_out(g["dx1"], g["mix"], modnb, w["b_ada"], w["w_out"])
    return dict(d_out_a=d_out_a, d_out_b=d_out_b, st_out=st_out, gw_out=_grad_w("grad_w_out", g["oab"], dmix, 512))


def _heads_backward(w, f, a):
    dp_conf, st_conf = _conf_bwd(a["d_out_a"], f["y_conv"], f["p_main"], w["conv_w"], w["conv_gn_w"],
                                 w["conv_gn_b"])
    dp_gdn, dp_ba, st_gdn = _gdn_bwd(a["d_out_b"], f["o_pre"], f["s_in"], f["t_inv"], f["p_main"], f["p_ba"],
                                     w["gdn_conv_w"], f["alog_l"], f["dt_l"], w["gdn_norm_w"])
    gw_in = _grad_w_in(dp_conf, dp_gdn, dp_ba, f["hb1"])[:NIN]
    return dict(dp_conf=dp_conf, dp_gdn=dp_gdn, dp_ba=dp_ba, st_conf=st_conf, st_gdn=st_gdn, gw_in=gw_in,
                gw_conv=st_conf[0:KC], gw_gconv=st_gdn[0:KS])


def _in_backward(w, f, g, a, h, xs, modnb):
    st_out, st_conf, st_gdn, st_ffn = a["st_out"], h["st_conf"], h["st_gdn"], g["st_ffn"]
    grad_x, st_in = _bwd_in(h["dp_conf"], h["dp_gdn"], h["dp_ba"], xs, g["dx1"], w["norm_mix_w"], modnb,
                            w["b_ada"], f["w_main"], f["w_ba"])
    dmod = jnp.concatenate([st_in[0:1], st_in[1:2], st_out[0:1], st_ffn[2:3], st_ffn[3:4], st_ffn[1:2]], axis=1)
    small = jnp.concatenate([
        dmod.reshape(48, LANES), st_in[2:3].reshape(8, LANES), st_ffn[4:5].reshape(8, LANES),
        st_ffn[0:1].reshape(8, LANES), st_conf[31:32].reshape(4, LANES), st_conf[32:33].reshape(4, LANES),
        st_conf[33:34].reshape(4, LANES), st_gdn[4:5, 0:LANES],
        _lanes(st_gdn[5:6, NH:2 * NH]), _lanes(st_gdn[6:7, NH:2 * NH]), st_ffn[5:6, 0:LANES]], axis=0)
    return dict(grad_x=grad_x, small=small)


def _local(w, xs, tgt, modnb):
    f = _mix_forward(w, xs, modnb)
    g = _ffn_stage(w, f, xs, tgt, modnb)
    a = _out_backward(w, g, modnb)
    h = _heads_backward(w, f, a)
    b = _in_backward(w, f, g, a, h, xs, modnb)
    return dict(b, gw_in=h["gw_in"], gw_conv=h["gw_conv"], gw_gconv=h["gw_gconv"], gw_out=a["gw_out"],
                gw_ffn_in=g["gw_ffn_in"], gw_ffn_out=g["gw_ffn_out"])


def kernel(x, c, w_ada, b_ada, norm_mix_w, w_in, conv_w, conv_b, conv_gn_w, conv_gn_b, gdn_conv_w, gdn_a_log, gdn_dt_bias, gdn_norm_w, w_out, norm_ffn_w, w_ffn_in, w_ffn_out, norm_final_w, loss_target, m_w_ada, m_b_ada, m_norm_mix_w, m_w_in, m_conv_w, m_conv_b, m_conv_gn_w, m_conv_gn_b, m_gdn_conv_w, m_gdn_a_log, m_gdn_dt_bias, m_gdn_norm_w, m_w_out, m_norm_ffn_w, m_w_ffn_in, m_w_ffn_out, m_norm_final_w, v_w_ada, v_b_ada, v_norm_mix_w, v_w_in, v_conv_w, v_conv_b, v_conv_gn_w, v_conv_gn_b, v_gdn_conv_w, v_gdn_a_log, v_gdn_dt_bias, v_gdn_norm_w, v_w_out, v_norm_ffn_w, v_w_ffn_in, v_w_ffn_out, v_norm_final_w):
    me = 4 * lax.axis_index("x") + 2 * lax.axis_index("y") + lax.axis_index("c")
    xs = x.reshape(S, D)
    tgt = loss_target.reshape(S, D)

    g_c, g_cw, g_gcw = _exchange("gather_cond", [c, conv_w[0], gdn_conv_w[0]], [False] * 3)
    c_all = g_c.reshape(N_DEV, D)
    g_mod, mod_token = _exchange("gather_mod", [_mod_shard(c_all, w_ada[0])], [False], with_token=True)
    modnb = lax.dynamic_index_in_dim(g_mod, me, axis=1, keepdims=False).reshape(1, 6 * D)

    late = [w_out[0].astype(BF16), jnp.transpose(w_ffn_in[0]).astype(BF16), w_ffn_out[0].astype(BF16)]
    g_win, *late_lands = _gather_two_level(
        "gather_weights", [_after(jnp.transpose(w_in[0]), mod_token).astype(BF16)] + late, seed_only=(1, 2, 3))
    late_started = _exchange_start("gather_late_start", late, late_lands, [False] * 3, only=LEVEL_ONE)
    modnb = _after(modnb, late_started[-1])
    w = dict(b_ada=b_ada, norm_mix_w=norm_mix_w, conv_b=conv_b, conv_gn_w=conv_gn_w, conv_gn_b=conv_gn_b,
             gdn_a_log=gdn_a_log, gdn_dt_bias=gdn_dt_bias, gdn_norm_w=gdn_norm_w, norm_ffn_w=norm_ffn_w,
             norm_final_w=norm_final_w.reshape(1, D),
             conv_w=jnp.transpose(g_cw, (1, 0, 2)).reshape(KC, CW),
             gdn_conv_w=jnp.transpose(g_gcw, (1, 0, 2)).reshape(KS, 3 * GW),
             w_in=g_win.reshape(NIN, D))

    relay = {}

    def relay_late(out_b):
        _, late_landed = _exchange_wait("gather_late_wait", late_started, [False] * 3, (out_b,), only=LEVEL_ONE)
        relay["started"] = _relay_start("gather_late_relay_start", late_landed)
        return relay["started"][-1]

    f = _mix_forward(w, xs, modnb, relay_late)
    g_wout, g_wfi, g_wfo = _relay_wait("gather_late_relay_wait", relay["started"], (f["out_a"],))
    w.update(w_out=g_wout.reshape(D, D), w_ffn_in=g_wfi, w_ffn_out=g_wfo.reshape(4, FB, D))
    g = _ffn_stage(w, f, xs, tgt, modnb)

    ffn_grads = [g["gw_ffn_in"], g["gw_ffn_out"].reshape(N_DEV, DFF // N_DEV, D)]
    ffn_started = _exchange_start("scatter_ffn_start", ffn_grads,
                                  [lax.empty(a.shape, a.dtype) for a in ffn_grads], [True] * 2)
    a = _out_backward(w, g, _after(modnb, ffn_started[-1]))
    out_grads = [a["gw_out"].reshape(N_DEV, D // N_DEV, D)]
    out_started = _exchange_start("scatter_out_start", out_grads,
                                  [lax.empty(t.shape, t.dtype) for t in out_grads], [True])
    h = _heads_backward(dict(w, conv_gn_w=_after(w["conv_gn_w"], out_started[-1])), f, a)

    in_grads = [h["gw_in"].reshape(N_DEV, NIN // N_DEV, D),
                jnp.transpose(h["gw_conv"].reshape(KC, N_DEV, CW // N_DEV), (1, 0, 2)),
                jnp.transpose(h["gw_gconv"].reshape(KS, N_DEV, 3 * GW // N_DEV), (1, 0, 2))]
    in_started = _exchange_start("scatter_in_start", in_grads,
                                 [lax.empty(t.shape, t.dtype) for t in in_grads], [True] * 3)
    loc = _in_backward(w, f, g, a, h, xs, _after(modnb, in_started[-1]))
    small_started = _exchange_start("gather_small_start", [loc["small"]],
                                    [lax.empty((N_DEV, SMALL_ROWS, LANES), F32)], [False])

    def own(sent):
        return lax.dynamic_index_in_dim(sent, me, axis=0, keepdims=False)

    big = {}
    (sent_fi, sent_fo), (r_fi, r_fo) = _exchange_wait("scatter_ffn_wait", ffn_started, [True] * 2,
                                                         (small_started[-1],))
    big["w_ffn_in"] = [jnp.transpose(t) for t in _reduce_adam(
        "adam_w_ffn_in", r_fi, jnp.transpose(w_ffn_in[0]), jnp.transpose(m_w_ffn_in[0]),
        jnp.transpose(v_w_ffn_in[0]), own(sent_fi))]
    big["w_ffn_out"] = _reduce_adam("adam_w_ffn_out", r_fo, w_ffn_out[0], m_w_ffn_out[0], v_w_ffn_out[0],
                                    own(sent_fo))
    (sent_out,), (r_out,) = _exchange_wait("scatter_out_wait", out_started, [True], (big["w_ffn_out"][0],))
    big["w_out"] = _reduce_adam("adam_w_out", r_out, w_out[0], m_w_out[0], v_w_out[0], own(sent_out))

    (sent_small,), (r_small,) = _exchange_wait("gather_small_wait", small_started, [False], (big["w_out"][0],))
    slot = lax.broadcasted_iota(jnp.int32, (N_DEV, 1, 1), 0)
    g_small = jnp.where(slot == me, sent_small[None], r_small)
    def views(b_, nm_, nf_, nl_, cb_, gw_, gb_, gn_, al_, dt_):
        arrs = [b_, nm_, nf_, nl_, cb_, gw_, gb_, gn_, al_, dt_]
        return {nm: t.reshape(rows, lanes) for (nm, _, rows, lanes), t in zip(SMALL_LAYOUT, arrs)}

    loss_row, res = _adam_small(
        g_small,
        views(b_ada, norm_mix_w, norm_ffn_w, norm_final_w, conv_b, conv_gn_w, conv_gn_b, gdn_norm_w, gdn_a_log,
              gdn_dt_bias),
        views(m_b_ada, m_norm_mix_w, m_norm_ffn_w, m_norm_final_w, m_conv_b, m_conv_gn_w, m_conv_gn_b,
              m_gdn_norm_w, m_gdn_a_log, m_gdn_dt_bias),
        views(v_b_ada, v_norm_mix_w, v_norm_ffn_w, v_norm_final_w, v_conv_b, v_conv_gn_w, v_conv_gn_b,
              v_gdn_norm_w, v_gdn_a_log, v_gdn_dt_bias))
    loss = loss_row[0, 0]
    small_shapes = dict(b_ada=(1, 6 * D), norm_mix_w=(1, D), norm_ffn_w=(1, D), norm_final_w=(D,),
                        conv_b=(1, CW), conv_gn_w=(1, CW), conv_gn_b=(1, CW), gdn_norm_w=(1, DH),
                        gdn_a_log=(1, NH), gdn_dt_bias=(1, NH))
    res = [{nm: t.reshape(small_shapes[nm]) for nm, t in kind.items()} for kind in res]

    dmod_rows = g_small[:, 0:48, :].reshape(N_DEV, 6 * D)
    dmod_sh = lax.dynamic_slice_in_dim(dmod_rows, me * (6 * D // N_DEV), 6 * D // N_DEV, axis=1)

    big["w_ada"] = _ada_adam(c_all, dmod_sh, w_ada[0], m_w_ada[0], v_w_ada[0])
    (sent_in, sent_cw, sent_gcw), (r_in, r_cw, r_gcw) = _exchange_wait(
        "scatter_in_wait", in_started, [True] * 3, (big["w_ada"][0],))
    big["w_in"] = [jnp.transpose(t) for t in _reduce_adam(
        "adam_w_in", r_in, jnp.transpose(w_in[0]), jnp.transpose(m_w_in[0]), jnp.transpose(v_w_in[0]),
        own(sent_in))]
    big["conv_w"] = _reduce_adam("adam_conv_w", r_cw, conv_w[0], m_conv_w[0], v_conv_w[0], own(sent_cw))
    big["gdn_conv_w"] = _reduce_adam("adam_gdn_conv_w", r_gcw, gdn_conv_w[0], m_gdn_conv_w[0], v_gdn_conv_w[0],
                                     own(sent_gcw))
    outs = [loss, loc["grad_x"].reshape(1, S, D)]
    for kind in range(4):
        for nm in WEIGHT_NAMES:
            outs.append(big[nm][kind][None] if nm in big else res[kind][nm])
    return tuple(outs)
```

```python
import functools

import jax
import jax.numpy as jnp
from jax import lax
from jax.experimental import pallas as pl
from jax.experimental.pallas import tpu as pltpu

F32 = jnp.float32
BF16 = jnp.bfloat16
HI = lax.Precision.HIGHEST
MESH = pl.DeviceIdType.MESH

N_DEV = 8
S = 2048
D = 1024
TM = 256
NT = S // TM
CW = 512
KC = 31
NG = 8
GSZ = CW // NG
HALO = 32
GW = 512
NH = 4
DH = 128
KS = 4
SH = 8
CL = 64
NCH = S // CL
NMAIN = 2 * CW + 4 * GW
NIN = NMAIN + 2 * NH
DFF = 2816
FB = DFF // 4
EPS = 1e-6
QSCALE = DH ** -0.5
LANES = 128
SMALL_ROWS = 88

ADAM_LR = 0.001
ADAM_B1 = 0.9
ADAM_B2 = 0.999
ADAM_EPS = 1e-08
ADAM_WD = 0.01
ADAM_STEP = 10
BC1 = 1.0 - ADAM_B1 ** ADAM_STEP
BC2 = 1.0 - ADAM_B2 ** ADAM_STEP

MIB = 1024 * 1024
VMEM_LIMIT_MIB = 32


def _params(limit_mib=VMEM_LIMIT_MIB, **kw):
    return pltpu.CompilerParams(vmem_limit_bytes=limit_mib * MIB, **kw)


def _sig(x):
    return jax.nn.sigmoid(x)


GP = BF16


def _operands(a, b, prec):
    if prec is BF16:
        return a.astype(BF16), b.astype(BF16), None
    return a, b, prec


def _dot(a, b, prec=None):
    a, b, prec = _operands(a, b, prec)
    return jnp.dot(a, b, preferred_element_type=F32, precision=prec)


def _dot_nt(a, b, prec=None):
    a, b, prec = _operands(a, b, prec)
    return lax.dot_general(a, b, (((1,), (1,)), ((), ())), preferred_element_type=F32, precision=prec)


def _dot_tn(a, b, prec=None):
    a, b, prec = _operands(a, b, prec)
    return lax.dot_general(a, b, (((0,), (0,)), ((), ())), preferred_element_type=F32, precision=prec)


def _lockstep(gens):
    gens = list(gens)
    while gens:
        alive = []
        for g in gens:
            try:
                next(g)
                alive.append(g)
            except StopIteration:
                pass
        gens = alive


def _rowsum(x):
    return jnp.sum(x, axis=-1, keepdims=True)


def _colsum(x):
    return jnp.sum(x, axis=0, keepdims=True)


def _mod(mod_ref, b_ref, k):
    return mod_ref[:, k * D:(k + 1) * D] + b_ref[:, k * D:(k + 1) * D]


def _const(shape):
    nd = len(shape)
    return pl.BlockSpec(shape, lambda *_: (0,) * nd)


def _const1(shape):
    nd = len(shape)
    return pl.BlockSpec(shape, lambda *_: (0,) * nd, pipeline_mode=pl.Buffered(1))


PEER_FLIPS = [(dx, dy, dc) for dx in (0, 1) for dy in (0, 1) for dc in (0, 1)][1:]


def _after(x, token):
    return x + token[0:1, 0:1].astype(x.dtype).reshape((1,) * x.ndim)


def _exchange(name, srcs, per_dest, seed_only=(), with_token=False):
    n = len(srcs)
    out_shape = []
    for a, pd in zip(srcs, per_dest):
        blk = a.shape[1:] if pd else a.shape
        out_shape.append(jax.ShapeDtypeStruct((N_DEV,) + tuple(blk), a.dtype))

    def body(*refs):
        src = refs[:n]
        dst = refs[n:2 * n]
        send_sems, recv_sems, local_sems = refs[-3:]
        if with_token:
            refs[2 * n][...] = jnp.zeros((8, LANES), F32)
        x, y, c = lax.axis_index("x"), lax.axis_index("y"), lax.axis_index("c")
        me = 4 * x + 2 * y + c

        def piece(i, j):
            return src[i].at[j] if per_dest[i] else src[i]

        copies = []
        for k, (dx, dy, dc) in enumerate(PEER_FLIPS):
            px = 1 - x if dx else x
            py = 1 - y if dy else y
            pc = 1 - c if dc else c
            pj = 4 * px + 2 * py + pc
            for i in range(n):
                if i in seed_only:
                    continue
                cp = pltpu.make_async_remote_copy(
                    src_ref=piece(i, pj), dst_ref=dst[i].at[me],
                    send_sem=send_sems.at[k * n + i], recv_sem=recv_sems.at[k * n + i],
                    device_id=(px, py, pc), device_id_type=MESH)
                cp.start()
                arrive = pltpu.make_async_remote_copy(
                    src_ref=piece(i, pj), dst_ref=dst[i].at[pj],
                    send_sem=send_sems.at[k * n + i], recv_sem=recv_sems.at[k * n + i],
                    device_id=(px, py, pc), device_id_type=MESH)
                copies.append((cp, arrive))
        own = []
        for i in range(n):
            lc = pltpu.make_async_copy(piece(i, me), dst[i].at[me], local_sems.at[i])
            lc.start()
            own.append(lc)
        for cp, arrive in copies:
            arrive.wait_recv()
        for cp, arrive in copies:
            cp.wait_send()
        for lc in own:
            lc.wait()

    any_spec = pl.BlockSpec(memory_space=pl.ANY)
    out_specs = [any_spec] * n
    if with_token:
        out_shape.append(jax.ShapeDtypeStruct((8, LANES), F32))
        out_specs.append(pl.BlockSpec(memory_space=pltpu.VMEM))
    return pl.pallas_call(
        body, name=name, out_shape=tuple(out_shape),
        in_specs=[any_spec] * n, out_specs=tuple(out_specs),
        scratch_shapes=[pltpu.SemaphoreType.DMA((7 * n,)), pltpu.SemaphoreType.DMA((7 * n,)),
                        pltpu.SemaphoreType.DMA((n,))],
        compiler_params=pltpu.CompilerParams(has_side_effects=True),
    )(*srcs)


CHIP_FLIPS = [(0, 1), (1, 0), (1, 1)]
LEVEL_ONE = [k for k, (dx, dy, dc) in enumerate(PEER_FLIPS) if (dx, dy, dc) == (0, 0, 1) or dc == 0]


def _chip_peers(x, y):
    return [(1 - x if dx else x, 1 - y if dy else y) for dx, dy in CHIP_FLIPS]


def _gather_two_level(name, srcs, seed_only=()):
    n = len(srcs)
    live = [i for i in range(n) if i not in seed_only]

    def body(*refs):
        src, dst = refs[:n], refs[n:2 * n]
        send_sems, recv_sems, local_sems = refs[2 * n:2 * n + 3]
        bounce = refs[2 * n + 3:]
        x, y, c = lax.axis_index("x"), lax.axis_index("y"), lax.axis_index("c")
        me = 4 * x + 2 * y + c
        sibling = (x, y, 1 - c)
        chips = _chip_peers(x, y)

        def copy(k, i, src_ref, slot, to):
            return pltpu.make_async_remote_copy(
                src_ref=src_ref, dst_ref=dst[i].at[slot], send_sem=send_sems.at[k * n + i],
                recv_sem=recv_sems.at[k * n + i], device_id=to, device_id_type=MESH)

        first = []
        for i in live:
            first.append(copy(0, i, src[i], me, sibling))
            first += [copy(1 + j, i, src[i], me, (px, py, c)) for j, (px, py) in enumerate(chips)]
        for cp in first:
            cp.start()
        up = [pltpu.make_async_copy(src[i], bounce[i], local_sems.at[i]) for i in range(n)]
        for cp in up:
            cp.start()
        for cp in up:
            cp.wait()
        own = [pltpu.make_async_copy(bounce[i], dst[i].at[me], local_sems.at[i]) for i in range(n)]
        for cp in own:
            cp.start()
        passed = []
        for j, (px, py) in enumerate(chips):
            slot = 4 * px + 2 * py + c
            for i in live:
                copy(1 + j, i, src[i], slot, (px, py, c)).wait_recv()
                fwd = copy(4 + j, i, dst[i].at[slot], slot, sibling)
                fwd.start()
                passed.append(fwd)
        for i in live:
            copy(0, i, src[i], 4 * x + 2 * y + 1 - c, sibling).wait_recv()
            for j, (px, py) in enumerate(chips):
                copy(4 + j, i, src[i], 4 * px + 2 * py + 1 - c, sibling).wait_recv()
        for cp in first + passed:
            cp.wait_send()
        for cp in own:
            cp.wait()

    any_spec = pl.BlockSpec(memory_space=pl.ANY)
    return pl.pallas_call(
        body, name=name, out_shape=tuple(jax.ShapeDtypeStruct((N_DEV,) + a.shape, a.dtype) for a in srcs),
        in_specs=[any_spec] * n, out_specs=tuple([any_spec] * n),
        scratch_shapes=[pltpu.SemaphoreType.DMA((7 * n,)), pltpu.SemaphoreType.DMA((7 * n,)),
                        pltpu.SemaphoreType.DMA((n,))] + [pltpu.VMEM(a.shape, a.dtype) for a in srcs],
        compiler_params=pltpu.CompilerParams(has_side_effects=True),
    )(*srcs)


def _relay_copy(land, sems, i, n, j, slot, sibling):
    send_sems, recv_sems = sems
    return pltpu.make_async_remote_copy(
        src_ref=land[i].at[slot], dst_ref=land[i].at[slot], send_sem=send_sems.at[j * n + i],
        recv_sem=recv_sems.at[j * n + i], device_id=sibling, device_id_type=MESH)


def _relay_start(name, lands):
    n = len(lands)

    def body(*refs):
        land = refs[:n]
        sems = refs[n], refs[n + 1]
        x, y, c = lax.axis_index("x"), lax.axis_index("y"), lax.axis_index("c")
        for j, (px, py) in enumerate(_chip_peers(x, y)):
            for i in range(n):
                _relay_copy(land, sems, i, n, j, 4 * px + 2 * py + c, (x, y, 1 - c)).start()
        refs[-1][...] = jnp.zeros((8, LANES), F32)

    return pl.pallas_call(
        body, name=name,
        out_shape=(pltpu.SemaphoreType.DMA((3 * n,)), pltpu.SemaphoreType.DMA((3 * n,)),
                   *[pltpu.HBM(a.shape, a.dtype) for a in lands], jax.ShapeDtypeStruct((8, LANES), F32)),
        in_specs=[HBM_SPEC] * n,
        out_specs=(SEM_SPEC, SEM_SPEC, *[HBM_SPEC] * n, pl.BlockSpec(memory_space=pltpu.VMEM)),
        input_output_aliases={i: 2 + i for i in range(n)},
        compiler_params=pltpu.CompilerParams(has_side_effects=DATAFLOW),
    )(*[pltpu.with_memory_space_constraint(a, pltpu.HBM) for a in lands])


def _relay_wait(name, started, after):
    n = len(started) - 3
    arrays = list(started[2:2 + n])

    def body(*refs):
        land = refs[:n]
        sems = refs[n], refs[n + 1]
        x, y, c = lax.axis_index("x"), lax.axis_index("y"), lax.axis_index("c")
        for j, (px, py) in enumerate(_chip_peers(x, y)):
            for i in range(n):
                _relay_copy(land, sems, i, n, j, 4 * px + 2 * py + c, (x, y, 1 - c)).wait_send()
                _relay_copy(land, sems, i, n, j, 4 * px + 2 * py + 1 - c, (x, y, 1 - c)).wait_recv()

    return pl.pallas_call(
        body, name=name,
        out_shape=tuple(pltpu.HBM(a.shape, a.dtype) for a in arrays),
        in_specs=[HBM_SPEC] * n + [SEM_SPEC, SEM_SPEC] + [pl.BlockSpec(memory_space=pl.ANY)] * len(after),
        out_specs=tuple([HBM_SPEC] * n),
        input_output_aliases={i: i for i in range(n)},
        compiler_params=pltpu.CompilerParams(has_side_effects=DATAFLOW),
    )(*arrays, started[0], started[1], *after)


HBM_SPEC = pl.BlockSpec(memory_space=pltpu.HBM)
SEM_SPEC = pl.BlockSpec(memory_space=pltpu.SEMAPHORE)
DATAFLOW = pltpu.SideEffectType.DATAFLOW_SIDE_EFFECTING


def _peers(only=None):
    x, y, c = lax.axis_index("x"), lax.axis_index("y"), lax.axis_index("c")
    out = []
    for k, (dx, dy, dc) in enumerate(PEER_FLIPS):
        if only is not None and k not in only:
            continue
        px = 1 - x if dx else x
        py = 1 - y if dy else y
        pc = 1 - c if dc else c
        out.append((k, (px, py, pc), 4 * px + 2 * py + pc))
    return 4 * x + 2 * y + c, out


def _exchange_start(name, srcs, lands, per_dest, only=None):
    n = len(srcs)

    def body(*refs):
        src, land = refs[:n], refs[n:2 * n]
        send_sems, recv_sems = refs[2 * n], refs[2 * n + 1]
        token = refs[-1]
        me, peers = _peers(only)
        for k, peer, pj in peers:
            for i in range(n):
                pltpu.make_async_remote_copy(
                    src_ref=src[i].at[pj] if per_dest[i] else src[i], dst_ref=land[i].at[me],
                    send_sem=send_sems.at[k * n + i], recv_sem=recv_sems.at[k * n + i],
                    device_id=peer, device_id_type=MESH).start()
        token[...] = jnp.zeros((8, LANES), F32)

    arrays = list(srcs) + list(lands)
    return pl.pallas_call(
        body, name=name,
        out_shape=(pltpu.SemaphoreType.DMA((7 * n,)), pltpu.SemaphoreType.DMA((7 * n,)),
                   *[pltpu.HBM(a.shape, a.dtype) for a in arrays], jax.ShapeDtypeStruct((8, LANES), F32)),
        in_specs=[HBM_SPEC] * (2 * n),
        out_specs=(SEM_SPEC, SEM_SPEC, *[HBM_SPEC] * (2 * n), pl.BlockSpec(memory_space=pltpu.VMEM)),
        input_output_aliases={i: 2 + i for i in range(2 * n)},
        compiler_params=pltpu.CompilerParams(has_side_effects=DATAFLOW),
    )(*[pltpu.with_memory_space_constraint(a, pltpu.HBM) for a in arrays])


def _exchange_wait(name, started, per_dest, after, only=None):
    n = (len(started) - 3) // 2
    send_sems, recv_sems = started[0], started[1]
    arrays = list(started[2:2 + 2 * n])

    def body(*refs):
        src, land = refs[:n], refs[n:2 * n]
        send, recv = refs[2 * n], refs[2 * n + 1]
        me, peers = _peers(only)
        for k, peer, pj in peers:
            for i in range(n):
                cp = pltpu.make_async_remote_copy(
                    src_ref=src[i].at[pj] if per_dest[i] else src[i], dst_ref=land[i].at[pj],
                    send_sem=send.at[k * n + i], recv_sem=recv.at[k * n + i],
                    device_id=peer, device_id_type=MESH)
                cp.wait_send()
                cp.wait_recv()

    outs = pl.pallas_call(
        body, name=name,
        out_shape=tuple(pltpu.HBM(a.shape, a.dtype) for a in arrays),
        in_specs=[HBM_SPEC] * (2 * n) + [SEM_SPEC, SEM_SPEC] + [pl.BlockSpec(memory_space=pl.ANY)] * len(after),
        out_specs=tuple([HBM_SPEC] * (2 * n)),
        input_output_aliases={i: i for i in range(2 * n)},
        compiler_params=pltpu.CompilerParams(has_side_effects=DATAFLOW),
    )(*arrays, send_sems, recv_sems, *after)
    return outs[:n], outs[n:]


def _mod_shard(c_all, w_ada):
    def body(c_ref, w_ref, o_ref):
        cv = c_ref[...]
        ca = cv * _sig(cv)
        o_ref[...] = _dot(ca.astype(BF16), w_ref[...].astype(BF16))

    return pl.pallas_call(
        body, name="mod_shard", out_shape=jax.ShapeDtypeStruct((N_DEV, w_ada.shape[1]), F32),
        compiler_params=_params(),
    )(c_all, w_ada)


def _fwd_in(x, nw1, modnb, bada, w_main, w_ba):
    def body(x_ref, nw_ref, mod_ref, b_ref, wm_ref, wb_ref, pm_ref, pb_ref, hb_ref):
        xv = x_ref[...]
        r = lax.rsqrt(jnp.mean(xv * xv, axis=-1, keepdims=True) + EPS)
        h = (xv * r * nw_ref[...]) * (1.0 + _mod(mod_ref, b_ref, 1)) + _mod(mod_ref, b_ref, 0)
        hb = h.astype(BF16)
        hb_ref[...] = hb
        pm_ref[...] = _dot_nt(hb, wm_ref[...])
        pb_ref[...] = _dot_nt(hb, wb_ref[...])

    return pl.pallas_call(
        body, name="fwd_in", grid=(NT,),
        in_specs=[pl.BlockSpec((TM, D), lambda i: (i, 0)), _const((1, D)), _const((1, 6 * D)), _const((1, 6 * D)),
                  _const((NMAIN, D)), _const((LANES, D))],
        out_specs=(pl.BlockSpec((TM, NMAIN), lambda i: (i, 0)), pl.BlockSpec((TM, LANES), lambda i: (i, 0)),
                   pl.BlockSpec((TM, D), lambda i: (i, 0))),
        out_shape=(jax.ShapeDtypeStruct((S, NMAIN), F32), jax.ShapeDtypeStruct((S, LANES), F32),
                   jax.ShapeDtypeStruct((S, D), BF16)),
        compiler_params=_params(dimension_semantics=("arbitrary",)),
    )(x, nw1, modnb, bada, w_main, w_ba)


def _group_mean_matrix():
    ii = lax.broadcasted_iota(jnp.int32, (CW, CW), 0) // GSZ
    jj = lax.broadcasted_iota(jnp.int32, (CW, CW), 1) // GSZ
    return jnp.where(ii == jj, 1.0 / GSZ, 0.0).astype(F32)


SUB = 8
SHIFT_ROWS = HALO + TM - SUB


def _fill_shifted(buf, sh):
    for b in range(1, SUB):
        sh[b - 1] = buf[b:b + SHIFT_ROWS, :]


def _rows_at(buf, sh, off):
    a, b = divmod(off, SUB)
    if b == 0:
        return buf[off:off + TM, :]
    return sh[b - 1, SUB * a:SUB * a + TM, :]


def _group_mean(x, pm):
    hi = x.astype(BF16)
    r1 = x - hi.astype(F32)
    mid = r1.astype(BF16)
    lo = (r1 - mid.astype(F32)).astype(BF16)
    return _dot(hi, pm) + _dot(mid, pm) + _dot(lo, pm)


def _conf_fwd(p_main, conv_w, conv_b, gn_w, gn_b):
    def body(a_ref, g_ref, w_ref, b_ref, gw_ref, gb_ref, y_ref, oa_ref, ubuf, ush):
        i = pl.program_id(0)

        @pl.when(i == 0)
        def _():
            ubuf[0:HALO, :] = jnp.zeros((HALO, CW), F32)

        ubuf[HALO:HALO + TM, :] = a_ref[...] * _sig(g_ref[...])
        _fill_shifted(ubuf, ush)
        acc = jnp.zeros((TM, CW), F32) + b_ref[...]
        for k in range(KC):
            acc = acc + w_ref[k:k + 1, :] * _rows_at(ubuf, ush, HALO - (KC - 1) + k)
        y_ref[...] = acc
        ubuf[0:HALO, :] = ubuf[TM:TM + HALO, :]
        pm = _group_mean_matrix().astype(BF16)
        dlt = acc - _group_mean(acc, pm)
        var = _group_mean(dlt * dlt, pm)
        o = dlt * lax.rsqrt(var + EPS) * gw_ref[...] + gb_ref[...]
        oa_ref[...] = o * _sig(o)

    return pl.pallas_call(
        body, name="conf_fwd", grid=(NT,),
        in_specs=[pl.BlockSpec((TM, CW), lambda i: (i, 0)), pl.BlockSpec((TM, CW), lambda i: (i, 1)),
                  _const((KC, CW)), _const((1, CW)), _const((1, CW)), _const((1, CW))],
        out_specs=(pl.BlockSpec((TM, CW), lambda i: (i, 0)), pl.BlockSpec((TM, CW), lambda i: (i, 0))),
        out_shape=(jax.ShapeDtypeStruct((S, CW), F32), jax.ShapeDtypeStruct((S, CW), F32)),
        scratch_shapes=[pltpu.VMEM((HALO + TM, CW), F32), pltpu.VMEM((SUB - 1, SHIFT_ROWS, CW), F32)],
        compiler_params=_params(dimension_semantics=("arbitrary",)),
    )(p_main, p_main, conv_w, conv_b, gn_w, gn_b)


def _tri_iota():
    ii = lax.broadcasted_iota(jnp.int32, (CL, CL), 0)
    jj = lax.broadcasted_iota(jnp.int32, (CL, CL), 1)
    return ii, jj


def _gdn_gates(ba, alog_l, dt_l):
    beta_all = _sig(ba)
    xg = ba + dt_l
    sp = jnp.maximum(xg, 0.0) + jnp.log(1.0 + jnp.exp(-jnp.abs(xg)))
    neg_a = -jnp.exp(alog_l)
    return beta_all, neg_a * sp, xg, neg_a


def _ones_dot(ones, x):
    hi = x.astype(BF16)
    r1 = x - hi.astype(F32)
    mid = r1.astype(BF16)
    lo = (r1 - mid.astype(F32)).astype(BF16)
    return _dot(ones, hi) + _dot(ones, mid) + _dot(ones, lo)


def _gdn_cumsum(g_all):
    ii, jj = _tri_iota()
    low = jnp.where(ii >= jj, 1.0, 0.0).astype(BF16)
    gcum = _ones_dot(low, g_all)
    return gcum, jnp.transpose(gcum)


def _split(x):
    hi = x.astype(BF16)
    return hi, (x - hi.astype(F32)).astype(BF16)


def _dot_split(a, b):
    (ah, al), (bh, bl) = a, b
    return _dot(ah, bh) + (_dot(ah, bl) + _dot(al, bh))


def _unit_lower_inverses(mats):
    ii, jj = _tri_iota()
    eye = jnp.where(ii == jj, 1.0, 0.0).astype(F32)
    ts = [eye - a for a in mats]
    ps = [_dot_split(s, s) for s in map(_split, mats)]
    for _ in range(4):
        sp = [_split(p) for p in ps]
        ts = [t + _dot_split(_split(t), s) for t, s in zip(ts, sp)]
        ps = [_dot_split(s, s) for s in sp]
    return [t + _dot_split(_split(t), _split(p)) for t, p in zip(ts, ps)]


def _head_terms(qh, kh, beta, gcol, grow):
    ii, jj = _tri_iota()
    causal = ii >= jj
    strict = ii > jj
    rq = lax.rsqrt(_rowsum(qh * qh) + EPS)
    rk = lax.rsqrt(_rowsum(kh * kh) + EPS)
    qn = qh * rq
    kn = kh * rk
    qs = qn * QSCALE
    decay = jnp.where(causal, jnp.exp(jnp.where(causal, gcol - grow, 0.0)), 0.0)
    gam = jnp.exp(gcol)
    gl = gcol[CL - 1:CL, :]
    kds = jnp.exp(gl - gcol)
    cd = jnp.exp(gl)
    kb = kn * beta
    a = jnp.where(strict, _dot_nt(kb, kn, GP) * decay, 0.0)
    qk = jnp.where(causal, _dot_nt(qs, kn, GP) * decay, 0.0)
    return dict(rq=rq, rk=rk, qn=qn, kn=kn, qs=qs, decay=decay, gam=gam, kds=kds, cd=cd, kb=kb, a=a, qk=qk,
                causal=causal, strict=strict)


def _short_conv(w_ref, buf, rows=CL):
    acc = w_ref[0:1, :] * buf[SH - KS + 1:SH - KS + 1 + rows, :]
    for k in range(1, KS):
        off = SH - (KS - 1) + k
        acc = acc + w_ref[k:k + 1, :] * buf[off:off + rows, :]
    return acc


CPS = 4
TG = CPS * CL


def _gdn_prep(p_main, p_ba, gdn_conv_w, alog_l, dt_l):
    def body(q_ref, k_ref, v_ref, qh_ref, kh_ref, vh_ref, ba_ref, w_ref, al_ref, dt_ref,
             wo_ref, uo_ref, qg_ref, kd_ref, qk_ref, cd_ref, t_ref, xbuf):
        i = pl.program_id(0)
        first = i == 0
        xbuf[0:SH, 0:GW] = jnp.where(first, 0.0, qh_ref[...])
        xbuf[0:SH, GW:2 * GW] = jnp.where(first, 0.0, kh_ref[...])
        xbuf[0:SH, 2 * GW:3 * GW] = jnp.where(first, 0.0, vh_ref[...])
        xbuf[SH:SH + TG, 0:GW] = q_ref[...]
        xbuf[SH:SH + TG, GW:2 * GW] = k_ref[...]
        xbuf[SH:SH + TG, 2 * GW:3 * GW] = v_ref[...]
        conv = _short_conv(w_ref, xbuf, TG)
        qkv = conv * _sig(conv)
        beta_all, g_all, _, _ = _gdn_gates(ba_ref[...], al_ref[...], dt_ref[...])
        lane = lax.broadcasted_iota(jnp.int32, (8, LANES), 1)
        cums = [_gdn_cumsum(g_all[cc * CL:(cc + 1) * CL, :]) for cc in range(CPS)]
        pairs = [(cc, h) for cc in range(CPS) for h in range(NH)]
        terms, vbs = [], []
        for cc, h in pairs:
            r0, lo = cc * CL, h * DH
            beta = beta_all[r0:r0 + CL, h:h + 1]
            gcum, gcum_t = cums[cc]
            terms.append(_head_terms(qkv[r0:r0 + CL, lo:lo + DH], qkv[r0:r0 + CL, GW + lo:GW + lo + DH], beta,
                                     gcum[:, NH + h:NH + h + 1], gcum_t[NH + h:NH + h + 1, :]))
            vbs.append(qkv[r0:r0 + CL, 2 * GW + lo:2 * GW + lo + DH] * beta)
        invs = _unit_lower_inverses([f["a"] for f in terms])
        cds = [jnp.zeros((8, LANES), F32) for _ in range(CPS)]
        for (cc, h), f, t, vb in zip(pairs, terms, invs, vbs):
            r0, lo = cc * CL, h * DH
            t_ref[cc, h] = t
            uo_ref[r0:r0 + CL, lo:lo + DH] = _dot(t, vb, GP)
            wo_ref[r0:r0 + CL, lo:lo + DH] = _dot(t, f["kb"] * f["gam"], GP).astype(BF16)
            qg_ref[r0:r0 + CL, lo:lo + DH] = (f["qs"] * f["gam"]).astype(BF16)
            kd_ref[r0:r0 + CL, lo:lo + DH] = (f["kn"] * f["kds"]).astype(BF16)
            qk_ref[cc, h] = f["qk"].astype(BF16)
            cds[cc] = cds[cc] + jnp.where(lane == h, f["cd"], 0.0)
        for cc in range(CPS):
            cd_ref[cc] = cds[cc]

    col = lambda j: pl.BlockSpec((TG, GW), lambda i: (i, j))
    halo = lambda j: pl.BlockSpec((SH, GW), lambda i: (jnp.maximum(i * (TG // SH) - 1, 0), j))
    tile = lambda: pl.BlockSpec((TG, GW), lambda i: (i, 0))
    sq = lambda: pl.BlockSpec((CPS, NH, CL, CL), lambda i: (i, 0, 0, 0))
    return pl.pallas_call(
        body, name="gdn_prep", grid=(NCH // CPS,),
        in_specs=[col(2), col(3), col(4), halo(2), halo(3), halo(4), pl.BlockSpec((TG, LANES), lambda i: (i, 0)),
                  _const((KS, 3 * GW)), _const((1, LANES)), _const((1, LANES))],
        out_specs=(tile(), tile(), tile(), tile(), sq(), pl.BlockSpec((CPS, 8, LANES), lambda i: (i, 0, 0)), sq()),
        out_shape=(jax.ShapeDtypeStruct((S, GW), BF16), jax.ShapeDtypeStruct((S, GW), F32),
                   jax.ShapeDtypeStruct((S, GW), BF16), jax.ShapeDtypeStruct((S, GW), BF16),
                   jax.ShapeDtypeStruct((NCH, NH, CL, CL), BF16), jax.ShapeDtypeStruct((NCH, 8, LANES), F32),
                   jax.ShapeDtypeStruct((NCH, NH, CL, CL), F32)),
        scratch_shapes=[pltpu.VMEM((SH + TG, 3 * GW), F32)],
        compiler_params=_params(dimension_semantics=("arbitrary",)),
    )(p_main, p_main, p_main, p_main, p_main, p_main, p_ba, gdn_conv_w, alog_l, dt_l)


def _gdn_scan(w_o, u_o, qg, kd, qk, cd, p_main, gdn_nw):
    def body(w_ref, u_ref, qg_ref, kd_ref, qk_ref, cd_ref, z_ref, nw_ref, ob_ref, o_ref, sin_ref, state):
        n = pl.program_id(0)

        @pl.when(n == 0)
        def _():
            state[...] = jnp.zeros((NH, DH, DH), F32)

        def head(cc, h):
            rows, lo = pl.ds(cc * CL, CL), h * DH
            st = state[h]
            sin_ref[cc, h] = st
            sb = st.astype(BF16)
            v_new = u_ref[rows, lo:lo + DH] - _dot(w_ref[rows, lo:lo + DH], sb)
            yield
            vb = v_new.astype(BF16)
            o = _dot(qg_ref[rows, lo:lo + DH], sb) + _dot(qk_ref[cc, h], vb)
            state[h] = st * cd_ref[cc, 0:1, h:h + 1] + _dot_tn(kd_ref[rows, lo:lo + DH], vb)
            yield
            o_ref[rows, lo:lo + DH] = o
            r = lax.rsqrt(jnp.mean(o * o, axis=-1, keepdims=True) + EPS)
            zh = z_ref[rows, lo:lo + DH]
            ob_ref[rows, lo:lo + DH] = o * r * nw_ref[...] * (zh * _sig(zh))

        for cc in range(CPS):
            _lockstep(head(cc, h) for h in range(NH))

    tile = lambda: pl.BlockSpec((TG, GW), lambda n: (n, 0))
    return pl.pallas_call(
        body, name="gdn_scan", grid=(NCH // CPS,),
        in_specs=[tile(), tile(), tile(), tile(), pl.BlockSpec((CPS, NH, CL, CL), lambda n: (n, 0, 0, 0)),
                  pl.BlockSpec((CPS, 8, LANES), lambda n: (n, 0, 0)), pl.BlockSpec((TG, GW), lambda n: (n, 5)),
                  _const((1, DH))],
        out_specs=(tile(), tile(), pl.BlockSpec((CPS, NH, DH, DH), lambda n: (n, 0, 0, 0))),
        out_shape=(jax.ShapeDtypeStruct((S, GW), F32), jax.ShapeDtypeStruct((S, GW), F32),
                   jax.ShapeDtypeStruct((NCH, NH, DH, DH), F32)),
        scratch_shapes=[pltpu.VMEM((NH, DH, DH), F32)],
        compiler_params=_params(dimension_semantics=("arbitrary",)),
    )(w_o, u_o, qg, kd, qk, cd, p_main, gdn_nw)


def _fwd_out(out_a, out_b, x, modnb, bada, w_out):
    def body(oa_ref, ob_ref, x_ref, mod_ref, b_ref, w_ref, x1_ref, mix_ref, oab_ref):
        oa = oa_ref[...].astype(BF16)
        ob = ob_ref[...].astype(BF16)
        oab_ref[:, 0:CW] = oa
        oab_ref[:, CW:D] = ob
        mix = _dot(oa, w_ref[0:CW, :]) + _dot(ob, w_ref[CW:D, :])
        mix_ref[...] = mix
        x1_ref[...] = x_ref[...] + _mod(mod_ref, b_ref, 2) * mix

    tile = lambda w: pl.BlockSpec((TM, w), lambda i: (i, 0))
    return pl.pallas_call(
        body, name="fwd_out", grid=(NT,),
        in_specs=[tile(CW), tile(GW), tile(D), _const((1, 6 * D)), _const((1, 6 * D)), _const((D, D))],
        out_specs=(tile(D), tile(D), tile(D)),
        out_shape=(jax.ShapeDtypeStruct((S, D), F32), jax.ShapeDtypeStruct((S, D), F32),
                   jax.ShapeDtypeStruct((S, D), BF16)),
        compiler_params=_params(dimension_semantics=("arbitrary",)),
    )(out_a, out_b, x, modnb, bada, w_out)


FFN_STATS = 8


def _ffn_forward(x1, tgt, modnb, bada, nw2, nfw, w_fi, w_fo):
    def body(x1_ref, tgt_ref, mod_ref, b_ref, nw2_ref, nfw_ref, wi_ref, wo_ref,
             hb_ref, act_ref, pre_ref, dx2_ref, dffn_ref, st_ref):
        i = pl.program_id(0)

        @pl.when(i == 0)
        def _():
            st_ref[...] = jnp.zeros((FFN_STATS, D), F32)

        sh2, sc2, gt2 = _mod(mod_ref, b_ref, 3), _mod(mod_ref, b_ref, 4), _mod(mod_ref, b_ref, 5)
        x1v = x1_ref[...]
        r2 = lax.rsqrt(jnp.mean(x1v * x1v, axis=-1, keepdims=True) + EPS)
        hb = ((x1v * r2 * nw2_ref[...]) * (1.0 + sc2) + sh2).astype(BF16)
        hb_ref[...] = hb
        ffn = jnp.zeros((TM, D), F32)
        for j in range(4):
            fgj = _dot_nt(hb, wi_ref[j])
            fuj = _dot_nt(hb, wi_ref[j + 4])
            pre_ref[j] = fgj.astype(BF16)
            pre_ref[j + 4] = fuj.astype(BF16)
            aj = (fgj * _sig(fgj) * fuj).astype(BF16)
            act_ref[j] = aj
            ffn = ffn + _dot(aj, wo_ref[j])
        x2 = x1v + gt2 * ffn
        r3 = lax.rsqrt(jnp.mean(x2 * x2, axis=-1, keepdims=True) + EPS)
        xr3 = x2 * r3
        err = xr3 * nfw_ref[...] - tgt_ref[...]
        loss = 0.5 * jnp.sum(jnp.mean(err * err, axis=-1, keepdims=True), axis=0, keepdims=True)
        dy = err * (1.0 / D)
        st_ref[0:1, :] += _colsum(dy * xr3)
        dyr = dy * nfw_ref[...]
        dx2 = r3 * (dyr - xr3 * jnp.mean(dyr * xr3, axis=-1, keepdims=True))
        st_ref[1:2, :] += _colsum(dx2 * ffn)
        st_ref[5:6, :] += jnp.broadcast_to(loss, (1, D))
        dx2_ref[...] = dx2
        dffn_ref[...] = (gt2 * dx2).astype(BF16)

    tile = lambda w: pl.BlockSpec((TM, w), lambda i: (i, 0))
    return pl.pallas_call(
        body, name="ffn_forward", grid=(NT,),
        in_specs=[tile(D), tile(D), _const((1, 6 * D)), _const((1, 6 * D)), _const((1, D)), _const((1, D)),
                  _const1((N_DEV, FB, D)), _const1((4, FB, D))],
        out_specs=(tile(D), pl.BlockSpec((4, TM, FB), lambda i: (0, i, 0)),
                   pl.BlockSpec((N_DEV, TM, FB), lambda i: (0, i, 0)), tile(D), tile(D), _const((FFN_STATS, D))),
        out_shape=(jax.ShapeDtypeStruct((S, D), BF16), jax.ShapeDtypeStruct((4, S, FB), BF16),
                   jax.ShapeDtypeStruct((N_DEV, S, FB), BF16), jax.ShapeDtypeStruct((S, D), F32),
                   jax.ShapeDtypeStruct((S, D), BF16), jax.ShapeDtypeStruct((FFN_STATS, D), F32)),
        compiler_params=_params(42, dimension_semantics=("arbitrary",)),
    )(x1, tgt, modnb, bada, nw2, nfw, w_fi, w_fo)


def _ffn_backward(dffn, pre, x1, dx2, modnb, bada, nw2, w_fi, w_fo):
    def body(dffn_ref, pre_ref, x1_ref, dx2_ref, mod_ref, b_ref, nw2_ref, wi_ref, wo_ref, df_ref, dx1_ref, st_ref):
        i = pl.program_id(0)

        @pl.when(i == 0)
        def _():
            st_ref[...] = jnp.zeros((FFN_STATS, D), F32)

        dffn = dffn_ref[...]
        dh = jnp.zeros((TM, D), F32)
        for j in range(4):
            fg = pre_ref[j].astype(F32)
            fu = pre_ref[j + 4].astype(F32)
            sg = _sig(fg)
            dact = _dot_nt(dffn, wo_ref[j])
            dfg = (dact * fu * (sg * (1.0 + fg * (1.0 - sg)))).astype(BF16)
            dfu = (dact * (fg * sg)).astype(BF16)
            df_ref[j] = dfg
            df_ref[j + 4] = dfu
            dh = dh + _dot(dfg, wi_ref[j]) + _dot(dfu, wi_ref[j + 4])
        x1v = x1_ref[...]
        r2 = lax.rsqrt(jnp.mean(x1v * x1v, axis=-1, keepdims=True) + EPS)
        xr2 = x1v * r2
        st_ref[2:3, :] += _colsum(dh)
        st_ref[3:4, :] += _colsum(dh * (xr2 * nw2_ref[...]))
        dxn = dh * (1.0 + _mod(mod_ref, b_ref, 4))
        st_ref[4:5, :] += _colsum(dxn * xr2)
        dxr = dxn * nw2_ref[...]
        dx1_ref[...] = dx2_ref[...] + r2 * (dxr - xr2 * jnp.mean(dxr * xr2, axis=-1, keepdims=True))

    tile = lambda w: pl.BlockSpec((TM, w), lambda i: (i, 0))
    wide = lambda: pl.BlockSpec((N_DEV, TM, FB), lambda i: (0, i, 0))
    return pl.pallas_call(
        body, name="ffn_backward", grid=(NT,),
        in_specs=[tile(D), wide(), tile(D), tile(D), _const((1, 6 * D)), _const((1, 6 * D)), _const((1, D)),
                  _const1((N_DEV, FB, D)), _const1((4, FB, D))],
        out_specs=(wide(), tile(D), _const((FFN_STATS, D))),
        out_shape=(jax.ShapeDtypeStruct((N_DEV, S, FB), BF16), jax.ShapeDtypeStruct((S, D), F32),
                   jax.ShapeDtypeStruct((FFN_STATS, D), F32)),
        compiler_params=_params(44, dimension_semantics=("arbitrary",)),
    )(dffn, pre, x1, dx2, modnb, bada, nw2, w_fi, w_fo)


def _grad_w(name, a, b, nb):
    m, n = a.shape[1], b.shape[1]

    def body(a_ref, b_ref, o_ref):
        o_ref[...] = _dot_tn(a_ref[...], b_ref[...]).astype(BF16)

    return pl.pallas_call(
        body, name=name, grid=(m // nb,),
        in_specs=[pl.BlockSpec((S, nb), lambda j: (0, j)), _const((S, n))],
        out_specs=pl.BlockSpec((nb, n), lambda j: (j, 0)),
        out_shape=jax.ShapeDtypeStruct((m, n), BF16),
        compiler_params=_params(dimension_semantics=("arbitrary",)),
    )(a, b)


def _grad_w_in(dp_conf, dp_gdn, dp_ba, hb1):
    nb = 512
    n_conf, n_gdn = 2 * CW // nb, 4 * GW // nb

    def body(c_ref, g_ref, ba_ref, h_ref, o_ref):
        j = pl.program_id(0)

        @pl.when(j < n_conf)
        def _():
            o_ref[...] = _dot_tn(c_ref[...], h_ref[...]).astype(BF16)

        @pl.when((j >= n_conf) & (j < n_conf + n_gdn))
        def _():
            o_ref[...] = _dot_tn(g_ref[...], h_ref[...]).astype(BF16)

        @pl.when(j == n_conf + n_gdn)
        def _():
            o_ref[0:2 * NH, :] = _dot_tn(ba_ref[...], h_ref[...])[0:2 * NH].astype(BF16)

    return pl.pallas_call(
        body, name="grad_w_in", grid=(n_conf + n_gdn + 1,),
        in_specs=[pl.BlockSpec((S, nb), lambda j: (0, jnp.minimum(j, n_conf - 1))),
                  pl.BlockSpec((S, nb), lambda j: (0, jnp.clip(j - n_conf, 0, n_gdn - 1))),
                  _const((S, LANES)), _const((S, D))],
        out_specs=pl.BlockSpec((nb, D), lambda j: (j, 0)),
        out_shape=jax.ShapeDtypeStruct((NIN, D), BF16),
        compiler_params=_params(dimension_semantics=("arbitrary",)),
    )(dp_conf, dp_gdn, dp_ba, hb1)


def _grad_w_ffn_in(hb2, df):
    def body(a_ref, b_ref, o_ref):
        o_ref[0] = _dot_tn(b_ref[0], a_ref[...]).astype(BF16)

    return pl.pallas_call(
        body, name="grad_w_ffn_in", grid=(N_DEV,),
        in_specs=[_const((S, D)), pl.BlockSpec((1, S, FB), lambda j: (j, 0, 0))],
        out_specs=pl.BlockSpec((1, FB, D), lambda j: (j, 0, 0)),
        out_shape=jax.ShapeDtypeStruct((N_DEV, FB, D), BF16),
        compiler_params=_params(dimension_semantics=("arbitrary",)),
    )(hb2, df)


def _grad_w_ffn_out(act, dffn):
    def body(a_ref, b_ref, o_ref):
        o_ref[0] = _dot_tn(a_ref[0], b_ref[...]).astype(BF16)

    return pl.pallas_call(
        body, name="grad_w_ffn_out", grid=(4,),
        in_specs=[pl.BlockSpec((1, S, FB), lambda j: (j, 0, 0)), _const((S, D))],
        out_specs=pl.BlockSpec((1, FB, D), lambda j: (j, 0, 0)),
        out_shape=jax.ShapeDtypeStruct((4, FB, D), BF16),
        compiler_params=_params(dimension_semantics=("arbitrary",)),
    )(act, dffn)


def _bwd_out(dx1, mix, modnb, bada, w_out):
    def body(dx_ref, mix_ref, mod_ref, b_ref, w_ref, dmix_ref, doa_ref, dob_ref, st_ref):
        i = pl.program_id(0)

        @pl.when(i == 0)
        def _():
            st_ref[...] = jnp.zeros((8, D), F32)

        dx = dx_ref[...]
        st_ref[0:1, :] += _colsum(dx * mix_ref[...])
        dmix = (_mod(mod_ref, b_ref, 2) * dx).astype(BF16)
        dmix_ref[...] = dmix
        doa_ref[...] = _dot_nt(dmix, w_ref[0:CW, :])
        dob_ref[...] = _dot_nt(dmix, w_ref[CW:D, :])

    tile = lambda w: pl.BlockSpec((TM, w), lambda i: (i, 0))
    return pl.pallas_call(
        body, name="bwd_out", grid=(NT,),
        in_specs=[tile(D), tile(D), _const((1, 6 * D)), _const((1, 6 * D)), _const((D, D))],
        out_specs=(tile(D), tile(CW), tile(GW), _const((8, D))),
        out_shape=(jax.ShapeDtypeStruct((S, D), BF16), jax.ShapeDtypeStruct((S, CW), F32),
                   jax.ShapeDtypeStruct((S, GW), F32), jax.ShapeDtypeStruct((8, D), F32)),
        compiler_params=_params(dimension_semantics=("arbitrary",)),
    )(dx1, mix, modnb, bada, w_out)


CONF_STATS = 40


def _conf_bwd(d_out_a, y, p_main, conv_w, gn_w, gn_b):
    def body(do_ref, y_ref, a_ref, g_ref, ah_ref, gh_ref, w_ref, gw_ref, gb_ref, dp_ref, st_ref,
             ubuf, dybuf, ush, dysh):
        i = pl.program_id(0)

        @pl.when(i == 0)
        def _():
            st_ref[...] = jnp.zeros((CONF_STATS, CW), F32)
            dybuf[TM:TM + HALO, :] = jnp.zeros((HALO, CW), F32)

        pm = _group_mean_matrix().astype(BF16)
        yv = y_ref[...]
        dlt = yv - _group_mean(yv, pm)
        rstd = lax.rsqrt(_group_mean(dlt * dlt, pm) + EPS)
        un = dlt * rstd
        o = un * gw_ref[...] + gb_ref[...]
        so = _sig(o)
        d_o = do_ref[...] * (so * (1.0 + o * (1.0 - so)))
        st_ref[33:34, :] += _colsum(d_o)
        st_ref[32:33, :] += _colsum(d_o * un)
        dun = d_o * gw_ref[...]
        dy = rstd * (dun - _group_mean(dun, pm) - un * _group_mean(dun * un, pm))
        st_ref[31:32, :] += _colsum(dy)
        dybuf[0:TM, :] = dy
        _fill_shifted(dybuf, dysh)

        a = a_ref[...]
        sg = _sig(g_ref[...])
        first = i == NT - 1
        ubuf[0:HALO, :] = jnp.where(first, 0.0, ah_ref[...] * _sig(gh_ref[...]))
        ubuf[HALO:HALO + TM, :] = a * sg
        _fill_shifted(ubuf, ush)
        du = jnp.zeros((TM, CW), F32)
        for k in range(KC):
            st_ref[k:k + 1, :] += _colsum(dy * _rows_at(ubuf, ush, HALO - (KC - 1) + k))
            du = du + w_ref[k:k + 1, :] * _rows_at(dybuf, dysh, KC - 1 - k)
        dybuf[TM:TM + HALO, :] = dybuf[0:HALO, :]
        dp_ref[:, 0:CW] = (du * sg).astype(BF16)
        dp_ref[:, CW:2 * CW] = (du * a * sg * (1.0 - sg)).astype(BF16)

    rev = lambda w, j=0: pl.BlockSpec((TM, w), lambda i: (NT - 1 - i, j))
    halo = lambda j: pl.BlockSpec((HALO, CW), lambda i: (jnp.maximum((NT - 1 - i) * (TM // HALO) - 1, 0), j))
    return pl.pallas_call(
        body, name="conf_bwd", grid=(NT,),
        in_specs=[rev(CW), rev(CW), rev(CW, 0), rev(CW, 1), halo(0), halo(1),
                  _const((KC, CW)), _const((1, CW)), _const((1, CW))],
        out_specs=(rev(2 * CW), _const((CONF_STATS, CW))),
        out_shape=(jax.ShapeDtypeStruct((S, 2 * CW), BF16), jax.ShapeDtypeStruct((CONF_STATS, CW), F32)),
        scratch_shapes=[pltpu.VMEM((HALO + TM, CW), F32), pltpu.VMEM((TM + HALO, CW), F32),
                        pltpu.VMEM((SUB - 1, SHIFT_ROWS, CW), F32), pltpu.VMEM((SUB - 1, SHIFT_ROWS, CW), F32)],
        compiler_params=_params(dimension_semantics=("arbitrary",)),
    )(d_out_a, y, p_main, p_main, p_main, p_main, conv_w, gn_w, gn_b)


GDN_STATS = 8


def _gdn_bwd(d_out_b, o_pre, s_in, t_inv, p_main, p_ba, gdn_conv_w, alog_l, dt_l, gdn_nw):
    def body(dob_ref, o_ref, sin_ref, t_ref, q_ref, k_ref, v_ref, z_ref, qh_ref, kh_ref, vh_ref, ba_ref,
             w_ref, al_ref, dt_ref, nw_ref, dp_ref, dba_ref, st_ref, xbuf, dcbuf, dstate):
        n = pl.program_id(0)

        @pl.when(n == 0)
        def _():
            st_ref[...] = jnp.zeros((GDN_STATS, 3 * GW), F32)
            dcbuf[CL:CL + SH, :] = jnp.zeros((SH, 3 * GW), F32)
            dstate[...] = jnp.zeros((NH, DH, DH), F32)

        for cc in reversed(range(CPS)):
            chunk(n, cc, dob_ref, o_ref, sin_ref, t_ref, q_ref, k_ref, v_ref, z_ref, qh_ref, kh_ref, vh_ref, ba_ref,
                  w_ref, al_ref, dt_ref, nw_ref, dp_ref, dba_ref, st_ref, xbuf, dcbuf, dstate)

    def chunk(n, cc, dob_ref, o_ref, sin_ref, t_ref, q_ref, k_ref, v_ref, z_ref, qh_ref, kh_ref, vh_ref, ba_ref,
              w_ref, al_ref, dt_ref, nw_ref, dp_ref, dba_ref, st_ref, xbuf, dcbuf, dstate):
        r0 = cc * CL
        if cc == 0:
            first = n == NCH // CPS - 1
            xbuf[0:SH, 0:GW] = jnp.where(first, 0.0, qh_ref[...])
            xbuf[0:SH, GW:2 * GW] = jnp.where(first, 0.0, kh_ref[...])
            xbuf[0:SH, 2 * GW:3 * GW] = jnp.where(first, 0.0, vh_ref[...])
        else:
            xbuf[0:SH, 0:GW] = q_ref[r0 - SH:r0, :]
            xbuf[0:SH, GW:2 * GW] = k_ref[r0 - SH:r0, :]
            xbuf[0:SH, 2 * GW:3 * GW] = v_ref[r0 - SH:r0, :]
        xbuf[SH:SH + CL, 0:GW] = q_ref[r0:r0 + CL, :]
        xbuf[SH:SH + CL, GW:2 * GW] = k_ref[r0:r0 + CL, :]
        xbuf[SH:SH + CL, 2 * GW:3 * GW] = v_ref[r0:r0 + CL, :]
        conv = _short_conv(w_ref, xbuf)
        sc = _sig(conv)
        qkv = conv * sc
        ba = ba_ref[r0:r0 + CL, :]
        beta_all, g_all, xg, neg_a = _gdn_gates(ba, al_ref[...], dt_ref[...])
        gcum, gcum_t = _gdn_cumsum(g_all)
        lane = lax.broadcasted_iota(jnp.int32, (CL, LANES), 1)
        row = lax.broadcasted_iota(jnp.int32, (CL, 1), 0)
        acc = dict(dgcum=jnp.zeros((CL, LANES), F32), dbeta=jnp.zeros((CL, LANES), F32))

        def head(h):
            lo = h * DH
            qh = qkv[:, lo:lo + DH]
            kh = qkv[:, GW + lo:GW + lo + DH]
            vh = qkv[:, 2 * GW + lo:2 * GW + lo + DH]
            beta = beta_all[:, h:h + 1]
            f = _head_terms(qh, kh, beta, gcum[:, NH + h:NH + h + 1], gcum_t[NH + h:NH + h + 1, :])
            qn, kn, qs, kb, gam, kds, cd, decay = (f[s] for s in ("qn", "kn", "qs", "kb", "gam", "kds", "cd", "decay"))
            t = t_ref[cc, h]
            st = sin_ref[cc, h]
            vb = vh * beta
            kbg = kb * gam
            u = _dot(t, vb, GP)
            w = _dot(t, kbg, GP)
            yield
            v_new = u - _dot(w, st, GP)
            q_dec = qs * gam
            k_dec = kn * kds

            o = o_ref[r0:r0 + CL, lo:lo + DH]
            zh = z_ref[r0:r0 + CL, lo:lo + DH]
            sz = _sig(zh)
            r = lax.rsqrt(jnp.mean(o * o, axis=-1, keepdims=True) + EPS)
            orr = o * r
            d_out = dob_ref[r0:r0 + CL, lo:lo + DH]
            dz = d_out * (orr * nw_ref[...]) * (sz * (1.0 + zh * (1.0 - sz)))
            don = d_out * (zh * sz)
            st_ref[4:5, 0:DH] += _colsum(don * orr)
            tt = don * nw_ref[...]
            d_o = r * (tt - orr * jnp.mean(tt * orr, axis=-1, keepdims=True))

            yield
            ds_out = dstate[h]
            dv_new = _dot_tn(f["qk"], d_o, GP) + _dot(k_dec, ds_out, GP)
            dqk = jnp.where(f["causal"], _dot_nt(d_o, v_new, GP), 0.0)
            dq_dec = _dot_nt(d_o, st, GP)
            dk_dec = _dot_nt(v_new, ds_out, GP)
            yield
            dstate[h] = _dot_tn(q_dec, d_o, GP) + cd * ds_out - _dot_tn(w, dv_new, GP)
            dcd = jnp.sum(_rowsum(st * ds_out), axis=0, keepdims=True)
            dw = -_dot_nt(dv_new, st, GP)
            dvb = _dot_tn(t, dv_new, GP)
            yield
            dt_m = _dot_nt(dv_new, vb, GP) + _dot_nt(dw, kbg, GP)
            dkbg = _dot_tn(t, dw, GP)
            yield
            dtt = _dot_nt(dt_m, t, GP)
            yield
            da = jnp.where(f["strict"], -_dot_tn(t, dtt, GP), 0.0)
            yield
            dad = da * decay
            dqkd = dqk * decay
            dkb = _dot(dad, kn, GP) + dkbg * gam
            dkn = _dot_tn(dad, kb, GP) + _dot_tn(dqkd, qs, GP) + dk_dec * kds + dkb * beta
            dqs = _dot(dqkd, kn, GP) + dq_dec * gam
            yield
            m = da * f["a"] + dqk * f["qk"]
            tk = _rowsum(dk_dec * k_dec)
            dgl = jnp.sum(tk, axis=0, keepdims=True) + dcd * cd
            dgc = (_rowsum(m) - _rowsum(jnp.transpose(m)) + _rowsum(dq_dec * q_dec) - tk + _rowsum(dkbg * kbg)
                   + jnp.where(row == CL - 1, dgl, 0.0))
            dbeta = _rowsum(dkb * kn) + _rowsum(dvb * vh)
            acc["dgcum"] = acc["dgcum"] + jnp.where(lane == NH + h, dgc, 0.0)
            acc["dbeta"] = acc["dbeta"] + jnp.where(lane == h, dbeta, 0.0)
            dvh = dvb * beta
            dqn = dqs * QSCALE
            dqh = f["rq"] * (dqn - qn * _rowsum(dqn * qn))
            dkh = f["rk"] * (dkn - kn * _rowsum(dkn * kn))
            dsilu = lambda c0: sc[:, c0:c0 + DH] * (1.0 + conv[:, c0:c0 + DH] * (1.0 - sc[:, c0:c0 + DH]))
            dcbuf[0:CL, lo:lo + DH] = dqh * dsilu(lo)
            dcbuf[0:CL, GW + lo:GW + lo + DH] = dkh * dsilu(GW + lo)
            dcbuf[0:CL, 2 * GW + lo:2 * GW + lo + DH] = dvh * dsilu(2 * GW + lo)
            dp_ref[r0:r0 + CL, 3 * GW + lo:3 * GW + lo + DH] = dz.astype(BF16)

        _lockstep(head(h) for h in range(NH))
        dgcum_all, dbeta_all = acc["dgcum"], acc["dbeta"]

        ii, jj = _tri_iota()
        upper = jnp.where(ii <= jj, 1.0, 0.0).astype(BF16)
        dg_all = _ones_dot(upper, dgcum_all)
        dxg = dg_all * neg_a * _sig(xg)
        st_ref[5:6, 0:LANES] += _colsum(dg_all * g_all)
        st_ref[6:7, 0:LANES] += _colsum(dxg)
        dbl = dbeta_all * beta_all * (1.0 - beta_all)
        dba_ref[r0:r0 + CL, :] = jnp.where(lane < NH, dbl, jnp.where(lane < 2 * NH, dxg, 0.0)).astype(BF16)

        dconv = dcbuf[0:CL, :]
        dx = w_ref[0:1, :] * dcbuf[KS - 1:KS - 1 + CL, :]
        st_ref[0:1, :] += _colsum(dconv * xbuf[SH - KS + 1:SH - KS + 1 + CL, :])
        for k in range(1, KS):
            off = SH - (KS - 1) + k
            st_ref[k:k + 1, :] += _colsum(dconv * xbuf[off:off + CL, :])
            dx = dx + w_ref[k:k + 1, :] * dcbuf[KS - 1 - k:KS - 1 - k + CL, :]
        dcbuf[CL:CL + SH, :] = dcbuf[0:SH, :]
        dp_ref[r0:r0 + CL, 0:3 * GW] = dx.astype(BF16)

    steps = NCH // CPS
    rev = lambda w, j=0: pl.BlockSpec((TG, w), lambda n: (steps - 1 - n, j))
    halo = lambda j: pl.BlockSpec((SH, GW), lambda n: (jnp.maximum((steps - 1 - n) * (TG // SH) - 1, 0), j))
    blk4 = lambda a, b: pl.BlockSpec((CPS, NH, a, b), lambda n: (steps - 1 - n, 0, 0, 0))
    return pl.pallas_call(
        body, name="gdn_bwd", grid=(steps,),
        in_specs=[rev(GW), rev(GW), blk4(DH, DH), blk4(CL, CL), rev(GW, 2), rev(GW, 3), rev(GW, 4), rev(GW, 5),
                  halo(2), halo(3), halo(4), rev(LANES), _const((KS, 3 * GW)), _const((1, LANES)),
                  _const((1, LANES)), _const((1, DH))],
        out_specs=(rev(4 * GW), rev(LANES), _const((GDN_STATS, 3 * GW))),
        out_shape=(jax.ShapeDtypeStruct((S, 4 * GW), BF16), jax.ShapeDtypeStruct((S, LANES), BF16),
                   jax.ShapeDtypeStruct((GDN_STATS, 3 * GW), F32)),
        scratch_shapes=[pltpu.VMEM((SH + CL, 3 * GW), F32), pltpu.VMEM((CL + SH, 3 * GW), F32),
                        pltpu.VMEM((NH, DH, DH), F32)],
        compiler_params=_params(dimension_semantics=("arbitrary",)),
    )(d_out_b, o_pre, s_in, t_inv, p_main, p_main, p_main, p_main, p_main, p_main, p_main, p_ba,
      gdn_conv_w, alog_l, dt_l, gdn_nw)


def _bwd_in(dp_conf, dp_gdn, dp_ba, x, dx1, nw1, modnb, bada, w_main, w_ba):
    def body(dc_ref, dg_ref, db_ref, x_ref, dx1_ref, nw_ref, mod_ref, b_ref, wm_ref, wb_ref, gx_ref, st_ref):
        i = pl.program_id(0)

        @pl.when(i == 0)
        def _():
            st_ref[...] = jnp.zeros((8, D), F32)

        dh = (_dot(dc_ref[...], wm_ref[0:2 * CW, :]) + _dot(dg_ref[...], wm_ref[2 * CW:NMAIN, :])
              + _dot(db_ref[...], wb_ref[...]))
        xv = x_ref[...]
        r = lax.rsqrt(jnp.mean(xv * xv, axis=-1, keepdims=True) + EPS)
        xr = xv * r
        st_ref[0:1, :] += _colsum(dh)
        st_ref[1:2, :] += _colsum(dh * (xr * nw_ref[...]))
        dxn = dh * (1.0 + _mod(mod_ref, b_ref, 1))
        st_ref[2:3, :] += _colsum(dxn * xr)
        dxr = dxn * nw_ref[...]
        gx_ref[...] = dx1_ref[...] + r * (dxr - xr * jnp.mean(dxr * xr, axis=-1, keepdims=True))

    tile = lambda w: pl.BlockSpec((TM, w), lambda i: (i, 0))
    return pl.pallas_call(
        body, name="bwd_in", grid=(NT,),
        in_specs=[tile(2 * CW), tile(4 * GW), tile(LANES), tile(D), tile(D), _const((1, D)), _const((1, 6 * D)),
                  _const((1, 6 * D)), _const((NMAIN, D)), _const((LANES, D))],
        out_specs=(tile(D), _const((8, D))),
        out_shape=(jax.ShapeDtypeStruct((S, D), F32), jax.ShapeDtypeStruct((8, D), F32)),
        compiler_params=_params(dimension_semantics=("arbitrary",)),
    )(dp_conf, dp_gdn, dp_ba, x, dx1, nw1, modnb, bada, w_main, w_ba)


def _adamw(w, g, m, v):
    m = ADAM_B1 * m + (1.0 - ADAM_B1) * g
    v = ADAM_B2 * v + (1.0 - ADAM_B2) * (g * g)
    m_hat = m / BC1
    v_hat = v / BC2
    delta = -ADAM_LR * (m_hat / (jnp.sqrt(v_hat) + ADAM_EPS) + ADAM_WD * w)
    return delta, m, v


ADAM_BLOCK_BYTES = 6 * 1024 * 1024


def _adam_tile(rows, cols):
    padded = -(-cols // LANES) * LANES
    if N_DEV * rows * padded * 4 <= ADAM_BLOCK_BYTES:
        return rows, cols
    best = None
    for tr in range(16, rows, 16):
        if rows % tr == 0 and N_DEV * tr * padded * 4 <= ADAM_BLOCK_BYTES:
            best = tr
    if best is not None:
        return best, cols
    rows_padded = -(-rows // 16) * 16
    tc = LANES
    for cand in range(LANES, cols, LANES):
        if cols % cand == 0 and N_DEV * rows_padded * cand * 4 <= ADAM_BLOCK_BYTES:
            tc = cand
    return rows, tc


def _reduce_adam(name, parts, w, m, v, own=None):
    rows, cols = w.shape
    tr, tc = _adam_tile(rows, cols)

    def body(*refs):
        p_ref, w_ref, m_ref, v_ref = refs[:4]
        g_ref, d_ref, nm_ref, nv_ref = refs[-4:]
        if own is None:
            part = lambda j: p_ref[j].astype(F32)
        else:
            me = 4 * lax.axis_index("x") + 2 * lax.axis_index("y") + lax.axis_index("c")
            part = lambda j: jnp.where(me == j, refs[4][...], p_ref[j]).astype(F32)
        g = part(0)
        for j in range(1, N_DEV):
            g = g + part(j)
        g_ref[...] = g
        d_ref[...], nm_ref[...], nv_ref[...] = _adamw(w_ref[...], g, m_ref[...], v_ref[...])

    blk = pl.BlockSpec((tr, tc), lambda i, j: (i, j))
    sds = jax.ShapeDtypeStruct((rows, cols), F32)
    extra = [] if own is None else [own]
    return pl.pallas_call(
        body, name=name, grid=(rows // tr, cols // tc),
        in_specs=[pl.BlockSpec((N_DEV, tr, tc), lambda i, j: (0, i, j)), blk, blk, blk] + [blk] * len(extra),
        out_specs=(blk, blk, blk, blk), out_shape=(sds, sds, sds, sds),
        compiler_params=_params(dimension_semantics=("arbitrary", "arbitrary")),
    )(parts, w, m, v, *extra)


W_IN_ROWS = NIN // N_DEV
CHUNKS = D // LANES


def _adam_w_in(parts, own, w, m, v):
    def body(p_ref, own_ref, w_ref, m_ref, v_ref, g_ref, d_ref, nm_ref, nv_ref):
        c = pl.program_id(0)
        me = 4 * lax.axis_index("x") + 2 * lax.axis_index("y") + lax.axis_index("c")
        part = lambda j: jnp.where(me == j, own_ref[...], p_ref[j]).astype(F32)
        g = part(0)
        for j in range(1, N_DEV):
            g = g + part(j)
        rows = pl.ds(c, W_IN_ROWS, stride=CHUNKS)
        g_ref[rows, :] = g
        d_ref[rows, :], nm_ref[rows, :], nv_ref[rows, :] = _adamw(w_ref[rows, :], g, m_ref[rows, :], v_ref[rows, :])

    lin = _const((W_IN_ROWS * CHUNKS, LANES))
    sds = jax.ShapeDtypeStruct((W_IN_ROWS * CHUNKS, LANES), F32)
    return pl.pallas_call(
        body, name="adam_w_in", grid=(CHUNKS,),
        in_specs=[pl.BlockSpec((N_DEV, W_IN_ROWS, LANES), lambda c: (0, 0, c)),
                  pl.BlockSpec((W_IN_ROWS, LANES), lambda c: (0, c)), lin, lin, lin],
        out_specs=(lin, lin, lin, lin), out_shape=(sds, sds, sds, sds),
        compiler_params=_params(dimension_semantics=("arbitrary",)),
    )(parts, own, w, m, v)


def _ada_adam(c_all, dmod_sh, w, m, v):
    rows, cols = w.shape
    tr = 256

    def body(c_ref, dm_ref, w_ref, m_ref, v_ref, g_ref, d_ref, nm_ref, nv_ref):
        cv = c_ref[...]
        g = _dot_tn(cv * _sig(cv), dm_ref[...], HI)
        g_ref[...] = g
        d_ref[...], nm_ref[...], nv_ref[...] = _adamw(w_ref[...], g, m_ref[...], v_ref[...])

    blk = pl.BlockSpec((tr, cols), lambda i: (i, 0))
    sds = jax.ShapeDtypeStruct((rows, cols), F32)
    return pl.pallas_call(
        body, name="ada_adam", grid=(rows // tr,),
        in_specs=[pl.BlockSpec((N_DEV, tr), lambda i: (0, i)), _const((N_DEV, cols)), blk, blk, blk],
        out_specs=(blk, blk, blk, blk), out_shape=(sds, sds, sds, sds),
        compiler_params=_params(dimension_semantics=("arbitrary",)),
    )(c_all, dmod_sh, w, m, v)


def _lanes(a, at=0):
    return jnp.pad(a, ((0, 0), (at, LANES - at - a.shape[1])))


WEIGHT_NAMES = ["w_ada", "b_ada", "norm_mix_w", "w_in", "conv_w", "conv_b", "conv_gn_w", "conv_gn_b", "gdn_conv_w",
                "gdn_a_log", "gdn_dt_bias", "gdn_norm_w", "w_out", "norm_ffn_w", "w_ffn_in", "w_ffn_out",
                "norm_final_w"]


SMALL_LAYOUT = [("b_ada", 0, 48, LANES), ("norm_mix_w", 48, 8, LANES), ("norm_ffn_w", 56, 8, LANES),
                ("norm_final_w", 64, 8, LANES), ("conv_b", 72, 4, LANES), ("conv_gn_w", 76, 4, LANES),
                ("conv_gn_b", 80, 4, LANES), ("gdn_norm_w", 84, 1, LANES), ("gdn_a_log", 85, 1, NH),
                ("gdn_dt_bias", 86, 1, NH)]
LOSS_ROW = 87


def _adam_small(g_small, weights, m1, m2):
    names = [nm for nm, _, _, _ in SMALL_LAYOUT]
    k = len(names)

    def body(*refs):
        g_ref = refs[0]
        w_refs, m_refs, v_refs = refs[1:1 + k], refs[1 + k:1 + 2 * k], refs[1 + 2 * k:1 + 3 * k]
        loss_ref = refs[1 + 3 * k]
        outs = refs[2 + 3 * k:2 + 7 * k]
        total = refs[-1]
        g = g_ref[0]
        for j in range(1, N_DEV):
            g = g + g_ref[j]
        total[...] = g
        loss_ref[...] = total[LOSS_ROW:LOSS_ROW + 1, :]
        for i, (_, r0, rows, lanes) in enumerate(SMALL_LAYOUT):
            gp = total[r0:r0 + rows, 0:lanes]
            outs[i][...] = gp
            outs[k + i][...], outs[2 * k + i][...], outs[3 * k + i][...] = _adamw(
                w_refs[i][...], gp, m_refs[i][...], v_refs[i][...])

    shapes = [jax.ShapeDtypeStruct((rows, lanes), F32) for _, _, rows, lanes in SMALL_LAYOUT]
    res = pl.pallas_call(
        body, name="adam_small",
        out_shape=tuple([jax.ShapeDtypeStruct((1, LANES), F32)] + shapes * 4),
        scratch_shapes=[pltpu.VMEM((SMALL_ROWS, LANES), F32)],
        compiler_params=_params(),
    )(g_small, *[weights[n] for n in names], *[m1[n] for n in names], *[m2[n] for n in names])
    kinds = [dict(zip(names, res[1 + q * k:1 + (q + 1) * k])) for q in range(4)]
    return res[0], kinds


def _mix_forward(w, xs, modnb, between=None):
    w_main = w["w_in"]
    w_ba = jnp.pad(w["w_in"][NMAIN:], ((0, LANES - 2 * NH), (0, 0)))
    alog_l = _lanes(w["gdn_a_log"], NH)
    dt_l = _lanes(w["gdn_dt_bias"], NH)
    p_main, p_ba, hb1 = _fwd_in(xs, w["norm_mix_w"], modnb, w["b_ada"], w_main, w_ba)
    w_o, u_o, qg, kd, qk, cd, t_inv = _gdn_prep(p_main, p_ba, w["gdn_conv_w"], alog_l, dt_l)
    out_b, o_pre, s_in = _gdn_scan(w_o, u_o, qg, kd, qk, cd, p_main, w["gdn_norm_w"])
    conv_b = w["conv_b"] if between is None else _after(w["conv_b"], between(out_b))
    y_conv, out_a = _conf_fwd(p_main, w["conv_w"], conv_b, w["conv_gn_w"], w["conv_gn_b"])
    return dict(w_main=w_main, w_ba=w_ba, alog_l=alog_l, dt_l=dt_l, p_main=p_main, p_ba=p_ba, hb1=hb1,
                y_conv=y_conv, out_a=out_a, out_b=out_b, o_pre=o_pre, s_in=s_in, t_inv=t_inv)


def _ffn_stage(w, f, xs, tgt, modnb):
    x1, mix, oab = _fwd_out(f["out_a"], f["out_b"], xs, modnb, w["b_ada"], w["w_out"])
    hb2, act, pre, dx2, dffn, st_fwd = _ffn_forward(x1, tgt, modnb, w["b_ada"], w["norm_ffn_w"],
                                                    w["norm_final_w"], w["w_ffn_in"], w["w_ffn_out"])
    gw_ffn_out = _grad_w_ffn_out(act, dffn)
    df, dx1, st_bwd = _ffn_backward(dffn, pre, x1, dx2, modnb, w["b_ada"], w["norm_ffn_w"], w["w_ffn_in"],
                                    w["w_ffn_out"])
    gw_ffn_in = _grad_w_ffn_in(hb2, df)
    return dict(mix=mix, oab=oab, dx1=dx1, st_ffn=st_fwd + st_bwd, gw_ffn_in=gw_ffn_in, gw_ffn_out=gw_ffn_out)


def _out_backward(w, g, modnb):
    dmix, d_out_a, d_out_b, st_out = _bwd_out(g["dx1"], g["mix"], modnb, w["b_ada"], w["w_out"])
    return dict(d_out_a=d_out_a, d_out_b=d_out_b, st_out=st_out, gw_out=_grad_w("grad_w_out", g["oab"], dmix, 512))


def _heads_backward(w, f, a):
    dp_conf, st_conf = _conf_bwd(a["d_out_a"], f["y_conv"], f["p_main"], w["conv_w"], w["conv_gn_w"],
                                 w["conv_gn_b"])
    dp_gdn, dp_ba, st_gdn = _gdn_bwd(a["d_out_b"], f["o_pre"], f["s_in"], f["t_inv"], f["p_main"], f["p_ba"],
                                     w["gdn_conv_w"], f["alog_l"], f["dt_l"], w["gdn_norm_w"])
    gw_in = _grad_w_in(dp_conf, dp_gdn, dp_ba, f["hb1"])
    return dict(dp_conf=dp_conf, dp_gdn=dp_gdn, dp_ba=dp_ba, st_conf=st_conf, st_gdn=st_gdn, gw_in=gw_in,
                gw_conv=st_conf[0:KC], gw_gconv=st_gdn[0:KS])


def _in_backward(w, f, g, a, h, xs, modnb):
    st_out, st_conf, st_gdn, st_ffn = a["st_out"], h["st_conf"], h["st_gdn"], g["st_ffn"]
    grad_x, st_in = _bwd_in(h["dp_conf"], h["dp_gdn"], h["dp_ba"], xs, g["dx1"], w["norm_mix_w"], modnb,
                            w["b_ada"], f["w_main"], f["w_ba"])
    dmod = jnp.concatenate([st_in[0:1], st_in[1:2], st_out[0:1], st_ffn[2:3], st_ffn[3:4], st_ffn[1:2]], axis=1)
    small = jnp.concatenate([
        dmod.reshape(48, LANES), st_in[2:3].reshape(8, LANES), st_ffn[4:5].reshape(8, LANES),
        st_ffn[0:1].reshape(8, LANES), st_conf[31:32].reshape(4, LANES), st_conf[32:33].reshape(4, LANES),
        st_conf[33:34].reshape(4, LANES), st_gdn[4:5, 0:LANES],
        _lanes(st_gdn[5:6, NH:2 * NH]), _lanes(st_gdn[6:7, NH:2 * NH]), st_ffn[5:6, 0:LANES]], axis=0)
    return dict(grad_x=grad_x, small=small)


def _local(w, xs, tgt, modnb):
    f = _mix_forward(w, xs, modnb)
    g = _ffn_stage(w, f, xs, tgt, modnb)
    a = _out_backward(w, g, modnb)
    h = _heads_backward(w, f, a)
    b = _in_backward(w, f, g, a, h, xs, modnb)
    return dict(b, gw_in=h["gw_in"], gw_conv=h["gw_conv"], gw_gconv=h["gw_gconv"], gw_out=a["gw_out"],
                gw_ffn_in=g["gw_ffn_in"], gw_ffn_out=g["gw_ffn_out"])


def kernel(x, c, w_ada, b_ada, norm_mix_w, w_in, conv_w, conv_b, conv_gn_w, conv_gn_b, gdn_conv_w, gdn_a_log, gdn_dt_bias, gdn_norm_w, w_out, norm_ffn_w, w_ffn_in, w_ffn_out, norm_final_w, loss_target, m_w_ada, m_b_ada, m_norm_mix_w, m_w_in, m_conv_w, m_conv_b, m_conv_gn_w, m_conv_gn_b, m_gdn_conv_w, m_gdn_a_log, m_gdn_dt_bias, m_gdn_norm_w, m_w_out, m_norm_ffn_w, m_w_ffn_in, m_w_ffn_out, m_norm_final_w, v_w_ada, v_b_ada, v_norm_mix_w, v_w_in, v_conv_w, v_conv_b, v_conv_gn_w, v_conv_gn_b, v_gdn_conv_w, v_gdn_a_log, v_gdn_dt_bias, v_gdn_norm_w, v_w_out, v_norm_ffn_w, v_w_ffn_in, v_w_ffn_out, v_norm_final_w):
    me = 4 * lax.axis_index("x") + 2 * lax.axis_index("y") + lax.axis_index("c")
    xs = x.reshape(S, D)
    tgt = loss_target.reshape(S, D)

    g_c, g_cw, g_gcw = _exchange("gather_cond", [c, conv_w[0], gdn_conv_w[0]], [False] * 3)
    c_all = g_c.reshape(N_DEV, D)
    g_mod, mod_token = _exchange("gather_mod", [_mod_shard(c_all, w_ada[0])], [False], with_token=True)
    modnb = lax.dynamic_index_in_dim(g_mod, me, axis=1, keepdims=False).reshape(1, 6 * D)

    late = [w_out[0].astype(BF16), jnp.transpose(w_ffn_in[0]).astype(BF16), w_ffn_out[0].astype(BF16)]
    g_win, *late_lands = _gather_two_level(
        "gather_weights", [_after(jnp.transpose(w_in[0]), mod_token).astype(BF16)] + late, seed_only=(1, 2, 3))
    late_started = _exchange_start("gather_late_start", late, late_lands, [False] * 3, only=LEVEL_ONE)
    modnb = _after(modnb, late_started[-1])
    w = dict(b_ada=b_ada, norm_mix_w=norm_mix_w, conv_b=conv_b, conv_gn_w=conv_gn_w, conv_gn_b=conv_gn_b,
             gdn_a_log=gdn_a_log, gdn_dt_bias=gdn_dt_bias, gdn_norm_w=gdn_norm_w, norm_ffn_w=norm_ffn_w,
             norm_final_w=norm_final_w.reshape(1, D),
             conv_w=jnp.transpose(g_cw, (1, 0, 2)).reshape(KC, CW),
             gdn_conv_w=jnp.transpose(g_gcw, (1, 0, 2)).reshape(KS, 3 * GW),
             w_in=g_win.reshape(NIN, D))

    relay = {}

    def relay_late(out_b):
        _, late_landed = _exchange_wait("gather_late_wait", late_started, [False] * 3, (out_b,), only=LEVEL_ONE)
        relay["started"] = _relay_start("gather_late_relay_start", late_landed)
        return relay["started"][-1]

    f = _mix_forward(w, xs, modnb, relay_late)
    g_wout, g_wfi, g_wfo = _relay_wait("gather_late_relay_wait", relay["started"], (f["out_a"],))
    w.update(w_out=g_wout.reshape(D, D), w_ffn_in=g_wfi, w_ffn_out=g_wfo.reshape(4, FB, D))
    g = _ffn_stage(w, f, xs, tgt, modnb)

    ffn_grads = [g["gw_ffn_in"], g["gw_ffn_out"].reshape(N_DEV, DFF // N_DEV, D)]
    ffn_started = _exchange_start("scatter_ffn_start", ffn_grads,
                                  [lax.empty(a.shape, a.dtype) for a in ffn_grads], [True] * 2)
    a = _out_backward(w, g, _after(modnb, ffn_started[-1]))
    out_grads = [a["gw_out"].reshape(N_DEV, D // N_DEV, D)]
    out_started = _exchange_start("scatter_out_start", out_grads,
                                  [lax.empty(t.shape, t.dtype) for t in out_grads], [True])
    h = _heads_backward(dict(w, conv_gn_w=_after(w["conv_gn_w"], out_started[-1])), f, a)

    in_grads = [h["gw_in"].reshape(N_DEV, NIN // N_DEV, D),
                jnp.transpose(h["gw_conv"].reshape(KC, N_DEV, CW // N_DEV), (1, 0, 2)),
                jnp.transpose(h["gw_gconv"].reshape(KS, N_DEV, 3 * GW // N_DEV), (1, 0, 2))]
    in_started = _exchange_start("scatter_in_start", in_grads,
                                 [lax.empty(t.shape, t.dtype) for t in in_grads], [True] * 3)
    loc = _in_backward(w, f, g, a, h, xs, _after(modnb, in_started[-1]))
    small_started = _exchange_start("gather_small_start", [loc["small"]],
                                    [lax.empty((N_DEV, SMALL_ROWS, LANES), F32)], [False])

    def own(sent):
        return lax.dynamic_index_in_dim(sent, me, axis=0, keepdims=False)

    big = {}
    (sent_fi, sent_fo), (r_fi, r_fo) = _exchange_wait("scatter_ffn_wait", ffn_started, [True] * 2,
                                                         (small_started[-1],))
    big["w_ffn_in"] = [jnp.transpose(t) for t in _reduce_adam(
        "adam_w_ffn_in", r_fi, jnp.transpose(w_ffn_in[0]), jnp.transpose(m_w_ffn_in[0]),
        jnp.transpose(v_w_ffn_in[0]), own(sent_fi))]
    big["w_ffn_out"] = _reduce_adam("adam_w_ffn_out", r_fo, w_ffn_out[0], m_w_ffn_out[0], v_w_ffn_out[0],
                                    own(sent_fo))
    (sent_out,), (r_out,) = _exchange_wait("scatter_out_wait", out_started, [True], (big["w_ffn_out"][0],))
    big["w_out"] = _reduce_adam("adam_w_out", r_out, w_out[0], m_w_out[0], v_w_out[0], own(sent_out))

    (sent_small,), (r_small,) = _exchange_wait("gather_small_wait", small_started, [False], (big["w_out"][0],))
    slot = lax.broadcasted_iota(jnp.int32, (N_DEV, 1, 1), 0)
    g_small = jnp.where(slot == me, sent_small[None], r_small)
    def views(b_, nm_, nf_, nl_, cb_, gw_, gb_, gn_, al_, dt_):
        arrs = [b_, nm_, nf_, nl_, cb_, gw_, gb_, gn_, al_, dt_]
        return {nm: t.reshape(rows, lanes) for (nm, _, rows, lanes), t in zip(SMALL_LAYOUT, arrs)}

    loss_row, res = _adam_small(
        g_small,
        views(b_ada, norm_mix_w, norm_ffn_w, norm_final_w, conv_b, conv_gn_w, conv_gn_b, gdn_norm_w, gdn_a_log,
              gdn_dt_bias),
        views(m_b_ada, m_norm_mix_w, m_norm_ffn_w, m_norm_final_w, m_conv_b, m_conv_gn_w, m_conv_gn_b,
              m_gdn_norm_w, m_gdn_a_log, m_gdn_dt_bias),
        views(v_b_ada, v_norm_mix_w, v_norm_ffn_w, v_norm_final_w, v_conv_b, v_conv_gn_w, v_conv_gn_b,
              v_gdn_norm_w, v_gdn_a_log, v_gdn_dt_bias))
    loss = loss_row[0, 0]
    small_shapes = dict(b_ada=(1, 6 * D), norm_mix_w=(1, D), norm_ffn_w=(1, D), norm_final_w=(D,),
                        conv_b=(1, CW), conv_gn_w=(1, CW), conv_gn_b=(1, CW), gdn_norm_w=(1, DH),
                        gdn_a_log=(1, NH), gdn_dt_bias=(1, NH))
    res = [{nm: t.reshape(small_shapes[nm]) for nm, t in kind.items()} for kind in res]

    dmod_rows = g_small[:, 0:48, :].reshape(N_DEV, 6 * D)
    dmod_sh = lax.dynamic_slice_in_dim(dmod_rows, me * (6 * D // N_DEV), 6 * D // N_DEV, axis=1)

    big["w_ada"] = _ada_adam(c_all, dmod_sh, w_ada[0], m_w_ada[0], v_w_ada[0])
    (sent_in, sent_cw, sent_gcw), (r_in, r_cw, r_gcw) = _exchange_wait(
        "scatter_in_wait", in_started, [True] * 3, (big["w_ada"][0],))
    lin = lambda t: jnp.transpose(t[0]).reshape(W_IN_ROWS * CHUNKS, LANES)
    big["w_in"] = [jnp.transpose(t.reshape(W_IN_ROWS, D)) for t in _adam_w_in(
        r_in, own(sent_in), lin(w_in), lin(m_w_in), lin(v_w_in))]
    big["conv_w"] = _reduce_adam("adam_conv_w", r_cw, conv_w[0], m_conv_w[0], v_conv_w[0], own(sent_cw))
    big["gdn_conv_w"] = _reduce_adam("adam_gdn_conv_w", r_gcw, gdn_conv_w[0], m_gdn_conv_w[0], v_gdn_conv_w[0],
                                     own(sent_gcw))
    outs = [loss, loc["grad_x"].reshape(1, S, D)]
    for kind in range(4):
        for nm in WEIGHT_NAMES:
            outs.append(big[nm][kind][None] if nm in big else res[kind][nm])
    return tuple(outs)
```

```python
import functools

import jax
import jax.numpy as jnp
from jax import lax
from jax.experimental import pallas as pl
from jax.experimental.pallas import tpu as pltpu

F32 = jnp.float32
BF16 = jnp.bfloat16
HI = lax.Precision.HIGHEST
MESH = pl.DeviceIdType.MESH

N_DEV = 8
S = 2048
D = 1024
TM = 256
NT = S // TM
CW = 512
KC = 31
NG = 8
GSZ = CW // NG
HALO = 32
GW = 512
NH = 4
DH = 128
KS = 4
SH = 8
CL = 64
NCH = S // CL
NMAIN = 2 * CW + 4 * GW
NIN = NMAIN + 2 * NH
DFF = 2816
FB = DFF // 4
EPS = 1e-6
QSCALE = DH ** -0.5
LANES = 128
SMALL_ROWS = 88

ADAM_LR = 0.001
ADAM_B1 = 0.9
ADAM_B2 = 0.999
ADAM_EPS = 1e-08
ADAM_WD = 0.01
ADAM_STEP = 10
BC1 = 1.0 - ADAM_B1 ** ADAM_STEP
BC2 = 1.0 - ADAM_B2 ** ADAM_STEP

MIB = 1024 * 1024
VMEM_LIMIT_MIB = 32


def _params(limit_mib=VMEM_LIMIT_MIB, **kw):
    return pltpu.CompilerParams(vmem_limit_bytes=limit_mib * MIB, **kw)


def _sig(x):
    return jax.nn.sigmoid(x)


GP = BF16


def _operands(a, b, prec):
    if prec is BF16:
        return a.astype(BF16), b.astype(BF16), None
    return a, b, prec


def _dot(a, b, prec=None):
    a, b, prec = _operands(a, b, prec)
    return jnp.dot(a, b, preferred_element_type=F32, precision=prec)


def _dot_nt(a, b, prec=None):
    a, b, prec = _operands(a, b, prec)
    return lax.dot_general(a, b, (((1,), (1,)), ((), ())), preferred_element_type=F32, precision=prec)


def _dot_tn(a, b, prec=None):
    a, b, prec = _operands(a, b, prec)
    return lax.dot_general(a, b, (((0,), (0,)), ((), ())), preferred_element_type=F32, precision=prec)


def _lockstep(gens):
    gens = list(gens)
    while gens:
        alive = []
        for g in gens:
            try:
                next(g)
                alive.append(g)
            except StopIteration:
                pass
        gens = alive


def _rowsum(x):
    return jnp.sum(x, axis=-1, keepdims=True)


def _colsum(x):
    return jnp.sum(x, axis=0, keepdims=True)


def _mod(mod_ref, b_ref, k):
    return mod_ref[:, k * D:(k + 1) * D] + b_ref[:, k * D:(k + 1) * D]


def _const(shape):
    nd = len(shape)
    return pl.BlockSpec(shape, lambda *_: (0,) * nd)


def _const1(shape):
    nd = len(shape)
    return pl.BlockSpec(shape, lambda *_: (0,) * nd, pipeline_mode=pl.Buffered(1))


PEER_FLIPS = [(dx, dy, dc) for dx in (0, 1) for dy in (0, 1) for dc in (0, 1)][1:]


def _after(x, token):
    return x + token[0:1, 0:1].astype(x.dtype).reshape((1,) * x.ndim)


def _exchange(name, srcs, per_dest, seed_only=(), with_token=False):
    n = len(srcs)
    out_shape = []
    for a, pd in zip(srcs, per_dest):
        blk = a.shape[1:] if pd else a.shape
        out_shape.append(jax.ShapeDtypeStruct((N_DEV,) + tuple(blk), a.dtype))

    def body(*refs):
        src = refs[:n]
        dst = refs[n:2 * n]
        send_sems, recv_sems, local_sems = refs[-3:]
        if with_token:
            refs[2 * n][...] = jnp.zeros((8, LANES), F32)
        x, y, c = lax.axis_index("x"), lax.axis_index("y"), lax.axis_index("c")
        me = 4 * x + 2 * y + c

        def piece(i, j):
            return src[i].at[j] if per_dest[i] else src[i]

        copies = []
        for k, (dx, dy, dc) in enumerate(PEER_FLIPS):
            px = 1 - x if dx else x
            py = 1 - y if dy else y
            pc = 1 - c if dc else c
            pj = 4 * px + 2 * py + pc
            for i in range(n):
                if i in seed_only:
                    continue
                cp = pltpu.make_async_remote_copy(
                    src_ref=piece(i, pj), dst_ref=dst[i].at[me],
                    send_sem=send_sems.at[k * n + i], recv_sem=recv_sems.at[k * n + i],
                    device_id=(px, py, pc), device_id_type=MESH)
                cp.start()
                arrive = pltpu.make_async_remote_copy(
                    src_ref=piece(i, pj), dst_ref=dst[i].at[pj],
                    send_sem=send_sems.at[k * n + i], recv_sem=recv_sems.at[k * n + i],
                    device_id=(px, py, pc), device_id_type=MESH)
                copies.append((cp, arrive))
        own = []
        for i in range(n):
            lc = pltpu.make_async_copy(piece(i, me), dst[i].at[me], local_sems.at[i])
            lc.start()
            own.append(lc)
        for cp, arrive in copies:
            arrive.wait_recv()
        for cp, arrive in copies:
            cp.wait_send()
        for lc in own:
            lc.wait()

    any_spec = pl.BlockSpec(memory_space=pl.ANY)
    out_specs = [any_spec] * n
    if with_token:
        out_shape.append(jax.ShapeDtypeStruct((8, LANES), F32))
        out_specs.append(pl.BlockSpec(memory_space=pltpu.VMEM))
    return pl.pallas_call(
        body, name=name, out_shape=tuple(out_shape),
        in_specs=[any_spec] * n, out_specs=tuple(out_specs),
        scratch_shapes=[pltpu.SemaphoreType.DMA((7 * n,)), pltpu.SemaphoreType.DMA((7 * n,)),
                        pltpu.SemaphoreType.DMA((n,))],
        compiler_params=pltpu.CompilerParams(has_side_effects=True),
    )(*srcs)


CHIP_FLIPS = [(0, 1), (1, 0), (1, 1)]
LEVEL_ONE = [k for k, (dx, dy, dc) in enumerate(PEER_FLIPS) if (dx, dy, dc) == (0, 0, 1) or dc == 0]


def _chip_peers(x, y):
    return [(1 - x if dx else x, 1 - y if dy else y) for dx, dy in CHIP_FLIPS]


def _gather_two_level(name, srcs, seed_only=()):
    n = len(srcs)
    live = [i for i in range(n) if i not in seed_only]

    def body(*refs):
        src, dst = refs[:n], refs[n:2 * n]
        send_sems, recv_sems, local_sems = refs[2 * n:2 * n + 3]
        bounce = refs[2 * n + 3:]
        x, y, c = lax.axis_index("x"), lax.axis_index("y"), lax.axis_index("c")
        me = 4 * x + 2 * y + c
        sibling = (x, y, 1 - c)
        chips = _chip_peers(x, y)

        def copy(k, i, src_ref, slot, to):
            return pltpu.make_async_remote_copy(
                src_ref=src_ref, dst_ref=dst[i].at[slot], send_sem=send_sems.at[k * n + i],
                recv_sem=recv_sems.at[k * n + i], device_id=to, device_id_type=MESH)

        first = []
        for i in live:
            first.append(copy(0, i, src[i], me, sibling))
            first += [copy(1 + j, i, src[i], me, (px, py, c)) for j, (px, py) in enumerate(chips)]
        for cp in first:
            cp.start()
        up = [pltpu.make_async_copy(src[i], bounce[i], local_sems.at[i]) for i in range(n)]
        for cp in up:
            cp.start()
        for cp in up:
            cp.wait()
        own = [pltpu.make_async_copy(bounce[i], dst[i].at[me], local_sems.at[i]) for i in range(n)]
        for cp in own:
            cp.start()
        passed = []
        for j, (px, py) in enumerate(chips):
            slot = 4 * px + 2 * py + c
            for i in live:
                copy(1 + j, i, src[i], slot, (px, py, c)).wait_recv()
                fwd = copy(4 + j, i, dst[i].at[slot], slot, sibling)
                fwd.start()
                passed.append(fwd)
        for i in live:
            copy(0, i, src[i], 4 * x + 2 * y + 1 - c, sibling).wait_recv()
            for j, (px, py) in enumerate(chips):
                copy(4 + j, i, src[i], 4 * px + 2 * py + 1 - c, sibling).wait_recv()
        for cp in first + passed:
            cp.wait_send()
        for cp in own:
            cp.wait()

    any_spec = pl.BlockSpec(memory_space=pl.ANY)
    return pl.pallas_call(
        body, name=name, out_shape=tuple(jax.ShapeDtypeStruct((N_DEV,) + a.shape, a.dtype) for a in srcs),
        in_specs=[any_spec] * n, out_specs=tuple([any_spec] * n),
        scratch_shapes=[pltpu.SemaphoreType.DMA((7 * n,)), pltpu.SemaphoreType.DMA((7 * n,)),
                        pltpu.SemaphoreType.DMA((n,))] + [pltpu.VMEM(a.shape, a.dtype) for a in srcs],
        compiler_params=pltpu.CompilerParams(has_side_effects=True),
    )(*srcs)


def _relay_copy(land, sems, i, n, j, slot, sibling):
    send_sems, recv_sems = sems
    return pltpu.make_async_remote_copy(
        src_ref=land[i].at[slot], dst_ref=land[i].at[slot], send_sem=send_sems.at[j * n + i],
        recv_sem=recv_sems.at[j * n + i], device_id=sibling, device_id_type=MESH)


def _relay_start(name, lands):
    n = len(lands)

    def body(*refs):
        land = refs[:n]
        sems = refs[n], refs[n + 1]
        x, y, c = lax.axis_index("x"), lax.axis_index("y"), lax.axis_index("c")
        for j, (px, py) in enumerate(_chip_peers(x, y)):
            for i in range(n):
                _relay_copy(land, sems, i, n, j, 4 * px + 2 * py + c, (x, y, 1 - c)).start()
        refs[-1][...] = jnp.zeros((8, LANES), F32)

    return pl.pallas_call(
        body, name=name,
        out_shape=(pltpu.SemaphoreType.DMA((3 * n,)), pltpu.SemaphoreType.DMA((3 * n,)),
                   *[pltpu.HBM(a.shape, a.dtype) for a in lands], jax.ShapeDtypeStruct((8, LANES), F32)),
        in_specs=[HBM_SPEC] * n,
        out_specs=(SEM_SPEC, SEM_SPEC, *[HBM_SPEC] * n, pl.BlockSpec(memory_space=pltpu.VMEM)),
        input_output_aliases={i: 2 + i for i in range(n)},
        compiler_params=pltpu.CompilerParams(has_side_effects=DATAFLOW),
    )(*[pltpu.with_memory_space_constraint(a, pltpu.HBM) for a in lands])


def _relay_wait(name, started, after):
    n = len(started) - 3
    arrays = list(started[2:2 + n])

    def body(*refs):
        land = refs[:n]
        sems = refs[n], refs[n + 1]
        x, y, c = lax.axis_index("x"), lax.axis_index("y"), lax.axis_index("c")
        for j, (px, py) in enumerate(_chip_peers(x, y)):
            for i in range(n):
                _relay_copy(land, sems, i, n, j, 4 * px + 2 * py + c, (x, y, 1 - c)).wait_send()
                _relay_copy(land, sems, i, n, j, 4 * px + 2 * py + 1 - c, (x, y, 1 - c)).wait_recv()

    return pl.pallas_call(
        body, name=name,
        out_shape=tuple(pltpu.HBM(a.shape, a.dtype) for a in arrays),
        in_specs=[HBM_SPEC] * n + [SEM_SPEC, SEM_SPEC] + [pl.BlockSpec(memory_space=pl.ANY)] * len(after),
        out_specs=tuple([HBM_SPEC] * n),
        input_output_aliases={i: i for i in range(n)},
        compiler_params=pltpu.CompilerParams(has_side_effects=DATAFLOW),
    )(*arrays, started[0], started[1], *after)


HBM_SPEC = pl.BlockSpec(memory_space=pltpu.HBM)
SEM_SPEC = pl.BlockSpec(memory_space=pltpu.SEMAPHORE)
DATAFLOW = pltpu.SideEffectType.DATAFLOW_SIDE_EFFECTING


def _peers(only=None):
    x, y, c = lax.axis_index("x"), lax.axis_index("y"), lax.axis_index("c")
    out = []
    for k, (dx, dy, dc) in enumerate(PEER_FLIPS):
        if only is not None and k not in only:
            continue
        px = 1 - x if dx else x
        py = 1 - y if dy else y
        pc = 1 - c if dc else c
        out.append((k, (px, py, pc), 4 * px + 2 * py + pc))
    return 4 * x + 2 * y + c, out


def _exchange_start(name, srcs, lands, per_dest, only=None):
    n = len(srcs)

    def body(*refs):
        src, land = refs[:n], refs[n:2 * n]
        send_sems, recv_sems = refs[2 * n], refs[2 * n + 1]
        token = refs[-1]
        me, peers = _peers(only)
        for k, peer, pj in peers:
            for i in range(n):
                pltpu.make_async_remote_copy(
                    src_ref=src[i].at[pj] if per_dest[i] else src[i], dst_ref=land[i].at[me],
                    send_sem=send_sems.at[k * n + i], recv_sem=recv_sems.at[k * n + i],
                    device_id=peer, device_id_type=MESH).start()
        token[...] = jnp.zeros((8, LANES), F32)

    arrays = list(srcs) + list(lands)
    return pl.pallas_call(
        body, name=name,
        out_shape=(pltpu.SemaphoreType.DMA((7 * n,)), pltpu.SemaphoreType.DMA((7 * n,)),
                   *[pltpu.HBM(a.shape, a.dtype) for a in arrays], jax.ShapeDtypeStruct((8, LANES), F32)),
        in_specs=[HBM_SPEC] * (2 * n),
        out_specs=(SEM_SPEC, SEM_SPEC, *[HBM_SPEC] * (2 * n), pl.BlockSpec(memory_space=pltpu.VMEM)),
        input_output_aliases={i: 2 + i for i in range(2 * n)},
        compiler_params=pltpu.CompilerParams(has_side_effects=DATAFLOW),
    )(*[pltpu.with_memory_space_constraint(a, pltpu.HBM) for a in arrays])


def _exchange_wait(name, started, per_dest, after, only=None):
    n = (len(started) - 3) // 2
    send_sems, recv_sems = started[0], started[1]
    arrays = list(started[2:2 + 2 * n])

    def body(*refs):
        src, land = refs[:n], refs[n:2 * n]
        send, recv = refs[2 * n], refs[2 * n + 1]
        me, peers = _peers(only)
        for k, peer, pj in peers:
            for i in range(n):
                cp = pltpu.make_async_remote_copy(
                    src_ref=src[i].at[pj] if per_dest[i] else src[i], dst_ref=land[i].at[pj],
                    send_sem=send.at[k * n + i], recv_sem=recv.at[k * n + i],
                    device_id=peer, device_id_type=MESH)
                cp.wait_send()
                cp.wait_recv()

    outs = pl.pallas_call(
        body, name=name,
        out_shape=tuple(pltpu.HBM(a.shape, a.dtype) for a in arrays),
        in_specs=[HBM_SPEC] * (2 * n) + [SEM_SPEC, SEM_SPEC] + [pl.BlockSpec(memory_space=pl.ANY)] * len(after),
        out_specs=tuple([HBM_SPEC] * (2 * n)),
        input_output_aliases={i: i for i in range(2 * n)},
        compiler_params=pltpu.CompilerParams(has_side_effects=DATAFLOW),
    )(*arrays, send_sems, recv_sems, *after)
    return outs[:n], outs[n:]


def _mod_shard(c_all, w_ada):
    def body(c_ref, w_ref, o_ref):
        cv = c_ref[...]
        ca = cv * _sig(cv)
        o_ref[...] = _dot(ca.astype(BF16), w_ref[...].astype(BF16))

    return pl.pallas_call(
        body, name="mod_shard", out_shape=jax.ShapeDtypeStruct((N_DEV, w_ada.shape[1]), F32),
        compiler_params=_params(),
    )(c_all, w_ada)


TI = 512


def _fwd_in(x, nw1, modnb, bada, w_main, w_ba):
    def body(x_ref, nw_ref, mod_ref, b_ref, wm_ref, wb_ref, pm_ref, pb_ref, hb_ref):
        xv = x_ref[...]
        r = lax.rsqrt(jnp.mean(xv * xv, axis=-1, keepdims=True) + EPS)
        h = (xv * r * nw_ref[...]) * (1.0 + _mod(mod_ref, b_ref, 1)) + _mod(mod_ref, b_ref, 0)
        hb = h.astype(BF16)
        hb_ref[...] = hb
        pm_ref[...] = _dot_nt(hb, wm_ref[...])
        pb_ref[...] = _dot_nt(hb, wb_ref[...])

    return pl.pallas_call(
        body, name="fwd_in", grid=(S // TI,),
        in_specs=[pl.BlockSpec((TI, D), lambda i: (i, 0)), _const((1, D)), _const((1, 6 * D)), _const((1, 6 * D)),
                  _const1((NMAIN, D)), _const((LANES, D))],
        out_specs=(pl.BlockSpec((TI, NMAIN), lambda i: (i, 0)), pl.BlockSpec((TI, LANES), lambda i: (i, 0)),
                   pl.BlockSpec((TI, D), lambda i: (i, 0))),
        out_shape=(jax.ShapeDtypeStruct((S, NMAIN), F32), jax.ShapeDtypeStruct((S, LANES), F32),
                   jax.ShapeDtypeStruct((S, D), BF16)),
        compiler_params=_params(dimension_semantics=("arbitrary",)),
    )(x, nw1, modnb, bada, w_main, w_ba)


def _group_mean_matrix():
    ii = lax.broadcasted_iota(jnp.int32, (CW, CW), 0) // GSZ
    jj = lax.broadcasted_iota(jnp.int32, (CW, CW), 1) // GSZ
    return jnp.where(ii == jj, 1.0 / GSZ, 0.0).astype(F32)


SUB = 8
SHIFT_ROWS = HALO + TM - SUB


def _fill_shifted(buf, sh):
    for b in range(1, SUB):
        sh[b - 1] = buf[b:b + SHIFT_ROWS, :]


def _rows_at(buf, sh, off):
    a, b = divmod(off, SUB)
    if b == 0:
        return buf[off:off + TM, :]
    return sh[b - 1, SUB * a:SUB * a + TM, :]


def _group_mean(x, pm):
    hi = x.astype(BF16)
    r1 = x - hi.astype(F32)
    mid = r1.astype(BF16)
    lo = (r1 - mid.astype(F32)).astype(BF16)
    return _dot(hi, pm) + _dot(mid, pm) + _dot(lo, pm)


def _conf_fwd(p_main, conv_w, conv_b, gn_w, gn_b):
    def body(a_ref, g_ref, w_ref, b_ref, gw_ref, gb_ref, y_ref, oa_ref, ubuf, ush):
        i = pl.program_id(0)

        @pl.when(i == 0)
        def _():
            ubuf[0:HALO, :] = jnp.zeros((HALO, CW), F32)

        ubuf[HALO:HALO + TM, :] = a_ref[...] * _sig(g_ref[...])
        _fill_shifted(ubuf, ush)
        acc = jnp.zeros((TM, CW), F32) + b_ref[...]
        for k in range(KC):
            acc = acc + w_ref[k:k + 1, :] * _rows_at(ubuf, ush, HALO - (KC - 1) + k)
        y_ref[...] = acc
        ubuf[0:HALO, :] = ubuf[TM:TM + HALO, :]
        pm = _group_mean_matrix().astype(BF16)
        dlt = acc - _group_mean(acc, pm)
        var = _group_mean(dlt * dlt, pm)
        o = dlt * lax.rsqrt(var + EPS) * gw_ref[...] + gb_ref[...]
        oa_ref[...] = o * _sig(o)

    return pl.pallas_call(
        body, name="conf_fwd", grid=(NT,),
        in_specs=[pl.BlockSpec((TM, CW), lambda i: (i, 0)), pl.BlockSpec((TM, CW), lambda i: (i, 1)),
                  _const((KC, CW)), _const((1, CW)), _const((1, CW)), _const((1, CW))],
        out_specs=(pl.BlockSpec((TM, CW), lambda i: (i, 0)), pl.BlockSpec((TM, CW), lambda i: (i, 0))),
        out_shape=(jax.ShapeDtypeStruct((S, CW), F32), jax.ShapeDtypeStruct((S, CW), F32)),
        scratch_shapes=[pltpu.VMEM((HALO + TM, CW), F32), pltpu.VMEM((SUB - 1, SHIFT_ROWS, CW), F32)],
        compiler_params=_params(dimension_semantics=("arbitrary",)),
    )(p_main, p_main, conv_w, conv_b, gn_w, gn_b)


def _tri_iota():
    ii = lax.broadcasted_iota(jnp.int32, (CL, CL), 0)
    jj = lax.broadcasted_iota(jnp.int32, (CL, CL), 1)
    return ii, jj


def _gdn_gates(ba, alog_l, dt_l):
    beta_all = _sig(ba)
    xg = ba + dt_l
    sp = jnp.maximum(xg, 0.0) + jnp.log(1.0 + jnp.exp(-jnp.abs(xg)))
    neg_a = -jnp.exp(alog_l)
    return beta_all, neg_a * sp, xg, neg_a


def _ones_dot(ones, x):
    hi = x.astype(BF16)
    r1 = x - hi.astype(F32)
    mid = r1.astype(BF16)
    lo = (r1 - mid.astype(F32)).astype(BF16)
    return _dot(ones, hi) + _dot(ones, mid) + _dot(ones, lo)


def _gdn_cumsum(g_all):
    ii, jj = _tri_iota()
    low = jnp.where(ii >= jj, 1.0, 0.0).astype(BF16)
    gcum = _ones_dot(low, g_all)
    return gcum, jnp.transpose(gcum)


def _split(x):
    hi = x.astype(BF16)
    return hi, (x - hi.astype(F32)).astype(BF16)


def _dot_split(a, b):
    (ah, al), (bh, bl) = a, b
    return _dot(ah, bh) + (_dot(ah, bl) + _dot(al, bh))


def _unit_lower_inverses(mats):
    ii, jj = _tri_iota()
    eye = jnp.where(ii == jj, 1.0, 0.0).astype(F32)
    ts = [eye - a for a in mats]
    ps = [_dot_split(s, s) for s in map(_split, mats)]
    for _ in range(4):
        sp = [_split(p) for p in ps]
        ts = [t + _dot_split(_split(t), s) for t, s in zip(ts, sp)]
        ps = [_dot_split(s, s) for s in sp]
    return [t + _dot_split(_split(t), _split(p)) for t, p in zip(ts, ps)]


def _head_terms(qh, kh, beta, gcol, grow):
    ii, jj = _tri_iota()
    causal = ii >= jj
    strict = ii > jj
    rq = lax.rsqrt(_rowsum(qh * qh) + EPS)
    rk = lax.rsqrt(_rowsum(kh * kh) + EPS)
    qn = qh * rq
    kn = kh * rk
    qs = qn * QSCALE
    decay = jnp.where(causal, jnp.exp(jnp.where(causal, gcol - grow, 0.0)), 0.0)
    gam = jnp.exp(gcol)
    gl = gcol[CL - 1:CL, :]
    kds = jnp.exp(gl - gcol)
    cd = jnp.exp(gl)
    kb = kn * beta
    a = jnp.where(strict, _dot_nt(kb, kn, GP) * decay, 0.0)
    qk = jnp.where(causal, _dot_nt(qs, kn, GP) * decay, 0.0)
    return dict(rq=rq, rk=rk, qn=qn, kn=kn, qs=qs, decay=decay, gam=gam, kds=kds, cd=cd, kb=kb, a=a, qk=qk,
                causal=causal, strict=strict)


def _short_conv(w_ref, buf, rows=CL):
    acc = w_ref[0:1, :] * buf[SH - KS + 1:SH - KS + 1 + rows, :]
    for k in range(1, KS):
        off = SH - (KS - 1) + k
        acc = acc + w_ref[k:k + 1, :] * buf[off:off + rows, :]
    return acc


CPS = 4
TG = CPS * CL


def _gdn_prep(p_main, p_ba, gdn_conv_w, alog_l, dt_l):
    def body(q_ref, k_ref, v_ref, qh_ref, kh_ref, vh_ref, ba_ref, w_ref, al_ref, dt_ref,
             wo_ref, uo_ref, qg_ref, kd_ref, qk_ref, cd_ref, t_ref, xbuf):
        i = pl.program_id(0)
        first = i == 0
        xbuf[0:SH, 0:GW] = jnp.where(first, 0.0, qh_ref[...])
        xbuf[0:SH, GW:2 * GW] = jnp.where(first, 0.0, kh_ref[...])
        xbuf[0:SH, 2 * GW:3 * GW] = jnp.where(first, 0.0, vh_ref[...])
        xbuf[SH:SH + TG, 0:GW] = q_ref[...]
        xbuf[SH:SH + TG, GW:2 * GW] = k_ref[...]
        xbuf[SH:SH + TG, 2 * GW:3 * GW] = v_ref[...]
        conv = _short_conv(w_ref, xbuf, TG)
        qkv = conv * _sig(conv)
        beta_all, g_all, _, _ = _gdn_gates(ba_ref[...], al_ref[...], dt_ref[...])
        lane = lax.broadcasted_iota(jnp.int32, (8, LANES), 1)
        cums = [_gdn_cumsum(g_all[cc * CL:(cc + 1) * CL, :]) for cc in range(CPS)]
        pairs = [(cc, h) for cc in range(CPS) for h in range(NH)]
        terms, vbs = [], []
        for cc, h in pairs:
            r0, lo = cc * CL, h * DH
            beta = beta_all[r0:r0 + CL, h:h + 1]
            gcum, gcum_t = cums[cc]
            terms.append(_head_terms(qkv[r0:r0 + CL, lo:lo + DH], qkv[r0:r0 + CL, GW + lo:GW + lo + DH], beta,
                                     gcum[:, NH + h:NH + h + 1], gcum_t[NH + h:NH + h + 1, :]))
            vbs.append(qkv[r0:r0 + CL, 2 * GW + lo:2 * GW + lo + DH] * beta)
        invs = _unit_lower_inverses([f["a"] for f in terms])
        cds = [jnp.zeros((8, LANES), F32) for _ in range(CPS)]
        for (cc, h), f, t, vb in zip(pairs, terms, invs, vbs):
            r0, lo = cc * CL, h * DH
            t_ref[cc, h] = t
            uo_ref[r0:r0 + CL, lo:lo + DH] = _dot(t, vb, GP)
            wo_ref[r0:r0 + CL, lo:lo + DH] = _dot(t, f["kb"] * f["gam"], GP).astype(BF16)
            qg_ref[r0:r0 + CL, lo:lo + DH] = (f["qs"] * f["gam"]).astype(BF16)
            kd_ref[r0:r0 + CL, lo:lo + DH] = (f["kn"] * f["kds"]).astype(BF16)
            qk_ref[cc, h] = f["qk"].astype(BF16)
            cds[cc] = cds[cc] + jnp.where(lane == h, f["cd"], 0.0)
        for cc in range(CPS):
            cd_ref[cc] = cds[cc]

    col = lambda j: pl.BlockSpec((TG, GW), lambda i: (i, j))
    halo = lambda j: pl.BlockSpec((SH, GW), lambda i: (jnp.maximum(i * (TG // SH) - 1, 0), j))
    tile = lambda: pl.BlockSpec((TG, GW), lambda i: (i, 0))
    sq = lambda: pl.BlockSpec((CPS, NH, CL, CL), lambda i: (i, 0, 0, 0))
    return pl.pallas_call(
        body, name="gdn_prep", grid=(NCH // CPS,),
        in_specs=[col(2), col(3), col(4), halo(2), halo(3), halo(4), pl.BlockSpec((TG, LANES), lambda i: (i, 0)),
                  _const((KS, 3 * GW)), _const((1, LANES)), _const((1, LANES))],
        out_specs=(tile(), tile(), tile(), tile(), sq(), pl.BlockSpec((CPS, 8, LANES), lambda i: (i, 0, 0)), sq()),
        out_shape=(jax.ShapeDtypeStruct((S, GW), BF16), jax.ShapeDtypeStruct((S, GW), F32),
                   jax.ShapeDtypeStruct((S, GW), BF16), jax.ShapeDtypeStruct((S, GW), BF16),
                   jax.ShapeDtypeStruct((NCH, NH, CL, CL), BF16), jax.ShapeDtypeStruct((NCH, 8, LANES), F32),
                   jax.ShapeDtypeStruct((NCH, NH, CL, CL), F32)),
        scratch_shapes=[pltpu.VMEM((SH + TG, 3 * GW), F32)],
        compiler_params=_params(dimension_semantics=("arbitrary",)),
    )(p_main, p_main, p_main, p_main, p_main, p_main, p_ba, gdn_conv_w, alog_l, dt_l)


def _gdn_scan(w_o, u_o, qg, kd, qk, cd, p_main, gdn_nw):
    def body(w_ref, u_ref, qg_ref, kd_ref, qk_ref, cd_ref, z_ref, nw_ref, ob_ref, o_ref, sin_ref, state):
        n = pl.program_id(0)

        @pl.when(n == 0)
        def _():
            state[...] = jnp.zeros((NH, DH, DH), F32)

        def head(cc, h):
            rows, lo = pl.ds(cc * CL, CL), h * DH
            st = state[h]
            sin_ref[cc, h] = st
            sb = st.astype(BF16)
            v_new = u_ref[rows, lo:lo + DH] - _dot(w_ref[rows, lo:lo + DH], sb)
            yield
            vb = v_new.astype(BF16)
            o = _dot(qg_ref[rows, lo:lo + DH], sb) + _dot(qk_ref[cc, h], vb)
            state[h] = st * cd_ref[cc, 0:1, h:h + 1] + _dot_tn(kd_ref[rows, lo:lo + DH], vb)
            yield
            o_ref[rows, lo:lo + DH] = o
            r = lax.rsqrt(jnp.mean(o * o, axis=-1, keepdims=True) + EPS)
            zh = z_ref[rows, lo:lo + DH]
            ob_ref[rows, lo:lo + DH] = o * r * nw_ref[...] * (zh * _sig(zh))

        for cc in range(CPS):
            _lockstep(head(cc, h) for h in range(NH))

    tile = lambda: pl.BlockSpec((TG, GW), lambda n: (n, 0))
    return pl.pallas_call(
        body, name="gdn_scan", grid=(NCH // CPS,),
        in_specs=[tile(), tile(), tile(), tile(), pl.BlockSpec((CPS, NH, CL, CL), lambda n: (n, 0, 0, 0)),
                  pl.BlockSpec((CPS, 8, LANES), lambda n: (n, 0, 0)), pl.BlockSpec((TG, GW), lambda n: (n, 5)),
                  _const((1, DH))],
        out_specs=(tile(), tile(), pl.BlockSpec((CPS, NH, DH, DH), lambda n: (n, 0, 0, 0))),
        out_shape=(jax.ShapeDtypeStruct((S, GW), F32), jax.ShapeDtypeStruct((S, GW), F32),
                   jax.ShapeDtypeStruct((NCH, NH, DH, DH), F32)),
        scratch_shapes=[pltpu.VMEM((NH, DH, DH), F32)],
        compiler_params=_params(dimension_semantics=("arbitrary",)),
    )(w_o, u_o, qg, kd, qk, cd, p_main, gdn_nw)


def _fwd_out(out_a, out_b, x, modnb, bada, w_out):
    def body(oa_ref, ob_ref, x_ref, mod_ref, b_ref, w_ref, x1_ref, mix_ref, oab_ref):
        oa = oa_ref[...].astype(BF16)
        ob = ob_ref[...].astype(BF16)
        oab_ref[:, 0:CW] = oa
        oab_ref[:, CW:D] = ob
        mix = _dot(oa, w_ref[0:CW, :]) + _dot(ob, w_ref[CW:D, :])
        mix_ref[...] = mix
        x1_ref[...] = x_ref[...] + _mod(mod_ref, b_ref, 2) * mix

    tile = lambda w: pl.BlockSpec((TM, w), lambda i: (i, 0))
    return pl.pallas_call(
        body, name="fwd_out", grid=(NT,),
        in_specs=[tile(CW), tile(GW), tile(D), _const((1, 6 * D)), _const((1, 6 * D)), _const((D, D))],
        out_specs=(tile(D), tile(D), tile(D)),
        out_shape=(jax.ShapeDtypeStruct((S, D), F32), jax.ShapeDtypeStruct((S, D), F32),
                   jax.ShapeDtypeStruct((S, D), BF16)),
        compiler_params=_params(dimension_semantics=("arbitrary",)),
    )(out_a, out_b, x, modnb, bada, w_out)


FFN_STATS = 8


def _ffn_forward(x1, tgt, modnb, bada, nw2, nfw, w_fi, w_fo):
    def body(x1_ref, tgt_ref, mod_ref, b_ref, nw2_ref, nfw_ref, wi_ref, wo_ref,
             hb_ref, act_ref, pre_ref, dx2_ref, dffn_ref, st_ref):
        i = pl.program_id(0)

        @pl.when(i == 0)
        def _():
            st_ref[...] = jnp.zeros((FFN_STATS, D), F32)

        sh2, sc2, gt2 = _mod(mod_ref, b_ref, 3), _mod(mod_ref, b_ref, 4), _mod(mod_ref, b_ref, 5)
        x1v = x1_ref[...]
        r2 = lax.rsqrt(jnp.mean(x1v * x1v, axis=-1, keepdims=True) + EPS)
        hb = ((x1v * r2 * nw2_ref[...]) * (1.0 + sc2) + sh2).astype(BF16)
        hb_ref[...] = hb
        ffn = jnp.zeros((TM, D), F32)
        for j in range(4):
            fgj = _dot_nt(hb, wi_ref[j])
            fuj = _dot_nt(hb, wi_ref[j + 4])
            pre_ref[j] = fgj.astype(BF16)
            pre_ref[j + 4] = fuj.astype(BF16)
            aj = (fgj * _sig(fgj) * fuj).astype(BF16)
            act_ref[j] = aj
            ffn = ffn + _dot(aj, wo_ref[j])
        x2 = x1v + gt2 * ffn
        r3 = lax.rsqrt(jnp.mean(x2 * x2, axis=-1, keepdims=True) + EPS)
        xr3 = x2 * r3
        err = xr3 * nfw_ref[...] - tgt_ref[...]
        loss = 0.5 * jnp.sum(jnp.mean(err * err, axis=-1, keepdims=True), axis=0, keepdims=True)
        dy = err * (1.0 / D)
        st_ref[0:1, :] += _colsum(dy * xr3)
        dyr = dy * nfw_ref[...]
        dx2 = r3 * (dyr - xr3 * jnp.mean(dyr * xr3, axis=-1, keepdims=True))
        st_ref[1:2, :] += _colsum(dx2 * ffn)
        st_ref[5:6, :] += jnp.broadcast_to(loss, (1, D))
        dx2_ref[...] = dx2
        dffn_ref[...] = (gt2 * dx2).astype(BF16)

    tile = lambda w: pl.BlockSpec((TM, w), lambda i: (i, 0))
    return pl.pallas_call(
        body, name="ffn_forward", grid=(NT,),
        in_specs=[tile(D), tile(D), _const((1, 6 * D)), _const((1, 6 * D)), _const((1, D)), _const((1, D)),
                  _const1((N_DEV, FB, D)), _const1((4, FB, D))],
        out_specs=(tile(D), pl.BlockSpec((4, TM, FB), lambda i: (0, i, 0)),
                   pl.BlockSpec((N_DEV, TM, FB), lambda i: (0, i, 0)), tile(D), tile(D), _const((FFN_STATS, D))),
        out_shape=(jax.ShapeDtypeStruct((S, D), BF16), jax.ShapeDtypeStruct((4, S, FB), BF16),
                   jax.ShapeDtypeStruct((N_DEV, S, FB), BF16), jax.ShapeDtypeStruct((S, D), F32),
                   jax.ShapeDtypeStruct((S, D), BF16), jax.ShapeDtypeStruct((FFN_STATS, D), F32)),
        compiler_params=_params(42, dimension_semantics=("arbitrary",)),
    )(x1, tgt, modnb, bada, nw2, nfw, w_fi, w_fo)


def _ffn_backward(dffn, pre, x1, dx2, modnb, bada, nw2, w_fi, w_fo):
    def body(dffn_ref, pre_ref, x1_ref, dx2_ref, mod_ref, b_ref, nw2_ref, wi_ref, wo_ref, df_ref, dx1_ref, st_ref):
        i = pl.program_id(0)

        @pl.when(i == 0)
        def _():
            st_ref[...] = jnp.zeros((FFN_STATS, D), F32)

        dffn = dffn_ref[...]
        dh = jnp.zeros((TM, D), F32)
        for j in range(4):
            fg = pre_ref[j].astype(F32)
            fu = pre_ref[j + 4].astype(F32)
            sg = _sig(fg)
            dact = _dot_nt(dffn, wo_ref[j])
            dfg = (dact * fu * (sg * (1.0 + fg * (1.0 - sg)))).astype(BF16)
            dfu = (dact * (fg * sg)).astype(BF16)
            df_ref[j] = dfg
            df_ref[j + 4] = dfu
            dh = dh + _dot(dfg, wi_ref[j]) + _dot(dfu, wi_ref[j + 4])
        x1v = x1_ref[...]
        r2 = lax.rsqrt(jnp.mean(x1v * x1v, axis=-1, keepdims=True) + EPS)
        xr2 = x1v * r2
        st_ref[2:3, :] += _colsum(dh)
        st_ref[3:4, :] += _colsum(dh * (xr2 * nw2_ref[...]))
        dxn = dh * (1.0 + _mod(mod_ref, b_ref, 4))
        st_ref[4:5, :] += _colsum(dxn * xr2)
        dxr = dxn * nw2_ref[...]
        dx1_ref[...] = dx2_ref[...] + r2 * (dxr - xr2 * jnp.mean(dxr * xr2, axis=-1, keepdims=True))

    tile = lambda w: pl.BlockSpec((TM, w), lambda i: (i, 0))
    wide = lambda: pl.BlockSpec((N_DEV, TM, FB), lambda i: (0, i, 0))
    return pl.pallas_call(
        body, name="ffn_backward", grid=(NT,),
        in_specs=[tile(D), wide(), tile(D), tile(D), _const((1, 6 * D)), _const((1, 6 * D)), _const((1, D)),
                  _const1((N_DEV, FB, D)), _const1((4, FB, D))],
        out_specs=(wide(), tile(D), _const((FFN_STATS, D))),
        out_shape=(jax.ShapeDtypeStruct((N_DEV, S, FB), BF16), jax.ShapeDtypeStruct((S, D), F32),
                   jax.ShapeDtypeStruct((FFN_STATS, D), F32)),
        compiler_params=_params(44, dimension_semantics=("arbitrary",)),
    )(dffn, pre, x1, dx2, modnb, bada, nw2, w_fi, w_fo)


def _grad_w(name, a, b, nb):
    m, n = a.shape[1], b.shape[1]

    def body(a_ref, b_ref, o_ref):
        o_ref[...] = _dot_tn(a_ref[...], b_ref[...]).astype(BF16)

    return pl.pallas_call(
        body, name=name, grid=(m // nb,),
        in_specs=[pl.BlockSpec((S, nb), lambda j: (0, j)), _const((S, n))],
        out_specs=pl.BlockSpec((nb, n), lambda j: (j, 0)),
        out_shape=jax.ShapeDtypeStruct((m, n), BF16),
        compiler_params=_params(dimension_semantics=("arbitrary",)),
    )(a, b)


GW_IN_ROWS = NMAIN + LANES


def _grad_w_in(dp_conf, dp_gdn, dp_ba, hb1):
    nb = 512
    n_conf, n_gdn = 2 * CW // nb, 4 * GW // nb

    def body(c_ref, g_ref, ba_ref, h_ref, o_ref):
        j = pl.program_id(0)

        @pl.when(j < n_conf)
        def _():
            o_ref[...] = _dot_tn(c_ref[...], h_ref[...]).astype(BF16)

        @pl.when((j >= n_conf) & (j < n_conf + n_gdn))
        def _():
            o_ref[...] = _dot_tn(g_ref[...], h_ref[...]).astype(BF16)

        @pl.when(j == n_conf + n_gdn)
        def _():
            o_ref[0:LANES, :] = _dot_tn(ba_ref[...], h_ref[...]).astype(BF16)

    return pl.pallas_call(
        body, name="grad_w_in", grid=(n_conf + n_gdn + 1,),
        in_specs=[pl.BlockSpec((S, nb), lambda j: (0, jnp.minimum(j, n_conf - 1))),
                  pl.BlockSpec((S, nb), lambda j: (0, jnp.clip(j - n_conf, 0, n_gdn - 1))),
                  _const((S, LANES)), _const((S, D))],
        out_specs=pl.BlockSpec((nb, D), lambda j: (j, 0)),
        out_shape=jax.ShapeDtypeStruct((GW_IN_ROWS, D), BF16),
        compiler_params=_params(dimension_semantics=("arbitrary",)),
    )(dp_conf, dp_gdn, dp_ba, hb1)


def _grad_w_ffn_in(hb2, df):
    def body(a_ref, b_ref, o_ref):
        o_ref[0] = _dot_tn(b_ref[0], a_ref[...]).astype(BF16)

    return pl.pallas_call(
        body, name="grad_w_ffn_in", grid=(N_DEV,),
        in_specs=[_const((S, D)), pl.BlockSpec((1, S, FB), lambda j: (j, 0, 0))],
        out_specs=pl.BlockSpec((1, FB, D), lambda j: (j, 0, 0)),
        out_shape=jax.ShapeDtypeStruct((N_DEV, FB, D), BF16),
        compiler_params=_params(dimension_semantics=("arbitrary",)),
    )(hb2, df)


def _grad_w_ffn_out(act, dffn):
    def body(a_ref, b_ref, o_ref):
        o_ref[0] = _dot_tn(a_ref[0], b_ref[...]).astype(BF16)

    return pl.pallas_call(
        body, name="grad_w_ffn_out", grid=(4,),
        in_specs=[pl.BlockSpec((1, S, FB), lambda j: (j, 0, 0)), _const((S, D))],
        out_specs=pl.BlockSpec((1, FB, D), lambda j: (j, 0, 0)),
        out_shape=jax.ShapeDtypeStruct((4, FB, D), BF16),
        compiler_params=_params(dimension_semantics=("arbitrary",)),
    )(act, dffn)


def _bwd_out(dx1, mix, modnb, bada, w_out):
    def body(dx_ref, mix_ref, mod_ref, b_ref, w_ref, dmix_ref, doa_ref, dob_ref, st_ref):
        i = pl.program_id(0)

        @pl.when(i == 0)
        def _():
            st_ref[...] = jnp.zeros((8, D), F32)

        dx = dx_ref[...]
        st_ref[0:1, :] += _colsum(dx * mix_ref[...])
        dmix = (_mod(mod_ref, b_ref, 2) * dx).astype(BF16)
        dmix_ref[...] = dmix
        doa_ref[...] = _dot_nt(dmix, w_ref[0:CW, :])
        dob_ref[...] = _dot_nt(dmix, w_ref[CW:D, :])

    tile = lambda w: pl.BlockSpec((TM, w), lambda i: (i, 0))
    return pl.pallas_call(
        body, name="bwd_out", grid=(NT,),
        in_specs=[tile(D), tile(D), _const((1, 6 * D)), _const((1, 6 * D)), _const((D, D))],
        out_specs=(tile(D), tile(CW), tile(GW), _const((8, D))),
        out_shape=(jax.ShapeDtypeStruct((S, D), BF16), jax.ShapeDtypeStruct((S, CW), F32),
                   jax.ShapeDtypeStruct((S, GW), F32), jax.ShapeDtypeStruct((8, D), F32)),
        compiler_params=_params(dimension_semantics=("arbitrary",)),
    )(dx1, mix, modnb, bada, w_out)


CONF_STATS = 40


def _conf_bwd(d_out_a, y, p_main, conv_w, gn_w, gn_b):
    def body(do_ref, y_ref, a_ref, g_ref, ah_ref, gh_ref, w_ref, gw_ref, gb_ref, dp_ref, st_ref,
             ubuf, dybuf, ush, dysh):
        i = pl.program_id(0)

        @pl.when(i == 0)
        def _():
            st_ref[...] = jnp.zeros((CONF_STATS, CW), F32)
            dybuf[TM:TM + HALO, :] = jnp.zeros((HALO, CW), F32)

        pm = _group_mean_matrix().astype(BF16)
        yv = y_ref[...]
        dlt = yv - _group_mean(yv, pm)
        rstd = lax.rsqrt(_group_mean(dlt * dlt, pm) + EPS)
        un = dlt * rstd
        o = un * gw_ref[...] + gb_ref[...]
        so = _sig(o)
        d_o = do_ref[...] * (so * (1.0 + o * (1.0 - so)))
        st_ref[33:34, :] += _colsum(d_o)
        st_ref[32:33, :] += _colsum(d_o * un)
        dun = d_o * gw_ref[...]
        dy = rstd * (dun - _group_mean(dun, pm) - un * _group_mean(dun * un, pm))
        st_ref[31:32, :] += _colsum(dy)
        dybuf[0:TM, :] = dy
        _fill_shifted(dybuf, dysh)

        a = a_ref[...]
        sg = _sig(g_ref[...])
        first = i == NT - 1
        ubuf[0:HALO, :] = jnp.where(first, 0.0, ah_ref[...] * _sig(gh_ref[...]))
        ubuf[HALO:HALO + TM, :] = a * sg
        _fill_shifted(ubuf, ush)
        du = jnp.zeros((TM, CW), F32)
        for k in range(KC):
            st_ref[k:k + 1, :] += _colsum(dy * _rows_at(ubuf, ush, HALO - (KC - 1) + k))
            du = du + w_ref[k:k + 1, :] * _rows_at(dybuf, dysh, KC - 1 - k)
        dybuf[TM:TM + HALO, :] = dybuf[0:HALO, :]
        dp_ref[:, 0:CW] = (du * sg).astype(BF16)
        dp_ref[:, CW:2 * CW] = (du * a * sg * (1.0 - sg)).astype(BF16)

    rev = lambda w, j=0: pl.BlockSpec((TM, w), lambda i: (NT - 1 - i, j))
    halo = lambda j: pl.BlockSpec((HALO, CW), lambda i: (jnp.maximum((NT - 1 - i) * (TM // HALO) - 1, 0), j))
    return pl.pallas_call(
        body, name="conf_bwd", grid=(NT,),
        in_specs=[rev(CW), rev(CW), rev(CW, 0), rev(CW, 1), halo(0), halo(1),
                  _const((KC, CW)), _const((1, CW)), _const((1, CW))],
        out_specs=(rev(2 * CW), _const((CONF_STATS, CW))),
        out_shape=(jax.ShapeDtypeStruct((S, 2 * CW), BF16), jax.ShapeDtypeStruct((CONF_STATS, CW), F32)),
        scratch_shapes=[pltpu.VMEM((HALO + TM, CW), F32), pltpu.VMEM((TM + HALO, CW), F32),
                        pltpu.VMEM((SUB - 1, SHIFT_ROWS, CW), F32), pltpu.VMEM((SUB - 1, SHIFT_ROWS, CW), F32)],
        compiler_params=_params(dimension_semantics=("arbitrary",)),
    )(d_out_a, y, p_main, p_main, p_main, p_main, conv_w, gn_w, gn_b)


GDN_STATS = 8


def _gdn_bwd(d_out_b, o_pre, s_in, t_inv, p_main, p_ba, gdn_conv_w, alog_l, dt_l, gdn_nw):
    def body(dob_ref, o_ref, sin_ref, t_ref, q_ref, k_ref, v_ref, z_ref, qh_ref, kh_ref, vh_ref, ba_ref,
             w_ref, al_ref, dt_ref, nw_ref, dp_ref, dba_ref, st_ref, xbuf, dcbuf, dstate):
        n = pl.program_id(0)

        @pl.when(n == 0)
        def _():
            st_ref[...] = jnp.zeros((GDN_STATS, 3 * GW), F32)
            dcbuf[CL:CL + SH, :] = jnp.zeros((SH, 3 * GW), F32)
            dstate[...] = jnp.zeros((NH, DH, DH), F32)

        for cc in reversed(range(CPS)):
            chunk(n, cc, dob_ref, o_ref, sin_ref, t_ref, q_ref, k_ref, v_ref, z_ref, qh_ref, kh_ref, vh_ref, ba_ref,
                  w_ref, al_ref, dt_ref, nw_ref, dp_ref, dba_ref, st_ref, xbuf, dcbuf, dstate)

    def chunk(n, cc, dob_ref, o_ref, sin_ref, t_ref, q_ref, k_ref, v_ref, z_ref, qh_ref, kh_ref, vh_ref, ba_ref,
              w_ref, al_ref, dt_ref, nw_ref, dp_ref, dba_ref, st_ref, xbuf, dcbuf, dstate):
        r0 = cc * CL
        if cc == 0:
            first = n == NCH // CPS - 1
            xbuf[0:SH, 0:GW] = jnp.where(first, 0.0, qh_ref[...])
            xbuf[0:SH, GW:2 * GW] = jnp.where(first, 0.0, kh_ref[...])
            xbuf[0:SH, 2 * GW:3 * GW] = jnp.where(first, 0.0, vh_ref[...])
        else:
            xbuf[0:SH, 0:GW] = q_ref[r0 - SH:r0, :]
            xbuf[0:SH, GW:2 * GW] = k_ref[r0 - SH:r0, :]
            xbuf[0:SH, 2 * GW:3 * GW] = v_ref[r0 - SH:r0, :]
        xbuf[SH:SH + CL, 0:GW] = q_ref[r0:r0 + CL, :]
        xbuf[SH:SH + CL, GW:2 * GW] = k_ref[r0:r0 + CL, :]
        xbuf[SH:SH + CL, 2 * GW:3 * GW] = v_ref[r0:r0 + CL, :]
        conv = _short_conv(w_ref, xbuf)
        sc = _sig(conv)
        qkv = conv * sc
        ba = ba_ref[r0:r0 + CL, :]
        beta_all, g_all, xg, neg_a = _gdn_gates(ba, al_ref[...], dt_ref[...])
        gcum, gcum_t = _gdn_cumsum(g_all)
        lane = lax.broadcasted_iota(jnp.int32, (CL, LANES), 1)
        row = lax.broadcasted_iota(jnp.int32, (CL, 1), 0)
        acc = dict(dgcum=jnp.zeros((CL, LANES), F32), dbeta=jnp.zeros((CL, LANES), F32))

        def head(h):
            lo = h * DH
            qh = qkv[:, lo:lo + DH]
            kh = qkv[:, GW + lo:GW + lo + DH]
            vh = qkv[:, 2 * GW + lo:2 * GW + lo + DH]
            beta = beta_all[:, h:h + 1]
            f = _head_terms(qh, kh, beta, gcum[:, NH + h:NH + h + 1], gcum_t[NH + h:NH + h + 1, :])
            qn, kn, qs, kb, gam, kds, cd, decay = (f[s] for s in ("qn", "kn", "qs", "kb", "gam", "kds", "cd", "decay"))
            t = t_ref[cc, h]
            st = sin_ref[cc, h]
            vb = vh * beta
            kbg = kb * gam
            u = _dot(t, vb, GP)
            w = _dot(t, kbg, GP)
            yield
            v_new = u - _dot(w, st, GP)
            q_dec = qs * gam
            k_dec = kn * kds

            o = o_ref[r0:r0 + CL, lo:lo + DH]
            zh = z_ref[r0:r0 + CL, lo:lo + DH]
            sz = _sig(zh)
            r = lax.rsqrt(jnp.mean(o * o, axis=-1, keepdims=True) + EPS)
            orr = o * r
            d_out = dob_ref[r0:r0 + CL, lo:lo + DH]
            dz = d_out * (orr * nw_ref[...]) * (sz * (1.0 + zh * (1.0 - sz)))
            don = d_out * (zh * sz)
            st_ref[4:5, 0:DH] += _colsum(don * orr)
            tt = don * nw_ref[...]
            d_o = r * (tt - orr * jnp.mean(tt * orr, axis=-1, keepdims=True))

            yield
            ds_out = dstate[h]
            dv_new = _dot_tn(f["qk"], d_o, GP) + _dot(k_dec, ds_out, GP)
            dqk = jnp.where(f["causal"], _dot_nt(d_o, v_new, GP), 0.0)
            dq_dec = _dot_nt(d_o, st, GP)
            dk_dec = _dot_nt(v_new, ds_out, GP)
            yield
            dstate[h] = _dot_tn(q_dec, d_o, GP) + cd * ds_out - _dot_tn(w, dv_new, GP)
            dcd = jnp.sum(_rowsum(st * ds_out), axis=0, keepdims=True)
            dw = -_dot_nt(dv_new, st, GP)
            dvb = _dot_tn(t, dv_new, GP)
            yield
            dt_m = _dot_nt(dv_new, vb, GP) + _dot_nt(dw, kbg, GP)
            dkbg = _dot_tn(t, dw, GP)
            yield
            dtt = _dot_nt(dt_m, t, GP)
            yield
            da = jnp.where(f["strict"], -_dot_tn(t, dtt, GP), 0.0)
            yield
            dad = da * decay
            dqkd = dqk * decay
            dkb = _dot(dad, kn, GP) + dkbg * gam
            dkn = _dot_tn(dad, kb, GP) + _dot_tn(dqkd, qs, GP) + dk_dec * kds + dkb * beta
            dqs = _dot(dqkd, kn, GP) + dq_dec * gam
            yield
            m = da * f["a"] + dqk * f["qk"]
            tk = _rowsum(dk_dec * k_dec)
            dgl = jnp.sum(tk, axis=0, keepdims=True) + dcd * cd
            dgc = (_rowsum(m) - _rowsum(jnp.transpose(m)) + _rowsum(dq_dec * q_dec) - tk + _rowsum(dkbg * kbg)
                   + jnp.where(row == CL - 1, dgl, 0.0))
            dbeta = _rowsum(dkb * kn) + _rowsum(dvb * vh)
            acc["dgcum"] = acc["dgcum"] + jnp.where(lane == NH + h, dgc, 0.0)
            acc["dbeta"] = acc["dbeta"] + jnp.where(lane == h, dbeta, 0.0)
            dvh = dvb * beta
            dqn = dqs * QSCALE
            dqh = f["rq"] * (dqn - qn * _rowsum(dqn * qn))
            dkh = f["rk"] * (dkn - kn * _rowsum(dkn * kn))
            dsilu = lambda c0: sc[:, c0:c0 + DH] * (1.0 + conv[:, c0:c0 + DH] * (1.0 - sc[:, c0:c0 + DH]))
            dcbuf[0:CL, lo:lo + DH] = dqh * dsilu(lo)
            dcbuf[0:CL, GW + lo:GW + lo + DH] = dkh * dsilu(GW + lo)
            dcbuf[0:CL, 2 * GW + lo:2 * GW + lo + DH] = dvh * dsilu(2 * GW + lo)
            dp_ref[r0:r0 + CL, 3 * GW + lo:3 * GW + lo + DH] = dz.astype(BF16)

        _lockstep(head(h) for h in range(NH))
        dgcum_all, dbeta_all = acc["dgcum"], acc["dbeta"]

        ii, jj = _tri_iota()
        upper = jnp.where(ii <= jj, 1.0, 0.0).astype(BF16)
        dg_all = _ones_dot(upper, dgcum_all)
        dxg = dg_all * neg_a * _sig(xg)
        st_ref[5:6, 0:LANES] += _colsum(dg_all * g_all)
        st_ref[6:7, 0:LANES] += _colsum(dxg)
        dbl = dbeta_all * beta_all * (1.0 - beta_all)
        dba_ref[r0:r0 + CL, :] = jnp.where(lane < NH, dbl, jnp.where(lane < 2 * NH, dxg, 0.0)).astype(BF16)

        dconv = dcbuf[0:CL, :]
        dx = w_ref[0:1, :] * dcbuf[KS - 1:KS - 1 + CL, :]
        st_ref[0:1, :] += _colsum(dconv * xbuf[SH - KS + 1:SH - KS + 1 + CL, :])
        for k in range(1, KS):
            off = SH - (KS - 1) + k
            st_ref[k:k + 1, :] += _colsum(dconv * xbuf[off:off + CL, :])
            dx = dx + w_ref[k:k + 1, :] * dcbuf[KS - 1 - k:KS - 1 - k + CL, :]
        dcbuf[CL:CL + SH, :] = dcbuf[0:SH, :]
        dp_ref[r0:r0 + CL, 0:3 * GW] = dx.astype(BF16)

    steps = NCH // CPS
    rev = lambda w, j=0: pl.BlockSpec((TG, w), lambda n: (steps - 1 - n, j))
    halo = lambda j: pl.BlockSpec((SH, GW), lambda n: (jnp.maximum((steps - 1 - n) * (TG // SH) - 1, 0), j))
    blk4 = lambda a, b: pl.BlockSpec((CPS, NH, a, b), lambda n: (steps - 1 - n, 0, 0, 0))
    return pl.pallas_call(
        body, name="gdn_bwd", grid=(steps,),
        in_specs=[rev(GW), rev(GW), blk4(DH, DH), blk4(CL, CL), rev(GW, 2), rev(GW, 3), rev(GW, 4), rev(GW, 5),
                  halo(2), halo(3), halo(4), rev(LANES), _const((KS, 3 * GW)), _const((1, LANES)),
                  _const((1, LANES)), _const((1, DH))],
        out_specs=(rev(4 * GW), rev(LANES), _const((GDN_STATS, 3 * GW))),
        out_shape=(jax.ShapeDtypeStruct((S, 4 * GW), BF16), jax.ShapeDtypeStruct((S, LANES), BF16),
                   jax.ShapeDtypeStruct((GDN_STATS, 3 * GW), F32)),
        scratch_shapes=[pltpu.VMEM((SH + CL, 3 * GW), F32), pltpu.VMEM((CL + SH, 3 * GW), F32),
                        pltpu.VMEM((NH, DH, DH), F32)],
        compiler_params=_params(dimension_semantics=("arbitrary",)),
    )(d_out_b, o_pre, s_in, t_inv, p_main, p_main, p_main, p_main, p_main, p_main, p_main, p_ba,
      gdn_conv_w, alog_l, dt_l, gdn_nw)


def _bwd_in(dp_conf, dp_gdn, dp_ba, x, dx1, nw1, modnb, bada, w_main, w_ba):
    def body(dc_ref, dg_ref, db_ref, x_ref, dx1_ref, nw_ref, mod_ref, b_ref, wm_ref, wb_ref, gx_ref, st_ref):
        i = pl.program_id(0)

        @pl.when(i == 0)
        def _():
            st_ref[...] = jnp.zeros((8, D), F32)

        dh = (_dot(dc_ref[...], wm_ref[0:2 * CW, :]) + _dot(dg_ref[...], wm_ref[2 * CW:NMAIN, :])
              + _dot(db_ref[...], wb_ref[...]))
        xv = x_ref[...]
        r = lax.rsqrt(jnp.mean(xv * xv, axis=-1, keepdims=True) + EPS)
        xr = xv * r
        st_ref[0:1, :] += _colsum(dh)
        st_ref[1:2, :] += _colsum(dh * (xr * nw_ref[...]))
        dxn = dh * (1.0 + _mod(mod_ref, b_ref, 1))
        st_ref[2:3, :] += _colsum(dxn * xr)
        dxr = dxn * nw_ref[...]
        gx_ref[...] = dx1_ref[...] + r * (dxr - xr * jnp.mean(dxr * xr, axis=-1, keepdims=True))

    tile = lambda w: pl.BlockSpec((TI, w), lambda i: (i, 0))
    return pl.pallas_call(
        body, name="bwd_in", grid=(S // TI,),
        in_specs=[tile(2 * CW), tile(4 * GW), tile(LANES), tile(D), tile(D), _const((1, D)), _const((1, 6 * D)),
                  _const((1, 6 * D)), _const1((NMAIN, D)), _const((LANES, D))],
        out_specs=(tile(D), _const((8, D))),
        out_shape=(jax.ShapeDtypeStruct((S, D), F32), jax.ShapeDtypeStruct((8, D), F32)),
        compiler_params=_params(dimension_semantics=("arbitrary",)),
    )(dp_conf, dp_gdn, dp_ba, x, dx1, nw1, modnb, bada, w_main, w_ba)


def _adamw(w, g, m, v):
    m = ADAM_B1 * m + (1.0 - ADAM_B1) * g
    v = ADAM_B2 * v + (1.0 - ADAM_B2) * (g * g)
    m_hat = m / BC1
    v_hat = v / BC2
    delta = -ADAM_LR * (m_hat / (jnp.sqrt(v_hat) + ADAM_EPS) + ADAM_WD * w)
    return delta, m, v


ADAM_BLOCK_BYTES = 6 * 1024 * 1024


def _adam_tile(rows, cols):
    padded = -(-cols // LANES) * LANES
    if N_DEV * rows * padded * 4 <= ADAM_BLOCK_BYTES:
        return rows, cols
    best = None
    for tr in range(16, rows, 16):
        if rows % tr == 0 and N_DEV * tr * padded * 4 <= ADAM_BLOCK_BYTES:
            best = tr
    if best is not None:
        return best, cols
    rows_padded = -(-rows // 16) * 16
    tc = LANES
    for cand in range(LANES, cols, LANES):
        if cols % cand == 0 and N_DEV * rows_padded * cand * 4 <= ADAM_BLOCK_BYTES:
            tc = cand
    return rows, tc


def _reduce_adam(name, parts, w, m, v, own=None):
    rows, cols = w.shape
    tr, tc = _adam_tile(rows, cols)

    def body(*refs):
        p_ref, w_ref, m_ref, v_ref = refs[:4]
        g_ref, d_ref, nm_ref, nv_ref = refs[-4:]
        if own is None:
            part = lambda j: p_ref[j].astype(F32)
        else:
            me = 4 * lax.axis_index("x") + 2 * lax.axis_index("y") + lax.axis_index("c")
            part = lambda j: jnp.where(me == j, refs[4][...], p_ref[j]).astype(F32)
        g = part(0)
        for j in range(1, N_DEV):
            g = g + part(j)
        g_ref[...] = g
        d_ref[...], nm_ref[...], nv_ref[...] = _adamw(w_ref[...], g, m_ref[...], v_ref[...])

    blk = pl.BlockSpec((tr, tc), lambda i, j: (i, j))
    sds = jax.ShapeDtypeStruct((rows, cols), F32)
    extra = [] if own is None else [own]
    return pl.pallas_call(
        body, name=name, grid=(rows // tr, cols // tc),
        in_specs=[pl.BlockSpec((N_DEV, tr, tc), lambda i, j: (0, i, j)), blk, blk, blk] + [blk] * len(extra),
        out_specs=(blk, blk, blk, blk), out_shape=(sds, sds, sds, sds),
        compiler_params=_params(dimension_semantics=("arbitrary", "arbitrary")),
    )(parts, w, m, v, *extra)


def _ada_adam(c_all, dmod_sh, w, m, v):
    rows, cols = w.shape
    tr = 256

    def body(c_ref, dm_ref, w_ref, m_ref, v_ref, g_ref, d_ref, nm_ref, nv_ref):
        cv = c_ref[...]
        g = _dot_tn(cv * _sig(cv), dm_ref[...], HI)
        g_ref[...] = g
        d_ref[...], nm_ref[...], nv_ref[...] = _adamw(w_ref[...], g, m_ref[...], v_ref[...])

    blk = pl.BlockSpec((tr, cols), lambda i: (i, 0))
    sds = jax.ShapeDtypeStruct((rows, cols), F32)
    return pl.pallas_call(
        body, name="ada_adam", grid=(rows // tr,),
        in_specs=[pl.BlockSpec((N_DEV, tr), lambda i: (0, i)), _const((N_DEV, cols)), blk, blk, blk],
        out_specs=(blk, blk, blk, blk), out_shape=(sds, sds, sds, sds),
        compiler_params=_params(dimension_semantics=("arbitrary",)),
    )(c_all, dmod_sh, w, m, v)


def _lanes(a, at=0):
    return jnp.pad(a, ((0, 0), (at, LANES - at - a.shape[1])))


WEIGHT_NAMES = ["w_ada", "b_ada", "norm_mix_w", "w_in", "conv_w", "conv_b", "conv_gn_w", "conv_gn_b", "gdn_conv_w",
                "gdn_a_log", "gdn_dt_bias", "gdn_norm_w", "w_out", "norm_ffn_w", "w_ffn_in", "w_ffn_out",
                "norm_final_w"]


SMALL_LAYOUT = [("b_ada", 0, 48, LANES), ("norm_mix_w", 48, 8, LANES), ("norm_ffn_w", 56, 8, LANES),
                ("norm_final_w", 64, 8, LANES), ("conv_b", 72, 4, LANES), ("conv_gn_w", 76, 4, LANES),
                ("conv_gn_b", 80, 4, LANES), ("gdn_norm_w", 84, 1, LANES), ("gdn_a_log", 85, 1, NH),
                ("gdn_dt_bias", 86, 1, NH)]
LOSS_ROW = 87


def _adam_small(g_small, weights, m1, m2):
    names = [nm for nm, _, _, _ in SMALL_LAYOUT]
    k = len(names)

    def body(*refs):
        g_ref = refs[0]
        w_refs, m_refs, v_refs = refs[1:1 + k], refs[1 + k:1 + 2 * k], refs[1 + 2 * k:1 + 3 * k]
        loss_ref = refs[1 + 3 * k]
        outs = refs[2 + 3 * k:2 + 7 * k]
        total = refs[-1]
        g = g_ref[0]
        for j in range(1, N_DEV):
            g = g + g_ref[j]
        total[...] = g
        loss_ref[...] = total[LOSS_ROW:LOSS_ROW + 1, :]
        for i, (_, r0, rows, lanes) in enumerate(SMALL_LAYOUT):
            gp = total[r0:r0 + rows, 0:lanes]
            outs[i][...] = gp
            outs[k + i][...], outs[2 * k + i][...], outs[3 * k + i][...] = _adamw(
                w_refs[i][...], gp, m_refs[i][...], v_refs[i][...])

    shapes = [jax.ShapeDtypeStruct((rows, lanes), F32) for _, _, rows, lanes in SMALL_LAYOUT]
    res = pl.pallas_call(
        body, name="adam_small",
        out_shape=tuple([jax.ShapeDtypeStruct((1, LANES), F32)] + shapes * 4),
        scratch_shapes=[pltpu.VMEM((SMALL_ROWS, LANES), F32)],
        compiler_params=_params(),
    )(g_small, *[weights[n] for n in names], *[m1[n] for n in names], *[m2[n] for n in names])
    kinds = [dict(zip(names, res[1 + q * k:1 + (q + 1) * k])) for q in range(4)]
    return res[0], kinds


def _mix_forward(w, xs, modnb, between=None):
    w_main = w["w_in"]
    w_ba = jnp.pad(w["w_in"][NMAIN:], ((0, LANES - 2 * NH), (0, 0)))
    alog_l = _lanes(w["gdn_a_log"], NH)
    dt_l = _lanes(w["gdn_dt_bias"], NH)
    p_main, p_ba, hb1 = _fwd_in(xs, w["norm_mix_w"], modnb, w["b_ada"], w_main, w_ba)
    w_o, u_o, qg, kd, qk, cd, t_inv = _gdn_prep(p_main, p_ba, w["gdn_conv_w"], alog_l, dt_l)
    out_b, o_pre, s_in = _gdn_scan(w_o, u_o, qg, kd, qk, cd, p_main, w["gdn_norm_w"])
    conv_b = w["conv_b"] if between is None else _after(w["conv_b"], between(out_b))
    y_conv, out_a = _conf_fwd(p_main, w["conv_w"], conv_b, w["conv_gn_w"], w["conv_gn_b"])
    return dict(w_main=w_main, w_ba=w_ba, alog_l=alog_l, dt_l=dt_l, p_main=p_main, p_ba=p_ba, hb1=hb1,
                y_conv=y_conv, out_a=out_a, out_b=out_b, o_pre=o_pre, s_in=s_in, t_inv=t_inv)


def _ffn_stage(w, f, xs, tgt, modnb):
    x1, mix, oab = _fwd_out(f["out_a"], f["out_b"], xs, modnb, w["b_ada"], w["w_out"])
    hb2, act, pre, dx2, dffn, st_fwd = _ffn_forward(x1, tgt, modnb, w["b_ada"], w["norm_ffn_w"],
                                                    w["norm_final_w"], w["w_ffn_in"], w["w_ffn_out"])
    gw_ffn_out = _grad_w_ffn_out(act, dffn)
    df, dx1, st_bwd = _ffn_backward(dffn, pre, x1, dx2, modnb, w["b_ada"], w["norm_ffn_w"], w["w_ffn_in"],
                                    w["w_ffn_out"])
    gw_ffn_in = _grad_w_ffn_in(hb2, df)
    return dict(mix=mix, oab=oab, dx1=dx1, st_ffn=st_fwd + st_bwd, gw_ffn_in=gw_ffn_in, gw_ffn_out=gw_ffn_out)


def _out_backward(w, g, modnb):
    dmix, d_out_a, d_out_b, st_out = _bwd_out(g["dx1"], g["mix"], modnb, w["b_ada"], w["w_out"])
    return dict(d_out_a=d_out_a, d_out_b=d_out_b, st_out=st_out, gw_out=_grad_w("grad_w_out", g["oab"], dmix, 512))


def _heads_backward(w, f, a):
    dp_conf, st_conf = _conf_bwd(a["d_out_a"], f["y_conv"], f["p_main"], w["conv_w"], w["conv_gn_w"],
                                 w["conv_gn_b"])
    dp_gdn, dp_ba, st_gdn = _gdn_bwd(a["d_out_b"], f["o_pre"], f["s_in"], f["t_inv"], f["p_main"], f["p_ba"],
                                     w["gdn_conv_w"], f["alog_l"], f["dt_l"], w["gdn_norm_w"])
    gw_in = _grad_w_in(dp_conf, dp_gdn, dp_ba, f["hb1"])[:NIN]
    return dict(dp_conf=dp_conf, dp_gdn=dp_gdn, dp_ba=dp_ba, st_conf=st_conf, st_gdn=st_gdn, gw_in=gw_in,
                gw_conv=st_conf[0:KC], gw_gconv=st_gdn[0:KS])


def _in_backward(w, f, g, a, h, xs, modnb):
    st_out, st_conf, st_gdn, st_ffn = a["st_out"], h["st_conf"], h["st_gdn"], g["st_ffn"]
    grad_x, st_in = _bwd_in(h["dp_conf"], h["dp_gdn"], h["dp_ba"], xs, g["dx1"], w["norm_mix_w"], modnb,
                            w["b_ada"], f["w_main"], f["w_ba"])
    dmod = jnp.concatenate([st_in[0:1], st_in[1:2], st_out[0:1], st_ffn[2:3], st_ffn[3:4], st_ffn[1:2]], axis=1)
    small = jnp.concatenate([
        dmod.reshape(48, LANES), st_in[2:3].reshape(8, LANES), st_ffn[4:5].reshape(8, LANES),
        st_ffn[0:1].reshape(8, LANES), st_conf[31:32].reshape(4, LANES), st_conf[32:33].reshape(4, LANES),
        st_conf[33:34].reshape(4, LANES), st_gdn[4:5, 0:LANES],
        _lanes(st_gdn[5:6, NH:2 * NH]), _lanes(st_gdn[6:7, NH:2 * NH]), st_ffn[5:6, 0:LANES]], axis=0)
    return dict(grad_x=grad_x, small=small)


def _local(w, xs, tgt, modnb):
    f = _mix_forward(w, xs, modnb)
    g = _ffn_stage(w, f, xs, tgt, modnb)
    a = _out_backward(w, g, modnb)
    h = _heads_backward(w, f, a)
    b = _in_backward(w, f, g, a, h, xs, modnb)
    return dict(b, gw_in=h["gw_in"], gw_conv=h["gw_conv"], gw_gconv=h["gw_gconv"], gw_out=a["gw_out"],
                gw_ffn_in=g["gw_ffn_in"], gw_ffn_out=g["gw_ffn_out"])


def kernel(x, c, w_ada, b_ada, norm_mix_w, w_in, conv_w, conv_b, conv_gn_w, conv_gn_b, gdn_conv_w, gdn_a_log, gdn_dt_bias, gdn_norm_w, w_out, norm_ffn_w, w_ffn_in, w_ffn_out, norm_final_w, loss_target, m_w_ada, m_b_ada, m_norm_mix_w, m_w_in, m_conv_w, m_conv_b, m_conv_gn_w, m_conv_gn_b, m_gdn_conv_w, m_gdn_a_log, m_gdn_dt_bias, m_gdn_norm_w, m_w_out, m_norm_ffn_w, m_w_ffn_in, m_w_ffn_out, m_norm_final_w, v_w_ada, v_b_ada, v_norm_mix_w, v_w_in, v_conv_w, v_conv_b, v_conv_gn_w, v_conv_gn_b, v_gdn_conv_w, v_gdn_a_log, v_gdn_dt_bias, v_gdn_norm_w, v_w_out, v_norm_ffn_w, v_w_ffn_in, v_w_ffn_out, v_norm_final_w):
    me = 4 * lax.axis_index("x") + 2 * lax.axis_index("y") + lax.axis_index("c")
    xs = x.reshape(S, D)
    tgt = loss_target.reshape(S, D)

    late = [w_out[0].astype(BF16), jnp.transpose(w_ffn_in[0]).astype(BF16), w_ffn_out[0].astype(BF16)]
    g_c, g_cw, g_gcw, g_win, *late_lands = _gather_two_level(
        "gather_weights", [c, conv_w[0], gdn_conv_w[0], jnp.transpose(w_in[0]).astype(BF16)] + late,
        seed_only=(4, 5, 6))
    c_all = g_c.reshape(N_DEV, D)
    g_mod, mod_token = _exchange("gather_mod", [_mod_shard(c_all, w_ada[0])], [False], with_token=True)
    modnb = lax.dynamic_index_in_dim(g_mod, me, axis=1, keepdims=False).reshape(1, 6 * D)
    late_started = _exchange_start("gather_late_start", [_after(late[0], mod_token)] + late[1:], late_lands,
                                   [False] * 3, only=LEVEL_ONE)
    modnb = _after(modnb, late_started[-1])
    w = dict(b_ada=b_ada, norm_mix_w=norm_mix_w, conv_b=conv_b, conv_gn_w=conv_gn_w, conv_gn_b=conv_gn_b,
             gdn_a_log=gdn_a_log, gdn_dt_bias=gdn_dt_bias, gdn_norm_w=gdn_norm_w, norm_ffn_w=norm_ffn_w,
             norm_final_w=norm_final_w.reshape(1, D),
             conv_w=jnp.transpose(g_cw, (1, 0, 2)).reshape(KC, CW),
             gdn_conv_w=jnp.transpose(g_gcw, (1, 0, 2)).reshape(KS, 3 * GW),
             w_in=g_win.reshape(NIN, D))

    relay = {}

    def relay_late(out_b):
        _, late_landed = _exchange_wait("gather_late_wait", late_started, [False] * 3, (out_b,), only=LEVEL_ONE)
        relay["started"] = _relay_start("gather_late_relay_start", late_landed)
        return relay["started"][-1]

    f = _mix_forward(w, xs, modnb, relay_late)
    g_wout, g_wfi, g_wfo = _relay_wait("gather_late_relay_wait", relay["started"], (f["out_a"],))
    w.update(w_out=g_wout.reshape(D, D), w_ffn_in=g_wfi, w_ffn_out=g_wfo.reshape(4, FB, D))
    g = _ffn_stage(w, f, xs, tgt, modnb)

    ffn_grads = [g["gw_ffn_in"], g["gw_ffn_out"].reshape(N_DEV, DFF // N_DEV, D)]
    ffn_started = _exchange_start("scatter_ffn_start", ffn_grads,
                                  [lax.empty(a.shape, a.dtype) for a in ffn_grads], [True] * 2)
    a = _out_backward(w, g, _after(modnb, ffn_started[-1]))
    out_grads = [a["gw_out"].reshape(N_DEV, D // N_DEV, D)]
    out_started = _exchange_start("scatter_out_start", out_grads,
                                  [lax.empty(t.shape, t.dtype) for t in out_grads], [True])
    h = _heads_backward(dict(w, conv_gn_w=_after(w["conv_gn_w"], out_started[-1])), f, a)

    in_grads = [h["gw_in"].reshape(N_DEV, NIN // N_DEV, D),
                jnp.transpose(h["gw_conv"].reshape(KC, N_DEV, CW // N_DEV), (1, 0, 2)),
                jnp.transpose(h["gw_gconv"].reshape(KS, N_DEV, 3 * GW // N_DEV), (1, 0, 2))]
    in_started = _exchange_start("scatter_in_start", in_grads,
                                 [lax.empty(t.shape, t.dtype) for t in in_grads], [True] * 3)
    loc = _in_backward(w, f, g, a, h, xs, _after(modnb, in_started[-1]))
    small_started = _exchange_start("gather_small_start", [loc["small"]],
                                    [lax.empty((N_DEV, SMALL_ROWS, LANES), F32)], [False])

    def own(sent):
        return lax.dynamic_index_in_dim(sent, me, axis=0, keepdims=False)

    big = {}
    (sent_fi, sent_fo), (r_fi, r_fo) = _exchange_wait("scatter_ffn_wait", ffn_started, [True] * 2,
                                                         (small_started[-1],))
    big["w_ffn_in"] = [jnp.transpose(t) for t in _reduce_adam(
        "adam_w_ffn_in", r_fi, jnp.transpose(w_ffn_in[0]), jnp.transpose(m_w_ffn_in[0]),
        jnp.transpose(v_w_ffn_in[0]), own(sent_fi))]
    big["w_ffn_out"] = _reduce_adam("adam_w_ffn_out", r_fo, w_ffn_out[0], m_w_ffn_out[0], v_w_ffn_out[0],
                                    own(sent_fo))
    (sent_out,), (r_out,) = _exchange_wait("scatter_out_wait", out_started, [True], (big["w_ffn_out"][0],))
    big["w_out"] = _reduce_adam("adam_w_out", r_out, w_out[0], m_w_out[0], v_w_out[0], own(sent_out))

    (sent_small,), (r_small,) = _exchange_wait("gather_small_wait", small_started, [False], (big["w_out"][0],))
    slot = lax.broadcasted_iota(jnp.int32, (N_DEV, 1, 1), 0)
    g_small = jnp.where(slot == me, sent_small[None], r_small)
    def views(b_, nm_, nf_, nl_, cb_, gw_, gb_, gn_, al_, dt_):
        arrs = [b_, nm_, nf_, nl_, cb_, gw_, gb_, gn_, al_, dt_]
        return {nm: t.reshape(rows, lanes) for (nm, _, rows, lanes), t in zip(SMALL_LAYOUT, arrs)}

    loss_row, res = _adam_small(
        g_small,
        views(b_ada, norm_mix_w, norm_ffn_w, norm_final_w, conv_b, conv_gn_w, conv_gn_b, gdn_norm_w, gdn_a_log,
              gdn_dt_bias),
        views(m_b_ada, m_norm_mix_w, m_norm_ffn_w, m_norm_final_w, m_conv_b, m_conv_gn_w, m_conv_gn_b,
              m_gdn_norm_w, m_gdn_a_log, m_gdn_dt_bias),
        views(v_b_ada, v_norm_mix_w, v_norm_ffn_w, v_norm_final_w, v_conv_b, v_conv_gn_w, v_conv_gn_b,
              v_gdn_norm_w, v_gdn_a_log, v_gdn_dt_bias))
    loss = loss_row[0, 0]
    small_shapes = dict(b_ada=(1, 6 * D), norm_mix_w=(1, D), norm_ffn_w=(1, D), norm_final_w=(D,),
                        conv_b=(1, CW), conv_gn_w=(1, CW), conv_gn_b=(1, CW), gdn_norm_w=(1, DH),
                        gdn_a_log=(1, NH), gdn_dt_bias=(1, NH))
    res = [{nm: t.reshape(small_shapes[nm]) for nm, t in kind.items()} for kind in res]

    dmod_rows = g_small[:, 0:48, :].reshape(N_DEV, 6 * D)
    dmod_sh = lax.dynamic_slice_in_dim(dmod_rows, me * (6 * D // N_DEV), 6 * D // N_DEV, axis=1)

    big["w_ada"] = _ada_adam(c_all, dmod_sh, w_ada[0], m_w_ada[0], v_w_ada[0])
    (sent_in, sent_cw, sent_gcw), (r_in, r_cw, r_gcw) = _exchange_wait(
        "scatter_in_wait", in_started, [True] * 3, (big["w_ada"][0],))
    big["w_in"] = [jnp.transpose(t) for t in _reduce_adam(
        "adam_w_in", r_in, jnp.transpose(w_in[0]), jnp.transpose(m_w_in[0]), jnp.transpose(v_w_in[0]),
        own(sent_in))]
    big["conv_w"] = _reduce_adam("adam_conv_w", r_cw, conv_w[0], m_conv_w[0], v_conv_w[0], own(sent_cw))
    big["gdn_conv_w"] = _reduce_adam("adam_gdn_conv_w", r_gcw, gdn_conv_w[0], m_gdn_conv_w[0], v_gdn_conv_w[0],
                                     own(sent_gcw))
    outs = [loss, loc["grad_x"].reshape(1, S, D)]
    for kind in range(4):
        for nm in WEIGHT_NAMES:
            outs.append(big[nm][kind][None] if nm in big else res[kind][nm])
    return tuple(outs)
```

```python
import functools

import jax
import jax.numpy as jnp
from jax import lax
from jax.experimental import pallas as pl
from jax.experimental.pallas import tpu as pltpu

F32 = jnp.float32
BF16 = jnp.bfloat16
HI = lax.Precision.HIGHEST
MESH = pl.DeviceIdType.MESH

N_DEV = 8
S = 2048
D = 1024
TM = 256
NT = S // TM
CW = 512
KC = 31
NG = 8
GSZ = CW // NG
HALO = 32
GW = 512
NH = 4
DH = 128
KS = 4
SH = 8
CL = 64
NCH = S // CL
NMAIN = 2 * CW + 4 * GW
NIN = NMAIN + 2 * NH
DFF = 2816
FB = DFF // 4
EPS = 1e-6
QSCALE = DH ** -0.5
LANES = 128
SMALL_ROWS = 88

ADAM_LR = 0.001
ADAM_B1 = 0.9
ADAM_B2 = 0.999
ADAM_EPS = 1e-08
ADAM_WD = 0.01
ADAM_STEP = 10
BC1 = 1.0 - ADAM_B1 ** ADAM_STEP
BC2 = 1.0 - ADAM_B2 ** ADAM_STEP

MIB = 1024 * 1024
VMEM_LIMIT_MIB = 32


def _params(limit_mib=VMEM_LIMIT_MIB, **kw):
    return pltpu.CompilerParams(vmem_limit_bytes=limit_mib * MIB, **kw)


def _sig(x):
    return jax.nn.sigmoid(x)


GP = BF16


def _operands(a, b, prec):
    if prec is BF16:
        return a.astype(BF16), b.astype(BF16), None
    return a, b, prec


def _dot(a, b, prec=None):
    a, b, prec = _operands(a, b, prec)
    return jnp.dot(a, b, preferred_element_type=F32, precision=prec)


def _dot_nt(a, b, prec=None):
    a, b, prec = _operands(a, b, prec)
    return lax.dot_general(a, b, (((1,), (1,)), ((), ())), preferred_element_type=F32, precision=prec)


def _dot_tn(a, b, prec=None):
    a, b, prec = _operands(a, b, prec)
    return lax.dot_general(a, b, (((0,), (0,)), ((), ())), preferred_element_type=F32, precision=prec)


def _lockstep(gens):
    gens = list(gens)
    while gens:
        alive = []
        for g in gens:
            try:
                next(g)
                alive.append(g)
            except StopIteration:
                pass
        gens = alive


def _rowsum(x):
    return jnp.sum(x, axis=-1, keepdims=True)


def _colsum(x):
    return jnp.sum(x, axis=0, keepdims=True)


def _mod(mod_ref, b_ref, k):
    return mod_ref[:, k * D:(k + 1) * D] + b_ref[:, k * D:(k + 1) * D]


def _const(shape):
    nd = len(shape)
    return pl.BlockSpec(shape, lambda *_: (0,) * nd)


def _const1(shape):
    nd = len(shape)
    return pl.BlockSpec(shape, lambda *_: (0,) * nd, pipeline_mode=pl.Buffered(1))


PEER_FLIPS = [(dx, dy, dc) for dx in (0, 1) for dy in (0, 1) for dc in (0, 1)][1:]


def _after(x, token):
    return x + token[0:1, 0:1].astype(x.dtype).reshape((1,) * x.ndim)


def _exchange(name, srcs, per_dest, seed_only=(), with_token=False):
    n = len(srcs)
    out_shape = []
    for a, pd in zip(srcs, per_dest):
        blk = a.shape[1:] if pd else a.shape
        out_shape.append(jax.ShapeDtypeStruct((N_DEV,) + tuple(blk), a.dtype))

    def body(*refs):
        src = refs[:n]
        dst = refs[n:2 * n]
        send_sems, recv_sems, local_sems = refs[-3:]
        if with_token:
            refs[2 * n][...] = jnp.zeros((8, LANES), F32)
        x, y, c = lax.axis_index("x"), lax.axis_index("y"), lax.axis_index("c")
        me = 4 * x + 2 * y + c

        def piece(i, j):
            return src[i].at[j] if per_dest[i] else src[i]

        copies = []
        for k, (dx, dy, dc) in enumerate(PEER_FLIPS):
            px = 1 - x if dx else x
            py = 1 - y if dy else y
            pc = 1 - c if dc else c
            pj = 4 * px + 2 * py + pc
            for i in range(n):
                if i in seed_only:
                    continue
                cp = pltpu.make_async_remote_copy(
                    src_ref=piece(i, pj), dst_ref=dst[i].at[me],
                    send_sem=send_sems.at[k * n + i], recv_sem=recv_sems.at[k * n + i],
                    device_id=(px, py, pc), device_id_type=MESH)
                cp.start()
                arrive = pltpu.make_async_remote_copy(
                    src_ref=piece(i, pj), dst_ref=dst[i].at[pj],
                    send_sem=send_sems.at[k * n + i], recv_sem=recv_sems.at[k * n + i],
                    device_id=(px, py, pc), device_id_type=MESH)
                copies.append((cp, arrive))
        own = []
        for i in range(n):
            lc = pltpu.make_async_copy(piece(i, me), dst[i].at[me], local_sems.at[i])
            lc.start()
            own.append(lc)
        for cp, arrive in copies:
            arrive.wait_recv()
        for cp, arrive in copies:
            cp.wait_send()
        for lc in own:
            lc.wait()

    any_spec = pl.BlockSpec(memory_space=pl.ANY)
    out_specs = [any_spec] * n
    if with_token:
        out_shape.append(jax.ShapeDtypeStruct((8, LANES), F32))
        out_specs.append(pl.BlockSpec(memory_space=pltpu.VMEM))
    return pl.pallas_call(
        body, name=name, out_shape=tuple(out_shape),
        in_specs=[any_spec] * n, out_specs=tuple(out_specs),
        scratch_shapes=[pltpu.SemaphoreType.DMA((7 * n,)), pltpu.SemaphoreType.DMA((7 * n,)),
                        pltpu.SemaphoreType.DMA((n,))],
        compiler_params=pltpu.CompilerParams(has_side_effects=True),
    )(*srcs)


CHIP_FLIPS = [(0, 1), (1, 0), (1, 1)]
LEVEL_ONE = [k for k, (dx, dy, dc) in enumerate(PEER_FLIPS) if (dx, dy, dc) == (0, 0, 1) or dc == 0]


def _chip_peers(x, y):
    return [(1 - x if dx else x, 1 - y if dy else y) for dx, dy in CHIP_FLIPS]


def _gather_two_level(name, srcs, seed_only=()):
    n = len(srcs)
    live = [i for i in range(n) if i not in seed_only]

    def body(*refs):
        src, dst = refs[:n], refs[n:2 * n]
        send_sems, recv_sems, local_sems = refs[2 * n:2 * n + 3]
        bounce = refs[2 * n + 3:]
        x, y, c = lax.axis_index("x"), lax.axis_index("y"), lax.axis_index("c")
        me = 4 * x + 2 * y + c
        sibling = (x, y, 1 - c)
        chips = _chip_peers(x, y)

        def copy(k, i, src_ref, slot, to):
            return pltpu.make_async_remote_copy(
                src_ref=src_ref, dst_ref=dst[i].at[slot], send_sem=send_sems.at[k * n + i],
                recv_sem=recv_sems.at[k * n + i], device_id=to, device_id_type=MESH)

        first = []
        for i in live:
            first.append(copy(0, i, src[i], me, sibling))
            first += [copy(1 + j, i, src[i], me, (px, py, c)) for j, (px, py) in enumerate(chips)]
        for cp in first:
            cp.start()
        up = [pltpu.make_async_copy(src[i], bounce[i], local_sems.at[i]) for i in range(n)]
        for cp in up:
            cp.start()
        for cp in up:
            cp.wait()
        own = [pltpu.make_async_copy(bounce[i], dst[i].at[me], local_sems.at[i]) for i in range(n)]
        for cp in own:
            cp.start()
        passed = []
        for j, (px, py) in enumerate(chips):
            slot = 4 * px + 2 * py + c
            for i in live:
                copy(1 + j, i, src[i], slot, (px, py, c)).wait_recv()
                fwd = copy(4 + j, i, dst[i].at[slot], slot, sibling)
                fwd.start()
                passed.append(fwd)
        for i in live:
            copy(0, i, src[i], 4 * x + 2 * y + 1 - c, sibling).wait_recv()
            for j, (px, py) in enumerate(chips):
                copy(4 + j, i, src[i], 4 * px + 2 * py + 1 - c, sibling).wait_recv()
        for cp in first + passed:
            cp.wait_send()
        for cp in own:
            cp.wait()

    any_spec = pl.BlockSpec(memory_space=pl.ANY)
    return pl.pallas_call(
        body, name=name, out_shape=tuple(jax.ShapeDtypeStruct((N_DEV,) + a.shape, a.dtype) for a in srcs),
        in_specs=[any_spec] * n, out_specs=tuple([any_spec] * n),
        scratch_shapes=[pltpu.SemaphoreType.DMA((7 * n,)), pltpu.SemaphoreType.DMA((7 * n,)),
                        pltpu.SemaphoreType.DMA((n,))] + [pltpu.VMEM(a.shape, a.dtype) for a in srcs],
        compiler_params=pltpu.CompilerParams(has_side_effects=True),
    )(*srcs)


def _relay_copy(land, sems, i, n, j, slot, sibling):
    send_sems, recv_sems = sems
    return pltpu.make_async_remote_copy(
        src_ref=land[i].at[slot], dst_ref=land[i].at[slot], send_sem=send_sems.at[j * n + i],
        recv_sem=recv_sems.at[j * n + i], device_id=sibling, device_id_type=MESH)


def _relay_start(name, lands):
    n = len(lands)

    def body(*refs):
        land = refs[:n]
        sems = refs[n], refs[n + 1]
        x, y, c = lax.axis_index("x"), lax.axis_index("y"), lax.axis_index("c")
        for j, (px, py) in enumerate(_chip_peers(x, y)):
            for i in range(n):
                _relay_copy(land, sems, i, n, j, 4 * px + 2 * py + c, (x, y, 1 - c)).start()
        refs[-1][...] = jnp.zeros((8, LANES), F32)

    return pl.pallas_call(
        body, name=name,
        out_shape=(pltpu.SemaphoreType.DMA((3 * n,)), pltpu.SemaphoreType.DMA((3 * n,)),
                   *[pltpu.HBM(a.shape, a.dtype) for a in lands], jax.ShapeDtypeStruct((8, LANES), F32)),
        in_specs=[HBM_SPEC] * n,
        out_specs=(SEM_SPEC, SEM_SPEC, *[HBM_SPEC] * n, pl.BlockSpec(memory_space=pltpu.VMEM)),
        input_output_aliases={i: 2 + i for i in range(n)},
        compiler_params=pltpu.CompilerParams(has_side_effects=DATAFLOW),
    )(*[pltpu.with_memory_space_constraint(a, pltpu.HBM) for a in lands])


def _relay_wait(name, started, after):
    n = len(started) - 3
    arrays = list(started[2:2 + n])

    def body(*refs):
        land = refs[:n]
        sems = refs[n], refs[n + 1]
        x, y, c = lax.axis_index("x"), lax.axis_index("y"), lax.axis_index("c")
        for j, (px, py) in enumerate(_chip_peers(x, y)):
            for i in range(n):
                _relay_copy(land, sems, i, n, j, 4 * px + 2 * py + c, (x, y, 1 - c)).wait_send()
                _relay_copy(land, sems, i, n, j, 4 * px + 2 * py + 1 - c, (x, y, 1 - c)).wait_recv()

    return pl.pallas_call(
        body, name=name,
        out_shape=tuple(pltpu.HBM(a.shape, a.dtype) for a in arrays),
        in_specs=[HBM_SPEC] * n + [SEM_SPEC, SEM_SPEC] + [pl.BlockSpec(memory_space=pl.ANY)] * len(after),
        out_specs=tuple([HBM_SPEC] * n),
        input_output_aliases={i: i for i in range(n)},
        compiler_params=pltpu.CompilerParams(has_side_effects=DATAFLOW),
    )(*arrays, started[0], started[1], *after)


HBM_SPEC = pl.BlockSpec(memory_space=pltpu.HBM)
SEM_SPEC = pl.BlockSpec(memory_space=pltpu.SEMAPHORE)
DATAFLOW = pltpu.SideEffectType.DATAFLOW_SIDE_EFFECTING


def _peers(only=None):
    x, y, c = lax.axis_index("x"), lax.axis_index("y"), lax.axis_index("c")
    out = []
    for k, (dx, dy, dc) in enumerate(PEER_FLIPS):
        if only is not None and k not in only:
            continue
        px = 1 - x if dx else x
        py = 1 - y if dy else y
        pc = 1 - c if dc else c
        out.append((k, (px, py, pc), 4 * px + 2 * py + pc))
    return 4 * x + 2 * y + c, out


def _exchange_start(name, srcs, lands, per_dest, only=None):
    n = len(srcs)

    def body(*refs):
        src, land = refs[:n], refs[n:2 * n]
        send_sems, recv_sems = refs[2 * n], refs[2 * n + 1]
        token = refs[-1]
        me, peers = _peers(only)
        for k, peer, pj in peers:
            for i in range(n):
                pltpu.make_async_remote_copy(
                    src_ref=src[i].at[pj] if per_dest[i] else src[i], dst_ref=land[i].at[me],
                    send_sem=send_sems.at[k * n + i], recv_sem=recv_sems.at[k * n + i],
                    device_id=peer, device_id_type=MESH).start()
        token[...] = jnp.zeros((8, LANES), F32)

    arrays = list(srcs) + list(lands)
    return pl.pallas_call(
        body, name=name,
        out_shape=(pltpu.SemaphoreType.DMA((7 * n,)), pltpu.SemaphoreType.DMA((7 * n,)),
                   *[pltpu.HBM(a.shape, a.dtype) for a in arrays], jax.ShapeDtypeStruct((8, LANES), F32)),
        in_specs=[HBM_SPEC] * (2 * n),
        out_specs=(SEM_SPEC, SEM_SPEC, *[HBM_SPEC] * (2 * n), pl.BlockSpec(memory_space=pltpu.VMEM)),
        input_output_aliases={i: 2 + i for i in range(2 * n)},
        compiler_params=pltpu.CompilerParams(has_side_effects=DATAFLOW),
    )(*[pltpu.with_memory_space_constraint(a, pltpu.HBM) for a in arrays])


def _exchange_wait(name, started, per_dest, after, only=None):
    n = (len(started) - 3) // 2
    send_sems, recv_sems = started[0], started[1]
    arrays = list(started[2:2 + 2 * n])

    def body(*refs):
        src, land = refs[:n], refs[n:2 * n]
        send, recv = refs[2 * n], refs[2 * n + 1]
        me, peers = _peers(only)
        for k, peer, pj in peers:
            for i in range(n):
                cp = pltpu.make_async_remote_copy(
                    src_ref=src[i].at[pj] if per_dest[i] else src[i], dst_ref=land[i].at[pj],
                    send_sem=send.at[k * n + i], recv_sem=recv.at[k * n + i],
                    device_id=peer, device_id_type=MESH)
                cp.wait_send()
                cp.wait_recv()

    outs = pl.pallas_call(
        body, name=name,
        out_shape=tuple(pltpu.HBM(a.shape, a.dtype) for a in arrays),
        in_specs=[HBM_SPEC] * (2 * n) + [SEM_SPEC, SEM_SPEC] + [pl.BlockSpec(memory_space=pl.ANY)] * len(after),
        out_specs=tuple([HBM_SPEC] * (2 * n)),
        input_output_aliases={i: i for i in range(2 * n)},
        compiler_params=pltpu.CompilerParams(has_side_effects=DATAFLOW),
    )(*arrays, send_sems, recv_sems, *after)
    return outs[:n], outs[n:]


def _mod_shard(c_all, w_ada):
    def body(c_ref, w_ref, o_ref):
        cv = c_ref[...]
        ca = cv * _sig(cv)
        o_ref[...] = _dot(ca.astype(BF16), w_ref[...].astype(BF16))

    return pl.pallas_call(
        body, name="mod_shard", out_shape=jax.ShapeDtypeStruct((N_DEV, w_ada.shape[1]), F32),
        compiler_params=_params(),
    )(c_all, w_ada)


TI = 512


def _fwd_in(x, nw1, modnb, bada, w_main, w_ba):
    def body(x_ref, nw_ref, mod_ref, b_ref, wm_ref, wb_ref, pm_ref, pb_ref, hb_ref):
        xv = x_ref[...]
        r = lax.rsqrt(jnp.mean(xv * xv, axis=-1, keepdims=True) + EPS)
        h = (xv * r * nw_ref[...]) * (1.0 + _mod(mod_ref, b_ref, 1)) + _mod(mod_ref, b_ref, 0)
        hb = h.astype(BF16)
        hb_ref[...] = hb
        pm_ref[...] = _dot_nt(hb, wm_ref[...])
        pb_ref[...] = _dot_nt(hb, wb_ref[...])

    return pl.pallas_call(
        body, name="fwd_in", grid=(S // TI,),
        in_specs=[pl.BlockSpec((TI, D), lambda i: (i, 0)), _const((1, D)), _const((1, 6 * D)), _const((1, 6 * D)),
                  _const1((NMAIN, D)), _const((LANES, D))],
        out_specs=(pl.BlockSpec((TI, NMAIN), lambda i: (i, 0)), pl.BlockSpec((TI, LANES), lambda i: (i, 0)),
                   pl.BlockSpec((TI, D), lambda i: (i, 0))),
        out_shape=(jax.ShapeDtypeStruct((S, NMAIN), F32), jax.ShapeDtypeStruct((S, LANES), F32),
                   jax.ShapeDtypeStruct((S, D), BF16)),
        compiler_params=_params(dimension_semantics=("arbitrary",)),
    )(x, nw1, modnb, bada, w_main, w_ba)


def _group_mean_matrix():
    ii = lax.broadcasted_iota(jnp.int32, (CW, CW), 0) // GSZ
    jj = lax.broadcasted_iota(jnp.int32, (CW, CW), 1) // GSZ
    return jnp.where(ii == jj, 1.0 / GSZ, 0.0).astype(F32)


SUB = 8
SHIFT_ROWS = HALO + TM - SUB


def _fill_shifted(buf, sh):
    for b in range(1, SUB):
        sh[b - 1] = buf[b:b + SHIFT_ROWS, :]


def _rows_at(buf, sh, off):
    a, b = divmod(off, SUB)
    if b == 0:
        return buf[off:off + TM, :]
    return sh[b - 1, SUB * a:SUB * a + TM, :]


def _group_mean(x, pm):
    hi = x.astype(BF16)
    r1 = x - hi.astype(F32)
    mid = r1.astype(BF16)
    lo = (r1 - mid.astype(F32)).astype(BF16)
    return _dot(hi, pm) + _dot(mid, pm) + _dot(lo, pm)


def _conf_fwd(p_main, conv_w, conv_b, gn_w, gn_b):
    def body(a_ref, g_ref, w_ref, b_ref, gw_ref, gb_ref, y_ref, oa_ref, ubuf, ush):
        i = pl.program_id(0)

        @pl.when(i == 0)
        def _():
            ubuf[0:HALO, :] = jnp.zeros((HALO, CW), F32)

        ubuf[HALO:HALO + TM, :] = a_ref[...] * _sig(g_ref[...])
        _fill_shifted(ubuf, ush)
        acc = jnp.zeros((TM, CW), F32) + b_ref[...]
        for k in range(KC):
            acc = acc + w_ref[k:k + 1, :] * _rows_at(ubuf, ush, HALO - (KC - 1) + k)
        y_ref[...] = acc
        ubuf[0:HALO, :] = ubuf[TM:TM + HALO, :]
        pm = _group_mean_matrix().astype(BF16)
        dlt = acc - _group_mean(acc, pm)
        var = _group_mean(dlt * dlt, pm)
        o = dlt * lax.rsqrt(var + EPS) * gw_ref[...] + gb_ref[...]
        oa_ref[...] = o * _sig(o)

    return pl.pallas_call(
        body, name="conf_fwd", grid=(NT,),
        in_specs=[pl.BlockSpec((TM, CW), lambda i: (i, 0)), pl.BlockSpec((TM, CW), lambda i: (i, 1)),
                  _const((KC, CW)), _const((1, CW)), _const((1, CW)), _const((1, CW))],
        out_specs=(pl.BlockSpec((TM, CW), lambda i: (i, 0)), pl.BlockSpec((TM, CW), lambda i: (i, 0))),
        out_shape=(jax.ShapeDtypeStruct((S, CW), F32), jax.ShapeDtypeStruct((S, CW), F32)),
        scratch_shapes=[pltpu.VMEM((HALO + TM, CW), F32), pltpu.VMEM((SUB - 1, SHIFT_ROWS, CW), F32)],
        compiler_params=_params(dimension_semantics=("arbitrary",)),
    )(p_main, p_main, conv_w, conv_b, gn_w, gn_b)


def _tri_iota():
    ii = lax.broadcasted_iota(jnp.int32, (CL, CL), 0)
    jj = lax.broadcasted_iota(jnp.int32, (CL, CL), 1)
    return ii, jj


def _gdn_gates(ba, alog_l, dt_l):
    beta_all = _sig(ba)
    xg = ba + dt_l
    sp = jnp.maximum(xg, 0.0) + jnp.log(1.0 + jnp.exp(-jnp.abs(xg)))
    neg_a = -jnp.exp(alog_l)
    return beta_all, neg_a * sp, xg, neg_a


def _ones_dot(ones, x):
    hi = x.astype(BF16)
    r1 = x - hi.astype(F32)
    mid = r1.astype(BF16)
    lo = (r1 - mid.astype(F32)).astype(BF16)
    return _dot(ones, hi) + _dot(ones, mid) + _dot(ones, lo)


def _gdn_cumsum(g_all):
    ii, jj = _tri_iota()
    low = jnp.where(ii >= jj, 1.0, 0.0).astype(BF16)
    gcum = _ones_dot(low, g_all)
    return gcum, jnp.transpose(gcum)


def _split(x):
    hi = x.astype(BF16)
    return hi, (x - hi.astype(F32)).astype(BF16)


def _dot_split(a, b):
    (ah, al), (bh, bl) = a, b
    return _dot(ah, bh) + (_dot(ah, bl) + _dot(al, bh))


def _unit_lower_inverses(mats):
    ii, jj = _tri_iota()
    eye = jnp.where(ii == jj, 1.0, 0.0).astype(F32)
    ts = [eye - a for a in mats]
    ps = [_dot_split(s, s) for s in map(_split, mats)]
    for _ in range(4):
        sp = [_split(p) for p in ps]
        ts = [t + _dot_split(_split(t), s) for t, s in zip(ts, sp)]
        ps = [_dot_split(s, s) for s in sp]
    return [t + _dot_split(_split(t), _split(p)) for t, p in zip(ts, ps)]


def _head_terms(qh, kh, beta, gcol, grow):
    ii, jj = _tri_iota()
    causal = ii >= jj
    strict = ii > jj
    rq = lax.rsqrt(_rowsum(qh * qh) + EPS)
    rk = lax.rsqrt(_rowsum(kh * kh) + EPS)
    qn = qh * rq
    kn = kh * rk
    qs = qn * QSCALE
    decay = jnp.where(causal, jnp.exp(jnp.where(causal, gcol - grow, 0.0)), 0.0)
    gam = jnp.exp(gcol)
    gl = gcol[CL - 1:CL, :]
    kds = jnp.exp(gl - gcol)
    cd = jnp.exp(gl)
    kb = kn * beta
    a = jnp.where(strict, _dot_nt(kb, kn, GP) * decay, 0.0)
    qk = jnp.where(causal, _dot_nt(qs, kn, GP) * decay, 0.0)
    return dict(rq=rq, rk=rk, qn=qn, kn=kn, qs=qs, decay=decay, gam=gam, kds=kds, cd=cd, kb=kb, a=a, qk=qk,
                causal=causal, strict=strict)


def _short_conv(w_ref, buf, rows=CL):
    acc = w_ref[0:1, :] * buf[SH - KS + 1:SH - KS + 1 + rows, :]
    for k in range(1, KS):
        off = SH - (KS - 1) + k
        acc = acc + w_ref[k:k + 1, :] * buf[off:off + rows, :]
    return acc


CPS = 4
TG = CPS * CL


def _gdn_prep(p_main, p_ba, gdn_conv_w, alog_l, dt_l):
    def body(q_ref, k_ref, v_ref, qh_ref, kh_ref, vh_ref, ba_ref, w_ref, al_ref, dt_ref,
             wo_ref, uo_ref, qg_ref, kd_ref, qk_ref, cd_ref, t_ref, xbuf):
        i = pl.program_id(0)
        first = i == 0
        xbuf[0:SH, 0:GW] = jnp.where(first, 0.0, qh_ref[...])
        xbuf[0:SH, GW:2 * GW] = jnp.where(first, 0.0, kh_ref[...])
        xbuf[0:SH, 2 * GW:3 * GW] = jnp.where(first, 0.0, vh_ref[...])
        xbuf[SH:SH + TG, 0:GW] = q_ref[...]
        xbuf[SH:SH + TG, GW:2 * GW] = k_ref[...]
        xbuf[SH:SH + TG, 2 * GW:3 * GW] = v_ref[...]
        conv = _short_conv(w_ref, xbuf, TG)
        qkv = conv * _sig(conv)
        beta_all, g_all, _, _ = _gdn_gates(ba_ref[...], al_ref[...], dt_ref[...])
        lane = lax.broadcasted_iota(jnp.int32, (8, LANES), 1)
        cums = [_gdn_cumsum(g_all[cc * CL:(cc + 1) * CL, :]) for cc in range(CPS)]
        pairs = [(cc, h) for cc in range(CPS) for h in range(NH)]
        terms, vbs = [], []
        for cc, h in pairs:
            r0, lo = cc * CL, h * DH
            beta = beta_all[r0:r0 + CL, h:h + 1]
            gcum, gcum_t = cums[cc]
            terms.append(_head_terms(qkv[r0:r0 + CL, lo:lo + DH], qkv[r0:r0 + CL, GW + lo:GW + lo + DH], beta,
                                     gcum[:, NH + h:NH + h + 1], gcum_t[NH + h:NH + h + 1, :]))
            vbs.append(qkv[r0:r0 + CL, 2 * GW + lo:2 * GW + lo + DH] * beta)
        invs = _unit_lower_inverses([f["a"] for f in terms])
        cds = [jnp.zeros((8, LANES), F32) for _ in range(CPS)]
        for (cc, h), f, t, vb in zip(pairs, terms, invs, vbs):
            r0, lo = cc * CL, h * DH
            t_ref[cc, h] = t
            uo_ref[r0:r0 + CL, lo:lo + DH] = _dot(t, vb, GP)
            wo_ref[r0:r0 + CL, lo:lo + DH] = _dot(t, f["kb"] * f["gam"], GP).astype(BF16)
            qg_ref[r0:r0 + CL, lo:lo + DH] = (f["qs"] * f["gam"]).astype(BF16)
            kd_ref[r0:r0 + CL, lo:lo + DH] = (f["kn"] * f["kds"]).astype(BF16)
            qk_ref[cc, h] = f["qk"].astype(BF16)
            cds[cc] = cds[cc] + jnp.where(lane == h, f["cd"], 0.0)
        for cc in range(CPS):
            cd_ref[cc] = cds[cc]

    col = lambda j: pl.BlockSpec((TG, GW), lambda i: (i, j))
    halo = lambda j: pl.BlockSpec((SH, GW), lambda i: (jnp.maximum(i * (TG // SH) - 1, 0), j))
    tile = lambda: pl.BlockSpec((TG, GW), lambda i: (i, 0))
    sq = lambda: pl.BlockSpec((CPS, NH, CL, CL), lambda i: (i, 0, 0, 0))
    return pl.pallas_call(
        body, name="gdn_prep", grid=(NCH // CPS,),
        in_specs=[col(2), col(3), col(4), halo(2), halo(3), halo(4), pl.BlockSpec((TG, LANES), lambda i: (i, 0)),
                  _const((KS, 3 * GW)), _const((1, LANES)), _const((1, LANES))],
        out_specs=(tile(), tile(), tile(), tile(), sq(), pl.BlockSpec((CPS, 8, LANES), lambda i: (i, 0, 0)), sq()),
        out_shape=(jax.ShapeDtypeStruct((S, GW), BF16), jax.ShapeDtypeStruct((S, GW), F32),
                   jax.ShapeDtypeStruct((S, GW), BF16), jax.ShapeDtypeStruct((S, GW), BF16),
                   jax.ShapeDtypeStruct((NCH, NH, CL, CL), BF16), jax.ShapeDtypeStruct((NCH, 8, LANES), F32),
                   jax.ShapeDtypeStruct((NCH, NH, CL, CL), F32)),
        scratch_shapes=[pltpu.VMEM((SH + TG, 3 * GW), F32)],
        compiler_params=_params(dimension_semantics=("arbitrary",)),
    )(p_main, p_main, p_main, p_main, p_main, p_main, p_ba, gdn_conv_w, alog_l, dt_l)


def _gdn_scan(w_o, u_o, qg, kd, qk, cd, p_main, gdn_nw):
    def body(w_ref, u_ref, qg_ref, kd_ref, qk_ref, cd_ref, z_ref, nw_ref, ob_ref, o_ref, sin_ref, state):
        n = pl.program_id(0)

        @pl.when(n == 0)
        def _():
            state[...] = jnp.zeros((NH, DH, DH), F32)

        def head(cc, h):
            rows, lo = pl.ds(cc * CL, CL), h * DH
            st = state[h]
            sin_ref[cc, h] = st
            sb = st.astype(BF16)
            v_new = u_ref[rows, lo:lo + DH] - _dot(w_ref[rows, lo:lo + DH], sb)
            yield
            vb = v_new.astype(BF16)
            o = _dot(qg_ref[rows, lo:lo + DH], sb) + _dot(qk_ref[cc, h], vb)
            state[h] = st * cd_ref[cc, 0:1, h:h + 1] + _dot_tn(kd_ref[rows, lo:lo + DH], vb)
            yield
            o_ref[rows, lo:lo + DH] = o
            r = lax.rsqrt(jnp.mean(o * o, axis=-1, keepdims=True) + EPS)
            zh = z_ref[rows, lo:lo + DH]
            ob_ref[rows, lo:lo + DH] = o * r * nw_ref[...] * (zh * _sig(zh))

        for cc in range(CPS):
            _lockstep(head(cc, h) for h in range(NH))

    tile = lambda: pl.BlockSpec((TG, GW), lambda n: (n, 0))
    return pl.pallas_call(
        body, name="gdn_scan", grid=(NCH // CPS,),
        in_specs=[tile(), tile(), tile(), tile(), pl.BlockSpec((CPS, NH, CL, CL), lambda n: (n, 0, 0, 0)),
                  pl.BlockSpec((CPS, 8, LANES), lambda n: (n, 0, 0)), pl.BlockSpec((TG, GW), lambda n: (n, 5)),
                  _const((1, DH))],
        out_specs=(tile(), tile(), pl.BlockSpec((CPS, NH, DH, DH), lambda n: (n, 0, 0, 0))),
        out_shape=(jax.ShapeDtypeStruct((S, GW), F32), jax.ShapeDtypeStruct((S, GW), F32),
                   jax.ShapeDtypeStruct((NCH, NH, DH, DH), F32)),
        scratch_shapes=[pltpu.VMEM((NH, DH, DH), F32)],
        compiler_params=_params(dimension_semantics=("arbitrary",)),
    )(w_o, u_o, qg, kd, qk, cd, p_main, gdn_nw)


def _fwd_out(out_a, out_b, x, modnb, bada, w_out):
    def body(oa_ref, ob_ref, x_ref, mod_ref, b_ref, w_ref, x1_ref, mix_ref, oab_ref):
        oa = oa_ref[...].astype(BF16)
        ob = ob_ref[...].astype(BF16)
        oab_ref[:, 0:CW] = oa
        oab_ref[:, CW:D] = ob
        mix = _dot(oa, w_ref[0:CW, :]) + _dot(ob, w_ref[CW:D, :])
        mix_ref[...] = mix
        x1_ref[...] = x_ref[...] + _mod(mod_ref, b_ref, 2) * mix

    tile = lambda w: pl.BlockSpec((TM, w), lambda i: (i, 0))
    return pl.pallas_call(
        body, name="fwd_out", grid=(NT,),
        in_specs=[tile(CW), tile(GW), tile(D), _const((1, 6 * D)), _const((1, 6 * D)), _const((D, D))],
        out_specs=(tile(D), tile(D), tile(D)),
        out_shape=(jax.ShapeDtypeStruct((S, D), F32), jax.ShapeDtypeStruct((S, D), F32),
                   jax.ShapeDtypeStruct((S, D), BF16)),
        compiler_params=_params(dimension_semantics=("arbitrary",)),
    )(out_a, out_b, x, modnb, bada, w_out)


FFN_STATS = 8


def _ffn_forward(x1, tgt, modnb, bada, nw2, nfw, w_fi, w_fo):
    def body(x1_ref, tgt_ref, mod_ref, b_ref, nw2_ref, nfw_ref, wi_ref, wo_ref,
             hb_ref, act_ref, pre_ref, dx2_ref, dffn_ref, st_ref):
        i = pl.program_id(0)

        @pl.when(i == 0)
        def _():
            st_ref[...] = jnp.zeros((FFN_STATS, D), F32)

        sh2, sc2, gt2 = _mod(mod_ref, b_ref, 3), _mod(mod_ref, b_ref, 4), _mod(mod_ref, b_ref, 5)
        x1v = x1_ref[...]
        r2 = lax.rsqrt(jnp.mean(x1v * x1v, axis=-1, keepdims=True) + EPS)
        hb = ((x1v * r2 * nw2_ref[...]) * (1.0 + sc2) + sh2).astype(BF16)
        hb_ref[...] = hb
        ffn = jnp.zeros((TM, D), F32)
        for j in range(4):
            fgj = _dot_nt(hb, wi_ref[j])
            fuj = _dot_nt(hb, wi_ref[j + 4])
            pre_ref[j] = fgj.astype(BF16)
            pre_ref[j + 4] = fuj.astype(BF16)
            aj = (fgj * _sig(fgj) * fuj).astype(BF16)
            act_ref[j] = aj
            ffn = ffn + _dot(aj, wo_ref[j])
        x2 = x1v + gt2 * ffn
        r3 = lax.rsqrt(jnp.mean(x2 * x2, axis=-1, keepdims=True) + EPS)
        xr3 = x2 * r3
        err = xr3 * nfw_ref[...] - tgt_ref[...]
        loss = 0.5 * jnp.sum(jnp.mean(err * err, axis=-1, keepdims=True), axis=0, keepdims=True)
        dy = err * (1.0 / D)
        st_ref[0:1, :] += _colsum(dy * xr3)
        dyr = dy * nfw_ref[...]
        dx2 = r3 * (dyr - xr3 * jnp.mean(dyr * xr3, axis=-1, keepdims=True))
        st_ref[1:2, :] += _colsum(dx2 * ffn)
        st_ref[5:6, :] += jnp.broadcast_to(loss, (1, D))
        dx2_ref[...] = dx2
        dffn_ref[...] = (gt2 * dx2).astype(BF16)

    tile = lambda w: pl.BlockSpec((TM, w), lambda i: (i, 0))
    return pl.pallas_call(
        body, name="ffn_forward", grid=(NT,),
        in_specs=[tile(D), tile(D), _const((1, 6 * D)), _const((1, 6 * D)), _const((1, D)), _const((1, D)),
                  _const1((N_DEV, FB, D)), _const1((4, FB, D))],
        out_specs=(tile(D), pl.BlockSpec((4, TM, FB), lambda i: (0, i, 0)),
                   pl.BlockSpec((N_DEV, TM, FB), lambda i: (0, i, 0)), tile(D), tile(D), _const((FFN_STATS, D))),
        out_shape=(jax.ShapeDtypeStruct((S, D), BF16), jax.ShapeDtypeStruct((4, S, FB), BF16),
                   jax.ShapeDtypeStruct((N_DEV, S, FB), BF16), jax.ShapeDtypeStruct((S, D), F32),
                   jax.ShapeDtypeStruct((S, D), BF16), jax.ShapeDtypeStruct((FFN_STATS, D), F32)),
        compiler_params=_params(42, dimension_semantics=("arbitrary",)),
    )(x1, tgt, modnb, bada, nw2, nfw, w_fi, w_fo)


def _ffn_backward(dffn, pre, x1, dx2, modnb, bada, nw2, w_fi, w_fo):
    def body(dffn_ref, pre_ref, x1_ref, dx2_ref, mod_ref, b_ref, nw2_ref, wi_ref, wo_ref, df_ref, dx1_ref, st_ref):
        i = pl.program_id(0)

        @pl.when(i == 0)
        def _():
            st_ref[...] = jnp.zeros((FFN_STATS, D), F32)

        dffn = dffn_ref[...]
        dh = jnp.zeros((TM, D), F32)
        for j in range(4):
            fg = pre_ref[j].astype(F32)
            fu = pre_ref[j + 4].astype(F32)
            sg = _sig(fg)
            dact = _dot_nt(dffn, wo_ref[j])
            dfg = (dact * fu * (sg * (1.0 + fg * (1.0 - sg)))).astype(BF16)
            dfu = (dact * (fg * sg)).astype(BF16)
            df_ref[j] = dfg
            df_ref[j + 4] = dfu
            dh = dh + _dot(dfg, wi_ref[j]) + _dot(dfu, wi_ref[j + 4])
        x1v = x1_ref[...]
        r2 = lax.rsqrt(jnp.mean(x1v * x1v, axis=-1, keepdims=True) + EPS)
        xr2 = x1v * r2
        st_ref[2:3, :] += _colsum(dh)
        st_ref[3:4, :] += _colsum(dh * (xr2 * nw2_ref[...]))
        dxn = dh * (1.0 + _mod(mod_ref, b_ref, 4))
        st_ref[4:5, :] += _colsum(dxn * xr2)
        dxr = dxn * nw2_ref[...]
        dx1_ref[...] = dx2_ref[...] + r2 * (dxr - xr2 * jnp.mean(dxr * xr2, axis=-1, keepdims=True))

    tile = lambda w: pl.BlockSpec((TM, w), lambda i: (i, 0))
    wide = lambda: pl.BlockSpec((N_DEV, TM, FB), lambda i: (0, i, 0))
    return pl.pallas_call(
        body, name="ffn_backward", grid=(NT,),
        in_specs=[tile(D), wide(), tile(D), tile(D), _const((1, 6 * D)), _const((1, 6 * D)), _const((1, D)),
                  _const1((N_DEV, FB, D)), _const1((4, FB, D))],
        out_specs=(wide(), tile(D), _const((FFN_STATS, D))),
        out_shape=(jax.ShapeDtypeStruct((N_DEV, S, FB), BF16), jax.ShapeDtypeStruct((S, D), F32),
                   jax.ShapeDtypeStruct((FFN_STATS, D), F32)),
        compiler_params=_params(44, dimension_semantics=("arbitrary",)),
    )(dffn, pre, x1, dx2, modnb, bada, nw2, w_fi, w_fo)


def _grad_w(name, a, b, nb):
    m, n = a.shape[1], b.shape[1]

    def body(a_ref, b_ref, o_ref):
        o_ref[...] = _dot_tn(a_ref[...], b_ref[...]).astype(BF16)

    return pl.pallas_call(
        body, name=name, grid=(m // nb,),
        in_specs=[pl.BlockSpec((S, nb), lambda j: (0, j)), _const((S, n))],
        out_specs=pl.BlockSpec((nb, n), lambda j: (j, 0)),
        out_shape=jax.ShapeDtypeStruct((m, n), BF16),
        compiler_params=_params(dimension_semantics=("arbitrary",)),
    )(a, b)


GW_IN_ROWS = NMAIN + LANES


def _grad_w_in(dp_conf, dp_gdn, dp_ba, hb1):
    nb = 512
    n_conf, n_gdn = 2 * CW // nb, 4 * GW // nb

    def body(c_ref, g_ref, ba_ref, h_ref, o_ref):
        j = pl.program_id(0)

        @pl.when(j < n_conf)
        def _():
            o_ref[...] = _dot_tn(c_ref[...], h_ref[...]).astype(BF16)

        @pl.when((j >= n_conf) & (j < n_conf + n_gdn))
        def _():
            o_ref[...] = _dot_tn(g_ref[...], h_ref[...]).astype(BF16)

        @pl.when(j == n_conf + n_gdn)
        def _():
            o_ref[0:LANES, :] = _dot_tn(ba_ref[...], h_ref[...]).astype(BF16)

    return pl.pallas_call(
        body, name="grad_w_in", grid=(n_conf + n_gdn + 1,),
        in_specs=[pl.BlockSpec((S, nb), lambda j: (0, jnp.minimum(j, n_conf - 1))),
                  pl.BlockSpec((S, nb), lambda j: (0, jnp.clip(j - n_conf, 0, n_gdn - 1))),
                  _const((S, LANES)), _const((S, D))],
        out_specs=pl.BlockSpec((nb, D), lambda j: (j, 0)),
        out_shape=jax.ShapeDtypeStruct((GW_IN_ROWS, D), BF16),
        compiler_params=_params(dimension_semantics=("arbitrary",)),
    )(dp_conf, dp_gdn, dp_ba, hb1)


def _grad_w_ffn_in(hb2, df):
    def body(a_ref, b_ref, o_ref):
        o_ref[0] = _dot_tn(b_ref[0], a_ref[...]).astype(BF16)

    return pl.pallas_call(
        body, name="grad_w_ffn_in", grid=(N_DEV,),
        in_specs=[_const((S, D)), pl.BlockSpec((1, S, FB), lambda j: (j, 0, 0))],
        out_specs=pl.BlockSpec((1, FB, D), lambda j: (j, 0, 0)),
        out_shape=jax.ShapeDtypeStruct((N_DEV, FB, D), BF16),
        compiler_params=_params(dimension_semantics=("arbitrary",)),
    )(hb2, df)


def _grad_w_ffn_out(act, dffn):
    def body(a_ref, b_ref, o_ref):
        o_ref[0] = _dot_tn(a_ref[0], b_ref[...]).astype(BF16)

    return pl.pallas_call(
        body, name="grad_w_ffn_out", grid=(4,),
        in_specs=[pl.BlockSpec((1, S, FB), lambda j: (j, 0, 0)), _const((S, D))],
        out_specs=pl.BlockSpec((1, FB, D), lambda j: (j, 0, 0)),
        out_shape=jax.ShapeDtypeStruct((4, FB, D), BF16),
        compiler_params=_params(dimension_semantics=("arbitrary",)),
    )(act, dffn)


def _bwd_out(dx1, mix, modnb, bada, w_out):
    def body(dx_ref, mix_ref, mod_ref, b_ref, w_ref, dmix_ref, doa_ref, dob_ref, st_ref):
        i = pl.program_id(0)

        @pl.when(i == 0)
        def _():
            st_ref[...] = jnp.zeros((8, D), F32)

        dx = dx_ref[...]
        st_ref[0:1, :] += _colsum(dx * mix_ref[...])
        dmix = (_mod(mod_ref, b_ref, 2) * dx).astype(BF16)
        dmix_ref[...] = dmix
        doa_ref[...] = _dot_nt(dmix, w_ref[0:CW, :])
        dob_ref[...] = _dot_nt(dmix, w_ref[CW:D, :])

    tile = lambda w: pl.BlockSpec((TM, w), lambda i: (i, 0))
    return pl.pallas_call(
        body, name="bwd_out", grid=(NT,),
        in_specs=[tile(D), tile(D), _const((1, 6 * D)), _const((1, 6 * D)), _const((D, D))],
        out_specs=(tile(D), tile(CW), tile(GW), _const((8, D))),
        out_shape=(jax.ShapeDtypeStruct((S, D), BF16), jax.ShapeDtypeStruct((S, CW), F32),
                   jax.ShapeDtypeStruct((S, GW), F32), jax.ShapeDtypeStruct((8, D), F32)),
        compiler_params=_params(dimension_semantics=("arbitrary",)),
    )(dx1, mix, modnb, bada, w_out)


CONF_STATS = 40


def _conf_bwd(d_out_a, y, p_main, conv_w, gn_w, gn_b):
    def body(do_ref, y_ref, a_ref, g_ref, ah_ref, gh_ref, w_ref, gw_ref, gb_ref, dp_ref, st_ref,
             ubuf, dybuf, ush, dysh):
        i = pl.program_id(0)

        @pl.when(i == 0)
        def _():
            st_ref[...] = jnp.zeros((CONF_STATS, CW), F32)
            dybuf[TM:TM + HALO, :] = jnp.zeros((HALO, CW), F32)

        pm = _group_mean_matrix().astype(BF16)
        yv = y_ref[...]
        dlt = yv - _group_mean(yv, pm)
        rstd = lax.rsqrt(_group_mean(dlt * dlt, pm) + EPS)
        un = dlt * rstd
        o = un * gw_ref[...] + gb_ref[...]
        so = _sig(o)
        d_o = do_ref[...] * (so * (1.0 + o * (1.0 - so)))
        st_ref[33:34, :] += _colsum(d_o)
        st_ref[32:33, :] += _colsum(d_o * un)
        dun = d_o * gw_ref[...]
        dy = rstd * (dun - _group_mean(dun, pm) - un * _group_mean(dun * un, pm))
        st_ref[31:32, :] += _colsum(dy)
        dybuf[0:TM, :] = dy
        _fill_shifted(dybuf, dysh)

        a = a_ref[...]
        sg = _sig(g_ref[...])
        first = i == NT - 1
        ubuf[0:HALO, :] = jnp.where(first, 0.0, ah_ref[...] * _sig(gh_ref[...]))
        ubuf[HALO:HALO + TM, :] = a * sg
        _fill_shifted(ubuf, ush)
        du = jnp.zeros((TM, CW), F32)
        for k in range(KC):
            st_ref[k:k + 1, :] += _colsum(dy * _rows_at(ubuf, ush, HALO - (KC - 1) + k))
            du = du + w_ref[k:k + 1, :] * _rows_at(dybuf, dysh, KC - 1 - k)
        dybuf[TM:TM + HALO, :] = dybuf[0:HALO, :]
        dp_ref[:, 0:CW] = (du * sg).astype(BF16)
        dp_ref[:, CW:2 * CW] = (du * a * sg * (1.0 - sg)).astype(BF16)

    rev = lambda w, j=0: pl.BlockSpec((TM, w), lambda i: (NT - 1 - i, j))
    halo = lambda j: pl.BlockSpec((HALO, CW), lambda i: (jnp.maximum((NT - 1 - i) * (TM // HALO) - 1, 0), j))
    return pl.pallas_call(
        body, name="conf_bwd", grid=(NT,),
        in_specs=[rev(CW), rev(CW), rev(CW, 0), rev(CW, 1), halo(0), halo(1),
                  _const((KC, CW)), _const((1, CW)), _const((1, CW))],
        out_specs=(rev(2 * CW), _const((CONF_STATS, CW))),
        out_shape=(jax.ShapeDtypeStruct((S, 2 * CW), BF16), jax.ShapeDtypeStruct((CONF_STATS, CW), F32)),
        scratch_shapes=[pltpu.VMEM((HALO + TM, CW), F32), pltpu.VMEM((TM + HALO, CW), F32),
                        pltpu.VMEM((SUB - 1, SHIFT_ROWS, CW), F32), pltpu.VMEM((SUB - 1, SHIFT_ROWS, CW), F32)],
        compiler_params=_params(dimension_semantics=("arbitrary",)),
    )(d_out_a, y, p_main, p_main, p_main, p_main, conv_w, gn_w, gn_b)


GDN_STATS = 8


def _gdn_bwd(d_out_b, o_pre, s_in, t_inv, p_main, p_ba, gdn_conv_w, alog_l, dt_l, gdn_nw):
    def body(dob_ref, o_ref, sin_ref, t_ref, q_ref, k_ref, v_ref, z_ref, qh_ref, kh_ref, vh_ref, ba_ref,
             w_ref, al_ref, dt_ref, nw_ref, dp_ref, dba_ref, st_ref, xbuf, dcbuf, dstate):
        n = pl.program_id(0)

        @pl.when(n == 0)
        def _():
            st_ref[...] = jnp.zeros((GDN_STATS, 3 * GW), F32)
            dcbuf[CL:CL + SH, :] = jnp.zeros((SH, 3 * GW), F32)
            dstate[...] = jnp.zeros((NH, DH, DH), F32)

        for cc in reversed(range(CPS)):
            chunk(n, cc, dob_ref, o_ref, sin_ref, t_ref, q_ref, k_ref, v_ref, z_ref, qh_ref, kh_ref, vh_ref, ba_ref,
                  w_ref, al_ref, dt_ref, nw_ref, dp_ref, dba_ref, st_ref, xbuf, dcbuf, dstate)

    def chunk(n, cc, dob_ref, o_ref, sin_ref, t_ref, q_ref, k_ref, v_ref, z_ref, qh_ref, kh_ref, vh_ref, ba_ref,
              w_ref, al_ref, dt_ref, nw_ref, dp_ref, dba_ref, st_ref, xbuf, dcbuf, dstate):
        r0 = cc * CL
        if cc == 0:
            first = n == NCH // CPS - 1
            xbuf[0:SH, 0:GW] = jnp.where(first, 0.0, qh_ref[...])
            xbuf[0:SH, GW:2 * GW] = jnp.where(first, 0.0, kh_ref[...])
            xbuf[0:SH, 2 * GW:3 * GW] = jnp.where(first, 0.0, vh_ref[...])
        else:
            xbuf[0:SH, 0:GW] = q_ref[r0 - SH:r0, :]
            xbuf[0:SH, GW:2 * GW] = k_ref[r0 - SH:r0, :]
            xbuf[0:SH, 2 * GW:3 * GW] = v_ref[r0 - SH:r0, :]
        xbuf[SH:SH + CL, 0:GW] = q_ref[r0:r0 + CL, :]
        xbuf[SH:SH + CL, GW:2 * GW] = k_ref[r0:r0 + CL, :]
        xbuf[SH:SH + CL, 2 * GW:3 * GW] = v_ref[r0:r0 + CL, :]
        taps = [xbuf[SH - (KS - 1) + k:SH - (KS - 1) + k + CL, :] for k in range(KS)]
        conv = taps[0] * w_ref[0:1, :]
        for k in range(1, KS):
            conv = conv + taps[k] * w_ref[k:k + 1, :]
        sc = _sig(conv)
        qkv = conv * sc
        ba = ba_ref[r0:r0 + CL, :]
        beta_all, g_all, xg, neg_a = _gdn_gates(ba, al_ref[...], dt_ref[...])
        gcum, gcum_t = _gdn_cumsum(g_all)
        lane = lax.broadcasted_iota(jnp.int32, (CL, LANES), 1)
        row = lax.broadcasted_iota(jnp.int32, (CL, 1), 0)
        acc = dict(dgcum=jnp.zeros((CL, LANES), F32), dbeta=jnp.zeros((CL, LANES), F32))

        def head(h):
            lo = h * DH
            qh = qkv[:, lo:lo + DH]
            kh = qkv[:, GW + lo:GW + lo + DH]
            vh = qkv[:, 2 * GW + lo:2 * GW + lo + DH]
            beta = beta_all[:, h:h + 1]
            f = _head_terms(qh, kh, beta, gcum[:, NH + h:NH + h + 1], gcum_t[NH + h:NH + h + 1, :])
            qn, kn, qs, kb, gam, kds, cd, decay = (f[s] for s in ("qn", "kn", "qs", "kb", "gam", "kds", "cd", "decay"))
            t = t_ref[cc, h]
            st = sin_ref[cc, h]
            vb = vh * beta
            kbg = kb * gam
            u = _dot(t, vb, GP)
            w = _dot(t, kbg, GP)
            yield
            v_new = u - _dot(w, st, GP)
            q_dec = qs * gam
            k_dec = kn * kds

            o = o_ref[r0:r0 + CL, lo:lo + DH]
            zh = z_ref[r0:r0 + CL, lo:lo + DH]
            sz = _sig(zh)
            r = lax.rsqrt(jnp.mean(o * o, axis=-1, keepdims=True) + EPS)
            orr = o * r
            d_out = dob_ref[r0:r0 + CL, lo:lo + DH]
            dz = d_out * (orr * nw_ref[...]) * (sz * (1.0 + zh * (1.0 - sz)))
            don = d_out * (zh * sz)
            st_ref[4:5, 0:DH] += _colsum(don * orr)
            tt = don * nw_ref[...]
            d_o = r * (tt - orr * jnp.mean(tt * orr, axis=-1, keepdims=True))

            yield
            ds_out = dstate[h]
            dv_new = _dot_tn(f["qk"], d_o, GP) + _dot(k_dec, ds_out, GP)
            dqk = jnp.where(f["causal"], _dot_nt(d_o, v_new, GP), 0.0)
            dq_dec = _dot_nt(d_o, st, GP)
            dk_dec = _dot_nt(v_new, ds_out, GP)
            yield
            dstate[h] = _dot_tn(q_dec, d_o, GP) + cd * ds_out - _dot_tn(w, dv_new, GP)
            dcd = jnp.sum(_rowsum(st * ds_out), axis=0, keepdims=True)
            dw = -_dot_nt(dv_new, st, GP)
            dvb = _dot_tn(t, dv_new, GP)
            yield
            dt_m = _dot_nt(dv_new, vb, GP) + _dot_nt(dw, kbg, GP)
            dkbg = _dot_tn(t, dw, GP)
            yield
            dtt = _dot_nt(dt_m, t, GP)
            yield
            da = jnp.where(f["strict"], -_dot_tn(t, dtt, GP), 0.0)
            yield
            dad = da * decay
            dqkd = dqk * decay
            dkb = _dot(dad, kn, GP) + dkbg * gam
            dkn = _dot_tn(dad, kb, GP) + _dot_tn(dqkd, qs, GP) + dk_dec * kds + dkb * beta
            dqs = _dot(dqkd, kn, GP) + dq_dec * gam
            yield
            m = da * f["a"] + dqk * f["qk"]
            tk = _rowsum(dk_dec * k_dec)
            dgl = jnp.sum(tk, axis=0, keepdims=True) + dcd * cd
            dgc = (_rowsum(m) - _rowsum(jnp.transpose(m)) + _rowsum(dq_dec * q_dec) - tk + _rowsum(dkbg * kbg)
                   + jnp.where(row == CL - 1, dgl, 0.0))
            dbeta = _rowsum(dkb * kn) + _rowsum(dvb * vh)
            acc["dgcum"] = acc["dgcum"] + jnp.where(lane == NH + h, dgc, 0.0)
            acc["dbeta"] = acc["dbeta"] + jnp.where(lane == h, dbeta, 0.0)
            dvh = dvb * beta
            dqn = dqs * QSCALE
            dqh = f["rq"] * (dqn - qn * _rowsum(dqn * qn))
            dkh = f["rk"] * (dkn - kn * _rowsum(dkn * kn))
            dsilu = lambda c0: sc[:, c0:c0 + DH] * (1.0 + conv[:, c0:c0 + DH] * (1.0 - sc[:, c0:c0 + DH]))
            dcbuf[0:CL, lo:lo + DH] = dqh * dsilu(lo)
            dcbuf[0:CL, GW + lo:GW + lo + DH] = dkh * dsilu(GW + lo)
            dcbuf[0:CL, 2 * GW + lo:2 * GW + lo + DH] = dvh * dsilu(2 * GW + lo)
            dp_ref[r0:r0 + CL, 3 * GW + lo:3 * GW + lo + DH] = dz.astype(BF16)

        _lockstep(head(h) for h in range(NH))
        dgcum_all, dbeta_all = acc["dgcum"], acc["dbeta"]

        ii, jj = _tri_iota()
        upper = jnp.where(ii <= jj, 1.0, 0.0).astype(BF16)
        dg_all = _ones_dot(upper, dgcum_all)
        dxg = dg_all * neg_a * _sig(xg)
        st_ref[5:6, 0:LANES] += _colsum(dg_all * g_all)
        st_ref[6:7, 0:LANES] += _colsum(dxg)
        dbl = dbeta_all * beta_all * (1.0 - beta_all)
        dba_ref[r0:r0 + CL, :] = jnp.where(lane < NH, dbl, jnp.where(lane < 2 * NH, dxg, 0.0)).astype(BF16)

        dconv = dcbuf[0:CL, :]
        dx = w_ref[0:1, :] * dcbuf[KS - 1:KS - 1 + CL, :]
        st_ref[0:1, :] += _colsum(dconv * taps[0])
        for k in range(1, KS):
            st_ref[k:k + 1, :] += _colsum(dconv * taps[k])
            dx = dx + w_ref[k:k + 1, :] * dcbuf[KS - 1 - k:KS - 1 - k + CL, :]
        dcbuf[CL:CL + SH, :] = dcbuf[0:SH, :]
        dp_ref[r0:r0 + CL, 0:3 * GW] = dx.astype(BF16)

    steps = NCH // CPS
    rev = lambda w, j=0: pl.BlockSpec((TG, w), lambda n: (steps - 1 - n, j))
    halo = lambda j: pl.BlockSpec((SH, GW), lambda n: (jnp.maximum((steps - 1 - n) * (TG // SH) - 1, 0), j))
    blk4 = lambda a, b: pl.BlockSpec((CPS, NH, a, b), lambda n: (steps - 1 - n, 0, 0, 0))
    return pl.pallas_call(
        body, name="gdn_bwd", grid=(steps,),
        in_specs=[rev(GW), rev(GW), blk4(DH, DH), blk4(CL, CL), rev(GW, 2), rev(GW, 3), rev(GW, 4), rev(GW, 5),
                  halo(2), halo(3), halo(4), rev(LANES), _const((KS, 3 * GW)), _const((1, LANES)),
                  _const((1, LANES)), _const((1, DH))],
        out_specs=(rev(4 * GW), rev(LANES), _const((GDN_STATS, 3 * GW))),
        out_shape=(jax.ShapeDtypeStruct((S, 4 * GW), BF16), jax.ShapeDtypeStruct((S, LANES), BF16),
                   jax.ShapeDtypeStruct((GDN_STATS, 3 * GW), F32)),
        scratch_shapes=[pltpu.VMEM((SH + CL, 3 * GW), F32), pltpu.VMEM((CL + SH, 3 * GW), F32),
                        pltpu.VMEM((NH, DH, DH), F32)],
        compiler_params=_params(dimension_semantics=("arbitrary",)),
    )(d_out_b, o_pre, s_in, t_inv, p_main, p_main, p_main, p_main, p_main, p_main, p_main, p_ba,
      gdn_conv_w, alog_l, dt_l, gdn_nw)


def _bwd_in(dp_conf, dp_gdn, dp_ba, x, dx1, nw1, modnb, bada, w_main, w_ba):
    def body(dc_ref, dg_ref, db_ref, x_ref, dx1_ref, nw_ref, mod_ref, b_ref, wm_ref, wb_ref, gx_ref, st_ref):
        i = pl.program_id(0)

        @pl.when(i == 0)
        def _():
            st_ref[...] = jnp.zeros((8, D), F32)

        dh = (_dot(dc_ref[...], wm_ref[0:2 * CW, :]) + _dot(dg_ref[...], wm_ref[2 * CW:NMAIN, :])
              + _dot(db_ref[...], wb_ref[...]))
        xv = x_ref[...]
        r = lax.rsqrt(jnp.mean(xv * xv, axis=-1, keepdims=True) + EPS)
        xr = xv * r
        st_ref[0:1, :] += _colsum(dh)
        st_ref[1:2, :] += _colsum(dh * (xr * nw_ref[...]))
        dxn = dh * (1.0 + _mod(mod_ref, b_ref, 1))
        st_ref[2:3, :] += _colsum(dxn * xr)
        dxr = dxn * nw_ref[...]
        gx_ref[...] = dx1_ref[...] + r * (dxr - xr * jnp.mean(dxr * xr, axis=-1, keepdims=True))

    tile = lambda w: pl.BlockSpec((TI, w), lambda i: (i, 0))
    return pl.pallas_call(
        body, name="bwd_in", grid=(S // TI,),
        in_specs=[tile(2 * CW), tile(4 * GW), tile(LANES), tile(D), tile(D), _const((1, D)), _const((1, 6 * D)),
                  _const((1, 6 * D)), _const1((NMAIN, D)), _const((LANES, D))],
        out_specs=(tile(D), _const((8, D))),
        out_shape=(jax.ShapeDtypeStruct((S, D), F32), jax.ShapeDtypeStruct((8, D), F32)),
        compiler_params=_params(dimension_semantics=("arbitrary",)),
    )(dp_conf, dp_gdn, dp_ba, x, dx1, nw1, modnb, bada, w_main, w_ba)


def _adamw(w, g, m, v):
    m = ADAM_B1 * m + (1.0 - ADAM_B1) * g
    v = ADAM_B2 * v + (1.0 - ADAM_B2) * (g * g)
    m_hat = m / BC1
    v_hat = v / BC2
    delta = -ADAM_LR * (m_hat / (jnp.sqrt(v_hat) + ADAM_EPS) + ADAM_WD * w)
    return delta, m, v


ADAM_BLOCK_BYTES = 6 * 1024 * 1024


def _adam_tile(rows, cols):
    padded = -(-cols // LANES) * LANES
    if N_DEV * rows * padded * 4 <= ADAM_BLOCK_BYTES:
        return rows, cols
    best = None
    for tr in range(16, rows, 16):
        if rows % tr == 0 and N_DEV * tr * padded * 4 <= ADAM_BLOCK_BYTES:
            best = tr
    if best is not None:
        return best, cols
    rows_padded = -(-rows // 16) * 16
    tc = LANES
    for cand in range(LANES, cols, LANES):
        if cols % cand == 0 and N_DEV * rows_padded * cand * 4 <= ADAM_BLOCK_BYTES:
            tc = cand
    return rows, tc


def _reduce_adam(name, parts, w, m, v, own=None):
    rows, cols = w.shape
    tr, tc = _adam_tile(rows, cols)

    def body(*refs):
        p_ref, w_ref, m_ref, v_ref = refs[:4]
        g_ref, d_ref, nm_ref, nv_ref = refs[-4:]
        if own is None:
            part = lambda j: p_ref[j].astype(F32)
        else:
            me = 4 * lax.axis_index("x") + 2 * lax.axis_index("y") + lax.axis_index("c")
            part = lambda j: jnp.where(me == j, refs[4][...], p_ref[j]).astype(F32)
        g = part(0)
        for j in range(1, N_DEV):
            g = g + part(j)
        g_ref[...] = g
        d_ref[...], nm_ref[...], nv_ref[...] = _adamw(w_ref[...], g, m_ref[...], v_ref[...])

    blk = pl.BlockSpec((tr, tc), lambda i, j: (i, j))
    sds = jax.ShapeDtypeStruct((rows, cols), F32)
    extra = [] if own is None else [own]
    return pl.pallas_call(
        body, name=name, grid=(rows // tr, cols // tc),
        in_specs=[pl.BlockSpec((N_DEV, tr, tc), lambda i, j: (0, i, j)), blk, blk, blk] + [blk] * len(extra),
        out_specs=(blk, blk, blk, blk), out_shape=(sds, sds, sds, sds),
        compiler_params=_params(dimension_semantics=("arbitrary", "arbitrary")),
    )(parts, w, m, v, *extra)


def _ada_adam(c_all, dmod_sh, w, m, v):
    rows, cols = w.shape
    tr = 256

    def body(c_ref, dm_ref, w_ref, m_ref, v_ref, g_ref, d_ref, nm_ref, nv_ref):
        cv = c_ref[...]
        g = _dot_tn(cv * _sig(cv), dm_ref[...], HI)
        g_ref[...] = g
        d_ref[...], nm_ref[...], nv_ref[...] = _adamw(w_ref[...], g, m_ref[...], v_ref[...])

    blk = pl.BlockSpec((tr, cols), lambda i: (i, 0))
    sds = jax.ShapeDtypeStruct((rows, cols), F32)
    return pl.pallas_call(
        body, name="ada_adam", grid=(rows // tr,),
        in_specs=[pl.BlockSpec((N_DEV, tr), lambda i: (0, i)), _const((N_DEV, cols)), blk, blk, blk],
        out_specs=(blk, blk, blk, blk), out_shape=(sds, sds, sds, sds),
        compiler_params=_params(dimension_semantics=("arbitrary",)),
    )(c_all, dmod_sh, w, m, v)


def _lanes(a, at=0):
    return jnp.pad(a, ((0, 0), (at, LANES - at - a.shape[1])))


WEIGHT_NAMES = ["w_ada", "b_ada", "norm_mix_w", "w_in", "conv_w", "conv_b", "conv_gn_w", "conv_gn_b", "gdn_conv_w",
                "gdn_a_log", "gdn_dt_bias", "gdn_norm_w", "w_out", "norm_ffn_w", "w_ffn_in", "w_ffn_out",
                "norm_final_w"]


SMALL_LAYOUT = [("b_ada", 0, 48, LANES), ("norm_mix_w", 48, 8, LANES), ("norm_ffn_w", 56, 8, LANES),
                ("norm_final_w", 64, 8, LANES), ("conv_b", 72, 4, LANES), ("conv_gn_w", 76, 4, LANES),
                ("conv_gn_b", 80, 4, LANES), ("gdn_norm_w", 84, 1, LANES), ("gdn_a_log", 85, 1, NH),
                ("gdn_dt_bias", 86, 1, NH)]
LOSS_ROW = 87


def _adam_small(g_small, weights, m1, m2):
    names = [nm for nm, _, _, _ in SMALL_LAYOUT]
    k = len(names)

    def body(*refs):
        g_ref = refs[0]
        w_refs, m_refs, v_refs = refs[1:1 + k], refs[1 + k:1 + 2 * k], refs[1 + 2 * k:1 + 3 * k]
        loss_ref = refs[1 + 3 * k]
        outs = refs[2 + 3 * k:2 + 7 * k]
        total = refs[-1]
        g = g_ref[0]
        for j in range(1, N_DEV):
            g = g + g_ref[j]
        total[...] = g
        loss_ref[...] = total[LOSS_ROW:LOSS_ROW + 1, :]
        for i, (_, r0, rows, lanes) in enumerate(SMALL_LAYOUT):
            gp = total[r0:r0 + rows, 0:lanes]
            outs[i][...] = gp
            outs[k + i][...], outs[2 * k + i][...], outs[3 * k + i][...] = _adamw(
                w_refs[i][...], gp, m_refs[i][...], v_refs[i][...])

    shapes = [jax.ShapeDtypeStruct((rows, lanes), F32) for _, _, rows, lanes in SMALL_LAYOUT]
    res = pl.pallas_call(
        body, name="adam_small",
        out_shape=tuple([jax.ShapeDtypeStruct((1, LANES), F32)] + shapes * 4),
        scratch_shapes=[pltpu.VMEM((SMALL_ROWS, LANES), F32)],
        compiler_params=_params(),
    )(g_small, *[weights[n] for n in names], *[m1[n] for n in names], *[m2[n] for n in names])
    kinds = [dict(zip(names, res[1 + q * k:1 + (q + 1) * k])) for q in range(4)]
    return res[0], kinds


def _mix_forward(w, xs, modnb, between=None):
    w_main = w["w_in"]
    w_ba = jnp.pad(w["w_in"][NMAIN:], ((0, LANES - 2 * NH), (0, 0)))
    alog_l = _lanes(w["gdn_a_log"], NH)
    dt_l = _lanes(w["gdn_dt_bias"], NH)
    p_main, p_ba, hb1 = _fwd_in(xs, w["norm_mix_w"], modnb, w["b_ada"], w_main, w_ba)
    w_o, u_o, qg, kd, qk, cd, t_inv = _gdn_prep(p_main, p_ba, w["gdn_conv_w"], alog_l, dt_l)
    out_b, o_pre, s_in = _gdn_scan(w_o, u_o, qg, kd, qk, cd, p_main, w["gdn_norm_w"])
    conv_b = w["conv_b"] if between is None else _after(w["conv_b"], between(out_b))
    y_conv, out_a = _conf_fwd(p_main, w["conv_w"], conv_b, w["conv_gn_w"], w["conv_gn_b"])
    return dict(w_main=w_main, w_ba=w_ba, alog_l=alog_l, dt_l=dt_l, p_main=p_main, p_ba=p_ba, hb1=hb1,
                y_conv=y_conv, out_a=out_a, out_b=out_b, o_pre=o_pre, s_in=s_in, t_inv=t_inv)


def _ffn_stage(w, f, xs, tgt, modnb):
    x1, mix, oab = _fwd_out(f["out_a"], f["out_b"], xs, modnb, w["b_ada"], w["w_out"])
    hb2, act, pre, dx2, dffn, st_fwd = _ffn_forward(x1, tgt, modnb, w["b_ada"], w["norm_ffn_w"],
                                                    w["norm_final_w"], w["w_ffn_in"], w["w_ffn_out"])
    gw_ffn_out = _grad_w_ffn_out(act, dffn)
    df, dx1, st_bwd = _ffn_backward(dffn, pre, x1, dx2, modnb, w["b_ada"], w["norm_ffn_w"], w["w_ffn_in"],
                                    w["w_ffn_out"])
    gw_ffn_in = _grad_w_ffn_in(hb2, df)
    return dict(mix=mix, oab=oab, dx1=dx1, st_ffn=st_fwd + st_bwd, gw_ffn_in=gw_ffn_in, gw_ffn_out=gw_ffn_out)


def _out_backward(w, g, modnb):
    dmix, d_out_a, d_out_b, st_out = _bwd_out(g["dx1"], g["mix"], modnb, w["b_ada"], w["w_out"])
    return dict(d_out_a=d_out_a, d_out_b=d_out_b, st_out=st_out, gw_out=_grad_w("grad_w_out", g["oab"], dmix, 512))


def _heads_backward(w, f, a):
    dp_conf, st_conf = _conf_bwd(a["d_out_a"], f["y_conv"], f["p_main"], w["conv_w"], w["conv_gn_w"],
                                 w["conv_gn_b"])
    dp_gdn, dp_ba, st_gdn = _gdn_bwd(a["d_out_b"], f["o_pre"], f["s_in"], f["t_inv"], f["p_main"], f["p_ba"],
                                     w["gdn_conv_w"], f["alog_l"], f["dt_l"], w["gdn_norm_w"])
    gw_in = _grad_w_in(dp_conf, dp_gdn, dp_ba, f["hb1"])[:NIN]
    return dict(dp_conf=dp_conf, dp_gdn=dp_gdn, dp_ba=dp_ba, st_conf=st_conf, st_gdn=st_gdn, gw_in=gw_in,
                gw_conv=st_conf[0:KC], gw_gconv=st_gdn[0:KS])


def _in_backward(w, f, g, a, h, xs, modnb):
    st_out, st_conf, st_gdn, st_ffn = a["st_out"], h["st_conf"], h["st_gdn"], g["st_ffn"]
    grad_x, st_in = _bwd_in(h["dp_conf"], h["dp_gdn"], h["dp_ba"], xs, g["dx1"], w["norm_mix_w"], modnb,
                            w["b_ada"], f["w_main"], f["w_ba"])
    dmod = jnp.concatenate([st_in[0:1], st_in[1:2], st_out[0:1], st_ffn[2:3], st_ffn[3:4], st_ffn[1:2]], axis=1)
    small = jnp.concatenate([
        dmod.reshape(48, LANES), st_in[2:3].reshape(8, LANES), st_ffn[4:5].reshape(8, LANES),
        st_ffn[0:1].reshape(8, LANES), st_conf[31:32].reshape(4, LANES), st_conf[32:33].reshape(4, LANES),
        st_conf[33:34].reshape(4, LANES), st_gdn[4:5, 0:LANES],
        _lanes(st_gdn[5:6, NH:2 * NH]), _lanes(st_gdn[6:7, NH:2 * NH]), st_ffn[5:6, 0:LANES]], axis=0)
    return dict(grad_x=grad_x, small=small)


def _local(w, xs, tgt, modnb):
    f = _mix_forward(w, xs, modnb)
    g = _ffn_stage(w, f, xs, tgt, modnb)
    a = _out_backward(w, g, modnb)
    h = _heads_backward(w, f, a)
    b = _in_backward(w, f, g, a, h, xs, modnb)
    return dict(b, gw_in=h["gw_in"], gw_conv=h["gw_conv"], gw_gconv=h["gw_gconv"], gw_out=a["gw_out"],
                gw_ffn_in=g["gw_ffn_in"], gw_ffn_out=g["gw_ffn_out"])


def kernel(x, c, w_ada, b_ada, norm_mix_w, w_in, conv_w, conv_b, conv_gn_w, conv_gn_b, gdn_conv_w, gdn_a_log, gdn_dt_bias, gdn_norm_w, w_out, norm_ffn_w, w_ffn_in, w_ffn_out, norm_final_w, loss_target, m_w_ada, m_b_ada, m_norm_mix_w, m_w_in, m_conv_w, m_conv_b, m_conv_gn_w, m_conv_gn_b, m_gdn_conv_w, m_gdn_a_log, m_gdn_dt_bias, m_gdn_norm_w, m_w_out, m_norm_ffn_w, m_w_ffn_in, m_w_ffn_out, m_norm_final_w, v_w_ada, v_b_ada, v_norm_mix_w, v_w_in, v_conv_w, v_conv_b, v_conv_gn_w, v_conv_gn_b, v_gdn_conv_w, v_gdn_a_log, v_gdn_dt_bias, v_gdn_norm_w, v_w_out, v_norm_ffn_w, v_w_ffn_in, v_w_ffn_out, v_norm_final_w):
    me = 4 * lax.axis_index("x") + 2 * lax.axis_index("y") + lax.axis_index("c")
    xs = x.reshape(S, D)
    tgt = loss_target.reshape(S, D)

    late = [w_out[0].astype(BF16), jnp.transpose(w_ffn_in[0]).astype(BF16), w_ffn_out[0].astype(BF16)]
    g_c, g_cw, g_gcw, g_win, *late_lands = _gather_two_level(
        "gather_weights", [c, conv_w[0], gdn_conv_w[0], jnp.transpose(w_in[0]).astype(BF16)] + late,
        seed_only=(4, 5, 6))
    c_all = g_c.reshape(N_DEV, D)
    g_mod, mod_token = _exchange("gather_mod", [_mod_shard(c_all, w_ada[0])], [False], with_token=True)
    modnb = lax.dynamic_index_in_dim(g_mod, me, axis=1, keepdims=False).reshape(1, 6 * D)
    late_started = _exchange_start("gather_late_start", [_after(late[0], mod_token)] + late[1:], late_lands,
                                   [False] * 3, only=LEVEL_ONE)
    modnb = _after(modnb, late_started[-1])
    w = dict(b_ada=b_ada, norm_mix_w=norm_mix_w, conv_b=conv_b, conv_gn_w=conv_gn_w, conv_gn_b=conv_gn_b,
             gdn_a_log=gdn_a_log, gdn_dt_bias=gdn_dt_bias, gdn_norm_w=gdn_norm_w, norm_ffn_w=norm_ffn_w,
             norm_final_w=norm_final_w.reshape(1, D),
             conv_w=jnp.transpose(g_cw, (1, 0, 2)).reshape(KC, CW),
             gdn_conv_w=jnp.transpose(g_gcw, (1, 0, 2)).reshape(KS, 3 * GW),
             w_in=g_win.reshape(NIN, D))

    relay = {}

    def relay_late(out_b):
        _, late_landed = _exchange_wait("gather_late_wait", late_started, [False] * 3, (out_b,), only=LEVEL_ONE)
        relay["started"] = _relay_start("gather_late_relay_start", late_landed)
        return relay["started"][-1]

    f = _mix_forward(w, xs, modnb, relay_late)
    g_wout, g_wfi, g_wfo = _relay_wait("gather_late_relay_wait", relay["started"], (f["out_a"],))
    w.update(w_out=g_wout.reshape(D, D), w_ffn_in=g_wfi, w_ffn_out=g_wfo.reshape(4, FB, D))
    g = _ffn_stage(w, f, xs, tgt, modnb)

    ffn_grads = [g["gw_ffn_in"], g["gw_ffn_out"].reshape(N_DEV, DFF // N_DEV, D)]
    ffn_started = _exchange_start("scatter_ffn_start", ffn_grads,
                                  [lax.empty(a.shape, a.dtype) for a in ffn_grads], [True] * 2)
    a = _out_backward(w, g, _after(modnb, ffn_started[-1]))
    out_grads = [a["gw_out"].reshape(N_DEV, D // N_DEV, D)]
    out_started = _exchange_start("scatter_out_start", out_grads,
                                  [lax.empty(t.shape, t.dtype) for t in out_grads], [True])
    h = _heads_backward(dict(w, conv_gn_w=_after(w["conv_gn_w"], out_started[-1])), f, a)

    in_grads = [h["gw_in"].reshape(N_DEV, NIN // N_DEV, D),
                jnp.transpose(h["gw_conv"].reshape(KC, N_DEV, CW // N_DEV), (1, 0, 2)),
                jnp.transpose(h["gw_gconv"].reshape(KS, N_DEV, 3 * GW // N_DEV), (1, 0, 2))]
    in_started = _exchange_start("scatter_in_start", in_grads,
                                 [lax.empty(t.shape, t.dtype) for t in in_grads], [True] * 3)
    loc = _in_backward(w, f, g, a, h, xs, _after(modnb, in_started[-1]))
    small_started = _exchange_start("gather_small_start", [loc["small"]],
                                    [lax.empty((N_DEV, SMALL_ROWS, LANES), F32)], [False])

    def own(sent):
        return lax.dynamic_index_in_dim(sent, me, axis=0, keepdims=False)

    big = {}
    (sent_fi, sent_fo), (r_fi, r_fo) = _exchange_wait("scatter_ffn_wait", ffn_started, [True] * 2,
                                                         (small_started[-1],))
    big["w_ffn_in"] = [jnp.transpose(t) for t in _reduce_adam(
        "adam_w_ffn_in", r_fi, jnp.transpose(w_ffn_in[0]), jnp.transpose(m_w_ffn_in[0]),
        jnp.transpose(v_w_ffn_in[0]), own(sent_fi))]
    big["w_ffn_out"] = _reduce_adam("adam_w_ffn_out", r_fo, w_ffn_out[0], m_w_ffn_out[0], v_w_ffn_out[0],
                                    own(sent_fo))
    (sent_out,), (r_out,) = _exchange_wait("scatter_out_wait", out_started, [True], (big["w_ffn_out"][0],))
    big["w_out"] = _reduce_adam("adam_w_out", r_out, w_out[0], m_w_out[0], v_w_out[0], own(sent_out))

    (sent_small,), (r_small,) = _exchange_wait("gather_small_wait", small_started, [False], (big["w_out"][0],))
    slot = lax.broadcasted_iota(jnp.int32, (N_DEV, 1, 1), 0)
    g_small = jnp.where(slot == me, sent_small[None], r_small)
    def views(b_, nm_, nf_, nl_, cb_, gw_, gb_, gn_, al_, dt_):
        arrs = [b_, nm_, nf_, nl_, cb_, gw_, gb_, gn_, al_, dt_]
        return {nm: t.reshape(rows, lanes) for (nm, _, rows, lanes), t in zip(SMALL_LAYOUT, arrs)}

    loss_row, res = _adam_small(
        g_small,
        views(b_ada, norm_mix_w, norm_ffn_w, norm_final_w, conv_b, conv_gn_w, conv_gn_b, gdn_norm_w, gdn_a_log,
              gdn_dt_bias),
        views(m_b_ada, m_norm_mix_w, m_norm_ffn_w, m_norm_final_w, m_conv_b, m_conv_gn_w, m_conv_gn_b,
              m_gdn_norm_w, m_gdn_a_log, m_gdn_dt_bias),
        views(v_b_ada, v_norm_mix_w, v_norm_ffn_w, v_norm_final_w, v_conv_b, v_conv_gn_w, v_conv_gn_b,
              v_gdn_norm_w, v_gdn_a_log, v_gdn_dt_bias))
    loss = loss_row[0, 0]
    small_shapes = dict(b_ada=(1, 6 * D), norm_mix_w=(1, D), norm_ffn_w=(1, D), norm_final_w=(D,),
                        conv_b=(1, CW), conv_gn_w=(1, CW), conv_gn_b=(1, CW), gdn_norm_w=(1, DH),
                        gdn_a_log=(1, NH), gdn_dt_bias=(1, NH))
    res = [{nm: t.reshape(small_shapes[nm]) for nm, t in kind.items()} for kind in res]

    dmod_rows = g_small[:, 0:48, :].reshape(N_DEV, 6 * D)
    dmod_sh = lax.dynamic_slice_in_dim(dmod_rows, me * (6 * D // N_DEV), 6 * D // N_DEV, axis=1)

    big["w_ada"] = _ada_adam(c_all, dmod_sh, w_ada[0], m_w_ada[0], v_w_ada[0])
    (sent_in, sent_cw, sent_gcw), (r_in, r_cw, r_gcw) = _exchange_wait(
        "scatter_in_wait", in_started, [True] * 3, (big["w_ada"][0],))
    big["w_in"] = [jnp.transpose(t) for t in _reduce_adam(
        "adam_w_in", r_in, jnp.transpose(w_in[0]), jnp.transpose(m_w_in[0]), jnp.transpose(v_w_in[0]),
        own(sent_in))]
    big["conv_w"] = _reduce_adam("adam_conv_w", r_cw, conv_w[0], m_conv_w[0], v_conv_w[0], own(sent_cw))
    big["gdn_conv_w"] = _reduce_adam("adam_gdn_conv_w", r_gcw, gdn_conv_w[0], m_gdn_conv_w[0], v_gdn_conv_w[0],
                                     own(sent_gcw))
    outs = [loss, loc["grad_x"].reshape(1, S, D)]
    for kind in range(4):
        for nm in WEIGHT_NAMES:
            outs.append(big[nm][kind][None] if nm in big else res[kind][nm])
    return tuple(outs)
```

```python
import functools

import jax
import jax.numpy as jnp
from jax import lax
from jax.experimental import pallas as pl
from jax.experimental.pallas import tpu as pltpu

F32 = jnp.float32
BF16 = jnp.bfloat16
HI = lax.Precision.HIGHEST
MESH = pl.DeviceIdType.MESH

N_DEV = 8
S = 2048
D = 1024
TM = 256
NT = S // TM
CW = 512
KC = 31
NG = 8
GSZ = CW // NG
HALO = 32
GW = 512
NH = 4
DH = 128
KS = 4
SH = 8
CL = 64
NCH = S // CL
NMAIN = 2 * CW + 4 * GW
NIN = NMAIN + 2 * NH
DFF = 2816
FB = DFF // 4
EPS = 1e-6
QSCALE = DH ** -0.5
LANES = 128
SMALL_ROWS = 88

ADAM_LR = 0.001
ADAM_B1 = 0.9
ADAM_B2 = 0.999
ADAM_EPS = 1e-08
ADAM_WD = 0.01
ADAM_STEP = 10
BC1 = 1.0 - ADAM_B1 ** ADAM_STEP
BC2 = 1.0 - ADAM_B2 ** ADAM_STEP

MIB = 1024 * 1024
VMEM_LIMIT_MIB = 32


def _params(limit_mib=VMEM_LIMIT_MIB, **kw):
    return pltpu.CompilerParams(vmem_limit_bytes=limit_mib * MIB, **kw)


def _sig(x):
    return jax.nn.sigmoid(x)


GP = BF16


def _operands(a, b, prec):
    if prec is BF16:
        return a.astype(BF16), b.astype(BF16), None
    return a, b, prec


def _dot(a, b, prec=None):
    a, b, prec = _operands(a, b, prec)
    return jnp.dot(a, b, preferred_element_type=F32, precision=prec)


def _dot_nt(a, b, prec=None):
    a, b, prec = _operands(a, b, prec)
    return lax.dot_general(a, b, (((1,), (1,)), ((), ())), preferred_element_type=F32, precision=prec)


def _dot_tn(a, b, prec=None):
    a, b, prec = _operands(a, b, prec)
    return lax.dot_general(a, b, (((0,), (0,)), ((), ())), preferred_element_type=F32, precision=prec)


def _lockstep(gens):
    gens = list(gens)
    while gens:
        alive = []
        for g in gens:
            try:
                next(g)
                alive.append(g)
            except StopIteration:
                pass
        gens = alive


def _rowsum(x):
    return jnp.sum(x, axis=-1, keepdims=True)


def _colsum(x):
    return jnp.sum(x, axis=0, keepdims=True)


def _mod(mod_ref, b_ref, k):
    return mod_ref[:, k * D:(k + 1) * D] + b_ref[:, k * D:(k + 1) * D]


def _const(shape):
    nd = len(shape)
    return pl.BlockSpec(shape, lambda *_: (0,) * nd)


def _const1(shape):
    nd = len(shape)
    return pl.BlockSpec(shape, lambda *_: (0,) * nd, pipeline_mode=pl.Buffered(1))


PEER_FLIPS = [(dx, dy, dc) for dx in (0, 1) for dy in (0, 1) for dc in (0, 1)][1:]


def _after(x, token):
    return x + token[0:1, 0:1].astype(x.dtype).reshape((1,) * x.ndim)


def _exchange(name, srcs, per_dest, seed_only=(), with_token=False):
    n = len(srcs)
    out_shape = []
    for a, pd in zip(srcs, per_dest):
        blk = a.shape[1:] if pd else a.shape
        out_shape.append(jax.ShapeDtypeStruct((N_DEV,) + tuple(blk), a.dtype))

    def body(*refs):
        src = refs[:n]
        dst = refs[n:2 * n]
        send_sems, recv_sems, local_sems = refs[-3:]
        if with_token:
            refs[2 * n][...] = jnp.zeros((8, LANES), F32)
        x, y, c = lax.axis_index("x"), lax.axis_index("y"), lax.axis_index("c")
        me = 4 * x + 2 * y + c

        def piece(i, j):
            return src[i].at[j] if per_dest[i] else src[i]

        copies = []
        for k, (dx, dy, dc) in enumerate(PEER_FLIPS):
            px = 1 - x if dx else x
            py = 1 - y if dy else y
            pc = 1 - c if dc else c
            pj = 4 * px + 2 * py + pc
            for i in range(n):
                if i in seed_only:
                    continue
                cp = pltpu.make_async_remote_copy(
                    src_ref=piece(i, pj), dst_ref=dst[i].at[me],
                    send_sem=send_sems.at[k * n + i], recv_sem=recv_sems.at[k * n + i],
                    device_id=(px, py, pc), device_id_type=MESH)
                cp.start()
                arrive = pltpu.make_async_remote_copy(
                    src_ref=piece(i, pj), dst_ref=dst[i].at[pj],
                    send_sem=send_sems.at[k * n + i], recv_sem=recv_sems.at[k * n + i],
                    device_id=(px, py, pc), device_id_type=MESH)
                copies.append((cp, arrive))
        own = []
        for i in range(n):
            lc = pltpu.make_async_copy(piece(i, me), dst[i].at[me], local_sems.at[i])
            lc.start()
            own.append(lc)
        for cp, arrive in copies:
            arrive.wait_recv()
        for cp, arrive in copies:
            cp.wait_send()
        for lc in own:
            lc.wait()

    any_spec = pl.BlockSpec(memory_space=pl.ANY)
    out_specs = [any_spec] * n
    if with_token:
        out_shape.append(jax.ShapeDtypeStruct((8, LANES), F32))
        out_specs.append(pl.BlockSpec(memory_space=pltpu.VMEM))
    return pl.pallas_call(
        body, name=name, out_shape=tuple(out_shape),
        in_specs=[any_spec] * n, out_specs=tuple(out_specs),
        scratch_shapes=[pltpu.SemaphoreType.DMA((7 * n,)), pltpu.SemaphoreType.DMA((7 * n,)),
                        pltpu.SemaphoreType.DMA((n,))],
        compiler_params=pltpu.CompilerParams(has_side_effects=True),
    )(*srcs)


CHIP_FLIPS = [(0, 1), (1, 0), (1, 1)]
LEVEL_ONE = [k for k, (dx, dy, dc) in enumerate(PEER_FLIPS) if (dx, dy, dc) == (0, 0, 1) or dc == 0]


def _chip_peers(x, y):
    return [(1 - x if dx else x, 1 - y if dy else y) for dx, dy in CHIP_FLIPS]


def _gather_two_level(name, srcs, seed_only=()):
    n = len(srcs)
    live = [i for i in range(n) if i not in seed_only]

    def body(*refs):
        src, dst = refs[:n], refs[n:2 * n]
        send_sems, recv_sems, local_sems = refs[2 * n:2 * n + 3]
        bounce = refs[2 * n + 3:]
        x, y, c = lax.axis_index("x"), lax.axis_index("y"), lax.axis_index("c")
        me = 4 * x + 2 * y + c
        sibling = (x, y, 1 - c)
        chips = _chip_peers(x, y)

        def copy(k, i, src_ref, slot, to):
            return pltpu.make_async_remote_copy(
                src_ref=src_ref, dst_ref=dst[i].at[slot], send_sem=send_sems.at[k * n + i],
                recv_sem=recv_sems.at[k * n + i], device_id=to, device_id_type=MESH)

        first = []
        for i in live:
            first.append(copy(0, i, src[i], me, sibling))
            first += [copy(1 + j, i, src[i], me, (px, py, c)) for j, (px, py) in enumerate(chips)]
        for cp in first:
            cp.start()
        up = [pltpu.make_async_copy(src[i], bounce[i], local_sems.at[i]) for i in range(n)]
        for cp in up:
            cp.start()
        for cp in up:
            cp.wait()
        own = [pltpu.make_async_copy(bounce[i], dst[i].at[me], local_sems.at[i]) for i in range(n)]
        for cp in own:
            cp.start()
        passed = []
        for j, (px, py) in enumerate(chips):
            slot = 4 * px + 2 * py + c
            for i in live:
                copy(1 + j, i, src[i], slot, (px, py, c)).wait_recv()
                fwd = copy(4 + j, i, dst[i].at[slot], slot, sibling)
                fwd.start()
                passed.append(fwd)
        for i in live:
            copy(0, i, src[i], 4 * x + 2 * y + 1 - c, sibling).wait_recv()
            for j, (px, py) in enumerate(chips):
                copy(4 + j, i, src[i], 4 * px + 2 * py + 1 - c, sibling).wait_recv()
        for cp in first + passed:
            cp.wait_send()
        for cp in own:
            cp.wait()

    any_spec = pl.BlockSpec(memory_space=pl.ANY)
    return pl.pallas_call(
        body, name=name, out_shape=tuple(jax.ShapeDtypeStruct((N_DEV,) + a.shape, a.dtype) for a in srcs),
        in_specs=[any_spec] * n, out_specs=tuple([any_spec] * n),
        scratch_shapes=[pltpu.SemaphoreType.DMA((7 * n,)), pltpu.SemaphoreType.DMA((7 * n,)),
                        pltpu.SemaphoreType.DMA((n,))] + [pltpu.VMEM(a.shape, a.dtype) for a in srcs],
        compiler_params=pltpu.CompilerParams(has_side_effects=True),
    )(*srcs)


def _relay_copy(land, sems, i, n, j, slot, sibling):
    send_sems, recv_sems = sems
    return pltpu.make_async_remote_copy(
        src_ref=land[i].at[slot], dst_ref=land[i].at[slot], send_sem=send_sems.at[j * n + i],
        recv_sem=recv_sems.at[j * n + i], device_id=sibling, device_id_type=MESH)


def _relay_start(name, lands):
    n = len(lands)

    def body(*refs):
        land = refs[:n]
        sems = refs[n], refs[n + 1]
        x, y, c = lax.axis_index("x"), lax.axis_index("y"), lax.axis_index("c")
        for j, (px, py) in enumerate(_chip_peers(x, y)):
            for i in range(n):
                _relay_copy(land, sems, i, n, j, 4 * px + 2 * py + c, (x, y, 1 - c)).start()
        refs[-1][...] = jnp.zeros((8, LANES), F32)

    return pl.pallas_call(
        body, name=name,
        out_shape=(pltpu.SemaphoreType.DMA((3 * n,)), pltpu.SemaphoreType.DMA((3 * n,)),
                   *[pltpu.HBM(a.shape, a.dtype) for a in lands], jax.ShapeDtypeStruct((8, LANES), F32)),
        in_specs=[HBM_SPEC] * n,
        out_specs=(SEM_SPEC, SEM_SPEC, *[HBM_SPEC] * n, pl.BlockSpec(memory_space=pltpu.VMEM)),
        input_output_aliases={i: 2 + i for i in range(n)},
        compiler_params=pltpu.CompilerParams(has_side_effects=DATAFLOW),
    )(*[pltpu.with_memory_space_constraint(a, pltpu.HBM) for a in lands])


def _relay_wait(name, started, after):
    n = len(started) - 3
    arrays = list(started[2:2 + n])

    def body(*refs):
        land = refs[:n]
        sems = refs[n], refs[n + 1]
        x, y, c = lax.axis_index("x"), lax.axis_index("y"), lax.axis_index("c")
        for j, (px, py) in enumerate(_chip_peers(x, y)):
            for i in range(n):
                _relay_copy(land, sems, i, n, j, 4 * px + 2 * py + c, (x, y, 1 - c)).wait_send()
                _relay_copy(land, sems, i, n, j, 4 * px + 2 * py + 1 - c, (x, y, 1 - c)).wait_recv()

    return pl.pallas_call(
        body, name=name,
        out_shape=tuple(pltpu.HBM(a.shape, a.dtype) for a in arrays),
        in_specs=[HBM_SPEC] * n + [SEM_SPEC, SEM_SPEC] + [pl.BlockSpec(memory_space=pl.ANY)] * len(after),
        out_specs=tuple([HBM_SPEC] * n),
        input_output_aliases={i: i for i in range(n)},
        compiler_params=pltpu.CompilerParams(has_side_effects=DATAFLOW),
    )(*arrays, started[0], started[1], *after)


HBM_SPEC = pl.BlockSpec(memory_space=pltpu.HBM)
SEM_SPEC = pl.BlockSpec(memory_space=pltpu.SEMAPHORE)
DATAFLOW = pltpu.SideEffectType.DATAFLOW_SIDE_EFFECTING


def _peers(only=None):
    x, y, c = lax.axis_index("x"), lax.axis_index("y"), lax.axis_index("c")
    out = []
    for k, (dx, dy, dc) in enumerate(PEER_FLIPS):
        if only is not None and k not in only:
            continue
        px = 1 - x if dx else x
        py = 1 - y if dy else y
        pc = 1 - c if dc else c
        out.append((k, (px, py, pc), 4 * px + 2 * py + pc))
    return 4 * x + 2 * y + c, out


def _exchange_start(name, srcs, lands, per_dest, only=None):
    n = len(srcs)

    def body(*refs):
        src, land = refs[:n], refs[n:2 * n]
        send_sems, recv_sems = refs[2 * n], refs[2 * n + 1]
        token = refs[-1]
        me, peers = _peers(only)
        for k, peer, pj in peers:
            for i in range(n):
                pltpu.make_async_remote_copy(
                    src_ref=src[i].at[pj] if per_dest[i] else src[i], dst_ref=land[i].at[me],
                    send_sem=send_sems.at[k * n + i], recv_sem=recv_sems.at[k * n + i],
                    device_id=peer, device_id_type=MESH).start()
        token[...] = jnp.zeros((8, LANES), F32)

    arrays = list(srcs) + list(lands)
    return pl.pallas_call(
        body, name=name,
        out_shape=(pltpu.SemaphoreType.DMA((7 * n,)), pltpu.SemaphoreType.DMA((7 * n,)),
                   *[pltpu.HBM(a.shape, a.dtype) for a in arrays], jax.ShapeDtypeStruct((8, LANES), F32)),
        in_specs=[HBM_SPEC] * (2 * n),
        out_specs=(SEM_SPEC, SEM_SPEC, *[HBM_SPEC] * (2 * n), pl.BlockSpec(memory_space=pltpu.VMEM)),
        input_output_aliases={i: 2 + i for i in range(2 * n)},
        compiler_params=pltpu.CompilerParams(has_side_effects=DATAFLOW),
    )(*[pltpu.with_memory_space_constraint(a, pltpu.HBM) for a in arrays])


def _exchange_wait(name, started, per_dest, after, only=None):
    n = (len(started) - 3) // 2
    send_sems, recv_sems = started[0], started[1]
    arrays = list(started[2:2 + 2 * n])

    def body(*refs):
        src, land = refs[:n], refs[n:2 * n]
        send, recv = refs[2 * n], refs[2 * n + 1]
        me, peers = _peers(only)
        for k, peer, pj in peers:
            for i in range(n):
                cp = pltpu.make_async_remote_copy(
                    src_ref=src[i].at[pj] if per_dest[i] else src[i], dst_ref=land[i].at[pj],
                    send_sem=send.at[k * n + i], recv_sem=recv.at[k * n + i],
                    device_id=peer, device_id_type=MESH)
                cp.wait_send()
                cp.wait_recv()

    outs = pl.pallas_call(
        body, name=name,
        out_shape=tuple(pltpu.HBM(a.shape, a.dtype) for a in arrays),
        in_specs=[HBM_SPEC] * (2 * n) + [SEM_SPEC, SEM_SPEC] + [pl.BlockSpec(memory_space=pl.ANY)] * len(after),
        out_specs=tuple([HBM_SPEC] * (2 * n)),
        input_output_aliases={i: i for i in range(2 * n)},
        compiler_params=pltpu.CompilerParams(has_side_effects=DATAFLOW),
    )(*arrays, send_sems, recv_sems, *after)
    return outs[:n], outs[n:]


def _mod_shard(c_all, w_ada):
    def body(c_ref, w_ref, o_ref):
        cv = c_ref[...]
        ca = cv * _sig(cv)
        o_ref[...] = _dot(ca.astype(BF16), w_ref[...].astype(BF16))

    return pl.pallas_call(
        body, name="mod_shard", out_shape=jax.ShapeDtypeStruct((N_DEV, w_ada.shape[1]), F32),
        compiler_params=_params(),
    )(c_all, w_ada)


TI = 512


def _fwd_in(x, nw1, modnb, bada, w_main, w_ba):
    def body(x_ref, nw_ref, mod_ref, b_ref, wm_ref, wb_ref, pm_ref, pb_ref, hb_ref):
        xv = x_ref[...]
        r = lax.rsqrt(jnp.mean(xv * xv, axis=-1, keepdims=True) + EPS)
        h = (xv * r * nw_ref[...]) * (1.0 + _mod(mod_ref, b_ref, 1)) + _mod(mod_ref, b_ref, 0)
        hb = h.astype(BF16)
        hb_ref[...] = hb
        pm_ref[...] = _dot_nt(hb, wm_ref[...])
        pb_ref[...] = _dot_nt(hb, wb_ref[...])

    return pl.pallas_call(
        body, name="fwd_in", grid=(S // TI,),
        in_specs=[pl.BlockSpec((TI, D), lambda i: (i, 0)), _const((1, D)), _const((1, 6 * D)), _const((1, 6 * D)),
                  _const1((NMAIN, D)), _const((LANES, D))],
        out_specs=(pl.BlockSpec((TI, NMAIN), lambda i: (i, 0)), pl.BlockSpec((TI, LANES), lambda i: (i, 0)),
                   pl.BlockSpec((TI, D), lambda i: (i, 0))),
        out_shape=(jax.ShapeDtypeStruct((S, NMAIN), F32), jax.ShapeDtypeStruct((S, LANES), F32),
                   jax.ShapeDtypeStruct((S, D), BF16)),
        compiler_params=_params(dimension_semantics=("arbitrary",)),
    )(x, nw1, modnb, bada, w_main, w_ba)


def _group_mean_matrix():
    ii = lax.broadcasted_iota(jnp.int32, (CW, CW), 0) // GSZ
    jj = lax.broadcasted_iota(jnp.int32, (CW, CW), 1) // GSZ
    return jnp.where(ii == jj, 1.0 / GSZ, 0.0).astype(F32)


SUB = 8
SHIFT_ROWS = HALO + TM - SUB


def _fill_shifted(buf, sh):
    for b in range(1, SUB):
        sh[b - 1] = buf[b:b + SHIFT_ROWS, :]


def _rows_at(buf, sh, off):
    a, b = divmod(off, SUB)
    if b == 0:
        return buf[off:off + TM, :]
    return sh[b - 1, SUB * a:SUB * a + TM, :]


def _group_mean(x, pm):
    hi = x.astype(BF16)
    r1 = x - hi.astype(F32)
    mid = r1.astype(BF16)
    lo = (r1 - mid.astype(F32)).astype(BF16)
    return _dot(hi, pm) + _dot(mid, pm) + _dot(lo, pm)


def _conf_fwd(p_main, conv_w, conv_b, gn_w, gn_b):
    def body(a_ref, g_ref, w_ref, b_ref, gw_ref, gb_ref, y_ref, oa_ref, ubuf, ush):
        i = pl.program_id(0)

        @pl.when(i == 0)
        def _():
            ubuf[0:HALO, :] = jnp.zeros((HALO, CW), F32)

        ubuf[HALO:HALO + TM, :] = a_ref[...] * _sig(g_ref[...])
        _fill_shifted(ubuf, ush)
        acc = jnp.zeros((TM, CW), F32) + b_ref[...]
        for k in range(KC):
            acc = acc + w_ref[k:k + 1, :] * _rows_at(ubuf, ush, HALO - (KC - 1) + k)
        y_ref[...] = acc
        ubuf[0:HALO, :] = ubuf[TM:TM + HALO, :]
        pm = _group_mean_matrix().astype(BF16)
        dlt = acc - _group_mean(acc, pm)
        var = _group_mean(dlt * dlt, pm)
        o = dlt * lax.rsqrt(var + EPS) * gw_ref[...] + gb_ref[...]
        oa_ref[...] = o * _sig(o)

    return pl.pallas_call(
        body, name="conf_fwd", grid=(NT,),
        in_specs=[pl.BlockSpec((TM, CW), lambda i: (i, 0)), pl.BlockSpec((TM, CW), lambda i: (i, 1)),
                  _const((KC, CW)), _const((1, CW)), _const((1, CW)), _const((1, CW))],
        out_specs=(pl.BlockSpec((TM, CW), lambda i: (i, 0)), pl.BlockSpec((TM, CW), lambda i: (i, 0))),
        out_shape=(jax.ShapeDtypeStruct((S, CW), F32), jax.ShapeDtypeStruct((S, CW), F32)),
        scratch_shapes=[pltpu.VMEM((HALO + TM, CW), F32), pltpu.VMEM((SUB - 1, SHIFT_ROWS, CW), F32)],
        compiler_params=_params(dimension_semantics=("arbitrary",)),
    )(p_main, p_main, conv_w, conv_b, gn_w, gn_b)


def _tri_iota():
    ii = lax.broadcasted_iota(jnp.int32, (CL, CL), 0)
    jj = lax.broadcasted_iota(jnp.int32, (CL, CL), 1)
    return ii, jj


def _gdn_gates(ba, alog_l, dt_l):
    beta_all = _sig(ba)
    xg = ba + dt_l
    sp = jnp.maximum(xg, 0.0) + jnp.log(1.0 + jnp.exp(-jnp.abs(xg)))
    neg_a = -jnp.exp(alog_l)
    return beta_all, neg_a * sp, xg, neg_a


def _ones_dot(ones, x):
    hi = x.astype(BF16)
    r1 = x - hi.astype(F32)
    mid = r1.astype(BF16)
    lo = (r1 - mid.astype(F32)).astype(BF16)
    return _dot(ones, hi) + _dot(ones, mid) + _dot(ones, lo)


def _gdn_cumsum(g_all):
    ii, jj = _tri_iota()
    low = jnp.where(ii >= jj, 1.0, 0.0).astype(BF16)
    gcum = _ones_dot(low, g_all)
    return gcum, jnp.transpose(gcum)


def _split(x):
    hi = x.astype(BF16)
    return hi, (x - hi.astype(F32)).astype(BF16)


def _dot_split(a, b):
    (ah, al), (bh, bl) = a, b
    return _dot(ah, bh) + (_dot(ah, bl) + _dot(al, bh))


def _unit_lower_inverses(mats):
    ii, jj = _tri_iota()
    eye = jnp.where(ii == jj, 1.0, 0.0).astype(F32)
    ts = [eye - a for a in mats]
    ps = [_dot_split(s, s) for s in map(_split, mats)]
    for _ in range(4):
        sp = [_split(p) for p in ps]
        ts = [t + _dot_split(_split(t), s) for t, s in zip(ts, sp)]
        ps = [_dot_split(s, s) for s in sp]
    return [t + _dot_split(_split(t), _split(p)) for t, p in zip(ts, ps)]


def _head_terms(qh, kh, beta, gcol, grow):
    ii, jj = _tri_iota()
    causal = ii >= jj
    strict = ii > jj
    rq = lax.rsqrt(_rowsum(qh * qh) + EPS)
    rk = lax.rsqrt(_rowsum(kh * kh) + EPS)
    qn = qh * rq
    kn = kh * rk
    qs = qn * QSCALE
    decay = jnp.where(causal, jnp.exp(jnp.where(causal, gcol - grow, 0.0)), 0.0)
    gam = jnp.exp(gcol)
    gl = gcol[CL - 1:CL, :]
    kds = jnp.exp(gl - gcol)
    cd = jnp.exp(gl)
    kb = kn * beta
    a = jnp.where(strict, _dot_nt(kb, kn, GP) * decay, 0.0)
    qk = jnp.where(causal, _dot_nt(qs, kn, GP) * decay, 0.0)
    return dict(rq=rq, rk=rk, qn=qn, kn=kn, qs=qs, decay=decay, gam=gam, kds=kds, cd=cd, kb=kb, a=a, qk=qk,
                causal=causal, strict=strict)


def _short_conv(w_ref, buf, rows=CL):
    acc = w_ref[0:1, :] * buf[SH - KS + 1:SH - KS + 1 + rows, :]
    for k in range(1, KS):
        off = SH - (KS - 1) + k
        acc = acc + w_ref[k:k + 1, :] * buf[off:off + rows, :]
    return acc


CPS = 4
TG = CPS * CL


def _gdn_prep(p_main, p_ba, gdn_conv_w, alog_l, dt_l):
    def body(q_ref, k_ref, v_ref, qh_ref, kh_ref, vh_ref, ba_ref, w_ref, al_ref, dt_ref,
             wo_ref, uo_ref, qg_ref, kd_ref, qk_ref, cd_ref, t_ref, xbuf):
        i = pl.program_id(0)
        first = i == 0
        xbuf[0:SH, 0:GW] = jnp.where(first, 0.0, qh_ref[...])
        xbuf[0:SH, GW:2 * GW] = jnp.where(first, 0.0, kh_ref[...])
        xbuf[0:SH, 2 * GW:3 * GW] = jnp.where(first, 0.0, vh_ref[...])
        xbuf[SH:SH + TG, 0:GW] = q_ref[...]
        xbuf[SH:SH + TG, GW:2 * GW] = k_ref[...]
        xbuf[SH:SH + TG, 2 * GW:3 * GW] = v_ref[...]
        conv = _short_conv(w_ref, xbuf, TG)
        qkv = conv * _sig(conv)
        beta_all, g_all, _, _ = _gdn_gates(ba_ref[...], al_ref[...], dt_ref[...])
        lane = lax.broadcasted_iota(jnp.int32, (8, LANES), 1)
        cums = [_gdn_cumsum(g_all[cc * CL:(cc + 1) * CL, :]) for cc in range(CPS)]
        pairs = [(cc, h) for cc in range(CPS) for h in range(NH)]
        terms, vbs = [], []
        for cc, h in pairs:
            r0, lo = cc * CL, h * DH
            beta = beta_all[r0:r0 + CL, h:h + 1]
            gcum, gcum_t = cums[cc]
            terms.append(_head_terms(qkv[r0:r0 + CL, lo:lo + DH], qkv[r0:r0 + CL, GW + lo:GW + lo + DH], beta,
                                     gcum[:, NH + h:NH + h + 1], gcum_t[NH + h:NH + h + 1, :]))
            vbs.append(qkv[r0:r0 + CL, 2 * GW + lo:2 * GW + lo + DH] * beta)
        invs = _unit_lower_inverses([f["a"] for f in terms])
        cds = [jnp.zeros((8, LANES), F32) for _ in range(CPS)]
        for (cc, h), f, t, vb in zip(pairs, terms, invs, vbs):
            r0, lo = cc * CL, h * DH
            t_ref[cc, h] = t
            uo_ref[r0:r0 + CL, lo:lo + DH] = _dot(t, vb, GP)
            wo_ref[r0:r0 + CL, lo:lo + DH] = _dot(t, f["kb"] * f["gam"], GP).astype(BF16)
            qg_ref[r0:r0 + CL, lo:lo + DH] = (f["qs"] * f["gam"]).astype(BF16)
            kd_ref[r0:r0 + CL, lo:lo + DH] = (f["kn"] * f["kds"]).astype(BF16)
            qk_ref[cc, h] = f["qk"].astype(BF16)
            cds[cc] = cds[cc] + jnp.where(lane == h, f["cd"], 0.0)
        for cc in range(CPS):
            cd_ref[cc] = cds[cc]

    col = lambda j: pl.BlockSpec((TG, GW), lambda i: (i, j))
    halo = lambda j: pl.BlockSpec((SH, GW), lambda i: (jnp.maximum(i * (TG // SH) - 1, 0), j))
    tile = lambda: pl.BlockSpec((TG, GW), lambda i: (i, 0))
    sq = lambda: pl.BlockSpec((CPS, NH, CL, CL), lambda i: (i, 0, 0, 0))
    return pl.pallas_call(
        body, name="gdn_prep", grid=(NCH // CPS,),
        in_specs=[col(2), col(3), col(4), halo(2), halo(3), halo(4), pl.BlockSpec((TG, LANES), lambda i: (i, 0)),
                  _const((KS, 3 * GW)), _const((1, LANES)), _const((1, LANES))],
        out_specs=(tile(), tile(), tile(), tile(), sq(), pl.BlockSpec((CPS, 8, LANES), lambda i: (i, 0, 0)), sq()),
        out_shape=(jax.ShapeDtypeStruct((S, GW), BF16), jax.ShapeDtypeStruct((S, GW), F32),
                   jax.ShapeDtypeStruct((S, GW), BF16), jax.ShapeDtypeStruct((S, GW), BF16),
                   jax.ShapeDtypeStruct((NCH, NH, CL, CL), BF16), jax.ShapeDtypeStruct((NCH, 8, LANES), F32),
                   jax.ShapeDtypeStruct((NCH, NH, CL, CL), F32)),
        scratch_shapes=[pltpu.VMEM((SH + TG, 3 * GW), F32)],
        compiler_params=_params(dimension_semantics=("arbitrary",)),
    )(p_main, p_main, p_main, p_main, p_main, p_main, p_ba, gdn_conv_w, alog_l, dt_l)


def _gdn_scan(w_o, u_o, qg, kd, qk, cd, p_main, gdn_nw):
    def body(w_ref, u_ref, qg_ref, kd_ref, qk_ref, cd_ref, z_ref, nw_ref, ob_ref, o_ref, sin_ref, state):
        n = pl.program_id(0)

        @pl.when(n == 0)
        def _():
            state[...] = jnp.zeros((NH, DH, DH), F32)

        def head(cc, h):
            rows, lo = pl.ds(cc * CL, CL), h * DH
            st = state[h]
            sin_ref[cc, h] = st
            sb = st.astype(BF16)
            v_new = u_ref[rows, lo:lo + DH] - _dot(w_ref[rows, lo:lo + DH], sb)
            yield
            vb = v_new.astype(BF16)
            o = _dot(qg_ref[rows, lo:lo + DH], sb) + _dot(qk_ref[cc, h], vb)
            state[h] = st * cd_ref[cc, 0:1, h:h + 1] + _dot_tn(kd_ref[rows, lo:lo + DH], vb)
            yield
            o_ref[rows, lo:lo + DH] = o
            r = lax.rsqrt(jnp.mean(o * o, axis=-1, keepdims=True) + EPS)
            zh = z_ref[rows, lo:lo + DH]
            ob_ref[rows, lo:lo + DH] = o * r * nw_ref[...] * (zh * _sig(zh))

        for cc in range(CPS):
            _lockstep(head(cc, h) for h in range(NH))

    tile = lambda: pl.BlockSpec((TG, GW), lambda n: (n, 0))
    return pl.pallas_call(
        body, name="gdn_scan", grid=(NCH // CPS,),
        in_specs=[tile(), tile(), tile(), tile(), pl.BlockSpec((CPS, NH, CL, CL), lambda n: (n, 0, 0, 0)),
                  pl.BlockSpec((CPS, 8, LANES), lambda n: (n, 0, 0)), pl.BlockSpec((TG, GW), lambda n: (n, 5)),
                  _const((1, DH))],
        out_specs=(tile(), tile(), pl.BlockSpec((CPS, NH, DH, DH), lambda n: (n, 0, 0, 0))),
        out_shape=(jax.ShapeDtypeStruct((S, GW), F32), jax.ShapeDtypeStruct((S, GW), F32),
                   jax.ShapeDtypeStruct((NCH, NH, DH, DH), F32)),
        scratch_shapes=[pltpu.VMEM((NH, DH, DH), F32)],
        compiler_params=_params(dimension_semantics=("arbitrary",)),
    )(w_o, u_o, qg, kd, qk, cd, p_main, gdn_nw)


def _fwd_out(out_a, out_b, x, modnb, bada, w_out):
    def body(oa_ref, ob_ref, x_ref, mod_ref, b_ref, w_ref, x1_ref, mix_ref, oab_ref):
        oa = oa_ref[...].astype(BF16)
        ob = ob_ref[...].astype(BF16)
        oab_ref[:, 0:CW] = oa
        oab_ref[:, CW:D] = ob
        mix = _dot(oa, w_ref[0:CW, :]) + _dot(ob, w_ref[CW:D, :])
        mix_ref[...] = mix
        x1_ref[...] = x_ref[...] + _mod(mod_ref, b_ref, 2) * mix

    tile = lambda w: pl.BlockSpec((TM, w), lambda i: (i, 0))
    return pl.pallas_call(
        body, name="fwd_out", grid=(NT,),
        in_specs=[tile(CW), tile(GW), tile(D), _const((1, 6 * D)), _const((1, 6 * D)), _const((D, D))],
        out_specs=(tile(D), tile(D), tile(D)),
        out_shape=(jax.ShapeDtypeStruct((S, D), F32), jax.ShapeDtypeStruct((S, D), F32),
                   jax.ShapeDtypeStruct((S, D), BF16)),
        compiler_params=_params(dimension_semantics=("arbitrary",)),
    )(out_a, out_b, x, modnb, bada, w_out)


FFN_STATS = 8


def _ffn_weight_scratch(w_fi, w_fo):
    return [pltpu.VMEM(w_fi.shape, w_fi.dtype), pltpu.VMEM(w_fo.shape, w_fo.dtype),
            pltpu.SemaphoreType.DMA((N_DEV + 4,))]


def _ffn_weight_loads(step, wi_hbm, wo_hbm, wi, wo, sems):
    def loads(j):
        return [pltpu.make_async_copy(wo_hbm.at[j], wo.at[j], sems.at[N_DEV + j]),
                pltpu.make_async_copy(wi_hbm.at[j], wi.at[j], sems.at[j]),
                pltpu.make_async_copy(wi_hbm.at[j + 4], wi.at[j + 4], sems.at[j + 4])]

    @pl.when(step == 0)
    def _():
        for j in range(4):
            for cp in loads(j):
                cp.start()

    def wait(j):
        @pl.when(step == 0)
        def _():
            for cp in loads(j):
                cp.wait()

    return wait


def _ffn_forward(x1, tgt, modnb, bada, nw2, nfw, w_fi, w_fo):
    def body(x1_ref, tgt_ref, mod_ref, b_ref, nw2_ref, nfw_ref, wi_hbm, wo_hbm,
             hb_ref, act_ref, pre_ref, dx2_ref, dffn_ref, st_ref, wi_ref, wo_ref, sems):
        i = pl.program_id(0)
        weights_ready = _ffn_weight_loads(i, wi_hbm, wo_hbm, wi_ref, wo_ref, sems)

        @pl.when(i == 0)
        def _():
            st_ref[...] = jnp.zeros((FFN_STATS, D), F32)

        sh2, sc2, gt2 = _mod(mod_ref, b_ref, 3), _mod(mod_ref, b_ref, 4), _mod(mod_ref, b_ref, 5)
        x1v = x1_ref[...]
        r2 = lax.rsqrt(jnp.mean(x1v * x1v, axis=-1, keepdims=True) + EPS)
        hb = ((x1v * r2 * nw2_ref[...]) * (1.0 + sc2) + sh2).astype(BF16)
        hb_ref[...] = hb
        ffn = jnp.zeros((TM, D), F32)
        for j in range(4):
            weights_ready(j)
            fgj = _dot_nt(hb, wi_ref[j])
            fuj = _dot_nt(hb, wi_ref[j + 4])
            pre_ref[j] = fgj.astype(BF16)
            pre_ref[j + 4] = fuj.astype(BF16)
            aj = (fgj * _sig(fgj) * fuj).astype(BF16)
            act_ref[j] = aj
            ffn = ffn + _dot(aj, wo_ref[j])
        x2 = x1v + gt2 * ffn
        r3 = lax.rsqrt(jnp.mean(x2 * x2, axis=-1, keepdims=True) + EPS)
        xr3 = x2 * r3
        err = xr3 * nfw_ref[...] - tgt_ref[...]
        loss = 0.5 * jnp.sum(jnp.mean(err * err, axis=-1, keepdims=True), axis=0, keepdims=True)
        dy = err * (1.0 / D)
        st_ref[0:1, :] += _colsum(dy * xr3)
        dyr = dy * nfw_ref[...]
        dx2 = r3 * (dyr - xr3 * jnp.mean(dyr * xr3, axis=-1, keepdims=True))
        st_ref[1:2, :] += _colsum(dx2 * ffn)
        st_ref[5:6, :] += jnp.broadcast_to(loss, (1, D))
        dx2_ref[...] = dx2
        dffn_ref[...] = (gt2 * dx2).astype(BF16)

    tile = lambda w: pl.BlockSpec((TM, w), lambda i: (i, 0))
    return pl.pallas_call(
        body, name="ffn_forward", grid=(NT,),
        in_specs=[tile(D), tile(D), _const((1, 6 * D)), _const((1, 6 * D)), _const((1, D)), _const((1, D)),
                  pl.BlockSpec(memory_space=pl.ANY), pl.BlockSpec(memory_space=pl.ANY)],
        out_specs=(tile(D), pl.BlockSpec((4, TM, FB), lambda i: (0, i, 0)),
                   pl.BlockSpec((N_DEV, TM, FB), lambda i: (0, i, 0)), tile(D), tile(D), _const((FFN_STATS, D))),
        out_shape=(jax.ShapeDtypeStruct((S, D), BF16), jax.ShapeDtypeStruct((4, S, FB), BF16),
                   jax.ShapeDtypeStruct((N_DEV, S, FB), BF16), jax.ShapeDtypeStruct((S, D), F32),
                   jax.ShapeDtypeStruct((S, D), BF16), jax.ShapeDtypeStruct((FFN_STATS, D), F32)),
        scratch_shapes=_ffn_weight_scratch(w_fi, w_fo),
        compiler_params=_params(42, dimension_semantics=("arbitrary",)),
    )(x1, tgt, modnb, bada, nw2, nfw, w_fi, w_fo)


def _ffn_backward(dffn, pre, x1, dx2, modnb, bada, nw2, w_fi, w_fo):
    def body(dffn_ref, pre_ref, x1_ref, dx2_ref, mod_ref, b_ref, nw2_ref, wi_hbm, wo_hbm, df_ref, dx1_ref, st_ref,
             wi_ref, wo_ref, sems):
        i = pl.program_id(0)
        weights_ready = _ffn_weight_loads(i, wi_hbm, wo_hbm, wi_ref, wo_ref, sems)

        @pl.when(i == 0)
        def _():
            st_ref[...] = jnp.zeros((FFN_STATS, D), F32)

        dffn = dffn_ref[...]
        dh = jnp.zeros((TM, D), F32)
        for j in range(4):
            weights_ready(j)
            fg = pre_ref[j].astype(F32)
            fu = pre_ref[j + 4].astype(F32)
            sg = _sig(fg)
            dact = _dot_nt(dffn, wo_ref[j])
            dfg = (dact * fu * (sg * (1.0 + fg * (1.0 - sg)))).astype(BF16)
            dfu = (dact * (fg * sg)).astype(BF16)
            df_ref[j] = dfg
            df_ref[j + 4] = dfu
            dh = dh + _dot(dfg, wi_ref[j]) + _dot(dfu, wi_ref[j + 4])
        x1v = x1_ref[...]
        r2 = lax.rsqrt(jnp.mean(x1v * x1v, axis=-1, keepdims=True) + EPS)
        xr2 = x1v * r2
        st_ref[2:3, :] += _colsum(dh)
        st_ref[3:4, :] += _colsum(dh * (xr2 * nw2_ref[...]))
        dxn = dh * (1.0 + _mod(mod_ref, b_ref, 4))
        st_ref[4:5, :] += _colsum(dxn * xr2)
        dxr = dxn * nw2_ref[...]
        dx1_ref[...] = dx2_ref[...] + r2 * (dxr - xr2 * jnp.mean(dxr * xr2, axis=-1, keepdims=True))

    tile = lambda w: pl.BlockSpec((TM, w), lambda i: (i, 0))
    wide = lambda: pl.BlockSpec((N_DEV, TM, FB), lambda i: (0, i, 0))
    return pl.pallas_call(
        body, name="ffn_backward", grid=(NT,),
        in_specs=[tile(D), wide(), tile(D), tile(D), _const((1, 6 * D)), _const((1, 6 * D)), _const((1, D)),
                  pl.BlockSpec(memory_space=pl.ANY), pl.BlockSpec(memory_space=pl.ANY)],
        out_specs=(wide(), tile(D), _const((FFN_STATS, D))),
        out_shape=(jax.ShapeDtypeStruct((N_DEV, S, FB), BF16), jax.ShapeDtypeStruct((S, D), F32),
                   jax.ShapeDtypeStruct((FFN_STATS, D), F32)),
        scratch_shapes=_ffn_weight_scratch(w_fi, w_fo),
        compiler_params=_params(44, dimension_semantics=("arbitrary",)),
    )(dffn, pre, x1, dx2, modnb, bada, nw2, w_fi, w_fo)


def _grad_w(name, a, b, nb):
    m, n = a.shape[1], b.shape[1]

    def body(a_ref, b_ref, o_ref):
        o_ref[...] = _dot_tn(a_ref[...], b_ref[...]).astype(BF16)

    return pl.pallas_call(
        body, name=name, grid=(m // nb,),
        in_specs=[pl.BlockSpec((S, nb), lambda j: (0, j)), _const((S, n))],
        out_specs=pl.BlockSpec((nb, n), lambda j: (j, 0)),
        out_shape=jax.ShapeDtypeStruct((m, n), BF16),
        compiler_params=_params(dimension_semantics=("arbitrary",)),
    )(a, b)


GW_IN_ROWS = NMAIN + LANES


def _grad_w_in(dp_conf, dp_gdn, dp_ba, hb1):
    nb = 512
    n_conf, n_gdn = 2 * CW // nb, 4 * GW // nb

    def body(c_ref, g_ref, ba_ref, h_ref, o_ref):
        j = pl.program_id(0)

        @pl.when(j < n_conf)
        def _():
            o_ref[...] = _dot_tn(c_ref[...], h_ref[...]).astype(BF16)

        @pl.when((j >= n_conf) & (j < n_conf + n_gdn))
        def _():
            o_ref[...] = _dot_tn(g_ref[...], h_ref[...]).astype(BF16)

        @pl.when(j == n_conf + n_gdn)
        def _():
            o_ref[0:LANES, :] = _dot_tn(ba_ref[...], h_ref[...]).astype(BF16)

    return pl.pallas_call(
        body, name="grad_w_in", grid=(n_conf + n_gdn + 1,),
        in_specs=[pl.BlockSpec((S, nb), lambda j: (0, jnp.minimum(j, n_conf - 1))),
                  pl.BlockSpec((S, nb), lambda j: (0, jnp.clip(j - n_conf, 0, n_gdn - 1))),
                  _const((S, LANES)), _const((S, D))],
        out_specs=pl.BlockSpec((nb, D), lambda j: (j, 0)),
        out_shape=jax.ShapeDtypeStruct((GW_IN_ROWS, D), BF16),
        compiler_params=_params(dimension_semantics=("arbitrary",)),
    )(dp_conf, dp_gdn, dp_ba, hb1)


def _grad_w_ffn_in(hb2, df):
    def body(a_ref, b_ref, o_ref):
        o_ref[0] = _dot_tn(b_ref[0], a_ref[...]).astype(BF16)

    return pl.pallas_call(
        body, name="grad_w_ffn_in", grid=(N_DEV,),
        in_specs=[_const((S, D)), pl.BlockSpec((1, S, FB), lambda j: (j, 0, 0))],
        out_specs=pl.BlockSpec((1, FB, D), lambda j: (j, 0, 0)),
        out_shape=jax.ShapeDtypeStruct((N_DEV, FB, D), BF16),
        compiler_params=_params(dimension_semantics=("arbitrary",)),
    )(hb2, df)


def _grad_w_ffn_out(act, dffn):
    def body(a_ref, b_ref, o_ref):
        o_ref[0] = _dot_tn(a_ref[0], b_ref[...]).astype(BF16)

    return pl.pallas_call(
        body, name="grad_w_ffn_out", grid=(4,),
        in_specs=[pl.BlockSpec((1, S, FB), lambda j: (j, 0, 0)), _const((S, D))],
        out_specs=pl.BlockSpec((1, FB, D), lambda j: (j, 0, 0)),
        out_shape=jax.ShapeDtypeStruct((4, FB, D), BF16),
        compiler_params=_params(dimension_semantics=("arbitrary",)),
    )(act, dffn)


def _bwd_out(dx1, mix, modnb, bada, w_out):
    def body(dx_ref, mix_ref, mod_ref, b_ref, w_ref, dmix_ref, doa_ref, dob_ref, st_ref):
        i = pl.program_id(0)

        @pl.when(i == 0)
        def _():
            st_ref[...] = jnp.zeros((8, D), F32)

        dx = dx_ref[...]
        st_ref[0:1, :] += _colsum(dx * mix_ref[...])
        dmix = (_mod(mod_ref, b_ref, 2) * dx).astype(BF16)
        dmix_ref[...] = dmix
        doa_ref[...] = _dot_nt(dmix, w_ref[0:CW, :])
        dob_ref[...] = _dot_nt(dmix, w_ref[CW:D, :])

    tile = lambda w: pl.BlockSpec((TM, w), lambda i: (i, 0))
    return pl.pallas_call(
        body, name="bwd_out", grid=(NT,),
        in_specs=[tile(D), tile(D), _const((1, 6 * D)), _const((1, 6 * D)), _const((D, D))],
        out_specs=(tile(D), tile(CW), tile(GW), _const((8, D))),
        out_shape=(jax.ShapeDtypeStruct((S, D), BF16), jax.ShapeDtypeStruct((S, CW), F32),
                   jax.ShapeDtypeStruct((S, GW), F32), jax.ShapeDtypeStruct((8, D), F32)),
        compiler_params=_params(dimension_semantics=("arbitrary",)),
    )(dx1, mix, modnb, bada, w_out)


CONF_STATS = 40


def _conf_bwd(d_out_a, y, p_main, conv_w, gn_w, gn_b):
    def body(do_ref, y_ref, a_ref, g_ref, ah_ref, gh_ref, w_ref, gw_ref, gb_ref, dp_ref, st_ref,
             ubuf, dybuf, ush, dysh):
        i = pl.program_id(0)

        @pl.when(i == 0)
        def _():
            st_ref[...] = jnp.zeros((CONF_STATS, CW), F32)
            dybuf[TM:TM + HALO, :] = jnp.zeros((HALO, CW), F32)

        pm = _group_mean_matrix().astype(BF16)
        yv = y_ref[...]
        dlt = yv - _group_mean(yv, pm)
        rstd = lax.rsqrt(_group_mean(dlt * dlt, pm) + EPS)
        un = dlt * rstd
        o = un * gw_ref[...] + gb_ref[...]
        so = _sig(o)
        d_o = do_ref[...] * (so * (1.0 + o * (1.0 - so)))
        st_ref[33:34, :] += _colsum(d_o)
        st_ref[32:33, :] += _colsum(d_o * un)
        dun = d_o * gw_ref[...]
        dy = rstd * (dun - _group_mean(dun, pm) - un * _group_mean(dun * un, pm))
        st_ref[31:32, :] += _colsum(dy)
        dybuf[0:TM, :] = dy
        _fill_shifted(dybuf, dysh)

        a = a_ref[...]
        sg = _sig(g_ref[...])
        first = i == NT - 1
        ubuf[0:HALO, :] = jnp.where(first, 0.0, ah_ref[...] * _sig(gh_ref[...]))
        ubuf[HALO:HALO + TM, :] = a * sg
        _fill_shifted(ubuf, ush)
        du = jnp.zeros((TM, CW), F32)
        for k in range(KC):
            st_ref[k:k + 1, :] += _colsum(dy * _rows_at(ubuf, ush, HALO - (KC - 1) + k))
            du = du + w_ref[k:k + 1, :] * _rows_at(dybuf, dysh, KC - 1 - k)
        dybuf[TM:TM + HALO, :] = dybuf[0:HALO, :]
        dp_ref[:, 0:CW] = (du * sg).astype(BF16)
        dp_ref[:, CW:2 * CW] = (du * a * sg * (1.0 - sg)).astype(BF16)

    rev = lambda w, j=0: pl.BlockSpec((TM, w), lambda i: (NT - 1 - i, j))
    halo = lambda j: pl.BlockSpec((HALO, CW), lambda i: (jnp.maximum((NT - 1 - i) * (TM // HALO) - 1, 0), j))
    return pl.pallas_call(
        body, name="conf_bwd", grid=(NT,),
        in_specs=[rev(CW), rev(CW), rev(CW, 0), rev(CW, 1), halo(0), halo(1),
                  _const((KC, CW)), _const((1, CW)), _const((1, CW))],
        out_specs=(rev(2 * CW), _const((CONF_STATS, CW))),
        out_shape=(jax.ShapeDtypeStruct((S, 2 * CW), BF16), jax.ShapeDtypeStruct((CONF_STATS, CW), F32)),
        scratch_shapes=[pltpu.VMEM((HALO + TM, CW), F32), pltpu.VMEM((TM + HALO, CW), F32),
                        pltpu.VMEM((SUB - 1, SHIFT_ROWS, CW), F32), pltpu.VMEM((SUB - 1, SHIFT_ROWS, CW), F32)],
        compiler_params=_params(dimension_semantics=("arbitrary",)),
    )(d_out_a, y, p_main, p_main, p_main, p_main, conv_w, gn_w, gn_b)


GDN_STATS = 8


def _gdn_bwd(d_out_b, o_pre, s_in, t_inv, p_main, p_ba, gdn_conv_w, alog_l, dt_l, gdn_nw):
    def body(dob_ref, o_ref, sin_ref, t_ref, q_ref, k_ref, v_ref, z_ref, qh_ref, kh_ref, vh_ref, ba_ref,
             w_ref, al_ref, dt_ref, nw_ref, dp_ref, dba_ref, st_ref, xbuf, dcbuf, dstate):
        n = pl.program_id(0)

        @pl.when(n == 0)
        def _():
            st_ref[...] = jnp.zeros((GDN_STATS, 3 * GW), F32)
            dcbuf[CL:CL + SH, :] = jnp.zeros((SH, 3 * GW), F32)
            dstate[...] = jnp.zeros((NH, DH, DH), F32)

        for cc in reversed(range(CPS)):
            chunk(n, cc, dob_ref, o_ref, sin_ref, t_ref, q_ref, k_ref, v_ref, z_ref, qh_ref, kh_ref, vh_ref, ba_ref,
                  w_ref, al_ref, dt_ref, nw_ref, dp_ref, dba_ref, st_ref, xbuf, dcbuf, dstate)

    def chunk(n, cc, dob_ref, o_ref, sin_ref, t_ref, q_ref, k_ref, v_ref, z_ref, qh_ref, kh_ref, vh_ref, ba_ref,
              w_ref, al_ref, dt_ref, nw_ref, dp_ref, dba_ref, st_ref, xbuf, dcbuf, dstate):
        r0 = cc * CL
        if cc == 0:
            first = n == NCH // CPS - 1
            xbuf[0:SH, 0:GW] = jnp.where(first, 0.0, qh_ref[...])
            xbuf[0:SH, GW:2 * GW] = jnp.where(first, 0.0, kh_ref[...])
            xbuf[0:SH, 2 * GW:3 * GW] = jnp.where(first, 0.0, vh_ref[...])
        else:
            xbuf[0:SH, 0:GW] = q_ref[r0 - SH:r0, :]
            xbuf[0:SH, GW:2 * GW] = k_ref[r0 - SH:r0, :]
            xbuf[0:SH, 2 * GW:3 * GW] = v_ref[r0 - SH:r0, :]
        xbuf[SH:SH + CL, 0:GW] = q_ref[r0:r0 + CL, :]
        xbuf[SH:SH + CL, GW:2 * GW] = k_ref[r0:r0 + CL, :]
        xbuf[SH:SH + CL, 2 * GW:3 * GW] = v_ref[r0:r0 + CL, :]
        conv = _short_conv(w_ref, xbuf)
        sc = _sig(conv)
        qkv = conv * sc
        ba = ba_ref[r0:r0 + CL, :]
        beta_all, g_all, xg, neg_a = _gdn_gates(ba, al_ref[...], dt_ref[...])
        gcum, gcum_t = _gdn_cumsum(g_all)
        lane = lax.broadcasted_iota(jnp.int32, (CL, LANES), 1)
        row = lax.broadcasted_iota(jnp.int32, (CL, 1), 0)
        acc = dict(dgcum=jnp.zeros((CL, LANES), F32), dbeta=jnp.zeros((CL, LANES), F32))

        def head(h):
            lo = h * DH
            qh = qkv[:, lo:lo + DH]
            kh = qkv[:, GW + lo:GW + lo + DH]
            vh = qkv[:, 2 * GW + lo:2 * GW + lo + DH]
            beta = beta_all[:, h:h + 1]
            f = _head_terms(qh, kh, beta, gcum[:, NH + h:NH + h + 1], gcum_t[NH + h:NH + h + 1, :])
            qn, kn, qs, kb, gam, kds, cd, decay = (f[s] for s in ("qn", "kn", "qs", "kb", "gam", "kds", "cd", "decay"))
            t = t_ref[cc, h]
            st = sin_ref[cc, h]
            vb = vh * beta
            kbg = kb * gam
            u = _dot(t, vb, GP)
            w = _dot(t, kbg, GP)
            yield
            v_new = u - _dot(w, st, GP)
            q_dec = qs * gam
            k_dec = kn * kds

            o = o_ref[r0:r0 + CL, lo:lo + DH]
            zh = z_ref[r0:r0 + CL, lo:lo + DH]
            sz = _sig(zh)
            r = lax.rsqrt(jnp.mean(o * o, axis=-1, keepdims=True) + EPS)
            orr = o * r
            d_out = dob_ref[r0:r0 + CL, lo:lo + DH]
            dz = d_out * (orr * nw_ref[...]) * (sz * (1.0 + zh * (1.0 - sz)))
            don = d_out * (zh * sz)
            st_ref[4:5, 0:DH] += _colsum(don * orr)
            tt = don * nw_ref[...]
            d_o = r * (tt - orr * jnp.mean(tt * orr, axis=-1, keepdims=True))

            yield
            ds_out = dstate[h]
            dv_new = _dot_tn(f["qk"], d_o, GP) + _dot(k_dec, ds_out, GP)
            dqk = jnp.where(f["causal"], _dot_nt(d_o, v_new, GP), 0.0)
            dq_dec = _dot_nt(d_o, st, GP)
            dk_dec = _dot_nt(v_new, ds_out, GP)
            yield
            dstate[h] = _dot_tn(q_dec, d_o, GP) + cd * ds_out - _dot_tn(w, dv_new, GP)
            dcd = jnp.sum(_rowsum(st * ds_out), axis=0, keepdims=True)
            dw = -_dot_nt(dv_new, st, GP)
            dvb = _dot_tn(t, dv_new, GP)
            yield
            dt_m = _dot_nt(dv_new, vb, GP) + _dot_nt(dw, kbg, GP)
            dkbg = _dot_tn(t, dw, GP)
            yield
            dtt = _dot_nt(dt_m, t, GP)
            yield
            da = jnp.where(f["strict"], -_dot_tn(t, dtt, GP), 0.0)
            yield
            dad = da * decay
            dqkd = dqk * decay
            dkb = _dot(dad, kn, GP) + dkbg * gam
            dkn = _dot_tn(dad, kb, GP) + _dot_tn(dqkd, qs, GP) + dk_dec * kds + dkb * beta
            dqs = _dot(dqkd, kn, GP) + dq_dec * gam
            yield
            m = da * f["a"] + dqk * f["qk"]
            tk = _rowsum(dk_dec * k_dec)
            dgl = jnp.sum(tk, axis=0, keepdims=True) + dcd * cd
            dgc = (_rowsum(m) - _rowsum(jnp.transpose(m)) + _rowsum(dq_dec * q_dec) - tk + _rowsum(dkbg * kbg)
                   + jnp.where(row == CL - 1, dgl, 0.0))
            dbeta = _rowsum(dkb * kn) + _rowsum(dvb * vh)
            acc["dgcum"] = acc["dgcum"] + jnp.where(lane == NH + h, dgc, 0.0)
            acc["dbeta"] = acc["dbeta"] + jnp.where(lane == h, dbeta, 0.0)
            dvh = dvb * beta
            dqn = dqs * QSCALE
            dqh = f["rq"] * (dqn - qn * _rowsum(dqn * qn))
            dkh = f["rk"] * (dkn - kn * _rowsum(dkn * kn))
            dsilu = lambda c0: sc[:, c0:c0 + DH] * (1.0 + conv[:, c0:c0 + DH] * (1.0 - sc[:, c0:c0 + DH]))
            dcbuf[0:CL, lo:lo + DH] = dqh * dsilu(lo)
            dcbuf[0:CL, GW + lo:GW + lo + DH] = dkh * dsilu(GW + lo)
            dcbuf[0:CL, 2 * GW + lo:2 * GW + lo + DH] = dvh * dsilu(2 * GW + lo)
            dp_ref[r0:r0 + CL, 3 * GW + lo:3 * GW + lo + DH] = dz.astype(BF16)

        _lockstep(head(h) for h in range(NH))
        dgcum_all, dbeta_all = acc["dgcum"], acc["dbeta"]

        ii, jj = _tri_iota()
        upper = jnp.where(ii <= jj, 1.0, 0.0).astype(BF16)
        dg_all = _ones_dot(upper, dgcum_all)
        dxg = dg_all * neg_a * _sig(xg)
        st_ref[5:6, 0:LANES] += _colsum(dg_all * g_all)
        st_ref[6:7, 0:LANES] += _colsum(dxg)
        dbl = dbeta_all * beta_all * (1.0 - beta_all)
        dba_ref[r0:r0 + CL, :] = jnp.where(lane < NH, dbl, jnp.where(lane < 2 * NH, dxg, 0.0)).astype(BF16)

        dconv = dcbuf[0:CL, :]
        dx = w_ref[0:1, :] * dcbuf[KS - 1:KS - 1 + CL, :]
        st_ref[0:1, :] += _colsum(dconv * xbuf[SH - KS + 1:SH - KS + 1 + CL, :])
        for k in range(1, KS):
            off = SH - (KS - 1) + k
            st_ref[k:k + 1, :] += _colsum(dconv * xbuf[off:off + CL, :])
            dx = dx + w_ref[k:k + 1, :] * dcbuf[KS - 1 - k:KS - 1 - k + CL, :]
        dcbuf[CL:CL + SH, :] = dcbuf[0:SH, :]
        dp_ref[r0:r0 + CL, 0:3 * GW] = dx.astype(BF16)

    steps = NCH // CPS
    rev = lambda w, j=0: pl.BlockSpec((TG, w), lambda n: (steps - 1 - n, j))
    halo = lambda j: pl.BlockSpec((SH, GW), lambda n: (jnp.maximum((steps - 1 - n) * (TG // SH) - 1, 0), j))
    blk4 = lambda a, b: pl.BlockSpec((CPS, NH, a, b), lambda n: (steps - 1 - n, 0, 0, 0))
    return pl.pallas_call(
        body, name="gdn_bwd", grid=(steps,),
        in_specs=[rev(GW), rev(GW), blk4(DH, DH), blk4(CL, CL), rev(GW, 2), rev(GW, 3), rev(GW, 4), rev(GW, 5),
                  halo(2), halo(3), halo(4), rev(LANES), _const((KS, 3 * GW)), _const((1, LANES)),
                  _const((1, LANES)), _const((1, DH))],
        out_specs=(rev(4 * GW), rev(LANES), _const((GDN_STATS, 3 * GW))),
        out_shape=(jax.ShapeDtypeStruct((S, 4 * GW), BF16), jax.ShapeDtypeStruct((S, LANES), BF16),
                   jax.ShapeDtypeStruct((GDN_STATS, 3 * GW), F32)),
        scratch_shapes=[pltpu.VMEM((SH + CL, 3 * GW), F32), pltpu.VMEM((CL + SH, 3 * GW), F32),
                        pltpu.VMEM((NH, DH, DH), F32)],
        compiler_params=_params(dimension_semantics=("arbitrary",)),
    )(d_out_b, o_pre, s_in, t_inv, p_main, p_main, p_main, p_main, p_main, p_main, p_main, p_ba,
      gdn_conv_w, alog_l, dt_l, gdn_nw)


def _bwd_in(dp_conf, dp_gdn, dp_ba, x, dx1, nw1, modnb, bada, w_main, w_ba):
    def body(dc_ref, dg_ref, db_ref, x_ref, dx1_ref, nw_ref, mod_ref, b_ref, wm_ref, wb_ref, gx_ref, st_ref):
        i = pl.program_id(0)

        @pl.when(i == 0)
        def _():
            st_ref[...] = jnp.zeros((8, D), F32)

        dh = (_dot(dc_ref[...], wm_ref[0:2 * CW, :]) + _dot(dg_ref[...], wm_ref[2 * CW:NMAIN, :])
              + _dot(db_ref[...], wb_ref[...]))
        xv = x_ref[...]
        r = lax.rsqrt(jnp.mean(xv * xv, axis=-1, keepdims=True) + EPS)
        xr = xv * r
        st_ref[0:1, :] += _colsum(dh)
        st_ref[1:2, :] += _colsum(dh * (xr * nw_ref[...]))
        dxn = dh * (1.0 + _mod(mod_ref, b_ref, 1))
        st_ref[2:3, :] += _colsum(dxn * xr)
        dxr = dxn * nw_ref[...]
        gx_ref[...] = dx1_ref[...] + r * (dxr - xr * jnp.mean(dxr * xr, axis=-1, keepdims=True))

    tile = lambda w: pl.BlockSpec((TI, w), lambda i: (i, 0))
    return pl.pallas_call(
        body, name="bwd_in", grid=(S // TI,),
        in_specs=[tile(2 * CW), tile(4 * GW), tile(LANES), tile(D), tile(D), _const((1, D)), _const((1, 6 * D)),
                  _const((1, 6 * D)), _const1((NMAIN, D)), _const((LANES, D))],
        out_specs=(tile(D), _const((8, D))),
        out_shape=(jax.ShapeDtypeStruct((S, D), F32), jax.ShapeDtypeStruct((8, D), F32)),
        compiler_params=_params(dimension_semantics=("arbitrary",)),
    )(dp_conf, dp_gdn, dp_ba, x, dx1, nw1, modnb, bada, w_main, w_ba)


def _adamw(w, g, m, v):
    m = ADAM_B1 * m + (1.0 - ADAM_B1) * g
    v = ADAM_B2 * v + (1.0 - ADAM_B2) * (g * g)
    m_hat = m / BC1
    v_hat = v / BC2
    delta = -ADAM_LR * (m_hat / (jnp.sqrt(v_hat) + ADAM_EPS) + ADAM_WD * w)
    return delta, m, v


ADAM_BLOCK_BYTES = 6 * 1024 * 1024


def _adam_tile(rows, cols):
    padded = -(-cols // LANES) * LANES
    if N_DEV * rows * padded * 4 <= ADAM_BLOCK_BYTES:
        return rows, cols
    best = None
    for tr in range(16, rows, 16):
        if rows % tr == 0 and N_DEV * tr * padded * 4 <= ADAM_BLOCK_BYTES:
            best = tr
    if best is not None:
        return best, cols
    rows_padded = -(-rows // 16) * 16
    tc = LANES
    for cand in range(LANES, cols, LANES):
        if cols % cand == 0 and N_DEV * rows_padded * cand * 4 <= ADAM_BLOCK_BYTES:
            tc = cand
    return rows, tc


def _reduce_adam(name, parts, w, m, v, own=None):
    rows, cols = w.shape
    tr, tc = _adam_tile(rows, cols)

    def body(*refs):
        p_ref, w_ref, m_ref, v_ref = refs[:4]
        g_ref, d_ref, nm_ref, nv_ref = refs[-4:]
        if own is None:
            part = lambda j: p_ref[j].astype(F32)
        else:
            me = 4 * lax.axis_index("x") + 2 * lax.axis_index("y") + lax.axis_index("c")
            part = lambda j: jnp.where(me == j, refs[4][...], p_ref[j]).astype(F32)
        g = part(0)
        for j in range(1, N_DEV):
            g = g + part(j)
        g_ref[...] = g
        d_ref[...], nm_ref[...], nv_ref[...] = _adamw(w_ref[...], g, m_ref[...], v_ref[...])

    blk = pl.BlockSpec((tr, tc), lambda i, j: (i, j))
    sds = jax.ShapeDtypeStruct((rows, cols), F32)
    extra = [] if own is None else [own]
    return pl.pallas_call(
        body, name=name, grid=(rows // tr, cols // tc),
        in_specs=[pl.BlockSpec((N_DEV, tr, tc), lambda i, j: (0, i, j)), blk, blk, blk] + [blk] * len(extra),
        out_specs=(blk, blk, blk, blk), out_shape=(sds, sds, sds, sds),
        compiler_params=_params(dimension_semantics=("arbitrary", "arbitrary")),
    )(parts, w, m, v, *extra)


def _ada_adam(c_all, dmod_sh, w, m, v):
    rows, cols = w.shape
    tr = 256

    def body(c_ref, dm_ref, w_ref, m_ref, v_ref, g_ref, d_ref, nm_ref, nv_ref):
        cv = c_ref[...]
        g = _dot_tn(cv * _sig(cv), dm_ref[...], HI)
        g_ref[...] = g
        d_ref[...], nm_ref[...], nv_ref[...] = _adamw(w_ref[...], g, m_ref[...], v_ref[...])

    blk = pl.BlockSpec((tr, cols), lambda i: (i, 0))
    sds = jax.ShapeDtypeStruct((rows, cols), F32)
    return pl.pallas_call(
        body, name="ada_adam", grid=(rows // tr,),
        in_specs=[pl.BlockSpec((N_DEV, tr), lambda i: (0, i)), _const((N_DEV, cols)), blk, blk, blk],
        out_specs=(blk, blk, blk, blk), out_shape=(sds, sds, sds, sds),
        compiler_params=_params(dimension_semantics=("arbitrary",)),
    )(c_all, dmod_sh, w, m, v)


def _lanes(a, at=0):
    return jnp.pad(a, ((0, 0), (at, LANES - at - a.shape[1])))


WEIGHT_NAMES = ["w_ada", "b_ada", "norm_mix_w", "w_in", "conv_w", "conv_b", "conv_gn_w", "conv_gn_b", "gdn_conv_w",
                "gdn_a_log", "gdn_dt_bias", "gdn_norm_w", "w_out", "norm_ffn_w", "w_ffn_in", "w_ffn_out",
                "norm_final_w"]


SMALL_LAYOUT = [("b_ada", 0, 48, LANES), ("norm_mix_w", 48, 8, LANES), ("norm_ffn_w", 56, 8, LANES),
                ("norm_final_w", 64, 8, LANES), ("conv_b", 72, 4, LANES), ("conv_gn_w", 76, 4, LANES),
                ("conv_gn_b", 80, 4, LANES), ("gdn_norm_w", 84, 1, LANES), ("gdn_a_log", 85, 1, NH),
                ("gdn_dt_bias", 86, 1, NH)]
LOSS_ROW = 87


def _adam_small(g_small, weights, m1, m2):
    names = [nm for nm, _, _, _ in SMALL_LAYOUT]
    k = len(names)

    def body(*refs):
        g_ref = refs[0]
        w_refs, m_refs, v_refs = refs[1:1 + k], refs[1 + k:1 + 2 * k], refs[1 + 2 * k:1 + 3 * k]
        loss_ref = refs[1 + 3 * k]
        outs = refs[2 + 3 * k:2 + 7 * k]
        total = refs[-1]
        g = g_ref[0]
        for j in range(1, N_DEV):
            g = g + g_ref[j]
        total[...] = g
        loss_ref[...] = total[LOSS_ROW:LOSS_ROW + 1, :]
        for i, (_, r0, rows, lanes) in enumerate(SMALL_LAYOUT):
            gp = total[r0:r0 + rows, 0:lanes]
            outs[i][...] = gp
            outs[k + i][...], outs[2 * k + i][...], outs[3 * k + i][...] = _adamw(
                w_refs[i][...], gp, m_refs[i][...], v_refs[i][...])

    shapes = [jax.ShapeDtypeStruct((rows, lanes), F32) for _, _, rows, lanes in SMALL_LAYOUT]
    res = pl.pallas_call(
        body, name="adam_small",
        out_shape=tuple([jax.ShapeDtypeStruct((1, LANES), F32)] + shapes * 4),
        scratch_shapes=[pltpu.VMEM((SMALL_ROWS, LANES), F32)],
        compiler_params=_params(),
    )(g_small, *[weights[n] for n in names], *[m1[n] for n in names], *[m2[n] for n in names])
    kinds = [dict(zip(names, res[1 + q * k:1 + (q + 1) * k])) for q in range(4)]
    return res[0], kinds


def _mix_forward(w, xs, modnb, between=None):
    w_main = w["w_in"]
    w_ba = jnp.pad(w["w_in"][NMAIN:], ((0, LANES - 2 * NH), (0, 0)))
    alog_l = _lanes(w["gdn_a_log"], NH)
    dt_l = _lanes(w["gdn_dt_bias"], NH)
    p_main, p_ba, hb1 = _fwd_in(xs, w["norm_mix_w"], modnb, w["b_ada"], w_main, w_ba)
    w_o, u_o, qg, kd, qk, cd, t_inv = _gdn_prep(p_main, p_ba, w["gdn_conv_w"], alog_l, dt_l)
    out_b, o_pre, s_in = _gdn_scan(w_o, u_o, qg, kd, qk, cd, p_main, w["gdn_norm_w"])
    conv_b = w["conv_b"] if between is None else _after(w["conv_b"], between(out_b))
    y_conv, out_a = _conf_fwd(p_main, w["conv_w"], conv_b, w["conv_gn_w"], w["conv_gn_b"])
    return dict(w_main=w_main, w_ba=w_ba, alog_l=alog_l, dt_l=dt_l, p_main=p_main, p_ba=p_ba, hb1=hb1,
                y_conv=y_conv, out_a=out_a, out_b=out_b, o_pre=o_pre, s_in=s_in, t_inv=t_inv)


def _ffn_stage(w, f, xs, tgt, modnb):
    x1, mix, oab = _fwd_out(f["out_a"], f["out_b"], xs, modnb, w["b_ada"], w["w_out"])
    hb2, act, pre, dx2, dffn, st_fwd = _ffn_forward(x1, tgt, modnb, w["b_ada"], w["norm_ffn_w"],
                                                    w["norm_final_w"], w["w_ffn_in"], w["w_ffn_out"])
    gw_ffn_out = _grad_w_ffn_out(act, dffn)
    df, dx1, st_bwd = _ffn_backward(dffn, pre, x1, dx2, modnb, w["b_ada"], w["norm_ffn_w"], w["w_ffn_in"],
                                    w["w_ffn_out"])
    gw_ffn_in = _grad_w_ffn_in(hb2, df)
    return dict(mix=mix, oab=oab, dx1=dx1, st_ffn=st_fwd + st_bwd, gw_ffn_in=gw_ffn_in, gw_ffn_out=gw_ffn_out)


def _out_backward(w, g, modnb):
    dmix, d_out_a, d_out_b, st_out = _bwd_out(g["dx1"], g["mix"], modnb, w["b_ada"], w["w_out"])
    return dict(d_out_a=d_out_a, d_out_b=d_out_b, st_out=st_out, gw_out=_grad_w("grad_w_out", g["oab"], dmix, 512))


def _heads_backward(w, f, a):
    dp_conf, st_conf = _conf_bwd(a["d_out_a"], f["y_conv"], f["p_main"], w["conv_w"], w["conv_gn_w"],
                                 w["conv_gn_b"])
    dp_gdn, dp_ba, st_gdn = _gdn_bwd(a["d_out_b"], f["o_pre"], f["s_in"], f["t_inv"], f["p_main"], f["p_ba"],
                                     w["gdn_conv_w"], f["alog_l"], f["dt_l"], w["gdn_norm_w"])
    gw_in = _grad_w_in(dp_conf, dp_gdn, dp_ba, f["hb1"])[:NIN]
    return dict(dp_conf=dp_conf, dp_gdn=dp_gdn, dp_ba=dp_ba, st_conf=st_conf, st_gdn=st_gdn, gw_in=gw_in,
                gw_conv=st_conf[0:KC], gw_gconv=st_gdn[0:KS])


def _in_backward(w, f, g, a, h, xs, modnb):
    st_out, st_conf, st_gdn, st_ffn = a["st_out"], h["st_conf"], h["st_gdn"], g["st_ffn"]
    grad_x, st_in = _bwd_in(h["dp_conf"], h["dp_gdn"], h["dp_ba"], xs, g["dx1"], w["norm_mix_w"], modnb,
                            w["b_ada"], f["w_main"], f["w_ba"])
    dmod = jnp.concatenate([st_in[0:1], st_in[1:2], st_out[0:1], st_ffn[2:3], st_ffn[3:4], st_ffn[1:2]], axis=1)
    small = jnp.concatenate([
        dmod.reshape(48, LANES), st_in[2:3].reshape(8, LANES), st_ffn[4:5].reshape(8, LANES),
        st_ffn[0:1].reshape(8, LANES), st_conf[31:32].reshape(4, LANES), st_conf[32:33].reshape(4, LANES),
        st_conf[33:34].reshape(4, LANES), st_gdn[4:5, 0:LANES],
        _lanes(st_gdn[5:6, NH:2 * NH]), _lanes(st_gdn[6:7, NH:2 * NH]), st_ffn[5:6, 0:LANES]], axis=0)
    return dict(grad_x=grad_x, small=small)


def _local(w, xs, tgt, modnb):
    f = _mix_forward(w, xs, modnb)
    g = _ffn_stage(w, f, xs, tgt, modnb)
    a = _out_backward(w, g, modnb)
    h = _heads_backward(w, f, a)
    b = _in_backward(w, f, g, a, h, xs, modnb)
    return dict(b, gw_in=h["gw_in"], gw_conv=h["gw_conv"], gw_gconv=h["gw_gconv"], gw_out=a["gw_out"],
                gw_ffn_in=g["gw_ffn_in"], gw_ffn_out=g["gw_ffn_out"])


def kernel(x, c, w_ada, b_ada, norm_mix_w, w_in, conv_w, conv_b, conv_gn_w, conv_gn_b, gdn_conv_w, gdn_a_log, gdn_dt_bias, gdn_norm_w, w_out, norm_ffn_w, w_ffn_in, w_ffn_out, norm_final_w, loss_target, m_w_ada, m_b_ada, m_norm_mix_w, m_w_in, m_conv_w, m_conv_b, m_conv_gn_w, m_conv_gn_b, m_gdn_conv_w, m_gdn_a_log, m_gdn_dt_bias, m_gdn_norm_w, m_w_out, m_norm_ffn_w, m_w_ffn_in, m_w_ffn_out, m_norm_final_w, v_w_ada, v_b_ada, v_norm_mix_w, v_w_in, v_conv_w, v_conv_b, v_conv_gn_w, v_conv_gn_b, v_gdn_conv_w, v_gdn_a_log, v_gdn_dt_bias, v_gdn_norm_w, v_w_out, v_norm_ffn_w, v_w_ffn_in, v_w_ffn_out, v_norm_final_w):
    me = 4 * lax.axis_index("x") + 2 * lax.axis_index("y") + lax.axis_index("c")
    xs = x.reshape(S, D)
    tgt = loss_target.reshape(S, D)

    late = [w_out[0].astype(BF16), jnp.transpose(w_ffn_in[0]).astype(BF16), w_ffn_out[0].astype(BF16)]
    g_c, g_cw, g_gcw, g_win, *late_lands = _gather_two_level(
        "gather_weights", [c, conv_w[0], gdn_conv_w[0], jnp.transpose(w_in[0]).astype(BF16)] + late,
        seed_only=(4, 5, 6))
    c_all = g_c.reshape(N_DEV, D)
    g_mod, mod_token = _exchange("gather_mod", [_mod_shard(c_all, w_ada[0])], [False], with_token=True)
    modnb = lax.dynamic_index_in_dim(g_mod, me, axis=1, keepdims=False).reshape(1, 6 * D)
    late_started = _exchange_start("gather_late_start", [_after(late[0], mod_token)] + late[1:], late_lands,
                                   [False] * 3, only=LEVEL_ONE)
    modnb = _after(modnb, late_started[-1])
    w = dict(b_ada=b_ada, norm_mix_w=norm_mix_w, conv_b=conv_b, conv_gn_w=conv_gn_w, conv_gn_b=conv_gn_b,
             gdn_a_log=gdn_a_log, gdn_dt_bias=gdn_dt_bias, gdn_norm_w=gdn_norm_w, norm_ffn_w=norm_ffn_w,
             norm_final_w=norm_final_w.reshape(1, D),
             conv_w=jnp.transpose(g_cw, (1, 0, 2)).reshape(KC, CW),
             gdn_conv_w=jnp.transpose(g_gcw, (1, 0, 2)).reshape(KS, 3 * GW),
             w_in=g_win.reshape(NIN, D))

    relay = {}

    def relay_late(out_b):
        _, late_landed = _exchange_wait("gather_late_wait", late_started, [False] * 3, (out_b,), only=LEVEL_ONE)
        relay["started"] = _relay_start("gather_late_relay_start", late_landed)
        return relay["started"][-1]

    f = _mix_forward(w, xs, modnb, relay_late)
    g_wout, g_wfi, g_wfo = _relay_wait("gather_late_relay_wait", relay["started"], (f["out_a"],))
    w.update(w_out=g_wout.reshape(D, D), w_ffn_in=g_wfi, w_ffn_out=g_wfo.reshape(4, FB, D))
    g = _ffn_stage(w, f, xs, tgt, modnb)

    ffn_grads = [g["gw_ffn_in"], g["gw_ffn_out"].reshape(N_DEV, DFF // N_DEV, D)]
    ffn_started = _exchange_start("scatter_ffn_start", ffn_grads,
                                  [lax.empty(a.shape, a.dtype) for a in ffn_grads], [True] * 2)
    a = _out_backward(w, g, _after(modnb, ffn_started[-1]))
    out_grads = [a["gw_out"].reshape(N_DEV, D // N_DEV, D)]
    out_started = _exchange_start("scatter_out_start", out_grads,
                                  [lax.empty(t.shape, t.dtype) for t in out_grads], [True])
    h = _heads_backward(dict(w, conv_gn_w=_after(w["conv_gn_w"], out_started[-1])), f, a)

    in_grads = [h["gw_in"].reshape(N_DEV, NIN // N_DEV, D),
                jnp.transpose(h["gw_conv"].reshape(KC, N_DEV, CW // N_DEV), (1, 0, 2)),
                jnp.transpose(h["gw_gconv"].reshape(KS, N_DEV, 3 * GW // N_DEV), (1, 0, 2))]
    in_started = _exchange_start("scatter_in_start", in_grads,
                                 [lax.empty(t.shape, t.dtype) for t in in_grads], [True] * 3)
    loc = _in_backward(w, f, g, a, h, xs, _after(modnb, in_started[-1]))
    small_started = _exchange_start("gather_small_start", [loc["small"]],
                                    [lax.empty((N_DEV, SMALL_ROWS, LANES), F32)], [False])

    def own(sent):
        return lax.dynamic_index_in_dim(sent, me, axis=0, keepdims=False)

    big = {}
    (sent_fi, sent_fo), (r_fi, r_fo) = _exchange_wait("scatter_ffn_wait", ffn_started, [True] * 2,
                                                         (small_started[-1],))
    big["w_ffn_in"] = [jnp.transpose(t) for t in _reduce_adam(
        "adam_w_ffn_in", r_fi, jnp.transpose(w_ffn_in[0]), jnp.transpose(m_w_ffn_in[0]),
        jnp.transpose(v_w_ffn_in[0]), own(sent_fi))]
    big["w_ffn_out"] = _reduce_adam("adam_w_ffn_out", r_fo, w_ffn_out[0], m_w_ffn_out[0], v_w_ffn_out[0],
                                    own(sent_fo))
    (sent_out,), (r_out,) = _exchange_wait("scatter_out_wait", out_started, [True], (big["w_ffn_out"][0],))
    big["w_out"] = _reduce_adam("adam_w_out", r_out, w_out[0], m_w_out[0], v_w_out[0], own(sent_out))

    (sent_small,), (r_small,) = _exchange_wait("gather_small_wait", small_started, [False], (big["w_out"][0],))
    slot = lax.broadcasted_iota(jnp.int32, (N_DEV, 1, 1), 0)
    g_small = jnp.where(slot == me, sent_small[None], r_small)
    def views(b_, nm_, nf_, nl_, cb_, gw_, gb_, gn_, al_, dt_):
        arrs = [b_, nm_, nf_, nl_, cb_, gw_, gb_, gn_, al_, dt_]
        return {nm: t.reshape(rows, lanes) for (nm, _, rows, lanes), t in zip(SMALL_LAYOUT, arrs)}

    loss_row, res = _adam_small(
        g_small,
        views(b_ada, norm_mix_w, norm_ffn_w, norm_final_w, conv_b, conv_gn_w, conv_gn_b, gdn_norm_w, gdn_a_log,
              gdn_dt_bias),
        views(m_b_ada, m_norm_mix_w, m_norm_ffn_w, m_norm_final_w, m_conv_b, m_conv_gn_w, m_conv_gn_b,
              m_gdn_norm_w, m_gdn_a_log, m_gdn_dt_bias),
        views(v_b_ada, v_norm_mix_w, v_norm_ffn_w, v_norm_final_w, v_conv_b, v_conv_gn_w, v_conv_gn_b,
              v_gdn_norm_w, v_gdn_a_log, v_gdn_dt_bias))
    loss = loss_row[0, 0]
    small_shapes = dict(b_ada=(1, 6 * D), norm_mix_w=(1, D), norm_ffn_w=(1, D), norm_final_w=(D,),
                        conv_b=(1, CW), conv_gn_w=(1, CW), conv_gn_b=(1, CW), gdn_norm_w=(1, DH),
                        gdn_a_log=(1, NH), gdn_dt_bias=(1, NH))
    res = [{nm: t.reshape(small_shapes[nm]) for nm, t in kind.items()} for kind in res]

    dmod_rows = g_small[:, 0:48, :].reshape(N_DEV, 6 * D)
    dmod_sh = lax.dynamic_slice_in_dim(dmod_rows, me * (6 * D // N_DEV), 6 * D // N_DEV, axis=1)

    big["w_ada"] = _ada_adam(c_all, dmod_sh, w_ada[0], m_w_ada[0], v_w_ada[0])
    (sent_in, sent_cw, sent_gcw), (r_in, r_cw, r_gcw) = _exchange_wait(
        "scatter_in_wait", in_started, [True] * 3, (big["w_ada"][0],))
    big["w_in"] = [jnp.transpose(t) for t in _reduce_adam(
        "adam_w_in", r_in, jnp.transpose(w_in[0]), jnp.transpose(m_w_in[0]), jnp.transpose(v_w_in[0]),
        own(sent_in))]
    big["conv_w"] = _reduce_adam("adam_conv_w", r_cw, conv_w[0], m_conv_w[0], v_conv_w[0], own(sent_cw))
    big["gdn_conv_w"] = _reduce_adam("adam_gdn_conv_w", r_gcw, gdn_conv_w[0], m_gdn_conv_w[0], v_gdn_conv_w[0],
                                     own(sent_gcw))
    outs = [loss, loc["grad_x"].reshape(1, S, D)]
    for kind in range(4):
        for nm in WEIGHT_NAMES:
            outs.append(big[nm][kind][None] if nm in big else res[kind][nm])
    return tuple(outs)
```

```python
import functools

import jax
import jax.numpy as jnp
from jax import lax
from jax.experimental import pallas as pl
from jax.experimental.pallas import tpu as pltpu

F32 = jnp.float32
BF16 = jnp.bfloat16
HI = lax.Precision.HIGHEST
MESH = pl.DeviceIdType.MESH

N_DEV = 8
S = 2048
D = 1024
TM = 256
NT = S // TM
CW = 512
KC = 31
NG = 8
GSZ = CW // NG
HALO = 32
GW = 512
NH = 4
DH = 128
KS = 4
SH = 8
CL = 64
NCH = S // CL
NMAIN = 2 * CW + 4 * GW
NIN = NMAIN + 2 * NH
DFF = 2816
FB = DFF // 4
EPS = 1e-6
QSCALE = DH ** -0.5
LANES = 128
SMALL_ROWS = 88

ADAM_LR = 0.001
ADAM_B1 = 0.9
ADAM_B2 = 0.999
ADAM_EPS = 1e-08
ADAM_WD = 0.01
ADAM_STEP = 10
BC1 = 1.0 - ADAM_B1 ** ADAM_STEP
BC2 = 1.0 - ADAM_B2 ** ADAM_STEP

MIB = 1024 * 1024
VMEM_LIMIT_MIB = 24


def _params(limit_mib=VMEM_LIMIT_MIB, **kw):
    return pltpu.CompilerParams(vmem_limit_bytes=limit_mib * MIB, **kw)


def _sig(x):
    return jax.nn.sigmoid(x)


GP = BF16


def _operands(a, b, prec):
    if prec is BF16:
        return a.astype(BF16), b.astype(BF16), None
    return a, b, prec


def _dot(a, b, prec=None):
    a, b, prec = _operands(a, b, prec)
    return jnp.dot(a, b, preferred_element_type=F32, precision=prec)


def _dot_nt(a, b, prec=None):
    a, b, prec = _operands(a, b, prec)
    return lax.dot_general(a, b, (((1,), (1,)), ((), ())), preferred_element_type=F32, precision=prec)


def _dot_tn(a, b, prec=None):
    a, b, prec = _operands(a, b, prec)
    return lax.dot_general(a, b, (((0,), (0,)), ((), ())), preferred_element_type=F32, precision=prec)


def _lockstep(gens):
    gens = list(gens)
    while gens:
        alive = []
        for g in gens:
            try:
                next(g)
                alive.append(g)
            except StopIteration:
                pass
        gens = alive


def _rowsum(x):
    return jnp.sum(x, axis=-1, keepdims=True)


def _colsum(x):
    return jnp.sum(x, axis=0, keepdims=True)


def _mod(mod_ref, b_ref, k):
    return mod_ref[:, k * D:(k + 1) * D] + b_ref[:, k * D:(k + 1) * D]


def _const(shape):
    nd = len(shape)
    return pl.BlockSpec(shape, lambda *_: (0,) * nd)


def _const1(shape):
    nd = len(shape)
    return pl.BlockSpec(shape, lambda *_: (0,) * nd, pipeline_mode=pl.Buffered(1))


PEER_FLIPS = [(dx, dy, dc) for dx in (0, 1) for dy in (0, 1) for dc in (0, 1)][1:]


def _after(x, token):
    return x + token[0:1, 0:1].astype(x.dtype).reshape((1,) * x.ndim)


def _exchange(name, srcs, per_dest, seed_only=(), with_token=False):
    n = len(srcs)
    out_shape = []
    for a, pd in zip(srcs, per_dest):
        blk = a.shape[1:] if pd else a.shape
        out_shape.append(jax.ShapeDtypeStruct((N_DEV,) + tuple(blk), a.dtype))

    def body(*refs):
        src = refs[:n]
        dst = refs[n:2 * n]
        send_sems, recv_sems, local_sems = refs[-3:]
        if with_token:
            refs[2 * n][...] = jnp.zeros((8, LANES), F32)
        x, y, c = lax.axis_index("x"), lax.axis_index("y"), lax.axis_index("c")
        me = 4 * x + 2 * y + c

        def piece(i, j):
            return src[i].at[j] if per_dest[i] else src[i]

        copies = []
        for k, (dx, dy, dc) in enumerate(PEER_FLIPS):
            px = 1 - x if dx else x
            py = 1 - y if dy else y
            pc = 1 - c if dc else c
            pj = 4 * px + 2 * py + pc
            for i in range(n):
                if i in seed_only:
                    continue
                cp = pltpu.make_async_remote_copy(
                    src_ref=piece(i, pj), dst_ref=dst[i].at[me],
                    send_sem=send_sems.at[k * n + i], recv_sem=recv_sems.at[k * n + i],
                    device_id=(px, py, pc), device_id_type=MESH)
                cp.start()
                arrive = pltpu.make_async_remote_copy(
                    src_ref=piece(i, pj), dst_ref=dst[i].at[pj],
                    send_sem=send_sems.at[k * n + i], recv_sem=recv_sems.at[k * n + i],
                    device_id=(px, py, pc), device_id_type=MESH)
                copies.append((cp, arrive))
        own = []
        for i in range(n):
            lc = pltpu.make_async_copy(piece(i, me), dst[i].at[me], local_sems.at[i])
            lc.start()
            own.append(lc)
        for cp, arrive in copies:
            arrive.wait_recv()
        for cp, arrive in copies:
            cp.wait_send()
        for lc in own:
            lc.wait()

    any_spec = pl.BlockSpec(memory_space=pl.ANY)
    out_specs = [any_spec] * n
    if with_token:
        out_shape.append(jax.ShapeDtypeStruct((8, LANES), F32))
        out_specs.append(pl.BlockSpec(memory_space=pltpu.VMEM))
    return pl.pallas_call(
        body, name=name, out_shape=tuple(out_shape),
        in_specs=[any_spec] * n, out_specs=tuple(out_specs),
        scratch_shapes=[pltpu.SemaphoreType.DMA((7 * n,)), pltpu.SemaphoreType.DMA((7 * n,)),
                        pltpu.SemaphoreType.DMA((n,))],
        compiler_params=pltpu.CompilerParams(has_side_effects=True),
    )(*srcs)


CHIP_FLIPS = [(0, 1), (1, 0), (1, 1)]
LEVEL_ONE = [k for k, (dx, dy, dc) in enumerate(PEER_FLIPS) if (dx, dy, dc) == (0, 0, 1) or dc == 0]


def _chip_peers(x, y):
    return [(1 - x if dx else x, 1 - y if dy else y) for dx, dy in CHIP_FLIPS]


def _gather_two_level(name, srcs, seed_only=()):
    n = len(srcs)
    live = [i for i in range(n) if i not in seed_only]

    def body(*refs):
        src, dst = refs[:n], refs[n:2 * n]
        send_sems, recv_sems, local_sems = refs[2 * n:2 * n + 3]
        bounce = refs[2 * n + 3:]
        x, y, c = lax.axis_index("x"), lax.axis_index("y"), lax.axis_index("c")
        me = 4 * x + 2 * y + c
        sibling = (x, y, 1 - c)
        chips = _chip_peers(x, y)

        def copy(k, i, src_ref, slot, to):
            return pltpu.make_async_remote_copy(
                src_ref=src_ref, dst_ref=dst[i].at[slot], send_sem=send_sems.at[k * n + i],
                recv_sem=recv_sems.at[k * n + i], device_id=to, device_id_type=MESH)

        first = []
        for i in live:
            first.append(copy(0, i, src[i], me, sibling))
            first += [copy(1 + j, i, src[i], me, (px, py, c)) for j, (px, py) in enumerate(chips)]
        for cp in first:
            cp.start()
        up = [pltpu.make_async_copy(src[i], bounce[i], local_sems.at[i]) for i in range(n)]
        for cp in up:
            cp.start()
        for cp in up:
            cp.wait()
        own = [pltpu.make_async_copy(bounce[i], dst[i].at[me], local_sems.at[i]) for i in range(n)]
        for cp in own:
            cp.start()
        passed = []
        for j, (px, py) in enumerate(chips):
            slot = 4 * px + 2 * py + c
            for i in live:
                copy(1 + j, i, src[i], slot, (px, py, c)).wait_recv()
                fwd = copy(4 + j, i, dst[i].at[slot], slot, sibling)
                fwd.start()
                passed.append(fwd)
        for i in live:
            copy(0, i, src[i], 4 * x + 2 * y + 1 - c, sibling).wait_recv()
            for j, (px, py) in enumerate(chips):
                copy(4 + j, i, src[i], 4 * px + 2 * py + 1 - c, sibling).wait_recv()
        for cp in first + passed:
            cp.wait_send()
        for cp in own:
            cp.wait()

    any_spec = pl.BlockSpec(memory_space=pl.ANY)
    return pl.pallas_call(
        body, name=name, out_shape=tuple(jax.ShapeDtypeStruct((N_DEV,) + a.shape, a.dtype) for a in srcs),
        in_specs=[any_spec] * n, out_specs=tuple([any_spec] * n),
        scratch_shapes=[pltpu.SemaphoreType.DMA((7 * n,)), pltpu.SemaphoreType.DMA((7 * n,)),
                        pltpu.SemaphoreType.DMA((n,))] + [pltpu.VMEM(a.shape, a.dtype) for a in srcs],
        compiler_params=pltpu.CompilerParams(has_side_effects=True),
    )(*srcs)


def _relay_copy(land, sems, i, n, j, slot, sibling):
    send_sems, recv_sems = sems
    return pltpu.make_async_remote_copy(
        src_ref=land[i].at[slot], dst_ref=land[i].at[slot], send_sem=send_sems.at[j * n + i],
        recv_sem=recv_sems.at[j * n + i], device_id=sibling, device_id_type=MESH)


def _relay_start(name, lands):
    n = len(lands)

    def body(*refs):
        land = refs[:n]
        sems = refs[n], refs[n + 1]
        x, y, c = lax.axis_index("x"), lax.axis_index("y"), lax.axis_index("c")
        for j, (px, py) in enumerate(_chip_peers(x, y)):
            for i in range(n):
                _relay_copy(land, sems, i, n, j, 4 * px + 2 * py + c, (x, y, 1 - c)).start()
        refs[-1][...] = jnp.zeros((8, LANES), F32)

    return pl.pallas_call(
        body, name=name,
        out_shape=(pltpu.SemaphoreType.DMA((3 * n,)), pltpu.SemaphoreType.DMA((3 * n,)),
                   *[pltpu.HBM(a.shape, a.dtype) for a in lands], jax.ShapeDtypeStruct((8, LANES), F32)),
        in_specs=[HBM_SPEC] * n,
        out_specs=(SEM_SPEC, SEM_SPEC, *[HBM_SPEC] * n, pl.BlockSpec(memory_space=pltpu.VMEM)),
        input_output_aliases={i: 2 + i for i in range(n)},
        compiler_params=pltpu.CompilerParams(has_side_effects=DATAFLOW),
    )(*[pltpu.with_memory_space_constraint(a, pltpu.HBM) for a in lands])


def _relay_wait(name, started, after):
    n = len(started) - 3
    arrays = list(started[2:2 + n])

    def body(*refs):
        land = refs[:n]
        sems = refs[n], refs[n + 1]
        x, y, c = lax.axis_index("x"), lax.axis_index("y"), lax.axis_index("c")
        for j, (px, py) in enumerate(_chip_peers(x, y)):
            for i in range(n):
                _relay_copy(land, sems, i, n, j, 4 * px + 2 * py + c, (x, y, 1 - c)).wait_send()
                _relay_copy(land, sems, i, n, j, 4 * px + 2 * py + 1 - c, (x, y, 1 - c)).wait_recv()

    return pl.pallas_call(
        body, name=name,
        out_shape=tuple(pltpu.HBM(a.shape, a.dtype) for a in arrays),
        in_specs=[HBM_SPEC] * n + [SEM_SPEC, SEM_SPEC] + [pl.BlockSpec(memory_space=pl.ANY)] * len(after),
        out_specs=tuple([HBM_SPEC] * n),
        input_output_aliases={i: i for i in range(n)},
        compiler_params=pltpu.CompilerParams(has_side_effects=DATAFLOW),
    )(*arrays, started[0], started[1], *after)


HBM_SPEC = pl.BlockSpec(memory_space=pltpu.HBM)
SEM_SPEC = pl.BlockSpec(memory_space=pltpu.SEMAPHORE)
DATAFLOW = pltpu.SideEffectType.DATAFLOW_SIDE_EFFECTING


def _peers(only=None):
    x, y, c = lax.axis_index("x"), lax.axis_index("y"), lax.axis_index("c")
    out = []
    for k, (dx, dy, dc) in enumerate(PEER_FLIPS):
        if only is not None and k not in only:
            continue
        px = 1 - x if dx else x
        py = 1 - y if dy else y
        pc = 1 - c if dc else c
        out.append((k, (px, py, pc), 4 * px + 2 * py + pc))
    return 4 * x + 2 * y + c, out


def _exchange_start(name, srcs, lands, per_dest, only=None):
    n = len(srcs)

    def body(*refs):
        src, land = refs[:n], refs[n:2 * n]
        send_sems, recv_sems = refs[2 * n], refs[2 * n + 1]
        token = refs[-1]
        me, peers = _peers(only)
        for k, peer, pj in peers:
            for i in range(n):
                pltpu.make_async_remote_copy(
                    src_ref=src[i].at[pj] if per_dest[i] else src[i], dst_ref=land[i].at[me],
                    send_sem=send_sems.at[k * n + i], recv_sem=recv_sems.at[k * n + i],
                    device_id=peer, device_id_type=MESH).start()
        token[...] = jnp.zeros((8, LANES), F32)

    arrays = list(srcs) + list(lands)
    return pl.pallas_call(
        body, name=name,
        out_shape=(pltpu.SemaphoreType.DMA((7 * n,)), pltpu.SemaphoreType.DMA((7 * n,)),
                   *[pltpu.HBM(a.shape, a.dtype) for a in arrays], jax.ShapeDtypeStruct((8, LANES), F32)),
        in_specs=[HBM_SPEC] * (2 * n),
        out_specs=(SEM_SPEC, SEM_SPEC, *[HBM_SPEC] * (2 * n), pl.BlockSpec(memory_space=pltpu.VMEM)),
        input_output_aliases={i: 2 + i for i in range(2 * n)},
        compiler_params=pltpu.CompilerParams(has_side_effects=DATAFLOW),
    )(*[pltpu.with_memory_space_constraint(a, pltpu.HBM) for a in arrays])


def _exchange_wait(name, started, per_dest, after, only=None):
    n = (len(started) - 3) // 2
    send_sems, recv_sems = started[0], started[1]
    arrays = list(started[2:2 + 2 * n])

    def body(*refs):
        src, land = refs[:n], refs[n:2 * n]
        send, recv = refs[2 * n], refs[2 * n + 1]
        me, peers = _peers(only)
        for k, peer, pj in peers:
            for i in range(n):
                cp = pltpu.make_async_remote_copy(
                    src_ref=src[i].at[pj] if per_dest[i] else src[i], dst_ref=land[i].at[pj],
                    send_sem=send.at[k * n + i], recv_sem=recv.at[k * n + i],
                    device_id=peer, device_id_type=MESH)
                cp.wait_send()
                cp.wait_recv()

    outs = pl.pallas_call(
        body, name=name,
        out_shape=tuple(pltpu.HBM(a.shape, a.dtype) for a in arrays),
        in_specs=[HBM_SPEC] * (2 * n) + [SEM_SPEC, SEM_SPEC] + [pl.BlockSpec(memory_space=pl.ANY)] * len(after),
        out_specs=tuple([HBM_SPEC] * (2 * n)),
        input_output_aliases={i: i for i in range(2 * n)},
        compiler_params=pltpu.CompilerParams(has_side_effects=DATAFLOW),
    )(*arrays, send_sems, recv_sems, *after)
    return outs[:n], outs[n:]


def _mod_shard(c_all, w_ada):
    def body(c_ref, w_ref, o_ref):
        cv = c_ref[...]
        ca = cv * _sig(cv)
        o_ref[...] = _dot(ca.astype(BF16), w_ref[...].astype(BF16))

    return pl.pallas_call(
        body, name="mod_shard", out_shape=jax.ShapeDtypeStruct((N_DEV, w_ada.shape[1]), F32),
        compiler_params=_params(),
    )(c_all, w_ada)


TI = 512


def _fwd_in(x, nw1, modnb, bada, w_main, w_ba):
    def body(x_ref, nw_ref, mod_ref, b_ref, wm_ref, wb_ref, pm_ref, pb_ref, hb_ref):
        xv = x_ref[...]
        r = lax.rsqrt(jnp.mean(xv * xv, axis=-1, keepdims=True) + EPS)
        h = (xv * r * nw_ref[...]) * (1.0 + _mod(mod_ref, b_ref, 1)) + _mod(mod_ref, b_ref, 0)
        hb = h.astype(BF16)
        hb_ref[...] = hb
        pm_ref[...] = _dot_nt(hb, wm_ref[...])
        pb_ref[...] = _dot_nt(hb, wb_ref[...])

    return pl.pallas_call(
        body, name="fwd_in", grid=(S // TI,),
        in_specs=[pl.BlockSpec((TI, D), lambda i: (i, 0)), _const((1, D)), _const((1, 6 * D)), _const((1, 6 * D)),
                  _const1((NMAIN, D)), _const((LANES, D))],
        out_specs=(pl.BlockSpec((TI, NMAIN), lambda i: (i, 0)), pl.BlockSpec((TI, LANES), lambda i: (i, 0)),
                   pl.BlockSpec((TI, D), lambda i: (i, 0))),
        out_shape=(jax.ShapeDtypeStruct((S, NMAIN), F32), jax.ShapeDtypeStruct((S, LANES), F32),
                   jax.ShapeDtypeStruct((S, D), BF16)),
        compiler_params=_params(32, dimension_semantics=("arbitrary",)),
    )(x, nw1, modnb, bada, w_main, w_ba)


def _group_mean_matrix():
    ii = lax.broadcasted_iota(jnp.int32, (CW, CW), 0) // GSZ
    jj = lax.broadcasted_iota(jnp.int32, (CW, CW), 1) // GSZ
    return jnp.where(ii == jj, 1.0 / GSZ, 0.0).astype(F32)


SUB = 8
SHIFT_ROWS = HALO + TM - SUB


def _fill_shifted(buf, sh):
    for b in range(1, SUB):
        sh[b - 1] = buf[b:b + SHIFT_ROWS, :]


def _rows_at(buf, sh, off):
    a, b = divmod(off, SUB)
    if b == 0:
        return buf[off:off + TM, :]
    return sh[b - 1, SUB * a:SUB * a + TM, :]


def _group_mean(x, pm):
    hi = x.astype(BF16)
    r1 = x - hi.astype(F32)
    mid = r1.astype(BF16)
    lo = (r1 - mid.astype(F32)).astype(BF16)
    return _dot(hi, pm) + _dot(mid, pm) + _dot(lo, pm)


def _conf_fwd(p_main, conv_w, conv_b, gn_w, gn_b):
    def body(a_ref, g_ref, w_ref, b_ref, gw_ref, gb_ref, y_ref, oa_ref, ubuf, ush):
        i = pl.program_id(0)

        @pl.when(i == 0)
        def _():
            ubuf[0:HALO, :] = jnp.zeros((HALO, CW), F32)

        ubuf[HALO:HALO + TM, :] = a_ref[...] * _sig(g_ref[...])
        _fill_shifted(ubuf, ush)
        acc = jnp.zeros((TM, CW), F32) + b_ref[...]
        for k in range(KC):
            acc = acc + w_ref[k:k + 1, :] * _rows_at(ubuf, ush, HALO - (KC - 1) + k)
        y_ref[...] = acc
        ubuf[0:HALO, :] = ubuf[TM:TM + HALO, :]
        pm = _group_mean_matrix().astype(BF16)
        dlt = acc - _group_mean(acc, pm)
        var = _group_mean(dlt * dlt, pm)
        o = dlt * lax.rsqrt(var + EPS) * gw_ref[...] + gb_ref[...]
        oa_ref[...] = o * _sig(o)

    return pl.pallas_call(
        body, name="conf_fwd", grid=(NT,),
        in_specs=[pl.BlockSpec((TM, CW), lambda i: (i, 0)), pl.BlockSpec((TM, CW), lambda i: (i, 1)),
                  _const((KC, CW)), _const((1, CW)), _const((1, CW)), _const((1, CW))],
        out_specs=(pl.BlockSpec((TM, CW), lambda i: (i, 0)), pl.BlockSpec((TM, CW), lambda i: (i, 0))),
        out_shape=(jax.ShapeDtypeStruct((S, CW), F32), jax.ShapeDtypeStruct((S, CW), F32)),
        scratch_shapes=[pltpu.VMEM((HALO + TM, CW), F32), pltpu.VMEM((SUB - 1, SHIFT_ROWS, CW), F32)],
        compiler_params=_params(dimension_semantics=("arbitrary",)),
    )(p_main, p_main, conv_w, conv_b, gn_w, gn_b)


def _tri_iota():
    ii = lax.broadcasted_iota(jnp.int32, (CL, CL), 0)
    jj = lax.broadcasted_iota(jnp.int32, (CL, CL), 1)
    return ii, jj


def _gdn_gates(ba, alog_l, dt_l):
    beta_all = _sig(ba)
    xg = ba + dt_l
    sp = jnp.maximum(xg, 0.0) + jnp.log(1.0 + jnp.exp(-jnp.abs(xg)))
    neg_a = -jnp.exp(alog_l)
    return beta_all, neg_a * sp, xg, neg_a


def _ones_dot(ones, x):
    hi = x.astype(BF16)
    r1 = x - hi.astype(F32)
    mid = r1.astype(BF16)
    lo = (r1 - mid.astype(F32)).astype(BF16)
    return _dot(ones, hi) + _dot(ones, mid) + _dot(ones, lo)


def _gdn_cumsum(g_all):
    ii, jj = _tri_iota()
    low = jnp.where(ii >= jj, 1.0, 0.0).astype(BF16)
    gcum = _ones_dot(low, g_all)
    return gcum, jnp.transpose(gcum)


def _split(x):
    hi = x.astype(BF16)
    return hi, (x - hi.astype(F32)).astype(BF16)


def _dot_split(a, b):
    (ah, al), (bh, bl) = a, b
    return _dot(ah, bh) + (_dot(ah, bl) + _dot(al, bh))


def _unit_lower_inverses(mats):
    ii, jj = _tri_iota()
    eye = jnp.where(ii == jj, 1.0, 0.0).astype(F32)
    ts = [eye - a for a in mats]
    ps = [_dot_split(s, s) for s in map(_split, mats)]
    for _ in range(4):
        sp = [_split(p) for p in ps]
        ts = [t + _dot_split(_split(t), s) for t, s in zip(ts, sp)]
        ps = [_dot_split(s, s) for s in sp]
    return [t + _dot_split(_split(t), _split(p)) for t, p in zip(ts, ps)]


def _head_terms(qh, kh, beta, gcol, grow):
    ii, jj = _tri_iota()
    causal = ii >= jj
    strict = ii > jj
    rq = lax.rsqrt(_rowsum(qh * qh) + EPS)
    rk = lax.rsqrt(_rowsum(kh * kh) + EPS)
    qn = qh * rq
    kn = kh * rk
    qs = qn * QSCALE
    decay = jnp.where(causal, jnp.exp(jnp.where(causal, gcol - grow, 0.0)), 0.0)
    gam = jnp.exp(gcol)
    gl = gcol[CL - 1:CL, :]
    kds = jnp.exp(gl - gcol)
    cd = jnp.exp(gl)
    kb = kn * beta
    a = jnp.where(strict, _dot_nt(kb, kn, GP) * decay, 0.0)
    qk = jnp.where(causal, _dot_nt(qs, kn, GP) * decay, 0.0)
    return dict(rq=rq, rk=rk, qn=qn, kn=kn, qs=qs, decay=decay, gam=gam, kds=kds, cd=cd, kb=kb, a=a, qk=qk,
                causal=causal, strict=strict)


def _short_conv(w_ref, buf, rows=CL):
    acc = w_ref[0:1, :] * buf[SH - KS + 1:SH - KS + 1 + rows, :]
    for k in range(1, KS):
        off = SH - (KS - 1) + k
        acc = acc + w_ref[k:k + 1, :] * buf[off:off + rows, :]
    return acc


CPS = 4
TG = CPS * CL


def _gdn_prep(p_main, p_ba, gdn_conv_w, alog_l, dt_l):
    def body(q_ref, k_ref, v_ref, qh_ref, kh_ref, vh_ref, ba_ref, w_ref, al_ref, dt_ref,
             wo_ref, uo_ref, qg_ref, kd_ref, qk_ref, cd_ref, t_ref, xbuf):
        i = pl.program_id(0)
        first = i == 0
        xbuf[0:SH, 0:GW] = jnp.where(first, 0.0, qh_ref[...])
        xbuf[0:SH, GW:2 * GW] = jnp.where(first, 0.0, kh_ref[...])
        xbuf[0:SH, 2 * GW:3 * GW] = jnp.where(first, 0.0, vh_ref[...])
        xbuf[SH:SH + TG, 0:GW] = q_ref[...]
        xbuf[SH:SH + TG, GW:2 * GW] = k_ref[...]
        xbuf[SH:SH + TG, 2 * GW:3 * GW] = v_ref[...]
        conv = _short_conv(w_ref, xbuf, TG)
        qkv = conv * _sig(conv)
        beta_all, g_all, _, _ = _gdn_gates(ba_ref[...], al_ref[...], dt_ref[...])
        lane = lax.broadcasted_iota(jnp.int32, (8, LANES), 1)
        cums = [_gdn_cumsum(g_all[cc * CL:(cc + 1) * CL, :]) for cc in range(CPS)]
        pairs = [(cc, h) for cc in range(CPS) for h in range(NH)]
        terms, vbs = [], []
        for cc, h in pairs:
            r0, lo = cc * CL, h * DH
            beta = beta_all[r0:r0 + CL, h:h + 1]
            gcum, gcum_t = cums[cc]
            terms.append(_head_terms(qkv[r0:r0 + CL, lo:lo + DH], qkv[r0:r0 + CL, GW + lo:GW + lo + DH], beta,
                                     gcum[:, NH + h:NH + h + 1], gcum_t[NH + h:NH + h + 1, :]))
            vbs.append(qkv[r0:r0 + CL, 2 * GW + lo:2 * GW + lo + DH] * beta)
        invs = _unit_lower_inverses([f["a"] for f in terms])
        cds = [jnp.zeros((8, LANES), F32) for _ in range(CPS)]
        for (cc, h), f, t, vb in zip(pairs, terms, invs, vbs):
            r0, lo = cc * CL, h * DH
            t_ref[cc, h] = t
            uo_ref[r0:r0 + CL, lo:lo + DH] = _dot(t, vb, GP)
            wo_ref[r0:r0 + CL, lo:lo + DH] = _dot(t, f["kb"] * f["gam"], GP).astype(BF16)
            qg_ref[r0:r0 + CL, lo:lo + DH] = (f["qs"] * f["gam"]).astype(BF16)
            kd_ref[r0:r0 + CL, lo:lo + DH] = (f["kn"] * f["kds"]).astype(BF16)
            qk_ref[cc, h] = f["qk"].astype(BF16)
            cds[cc] = cds[cc] + jnp.where(lane == h, f["cd"], 0.0)
        for cc in range(CPS):
            cd_ref[cc] = cds[cc]

    col = lambda j: pl.BlockSpec((TG, GW), lambda i: (i, j))
    halo = lambda j: pl.BlockSpec((SH, GW), lambda i: (jnp.maximum(i * (TG // SH) - 1, 0), j))
    tile = lambda: pl.BlockSpec((TG, GW), lambda i: (i, 0))
    sq = lambda: pl.BlockSpec((CPS, NH, CL, CL), lambda i: (i, 0, 0, 0))
    return pl.pallas_call(
        body, name="gdn_prep", grid=(NCH // CPS,),
        in_specs=[col(2), col(3), col(4), halo(2), halo(3), halo(4), pl.BlockSpec((TG, LANES), lambda i: (i, 0)),
                  _const((KS, 3 * GW)), _const((1, LANES)), _const((1, LANES))],
        out_specs=(tile(), tile(), tile(), tile(), sq(), pl.BlockSpec((CPS, 8, LANES), lambda i: (i, 0, 0)), sq()),
        out_shape=(jax.ShapeDtypeStruct((S, GW), BF16), jax.ShapeDtypeStruct((S, GW), F32),
                   jax.ShapeDtypeStruct((S, GW), BF16), jax.ShapeDtypeStruct((S, GW), BF16),
                   jax.ShapeDtypeStruct((NCH, NH, CL, CL), BF16), jax.ShapeDtypeStruct((NCH, 8, LANES), F32),
                   jax.ShapeDtypeStruct((NCH, NH, CL, CL), F32)),
        scratch_shapes=[pltpu.VMEM((SH + TG, 3 * GW), F32)],
        compiler_params=_params(dimension_semantics=("arbitrary",)),
    )(p_main, p_main, p_main, p_main, p_main, p_main, p_ba, gdn_conv_w, alog_l, dt_l)


def _gdn_scan(w_o, u_o, qg, kd, qk, cd, p_main, gdn_nw):
    def body(w_ref, u_ref, qg_ref, kd_ref, qk_ref, cd_ref, z_ref, nw_ref, ob_ref, o_ref, sin_ref, state):
        n = pl.program_id(0)

        @pl.when(n == 0)
        def _():
            state[...] = jnp.zeros((NH, DH, DH), F32)

        def head(cc, h):
            rows, lo = pl.ds(cc * CL, CL), h * DH
            st = state[h]
            sin_ref[cc, h] = st
            sb = st.astype(BF16)
            v_new = u_ref[rows, lo:lo + DH] - _dot(w_ref[rows, lo:lo + DH], sb)
            yield
            vb = v_new.astype(BF16)
            o = _dot(qg_ref[rows, lo:lo + DH], sb) + _dot(qk_ref[cc, h], vb)
            state[h] = st * cd_ref[cc, 0:1, h:h + 1] + _dot_tn(kd_ref[rows, lo:lo + DH], vb)
            yield
            o_ref[rows, lo:lo + DH] = o
            r = lax.rsqrt(jnp.mean(o * o, axis=-1, keepdims=True) + EPS)
            zh = z_ref[rows, lo:lo + DH]
            ob_ref[rows, lo:lo + DH] = o * r * nw_ref[...] * (zh * _sig(zh))

        for cc in range(CPS):
            _lockstep(head(cc, h) for h in range(NH))

    tile = lambda: pl.BlockSpec((TG, GW), lambda n: (n, 0))
    return pl.pallas_call(
        body, name="gdn_scan", grid=(NCH // CPS,),
        in_specs=[tile(), tile(), tile(), tile(), pl.BlockSpec((CPS, NH, CL, CL), lambda n: (n, 0, 0, 0)),
                  pl.BlockSpec((CPS, 8, LANES), lambda n: (n, 0, 0)), pl.BlockSpec((TG, GW), lambda n: (n, 5)),
                  _const((1, DH))],
        out_specs=(tile(), tile(), pl.BlockSpec((CPS, NH, DH, DH), lambda n: (n, 0, 0, 0))),
        out_shape=(jax.ShapeDtypeStruct((S, GW), F32), jax.ShapeDtypeStruct((S, GW), F32),
                   jax.ShapeDtypeStruct((NCH, NH, DH, DH), F32)),
        scratch_shapes=[pltpu.VMEM((NH, DH, DH), F32)],
        compiler_params=_params(dimension_semantics=("arbitrary",)),
    )(w_o, u_o, qg, kd, qk, cd, p_main, gdn_nw)


def _fwd_out(out_a, out_b, x, modnb, bada, w_out):
    def body(oa_ref, ob_ref, x_ref, mod_ref, b_ref, w_ref, x1_ref, mix_ref, oab_ref):
        oa = oa_ref[...].astype(BF16)
        ob = ob_ref[...].astype(BF16)
        oab_ref[:, 0:CW] = oa
        oab_ref[:, CW:D] = ob
        mix = _dot(oa, w_ref[0:CW, :]) + _dot(ob, w_ref[CW:D, :])
        mix_ref[...] = mix
        x1_ref[...] = x_ref[...] + _mod(mod_ref, b_ref, 2) * mix

    tile = lambda w: pl.BlockSpec((TM, w), lambda i: (i, 0))
    return pl.pallas_call(
        body, name="fwd_out", grid=(NT,),
        in_specs=[tile(CW), tile(GW), tile(D), _const((1, 6 * D)), _const((1, 6 * D)), _const((D, D))],
        out_specs=(tile(D), tile(D), tile(D)),
        out_shape=(jax.ShapeDtypeStruct((S, D), F32), jax.ShapeDtypeStruct((S, D), F32),
                   jax.ShapeDtypeStruct((S, D), BF16)),
        compiler_params=_params(dimension_semantics=("arbitrary",)),
    )(out_a, out_b, x, modnb, bada, w_out)


FFN_STATS = 8


def _ffn_forward(x1, tgt, modnb, bada, nw2, nfw, w_fi, w_fo):
    def body(x1_ref, tgt_ref, mod_ref, b_ref, nw2_ref, nfw_ref, wi_ref, wo_ref,
             hb_ref, act_ref, pre_ref, dx2_ref, dffn_ref, st_ref):
        i = pl.program_id(0)

        @pl.when(i == 0)
        def _():
            st_ref[...] = jnp.zeros((FFN_STATS, D), F32)

        sh2, sc2, gt2 = _mod(mod_ref, b_ref, 3), _mod(mod_ref, b_ref, 4), _mod(mod_ref, b_ref, 5)
        x1v = x1_ref[...]
        r2 = lax.rsqrt(jnp.mean(x1v * x1v, axis=-1, keepdims=True) + EPS)
        hb = ((x1v * r2 * nw2_ref[...]) * (1.0 + sc2) + sh2).astype(BF16)
        hb_ref[...] = hb
        ffn = jnp.zeros((TM, D), F32)
        for j in range(4):
            fgj = _dot_nt(hb, wi_ref[j])
            fuj = _dot_nt(hb, wi_ref[j + 4])
            pre_ref[j] = fgj.astype(BF16)
            pre_ref[j + 4] = fuj.astype(BF16)
            aj = (fgj * _sig(fgj) * fuj).astype(BF16)
            act_ref[j] = aj
            ffn = ffn + _dot(aj, wo_ref[j])
        x2 = x1v + gt2 * ffn
        r3 = lax.rsqrt(jnp.mean(x2 * x2, axis=-1, keepdims=True) + EPS)
        xr3 = x2 * r3
        err = xr3 * nfw_ref[...] - tgt_ref[...]
        loss = 0.5 * jnp.sum(jnp.mean(err * err, axis=-1, keepdims=True), axis=0, keepdims=True)
        dy = err * (1.0 / D)
        st_ref[0:1, :] += _colsum(dy * xr3)
        dyr = dy * nfw_ref[...]
        dx2 = r3 * (dyr - xr3 * jnp.mean(dyr * xr3, axis=-1, keepdims=True))
        st_ref[1:2, :] += _colsum(dx2 * ffn)
        st_ref[5:6, :] += jnp.broadcast_to(loss, (1, D))
        dx2_ref[...] = dx2
        dffn_ref[...] = (gt2 * dx2).astype(BF16)

    tile = lambda w: pl.BlockSpec((TM, w), lambda i: (i, 0))
    return pl.pallas_call(
        body, name="ffn_forward", grid=(NT,),
        in_specs=[tile(D), tile(D), _const((1, 6 * D)), _const((1, 6 * D)), _const((1, D)), _const((1, D)),
                  _const1((N_DEV, FB, D)), _const1((4, FB, D))],
        out_specs=(tile(D), pl.BlockSpec((4, TM, FB), lambda i: (0, i, 0)),
                   pl.BlockSpec((N_DEV, TM, FB), lambda i: (0, i, 0)), tile(D), tile(D), _const((FFN_STATS, D))),
        out_shape=(jax.ShapeDtypeStruct((S, D), BF16), jax.ShapeDtypeStruct((4, S, FB), BF16),
                   jax.ShapeDtypeStruct((N_DEV, S, FB), BF16), jax.ShapeDtypeStruct((S, D), F32),
                   jax.ShapeDtypeStruct((S, D), BF16), jax.ShapeDtypeStruct((FFN_STATS, D), F32)),
        compiler_params=_params(42, dimension_semantics=("arbitrary",)),
    )(x1, tgt, modnb, bada, nw2, nfw, w_fi, w_fo)


def _ffn_backward(dffn, pre, x1, dx2, modnb, bada, nw2, w_fi, w_fo):
    def body(dffn_ref, pre_ref, x1_ref, dx2_ref, mod_ref, b_ref, nw2_ref, wi_ref, wo_ref, df_ref, dx1_ref, st_ref):
        i = pl.program_id(0)

        @pl.when(i == 0)
        def _():
            st_ref[...] = jnp.zeros((FFN_STATS, D), F32)

        dffn = dffn_ref[...]
        dh = jnp.zeros((TM, D), F32)
        for j in range(4):
            fg = pre_ref[j].astype(F32)
            fu = pre_ref[j + 4].astype(F32)
            sg = _sig(fg)
            dact = _dot_nt(dffn, wo_ref[j])
            dfg = (dact * fu * (sg * (1.0 + fg * (1.0 - sg)))).astype(BF16)
            dfu = (dact * (fg * sg)).astype(BF16)
            df_ref[j] = dfg
            df_ref[j + 4] = dfu
            dh = dh + _dot(dfg, wi_ref[j]) + _dot(dfu, wi_ref[j + 4])
        x1v = x1_ref[...]
        r2 = lax.rsqrt(jnp.mean(x1v * x1v, axis=-1, keepdims=True) + EPS)
        xr2 = x1v * r2
        st_ref[2:3, :] += _colsum(dh)
        st_ref[3:4, :] += _colsum(dh * (xr2 * nw2_ref[...]))
        dxn = dh * (1.0 + _mod(mod_ref, b_ref, 4))
        st_ref[4:5, :] += _colsum(dxn * xr2)
        dxr = dxn * nw2_ref[...]
        dx1_ref[...] = dx2_ref[...] + r2 * (dxr - xr2 * jnp.mean(dxr * xr2, axis=-1, keepdims=True))

    tile = lambda w: pl.BlockSpec((TM, w), lambda i: (i, 0))
    wide = lambda: pl.BlockSpec((N_DEV, TM, FB), lambda i: (0, i, 0))
    return pl.pallas_call(
        body, name="ffn_backward", grid=(NT,),
        in_specs=[tile(D), wide(), tile(D), tile(D), _const((1, 6 * D)), _const((1, 6 * D)), _const((1, D)),
                  _const1((N_DEV, FB, D)), _const1((4, FB, D))],
        out_specs=(wide(), tile(D), _const((FFN_STATS, D))),
        out_shape=(jax.ShapeDtypeStruct((N_DEV, S, FB), BF16), jax.ShapeDtypeStruct((S, D), F32),
                   jax.ShapeDtypeStruct((FFN_STATS, D), F32)),
        compiler_params=_params(44, dimension_semantics=("arbitrary",)),
    )(dffn, pre, x1, dx2, modnb, bada, nw2, w_fi, w_fo)


def _grad_w(name, a, b, nb):
    m, n = a.shape[1], b.shape[1]

    def body(a_ref, b_ref, o_ref):
        o_ref[...] = _dot_tn(a_ref[...], b_ref[...]).astype(BF16)

    return pl.pallas_call(
        body, name=name, grid=(m // nb,),
        in_specs=[pl.BlockSpec((S, nb), lambda j: (0, j)), _const((S, n))],
        out_specs=pl.BlockSpec((nb, n), lambda j: (j, 0)),
        out_shape=jax.ShapeDtypeStruct((m, n), BF16),
        compiler_params=_params(dimension_semantics=("arbitrary",)),
    )(a, b)


GW_IN_ROWS = NMAIN + LANES


def _grad_w_in(dp_conf, dp_gdn, dp_ba, hb1):
    nb = 512
    n_conf, n_gdn = 2 * CW // nb, 4 * GW // nb

    def body(c_ref, g_ref, ba_ref, h_ref, o_ref):
        j = pl.program_id(0)

        @pl.when(j < n_conf)
        def _():
            o_ref[...] = _dot_tn(c_ref[...], h_ref[...]).astype(BF16)

        @pl.when((j >= n_conf) & (j < n_conf + n_gdn))
        def _():
            o_ref[...] = _dot_tn(g_ref[...], h_ref[...]).astype(BF16)

        @pl.when(j == n_conf + n_gdn)
        def _():
            o_ref[0:LANES, :] = _dot_tn(ba_ref[...], h_ref[...]).astype(BF16)

    return pl.pallas_call(
        body, name="grad_w_in", grid=(n_conf + n_gdn + 1,),
        in_specs=[pl.BlockSpec((S, nb), lambda j: (0, jnp.minimum(j, n_conf - 1))),
                  pl.BlockSpec((S, nb), lambda j: (0, jnp.clip(j - n_conf, 0, n_gdn - 1))),
                  _const((S, LANES)), _const((S, D))],
        out_specs=pl.BlockSpec((nb, D), lambda j: (j, 0)),
        out_shape=jax.ShapeDtypeStruct((GW_IN_ROWS, D), BF16),
        compiler_params=_params(dimension_semantics=("arbitrary",)),
    )(dp_conf, dp_gdn, dp_ba, hb1)


def _grad_w_ffn_in(hb2, df):
    def body(a_ref, b_ref, o_ref):
        o_ref[0] = _dot_tn(b_ref[0], a_ref[...]).astype(BF16)

    return pl.pallas_call(
        body, name="grad_w_ffn_in", grid=(N_DEV,),
        in_specs=[_const((S, D)), pl.BlockSpec((1, S, FB), lambda j: (j, 0, 0))],
        out_specs=pl.BlockSpec((1, FB, D), lambda j: (j, 0, 0)),
        out_shape=jax.ShapeDtypeStruct((N_DEV, FB, D), BF16),
        compiler_params=_params(dimension_semantics=("arbitrary",)),
    )(hb2, df)


def _grad_w_ffn_out(act, dffn):
    def body(a_ref, b_ref, o_ref):
        o_ref[0] = _dot_tn(a_ref[0], b_ref[...]).astype(BF16)

    return pl.pallas_call(
        body, name="grad_w_ffn_out", grid=(4,),
        in_specs=[pl.BlockSpec((1, S, FB), lambda j: (j, 0, 0)), _const((S, D))],
        out_specs=pl.BlockSpec((1, FB, D), lambda j: (j, 0, 0)),
        out_shape=jax.ShapeDtypeStruct((4, FB, D), BF16),
        compiler_params=_params(dimension_semantics=("arbitrary",)),
    )(act, dffn)


def _bwd_out(dx1, mix, modnb, bada, w_out):
    def body(dx_ref, mix_ref, mod_ref, b_ref, w_ref, dmix_ref, doa_ref, dob_ref, st_ref):
        i = pl.program_id(0)

        @pl.when(i == 0)
        def _():
            st_ref[...] = jnp.zeros((8, D), F32)

        dx = dx_ref[...]
        st_ref[0:1, :] += _colsum(dx * mix_ref[...])
        dmix = (_mod(mod_ref, b_ref, 2) * dx).astype(BF16)
        dmix_ref[...] = dmix
        doa_ref[...] = _dot_nt(dmix, w_ref[0:CW, :])
        dob_ref[...] = _dot_nt(dmix, w_ref[CW:D, :])

    tile = lambda w: pl.BlockSpec((TM, w), lambda i: (i, 0))
    return pl.pallas_call(
        body, name="bwd_out", grid=(NT,),
        in_specs=[tile(D), tile(D), _const((1, 6 * D)), _const((1, 6 * D)), _const((D, D))],
        out_specs=(tile(D), tile(CW), tile(GW), _const((8, D))),
        out_shape=(jax.ShapeDtypeStruct((S, D), BF16), jax.ShapeDtypeStruct((S, CW), F32),
                   jax.ShapeDtypeStruct((S, GW), F32), jax.ShapeDtypeStruct((8, D), F32)),
        compiler_params=_params(dimension_semantics=("arbitrary",)),
    )(dx1, mix, modnb, bada, w_out)


CONF_STATS = 40


def _conf_bwd(d_out_a, y, p_main, conv_w, gn_w, gn_b):
    def body(do_ref, y_ref, a_ref, g_ref, ah_ref, gh_ref, w_ref, gw_ref, gb_ref, dp_ref, st_ref,
             ubuf, dybuf, ush, dysh):
        i = pl.program_id(0)

        @pl.when(i == 0)
        def _():
            st_ref[...] = jnp.zeros((CONF_STATS, CW), F32)
            dybuf[TM:TM + HALO, :] = jnp.zeros((HALO, CW), F32)

        pm = _group_mean_matrix().astype(BF16)
        yv = y_ref[...]
        dlt = yv - _group_mean(yv, pm)
        rstd = lax.rsqrt(_group_mean(dlt * dlt, pm) + EPS)
        un = dlt * rstd
        o = un * gw_ref[...] + gb_ref[...]
        so = _sig(o)
        d_o = do_ref[...] * (so * (1.0 + o * (1.0 - so)))
        st_ref[33:34, :] += _colsum(d_o)
        st_ref[32:33, :] += _colsum(d_o * un)
        dun = d_o * gw_ref[...]
        dy = rstd * (dun - _group_mean(dun, pm) - un * _group_mean(dun * un, pm))
        st_ref[31:32, :] += _colsum(dy)
        dybuf[0:TM, :] = dy
        _fill_shifted(dybuf, dysh)

        a = a_ref[...]
        sg = _sig(g_ref[...])
        first = i == NT - 1
        ubuf[0:HALO, :] = jnp.where(first, 0.0, ah_ref[...] * _sig(gh_ref[...]))
        ubuf[HALO:HALO + TM, :] = a * sg
        _fill_shifted(ubuf, ush)
        du = jnp.zeros((TM, CW), F32)
        for k in range(KC):
            st_ref[k:k + 1, :] += _colsum(dy * _rows_at(ubuf, ush, HALO - (KC - 1) + k))
            du = du + w_ref[k:k + 1, :] * _rows_at(dybuf, dysh, KC - 1 - k)
        dybuf[TM:TM + HALO, :] = dybuf[0:HALO, :]
        dp_ref[:, 0:CW] = (du * sg).astype(BF16)
        dp_ref[:, CW:2 * CW] = (du * a * sg * (1.0 - sg)).astype(BF16)

    rev = lambda w, j=0: pl.BlockSpec((TM, w), lambda i: (NT - 1 - i, j))
    halo = lambda j: pl.BlockSpec((HALO, CW), lambda i: (jnp.maximum((NT - 1 - i) * (TM // HALO) - 1, 0), j))
    return pl.pallas_call(
        body, name="conf_bwd", grid=(NT,),
        in_specs=[rev(CW), rev(CW), rev(CW, 0), rev(CW, 1), halo(0), halo(1),
                  _const((KC, CW)), _const((1, CW)), _const((1, CW))],
        out_specs=(rev(2 * CW), _const((CONF_STATS, CW))),
        out_shape=(jax.ShapeDtypeStruct((S, 2 * CW), BF16), jax.ShapeDtypeStruct((CONF_STATS, CW), F32)),
        scratch_shapes=[pltpu.VMEM((HALO + TM, CW), F32), pltpu.VMEM((TM + HALO, CW), F32),
                        pltpu.VMEM((SUB - 1, SHIFT_ROWS, CW), F32), pltpu.VMEM((SUB - 1, SHIFT_ROWS, CW), F32)],
        compiler_params=_params(dimension_semantics=("arbitrary",)),
    )(d_out_a, y, p_main, p_main, p_main, p_main, conv_w, gn_w, gn_b)


GDN_STATS = 8


def _gdn_bwd(d_out_b, o_pre, s_in, t_inv, p_main, p_ba, gdn_conv_w, alog_l, dt_l, gdn_nw):
    def body(dob_ref, o_ref, sin_ref, t_ref, q_ref, k_ref, v_ref, z_ref, qh_ref, kh_ref, vh_ref, ba_ref,
             w_ref, al_ref, dt_ref, nw_ref, dp_ref, dba_ref, st_ref, xbuf, dcbuf, dstate):
        n = pl.program_id(0)

        @pl.when(n == 0)
        def _():
            st_ref[...] = jnp.zeros((GDN_STATS, 3 * GW), F32)
            dcbuf[CL:CL + SH, :] = jnp.zeros((SH, 3 * GW), F32)
            dstate[...] = jnp.zeros((NH, DH, DH), F32)

        for cc in reversed(range(CPS)):
            chunk(n, cc, dob_ref, o_ref, sin_ref, t_ref, q_ref, k_ref, v_ref, z_ref, qh_ref, kh_ref, vh_ref, ba_ref,
                  w_ref, al_ref, dt_ref, nw_ref, dp_ref, dba_ref, st_ref, xbuf, dcbuf, dstate)

    def chunk(n, cc, dob_ref, o_ref, sin_ref, t_ref, q_ref, k_ref, v_ref, z_ref, qh_ref, kh_ref, vh_ref, ba_ref,
              w_ref, al_ref, dt_ref, nw_ref, dp_ref, dba_ref, st_ref, xbuf, dcbuf, dstate):
        r0 = cc * CL
        if cc == 0:
            first = n == NCH // CPS - 1
            xbuf[0:SH, 0:GW] = jnp.where(first, 0.0, qh_ref[...])
            xbuf[0:SH, GW:2 * GW] = jnp.where(first, 0.0, kh_ref[...])
            xbuf[0:SH, 2 * GW:3 * GW] = jnp.where(first, 0.0, vh_ref[...])
        else:
            xbuf[0:SH, 0:GW] = q_ref[r0 - SH:r0, :]
            xbuf[0:SH, GW:2 * GW] = k_ref[r0 - SH:r0, :]
            xbuf[0:SH, 2 * GW:3 * GW] = v_ref[r0 - SH:r0, :]
        xbuf[SH:SH + CL, 0:GW] = q_ref[r0:r0 + CL, :]
        xbuf[SH:SH + CL, GW:2 * GW] = k_ref[r0:r0 + CL, :]
        xbuf[SH:SH + CL, 2 * GW:3 * GW] = v_ref[r0:r0 + CL, :]
        conv = _short_conv(w_ref, xbuf)
        sc = _sig(conv)
        qkv = conv * sc
        ba = ba_ref[r0:r0 + CL, :]
        beta_all, g_all, xg, neg_a = _gdn_gates(ba, al_ref[...], dt_ref[...])
        gcum, gcum_t = _gdn_cumsum(g_all)
        lane = lax.broadcasted_iota(jnp.int32, (CL, LANES), 1)
        row = lax.broadcasted_iota(jnp.int32, (CL, 1), 0)
        acc = dict(dgcum=jnp.zeros((CL, LANES), F32), dbeta=jnp.zeros((CL, LANES), F32))

        def head(h):
            lo = h * DH
            qh = qkv[:, lo:lo + DH]
            kh = qkv[:, GW + lo:GW + lo + DH]
            vh = qkv[:, 2 * GW + lo:2 * GW + lo + DH]
            beta = beta_all[:, h:h + 1]
            f = _head_terms(qh, kh, beta, gcum[:, NH + h:NH + h + 1], gcum_t[NH + h:NH + h + 1, :])
            qn, kn, qs, kb, gam, kds, cd, decay = (f[s] for s in ("qn", "kn", "qs", "kb", "gam", "kds", "cd", "decay"))
            t = t_ref[cc, h]
            st = sin_ref[cc, h]
            vb = vh * beta
            kbg = kb * gam
            u = _dot(t, vb, GP)
            w = _dot(t, kbg, GP)
            yield
            v_new = u - _dot(w, st, GP)
            q_dec = qs * gam
            k_dec = kn * kds

            o = o_ref[r0:r0 + CL, lo:lo + DH]
            zh = z_ref[r0:r0 + CL, lo:lo + DH]
            sz = _sig(zh)
            r = lax.rsqrt(jnp.mean(o * o, axis=-1, keepdims=True) + EPS)
            orr = o * r
            d_out = dob_ref[r0:r0 + CL, lo:lo + DH]
            dz = d_out * (orr * nw_ref[...]) * (sz * (1.0 + zh * (1.0 - sz)))
            don = d_out * (zh * sz)
            st_ref[4:5, 0:DH] += _colsum(don * orr)
            tt = don * nw_ref[...]
            d_o = r * (tt - orr * jnp.mean(tt * orr, axis=-1, keepdims=True))

            yield
            ds_out = dstate[h]
            dv_new = _dot_tn(f["qk"], d_o, GP) + _dot(k_dec, ds_out, GP)
            dqk = jnp.where(f["causal"], _dot_nt(d_o, v_new, GP), 0.0)
            dq_dec = _dot_nt(d_o, st, GP)
            dk_dec = _dot_nt(v_new, ds_out, GP)
            yield
            dstate[h] = _dot_tn(q_dec, d_o, GP) + cd * ds_out - _dot_tn(w, dv_new, GP)
            dcd = jnp.sum(_rowsum(st * ds_out), axis=0, keepdims=True)
            dw = -_dot_nt(dv_new, st, GP)
            dvb = _dot_tn(t, dv_new, GP)
            yield
            dt_m = _dot_nt(dv_new, vb, GP) + _dot_nt(dw, kbg, GP)
            dkbg = _dot_tn(t, dw, GP)
            yield
            dtt = _dot_nt(dt_m, t, GP)
            yield
            da = jnp.where(f["strict"], -_dot_tn(t, dtt, GP), 0.0)
            yield
            dad = da * decay
            dqkd = dqk * decay
            dkb = _dot(dad, kn, GP) + dkbg * gam
            dkn = _dot_tn(dad, kb, GP) + _dot_tn(dqkd, qs, GP) + dk_dec * kds + dkb * beta
            dqs = _dot(dqkd, kn, GP) + dq_dec * gam
            yield
            m = da * f["a"] + dqk * f["qk"]
            tk = _rowsum(dk_dec * k_dec)
            dgl = jnp.sum(tk, axis=0, keepdims=True) + dcd * cd
            dgc = (_rowsum(m) - _rowsum(jnp.transpose(m)) + _rowsum(dq_dec * q_dec) - tk + _rowsum(dkbg * kbg)
                   + jnp.where(row == CL - 1, dgl, 0.0))
            dbeta = _rowsum(dkb * kn) + _rowsum(dvb * vh)
            acc["dgcum"] = acc["dgcum"] + jnp.where(lane == NH + h, dgc, 0.0)
            acc["dbeta"] = acc["dbeta"] + jnp.where(lane == h, dbeta, 0.0)
            dvh = dvb * beta
            dqn = dqs * QSCALE
            dqh = f["rq"] * (dqn - qn * _rowsum(dqn * qn))
            dkh = f["rk"] * (dkn - kn * _rowsum(dkn * kn))
            dsilu = lambda c0: sc[:, c0:c0 + DH] * (1.0 + conv[:, c0:c0 + DH] * (1.0 - sc[:, c0:c0 + DH]))
            dcbuf[0:CL, lo:lo + DH] = dqh * dsilu(lo)
            dcbuf[0:CL, GW + lo:GW + lo + DH] = dkh * dsilu(GW + lo)
            dcbuf[0:CL, 2 * GW + lo:2 * GW + lo + DH] = dvh * dsilu(2 * GW + lo)
            dp_ref[r0:r0 + CL, 3 * GW + lo:3 * GW + lo + DH] = dz.astype(BF16)

        _lockstep(head(h) for h in range(NH))
        dgcum_all, dbeta_all = acc["dgcum"], acc["dbeta"]

        ii, jj = _tri_iota()
        upper = jnp.where(ii <= jj, 1.0, 0.0).astype(BF16)
        dg_all = _ones_dot(upper, dgcum_all)
        dxg = dg_all * neg_a * _sig(xg)
        st_ref[5:6, 0:LANES] += _colsum(dg_all * g_all)
        st_ref[6:7, 0:LANES] += _colsum(dxg)
        dbl = dbeta_all * beta_all * (1.0 - beta_all)
        dba_ref[r0:r0 + CL, :] = jnp.where(lane < NH, dbl, jnp.where(lane < 2 * NH, dxg, 0.0)).astype(BF16)

        dconv = dcbuf[0:CL, :]
        dx = w_ref[0:1, :] * dcbuf[KS - 1:KS - 1 + CL, :]
        st_ref[0:1, :] += _colsum(dconv * xbuf[SH - KS + 1:SH - KS + 1 + CL, :])
        for k in range(1, KS):
            off = SH - (KS - 1) + k
            st_ref[k:k + 1, :] += _colsum(dconv * xbuf[off:off + CL, :])
            dx = dx + w_ref[k:k + 1, :] * dcbuf[KS - 1 - k:KS - 1 - k + CL, :]
        dcbuf[CL:CL + SH, :] = dcbuf[0:SH, :]
        dp_ref[r0:r0 + CL, 0:3 * GW] = dx.astype(BF16)

    steps = NCH // CPS
    rev = lambda w, j=0: pl.BlockSpec((TG, w), lambda n: (steps - 1 - n, j))
    halo = lambda j: pl.BlockSpec((SH, GW), lambda n: (jnp.maximum((steps - 1 - n) * (TG // SH) - 1, 0), j))
    blk4 = lambda a, b: pl.BlockSpec((CPS, NH, a, b), lambda n: (steps - 1 - n, 0, 0, 0))
    return pl.pallas_call(
        body, name="gdn_bwd", grid=(steps,),
        in_specs=[rev(GW), rev(GW), blk4(DH, DH), blk4(CL, CL), rev(GW, 2), rev(GW, 3), rev(GW, 4), rev(GW, 5),
                  halo(2), halo(3), halo(4), rev(LANES), _const((KS, 3 * GW)), _const((1, LANES)),
                  _const((1, LANES)), _const((1, DH))],
        out_specs=(rev(4 * GW), rev(LANES), _const((GDN_STATS, 3 * GW))),
        out_shape=(jax.ShapeDtypeStruct((S, 4 * GW), BF16), jax.ShapeDtypeStruct((S, LANES), BF16),
                   jax.ShapeDtypeStruct((GDN_STATS, 3 * GW), F32)),
        scratch_shapes=[pltpu.VMEM((SH + CL, 3 * GW), F32), pltpu.VMEM((CL + SH, 3 * GW), F32),
                        pltpu.VMEM((NH, DH, DH), F32)],
        compiler_params=_params(dimension_semantics=("arbitrary",)),
    )(d_out_b, o_pre, s_in, t_inv, p_main, p_main, p_main, p_main, p_main, p_main, p_main, p_ba,
      gdn_conv_w, alog_l, dt_l, gdn_nw)


def _bwd_in(dp_conf, dp_gdn, dp_ba, x, dx1, nw1, modnb, bada, w_main, w_ba):
    def body(dc_ref, dg_ref, db_ref, x_ref, dx1_ref, nw_ref, mod_ref, b_ref, wm_ref, wb_ref, gx_ref, st_ref):
        i = pl.program_id(0)

        @pl.when(i == 0)
        def _():
            st_ref[...] = jnp.zeros((8, D), F32)

        dh = (_dot(dc_ref[...], wm_ref[0:2 * CW, :]) + _dot(dg_ref[...], wm_ref[2 * CW:NMAIN, :])
              + _dot(db_ref[...], wb_ref[...]))
        xv = x_ref[...]
        r = lax.rsqrt(jnp.mean(xv * xv, axis=-1, keepdims=True) + EPS)
        xr = xv * r
        st_ref[0:1, :] += _colsum(dh)
        st_ref[1:2, :] += _colsum(dh * (xr * nw_ref[...]))
        dxn = dh * (1.0 + _mod(mod_ref, b_ref, 1))
        st_ref[2:3, :] += _colsum(dxn * xr)
        dxr = dxn * nw_ref[...]
        gx_ref[...] = dx1_ref[...] + r * (dxr - xr * jnp.mean(dxr * xr, axis=-1, keepdims=True))

    tile = lambda w: pl.BlockSpec((TI, w), lambda i: (i, 0))
    return pl.pallas_call(
        body, name="bwd_in", grid=(S // TI,),
        in_specs=[tile(2 * CW), tile(4 * GW), tile(LANES), tile(D), tile(D), _const((1, D)), _const((1, 6 * D)),
                  _const((1, 6 * D)), _const1((NMAIN, D)), _const((LANES, D))],
        out_specs=(tile(D), _const((8, D))),
        out_shape=(jax.ShapeDtypeStruct((S, D), F32), jax.ShapeDtypeStruct((8, D), F32)),
        compiler_params=_params(32, dimension_semantics=("arbitrary",)),
    )(dp_conf, dp_gdn, dp_ba, x, dx1, nw1, modnb, bada, w_main, w_ba)


def _adamw(w, g, m, v):
    m = ADAM_B1 * m + (1.0 - ADAM_B1) * g
    v = ADAM_B2 * v + (1.0 - ADAM_B2) * (g * g)
    m_hat = m / BC1
    v_hat = v / BC2
    delta = -ADAM_LR * (m_hat / (jnp.sqrt(v_hat) + ADAM_EPS) + ADAM_WD * w)
    return delta, m, v


ADAM_BLOCK_BYTES = 6 * 1024 * 1024


def _adam_tile(rows, cols):
    padded = -(-cols // LANES) * LANES
    if N_DEV * rows * padded * 4 <= ADAM_BLOCK_BYTES:
        return rows, cols
    best = None
    for tr in range(16, rows, 16):
        if rows % tr == 0 and N_DEV * tr * padded * 4 <= ADAM_BLOCK_BYTES:
            best = tr
    if best is not None:
        return best, cols
    rows_padded = -(-rows // 16) * 16
    tc = LANES
    for cand in range(LANES, cols, LANES):
        if cols % cand == 0 and N_DEV * rows_padded * cand * 4 <= ADAM_BLOCK_BYTES:
            tc = cand
    return rows, tc


def _reduce_adam(name, parts, w, m, v, own=None):
    rows, cols = w.shape
    tr, tc = _adam_tile(rows, cols)

    def body(*refs):
        p_ref, w_ref, m_ref, v_ref = refs[:4]
        g_ref, d_ref, nm_ref, nv_ref = refs[-4:]
        if own is None:
            part = lambda j: p_ref[j].astype(F32)
        else:
            me = 4 * lax.axis_index("x") + 2 * lax.axis_index("y") + lax.axis_index("c")
            part = lambda j: jnp.where(me == j, refs[4][...], p_ref[j]).astype(F32)
        g = part(0)
        for j in range(1, N_DEV):
            g = g + part(j)
        g_ref[...] = g
        d_ref[...], nm_ref[...], nv_ref[...] = _adamw(w_ref[...], g, m_ref[...], v_ref[...])

    blk = pl.BlockSpec((tr, tc), lambda i, j: (i, j))
    sds = jax.ShapeDtypeStruct((rows, cols), F32)
    extra = [] if own is None else [own]
    return pl.pallas_call(
        body, name=name, grid=(rows // tr, cols // tc),
        in_specs=[pl.BlockSpec((N_DEV, tr, tc), lambda i, j: (0, i, j)), blk, blk, blk] + [blk] * len(extra),
        out_specs=(blk, blk, blk, blk), out_shape=(sds, sds, sds, sds),
        compiler_params=_params(dimension_semantics=("arbitrary", "arbitrary")),
    )(parts, w, m, v, *extra)


def _ada_adam(c_all, dmod_sh, w, m, v):
    rows, cols = w.shape
    tr = 256

    def body(c_ref, dm_ref, w_ref, m_ref, v_ref, g_ref, d_ref, nm_ref, nv_ref):
        cv = c_ref[...]
        g = _dot_tn(cv * _sig(cv), dm_ref[...], HI)
        g_ref[...] = g
        d_ref[...], nm_ref[...], nv_ref[...] = _adamw(w_ref[...], g, m_ref[...], v_ref[...])

    blk = pl.BlockSpec((tr, cols), lambda i: (i, 0))
    sds = jax.ShapeDtypeStruct((rows, cols), F32)
    return pl.pallas_call(
        body, name="ada_adam", grid=(rows // tr,),
        in_specs=[pl.BlockSpec((N_DEV, tr), lambda i: (0, i)), _const((N_DEV, cols)), blk, blk, blk],
        out_specs=(blk, blk, blk, blk), out_shape=(sds, sds, sds, sds),
        compiler_params=_params(dimension_semantics=("arbitrary",)),
    )(c_all, dmod_sh, w, m, v)


def _lanes(a, at=0):
    return jnp.pad(a, ((0, 0), (at, LANES - at - a.shape[1])))


WEIGHT_NAMES = ["w_ada", "b_ada", "norm_mix_w", "w_in", "conv_w", "conv_b", "conv_gn_w", "conv_gn_b", "gdn_conv_w",
                "gdn_a_log", "gdn_dt_bias", "gdn_norm_w", "w_out", "norm_ffn_w", "w_ffn_in", "w_ffn_out",
                "norm_final_w"]


SMALL_LAYOUT = [("b_ada", 0, 48, LANES), ("norm_mix_w", 48, 8, LANES), ("norm_ffn_w", 56, 8, LANES),
                ("norm_final_w", 64, 8, LANES), ("conv_b", 72, 4, LANES), ("conv_gn_w", 76, 4, LANES),
                ("conv_gn_b", 80, 4, LANES), ("gdn_norm_w", 84, 1, LANES), ("gdn_a_log", 85, 1, NH),
                ("gdn_dt_bias", 86, 1, NH)]
LOSS_ROW = 87


def _adam_small(g_small, weights, m1, m2):
    names = [nm for nm, _, _, _ in SMALL_LAYOUT]
    k = len(names)

    def body(*refs):
        g_ref = refs[0]
        w_refs, m_refs, v_refs = refs[1:1 + k], refs[1 + k:1 + 2 * k], refs[1 + 2 * k:1 + 3 * k]
        loss_ref = refs[1 + 3 * k]
        outs = refs[2 + 3 * k:2 + 7 * k]
        total = refs[-1]
        g = g_ref[0]
        for j in range(1, N_DEV):
            g = g + g_ref[j]
        total[...] = g
        loss_ref[...] = total[LOSS_ROW:LOSS_ROW + 1, :]
        for i, (_, r0, rows, lanes) in enumerate(SMALL_LAYOUT):
            gp = total[r0:r0 + rows, 0:lanes]
            outs[i][...] = gp
            outs[k + i][...], outs[2 * k + i][...], outs[3 * k + i][...] = _adamw(
                w_refs[i][...], gp, m_refs[i][...], v_refs[i][...])

    shapes = [jax.ShapeDtypeStruct((rows, lanes), F32) for _, _, rows, lanes in SMALL_LAYOUT]
    res = pl.pallas_call(
        body, name="adam_small",
        out_shape=tuple([jax.ShapeDtypeStruct((1, LANES), F32)] + shapes * 4),
        scratch_shapes=[pltpu.VMEM((SMALL_ROWS, LANES), F32)],
        compiler_params=_params(),
    )(g_small, *[weights[n] for n in names], *[m1[n] for n in names], *[m2[n] for n in names])
    kinds = [dict(zip(names, res[1 + q * k:1 + (q + 1) * k])) for q in range(4)]
    return res[0], kinds


def _mix_forward(w, xs, modnb, between=None):
    w_main = w["w_in"]
    w_ba = jnp.pad(w["w_in"][NMAIN:], ((0, LANES - 2 * NH), (0, 0)))
    alog_l = _lanes(w["gdn_a_log"], NH)
    dt_l = _lanes(w["gdn_dt_bias"], NH)
    p_main, p_ba, hb1 = _fwd_in(xs, w["norm_mix_w"], modnb, w["b_ada"], w_main, w_ba)
    w_o, u_o, qg, kd, qk, cd, t_inv = _gdn_prep(p_main, p_ba, w["gdn_conv_w"], alog_l, dt_l)
    out_b, o_pre, s_in = _gdn_scan(w_o, u_o, qg, kd, qk, cd, p_main, w["gdn_norm_w"])
    conv_b = w["conv_b"] if between is None else _after(w["conv_b"], between(out_b))
    y_conv, out_a = _conf_fwd(p_main, w["conv_w"], conv_b, w["conv_gn_w"], w["conv_gn_b"])
    return dict(w_main=w_main, w_ba=w_ba, alog_l=alog_l, dt_l=dt_l, p_main=p_main, p_ba=p_ba, hb1=hb1,
                y_conv=y_conv, out_a=out_a, out_b=out_b, o_pre=o_pre, s_in=s_in, t_inv=t_inv)


def _ffn_stage(w, f, xs, tgt, modnb):
    x1, mix, oab = _fwd_out(f["out_a"], f["out_b"], xs, modnb, w["b_ada"], w["w_out"])
    hb2, act, pre, dx2, dffn, st_fwd = _ffn_forward(x1, tgt, modnb, w["b_ada"], w["norm_ffn_w"],
                                                    w["norm_final_w"], w["w_ffn_in"], w["w_ffn_out"])
    gw_ffn_out = _grad_w_ffn_out(act, dffn)
    df, dx1, st_bwd = _ffn_backward(dffn, pre, x1, dx2, modnb, w["b_ada"], w["norm_ffn_w"], w["w_ffn_in"],
                                    w["w_ffn_out"])
    gw_ffn_in = _grad_w_ffn_in(hb2, df)
    return dict(mix=mix, oab=oab, dx1=dx1, st_ffn=st_fwd + st_bwd, gw_ffn_in=gw_ffn_in, gw_ffn_out=gw_ffn_out)


def _out_backward(w, g, modnb):
    dmix, d_out_a, d_out_b, st_out = _bwd_out(g["dx1"], g["mix"], modnb, w["b_ada"], w["w_out"])
    return dict(d_out_a=d_out_a, d_out_b=d_out_b, st_out=st_out, gw_out=_grad_w("grad_w_out", g["oab"], dmix, 512))


def _heads_backward(w, f, a):
    dp_conf, st_conf = _conf_bwd(a["d_out_a"], f["y_conv"], f["p_main"], w["conv_w"], w["conv_gn_w"],
                                 w["conv_gn_b"])
    dp_gdn, dp_ba, st_gdn = _gdn_bwd(a["d_out_b"], f["o_pre"], f["s_in"], f["t_inv"], f["p_main"], f["p_ba"],
                                     w["gdn_conv_w"], f["alog_l"], f["dt_l"], w["gdn_norm_w"])
    gw_in = _grad_w_in(dp_conf, dp_gdn, dp_ba, f["hb1"])[:NIN]
    return dict(dp_conf=dp_conf, dp_gdn=dp_gdn, dp_ba=dp_ba, st_conf=st_conf, st_gdn=st_gdn, gw_in=gw_in,
                gw_conv=st_conf[0:KC], gw_gconv=st_gdn[0:KS])


def _in_backward(w, f, g, a, h, xs, modnb):
    st_out, st_conf, st_gdn, st_ffn = a["st_out"], h["st_conf"], h["st_gdn"], g["st_ffn"]
    grad_x, st_in = _bwd_in(h["dp_conf"], h["dp_gdn"], h["dp_ba"], xs, g["dx1"], w["norm_mix_w"], modnb,
                            w["b_ada"], f["w_main"], f["w_ba"])
    dmod = jnp.concatenate([st_in[0:1], st_in[1:2], st_out[0:1], st_ffn[2:3], st_ffn[3:4], st_ffn[1:2]], axis=1)
    small = jnp.concatenate([
        dmod.reshape(48, LANES), st_in[2:3].reshape(8, LANES), st_ffn[4:5].reshape(8, LANES),
        st_ffn[0:1].reshape(8, LANES), st_conf[31:32].reshape(4, LANES), st_conf[32:33].reshape(4, LANES),
        st_conf[33:34].reshape(4, LANES), st_gdn[4:5, 0:LANES],
        _lanes(st_gdn[5:6, NH:2 * NH]), _lanes(st_gdn[6:7, NH:2 * NH]), st_ffn[5:6, 0:LANES]], axis=0)
    return dict(grad_x=grad_x, small=small)


def _local(w, xs, tgt, modnb):
    f = _mix_forward(w, xs, modnb)
    g = _ffn_stage(w, f, xs, tgt, modnb)
    a = _out_backward(w, g, modnb)
    h = _heads_backward(w, f, a)
    b = _in_backward(w, f, g, a, h, xs, modnb)
    return dict(b, gw_in=h["gw_in"], gw_conv=h["gw_conv"], gw_gconv=h["gw_gconv"], gw_out=a["gw_out"],
                gw_ffn_in=g["gw_ffn_in"], gw_ffn_out=g["gw_ffn_out"])


def kernel(x, c, w_ada, b_ada, norm_mix_w, w_in, conv_w, conv_b, conv_gn_w, conv_gn_b, gdn_conv_w, gdn_a_log, gdn_dt_bias, gdn_norm_w, w_out, norm_ffn_w, w_ffn_in, w_ffn_out, norm_final_w, loss_target, m_w_ada, m_b_ada, m_norm_mix_w, m_w_in, m_conv_w, m_conv_b, m_conv_gn_w, m_conv_gn_b, m_gdn_conv_w, m_gdn_a_log, m_gdn_dt_bias, m_gdn_norm_w, m_w_out, m_norm_ffn_w, m_w_ffn_in, m_w_ffn_out, m_norm_final_w, v_w_ada, v_b_ada, v_norm_mix_w, v_w_in, v_conv_w, v_conv_b, v_conv_gn_w, v_conv_gn_b, v_gdn_conv_w, v_gdn_a_log, v_gdn_dt_bias, v_gdn_norm_w, v_w_out, v_norm_ffn_w, v_w_ffn_in, v_w_ffn_out, v_norm_final_w):
    me = 4 * lax.axis_index("x") + 2 * lax.axis_index("y") + lax.axis_index("c")
    xs = x.reshape(S, D)
    tgt = loss_target.reshape(S, D)

    late = [w_out[0].astype(BF16), jnp.transpose(w_ffn_in[0]).astype(BF16), w_ffn_out[0].astype(BF16)]
    g_c, g_cw, g_gcw, g_win, *late_lands = _gather_two_level(
        "gather_weights", [c, conv_w[0], gdn_conv_w[0], jnp.transpose(w_in[0]).astype(BF16)] + late,
        seed_only=(4, 5, 6))
    c_all = g_c.reshape(N_DEV, D)
    g_mod, mod_token = _exchange("gather_mod", [_mod_shard(c_all, w_ada[0])], [False], with_token=True)
    modnb = lax.dynamic_index_in_dim(g_mod, me, axis=1, keepdims=False).reshape(1, 6 * D)
    late_started = _exchange_start("gather_late_start", [_after(late[0], mod_token)] + late[1:], late_lands,
                                   [False] * 3, only=LEVEL_ONE)
    modnb = _after(modnb, late_started[-1])
    w = dict(b_ada=b_ada, norm_mix_w=norm_mix_w, conv_b=conv_b, conv_gn_w=conv_gn_w, conv_gn_b=conv_gn_b,
             gdn_a_log=gdn_a_log, gdn_dt_bias=gdn_dt_bias, gdn_norm_w=gdn_norm_w, norm_ffn_w=norm_ffn_w,
             norm_final_w=norm_final_w.reshape(1, D),
             conv_w=jnp.transpose(g_cw, (1, 0, 2)).reshape(KC, CW),
             gdn_conv_w=jnp.transpose(g_gcw, (1, 0, 2)).reshape(KS, 3 * GW),
             w_in=g_win.reshape(NIN, D))

    relay = {}

    def relay_late(out_b):
        _, late_landed = _exchange_wait("gather_late_wait", late_started, [False] * 3, (out_b,), only=LEVEL_ONE)
        relay["started"] = _relay_start("gather_late_relay_start", late_landed)
        return relay["started"][-1]

    f = _mix_forward(w, xs, modnb, relay_late)
    g_wout, g_wfi, g_wfo = _relay_wait("gather_late_relay_wait", relay["started"], (f["out_a"],))
    w.update(w_out=g_wout.reshape(D, D), w_ffn_in=g_wfi, w_ffn_out=g_wfo.reshape(4, FB, D))
    g = _ffn_stage(w, f, xs, tgt, modnb)

    ffn_grads = [g["gw_ffn_in"], g["gw_ffn_out"].reshape(N_DEV, DFF // N_DEV, D)]
    ffn_started = _exchange_start("scatter_ffn_start", ffn_grads,
                                  [lax.empty(a.shape, a.dtype) for a in ffn_grads], [True] * 2)
    a = _out_backward(w, g, _after(modnb, ffn_started[-1]))
    out_grads = [a["gw_out"].reshape(N_DEV, D // N_DEV, D)]
    out_started = _exchange_start("scatter_out_start", out_grads,
                                  [lax.empty(t.shape, t.dtype) for t in out_grads], [True])
    h = _heads_backward(dict(w, conv_gn_w=_after(w["conv_gn_w"], out_started[-1])), f, a)

    in_grads = [h["gw_in"].reshape(N_DEV, NIN // N_DEV, D),
                jnp.transpose(h["gw_conv"].reshape(KC, N_DEV, CW // N_DEV), (1, 0, 2)),
                jnp.transpose(h["gw_gconv"].reshape(KS, N_DEV, 3 * GW // N_DEV), (1, 0, 2))]
    in_started = _exchange_start("scatter_in_start", in_grads,
                                 [lax.empty(t.shape, t.dtype) for t in in_grads], [True] * 3)
    loc = _in_backward(w, f, g, a, h, xs, _after(modnb, in_started[-1]))
    small_started = _exchange_start("gather_small_start", [loc["small"]],
                                    [lax.empty((N_DEV, SMALL_ROWS, LANES), F32)], [False])

    def own(sent):
        return lax.dynamic_index_in_dim(sent, me, axis=0, keepdims=False)

    big = {}
    (sent_fi, sent_fo), (r_fi, r_fo) = _exchange_wait("scatter_ffn_wait", ffn_started, [True] * 2,
                                                         (small_started[-1],))
    big["w_ffn_in"] = [jnp.transpose(t) for t in _reduce_adam(
        "adam_w_ffn_in", r_fi, jnp.transpose(w_ffn_in[0]), jnp.transpose(m_w_ffn_in[0]),
        jnp.transpose(v_w_ffn_in[0]), own(sent_fi))]
    big["w_ffn_out"] = _reduce_adam("adam_w_ffn_out", r_fo, w_ffn_out[0], m_w_ffn_out[0], v_w_ffn_out[0],
                                    own(sent_fo))
    (sent_out,), (r_out,) = _exchange_wait("scatter_out_wait", out_started, [True], (big["w_ffn_out"][0],))
    big["w_out"] = _reduce_adam("adam_w_out", r_out, w_out[0], m_w_out[0], v_w_out[0], own(sent_out))

    (sent_small,), (r_small,) = _exchange_wait("gather_small_wait", small_started, [False], (big["w_out"][0],))
    slot = lax.broadcasted_iota(jnp.int32, (N_DEV, 1, 1), 0)
    g_small = jnp.where(slot == me, sent_small[None], r_small)
    def views(b_, nm_, nf_, nl_, cb_, gw_, gb_, gn_, al_, dt_):
        arrs = [b_, nm_, nf_, nl_, cb_, gw_, gb_, gn_, al_, dt_]
        return {nm: t.reshape(rows, lanes) for (nm, _, rows, lanes), t in zip(SMALL_LAYOUT, arrs)}

    loss_row, res = _adam_small(
        g_small,
        views(b_ada, norm_mix_w, norm_ffn_w, norm_final_w, conv_b, conv_gn_w, conv_gn_b, gdn_norm_w, gdn_a_log,
              gdn_dt_bias),
        views(m_b_ada, m_norm_mix_w, m_norm_ffn_w, m_norm_final_w, m_conv_b, m_conv_gn_w, m_conv_gn_b,
              m_gdn_norm_w, m_gdn_a_log, m_gdn_dt_bias),
        views(v_b_ada, v_norm_mix_w, v_norm_ffn_w, v_norm_final_w, v_conv_b, v_conv_gn_w, v_conv_gn_b,
              v_gdn_norm_w, v_gdn_a_log, v_gdn_dt_bias))
    loss = loss_row[0, 0]
    small_shapes = dict(b_ada=(1, 6 * D), norm_mix_w=(1, D), norm_ffn_w=(1, D), norm_final_w=(D,),
                        conv_b=(1, CW), conv_gn_w=(1, CW), conv_gn_b=(1, CW), gdn_norm_w=(1, DH),
                        gdn_a_log=(1, NH), gdn_dt_bias=(1, NH))
    res = [{nm: t.reshape(small_shapes[nm]) for nm, t in kind.items()} for kind in res]

    dmod_rows = g_small[:, 0:48, :].reshape(N_DEV, 6 * D)
    dmod_sh = lax.dynamic_slice_in_dim(dmod_rows, me * (6 * D // N_DEV), 6 * D // N_DEV, axis=1)

    big["w_ada"] = _ada_adam(c_all, dmod_sh, w_ada[0], m_w_ada[0], v_w_ada[0])
    (sent_in, sent_cw, sent_gcw), (r_in, r_cw, r_gcw) = _exchange_wait(
        "scatter_in_wait", in_started, [True] * 3, (big["w_ada"][0],))
    big["w_in"] = [jnp.transpose(t) for t in _reduce_adam(
        "adam_w_in", r_in, jnp.transpose(w_in[0]), jnp.transpose(m_w_in[0]), jnp.transpose(v_w_in[0]),
        own(sent_in))]
    big["conv_w"] = _reduce_adam("adam_conv_w", r_cw, conv_w[0], m_conv_w[0], v_conv_w[0], own(sent_cw))
    big["gdn_conv_w"] = _reduce_adam("adam_gdn_conv_w", r_gcw, gdn_conv_w[0], m_gdn_conv_w[0], v_gdn_conv_w[0],
                                     own(sent_gcw))
    outs = [loss, loc["grad_x"].reshape(1, S, D)]
    for kind in range(4):
        for nm in WEIGHT_NAMES:
            outs.append(big[nm][kind][None] if nm in big else res[kind][nm])
    return tuple(outs)
```

```python
import functools

import jax
import jax.numpy as jnp
from jax import lax
from jax.experimental import pallas as pl
from jax.experimental.pallas import tpu as pltpu

F32 = jnp.float32
BF16 = jnp.bfloat16
HI = lax.Precision.HIGHEST
MESH = pl.DeviceIdType.MESH

N_DEV = 8
S = 2048
D = 1024
TM = 256
NT = S // TM
CW = 512
KC = 31
NG = 8
GSZ = CW // NG
HALO = 32
GW = 512
NH = 4
DH = 128
KS = 4
SH = 8
CL = 64
NCH = S // CL
NMAIN = 2 * CW + 4 * GW
NIN = NMAIN + 2 * NH
DFF = 2816
FB = DFF // 4
EPS = 1e-6
QSCALE = DH ** -0.5
LANES = 128
SMALL_ROWS = 88

ADAM_LR = 0.001
ADAM_B1 = 0.9
ADAM_B2 = 0.999
ADAM_EPS = 1e-08
ADAM_WD = 0.01
ADAM_STEP = 10
BC1 = 1.0 - ADAM_B1 ** ADAM_STEP
BC2 = 1.0 - ADAM_B2 ** ADAM_STEP

MIB = 1024 * 1024
VMEM_LIMIT_MIB = 32


def _params(limit_mib=VMEM_LIMIT_MIB, **kw):
    return pltpu.CompilerParams(vmem_limit_bytes=limit_mib * MIB, **kw)


def _sig(x):
    return jax.nn.sigmoid(x)


GP = BF16


def _operands(a, b, prec):
    if prec is BF16:
        return a.astype(BF16), b.astype(BF16), None
    return a, b, prec


def _dot(a, b, prec=None):
    a, b, prec = _operands(a, b, prec)
    return jnp.dot(a, b, preferred_element_type=F32, precision=prec)


def _dot_nt(a, b, prec=None):
    a, b, prec = _operands(a, b, prec)
    return lax.dot_general(a, b, (((1,), (1,)), ((), ())), preferred_element_type=F32, precision=prec)


def _dot_tn(a, b, prec=None):
    a, b, prec = _operands(a, b, prec)
    return lax.dot_general(a, b, (((0,), (0,)), ((), ())), preferred_element_type=F32, precision=prec)


def _lockstep(gens):
    gens = list(gens)
    while gens:
        alive = []
        for g in gens:
            try:
                next(g)
                alive.append(g)
            except StopIteration:
                pass
        gens = alive


def _rowsum(x):
    return jnp.sum(x, axis=-1, keepdims=True)


def _colsum(x):
    return jnp.sum(x, axis=0, keepdims=True)


def _mod(mod_ref, b_ref, k):
    return mod_ref[:, k * D:(k + 1) * D] + b_ref[:, k * D:(k + 1) * D]


def _const(shape):
    nd = len(shape)
    return pl.BlockSpec(shape, lambda *_: (0,) * nd)


def _const1(shape):
    nd = len(shape)
    return pl.BlockSpec(shape, lambda *_: (0,) * nd, pipeline_mode=pl.Buffered(1))


PEER_FLIPS = [(dx, dy, dc) for dx in (0, 1) for dy in (0, 1) for dc in (0, 1)][1:]


def _after(x, token):
    return x + token[0:1, 0:1].astype(x.dtype).reshape((1,) * x.ndim)


def _exchange(name, srcs, per_dest, seed_only=(), with_token=False):
    n = len(srcs)
    out_shape = []
    for a, pd in zip(srcs, per_dest):
        blk = a.shape[1:] if pd else a.shape
        out_shape.append(jax.ShapeDtypeStruct((N_DEV,) + tuple(blk), a.dtype))

    def body(*refs):
        src = refs[:n]
        dst = refs[n:2 * n]
        send_sems, recv_sems, local_sems = refs[-3:]
        if with_token:
            refs[2 * n][...] = jnp.zeros((8, LANES), F32)
        x, y, c = lax.axis_index("x"), lax.axis_index("y"), lax.axis_index("c")
        me = 4 * x + 2 * y + c

        def piece(i, j):
            return src[i].at[j] if per_dest[i] else src[i]

        copies = []
        for k, (dx, dy, dc) in enumerate(PEER_FLIPS):
            px = 1 - x if dx else x
            py = 1 - y if dy else y
            pc = 1 - c if dc else c
            pj = 4 * px + 2 * py + pc
            for i in range(n):
                if i in seed_only:
                    continue
                cp = pltpu.make_async_remote_copy(
                    src_ref=piece(i, pj), dst_ref=dst[i].at[me],
                    send_sem=send_sems.at[k * n + i], recv_sem=recv_sems.at[k * n + i],
                    device_id=(px, py, pc), device_id_type=MESH)
                cp.start()
                arrive = pltpu.make_async_remote_copy(
                    src_ref=piece(i, pj), dst_ref=dst[i].at[pj],
                    send_sem=send_sems.at[k * n + i], recv_sem=recv_sems.at[k * n + i],
                    device_id=(px, py, pc), device_id_type=MESH)
                copies.append((cp, arrive))
        own = []
        for i in range(n):
            lc = pltpu.make_async_copy(piece(i, me), dst[i].at[me], local_sems.at[i])
            lc.start()
            own.append(lc)
        for cp, arrive in copies:
            arrive.wait_recv()
        for cp, arrive in copies:
            cp.wait_send()
        for lc in own:
            lc.wait()

    any_spec = pl.BlockSpec(memory_space=pl.ANY)
    out_specs = [any_spec] * n
    if with_token:
        out_shape.append(jax.ShapeDtypeStruct((8, LANES), F32))
        out_specs.append(pl.BlockSpec(memory_space=pltpu.VMEM))
    return pl.pallas_call(
        body, name=name, out_shape=tuple(out_shape),
        in_specs=[any_spec] * n, out_specs=tuple(out_specs),
        scratch_shapes=[pltpu.SemaphoreType.DMA((7 * n,)), pltpu.SemaphoreType.DMA((7 * n,)),
                        pltpu.SemaphoreType.DMA((n,))],
        compiler_params=pltpu.CompilerParams(has_side_effects=True),
    )(*srcs)


CHIP_FLIPS = [(0, 1), (1, 0), (1, 1)]
LEVEL_ONE = [k for k, (dx, dy, dc) in enumerate(PEER_FLIPS) if (dx, dy, dc) == (0, 0, 1) or dc == 0]


def _chip_peers(x, y):
    return [(1 - x if dx else x, 1 - y if dy else y) for dx, dy in CHIP_FLIPS]


def _gather_two_level(name, srcs, seed_only=()):
    n = len(srcs)
    live = [i for i in range(n) if i not in seed_only]

    def body(*refs):
        src, dst = refs[:n], refs[n:2 * n]
        send_sems, recv_sems, local_sems = refs[2 * n:2 * n + 3]
        bounce = refs[2 * n + 3:]
        x, y, c = lax.axis_index("x"), lax.axis_index("y"), lax.axis_index("c")
        me = 4 * x + 2 * y + c
        sibling = (x, y, 1 - c)
        chips = _chip_peers(x, y)

        def copy(k, i, src_ref, slot, to):
            return pltpu.make_async_remote_copy(
                src_ref=src_ref, dst_ref=dst[i].at[slot], send_sem=send_sems.at[k * n + i],
                recv_sem=recv_sems.at[k * n + i], device_id=to, device_id_type=MESH)

        first = []
        for i in live:
            first.append(copy(0, i, src[i], me, sibling))
            first += [copy(1 + j, i, src[i], me, (px, py, c)) for j, (px, py) in enumerate(chips)]
        for cp in first:
            cp.start()
        up = [pltpu.make_async_copy(src[i], bounce[i], local_sems.at[i]) for i in range(n)]
        for cp in up:
            cp.start()
        for cp in up:
            cp.wait()
        own = [pltpu.make_async_copy(bounce[i], dst[i].at[me], local_sems.at[i]) for i in range(n)]
        for cp in own:
            cp.start()
        passed = []
        for j, (px, py) in enumerate(chips):
            slot = 4 * px + 2 * py + c
            for i in live:
                copy(1 + j, i, src[i], slot, (px, py, c)).wait_recv()
                fwd = copy(4 + j, i, dst[i].at[slot], slot, sibling)
                fwd.start()
                passed.append(fwd)
        for i in live:
            copy(0, i, src[i], 4 * x + 2 * y + 1 - c, sibling).wait_recv()
            for j, (px, py) in enumerate(chips):
                copy(4 + j, i, src[i], 4 * px + 2 * py + 1 - c, sibling).wait_recv()
        for cp in first + passed:
            cp.wait_send()
        for cp in own:
            cp.wait()

    any_spec = pl.BlockSpec(memory_space=pl.ANY)
    return pl.pallas_call(
        body, name=name, out_shape=tuple(jax.ShapeDtypeStruct((N_DEV,) + a.shape, a.dtype) for a in srcs),
        in_specs=[any_spec] * n, out_specs=tuple([any_spec] * n),
        scratch_shapes=[pltpu.SemaphoreType.DMA((7 * n,)), pltpu.SemaphoreType.DMA((7 * n,)),
                        pltpu.SemaphoreType.DMA((n,))] + [pltpu.VMEM(a.shape, a.dtype) for a in srcs],
        compiler_params=pltpu.CompilerParams(has_side_effects=True),
    )(*srcs)


def _relay_copy(land, sems, i, n, j, slot, sibling):
    send_sems, recv_sems = sems
    return pltpu.make_async_remote_copy(
        src_ref=land[i].at[slot], dst_ref=land[i].at[slot], send_sem=send_sems.at[j * n + i],
        recv_sem=recv_sems.at[j * n + i], device_id=sibling, device_id_type=MESH)


def _relay_start(name, lands):
    n = len(lands)

    def body(*refs):
        land = refs[:n]
        sems = refs[n], refs[n + 1]
        x, y, c = lax.axis_index("x"), lax.axis_index("y"), lax.axis_index("c")
        for j, (px, py) in enumerate(_chip_peers(x, y)):
            for i in range(n):
                _relay_copy(land, sems, i, n, j, 4 * px + 2 * py + c, (x, y, 1 - c)).start()
        refs[-1][...] = jnp.zeros((8, LANES), F32)

    return pl.pallas_call(
        body, name=name,
        out_shape=(pltpu.SemaphoreType.DMA((3 * n,)), pltpu.SemaphoreType.DMA((3 * n,)),
                   *[pltpu.HBM(a.shape, a.dtype) for a in lands], jax.ShapeDtypeStruct((8, LANES), F32)),
        in_specs=[HBM_SPEC] * n,
        out_specs=(SEM_SPEC, SEM_SPEC, *[HBM_SPEC] * n, pl.BlockSpec(memory_space=pltpu.VMEM)),
        input_output_aliases={i: 2 + i for i in range(n)},
        compiler_params=pltpu.CompilerParams(has_side_effects=DATAFLOW),
    )(*[pltpu.with_memory_space_constraint(a, pltpu.HBM) for a in lands])


def _relay_wait(name, started, after):
    n = len(started) - 3
    arrays = list(started[2:2 + n])

    def body(*refs):
        land = refs[:n]
        sems = refs[n], refs[n + 1]
        x, y, c = lax.axis_index("x"), lax.axis_index("y"), lax.axis_index("c")
        for j, (px, py) in enumerate(_chip_peers(x, y)):
            for i in range(n):
                _relay_copy(land, sems, i, n, j, 4 * px + 2 * py + c, (x, y, 1 - c)).wait_send()
                _relay_copy(land, sems, i, n, j, 4 * px + 2 * py + 1 - c, (x, y, 1 - c)).wait_recv()

    return pl.pallas_call(
        body, name=name,
        out_shape=tuple(pltpu.HBM(a.shape, a.dtype) for a in arrays),
        in_specs=[HBM_SPEC] * n + [SEM_SPEC, SEM_SPEC] + [pl.BlockSpec(memory_space=pl.ANY)] * len(after),
        out_specs=tuple([HBM_SPEC] * n),
        input_output_aliases={i: i for i in range(n)},
        compiler_params=pltpu.CompilerParams(has_side_effects=DATAFLOW),
    )(*arrays, started[0], started[1], *after)


HBM_SPEC = pl.BlockSpec(memory_space=pltpu.HBM)
SEM_SPEC = pl.BlockSpec(memory_space=pltpu.SEMAPHORE)
DATAFLOW = pltpu.SideEffectType.DATAFLOW_SIDE_EFFECTING


def _peers(only=None):
    x, y, c = lax.axis_index("x"), lax.axis_index("y"), lax.axis_index("c")
    out = []
    for k, (dx, dy, dc) in enumerate(PEER_FLIPS):
        if only is not None and k not in only:
            continue
        px = 1 - x if dx else x
        py = 1 - y if dy else y
        pc = 1 - c if dc else c
        out.append((k, (px, py, pc), 4 * px + 2 * py + pc))
    return 4 * x + 2 * y + c, out


def _exchange_start(name, srcs, lands, per_dest, only=None):
    n = len(srcs)

    def body(*refs):
        src, land = refs[:n], refs[n:2 * n]
        send_sems, recv_sems = refs[2 * n], refs[2 * n + 1]
        token = refs[-1]
        me, peers = _peers(only)
        for k, peer, pj in peers:
            for i in range(n):
                pltpu.make_async_remote_copy(
                    src_ref=src[i].at[pj] if per_dest[i] else src[i], dst_ref=land[i].at[me],
                    send_sem=send_sems.at[k * n + i], recv_sem=recv_sems.at[k * n + i],
                    device_id=peer, device_id_type=MESH).start()
        token[...] = jnp.zeros((8, LANES), F32)

    arrays = list(srcs) + list(lands)
    return pl.pallas_call(
        body, name=name,
        out_shape=(pltpu.SemaphoreType.DMA((7 * n,)), pltpu.SemaphoreType.DMA((7 * n,)),
                   *[pltpu.HBM(a.shape, a.dtype) for a in arrays], jax.ShapeDtypeStruct((8, LANES), F32)),
        in_specs=[HBM_SPEC] * (2 * n),
        out_specs=(SEM_SPEC, SEM_SPEC, *[HBM_SPEC] * (2 * n), pl.BlockSpec(memory_space=pltpu.VMEM)),
        input_output_aliases={i: 2 + i for i in range(2 * n)},
        compiler_params=pltpu.CompilerParams(has_side_effects=DATAFLOW),
    )(*[pltpu.with_memory_space_constraint(a, pltpu.HBM) for a in arrays])


def _exchange_wait(name, started, per_dest, after, only=None):
    n = (len(started) - 3) // 2
    send_sems, recv_sems = started[0], started[1]
    arrays = list(started[2:2 + 2 * n])

    def body(*refs):
        src, land = refs[:n], refs[n:2 * n]
        send, recv = refs[2 * n], refs[2 * n + 1]
        me, peers = _peers(only)
        for k, peer, pj in peers:
            for i in range(n):
                cp = pltpu.make_async_remote_copy(
                    src_ref=src[i].at[pj] if per_dest[i] else src[i], dst_ref=land[i].at[pj],
                    send_sem=send.at[k * n + i], recv_sem=recv.at[k * n + i],
                    device_id=peer, device_id_type=MESH)
                cp.wait_send()
                cp.wait_recv()

    outs = pl.pallas_call(
        body, name=name,
        out_shape=tuple(pltpu.HBM(a.shape, a.dtype) for a in arrays),
        in_specs=[HBM_SPEC] * (2 * n) + [SEM_SPEC, SEM_SPEC] + [pl.BlockSpec(memory_space=pl.ANY)] * len(after),
        out_specs=tuple([HBM_SPEC] * (2 * n)),
        input_output_aliases={i: i for i in range(2 * n)},
        compiler_params=pltpu.CompilerParams(has_side_effects=DATAFLOW),
    )(*arrays, send_sems, recv_sems, *after)
    return outs[:n], outs[n:]


def _mod_shard(c_all, w_ada):
    def body(c_ref, w_ref, o_ref):
        cv = c_ref[...]
        ca = cv * _sig(cv)
        o_ref[...] = _dot(ca.astype(BF16), w_ref[...].astype(BF16))

    return pl.pallas_call(
        body, name="mod_shard", out_shape=jax.ShapeDtypeStruct((N_DEV, w_ada.shape[1]), F32),
        compiler_params=_params(),
    )(c_all, w_ada)


TI = 512


def _fwd_in(x, nw1, modnb, bada, w_main, w_ba):
    def body(x_ref, nw_ref, mod_ref, b_ref, wm_ref, wb_ref, pm_ref, pb_ref, hb_ref):
        xv = x_ref[...]
        r = lax.rsqrt(jnp.mean(xv * xv, axis=-1, keepdims=True) + EPS)
        h = (xv * r * nw_ref[...]) * (1.0 + _mod(mod_ref, b_ref, 1)) + _mod(mod_ref, b_ref, 0)
        hb = h.astype(BF16)
        hb_ref[...] = hb
        pm_ref[...] = _dot_nt(hb, wm_ref[...])
        pb_ref[...] = _dot_nt(hb, wb_ref[...])

    return pl.pallas_call(
        body, name="fwd_in", grid=(S // TI,),
        in_specs=[pl.BlockSpec((TI, D), lambda i: (i, 0)), _const((1, D)), _const((1, 6 * D)), _const((1, 6 * D)),
                  _const1((NMAIN, D)), _const((LANES, D))],
        out_specs=(pl.BlockSpec((TI, NMAIN), lambda i: (i, 0)), pl.BlockSpec((TI, LANES), lambda i: (i, 0)),
                   pl.BlockSpec((TI, D), lambda i: (i, 0))),
        out_shape=(jax.ShapeDtypeStruct((S, NMAIN), F32), jax.ShapeDtypeStruct((S, LANES), F32),
                   jax.ShapeDtypeStruct((S, D), BF16)),
        compiler_params=_params(dimension_semantics=("arbitrary",)),
    )(x, nw1, modnb, bada, w_main, w_ba)


def _group_mean_matrix():
    ii = lax.broadcasted_iota(jnp.int32, (CW, CW), 0) // GSZ
    jj = lax.broadcasted_iota(jnp.int32, (CW, CW), 1) // GSZ
    return jnp.where(ii == jj, 1.0 / GSZ, 0.0).astype(F32)


SUB = 8
SHIFT_ROWS = HALO + TM - SUB


def _fill_shifted(buf, sh):
    for b in range(1, SUB):
        sh[b - 1] = buf[b:b + SHIFT_ROWS, :]


def _rows_at(buf, sh, off):
    a, b = divmod(off, SUB)
    if b == 0:
        return buf[off:off + TM, :]
    return sh[b - 1, SUB * a:SUB * a + TM, :]


def _group_mean(x, pm):
    hi = x.astype(BF16)
    r1 = x - hi.astype(F32)
    mid = r1.astype(BF16)
    lo = (r1 - mid.astype(F32)).astype(BF16)
    return _dot(hi, pm) + _dot(mid, pm) + _dot(lo, pm)


def _conf_fwd(p_main, conv_w, conv_b, gn_w, gn_b):
    def body(a_ref, g_ref, w_ref, b_ref, gw_ref, gb_ref, y_ref, oa_ref, ubuf, ush):
        i = pl.program_id(0)

        @pl.when(i == 0)
        def _():
            ubuf[0:HALO, :] = jnp.zeros((HALO, CW), F32)

        ubuf[HALO:HALO + TM, :] = a_ref[...] * _sig(g_ref[...])
        _fill_shifted(ubuf, ush)
        acc = jnp.zeros((TM, CW), F32) + b_ref[...]
        for k in range(KC):
            acc = acc + w_ref[k:k + 1, :] * _rows_at(ubuf, ush, HALO - (KC - 1) + k)
        y_ref[...] = acc
        ubuf[0:HALO, :] = ubuf[TM:TM + HALO, :]
        pm = _group_mean_matrix().astype(BF16)
        dlt = acc - _group_mean(acc, pm)
        var = _group_mean(dlt * dlt, pm)
        o = dlt * lax.rsqrt(var + EPS) * gw_ref[...] + gb_ref[...]
        oa_ref[...] = o * _sig(o)

    return pl.pallas_call(
        body, name="conf_fwd", grid=(NT,),
        in_specs=[pl.BlockSpec((TM, CW), lambda i: (i, 0)), pl.BlockSpec((TM, CW), lambda i: (i, 1)),
                  _const((KC, CW)), _const((1, CW)), _const((1, CW)), _const((1, CW))],
        out_specs=(pl.BlockSpec((TM, CW), lambda i: (i, 0)), pl.BlockSpec((TM, CW), lambda i: (i, 0))),
        out_shape=(jax.ShapeDtypeStruct((S, CW), F32), jax.ShapeDtypeStruct((S, CW), F32)),
        scratch_shapes=[pltpu.VMEM((HALO + TM, CW), F32), pltpu.VMEM((SUB - 1, SHIFT_ROWS, CW), F32)],
        compiler_params=_params(dimension_semantics=("arbitrary",)),
    )(p_main, p_main, conv_w, conv_b, gn_w, gn_b)


def _tri_iota():
    ii = lax.broadcasted_iota(jnp.int32, (CL, CL), 0)
    jj = lax.broadcasted_iota(jnp.int32, (CL, CL), 1)
    return ii, jj


def _gdn_gates(ba, alog_l, dt_l):
    beta_all = _sig(ba)
    xg = ba + dt_l
    sp = jnp.maximum(xg, 0.0) + jnp.log(1.0 + jnp.exp(-jnp.abs(xg)))
    neg_a = -jnp.exp(alog_l)
    return beta_all, neg_a * sp, xg, neg_a


def _ones_dot(ones, x):
    hi = x.astype(BF16)
    r1 = x - hi.astype(F32)
    mid = r1.astype(BF16)
    lo = (r1 - mid.astype(F32)).astype(BF16)
    return _dot(ones, hi) + _dot(ones, mid) + _dot(ones, lo)


def _gdn_cumsum(g_all):
    ii, jj = _tri_iota()
    low = jnp.where(ii >= jj, 1.0, 0.0).astype(BF16)
    gcum = _ones_dot(low, g_all)
    return gcum, jnp.transpose(gcum)


def _split(x):
    hi = x.astype(BF16)
    return hi, (x - hi.astype(F32)).astype(BF16)


def _dot_split(a, b):
    (ah, al), (bh, bl) = a, b
    return _dot(ah, bh) + (_dot(ah, bl) + _dot(al, bh))


def _unit_lower_inverses(mats):
    ii, jj = _tri_iota()
    eye = jnp.where(ii == jj, 1.0, 0.0).astype(F32)
    ts = [eye - a for a in mats]
    ps = [_dot_split(s, s) for s in map(_split, mats)]
    for _ in range(4):
        sp = [_split(p) for p in ps]
        ts = [t + _dot_split(_split(t), s) for t, s in zip(ts, sp)]
        ps = [_dot_split(s, s) for s in sp]
    return [t + _dot_split(_split(t), _split(p)) for t, p in zip(ts, ps)]


def _head_terms(qh, kh, beta, gcol, grow):
    ii, jj = _tri_iota()
    causal = ii >= jj
    strict = ii > jj
    rq = lax.rsqrt(_rowsum(qh * qh) + EPS)
    rk = lax.rsqrt(_rowsum(kh * kh) + EPS)
    qn = qh * rq
    kn = kh * rk
    qs = qn * QSCALE
    decay = jnp.where(causal, jnp.exp(jnp.where(causal, gcol - grow, 0.0)), 0.0)
    gam = jnp.exp(gcol)
    gl = gcol[CL - 1:CL, :]
    kds = jnp.exp(gl - gcol)
    cd = jnp.exp(gl)
    kb = kn * beta
    a = jnp.where(strict, _dot_nt(kb, kn, GP) * decay, 0.0)
    qk = jnp.where(causal, _dot_nt(qs, kn, GP) * decay, 0.0)
    return dict(rq=rq, rk=rk, qn=qn, kn=kn, qs=qs, decay=decay, gam=gam, kds=kds, cd=cd, kb=kb, a=a, qk=qk,
                causal=causal, strict=strict)


def _short_conv(w_ref, buf, rows=CL):
    acc = w_ref[0:1, :] * buf[SH - KS + 1:SH - KS + 1 + rows, :]
    for k in range(1, KS):
        off = SH - (KS - 1) + k
        acc = acc + w_ref[k:k + 1, :] * buf[off:off + rows, :]
    return acc


CPS = 4
TG = CPS * CL


def _gdn_prep(p_main, p_ba, gdn_conv_w, alog_l, dt_l):
    def body(q_ref, k_ref, v_ref, qh_ref, kh_ref, vh_ref, ba_ref, w_ref, al_ref, dt_ref,
             wo_ref, uo_ref, qg_ref, kd_ref, qk_ref, cd_ref, t_ref, xbuf):
        i = pl.program_id(0)
        first = i == 0
        xbuf[0:SH, 0:GW] = jnp.where(first, 0.0, qh_ref[...])
        xbuf[0:SH, GW:2 * GW] = jnp.where(first, 0.0, kh_ref[...])
        xbuf[0:SH, 2 * GW:3 * GW] = jnp.where(first, 0.0, vh_ref[...])
        xbuf[SH:SH + TG, 0:GW] = q_ref[...]
        xbuf[SH:SH + TG, GW:2 * GW] = k_ref[...]
        xbuf[SH:SH + TG, 2 * GW:3 * GW] = v_ref[...]
        conv = _short_conv(w_ref, xbuf, TG)
        qkv = conv * _sig(conv)
        beta_all, g_all, _, _ = _gdn_gates(ba_ref[...], al_ref[...], dt_ref[...])
        lane = lax.broadcasted_iota(jnp.int32, (8, LANES), 1)
        cums = [_gdn_cumsum(g_all[cc * CL:(cc + 1) * CL, :]) for cc in range(CPS)]
        pairs = [(cc, h) for cc in range(CPS) for h in range(NH)]
        terms, vbs = [], []
        for cc, h in pairs:
            r0, lo = cc * CL, h * DH
            beta = beta_all[r0:r0 + CL, h:h + 1]
            gcum, gcum_t = cums[cc]
            terms.append(_head_terms(qkv[r0:r0 + CL, lo:lo + DH], qkv[r0:r0 + CL, GW + lo:GW + lo + DH], beta,
                                     gcum[:, NH + h:NH + h + 1], gcum_t[NH + h:NH + h + 1, :]))
            vbs.append(qkv[r0:r0 + CL, 2 * GW + lo:2 * GW + lo + DH] * beta)
        invs = _unit_lower_inverses([f["a"] for f in terms])
        cds = [jnp.zeros((8, LANES), F32) for _ in range(CPS)]
        for (cc, h), f, t, vb in zip(pairs, terms, invs, vbs):
            r0, lo = cc * CL, h * DH
            t_ref[cc, h] = t
            uo_ref[r0:r0 + CL, lo:lo + DH] = _dot(t, vb, GP)
            wo_ref[r0:r0 + CL, lo:lo + DH] = _dot(t, f["kb"] * f["gam"], GP).astype(BF16)
            qg_ref[r0:r0 + CL, lo:lo + DH] = (f["qs"] * f["gam"]).astype(BF16)
            kd_ref[r0:r0 + CL, lo:lo + DH] = (f["kn"] * f["kds"]).astype(BF16)
            qk_ref[cc, h] = f["qk"].astype(BF16)
            cds[cc] = cds[cc] + jnp.where(lane == h, f["cd"], 0.0)
        for cc in range(CPS):
            cd_ref[cc] = cds[cc]

    col = lambda j: pl.BlockSpec((TG, GW), lambda i: (i, j))
    halo = lambda j: pl.BlockSpec((SH, GW), lambda i: (jnp.maximum(i * (TG // SH) - 1, 0), j))
    tile = lambda: pl.BlockSpec((TG, GW), lambda i: (i, 0))
    sq = lambda: pl.BlockSpec((CPS, NH, CL, CL), lambda i: (i, 0, 0, 0))
    return pl.pallas_call(
        body, name="gdn_prep", grid=(NCH // CPS,),
        in_specs=[col(2), col(3), col(4), halo(2), halo(3), halo(4), pl.BlockSpec((TG, LANES), lambda i: (i, 0)),
                  _const((KS, 3 * GW)), _const((1, LANES)), _const((1, LANES))],
        out_specs=(tile(), tile(), tile(), tile(), sq(), pl.BlockSpec((CPS, 8, LANES), lambda i: (i, 0, 0)), sq()),
        out_shape=(jax.ShapeDtypeStruct((S, GW), BF16), jax.ShapeDtypeStruct((S, GW), F32),
                   jax.ShapeDtypeStruct((S, GW), BF16), jax.ShapeDtypeStruct((S, GW), BF16),
                   jax.ShapeDtypeStruct((NCH, NH, CL, CL), BF16), jax.ShapeDtypeStruct((NCH, 8, LANES), F32),
                   jax.ShapeDtypeStruct((NCH, NH, CL, CL), F32)),
        scratch_shapes=[pltpu.VMEM((SH + TG, 3 * GW), F32)],
        compiler_params=_params(dimension_semantics=("arbitrary",)),
    )(p_main, p_main, p_main, p_main, p_main, p_main, p_ba, gdn_conv_w, alog_l, dt_l)


def _gdn_scan(w_o, u_o, qg, kd, qk, cd, p_main, gdn_nw):
    def body(w_ref, u_ref, qg_ref, kd_ref, qk_ref, cd_ref, z_ref, nw_ref, ob_ref, o_ref, sin_ref, state):
        n = pl.program_id(0)

        @pl.when(n == 0)
        def _():
            state[...] = jnp.zeros((NH, DH, DH), F32)

        def head(cc, h):
            rows, lo = pl.ds(cc * CL, CL), h * DH
            st = state[h]
            sin_ref[cc, h] = st
            sb = st.astype(BF16)
            v_new = u_ref[rows, lo:lo + DH] - _dot(w_ref[rows, lo:lo + DH], sb)
            yield
            vb = v_new.astype(BF16)
            o = _dot(qg_ref[rows, lo:lo + DH], sb) + _dot(qk_ref[cc, h], vb)
            state[h] = st * cd_ref[cc, 0:1, h:h + 1] + _dot_tn(kd_ref[rows, lo:lo + DH], vb)
            yield
            o_ref[rows, lo:lo + DH] = o
            r = lax.rsqrt(jnp.mean(o * o, axis=-1, keepdims=True) + EPS)
            zh = z_ref[rows, lo:lo + DH]
            ob_ref[rows, lo:lo + DH] = o * r * nw_ref[...] * (zh * _sig(zh))

        for cc in range(CPS):
            _lockstep(head(cc, h) for h in range(NH))

    tile = lambda: pl.BlockSpec((TG, GW), lambda n: (n, 0))
    return pl.pallas_call(
        body, name="gdn_scan", grid=(NCH // CPS,),
        in_specs=[tile(), tile(), tile(), tile(), pl.BlockSpec((CPS, NH, CL, CL), lambda n: (n, 0, 0, 0)),
                  pl.BlockSpec((CPS, 8, LANES), lambda n: (n, 0, 0)), pl.BlockSpec((TG, GW), lambda n: (n, 5)),
                  _const((1, DH))],
        out_specs=(tile(), tile(), pl.BlockSpec((CPS, NH, DH, DH), lambda n: (n, 0, 0, 0))),
        out_shape=(jax.ShapeDtypeStruct((S, GW), F32), jax.ShapeDtypeStruct((S, GW), F32),
                   jax.ShapeDtypeStruct((NCH, NH, DH, DH), F32)),
        scratch_shapes=[pltpu.VMEM((NH, DH, DH), F32)],
        compiler_params=_params(dimension_semantics=("arbitrary",)),
    )(w_o, u_o, qg, kd, qk, cd, p_main, gdn_nw)


def _fwd_out(out_a, out_b, x, modnb, bada, w_out):
    def body(oa_ref, ob_ref, x_ref, mod_ref, b_ref, w_ref, x1_ref, mix_ref, oab_ref):
        oa = oa_ref[...].astype(BF16)
        ob = ob_ref[...].astype(BF16)
        oab_ref[:, 0:CW] = oa
        oab_ref[:, CW:D] = ob
        mix = _dot(oa, w_ref[0:CW, :]) + _dot(ob, w_ref[CW:D, :])
        mix_ref[...] = mix
        x1_ref[...] = x_ref[...] + _mod(mod_ref, b_ref, 2) * mix

    tile = lambda w: pl.BlockSpec((TM, w), lambda i: (i, 0))
    return pl.pallas_call(
        body, name="fwd_out", grid=(NT,),
        in_specs=[tile(CW), tile(GW), tile(D), _const((1, 6 * D)), _const((1, 6 * D)), _const((D, D))],
        out_specs=(tile(D), tile(D), tile(D)),
        out_shape=(jax.ShapeDtypeStruct((S, D), F32), jax.ShapeDtypeStruct((S, D), F32),
                   jax.ShapeDtypeStruct((S, D), BF16)),
        compiler_params=_params(dimension_semantics=("arbitrary",)),
    )(out_a, out_b, x, modnb, bada, w_out)


FFN_STATS = 8


def _ffn_forward(x1, tgt, modnb, bada, nw2, nfw, w_fi, w_fo):
    def body(x1_ref, tgt_ref, mod_ref, b_ref, nw2_ref, nfw_ref, wi_ref, wo_ref,
             hb_ref, act_ref, pre_ref, dx2_ref, dffn_ref, st_ref):
        i = pl.program_id(0)

        @pl.when(i == 0)
        def _():
            st_ref[...] = jnp.zeros((FFN_STATS, D), F32)

        sh2, sc2, gt2 = _mod(mod_ref, b_ref, 3), _mod(mod_ref, b_ref, 4), _mod(mod_ref, b_ref, 5)
        x1v = x1_ref[...]
        r2 = lax.rsqrt(jnp.mean(x1v * x1v, axis=-1, keepdims=True) + EPS)
        hb = ((x1v * r2 * nw2_ref[...]) * (1.0 + sc2) + sh2).astype(BF16)
        hb_ref[...] = hb
        ffn = jnp.zeros((TM, D), F32)
        for j in range(4):
            fgj = _dot_nt(hb, wi_ref[j])
            fuj = _dot_nt(hb, wi_ref[j + 4])
            pre_ref[j] = fgj.astype(BF16)
            pre_ref[j + 4] = fuj.astype(BF16)
            aj = (fgj * _sig(fgj) * fuj).astype(BF16)
            act_ref[j] = aj
            ffn = ffn + _dot(aj, wo_ref[j])
        x2 = x1v + gt2 * ffn
        r3 = lax.rsqrt(jnp.mean(x2 * x2, axis=-1, keepdims=True) + EPS)
        xr3 = x2 * r3
        err = xr3 * nfw_ref[...] - tgt_ref[...]
        loss = 0.5 * jnp.sum(jnp.mean(err * err, axis=-1, keepdims=True), axis=0, keepdims=True)
        dy = err * (1.0 / D)
        st_ref[0:1, :] += _colsum(dy * xr3)
        dyr = dy * nfw_ref[...]
        dx2 = r3 * (dyr - xr3 * jnp.mean(dyr * xr3, axis=-1, keepdims=True))
        st_ref[1:2, :] += _colsum(dx2 * ffn)
        st_ref[5:6, :] += jnp.broadcast_to(loss, (1, D))
        dx2_ref[...] = dx2
        dffn_ref[...] = (gt2 * dx2).astype(BF16)

    tile = lambda w: pl.BlockSpec((TM, w), lambda i: (i, 0))
    return pl.pallas_call(
        body, name="ffn_forward", grid=(NT,),
        in_specs=[tile(D), tile(D), _const((1, 6 * D)), _const((1, 6 * D)), _const((1, D)), _const((1, D)),
                  _const1((N_DEV, FB, D)), _const1((4, FB, D))],
        out_specs=(tile(D), pl.BlockSpec((4, TM, FB), lambda i: (0, i, 0)),
                   pl.BlockSpec((N_DEV, TM, FB), lambda i: (0, i, 0)), tile(D), tile(D), _const((FFN_STATS, D))),
        out_shape=(jax.ShapeDtypeStruct((S, D), BF16), jax.ShapeDtypeStruct((4, S, FB), BF16),
                   jax.ShapeDtypeStruct((N_DEV, S, FB), BF16), jax.ShapeDtypeStruct((S, D), F32),
                   jax.ShapeDtypeStruct((S, D), BF16), jax.ShapeDtypeStruct((FFN_STATS, D), F32)),
        compiler_params=_params(42, dimension_semantics=("arbitrary",)),
    )(x1, tgt, modnb, bada, nw2, nfw, w_fi, w_fo)


def _ffn_backward(dffn, pre, x1, dx2, modnb, bada, nw2, w_fi, w_fo):
    def body(dffn_ref, pre_ref, x1_ref, dx2_ref, mod_ref, b_ref, nw2_ref, wi_ref, wo_ref, df_ref, dx1_ref, st_ref):
        i = pl.program_id(0)

        @pl.when(i == 0)
        def _():
            st_ref[...] = jnp.zeros((FFN_STATS, D), F32)

        dffn = dffn_ref[...]
        dh = jnp.zeros((TM, D), F32)
        for j in range(4):
            fg = pre_ref[j].astype(F32)
            fu = pre_ref[j + 4].astype(F32)
            sg = _sig(fg)
            dact = _dot_nt(dffn, wo_ref[j])
            dfg = (dact * fu * (sg * (1.0 + fg * (1.0 - sg)))).astype(BF16)
            dfu = (dact * (fg * sg)).astype(BF16)
            df_ref[j] = dfg
            df_ref[j + 4] = dfu
            dh = dh + _dot(dfg, wi_ref[j]) + _dot(dfu, wi_ref[j + 4])
        x1v = x1_ref[...]
        r2 = lax.rsqrt(jnp.mean(x1v * x1v, axis=-1, keepdims=True) + EPS)
        xr2 = x1v * r2
        st_ref[2:3, :] += _colsum(dh)
        st_ref[3:4, :] += _colsum(dh * (xr2 * nw2_ref[...]))
        dxn = dh * (1.0 + _mod(mod_ref, b_ref, 4))
        st_ref[4:5, :] += _colsum(dxn * xr2)
        dxr = dxn * nw2_ref[...]
        dx1_ref[...] = dx2_ref[...] + r2 * (dxr - xr2 * jnp.mean(dxr * xr2, axis=-1, keepdims=True))

    tile = lambda w: pl.BlockSpec((TM, w), lambda i: (i, 0))
    wide = lambda: pl.BlockSpec((N_DEV, TM, FB), lambda i: (0, i, 0))
    return pl.pallas_call(
        body, name="ffn_backward", grid=(NT,),
        in_specs=[tile(D), wide(), tile(D), tile(D), _const((1, 6 * D)), _const((1, 6 * D)), _const((1, D)),
                  _const1((N_DEV, FB, D)), _const1((4, FB, D))],
        out_specs=(wide(), tile(D), _const((FFN_STATS, D))),
        out_shape=(jax.ShapeDtypeStruct((N_DEV, S, FB), BF16), jax.ShapeDtypeStruct((S, D), F32),
                   jax.ShapeDtypeStruct((FFN_STATS, D), F32)),
        compiler_params=_params(44, dimension_semantics=("arbitrary",)),
    )(dffn, pre, x1, dx2, modnb, bada, nw2, w_fi, w_fo)


def _grad_w(name, a, b, nb):
    m, n = a.shape[1], b.shape[1]

    def body(a_ref, b_ref, o_ref):
        o_ref[...] = _dot_tn(a_ref[...], b_ref[...]).astype(BF16)

    return pl.pallas_call(
        body, name=name, grid=(m // nb,),
        in_specs=[pl.BlockSpec((S, nb), lambda j: (0, j)), _const((S, n))],
        out_specs=pl.BlockSpec((nb, n), lambda j: (j, 0)),
        out_shape=jax.ShapeDtypeStruct((m, n), BF16),
        compiler_params=_params(dimension_semantics=("arbitrary",)),
    )(a, b)


def _grad_w_in(dp_conf, dp_gdn, dp_ba, hb1):
    nb = 512
    n_conf, n_gdn = 2 * CW // nb, 4 * GW // nb

    def body(c_ref, g_ref, ba_ref, h_ref, o_ref):
        j = pl.program_id(0)

        @pl.when(j < n_conf)
        def _():
            o_ref[...] = _dot_tn(c_ref[...], h_ref[...]).astype(BF16)

        @pl.when((j >= n_conf) & (j < n_conf + n_gdn))
        def _():
            o_ref[...] = _dot_tn(g_ref[...], h_ref[...]).astype(BF16)

        @pl.when(j == n_conf + n_gdn)
        def _():
            o_ref[0:2 * NH, :] = _dot_tn(ba_ref[...], h_ref[...])[0:2 * NH].astype(BF16)

    return pl.pallas_call(
        body, name="grad_w_in", grid=(n_conf + n_gdn + 1,),
        in_specs=[pl.BlockSpec((S, nb), lambda j: (0, jnp.minimum(j, n_conf - 1))),
                  pl.BlockSpec((S, nb), lambda j: (0, jnp.clip(j - n_conf, 0, n_gdn - 1))),
                  _const((S, LANES)), _const((S, D))],
        out_specs=pl.BlockSpec((nb, D), lambda j: (j, 0)),
        out_shape=jax.ShapeDtypeStruct((NIN, D), BF16),
        compiler_params=_params(dimension_semantics=("arbitrary",)),
    )(dp_conf, dp_gdn, dp_ba, hb1)


def _grad_w_ffn_in(hb2, df):
    def body(a_ref, b_ref, o_ref):
        o_ref[0] = _dot_tn(b_ref[0], a_ref[...]).astype(BF16)

    return pl.pallas_call(
        body, name="grad_w_ffn_in", grid=(N_DEV,),
        in_specs=[_const((S, D)), pl.BlockSpec((1, S, FB), lambda j: (j, 0, 0))],
        out_specs=pl.BlockSpec((1, FB, D), lambda j: (j, 0, 0)),
        out_shape=jax.ShapeDtypeStruct((N_DEV, FB, D), BF16),
        compiler_params=_params(dimension_semantics=("arbitrary",)),
    )(hb2, df)


def _grad_w_ffn_out(act, dffn):
    def body(a_ref, b_ref, o_ref):
        o_ref[0] = _dot_tn(a_ref[0], b_ref[...]).astype(BF16)

    return pl.pallas_call(
        body, name="grad_w_ffn_out", grid=(4,),
        in_specs=[pl.BlockSpec((1, S, FB), lambda j: (j, 0, 0)), _const((S, D))],
        out_specs=pl.BlockSpec((1, FB, D), lambda j: (j, 0, 0)),
        out_shape=jax.ShapeDtypeStruct((4, FB, D), BF16),
        compiler_params=_params(dimension_semantics=("arbitrary",)),
    )(act, dffn)


def _bwd_out(dx1, mix, modnb, bada, w_out):
    def body(dx_ref, mix_ref, mod_ref, b_ref, w_ref, dmix_ref, doa_ref, dob_ref, st_ref):
        i = pl.program_id(0)

        @pl.when(i == 0)
        def _():
            st_ref[...] = jnp.zeros((8, D), F32)

        dx = dx_ref[...]
        st_ref[0:1, :] += _colsum(dx * mix_ref[...])
        dmix = (_mod(mod_ref, b_ref, 2) * dx).astype(BF16)
        dmix_ref[...] = dmix
        doa_ref[...] = _dot_nt(dmix, w_ref[0:CW, :])
        dob_ref[...] = _dot_nt(dmix, w_ref[CW:D, :])

    tile = lambda w: pl.BlockSpec((TM, w), lambda i: (i, 0))
    return pl.pallas_call(
        body, name="bwd_out", grid=(NT,),
        in_specs=[tile(D), tile(D), _const((1, 6 * D)), _const((1, 6 * D)), _const((D, D))],
        out_specs=(tile(D), tile(CW), tile(GW), _const((8, D))),
        out_shape=(jax.ShapeDtypeStruct((S, D), BF16), jax.ShapeDtypeStruct((S, CW), F32),
                   jax.ShapeDtypeStruct((S, GW), F32), jax.ShapeDtypeStruct((8, D), F32)),
        compiler_params=_params(dimension_semantics=("arbitrary",)),
    )(dx1, mix, modnb, bada, w_out)


CONF_STATS = 40


def _conf_bwd(d_out_a, y, p_main, conv_w, gn_w, gn_b):
    def body(do_ref, y_ref, a_ref, g_ref, ah_ref, gh_ref, w_ref, gw_ref, gb_ref, dp_ref, st_ref,
             ubuf, dybuf, ush, dysh):
        i = pl.program_id(0)

        @pl.when(i == 0)
        def _():
            st_ref[...] = jnp.zeros((CONF_STATS, CW), F32)
            dybuf[TM:TM + HALO, :] = jnp.zeros((HALO, CW), F32)

        pm = _group_mean_matrix().astype(BF16)
        yv = y_ref[...]
        dlt = yv - _group_mean(yv, pm)
        rstd = lax.rsqrt(_group_mean(dlt * dlt, pm) + EPS)
        un = dlt * rstd
        o = un * gw_ref[...] + gb_ref[...]
        so = _sig(o)
        d_o = do_ref[...] * (so * (1.0 + o * (1.0 - so)))
        st_ref[33:34, :] += _colsum(d_o)
        st_ref[32:33, :] += _colsum(d_o * un)
        dun = d_o * gw_ref[...]
        dy = rstd * (dun - _group_mean(dun, pm) - un * _group_mean(dun * un, pm))
        st_ref[31:32, :] += _colsum(dy)
        dybuf[0:TM, :] = dy
        _fill_shifted(dybuf, dysh)

        a = a_ref[...]
        sg = _sig(g_ref[...])
        first = i == NT - 1
        ubuf[0:HALO, :] = jnp.where(first, 0.0, ah_ref[...] * _sig(gh_ref[...]))
        ubuf[HALO:HALO + TM, :] = a * sg
        _fill_shifted(ubuf, ush)
        du = jnp.zeros((TM, CW), F32)
        for k in range(KC):
            st_ref[k:k + 1, :] += _colsum(dy * _rows_at(ubuf, ush, HALO - (KC - 1) + k))
            du = du + w_ref[k:k + 1, :] * _rows_at(dybuf, dysh, KC - 1 - k)
        dybuf[TM:TM + HALO, :] = dybuf[0:HALO, :]
        dp_ref[:, 0:CW] = (du * sg).astype(BF16)
        dp_ref[:, CW:2 * CW] = (du * a * sg * (1.0 - sg)).astype(BF16)

    rev = lambda w, j=0: pl.BlockSpec((TM, w), lambda i: (NT - 1 - i, j))
    halo = lambda j: pl.BlockSpec((HALO, CW), lambda i: (jnp.maximum((NT - 1 - i) * (TM // HALO) - 1, 0), j))
    return pl.pallas_call(
        body, name="conf_bwd", grid=(NT,),
        in_specs=[rev(CW), rev(CW), rev(CW, 0), rev(CW, 1), halo(0), halo(1),
                  _const((KC, CW)), _const((1, CW)), _const((1, CW))],
        out_specs=(rev(2 * CW), _const((CONF_STATS, CW))),
        out_shape=(jax.ShapeDtypeStruct((S, 2 * CW), BF16), jax.ShapeDtypeStruct((CONF_STATS, CW), F32)),
        scratch_shapes=[pltpu.VMEM((HALO + TM, CW), F32), pltpu.VMEM((TM + HALO, CW), F32),
                        pltpu.VMEM((SUB - 1, SHIFT_ROWS, CW), F32), pltpu.VMEM((SUB - 1, SHIFT_ROWS, CW), F32)],
        compiler_params=_params(dimension_semantics=("arbitrary",)),
    )(d_out_a, y, p_main, p_main, p_main, p_main, conv_w, gn_w, gn_b)


GDN_STATS = 8


def _gdn_bwd(d_out_b, o_pre, s_in, t_inv, p_main, p_ba, gdn_conv_w, alog_l, dt_l, gdn_nw):
    def body(dob_ref, o_ref, sin_ref, t_ref, q_ref, k_ref, v_ref, z_ref, qh_ref, kh_ref, vh_ref, ba_ref,
             w_ref, al_ref, dt_ref, nw_ref, dp_ref, dba_ref, st_ref, xbuf, dcbuf, dstate):
        n = pl.program_id(0)

        @pl.when(n == 0)
        def _():
            st_ref[...] = jnp.zeros((GDN_STATS, 3 * GW), F32)
            dcbuf[CL:CL + SH, :] = jnp.zeros((SH, 3 * GW), F32)
            dstate[...] = jnp.zeros((NH, DH, DH), F32)

        for cc in reversed(range(CPS)):
            chunk(n, cc, dob_ref, o_ref, sin_ref, t_ref, q_ref, k_ref, v_ref, z_ref, qh_ref, kh_ref, vh_ref, ba_ref,
                  w_ref, al_ref, dt_ref, nw_ref, dp_ref, dba_ref, st_ref, xbuf, dcbuf, dstate)

    def chunk(n, cc, dob_ref, o_ref, sin_ref, t_ref, q_ref, k_ref, v_ref, z_ref, qh_ref, kh_ref, vh_ref, ba_ref,
              w_ref, al_ref, dt_ref, nw_ref, dp_ref, dba_ref, st_ref, xbuf, dcbuf, dstate):
        r0 = cc * CL
        if cc == 0:
            first = n == NCH // CPS - 1
            xbuf[0:SH, 0:GW] = jnp.where(first, 0.0, qh_ref[...])
            xbuf[0:SH, GW:2 * GW] = jnp.where(first, 0.0, kh_ref[...])
            xbuf[0:SH, 2 * GW:3 * GW] = jnp.where(first, 0.0, vh_ref[...])
        else:
            xbuf[0:SH, 0:GW] = q_ref[r0 - SH:r0, :]
            xbuf[0:SH, GW:2 * GW] = k_ref[r0 - SH:r0, :]
            xbuf[0:SH, 2 * GW:3 * GW] = v_ref[r0 - SH:r0, :]
        xbuf[SH:SH + CL, 0:GW] = q_ref[r0:r0 + CL, :]
        xbuf[SH:SH + CL, GW:2 * GW] = k_ref[r0:r0 + CL, :]
        xbuf[SH:SH + CL, 2 * GW:3 * GW] = v_ref[r0:r0 + CL, :]
        conv = _short_conv(w_ref, xbuf)
        sc = _sig(conv)
        qkv = conv * sc
        ba = ba_ref[r0:r0 + CL, :]
        beta_all, g_all, xg, neg_a = _gdn_gates(ba, al_ref[...], dt_ref[...])
        gcum, gcum_t = _gdn_cumsum(g_all)
        lane = lax.broadcasted_iota(jnp.int32, (CL, LANES), 1)
        row = lax.broadcasted_iota(jnp.int32, (CL, 1), 0)
        acc = dict(dgcum=jnp.zeros((CL, LANES), F32), dbeta=jnp.zeros((CL, LANES), F32))

        def head(h):
            lo = h * DH
            qh = qkv[:, lo:lo + DH]
            kh = qkv[:, GW + lo:GW + lo + DH]
            vh = qkv[:, 2 * GW + lo:2 * GW + lo + DH]
            beta = beta_all[:, h:h + 1]
            f = _head_terms(qh, kh, beta, gcum[:, NH + h:NH + h + 1], gcum_t[NH + h:NH + h + 1, :])
            qn, kn, qs, kb, gam, kds, cd, decay = (f[s] for s in ("qn", "kn", "qs", "kb", "gam", "kds", "cd", "decay"))
            t = t_ref[cc, h]
            st = sin_ref[cc, h]
            vb = vh * beta
            kbg = kb * gam
            u = _dot(t, vb, GP)
            w = _dot(t, kbg, GP)
            yield
            v_new = u - _dot(w, st, GP)
            q_dec = qs * gam
            k_dec = kn * kds

            o = o_ref[r0:r0 + CL, lo:lo + DH]
            zh = z_ref[r0:r0 + CL, lo:lo + DH]
            sz = _sig(zh)
            r = lax.rsqrt(jnp.mean(o * o, axis=-1, keepdims=True) + EPS)
            orr = o * r
            d_out = dob_ref[r0:r0 + CL, lo:lo + DH]
            dz = d_out * (orr * nw_ref[...]) * (sz * (1.0 + zh * (1.0 - sz)))
            don = d_out * (zh * sz)
            st_ref[4:5, 0:DH] += _colsum(don * orr)
            tt = don * nw_ref[...]
            d_o = r * (tt - orr * jnp.mean(tt * orr, axis=-1, keepdims=True))

            yield
            ds_out = dstate[h]
            dv_new = _dot_tn(f["qk"], d_o, GP) + _dot(k_dec, ds_out, GP)
            dqk = jnp.where(f["causal"], _dot_nt(d_o, v_new, GP), 0.0)
            dq_dec = _dot_nt(d_o, st, GP)
            dk_dec = _dot_nt(v_new, ds_out, GP)
            yield
            dstate[h] = _dot_tn(q_dec, d_o, GP) + cd * ds_out - _dot_tn(w, dv_new, GP)
            dcd = jnp.sum(_rowsum(st * ds_out), axis=0, keepdims=True)
            dw = -_dot_nt(dv_new, st, GP)
            dvb = _dot_tn(t, dv_new, GP)
            yield
            dt_m = _dot_nt(dv_new, vb, GP) + _dot_nt(dw, kbg, GP)
            dkbg = _dot_tn(t, dw, GP)
            yield
            dtt = _dot_nt(dt_m, t, GP)
            yield
            da = jnp.where(f["strict"], -_dot_tn(t, dtt, GP), 0.0)
            yield
            dad = da * decay
            dqkd = dqk * decay
            dkb = _dot(dad, kn, GP) + dkbg * gam
            dkn = _dot_tn(dad, kb, GP) + _dot_tn(dqkd, qs, GP) + dk_dec * kds + dkb * beta
            dqs = _dot(dqkd, kn, GP) + dq_dec * gam
            yield
            m = da * f["a"] + dqk * f["qk"]
            tk = _rowsum(dk_dec * k_dec)
            dgl = jnp.sum(tk, axis=0, keepdims=True) + dcd * cd
            dgc = (_rowsum(m) - _rowsum(jnp.transpose(m)) + _rowsum(dq_dec * q_dec) - tk + _rowsum(dkbg * kbg)
                   + jnp.where(row == CL - 1, dgl, 0.0))
            dbeta = _rowsum(dkb * kn) + _rowsum(dvb * vh)
            acc["dgcum"] = acc["dgcum"] + jnp.where(lane == NH + h, dgc, 0.0)
            acc["dbeta"] = acc["dbeta"] + jnp.where(lane == h, dbeta, 0.0)
            dvh = dvb * beta
            dqn = dqs * QSCALE
            dqh = f["rq"] * (dqn - qn * _rowsum(dqn * qn))
            dkh = f["rk"] * (dkn - kn * _rowsum(dkn * kn))
            dsilu = lambda c0: sc[:, c0:c0 + DH] * (1.0 + conv[:, c0:c0 + DH] * (1.0 - sc[:, c0:c0 + DH]))
            dcbuf[0:CL, lo:lo + DH] = dqh * dsilu(lo)
            dcbuf[0:CL, GW + lo:GW + lo + DH] = dkh * dsilu(GW + lo)
            dcbuf[0:CL, 2 * GW + lo:2 * GW + lo + DH] = dvh * dsilu(2 * GW + lo)
            dp_ref[r0:r0 + CL, 3 * GW + lo:3 * GW + lo + DH] = dz.astype(BF16)

        _lockstep(head(h) for h in range(NH))
        dgcum_all, dbeta_all = acc["dgcum"], acc["dbeta"]

        ii, jj = _tri_iota()
        upper = jnp.where(ii <= jj, 1.0, 0.0).astype(BF16)
        dg_all = _ones_dot(upper, dgcum_all)
        dxg = dg_all * neg_a * _sig(xg)
        st_ref[5:6, 0:LANES] += _colsum(dg_all * g_all)
        st_ref[6:7, 0:LANES] += _colsum(dxg)
        dbl = dbeta_all * beta_all * (1.0 - beta_all)
        dba_ref[r0:r0 + CL, :] = jnp.where(lane < NH, dbl, jnp.where(lane < 2 * NH, dxg, 0.0)).astype(BF16)

        dconv = dcbuf[0:CL, :]
        dx = w_ref[0:1, :] * dcbuf[KS - 1:KS - 1 + CL, :]
        st_ref[0:1, :] += _colsum(dconv * xbuf[SH - KS + 1:SH - KS + 1 + CL, :])
        for k in range(1, KS):
            off = SH - (KS - 1) + k
            st_ref[k:k + 1, :] += _colsum(dconv * xbuf[off:off + CL, :])
            dx = dx + w_ref[k:k + 1, :] * dcbuf[KS - 1 - k:KS - 1 - k + CL, :]
        dcbuf[CL:CL + SH, :] = dcbuf[0:SH, :]
        dp_ref[r0:r0 + CL, 0:3 * GW] = dx.astype(BF16)

    steps = NCH // CPS
    rev = lambda w, j=0: pl.BlockSpec((TG, w), lambda n: (steps - 1 - n, j))
    halo = lambda j: pl.BlockSpec((SH, GW), lambda n: (jnp.maximum((steps - 1 - n) * (TG // SH) - 1, 0), j))
    blk4 = lambda a, b: pl.BlockSpec((CPS, NH, a, b), lambda n: (steps - 1 - n, 0, 0, 0))
    return pl.pallas_call(
        body, name="gdn_bwd", grid=(steps,),
        in_specs=[rev(GW), rev(GW), blk4(DH, DH), blk4(CL, CL), rev(GW, 2), rev(GW, 3), rev(GW, 4), rev(GW, 5),
                  halo(2), halo(3), halo(4), rev(LANES), _const((KS, 3 * GW)), _const((1, LANES)),
                  _const((1, LANES)), _const((1, DH))],
        out_specs=(rev(4 * GW), rev(LANES), _const((GDN_STATS, 3 * GW))),
        out_shape=(jax.ShapeDtypeStruct((S, 4 * GW), BF16), jax.ShapeDtypeStruct((S, LANES), BF16),
                   jax.ShapeDtypeStruct((GDN_STATS, 3 * GW), F32)),
        scratch_shapes=[pltpu.VMEM((SH + CL, 3 * GW), F32), pltpu.VMEM((CL + SH, 3 * GW), F32),
                        pltpu.VMEM((NH, DH, DH), F32)],
        compiler_params=_params(dimension_semantics=("arbitrary",)),
    )(d_out_b, o_pre, s_in, t_inv, p_main, p_main, p_main, p_main, p_main, p_main, p_main, p_ba,
      gdn_conv_w, alog_l, dt_l, gdn_nw)


def _bwd_in(dp_conf, dp_gdn, dp_ba, x, dx1, nw1, modnb, bada, w_main, w_ba):
    def body(dc_ref, dg_ref, db_ref, x_ref, dx1_ref, nw_ref, mod_ref, b_ref, wm_ref, wb_ref, gx_ref, st_ref):
        i = pl.program_id(0)

        @pl.when(i == 0)
        def _():
            st_ref[...] = jnp.zeros((8, D), F32)

        dh = (_dot(dc_ref[...], wm_ref[0:2 * CW, :]) + _dot(dg_ref[...], wm_ref[2 * CW:NMAIN, :])
              + _dot(db_ref[...], wb_ref[...]))
        xv = x_ref[...]
        r = lax.rsqrt(jnp.mean(xv * xv, axis=-1, keepdims=True) + EPS)
        xr = xv * r
        st_ref[0:1, :] += _colsum(dh)
        st_ref[1:2, :] += _colsum(dh * (xr * nw_ref[...]))
        dxn = dh * (1.0 + _mod(mod_ref, b_ref, 1))
        st_ref[2:3, :] += _colsum(dxn * xr)
        dxr = dxn * nw_ref[...]
        gx_ref[...] = dx1_ref[...] + r * (dxr - xr * jnp.mean(dxr * xr, axis=-1, keepdims=True))

    tile = lambda w: pl.BlockSpec((TI, w), lambda i: (i, 0))
    return pl.pallas_call(
        body, name="bwd_in", grid=(S // TI,),
        in_specs=[tile(2 * CW), tile(4 * GW), tile(LANES), tile(D), tile(D), _const((1, D)), _const((1, 6 * D)),
                  _const((1, 6 * D)), _const1((NMAIN, D)), _const((LANES, D))],
        out_specs=(tile(D), _const((8, D))),
        out_shape=(jax.ShapeDtypeStruct((S, D), F32), jax.ShapeDtypeStruct((8, D), F32)),
        compiler_params=_params(dimension_semantics=("arbitrary",)),
    )(dp_conf, dp_gdn, dp_ba, x, dx1, nw1, modnb, bada, w_main, w_ba)


def _adamw(w, g, m, v):
    m = ADAM_B1 * m + (1.0 - ADAM_B1) * g
    v = ADAM_B2 * v + (1.0 - ADAM_B2) * (g * g)
    m_hat = m / BC1
    v_hat = v / BC2
    delta = -ADAM_LR * (m_hat / (jnp.sqrt(v_hat) + ADAM_EPS) + ADAM_WD * w)
    return delta, m, v


ADAM_BLOCK_BYTES = 6 * 1024 * 1024


def _adam_tile(rows, cols):
    padded = -(-cols // LANES) * LANES
    if N_DEV * rows * padded * 4 <= ADAM_BLOCK_BYTES:
        return rows, cols
    best = None
    for tr in range(16, rows, 16):
        if rows % tr == 0 and N_DEV * tr * padded * 4 <= ADAM_BLOCK_BYTES:
            best = tr
    if best is not None:
        return best, cols
    rows_padded = -(-rows // 16) * 16
    tc = LANES
    for cand in range(LANES, cols, LANES):
        if cols % cand == 0 and N_DEV * rows_padded * cand * 4 <= ADAM_BLOCK_BYTES:
            tc = cand
    return rows, tc


def _reduce_adam(name, parts, w, m, v, own=None):
    rows, cols = w.shape
    tr, tc = _adam_tile(rows, cols)

    def body(*refs):
        p_ref, w_ref, m_ref, v_ref = refs[:4]
        g_ref, d_ref, nm_ref, nv_ref = refs[-4:]
        if own is None:
            part = lambda j: p_ref[j].astype(F32)
        else:
            me = 4 * lax.axis_index("x") + 2 * lax.axis_index("y") + lax.axis_index("c")
            part = lambda j: jnp.where(me == j, refs[4][...], p_ref[j]).astype(F32)
        g = part(0)
        for j in range(1, N_DEV):
            g = g + part(j)
        g_ref[...] = g
        d_ref[...], nm_ref[...], nv_ref[...] = _adamw(w_ref[...], g, m_ref[...], v_ref[...])

    blk = pl.BlockSpec((tr, tc), lambda i, j: (i, j))
    sds = jax.ShapeDtypeStruct((rows, cols), F32)
    extra = [] if own is None else [own]
    return pl.pallas_call(
        body, name=name, grid=(rows // tr, cols // tc),
        in_specs=[pl.BlockSpec((N_DEV, tr, tc), lambda i, j: (0, i, j)), blk, blk, blk] + [blk] * len(extra),
        out_specs=(blk, blk, blk, blk), out_shape=(sds, sds, sds, sds),
        compiler_params=_params(dimension_semantics=("arbitrary", "arbitrary")),
    )(parts, w, m, v, *extra)


def _ada_adam(c_all, dmod_sh, w, m, v):
    rows, cols = w.shape
    tr = 256

    def body(c_ref, dm_ref, w_ref, m_ref, v_ref, g_ref, d_ref, nm_ref, nv_ref):
        cv = c_ref[...]
        g = _dot_tn(cv * _sig(cv), dm_ref[...], HI)
        g_ref[...] = g
        d_ref[...], nm_ref[...], nv_ref[...] = _adamw(w_ref[...], g, m_ref[...], v_ref[...])

    blk = pl.BlockSpec((tr, cols), lambda i: (i, 0))
    sds = jax.ShapeDtypeStruct((rows, cols), F32)
    return pl.pallas_call(
        body, name="ada_adam", grid=(rows // tr,),
        in_specs=[pl.BlockSpec((N_DEV, tr), lambda i: (0, i)), _const((N_DEV, cols)), blk, blk, blk],
        out_specs=(blk, blk, blk, blk), out_shape=(sds, sds, sds, sds),
        compiler_params=_params(dimension_semantics=("arbitrary",)),
    )(c_all, dmod_sh, w, m, v)


def _lanes(a, at=0):
    return jnp.pad(a, ((0, 0), (at, LANES - at - a.shape[1])))


WEIGHT_NAMES = ["w_ada", "b_ada", "norm_mix_w", "w_in", "conv_w", "conv_b", "conv_gn_w", "conv_gn_b", "gdn_conv_w",
                "gdn_a_log", "gdn_dt_bias", "gdn_norm_w", "w_out", "norm_ffn_w", "w_ffn_in", "w_ffn_out",
                "norm_final_w"]


SMALL_LAYOUT = [("b_ada", 0, 48, LANES), ("norm_mix_w", 48, 8, LANES), ("norm_ffn_w", 56, 8, LANES),
                ("norm_final_w", 64, 8, LANES), ("conv_b", 72, 4, LANES), ("conv_gn_w", 76, 4, LANES),
                ("conv_gn_b", 80, 4, LANES), ("gdn_norm_w", 84, 1, LANES), ("gdn_a_log", 85, 1, NH),
                ("gdn_dt_bias", 86, 1, NH)]
LOSS_ROW = 87


def _adam_small(g_small, weights, m1, m2):
    names = [nm for nm, _, _, _ in SMALL_LAYOUT]
    k = len(names)

    def body(*refs):
        g_ref = refs[0]
        w_refs, m_refs, v_refs = refs[1:1 + k], refs[1 + k:1 + 2 * k], refs[1 + 2 * k:1 + 3 * k]
        loss_ref = refs[1 + 3 * k]
        outs = refs[2 + 3 * k:2 + 7 * k]
        total = refs[-1]
        g = g_ref[0]
        for j in range(1, N_DEV):
            g = g + g_ref[j]
        total[...] = g
        loss_ref[...] = total[LOSS_ROW:LOSS_ROW + 1, :]
        for i, (_, r0, rows, lanes) in enumerate(SMALL_LAYOUT):
            gp = total[r0:r0 + rows, 0:lanes]
            outs[i][...] = gp
            outs[k + i][...], outs[2 * k + i][...], outs[3 * k + i][...] = _adamw(
                w_refs[i][...], gp, m_refs[i][...], v_refs[i][...])

    shapes = [jax.ShapeDtypeStruct((rows, lanes), F32) for _, _, rows, lanes in SMALL_LAYOUT]
    res = pl.pallas_call(
        body, name="adam_small",
        out_shape=tuple([jax.ShapeDtypeStruct((1, LANES), F32)] + shapes * 4),
        scratch_shapes=[pltpu.VMEM((SMALL_ROWS, LANES), F32)],
        compiler_params=_params(),
    )(g_small, *[weights[n] for n in names], *[m1[n] for n in names], *[m2[n] for n in names])
    kinds = [dict(zip(names, res[1 + q * k:1 + (q + 1) * k])) for q in range(4)]
    return res[0], kinds


def _mix_forward(w, xs, modnb, between=None):
    w_main = w["w_in"]
    w_ba = jnp.pad(w["w_in"][NMAIN:], ((0, LANES - 2 * NH), (0, 0)))
    alog_l = _lanes(w["gdn_a_log"], NH)
    dt_l = _lanes(w["gdn_dt_bias"], NH)
    p_main, p_ba, hb1 = _fwd_in(xs, w["norm_mix_w"], modnb, w["b_ada"], w_main, w_ba)
    w_o, u_o, qg, kd, qk, cd, t_inv = _gdn_prep(p_main, p_ba, w["gdn_conv_w"], alog_l, dt_l)
    out_b, o_pre, s_in = _gdn_scan(w_o, u_o, qg, kd, qk, cd, p_main, w["gdn_norm_w"])
    conv_b = w["conv_b"] if between is None else _after(w["conv_b"], between(out_b))
    y_conv, out_a = _conf_fwd(p_main, w["conv_w"], conv_b, w["conv_gn_w"], w["conv_gn_b"])
    return dict(w_main=w_main, w_ba=w_ba, alog_l=alog_l, dt_l=dt_l, p_main=p_main, p_ba=p_ba, hb1=hb1,
                y_conv=y_conv, out_a=out_a, out_b=out_b, o_pre=o_pre, s_in=s_in, t_inv=t_inv)


def _ffn_stage(w, f, xs, tgt, modnb):
    x1, mix, oab = _fwd_out(f["out_a"], f["out_b"], xs, modnb, w["b_ada"], w["w_out"])
    hb2, act, pre, dx2, dffn, st_fwd = _ffn_forward(x1, tgt, modnb, w["b_ada"], w["norm_ffn_w"],
                                                    w["norm_final_w"], w["w_ffn_in"], w["w_ffn_out"])
    gw_ffn_out = _grad_w_ffn_out(act, dffn)
    df, dx1, st_bwd = _ffn_backward(dffn, pre, x1, dx2, modnb, w["b_ada"], w["norm_ffn_w"], w["w_ffn_in"],
                                    w["w_ffn_out"])
    gw_ffn_in = _grad_w_ffn_in(hb2, df)
    return dict(mix=mix, oab=oab, dx1=dx1, st_ffn=st_fwd + st_bwd, gw_ffn_in=gw_ffn_in, gw_ffn_out=gw_ffn_out)


def _out_backward(w, g, modnb):
    dmix, d_out_a, d_out_b, st_out = _bwd_out(g["dx1"], g["mix"], modnb, w["b_ada"], w["w_out"])
    return dict(d_out_a=d_out_a, d_out_b=d_out_b, st_out=st_out, gw_out=_grad_w("grad_w_out", g["oab"], dmix, 512))


def _heads_backward(w, f, a):
    dp_conf, st_conf = _conf_bwd(a["d_out_a"], f["y_conv"], f["p_main"], w["conv_w"], w["conv_gn_w"],
                                 w["conv_gn_b"])
    dp_gdn, dp_ba, st_gdn = _gdn_bwd(a["d_out_b"], f["o_pre"], f["s_in"], f["t_inv"], f["p_main"], f["p_ba"],
                                     w["gdn_conv_w"], f["alog_l"], f["dt_l"], w["gdn_norm_w"])
    gw_in = _grad_w_in(dp_conf, dp_gdn, dp_ba, f["hb1"])
    return dict(dp_conf=dp_conf, dp_gdn=dp_gdn, dp_ba=dp_ba, st_conf=st_conf, st_gdn=st_gdn, gw_in=gw_in,
                gw_conv=st_conf[0:KC], gw_gconv=st_gdn[0:KS])


def _in_backward(w, f, g, a, h, xs, modnb):
    st_out, st_conf, st_gdn, st_ffn = a["st_out"], h["st_conf"], h["st_gdn"], g["st_ffn"]
    grad_x, st_in = _bwd_in(h["dp_conf"], h["dp_gdn"], h["dp_ba"], xs, g["dx1"], w["norm_mix_w"], modnb,
                            w["b_ada"], f["w_main"], f["w_ba"])
    dmod = jnp.concatenate([st_in[0:1], st_in[1:2], st_out[0:1], st_ffn[2:3], st_ffn[3:4], st_ffn[1:2]], axis=1)
    small = jnp.concatenate([
        dmod.reshape(48, LANES), st_in[2:3].reshape(8, LANES), st_ffn[4:5].reshape(8, LANES),
        st_ffn[0:1].reshape(8, LANES), st_conf[31:32].reshape(4, LANES), st_conf[32:33].reshape(4, LANES),
        st_conf[33:34].reshape(4, LANES), st_gdn[4:5, 0:LANES],
        _lanes(st_gdn[5:6, NH:2 * NH]), _lanes(st_gdn[6:7, NH:2 * NH]), st_ffn[5:6, 0:LANES]], axis=0)
    return dict(grad_x=grad_x, small=small)


def _local(w, xs, tgt, modnb):
    f = _mix_forward(w, xs, modnb)
    g = _ffn_stage(w, f, xs, tgt, modnb)
    a = _out_backward(w, g, modnb)
    h = _heads_backward(w, f, a)
    b = _in_backward(w, f, g, a, h, xs, modnb)
    return dict(b, gw_in=h["gw_in"], gw_conv=h["gw_conv"], gw_gconv=h["gw_gconv"], gw_out=a["gw_out"],
                gw_ffn_in=g["gw_ffn_in"], gw_ffn_out=g["gw_ffn_out"])


def kernel(x, c, w_ada, b_ada, norm_mix_w, w_in, conv_w, conv_b, conv_gn_w, conv_gn_b, gdn_conv_w, gdn_a_log, gdn_dt_bias, gdn_norm_w, w_out, norm_ffn_w, w_ffn_in, w_ffn_out, norm_final_w, loss_target, m_w_ada, m_b_ada, m_norm_mix_w, m_w_in, m_conv_w, m_conv_b, m_conv_gn_w, m_conv_gn_b, m_gdn_conv_w, m_gdn_a_log, m_gdn_dt_bias, m_gdn_norm_w, m_w_out, m_norm_ffn_w, m_w_ffn_in, m_w_ffn_out, m_norm_final_w, v_w_ada, v_b_ada, v_norm_mix_w, v_w_in, v_conv_w, v_conv_b, v_conv_gn_w, v_conv_gn_b, v_gdn_conv_w, v_gdn_a_log, v_gdn_dt_bias, v_gdn_norm_w, v_w_out, v_norm_ffn_w, v_w_ffn_in, v_w_ffn_out, v_norm_final_w):
    me = 4 * lax.axis_index("x") + 2 * lax.axis_index("y") + lax.axis_index("c")
    xs = x.reshape(S, D)
    tgt = loss_target.reshape(S, D)

    late = [w_out[0].astype(BF16), jnp.transpose(w_ffn_in[0]).astype(BF16), w_ffn_out[0].astype(BF16)]
    g_c, g_cw, g_gcw, g_win, *late_lands = _gather_two_level(
        "gather_weights", [c, conv_w[0], gdn_conv_w[0], jnp.transpose(w_in[0]).astype(BF16)] + late,
        seed_only=(4, 5, 6))
    c_all = g_c.reshape(N_DEV, D)
    g_mod, mod_token = _exchange("gather_mod", [_mod_shard(c_all, w_ada[0])], [False], with_token=True)
    modnb = lax.dynamic_index_in_dim(g_mod, me, axis=1, keepdims=False).reshape(1, 6 * D)
    late_started = _exchange_start("gather_late_start", [_after(late[0], mod_token)] + late[1:], late_lands,
                                   [False] * 3, only=LEVEL_ONE)
    modnb = _after(modnb, late_started[-1])
    w = dict(b_ada=b_ada, norm_mix_w=norm_mix_w, conv_b=conv_b, conv_gn_w=conv_gn_w, conv_gn_b=conv_gn_b,
             gdn_a_log=gdn_a_log, gdn_dt_bias=gdn_dt_bias, gdn_norm_w=gdn_norm_w, norm_ffn_w=norm_ffn_w,
             norm_final_w=norm_final_w.reshape(1, D),
             conv_w=jnp.transpose(g_cw, (1, 0, 2)).reshape(KC, CW),
             gdn_conv_w=jnp.transpose(g_gcw, (1, 0, 2)).reshape(KS, 3 * GW),
             w_in=g_win.reshape(NIN, D))

    relay = {}

    def relay_late(out_b):
        _, late_landed = _exchange_wait("gather_late_wait", late_started, [False] * 3, (out_b,), only=LEVEL_ONE)
        relay["started"] = _relay_start("gather_late_relay_start", late_landed)
        return relay["started"][-1]

    f = _mix_forward(w, xs, modnb, relay_late)
    g_wout, g_wfi, g_wfo = _relay_wait("gather_late_relay_wait", relay["started"], (f["out_a"],))
    w.update(w_out=g_wout.reshape(D, D), w_ffn_in=g_wfi, w_ffn_out=g_wfo.reshape(4, FB, D))
    g = _ffn_stage(w, f, xs, tgt, modnb)

    ffn_grads = [g["gw_ffn_in"], g["gw_ffn_out"].reshape(N_DEV, DFF // N_DEV, D)]
    ffn_started = _exchange_start("scatter_ffn_start", ffn_grads,
                                  [lax.empty(a.shape, a.dtype) for a in ffn_grads], [True] * 2)
    a = _out_backward(w, g, _after(modnb, ffn_started[-1]))
    out_grads = [a["gw_out"].reshape(N_DEV, D // N_DEV, D)]
    out_started = _exchange_start("scatter_out_start", out_grads,
                                  [lax.empty(t.shape, t.dtype) for t in out_grads], [True])
    h = _heads_backward(dict(w, conv_gn_w=_after(w["conv_gn_w"], out_started[-1])), f, a)

    in_grads = [h["gw_in"].reshape(N_DEV, NIN // N_DEV, D),
                jnp.transpose(h["gw_conv"].reshape(KC, N_DEV, CW // N_DEV), (1, 0, 2)),
                jnp.transpose(h["gw_gconv"].reshape(KS, N_DEV, 3 * GW // N_DEV), (1, 0, 2))]
    in_started = _exchange_start("scatter_in_start", in_grads,
                                 [lax.empty(t.shape, t.dtype) for t in in_grads], [True] * 3)
    loc = _in_backward(w, f, g, a, h, xs, _after(modnb, in_started[-1]))
    small_started = _exchange_start("gather_small_start", [loc["small"]],
                                    [lax.empty((N_DEV, SMALL_ROWS, LANES), F32)], [False])

    def own(sent):
        return lax.dynamic_index_in_dim(sent, me, axis=0, keepdims=False)

    big = {}
    (sent_fi, sent_fo), (r_fi, r_fo) = _exchange_wait("scatter_ffn_wait", ffn_started, [True] * 2,
                                                         (small_started[-1],))
    big["w_ffn_in"] = [jnp.transpose(t) for t in _reduce_adam(
        "adam_w_ffn_in", r_fi, jnp.transpose(w_ffn_in[0]), jnp.transpose(m_w_ffn_in[0]),
        jnp.transpose(v_w_ffn_in[0]), own(sent_fi))]
    big["w_ffn_out"] = _reduce_adam("adam_w_ffn_out", r_fo, w_ffn_out[0], m_w_ffn_out[0], v_w_ffn_out[0],
                                    own(sent_fo))
    (sent_out,), (r_out,) = _exchange_wait("scatter_out_wait", out_started, [True], (big["w_ffn_out"][0],))
    big["w_out"] = _reduce_adam("adam_w_out", r_out, w_out[0], m_w_out[0], v_w_out[0], own(sent_out))

    (sent_small,), (r_small,) = _exchange_wait("gather_small_wait", small_started, [False], (big["w_out"][0],))
    slot = lax.broadcasted_iota(jnp.int32, (N_DEV, 1, 1), 0)
    g_small = jnp.where(slot == me, sent_small[None], r_small)
    def views(b_, nm_, nf_, nl_, cb_, gw_, gb_, gn_, al_, dt_):
        arrs = [b_, nm_, nf_, nl_, cb_, gw_, gb_, gn_, al_, dt_]
        return {nm: t.reshape(rows, lanes) for (nm, _, rows, lanes), t in zip(SMALL_LAYOUT, arrs)}

    loss_row, res = _adam_small(
        g_small,
        views(b_ada, norm_mix_w, norm_ffn_w, norm_final_w, conv_b, conv_gn_w, conv_gn_b, gdn_norm_w, gdn_a_log,
              gdn_dt_bias),
        views(m_b_ada, m_norm_mix_w, m_norm_ffn_w, m_norm_final_w, m_conv_b, m_conv_gn_w, m_conv_gn_b,
              m_gdn_norm_w, m_gdn_a_log, m_gdn_dt_bias),
        views(v_b_ada, v_norm_mix_w, v_norm_ffn_w, v_norm_final_w, v_conv_b, v_conv_gn_w, v_conv_gn_b,
              v_gdn_norm_w, v_gdn_a_log, v_gdn_dt_bias))
    loss = loss_row[0, 0]
    small_shapes = dict(b_ada=(1, 6 * D), norm_mix_w=(1, D), norm_ffn_w=(1, D), norm_final_w=(D,),
                        conv_b=(1, CW), conv_gn_w=(1, CW), conv_gn_b=(1, CW), gdn_norm_w=(1, DH),
                        gdn_a_log=(1, NH), gdn_dt_bias=(1, NH))
    res = [{nm: t.reshape(small_shapes[nm]) for nm, t in kind.items()} for kind in res]

    dmod_rows = g_small[:, 0:48, :].reshape(N_DEV, 6 * D)
    dmod_sh = lax.dynamic_slice_in_dim(dmod_rows, me * (6 * D // N_DEV), 6 * D // N_DEV, axis=1)

    big["w_ada"] = _ada_adam(c_all, dmod_sh, w_ada[0], m_w_ada[0], v_w_ada[0])
    (sent_in, sent_cw, sent_gcw), (r_in, r_cw, r_gcw) = _exchange_wait(
        "scatter_in_wait", in_started, [True] * 3, (big["w_ada"][0],))
    big["w_in"] = [jnp.transpose(t) for t in _reduce_adam(
        "adam_w_in", r_in, jnp.transpose(w_in[0]), jnp.transpose(m_w_in[0]), jnp.transpose(v_w_in[0]),
        own(sent_in))]
    big["conv_w"] = _reduce_adam("adam_conv_w", r_cw, conv_w[0], m_conv_w[0], v_conv_w[0], own(sent_cw))
    big["gdn_conv_w"] = _reduce_adam("adam_gdn_conv_w", r_gcw, gdn_conv_w[0], m_gdn_conv_w[0], v_gdn_conv_w[0],
                                     own(sent_gcw))
    outs = [loss, loc["grad_x"].reshape(1, S, D)]
    for kind in range(4):
        for nm in WEIGHT_NAMES:
            outs.append(big[nm][kind][None] if nm in big else res[kind][nm])
    return tuple(outs)
```

```python
import functools

import jax
import jax.numpy as jnp
from jax import lax
from jax.experimental import pallas as pl
from jax.experimental.pallas import tpu as pltpu

F32 = jnp.float32
BF16 = jnp.bfloat16
HI = lax.Precision.HIGHEST
MESH = pl.DeviceIdType.MESH

N_DEV = 8
S = 2048
D = 1024
TM = 256
NT = S // TM
CW = 512
KC = 31
NG = 8
GSZ = CW // NG
HALO = 32
GW = 512
NH = 4
DH = 128
KS = 4
SH = 8
CL = 64
NCH = S // CL
NMAIN = 2 * CW + 4 * GW
NIN = NMAIN + 2 * NH
DFF = 2816
FB = DFF // 4
EPS = 1e-6
QSCALE = DH ** -0.5
LANES = 128
SMALL_ROWS = 88

ADAM_LR = 0.001
ADAM_B1 = 0.9
ADAM_B2 = 0.999
ADAM_EPS = 1e-08
ADAM_WD = 0.01
ADAM_STEP = 10
BC1 = 1.0 - ADAM_B1 ** ADAM_STEP
BC2 = 1.0 - ADAM_B2 ** ADAM_STEP

MIB = 1024 * 1024
VMEM_LIMIT_MIB = 32


def _params(limit_mib=VMEM_LIMIT_MIB, **kw):
    return pltpu.CompilerParams(vmem_limit_bytes=limit_mib * MIB, **kw)


def _sig(x):
    return jax.nn.sigmoid(x)


GP = BF16


def _operands(a, b, prec):
    if prec is BF16:
        return a.astype(BF16), b.astype(BF16), None
    return a, b, prec


def _dot(a, b, prec=None):
    a, b, prec = _operands(a, b, prec)
    return jnp.dot(a, b, preferred_element_type=F32, precision=prec)


def _dot_nt(a, b, prec=None):
    a, b, prec = _operands(a, b, prec)
    return lax.dot_general(a, b, (((1,), (1,)), ((), ())), preferred_element_type=F32, precision=prec)


def _dot_tn(a, b, prec=None):
    a, b, prec = _operands(a, b, prec)
    return lax.dot_general(a, b, (((0,), (0,)), ((), ())), preferred_element_type=F32, precision=prec)


def _lockstep(gens):
    gens = list(gens)
    while gens:
        alive = []
        for g in gens:
            try:
                next(g)
                alive.append(g)
            except StopIteration:
                pass
        gens = alive


def _rowsum(x):
    return jnp.sum(x, axis=-1, keepdims=True)


def _colsum(x):
    return jnp.sum(x, axis=0, keepdims=True)


def _mod(mod_ref, b_ref, k):
    return mod_ref[:, k * D:(k + 1) * D] + b_ref[:, k * D:(k + 1) * D]


def _const(shape):
    nd = len(shape)
    return pl.BlockSpec(shape, lambda *_: (0,) * nd)


def _const1(shape):
    nd = len(shape)
    return pl.BlockSpec(shape, lambda *_: (0,) * nd, pipeline_mode=pl.Buffered(1))


PEER_FLIPS = [(dx, dy, dc) for dx in (0, 1) for dy in (0, 1) for dc in (0, 1)][1:]


def _after(x, token):
    return x + token[0:1, 0:1].astype(x.dtype).reshape((1,) * x.ndim)


def _exchange(name, srcs, per_dest, seed_only=(), with_token=False):
    n = len(srcs)
    out_shape = []
    for a, pd in zip(srcs, per_dest):
        blk = a.shape[1:] if pd else a.shape
        out_shape.append(jax.ShapeDtypeStruct((N_DEV,) + tuple(blk), a.dtype))

    def body(*refs):
        src = refs[:n]
        dst = refs[n:2 * n]
        send_sems, recv_sems, local_sems = refs[-3:]
        if with_token:
            refs[2 * n][...] = jnp.zeros((8, LANES), F32)
        x, y, c = lax.axis_index("x"), lax.axis_index("y"), lax.axis_index("c")
        me = 4 * x + 2 * y + c

        def piece(i, j):
            return src[i].at[j] if per_dest[i] else src[i]

        copies = []
        for k, (dx, dy, dc) in enumerate(PEER_FLIPS):
            px = 1 - x if dx else x
            py = 1 - y if dy else y
            pc = 1 - c if dc else c
            pj = 4 * px + 2 * py + pc
            for i in range(n):
                if i in seed_only:
                    continue
                cp = pltpu.make_async_remote_copy(
                    src_ref=piece(i, pj), dst_ref=dst[i].at[me],
                    send_sem=send_sems.at[k * n + i], recv_sem=recv_sems.at[k * n + i],
                    device_id=(px, py, pc), device_id_type=MESH)
                cp.start()
                arrive = pltpu.make_async_remote_copy(
                    src_ref=piece(i, pj), dst_ref=dst[i].at[pj],
                    send_sem=send_sems.at[k * n + i], recv_sem=recv_sems.at[k * n + i],
                    device_id=(px, py, pc), device_id_type=MESH)
                copies.append((cp, arrive))
        own = []
        for i in range(n):
            lc = pltpu.make_async_copy(piece(i, me), dst[i].at[me], local_sems.at[i])
            lc.start()
            own.append(lc)
        for cp, arrive in copies:
            arrive.wait_recv()
        for cp, arrive in copies:
            cp.wait_send()
        for lc in own:
            lc.wait()

    any_spec = pl.BlockSpec(memory_space=pl.ANY)
    out_specs = [any_spec] * n
    if with_token:
        out_shape.append(jax.ShapeDtypeStruct((8, LANES), F32))
        out_specs.append(pl.BlockSpec(memory_space=pltpu.VMEM))
    return pl.pallas_call(
        body, name=name, out_shape=tuple(out_shape),
        in_specs=[any_spec] * n, out_specs=tuple(out_specs),
        scratch_shapes=[pltpu.SemaphoreType.DMA((7 * n,)), pltpu.SemaphoreType.DMA((7 * n,)),
                        pltpu.SemaphoreType.DMA((n,))],
        compiler_params=pltpu.CompilerParams(has_side_effects=True),
    )(*srcs)


CHIP_FLIPS = [(0, 1), (1, 0), (1, 1)]
LEVEL_ONE = [k for k, (dx, dy, dc) in enumerate(PEER_FLIPS) if (dx, dy, dc) == (0, 0, 1) or dc == 0]


def _chip_peers(x, y):
    return [(1 - x if dx else x, 1 - y if dy else y) for dx, dy in CHIP_FLIPS]


def _gather_two_level(name, srcs, seed_only=()):
    n = len(srcs)
    live = [i for i in range(n) if i not in seed_only]

    def body(*refs):
        src, dst = refs[:n], refs[n:2 * n]
        send_sems, recv_sems, local_sems = refs[2 * n:2 * n + 3]
        bounce = refs[2 * n + 3:]
        x, y, c = lax.axis_index("x"), lax.axis_index("y"), lax.axis_index("c")
        me = 4 * x + 2 * y + c
        sibling = (x, y, 1 - c)
        chips = _chip_peers(x, y)

        def copy(k, i, src_ref, slot, to):
            return pltpu.make_async_remote_copy(
                src_ref=src_ref, dst_ref=dst[i].at[slot], send_sem=send_sems.at[k * n + i],
                recv_sem=recv_sems.at[k * n + i], device_id=to, device_id_type=MESH)

        first = []
        for i in live:
            first.append(copy(0, i, src[i], me, sibling))
            first += [copy(1 + j, i, src[i], me, (px, py, c)) for j, (px, py) in enumerate(chips)]
        for cp in first:
            cp.start()
        up = [pltpu.make_async_copy(src[i], bounce[i], local_sems.at[i]) for i in range(n)]
        for cp in up:
            cp.start()
        for cp in up:
            cp.wait()
        own = [pltpu.make_async_copy(bounce[i], dst[i].at[me], local_sems.at[i]) for i in range(n)]
        for cp in own:
            cp.start()
        passed = []
        for j, (px, py) in enumerate(chips):
            slot = 4 * px + 2 * py + c
            for i in live:
                copy(1 + j, i, src[i], slot, (px, py, c)).wait_recv()
                fwd = copy(4 + j, i, dst[i].at[slot], slot, sibling)
                fwd.start()
                passed.append(fwd)
        for i in live:
            copy(0, i, src[i], 4 * x + 2 * y + 1 - c, sibling).wait_recv()
            for j, (px, py) in enumerate(chips):
                copy(4 + j, i, src[i], 4 * px + 2 * py + 1 - c, sibling).wait_recv()
        for cp in first + passed:
            cp.wait_send()
        for cp in own:
            cp.wait()

    any_spec = pl.BlockSpec(memory_space=pl.ANY)
    return pl.pallas_call(
        body, name=name, out_shape=tuple(jax.ShapeDtypeStruct((N_DEV,) + a.shape, a.dtype) for a in srcs),
        in_specs=[any_spec] * n, out_specs=tuple([any_spec] * n),
        scratch_shapes=[pltpu.SemaphoreType.DMA((7 * n,)), pltpu.SemaphoreType.DMA((7 * n,)),
                        pltpu.SemaphoreType.DMA((n,))] + [pltpu.VMEM(a.shape, a.dtype) for a in srcs],
        compiler_params=pltpu.CompilerParams(has_side_effects=True),
    )(*srcs)


def _relay_copy(land, sems, i, n, j, slot, sibling):
    send_sems, recv_sems = sems
    return pltpu.make_async_remote_copy(
        src_ref=land[i].at[slot], dst_ref=land[i].at[slot], send_sem=send_sems.at[j * n + i],
        recv_sem=recv_sems.at[j * n + i], device_id=sibling, device_id_type=MESH)


def _relay_start(name, lands):
    n = len(lands)

    def body(*refs):
        land = refs[:n]
        sems = refs[n], refs[n + 1]
        x, y, c = lax.axis_index("x"), lax.axis_index("y"), lax.axis_index("c")
        for j, (px, py) in enumerate(_chip_peers(x, y)):
            for i in range(n):
                _relay_copy(land, sems, i, n, j, 4 * px + 2 * py + c, (x, y, 1 - c)).start()
        refs[-1][...] = jnp.zeros((8, LANES), F32)

    return pl.pallas_call(
        body, name=name,
        out_shape=(pltpu.SemaphoreType.DMA((3 * n,)), pltpu.SemaphoreType.DMA((3 * n,)),
                   *[pltpu.HBM(a.shape, a.dtype) for a in lands], jax.ShapeDtypeStruct((8, LANES), F32)),
        in_specs=[HBM_SPEC] * n,
        out_specs=(SEM_SPEC, SEM_SPEC, *[HBM_SPEC] * n, pl.BlockSpec(memory_space=pltpu.VMEM)),
        input_output_aliases={i: 2 + i for i in range(n)},
        compiler_params=pltpu.CompilerParams(has_side_effects=DATAFLOW),
    )(*[pltpu.with_memory_space_constraint(a, pltpu.HBM) for a in lands])


def _relay_wait(name, started, after):
    n = len(started) - 3
    arrays = list(started[2:2 + n])

    def body(*refs):
        land = refs[:n]
        sems = refs[n], refs[n + 1]
        x, y, c = lax.axis_index("x"), lax.axis_index("y"), lax.axis_index("c")
        for j, (px, py) in enumerate(_chip_peers(x, y)):
            for i in range(n):
                _relay_copy(land, sems, i, n, j, 4 * px + 2 * py + c, (x, y, 1 - c)).wait_send()
                _relay_copy(land, sems, i, n, j, 4 * px + 2 * py + 1 - c, (x, y, 1 - c)).wait_recv()

    return pl.pallas_call(
        body, name=name,
        out_shape=tuple(pltpu.HBM(a.shape, a.dtype) for a in arrays),
        in_specs=[HBM_SPEC] * n + [SEM_SPEC, SEM_SPEC] + [pl.BlockSpec(memory_space=pl.ANY)] * len(after),
        out_specs=tuple([HBM_SPEC] * n),
        input_output_aliases={i: i for i in range(n)},
        compiler_params=pltpu.CompilerParams(has_side_effects=DATAFLOW),
    )(*arrays, started[0], started[1], *after)


HBM_SPEC = pl.BlockSpec(memory_space=pltpu.HBM)
SEM_SPEC = pl.BlockSpec(memory_space=pltpu.SEMAPHORE)
DATAFLOW = pltpu.SideEffectType.DATAFLOW_SIDE_EFFECTING


def _peers(only=None):
    x, y, c = lax.axis_index("x"), lax.axis_index("y"), lax.axis_index("c")
    out = []
    for k, (dx, dy, dc) in enumerate(PEER_FLIPS):
        if only is not None and k not in only:
            continue
        px = 1 - x if dx else x
        py = 1 - y if dy else y
        pc = 1 - c if dc else c
        out.append((k, (px, py, pc), 4 * px + 2 * py + pc))
    return 4 * x + 2 * y + c, out


def _exchange_start(name, srcs, lands, per_dest, only=None):
    n = len(srcs)

    def body(*refs):
        src, land = refs[:n], refs[n:2 * n]
        send_sems, recv_sems = refs[2 * n], refs[2 * n + 1]
        token = refs[-1]
        me, peers = _peers(only)
        for k, peer, pj in peers:
            for i in range(n):
                pltpu.make_async_remote_copy(
                    src_ref=src[i].at[pj] if per_dest[i] else src[i], dst_ref=land[i].at[me],
                    send_sem=send_sems.at[k * n + i], recv_sem=recv_sems.at[k * n + i],
                    device_id=peer, device_id_type=MESH).start()
        token[...] = jnp.zeros((8, LANES), F32)

    arrays = list(srcs) + list(lands)
    return pl.pallas_call(
        body, name=name,
        out_shape=(pltpu.SemaphoreType.DMA((7 * n,)), pltpu.SemaphoreType.DMA((7 * n,)),
                   *[pltpu.HBM(a.shape, a.dtype) for a in arrays], jax.ShapeDtypeStruct((8, LANES), F32)),
        in_specs=[HBM_SPEC] * (2 * n),
        out_specs=(SEM_SPEC, SEM_SPEC, *[HBM_SPEC] * (2 * n), pl.BlockSpec(memory_space=pltpu.VMEM)),
        input_output_aliases={i: 2 + i for i in range(2 * n)},
        compiler_params=pltpu.CompilerParams(has_side_effects=DATAFLOW),
    )(*[pltpu.with_memory_space_constraint(a, pltpu.HBM) for a in arrays])


def _exchange_wait(name, started, per_dest, after, only=None):
    n = (len(started) - 3) // 2
    send_sems, recv_sems = started[0], started[1]
    arrays = list(started[2:2 + 2 * n])

    def body(*refs):
        src, land = refs[:n], refs[n:2 * n]
        send, recv = refs[2 * n], refs[2 * n + 1]
        me, peers = _peers(only)
        for k, peer, pj in peers:
            for i in range(n):
                cp = pltpu.make_async_remote_copy(
                    src_ref=src[i].at[pj] if per_dest[i] else src[i], dst_ref=land[i].at[pj],
                    send_sem=send.at[k * n + i], recv_sem=recv.at[k * n + i],
                    device_id=peer, device_id_type=MESH)
                cp.wait_send()
                cp.wait_recv()

    outs = pl.pallas_call(
        body, name=name,
        out_shape=tuple(pltpu.HBM(a.shape, a.dtype) for a in arrays),
        in_specs=[HBM_SPEC] * (2 * n) + [SEM_SPEC, SEM_SPEC] + [pl.BlockSpec(memory_space=pl.ANY)] * len(after),
        out_specs=tuple([HBM_SPEC] * (2 * n)),
        input_output_aliases={i: i for i in range(2 * n)},
        compiler_params=pltpu.CompilerParams(has_side_effects=DATAFLOW),
    )(*arrays, send_sems, recv_sems, *after)
    return outs[:n], outs[n:]


def _mod_shard(c_all, w_ada):
    def body(c_ref, w_ref, o_ref):
        cv = c_ref[...]
        ca = cv * _sig(cv)
        o_ref[...] = _dot(ca.astype(BF16), w_ref[...].astype(BF16))

    return pl.pallas_call(
        body, name="mod_shard", out_shape=jax.ShapeDtypeStruct((N_DEV, w_ada.shape[1]), F32),
        compiler_params=_params(),
    )(c_all, w_ada)


TI = 512


def _fwd_in(x, nw1, modnb, bada, w_main, w_ba):
    def body(x_ref, nw_ref, mod_ref, b_ref, wm_ref, wb_ref, pm_ref, pb_ref, hb_ref):
        xv = x_ref[...]
        r = lax.rsqrt(jnp.mean(xv * xv, axis=-1, keepdims=True) + EPS)
        h = (xv * r * nw_ref[...]) * (1.0 + _mod(mod_ref, b_ref, 1)) + _mod(mod_ref, b_ref, 0)
        hb = h.astype(BF16)
        hb_ref[...] = hb
        pm_ref[...] = _dot_nt(hb, wm_ref[...])
        pb_ref[...] = _dot_nt(hb, wb_ref[...])

    return pl.pallas_call(
        body, name="fwd_in", grid=(S // TI,),
        in_specs=[pl.BlockSpec((TI, D), lambda i: (i, 0)), _const((1, D)), _const((1, 6 * D)), _const((1, 6 * D)),
                  _const1((NMAIN, D)), _const((LANES, D))],
        out_specs=(pl.BlockSpec((TI, NMAIN), lambda i: (i, 0)), pl.BlockSpec((TI, LANES), lambda i: (i, 0)),
                   pl.BlockSpec((TI, D), lambda i: (i, 0))),
        out_shape=(jax.ShapeDtypeStruct((S, NMAIN), F32), jax.ShapeDtypeStruct((S, LANES), F32),
                   jax.ShapeDtypeStruct((S, D), BF16)),
        compiler_params=_params(dimension_semantics=("arbitrary",)),
    )(x, nw1, modnb, bada, w_main, w_ba)


def _group_mean_matrix():
    ii = lax.broadcasted_iota(jnp.int32, (CW, CW), 0) // GSZ
    jj = lax.broadcasted_iota(jnp.int32, (CW, CW), 1) // GSZ
    return jnp.where(ii == jj, 1.0 / GSZ, 0.0).astype(F32)


SUB = 8
SHIFT_ROWS = HALO + TM - SUB


def _fill_shifted(buf, sh):
    for b in range(1, SUB):
        sh[b - 1] = buf[b:b + SHIFT_ROWS, :]


def _rows_at(buf, sh, off):
    a, b = divmod(off, SUB)
    if b == 0:
        return buf[off:off + TM, :]
    return sh[b - 1, SUB * a:SUB * a + TM, :]


def _group_mean(x, pm):
    hi = x.astype(BF16)
    r1 = x - hi.astype(F32)
    mid = r1.astype(BF16)
    lo = (r1 - mid.astype(F32)).astype(BF16)
    return _dot(hi, pm) + _dot(mid, pm) + _dot(lo, pm)


def _conf_fwd(p_main, conv_w, conv_b, gn_w, gn_b):
    def body(a_ref, g_ref, w_ref, b_ref, gw_ref, gb_ref, y_ref, oa_ref, ubuf, ush):
        i = pl.program_id(0)

        @pl.when(i == 0)
        def _():
            ubuf[0:HALO, :] = jnp.zeros((HALO, CW), F32)

        ubuf[HALO:HALO + TM, :] = a_ref[...] * _sig(g_ref[...])
        _fill_shifted(ubuf, ush)
        acc = jnp.zeros((TM, CW), F32) + b_ref[...]
        for k in range(KC):
            acc = acc + w_ref[k:k + 1, :] * _rows_at(ubuf, ush, HALO - (KC - 1) + k)
        y_ref[...] = acc
        ubuf[0:HALO, :] = ubuf[TM:TM + HALO, :]
        pm = _group_mean_matrix().astype(BF16)
        dlt = acc - _group_mean(acc, pm)
        var = _group_mean(dlt * dlt, pm)
        o = dlt * lax.rsqrt(var + EPS) * gw_ref[...] + gb_ref[...]
        oa_ref[...] = o * _sig(o)

    return pl.pallas_call(
        body, name="conf_fwd", grid=(NT,),
        in_specs=[pl.BlockSpec((TM, CW), lambda i: (i, 0)), pl.BlockSpec((TM, CW), lambda i: (i, 1)),
                  _const((KC, CW)), _const((1, CW)), _const((1, CW)), _const((1, CW))],
        out_specs=(pl.BlockSpec((TM, CW), lambda i: (i, 0)), pl.BlockSpec((TM, CW), lambda i: (i, 0))),
        out_shape=(jax.ShapeDtypeStruct((S, CW), F32), jax.ShapeDtypeStruct((S, CW), F32)),
        scratch_shapes=[pltpu.VMEM((HALO + TM, CW), F32), pltpu.VMEM((SUB - 1, SHIFT_ROWS, CW), F32)],
        compiler_params=_params(dimension_semantics=("arbitrary",)),
    )(p_main, p_main, conv_w, conv_b, gn_w, gn_b)


def _tri_iota():
    ii = lax.broadcasted_iota(jnp.int32, (CL, CL), 0)
    jj = lax.broadcasted_iota(jnp.int32, (CL, CL), 1)
    return ii, jj


def _gdn_gates(ba, alog_l, dt_l):
    beta_all = _sig(ba)
    xg = ba + dt_l
    sp = jnp.maximum(xg, 0.0) + jnp.log(1.0 + jnp.exp(-jnp.abs(xg)))
    neg_a = -jnp.exp(alog_l)
    return beta_all, neg_a * sp, xg, neg_a


def _ones_dot(ones, x):
    hi = x.astype(BF16)
    r1 = x - hi.astype(F32)
    mid = r1.astype(BF16)
    lo = (r1 - mid.astype(F32)).astype(BF16)
    return _dot(ones, hi) + _dot(ones, mid) + _dot(ones, lo)


def _gdn_cumsum(g_all):
    ii, jj = _tri_iota()
    low = jnp.where(ii >= jj, 1.0, 0.0).astype(BF16)
    gcum = _ones_dot(low, g_all)
    return gcum, jnp.transpose(gcum)


def _split(x):
    hi = x.astype(BF16)
    return hi, (x - hi.astype(F32)).astype(BF16)


def _dot_split(a, b):
    (ah, al), (bh, bl) = a, b
    return _dot(ah, bh) + (_dot(ah, bl) + _dot(al, bh))


def _unit_lower_inverses(mats):
    ii, jj = _tri_iota()
    eye = jnp.where(ii == jj, 1.0, 0.0).astype(F32)
    ts = [eye - a for a in mats]
    ps = [_dot_split(s, s) for s in map(_split, mats)]
    for _ in range(4):
        sp = [_split(p) for p in ps]
        ts = [t + _dot_split(_split(t), s) for t, s in zip(ts, sp)]
        ps = [_dot_split(s, s) for s in sp]
    return [t + _dot_split(_split(t), _split(p)) for t, p in zip(ts, ps)]


def _head_terms(qh, kh, beta, gcol, grow):
    ii, jj = _tri_iota()
    causal = ii >= jj
    strict = ii > jj
    rq = lax.rsqrt(_rowsum(qh * qh) + EPS)
    rk = lax.rsqrt(_rowsum(kh * kh) + EPS)
    qn = qh * rq
    kn = kh * rk
    qs = qn * QSCALE
    decay = jnp.where(causal, jnp.exp(jnp.where(causal, gcol - grow, 0.0)), 0.0)
    gam = jnp.exp(gcol)
    gl = gcol[CL - 1:CL, :]
    kds = jnp.exp(gl - gcol)
    cd = jnp.exp(gl)
    kb = kn * beta
    a = jnp.where(strict, _dot_nt(kb, kn, GP) * decay, 0.0)
    qk = jnp.where(causal, _dot_nt(qs, kn, GP) * decay, 0.0)
    return dict(rq=rq, rk=rk, qn=qn, kn=kn, qs=qs, decay=decay, gam=gam, kds=kds, cd=cd, kb=kb, a=a, qk=qk,
                causal=causal, strict=strict)


def _short_conv(w_ref, buf, rows=CL):
    acc = w_ref[0:1, :] * buf[SH - KS + 1:SH - KS + 1 + rows, :]
    for k in range(1, KS):
        off = SH - (KS - 1) + k
        acc = acc + w_ref[k:k + 1, :] * buf[off:off + rows, :]
    return acc


CPS = 4
TG = CPS * CL


def _gdn_prep(p_main, p_ba, gdn_conv_w, alog_l, dt_l):
    def body(q_ref, k_ref, v_ref, qh_ref, kh_ref, vh_ref, ba_ref, w_ref, al_ref, dt_ref,
             wo_ref, uo_ref, qg_ref, kd_ref, qk_ref, cd_ref, t_ref, xbuf):
        i = pl.program_id(0)
        first = i == 0
        xbuf[0:SH, 0:GW] = jnp.where(first, 0.0, qh_ref[...])
        xbuf[0:SH, GW:2 * GW] = jnp.where(first, 0.0, kh_ref[...])
        xbuf[0:SH, 2 * GW:3 * GW] = jnp.where(first, 0.0, vh_ref[...])
        xbuf[SH:SH + TG, 0:GW] = q_ref[...]
        xbuf[SH:SH + TG, GW:2 * GW] = k_ref[...]
        xbuf[SH:SH + TG, 2 * GW:3 * GW] = v_ref[...]
        conv = _short_conv(w_ref, xbuf, TG)
        qkv = conv * _sig(conv)
        beta_all, g_all, _, _ = _gdn_gates(ba_ref[...], al_ref[...], dt_ref[...])
        lane = lax.broadcasted_iota(jnp.int32, (8, LANES), 1)
        cums = [_gdn_cumsum(g_all[cc * CL:(cc + 1) * CL, :]) for cc in range(CPS)]
        pairs = [(cc, h) for cc in range(CPS) for h in range(NH)]
        terms, vbs = [], []
        for cc, h in pairs:
            r0, lo = cc * CL, h * DH
            beta = beta_all[r0:r0 + CL, h:h + 1]
            gcum, gcum_t = cums[cc]
            terms.append(_head_terms(qkv[r0:r0 + CL, lo:lo + DH], qkv[r0:r0 + CL, GW + lo:GW + lo + DH], beta,
                                     gcum[:, NH + h:NH + h + 1], gcum_t[NH + h:NH + h + 1, :]))
            vbs.append(qkv[r0:r0 + CL, 2 * GW + lo:2 * GW + lo + DH] * beta)
        invs = _unit_lower_inverses([f["a"] for f in terms])
        cds = [jnp.zeros((8, LANES), F32) for _ in range(CPS)]
        for (cc, h), f, t, vb in zip(pairs, terms, invs, vbs):
            r0, lo = cc * CL, h * DH
            t_ref[cc, h] = t
            uo_ref[r0:r0 + CL, lo:lo + DH] = _dot(t, vb, GP)
            wo_ref[r0:r0 + CL, lo:lo + DH] = _dot(t, f["kb"] * f["gam"], GP).astype(BF16)
            qg_ref[r0:r0 + CL, lo:lo + DH] = (f["qs"] * f["gam"]).astype(BF16)
            kd_ref[r0:r0 + CL, lo:lo + DH] = (f["kn"] * f["kds"]).astype(BF16)
            qk_ref[cc, h] = f["qk"].astype(BF16)
            cds[cc] = cds[cc] + jnp.where(lane == h, f["cd"], 0.0)
        for cc in range(CPS):
            cd_ref[cc] = cds[cc]

    col = lambda j: pl.BlockSpec((TG, GW), lambda i: (i, j))
    halo = lambda j: pl.BlockSpec((SH, GW), lambda i: (jnp.maximum(i * (TG // SH) - 1, 0), j))
    tile = lambda: pl.BlockSpec((TG, GW), lambda i: (i, 0))
    sq = lambda: pl.BlockSpec((CPS, NH, CL, CL), lambda i: (i, 0, 0, 0))
    return pl.pallas_call(
        body, name="gdn_prep", grid=(NCH // CPS,),
        in_specs=[col(2), col(3), col(4), halo(2), halo(3), halo(4), pl.BlockSpec((TG, LANES), lambda i: (i, 0)),
                  _const((KS, 3 * GW)), _const((1, LANES)), _const((1, LANES))],
        out_specs=(tile(), tile(), tile(), tile(), sq(), pl.BlockSpec((CPS, 8, LANES), lambda i: (i, 0, 0)), sq()),
        out_shape=(jax.ShapeDtypeStruct((S, GW), BF16), jax.ShapeDtypeStruct((S, GW), F32),
                   jax.ShapeDtypeStruct((S, GW), BF16), jax.ShapeDtypeStruct((S, GW), BF16),
                   jax.ShapeDtypeStruct((NCH, NH, CL, CL), BF16), jax.ShapeDtypeStruct((NCH, 8, LANES), F32),
                   jax.ShapeDtypeStruct((NCH, NH, CL, CL), F32)),
        scratch_shapes=[pltpu.VMEM((SH + TG, 3 * GW), F32)],
        compiler_params=_params(dimension_semantics=("arbitrary",)),
    )(p_main, p_main, p_main, p_main, p_main, p_main, p_ba, gdn_conv_w, alog_l, dt_l)


def _gdn_scan(w_o, u_o, qg, kd, qk, cd, p_main, gdn_nw):
    def body(w_ref, u_ref, qg_ref, kd_ref, qk_ref, cd_ref, z_ref, nw_ref, ob_ref, o_ref, sin_ref, state):
        n = pl.program_id(0)

        @pl.when(n == 0)
        def _():
            state[...] = jnp.zeros((NH, DH, DH), F32)

        def head(cc, h):
            rows, lo = pl.ds(cc * CL, CL), h * DH
            st = state[h]
            sin_ref[cc, h] = st
            sb = st.astype(BF16)
            v_new = u_ref[rows, lo:lo + DH] - _dot(w_ref[rows, lo:lo + DH], sb)
            yield
            vb = v_new.astype(BF16)
            o = _dot(qg_ref[rows, lo:lo + DH], sb) + _dot(qk_ref[cc, h], vb)
            state[h] = st * cd_ref[cc, 0:1, h:h + 1] + _dot_tn(kd_ref[rows, lo:lo + DH], vb)
            yield
            o_ref[rows, lo:lo + DH] = o
            r = lax.rsqrt(jnp.mean(o * o, axis=-1, keepdims=True) + EPS)
            zh = z_ref[rows, lo:lo + DH]
            ob_ref[rows, lo:lo + DH] = o * r * nw_ref[...] * (zh * _sig(zh))

        for cc in range(CPS):
            _lockstep(head(cc, h) for h in range(NH))

    tile = lambda: pl.BlockSpec((TG, GW), lambda n: (n, 0))
    return pl.pallas_call(
        body, name="gdn_scan", grid=(NCH // CPS,),
        in_specs=[tile(), tile(), tile(), tile(), pl.BlockSpec((CPS, NH, CL, CL), lambda n: (n, 0, 0, 0)),
                  pl.BlockSpec((CPS, 8, LANES), lambda n: (n, 0, 0)), pl.BlockSpec((TG, GW), lambda n: (n, 5)),
                  _const((1, DH))],
        out_specs=(tile(), tile(), pl.BlockSpec((CPS, NH, DH, DH), lambda n: (n, 0, 0, 0))),
        out_shape=(jax.ShapeDtypeStruct((S, GW), F32), jax.ShapeDtypeStruct((S, GW), F32),
                   jax.ShapeDtypeStruct((NCH, NH, DH, DH), F32)),
        scratch_shapes=[pltpu.VMEM((NH, DH, DH), F32)],
        compiler_params=_params(dimension_semantics=("arbitrary",)),
    )(w_o, u_o, qg, kd, qk, cd, p_main, gdn_nw)


def _fwd_out(out_a, out_b, x, modnb, bada, w_out):
    def body(oa_ref, ob_ref, x_ref, mod_ref, b_ref, w_ref, x1_ref, mix_ref, oab_ref):
        oa = oa_ref[...].astype(BF16)
        ob = ob_ref[...].astype(BF16)
        oab_ref[:, 0:CW] = oa
        oab_ref[:, CW:D] = ob
        mix = _dot(oa, w_ref[0:CW, :]) + _dot(ob, w_ref[CW:D, :])
        mix_ref[...] = mix
        x1_ref[...] = x_ref[...] + _mod(mod_ref, b_ref, 2) * mix

    tile = lambda w: pl.BlockSpec((TM, w), lambda i: (i, 0))
    return pl.pallas_call(
        body, name="fwd_out", grid=(NT,),
        in_specs=[tile(CW), tile(GW), tile(D), _const((1, 6 * D)), _const((1, 6 * D)), _const((D, D))],
        out_specs=(tile(D), tile(D), tile(D)),
        out_shape=(jax.ShapeDtypeStruct((S, D), F32), jax.ShapeDtypeStruct((S, D), F32),
                   jax.ShapeDtypeStruct((S, D), BF16)),
        compiler_params=_params(dimension_semantics=("arbitrary",)),
    )(out_a, out_b, x, modnb, bada, w_out)


FFN_STATS = 8


def _ffn_forward(x1, tgt, modnb, bada, nw2, nfw, w_fi, w_fo):
    def body(x1_ref, tgt_ref, mod_ref, b_ref, nw2_ref, nfw_ref, wi_ref, wo_ref,
             hb_ref, act_ref, pre_ref, dx2_ref, dffn_ref, st_ref):
        i = pl.program_id(0)

        @pl.when(i == 0)
        def _():
            st_ref[...] = jnp.zeros((FFN_STATS, D), F32)

        sh2, sc2, gt2 = _mod(mod_ref, b_ref, 3), _mod(mod_ref, b_ref, 4), _mod(mod_ref, b_ref, 5)
        x1v = x1_ref[...]
        r2 = lax.rsqrt(jnp.mean(x1v * x1v, axis=-1, keepdims=True) + EPS)
        hb = ((x1v * r2 * nw2_ref[...]) * (1.0 + sc2) + sh2).astype(BF16)
        hb_ref[...] = hb
        ffn = jnp.zeros((TM, D), F32)
        for j in range(4):
            fgj = _dot_nt(hb, wi_ref[j])
            fuj = _dot_nt(hb, wi_ref[j + 4])
            pre_ref[j] = fgj.astype(BF16)
            pre_ref[j + 4] = fuj.astype(BF16)
            aj = (fgj * _sig(fgj) * fuj).astype(BF16)
            act_ref[j] = aj
            ffn = ffn + _dot(aj, wo_ref[j])
        x2 = x1v + gt2 * ffn
        r3 = lax.rsqrt(jnp.mean(x2 * x2, axis=-1, keepdims=True) + EPS)
        xr3 = x2 * r3
        err = xr3 * nfw_ref[...] - tgt_ref[...]
        loss = 0.5 * jnp.sum(jnp.mean(err * err, axis=-1, keepdims=True), axis=0, keepdims=True)
        dy = err * (1.0 / D)
        st_ref[0:1, :] += _colsum(dy * xr3)
        dyr = dy * nfw_ref[...]
        dx2 = r3 * (dyr - xr3 * jnp.mean(dyr * xr3, axis=-1, keepdims=True))
        st_ref[1:2, :] += _colsum(dx2 * ffn)
        st_ref[5:6, :] += jnp.broadcast_to(loss, (1, D))
        dx2_ref[...] = dx2
        dffn_ref[...] = (gt2 * dx2).astype(BF16)

    tile = lambda w: pl.BlockSpec((TM, w), lambda i: (i, 0))
    return pl.pallas_call(
        body, name="ffn_forward", grid=(NT,),
        in_specs=[tile(D), tile(D), _const((1, 6 * D)), _const((1, 6 * D)), _const((1, D)), _const((1, D)),
                  _const1((N_DEV, FB, D)), _const1((4, FB, D))],
        out_specs=(tile(D), pl.BlockSpec((4, TM, FB), lambda i: (0, i, 0)),
                   pl.BlockSpec((N_DEV, TM, FB), lambda i: (0, i, 0)), tile(D), tile(D), _const((FFN_STATS, D))),
        out_shape=(jax.ShapeDtypeStruct((S, D), BF16), jax.ShapeDtypeStruct((4, S, FB), BF16),
                   jax.ShapeDtypeStruct((N_DEV, S, FB), BF16), jax.ShapeDtypeStruct((S, D), F32),
                   jax.ShapeDtypeStruct((S, D), BF16), jax.ShapeDtypeStruct((FFN_STATS, D), F32)),
        compiler_params=_params(42, dimension_semantics=("arbitrary",)),
    )(x1, tgt, modnb, bada, nw2, nfw, w_fi, w_fo)


def _ffn_backward(dffn, pre, x1, dx2, modnb, bada, nw2, w_fi, w_fo):
    def body(dffn_ref, pre_ref, x1_ref, dx2_ref, mod_ref, b_ref, nw2_ref, wi_ref, wo_ref, df_ref, dx1_ref, st_ref):
        i = pl.program_id(0)

        @pl.when(i == 0)
        def _():
            st_ref[...] = jnp.zeros((FFN_STATS, D), F32)

        dffn = dffn_ref[...]
        dh = jnp.zeros((TM, D), F32)
        for j in range(4):
            fg = pre_ref[j].astype(F32)
            fu = pre_ref[j + 4].astype(F32)
            sg = _sig(fg)
            dact = _dot_nt(dffn, wo_ref[j])
            dfg = (dact * fu * (sg * (1.0 + fg * (1.0 - sg)))).astype(BF16)
            dfu = (dact * (fg * sg)).astype(BF16)
            df_ref[j] = dfg
            df_ref[j + 4] = dfu
            dh = dh + _dot(dfg, wi_ref[j]) + _dot(dfu, wi_ref[j + 4])
        x1v = x1_ref[...]
        r2 = lax.rsqrt(jnp.mean(x1v * x1v, axis=-1, keepdims=True) + EPS)
        xr2 = x1v * r2
        st_ref[2:3, :] += _colsum(dh)
        st_ref[3:4, :] += _colsum(dh * (xr2 * nw2_ref[...]))
        dxn = dh * (1.0 + _mod(mod_ref, b_ref, 4))
        st_ref[4:5, :] += _colsum(dxn * xr2)
        dxr = dxn * nw2_ref[...]
        dx1_ref[...] = dx2_ref[...] + r2 * (dxr - xr2 * jnp.mean(dxr * xr2, axis=-1, keepdims=True))

    tile = lambda w: pl.BlockSpec((TM, w), lambda i: (i, 0))
    wide = lambda: pl.BlockSpec((N_DEV, TM, FB), lambda i: (0, i, 0))
    return pl.pallas_call(
        body, name="ffn_backward", grid=(NT,),
        in_specs=[tile(D), wide(), tile(D), tile(D), _const((1, 6 * D)), _const((1, 6 * D)), _const((1, D)),
                  _const1((N_DEV, FB, D)), _const1((4, FB, D))],
        out_specs=(wide(), tile(D), _const((FFN_STATS, D))),
        out_shape=(jax.ShapeDtypeStruct((N_DEV, S, FB), BF16), jax.ShapeDtypeStruct((S, D), F32),
                   jax.ShapeDtypeStruct((FFN_STATS, D), F32)),
        compiler_params=_params(44, dimension_semantics=("arbitrary",)),
    )(dffn, pre, x1, dx2, modnb, bada, nw2, w_fi, w_fo)


def _grad_w_in(dp_conf, dp_gdn, dp_ba, hb1):
    nb = 512
    n_conf, n_gdn = 2 * CW // nb, 4 * GW // nb

    def body(c_ref, g_ref, ba_ref, h_ref, o_ref):
        j = pl.program_id(0)

        @pl.when(j < n_conf)
        def _():
            o_ref[...] = _dot_tn(c_ref[...], h_ref[...]).astype(BF16)

        @pl.when((j >= n_conf) & (j < n_conf + n_gdn))
        def _():
            o_ref[...] = _dot_tn(g_ref[...], h_ref[...]).astype(BF16)

        @pl.when(j == n_conf + n_gdn)
        def _():
            o_ref[0:2 * NH, :] = _dot_tn(ba_ref[...], h_ref[...])[0:2 * NH].astype(BF16)

    return pl.pallas_call(
        body, name="grad_w_in", grid=(n_conf + n_gdn + 1,),
        in_specs=[pl.BlockSpec((S, nb), lambda j: (0, jnp.minimum(j, n_conf - 1))),
                  pl.BlockSpec((S, nb), lambda j: (0, jnp.clip(j - n_conf, 0, n_gdn - 1))),
                  _const((S, LANES)), _const((S, D))],
        out_specs=pl.BlockSpec((nb, D), lambda j: (j, 0)),
        out_shape=jax.ShapeDtypeStruct((NIN, D), BF16),
        compiler_params=_params(dimension_semantics=("arbitrary",)),
    )(dp_conf, dp_gdn, dp_ba, hb1)


def _grad_w_ffn_in(hb2, df):
    def body(a_ref, b_ref, o_ref):
        o_ref[0] = _dot_tn(b_ref[0], a_ref[...]).astype(BF16)

    return pl.pallas_call(
        body, name="grad_w_ffn_in", grid=(N_DEV,),
        in_specs=[_const((S, D)), pl.BlockSpec((1, S, FB), lambda j: (j, 0, 0))],
        out_specs=pl.BlockSpec((1, FB, D), lambda j: (j, 0, 0)),
        out_shape=jax.ShapeDtypeStruct((N_DEV, FB, D), BF16),
        compiler_params=_params(dimension_semantics=("arbitrary",)),
    )(hb2, df)


def _grad_w_ffn_out(act, dffn):
    def body(a_ref, b_ref, o_ref):
        o_ref[0] = _dot_tn(a_ref[0], b_ref[...]).astype(BF16)

    return pl.pallas_call(
        body, name="grad_w_ffn_out", grid=(4,),
        in_specs=[pl.BlockSpec((1, S, FB), lambda j: (j, 0, 0)), _const((S, D))],
        out_specs=pl.BlockSpec((1, FB, D), lambda j: (j, 0, 0)),
        out_shape=jax.ShapeDtypeStruct((4, FB, D), BF16),
        compiler_params=_params(dimension_semantics=("arbitrary",)),
    )(act, dffn)


def _bwd_out(dx1, mix, oab, modnb, bada, w_out):
    def body(dx_ref, mix_ref, oab_ref, mod_ref, b_ref, w_ref, doa_ref, dob_ref, st_ref, gw_ref, acc):
        i = pl.program_id(0)

        @pl.when(i == 0)
        def _():
            st_ref[...] = jnp.zeros((8, D), F32)
            acc[...] = jnp.zeros((D, D), F32)

        dx = dx_ref[...]
        st_ref[0:1, :] += _colsum(dx * mix_ref[...])
        dmix = (_mod(mod_ref, b_ref, 2) * dx).astype(BF16)
        doa_ref[...] = _dot_nt(dmix, w_ref[0:CW, :])
        dob_ref[...] = _dot_nt(dmix, w_ref[CW:D, :])
        acc[...] += _dot_tn(oab_ref[...], dmix)

        @pl.when(i == NT - 1)
        def _():
            gw_ref[...] = acc[...].astype(BF16)

    tile = lambda w: pl.BlockSpec((TM, w), lambda i: (i, 0))
    return pl.pallas_call(
        body, name="bwd_out", grid=(NT,),
        in_specs=[tile(D), tile(D), tile(D), _const((1, 6 * D)), _const((1, 6 * D)), _const((D, D))],
        out_specs=(tile(CW), tile(GW), _const((8, D)), _const((D, D))),
        out_shape=(jax.ShapeDtypeStruct((S, CW), F32), jax.ShapeDtypeStruct((S, GW), F32),
                   jax.ShapeDtypeStruct((8, D), F32), jax.ShapeDtypeStruct((D, D), BF16)),
        scratch_shapes=[pltpu.VMEM((D, D), F32)],
        compiler_params=_params(dimension_semantics=("arbitrary",)),
    )(dx1, mix, oab, modnb, bada, w_out)


CONF_STATS = 40


def _conf_bwd(d_out_a, y, p_main, conv_w, gn_w, gn_b):
    def body(do_ref, y_ref, a_ref, g_ref, ah_ref, gh_ref, w_ref, gw_ref, gb_ref, dp_ref, st_ref,
             ubuf, dybuf, ush, dysh):
        i = pl.program_id(0)

        @pl.when(i == 0)
        def _():
            st_ref[...] = jnp.zeros((CONF_STATS, CW), F32)
            dybuf[TM:TM + HALO, :] = jnp.zeros((HALO, CW), F32)

        pm = _group_mean_matrix().astype(BF16)
        yv = y_ref[...]
        dlt = yv - _group_mean(yv, pm)
        rstd = lax.rsqrt(_group_mean(dlt * dlt, pm) + EPS)
        un = dlt * rstd
        o = un * gw_ref[...] + gb_ref[...]
        so = _sig(o)
        d_o = do_ref[...] * (so * (1.0 + o * (1.0 - so)))
        st_ref[33:34, :] += _colsum(d_o)
        st_ref[32:33, :] += _colsum(d_o * un)
        dun = d_o * gw_ref[...]
        dy = rstd * (dun - _group_mean(dun, pm) - un * _group_mean(dun * un, pm))
        st_ref[31:32, :] += _colsum(dy)
        dybuf[0:TM, :] = dy
        _fill_shifted(dybuf, dysh)

        a = a_ref[...]
        sg = _sig(g_ref[...])
        first = i == NT - 1
        ubuf[0:HALO, :] = jnp.where(first, 0.0, ah_ref[...] * _sig(gh_ref[...]))
        ubuf[HALO:HALO + TM, :] = a * sg
        _fill_shifted(ubuf, ush)
        du = jnp.zeros((TM, CW), F32)
        for k in range(KC):
            st_ref[k:k + 1, :] += _colsum(dy * _rows_at(ubuf, ush, HALO - (KC - 1) + k))
            du = du + w_ref[k:k + 1, :] * _rows_at(dybuf, dysh, KC - 1 - k)
        dybuf[TM:TM + HALO, :] = dybuf[0:HALO, :]
        dp_ref[:, 0:CW] = (du * sg).astype(BF16)
        dp_ref[:, CW:2 * CW] = (du * a * sg * (1.0 - sg)).astype(BF16)

    rev = lambda w, j=0: pl.BlockSpec((TM, w), lambda i: (NT - 1 - i, j))
    halo = lambda j: pl.BlockSpec((HALO, CW), lambda i: (jnp.maximum((NT - 1 - i) * (TM // HALO) - 1, 0), j))
    return pl.pallas_call(
        body, name="conf_bwd", grid=(NT,),
        in_specs=[rev(CW), rev(CW), rev(CW, 0), rev(CW, 1), halo(0), halo(1),
                  _const((KC, CW)), _const((1, CW)), _const((1, CW))],
        out_specs=(rev(2 * CW), _const((CONF_STATS, CW))),
        out_shape=(jax.ShapeDtypeStruct((S, 2 * CW), BF16), jax.ShapeDtypeStruct((CONF_STATS, CW), F32)),
        scratch_shapes=[pltpu.VMEM((HALO + TM, CW), F32), pltpu.VMEM((TM + HALO, CW), F32),
                        pltpu.VMEM((SUB - 1, SHIFT_ROWS, CW), F32), pltpu.VMEM((SUB - 1, SHIFT_ROWS, CW), F32)],
        compiler_params=_params(dimension_semantics=("arbitrary",)),
    )(d_out_a, y, p_main, p_main, p_main, p_main, conv_w, gn_w, gn_b)


GDN_STATS = 8


def _gdn_bwd(d_out_b, o_pre, s_in, t_inv, p_main, p_ba, gdn_conv_w, alog_l, dt_l, gdn_nw):
    def body(dob_ref, o_ref, sin_ref, t_ref, q_ref, k_ref, v_ref, z_ref, qh_ref, kh_ref, vh_ref, ba_ref,
             w_ref, al_ref, dt_ref, nw_ref, dp_ref, dba_ref, st_ref, xbuf, dcbuf, dstate):
        n = pl.program_id(0)

        @pl.when(n == 0)
        def _():
            st_ref[...] = jnp.zeros((GDN_STATS, 3 * GW), F32)
            dcbuf[CL:CL + SH, :] = jnp.zeros((SH, 3 * GW), F32)
            dstate[...] = jnp.zeros((NH, DH, DH), F32)

        for cc in reversed(range(CPS)):
            chunk(n, cc, dob_ref, o_ref, sin_ref, t_ref, q_ref, k_ref, v_ref, z_ref, qh_ref, kh_ref, vh_ref, ba_ref,
                  w_ref, al_ref, dt_ref, nw_ref, dp_ref, dba_ref, st_ref, xbuf, dcbuf, dstate)

    def chunk(n, cc, dob_ref, o_ref, sin_ref, t_ref, q_ref, k_ref, v_ref, z_ref, qh_ref, kh_ref, vh_ref, ba_ref,
              w_ref, al_ref, dt_ref, nw_ref, dp_ref, dba_ref, st_ref, xbuf, dcbuf, dstate):
        r0 = cc * CL
        if cc == 0:
            first = n == NCH // CPS - 1
            xbuf[0:SH, 0:GW] = jnp.where(first, 0.0, qh_ref[...])
            xbuf[0:SH, GW:2 * GW] = jnp.where(first, 0.0, kh_ref[...])
            xbuf[0:SH, 2 * GW:3 * GW] = jnp.where(first, 0.0, vh_ref[...])
        else:
            xbuf[0:SH, 0:GW] = q_ref[r0 - SH:r0, :]
            xbuf[0:SH, GW:2 * GW] = k_ref[r0 - SH:r0, :]
            xbuf[0:SH, 2 * GW:3 * GW] = v_ref[r0 - SH:r0, :]
        xbuf[SH:SH + CL, 0:GW] = q_ref[r0:r0 + CL, :]
        xbuf[SH:SH + CL, GW:2 * GW] = k_ref[r0:r0 + CL, :]
        xbuf[SH:SH + CL, 2 * GW:3 * GW] = v_ref[r0:r0 + CL, :]
        conv = _short_conv(w_ref, xbuf)
        sc = _sig(conv)
        qkv = conv * sc
        ba = ba_ref[r0:r0 + CL, :]
        beta_all, g_all, xg, neg_a = _gdn_gates(ba, al_ref[...], dt_ref[...])
        gcum, gcum_t = _gdn_cumsum(g_all)
        lane = lax.broadcasted_iota(jnp.int32, (CL, LANES), 1)
        row = lax.broadcasted_iota(jnp.int32, (CL, 1), 0)
        acc = dict(dgcum=jnp.zeros((CL, LANES), F32), dbeta=jnp.zeros((CL, LANES), F32))

        def head(h):
            lo = h * DH
            qh = qkv[:, lo:lo + DH]
            kh = qkv[:, GW + lo:GW + lo + DH]
            vh = qkv[:, 2 * GW + lo:2 * GW + lo + DH]
            beta = beta_all[:, h:h + 1]
            f = _head_terms(qh, kh, beta, gcum[:, NH + h:NH + h + 1], gcum_t[NH + h:NH + h + 1, :])
            qn, kn, qs, kb, gam, kds, cd, decay = (f[s] for s in ("qn", "kn", "qs", "kb", "gam", "kds", "cd", "decay"))
            t = t_ref[cc, h]
            st = sin_ref[cc, h]
            vb = vh * beta
            kbg = kb * gam
            u = _dot(t, vb, GP)
            w = _dot(t, kbg, GP)
            yield
            v_new = u - _dot(w, st, GP)
            q_dec = qs * gam
            k_dec = kn * kds

            o = o_ref[r0:r0 + CL, lo:lo + DH]
            zh = z_ref[r0:r0 + CL, lo:lo + DH]
            sz = _sig(zh)
            r = lax.rsqrt(jnp.mean(o * o, axis=-1, keepdims=True) + EPS)
            orr = o * r
            d_out = dob_ref[r0:r0 + CL, lo:lo + DH]
            dz = d_out * (orr * nw_ref[...]) * (sz * (1.0 + zh * (1.0 - sz)))
            don = d_out * (zh * sz)
            st_ref[4:5, 0:DH] += _colsum(don * orr)
            tt = don * nw_ref[...]
            d_o = r * (tt - orr * jnp.mean(tt * orr, axis=-1, keepdims=True))

            yield
            ds_out = dstate[h]
            dv_new = _dot_tn(f["qk"], d_o, GP) + _dot(k_dec, ds_out, GP)
            dqk = jnp.where(f["causal"], _dot_nt(d_o, v_new, GP), 0.0)
            dq_dec = _dot_nt(d_o, st, GP)
            dk_dec = _dot_nt(v_new, ds_out, GP)
            yield
            dstate[h] = _dot_tn(q_dec, d_o, GP) + cd * ds_out - _dot_tn(w, dv_new, GP)
            dcd = jnp.sum(_rowsum(st * ds_out), axis=0, keepdims=True)
            dw = -_dot_nt(dv_new, st, GP)
            dvb = _dot_tn(t, dv_new, GP)
            yield
            dt_m = _dot_nt(dv_new, vb, GP) + _dot_nt(dw, kbg, GP)
            dkbg = _dot_tn(t, dw, GP)
            yield
            dtt = _dot_nt(dt_m, t, GP)
            yield
            da = jnp.where(f["strict"], -_dot_tn(t, dtt, GP), 0.0)
            yield
            dad = da * decay
            dqkd = dqk * decay
            dkb = _dot(dad, kn, GP) + dkbg * gam
            dkn = _dot_tn(dad, kb, GP) + _dot_tn(dqkd, qs, GP) + dk_dec * kds + dkb * beta
            dqs = _dot(dqkd, kn, GP) + dq_dec * gam
            yield
            m = da * f["a"] + dqk * f["qk"]
            tk = _rowsum(dk_dec * k_dec)
            dgl = jnp.sum(tk, axis=0, keepdims=True) + dcd * cd
            dgc = (_rowsum(m) - _rowsum(jnp.transpose(m)) + _rowsum(dq_dec * q_dec) - tk + _rowsum(dkbg * kbg)
                   + jnp.where(row == CL - 1, dgl, 0.0))
            dbeta = _rowsum(dkb * kn) + _rowsum(dvb * vh)
            acc["dgcum"] = acc["dgcum"] + jnp.where(lane == NH + h, dgc, 0.0)
            acc["dbeta"] = acc["dbeta"] + jnp.where(lane == h, dbeta, 0.0)
            dvh = dvb * beta
            dqn = dqs * QSCALE
            dqh = f["rq"] * (dqn - qn * _rowsum(dqn * qn))
            dkh = f["rk"] * (dkn - kn * _rowsum(dkn * kn))
            dsilu = lambda c0: sc[:, c0:c0 + DH] * (1.0 + conv[:, c0:c0 + DH] * (1.0 - sc[:, c0:c0 + DH]))
            dcbuf[0:CL, lo:lo + DH] = dqh * dsilu(lo)
            dcbuf[0:CL, GW + lo:GW + lo + DH] = dkh * dsilu(GW + lo)
            dcbuf[0:CL, 2 * GW + lo:2 * GW + lo + DH] = dvh * dsilu(2 * GW + lo)
            dp_ref[r0:r0 + CL, 3 * GW + lo:3 * GW + lo + DH] = dz.astype(BF16)

        _lockstep(head(h) for h in range(NH))
        dgcum_all, dbeta_all = acc["dgcum"], acc["dbeta"]

        ii, jj = _tri_iota()
        upper = jnp.where(ii <= jj, 1.0, 0.0).astype(BF16)
        dg_all = _ones_dot(upper, dgcum_all)
        dxg = dg_all * neg_a * _sig(xg)
        st_ref[5:6, 0:LANES] += _colsum(dg_all * g_all)
        st_ref[6:7, 0:LANES] += _colsum(dxg)
        dbl = dbeta_all * beta_all * (1.0 - beta_all)
        dba_ref[r0:r0 + CL, :] = jnp.where(lane < NH, dbl, jnp.where(lane < 2 * NH, dxg, 0.0)).astype(BF16)

        dconv = dcbuf[0:CL, :]
        dx = w_ref[0:1, :] * dcbuf[KS - 1:KS - 1 + CL, :]
        st_ref[0:1, :] += _colsum(dconv * xbuf[SH - KS + 1:SH - KS + 1 + CL, :])
        for k in range(1, KS):
            off = SH - (KS - 1) + k
            st_ref[k:k + 1, :] += _colsum(dconv * xbuf[off:off + CL, :])
            dx = dx + w_ref[k:k + 1, :] * dcbuf[KS - 1 - k:KS - 1 - k + CL, :]
        dcbuf[CL:CL + SH, :] = dcbuf[0:SH, :]
        dp_ref[r0:r0 + CL, 0:3 * GW] = dx.astype(BF16)

    steps = NCH // CPS
    rev = lambda w, j=0: pl.BlockSpec((TG, w), lambda n: (steps - 1 - n, j))
    halo = lambda j: pl.BlockSpec((SH, GW), lambda n: (jnp.maximum((steps - 1 - n) * (TG // SH) - 1, 0), j))
    blk4 = lambda a, b: pl.BlockSpec((CPS, NH, a, b), lambda n: (steps - 1 - n, 0, 0, 0))
    return pl.pallas_call(
        body, name="gdn_bwd", grid=(steps,),
        in_specs=[rev(GW), rev(GW), blk4(DH, DH), blk4(CL, CL), rev(GW, 2), rev(GW, 3), rev(GW, 4), rev(GW, 5),
                  halo(2), halo(3), halo(4), rev(LANES), _const((KS, 3 * GW)), _const((1, LANES)),
                  _const((1, LANES)), _const((1, DH))],
        out_specs=(rev(4 * GW), rev(LANES), _const((GDN_STATS, 3 * GW))),
        out_shape=(jax.ShapeDtypeStruct((S, 4 * GW), BF16), jax.ShapeDtypeStruct((S, LANES), BF16),
                   jax.ShapeDtypeStruct((GDN_STATS, 3 * GW), F32)),
        scratch_shapes=[pltpu.VMEM((SH + CL, 3 * GW), F32), pltpu.VMEM((CL + SH, 3 * GW), F32),
                        pltpu.VMEM((NH, DH, DH), F32)],
        compiler_params=_params(dimension_semantics=("arbitrary",)),
    )(d_out_b, o_pre, s_in, t_inv, p_main, p_main, p_main, p_main, p_main, p_main, p_main, p_ba,
      gdn_conv_w, alog_l, dt_l, gdn_nw)


def _bwd_in(dp_conf, dp_gdn, dp_ba, x, dx1, nw1, modnb, bada, w_main, w_ba):
    def body(dc_ref, dg_ref, db_ref, x_ref, dx1_ref, nw_ref, mod_ref, b_ref, wm_ref, wb_ref, gx_ref, st_ref):
        i = pl.program_id(0)

        @pl.when(i == 0)
        def _():
            st_ref[...] = jnp.zeros((8, D), F32)

        dh = (_dot(dc_ref[...], wm_ref[0:2 * CW, :]) + _dot(dg_ref[...], wm_ref[2 * CW:NMAIN, :])
              + _dot(db_ref[...], wb_ref[...]))
        xv = x_ref[...]
        r = lax.rsqrt(jnp.mean(xv * xv, axis=-1, keepdims=True) + EPS)
        xr = xv * r
        st_ref[0:1, :] += _colsum(dh)
        st_ref[1:2, :] += _colsum(dh * (xr * nw_ref[...]))
        dxn = dh * (1.0 + _mod(mod_ref, b_ref, 1))
        st_ref[2:3, :] += _colsum(dxn * xr)
        dxr = dxn * nw_ref[...]
        gx_ref[...] = dx1_ref[...] + r * (dxr - xr * jnp.mean(dxr * xr, axis=-1, keepdims=True))

    tile = lambda w: pl.BlockSpec((TI, w), lambda i: (i, 0))
    return pl.pallas_call(
        body, name="bwd_in", grid=(S // TI,),
        in_specs=[tile(2 * CW), tile(4 * GW), tile(LANES), tile(D), tile(D), _const((1, D)), _const((1, 6 * D)),
                  _const((1, 6 * D)), _const1((NMAIN, D)), _const((LANES, D))],
        out_specs=(tile(D), _const((8, D))),
        out_shape=(jax.ShapeDtypeStruct((S, D), F32), jax.ShapeDtypeStruct((8, D), F32)),
        compiler_params=_params(dimension_semantics=("arbitrary",)),
    )(dp_conf, dp_gdn, dp_ba, x, dx1, nw1, modnb, bada, w_main, w_ba)


def _adamw(w, g, m, v):
    m = ADAM_B1 * m + (1.0 - ADAM_B1) * g
    v = ADAM_B2 * v + (1.0 - ADAM_B2) * (g * g)
    m_hat = m / BC1
    v_hat = v / BC2
    delta = -ADAM_LR * (m_hat / (jnp.sqrt(v_hat) + ADAM_EPS) + ADAM_WD * w)
    return delta, m, v


ADAM_BLOCK_BYTES = 6 * 1024 * 1024


def _adam_tile(rows, cols):
    padded = -(-cols // LANES) * LANES
    if N_DEV * rows * padded * 4 <= ADAM_BLOCK_BYTES:
        return rows, cols
    best = None
    for tr in range(16, rows, 16):
        if rows % tr == 0 and N_DEV * tr * padded * 4 <= ADAM_BLOCK_BYTES:
            best = tr
    if best is not None:
        return best, cols
    rows_padded = -(-rows // 16) * 16
    tc = LANES
    for cand in range(LANES, cols, LANES):
        if cols % cand == 0 and N_DEV * rows_padded * cand * 4 <= ADAM_BLOCK_BYTES:
            tc = cand
    return rows, tc


def _reduce_adam(name, parts, w, m, v, own=None):
    rows, cols = w.shape
    tr, tc = _adam_tile(rows, cols)

    def body(*refs):
        p_ref, w_ref, m_ref, v_ref = refs[:4]
        g_ref, d_ref, nm_ref, nv_ref = refs[-4:]
        if own is None:
            part = lambda j: p_ref[j].astype(F32)
        else:
            me = 4 * lax.axis_index("x") + 2 * lax.axis_index("y") + lax.axis_index("c")
            part = lambda j: jnp.where(me == j, refs[4][...], p_ref[j]).astype(F32)
        g = part(0)
        for j in range(1, N_DEV):
            g = g + part(j)
        g_ref[...] = g
        d_ref[...], nm_ref[...], nv_ref[...] = _adamw(w_ref[...], g, m_ref[...], v_ref[...])

    blk = pl.BlockSpec((tr, tc), lambda i, j: (i, j))
    sds = jax.ShapeDtypeStruct((rows, cols), F32)
    extra = [] if own is None else [own]
    return pl.pallas_call(
        body, name=name, grid=(rows // tr, cols // tc),
        in_specs=[pl.BlockSpec((N_DEV, tr, tc), lambda i, j: (0, i, j)), blk, blk, blk] + [blk] * len(extra),
        out_specs=(blk, blk, blk, blk), out_shape=(sds, sds, sds, sds),
        compiler_params=_params(dimension_semantics=("arbitrary", "arbitrary")),
    )(parts, w, m, v, *extra)


def _ada_adam(c_all, dmod_sh, w, m, v):
    rows, cols = w.shape
    tr = 256

    def body(c_ref, dm_ref, w_ref, m_ref, v_ref, g_ref, d_ref, nm_ref, nv_ref):
        cv = c_ref[...]
        g = _dot_tn(cv * _sig(cv), dm_ref[...], HI)
        g_ref[...] = g
        d_ref[...], nm_ref[...], nv_ref[...] = _adamw(w_ref[...], g, m_ref[...], v_ref[...])

    blk = pl.BlockSpec((tr, cols), lambda i: (i, 0))
    sds = jax.ShapeDtypeStruct((rows, cols), F32)
    return pl.pallas_call(
        body, name="ada_adam", grid=(rows // tr,),
        in_specs=[pl.BlockSpec((N_DEV, tr), lambda i: (0, i)), _const((N_DEV, cols)), blk, blk, blk],
        out_specs=(blk, blk, blk, blk), out_shape=(sds, sds, sds, sds),
        compiler_params=_params(dimension_semantics=("arbitrary",)),
    )(c_all, dmod_sh, w, m, v)


def _lanes(a, at=0):
    return jnp.pad(a, ((0, 0), (at, LANES - at - a.shape[1])))


WEIGHT_NAMES = ["w_ada", "b_ada", "norm_mix_w", "w_in", "conv_w", "conv_b", "conv_gn_w", "conv_gn_b", "gdn_conv_w",
                "gdn_a_log", "gdn_dt_bias", "gdn_norm_w", "w_out", "norm_ffn_w", "w_ffn_in", "w_ffn_out",
                "norm_final_w"]


SMALL_LAYOUT = [("b_ada", 0, 48, LANES), ("norm_mix_w", 48, 8, LANES), ("norm_ffn_w", 56, 8, LANES),
                ("norm_final_w", 64, 8, LANES), ("conv_b", 72, 4, LANES), ("conv_gn_w", 76, 4, LANES),
                ("conv_gn_b", 80, 4, LANES), ("gdn_norm_w", 84, 1, LANES), ("gdn_a_log", 85, 1, NH),
                ("gdn_dt_bias", 86, 1, NH)]
LOSS_ROW = 87


def _adam_small(g_small, weights, m1, m2):
    names = [nm for nm, _, _, _ in SMALL_LAYOUT]
    k = len(names)

    def body(*refs):
        g_ref = refs[0]
        w_refs, m_refs, v_refs = refs[1:1 + k], refs[1 + k:1 + 2 * k], refs[1 + 2 * k:1 + 3 * k]
        loss_ref = refs[1 + 3 * k]
        outs = refs[2 + 3 * k:2 + 7 * k]
        total = refs[-1]
        g = g_ref[0]
        for j in range(1, N_DEV):
            g = g + g_ref[j]
        total[...] = g
        loss_ref[...] = total[LOSS_ROW:LOSS_ROW + 1, :]
        for i, (_, r0, rows, lanes) in enumerate(SMALL_LAYOUT):
            gp = total[r0:r0 + rows, 0:lanes]
            outs[i][...] = gp
            outs[k + i][...], outs[2 * k + i][...], outs[3 * k + i][...] = _adamw(
                w_refs[i][...], gp, m_refs[i][...], v_refs[i][...])

    shapes = [jax.ShapeDtypeStruct((rows, lanes), F32) for _, _, rows, lanes in SMALL_LAYOUT]
    res = pl.pallas_call(
        body, name="adam_small",
        out_shape=tuple([jax.ShapeDtypeStruct((1, LANES), F32)] + shapes * 4),
        scratch_shapes=[pltpu.VMEM((SMALL_ROWS, LANES), F32)],
        compiler_params=_params(),
    )(g_small, *[weights[n] for n in names], *[m1[n] for n in names], *[m2[n] for n in names])
    kinds = [dict(zip(names, res[1 + q * k:1 + (q + 1) * k])) for q in range(4)]
    return res[0], kinds


def _mix_forward(w, xs, modnb, between=None):
    w_main = w["w_in"]
    w_ba = jnp.pad(w["w_in"][NMAIN:], ((0, LANES - 2 * NH), (0, 0)))
    alog_l = _lanes(w["gdn_a_log"], NH)
    dt_l = _lanes(w["gdn_dt_bias"], NH)
    p_main, p_ba, hb1 = _fwd_in(xs, w["norm_mix_w"], modnb, w["b_ada"], w_main, w_ba)
    w_o, u_o, qg, kd, qk, cd, t_inv = _gdn_prep(p_main, p_ba, w["gdn_conv_w"], alog_l, dt_l)
    out_b, o_pre, s_in = _gdn_scan(w_o, u_o, qg, kd, qk, cd, p_main, w["gdn_norm_w"])
    conv_b = w["conv_b"] if between is None else _after(w["conv_b"], between(out_b))
    y_conv, out_a = _conf_fwd(p_main, w["conv_w"], conv_b, w["conv_gn_w"], w["conv_gn_b"])
    return dict(w_main=w_main, w_ba=w_ba, alog_l=alog_l, dt_l=dt_l, p_main=p_main, p_ba=p_ba, hb1=hb1,
                y_conv=y_conv, out_a=out_a, out_b=out_b, o_pre=o_pre, s_in=s_in, t_inv=t_inv)


def _ffn_stage(w, f, xs, tgt, modnb):
    x1, mix, oab = _fwd_out(f["out_a"], f["out_b"], xs, modnb, w["b_ada"], w["w_out"])
    hb2, act, pre, dx2, dffn, st_fwd = _ffn_forward(x1, tgt, modnb, w["b_ada"], w["norm_ffn_w"],
                                                    w["norm_final_w"], w["w_ffn_in"], w["w_ffn_out"])
    gw_ffn_out = _grad_w_ffn_out(act, dffn)
    df, dx1, st_bwd = _ffn_backward(dffn, pre, x1, dx2, modnb, w["b_ada"], w["norm_ffn_w"], w["w_ffn_in"],
                                    w["w_ffn_out"])
    gw_ffn_in = _grad_w_ffn_in(hb2, df)
    return dict(mix=mix, oab=oab, dx1=dx1, st_ffn=st_fwd + st_bwd, gw_ffn_in=gw_ffn_in, gw_ffn_out=gw_ffn_out)


def _out_backward(w, g, modnb):
    d_out_a, d_out_b, st_out, gw_out = _bwd_out(g["dx1"], g["mix"], g["oab"], modnb, w["b_ada"], w["w_out"])
    return dict(d_out_a=d_out_a, d_out_b=d_out_b, st_out=st_out, gw_out=gw_out)


def _heads_backward(w, f, a):
    dp_conf, st_conf = _conf_bwd(a["d_out_a"], f["y_conv"], f["p_main"], w["conv_w"], w["conv_gn_w"],
                                 w["conv_gn_b"])
    dp_gdn, dp_ba, st_gdn = _gdn_bwd(a["d_out_b"], f["o_pre"], f["s_in"], f["t_inv"], f["p_main"], f["p_ba"],
                                     w["gdn_conv_w"], f["alog_l"], f["dt_l"], w["gdn_norm_w"])
    gw_in = _grad_w_in(dp_conf, dp_gdn, dp_ba, f["hb1"])
    return dict(dp_conf=dp_conf, dp_gdn=dp_gdn, dp_ba=dp_ba, st_conf=st_conf, st_gdn=st_gdn, gw_in=gw_in,
                gw_conv=st_conf[0:KC], gw_gconv=st_gdn[0:KS])


def _in_backward(w, f, g, a, h, xs, modnb):
    st_out, st_conf, st_gdn, st_ffn = a["st_out"], h["st_conf"], h["st_gdn"], g["st_ffn"]
    grad_x, st_in = _bwd_in(h["dp_conf"], h["dp_gdn"], h["dp_ba"], xs, g["dx1"], w["norm_mix_w"], modnb,
                            w["b_ada"], f["w_main"], f["w_ba"])
    dmod = jnp.concatenate([st_in[0:1], st_in[1:2], st_out[0:1], st_ffn[2:3], st_ffn[3:4], st_ffn[1:2]], axis=1)
    small = jnp.concatenate([
        dmod.reshape(48, LANES), st_in[2:3].reshape(8, LANES), st_ffn[4:5].reshape(8, LANES),
        st_ffn[0:1].reshape(8, LANES), st_conf[31:32].reshape(4, LANES), st_conf[32:33].reshape(4, LANES),
        st_conf[33:34].reshape(4, LANES), st_gdn[4:5, 0:LANES],
        _lanes(st_gdn[5:6, NH:2 * NH]), _lanes(st_gdn[6:7, NH:2 * NH]), st_ffn[5:6, 0:LANES]], axis=0)
    return dict(grad_x=grad_x, small=small)


def _local(w, xs, tgt, modnb):
    f = _mix_forward(w, xs, modnb)
    g = _ffn_stage(w, f, xs, tgt, modnb)
    a = _out_backward(w, g, modnb)
    h = _heads_backward(w, f, a)
    b = _in_backward(w, f, g, a, h, xs, modnb)
    return dict(b, gw_in=h["gw_in"], gw_conv=h["gw_conv"], gw_gconv=h["gw_gconv"], gw_out=a["gw_out"],
                gw_ffn_in=g["gw_ffn_in"], gw_ffn_out=g["gw_ffn_out"])


def kernel(x, c, w_ada, b_ada, norm_mix_w, w_in, conv_w, conv_b, conv_gn_w, conv_gn_b, gdn_conv_w, gdn_a_log, gdn_dt_bias, gdn_norm_w, w_out, norm_ffn_w, w_ffn_in, w_ffn_out, norm_final_w, loss_target, m_w_ada, m_b_ada, m_norm_mix_w, m_w_in, m_conv_w, m_conv_b, m_conv_gn_w, m_conv_gn_b, m_gdn_conv_w, m_gdn_a_log, m_gdn_dt_bias, m_gdn_norm_w, m_w_out, m_norm_ffn_w, m_w_ffn_in, m_w_ffn_out, m_norm_final_w, v_w_ada, v_b_ada, v_norm_mix_w, v_w_in, v_conv_w, v_conv_b, v_conv_gn_w, v_conv_gn_b, v_gdn_conv_w, v_gdn_a_log, v_gdn_dt_bias, v_gdn_norm_w, v_w_out, v_norm_ffn_w, v_w_ffn_in, v_w_ffn_out, v_norm_final_w):
    me = 4 * lax.axis_index("x") + 2 * lax.axis_index("y") + lax.axis_index("c")
    xs = x.reshape(S, D)
    tgt = loss_target.reshape(S, D)

    late = [w_out[0].astype(BF16), jnp.transpose(w_ffn_in[0]).astype(BF16), w_ffn_out[0].astype(BF16)]
    g_c, g_cw, g_gcw, g_win, *late_lands = _gather_two_level(
        "gather_weights", [c, conv_w[0], gdn_conv_w[0], jnp.transpose(w_in[0]).astype(BF16)] + late,
        seed_only=(4, 5, 6))
    c_all = g_c.reshape(N_DEV, D)
    g_mod, mod_token = _exchange("gather_mod", [_mod_shard(c_all, w_ada[0])], [False], with_token=True)
    modnb = lax.dynamic_index_in_dim(g_mod, me, axis=1, keepdims=False).reshape(1, 6 * D)
    late_started = _exchange_start("gather_late_start", [_after(late[0], mod_token)] + late[1:], late_lands,
                                   [False] * 3, only=LEVEL_ONE)
    modnb = _after(modnb, late_started[-1])
    w = dict(b_ada=b_ada, norm_mix_w=norm_mix_w, conv_b=conv_b, conv_gn_w=conv_gn_w, conv_gn_b=conv_gn_b,
             gdn_a_log=gdn_a_log, gdn_dt_bias=gdn_dt_bias, gdn_norm_w=gdn_norm_w, norm_ffn_w=norm_ffn_w,
             norm_final_w=norm_final_w.reshape(1, D),
             conv_w=jnp.transpose(g_cw, (1, 0, 2)).reshape(KC, CW),
             gdn_conv_w=jnp.transpose(g_gcw, (1, 0, 2)).reshape(KS, 3 * GW),
             w_in=g_win.reshape(NIN, D))

    relay = {}

    def relay_late(out_b):
        _, late_landed = _exchange_wait("gather_late_wait", late_started, [False] * 3, (out_b,), only=LEVEL_ONE)
        relay["started"] = _relay_start("gather_late_relay_start", late_landed)
        return relay["started"][-1]

    f = _mix_forward(w, xs, modnb, relay_late)
    g_wout, g_wfi, g_wfo = _relay_wait("gather_late_relay_wait", relay["started"], (f["out_a"],))
    w.update(w_out=g_wout.reshape(D, D), w_ffn_in=g_wfi, w_ffn_out=g_wfo.reshape(4, FB, D))
    g = _ffn_stage(w, f, xs, tgt, modnb)

    ffn_grads = [g["gw_ffn_in"], g["gw_ffn_out"].reshape(N_DEV, DFF // N_DEV, D)]
    ffn_started = _exchange_start("scatter_ffn_start", ffn_grads,
                                  [lax.empty(a.shape, a.dtype) for a in ffn_grads], [True] * 2)
    a = _out_backward(w, g, _after(modnb, ffn_started[-1]))
    out_grads = [a["gw_out"].reshape(N_DEV, D // N_DEV, D)]
    out_started = _exchange_start("scatter_out_start", out_grads,
                                  [lax.empty(t.shape, t.dtype) for t in out_grads], [True])
    h = _heads_backward(dict(w, conv_gn_w=_after(w["conv_gn_w"], out_started[-1])), f, a)

    in_grads = [h["gw_in"].reshape(N_DEV, NIN // N_DEV, D),
                jnp.transpose(h["gw_conv"].reshape(KC, N_DEV, CW // N_DEV), (1, 0, 2)),
                jnp.transpose(h["gw_gconv"].reshape(KS, N_DEV, 3 * GW // N_DEV), (1, 0, 2))]
    in_started = _exchange_start("scatter_in_start", in_grads,
                                 [lax.empty(t.shape, t.dtype) for t in in_grads], [True] * 3)
    loc = _in_backward(w, f, g, a, h, xs, _after(modnb, in_started[-1]))
    small_started = _exchange_start("gather_small_start", [loc["small"]],
                                    [lax.empty((N_DEV, SMALL_ROWS, LANES), F32)], [False])

    def own(sent):
        return lax.dynamic_index_in_dim(sent, me, axis=0, keepdims=False)

    big = {}
    (sent_fi, sent_fo), (r_fi, r_fo) = _exchange_wait("scatter_ffn_wait", ffn_started, [True] * 2,
                                                         (small_started[-1],))
    big["w_ffn_in"] = [jnp.transpose(t) for t in _reduce_adam(
        "adam_w_ffn_in", r_fi, jnp.transpose(w_ffn_in[0]), jnp.transpose(m_w_ffn_in[0]),
        jnp.transpose(v_w_ffn_in[0]), own(sent_fi))]
    big["w_ffn_out"] = _reduce_adam("adam_w_ffn_out", r_fo, w_ffn_out[0], m_w_ffn_out[0], v_w_ffn_out[0],
                                    own(sent_fo))
    (sent_out,), (r_out,) = _exchange_wait("scatter_out_wait", out_started, [True], (big["w_ffn_out"][0],))
    big["w_out"] = _reduce_adam("adam_w_out", r_out, w_out[0], m_w_out[0], v_w_out[0], own(sent_out))

    (sent_small,), (r_small,) = _exchange_wait("gather_small_wait", small_started, [False], (big["w_out"][0],))
    slot = lax.broadcasted_iota(jnp.int32, (N_DEV, 1, 1), 0)
    g_small = jnp.where(slot == me, sent_small[None], r_small)
    def views(b_, nm_, nf_, nl_, cb_, gw_, gb_, gn_, al_, dt_):
        arrs = [b_, nm_, nf_, nl_, cb_, gw_, gb_, gn_, al_, dt_]
        return {nm: t.reshape(rows, lanes) for (nm, _, rows, lanes), t in zip(SMALL_LAYOUT, arrs)}

    loss_row, res = _adam_small(
        g_small,
        views(b_ada, norm_mix_w, norm_ffn_w, norm_final_w, conv_b, conv_gn_w, conv_gn_b, gdn_norm_w, gdn_a_log,
              gdn_dt_bias),
        views(m_b_ada, m_norm_mix_w, m_norm_ffn_w, m_norm_final_w, m_conv_b, m_conv_gn_w, m_conv_gn_b,
              m_gdn_norm_w, m_gdn_a_log, m_gdn_dt_bias),
        views(v_b_ada, v_norm_mix_w, v_norm_ffn_w, v_norm_final_w, v_conv_b, v_conv_gn_w, v_conv_gn_b,
              v_gdn_norm_w, v_gdn_a_log, v_gdn_dt_bias))
    loss = loss_row[0, 0]
    small_shapes = dict(b_ada=(1, 6 * D), norm_mix_w=(1, D), norm_ffn_w=(1, D), norm_final_w=(D,),
                        conv_b=(1, CW), conv_gn_w=(1, CW), conv_gn_b=(1, CW), gdn_norm_w=(1, DH),
                        gdn_a_log=(1, NH), gdn_dt_bias=(1, NH))
    res = [{nm: t.reshape(small_shapes[nm]) for nm, t in kind.items()} for kind in res]

    dmod_rows = g_small[:, 0:48, :].reshape(N_DEV, 6 * D)
    dmod_sh = lax.dynamic_slice_in_dim(dmod_rows, me * (6 * D // N_DEV), 6 * D // N_DEV, axis=1)

    big["w_ada"] = _ada_adam(c_all, dmod_sh, w_ada[0], m_w_ada[0], v_w_ada[0])
    (sent_in, sent_cw, sent_gcw), (r_in, r_cw, r_gcw) = _exchange_wait(
        "scatter_in_wait", in_started, [True] * 3, (big["w_ada"][0],))
    big["w_in"] = [jnp.transpose(t) for t in _reduce_adam(
        "adam_w_in", r_in, jnp.transpose(w_in[0]), jnp.transpose(m_w_in[0]), jnp.transpose(v_w_in[0]),
        own(sent_in))]
    big["conv_w"] = _reduce_adam("adam_conv_w", r_cw, conv_w[0], m_conv_w[0], v_conv_w[0], own(sent_cw))
    big["gdn_conv_w"] = _reduce_adam("adam_gdn_conv_w", r_gcw, gdn_conv_w[0], m_gdn_conv_w[0], v_gdn_conv_w[0],
                                     own(sent_gcw))
    outs = [loss, loc["grad_x"].reshape(1, S, D)]
    for kind in range(4):
        for nm in WEIGHT_NAMES:
            outs.append(big[nm][kind][None] if nm in big else res[kind][nm])
    return tuple(outs)
```
